```python
import jax, jax.numpy as jnp
from jax import lax
import numpy as np

D_MODEL = 1024
BATCH = 8
SEQ = 4096
DEPTH = 1

POOL_WINDOWS = (2, 4, 8, 16)
N_POOL_GROUPS = len(POOL_WINDOWS)
D_POOL = D_MODEL // 2
POOL_GROUP = D_POOL // N_POOL_GROUPS
D_RNN = D_MODEL
N_RNN_HEADS = 8
RNN_HEAD = D_RNN // N_RNN_HEADS
CONV_WIDTH = 4
LRU_C = 8.0
N_BRANCHES = 2
D_IN = D_POOL + 2 * D_RNN + N_BRANCHES * D_MODEL
D_FF = -(-8 * D_MODEL // (3 * 256)) * 256
NORM_EPS = 1e-6

kernel_name = "hybrid_pool_rglru_gated_block"


def rmsnorm(x, g):
    xf = x.astype(jnp.float32)
    y = xf * lax.rsqrt(jnp.mean(xf * xf, axis=-1, keepdims=True) + NORM_EPS)
    return (y * g.astype(jnp.float32)).astype(x.dtype)


def pool_mixer(u, w_grp, scale):
    B, S, _ = u.shape
    uf = u.astype(jnp.float32).reshape(B, S, N_POOL_GROUPS, POOL_GROUP)
    c = jnp.cumsum(uf, axis=1)
    pos = jnp.arange(S)
    outs = []
    for g, w in enumerate(POOL_WINDOWS):
        cg = c[:, :, g]
        c_lo = jnp.pad(cg[:, : S - w], ((0, 0), (w, 0), (0, 0)))
        count = jnp.minimum(pos + 1, w).astype(jnp.float32)[None, :, None]
        outs.append((cg - c_lo) / count - uf[:, :, g])
    pooled = jnp.stack(outs, axis=2).astype(u.dtype)
    mixed = jnp.einsum("bsgc,gcd->bsgd", pooled, w_grp)
    return mixed.reshape(B, S, D_POOL) * scale


def causal_depthwise_conv(u, w, b):
    S = u.shape[1]
    up = jnp.pad(u, ((0, 0), (CONV_WIDTH - 1, 0), (0, 0)))
    y = b
    for k in range(CONV_WIDTH):
        y = y + up[:, k : k + S] * w[k]
    return y


def rg_lru(v, w_a, b_a, w_x, b_x, lam):
    B, S, _ = v.shape
    vh = v.reshape(B, S, N_RNN_HEADS, RNN_HEAD)
    r = jax.nn.sigmoid((jnp.einsum("bshi,hij->bshj", vh, w_a) + b_a).astype(jnp.float32)).reshape(B, S, D_RNN)
    i = jax.nn.sigmoid((jnp.einsum("bshi,hij->bshj", vh, w_x) + b_x).astype(jnp.float32)).reshape(B, S, D_RNN)
    log_a = -LRU_C * r * jax.nn.softplus(-lam.astype(jnp.float32))
    a = jnp.exp(log_a)
    b = jnp.sqrt(-jnp.expm1(2.0 * log_a)) * i * v.astype(jnp.float32)

    def combine(left, right):
        a1, b1 = left
        a2, b2 = right
        return a1 * a2, a2 * b1 + b2

    _, h = lax.associative_scan(combine, (a, b), axis=1)
    return h.astype(v.dtype)


def _fwd_setup_inputs(seed: int = 0) -> dict:
    key = jax.random.key(seed)
    ks = jax.random.split(key, 22)
    f32 = jnp.float32
    L = DEPTH

    def nrm(k, shape, fan_in):
        return jax.random.normal(k, shape, f32) * fan_in ** -0.5

    def small(k, shape, s=0.02):
        return jax.random.normal(k, shape, f32) * s

    x = jax.random.normal(ks[0], (BATCH, SEQ, D_MODEL), f32)
    norm_mix = 1.0 + small(ks[1], (L, D_MODEL))
    w_in = nrm(ks[2], (L, D_MODEL, D_IN), D_MODEL)
    w_pool_grp = nrm(ks[3], (L, N_POOL_GROUPS, POOL_GROUP, POOL_GROUP), POOL_GROUP)
    pool_scale = 1.0 + small(ks[4], (L, D_POOL))
    w_pool_out = nrm(ks[5], (L, D_POOL, D_MODEL), D_POOL)
    conv_w = nrm(ks[6], (L, CONV_WIDTH, D_RNN), CONV_WIDTH)
    conv_b = small(ks[7], (L, D_RNN))
    w_rg_a = nrm(ks[8], (L, N_RNN_HEADS, RNN_HEAD, RNN_HEAD), RNN_HEAD)
    b_rg_a = small(ks[9], (L, N_RNN_HEADS, RNN_HEAD))
    w_rg_x = nrm(ks[10], (L, N_RNN_HEADS, RNN_HEAD, RNN_HEAD), RNN_HEAD)
    b_rg_x = small(ks[11], (L, N_RNN_HEADS, RNN_HEAD))
    a_c = jax.random.uniform(ks[12], (L, D_RNN), f32, minval=0.9, maxval=0.999)
    a0 = a_c ** (1.0 / LRU_C)
    lru_lambda = jnp.log(a0) - jnp.log1p(-a0)
    w_rnn_out = nrm(ks[13], (L, D_RNN, D_MODEL), D_RNN)
    w_o = nrm(ks[14], (L, D_MODEL, D_MODEL), D_MODEL)
    norm_ffn = 1.0 + small(ks[15], (L, D_MODEL))
    w_ffn_in = nrm(ks[16], (L, D_MODEL, 2 * D_FF), D_MODEL)
    w_ffn_out = nrm(ks[17], (L, D_FF, D_MODEL), D_FF)
    norm_final = 1.0 + small(ks[18], (D_MODEL,))
    return {"x": x, "norm_mix": norm_mix, "w_in": w_in, "w_pool_grp": w_pool_grp,
            "pool_scale": pool_scale, "w_pool_out": w_pool_out, "conv_w": conv_w, "conv_b": conv_b,
            "w_rg_a": w_rg_a, "b_rg_a": b_rg_a, "w_rg_x": w_rg_x, "b_rg_x": b_rg_x,
            "lru_lambda": lru_lambda, "w_rnn_out": w_rnn_out, "w_o": w_o, "norm_ffn": norm_ffn,
            "w_ffn_in": w_ffn_in, "w_ffn_out": w_ffn_out, "norm_final": norm_final}


def _fwd_reference(x, norm_mix, w_in, w_pool_grp, pool_scale, w_pool_out, conv_w, conv_b,
              w_rg_a, b_rg_a, w_rg_x, b_rg_x, lru_lambda, w_rnn_out, w_o, norm_ffn,
              w_ffn_in, w_ffn_out, norm_final):
    B, S, _ = x.shape
    for l in range(DEPTH):
        h = rmsnorm(x, norm_mix[l])
        proj = h @ w_in[l]
        o1 = D_POOL
        o2 = o1 + D_RNN
        o3 = o2 + D_RNN
        u_pool = proj[..., :o1]
        u_rnn = proj[..., o1:o2]
        u_gate = proj[..., o2:o3]
        g_merge = jax.nn.sigmoid(proj[..., o3:].reshape(B, S, N_BRANCHES, D_MODEL))

        y_pool = pool_mixer(u_pool, w_pool_grp[l], pool_scale[l]) @ w_pool_out[l]

        v = causal_depthwise_conv(u_rnn, conv_w[l], conv_b[l])
        hr = rg_lru(v, w_rg_a[l], b_rg_a[l], w_rg_x[l], b_rg_x[l], lru_lambda[l])
        y_rnn = (hr * jax.nn.gelu(u_gate)) @ w_rnn_out[l]

        mix = g_merge[:, :, 0] * y_pool + g_merge[:, :, 1] * y_rnn
        x = x + mix @ w_o[l]

        h = rmsnorm(x, norm_ffn[l])
        gu = h @ w_ffn_in[l]
        gate, up = gu[..., :D_FF], gu[..., D_FF:]
        x = x + (jax.nn.silu(gate) * up) @ w_ffn_out[l]
    return rmsnorm(x, norm_final)


import jax as _jax
import jax.numpy as _jnp

TWIN_FORMAT = 'train_step'
FWD_PARAMS = ['x', 'norm_mix', 'w_in', 'w_pool_grp', 'pool_scale', 'w_pool_out', 'conv_w', 'conv_b', 'w_rg_a', 'b_rg_a', 'w_rg_x', 'b_rg_x', 'lru_lambda', 'w_rnn_out', 'w_o', 'norm_ffn', 'w_ffn_in', 'w_ffn_out', 'norm_final']
TWIN_WEIGHTS = ['norm_mix', 'w_in', 'w_pool_grp', 'pool_scale', 'w_pool_out', 'conv_w', 'conv_b', 'w_rg_a', 'b_rg_a', 'w_rg_x', 'b_rg_x', 'lru_lambda', 'w_rnn_out', 'w_o', 'norm_ffn', 'w_ffn_in', 'w_ffn_out', 'norm_final']
TWIN_DIFF_INPUT = 'x'
TWIN_INPUTS = ['x', 'norm_mix', 'w_in', 'w_pool_grp', 'pool_scale', 'w_pool_out', 'conv_w', 'conv_b', 'w_rg_a', 'b_rg_a', 'w_rg_x', 'b_rg_x', 'lru_lambda', 'w_rnn_out', 'w_o', 'norm_ffn', 'w_ffn_in', 'w_ffn_out', 'norm_final', 'loss_target', 'm_norm_mix', 'm_w_in', 'm_w_pool_grp', 'm_pool_scale', 'm_w_pool_out', 'm_conv_w', 'm_conv_b', 'm_w_rg_a', 'm_b_rg_a', 'm_w_rg_x', 'm_b_rg_x', 'm_lru_lambda', 'm_w_rnn_out', 'm_w_o', 'm_norm_ffn', 'm_w_ffn_in', 'm_w_ffn_out', 'm_norm_final', 'v_norm_mix', 'v_w_in', 'v_w_pool_grp', 'v_pool_scale', 'v_w_pool_out', 'v_conv_w', 'v_conv_b', 'v_w_rg_a', 'v_b_rg_a', 'v_w_rg_x', 'v_b_rg_x', 'v_lru_lambda', 'v_w_rnn_out', 'v_w_o', 'v_norm_ffn', 'v_w_ffn_in', 'v_w_ffn_out', 'v_norm_final']
TWIN_OUTPUTS = ['loss', 'grad_x', 'grad_norm_mix', 'grad_w_in', 'grad_w_pool_grp', 'grad_pool_scale', 'grad_w_pool_out', 'grad_conv_w', 'grad_conv_b', 'grad_w_rg_a', 'grad_b_rg_a', 'grad_w_rg_x', 'grad_b_rg_x', 'grad_lru_lambda', 'grad_w_rnn_out', 'grad_w_o', 'grad_norm_ffn', 'grad_w_ffn_in', 'grad_w_ffn_out', 'grad_norm_final', 'delta_norm_mix', 'delta_w_in', 'delta_w_pool_grp', 'delta_pool_scale', 'delta_w_pool_out', 'delta_conv_w', 'delta_conv_b', 'delta_w_rg_a', 'delta_b_rg_a', 'delta_w_rg_x', 'delta_b_rg_x', 'delta_lru_lambda', 'delta_w_rnn_out', 'delta_w_o', 'delta_norm_ffn', 'delta_w_ffn_in', 'delta_w_ffn_out', 'delta_norm_final', 'new_m_norm_mix', 'new_m_w_in', 'new_m_w_pool_grp', 'new_m_pool_scale', 'new_m_w_pool_out', 'new_m_conv_w', 'new_m_conv_b', 'new_m_w_rg_a', 'new_m_b_rg_a', 'new_m_w_rg_x', 'new_m_b_rg_x', 'new_m_lru_lambda', 'new_m_w_rnn_out', 'new_m_w_o', 'new_m_norm_ffn', 'new_m_w_ffn_in', 'new_m_w_ffn_out', 'new_m_norm_final', 'new_v_norm_mix', 'new_v_w_in', 'new_v_w_pool_grp', 'new_v_pool_scale', 'new_v_w_pool_out', 'new_v_conv_w', 'new_v_conv_b', 'new_v_w_rg_a', 'new_v_b_rg_a', 'new_v_w_rg_x', 'new_v_b_rg_x', 'new_v_lru_lambda', 'new_v_w_rnn_out', 'new_v_w_o', 'new_v_norm_ffn', 'new_v_w_ffn_in', 'new_v_w_ffn_out', 'new_v_norm_final']
TWIN_LEAF_KINDS = {'loss': 'loss', 'grad_x': 'grad_x', 'grad_norm_mix': 'grad_w', 'grad_w_in': 'grad_w', 'grad_w_pool_grp': 'grad_w', 'grad_pool_scale': 'grad_w', 'grad_w_pool_out': 'grad_w', 'grad_conv_w': 'grad_w', 'grad_conv_b': 'grad_w', 'grad_w_rg_a': 'grad_w', 'grad_b_rg_a': 'grad_w', 'grad_w_rg_x': 'grad_w', 'grad_b_rg_x': 'grad_w', 'grad_lru_lambda': 'grad_w', 'grad_w_rnn_out': 'grad_w', 'grad_w_o': 'grad_w', 'grad_norm_ffn': 'grad_w', 'grad_w_ffn_in': 'grad_w', 'grad_w_ffn_out': 'grad_w', 'grad_norm_final': 'grad_w', 'delta_norm_mix': 'delta_w', 'delta_w_in': 'delta_w', 'delta_w_pool_grp': 'delta_w', 'delta_pool_scale': 'delta_w', 'delta_w_pool_out': 'delta_w', 'delta_conv_w': 'delta_w', 'delta_conv_b': 'delta_w', 'delta_w_rg_a': 'delta_w', 'delta_b_rg_a': 'delta_w', 'delta_w_rg_x': 'delta_w', 'delta_b_rg_x': 'delta_w', 'delta_lru_lambda': 'delta_w', 'delta_w_rnn_out': 'delta_w', 'delta_w_o': 'delta_w', 'delta_norm_ffn': 'delta_w', 'delta_w_ffn_in': 'delta_w', 'delta_w_ffn_out': 'delta_w', 'delta_norm_final': 'delta_w', 'new_m_norm_mix': 'new_m', 'new_m_w_in': 'new_m', 'new_m_w_pool_grp': 'new_m', 'new_m_pool_scale': 'new_m', 'new_m_w_pool_out': 'new_m', 'new_m_conv_w': 'new_m', 'new_m_conv_b': 'new_m', 'new_m_w_rg_a': 'new_m', 'new_m_b_rg_a': 'new_m', 'new_m_w_rg_x': 'new_m', 'new_m_b_rg_x': 'new_m', 'new_m_lru_lambda': 'new_m', 'new_m_w_rnn_out': 'new_m', 'new_m_w_o': 'new_m', 'new_m_norm_ffn': 'new_m', 'new_m_w_ffn_in': 'new_m', 'new_m_w_ffn_out': 'new_m', 'new_m_norm_final': 'new_m', 'new_v_norm_mix': 'new_v', 'new_v_w_in': 'new_v', 'new_v_w_pool_grp': 'new_v', 'new_v_pool_scale': 'new_v', 'new_v_w_pool_out': 'new_v', 'new_v_conv_w': 'new_v', 'new_v_conv_b': 'new_v', 'new_v_w_rg_a': 'new_v', 'new_v_b_rg_a': 'new_v', 'new_v_w_rg_x': 'new_v', 'new_v_b_rg_x': 'new_v', 'new_v_lru_lambda': 'new_v', 'new_v_w_rnn_out': 'new_v', 'new_v_w_o': 'new_v', 'new_v_norm_ffn': 'new_v', 'new_v_w_ffn_in': 'new_v', 'new_v_w_ffn_out': 'new_v', 'new_v_norm_final': 'new_v'}


def _forward(args):
    return _fwd_reference(*[args[k] for k in FWD_PARAMS])


def _output_shape():
    def fwd():
        inp = _fwd_setup_inputs(0)
        return _fwd_reference(*[inp[k] for k in FWD_PARAMS])
    out = _jax.eval_shape(fwd)
    return out.shape, out.dtype

N_MICROBATCH = 1
ADAM_LR = 0.001
ADAM_B1 = 0.9
ADAM_B2 = 0.999
ADAM_EPS = 1e-08
ADAM_WD = 0.01
ADAM_STEP = 10
PER_EXAMPLE_BATCH_AXIS = {'x': 0, 'loss_target': 0}
SHARED_INPUTS = []
_WEIGHT_DTYPES = {'norm_mix': _jnp.float32, 'w_in': _jnp.float32, 'w_pool_grp': _jnp.float32, 'pool_scale': _jnp.float32, 'w_pool_out': _jnp.float32, 'conv_w': _jnp.float32, 'conv_b': _jnp.float32, 'w_rg_a': _jnp.float32, 'b_rg_a': _jnp.float32, 'w_rg_x': _jnp.float32, 'b_rg_x': _jnp.float32, 'lru_lambda': _jnp.float32, 'w_rnn_out': _jnp.float32, 'w_o': _jnp.float32, 'norm_ffn': _jnp.float32, 'w_ffn_in': _jnp.float32, 'w_ffn_out': _jnp.float32, 'norm_final': _jnp.float32}
MOMENT_SCALE = {'norm_mix': 1.083263e-01, 'w_in': 5.018704e-02, 'w_pool_grp': 1.192621e-01, 'pool_scale': 1.311352e-01, 'w_pool_out': 8.291022e-02, 'conv_w': 4.258505e-02, 'conv_b': 4.783169e-01, 'w_rg_a': 1.351920e-02, 'b_rg_a': 9.868131e-03, 'w_rg_x': 2.418329e-02, 'b_rg_x': 1.343083e-02, 'lru_lambda': 1.881815e-02, 'w_rnn_out': 4.008137e-02, 'w_o': 9.202318e-02, 'norm_ffn': 1.236777e-01, 'w_ffn_in': 5.095568e-02, 'w_ffn_out': 8.306365e-02, 'norm_final': 3.200607e+01}


def _to_microbatches(a, axis):
    t = _jnp.moveaxis(a, axis, 0)
    t = t.reshape((N_MICROBATCH, t.shape[0] // N_MICROBATCH) + t.shape[1:])
    return _jnp.moveaxis(t, 1, axis + 1)


def setup_inputs(seed: int = 0) -> dict:
    inp = _fwd_setup_inputs(seed)
    key = _jax.random.fold_in(_jax.random.key(seed), 7919)
    shape, _ = _output_shape()
    out = dict(inp)
    out["loss_target"] = _jax.random.normal(_jax.random.fold_in(key, 0), shape, _jnp.float32)
    for i, name in enumerate(TWIN_WEIGHTS):
        w = inp[name].astype(_jnp.float32)
        if MOMENT_SCALE is None:
            s = _jnp.sqrt(_jnp.mean(_jnp.square(w)) + 1e-30)
        else:
            s = MOMENT_SCALE[name]
        km, kv = _jax.random.split(_jax.random.fold_in(key, i + 1))
        out[name] = w
        out["m_" + name] = s * _jax.random.normal(km, w.shape, _jnp.float32)
        out["v_" + name] = (s * s) * _jax.random.uniform(kv, w.shape, _jnp.float32, 0.5, 1.5)
    if N_MICROBATCH > 1:
        for name, axis in PER_EXAMPLE_BATCH_AXIS.items():
            out[name] = _to_microbatches(out[name], axis)
    return {'x': out['x'], 'norm_mix': out['norm_mix'], 'w_in': out['w_in'], 'w_pool_grp': out['w_pool_grp'], 'pool_scale': out['pool_scale'], 'w_pool_out': out['w_pool_out'], 'conv_w': out['conv_w'], 'conv_b': out['conv_b'], 'w_rg_a': out['w_rg_a'], 'b_rg_a': out['b_rg_a'], 'w_rg_x': out['w_rg_x'], 'b_rg_x': out['b_rg_x'], 'lru_lambda': out['lru_lambda'], 'w_rnn_out': out['w_rnn_out'], 'w_o': out['w_o'], 'norm_ffn': out['norm_ffn'], 'w_ffn_in': out['w_ffn_in'], 'w_ffn_out': out['w_ffn_out'], 'norm_final': out['norm_final'], 'loss_target': out['loss_target'], 'm_norm_mix': out['m_norm_mix'], 'm_w_in': out['m_w_in'], 'm_w_pool_grp': out['m_w_pool_grp'], 'm_pool_scale': out['m_pool_scale'], 'm_w_pool_out': out['m_w_pool_out'], 'm_conv_w': out['m_conv_w'], 'm_conv_b': out['m_conv_b'], 'm_w_rg_a': out['m_w_rg_a'], 'm_b_rg_a': out['m_b_rg_a'], 'm_w_rg_x': out['m_w_rg_x'], 'm_b_rg_x': out['m_b_rg_x'], 'm_lru_lambda': out['m_lru_lambda'], 'm_w_rnn_out': out['m_w_rnn_out'], 'm_w_o': out['m_w_o'], 'm_norm_ffn': out['m_norm_ffn'], 'm_w_ffn_in': out['m_w_ffn_in'], 'm_w_ffn_out': out['m_w_ffn_out'], 'm_norm_final': out['m_norm_final'], 'v_norm_mix': out['v_norm_mix'], 'v_w_in': out['v_w_in'], 'v_w_pool_grp': out['v_w_pool_grp'], 'v_pool_scale': out['v_pool_scale'], 'v_w_pool_out': out['v_w_pool_out'], 'v_conv_w': out['v_conv_w'], 'v_conv_b': out['v_conv_b'], 'v_w_rg_a': out['v_w_rg_a'], 'v_b_rg_a': out['v_b_rg_a'], 'v_w_rg_x': out['v_w_rg_x'], 'v_b_rg_x': out['v_b_rg_x'], 'v_lru_lambda': out['v_lru_lambda'], 'v_w_rnn_out': out['v_w_rnn_out'], 'v_w_o': out['v_w_o'], 'v_norm_ffn': out['v_norm_ffn'], 'v_w_ffn_in': out['v_w_ffn_in'], 'v_w_ffn_out': out['v_w_ffn_out'], 'v_norm_final': out['v_norm_final']}


def _loss(weights, diff, rest, loss_target):
    with _jax.named_scope("forward"):
        args = {**rest, TWIN_DIFF_INPUT: diff, **{k: w.astype(_WEIGHT_DTYPES[k]) for k, w in weights.items()}}
        y = _forward(args)
    with _jax.named_scope("loss_head"):
        err = _jnp.square(y.astype(_jnp.float32) - loss_target)
        return 0.5 * _jnp.sum(_jnp.mean(err, axis=-1)) if err.ndim else 0.5 * err


def _adamw(w, g, m, v):
    m = ADAM_B1 * m + (1.0 - ADAM_B1) * g
    v = ADAM_B2 * v + (1.0 - ADAM_B2) * _jnp.square(g)
    m_hat = m / (1.0 - ADAM_B1 ** ADAM_STEP)
    v_hat = v / (1.0 - ADAM_B2 ** ADAM_STEP)
    delta = -ADAM_LR * (m_hat / (_jnp.sqrt(v_hat) + ADAM_EPS) + ADAM_WD * w)
    return delta, m, v


def reference(x, norm_mix, w_in, w_pool_grp, pool_scale, w_pool_out, conv_w, conv_b, w_rg_a, b_rg_a, w_rg_x, b_rg_x, lru_lambda, w_rnn_out, w_o, norm_ffn, w_ffn_in, w_ffn_out, norm_final, loss_target, m_norm_mix, m_w_in, m_w_pool_grp, m_pool_scale, m_w_pool_out, m_conv_w, m_conv_b, m_w_rg_a, m_b_rg_a, m_w_rg_x, m_b_rg_x, m_lru_lambda, m_w_rnn_out, m_w_o, m_norm_ffn, m_w_ffn_in, m_w_ffn_out, m_norm_final, v_norm_mix, v_w_in, v_w_pool_grp, v_pool_scale, v_w_pool_out, v_conv_w, v_conv_b, v_w_rg_a, v_b_rg_a, v_w_rg_x, v_b_rg_x, v_lru_lambda, v_w_rnn_out, v_w_o, v_norm_ffn, v_w_ffn_in, v_w_ffn_out, v_norm_final):
    given = dict(x=x, norm_mix=norm_mix, w_in=w_in, w_pool_grp=w_pool_grp, pool_scale=pool_scale, w_pool_out=w_pool_out, conv_w=conv_w, conv_b=conv_b, w_rg_a=w_rg_a, b_rg_a=b_rg_a, w_rg_x=w_rg_x, b_rg_x=b_rg_x, lru_lambda=lru_lambda, w_rnn_out=w_rnn_out, w_o=w_o, norm_ffn=norm_ffn, w_ffn_in=w_ffn_in, w_ffn_out=w_ffn_out, norm_final=norm_final, loss_target=loss_target, m_norm_mix=m_norm_mix, m_w_in=m_w_in, m_w_pool_grp=m_w_pool_grp, m_pool_scale=m_pool_scale, m_w_pool_out=m_w_pool_out, m_conv_w=m_conv_w, m_conv_b=m_conv_b, m_w_rg_a=m_w_rg_a, m_b_rg_a=m_b_rg_a, m_w_rg_x=m_w_rg_x, m_b_rg_x=m_b_rg_x, m_lru_lambda=m_lru_lambda, m_w_rnn_out=m_w_rnn_out, m_w_o=m_w_o, m_norm_ffn=m_norm_ffn, m_w_ffn_in=m_w_ffn_in, m_w_ffn_out=m_w_ffn_out, m_norm_final=m_norm_final, v_norm_mix=v_norm_mix, v_w_in=v_w_in, v_w_pool_grp=v_w_pool_grp, v_pool_scale=v_pool_scale, v_w_pool_out=v_w_pool_out, v_conv_w=v_conv_w, v_conv_b=v_conv_b, v_w_rg_a=v_w_rg_a, v_b_rg_a=v_b_rg_a, v_w_rg_x=v_w_rg_x, v_b_rg_x=v_b_rg_x, v_lru_lambda=v_lru_lambda, v_w_rnn_out=v_w_rnn_out, v_w_o=v_w_o, v_norm_ffn=v_norm_ffn, v_w_ffn_in=v_w_ffn_in, v_w_ffn_out=v_w_ffn_out, v_norm_final=v_norm_final)
    weights = {n: given[n] for n in TWIN_WEIGHTS}
    shared = {n: given[n] for n in SHARED_INPUTS}
    per_example = {n: given[n] for n in ['x']}
    grad_fn = _jax.value_and_grad(_loss, argnums=(0, 1))

    def one_microbatch(ex, loss_target):
        ex = dict(ex)
        diff = ex.pop(TWIN_DIFF_INPUT)
        return grad_fn(weights, diff, {**shared, **ex}, loss_target)

    if N_MICROBATCH == 1:
        loss, (grad_w, grad_x) = one_microbatch(per_example, given["loss_target"])
    else:
        def body(carry, xs):
            loss_sum, grad_sum = carry
            l_k, (gw_k, gx_k) = one_microbatch(xs[0], xs[1])
            with _jax.named_scope("update"):
                return (loss_sum + l_k, _jax.tree.map(_jnp.add, grad_sum, gw_k)), gx_k

        init = (_jnp.zeros((), _jnp.float32), _jax.tree.map(_jnp.zeros_like, weights))
        (loss, grad_w), grad_x = _jax.lax.scan(body, init, (per_example, given["loss_target"]))
    with _jax.named_scope("update"):
        delta_w, new_m, new_v = {}, {}, {}
        for n in TWIN_WEIGHTS:
            delta_w[n], new_m[n], new_v[n] = _adamw(weights[n], grad_w[n], given["m_" + n], given["v_" + n])
    return (loss, grad_x, *[grad_w[n] for n in TWIN_WEIGHTS], *[delta_w[n] for n in TWIN_WEIGHTS],
            *[new_m[n] for n in TWIN_WEIGHTS], *[new_v[n] for n in TWIN_WEIGHTS])
```

```python
import functools
import math

import jax
import jax.numpy as jnp
from jax import lax
from jax.experimental import pallas as pl
from jax.experimental.pallas import tpu as pltpu

F32 = jnp.float32
BF16 = jnp.bfloat16

D_MODEL = 1024
D_POOL = 512
N_POOL_GROUPS = 4
D_RNN = 1024
N_RNN_HEADS = 8
HEAD = 128
CONV_WIDTH = 4
LRU_C = 8.0
D_FF = 2816
D_IN = D_POOL + 2 * D_RNN + 2 * D_MODEL
NORM_EPS = 1e-6
COL_RNN = D_POOL // HEAD
COL_GATE = (D_POOL + D_RNN) // HEAD

ADAM_LR = 0.001
ADAM_B1 = 0.9
ADAM_B2 = 0.999
ADAM_EPS = 1e-08
ADAM_WD = 0.01
ADAM_STEP = 10

N_CHIPS = 4
N_DEV = 8
MESH = pl.DeviceIdType.MESH
ANY = pl.BlockSpec(memory_space=pl.ANY)
VMEM = pl.BlockSpec(memory_space=pltpu.VMEM)
VMEM_LIMIT_BYTES = 60 * 1024 * 1024
SUBLANES = 8
POOL_HALO = 16
CHUNK = 512

GELU_C = math.sqrt(2.0 / math.pi)
GELU_A = 0.044715


def _params(**kw):
    return pltpu.CompilerParams(vmem_limit_bytes=VMEM_LIMIT_BYTES, **kw)


def _sigmoid(x):
    return 1.0 / (1.0 + jnp.exp(-x))


def _log1p(y):
    u = 1.0 + y
    d = u - 1.0
    return jnp.where(d == 0.0, y, jnp.log(u) * (y / jnp.where(d == 0.0, 1.0, d)))


def _gelu_parts(x):
    x2 = x * x
    th = jnp.tanh(GELU_C * (x + GELU_A * x * x2))
    g = 0.5 * x * (1.0 + th)
    dg = 0.5 * (1.0 + th) + 0.5 * x * (1.0 - th * th) * GELU_C * (1.0 + 3.0 * GELU_A * x2)
    return g, dg


def _dot(a, b):
    return jnp.dot(a, b, preferred_element_type=F32)


def _dot_nt(a, b):
    return lax.dot_general(a, b, (((1,), (1,)), ((), ())), preferred_element_type=F32)


def _dot_tn(a, b):
    return lax.dot_general(a, b, (((0,), (0,)), ((), ())), preferred_element_type=F32)


def _rms_scale(xv):
    return lax.rsqrt(jnp.mean(xv * xv, axis=-1, keepdims=True) + NORM_EPS)


def _rms_bwd(dy, xv, g):
    r = _rms_scale(xv)
    xh = xv * r
    dyg = dy * g
    dx = r * (dyg - xh * jnp.mean(dyg * xh, axis=-1, keepdims=True))
    return dx, dy * xh


def _norm_matmul(x, g, w, *, tm, tn, name):
    T, K = x.shape
    N = w.shape[1]

    def body(x_ref, g_ref, w_ref, o_ref, h_ref):
        @pl.when(pl.program_id(1) == 0)
        def _():
            xv = x_ref[...]
            h_ref[...] = (xv * _rms_scale(xv) * g_ref[...]).astype(BF16)

        o_ref[...] = _dot(h_ref[...], w_ref[...])

    return pl.pallas_call(
        body, name=name, grid=(T // tm, N // tn),
        in_specs=[pl.BlockSpec((tm, K), lambda i, j: (i, 0)), pl.BlockSpec((1, K), lambda i, j: (0, 0)),
                  pl.BlockSpec((K, tn), lambda i, j: (0, j))],
        out_specs=[pl.BlockSpec((tm, tn), lambda i, j: (i, j)), pl.BlockSpec((tm, K), lambda i, j: (i, 0))],
        out_shape=[jax.ShapeDtypeStruct((T, N), F32), jax.ShapeDtypeStruct((T, K), BF16)],
        compiler_params=_params(dimension_semantics=("parallel", "arbitrary")),
    )(x, g, w)


def _ffn_in(x2, g, w, *, tm, tn):
    T, K = x2.shape
    nb = D_FF // tn

    def body(x_ref, g_ref, wg_ref, wu_ref, gate_ref, up_ref, act_ref, h_ref):
        @pl.when(pl.program_id(1) == 0)
        def _():
            xv = x_ref[...]
            h_ref[...] = (xv * _rms_scale(xv) * g_ref[...]).astype(BF16)

        h = h_ref[...]
        gate = _dot(h, wg_ref[...])
        up = _dot(h, wu_ref[...])
        gate_ref[...] = gate
        up_ref[...] = up
        act_ref[...] = (gate * _sigmoid(gate) * up).astype(BF16)

    blk = pl.BlockSpec((tm, tn), lambda i, j: (i, j))
    return pl.pallas_call(
        body, name="ffn_in", grid=(T // tm, nb),
        in_specs=[pl.BlockSpec((tm, K), lambda i, j: (i, 0)), pl.BlockSpec((1, K), lambda i, j: (0, 0)),
                  pl.BlockSpec((K, tn), lambda i, j: (0, j)), pl.BlockSpec((K, tn), lambda i, j: (0, j + nb))],
        out_specs=[blk, blk, blk, pl.BlockSpec((tm, K), lambda i, j: (i, 0))],
        out_shape=[jax.ShapeDtypeStruct((T, D_FF), F32), jax.ShapeDtypeStruct((T, D_FF), F32),
                   jax.ShapeDtypeStruct((T, D_FF), BF16), jax.ShapeDtypeStruct((T, K), BF16)],
        compiler_params=_params(dimension_semantics=("parallel", "arbitrary")),
    )(x2, g, w, w)


def _branch_mix(pm, z, w_pool_out, w_rnn_out, proj, *, tm, tn):
    T = pm.shape[0]
    col_gp = (D_POOL + 2 * D_RNN) // tn
    col_gr = col_gp + D_MODEL // tn

    def body(pm_ref, z_ref, wp_ref, wr_ref, gp_ref, gr_ref, yp_ref, yr_ref, mix_ref):
        yp = _dot(pm_ref[...], wp_ref[...])
        yr = _dot(z_ref[...], wr_ref[...])
        yp_ref[...] = yp
        yr_ref[...] = yr
        mix_ref[...] = (_sigmoid(gp_ref[...]) * yp + _sigmoid(gr_ref[...]) * yr).astype(BF16)

    blk = pl.BlockSpec((tm, tn), lambda i, j: (i, j))
    return pl.pallas_call(
        body, name="branch_mix", grid=(T // tm, D_MODEL // tn),
        in_specs=[pl.BlockSpec((tm, D_POOL), lambda i, j: (i, 0)), pl.BlockSpec((tm, D_RNN), lambda i, j: (i, 0)),
                  pl.BlockSpec((D_POOL, tn), lambda i, j: (0, j)), pl.BlockSpec((D_RNN, tn), lambda i, j: (0, j)),
                  pl.BlockSpec((tm, tn), lambda i, j: (i, col_gp + j)), pl.BlockSpec((tm, tn), lambda i, j: (i, col_gr + j))],
        out_specs=[blk, blk, blk],
        out_shape=[jax.ShapeDtypeStruct((T, D_MODEL), F32), jax.ShapeDtypeStruct((T, D_MODEL), F32),
                   jax.ShapeDtypeStruct((T, D_MODEL), BF16)],
        compiler_params=_params(dimension_semantics=("parallel", "parallel")),
    )(pm, z, w_pool_out, w_rnn_out, proj, proj)


def _out_proj_residual(mix, w_o, x, *, tm):
    T = x.shape[0]

    def body(mix_ref, w_ref, x_ref, o_ref):
        o_ref[...] = x_ref[...] + _dot(mix_ref[...], w_ref[...])

    row = pl.BlockSpec((tm, D_MODEL), lambda i: (i, 0))
    return pl.pallas_call(
        body, name="out_proj_residual", grid=(T // tm,),
        in_specs=[row, pl.BlockSpec((D_MODEL, D_MODEL), lambda i: (0, 0)), row],
        out_specs=row, out_shape=jax.ShapeDtypeStruct((T, D_MODEL), F32),
        compiler_params=_params(dimension_semantics=("parallel",)),
    )(mix, w_o, x)


def _ffn_out_loss(act, w, x2, g3, target, *, tm):
    T = x2.shape[0]

    def body(act_ref, w_ref, x2_ref, g_ref, t_ref, dx_ref, dxb_ref, sq_ref, dg_ref):
        @pl.when(pl.program_id(0) == 0)
        def _():
            sq_ref[...] = jnp.zeros_like(sq_ref)
            dg_ref[...] = jnp.zeros_like(dg_ref)

        x3 = x2_ref[...] + _dot(act_ref[...], w_ref[...])
        g = g_ref[...]
        err = x3 * _rms_scale(x3) * g - t_ref[...]
        sq_ref[...] += jnp.sum(err * err, axis=0, keepdims=True)
        dx, dgp = _rms_bwd(err * (1.0 / D_MODEL), x3, g)
        dg_ref[...] += jnp.sum(dgp, axis=0, keepdims=True)
        dx_ref[...] = dx
        dxb_ref[...] = dx.astype(BF16)

    row = pl.BlockSpec((tm, D_MODEL), lambda i: (i, 0))
    vec = pl.BlockSpec((1, D_MODEL), lambda i: (0, 0))
    return pl.pallas_call(
        body, name="ffn_out_loss", grid=(T // tm,),
        in_specs=[pl.BlockSpec((tm, D_FF), lambda i: (i, 0)), pl.BlockSpec((D_FF, D_MODEL), lambda i: (0, 0)), row, vec, row],
        out_specs=[row, row, vec, vec],
        out_shape=[jax.ShapeDtypeStruct((T, D_MODEL), F32), jax.ShapeDtypeStruct((T, D_MODEL), BF16),
                   jax.ShapeDtypeStruct((1, D_MODEL), F32), jax.ShapeDtypeStruct((1, D_MODEL), F32)],
        compiler_params=_params(dimension_semantics=("arbitrary",)),
    )(act, w, x2, g3, target)


def _ffn_out_bwd(dx3b, w, gate, up, *, tm, tn):
    T = dx3b.shape[0]

    def body(dx_ref, w_ref, gate_ref, up_ref, dgate_ref, dup_ref):
        dact = _dot_nt(dx_ref[...], w_ref[...])
        gate = gate_ref[...]
        s = _sigmoid(gate)
        dgate_ref[...] = (dact * up_ref[...] * s * (1.0 + gate * (1.0 - s))).astype(BF16)
        dup_ref[...] = (dact * gate * s).astype(BF16)

    blk = pl.BlockSpec((tm, tn), lambda i, j: (i, j))
    return pl.pallas_call(
        body, name="ffn_out_bwd", grid=(T // tm, D_FF // tn),
        in_specs=[pl.BlockSpec((tm, D_MODEL), lambda i, j: (i, 0)), pl.BlockSpec((tn, D_MODEL), lambda i, j: (j, 0)), blk, blk],
        out_specs=[blk, blk],
        out_shape=[jax.ShapeDtypeStruct((T, D_FF), BF16), jax.ShapeDtypeStruct((T, D_FF), BF16)],
        compiler_params=_params(dimension_semantics=("parallel", "parallel")),
    )(dx3b, w, gate, up)


def _ffn_in_bwd(dgate, dup, w, dx3, x2, g2, *, tm):
    T = x2.shape[0]

    def body(dgate_ref, dup_ref, w_ref, dx3_ref, x2_ref, g_ref, dx_ref, dxb_ref, dg_ref):
        @pl.when(pl.program_id(0) == 0)
        def _():
            dg_ref[...] = jnp.zeros_like(dg_ref)

        dh = _dot_nt(dgate_ref[...], w_ref[:, :D_FF]) + _dot_nt(dup_ref[...], w_ref[:, D_FF:])
        dxn, dgp = _rms_bwd(dh, x2_ref[...], g_ref[...])
        dx = dx3_ref[...] + dxn
        dg_ref[...] += jnp.sum(dgp, axis=0, keepdims=True)
        dx_ref[...] = dx
        dxb_ref[...] = dx.astype(BF16)

    row = pl.BlockSpec((tm, D_MODEL), lambda i: (i, 0))
    wide = pl.BlockSpec((tm, D_FF), lambda i: (i, 0))
    vec = pl.BlockSpec((1, D_MODEL), lambda i: (0, 0))
    return pl.pallas_call(
        body, name="ffn_in_bwd", grid=(T // tm,),
        in_specs=[wide, wide, pl.BlockSpec((D_MODEL, 2 * D_FF), lambda i: (0, 0)), row, row, vec],
        out_specs=[row, row, vec],
        out_shape=[jax.ShapeDtypeStruct((T, D_MODEL), F32), jax.ShapeDtypeStruct((T, D_MODEL), BF16),
                   jax.ShapeDtypeStruct((1, D_MODEL), F32)],
        compiler_params=_params(dimension_semantics=("arbitrary",)),
    )(dgate, dup, w, dx3, x2, g2)


def _out_proj_bwd(dx2b, w_o, proj, y_pool, y_rnn, *, tm, tn):
    T = dx2b.shape[0]
    col_gp = (D_POOL + 2 * D_RNN) // tn
    col_gr = col_gp + D_MODEL // tn

    def body(dx_ref, w_ref, gp_ref, gr_ref, yp_ref, yr_ref, dgp_ref, dgr_ref, dyp_ref, dyr_ref):
        dmix = _dot_nt(dx_ref[...], w_ref[...])
        sp = _sigmoid(gp_ref[...])
        sr = _sigmoid(gr_ref[...])
        dgp_ref[...] = (dmix * yp_ref[...] * sp * (1.0 - sp)).astype(BF16)
        dgr_ref[...] = (dmix * yr_ref[...] * sr * (1.0 - sr)).astype(BF16)
        dyp_ref[...] = (dmix * sp).astype(BF16)
        dyr_ref[...] = (dmix * sr).astype(BF16)

    blk = pl.BlockSpec((tm, tn), lambda i, j: (i, j))
    out = jax.ShapeDtypeStruct((T, D_MODEL), BF16)
    return pl.pallas_call(
        body, name="out_proj_bwd", grid=(T // tm, D_MODEL // tn),
        in_specs=[pl.BlockSpec((tm, D_MODEL), lambda i, j: (i, 0)), pl.BlockSpec((tn, D_MODEL), lambda i, j: (j, 0)),
                  pl.BlockSpec((tm, tn), lambda i, j: (i, col_gp + j)), pl.BlockSpec((tm, tn), lambda i, j: (i, col_gr + j)), blk, blk],
        out_specs=[blk, blk, blk, blk], out_shape=[out, out, out, out],
        compiler_params=_params(dimension_semantics=("parallel", "parallel")),
    )(dx2b, w_o, proj, proj, y_pool, y_rnn)


def _branch_bwd(dyp, dyr, w_pool_out, w_rnn_out, *, tm):
    T = dyp.shape[0]

    def body(dyp_ref, dyr_ref, wp_ref, wr_ref, dpm_ref, dz_ref):
        dpm_ref[...] = _dot_nt(dyp_ref[...], wp_ref[...])
        dz_ref[...] = _dot_nt(dyr_ref[...], wr_ref[...])

    row = pl.BlockSpec((tm, D_MODEL), lambda i: (i, 0))
    return pl.pallas_call(
        body, name="branch_bwd", grid=(T // tm,),
        in_specs=[row, row, pl.BlockSpec((D_POOL, D_MODEL), lambda i: (0, 0)), pl.BlockSpec((D_RNN, D_MODEL), lambda i: (0, 0))],
        out_specs=[pl.BlockSpec((tm, D_POOL), lambda i: (i, 0)), pl.BlockSpec((tm, D_RNN), lambda i: (i, 0))],
        out_shape=[jax.ShapeDtypeStruct((T, D_POOL), F32), jax.ShapeDtypeStruct((T, D_RNN), F32)],
        compiler_params=_params(dimension_semantics=("parallel",)),
    )(dyp, dyr, w_pool_out, w_rnn_out)


def _in_proj_bwd(segs, w, dx2, x, g1, *, tm):
    T = x.shape[0]
    widths = [s.shape[1] for s in segs]
    offs = [sum(widths[:k]) for k in range(len(widths))]
    n = len(segs)

    def body(*refs):
        seg_refs, (w_ref, dx2_ref, x_ref, g_ref, dx_ref, dg_ref) = refs[:n], refs[n:]

        @pl.when(pl.program_id(0) == 0)
        def _():
            dg_ref[...] = jnp.zeros_like(dg_ref)

        dh = _dot_nt(seg_refs[0][...], w_ref[:, offs[0]:offs[0] + widths[0]])
        for k in range(1, n):
            dh += _dot_nt(seg_refs[k][...], w_ref[:, offs[k]:offs[k] + widths[k]])
        dxn, dgp = _rms_bwd(dh, x_ref[...], g_ref[...])
        dg_ref[...] += jnp.sum(dgp, axis=0, keepdims=True)
        dx_ref[...] = dx2_ref[...] + dxn

    row = pl.BlockSpec((tm, D_MODEL), lambda i: (i, 0))
    vec = pl.BlockSpec((1, D_MODEL), lambda i: (0, 0))
    return pl.pallas_call(
        body, name="in_proj_bwd", grid=(T // tm,),
        in_specs=[pl.BlockSpec((tm, wd), lambda i: (i, 0)) for wd in widths]
        + [pl.BlockSpec((D_MODEL, D_IN), lambda i: (0, 0)), row, row, vec],
        out_specs=[row, vec],
        out_shape=[jax.ShapeDtypeStruct((T, D_MODEL), F32), jax.ShapeDtypeStruct((1, D_MODEL), F32)],
        compiler_params=_params(dimension_semantics=("arbitrary",)),
    )(*segs, w, dx2, x, g1)


def _weight_grad(a, segs, *, tm, tn, name):
    T, M = a.shape
    nblk = [s.shape[1] // tn for s in segs]
    first = [sum(nblk[:k]) for k in range(len(segs))]
    n = len(segs)

    def body(a_ref, *refs):
        seg_refs, o_ref = refs[:n], refs[n]
        j = pl.program_id(1)
        for k in range(n):
            @pl.when((j >= first[k]) & (j < first[k] + nblk[k]))
            def _(k=k):
                o_ref[...] = _dot_tn(a_ref[...], seg_refs[k][...])

    def seg_spec(k):
        return pl.BlockSpec((T, tn), lambda i, j: (0, jnp.clip(j - first[k], 0, nblk[k] - 1)))

    return pl.pallas_call(
        body, name=name, grid=(M // tm, sum(nblk)),
        in_specs=[pl.BlockSpec((T, tm), lambda i, j: (0, i))] + [seg_spec(k) for k in range(n)],
        out_specs=pl.BlockSpec((tm, tn), lambda i, j: (i, j)),
        out_shape=jax.ShapeDtypeStruct((M, sum(nblk) * tn), F32),
        compiler_params=_params(dimension_semantics=("parallel", "arbitrary")),
    )(a, *segs)


def _shift_rows(v, k):
    return pltpu.roll(v, k % v.shape[0], axis=0)


def _window_sums(xs, direction):
    s2 = xs + _shift_rows(xs, direction)
    s4 = s2 + _shift_rows(s2, 2 * direction)
    s8 = s4 + _shift_rows(s4, 4 * direction)
    s16 = s8 + _shift_rows(s8, 8 * direction)
    return s2, s4, s8, s16


def _select_window(g, sums):
    s2, s4, s8, s16 = sums
    return jnp.where(g == 0, s2, jnp.where(g == 1, s4, jnp.where(g == 2, s8, s16)))


def _pool_count(g, start, rows):
    t = start + lax.broadcasted_iota(jnp.int32, (rows, 1), 0)
    return jnp.minimum(t + 1, jnp.left_shift(2, g)).astype(F32)


def _pool_fwd(proj, w_grp, scale):
    T = proj.shape[0]
    nchunk = T // CHUNK

    def body(u_ref, w_ref, s_ref, o_ref, upad):
        g = pl.program_id(0)
        upad[pl.ds(0, POOL_HALO), :] = jnp.zeros((POOL_HALO, HEAD), F32)
        w = w_ref[...].astype(BF16)
        scale_row = s_ref[...]

        def chunk(i, carry):
            r0 = pl.multiple_of(i * CHUNK, CHUNK)
            upad[pl.ds(r0 + POOL_HALO, CHUNK), :] = u_ref[pl.ds(r0, CHUNK), :]
            xs = upad[pl.ds(r0, CHUNK + POOL_HALO), :]
            win = _select_window(g, _window_sums(xs, 1))[POOL_HALO:]
            pooled = win / _pool_count(g, r0, CHUNK) - xs[POOL_HALO:]
            o_ref[pl.ds(r0, CHUNK), :] = (_dot(pooled.astype(BF16), w) * scale_row).astype(BF16)
            return carry

        lax.fori_loop(0, nchunk, chunk, 0)

    return pl.pallas_call(
        body, name="pool_fwd", grid=(N_POOL_GROUPS,),
        in_specs=[pl.BlockSpec((T, HEAD), lambda g: (0, g)), pl.BlockSpec((None, HEAD, HEAD), lambda g: (g, 0, 0)),
                  pl.BlockSpec((1, HEAD), lambda g: (0, g))],
        out_specs=pl.BlockSpec((T, HEAD), lambda g: (0, g)),
        out_shape=jax.ShapeDtypeStruct((T, D_POOL), BF16),
        scratch_shapes=[pltpu.VMEM((T + POOL_HALO, HEAD), F32)],
        compiler_params=_params(dimension_semantics=("parallel",)),
    )(proj, w_grp, scale)


def _pool_bwd(proj, dpm, w_grp, scale):
    T = proj.shape[0]
    nchunk = T // CHUNK

    def body(u_ref, dpm_ref, w_ref, s_ref, du_ref, dw_ref, ds_ref, upad, zpad, dpool):
        g = pl.program_id(0)
        upad[pl.ds(0, POOL_HALO), :] = jnp.zeros((POOL_HALO, HEAD), F32)
        zpad[pl.ds(T, POOL_HALO), :] = jnp.zeros((POOL_HALO, HEAD), F32)
        dw_ref[...] = jnp.zeros_like(dw_ref)
        ds_ref[...] = jnp.zeros_like(ds_ref)
        w = w_ref[...].astype(BF16)
        scale_row = s_ref[...]

        def chunk(i, carry):
            r0 = pl.multiple_of(i * CHUNK, CHUNK)
            upad[pl.ds(r0 + POOL_HALO, CHUNK), :] = u_ref[pl.ds(r0, CHUNK), :]
            xs = upad[pl.ds(r0, CHUNK + POOL_HALO), :]
            cnt = _pool_count(g, r0, CHUNK)
            pooled = (_select_window(g, _window_sums(xs, 1))[POOL_HALO:] / cnt - xs[POOL_HALO:]).astype(BF16)
            mixed = _dot(pooled, w)
            d = dpm_ref[pl.ds(r0, CHUNK), :]
            ds_ref[...] += jnp.sum(d * mixed, axis=0, keepdims=True)
            dmixed = (d * scale_row).astype(BF16)
            dw_ref[...] += _dot_tn(pooled, dmixed)
            dp = _dot_nt(dmixed, w)
            dpool[pl.ds(r0, CHUNK), :] = dp
            zpad[pl.ds(r0, CHUNK), :] = dp / cnt
            return carry

        lax.fori_loop(0, nchunk, chunk, 0)

        def chunk2(i, carry):
            r0 = pl.multiple_of(i * CHUNK, CHUNK)
            zs = zpad[pl.ds(r0, CHUNK + POOL_HALO), :]
            win = _select_window(g, _window_sums(zs, -1))[:CHUNK]
            du_ref[pl.ds(r0, CHUNK), :] = (win - dpool[pl.ds(r0, CHUNK), :]).astype(BF16)
            return carry

        lax.fori_loop(0, nchunk, chunk2, 0)

    col = pl.BlockSpec((T, HEAD), lambda g: (0, g))
    return pl.pallas_call(
        body, name="pool_bwd", grid=(N_POOL_GROUPS,),
        in_specs=[col, col, pl.BlockSpec((None, HEAD, HEAD), lambda g: (g, 0, 0)), pl.BlockSpec((1, HEAD), lambda g: (0, g))],
        out_specs=[col, pl.BlockSpec((None, HEAD, HEAD), lambda g: (g, 0, 0)), pl.BlockSpec((1, HEAD), lambda g: (0, g))],
        out_shape=[jax.ShapeDtypeStruct((T, D_POOL), BF16), jax.ShapeDtypeStruct((N_POOL_GROUPS, HEAD, HEAD), F32),
                   jax.ShapeDtypeStruct((1, D_POOL), F32)],
        scratch_shapes=[pltpu.VMEM((T + POOL_HALO, HEAD), F32), pltpu.VMEM((T + POOL_HALO, HEAD), F32), pltpu.VMEM((T, HEAD), F32)],
        compiler_params=_params(dimension_semantics=("parallel",)),
    )(proj, dpm, w_grp, scale)


def _conv_taps(xs, cw):
    v = cw[CONV_WIDTH - 1] * xs[SUBLANES:]
    for k in range(CONV_WIDTH - 1):
        v += cw[k] * _shift_rows(xs, CONV_WIDTH - 1 - k)[SUBLANES:]
    return v


def _tap_rows(cw_ref):
    return [cw_ref[k:k + 1, :] for k in range(CONV_WIDTH)]


def _softplus_neg(lam):
    return jnp.maximum(-lam, 0.0) + _log1p(jnp.exp(-jnp.abs(lam)))


def _lru_gates(v, wa, ba, wx, bx, sp):
    vb = v.astype(BF16)
    ra = _sigmoid(_dot(vb, wa) + ba)
    ix = _sigmoid(_dot(vb, wx) + bx)
    log_a = -LRU_C * ra * sp
    a = jnp.exp(log_a)
    sq = jnp.sqrt(-jnp.tanh(log_a) * (a * a + 1.0))
    return ra, ix, a, sq


def _row_bcast(v, r):
    return jnp.broadcast_to(v[r:r + 1, :], v.shape)


def _tile_scan(a8, b8, direction):
    ri = lax.broadcasted_iota(jnp.int32, a8.shape, 0)
    A, B = a8, b8
    for s in (1, 2, 4):
        ok = (ri >= s) if direction == 1 else (ri + s < SUBLANES)
        As, Bs = _shift_rows(A, s * direction), _shift_rows(B, s * direction)
        B = jnp.where(ok, A * Bs + B, B)
        A = jnp.where(ok, A * As, A)
    return A, B


def _rnn_fwd(proj, conv_w, conv_b, w_a, b_a, w_x, b_x, lam):
    T = proj.shape[0]
    nchunk = T // CHUNK
    ntile = T // SUBLANES

    def body(u_ref, ug_ref, cw_ref, cb_ref, wa_ref, ba_ref, wx_ref, bx_ref, lam_ref, h_ref, z_ref, upad, a_s, b_s):
        upad[pl.ds(0, SUBLANES), :] = jnp.zeros((SUBLANES, HEAD), F32)
        cw, cb = _tap_rows(cw_ref), cb_ref[...]
        wa, wx = wa_ref[...].astype(BF16), wx_ref[...].astype(BF16)
        ba, bx = ba_ref[...], bx_ref[...]
        sp = _softplus_neg(lam_ref[...])

        def chunk(i, carry):
            r0 = pl.multiple_of(i * CHUNK, CHUNK)
            upad[pl.ds(r0 + SUBLANES, CHUNK), :] = u_ref[pl.ds(r0, CHUNK), :]
            v = _conv_taps(upad[pl.ds(r0, CHUNK + SUBLANES), :], cw) + cb
            _, ix, a, sq = _lru_gates(v, wa, ba, wx, bx, sp)
            a_s[pl.ds(r0, CHUNK), :] = a
            b_s[pl.ds(r0, CHUNK), :] = sq * ix * v
            return carry

        lax.fori_loop(0, nchunk, chunk, 0)

        def tile(i, carry):
            r0 = pl.multiple_of(i * SUBLANES, SUBLANES)
            A, B = _tile_scan(a_s[pl.ds(r0, SUBLANES), :], b_s[pl.ds(r0, SUBLANES), :], 1)
            h_ref[pl.ds(r0, SUBLANES), :] = A * carry + B
            return _row_bcast(A, SUBLANES - 1) * carry + _row_bcast(B, SUBLANES - 1)

        lax.fori_loop(0, ntile, tile, jnp.zeros((SUBLANES, HEAD), F32), unroll=4)

        def chunk3(i, carry):
            r0 = pl.multiple_of(i * CHUNK, CHUNK)
            gl, _ = _gelu_parts(ug_ref[pl.ds(r0, CHUNK), :])
            z_ref[pl.ds(r0, CHUNK), :] = (h_ref[pl.ds(r0, CHUNK), :] * gl).astype(BF16)
            return carry

        lax.fori_loop(0, nchunk, chunk3, 0)

    col = pl.BlockSpec((T, HEAD), lambda h: (0, h))
    vec = pl.BlockSpec((1, HEAD), lambda h: (0, h))
    mat = pl.BlockSpec((None, HEAD, HEAD), lambda h: (h, 0, 0))
    return pl.pallas_call(
        body, name="rnn_fwd", grid=(N_RNN_HEADS,),
        in_specs=[pl.BlockSpec((T, HEAD), lambda h: (0, COL_RNN + h)), pl.BlockSpec((T, HEAD), lambda h: (0, COL_GATE + h)),
                  pl.BlockSpec((CONV_WIDTH, HEAD), lambda h: (0, h)), vec, mat, vec, mat, vec, vec],
        out_specs=[col, col],
        out_shape=[jax.ShapeDtypeStruct((T, D_RNN), F32), jax.ShapeDtypeStruct((T, D_RNN), BF16)],
        scratch_shapes=[pltpu.VMEM((T + SUBLANES, HEAD), F32), pltpu.VMEM((T, HEAD), F32), pltpu.VMEM((T, HEAD), F32)],
        compiler_params=_params(dimension_semantics=("parallel",)),
    )(proj, proj, conv_w, conv_b, w_a, b_a, w_x, b_x, lam)


def _rnn_bwd(proj, hr, dz, conv_w, conv_b, w_a, b_a, w_x, b_x, lam):
    T = proj.shape[0]
    nchunk = T // CHUNK
    ntile = T // SUBLANES

    def body(u_ref, ug_ref, h_ref, dz_ref, cw_ref, cb_ref, wa_ref, ba_ref, wx_ref, bx_ref, lam_ref,
             du_ref, dug_ref, dwa_ref, dwx_ref, dba_ref, dbx_ref, dlam_ref, dcb_ref, dcw_ref,
             upad, hpad, apad, v_s, ra_s, ix_s, sq_s, g_s, dvpad):
        zero_tile = jnp.zeros((SUBLANES, HEAD), F32)
        upad[pl.ds(0, SUBLANES), :] = zero_tile
        hpad[pl.ds(0, SUBLANES), :] = zero_tile
        apad[pl.ds(T, SUBLANES), :] = zero_tile
        dvpad[pl.ds(T, SUBLANES), :] = zero_tile
        for ref in (dwa_ref, dwx_ref, dba_ref, dbx_ref, dlam_ref, dcb_ref, dcw_ref):
            ref[...] = jnp.zeros_like(ref)
        cw, cb = _tap_rows(cw_ref), cb_ref[...]
        wa, wx = wa_ref[...].astype(BF16), wx_ref[...].astype(BF16)
        ba, bx = ba_ref[...], bx_ref[...]
        lam_row = lam_ref[...]
        sp = _softplus_neg(lam_row)

        def chunk(i, carry):
            r0 = pl.multiple_of(i * CHUNK, CHUNK)
            rows = pl.ds(r0, CHUNK)
            upad[pl.ds(r0 + SUBLANES, CHUNK), :] = u_ref[rows, :]
            h = h_ref[rows, :]
            hpad[pl.ds(r0 + SUBLANES, CHUNK), :] = h
            v = _conv_taps(upad[pl.ds(r0, CHUNK + SUBLANES), :], cw) + cb
            ra, ix, a, sq = _lru_gates(v, wa, ba, wx, bx, sp)
            v_s[rows, :], ra_s[rows, :], ix_s[rows, :], sq_s[rows, :], apad[rows, :] = v, ra, ix, sq, a
            gl, dgl = _gelu_parts(ug_ref[rows, :])
            d = dz_ref[rows, :]
            g_s[rows, :] = d * gl
            dug_ref[rows, :] = (d * h * dgl).astype(BF16)
            return carry

        lax.fori_loop(0, nchunk, chunk, 0)

        def tile(k, carry):
            r0 = pl.multiple_of((ntile - 1 - k) * SUBLANES, SUBLANES)
            rows = pl.ds(r0, SUBLANES)
            ri = lax.broadcasted_iota(jnp.int32, (SUBLANES, HEAD), 0)
            a_next = jnp.where(ri < SUBLANES - 1, _shift_rows(apad[rows, :], -1),
                               _shift_rows(apad[pl.ds(r0 + SUBLANES, SUBLANES), :], -1))
            A, B = _tile_scan(a_next, g_s[rows, :], -1)
            g_s[rows, :] = A * carry + B
            return _row_bcast(A, 0) * carry + _row_bcast(B, 0)

        lax.fori_loop(0, ntile, tile, zero_tile, unroll=4)

        def chunk3(i, carry):
            r0 = pl.multiple_of(i * CHUNK, CHUNK)
            rows = pl.ds(r0, CHUNK)
            g = g_s[rows, :]
            h_prev = _shift_rows(hpad[pl.ds(r0, CHUNK + SUBLANES), :], 1)[SUBLANES:]
            v, ra, ix, sq, a = v_s[rows, :], ra_s[rows, :], ix_s[rows, :], sq_s[rows, :], apad[rows, :]
            d_sq = g * ix * v
            d_ix = g * sq * v
            d_la = a * g * h_prev - d_sq * a * a / sq
            dlam_ref[...] += jnp.sum(d_la * ra, axis=0, keepdims=True)
            d_pa = d_la * (-LRU_C) * sp * ra * (1.0 - ra)
            d_px = d_ix * ix * (1.0 - ix)
            vb, d_pab, d_pxb = v.astype(BF16), d_pa.astype(BF16), d_px.astype(BF16)
            dwa_ref[...] += _dot_tn(vb, d_pab)
            dwx_ref[...] += _dot_tn(vb, d_pxb)
            dba_ref[...] += jnp.sum(d_pa, axis=0, keepdims=True)
            dbx_ref[...] += jnp.sum(d_px, axis=0, keepdims=True)
            dv = g * sq * ix + _dot_nt(d_pab, wa) + _dot_nt(d_pxb, wx)
            dvpad[rows, :] = dv
            dcb_ref[...] += jnp.sum(dv, axis=0, keepdims=True)
            xs = upad[pl.ds(r0, CHUNK + SUBLANES), :]
            for k in range(CONV_WIDTH):
                u_k = _shift_rows(xs, CONV_WIDTH - 1 - k)[SUBLANES:] if k < CONV_WIDTH - 1 else xs[SUBLANES:]
                dcw_ref[k:k + 1, :] += jnp.sum(dv * u_k, axis=0, keepdims=True)
            return carry

        lax.fori_loop(0, nchunk, chunk3, 0)
        dlam_ref[...] = dlam_ref[...] * (LRU_C * _sigmoid(-lam_row))

        def chunk4(i, carry):
            r0 = pl.multiple_of(i * CHUNK, CHUNK)
            dvs = dvpad[pl.ds(r0, CHUNK + SUBLANES), :]
            du = cw[CONV_WIDTH - 1] * dvs[:CHUNK]
            for k in range(CONV_WIDTH - 1):
                du += cw[k] * _shift_rows(dvs, -(CONV_WIDTH - 1 - k))[:CHUNK]
            du_ref[pl.ds(r0, CHUNK), :] = du.astype(BF16)
            return carry

        lax.fori_loop(0, nchunk, chunk4, 0)

    col = pl.BlockSpec((T, HEAD), lambda h: (0, h))
    vec = pl.BlockSpec((1, HEAD), lambda h: (0, h))
    mat = pl.BlockSpec((None, HEAD, HEAD), lambda h: (h, 0, 0))
    taps = pl.BlockSpec((CONV_WIDTH, HEAD), lambda h: (0, h))
    vec_out = jax.ShapeDtypeStruct((1, D_RNN), F32)
    mat_out = jax.ShapeDtypeStruct((N_RNN_HEADS, HEAD, HEAD), F32)
    seq = pltpu.VMEM((T, HEAD), F32)
    seq_pad = pltpu.VMEM((T + SUBLANES, HEAD), F32)
    return pl.pallas_call(
        body, name="rnn_bwd", grid=(N_RNN_HEADS,),
        in_specs=[pl.BlockSpec((T, HEAD), lambda h: (0, COL_RNN + h)), pl.BlockSpec((T, HEAD), lambda h: (0, COL_GATE + h)),
                  col, col, taps, vec, mat, vec, mat, vec, vec],
        out_specs=[col, col, mat, mat, vec, vec, vec, vec, taps],
        out_shape=[jax.ShapeDtypeStruct((T, D_RNN), BF16), jax.ShapeDtypeStruct((T, D_RNN), BF16), mat_out, mat_out,
                   vec_out, vec_out, vec_out, vec_out, jax.ShapeDtypeStruct((CONV_WIDTH, D_RNN), F32)],
        scratch_shapes=[seq_pad, seq_pad, seq_pad, seq, seq, seq, seq, seq, seq_pad],
        compiler_params=_params(dimension_semantics=("parallel",)),
    )(proj, proj, hr, dz, conv_w, conv_b, w_a, b_a, w_x, b_x, lam)


def _local_step(x, target, norm_mix, w_in, w_pool_grp, pool_scale, w_pool_out, conv_w, conv_b, w_rg_a, b_rg_a,
                w_rg_x, b_rg_x, lam, w_rnn_out, w_o, norm_ffn, w_ffn_in, w_ffn_out, norm_final):
    T = x.shape[0]
    tm = min(T, 512)
    proj, h1 = _norm_matmul(x, norm_mix, w_in, tm=tm, tn=1536, name="in_proj")
    pm = _pool_fwd(proj, w_pool_grp, pool_scale)
    hr, z = _rnn_fwd(proj, conv_w, conv_b, w_rg_a, b_rg_a, w_rg_x, b_rg_x, lam)
    y_pool, y_rnn, mix = _branch_mix(pm, z, w_pool_out, w_rnn_out, proj, tm=tm, tn=512)
    x2 = _out_proj_residual(mix, w_o, x, tm=tm)
    gate, up, act, h2 = _ffn_in(x2, norm_ffn, w_ffn_in, tm=tm, tn=1408)
    tr = min(T, 256)
    dx3, dx3b, sq_cols, g_norm_final = _ffn_out_loss(act, w_ffn_out, x2, norm_final, target, tm=tr)

    g = {"norm_final": g_norm_final}
    dgate, dup = _ffn_out_bwd(dx3b, w_ffn_out, gate, up, tm=tm, tn=1408)
    g["w_ffn_out"] = _weight_grad(act, [dx3b], tm=256, tn=D_MODEL, name="w_ffn_out_grad")
    dx2, dx2b, g["norm_ffn"] = _ffn_in_bwd(dgate, dup, w_ffn_in, dx3, x2, norm_ffn, tm=tr)
    g["w_ffn_in"] = _weight_grad(h2, [dgate, dup], tm=D_MODEL, tn=256, name="w_ffn_in_grad")
    dgp, dgr, dyp, dyr = _out_proj_bwd(dx2b, w_o, proj, y_pool, y_rnn, tm=tm, tn=512)
    g["w_o"] = _weight_grad(mix, [dx2b], tm=D_MODEL, tn=256, name="w_o_grad")
    dpm, dz = _branch_bwd(dyp, dyr, w_pool_out, w_rnn_out, tm=tm)
    g["w_pool_out"] = _weight_grad(pm, [dyp], tm=D_POOL, tn=256, name="w_pool_out_grad")
    g["w_rnn_out"] = _weight_grad(z, [dyr], tm=D_RNN, tn=256, name="w_rnn_out_grad")
    dupool, g["w_pool_grp"], g["pool_scale"] = _pool_bwd(proj, dpm, w_pool_grp, pool_scale)
    (durnn, dugate, g["w_rg_a"], g["w_rg_x"], g["b_rg_a"], g["b_rg_x"], g["lru_lambda"], g["conv_b"],
     g["conv_w"]) = _rnn_bwd(proj, hr, dz, conv_w, conv_b, w_rg_a, b_rg_a, w_rg_x, b_rg_x, lam)
    segs = [dupool, durnn, dugate, dgp, dgr]
    grad_x, g["norm_mix"] = _in_proj_bwd(segs, w_in, dx2, x, norm_mix, tm=tr)
    g["w_in"] = _weight_grad(h1, segs, tm=D_MODEL, tn=256, name="w_in_grad")
    return sq_cols, grad_x, g


LARGE = {"w_in": "col", "w_pool_out": "col", "w_rnn_out": "row", "w_o": "row", "w_ffn_in": "col", "w_ffn_out": "row"}
LARGE_SHAPE = {"w_in": (D_MODEL, D_IN), "w_pool_out": (D_POOL, D_MODEL), "w_rnn_out": (D_RNN, D_MODEL),
               "w_o": (D_MODEL, D_MODEL), "w_ffn_in": (D_MODEL, 2 * D_FF), "w_ffn_out": (D_FF, D_MODEL)}


def _place():
    x, y, c = lax.axis_index("x"), lax.axis_index("y"), lax.axis_index("c")
    return 2 * x + y, c


def _chip_device(chip, c):
    return (chip // 2, chip % 2, c)


def _chip_window(ref, kind, shape, chip, half=None):
    K, N = shape
    if kind == "col":
        rows = slice(None) if half is None else pl.ds(half * (K // 2), K // 2)
        return ref.at[rows, pl.ds(chip * (N // N_CHIPS), N // N_CHIPS)]
    ks = K // N_CHIPS
    if half is None:
        return ref.at[pl.ds(chip * ks, ks), :]
    return ref.at[pl.ds(chip * ks + half * (ks // 2), ks // 2), :]


def _row_half(ref, half):
    rows = ref.shape[0] // 2
    return ref.at[pl.ds(half * rows, rows), :]


def _gather_weights(shards, conv_w_shard):
    names = list(LARGE)
    n = len(names)

    def body(*refs):
        shard_refs, cw_ref = refs[:n], refs[n]
        full_refs, cwg_ref = refs[n + 1:2 * n + 1], refs[2 * n + 1]
        send_sems, recv_sems, local_sems = refs[2 * n + 2:]
        chip, c = _place()
        others = [chip ^ r for r in (1, 2, 3)]

        def remote(k, src, dst, sem, to):
            return pltpu.make_async_remote_copy(src_ref=src, dst_ref=dst, send_sem=send_sems.at[k, sem],
                                                recv_sem=recv_sems.at[k, sem], device_id=to, device_id_type=MESH)

        local, sends = [], []
        for k, name in enumerate(names):
            kind, shape = LARGE[name], LARGE_SHAPE[name]
            cp = pltpu.make_async_copy(shard_refs[k], _chip_window(full_refs[k], kind, shape, chip), local_sems.at[k])
            cp.start()
            local.append(cp)
            for r, other in enumerate(others):
                cp = remote(k, _row_half(shard_refs[k], c), _chip_window(full_refs[k], kind, shape, chip, c), r, _chip_device(other, c))
                cp.start()
                sends.append(cp)
        cw_cols = conv_w_shard.shape[1]
        cp = pltpu.make_async_copy(cw_ref, cwg_ref.at[:, pl.ds(chip * cw_cols, cw_cols)], local_sems.at[n])
        cp.start()
        local.append(cp)
        for r, other in enumerate(others):
            cp = remote(n, cw_ref, cwg_ref.at[:, pl.ds(chip * cw_cols, cw_cols)], r, _chip_device(other, c))
            cp.start()
            sends.append(cp)
        for k, name in enumerate(names):
            kind, shape = LARGE[name], LARGE_SHAPE[name]
            for r, other in enumerate(others):
                win = _chip_window(full_refs[k], kind, shape, other, c)
                remote(k, win, win, r, _chip_device(other, c)).wait_recv()
                cp = remote(k, win, win, 3 + r, _chip_device(chip, 1 - c))
                cp.start()
                sends.append(cp)
        for r, other in enumerate(others):
            win = cwg_ref.at[:, pl.ds(other * cw_cols, cw_cols)]
            remote(n, win, win, r, _chip_device(other, c)).wait_recv()
        for k, name in enumerate(names):
            kind, shape = LARGE[name], LARGE_SHAPE[name]
            for r, other in enumerate(others):
                win = _chip_window(full_refs[k], kind, shape, other, 1 - c)
                remote(k, win, win, 3 + r, _chip_device(chip, 1 - c)).wait_recv()
        for cp in sends:
            cp.wait_send()
        for cp in local:
            cp.wait()

    out = pl.pallas_call(
        body, name="gather_weights",
        in_specs=[ANY] * (n + 1), out_specs=[ANY] * (n + 1),
        out_shape=[jax.ShapeDtypeStruct(LARGE_SHAPE[name], BF16) for name in names]
        + [jax.ShapeDtypeStruct((CONV_WIDTH, D_RNN), F32)],
        scratch_shapes=[pltpu.SemaphoreType.DMA((n + 1, 6)), pltpu.SemaphoreType.DMA((n + 1, 6)), pltpu.SemaphoreType.DMA((n + 1,))],
        compiler_params=pltpu.CompilerParams(has_side_effects=True),
    )(*[shards[name] for name in names], conv_w_shard)
    return dict(zip(names, out[:n])), out[n]


def _core_halves(ref, kind, shape, c):
    return [_chip_window(ref, kind, shape, chip, c) for chip in range(N_CHIPS)]


def _exchange_sibling(grads):
    names = list(LARGE)
    n = len(names)

    def body(*refs):
        g_refs, got_refs = refs[:n], refs[n:2 * n]
        send_sems, recv_sems = refs[2 * n:]
        chip, c = _place()
        sibling = _chip_device(chip, 1 - c)
        for k, name in enumerate(names):
            kind, shape = LARGE[name], LARGE_SHAPE[name]
            if kind == "col":
                pairs = [(_row_half(g_refs[k], 1 - c), got_refs[k])]
            else:
                rows = shape[0] // N_DEV
                pairs = [(win, got_refs[k].at[pl.ds(j * rows, rows), :])
                         for j, win in enumerate(_core_halves(g_refs[k], kind, shape, 1 - c))]
            for src, dst in pairs:
                pltpu.make_async_remote_copy(src_ref=src, dst_ref=dst, send_sem=send_sems.at[k], recv_sem=recv_sems.at[k],
                                             device_id=sibling, device_id_type=MESH).start()
        for k in range(n):
            pltpu.make_async_remote_copy(src_ref=got_refs[k], dst_ref=got_refs[k], send_sem=send_sems.at[k],
                                         recv_sem=recv_sems.at[k], device_id=sibling, device_id_type=MESH).wait()

    out = pl.pallas_call(
        body, name="exchange_sibling", in_specs=[ANY] * n, out_specs=[ANY] * n,
        out_shape=[jax.ShapeDtypeStruct((LARGE_SHAPE[name][0] // 2, LARGE_SHAPE[name][1]), F32) for name in names],
        scratch_shapes=[pltpu.SemaphoreType.DMA((n,)), pltpu.SemaphoreType.DMA((n,))],
        compiler_params=pltpu.CompilerParams(has_side_effects=True),
    )(*[grads[name] for name in names])
    return dict(zip(names, out))


def _chip_sum(name, g, got, c):
    kind, (K, N) = LARGE[name], LARGE_SHAPE[name]
    rows = K // N_DEV

    def body(c_ref, g_ref, got_ref, o_ref):
        o_ref[...] = g_ref[...] + got_ref[...]

    if kind == "col":
        mine = pl.BlockSpec((rows, N), lambda j, c_ref: (j + N_CHIPS * c_ref[0], 0))
    else:
        mine = pl.BlockSpec((rows, N), lambda j, c_ref: (2 * j + c_ref[0], 0))
    blk = pl.BlockSpec((rows, N), lambda j, c_ref: (j, 0))
    return pl.pallas_call(
        body, name=name + "_chip_sum",
        grid_spec=pltpu.PrefetchScalarGridSpec(num_scalar_prefetch=1, grid=(N_CHIPS,), in_specs=[mine, blk], out_specs=blk),
        out_shape=jax.ShapeDtypeStruct((K // 2, N), F32),
        compiler_params=_params(dimension_semantics=("parallel",)),
    )(c, g, got)


def _piece(ref, kind, shape, chip):
    K, N = shape
    if kind == "col":
        return ref.at[:, pl.ds(chip * (N // N_CHIPS), N // N_CHIPS)]
    return ref.at[pl.ds(chip * (K // N_DEV), K // N_DEV), :]


def _piece_shape(name):
    kind, (K, N) = LARGE[name], LARGE_SHAPE[name]
    return (K // 2, N // N_CHIPS) if kind == "col" else (K // N_DEV, N)


def _exchange_chips(sums):
    names = list(LARGE)
    n = len(names)

    def body(*refs):
        s_refs, got_refs = refs[:n], refs[n:2 * n]
        send_sems, recv_sems = refs[2 * n:]
        chip, c = _place()
        copies = []
        for k, name in enumerate(names):
            kind, shape = LARGE[name], LARGE_SHAPE[name]
            for r in range(3):
                other = chip ^ (r + 1)
                cp = pltpu.make_async_remote_copy(src_ref=_piece(s_refs[k], kind, shape, other), dst_ref=got_refs[k].at[r],
                                                  send_sem=send_sems.at[k, r], recv_sem=recv_sems.at[k, r],
                                                  device_id=_chip_device(other, c), device_id_type=MESH)
                cp.start()
                copies.append(cp)
        for cp in copies:
            cp.wait()

    out = pl.pallas_call(
        body, name="exchange_chips", in_specs=[ANY] * n, out_specs=[ANY] * n,
        out_shape=[jax.ShapeDtypeStruct((3,) + _piece_shape(name), F32) for name in names],
        scratch_shapes=[pltpu.SemaphoreType.DMA((n, 3)), pltpu.SemaphoreType.DMA((n, 3))],
        compiler_params=pltpu.CompilerParams(has_side_effects=True),
    )(*[sums[name] for name in names])
    return dict(zip(names, out))


def _final_sum(name, chip_sum, got, chip):
    kind = LARGE[name]
    rows, cols = _piece_shape(name)

    def body(chip_ref, s_ref, got_ref, o_ref):
        o_ref[...] = ((s_ref[...] + got_ref[0]) + got_ref[1]) + got_ref[2]

    if kind == "col":
        mine = pl.BlockSpec((rows, cols), lambda i, chip_ref: (0, chip_ref[0]))
    else:
        mine = pl.BlockSpec((rows, cols), lambda i, chip_ref: (chip_ref[0], 0))
    return pl.pallas_call(
        body, name=name + "_final_sum",
        grid_spec=pltpu.PrefetchScalarGridSpec(
            num_scalar_prefetch=1, grid=(1,), in_specs=[mine, pl.BlockSpec((3, rows, cols), lambda i, chip_ref: (0, 0, 0))],
            out_specs=pl.BlockSpec((rows, cols), lambda i, chip_ref: (0, 0))),
        out_shape=jax.ShapeDtypeStruct((rows, cols), F32),
        compiler_params=_params(dimension_semantics=("arbitrary",)),
    )(chip, chip_sum, got)


def _join_halves(pieces):
    names = list(LARGE)
    n = len(names)

    def body(*refs):
        p_refs, full_refs = refs[:n], refs[n:2 * n]
        send_sems, recv_sems, local_sems = refs[2 * n:]
        chip, c = _place()
        sibling = _chip_device(chip, 1 - c)
        copies, local = [], []
        for k in range(n):
            cp = pltpu.make_async_copy(p_refs[k], _row_half(full_refs[k], c), local_sems.at[k])
            cp.start()
            local.append(cp)
            cp = pltpu.make_async_remote_copy(src_ref=p_refs[k], dst_ref=_row_half(full_refs[k], c), send_sem=send_sems.at[k],
                                              recv_sem=recv_sems.at[k], device_id=sibling, device_id_type=MESH)
            cp.start()
            copies.append(cp)
        for cp in copies:
            cp.wait()
        for cp in local:
            cp.wait()

    out = pl.pallas_call(
        body, name="join_halves", in_specs=[ANY] * n, out_specs=[ANY] * n,
        out_shape=[jax.ShapeDtypeStruct((2 * _piece_shape(name)[0], _piece_shape(name)[1]), F32) for name in names],
        scratch_shapes=[pltpu.SemaphoreType.DMA((n,)), pltpu.SemaphoreType.DMA((n,)), pltpu.SemaphoreType.DMA((n,))],
        compiler_params=pltpu.CompilerParams(has_side_effects=True),
    )(*[pieces[name] for name in names])
    return dict(zip(names, out))


VEC_ROWS = 16


def _all_reduce_small(slabs):
    n = len(slabs)

    def body(*refs):
        in_refs, out_refs, got_refs = refs[:n], refs[n:2 * n], refs[2 * n:3 * n]
        send_sems, recv_sems = refs[3 * n:]
        x, y, c = lax.axis_index("x"), lax.axis_index("y"), lax.axis_index("c")
        me = 4 * x + 2 * y + c

        def remote(src, dst, k, phase, r):
            other = me ^ r
            return pltpu.make_async_remote_copy(src_ref=src, dst_ref=dst, send_sem=send_sems.at[k, phase, r],
                                                recv_sem=recv_sems.at[k, phase, r],
                                                device_id=(other // 4, (other // 2) % 2, other % 2), device_id_type=MESH)

        scatter = [remote(in_refs[k].at[me ^ r], got_refs[k].at[r], k, 0, r) for r in range(1, N_DEV) for k in range(n)]
        for cp in scatter:
            cp.start()
        for cp in scatter:
            cp.wait()
        for k in range(n):
            total = in_refs[k][me]
            for r in range(1, N_DEV):
                total = total + got_refs[k][r]
            out_refs[k][me] = total
        gather = [remote(out_refs[k].at[me], out_refs[k].at[me], k, 1, r) for r in range(1, N_DEV) for k in range(n)]
        for cp in gather:
            cp.start()
        for r in range(1, N_DEV):
            for k in range(n):
                remote(out_refs[k].at[me ^ r], out_refs[k].at[me ^ r], k, 1, r).wait_recv()
        for cp in gather:
            cp.wait_send()

    return pl.pallas_call(
        body, name="all_reduce_small", in_specs=[VMEM] * n, out_specs=[VMEM] * n,
        out_shape=[jax.ShapeDtypeStruct(s.shape, F32) for s in slabs],
        scratch_shapes=[pltpu.VMEM(s.shape, F32) for s in slabs]
        + [pltpu.SemaphoreType.DMA((n, 2, N_DEV)), pltpu.SemaphoreType.DMA((n, 2, N_DEV))],
        compiler_params=pltpu.CompilerParams(has_side_effects=True),
    )(*slabs)


def _cast_bf16(w, name):
    rows, cols = w.shape
    tr = rows // 2

    def body(w_ref, o_ref):
        o_ref[...] = w_ref[...].astype(BF16)

    blk = pl.BlockSpec((tr, cols), lambda i: (i, 0))
    return pl.pallas_call(body, name=name + "_cast", grid=(2,), in_specs=[blk], out_specs=blk,
                          out_shape=jax.ShapeDtypeStruct(w.shape, BF16),
                          compiler_params=_params(dimension_semantics=("parallel",)))(w)


def _adamw_math(w, g, m, v):
    m = ADAM_B1 * m + (1.0 - ADAM_B1) * g
    v = ADAM_B2 * v + (1.0 - ADAM_B2) * (g * g)
    m_hat = m / (1.0 - ADAM_B1 ** ADAM_STEP)
    v_hat = v / (1.0 - ADAM_B2 ** ADAM_STEP)
    delta = -ADAM_LR * (m_hat / (jnp.sqrt(v_hat) + ADAM_EPS) + ADAM_WD * w)
    return delta, m, v


def _adamw_large(w, g, m, v, name):
    rows, cols = w.shape
    tr = rows // 4

    def body(w_ref, g_ref, m_ref, v_ref, d_ref, mo_ref, vo_ref):
        d_ref[...], mo_ref[...], vo_ref[...] = _adamw_math(w_ref[...], g_ref[...], m_ref[...], v_ref[...])

    blk = pl.BlockSpec((tr, cols), lambda i: (i, 0))
    out = jax.ShapeDtypeStruct(w.shape, F32)
    return pl.pallas_call(body, name=name + "_adamw", grid=(4,), in_specs=[blk] * 4, out_specs=[blk] * 3, out_shape=[out] * 3,
                          compiler_params=_params(dimension_semantics=("parallel",)))(w, g, m, v)


def _adamw_small(ws, gs, ms, vs):
    n = len(ws)

    def body(*refs):
        for k in range(n):
            w_ref, g_ref, m_ref, v_ref = (refs[q * n + k] for q in range(4))
            d_ref, mo_ref, vo_ref = (refs[(4 + q) * n + k] for q in range(3))
            d_ref[...], mo_ref[...], vo_ref[...] = _adamw_math(w_ref[...], g_ref[...], m_ref[...], v_ref[...])

    out = [jax.ShapeDtypeStruct(w.shape, F32) for w in ws]
    res = pl.pallas_call(body, name="small_adamw", in_specs=[VMEM] * (4 * n), out_specs=[VMEM] * (3 * n), out_shape=out * 3,
                         compiler_params=_params())(*ws, *gs, *ms, *vs)
    return res[:n], res[n:2 * n], res[2 * n:]


WEIGHTS = ["norm_mix", "w_in", "w_pool_grp", "pool_scale", "w_pool_out", "conv_w", "conv_b", "w_rg_a", "b_rg_a", "w_rg_x",
           "b_rg_x", "lru_lambda", "w_rnn_out", "w_o", "norm_ffn", "w_ffn_in", "w_ffn_out", "norm_final"]
VEC_ITEMS = ["norm_mix", "norm_ffn", "norm_final", "pool_scale", "conv_b", "lru_lambda", "b_rg_a", "b_rg_x"]
MAT_ITEMS = ["w_pool_grp", "w_rg_a", "w_rg_x"]


def _as2d(name, a):
    if name in MAT_ITEMS:
        return a.reshape(-1, HEAD, HEAD)
    if name == "conv_w":
        return a.reshape(CONV_WIDTH, -1)
    return a.reshape(1, -1)


def kernel(x, norm_mix, w_in, w_pool_grp, pool_scale, w_pool_out, conv_w, conv_b, w_rg_a, b_rg_a, w_rg_x, b_rg_x, lru_lambda, w_rnn_out, w_o, norm_ffn, w_ffn_in, w_ffn_out, norm_final, loss_target, m_norm_mix, m_w_in, m_w_pool_grp, m_pool_scale, m_w_pool_out, m_conv_w, m_conv_b, m_w_rg_a, m_b_rg_a, m_w_rg_x, m_b_rg_x, m_lru_lambda, m_w_rnn_out, m_w_o, m_norm_ffn, m_w_ffn_in, m_w_ffn_out, m_norm_final, v_norm_mix, v_w_in, v_w_pool_grp, v_pool_scale, v_w_pool_out, v_conv_w, v_conv_b, v_w_rg_a, v_b_rg_a, v_w_rg_x, v_b_rg_x, v_lru_lambda, v_w_rnn_out, v_w_o, v_norm_ffn, v_w_ffn_in, v_w_ffn_out, v_norm_final):
    given = dict(locals())
    w = {name: given[name] for name in WEIGHTS}
    m = {name: given["m_" + name] for name in WEIGHTS}
    v = {name: given["v_" + name] for name in WEIGHTS}
    chip, c = _place()

    shards = {name: _cast_bf16(w[name][0], name) for name in LARGE}
    full, conv_w_full = _gather_weights(shards, w["conv_w"][0])
    small = {name: _as2d(name, w[name]) for name in WEIGHTS if name not in LARGE and name != "conv_w"}
    sq_cols, grad_x, g = _local_step(
        x[0], loss_target[0], small["norm_mix"], full["w_in"], small["w_pool_grp"], small["pool_scale"], full["w_pool_out"],
        conv_w_full, small["conv_b"], small["w_rg_a"], small["b_rg_a"], small["w_rg_x"], small["b_rg_x"], small["lru_lambda"],
        full["w_rnn_out"], full["w_o"], small["norm_ffn"], full["w_ffn_in"], full["w_ffn_out"], small["norm_final"])
    loss = lax.psum(0.5 / D_MODEL * jnp.sum(sq_cols), ("x", "y", "c"))

    c1 = jnp.reshape(c, (1,)).astype(jnp.int32)
    chip1 = jnp.reshape(chip, (1,)).astype(jnp.int32)
    from_sibling = _exchange_sibling({name: g[name] for name in LARGE})
    chip_sums = {name: _chip_sum(name, g[name], from_sibling[name], c1) for name in LARGE}
    from_chips = _exchange_chips(chip_sums)
    pieces = {name: _final_sum(name, chip_sums[name], from_chips[name], chip1) for name in LARGE}
    grads = _join_halves(pieces)

    vec_rows = [g[name] if name != "pool_scale" else jnp.pad(g[name], ((0, 0), (0, D_MODEL - D_POOL))) for name in VEC_ITEMS]
    vec = jnp.concatenate(vec_rows + [g["conv_w"], jnp.zeros((VEC_ROWS - len(VEC_ITEMS) - CONV_WIDTH, D_MODEL), F32)], axis=0)
    mat = jnp.concatenate([g[name].reshape(-1, HEAD) for name in MAT_ITEMS], axis=0)
    vec, mat = _all_reduce_small([vec.reshape(VEC_ROWS, N_DEV, HEAD).transpose(1, 0, 2), mat.reshape(N_DEV, -1, HEAD)])
    vec = vec.transpose(1, 0, 2).reshape(VEC_ROWS, D_MODEL)
    mat = mat.reshape(-1, HEAD)
    for k, name in enumerate(VEC_ITEMS):
        grads[name] = vec[k:k + 1, :w[name].size]
    conv_cols = w["conv_w"].shape[-1]
    grads["conv_w"] = lax.dynamic_slice_in_dim(vec[len(VEC_ITEMS):len(VEC_ITEMS) + CONV_WIDTH], chip * conv_cols, conv_cols, axis=1)
    row = 0
    for name in MAT_ITEMS:
        rows = w[name].size // HEAD
        grads[name] = mat[row:row + rows]
        row += rows

    delta, new_m, new_v = {}, {}, {}
    for name in LARGE:
        delta[name], new_m[name], new_v[name] = _adamw_large(w[name][0], grads[name], m[name][0], v[name][0], name)
    small_names = [name for name in WEIGHTS if name not in LARGE]
    flat = lambda d: [d[name].reshape(grads[name].shape) for name in small_names]
    ds, mo, vo = _adamw_small(flat(w), [grads[name] for name in small_names], flat(m), flat(v))
    for k, name in enumerate(small_names):
        delta[name], new_m[name], new_v[name] = ds[k], mo[k], vo[k]

    shaped = lambda d: [d[name].reshape(w[name].shape) for name in WEIGHTS]
    return (loss, grad_x[None], *shaped(grads), *shaped(delta), *shaped(new_m), *shaped(new_v))
```

```python
import functools
import math

import jax
import jax.numpy as jnp
from jax import lax
from jax.experimental import pallas as pl
from jax.experimental.pallas import tpu as pltpu

F32 = jnp.float32
BF16 = jnp.bfloat16

D_MODEL = 1024
D_POOL = 512
N_POOL_GROUPS = 4
D_RNN = 1024
N_RNN_HEADS = 8
HEAD = 128
CONV_WIDTH = 4
LRU_C = 8.0
D_FF = 2816
D_IN = D_POOL + 2 * D_RNN + 2 * D_MODEL
NORM_EPS = 1e-6
COL_RNN = D_POOL // HEAD
COL_GATE = (D_POOL + D_RNN) // HEAD

ADAM_LR = 0.001
ADAM_B1 = 0.9
ADAM_B2 = 0.999
ADAM_EPS = 1e-08
ADAM_WD = 0.01
ADAM_STEP = 10

N_CHIPS = 4
N_DEV = 8
MESH = pl.DeviceIdType.MESH
ANY = pl.BlockSpec(memory_space=pl.ANY)
VMEM = pl.BlockSpec(memory_space=pltpu.VMEM)
VMEM_LIMIT_BYTES = 60 * 1024 * 1024
SUBLANES = 8
POOL_HALO = 16
CHUNK = 512

GELU_C = math.sqrt(2.0 / math.pi)
GELU_A = 0.044715


def _params(**kw):
    return pltpu.CompilerParams(vmem_limit_bytes=VMEM_LIMIT_BYTES, **kw)


def _sigmoid(x):
    return 1.0 / (1.0 + jnp.exp(-x))


def _log1p(y):
    u = 1.0 + y
    d = u - 1.0
    return jnp.where(d == 0.0, y, jnp.log(u) * (y / jnp.where(d == 0.0, 1.0, d)))


def _gelu_parts(x):
    x2 = x * x
    th = jnp.tanh(GELU_C * (x + GELU_A * x * x2))
    g = 0.5 * x * (1.0 + th)
    dg = 0.5 * (1.0 + th) + 0.5 * x * (1.0 - th * th) * GELU_C * (1.0 + 3.0 * GELU_A * x2)
    return g, dg


def _dot(a, b):
    return jnp.dot(a, b, preferred_element_type=F32)


def _dot_nt(a, b):
    return lax.dot_general(a, b, (((1,), (1,)), ((), ())), preferred_element_type=F32)


def _dot_tn(a, b):
    return lax.dot_general(a, b, (((0,), (0,)), ((), ())), preferred_element_type=F32)


def _rms_scale(xv):
    return lax.rsqrt(jnp.mean(xv * xv, axis=-1, keepdims=True) + NORM_EPS)


def _rms_bwd(dy, xv, g):
    r = _rms_scale(xv)
    xh = xv * r
    dyg = dy * g
    dx = r * (dyg - xh * jnp.mean(dyg * xh, axis=-1, keepdims=True))
    return dx, dy * xh


def _norm_matmul(x, g, w, *, tm, tn, name):
    T, K = x.shape
    N = w.shape[1]

    def body(x_ref, g_ref, w_ref, o_ref, h_ref):
        @pl.when(pl.program_id(1) == 0)
        def _():
            xv = x_ref[...]
            h_ref[...] = (xv * _rms_scale(xv) * g_ref[...]).astype(BF16)

        o_ref[...] = _dot(h_ref[...], w_ref[...])

    return pl.pallas_call(
        body, name=name, grid=(T // tm, N // tn),
        in_specs=[pl.BlockSpec((tm, K), lambda i, j: (i, 0)), pl.BlockSpec((1, K), lambda i, j: (0, 0)),
                  pl.BlockSpec((K, tn), lambda i, j: (0, j))],
        out_specs=[pl.BlockSpec((tm, tn), lambda i, j: (i, j)), pl.BlockSpec((tm, K), lambda i, j: (i, 0))],
        out_shape=[jax.ShapeDtypeStruct((T, N), F32), jax.ShapeDtypeStruct((T, K), BF16)],
        compiler_params=_params(dimension_semantics=("parallel", "arbitrary")),
    )(x, g, w)


def _ffn_in(x2, g, w, *, tm, tn):
    T, K = x2.shape
    nb = D_FF // tn

    def body(x_ref, g_ref, wg_ref, wu_ref, gate_ref, up_ref, act_ref, h_ref):
        @pl.when(pl.program_id(1) == 0)
        def _():
            xv = x_ref[...]
            h_ref[...] = (xv * _rms_scale(xv) * g_ref[...]).astype(BF16)

        h = h_ref[...]
        gate = _dot(h, wg_ref[...])
        up = _dot(h, wu_ref[...])
        gate_ref[...] = gate
        up_ref[...] = up
        act_ref[...] = (gate * _sigmoid(gate) * up).astype(BF16)

    blk = pl.BlockSpec((tm, tn), lambda i, j: (i, j))
    return pl.pallas_call(
        body, name="ffn_in", grid=(T // tm, nb),
        in_specs=[pl.BlockSpec((tm, K), lambda i, j: (i, 0)), pl.BlockSpec((1, K), lambda i, j: (0, 0)),
                  pl.BlockSpec((K, tn), lambda i, j: (0, j)), pl.BlockSpec((K, tn), lambda i, j: (0, j + nb))],
        out_specs=[blk, blk, blk, pl.BlockSpec((tm, K), lambda i, j: (i, 0))],
        out_shape=[jax.ShapeDtypeStruct((T, D_FF), F32), jax.ShapeDtypeStruct((T, D_FF), F32),
                   jax.ShapeDtypeStruct((T, D_FF), BF16), jax.ShapeDtypeStruct((T, K), BF16)],
        compiler_params=_params(dimension_semantics=("parallel", "arbitrary")),
    )(x2, g, w, w)


def _branch_mix(pm, z, w_pool_out, w_rnn_out, proj, *, tm, tn):
    T = pm.shape[0]
    col_gp = (D_POOL + 2 * D_RNN) // tn
    col_gr = col_gp + D_MODEL // tn

    def body(pm_ref, z_ref, wp_ref, wr_ref, gp_ref, gr_ref, yp_ref, yr_ref, mix_ref):
        yp = _dot(pm_ref[...], wp_ref[...])
        yr = _dot(z_ref[...], wr_ref[...])
        yp_ref[...] = yp
        yr_ref[...] = yr
        mix_ref[...] = (_sigmoid(gp_ref[...]) * yp + _sigmoid(gr_ref[...]) * yr).astype(BF16)

    blk = pl.BlockSpec((tm, tn), lambda i, j: (i, j))
    return pl.pallas_call(
        body, name="branch_mix", grid=(T // tm, D_MODEL // tn),
        in_specs=[pl.BlockSpec((tm, D_POOL), lambda i, j: (i, 0)), pl.BlockSpec((tm, D_RNN), lambda i, j: (i, 0)),
                  pl.BlockSpec((D_POOL, tn), lambda i, j: (0, j)), pl.BlockSpec((D_RNN, tn), lambda i, j: (0, j)),
                  pl.BlockSpec((tm, tn), lambda i, j: (i, col_gp + j)), pl.BlockSpec((tm, tn), lambda i, j: (i, col_gr + j))],
        out_specs=[blk, blk, blk],
        out_shape=[jax.ShapeDtypeStruct((T, D_MODEL), F32), jax.ShapeDtypeStruct((T, D_MODEL), F32),
                   jax.ShapeDtypeStruct((T, D_MODEL), BF16)],
        compiler_params=_params(dimension_semantics=("parallel", "parallel")),
    )(pm, z, w_pool_out, w_rnn_out, proj, proj)


def _out_proj_residual(mix, w_o, x, *, tm):
    T = x.shape[0]

    def body(mix_ref, w_ref, x_ref, o_ref):
        o_ref[...] = x_ref[...] + _dot(mix_ref[...], w_ref[...])

    row = pl.BlockSpec((tm, D_MODEL), lambda i: (i, 0))
    return pl.pallas_call(
        body, name="out_proj_residual", grid=(T // tm,),
        in_specs=[row, pl.BlockSpec((D_MODEL, D_MODEL), lambda i: (0, 0)), row],
        out_specs=row, out_shape=jax.ShapeDtypeStruct((T, D_MODEL), F32),
        compiler_params=_params(dimension_semantics=("parallel",)),
    )(mix, w_o, x)


def _ffn_out_loss(act, w, x2, g3, target, *, tm):
    T = x2.shape[0]

    def body(act_ref, w_ref, x2_ref, g_ref, t_ref, dx_ref, dxb_ref, sq_ref, dg_ref):
        @pl.when(pl.program_id(0) == 0)
        def _():
            sq_ref[...] = jnp.zeros_like(sq_ref)
            dg_ref[...] = jnp.zeros_like(dg_ref)

        x3 = x2_ref[...] + _dot(act_ref[...], w_ref[...])
        g = g_ref[...]
        err = x3 * _rms_scale(x3) * g - t_ref[...]
        sq_ref[...] += jnp.sum(err * err, axis=0, keepdims=True)
        dx, dgp = _rms_bwd(err * (1.0 / D_MODEL), x3, g)
        dg_ref[...] += jnp.sum(dgp, axis=0, keepdims=True)
        dx_ref[...] = dx
        dxb_ref[...] = dx.astype(BF16)

    row = pl.BlockSpec((tm, D_MODEL), lambda i: (i, 0))
    vec = pl.BlockSpec((1, D_MODEL), lambda i: (0, 0))
    return pl.pallas_call(
        body, name="ffn_out_loss", grid=(T // tm,),
        in_specs=[pl.BlockSpec((tm, D_FF), lambda i: (i, 0)), pl.BlockSpec((D_FF, D_MODEL), lambda i: (0, 0)), row, vec, row],
        out_specs=[row, row, vec, vec],
        out_shape=[jax.ShapeDtypeStruct((T, D_MODEL), F32), jax.ShapeDtypeStruct((T, D_MODEL), BF16),
                   jax.ShapeDtypeStruct((1, D_MODEL), F32), jax.ShapeDtypeStruct((1, D_MODEL), F32)],
        compiler_params=_params(dimension_semantics=("arbitrary",)),
    )(act, w, x2, g3, target)


def _ffn_out_bwd(dx3b, w, gate, up, *, tm, tn):
    T = dx3b.shape[0]

    def body(dx_ref, w_ref, gate_ref, up_ref, dgate_ref, dup_ref):
        dact = _dot_nt(dx_ref[...], w_ref[...])
        gate = gate_ref[...]
        s = _sigmoid(gate)
        dgate_ref[...] = (dact * up_ref[...] * s * (1.0 + gate * (1.0 - s))).astype(BF16)
        dup_ref[...] = (dact * gate * s).astype(BF16)

    blk = pl.BlockSpec((tm, tn), lambda i, j: (i, j))
    return pl.pallas_call(
        body, name="ffn_out_bwd", grid=(T // tm, D_FF // tn),
        in_specs=[pl.BlockSpec((tm, D_MODEL), lambda i, j: (i, 0)), pl.BlockSpec((tn, D_MODEL), lambda i, j: (j, 0)), blk, blk],
        out_specs=[blk, blk],
        out_shape=[jax.ShapeDtypeStruct((T, D_FF), BF16), jax.ShapeDtypeStruct((T, D_FF), BF16)],
        compiler_params=_params(dimension_semantics=("parallel", "parallel")),
    )(dx3b, w, gate, up)


def _ffn_in_bwd(dgate, dup, w, dx3, x2, g2, *, tm):
    T = x2.shape[0]

    def body(dgate_ref, dup_ref, w_ref, dx3_ref, x2_ref, g_ref, dx_ref, dxb_ref, dg_ref):
        @pl.when(pl.program_id(0) == 0)
        def _():
            dg_ref[...] = jnp.zeros_like(dg_ref)

        dh = _dot_nt(dgate_ref[...], w_ref[:, :D_FF]) + _dot_nt(dup_ref[...], w_ref[:, D_FF:])
        dxn, dgp = _rms_bwd(dh, x2_ref[...], g_ref[...])
        dx = dx3_ref[...] + dxn
        dg_ref[...] += jnp.sum(dgp, axis=0, keepdims=True)
        dx_ref[...] = dx
        dxb_ref[...] = dx.astype(BF16)

    row = pl.BlockSpec((tm, D_MODEL), lambda i: (i, 0))
    wide = pl.BlockSpec((tm, D_FF), lambda i: (i, 0))
    vec = pl.BlockSpec((1, D_MODEL), lambda i: (0, 0))
    return pl.pallas_call(
        body, name="ffn_in_bwd", grid=(T // tm,),
        in_specs=[wide, wide, pl.BlockSpec((D_MODEL, 2 * D_FF), lambda i: (0, 0)), row, row, vec],
        out_specs=[row, row, vec],
        out_shape=[jax.ShapeDtypeStruct((T, D_MODEL), F32), jax.ShapeDtypeStruct((T, D_MODEL), BF16),
                   jax.ShapeDtypeStruct((1, D_MODEL), F32)],
        compiler_params=_params(dimension_semantics=("arbitrary",)),
    )(dgate, dup, w, dx3, x2, g2)


def _out_proj_bwd(dx2b, w_o, proj, y_pool, y_rnn, *, tm, tn):
    T = dx2b.shape[0]
    col_gp = (D_POOL + 2 * D_RNN) // tn
    col_gr = col_gp + D_MODEL // tn

    def body(dx_ref, w_ref, gp_ref, gr_ref, yp_ref, yr_ref, dgp_ref, dgr_ref, dyp_ref, dyr_ref):
        dmix = _dot_nt(dx_ref[...], w_ref[...])
        sp = _sigmoid(gp_ref[...])
        sr = _sigmoid(gr_ref[...])
        dgp_ref[...] = (dmix * yp_ref[...] * sp * (1.0 - sp)).astype(BF16)
        dgr_ref[...] = (dmix * yr_ref[...] * sr * (1.0 - sr)).astype(BF16)
        dyp_ref[...] = (dmix * sp).astype(BF16)
        dyr_ref[...] = (dmix * sr).astype(BF16)

    blk = pl.BlockSpec((tm, tn), lambda i, j: (i, j))
    out = jax.ShapeDtypeStruct((T, D_MODEL), BF16)
    return pl.pallas_call(
        body, name="out_proj_bwd", grid=(T // tm, D_MODEL // tn),
        in_specs=[pl.BlockSpec((tm, D_MODEL), lambda i, j: (i, 0)), pl.BlockSpec((tn, D_MODEL), lambda i, j: (j, 0)),
                  pl.BlockSpec((tm, tn), lambda i, j: (i, col_gp + j)), pl.BlockSpec((tm, tn), lambda i, j: (i, col_gr + j)), blk, blk],
        out_specs=[blk, blk, blk, blk], out_shape=[out, out, out, out],
        compiler_params=_params(dimension_semantics=("parallel", "parallel")),
    )(dx2b, w_o, proj, proj, y_pool, y_rnn)


def _branch_bwd(dyp, dyr, w_pool_out, w_rnn_out, *, tm):
    T = dyp.shape[0]

    def body(dyp_ref, dyr_ref, wp_ref, wr_ref, dpm_ref, dz_ref):
        dpm_ref[...] = _dot_nt(dyp_ref[...], wp_ref[...])
        dz_ref[...] = _dot_nt(dyr_ref[...], wr_ref[...])

    row = pl.BlockSpec((tm, D_MODEL), lambda i: (i, 0))
    return pl.pallas_call(
        body, name="branch_bwd", grid=(T // tm,),
        in_specs=[row, row, pl.BlockSpec((D_POOL, D_MODEL), lambda i: (0, 0)), pl.BlockSpec((D_RNN, D_MODEL), lambda i: (0, 0))],
        out_specs=[pl.BlockSpec((tm, D_POOL), lambda i: (i, 0)), pl.BlockSpec((tm, D_RNN), lambda i: (i, 0))],
        out_shape=[jax.ShapeDtypeStruct((T, D_POOL), F32), jax.ShapeDtypeStruct((T, D_RNN), F32)],
        compiler_params=_params(dimension_semantics=("parallel",)),
    )(dyp, dyr, w_pool_out, w_rnn_out)


def _in_proj_bwd(segs, w, dx2, x, g1, *, tm):
    T = x.shape[0]
    widths = [s.shape[1] for s in segs]
    offs = [sum(widths[:k]) for k in range(len(widths))]
    n = len(segs)

    def body(*refs):
        seg_refs, (w_ref, dx2_ref, x_ref, g_ref, dx_ref, dg_ref) = refs[:n], refs[n:]

        @pl.when(pl.program_id(0) == 0)
        def _():
            dg_ref[...] = jnp.zeros_like(dg_ref)

        dh = _dot_nt(seg_refs[0][...], w_ref[:, offs[0]:offs[0] + widths[0]])
        for k in range(1, n):
            dh += _dot_nt(seg_refs[k][...], w_ref[:, offs[k]:offs[k] + widths[k]])
        dxn, dgp = _rms_bwd(dh, x_ref[...], g_ref[...])
        dg_ref[...] += jnp.sum(dgp, axis=0, keepdims=True)
        dx_ref[...] = dx2_ref[...] + dxn

    row = pl.BlockSpec((tm, D_MODEL), lambda i: (i, 0))
    vec = pl.BlockSpec((1, D_MODEL), lambda i: (0, 0))
    return pl.pallas_call(
        body, name="in_proj_bwd", grid=(T // tm,),
        in_specs=[pl.BlockSpec((tm, wd), lambda i: (i, 0)) for wd in widths]
        + [pl.BlockSpec((D_MODEL, D_IN), lambda i: (0, 0)), row, row, vec],
        out_specs=[row, vec],
        out_shape=[jax.ShapeDtypeStruct((T, D_MODEL), F32), jax.ShapeDtypeStruct((1, D_MODEL), F32)],
        compiler_params=_params(dimension_semantics=("arbitrary",)),
    )(*segs, w, dx2, x, g1)


def _weight_grad(a, segs, *, tm, tn, name):
    T, M = a.shape
    nblk = [s.shape[1] // tn for s in segs]
    first = [sum(nblk[:k]) for k in range(len(segs))]
    n = len(segs)

    def body(a_ref, *refs):
        seg_refs, o_ref = refs[:n], refs[n]
        j = pl.program_id(1)
        for k in range(n):
            @pl.when((j >= first[k]) & (j < first[k] + nblk[k]))
            def _(k=k):
                o_ref[...] = _dot_tn(a_ref[...], seg_refs[k][...])

    def seg_spec(k):
        return pl.BlockSpec((T, tn), lambda i, j: (0, jnp.clip(j - first[k], 0, nblk[k] - 1)))

    return pl.pallas_call(
        body, name=name, grid=(M // tm, sum(nblk)),
        in_specs=[pl.BlockSpec((T, tm), lambda i, j: (0, i))] + [seg_spec(k) for k in range(n)],
        out_specs=pl.BlockSpec((tm, tn), lambda i, j: (i, j)),
        out_shape=jax.ShapeDtypeStruct((M, sum(nblk) * tn), F32),
        compiler_params=_params(dimension_semantics=("parallel", "arbitrary")),
    )(a, *segs)


def _pad_front(dst, src, halo):
    dst[pl.ds(0, halo), :] = jnp.zeros((halo, src.shape[1]), F32)

    def fill(i, carry):
        r0 = pl.multiple_of(i * CHUNK, CHUNK)
        dst[pl.ds(r0 + halo, CHUNK), :] = src[pl.ds(r0, CHUNK), :]
        return carry

    lax.fori_loop(0, src.shape[0] // CHUNK, fill, 0)


def _shift_rows(v, k):
    return pltpu.roll(v, k % v.shape[0], axis=0)


def _window_sums(xs, direction):
    s2 = xs + _shift_rows(xs, direction)
    s4 = s2 + _shift_rows(s2, 2 * direction)
    s8 = s4 + _shift_rows(s4, 4 * direction)
    s16 = s8 + _shift_rows(s8, 8 * direction)
    return s2, s4, s8, s16


def _select_window(g, sums):
    s2, s4, s8, s16 = sums
    return jnp.where(g == 0, s2, jnp.where(g == 1, s4, jnp.where(g == 2, s8, s16)))


def _pool_count(g, start, rows):
    t = start + lax.broadcasted_iota(jnp.int32, (rows, 1), 0)
    return jnp.minimum(t + 1, jnp.left_shift(2, g)).astype(F32)


def _pool_fwd(proj, w_grp, scale):
    T = proj.shape[0]
    nchunk = T // CHUNK

    def body(u_ref, w_ref, s_ref, o_ref, upad):
        g = pl.program_id(0)
        _pad_front(upad, u_ref, POOL_HALO)
        w = w_ref[...].astype(BF16)
        scale_row = s_ref[...]

        def chunk(i, carry):
            r0 = pl.multiple_of(i * CHUNK, CHUNK)
            xs = upad[pl.ds(r0, CHUNK + POOL_HALO), :]
            win = _select_window(g, _window_sums(xs, 1))[POOL_HALO:]
            pooled = win / _pool_count(g, r0, CHUNK) - xs[POOL_HALO:]
            o_ref[pl.ds(r0, CHUNK), :] = (_dot(pooled.astype(BF16), w) * scale_row).astype(BF16)
            return carry

        lax.fori_loop(0, nchunk, chunk, 0)

    return pl.pallas_call(
        body, name="pool_fwd", grid=(N_POOL_GROUPS,),
        in_specs=[pl.BlockSpec((T, HEAD), lambda g: (0, g)), pl.BlockSpec((None, HEAD, HEAD), lambda g: (g, 0, 0)),
                  pl.BlockSpec((1, HEAD), lambda g: (0, g))],
        out_specs=pl.BlockSpec((T, HEAD), lambda g: (0, g)),
        out_shape=jax.ShapeDtypeStruct((T, D_POOL), BF16),
        scratch_shapes=[pltpu.VMEM((T + POOL_HALO, HEAD), F32)],
        compiler_params=_params(dimension_semantics=("parallel",)),
    )(proj, w_grp, scale)


def _pool_bwd(proj, dpm, w_grp, scale):
    T = proj.shape[0]
    nchunk = T // CHUNK

    def body(u_ref, dpm_ref, w_ref, s_ref, du_ref, dw_ref, ds_ref, upad, zpad, dpool):
        g = pl.program_id(0)
        _pad_front(upad, u_ref, POOL_HALO)
        zpad[pl.ds(T, POOL_HALO), :] = jnp.zeros((POOL_HALO, HEAD), F32)
        dw_ref[...] = jnp.zeros_like(dw_ref)
        ds_ref[...] = jnp.zeros_like(ds_ref)
        w = w_ref[...].astype(BF16)
        scale_row = s_ref[...]

        def chunk(i, carry):
            r0 = pl.multiple_of(i * CHUNK, CHUNK)
            xs = upad[pl.ds(r0, CHUNK + POOL_HALO), :]
            cnt = _pool_count(g, r0, CHUNK)
            pooled = (_select_window(g, _window_sums(xs, 1))[POOL_HALO:] / cnt - xs[POOL_HALO:]).astype(BF16)
            mixed = _dot(pooled, w)
            d = dpm_ref[pl.ds(r0, CHUNK), :]
            ds_ref[...] += jnp.sum(d * mixed, axis=0, keepdims=True)
            dmixed = (d * scale_row).astype(BF16)
            dw_ref[...] += _dot_tn(pooled, dmixed)
            dp = _dot_nt(dmixed, w)
            dpool[pl.ds(r0, CHUNK), :] = dp
            zpad[pl.ds(r0, CHUNK), :] = dp / cnt
            return carry

        lax.fori_loop(0, nchunk, chunk, 0)

        def chunk2(i, carry):
            r0 = pl.multiple_of(i * CHUNK, CHUNK)
            zs = zpad[pl.ds(r0, CHUNK + POOL_HALO), :]
            win = _select_window(g, _window_sums(zs, -1))[:CHUNK]
            du_ref[pl.ds(r0, CHUNK), :] = (win - dpool[pl.ds(r0, CHUNK), :]).astype(BF16)
            return carry

        lax.fori_loop(0, nchunk, chunk2, 0)

    col = pl.BlockSpec((T, HEAD), lambda g: (0, g))
    return pl.pallas_call(
        body, name="pool_bwd", grid=(N_POOL_GROUPS,),
        in_specs=[col, col, pl.BlockSpec((None, HEAD, HEAD), lambda g: (g, 0, 0)), pl.BlockSpec((1, HEAD), lambda g: (0, g))],
        out_specs=[col, pl.BlockSpec((None, HEAD, HEAD), lambda g: (g, 0, 0)), pl.BlockSpec((1, HEAD), lambda g: (0, g))],
        out_shape=[jax.ShapeDtypeStruct((T, D_POOL), BF16), jax.ShapeDtypeStruct((N_POOL_GROUPS, HEAD, HEAD), F32),
                   jax.ShapeDtypeStruct((1, D_POOL), F32)],
        scratch_shapes=[pltpu.VMEM((T + POOL_HALO, HEAD), F32), pltpu.VMEM((T + POOL_HALO, HEAD), F32), pltpu.VMEM((T, HEAD), F32)],
        compiler_params=_params(dimension_semantics=("parallel",)),
    )(proj, dpm, w_grp, scale)


def _conv_taps(xs, cw):
    v = cw[CONV_WIDTH - 1] * xs[SUBLANES:]
    for k in range(CONV_WIDTH - 1):
        v += cw[k] * _shift_rows(xs, CONV_WIDTH - 1 - k)[SUBLANES:]
    return v


def _tap_rows(cw_ref):
    return [cw_ref[k:k + 1, :] for k in range(CONV_WIDTH)]


def _softplus_neg(lam):
    return jnp.maximum(-lam, 0.0) + _log1p(jnp.exp(-jnp.abs(lam)))


def _lru_gates(v, wa, ba, wx, bx, sp):
    vb = v.astype(BF16)
    ra = _sigmoid(_dot(vb, wa) + ba)
    ix = _sigmoid(_dot(vb, wx) + bx)
    log_a = -LRU_C * ra * sp
    a = jnp.exp(log_a)
    sq = jnp.sqrt(-jnp.tanh(log_a) * (a * a + 1.0))
    return ra, ix, a, sq


def _row_bcast(v, r):
    return jnp.broadcast_to(v[r:r + 1, :], v.shape)


def _tile_scan(a8, b8, direction):
    ri = lax.broadcasted_iota(jnp.int32, a8.shape, 0)
    A, B = a8, b8
    for s in (1, 2, 4):
        ok = (ri >= s) if direction == 1 else (ri + s < SUBLANES)
        As, Bs = _shift_rows(A, s * direction), _shift_rows(B, s * direction)
        B = jnp.where(ok, A * Bs + B, B)
        A = jnp.where(ok, A * As, A)
    return A, B


def _rnn_fwd(proj, conv_w, conv_b, w_a, b_a, w_x, b_x, lam):
    T = proj.shape[0]
    nchunk = T // CHUNK
    ntile = T // SUBLANES

    def body(u_ref, ug_ref, cw_ref, cb_ref, wa_ref, ba_ref, wx_ref, bx_ref, lam_ref, h_ref, z_ref, upad, a_s, b_s):
        _pad_front(upad, u_ref, SUBLANES)
        cw, cb = _tap_rows(cw_ref), cb_ref[...]
        wa, wx = wa_ref[...].astype(BF16), wx_ref[...].astype(BF16)
        ba, bx = ba_ref[...], bx_ref[...]
        sp = _softplus_neg(lam_ref[...])

        def chunk(i, carry):
            r0 = pl.multiple_of(i * CHUNK, CHUNK)
            v = _conv_taps(upad[pl.ds(r0, CHUNK + SUBLANES), :], cw) + cb
            _, ix, a, sq = _lru_gates(v, wa, ba, wx, bx, sp)
            a_s[pl.ds(r0, CHUNK), :] = a
            b_s[pl.ds(r0, CHUNK), :] = sq * ix * v
            return carry

        lax.fori_loop(0, nchunk, chunk, 0)

        def tile(i, carry):
            r0 = pl.multiple_of(i * SUBLANES, SUBLANES)
            A, B = _tile_scan(a_s[pl.ds(r0, SUBLANES), :], b_s[pl.ds(r0, SUBLANES), :], 1)
            h_ref[pl.ds(r0, SUBLANES), :] = A * carry + B
            return _row_bcast(A, SUBLANES - 1) * carry + _row_bcast(B, SUBLANES - 1)

        lax.fori_loop(0, ntile, tile, jnp.zeros((SUBLANES, HEAD), F32), unroll=4)

        def chunk3(i, carry):
            r0 = pl.multiple_of(i * CHUNK, CHUNK)
            gl, _ = _gelu_parts(ug_ref[pl.ds(r0, CHUNK), :])
            z_ref[pl.ds(r0, CHUNK), :] = (h_ref[pl.ds(r0, CHUNK), :] * gl).astype(BF16)
            return carry

        lax.fori_loop(0, nchunk, chunk3, 0)

    col = pl.BlockSpec((T, HEAD), lambda h: (0, h))
    vec = pl.BlockSpec((1, HEAD), lambda h: (0, h))
    mat = pl.BlockSpec((None, HEAD, HEAD), lambda h: (h, 0, 0))
    return pl.pallas_call(
        body, name="rnn_fwd", grid=(N_RNN_HEADS,),
        in_specs=[pl.BlockSpec((T, HEAD), lambda h: (0, COL_RNN + h)), pl.BlockSpec((T, HEAD), lambda h: (0, COL_GATE + h)),
                  pl.BlockSpec((CONV_WIDTH, HEAD), lambda h: (0, h)), vec, mat, vec, mat, vec, vec],
        out_specs=[col, col],
        out_shape=[jax.ShapeDtypeStruct((T, D_RNN), F32), jax.ShapeDtypeStruct((T, D_RNN), BF16)],
        scratch_shapes=[pltpu.VMEM((T + SUBLANES, HEAD), F32), pltpu.VMEM((T, HEAD), F32), pltpu.VMEM((T, HEAD), F32)],
        compiler_params=_params(dimension_semantics=("parallel",)),
    )(proj, proj, conv_w, conv_b, w_a, b_a, w_x, b_x, lam)


def _rnn_bwd(proj, hr, dz, conv_w, conv_b, w_a, b_a, w_x, b_x, lam):
    T = proj.shape[0]
    nchunk = T // CHUNK
    ntile = T // SUBLANES

    def body(u_ref, ug_ref, h_ref, dz_ref, cw_ref, cb_ref, wa_ref, ba_ref, wx_ref, bx_ref, lam_ref,
             du_ref, dug_ref, dwa_ref, dwx_ref, dba_ref, dbx_ref, dlam_ref, dcb_ref, dcw_ref,
             upad, hpad, apad, v_s, ra_s, ix_s, sq_s, g_s, dvpad):
        zero_tile = jnp.zeros((SUBLANES, HEAD), F32)
        _pad_front(upad, u_ref, SUBLANES)
        _pad_front(hpad, h_ref, SUBLANES)
        apad[pl.ds(T, SUBLANES), :] = zero_tile
        dvpad[pl.ds(T, SUBLANES), :] = zero_tile
        for ref in (dwa_ref, dwx_ref, dba_ref, dbx_ref, dlam_ref, dcb_ref, dcw_ref):
            ref[...] = jnp.zeros_like(ref)
        cw, cb = _tap_rows(cw_ref), cb_ref[...]
        wa, wx = wa_ref[...].astype(BF16), wx_ref[...].astype(BF16)
        ba, bx = ba_ref[...], bx_ref[...]
        lam_row = lam_ref[...]
        sp = _softplus_neg(lam_row)

        def chunk(i, carry):
            r0 = pl.multiple_of(i * CHUNK, CHUNK)
            rows = pl.ds(r0, CHUNK)
            h = h_ref[rows, :]
            v = _conv_taps(upad[pl.ds(r0, CHUNK + SUBLANES), :], cw) + cb
            ra, ix, a, sq = _lru_gates(v, wa, ba, wx, bx, sp)
            v_s[rows, :], ra_s[rows, :], ix_s[rows, :], sq_s[rows, :], apad[rows, :] = v, ra, ix, sq, a
            gl, dgl = _gelu_parts(ug_ref[rows, :])
            d = dz_ref[rows, :]
            g_s[rows, :] = d * gl
            dug_ref[rows, :] = (d * h * dgl).astype(BF16)
            return carry

        lax.fori_loop(0, nchunk, chunk, 0)

        def tile(k, carry):
            r0 = pl.multiple_of((ntile - 1 - k) * SUBLANES, SUBLANES)
            rows = pl.ds(r0, SUBLANES)
            ri = lax.broadcasted_iota(jnp.int32, (SUBLANES, HEAD), 0)
            a_next = jnp.where(ri < SUBLANES - 1, _shift_rows(apad[rows, :], -1),
                               _shift_rows(apad[pl.ds(r0 + SUBLANES, SUBLANES), :], -1))
            A, B = _tile_scan(a_next, g_s[rows, :], -1)
            g_s[rows, :] = A * carry + B
            return _row_bcast(A, 0) * carry + _row_bcast(B, 0)

        lax.fori_loop(0, ntile, tile, zero_tile, unroll=4)

        def chunk3(i, carry):
            r0 = pl.multiple_of(i * CHUNK, CHUNK)
            rows = pl.ds(r0, CHUNK)
            g = g_s[rows, :]
            h_prev = _shift_rows(hpad[pl.ds(r0, CHUNK + SUBLANES), :], 1)[SUBLANES:]
            v, ra, ix, sq, a = v_s[rows, :], ra_s[rows, :], ix_s[rows, :], sq_s[rows, :], apad[rows, :]
            d_sq = g * ix * v
            d_ix = g * sq * v
            d_la = a * g * h_prev - d_sq * a * a / sq
            dlam_ref[...] += jnp.sum(d_la * ra, axis=0, keepdims=True)
            d_pa = d_la * (-LRU_C) * sp * ra * (1.0 - ra)
            d_px = d_ix * ix * (1.0 - ix)
            vb, d_pab, d_pxb = v.astype(BF16), d_pa.astype(BF16), d_px.astype(BF16)
            dwa_ref[...] += _dot_tn(vb, d_pab)
            dwx_ref[...] += _dot_tn(vb, d_pxb)
            dba_ref[...] += jnp.sum(d_pa, axis=0, keepdims=True)
            dbx_ref[...] += jnp.sum(d_px, axis=0, keepdims=True)
            dv = g * sq * ix + _dot_nt(d_pab, wa) + _dot_nt(d_pxb, wx)
            dvpad[rows, :] = dv
            dcb_ref[...] += jnp.sum(dv, axis=0, keepdims=True)
            xs = upad[pl.ds(r0, CHUNK + SUBLANES), :]
            for k in range(CONV_WIDTH):
                u_k = _shift_rows(xs, CONV_WIDTH - 1 - k)[SUBLANES:] if k < CONV_WIDTH - 1 else xs[SUBLANES:]
                dcw_ref[k:k + 1, :] += jnp.sum(dv * u_k, axis=0, keepdims=True)
            return carry

        lax.fori_loop(0, nchunk, chunk3, 0)
        dlam_ref[...] = dlam_ref[...] * (LRU_C * _sigmoid(-lam_row))

        def chunk4(i, carry):
            r0 = pl.multiple_of(i * CHUNK, CHUNK)
            dvs = dvpad[pl.ds(r0, CHUNK + SUBLANES), :]
            du = cw[CONV_WIDTH - 1] * dvs[:CHUNK]
            for k in range(CONV_WIDTH - 1):
                du += cw[k] * _shift_rows(dvs, -(CONV_WIDTH - 1 - k))[:CHUNK]
            du_ref[pl.ds(r0, CHUNK), :] = du.astype(BF16)
            return carry

        lax.fori_loop(0, nchunk, chunk4, 0)

    col = pl.BlockSpec((T, HEAD), lambda h: (0, h))
    vec = pl.BlockSpec((1, HEAD), lambda h: (0, h))
    mat = pl.BlockSpec((None, HEAD, HEAD), lambda h: (h, 0, 0))
    taps = pl.BlockSpec((CONV_WIDTH, HEAD), lambda h: (0, h))
    vec_out = jax.ShapeDtypeStruct((1, D_RNN), F32)
    mat_out = jax.ShapeDtypeStruct((N_RNN_HEADS, HEAD, HEAD), F32)
    seq = pltpu.VMEM((T, HEAD), F32)
    seq_pad = pltpu.VMEM((T + SUBLANES, HEAD), F32)
    return pl.pallas_call(
        body, name="rnn_bwd", grid=(N_RNN_HEADS,),
        in_specs=[pl.BlockSpec((T, HEAD), lambda h: (0, COL_RNN + h)), pl.BlockSpec((T, HEAD), lambda h: (0, COL_GATE + h)),
                  col, col, taps, vec, mat, vec, mat, vec, vec],
        out_specs=[col, col, mat, mat, vec, vec, vec, vec, taps],
        out_shape=[jax.ShapeDtypeStruct((T, D_RNN), BF16), jax.ShapeDtypeStruct((T, D_RNN), BF16), mat_out, mat_out,
                   vec_out, vec_out, vec_out, vec_out, jax.ShapeDtypeStruct((CONV_WIDTH, D_RNN), F32)],
        scratch_shapes=[seq_pad, seq_pad, seq_pad, seq, seq, seq, seq, seq, seq_pad],
        compiler_params=_params(dimension_semantics=("parallel",)),
    )(proj, proj, hr, dz, conv_w, conv_b, w_a, b_a, w_x, b_x, lam)


def _local_step(x, target, norm_mix, w_in, w_pool_grp, pool_scale, w_pool_out, conv_w, conv_b, w_rg_a, b_rg_a,
                w_rg_x, b_rg_x, lam, w_rnn_out, w_o, norm_ffn, w_ffn_in, w_ffn_out, norm_final):
    T = x.shape[0]
    tm = min(T, 512)
    proj, h1 = _norm_matmul(x, norm_mix, w_in, tm=tm, tn=1536, name="in_proj")
    pm = _pool_fwd(proj, w_pool_grp, pool_scale)
    hr, z = _rnn_fwd(proj, conv_w, conv_b, w_rg_a, b_rg_a, w_rg_x, b_rg_x, lam)
    y_pool, y_rnn, mix = _branch_mix(pm, z, w_pool_out, w_rnn_out, proj, tm=tm, tn=512)
    x2 = _out_proj_residual(mix, w_o, x, tm=tm)
    gate, up, act, h2 = _ffn_in(x2, norm_ffn, w_ffn_in, tm=tm, tn=1408)
    tr = min(T, 256)
    dx3, dx3b, sq_cols, g_norm_final = _ffn_out_loss(act, w_ffn_out, x2, norm_final, target, tm=tr)

    g = {"norm_final": g_norm_final}
    dgate, dup = _ffn_out_bwd(dx3b, w_ffn_out, gate, up, tm=tm, tn=1408)
    g["w_ffn_out"] = _weight_grad(act, [dx3b], tm=256, tn=D_MODEL, name="w_ffn_out_grad")
    dx2, dx2b, g["norm_ffn"] = _ffn_in_bwd(dgate, dup, w_ffn_in, dx3, x2, norm_ffn, tm=tr)
    g["w_ffn_in"] = _weight_grad(h2, [dgate, dup], tm=D_MODEL, tn=256, name="w_ffn_in_grad")
    dgp, dgr, dyp, dyr = _out_proj_bwd(dx2b, w_o, proj, y_pool, y_rnn, tm=tm, tn=512)
    g["w_o"] = _weight_grad(mix, [dx2b], tm=D_MODEL, tn=256, name="w_o_grad")
    dpm, dz = _branch_bwd(dyp, dyr, w_pool_out, w_rnn_out, tm=tm)
    g["w_pool_out"] = _weight_grad(pm, [dyp], tm=D_POOL, tn=256, name="w_pool_out_grad")
    g["w_rnn_out"] = _weight_grad(z, [dyr], tm=D_RNN, tn=256, name="w_rnn_out_grad")
    dupool, g["w_pool_grp"], g["pool_scale"] = _pool_bwd(proj, dpm, w_pool_grp, pool_scale)
    (durnn, dugate, g["w_rg_a"], g["w_rg_x"], g["b_rg_a"], g["b_rg_x"], g["lru_lambda"], g["conv_b"],
     g["conv_w"]) = _rnn_bwd(proj, hr, dz, conv_w, conv_b, w_rg_a, b_rg_a, w_rg_x, b_rg_x, lam)
    segs = [dupool, durnn, dugate, dgp, dgr]
    grad_x, g["norm_mix"] = _in_proj_bwd(segs, w_in, dx2, x, norm_mix, tm=tr)
    g["w_in"] = _weight_grad(h1, segs, tm=D_MODEL, tn=256, name="w_in_grad")
    return sq_cols, grad_x, g


LARGE = {"w_in": "col", "w_pool_out": "col", "w_rnn_out": "row", "w_o": "row", "w_ffn_in": "col", "w_ffn_out": "row"}
LARGE_SHAPE = {"w_in": (D_MODEL, D_IN), "w_pool_out": (D_POOL, D_MODEL), "w_rnn_out": (D_RNN, D_MODEL),
               "w_o": (D_MODEL, D_MODEL), "w_ffn_in": (D_MODEL, 2 * D_FF), "w_ffn_out": (D_FF, D_MODEL)}


def _place():
    x, y, c = lax.axis_index("x"), lax.axis_index("y"), lax.axis_index("c")
    return 2 * x + y, c


def _chip_device(chip, c):
    return (chip // 2, chip % 2, c)


def _chip_window(ref, kind, shape, chip, half=None):
    K, N = shape
    if kind == "col":
        rows = slice(None) if half is None else pl.ds(half * (K // 2), K // 2)
        return ref.at[rows, pl.ds(chip * (N // N_CHIPS), N // N_CHIPS)]
    ks = K // N_CHIPS
    if half is None:
        return ref.at[pl.ds(chip * ks, ks), :]
    return ref.at[pl.ds(chip * ks + half * (ks // 2), ks // 2), :]


def _row_half(ref, half):
    rows = ref.shape[0] // 2
    return ref.at[pl.ds(half * rows, rows), :]


def _gather_weights(full, conv_w_full):
    names = list(LARGE)
    n = len(names)
    cw_cols = D_RNN // N_CHIPS

    def body(*refs):
        full_refs, cwg_ref = refs[n + 1:2 * n + 1], refs[2 * n + 1]
        send_sems, recv_sems = refs[2 * n + 2:]
        chip, c = _place()
        others = [chip ^ r for r in (1, 2, 3)]

        def remote(k, win, sem, to):
            return pltpu.make_async_remote_copy(src_ref=win, dst_ref=win, send_sem=send_sems.at[k, sem],
                                                recv_sem=recv_sems.at[k, sem], device_id=to, device_id_type=MESH)

        sends = []
        for k, name in enumerate(names):
            for r, other in enumerate(others):
                cp = remote(k, _chip_window(full_refs[k], LARGE[name], LARGE_SHAPE[name], chip, c), r, _chip_device(other, c))
                cp.start()
                sends.append(cp)
        for r, other in enumerate(others):
            cp = remote(n, cwg_ref.at[:, pl.ds(chip * cw_cols, cw_cols)], r, _chip_device(other, c))
            cp.start()
            sends.append(cp)
        for k, name in enumerate(names):
            for r, other in enumerate(others):
                win = _chip_window(full_refs[k], LARGE[name], LARGE_SHAPE[name], other, c)
                remote(k, win, r, _chip_device(other, c)).wait_recv()
                cp = remote(k, win, 3 + r, _chip_device(chip, 1 - c))
                cp.start()
                sends.append(cp)
        for r, other in enumerate(others):
            remote(n, cwg_ref.at[:, pl.ds(other * cw_cols, cw_cols)], r, _chip_device(other, c)).wait_recv()
        for k, name in enumerate(names):
            for r, other in enumerate(others):
                win = _chip_window(full_refs[k], LARGE[name], LARGE_SHAPE[name], other, 1 - c)
                remote(k, win, 3 + r, _chip_device(chip, 1 - c)).wait_recv()
        for cp in sends:
            cp.wait_send()

    out = pl.pallas_call(
        body, name="gather_weights",
        in_specs=[ANY] * (n + 1), out_specs=[ANY] * (n + 1),
        out_shape=[jax.ShapeDtypeStruct(LARGE_SHAPE[name], BF16) for name in names]
        + [jax.ShapeDtypeStruct((CONV_WIDTH, D_RNN), F32)],
        input_output_aliases={k: k for k in range(n + 1)},
        scratch_shapes=[pltpu.SemaphoreType.DMA((n + 1, 6)), pltpu.SemaphoreType.DMA((n + 1, 6))],
        compiler_params=pltpu.CompilerParams(has_side_effects=True),
    )(*[full[name] for name in names], conv_w_full)
    return dict(zip(names, out[:n])), out[n]


def _core_halves(ref, kind, shape, c):
    return [_chip_window(ref, kind, shape, chip, c) for chip in range(N_CHIPS)]


def _exchange_sibling(grads):
    names = list(LARGE)
    n = len(names)

    def body(*refs):
        g_refs, got_refs = refs[:n], refs[n:2 * n]
        send_sems, recv_sems = refs[2 * n:]
        chip, c = _place()
        sibling = _chip_device(chip, 1 - c)
        for k, name in enumerate(names):
            kind, shape = LARGE[name], LARGE_SHAPE[name]
            if kind == "col":
                pairs = [(_row_half(g_refs[k], 1 - c), got_refs[k])]
            else:
                rows = shape[0] // N_DEV
                pairs = [(win, got_refs[k].at[pl.ds(j * rows, rows), :])
                         for j, win in enumerate(_core_halves(g_refs[k], kind, shape, 1 - c))]
            for src, dst in pairs:
                pltpu.make_async_remote_copy(src_ref=src, dst_ref=dst, send_sem=send_sems.at[k], recv_sem=recv_sems.at[k],
                                             device_id=sibling, device_id_type=MESH).start()
        for k in range(n):
            pltpu.make_async_remote_copy(src_ref=got_refs[k], dst_ref=got_refs[k], send_sem=send_sems.at[k],
                                         recv_sem=recv_sems.at[k], device_id=sibling, device_id_type=MESH).wait()

    out = pl.pallas_call(
        body, name="exchange_sibling", in_specs=[ANY] * n, out_specs=[ANY] * n,
        out_shape=[jax.ShapeDtypeStruct((LARGE_SHAPE[name][0] // 2, LARGE_SHAPE[name][1]), F32) for name in names],
        scratch_shapes=[pltpu.SemaphoreType.DMA((n,)), pltpu.SemaphoreType.DMA((n,))],
        compiler_params=pltpu.CompilerParams(has_side_effects=True),
    )(*[grads[name] for name in names])
    return dict(zip(names, out))


def _chip_sum(name, g, got, c):
    kind, (K, N) = LARGE[name], LARGE_SHAPE[name]
    rows = K // N_DEV

    def body(c_ref, g_ref, got_ref, o_ref, ob_ref):
        total = g_ref[...] + got_ref[...]
        o_ref[...] = total
        ob_ref[...] = total.astype(BF16)

    if kind == "col":
        mine = pl.BlockSpec((rows, N), lambda j, c_ref: (j + N_CHIPS * c_ref[0], 0))
    else:
        mine = pl.BlockSpec((rows, N), lambda j, c_ref: (2 * j + c_ref[0], 0))
    blk = pl.BlockSpec((rows, N), lambda j, c_ref: (j, 0))
    return pl.pallas_call(
        body, name=name + "_chip_sum",
        grid_spec=pltpu.PrefetchScalarGridSpec(num_scalar_prefetch=1, grid=(N_CHIPS,), in_specs=[mine, blk], out_specs=[blk, blk]),
        out_shape=[jax.ShapeDtypeStruct((K // 2, N), F32), jax.ShapeDtypeStruct((K // 2, N), BF16)],
        compiler_params=_params(dimension_semantics=("parallel",)),
    )(c, g, got)


def _piece(ref, kind, shape, chip):
    K, N = shape
    if kind == "col":
        return ref.at[:, pl.ds(chip * (N // N_CHIPS), N // N_CHIPS)]
    return ref.at[pl.ds(chip * (K // N_DEV), K // N_DEV), :]


def _piece_shape(name):
    kind, (K, N) = LARGE[name], LARGE_SHAPE[name]
    return (K // 2, N // N_CHIPS) if kind == "col" else (K // N_DEV, N)


def _exchange_chips(sums):
    names = list(LARGE)
    n = len(names)

    def body(*refs):
        s_refs, got_refs = refs[:n], refs[n:2 * n]
        send_sems, recv_sems = refs[2 * n:]
        chip, c = _place()
        copies = []
        for k, name in enumerate(names):
            kind, shape = LARGE[name], LARGE_SHAPE[name]
            for r in range(3):
                other = chip ^ (r + 1)
                cp = pltpu.make_async_remote_copy(src_ref=_piece(s_refs[k], kind, shape, other), dst_ref=got_refs[k].at[r],
                                                  send_sem=send_sems.at[k, r], recv_sem=recv_sems.at[k, r],
                                                  device_id=_chip_device(other, c), device_id_type=MESH)
                cp.start()
                copies.append(cp)
        for cp in copies:
            cp.wait()

    out = pl.pallas_call(
        body, name="exchange_chips", in_specs=[ANY] * n, out_specs=[ANY] * n,
        out_shape=[jax.ShapeDtypeStruct((3,) + _piece_shape(name), BF16) for name in names],
        scratch_shapes=[pltpu.SemaphoreType.DMA((n, 3)), pltpu.SemaphoreType.DMA((n, 3))],
        compiler_params=pltpu.CompilerParams(has_side_effects=True),
    )(*[sums[name] for name in names])
    return dict(zip(names, out))


def _final_sum(name, chip_sum, got, place):
    kind = LARGE[name]
    rows, cols = _piece_shape(name)

    def body(place_ref, s_ref, got_ref, o_ref):
        o_ref[...] = ((s_ref[...] + got_ref[0].astype(F32)) + got_ref[1].astype(F32)) + got_ref[2].astype(F32)

    if kind == "col":
        mine = pl.BlockSpec((rows, cols), lambda i, place_ref: (0, place_ref[0]))
    else:
        mine = pl.BlockSpec((rows, cols), lambda i, place_ref: (place_ref[0], 0))
    return pl.pallas_call(
        body, name=name + "_final_sum",
        grid_spec=pltpu.PrefetchScalarGridSpec(
            num_scalar_prefetch=1, grid=(1,), in_specs=[mine, pl.BlockSpec((3, rows, cols), lambda i, place_ref: (0, 0, 0))],
            out_specs=pl.BlockSpec((rows, cols), lambda i, place_ref: (place_ref[1], 0))),
        out_shape=jax.ShapeDtypeStruct((2 * rows, cols), F32),
        compiler_params=_params(dimension_semantics=("arbitrary",)),
    )(place, chip_sum, got)


def _join_halves(shards):
    names = list(LARGE)
    n = len(names)

    def body(*refs):
        full_refs = refs[n:2 * n]
        send_sems, recv_sems = refs[2 * n:]
        chip, c = _place()
        copies = []
        for k in range(n):
            cp = pltpu.make_async_remote_copy(src_ref=_row_half(full_refs[k], c), dst_ref=_row_half(full_refs[k], c),
                                              send_sem=send_sems.at[k], recv_sem=recv_sems.at[k],
                                              device_id=_chip_device(chip, 1 - c), device_id_type=MESH)
            cp.start()
            copies.append(cp)
        for k in range(n):
            copies[k].wait_send()
            pltpu.make_async_remote_copy(src_ref=_row_half(full_refs[k], 1 - c), dst_ref=_row_half(full_refs[k], 1 - c),
                                         send_sem=send_sems.at[k], recv_sem=recv_sems.at[k],
                                         device_id=_chip_device(chip, 1 - c), device_id_type=MESH).wait_recv()

    out = pl.pallas_call(
        body, name="join_halves", in_specs=[ANY] * n, out_specs=[ANY] * n,
        out_shape=[jax.ShapeDtypeStruct(shards[name].shape, F32) for name in names],
        input_output_aliases={k: k for k in range(n)},
        scratch_shapes=[pltpu.SemaphoreType.DMA((n,)), pltpu.SemaphoreType.DMA((n,))],
        compiler_params=pltpu.CompilerParams(has_side_effects=True),
    )(*[shards[name] for name in names])
    return dict(zip(names, out))


VEC_ROWS = 16


def _all_reduce_small(slabs):
    n = len(slabs)

    def body(*refs):
        in_refs, out_refs, got_refs = refs[:n], refs[n:2 * n], refs[2 * n:3 * n]
        send_sems, recv_sems = refs[3 * n:]
        x, y, c = lax.axis_index("x"), lax.axis_index("y"), lax.axis_index("c")
        me = 4 * x + 2 * y + c

        def remote(src, dst, k, phase, r):
            other = me ^ r
            return pltpu.make_async_remote_copy(src_ref=src, dst_ref=dst, send_sem=send_sems.at[k, phase, r],
                                                recv_sem=recv_sems.at[k, phase, r],
                                                device_id=(other // 4, (other // 2) % 2, other % 2), device_id_type=MESH)

        scatter = [remote(in_refs[k].at[me ^ r], got_refs[k].at[r], k, 0, r) for r in range(1, N_DEV) for k in range(n)]
        for cp in scatter:
            cp.start()
        for cp in scatter:
            cp.wait()
        for k in range(n):
            total = in_refs[k][me]
            for r in range(1, N_DEV):
                total = total + got_refs[k][r]
            out_refs[k][me] = total
        gather = [remote(out_refs[k].at[me], out_refs[k].at[me], k, 1, r) for r in range(1, N_DEV) for k in range(n)]
        for cp in gather:
            cp.start()
        for r in range(1, N_DEV):
            for k in range(n):
                remote(out_refs[k].at[me ^ r], out_refs[k].at[me ^ r], k, 1, r).wait_recv()
        for cp in gather:
            cp.wait_send()

    return pl.pallas_call(
        body, name="all_reduce_small", in_specs=[VMEM] * n, out_specs=[VMEM] * n,
        out_shape=[jax.ShapeDtypeStruct(s.shape, F32) for s in slabs],
        scratch_shapes=[pltpu.VMEM(s.shape, F32) for s in slabs]
        + [pltpu.SemaphoreType.DMA((n, 2, N_DEV)), pltpu.SemaphoreType.DMA((n, 2, N_DEV))],
        compiler_params=pltpu.CompilerParams(has_side_effects=True),
    )(*slabs)


def _cast_into_whole(w, name, place):
    rows, cols = w.shape
    tr = rows // 2

    def body(place_ref, w_ref, o_ref):
        o_ref[...] = w_ref[...].astype(BF16)

    if LARGE[name] == "col":
        window = pl.BlockSpec((tr, cols), lambda i, place_ref: (i, place_ref[0]))
    else:
        window = pl.BlockSpec((tr, cols), lambda i, place_ref: (2 * place_ref[0] + i, 0))
    return pl.pallas_call(
        body, name=name + "_cast",
        grid_spec=pltpu.PrefetchScalarGridSpec(num_scalar_prefetch=1, grid=(2,),
                                               in_specs=[pl.BlockSpec((tr, cols), lambda i, place_ref: (i, 0))], out_specs=window),
        out_shape=jax.ShapeDtypeStruct(LARGE_SHAPE[name], BF16),
        compiler_params=_params(dimension_semantics=("parallel",)))(place, w)


def _adamw_math(w, g, m, v):
    m = ADAM_B1 * m + (1.0 - ADAM_B1) * g
    v = ADAM_B2 * v + (1.0 - ADAM_B2) * (g * g)
    m_hat = m / (1.0 - ADAM_B1 ** ADAM_STEP)
    v_hat = v / (1.0 - ADAM_B2 ** ADAM_STEP)
    delta = -ADAM_LR * (m_hat / (jnp.sqrt(v_hat) + ADAM_EPS) + ADAM_WD * w)
    return delta, m, v


def _adamw_large(w, g, m, v, name):
    rows, cols = w.shape
    tr = rows // 4

    def body(w_ref, g_ref, m_ref, v_ref, d_ref, mo_ref, vo_ref):
        d_ref[...], mo_ref[...], vo_ref[...] = _adamw_math(w_ref[...], g_ref[...], m_ref[...], v_ref[...])

    blk = pl.BlockSpec((tr, cols), lambda i: (i, 0))
    out = jax.ShapeDtypeStruct(w.shape, F32)
    return pl.pallas_call(body, name=name + "_adamw", grid=(4,), in_specs=[blk] * 4, out_specs=[blk] * 3, out_shape=[out] * 3,
                          compiler_params=_params(dimension_semantics=("parallel",)))(w, g, m, v)


def _adamw_small(ws, gs, ms, vs):
    n = len(ws)

    def body(*refs):
        for k in range(n):
            w_ref, g_ref, m_ref, v_ref = (refs[q * n + k] for q in range(4))
            d_ref, mo_ref, vo_ref = (refs[(4 + q) * n + k] for q in range(3))
            d_ref[...], mo_ref[...], vo_ref[...] = _adamw_math(w_ref[...], g_ref[...], m_ref[...], v_ref[...])

    out = [jax.ShapeDtypeStruct(w.shape, F32) for w in ws]
    res = pl.pallas_call(body, name="small_adamw", in_specs=[VMEM] * (4 * n), out_specs=[VMEM] * (3 * n), out_shape=out * 3,
                         compiler_params=_params())(*ws, *gs, *ms, *vs)
    return res[:n], res[n:2 * n], res[2 * n:]


WEIGHTS = ["norm_mix", "w_in", "w_pool_grp", "pool_scale", "w_pool_out", "conv_w", "conv_b", "w_rg_a", "b_rg_a", "w_rg_x",
           "b_rg_x", "lru_lambda", "w_rnn_out", "w_o", "norm_ffn", "w_ffn_in", "w_ffn_out", "norm_final"]
VEC_ITEMS = ["norm_mix", "norm_ffn", "norm_final", "pool_scale", "conv_b", "lru_lambda", "b_rg_a", "b_rg_x"]
MAT_ITEMS = ["w_pool_grp", "w_rg_a", "w_rg_x"]


def _as2d(name, a):
    if name in MAT_ITEMS:
        return a.reshape(-1, HEAD, HEAD)
    if name == "conv_w":
        return a.reshape(CONV_WIDTH, -1)
    return a.reshape(1, -1)


def kernel(x, norm_mix, w_in, w_pool_grp, pool_scale, w_pool_out, conv_w, conv_b, w_rg_a, b_rg_a, w_rg_x, b_rg_x, lru_lambda, w_rnn_out, w_o, norm_ffn, w_ffn_in, w_ffn_out, norm_final, loss_target, m_norm_mix, m_w_in, m_w_pool_grp, m_pool_scale, m_w_pool_out, m_conv_w, m_conv_b, m_w_rg_a, m_b_rg_a, m_w_rg_x, m_b_rg_x, m_lru_lambda, m_w_rnn_out, m_w_o, m_norm_ffn, m_w_ffn_in, m_w_ffn_out, m_norm_final, v_norm_mix, v_w_in, v_w_pool_grp, v_pool_scale, v_w_pool_out, v_conv_w, v_conv_b, v_w_rg_a, v_b_rg_a, v_w_rg_x, v_b_rg_x, v_lru_lambda, v_w_rnn_out, v_w_o, v_norm_ffn, v_w_ffn_in, v_w_ffn_out, v_norm_final):
    given = dict(locals())
    w = {name: given[name] for name in WEIGHTS}
    m = {name: given["m_" + name] for name in WEIGHTS}
    v = {name: given["v_" + name] for name in WEIGHTS}
    chip, c = _place()

    place = jnp.stack([chip, c]).astype(jnp.int32)
    conv_cols = w["conv_w"].shape[-1]
    conv_w_mine = lax.dynamic_update_slice_in_dim(jnp.zeros((CONV_WIDTH, D_RNN), F32), w["conv_w"][0], chip * conv_cols, axis=1)
    full, conv_w_full = _gather_weights({name: _cast_into_whole(w[name][0], name, place) for name in LARGE}, conv_w_mine)
    small = {name: _as2d(name, w[name]) for name in WEIGHTS if name not in LARGE and name != "conv_w"}
    sq_cols, grad_x, g = _local_step(
        x[0], loss_target[0], small["norm_mix"], full["w_in"], small["w_pool_grp"], small["pool_scale"], full["w_pool_out"],
        conv_w_full, small["conv_b"], small["w_rg_a"], small["b_rg_a"], small["w_rg_x"], small["b_rg_x"], small["lru_lambda"],
        full["w_rnn_out"], full["w_o"], small["norm_ffn"], full["w_ffn_in"], full["w_ffn_out"], small["norm_final"])
    loss = lax.psum(0.5 / D_MODEL * jnp.sum(sq_cols), ("x", "y", "c"))

    c1 = jnp.reshape(c, (1,)).astype(jnp.int32)
    from_sibling = _exchange_sibling({name: g[name] for name in LARGE})
    chip_sums = {name: _chip_sum(name, g[name], from_sibling[name], c1) for name in LARGE}
    from_chips = _exchange_chips({name: chip_sums[name][1] for name in LARGE})
    grads = _join_halves({name: _final_sum(name, chip_sums[name][0], from_chips[name], place) for name in LARGE})

    vec_rows = [g[name] if name != "pool_scale" else jnp.pad(g[name], ((0, 0), (0, D_MODEL - D_POOL))) for name in VEC_ITEMS]
    vec = jnp.concatenate(vec_rows + [g["conv_w"], jnp.zeros((VEC_ROWS - len(VEC_ITEMS) - CONV_WIDTH, D_MODEL), F32)], axis=0)
    mat = jnp.concatenate([g[name].reshape(-1, HEAD) for name in MAT_ITEMS], axis=0)
    vec, mat = _all_reduce_small([vec.reshape(VEC_ROWS, N_DEV, HEAD).transpose(1, 0, 2), mat.reshape(N_DEV, -1, HEAD)])
    vec = vec.transpose(1, 0, 2).reshape(VEC_ROWS, D_MODEL)
    mat = mat.reshape(-1, HEAD)
    for k, name in enumerate(VEC_ITEMS):
        grads[name] = vec[k:k + 1, :w[name].size]
    grads["conv_w"] = lax.dynamic_slice_in_dim(vec[len(VEC_ITEMS):len(VEC_ITEMS) + CONV_WIDTH], chip * conv_cols, conv_cols, axis=1)
    row = 0
    for name in MAT_ITEMS:
        rows = w[name].size // HEAD
        grads[name] = mat[row:row + rows]
        row += rows

    delta, new_m, new_v = {}, {}, {}
    for name in LARGE:
        delta[name], new_m[name], new_v[name] = _adamw_large(w[name][0], grads[name], m[name][0], v[name][0], name)
    small_names = [name for name in WEIGHTS if name not in LARGE]
    flat = lambda d: [d[name].reshape(grads[name].shape) for name in small_names]
    ds, mo, vo = _adamw_small(flat(w), [grads[name] for name in small_names], flat(m), flat(v))
    for k, name in enumerate(small_names):
        delta[name], new_m[name], new_v[name] = ds[k], mo[k], vo[k]

    shaped = lambda d: [d[name].reshape(w[name].shape) for name in WEIGHTS]
    return (loss, grad_x[None], *shaped(grads), *shaped(delta), *shaped(new_m), *shaped(new_v))
```

```python
import functools
import math

import jax
import jax.numpy as jnp
from jax import lax
from jax.experimental import pallas as pl
from jax.experimental.pallas import tpu as pltpu

F32 = jnp.float32
BF16 = jnp.bfloat16

D_MODEL = 1024
D_POOL = 512
N_POOL_GROUPS = 4
D_RNN = 1024
N_RNN_HEADS = 8
HEAD = 128
CONV_WIDTH = 4
LRU_C = 8.0
D_FF = 2816
D_IN = D_POOL + 2 * D_RNN + 2 * D_MODEL
NORM_EPS = 1e-6
COL_RNN = D_POOL // HEAD
COL_GATE = (D_POOL + D_RNN) // HEAD

ADAM_LR = 0.001
ADAM_B1 = 0.9
ADAM_B2 = 0.999
ADAM_EPS = 1e-08
ADAM_WD = 0.01
ADAM_STEP = 10

N_CHIPS = 4
N_DEV = 8
MESH = pl.DeviceIdType.MESH
ANY = pl.BlockSpec(memory_space=pl.ANY)
VMEM = pl.BlockSpec(memory_space=pltpu.VMEM)
VMEM_LIMIT_BYTES = 60 * 1024 * 1024
SUBLANES = 8
POOL_HALO = 16
CHUNK = 512

GELU_C = math.sqrt(2.0 / math.pi)
GELU_A = 0.044715


def _params(**kw):
    return pltpu.CompilerParams(vmem_limit_bytes=VMEM_LIMIT_BYTES, **kw)


def _sigmoid(x):
    return 0.5 * jnp.tanh(0.5 * x) + 0.5


def _log1p(y):
    u = 1.0 + y
    d = u - 1.0
    return jnp.where(d == 0.0, y, jnp.log(u) * (y / jnp.where(d == 0.0, 1.0, d)))


def _gelu_parts(x):
    x2 = x * x
    th = jnp.tanh(GELU_C * (x + GELU_A * x * x2))
    g = 0.5 * x * (1.0 + th)
    dg = 0.5 * (1.0 + th) + 0.5 * x * (1.0 - th * th) * GELU_C * (1.0 + 3.0 * GELU_A * x2)
    return g, dg


def _dot(a, b):
    return jnp.dot(a, b, preferred_element_type=F32)


def _dot_nt(a, b):
    return lax.dot_general(a, b, (((1,), (1,)), ((), ())), preferred_element_type=F32)


def _dot_tn(a, b):
    return lax.dot_general(a, b, (((0,), (0,)), ((), ())), preferred_element_type=F32)


def _rms_scale(xv):
    return lax.rsqrt(jnp.mean(xv * xv, axis=-1, keepdims=True) + NORM_EPS)


def _rms_bwd(dy, xv, g):
    r = _rms_scale(xv)
    xh = xv * r
    dyg = dy * g
    dx = r * (dyg - xh * jnp.mean(dyg * xh, axis=-1, keepdims=True))
    return dx, dy * xh


def _norm_matmul(x, g, w, *, tm, tn, name):
    T, K = x.shape
    N = w.shape[1]

    def body(x_ref, g_ref, w_ref, o_ref, h_ref):
        @pl.when(pl.program_id(1) == 0)
        def _():
            xv = x_ref[...]
            h_ref[...] = (xv * _rms_scale(xv) * g_ref[...]).astype(BF16)

        o_ref[...] = _dot(h_ref[...], w_ref[...])

    return pl.pallas_call(
        body, name=name, grid=(T // tm, N // tn),
        in_specs=[pl.BlockSpec((tm, K), lambda i, j: (i, 0)), pl.BlockSpec((1, K), lambda i, j: (0, 0)),
                  pl.BlockSpec((K, tn), lambda i, j: (0, j))],
        out_specs=[pl.BlockSpec((tm, tn), lambda i, j: (i, j)), pl.BlockSpec((tm, K), lambda i, j: (i, 0))],
        out_shape=[jax.ShapeDtypeStruct((T, N), F32), jax.ShapeDtypeStruct((T, K), BF16)],
        compiler_params=_params(dimension_semantics=("parallel", "arbitrary")),
    )(x, g, w)


def _ffn_in(x2, g, w, *, tm, tn):
    T, K = x2.shape
    nb = D_FF // tn

    def body(x_ref, g_ref, wg_ref, wu_ref, gate_ref, up_ref, act_ref, h_ref):
        @pl.when(pl.program_id(1) == 0)
        def _():
            xv = x_ref[...]
            h_ref[...] = (xv * _rms_scale(xv) * g_ref[...]).astype(BF16)

        h = h_ref[...]
        gate = _dot(h, wg_ref[...])
        up = _dot(h, wu_ref[...])
        gate_ref[...] = gate
        up_ref[...] = up
        act_ref[...] = (gate * _sigmoid(gate) * up).astype(BF16)

    blk = pl.BlockSpec((tm, tn), lambda i, j: (i, j))
    return pl.pallas_call(
        body, name="ffn_in", grid=(T // tm, nb),
        in_specs=[pl.BlockSpec((tm, K), lambda i, j: (i, 0)), pl.BlockSpec((1, K), lambda i, j: (0, 0)),
                  pl.BlockSpec((K, tn), lambda i, j: (0, j)), pl.BlockSpec((K, tn), lambda i, j: (0, j + nb))],
        out_specs=[blk, blk, blk, pl.BlockSpec((tm, K), lambda i, j: (i, 0))],
        out_shape=[jax.ShapeDtypeStruct((T, D_FF), F32), jax.ShapeDtypeStruct((T, D_FF), F32),
                   jax.ShapeDtypeStruct((T, D_FF), BF16), jax.ShapeDtypeStruct((T, K), BF16)],
        compiler_params=_params(dimension_semantics=("parallel", "arbitrary")),
    )(x2, g, w, w)


def _branch_mix(pm, z, w_pool_out, w_rnn_out, proj, *, tm, tn):
    T = pm.shape[0]
    col_gp = (D_POOL + 2 * D_RNN) // tn
    col_gr = col_gp + D_MODEL // tn

    def body(pm_ref, z_ref, wp_ref, wr_ref, gp_ref, gr_ref, yp_ref, yr_ref, mix_ref):
        yp = _dot(pm_ref[...], wp_ref[...])
        yr = _dot(z_ref[...], wr_ref[...])
        yp_ref[...] = yp
        yr_ref[...] = yr
        mix_ref[...] = (_sigmoid(gp_ref[...]) * yp + _sigmoid(gr_ref[...]) * yr).astype(BF16)

    blk = pl.BlockSpec((tm, tn), lambda i, j: (i, j))
    return pl.pallas_call(
        body, name="branch_mix", grid=(T // tm, D_MODEL // tn),
        in_specs=[pl.BlockSpec((tm, D_POOL), lambda i, j: (i, 0)), pl.BlockSpec((tm, D_RNN), lambda i, j: (i, 0)),
                  pl.BlockSpec((D_POOL, tn), lambda i, j: (0, j)), pl.BlockSpec((D_RNN, tn), lambda i, j: (0, j)),
                  pl.BlockSpec((tm, tn), lambda i, j: (i, col_gp + j)), pl.BlockSpec((tm, tn), lambda i, j: (i, col_gr + j))],
        out_specs=[blk, blk, blk],
        out_shape=[jax.ShapeDtypeStruct((T, D_MODEL), F32), jax.ShapeDtypeStruct((T, D_MODEL), F32),
                   jax.ShapeDtypeStruct((T, D_MODEL), BF16)],
        compiler_params=_params(dimension_semantics=("parallel", "parallel")),
    )(pm, z, w_pool_out, w_rnn_out, proj, proj)


def _out_proj_residual(mix, w_o, x, *, tm):
    T = x.shape[0]

    def body(mix_ref, w_ref, x_ref, o_ref):
        o_ref[...] = x_ref[...] + _dot(mix_ref[...], w_ref[...])

    row = pl.BlockSpec((tm, D_MODEL), lambda i: (i, 0))
    return pl.pallas_call(
        body, name="out_proj_residual", grid=(T // tm,),
        in_specs=[row, pl.BlockSpec((D_MODEL, D_MODEL), lambda i: (0, 0)), row],
        out_specs=row, out_shape=jax.ShapeDtypeStruct((T, D_MODEL), F32),
        compiler_params=_params(dimension_semantics=("parallel",)),
    )(mix, w_o, x)


def _ffn_out_loss(act, w, x2, g3, target, *, tm):
    T = x2.shape[0]

    def body(act_ref, w_ref, x2_ref, g_ref, t_ref, dx_ref, dxb_ref, sq_ref, dg_ref):
        @pl.when(pl.program_id(0) == 0)
        def _():
            sq_ref[...] = jnp.zeros_like(sq_ref)
            dg_ref[...] = jnp.zeros_like(dg_ref)

        x3 = x2_ref[...] + _dot(act_ref[...], w_ref[...])
        g = g_ref[...]
        err = x3 * _rms_scale(x3) * g - t_ref[...]
        sq_ref[...] += jnp.sum(err * err, axis=0, keepdims=True)
        dx, dgp = _rms_bwd(err * (1.0 / D_MODEL), x3, g)
        dg_ref[...] += jnp.sum(dgp, axis=0, keepdims=True)
        dx_ref[...] = dx
        dxb_ref[...] = dx.astype(BF16)

    row = pl.BlockSpec((tm, D_MODEL), lambda i: (i, 0))
    vec = pl.BlockSpec((1, D_MODEL), lambda i: (0, 0))
    return pl.pallas_call(
        body, name="ffn_out_loss", grid=(T // tm,),
        in_specs=[pl.BlockSpec((tm, D_FF), lambda i: (i, 0)), pl.BlockSpec((D_FF, D_MODEL), lambda i: (0, 0)), row, vec, row],
        out_specs=[row, row, vec, vec],
        out_shape=[jax.ShapeDtypeStruct((T, D_MODEL), F32), jax.ShapeDtypeStruct((T, D_MODEL), BF16),
                   jax.ShapeDtypeStruct((1, D_MODEL), F32), jax.ShapeDtypeStruct((1, D_MODEL), F32)],
        compiler_params=_params(dimension_semantics=("arbitrary",)),
    )(act, w, x2, g3, target)


def _ffn_out_bwd(dx3b, w, gate, up, *, tm, tn):
    T = dx3b.shape[0]

    def body(dx_ref, w_ref, gate_ref, up_ref, dgate_ref, dup_ref):
        dact = _dot_nt(dx_ref[...], w_ref[...])
        gate = gate_ref[...]
        s = _sigmoid(gate)
        dgate_ref[...] = (dact * up_ref[...] * s * (1.0 + gate * (1.0 - s))).astype(BF16)
        dup_ref[...] = (dact * gate * s).astype(BF16)

    blk = pl.BlockSpec((tm, tn), lambda i, j: (i, j))
    return pl.pallas_call(
        body, name="ffn_out_bwd", grid=(T // tm, D_FF // tn),
        in_specs=[pl.BlockSpec((tm, D_MODEL), lambda i, j: (i, 0)), pl.BlockSpec((tn, D_MODEL), lambda i, j: (j, 0)), blk, blk],
        out_specs=[blk, blk],
        out_shape=[jax.ShapeDtypeStruct((T, D_FF), BF16), jax.ShapeDtypeStruct((T, D_FF), BF16)],
        compiler_params=_params(dimension_semantics=("parallel", "parallel")),
    )(dx3b, w, gate, up)


def _ffn_in_bwd(dgate, dup, w, dx3, x2, g2, *, tm):
    T = x2.shape[0]

    def body(dgate_ref, dup_ref, w_ref, dx3_ref, x2_ref, g_ref, dx_ref, dxb_ref, dg_ref):
        @pl.when(pl.program_id(0) == 0)
        def _():
            dg_ref[...] = jnp.zeros_like(dg_ref)

        dh = _dot_nt(dgate_ref[...], w_ref[:, :D_FF]) + _dot_nt(dup_ref[...], w_ref[:, D_FF:])
        dxn, dgp = _rms_bwd(dh, x2_ref[...], g_ref[...])
        dx = dx3_ref[...] + dxn
        dg_ref[...] += jnp.sum(dgp, axis=0, keepdims=True)
        dx_ref[...] = dx
        dxb_ref[...] = dx.astype(BF16)

    row = pl.BlockSpec((tm, D_MODEL), lambda i: (i, 0))
    wide = pl.BlockSpec((tm, D_FF), lambda i: (i, 0))
    vec = pl.BlockSpec((1, D_MODEL), lambda i: (0, 0))
    return pl.pallas_call(
        body, name="ffn_in_bwd", grid=(T // tm,),
        in_specs=[wide, wide, pl.BlockSpec((D_MODEL, 2 * D_FF), lambda i: (0, 0)), row, row, vec],
        out_specs=[row, row, vec],
        out_shape=[jax.ShapeDtypeStruct((T, D_MODEL), F32), jax.ShapeDtypeStruct((T, D_MODEL), BF16),
                   jax.ShapeDtypeStruct((1, D_MODEL), F32)],
        compiler_params=_params(dimension_semantics=("arbitrary",)),
    )(dgate, dup, w, dx3, x2, g2)


def _out_proj_bwd(dx2b, w_o, proj, y_pool, y_rnn, *, tm, tn):
    T = dx2b.shape[0]
    col_gp = (D_POOL + 2 * D_RNN) // tn
    col_gr = col_gp + D_MODEL // tn

    def body(dx_ref, w_ref, gp_ref, gr_ref, yp_ref, yr_ref, dgp_ref, dgr_ref, dyp_ref, dyr_ref):
        dmix = _dot_nt(dx_ref[...], w_ref[...])
        sp = _sigmoid(gp_ref[...])
        sr = _sigmoid(gr_ref[...])
        dgp_ref[...] = (dmix * yp_ref[...] * sp * (1.0 - sp)).astype(BF16)
        dgr_ref[...] = (dmix * yr_ref[...] * sr * (1.0 - sr)).astype(BF16)
        dyp_ref[...] = (dmix * sp).astype(BF16)
        dyr_ref[...] = (dmix * sr).astype(BF16)

    blk = pl.BlockSpec((tm, tn), lambda i, j: (i, j))
    out = jax.ShapeDtypeStruct((T, D_MODEL), BF16)
    return pl.pallas_call(
        body, name="out_proj_bwd", grid=(T // tm, D_MODEL // tn),
        in_specs=[pl.BlockSpec((tm, D_MODEL), lambda i, j: (i, 0)), pl.BlockSpec((tn, D_MODEL), lambda i, j: (j, 0)),
                  pl.BlockSpec((tm, tn), lambda i, j: (i, col_gp + j)), pl.BlockSpec((tm, tn), lambda i, j: (i, col_gr + j)), blk, blk],
        out_specs=[blk, blk, blk, blk], out_shape=[out, out, out, out],
        compiler_params=_params(dimension_semantics=("parallel", "parallel")),
    )(dx2b, w_o, proj, proj, y_pool, y_rnn)


def _branch_bwd(dyp, dyr, w_pool_out, w_rnn_out, *, tm):
    T = dyp.shape[0]

    def body(dyp_ref, dyr_ref, wp_ref, wr_ref, dpm_ref, dz_ref):
        dpm_ref[...] = _dot_nt(dyp_ref[...], wp_ref[...])
        dz_ref[...] = _dot_nt(dyr_ref[...], wr_ref[...])

    row = pl.BlockSpec((tm, D_MODEL), lambda i: (i, 0))
    return pl.pallas_call(
        body, name="branch_bwd", grid=(T // tm,),
        in_specs=[row, row, pl.BlockSpec((D_POOL, D_MODEL), lambda i: (0, 0)), pl.BlockSpec((D_RNN, D_MODEL), lambda i: (0, 0))],
        out_specs=[pl.BlockSpec((tm, D_POOL), lambda i: (i, 0)), pl.BlockSpec((tm, D_RNN), lambda i: (i, 0))],
        out_shape=[jax.ShapeDtypeStruct((T, D_POOL), F32), jax.ShapeDtypeStruct((T, D_RNN), F32)],
        compiler_params=_params(dimension_semantics=("parallel",)),
    )(dyp, dyr, w_pool_out, w_rnn_out)


def _in_proj_bwd(segs, w, dx2, x, g1, *, tm):
    T = x.shape[0]
    widths = [s.shape[1] for s in segs]
    offs = [sum(widths[:k]) for k in range(len(widths))]
    n = len(segs)

    def body(*refs):
        seg_refs, (w_ref, dx2_ref, x_ref, g_ref, dx_ref, dg_ref) = refs[:n], refs[n:]

        @pl.when(pl.program_id(0) == 0)
        def _():
            dg_ref[...] = jnp.zeros_like(dg_ref)

        dh = _dot_nt(seg_refs[0][...], w_ref[:, offs[0]:offs[0] + widths[0]])
        for k in range(1, n):
            dh += _dot_nt(seg_refs[k][...], w_ref[:, offs[k]:offs[k] + widths[k]])
        dxn, dgp = _rms_bwd(dh, x_ref[...], g_ref[...])
        dg_ref[...] += jnp.sum(dgp, axis=0, keepdims=True)
        dx_ref[...] = dx2_ref[...] + dxn

    row = pl.BlockSpec((tm, D_MODEL), lambda i: (i, 0))
    vec = pl.BlockSpec((1, D_MODEL), lambda i: (0, 0))
    return pl.pallas_call(
        body, name="in_proj_bwd", grid=(T // tm,),
        in_specs=[pl.BlockSpec((tm, wd), lambda i: (i, 0)) for wd in widths]
        + [pl.BlockSpec((D_MODEL, D_IN), lambda i: (0, 0)), row, row, vec],
        out_specs=[row, vec],
        out_shape=[jax.ShapeDtypeStruct((T, D_MODEL), F32), jax.ShapeDtypeStruct((1, D_MODEL), F32)],
        compiler_params=_params(dimension_semantics=("arbitrary",)),
    )(*segs, w, dx2, x, g1)


def _weight_grad(a, segs, *, tm, tn, name):
    T, M = a.shape
    nblk = [s.shape[1] // tn for s in segs]
    first = [sum(nblk[:k]) for k in range(len(segs))]
    n = len(segs)

    def body(a_ref, *refs):
        seg_refs, o_ref = refs[:n], refs[n]
        j = pl.program_id(1)
        for k in range(n):
            @pl.when((j >= first[k]) & (j < first[k] + nblk[k]))
            def _(k=k):
                o_ref[...] = _dot_tn(a_ref[...], seg_refs[k][...])

    def seg_spec(k):
        return pl.BlockSpec((T, tn), lambda i, j: (0, jnp.clip(j - first[k], 0, nblk[k] - 1)))

    return pl.pallas_call(
        body, name=name, grid=(M // tm, sum(nblk)),
        in_specs=[pl.BlockSpec((T, tm), lambda i, j: (0, i))] + [seg_spec(k) for k in range(n)],
        out_specs=pl.BlockSpec((tm, tn), lambda i, j: (i, j)),
        out_shape=jax.ShapeDtypeStruct((M, sum(nblk) * tn), F32),
        compiler_params=_params(dimension_semantics=("parallel", "arbitrary")),
    )(a, *segs)


def _pad_front(dst, src, halo):
    dst[pl.ds(0, halo), :] = jnp.zeros((halo, src.shape[1]), F32)

    def fill(i, carry):
        r0 = pl.multiple_of(i * CHUNK, CHUNK)
        dst[pl.ds(r0 + halo, CHUNK), :] = src[pl.ds(r0, CHUNK), :]
        return carry

    lax.fori_loop(0, src.shape[0] // CHUNK, fill, 0)


def _shift_rows(v, k):
    return pltpu.roll(v, k % v.shape[0], axis=0)


def _window_sums(xs, direction):
    s2 = xs + _shift_rows(xs, direction)
    s4 = s2 + _shift_rows(s2, 2 * direction)
    s8 = s4 + _shift_rows(s4, 4 * direction)
    s16 = s8 + _shift_rows(s8, 8 * direction)
    return s2, s4, s8, s16


def _select_window(g, sums):
    s2, s4, s8, s16 = sums
    return jnp.where(g == 0, s2, jnp.where(g == 1, s4, jnp.where(g == 2, s8, s16)))


def _pool_count(g, start, rows):
    t = start + lax.broadcasted_iota(jnp.int32, (rows, 1), 0)
    return jnp.minimum(t + 1, jnp.left_shift(2, g)).astype(F32)


def _pool_fwd(proj, w_grp, scale):
    T = proj.shape[0]
    nchunk = T // CHUNK

    def body(u_ref, w_ref, s_ref, o_ref, upad):
        g = pl.program_id(0)
        _pad_front(upad, u_ref, POOL_HALO)
        w = w_ref[...].astype(BF16)
        scale_row = s_ref[...]

        def chunk(i, carry):
            r0 = pl.multiple_of(i * CHUNK, CHUNK)
            xs = upad[pl.ds(r0, CHUNK + POOL_HALO), :]
            win = _select_window(g, _window_sums(xs, 1))[POOL_HALO:]
            pooled = win / _pool_count(g, r0, CHUNK) - xs[POOL_HALO:]
            o_ref[pl.ds(r0, CHUNK), :] = (_dot(pooled.astype(BF16), w) * scale_row).astype(BF16)
            return carry

        lax.fori_loop(0, nchunk, chunk, 0)

    return pl.pallas_call(
        body, name="pool_fwd", grid=(N_POOL_GROUPS,),
        in_specs=[pl.BlockSpec((T, HEAD), lambda g: (0, g)), pl.BlockSpec((None, HEAD, HEAD), lambda g: (g, 0, 0)),
                  pl.BlockSpec((1, HEAD), lambda g: (0, g))],
        out_specs=pl.BlockSpec((T, HEAD), lambda g: (0, g)),
        out_shape=jax.ShapeDtypeStruct((T, D_POOL), BF16),
        scratch_shapes=[pltpu.VMEM((T + POOL_HALO, HEAD), F32)],
        compiler_params=_params(dimension_semantics=("parallel",)),
    )(proj, w_grp, scale)


def _pool_bwd(proj, dpm, w_grp, scale):
    T = proj.shape[0]
    nchunk = T // CHUNK

    def body(u_ref, dpm_ref, w_ref, s_ref, du_ref, dw_ref, ds_ref, upad, zpad, dpool):
        g = pl.program_id(0)
        _pad_front(upad, u_ref, POOL_HALO)
        zpad[pl.ds(T, POOL_HALO), :] = jnp.zeros((POOL_HALO, HEAD), F32)
        dw_ref[...] = jnp.zeros_like(dw_ref)
        ds_ref[...] = jnp.zeros_like(ds_ref)
        w = w_ref[...].astype(BF16)
        scale_row = s_ref[...]

        def chunk(i, carry):
            r0 = pl.multiple_of(i * CHUNK, CHUNK)
            xs = upad[pl.ds(r0, CHUNK + POOL_HALO), :]
            cnt = _pool_count(g, r0, CHUNK)
            pooled = (_select_window(g, _window_sums(xs, 1))[POOL_HALO:] / cnt - xs[POOL_HALO:]).astype(BF16)
            mixed = _dot(pooled, w)
            d = dpm_ref[pl.ds(r0, CHUNK), :]
            ds_ref[...] += jnp.sum(d * mixed, axis=0, keepdims=True)
            dmixed = (d * scale_row).astype(BF16)
            dw_ref[...] += _dot_tn(pooled, dmixed)
            dp = _dot_nt(dmixed, w)
            dpool[pl.ds(r0, CHUNK), :] = dp
            zpad[pl.ds(r0, CHUNK), :] = dp / cnt
            return carry

        lax.fori_loop(0, nchunk, chunk, 0)

        def chunk2(i, carry):
            r0 = pl.multiple_of(i * CHUNK, CHUNK)
            zs = zpad[pl.ds(r0, CHUNK + POOL_HALO), :]
            win = _select_window(g, _window_sums(zs, -1))[:CHUNK]
            du_ref[pl.ds(r0, CHUNK), :] = (win - dpool[pl.ds(r0, CHUNK), :]).astype(BF16)
            return carry

        lax.fori_loop(0, nchunk, chunk2, 0)

    col = pl.BlockSpec((T, HEAD), lambda g: (0, g))
    return pl.pallas_call(
        body, name="pool_bwd", grid=(N_POOL_GROUPS,),
        in_specs=[col, col, pl.BlockSpec((None, HEAD, HEAD), lambda g: (g, 0, 0)), pl.BlockSpec((1, HEAD), lambda g: (0, g))],
        out_specs=[col, pl.BlockSpec((None, HEAD, HEAD), lambda g: (g, 0, 0)), pl.BlockSpec((1, HEAD), lambda g: (0, g))],
        out_shape=[jax.ShapeDtypeStruct((T, D_POOL), BF16), jax.ShapeDtypeStruct((N_POOL_GROUPS, HEAD, HEAD), F32),
                   jax.ShapeDtypeStruct((1, D_POOL), F32)],
        scratch_shapes=[pltpu.VMEM((T + POOL_HALO, HEAD), F32), pltpu.VMEM((T + POOL_HALO, HEAD), F32), pltpu.VMEM((T, HEAD), F32)],
        compiler_params=_params(dimension_semantics=("parallel",)),
    )(proj, dpm, w_grp, scale)


def _conv_taps(xs, cw):
    v = cw[CONV_WIDTH - 1] * xs[SUBLANES:]
    for k in range(CONV_WIDTH - 1):
        v += cw[k] * _shift_rows(xs, CONV_WIDTH - 1 - k)[SUBLANES:]
    return v


def _tap_rows(cw_ref):
    return [cw_ref[k:k + 1, :] for k in range(CONV_WIDTH)]


def _softplus_neg(lam):
    return jnp.maximum(-lam, 0.0) + _log1p(jnp.exp(-jnp.abs(lam)))


def _lru_gates(v, wa, ba, wx, bx, sp):
    vb = v.astype(BF16)
    ra = _sigmoid(_dot(vb, wa) + ba)
    ix = _sigmoid(_dot(vb, wx) + bx)
    log_a = -LRU_C * ra * sp
    a = jnp.exp(log_a)
    sq = jnp.sqrt(-jnp.tanh(log_a) * (a * a + 1.0))
    return ra, ix, a, sq


def _row_bcast(v, r):
    return jnp.broadcast_to(v[r:r + 1, :], v.shape)


def _tile_scans(a, b, direction):
    ri = lax.broadcasted_iota(jnp.int32, a.shape, 0) % SUBLANES
    A, B = a, b
    for s in (1, 2, 4):
        ok = (ri >= s) if direction == 1 else (ri + s < SUBLANES)
        As, Bs = _shift_rows(A, s * direction), _shift_rows(B, s * direction)
        B = jnp.where(ok, A * Bs + B, B)
        A = jnp.where(ok, A * As, A)
    return A, B


TILES_PER_STEP = 8


def _carry_tiles(A_s, B_s, out, ntile, direction):
    out_row = SUBLANES - 1 if direction == 1 else 0

    def step(k, carry):
        for j in range(TILES_PER_STEP):
            t = k * TILES_PER_STEP + j
            r0 = pl.multiple_of((t if direction == 1 else ntile - 1 - t) * SUBLANES, SUBLANES)
            A, B = A_s[pl.ds(r0, SUBLANES), :], B_s[pl.ds(r0, SUBLANES), :]
            out[pl.ds(r0, SUBLANES), :] = A * carry + B
            carry = _row_bcast(A, out_row) * carry + _row_bcast(B, out_row)
        return carry

    lax.fori_loop(0, ntile // TILES_PER_STEP, step, jnp.zeros((SUBLANES, HEAD), F32))


def _rnn_fwd(proj, conv_w, conv_b, w_a, b_a, w_x, b_x, lam):
    T = proj.shape[0]
    nchunk = T // CHUNK
    ntile = T // SUBLANES

    def body(u_ref, ug_ref, cw_ref, cb_ref, wa_ref, ba_ref, wx_ref, bx_ref, lam_ref, h_ref, z_ref, upad, a_s, b_s):
        _pad_front(upad, u_ref, SUBLANES)
        cw, cb = _tap_rows(cw_ref), cb_ref[...]
        wa, wx = wa_ref[...].astype(BF16), wx_ref[...].astype(BF16)
        ba, bx = ba_ref[...], bx_ref[...]
        sp = _softplus_neg(lam_ref[...])

        def chunk(i, carry):
            r0 = pl.multiple_of(i * CHUNK, CHUNK)
            v = _conv_taps(upad[pl.ds(r0, CHUNK + SUBLANES), :], cw) + cb
            _, ix, a, sq = _lru_gates(v, wa, ba, wx, bx, sp)
            a_s[pl.ds(r0, CHUNK), :], b_s[pl.ds(r0, CHUNK), :] = _tile_scans(a, sq * ix * v, 1)
            return carry

        lax.fori_loop(0, nchunk, chunk, 0)
        _carry_tiles(a_s, b_s, h_ref, ntile, 1)

        def chunk3(i, carry):
            r0 = pl.multiple_of(i * CHUNK, CHUNK)
            gl, _ = _gelu_parts(ug_ref[pl.ds(r0, CHUNK), :])
            z_ref[pl.ds(r0, CHUNK), :] = (h_ref[pl.ds(r0, CHUNK), :] * gl).astype(BF16)
            return carry

        lax.fori_loop(0, nchunk, chunk3, 0)

    col = pl.BlockSpec((T, HEAD), lambda h: (0, h))
    vec = pl.BlockSpec((1, HEAD), lambda h: (0, h))
    mat = pl.BlockSpec((None, HEAD, HEAD), lambda h: (h, 0, 0))
    return pl.pallas_call(
        body, name="rnn_fwd", grid=(N_RNN_HEADS,),
        in_specs=[pl.BlockSpec((T, HEAD), lambda h: (0, COL_RNN + h)), pl.BlockSpec((T, HEAD), lambda h: (0, COL_GATE + h)),
                  pl.BlockSpec((CONV_WIDTH, HEAD), lambda h: (0, h)), vec, mat, vec, mat, vec, vec],
        out_specs=[col, col],
        out_shape=[jax.ShapeDtypeStruct((T, D_RNN), F32), jax.ShapeDtypeStruct((T, D_RNN), BF16)],
        scratch_shapes=[pltpu.VMEM((T + SUBLANES, HEAD), F32), pltpu.VMEM((T, HEAD), F32), pltpu.VMEM((T, HEAD), F32)],
        compiler_params=_params(dimension_semantics=("parallel",)),
    )(proj, proj, conv_w, conv_b, w_a, b_a, w_x, b_x, lam)


def _rnn_bwd(proj, hr, dz, conv_w, conv_b, w_a, b_a, w_x, b_x, lam):
    T = proj.shape[0]
    nchunk = T // CHUNK
    ntile = T // SUBLANES

    def body(u_ref, ug_ref, h_ref, dz_ref, cw_ref, cb_ref, wa_ref, ba_ref, wx_ref, bx_ref, lam_ref,
             du_ref, dug_ref, dwa_ref, dwx_ref, dba_ref, dbx_ref, dlam_ref, dcb_ref, dcw_ref,
             upad, hpad, apad, v_s, ra_s, ix_s, sq_s, g_s, dvpad, ga_s):
        zero_tile = jnp.zeros((SUBLANES, HEAD), F32)
        _pad_front(upad, u_ref, SUBLANES)
        _pad_front(hpad, h_ref, SUBLANES)
        apad[pl.ds(T, SUBLANES), :] = zero_tile
        dvpad[pl.ds(T, SUBLANES), :] = zero_tile
        for ref in (dwa_ref, dwx_ref, dba_ref, dbx_ref, dlam_ref, dcb_ref, dcw_ref):
            ref[...] = jnp.zeros_like(ref)
        cw, cb = _tap_rows(cw_ref), cb_ref[...]
        wa, wx = wa_ref[...].astype(BF16), wx_ref[...].astype(BF16)
        ba, bx = ba_ref[...], bx_ref[...]
        lam_row = lam_ref[...]
        sp = _softplus_neg(lam_row)

        def chunk(i, carry):
            r0 = pl.multiple_of(i * CHUNK, CHUNK)
            rows = pl.ds(r0, CHUNK)
            h = h_ref[rows, :]
            v = _conv_taps(upad[pl.ds(r0, CHUNK + SUBLANES), :], cw) + cb
            ra, ix, a, sq = _lru_gates(v, wa, ba, wx, bx, sp)
            v_s[rows, :], ra_s[rows, :], ix_s[rows, :], sq_s[rows, :], apad[rows, :] = v, ra, ix, sq, a
            gl, dgl = _gelu_parts(ug_ref[rows, :])
            d = dz_ref[rows, :]
            g_s[rows, :] = d * gl
            dug_ref[rows, :] = (d * h * dgl).astype(BF16)
            return carry

        lax.fori_loop(0, nchunk, chunk, 0)

        def chunk2(i, carry):
            r0 = pl.multiple_of(i * CHUNK, CHUNK)
            rows = pl.ds(r0, CHUNK)
            a_next = _shift_rows(apad[pl.ds(r0, CHUNK + SUBLANES), :], -1)[:CHUNK]
            ga_s[rows, :], g_s[rows, :] = _tile_scans(a_next, g_s[rows, :], -1)
            return carry

        lax.fori_loop(0, nchunk, chunk2, 0)
        _carry_tiles(ga_s, g_s, g_s, ntile, -1)

        def chunk3(i, carry):
            r0 = pl.multiple_of(i * CHUNK, CHUNK)
            rows = pl.ds(r0, CHUNK)
            g = g_s[rows, :]
            h_prev = _shift_rows(hpad[pl.ds(r0, CHUNK + SUBLANES), :], 1)[SUBLANES:]
            v, ra, ix, sq, a = v_s[rows, :], ra_s[rows, :], ix_s[rows, :], sq_s[rows, :], apad[rows, :]
            d_sq = g * ix * v
            d_ix = g * sq * v
            d_la = a * g * h_prev - d_sq * a * a / sq
            dlam_ref[...] += jnp.sum(d_la * ra, axis=0, keepdims=True)
            d_pa = d_la * (-LRU_C) * sp * ra * (1.0 - ra)
            d_px = d_ix * ix * (1.0 - ix)
            vb, d_pab, d_pxb = v.astype(BF16), d_pa.astype(BF16), d_px.astype(BF16)
            dwa_ref[...] += _dot_tn(vb, d_pab)
            dwx_ref[...] += _dot_tn(vb, d_pxb)
            dba_ref[...] += jnp.sum(d_pa, axis=0, keepdims=True)
            dbx_ref[...] += jnp.sum(d_px, axis=0, keepdims=True)
            dv = g * sq * ix + _dot_nt(d_pab, wa) + _dot_nt(d_pxb, wx)
            dvpad[rows, :] = dv
            dcb_ref[...] += jnp.sum(dv, axis=0, keepdims=True)
            xs = upad[pl.ds(r0, CHUNK + SUBLANES), :]
            for k in range(CONV_WIDTH):
                u_k = _shift_rows(xs, CONV_WIDTH - 1 - k)[SUBLANES:] if k < CONV_WIDTH - 1 else xs[SUBLANES:]
                dcw_ref[k:k + 1, :] += jnp.sum(dv * u_k, axis=0, keepdims=True)
            return carry

        lax.fori_loop(0, nchunk, chunk3, 0)
        dlam_ref[...] = dlam_ref[...] * (LRU_C * _sigmoid(-lam_row))

        def chunk4(i, carry):
            r0 = pl.multiple_of(i * CHUNK, CHUNK)
            dvs = dvpad[pl.ds(r0, CHUNK + SUBLANES), :]
            du = cw[CONV_WIDTH - 1] * dvs[:CHUNK]
            for k in range(CONV_WIDTH - 1):
                du += cw[k] * _shift_rows(dvs, -(CONV_WIDTH - 1 - k))[:CHUNK]
            du_ref[pl.ds(r0, CHUNK), :] = du.astype(BF16)
            return carry

        lax.fori_loop(0, nchunk, chunk4, 0)

    col = pl.BlockSpec((T, HEAD), lambda h: (0, h))
    vec = pl.BlockSpec((1, HEAD), lambda h: (0, h))
    mat = pl.BlockSpec((None, HEAD, HEAD), lambda h: (h, 0, 0))
    taps = pl.BlockSpec((CONV_WIDTH, HEAD), lambda h: (0, h))
    vec_out = jax.ShapeDtypeStruct((1, D_RNN), F32)
    mat_out = jax.ShapeDtypeStruct((N_RNN_HEADS, HEAD, HEAD), F32)
    seq = pltpu.VMEM((T, HEAD), F32)
    seq_pad = pltpu.VMEM((T + SUBLANES, HEAD), F32)
    return pl.pallas_call(
        body, name="rnn_bwd", grid=(N_RNN_HEADS,),
        in_specs=[pl.BlockSpec((T, HEAD), lambda h: (0, COL_RNN + h)), pl.BlockSpec((T, HEAD), lambda h: (0, COL_GATE + h)),
                  col, col, taps, vec, mat, vec, mat, vec, vec],
        out_specs=[col, col, mat, mat, vec, vec, vec, vec, taps],
        out_shape=[jax.ShapeDtypeStruct((T, D_RNN), BF16), jax.ShapeDtypeStruct((T, D_RNN), BF16), mat_out, mat_out,
                   vec_out, vec_out, vec_out, vec_out, jax.ShapeDtypeStruct((CONV_WIDTH, D_RNN), F32)],
        scratch_shapes=[seq_pad, seq_pad, seq_pad, seq, seq, seq, seq, seq, seq_pad, seq],
        compiler_params=_params(dimension_semantics=("parallel",)),
    )(proj, proj, hr, dz, conv_w, conv_b, w_a, b_a, w_x, b_x, lam)


def _local_step(x, target, norm_mix, w_in, w_pool_grp, pool_scale, w_pool_out, conv_w, conv_b, w_rg_a, b_rg_a,
                w_rg_x, b_rg_x, lam, w_rnn_out, w_o, norm_ffn, w_ffn_in, w_ffn_out, norm_final):
    T = x.shape[0]
    tm = min(T, 512)
    proj, h1 = _norm_matmul(x, norm_mix, w_in, tm=tm, tn=1536, name="in_proj")
    pm = _pool_fwd(proj, w_pool_grp, pool_scale)
    hr, z = _rnn_fwd(proj, conv_w, conv_b, w_rg_a, b_rg_a, w_rg_x, b_rg_x, lam)
    y_pool, y_rnn, mix = _branch_mix(pm, z, w_pool_out, w_rnn_out, proj, tm=tm, tn=512)
    x2 = _out_proj_residual(mix, w_o, x, tm=tm)
    gate, up, act, h2 = _ffn_in(x2, norm_ffn, w_ffn_in, tm=tm, tn=1408)
    tr = min(T, 256)
    dx3, dx3b, sq_cols, g_norm_final = _ffn_out_loss(act, w_ffn_out, x2, norm_final, target, tm=tr)

    g = {"norm_final": g_norm_final}
    dgate, dup = _ffn_out_bwd(dx3b, w_ffn_out, gate, up, tm=tm, tn=1408)
    g["w_ffn_out"] = _weight_grad(act, [dx3b], tm=256, tn=D_MODEL, name="w_ffn_out_grad")
    dx2, dx2b, g["norm_ffn"] = _ffn_in_bwd(dgate, dup, w_ffn_in, dx3, x2, norm_ffn, tm=tr)
    g["w_ffn_in"] = _weight_grad(h2, [dgate, dup], tm=D_MODEL, tn=256, name="w_ffn_in_grad")
    dgp, dgr, dyp, dyr = _out_proj_bwd(dx2b, w_o, proj, y_pool, y_rnn, tm=tm, tn=512)
    g["w_o"] = _weight_grad(mix, [dx2b], tm=D_MODEL, tn=256, name="w_o_grad")
    dpm, dz = _branch_bwd(dyp, dyr, w_pool_out, w_rnn_out, tm=tm)
    g["w_pool_out"] = _weight_grad(pm, [dyp], tm=D_POOL, tn=256, name="w_pool_out_grad")
    g["w_rnn_out"] = _weight_grad(z, [dyr], tm=D_RNN, tn=256, name="w_rnn_out_grad")
    dupool, g["w_pool_grp"], g["pool_scale"] = _pool_bwd(proj, dpm, w_pool_grp, pool_scale)
    (durnn, dugate, g["w_rg_a"], g["w_rg_x"], g["b_rg_a"], g["b_rg_x"], g["lru_lambda"], g["conv_b"],
     g["conv_w"]) = _rnn_bwd(proj, hr, dz, conv_w, conv_b, w_rg_a, b_rg_a, w_rg_x, b_rg_x, lam)
    segs = [dupool, durnn, dugate, dgp, dgr]
    grad_x, g["norm_mix"] = _in_proj_bwd(segs, w_in, dx2, x, norm_mix, tm=tr)
    g["w_in"] = _weight_grad(h1, segs, tm=D_MODEL, tn=256, name="w_in_grad")
    return sq_cols, grad_x, g


LARGE = {"w_in": "col", "w_pool_out": "col", "w_rnn_out": "row", "w_o": "row", "w_ffn_in": "col", "w_ffn_out": "row"}
LARGE_SHAPE = {"w_in": (D_MODEL, D_IN), "w_pool_out": (D_POOL, D_MODEL), "w_rnn_out": (D_RNN, D_MODEL),
               "w_o": (D_MODEL, D_MODEL), "w_ffn_in": (D_MODEL, 2 * D_FF), "w_ffn_out": (D_FF, D_MODEL)}


def _place():
    x, y, c = lax.axis_index("x"), lax.axis_index("y"), lax.axis_index("c")
    return 2 * x + y, c


def _chip_device(chip, c):
    return (chip // 2, chip % 2, c)


def _chip_window(ref, kind, shape, chip, half=None):
    K, N = shape
    if kind == "col":
        rows = slice(None) if half is None else pl.ds(half * (K // 2), K // 2)
        return ref.at[rows, pl.ds(chip * (N // N_CHIPS), N // N_CHIPS)]
    ks = K // N_CHIPS
    if half is None:
        return ref.at[pl.ds(chip * ks, ks), :]
    return ref.at[pl.ds(chip * ks + half * (ks // 2), ks // 2), :]


def _row_half(ref, half):
    rows = ref.shape[0] // 2
    return ref.at[pl.ds(half * rows, rows), :]


def _gather_weights(full, conv_w_full):
    names = list(LARGE)
    n = len(names)
    cw_cols = D_RNN // N_CHIPS

    def body(*refs):
        full_refs, cwg_ref = refs[n + 1:2 * n + 1], refs[2 * n + 1]
        send_sems, recv_sems = refs[2 * n + 2:]
        chip, c = _place()
        others = [chip ^ r for r in (1, 2, 3)]

        def remote(k, win, sem, to):
            return pltpu.make_async_remote_copy(src_ref=win, dst_ref=win, send_sem=send_sems.at[k, sem],
                                                recv_sem=recv_sems.at[k, sem], device_id=to, device_id_type=MESH)

        sends = []
        for k, name in enumerate(names):
            for r, other in enumerate(others):
                cp = remote(k, _chip_window(full_refs[k], LARGE[name], LARGE_SHAPE[name], chip, c), r, _chip_device(other, c))
                cp.start()
                sends.append(cp)
        for r, other in enumerate(others):
            cp = remote(n, cwg_ref.at[:, pl.ds(chip * cw_cols, cw_cols)], r, _chip_device(other, c))
            cp.start()
            sends.append(cp)
        for k, name in enumerate(names):
            for r, other in enumerate(others):
                win = _chip_window(full_refs[k], LARGE[name], LARGE_SHAPE[name], other, c)
                remote(k, win, r, _chip_device(other, c)).wait_recv()
                cp = remote(k, win, 3 + r, _chip_device(chip, 1 - c))
                cp.start()
                sends.append(cp)
        for r, other in enumerate(others):
            remote(n, cwg_ref.at[:, pl.ds(other * cw_cols, cw_cols)], r, _chip_device(other, c)).wait_recv()
        for k, name in enumerate(names):
            for r, other in enumerate(others):
                win = _chip_window(full_refs[k], LARGE[name], LARGE_SHAPE[name], other, 1 - c)
                remote(k, win, 3 + r, _chip_device(chip, 1 - c)).wait_recv()
        for cp in sends:
            cp.wait_send()

    out = pl.pallas_call(
        body, name="gather_weights",
        in_specs=[ANY] * (n + 1), out_specs=[ANY] * (n + 1),
        out_shape=[jax.ShapeDtypeStruct(LARGE_SHAPE[name], BF16) for name in names]
        + [jax.ShapeDtypeStruct((CONV_WIDTH, D_RNN), F32)],
        input_output_aliases={k: k for k in range(n + 1)},
        scratch_shapes=[pltpu.SemaphoreType.DMA((n + 1, 6)), pltpu.SemaphoreType.DMA((n + 1, 6))],
        compiler_params=pltpu.CompilerParams(has_side_effects=True),
    )(*[full[name] for name in names], conv_w_full)
    return dict(zip(names, out[:n])), out[n]


def _core_halves(ref, kind, shape, c):
    return [_chip_window(ref, kind, shape, chip, c) for chip in range(N_CHIPS)]


def _exchange_sibling(grads):
    names = list(LARGE)
    n = len(names)

    def body(*refs):
        g_refs, got_refs = refs[:n], refs[n:2 * n]
        send_sems, recv_sems = refs[2 * n:]
        chip, c = _place()
        sibling = _chip_device(chip, 1 - c)
        for k, name in enumerate(names):
            kind, shape = LARGE[name], LARGE_SHAPE[name]
            if kind == "col":
                pairs = [(_row_half(g_refs[k], 1 - c), got_refs[k])]
            else:
                rows = shape[0] // N_DEV
                pairs = [(win, got_refs[k].at[pl.ds(j * rows, rows), :])
                         for j, win in enumerate(_core_halves(g_refs[k], kind, shape, 1 - c))]
            for src, dst in pairs:
                pltpu.make_async_remote_copy(src_ref=src, dst_ref=dst, send_sem=send_sems.at[k], recv_sem=recv_sems.at[k],
                                             device_id=sibling, device_id_type=MESH).start()
        for k in range(n):
            pltpu.make_async_remote_copy(src_ref=got_refs[k], dst_ref=got_refs[k], send_sem=send_sems.at[k],
                                         recv_sem=recv_sems.at[k], device_id=sibling, device_id_type=MESH).wait()

    out = pl.pallas_call(
        body, name="exchange_sibling", in_specs=[ANY] * n, out_specs=[ANY] * n,
        out_shape=[jax.ShapeDtypeStruct((LARGE_SHAPE[name][0] // 2, LARGE_SHAPE[name][1]), F32) for name in names],
        scratch_shapes=[pltpu.SemaphoreType.DMA((n,)), pltpu.SemaphoreType.DMA((n,))],
        compiler_params=pltpu.CompilerParams(has_side_effects=True),
    )(*[grads[name] for name in names])
    return dict(zip(names, out))


def _chip_sum(name, g, got, c):
    kind, (K, N) = LARGE[name], LARGE_SHAPE[name]
    rows = K // N_DEV

    def body(c_ref, g_ref, got_ref, o_ref, ob_ref):
        total = g_ref[...] + got_ref[...]
        o_ref[...] = total
        ob_ref[...] = total.astype(BF16)

    if kind == "col":
        mine = pl.BlockSpec((rows, N), lambda j, c_ref: (j + N_CHIPS * c_ref[0], 0))
    else:
        mine = pl.BlockSpec((rows, N), lambda j, c_ref: (2 * j + c_ref[0], 0))
    blk = pl.BlockSpec((rows, N), lambda j, c_ref: (j, 0))
    return pl.pallas_call(
        body, name=name + "_chip_sum",
        grid_spec=pltpu.PrefetchScalarGridSpec(num_scalar_prefetch=1, grid=(N_CHIPS,), in_specs=[mine, blk], out_specs=[blk, blk]),
        out_shape=[jax.ShapeDtypeStruct((K // 2, N), F32), jax.ShapeDtypeStruct((K // 2, N), BF16)],
        compiler_params=_params(dimension_semantics=("parallel",)),
    )(c, g, got)


def _piece(ref, kind, shape, chip):
    K, N = shape
    if kind == "col":
        return ref.at[:, pl.ds(chip * (N // N_CHIPS), N // N_CHIPS)]
    return ref.at[pl.ds(chip * (K // N_DEV), K // N_DEV), :]


def _piece_shape(name):
    kind, (K, N) = LARGE[name], LARGE_SHAPE[name]
    return (K // 2, N // N_CHIPS) if kind == "col" else (K // N_DEV, N)


def _exchange_chips(sums):
    names = list(LARGE)
    n = len(names)

    def body(*refs):
        s_refs, got_refs = refs[:n], refs[n:2 * n]
        send_sems, recv_sems = refs[2 * n:]
        chip, c = _place()
        copies = []
        for k, name in enumerate(names):
            kind, shape = LARGE[name], LARGE_SHAPE[name]
            for r in range(3):
                other = chip ^ (r + 1)
                cp = pltpu.make_async_remote_copy(src_ref=_piece(s_refs[k], kind, shape, other), dst_ref=got_refs[k].at[r],
                                                  send_sem=send_sems.at[k, r], recv_sem=recv_sems.at[k, r],
                                                  device_id=_chip_device(other, c), device_id_type=MESH)
                cp.start()
                copies.append(cp)
        for cp in copies:
            cp.wait()

    out = pl.pallas_call(
        body, name="exchange_chips", in_specs=[ANY] * n, out_specs=[ANY] * n,
        out_shape=[jax.ShapeDtypeStruct((3,) + _piece_shape(name), BF16) for name in names],
        scratch_shapes=[pltpu.SemaphoreType.DMA((n, 3)), pltpu.SemaphoreType.DMA((n, 3))],
        compiler_params=pltpu.CompilerParams(has_side_effects=True),
    )(*[sums[name] for name in names])
    return dict(zip(names, out))


def _final_sum(name, chip_sum, got, place):
    kind = LARGE[name]
    rows, cols = _piece_shape(name)

    def body(place_ref, s_ref, got_ref, o_ref):
        o_ref[...] = ((s_ref[...] + got_ref[0].astype(F32)) + got_ref[1].astype(F32)) + got_ref[2].astype(F32)

    if kind == "col":
        mine = pl.BlockSpec((rows, cols), lambda i, place_ref: (0, place_ref[0]))
    else:
        mine = pl.BlockSpec((rows, cols), lambda i, place_ref: (place_ref[0], 0))
    return pl.pallas_call(
        body, name=name + "_final_sum",
        grid_spec=pltpu.PrefetchScalarGridSpec(
            num_scalar_prefetch=1, grid=(1,), in_specs=[mine, pl.BlockSpec((3, rows, cols), lambda i, place_ref: (0, 0, 0))],
            out_specs=pl.BlockSpec((rows, cols), lambda i, place_ref: (place_ref[1], 0))),
        out_shape=jax.ShapeDtypeStruct((2 * rows, cols), F32),
        compiler_params=_params(dimension_semantics=("arbitrary",)),
    )(place, chip_sum, got)


def _join_halves(shards):
    names = list(LARGE)
    n = len(names)

    def body(*refs):
        full_refs = refs[n:2 * n]
        send_sems, recv_sems = refs[2 * n:]
        chip, c = _place()
        copies = []
        for k in range(n):
            cp = pltpu.make_async_remote_copy(src_ref=_row_half(full_refs[k], c), dst_ref=_row_half(full_refs[k], c),
                                              send_sem=send_sems.at[k], recv_sem=recv_sems.at[k],
                                              device_id=_chip_device(chip, 1 - c), device_id_type=MESH)
            cp.start()
            copies.append(cp)
        for k in range(n):
            copies[k].wait_send()
            pltpu.make_async_remote_copy(src_ref=_row_half(full_refs[k], 1 - c), dst_ref=_row_half(full_refs[k], 1 - c),
                                         send_sem=send_sems.at[k], recv_sem=recv_sems.at[k],
                                         device_id=_chip_device(chip, 1 - c), device_id_type=MESH).wait_recv()

    out = pl.pallas_call(
        body, name="join_halves", in_specs=[ANY] * n, out_specs=[ANY] * n,
        out_shape=[jax.ShapeDtypeStruct(shards[name].shape, F32) for name in names],
        input_output_aliases={k: k for k in range(n)},
        scratch_shapes=[pltpu.SemaphoreType.DMA((n,)), pltpu.SemaphoreType.DMA((n,))],
        compiler_params=pltpu.CompilerParams(has_side_effects=True),
    )(*[shards[name] for name in names])
    return dict(zip(names, out))


VEC_ROWS = 16


def _all_reduce_small(slabs):
    n = len(slabs)

    def body(*refs):
        in_refs, out_refs, got_refs = refs[:n], refs[n:2 * n], refs[2 * n:3 * n]
        send_sems, recv_sems = refs[3 * n:]
        x, y, c = lax.axis_index("x"), lax.axis_index("y"), lax.axis_index("c")
        me = 4 * x + 2 * y + c

        def remote(src, dst, k, phase, r):
            other = me ^ r
            return pltpu.make_async_remote_copy(src_ref=src, dst_ref=dst, send_sem=send_sems.at[k, phase, r],
                                                recv_sem=recv_sems.at[k, phase, r],
                                                device_id=(other // 4, (other // 2) % 2, other % 2), device_id_type=MESH)

        scatter = [remote(in_refs[k].at[me ^ r], got_refs[k].at[r], k, 0, r) for r in range(1, N_DEV) for k in range(n)]
        for cp in scatter:
            cp.start()
        for cp in scatter:
            cp.wait()
        for k in range(n):
            total = in_refs[k][me]
            for r in range(1, N_DEV):
                total = total + got_refs[k][r]
            out_refs[k][me] = total
        gather = [remote(out_refs[k].at[me], out_refs[k].at[me], k, 1, r) for r in range(1, N_DEV) for k in range(n)]
        for cp in gather:
            cp.start()
        for r in range(1, N_DEV):
            for k in range(n):
                remote(out_refs[k].at[me ^ r], out_refs[k].at[me ^ r], k, 1, r).wait_recv()
        for cp in gather:
            cp.wait_send()

    return pl.pallas_call(
        body, name="all_reduce_small", in_specs=[VMEM] * n, out_specs=[VMEM] * n,
        out_shape=[jax.ShapeDtypeStruct(s.shape, F32) for s in slabs],
        scratch_shapes=[pltpu.VMEM(s.shape, F32) for s in slabs]
        + [pltpu.SemaphoreType.DMA((n, 2, N_DEV)), pltpu.SemaphoreType.DMA((n, 2, N_DEV))],
        compiler_params=pltpu.CompilerParams(has_side_effects=True),
    )(*slabs)


def _cast_into_whole(w, name, place):
    rows, cols = w.shape
    tr = rows // 2

    def body(place_ref, w_ref, o_ref):
        o_ref[...] = w_ref[...].astype(BF16)

    if LARGE[name] == "col":
        window = pl.BlockSpec((tr, cols), lambda i, place_ref: (i, place_ref[0]))
    else:
        window = pl.BlockSpec((tr, cols), lambda i, place_ref: (2 * place_ref[0] + i, 0))
    return pl.pallas_call(
        body, name=name + "_cast",
        grid_spec=pltpu.PrefetchScalarGridSpec(num_scalar_prefetch=1, grid=(2,),
                                               in_specs=[pl.BlockSpec((tr, cols), lambda i, place_ref: (i, 0))], out_specs=window),
        out_shape=jax.ShapeDtypeStruct(LARGE_SHAPE[name], BF16),
        compiler_params=_params(dimension_semantics=("parallel",)))(place, w)


def _adamw_math(w, g, m, v):
    m = ADAM_B1 * m + (1.0 - ADAM_B1) * g
    v = ADAM_B2 * v + (1.0 - ADAM_B2) * (g * g)
    m_hat = m / (1.0 - ADAM_B1 ** ADAM_STEP)
    v_hat = v / (1.0 - ADAM_B2 ** ADAM_STEP)
    delta = -ADAM_LR * (m_hat / (jnp.sqrt(v_hat) + ADAM_EPS) + ADAM_WD * w)
    return delta, m, v


def _adamw_large(w, g, m, v, name):
    rows, cols = w.shape
    tr = rows // 4

    def body(w_ref, g_ref, m_ref, v_ref, d_ref, mo_ref, vo_ref):
        d_ref[...], mo_ref[...], vo_ref[...] = _adamw_math(w_ref[...], g_ref[...], m_ref[...], v_ref[...])

    blk = pl.BlockSpec((tr, cols), lambda i: (i, 0))
    out = jax.ShapeDtypeStruct(w.shape, F32)
    return pl.pallas_call(body, name=name + "_adamw", grid=(4,), in_specs=[blk] * 4, out_specs=[blk] * 3, out_shape=[out] * 3,
                          compiler_params=_params(dimension_semantics=("parallel",)))(w, g, m, v)


def _adamw_small(ws, gs, ms, vs):
    n = len(ws)

    def body(*refs):
        for k in range(n):
            w_ref, g_ref, m_ref, v_ref = (refs[q * n + k] for q in range(4))
            d_ref, mo_ref, vo_ref = (refs[(4 + q) * n + k] for q in range(3))
            d_ref[...], mo_ref[...], vo_ref[...] = _adamw_math(w_ref[...], g_ref[...], m_ref[...], v_ref[...])

    out = [jax.ShapeDtypeStruct(w.shape, F32) for w in ws]
    res = pl.pallas_call(body, name="small_adamw", in_specs=[VMEM] * (4 * n), out_specs=[VMEM] * (3 * n), out_shape=out * 3,
                         compiler_params=_params())(*ws, *gs, *ms, *vs)
    return res[:n], res[n:2 * n], res[2 * n:]


WEIGHTS = ["norm_mix", "w_in", "w_pool_grp", "pool_scale", "w_pool_out", "conv_w", "conv_b", "w_rg_a", "b_rg_a", "w_rg_x",
           "b_rg_x", "lru_lambda", "w_rnn_out", "w_o", "norm_ffn", "w_ffn_in", "w_ffn_out", "norm_final"]
VEC_ITEMS = ["norm_mix", "norm_ffn", "norm_final", "pool_scale", "conv_b", "lru_lambda", "b_rg_a", "b_rg_x"]
MAT_ITEMS = ["w_pool_grp", "w_rg_a", "w_rg_x"]


def _as2d(name, a):
    if name in MAT_ITEMS:
        return a.reshape(-1, HEAD, HEAD)
    if name == "conv_w":
        return a.reshape(CONV_WIDTH, -1)
    return a.reshape(1, -1)


def kernel(x, norm_mix, w_in, w_pool_grp, pool_scale, w_pool_out, conv_w, conv_b, w_rg_a, b_rg_a, w_rg_x, b_rg_x, lru_lambda, w_rnn_out, w_o, norm_ffn, w_ffn_in, w_ffn_out, norm_final, loss_target, m_norm_mix, m_w_in, m_w_pool_grp, m_pool_scale, m_w_pool_out, m_conv_w, m_conv_b, m_w_rg_a, m_b_rg_a, m_w_rg_x, m_b_rg_x, m_lru_lambda, m_w_rnn_out, m_w_o, m_norm_ffn, m_w_ffn_in, m_w_ffn_out, m_norm_final, v_norm_mix, v_w_in, v_w_pool_grp, v_pool_scale, v_w_pool_out, v_conv_w, v_conv_b, v_w_rg_a, v_b_rg_a, v_w_rg_x, v_b_rg_x, v_lru_lambda, v_w_rnn_out, v_w_o, v_norm_ffn, v_w_ffn_in, v_w_ffn_out, v_norm_final):
    given = dict(locals())
    w = {name: given[name] for name in WEIGHTS}
    m = {name: given["m_" + name] for name in WEIGHTS}
    v = {name: given["v_" + name] for name in WEIGHTS}
    chip, c = _place()

    place = jnp.stack([chip, c]).astype(jnp.int32)
    conv_cols = w["conv_w"].shape[-1]
    conv_w_mine = lax.dynamic_update_slice_in_dim(jnp.zeros((CONV_WIDTH, D_RNN), F32), w["conv_w"][0], chip * conv_cols, axis=1)
    full, conv_w_full = _gather_weights({name: _cast_into_whole(w[name][0], name, place) for name in LARGE}, conv_w_mine)
    small = {name: _as2d(name, w[name]) for name in WEIGHTS if name not in LARGE and name != "conv_w"}
    sq_cols, grad_x, g = _local_step(
        x[0], loss_target[0], small["norm_mix"], full["w_in"], small["w_pool_grp"], small["pool_scale"], full["w_pool_out"],
        conv_w_full, small["conv_b"], small["w_rg_a"], small["b_rg_a"], small["w_rg_x"], small["b_rg_x"], small["lru_lambda"],
        full["w_rnn_out"], full["w_o"], small["norm_ffn"], full["w_ffn_in"], full["w_ffn_out"], small["norm_final"])
    loss = lax.psum(0.5 / D_MODEL * jnp.sum(sq_cols), ("x", "y", "c"))

    c1 = jnp.reshape(c, (1,)).astype(jnp.int32)
    from_sibling = _exchange_sibling({name: g[name] for name in LARGE})
    chip_sums = {name: _chip_sum(name, g[name], from_sibling[name], c1) for name in LARGE}
    from_chips = _exchange_chips({name: chip_sums[name][1] for name in LARGE})
    grads = _join_halves({name: _final_sum(name, chip_sums[name][0], from_chips[name], place) for name in LARGE})

    vec_rows = [g[name] if name != "pool_scale" else jnp.pad(g[name], ((0, 0), (0, D_MODEL - D_POOL))) for name in VEC_ITEMS]
    vec = jnp.concatenate(vec_rows + [g["conv_w"], jnp.zeros((VEC_ROWS - len(VEC_ITEMS) - CONV_WIDTH, D_MODEL), F32)], axis=0)
    mat = jnp.concatenate([g[name].reshape(-1, HEAD) for name in MAT_ITEMS], axis=0)
    vec, mat = _all_reduce_small([vec.reshape(VEC_ROWS, N_DEV, HEAD).transpose(1, 0, 2), mat.reshape(N_DEV, -1, HEAD)])
    vec = vec.transpose(1, 0, 2).reshape(VEC_ROWS, D_MODEL)
    mat = mat.reshape(-1, HEAD)
    for k, name in enumerate(VEC_ITEMS):
        grads[name] = vec[k:k + 1, :w[name].size]
    grads["conv_w"] = lax.dynamic_slice_in_dim(vec[len(VEC_ITEMS):len(VEC_ITEMS) + CONV_WIDTH], chip * conv_cols, conv_cols, axis=1)
    row = 0
    for name in MAT_ITEMS:
        rows = w[name].size // HEAD
        grads[name] = mat[row:row + rows]
        row += rows

    delta, new_m, new_v = {}, {}, {}
    for name in LARGE:
        delta[name], new_m[name], new_v[name] = _adamw_large(w[name][0], grads[name], m[name][0], v[name][0], name)
    small_names = [name for name in WEIGHTS if name not in LARGE]
    flat = lambda d: [d[name].reshape(grads[name].shape) for name in small_names]
    ds, mo, vo = _adamw_small(flat(w), [grads[name] for name in small_names], flat(m), flat(v))
    for k, name in enumerate(small_names):
        delta[name], new_m[name], new_v[name] = ds[k], mo[k], vo[k]

    shaped = lambda d: [d[name].reshape(w[name].shape) for name in WEIGHTS]
    return (loss, grad_x[None], *shaped(grads), *shaped(delta), *shaped(new_m), *shaped(new_v))
```

```python
import functools
import math

import jax
import jax.numpy as jnp
from jax import lax
from jax.experimental import pallas as pl
from jax.experimental.pallas import tpu as pltpu

F32 = jnp.float32
BF16 = jnp.bfloat16

D_MODEL = 1024
D_POOL = 512
N_POOL_GROUPS = 4
D_RNN = 1024
N_RNN_HEADS = 8
HEAD = 128
CONV_WIDTH = 4
LRU_C = 8.0
D_FF = 2816
D_IN = D_POOL + 2 * D_RNN + 2 * D_MODEL
NORM_EPS = 1e-6
COL_RNN = D_POOL // HEAD
COL_GATE = (D_POOL + D_RNN) // HEAD

ADAM_LR = 0.001
ADAM_B1 = 0.9
ADAM_B2 = 0.999
ADAM_EPS = 1e-08
ADAM_WD = 0.01
ADAM_STEP = 10

N_CHIPS = 4
N_DEV = 8
MESH = pl.DeviceIdType.MESH
ANY = pl.BlockSpec(memory_space=pl.ANY)
VMEM = pl.BlockSpec(memory_space=pltpu.VMEM)
VMEM_LIMIT_BYTES = 60 * 1024 * 1024
SUBLANES = 8
POOL_HALO = 16
CHUNK = 512

GELU_C = math.sqrt(2.0 / math.pi)
GELU_A = 0.044715


def _params(**kw):
    return pltpu.CompilerParams(vmem_limit_bytes=VMEM_LIMIT_BYTES, **kw)


def _sigmoid(x):
    return 0.5 * jnp.tanh(0.5 * x) + 0.5


def _log1p(y):
    u = 1.0 + y
    d = u - 1.0
    return jnp.where(d == 0.0, y, jnp.log(u) * (y / jnp.where(d == 0.0, 1.0, d)))


def _gelu_parts(x):
    x2 = x * x
    th = jnp.tanh(GELU_C * (x + GELU_A * x * x2))
    g = 0.5 * x * (1.0 + th)
    dg = 0.5 * (1.0 + th) + 0.5 * x * (1.0 - th * th) * GELU_C * (1.0 + 3.0 * GELU_A * x2)
    return g, dg


def _dot(a, b):
    return jnp.dot(a, b, preferred_element_type=F32)


def _dot_nt(a, b):
    return lax.dot_general(a, b, (((1,), (1,)), ((), ())), preferred_element_type=F32)


def _dot_tn(a, b):
    return lax.dot_general(a, b, (((0,), (0,)), ((), ())), preferred_element_type=F32)


def _rms_scale(xv):
    return lax.rsqrt(jnp.mean(xv * xv, axis=-1, keepdims=True) + NORM_EPS)


def _rms_bwd(dy, xv, g):
    r = _rms_scale(xv)
    xh = xv * r
    dyg = dy * g
    dx = r * (dyg - xh * jnp.mean(dyg * xh, axis=-1, keepdims=True))
    return dx, dy * xh


class _Job:
    def __init__(self, inputs, out_shapes, aliases, sem_shape, start, finish):
        self.inputs, self.out_shapes, self.aliases, self.sem_shape = list(inputs), list(out_shapes), dict(aliases), sem_shape
        self.start, self.finish = start, finish


def _pallas(body, operands, *, name, grid, in_specs, out_specs, out_shape, scratch_shapes=(), semantics=None, jobs=()):
    n_in, n_out, n_scr = len(in_specs), len(out_specs), len(scratch_shapes)
    job_in = [a for job in jobs for a in job.inputs]
    job_out = [s for job in jobs for s in job.out_shapes]
    aliases, i0, o0 = {}, n_in, n_out
    for job in jobs:
        aliases.update({i0 + i: o0 + o for i, o in job.aliases.items()})
        i0, o0 = i0 + len(job.inputs), o0 + len(job.out_shapes)

    def whole(*refs):
        ins, j_ins = refs[:n_in], refs[n_in:n_in + len(job_in)]
        outs = refs[n_in + len(job_in):][:n_out]
        j_outs = refs[n_in + len(job_in) + n_out:][:len(job_out)]
        rest = refs[n_in + len(job_in) + n_out + len(job_out):]
        scr, sems = rest[:n_scr], rest[n_scr:]

        def run(phase):
            i, o = 0, 0
            for k, job in enumerate(jobs):
                getattr(job, phase)(j_ins[i:i + len(job.inputs)], j_outs[o:o + len(job.out_shapes)], sems[2 * k], sems[2 * k + 1])
                i, o = i + len(job.inputs), o + len(job.out_shapes)

        def at(step_of, phase):
            if not jobs:
                return
            if not grid:
                run(phase)
                return
            cond = functools.reduce(jnp.logical_and, [pl.program_id(d) == step_of(d) for d in range(len(grid))])
            pl.when(cond)(functools.partial(run, phase))

        at(lambda d: 0, "start")
        body(*ins, *outs, *scr)
        at(lambda d: grid[d] - 1, "finish")

    res = pl.pallas_call(
        whole, name=name, grid=grid, in_specs=list(in_specs) + [ANY] * len(job_in), out_specs=list(out_specs) + [ANY] * len(job_out),
        out_shape=list(out_shape) + job_out, input_output_aliases=aliases,
        scratch_shapes=list(scratch_shapes) + [pltpu.SemaphoreType.DMA(job.sem_shape) for job in jobs for _ in range(2)],
        compiler_params=_params(dimension_semantics=semantics, has_side_effects=bool(jobs)),
    )(*operands, *job_in)
    per_job, o = [], n_out
    for job in jobs:
        per_job.append(res[o:o + len(job.out_shapes)])
        o += len(job.out_shapes)
    return res[:n_out], per_job


def _run_jobs(jobs, name):
    return _pallas(lambda: None, [], name=name, grid=(), in_specs=[], out_specs=[], out_shape=[], jobs=jobs)[1]


def _norm_matmul(x, g, w, *, tm, tn, name, jobs=()):
    T, K = x.shape
    N = w.shape[1]

    def body(x_ref, g_ref, w_ref, o_ref, h_ref):
        @pl.when(pl.program_id(1) == 0)
        def _():
            xv = x_ref[...]
            h_ref[...] = (xv * _rms_scale(xv) * g_ref[...]).astype(BF16)

        o_ref[...] = _dot(h_ref[...], w_ref[...])

    return _pallas(
        body, (x, g, w), name=name, grid=(T // tm, N // tn),
        in_specs=[pl.BlockSpec((tm, K), lambda i, j: (i, 0)), pl.BlockSpec((1, K), lambda i, j: (0, 0)),
                  pl.BlockSpec((K, tn), lambda i, j: (0, j))],
        out_specs=[pl.BlockSpec((tm, tn), lambda i, j: (i, j)), pl.BlockSpec((tm, K), lambda i, j: (i, 0))],
        out_shape=[jax.ShapeDtypeStruct((T, N), F32), jax.ShapeDtypeStruct((T, K), BF16)],
        semantics=("parallel", "arbitrary"), jobs=jobs)


def _ffn_in(x2, g, w, *, tm, tn):
    T, K = x2.shape
    nb = D_FF // tn

    def body(x_ref, g_ref, wg_ref, wu_ref, gate_ref, up_ref, act_ref, h_ref):
        @pl.when(pl.program_id(1) == 0)
        def _():
            xv = x_ref[...]
            h_ref[...] = (xv * _rms_scale(xv) * g_ref[...]).astype(BF16)

        h = h_ref[...]
        gate = _dot(h, wg_ref[...])
        up = _dot(h, wu_ref[...])
        gate_ref[...] = gate
        up_ref[...] = up
        act_ref[...] = (gate * _sigmoid(gate) * up).astype(BF16)

    blk = pl.BlockSpec((tm, tn), lambda i, j: (i, j))
    return pl.pallas_call(
        body, name="ffn_in", grid=(T // tm, nb),
        in_specs=[pl.BlockSpec((tm, K), lambda i, j: (i, 0)), pl.BlockSpec((1, K), lambda i, j: (0, 0)),
                  pl.BlockSpec((K, tn), lambda i, j: (0, j)), pl.BlockSpec((K, tn), lambda i, j: (0, j + nb))],
        out_specs=[blk, blk, blk, pl.BlockSpec((tm, K), lambda i, j: (i, 0))],
        out_shape=[jax.ShapeDtypeStruct((T, D_FF), F32), jax.ShapeDtypeStruct((T, D_FF), F32),
                   jax.ShapeDtypeStruct((T, D_FF), BF16), jax.ShapeDtypeStruct((T, K), BF16)],
        compiler_params=_params(dimension_semantics=("parallel", "arbitrary")),
    )(x2, g, w, w)


def _branch_mix(pm, z, w_pool_out, w_rnn_out, proj, *, tm, tn):
    T = pm.shape[0]
    col_gp = (D_POOL + 2 * D_RNN) // tn
    col_gr = col_gp + D_MODEL // tn

    def body(pm_ref, z_ref, wp_ref, wr_ref, gp_ref, gr_ref, yp_ref, yr_ref, mix_ref):
        yp = _dot(pm_ref[...], wp_ref[...])
        yr = _dot(z_ref[...], wr_ref[...])
        yp_ref[...] = yp
        yr_ref[...] = yr
        mix_ref[...] = (_sigmoid(gp_ref[...]) * yp + _sigmoid(gr_ref[...]) * yr).astype(BF16)

    blk = pl.BlockSpec((tm, tn), lambda i, j: (i, j))
    return pl.pallas_call(
        body, name="branch_mix", grid=(T // tm, D_MODEL // tn),
        in_specs=[pl.BlockSpec((tm, D_POOL), lambda i, j: (i, 0)), pl.BlockSpec((tm, D_RNN), lambda i, j: (i, 0)),
                  pl.BlockSpec((D_POOL, tn), lambda i, j: (0, j)), pl.BlockSpec((D_RNN, tn), lambda i, j: (0, j)),
                  pl.BlockSpec((tm, tn), lambda i, j: (i, col_gp + j)), pl.BlockSpec((tm, tn), lambda i, j: (i, col_gr + j))],
        out_specs=[blk, blk, blk],
        out_shape=[jax.ShapeDtypeStruct((T, D_MODEL), F32), jax.ShapeDtypeStruct((T, D_MODEL), F32),
                   jax.ShapeDtypeStruct((T, D_MODEL), BF16)],
        compiler_params=_params(dimension_semantics=("parallel", "parallel")),
    )(pm, z, w_pool_out, w_rnn_out, proj, proj)


def _out_proj_residual(mix, w_o, x, *, tm):
    T = x.shape[0]

    def body(mix_ref, w_ref, x_ref, o_ref):
        o_ref[...] = x_ref[...] + _dot(mix_ref[...], w_ref[...])

    row = pl.BlockSpec((tm, D_MODEL), lambda i: (i, 0))
    return pl.pallas_call(
        body, name="out_proj_residual", grid=(T // tm,),
        in_specs=[row, pl.BlockSpec((D_MODEL, D_MODEL), lambda i: (0, 0)), row],
        out_specs=row, out_shape=jax.ShapeDtypeStruct((T, D_MODEL), F32),
        compiler_params=_params(dimension_semantics=("parallel",)),
    )(mix, w_o, x)


def _ffn_out_loss(act, w, x2, g3, target, *, tm):
    T = x2.shape[0]

    def body(act_ref, w_ref, x2_ref, g_ref, t_ref, dx_ref, dxb_ref, sq_ref, dg_ref):
        @pl.when(pl.program_id(0) == 0)
        def _():
            sq_ref[...] = jnp.zeros_like(sq_ref)
            dg_ref[...] = jnp.zeros_like(dg_ref)

        x3 = x2_ref[...] + _dot(act_ref[...], w_ref[...])
        g = g_ref[...]
        err = x3 * _rms_scale(x3) * g - t_ref[...]
        sq_ref[...] += jnp.sum(err * err, axis=0, keepdims=True)
        dx, dgp = _rms_bwd(err * (1.0 / D_MODEL), x3, g)
        dg_ref[...] += jnp.sum(dgp, axis=0, keepdims=True)
        dx_ref[...] = dx
        dxb_ref[...] = dx.astype(BF16)

    row = pl.BlockSpec((tm, D_MODEL), lambda i: (i, 0))
    vec = pl.BlockSpec((1, D_MODEL), lambda i: (0, 0))
    return pl.pallas_call(
        body, name="ffn_out_loss", grid=(T // tm,),
        in_specs=[pl.BlockSpec((tm, D_FF), lambda i: (i, 0)), pl.BlockSpec((D_FF, D_MODEL), lambda i: (0, 0)), row, vec, row],
        out_specs=[row, row, vec, vec],
        out_shape=[jax.ShapeDtypeStruct((T, D_MODEL), F32), jax.ShapeDtypeStruct((T, D_MODEL), BF16),
                   jax.ShapeDtypeStruct((1, D_MODEL), F32), jax.ShapeDtypeStruct((1, D_MODEL), F32)],
        compiler_params=_params(dimension_semantics=("arbitrary",)),
    )(act, w, x2, g3, target)


def _ffn_out_bwd(dx3b, w, gate, up, *, tm, tn):
    T = dx3b.shape[0]

    def body(dx_ref, w_ref, gate_ref, up_ref, dgate_ref, dup_ref):
        dact = _dot_nt(dx_ref[...], w_ref[...])
        gate = gate_ref[...]
        s = _sigmoid(gate)
        dgate_ref[...] = (dact * up_ref[...] * s * (1.0 + gate * (1.0 - s))).astype(BF16)
        dup_ref[...] = (dact * gate * s).astype(BF16)

    blk = pl.BlockSpec((tm, tn), lambda i, j: (i, j))
    return pl.pallas_call(
        body, name="ffn_out_bwd", grid=(T // tm, D_FF // tn),
        in_specs=[pl.BlockSpec((tm, D_MODEL), lambda i, j: (i, 0)), pl.BlockSpec((tn, D_MODEL), lambda i, j: (j, 0)), blk, blk],
        out_specs=[blk, blk],
        out_shape=[jax.ShapeDtypeStruct((T, D_FF), BF16), jax.ShapeDtypeStruct((T, D_FF), BF16)],
        compiler_params=_params(dimension_semantics=("parallel", "parallel")),
    )(dx3b, w, gate, up)


def _ffn_in_bwd(dgate, dup, w, dx3, x2, g2, *, tm):
    T = x2.shape[0]

    def body(dgate_ref, dup_ref, w_ref, dx3_ref, x2_ref, g_ref, dx_ref, dxb_ref, dg_ref):
        @pl.when(pl.program_id(0) == 0)
        def _():
            dg_ref[...] = jnp.zeros_like(dg_ref)

        dh = _dot_nt(dgate_ref[...], w_ref[:, :D_FF]) + _dot_nt(dup_ref[...], w_ref[:, D_FF:])
        dxn, dgp = _rms_bwd(dh, x2_ref[...], g_ref[...])
        dx = dx3_ref[...] + dxn
        dg_ref[...] += jnp.sum(dgp, axis=0, keepdims=True)
        dx_ref[...] = dx
        dxb_ref[...] = dx.astype(BF16)

    row = pl.BlockSpec((tm, D_MODEL), lambda i: (i, 0))
    wide = pl.BlockSpec((tm, D_FF), lambda i: (i, 0))
    vec = pl.BlockSpec((1, D_MODEL), lambda i: (0, 0))
    return pl.pallas_call(
        body, name="ffn_in_bwd", grid=(T // tm,),
        in_specs=[wide, wide, pl.BlockSpec((D_MODEL, 2 * D_FF), lambda i: (0, 0)), row, row, vec],
        out_specs=[row, row, vec],
        out_shape=[jax.ShapeDtypeStruct((T, D_MODEL), F32), jax.ShapeDtypeStruct((T, D_MODEL), BF16),
                   jax.ShapeDtypeStruct((1, D_MODEL), F32)],
        compiler_params=_params(dimension_semantics=("arbitrary",)),
    )(dgate, dup, w, dx3, x2, g2)


def _out_proj_bwd(dx2b, w_o, proj, y_pool, y_rnn, *, tm, tn, jobs=()):
    T = dx2b.shape[0]
    col_gp = (D_POOL + 2 * D_RNN) // tn
    col_gr = col_gp + D_MODEL // tn

    def body(dx_ref, w_ref, gp_ref, gr_ref, yp_ref, yr_ref, dgp_ref, dgr_ref, dyp_ref, dyr_ref):
        dmix = _dot_nt(dx_ref[...], w_ref[...])
        sp = _sigmoid(gp_ref[...])
        sr = _sigmoid(gr_ref[...])
        dgp_ref[...] = (dmix * yp_ref[...] * sp * (1.0 - sp)).astype(BF16)
        dgr_ref[...] = (dmix * yr_ref[...] * sr * (1.0 - sr)).astype(BF16)
        dyp_ref[...] = (dmix * sp).astype(BF16)
        dyr_ref[...] = (dmix * sr).astype(BF16)

    blk = pl.BlockSpec((tm, tn), lambda i, j: (i, j))
    out = jax.ShapeDtypeStruct((T, D_MODEL), BF16)
    return _pallas(
        body, (dx2b, w_o, proj, proj, y_pool, y_rnn), name="out_proj_bwd", grid=(T // tm, D_MODEL // tn),
        in_specs=[pl.BlockSpec((tm, D_MODEL), lambda i, j: (i, 0)), pl.BlockSpec((tn, D_MODEL), lambda i, j: (j, 0)),
                  pl.BlockSpec((tm, tn), lambda i, j: (i, col_gp + j)), pl.BlockSpec((tm, tn), lambda i, j: (i, col_gr + j)), blk, blk],
        out_specs=[blk, blk, blk, blk], out_shape=[out, out, out, out], semantics=("parallel", "parallel"), jobs=jobs)


def _branch_bwd(dyp, dyr, w_pool_out, w_rnn_out, *, tm):
    T = dyp.shape[0]

    def body(dyp_ref, dyr_ref, wp_ref, wr_ref, dpm_ref, dz_ref):
        dpm_ref[...] = _dot_nt(dyp_ref[...], wp_ref[...])
        dz_ref[...] = _dot_nt(dyr_ref[...], wr_ref[...])

    row = pl.BlockSpec((tm, D_MODEL), lambda i: (i, 0))
    return pl.pallas_call(
        body, name="branch_bwd", grid=(T // tm,),
        in_specs=[row, row, pl.BlockSpec((D_POOL, D_MODEL), lambda i: (0, 0)), pl.BlockSpec((D_RNN, D_MODEL), lambda i: (0, 0))],
        out_specs=[pl.BlockSpec((tm, D_POOL), lambda i: (i, 0)), pl.BlockSpec((tm, D_RNN), lambda i: (i, 0))],
        out_shape=[jax.ShapeDtypeStruct((T, D_POOL), F32), jax.ShapeDtypeStruct((T, D_RNN), F32)],
        compiler_params=_params(dimension_semantics=("parallel",)),
    )(dyp, dyr, w_pool_out, w_rnn_out)


def _in_proj_bwd(segs, w, dx2, x, g1, *, tm):
    T = x.shape[0]
    widths = [s.shape[1] for s in segs]
    offs = [sum(widths[:k]) for k in range(len(widths))]
    n = len(segs)

    def body(*refs):
        seg_refs, (w_ref, dx2_ref, x_ref, g_ref, dx_ref, dg_ref) = refs[:n], refs[n:]

        @pl.when(pl.program_id(0) == 0)
        def _():
            dg_ref[...] = jnp.zeros_like(dg_ref)

        dh = _dot_nt(seg_refs[0][...], w_ref[:, offs[0]:offs[0] + widths[0]])
        for k in range(1, n):
            dh += _dot_nt(seg_refs[k][...], w_ref[:, offs[k]:offs[k] + widths[k]])
        dxn, dgp = _rms_bwd(dh, x_ref[...], g_ref[...])
        dg_ref[...] += jnp.sum(dgp, axis=0, keepdims=True)
        dx_ref[...] = dx2_ref[...] + dxn

    row = pl.BlockSpec((tm, D_MODEL), lambda i: (i, 0))
    vec = pl.BlockSpec((1, D_MODEL), lambda i: (0, 0))
    return pl.pallas_call(
        body, name="in_proj_bwd", grid=(T // tm,),
        in_specs=[pl.BlockSpec((tm, wd), lambda i: (i, 0)) for wd in widths]
        + [pl.BlockSpec((D_MODEL, D_IN), lambda i: (0, 0)), row, row, vec],
        out_specs=[row, vec],
        out_shape=[jax.ShapeDtypeStruct((T, D_MODEL), F32), jax.ShapeDtypeStruct((1, D_MODEL), F32)],
        compiler_params=_params(dimension_semantics=("arbitrary",)),
    )(*segs, w, dx2, x, g1)


def _weight_grad(a, segs, *, tm, tn, name, jobs=None):
    T, M = a.shape
    nblk = [s.shape[1] // tn for s in segs]
    first = [sum(nblk[:k]) for k in range(len(segs))]
    n = len(segs)

    def body(a_ref, *refs):
        seg_refs, o_ref = refs[:n], refs[n]
        j = pl.program_id(1)
        for k in range(n):
            @pl.when((j >= first[k]) & (j < first[k] + nblk[k]))
            def _(k=k):
                o_ref[...] = _dot_tn(a_ref[...], seg_refs[k][...])

    def seg_spec(k):
        return pl.BlockSpec((T, tn), lambda i, j: (0, jnp.clip(j - first[k], 0, nblk[k] - 1)))

    (grad,), results = _pallas(
        body, (a, *segs), name=name, grid=(M // tm, sum(nblk)),
        in_specs=[pl.BlockSpec((T, tm), lambda i, j: (0, i))] + [seg_spec(k) for k in range(n)],
        out_specs=[pl.BlockSpec((tm, tn), lambda i, j: (i, j))],
        out_shape=[jax.ShapeDtypeStruct((M, sum(nblk) * tn), F32)],
        semantics=("parallel", "arbitrary"), jobs=jobs or ())
    return grad if jobs is None else (grad, results)


def _pad_front(dst, src, halo):
    dst[pl.ds(0, halo), :] = jnp.zeros((halo, src.shape[1]), F32)

    def fill(i, carry):
        r0 = pl.multiple_of(i * CHUNK, CHUNK)
        dst[pl.ds(r0 + halo, CHUNK), :] = src[pl.ds(r0, CHUNK), :]
        return carry

    lax.fori_loop(0, src.shape[0] // CHUNK, fill, 0)


def _shift_rows(v, k):
    return pltpu.roll(v, k % v.shape[0], axis=0)


def _window_sums(xs, direction):
    s2 = xs + _shift_rows(xs, direction)
    s4 = s2 + _shift_rows(s2, 2 * direction)
    s8 = s4 + _shift_rows(s4, 4 * direction)
    s16 = s8 + _shift_rows(s8, 8 * direction)
    return s2, s4, s8, s16


def _select_window(g, sums):
    s2, s4, s8, s16 = sums
    return jnp.where(g == 0, s2, jnp.where(g == 1, s4, jnp.where(g == 2, s8, s16)))


def _pool_count(g, start, rows):
    t = start + lax.broadcasted_iota(jnp.int32, (rows, 1), 0)
    return jnp.minimum(t + 1, jnp.left_shift(2, g)).astype(F32)


def _pool_fwd(proj, w_grp, scale):
    T = proj.shape[0]
    nchunk = T // CHUNK

    def body(u_ref, w_ref, s_ref, o_ref, upad):
        g = pl.program_id(0)
        _pad_front(upad, u_ref, POOL_HALO)
        w = w_ref[...].astype(BF16)
        scale_row = s_ref[...]

        def chunk(i, carry):
            r0 = pl.multiple_of(i * CHUNK, CHUNK)
            xs = upad[pl.ds(r0, CHUNK + POOL_HALO), :]
            win = _select_window(g, _window_sums(xs, 1))[POOL_HALO:]
            pooled = win / _pool_count(g, r0, CHUNK) - xs[POOL_HALO:]
            o_ref[pl.ds(r0, CHUNK), :] = (_dot(pooled.astype(BF16), w) * scale_row).astype(BF16)
            return carry

        lax.fori_loop(0, nchunk, chunk, 0)

    return pl.pallas_call(
        body, name="pool_fwd", grid=(N_POOL_GROUPS,),
        in_specs=[pl.BlockSpec((T, HEAD), lambda g: (0, g)), pl.BlockSpec((None, HEAD, HEAD), lambda g: (g, 0, 0)),
                  pl.BlockSpec((1, HEAD), lambda g: (0, g))],
        out_specs=pl.BlockSpec((T, HEAD), lambda g: (0, g)),
        out_shape=jax.ShapeDtypeStruct((T, D_POOL), BF16),
        scratch_shapes=[pltpu.VMEM((T + POOL_HALO, HEAD), F32)],
        compiler_params=_params(dimension_semantics=("parallel",)),
    )(proj, w_grp, scale)


def _pool_bwd(proj, dpm, w_grp, scale, jobs=()):
    T = proj.shape[0]
    nchunk = T // CHUNK

    def body(u_ref, dpm_ref, w_ref, s_ref, du_ref, dw_ref, ds_ref, upad, zpad, dpool):
        g = pl.program_id(0)
        _pad_front(upad, u_ref, POOL_HALO)
        zpad[pl.ds(T, POOL_HALO), :] = jnp.zeros((POOL_HALO, HEAD), F32)
        dw_ref[...] = jnp.zeros_like(dw_ref)
        ds_ref[...] = jnp.zeros_like(ds_ref)
        w = w_ref[...].astype(BF16)
        scale_row = s_ref[...]

        def chunk(i, carry):
            r0 = pl.multiple_of(i * CHUNK, CHUNK)
            xs = upad[pl.ds(r0, CHUNK + POOL_HALO), :]
            cnt = _pool_count(g, r0, CHUNK)
            pooled = (_select_window(g, _window_sums(xs, 1))[POOL_HALO:] / cnt - xs[POOL_HALO:]).astype(BF16)
            mixed = _dot(pooled, w)
            d = dpm_ref[pl.ds(r0, CHUNK), :]
            ds_ref[...] += jnp.sum(d * mixed, axis=0, keepdims=True)
            dmixed = (d * scale_row).astype(BF16)
            dw_ref[...] += _dot_tn(pooled, dmixed)
            dp = _dot_nt(dmixed, w)
            dpool[pl.ds(r0, CHUNK), :] = dp
            zpad[pl.ds(r0, CHUNK), :] = dp / cnt
            return carry

        lax.fori_loop(0, nchunk, chunk, 0)

        def chunk2(i, carry):
            r0 = pl.multiple_of(i * CHUNK, CHUNK)
            zs = zpad[pl.ds(r0, CHUNK + POOL_HALO), :]
            win = _select_window(g, _window_sums(zs, -1))[:CHUNK]
            du_ref[pl.ds(r0, CHUNK), :] = (win - dpool[pl.ds(r0, CHUNK), :]).astype(BF16)
            return carry

        lax.fori_loop(0, nchunk, chunk2, 0)

    col = pl.BlockSpec((T, HEAD), lambda g: (0, g))
    return _pallas(
        body, (proj, dpm, w_grp, scale), name="pool_bwd", grid=(N_POOL_GROUPS,),
        in_specs=[col, col, pl.BlockSpec((None, HEAD, HEAD), lambda g: (g, 0, 0)), pl.BlockSpec((1, HEAD), lambda g: (0, g))],
        out_specs=[col, pl.BlockSpec((None, HEAD, HEAD), lambda g: (g, 0, 0)), pl.BlockSpec((1, HEAD), lambda g: (0, g))],
        out_shape=[jax.ShapeDtypeStruct((T, D_POOL), BF16), jax.ShapeDtypeStruct((N_POOL_GROUPS, HEAD, HEAD), F32),
                   jax.ShapeDtypeStruct((1, D_POOL), F32)],
        scratch_shapes=[pltpu.VMEM((T + POOL_HALO, HEAD), F32), pltpu.VMEM((T + POOL_HALO, HEAD), F32), pltpu.VMEM((T, HEAD), F32)],
        semantics=("parallel",), jobs=jobs)


def _conv_taps(xs, cw):
    v = cw[CONV_WIDTH - 1] * xs[SUBLANES:]
    for k in range(CONV_WIDTH - 1):
        v += cw[k] * _shift_rows(xs, CONV_WIDTH - 1 - k)[SUBLANES:]
    return v


def _tap_rows(cw_ref):
    return [cw_ref[k:k + 1, :] for k in range(CONV_WIDTH)]


def _softplus_neg(lam):
    return jnp.maximum(-lam, 0.0) + _log1p(jnp.exp(-jnp.abs(lam)))


def _lru_gates(v, wa, ba, wx, bx, sp):
    vb = v.astype(BF16)
    ra = _sigmoid(_dot(vb, wa) + ba)
    ix = _sigmoid(_dot(vb, wx) + bx)
    log_a = -LRU_C * ra * sp
    a = jnp.exp(log_a)
    sq = jnp.sqrt(-jnp.tanh(log_a) * (a * a + 1.0))
    return ra, ix, a, sq


def _row_bcast(v, r):
    return jnp.broadcast_to(v[r:r + 1, :], v.shape)


def _tile_scans(a, b, direction):
    ri = lax.broadcasted_iota(jnp.int32, a.shape, 0) % SUBLANES
    A, B = a, b
    for s in (1, 2, 4):
        ok = (ri >= s) if direction == 1 else (ri + s < SUBLANES)
        As, Bs = _shift_rows(A, s * direction), _shift_rows(B, s * direction)
        B = jnp.where(ok, A * Bs + B, B)
        A = jnp.where(ok, A * As, A)
    return A, B


TILES_PER_STEP = 8


def _carry_tiles(A_s, B_s, out, ntile, direction):
    out_row = SUBLANES - 1 if direction == 1 else 0

    def step(k, carry):
        for j in range(TILES_PER_STEP):
            t = k * TILES_PER_STEP + j
            r0 = pl.multiple_of((t if direction == 1 else ntile - 1 - t) * SUBLANES, SUBLANES)
            A, B = A_s[pl.ds(r0, SUBLANES), :], B_s[pl.ds(r0, SUBLANES), :]
            out[pl.ds(r0, SUBLANES), :] = A * carry + B
            carry = _row_bcast(A, out_row) * carry + _row_bcast(B, out_row)
        return carry

    lax.fori_loop(0, ntile // TILES_PER_STEP, step, jnp.zeros((SUBLANES, HEAD), F32))


def _rnn_fwd(proj, conv_w, conv_b, w_a, b_a, w_x, b_x, lam, jobs=()):
    T = proj.shape[0]
    nchunk = T // CHUNK
    ntile = T // SUBLANES

    def body(u_ref, ug_ref, cw_ref, cb_ref, wa_ref, ba_ref, wx_ref, bx_ref, lam_ref, h_ref, z_ref, upad, a_s, b_s):
        _pad_front(upad, u_ref, SUBLANES)
        cw, cb = _tap_rows(cw_ref), cb_ref[...]
        wa, wx = wa_ref[...].astype(BF16), wx_ref[...].astype(BF16)
        ba, bx = ba_ref[...], bx_ref[...]
        sp = _softplus_neg(lam_ref[...])

        def chunk(i, carry):
            r0 = pl.multiple_of(i * CHUNK, CHUNK)
            v = _conv_taps(upad[pl.ds(r0, CHUNK + SUBLANES), :], cw) + cb
            _, ix, a, sq = _lru_gates(v, wa, ba, wx, bx, sp)
            a_s[pl.ds(r0, CHUNK), :], b_s[pl.ds(r0, CHUNK), :] = _tile_scans(a, sq * ix * v, 1)
            return carry

        lax.fori_loop(0, nchunk, chunk, 0)
        _carry_tiles(a_s, b_s, h_ref, ntile, 1)

        def chunk3(i, carry):
            r0 = pl.multiple_of(i * CHUNK, CHUNK)
            gl, _ = _gelu_parts(ug_ref[pl.ds(r0, CHUNK), :])
            z_ref[pl.ds(r0, CHUNK), :] = (h_ref[pl.ds(r0, CHUNK), :] * gl).astype(BF16)
            return carry

        lax.fori_loop(0, nchunk, chunk3, 0)

    col = pl.BlockSpec((T, HEAD), lambda h: (0, h))
    vec = pl.BlockSpec((1, HEAD), lambda h: (0, h))
    mat = pl.BlockSpec((None, HEAD, HEAD), lambda h: (h, 0, 0))
    return _pallas(
        body, (proj, proj, conv_w, conv_b, w_a, b_a, w_x, b_x, lam), name="rnn_fwd", grid=(N_RNN_HEADS,),
        in_specs=[pl.BlockSpec((T, HEAD), lambda h: (0, COL_RNN + h)), pl.BlockSpec((T, HEAD), lambda h: (0, COL_GATE + h)),
                  pl.BlockSpec((CONV_WIDTH, HEAD), lambda h: (0, h)), vec, mat, vec, mat, vec, vec],
        out_specs=[col, col],
        out_shape=[jax.ShapeDtypeStruct((T, D_RNN), F32), jax.ShapeDtypeStruct((T, D_RNN), BF16)],
        scratch_shapes=[pltpu.VMEM((T + SUBLANES, HEAD), F32), pltpu.VMEM((T, HEAD), F32), pltpu.VMEM((T, HEAD), F32)],
        semantics=("parallel",), jobs=jobs)


def _rnn_bwd(proj, hr, dz, conv_w, conv_b, w_a, b_a, w_x, b_x, lam, jobs=()):
    T = proj.shape[0]
    nchunk = T // CHUNK
    ntile = T // SUBLANES

    def body(u_ref, ug_ref, h_ref, dz_ref, cw_ref, cb_ref, wa_ref, ba_ref, wx_ref, bx_ref, lam_ref,
             du_ref, dug_ref, dwa_ref, dwx_ref, dba_ref, dbx_ref, dlam_ref, dcb_ref, dcw_ref,
             upad, hpad, apad, v_s, ra_s, ix_s, sq_s, g_s, dvpad, ga_s):
        zero_tile = jnp.zeros((SUBLANES, HEAD), F32)
        _pad_front(upad, u_ref, SUBLANES)
        _pad_front(hpad, h_ref, SUBLANES)
        apad[pl.ds(T, SUBLANES), :] = zero_tile
        dvpad[pl.ds(T, SUBLANES), :] = zero_tile
        for ref in (dwa_ref, dwx_ref, dba_ref, dbx_ref, dlam_ref, dcb_ref, dcw_ref):
            ref[...] = jnp.zeros_like(ref)
        cw, cb = _tap_rows(cw_ref), cb_ref[...]
        wa, wx = wa_ref[...].astype(BF16), wx_ref[...].astype(BF16)
        ba, bx = ba_ref[...], bx_ref[...]
        lam_row = lam_ref[...]
        sp = _softplus_neg(lam_row)

        def chunk(i, carry):
            r0 = pl.multiple_of(i * CHUNK, CHUNK)
            rows = pl.ds(r0, CHUNK)
            h = h_ref[rows, :]
            v = _conv_taps(upad[pl.ds(r0, CHUNK + SUBLANES), :], cw) + cb
            ra, ix, a, sq = _lru_gates(v, wa, ba, wx, bx, sp)
            v_s[rows, :], ra_s[rows, :], ix_s[rows, :], sq_s[rows, :], apad[rows, :] = v, ra, ix, sq, a
            gl, dgl = _gelu_parts(ug_ref[rows, :])
            d = dz_ref[rows, :]
            g_s[rows, :] = d * gl
            dug_ref[rows, :] = (d * h * dgl).astype(BF16)
            return carry

        lax.fori_loop(0, nchunk, chunk, 0)

        def chunk2(i, carry):
            r0 = pl.multiple_of(i * CHUNK, CHUNK)
            rows = pl.ds(r0, CHUNK)
            a_next = _shift_rows(apad[pl.ds(r0, CHUNK + SUBLANES), :], -1)[:CHUNK]
            ga_s[rows, :], g_s[rows, :] = _tile_scans(a_next, g_s[rows, :], -1)
            return carry

        lax.fori_loop(0, nchunk, chunk2, 0)
        _carry_tiles(ga_s, g_s, g_s, ntile, -1)

        def chunk3(i, carry):
            r0 = pl.multiple_of(i * CHUNK, CHUNK)
            rows = pl.ds(r0, CHUNK)
            g = g_s[rows, :]
            h_prev = _shift_rows(hpad[pl.ds(r0, CHUNK + SUBLANES), :], 1)[SUBLANES:]
            v, ra, ix, sq, a = v_s[rows, :], ra_s[rows, :], ix_s[rows, :], sq_s[rows, :], apad[rows, :]
            d_sq = g * ix * v
            d_ix = g * sq * v
            d_la = a * g * h_prev - d_sq * a * a / sq
            dlam_ref[...] += jnp.sum(d_la * ra, axis=0, keepdims=True)
            d_pa = d_la * (-LRU_C) * sp * ra * (1.0 - ra)
            d_px = d_ix * ix * (1.0 - ix)
            vb, d_pab, d_pxb = v.astype(BF16), d_pa.astype(BF16), d_px.astype(BF16)
            dwa_ref[...] += _dot_tn(vb, d_pab)
            dwx_ref[...] += _dot_tn(vb, d_pxb)
            dba_ref[...] += jnp.sum(d_pa, axis=0, keepdims=True)
            dbx_ref[...] += jnp.sum(d_px, axis=0, keepdims=True)
            dv = g * sq * ix + _dot_nt(d_pab, wa) + _dot_nt(d_pxb, wx)
            dvpad[rows, :] = dv
            dcb_ref[...] += jnp.sum(dv, axis=0, keepdims=True)
            xs = upad[pl.ds(r0, CHUNK + SUBLANES), :]
            for k in range(CONV_WIDTH):
                u_k = _shift_rows(xs, CONV_WIDTH - 1 - k)[SUBLANES:] if k < CONV_WIDTH - 1 else xs[SUBLANES:]
                dcw_ref[k:k + 1, :] += jnp.sum(dv * u_k, axis=0, keepdims=True)
            return carry

        lax.fori_loop(0, nchunk, chunk3, 0)
        dlam_ref[...] = dlam_ref[...] * (LRU_C * _sigmoid(-lam_row))

        def chunk4(i, carry):
            r0 = pl.multiple_of(i * CHUNK, CHUNK)
            dvs = dvpad[pl.ds(r0, CHUNK + SUBLANES), :]
            du = cw[CONV_WIDTH - 1] * dvs[:CHUNK]
            for k in range(CONV_WIDTH - 1):
                du += cw[k] * _shift_rows(dvs, -(CONV_WIDTH - 1 - k))[:CHUNK]
            du_ref[pl.ds(r0, CHUNK), :] = du.astype(BF16)
            return carry

        lax.fori_loop(0, nchunk, chunk4, 0)

    col = pl.BlockSpec((T, HEAD), lambda h: (0, h))
    vec = pl.BlockSpec((1, HEAD), lambda h: (0, h))
    mat = pl.BlockSpec((None, HEAD, HEAD), lambda h: (h, 0, 0))
    taps = pl.BlockSpec((CONV_WIDTH, HEAD), lambda h: (0, h))
    vec_out = jax.ShapeDtypeStruct((1, D_RNN), F32)
    mat_out = jax.ShapeDtypeStruct((N_RNN_HEADS, HEAD, HEAD), F32)
    seq = pltpu.VMEM((T, HEAD), F32)
    seq_pad = pltpu.VMEM((T + SUBLANES, HEAD), F32)
    return _pallas(
        body, (proj, proj, hr, dz, conv_w, conv_b, w_a, b_a, w_x, b_x, lam), name="rnn_bwd", grid=(N_RNN_HEADS,),
        in_specs=[pl.BlockSpec((T, HEAD), lambda h: (0, COL_RNN + h)), pl.BlockSpec((T, HEAD), lambda h: (0, COL_GATE + h)),
                  col, col, taps, vec, mat, vec, mat, vec, vec],
        out_specs=[col, col, mat, mat, vec, vec, vec, vec, taps],
        out_shape=[jax.ShapeDtypeStruct((T, D_RNN), BF16), jax.ShapeDtypeStruct((T, D_RNN), BF16), mat_out, mat_out,
                   vec_out, vec_out, vec_out, vec_out, jax.ShapeDtypeStruct((CONV_WIDTH, D_RNN), F32)],
        scratch_shapes=[seq_pad, seq_pad, seq_pad, seq, seq, seq, seq, seq, seq_pad, seq],
        semantics=("parallel",), jobs=jobs)


GROUP_FFN = ["w_ffn_out", "w_ffn_in"]
GROUP_MIX = ["w_o", "w_pool_out", "w_rnn_out"]
GROUP_IN = ["w_in"]


def _step(x, target, s, full, conv_w_mine, place):
    T = x.shape[0]
    tm, tr = min(T, 512), min(T, 256)
    c1 = place[1:]
    full = dict(full)

    def gathered(names, results):
        full.update(zip(names, results))

    (full["w_in"], conv_w), = _run_jobs([_gather_job(full, ["w_in"], conv_w_mine)], "gather_w_in")
    early = ["w_pool_out", "w_rnn_out", "w_o", "w_ffn_out"]
    (proj, h1), (res,) = _norm_matmul(x, s["norm_mix"], full["w_in"], tm=tm, tn=1536, name="in_proj", jobs=[_gather_job(full, early)])
    gathered(early, res)
    pm = _pool_fwd(proj, s["w_pool_grp"], s["pool_scale"])
    (hr, z), (res,) = _rnn_fwd(proj, conv_w, s["conv_b"], s["w_rg_a"], s["b_rg_a"], s["w_rg_x"], s["b_rg_x"], s["lru_lambda"],
                               jobs=[_gather_job(full, ["w_ffn_in"])])
    gathered(["w_ffn_in"], res)
    y_pool, y_rnn, mix = _branch_mix(pm, z, full["w_pool_out"], full["w_rnn_out"], proj, tm=tm, tn=512)
    x2 = _out_proj_residual(mix, full["w_o"], x, tm=tm)
    gate, up, act, h2 = _ffn_in(x2, s["norm_ffn"], full["w_ffn_in"], tm=tm, tn=1408)
    dx3, dx3b, sq_cols, g_norm_final = _ffn_out_loss(act, full["w_ffn_out"], x2, s["norm_final"], target, tm=tr)

    g = {"norm_final": g_norm_final}

    def chip_sums(names, from_sibling):
        sums = {name: _chip_sum(name, g[name], got, c1) for name, got in zip(names, from_sibling)}
        return {name: v[0] for name, v in sums.items()}, {name: v[1] for name, v in sums.items()}

    def final_sums(names, sums, from_chips):
        return {name: _final_sum(name, sums[name], got, place) for name, got in zip(names, from_chips)}

    dgate, dup = _ffn_out_bwd(dx3b, full["w_ffn_out"], gate, up, tm=tm, tn=1408)
    g["w_ffn_out"] = _weight_grad(act, [dx3b], tm=256, tn=D_MODEL, name="w_ffn_out_grad")
    dx2, dx2b, g["norm_ffn"] = _ffn_in_bwd(dgate, dup, full["w_ffn_in"], dx3, x2, s["norm_ffn"], tm=tr)
    g["w_ffn_in"] = _weight_grad(h2, [dgate, dup], tm=D_MODEL, tn=256, name="w_ffn_in_grad")
    (dgp, dgr, dyp, dyr), (res,) = _out_proj_bwd(dx2b, full["w_o"], proj, y_pool, y_rnn, tm=tm, tn=512,
                                                 jobs=[_sibling_job(g, GROUP_FFN)])
    sums_ffn, sums_ffn_bf16 = chip_sums(GROUP_FFN, res)
    g["w_o"] = _weight_grad(mix, [dx2b], tm=D_MODEL, tn=256, name="w_o_grad")
    dpm, dz = _branch_bwd(dyp, dyr, full["w_pool_out"], full["w_rnn_out"], tm=tm)
    g["w_pool_out"] = _weight_grad(pm, [dyp], tm=D_POOL, tn=256, name="w_pool_out_grad")
    g["w_rnn_out"] = _weight_grad(z, [dyr], tm=D_RNN, tn=256, name="w_rnn_out_grad")
    (dupool, g["w_pool_grp"], g["pool_scale"]), (res,) = _pool_bwd(proj, dpm, s["w_pool_grp"], s["pool_scale"],
                                                                   jobs=[_sibling_job(g, GROUP_MIX)])
    sums_mix, sums_mix_bf16 = chip_sums(GROUP_MIX, res)
    ((durnn, dugate, g["w_rg_a"], g["w_rg_x"], g["b_rg_a"], g["b_rg_x"], g["lru_lambda"], g["conv_b"], g["conv_w"]),
     (res,)) = _rnn_bwd(proj, hr, dz, conv_w, s["conv_b"], s["w_rg_a"], s["b_rg_a"], s["w_rg_x"], s["b_rg_x"], s["lru_lambda"],
                        jobs=[_chips_job(sums_ffn_bf16, GROUP_FFN)])
    shards = final_sums(GROUP_FFN, sums_ffn, res)
    segs = [dupool, durnn, dugate, dgp, dgr]
    grad_x, g["norm_mix"] = _in_proj_bwd(segs, full["w_in"], dx2, x, s["norm_mix"], tm=tr)
    g["w_in"], (res, joined) = _weight_grad(h1, segs, tm=D_MODEL, tn=256, name="w_in_grad",
                                           jobs=[_chips_job(sums_mix_bf16, GROUP_MIX), _join_job(shards, GROUP_FFN)])
    grads = dict(zip(GROUP_FFN, joined))
    shards = final_sums(GROUP_MIX, sums_mix, res)
    (res,) = _run_jobs([_sibling_job(g, GROUP_IN)], "w_in_exchange_sibling")
    sums_in, sums_in_bf16 = chip_sums(GROUP_IN, res)

    vec_rows = [g[name] if name != "pool_scale" else jnp.pad(g[name], ((0, 0), (0, D_MODEL - D_POOL))) for name in VEC_ITEMS]
    vec_rows += [g["conv_w"], sq_cols, jnp.zeros((VEC_ROWS - len(VEC_ITEMS) - CONV_WIDTH - 1, D_MODEL), F32)]
    vec = jnp.concatenate(vec_rows, axis=0).reshape(VEC_ROWS, N_DEV, HEAD).transpose(1, 0, 2)
    mat = jnp.concatenate([g[name].reshape(-1, HEAD) for name in MAT_ITEMS], axis=0).reshape(N_DEV, -1, HEAD)
    (vec, mat), (res, joined) = _all_reduce_small([vec, mat], jobs=[_chips_job(sums_in_bf16, GROUP_IN), _join_job(shards, GROUP_MIX)])
    grads.update(zip(GROUP_MIX, joined))
    (joined,) = _run_jobs([_join_job(final_sums(GROUP_IN, sums_in, res), GROUP_IN)], "w_in_join_halves")
    grads.update(zip(GROUP_IN, joined))

    vec = vec.transpose(1, 0, 2).reshape(VEC_ROWS, D_MODEL)
    mat = mat.reshape(-1, HEAD)
    for k, name in enumerate(VEC_ITEMS):
        grads[name] = vec[k:k + 1, :s[name].shape[1]]
    grads["conv_w"] = vec[len(VEC_ITEMS):len(VEC_ITEMS) + CONV_WIDTH]
    row = 0
    for name in MAT_ITEMS:
        rows = s[name].shape[0] * HEAD
        grads[name] = mat[row:row + rows]
        row += rows
    return vec[len(VEC_ITEMS) + CONV_WIDTH], grad_x, grads


LARGE = {"w_in": "col", "w_pool_out": "col", "w_rnn_out": "row", "w_o": "row", "w_ffn_in": "col", "w_ffn_out": "row"}
LARGE_SHAPE = {"w_in": (D_MODEL, D_IN), "w_pool_out": (D_POOL, D_MODEL), "w_rnn_out": (D_RNN, D_MODEL),
               "w_o": (D_MODEL, D_MODEL), "w_ffn_in": (D_MODEL, 2 * D_FF), "w_ffn_out": (D_FF, D_MODEL)}


def _place():
    x, y, c = lax.axis_index("x"), lax.axis_index("y"), lax.axis_index("c")
    return 2 * x + y, c


def _chip_device(chip, c):
    return (chip // 2, chip % 2, c)


def _chip_window(ref, kind, shape, chip, half=None):
    K, N = shape
    if kind == "col":
        rows = slice(None) if half is None else pl.ds(half * (K // 2), K // 2)
        return ref.at[rows, pl.ds(chip * (N // N_CHIPS), N // N_CHIPS)]
    ks = K // N_CHIPS
    if half is None:
        return ref.at[pl.ds(chip * ks, ks), :]
    return ref.at[pl.ds(chip * ks + half * (ks // 2), ks // 2), :]


def _row_half(ref, half):
    rows = ref.shape[0] // 2
    return ref.at[pl.ds(half * rows, rows), :]


def _remote(win_src, win_dst, send_sems, recv_sems, idx, to):
    return pltpu.make_async_remote_copy(src_ref=win_src, dst_ref=win_dst, send_sem=send_sems.at[idx], recv_sem=recv_sems.at[idx],
                                        device_id=to, device_id_type=MESH)


def _gather_job(full, names, conv_w_full=None):
    n = len(names)
    cw_cols = D_RNN // N_CHIPS

    def windows(refs, chip, half):
        return [_chip_window(refs[k], LARGE[name], LARGE_SHAPE[name], chip, half) for k, name in enumerate(names)]

    def ici_copies(refs, send_sems, recv_sems, src_chip, dst_chip, c, r):
        wins = windows(refs, src_chip, c)
        if conv_w_full is not None:
            wins.append(refs[n].at[:, pl.ds(src_chip * cw_cols, cw_cols)])
        return [_remote(win, win, send_sems, recv_sems, (k, r), _chip_device(dst_chip, c)) for k, win in enumerate(wins)]

    def forwards(refs, send_sems, recv_sems, src_chip, half, to_core, chip, r):
        return [_remote(win, win, send_sems, recv_sems, (k, 3 + r), _chip_device(chip, to_core))
                for k, win in enumerate(windows(refs, src_chip, half))]

    def start(ins, outs, send_sems, recv_sems):
        chip, c = _place()
        for r in range(3):
            for cp in ici_copies(outs, send_sems, recv_sems, chip, chip ^ (r + 1), c, r):
                cp.start()

    def finish(ins, outs, send_sems, recv_sems):
        chip, c = _place()
        for r in range(3):
            for cp in ici_copies(outs, send_sems, recv_sems, chip ^ (r + 1), chip, c, r):
                cp.wait_recv()
            for cp in forwards(outs, send_sems, recv_sems, chip ^ (r + 1), c, 1 - c, chip, r):
                cp.start()
        for r in range(3):
            for cp in forwards(outs, send_sems, recv_sems, chip ^ (r + 1), 1 - c, c, chip, r):
                cp.wait_recv()
            for cp in ici_copies(outs, send_sems, recv_sems, chip, chip ^ (r + 1), c, r):
                cp.wait_send()
            for cp in forwards(outs, send_sems, recv_sems, chip ^ (r + 1), c, 1 - c, chip, r):
                cp.wait_send()

    arrays = [full[name] for name in names] + ([conv_w_full] if conv_w_full is not None else [])
    return _Job(arrays, [jax.ShapeDtypeStruct(a.shape, a.dtype) for a in arrays], {k: k for k in range(len(arrays))},
                (len(arrays), 6), start, finish)


def _core_halves(ref, kind, shape, c):
    return [_chip_window(ref, kind, shape, chip, c) for chip in range(N_CHIPS)]


def _sibling_job(grads, names):
    def start(ins, outs, send_sems, recv_sems):
        chip, c = _place()
        for k, name in enumerate(names):
            kind, shape = LARGE[name], LARGE_SHAPE[name]
            if kind == "col":
                pairs = [(_row_half(ins[k], 1 - c), outs[k])]
            else:
                rows = shape[0] // N_DEV
                pairs = [(win, outs[k].at[pl.ds(j * rows, rows), :]) for j, win in enumerate(_core_halves(ins[k], kind, shape, 1 - c))]
            for src, dst in pairs:
                _remote(src, dst, send_sems, recv_sems, k, _chip_device(chip, 1 - c)).start()

    def finish(ins, outs, send_sems, recv_sems):
        chip, c = _place()
        for k in range(len(names)):
            _remote(outs[k], outs[k], send_sems, recv_sems, k, _chip_device(chip, 1 - c)).wait()

    return _Job([grads[name] for name in names],
                [jax.ShapeDtypeStruct((LARGE_SHAPE[name][0] // 2, LARGE_SHAPE[name][1]), F32) for name in names], {},
                (len(names),), start, finish)


def _chip_sum(name, g, got, c):
    kind, (K, N) = LARGE[name], LARGE_SHAPE[name]
    rows = K // N_DEV

    def body(c_ref, g_ref, got_ref, o_ref, ob_ref):
        total = g_ref[...] + got_ref[...]
        o_ref[...] = total
        ob_ref[...] = total.astype(BF16)

    if kind == "col":
        mine = pl.BlockSpec((rows, N), lambda j, c_ref: (j + N_CHIPS * c_ref[0], 0))
    else:
        mine = pl.BlockSpec((rows, N), lambda j, c_ref: (2 * j + c_ref[0], 0))
    blk = pl.BlockSpec((rows, N), lambda j, c_ref: (j, 0))
    return pl.pallas_call(
        body, name=name + "_chip_sum",
        grid_spec=pltpu.PrefetchScalarGridSpec(num_scalar_prefetch=1, grid=(N_CHIPS,), in_specs=[mine, blk], out_specs=[blk, blk]),
        out_shape=[jax.ShapeDtypeStruct((K // 2, N), F32), jax.ShapeDtypeStruct((K // 2, N), BF16)],
        compiler_params=_params(dimension_semantics=("parallel",)),
    )(c, g, got)


def _piece(ref, kind, shape, chip):
    K, N = shape
    if kind == "col":
        return ref.at[:, pl.ds(chip * (N // N_CHIPS), N // N_CHIPS)]
    return ref.at[pl.ds(chip * (K // N_DEV), K // N_DEV), :]


def _piece_shape(name):
    kind, (K, N) = LARGE[name], LARGE_SHAPE[name]
    return (K // 2, N // N_CHIPS) if kind == "col" else (K // N_DEV, N)


def _chips_job(sums, names):
    def copies(ins, outs, send_sems, recv_sems):
        chip, c = _place()
        return [_remote(_piece(ins[k], LARGE[name], LARGE_SHAPE[name], chip ^ (r + 1)), outs[k].at[r], send_sems, recv_sems, (k, r),
                        _chip_device(chip ^ (r + 1), c)) for k, name in enumerate(names) for r in range(3)]

    def start(*refs):
        for cp in copies(*refs):
            cp.start()

    def finish(*refs):
        for cp in copies(*refs):
            cp.wait()

    return _Job([sums[name] for name in names], [jax.ShapeDtypeStruct((3,) + _piece_shape(name), BF16) for name in names], {},
                (len(names), 3), start, finish)


def _final_sum(name, chip_sum, got, place):
    kind = LARGE[name]
    rows, cols = _piece_shape(name)

    def body(place_ref, s_ref, got_ref, o_ref):
        o_ref[...] = ((s_ref[...] + got_ref[0].astype(F32)) + got_ref[1].astype(F32)) + got_ref[2].astype(F32)

    if kind == "col":
        mine = pl.BlockSpec((rows, cols), lambda i, place_ref: (0, place_ref[0]))
    else:
        mine = pl.BlockSpec((rows, cols), lambda i, place_ref: (place_ref[0], 0))
    return pl.pallas_call(
        body, name=name + "_final_sum",
        grid_spec=pltpu.PrefetchScalarGridSpec(
            num_scalar_prefetch=1, grid=(1,), in_specs=[mine, pl.BlockSpec((3, rows, cols), lambda i, place_ref: (0, 0, 0))],
            out_specs=pl.BlockSpec((rows, cols), lambda i, place_ref: (place_ref[1], 0))),
        out_shape=jax.ShapeDtypeStruct((2 * rows, cols), F32),
        compiler_params=_params(dimension_semantics=("arbitrary",)),
    )(place, chip_sum, got)


def _join_job(shards, names):
    def half_copy(outs, send_sems, recv_sems, k, mine):
        chip, c = _place()
        win = _row_half(outs[k], c if mine else 1 - c)
        return _remote(win, win, send_sems, recv_sems, k, _chip_device(chip, 1 - c))

    def start(ins, outs, send_sems, recv_sems):
        for k in range(len(names)):
            half_copy(outs, send_sems, recv_sems, k, True).start()

    def finish(ins, outs, send_sems, recv_sems):
        for k in range(len(names)):
            half_copy(outs, send_sems, recv_sems, k, True).wait_send()
            half_copy(outs, send_sems, recv_sems, k, False).wait_recv()

    arrays = [shards[name] for name in names]
    return _Job(arrays, [jax.ShapeDtypeStruct(a.shape, F32) for a in arrays], {k: k for k in range(len(arrays))},
                (len(arrays),), start, finish)


VEC_ROWS = 16


def _all_reduce_small(slabs, jobs=()):
    n = len(slabs)

    def body(*refs):
        in_refs, out_refs, got_refs = refs[:n], refs[n:2 * n], refs[2 * n:3 * n]
        send_sems, recv_sems = refs[3 * n:]
        x, y, c = lax.axis_index("x"), lax.axis_index("y"), lax.axis_index("c")
        me = 4 * x + 2 * y + c

        def remote(src, dst, k, phase, r):
            other = me ^ r
            return pltpu.make_async_remote_copy(src_ref=src, dst_ref=dst, send_sem=send_sems.at[k, phase, r],
                                                recv_sem=recv_sems.at[k, phase, r],
                                                device_id=(other // 4, (other // 2) % 2, other % 2), device_id_type=MESH)

        scatter = [remote(in_refs[k].at[me ^ r], got_refs[k].at[r], k, 0, r) for r in range(1, N_DEV) for k in range(n)]
        for cp in scatter:
            cp.start()
        for cp in scatter:
            cp.wait()
        for k in range(n):
            total = in_refs[k][me]
            for r in range(1, N_DEV):
                total = total + got_refs[k][r]
            out_refs[k][me] = total
        gather = [remote(out_refs[k].at[me], out_refs[k].at[me], k, 1, r) for r in range(1, N_DEV) for k in range(n)]
        for cp in gather:
            cp.start()
        for r in range(1, N_DEV):
            for k in range(n):
                remote(out_refs[k].at[me ^ r], out_refs[k].at[me ^ r], k, 1, r).wait_recv()
        for cp in gather:
            cp.wait_send()

    return _pallas(
        body, slabs, name="all_reduce_small", grid=(), in_specs=[VMEM] * n, out_specs=[VMEM] * n,
        out_shape=[jax.ShapeDtypeStruct(s.shape, F32) for s in slabs],
        scratch_shapes=[pltpu.VMEM(s.shape, F32) for s in slabs]
        + [pltpu.SemaphoreType.DMA((n, 2, N_DEV)), pltpu.SemaphoreType.DMA((n, 2, N_DEV))], jobs=jobs)


def _cast_into_whole(w, name, place):
    rows, cols = w.shape
    tr = rows // 2

    def body(place_ref, w_ref, o_ref):
        o_ref[...] = w_ref[...].astype(BF16)

    if LARGE[name] == "col":
        window = pl.BlockSpec((tr, cols), lambda i, place_ref: (i, place_ref[0]))
    else:
        window = pl.BlockSpec((tr, cols), lambda i, place_ref: (2 * place_ref[0] + i, 0))
    return pl.pallas_call(
        body, name=name + "_cast",
        grid_spec=pltpu.PrefetchScalarGridSpec(num_scalar_prefetch=1, grid=(2,),
                                               in_specs=[pl.BlockSpec((tr, cols), lambda i, place_ref: (i, 0))], out_specs=window),
        out_shape=jax.ShapeDtypeStruct(LARGE_SHAPE[name], BF16),
        compiler_params=_params(dimension_semantics=("parallel",)))(place, w)


def _adamw_math(w, g, m, v):
    m = ADAM_B1 * m + (1.0 - ADAM_B1) * g
    v = ADAM_B2 * v + (1.0 - ADAM_B2) * (g * g)
    m_hat = m / (1.0 - ADAM_B1 ** ADAM_STEP)
    v_hat = v / (1.0 - ADAM_B2 ** ADAM_STEP)
    delta = -ADAM_LR * (m_hat / (jnp.sqrt(v_hat) + ADAM_EPS) + ADAM_WD * w)
    return delta, m, v


def _adamw_large(w, g, m, v, name):
    rows, cols = w.shape
    tr = rows // 4

    def body(w_ref, g_ref, m_ref, v_ref, d_ref, mo_ref, vo_ref):
        d_ref[...], mo_ref[...], vo_ref[...] = _adamw_math(w_ref[...], g_ref[...], m_ref[...], v_ref[...])

    blk = pl.BlockSpec((tr, cols), lambda i: (i, 0))
    out = jax.ShapeDtypeStruct(w.shape, F32)
    return pl.pallas_call(body, name=name + "_adamw", grid=(4,), in_specs=[blk] * 4, out_specs=[blk] * 3, out_shape=[out] * 3,
                          compiler_params=_params(dimension_semantics=("parallel",)))(w, g, m, v)


def _adamw_small(ws, gs, ms, vs):
    n = len(ws)

    def body(*refs):
        for k in range(n):
            w_ref, g_ref, m_ref, v_ref = (refs[q * n + k] for q in range(4))
            d_ref, mo_ref, vo_ref = (refs[(4 + q) * n + k] for q in range(3))
            d_ref[...], mo_ref[...], vo_ref[...] = _adamw_math(w_ref[...], g_ref[...], m_ref[...], v_ref[...])

    out = [jax.ShapeDtypeStruct(w.shape, F32) for w in ws]
    res = pl.pallas_call(body, name="small_adamw", in_specs=[VMEM] * (4 * n), out_specs=[VMEM] * (3 * n), out_shape=out * 3,
                         compiler_params=_params())(*ws, *gs, *ms, *vs)
    return res[:n], res[n:2 * n], res[2 * n:]


WEIGHTS = ["norm_mix", "w_in", "w_pool_grp", "pool_scale", "w_pool_out", "conv_w", "conv_b", "w_rg_a", "b_rg_a", "w_rg_x",
           "b_rg_x", "lru_lambda", "w_rnn_out", "w_o", "norm_ffn", "w_ffn_in", "w_ffn_out", "norm_final"]
VEC_ITEMS = ["norm_mix", "norm_ffn", "norm_final", "pool_scale", "conv_b", "lru_lambda", "b_rg_a", "b_rg_x"]
MAT_ITEMS = ["w_pool_grp", "w_rg_a", "w_rg_x"]


def _as2d(name, a):
    if name in MAT_ITEMS:
        return a.reshape(-1, HEAD, HEAD)
    if name == "conv_w":
        return a.reshape(CONV_WIDTH, -1)
    return a.reshape(1, -1)


def kernel(x, norm_mix, w_in, w_pool_grp, pool_scale, w_pool_out, conv_w, conv_b, w_rg_a, b_rg_a, w_rg_x, b_rg_x, lru_lambda, w_rnn_out, w_o, norm_ffn, w_ffn_in, w_ffn_out, norm_final, loss_target, m_norm_mix, m_w_in, m_w_pool_grp, m_pool_scale, m_w_pool_out, m_conv_w, m_conv_b, m_w_rg_a, m_b_rg_a, m_w_rg_x, m_b_rg_x, m_lru_lambda, m_w_rnn_out, m_w_o, m_norm_ffn, m_w_ffn_in, m_w_ffn_out, m_norm_final, v_norm_mix, v_w_in, v_w_pool_grp, v_pool_scale, v_w_pool_out, v_conv_w, v_conv_b, v_w_rg_a, v_b_rg_a, v_w_rg_x, v_b_rg_x, v_lru_lambda, v_w_rnn_out, v_w_o, v_norm_ffn, v_w_ffn_in, v_w_ffn_out, v_norm_final):
    given = dict(locals())
    w = {name: given[name] for name in WEIGHTS}
    m = {name: given["m_" + name] for name in WEIGHTS}
    v = {name: given["v_" + name] for name in WEIGHTS}
    chip, c = _place()

    place = jnp.stack([chip, c]).astype(jnp.int32)
    conv_cols = w["conv_w"].shape[-1]
    conv_w_mine = lax.dynamic_update_slice_in_dim(jnp.zeros((CONV_WIDTH, D_RNN), F32), w["conv_w"][0], chip * conv_cols, axis=1)
    full = {name: _cast_into_whole(w[name][0], name, place) for name in LARGE}
    small = {name: _as2d(name, w[name]) for name in WEIGHTS if name not in LARGE and name != "conv_w"}
    sq_cols, grad_x, grads = _step(x[0], loss_target[0], small, full, conv_w_mine, place)
    loss = 0.5 / D_MODEL * jnp.sum(sq_cols)
    grads["conv_w"] = lax.dynamic_slice_in_dim(grads["conv_w"], chip * conv_cols, conv_cols, axis=1)

    delta, new_m, new_v = {}, {}, {}
    for name in LARGE:
        delta[name], new_m[name], new_v[name] = _adamw_large(w[name][0], grads[name], m[name][0], v[name][0], name)
    small_names = [name for name in WEIGHTS if name not in LARGE]
    flat = lambda d: [d[name].reshape(grads[name].shape) for name in small_names]
    ds, mo, vo = _adamw_small(flat(w), [grads[name] for name in small_names], flat(m), flat(v))
    for k, name in enumerate(small_names):
        delta[name], new_m[name], new_v[name] = ds[k], mo[k], vo[k]

    shaped = lambda d: [d[name].reshape(w[name].shape) for name in WEIGHTS]
    return (loss, grad_x[None], *shaped(grads), *shaped(delta), *shaped(new_m), *shaped(new_v))
```

```python
import functools
import math

import jax
import jax.numpy as jnp
from jax import lax
from jax.experimental import pallas as pl
from jax.experimental.pallas import tpu as pltpu

F32 = jnp.float32
BF16 = jnp.bfloat16

D_MODEL = 1024
D_POOL = 512
N_POOL_GROUPS = 4
D_RNN = 1024
N_RNN_HEADS = 8
HEAD = 128
CONV_WIDTH = 4
LRU_C = 8.0
D_FF = 2816
D_IN = D_POOL + 2 * D_RNN + 2 * D_MODEL
NORM_EPS = 1e-6
COL_RNN = D_POOL // HEAD
COL_GATE = (D_POOL + D_RNN) // HEAD

ADAM_LR = 0.001
ADAM_B1 = 0.9
ADAM_B2 = 0.999
ADAM_EPS = 1e-08
ADAM_WD = 0.01
ADAM_STEP = 10

N_CHIPS = 4
N_DEV = 8
MESH = pl.DeviceIdType.MESH
ANY = pl.BlockSpec(memory_space=pl.ANY)
VMEM = pl.BlockSpec(memory_space=pltpu.VMEM)
VMEM_LIMIT_BYTES = 60 * 1024 * 1024
SUBLANES = 8
POOL_HALO = 16
CHUNK = 512

GELU_C = math.sqrt(2.0 / math.pi)
GELU_A = 0.044715


def _params(**kw):
    return pltpu.CompilerParams(vmem_limit_bytes=VMEM_LIMIT_BYTES, **kw)


def _sigmoid(x):
    return 0.5 * jnp.tanh(0.5 * x) + 0.5


def _log1p(y):
    u = 1.0 + y
    d = u - 1.0
    return jnp.where(d == 0.0, y, jnp.log(u) * (y / jnp.where(d == 0.0, 1.0, d)))


def _gelu_parts(x):
    x2 = x * x
    th = jnp.tanh(GELU_C * (x + GELU_A * x * x2))
    g = 0.5 * x * (1.0 + th)
    dg = 0.5 * (1.0 + th) + 0.5 * x * (1.0 - th * th) * GELU_C * (1.0 + 3.0 * GELU_A * x2)
    return g, dg


def _dot(a, b):
    return jnp.dot(a, b, preferred_element_type=F32)


def _dot_nt(a, b):
    return lax.dot_general(a, b, (((1,), (1,)), ((), ())), preferred_element_type=F32)


def _dot_tn(a, b):
    return lax.dot_general(a, b, (((0,), (0,)), ((), ())), preferred_element_type=F32)


def _rms_scale(xv):
    return lax.rsqrt(jnp.mean(xv * xv, axis=-1, keepdims=True) + NORM_EPS)


def _rms_bwd(dy, xv, g):
    r = _rms_scale(xv)
    xh = xv * r
    dyg = dy * g
    dx = r * (dyg - xh * jnp.mean(dyg * xh, axis=-1, keepdims=True))
    return dx, dy * xh


class _Job:
    def __init__(self, inputs, out_shapes, aliases, sem_shape, start, finish):
        self.inputs, self.out_shapes, self.aliases, self.sem_shape = list(inputs), list(out_shapes), dict(aliases), sem_shape
        self.start, self.finish = start, finish


def _pallas(body, operands, *, name, grid, in_specs, out_specs, out_shape, scratch_shapes=(), semantics=None, jobs=()):
    n_in, n_out, n_scr = len(in_specs), len(out_specs), len(scratch_shapes)
    job_in = [a for job in jobs for a in job.inputs]
    job_out = [s for job in jobs for s in job.out_shapes]
    aliases, i0, o0 = {}, n_in, n_out
    for job in jobs:
        aliases.update({i0 + i: o0 + o for i, o in job.aliases.items()})
        i0, o0 = i0 + len(job.inputs), o0 + len(job.out_shapes)

    def whole(*refs):
        ins, j_ins = refs[:n_in], refs[n_in:n_in + len(job_in)]
        outs = refs[n_in + len(job_in):][:n_out]
        j_outs = refs[n_in + len(job_in) + n_out:][:len(job_out)]
        rest = refs[n_in + len(job_in) + n_out + len(job_out):]
        scr, sems = rest[:n_scr], rest[n_scr:]

        def run(phase):
            i, o = 0, 0
            for k, job in enumerate(jobs):
                getattr(job, phase)(j_ins[i:i + len(job.inputs)], j_outs[o:o + len(job.out_shapes)], sems[2 * k], sems[2 * k + 1])
                i, o = i + len(job.inputs), o + len(job.out_shapes)

        def at(step_of, phase):
            if not jobs:
                return
            if not grid:
                run(phase)
                return
            cond = functools.reduce(jnp.logical_and, [pl.program_id(d) == step_of(d) for d in range(len(grid))])
            pl.when(cond)(functools.partial(run, phase))

        at(lambda d: 0, "start")
        body(*ins, *outs, *scr)
        at(lambda d: grid[d] - 1, "finish")

    res = pl.pallas_call(
        whole, name=name, grid=grid, in_specs=list(in_specs) + [ANY] * len(job_in), out_specs=list(out_specs) + [ANY] * len(job_out),
        out_shape=list(out_shape) + job_out, input_output_aliases=aliases,
        scratch_shapes=list(scratch_shapes) + [pltpu.SemaphoreType.DMA(job.sem_shape) for job in jobs for _ in range(2)],
        compiler_params=_params(dimension_semantics=semantics, has_side_effects=bool(jobs)),
    )(*operands, *job_in)
    per_job, o = [], n_out
    for job in jobs:
        per_job.append(res[o:o + len(job.out_shapes)])
        o += len(job.out_shapes)
    return res[:n_out], per_job


def _run_jobs(jobs, name):
    return _pallas(lambda: None, [], name=name, grid=(), in_specs=[], out_specs=[], out_shape=[], jobs=jobs)[1]


NORM_ROWS = 256


def _norm_rows(x_ref, g_ref, h_ref):
    g = g_ref[...]

    def rows(i, carry):
        r = pl.ds(pl.multiple_of(i * NORM_ROWS, NORM_ROWS), NORM_ROWS)
        xv = x_ref[r, :]
        h_ref[r, :] = (xv * _rms_scale(xv) * g).astype(BF16)
        return carry

    lax.fori_loop(0, x_ref.shape[0] // NORM_ROWS, rows, 0)


def _norm_matmul(x, g, w, *, tm, tn, name, jobs=()):
    T, K = x.shape
    N = w.shape[1]

    def body(x_ref, g_ref, w_ref, o_ref, h_ref):
        @pl.when(pl.program_id(1) == 0)
        def _():
            _norm_rows(x_ref, g_ref, h_ref)

        o_ref[...] = _dot(h_ref[...], w_ref[...])

    return _pallas(
        body, (x, g, w), name=name, grid=(T // tm, N // tn),
        in_specs=[pl.BlockSpec((tm, K), lambda i, j: (i, 0)), pl.BlockSpec((1, K), lambda i, j: (0, 0)),
                  pl.BlockSpec((K, tn), lambda i, j: (0, j))],
        out_specs=[pl.BlockSpec((tm, tn), lambda i, j: (i, j)), pl.BlockSpec((tm, K), lambda i, j: (i, 0))],
        out_shape=[jax.ShapeDtypeStruct((T, N), F32), jax.ShapeDtypeStruct((T, K), BF16)],
        semantics=("parallel", "arbitrary"), jobs=jobs)


def _ffn_in(x2, g, w, *, tm, tn):
    T, K = x2.shape
    nb = D_FF // tn

    def body(x_ref, g_ref, wg_ref, wu_ref, gate_ref, up_ref, act_ref, h_ref):
        @pl.when(pl.program_id(1) == 0)
        def _():
            _norm_rows(x_ref, g_ref, h_ref)

        h = h_ref[...]
        gate = _dot(h, wg_ref[...])
        up = _dot(h, wu_ref[...])
        gate_ref[...] = gate
        up_ref[...] = up
        act_ref[...] = (gate * _sigmoid(gate) * up).astype(BF16)

    blk = pl.BlockSpec((tm, tn), lambda i, j: (i, j))
    return pl.pallas_call(
        body, name="ffn_in", grid=(T // tm, nb),
        in_specs=[pl.BlockSpec((tm, K), lambda i, j: (i, 0)), pl.BlockSpec((1, K), lambda i, j: (0, 0)),
                  pl.BlockSpec((K, tn), lambda i, j: (0, j)), pl.BlockSpec((K, tn), lambda i, j: (0, j + nb))],
        out_specs=[blk, blk, blk, pl.BlockSpec((tm, K), lambda i, j: (i, 0))],
        out_shape=[jax.ShapeDtypeStruct((T, D_FF), F32), jax.ShapeDtypeStruct((T, D_FF), F32),
                   jax.ShapeDtypeStruct((T, D_FF), BF16), jax.ShapeDtypeStruct((T, K), BF16)],
        compiler_params=_params(dimension_semantics=("parallel", "arbitrary")),
    )(x2, g, w, w)


def _branch_mix(pm, z, w_pool_out, w_rnn_out, proj, *, tm, tn):
    T = pm.shape[0]
    col_gp = (D_POOL + 2 * D_RNN) // tn
    col_gr = col_gp + D_MODEL // tn

    def body(pm_ref, z_ref, wp_ref, wr_ref, gp_ref, gr_ref, yp_ref, yr_ref, mix_ref):
        yp = _dot(pm_ref[...], wp_ref[...])
        yr = _dot(z_ref[...], wr_ref[...])
        yp_ref[...] = yp
        yr_ref[...] = yr
        mix_ref[...] = (_sigmoid(gp_ref[...]) * yp + _sigmoid(gr_ref[...]) * yr).astype(BF16)

    blk = pl.BlockSpec((tm, tn), lambda i, j: (i, j))
    return pl.pallas_call(
        body, name="branch_mix", grid=(T // tm, D_MODEL // tn),
        in_specs=[pl.BlockSpec((tm, D_POOL), lambda i, j: (i, 0)), pl.BlockSpec((tm, D_RNN), lambda i, j: (i, 0)),
                  pl.BlockSpec((D_POOL, tn), lambda i, j: (0, j)), pl.BlockSpec((D_RNN, tn), lambda i, j: (0, j)),
                  pl.BlockSpec((tm, tn), lambda i, j: (i, col_gp + j)), pl.BlockSpec((tm, tn), lambda i, j: (i, col_gr + j))],
        out_specs=[blk, blk, blk],
        out_shape=[jax.ShapeDtypeStruct((T, D_MODEL), F32), jax.ShapeDtypeStruct((T, D_MODEL), F32),
                   jax.ShapeDtypeStruct((T, D_MODEL), BF16)],
        compiler_params=_params(dimension_semantics=("parallel", "parallel")),
    )(pm, z, w_pool_out, w_rnn_out, proj, proj)


def _out_proj_residual(mix, w_o, x, *, tm):
    T = x.shape[0]

    def body(mix_ref, w_ref, x_ref, o_ref):
        o_ref[...] = x_ref[...] + _dot(mix_ref[...], w_ref[...])

    row = pl.BlockSpec((tm, D_MODEL), lambda i: (i, 0))
    return pl.pallas_call(
        body, name="out_proj_residual", grid=(T // tm,),
        in_specs=[row, pl.BlockSpec((D_MODEL, D_MODEL), lambda i: (0, 0)), row],
        out_specs=row, out_shape=jax.ShapeDtypeStruct((T, D_MODEL), F32),
        compiler_params=_params(dimension_semantics=("parallel",)),
    )(mix, w_o, x)


def _ffn_out_loss(act, w, x2, g3, target, *, tm):
    T = x2.shape[0]

    def body(act_ref, w_ref, x2_ref, g_ref, t_ref, dx_ref, dxb_ref, sq_ref, dg_ref):
        @pl.when(pl.program_id(0) == 0)
        def _():
            sq_ref[...] = jnp.zeros_like(sq_ref)
            dg_ref[...] = jnp.zeros_like(dg_ref)

        x3 = x2_ref[...] + _dot(act_ref[...], w_ref[...])
        g = g_ref[...]
        err = x3 * _rms_scale(x3) * g - t_ref[...]
        sq_ref[...] += jnp.sum(err * err, axis=0, keepdims=True)
        dx, dgp = _rms_bwd(err * (1.0 / D_MODEL), x3, g)
        dg_ref[...] += jnp.sum(dgp, axis=0, keepdims=True)
        dx_ref[...] = dx
        dxb_ref[...] = dx.astype(BF16)

    row = pl.BlockSpec((tm, D_MODEL), lambda i: (i, 0))
    vec = pl.BlockSpec((1, D_MODEL), lambda i: (0, 0))
    return pl.pallas_call(
        body, name="ffn_out_loss", grid=(T // tm,),
        in_specs=[pl.BlockSpec((tm, D_FF), lambda i: (i, 0)), pl.BlockSpec((D_FF, D_MODEL), lambda i: (0, 0)), row, vec, row],
        out_specs=[row, row, vec, vec],
        out_shape=[jax.ShapeDtypeStruct((T, D_MODEL), F32), jax.ShapeDtypeStruct((T, D_MODEL), BF16),
                   jax.ShapeDtypeStruct((1, D_MODEL), F32), jax.ShapeDtypeStruct((1, D_MODEL), F32)],
        compiler_params=_params(dimension_semantics=("arbitrary",)),
    )(act, w, x2, g3, target)


def _ffn_out_bwd(dx3b, w, gate, up, *, tm, tn):
    T = dx3b.shape[0]

    def body(dx_ref, w_ref, gate_ref, up_ref, dgate_ref, dup_ref):
        dact = _dot_nt(dx_ref[...], w_ref[...])
        gate = gate_ref[...]
        s = _sigmoid(gate)
        dgate_ref[...] = (dact * up_ref[...] * s * (1.0 + gate * (1.0 - s))).astype(BF16)
        dup_ref[...] = (dact * gate * s).astype(BF16)

    blk = pl.BlockSpec((tm, tn), lambda i, j: (i, j))
    return pl.pallas_call(
        body, name="ffn_out_bwd", grid=(T // tm, D_FF // tn),
        in_specs=[pl.BlockSpec((tm, D_MODEL), lambda i, j: (i, 0)), pl.BlockSpec((tn, D_MODEL), lambda i, j: (j, 0)), blk, blk],
        out_specs=[blk, blk],
        out_shape=[jax.ShapeDtypeStruct((T, D_FF), BF16), jax.ShapeDtypeStruct((T, D_FF), BF16)],
        compiler_params=_params(dimension_semantics=("parallel", "parallel")),
    )(dx3b, w, gate, up)


def _ffn_in_bwd(dgate, dup, w, dx3, x2, g2, *, tm):
    T = x2.shape[0]

    def body(dgate_ref, dup_ref, w_ref, dx3_ref, x2_ref, g_ref, dx_ref, dxb_ref, dg_ref):
        @pl.when(pl.program_id(0) == 0)
        def _():
            dg_ref[...] = jnp.zeros_like(dg_ref)

        dh = _dot_nt(dgate_ref[...], w_ref[:, :D_FF]) + _dot_nt(dup_ref[...], w_ref[:, D_FF:])
        dxn, dgp = _rms_bwd(dh, x2_ref[...], g_ref[...])
        dx = dx3_ref[...] + dxn
        dg_ref[...] += jnp.sum(dgp, axis=0, keepdims=True)
        dx_ref[...] = dx
        dxb_ref[...] = dx.astype(BF16)

    row = pl.BlockSpec((tm, D_MODEL), lambda i: (i, 0))
    wide = pl.BlockSpec((tm, D_FF), lambda i: (i, 0))
    vec = pl.BlockSpec((1, D_MODEL), lambda i: (0, 0))
    return pl.pallas_call(
        body, name="ffn_in_bwd", grid=(T // tm,),
        in_specs=[wide, wide, pl.BlockSpec((D_MODEL, 2 * D_FF), lambda i: (0, 0)), row, row, vec],
        out_specs=[row, row, vec],
        out_shape=[jax.ShapeDtypeStruct((T, D_MODEL), F32), jax.ShapeDtypeStruct((T, D_MODEL), BF16),
                   jax.ShapeDtypeStruct((1, D_MODEL), F32)],
        compiler_params=_params(dimension_semantics=("arbitrary",)),
    )(dgate, dup, w, dx3, x2, g2)


def _out_proj_bwd(dx2b, w_o, proj, y_pool, y_rnn, *, tm, tn, jobs=()):
    T = dx2b.shape[0]
    col_gp = (D_POOL + 2 * D_RNN) // tn
    col_gr = col_gp + D_MODEL // tn

    def body(dx_ref, w_ref, gp_ref, gr_ref, yp_ref, yr_ref, dgp_ref, dgr_ref, dyp_ref, dyr_ref):
        dmix = _dot_nt(dx_ref[...], w_ref[...])
        sp = _sigmoid(gp_ref[...])
        sr = _sigmoid(gr_ref[...])
        dgp_ref[...] = (dmix * yp_ref[...] * sp * (1.0 - sp)).astype(BF16)
        dgr_ref[...] = (dmix * yr_ref[...] * sr * (1.0 - sr)).astype(BF16)
        dyp_ref[...] = (dmix * sp).astype(BF16)
        dyr_ref[...] = (dmix * sr).astype(BF16)

    blk = pl.BlockSpec((tm, tn), lambda i, j: (i, j))
    out = jax.ShapeDtypeStruct((T, D_MODEL), BF16)
    return _pallas(
        body, (dx2b, w_o, proj, proj, y_pool, y_rnn), name="out_proj_bwd", grid=(T // tm, D_MODEL // tn),
        in_specs=[pl.BlockSpec((tm, D_MODEL), lambda i, j: (i, 0)), pl.BlockSpec((tn, D_MODEL), lambda i, j: (j, 0)),
                  pl.BlockSpec((tm, tn), lambda i, j: (i, col_gp + j)), pl.BlockSpec((tm, tn), lambda i, j: (i, col_gr + j)), blk, blk],
        out_specs=[blk, blk, blk, blk], out_shape=[out, out, out, out], semantics=("parallel", "parallel"), jobs=jobs)


def _branch_bwd(dyp, dyr, w_pool_out, w_rnn_out, *, tm):
    T = dyp.shape[0]

    def body(dyp_ref, dyr_ref, wp_ref, wr_ref, dpm_ref, dz_ref):
        dpm_ref[...] = _dot_nt(dyp_ref[...], wp_ref[...])
        dz_ref[...] = _dot_nt(dyr_ref[...], wr_ref[...])

    row = pl.BlockSpec((tm, D_MODEL), lambda i: (i, 0))
    return pl.pallas_call(
        body, name="branch_bwd", grid=(T // tm,),
        in_specs=[row, row, pl.BlockSpec((D_POOL, D_MODEL), lambda i: (0, 0)), pl.BlockSpec((D_RNN, D_MODEL), lambda i: (0, 0))],
        out_specs=[pl.BlockSpec((tm, D_POOL), lambda i: (i, 0)), pl.BlockSpec((tm, D_RNN), lambda i: (i, 0))],
        out_shape=[jax.ShapeDtypeStruct((T, D_POOL), F32), jax.ShapeDtypeStruct((T, D_RNN), F32)],
        compiler_params=_params(dimension_semantics=("parallel",)),
    )(dyp, dyr, w_pool_out, w_rnn_out)


def _in_proj_bwd(segs, w, dx2, x, g1, *, tm):
    T = x.shape[0]
    widths = [s.shape[1] for s in segs]
    offs = [sum(widths[:k]) for k in range(len(widths))]
    n = len(segs)

    def body(*refs):
        seg_refs, (w_ref, dx2_ref, x_ref, g_ref, dx_ref, dg_ref) = refs[:n], refs[n:]

        @pl.when(pl.program_id(0) == 0)
        def _():
            dg_ref[...] = jnp.zeros_like(dg_ref)

        dh = _dot_nt(seg_refs[0][...], w_ref[:, offs[0]:offs[0] + widths[0]])
        for k in range(1, n):
            dh += _dot_nt(seg_refs[k][...], w_ref[:, offs[k]:offs[k] + widths[k]])
        dxn, dgp = _rms_bwd(dh, x_ref[...], g_ref[...])
        dg_ref[...] += jnp.sum(dgp, axis=0, keepdims=True)
        dx_ref[...] = dx2_ref[...] + dxn

    row = pl.BlockSpec((tm, D_MODEL), lambda i: (i, 0))
    vec = pl.BlockSpec((1, D_MODEL), lambda i: (0, 0))
    return pl.pallas_call(
        body, name="in_proj_bwd", grid=(T // tm,),
        in_specs=[pl.BlockSpec((tm, wd), lambda i: (i, 0)) for wd in widths]
        + [pl.BlockSpec((D_MODEL, D_IN), lambda i: (0, 0)), row, row, vec],
        out_specs=[row, vec],
        out_shape=[jax.ShapeDtypeStruct((T, D_MODEL), F32), jax.ShapeDtypeStruct((1, D_MODEL), F32)],
        compiler_params=_params(dimension_semantics=("arbitrary",)),
    )(*segs, w, dx2, x, g1)


def _weight_grad(a, segs, *, tm, tn, name, jobs=None):
    T, M = a.shape
    nblk = [s.shape[1] // tn for s in segs]
    first = [sum(nblk[:k]) for k in range(len(segs))]
    n = len(segs)

    def body(a_ref, *refs):
        seg_refs, o_ref = refs[:n], refs[n]
        j = pl.program_id(1)
        for k in range(n):
            @pl.when((j >= first[k]) & (j < first[k] + nblk[k]))
            def _(k=k):
                o_ref[...] = _dot_tn(a_ref[...], seg_refs[k][...])

    def seg_spec(k):
        return pl.BlockSpec((T, tn), lambda i, j: (0, jnp.clip(j - first[k], 0, nblk[k] - 1)))

    (grad,), results = _pallas(
        body, (a, *segs), name=name, grid=(M // tm, sum(nblk)),
        in_specs=[pl.BlockSpec((T, tm), lambda i, j: (0, i))] + [seg_spec(k) for k in range(n)],
        out_specs=[pl.BlockSpec((tm, tn), lambda i, j: (i, j))],
        out_shape=[jax.ShapeDtypeStruct((M, sum(nblk) * tn), F32)],
        semantics=("parallel", "arbitrary"), jobs=jobs or ())
    return grad if jobs is None else (grad, results)


def _pad_front(dst, src, halo):
    dst[pl.ds(0, halo), :] = jnp.zeros((halo, src.shape[1]), F32)

    def fill(i, carry):
        r0 = pl.multiple_of(i * CHUNK, CHUNK)
        dst[pl.ds(r0 + halo, CHUNK), :] = src[pl.ds(r0, CHUNK), :]
        return carry

    lax.fori_loop(0, src.shape[0] // CHUNK, fill, 0)


def _shift_rows(v, k):
    return pltpu.roll(v, k % v.shape[0], axis=0)


def _window_sums(xs, direction):
    s2 = xs + _shift_rows(xs, direction)
    s4 = s2 + _shift_rows(s2, 2 * direction)
    s8 = s4 + _shift_rows(s4, 4 * direction)
    s16 = s8 + _shift_rows(s8, 8 * direction)
    return s2, s4, s8, s16


def _select_window(g, sums):
    s2, s4, s8, s16 = sums
    return jnp.where(g == 0, s2, jnp.where(g == 1, s4, jnp.where(g == 2, s8, s16)))


def _pool_count(g, start, rows):
    t = start + lax.broadcasted_iota(jnp.int32, (rows, 1), 0)
    return jnp.minimum(t + 1, jnp.left_shift(2, g)).astype(F32)


def _pool_fwd(proj, w_grp, scale):
    T = proj.shape[0]
    nchunk = T // CHUNK

    def body(u_ref, w_ref, s_ref, o_ref, upad):
        g = pl.program_id(0)
        _pad_front(upad, u_ref, POOL_HALO)
        w = w_ref[...].astype(BF16)
        scale_row = s_ref[...]

        def chunk(i, carry):
            r0 = pl.multiple_of(i * CHUNK, CHUNK)
            xs = upad[pl.ds(r0, CHUNK + POOL_HALO), :]
            win = _select_window(g, _window_sums(xs, 1))[POOL_HALO:]
            pooled = win / _pool_count(g, r0, CHUNK) - xs[POOL_HALO:]
            o_ref[pl.ds(r0, CHUNK), :] = (_dot(pooled.astype(BF16), w) * scale_row).astype(BF16)
            return carry

        lax.fori_loop(0, nchunk, chunk, 0)

    return pl.pallas_call(
        body, name="pool_fwd", grid=(N_POOL_GROUPS,),
        in_specs=[pl.BlockSpec((T, HEAD), lambda g: (0, g)), pl.BlockSpec((None, HEAD, HEAD), lambda g: (g, 0, 0)),
                  pl.BlockSpec((1, HEAD), lambda g: (0, g))],
        out_specs=pl.BlockSpec((T, HEAD), lambda g: (0, g)),
        out_shape=jax.ShapeDtypeStruct((T, D_POOL), BF16),
        scratch_shapes=[pltpu.VMEM((T + POOL_HALO, HEAD), F32)],
        compiler_params=_params(dimension_semantics=("parallel",)),
    )(proj, w_grp, scale)


def _pool_bwd(proj, dpm, w_grp, scale, jobs=()):
    T = proj.shape[0]
    nchunk = T // CHUNK

    def body(u_ref, dpm_ref, w_ref, s_ref, du_ref, dw_ref, ds_ref, upad, zpad, dpool):
        g = pl.program_id(0)
        _pad_front(upad, u_ref, POOL_HALO)
        zpad[pl.ds(T, POOL_HALO), :] = jnp.zeros((POOL_HALO, HEAD), F32)
        dw_ref[...] = jnp.zeros_like(dw_ref)
        ds_ref[...] = jnp.zeros_like(ds_ref)
        w = w_ref[...].astype(BF16)
        scale_row = s_ref[...]

        def chunk(i, carry):
            r0 = pl.multiple_of(i * CHUNK, CHUNK)
            xs = upad[pl.ds(r0, CHUNK + POOL_HALO), :]
            cnt = _pool_count(g, r0, CHUNK)
            pooled = (_select_window(g, _window_sums(xs, 1))[POOL_HALO:] / cnt - xs[POOL_HALO:]).astype(BF16)
            mixed = _dot(pooled, w)
            d = dpm_ref[pl.ds(r0, CHUNK), :]
            ds_ref[...] += jnp.sum(d * mixed, axis=0, keepdims=True)
            dmixed = (d * scale_row).astype(BF16)
            dw_ref[...] += _dot_tn(pooled, dmixed)
            dp = _dot_nt(dmixed, w)
            dpool[pl.ds(r0, CHUNK), :] = dp
            zpad[pl.ds(r0, CHUNK), :] = dp / cnt
            return carry

        lax.fori_loop(0, nchunk, chunk, 0)

        def chunk2(i, carry):
            r0 = pl.multiple_of(i * CHUNK, CHUNK)
            zs = zpad[pl.ds(r0, CHUNK + POOL_HALO), :]
            win = _select_window(g, _window_sums(zs, -1))[:CHUNK]
            du_ref[pl.ds(r0, CHUNK), :] = (win - dpool[pl.ds(r0, CHUNK), :]).astype(BF16)
            return carry

        lax.fori_loop(0, nchunk, chunk2, 0)

    col = pl.BlockSpec((T, HEAD), lambda g: (0, g))
    return _pallas(
        body, (proj, dpm, w_grp, scale), name="pool_bwd", grid=(N_POOL_GROUPS,),
        in_specs=[col, col, pl.BlockSpec((None, HEAD, HEAD), lambda g: (g, 0, 0)), pl.BlockSpec((1, HEAD), lambda g: (0, g))],
        out_specs=[col, pl.BlockSpec((None, HEAD, HEAD), lambda g: (g, 0, 0)), pl.BlockSpec((1, HEAD), lambda g: (0, g))],
        out_shape=[jax.ShapeDtypeStruct((T, D_POOL), BF16), jax.ShapeDtypeStruct((N_POOL_GROUPS, HEAD, HEAD), F32),
                   jax.ShapeDtypeStruct((1, D_POOL), F32)],
        scratch_shapes=[pltpu.VMEM((T + POOL_HALO, HEAD), F32), pltpu.VMEM((T + POOL_HALO, HEAD), F32), pltpu.VMEM((T, HEAD), F32)],
        semantics=("parallel",), jobs=jobs)


def _conv_taps(xs, cw):
    v = cw[CONV_WIDTH - 1] * xs[SUBLANES:]
    for k in range(CONV_WIDTH - 1):
        v += cw[k] * _shift_rows(xs, CONV_WIDTH - 1 - k)[SUBLANES:]
    return v


def _tap_rows(cw_ref):
    return [cw_ref[k:k + 1, :] for k in range(CONV_WIDTH)]


def _softplus_neg(lam):
    return jnp.maximum(-lam, 0.0) + _log1p(jnp.exp(-jnp.abs(lam)))


def _lru_gates(v, wa, ba, wx, bx, sp):
    vb = v.astype(BF16)
    ra = _sigmoid(_dot(vb, wa) + ba)
    ix = _sigmoid(_dot(vb, wx) + bx)
    log_a = -LRU_C * ra * sp
    a = jnp.exp(log_a)
    sq = jnp.sqrt(-jnp.tanh(log_a) * (a * a + 1.0))
    return ra, ix, a, sq


def _row_bcast(v, r):
    return jnp.broadcast_to(v[r:r + 1, :], v.shape)


def _tile_scans(a, b, direction):
    ri = lax.broadcasted_iota(jnp.int32, a.shape, 0) % SUBLANES
    A, B = a, b
    for s in (1, 2, 4):
        ok = (ri >= s) if direction == 1 else (ri + s < SUBLANES)
        As, Bs = _shift_rows(A, s * direction), _shift_rows(B, s * direction)
        B = jnp.where(ok, A * Bs + B, B)
        A = jnp.where(ok, A * As, A)
    return A, B


TILES_PER_STEP = 8


def _carry_tiles(A_s, B_s, out, ntile, direction):
    out_row = SUBLANES - 1 if direction == 1 else 0

    def step(k, carry):
        for j in range(TILES_PER_STEP):
            t = k * TILES_PER_STEP + j
            r0 = pl.multiple_of((t if direction == 1 else ntile - 1 - t) * SUBLANES, SUBLANES)
            A, B = A_s[pl.ds(r0, SUBLANES), :], B_s[pl.ds(r0, SUBLANES), :]
            out[pl.ds(r0, SUBLANES), :] = A * carry + B
            carry = _row_bcast(A, out_row) * carry + _row_bcast(B, out_row)
        return carry

    lax.fori_loop(0, ntile // TILES_PER_STEP, step, jnp.zeros((SUBLANES, HEAD), F32))


def _rnn_fwd(proj, conv_w, conv_b, w_a, b_a, w_x, b_x, lam, jobs=()):
    T = proj.shape[0]
    nchunk = T // CHUNK
    ntile = T // SUBLANES

    def body(u_ref, ug_ref, cw_ref, cb_ref, wa_ref, ba_ref, wx_ref, bx_ref, lam_ref, h_ref, z_ref, upad, a_s, b_s):
        _pad_front(upad, u_ref, SUBLANES)
        cw, cb = _tap_rows(cw_ref), cb_ref[...]
        wa, wx = wa_ref[...].astype(BF16), wx_ref[...].astype(BF16)
        ba, bx = ba_ref[...], bx_ref[...]
        sp = _softplus_neg(lam_ref[...])

        def chunk(i, carry):
            r0 = pl.multiple_of(i * CHUNK, CHUNK)
            v = _conv_taps(upad[pl.ds(r0, CHUNK + SUBLANES), :], cw) + cb
            _, ix, a, sq = _lru_gates(v, wa, ba, wx, bx, sp)
            a_s[pl.ds(r0, CHUNK), :], b_s[pl.ds(r0, CHUNK), :] = _tile_scans(a, sq * ix * v, 1)
            return carry

        lax.fori_loop(0, nchunk, chunk, 0)
        _carry_tiles(a_s, b_s, h_ref, ntile, 1)

        def chunk3(i, carry):
            r0 = pl.multiple_of(i * CHUNK, CHUNK)
            gl, _ = _gelu_parts(ug_ref[pl.ds(r0, CHUNK), :])
            z_ref[pl.ds(r0, CHUNK), :] = (h_ref[pl.ds(r0, CHUNK), :] * gl).astype(BF16)
            return carry

        lax.fori_loop(0, nchunk, chunk3, 0)

    col = pl.BlockSpec((T, HEAD), lambda h: (0, h))
    vec = pl.BlockSpec((1, HEAD), lambda h: (0, h))
    mat = pl.BlockSpec((None, HEAD, HEAD), lambda h: (h, 0, 0))
    return _pallas(
        body, (proj, proj, conv_w, conv_b, w_a, b_a, w_x, b_x, lam), name="rnn_fwd", grid=(N_RNN_HEADS,),
        in_specs=[pl.BlockSpec((T, HEAD), lambda h: (0, COL_RNN + h)), pl.BlockSpec((T, HEAD), lambda h: (0, COL_GATE + h)),
                  pl.BlockSpec((CONV_WIDTH, HEAD), lambda h: (0, h)), vec, mat, vec, mat, vec, vec],
        out_specs=[col, col],
        out_shape=[jax.ShapeDtypeStruct((T, D_RNN), F32), jax.ShapeDtypeStruct((T, D_RNN), BF16)],
        scratch_shapes=[pltpu.VMEM((T + SUBLANES, HEAD), F32), pltpu.VMEM((T, HEAD), F32), pltpu.VMEM((T, HEAD), F32)],
        semantics=("parallel",), jobs=jobs)


def _rnn_bwd(proj, hr, dz, conv_w, conv_b, w_a, b_a, w_x, b_x, lam, jobs=()):
    T = proj.shape[0]
    nchunk = T // CHUNK
    ntile = T // SUBLANES

    def body(u_ref, ug_ref, h_ref, dz_ref, cw_ref, cb_ref, wa_ref, ba_ref, wx_ref, bx_ref, lam_ref,
             du_ref, dug_ref, dwa_ref, dwx_ref, dba_ref, dbx_ref, dlam_ref, dcb_ref, dcw_ref,
             upad, hpad, apad, v_s, ra_s, ix_s, sq_s, g_s, dvpad, ga_s):
        zero_tile = jnp.zeros((SUBLANES, HEAD), F32)
        _pad_front(upad, u_ref, SUBLANES)
        _pad_front(hpad, h_ref, SUBLANES)
        apad[pl.ds(T, SUBLANES), :] = zero_tile
        dvpad[pl.ds(T, SUBLANES), :] = zero_tile
        for ref in (dwa_ref, dwx_ref, dba_ref, dbx_ref, dlam_ref, dcb_ref, dcw_ref):
            ref[...] = jnp.zeros_like(ref)
        cw, cb = _tap_rows(cw_ref), cb_ref[...]
        wa, wx = wa_ref[...].astype(BF16), wx_ref[...].astype(BF16)
        ba, bx = ba_ref[...], bx_ref[...]
        lam_row = lam_ref[...]
        sp = _softplus_neg(lam_row)

        def chunk(i, carry):
            r0 = pl.multiple_of(i * CHUNK, CHUNK)
            rows = pl.ds(r0, CHUNK)
            h = h_ref[rows, :]
            v = _conv_taps(upad[pl.ds(r0, CHUNK + SUBLANES), :], cw) + cb
            ra, ix, a, sq = _lru_gates(v, wa, ba, wx, bx, sp)
            v_s[rows, :], ra_s[rows, :], ix_s[rows, :], sq_s[rows, :], apad[rows, :] = v, ra, ix, sq, a
            gl, dgl = _gelu_parts(ug_ref[rows, :])
            d = dz_ref[rows, :]
            g_s[rows, :] = d * gl
            dug_ref[rows, :] = (d * h * dgl).astype(BF16)
            return carry

        lax.fori_loop(0, nchunk, chunk, 0)

        def chunk2(i, carry):
            r0 = pl.multiple_of(i * CHUNK, CHUNK)
            rows = pl.ds(r0, CHUNK)
            a_next = _shift_rows(apad[pl.ds(r0, CHUNK + SUBLANES), :], -1)[:CHUNK]
            ga_s[rows, :], g_s[rows, :] = _tile_scans(a_next, g_s[rows, :], -1)
            return carry

        lax.fori_loop(0, nchunk, chunk2, 0)
        _carry_tiles(ga_s, g_s, g_s, ntile, -1)

        def chunk3(i, carry):
            r0 = pl.multiple_of(i * CHUNK, CHUNK)
            rows = pl.ds(r0, CHUNK)
            g = g_s[rows, :]
            h_prev = _shift_rows(hpad[pl.ds(r0, CHUNK + SUBLANES), :], 1)[SUBLANES:]
            v, ra, ix, sq, a = v_s[rows, :], ra_s[rows, :], ix_s[rows, :], sq_s[rows, :], apad[rows, :]
            d_sq = g * ix * v
            d_ix = g * sq * v
            d_la = a * g * h_prev - d_sq * a * a / sq
            dlam_ref[...] += jnp.sum(d_la * ra, axis=0, keepdims=True)
            d_pa = d_la * (-LRU_C) * sp * ra * (1.0 - ra)
            d_px = d_ix * ix * (1.0 - ix)
            vb, d_pab, d_pxb = v.astype(BF16), d_pa.astype(BF16), d_px.astype(BF16)
            dwa_ref[...] += _dot_tn(vb, d_pab)
            dwx_ref[...] += _dot_tn(vb, d_pxb)
            dba_ref[...] += jnp.sum(d_pa, axis=0, keepdims=True)
            dbx_ref[...] += jnp.sum(d_px, axis=0, keepdims=True)
            dv = g * sq * ix + _dot_nt(d_pab, wa) + _dot_nt(d_pxb, wx)
            dvpad[rows, :] = dv
            dcb_ref[...] += jnp.sum(dv, axis=0, keepdims=True)
            xs = upad[pl.ds(r0, CHUNK + SUBLANES), :]
            for k in range(CONV_WIDTH):
                u_k = _shift_rows(xs, CONV_WIDTH - 1 - k)[SUBLANES:] if k < CONV_WIDTH - 1 else xs[SUBLANES:]
                dcw_ref[k:k + 1, :] += jnp.sum(dv * u_k, axis=0, keepdims=True)
            return carry

        lax.fori_loop(0, nchunk, chunk3, 0)
        dlam_ref[...] = dlam_ref[...] * (LRU_C * _sigmoid(-lam_row))

        def chunk4(i, carry):
            r0 = pl.multiple_of(i * CHUNK, CHUNK)
            dvs = dvpad[pl.ds(r0, CHUNK + SUBLANES), :]
            du = cw[CONV_WIDTH - 1] * dvs[:CHUNK]
            for k in range(CONV_WIDTH - 1):
                du += cw[k] * _shift_rows(dvs, -(CONV_WIDTH - 1 - k))[:CHUNK]
            du_ref[pl.ds(r0, CHUNK), :] = du.astype(BF16)
            return carry

        lax.fori_loop(0, nchunk, chunk4, 0)

    col = pl.BlockSpec((T, HEAD), lambda h: (0, h))
    vec = pl.BlockSpec((1, HEAD), lambda h: (0, h))
    mat = pl.BlockSpec((None, HEAD, HEAD), lambda h: (h, 0, 0))
    taps = pl.BlockSpec((CONV_WIDTH, HEAD), lambda h: (0, h))
    vec_out = jax.ShapeDtypeStruct((1, D_RNN), F32)
    mat_out = jax.ShapeDtypeStruct((N_RNN_HEADS, HEAD, HEAD), F32)
    seq = pltpu.VMEM((T, HEAD), F32)
    seq_pad = pltpu.VMEM((T + SUBLANES, HEAD), F32)
    return _pallas(
        body, (proj, proj, hr, dz, conv_w, conv_b, w_a, b_a, w_x, b_x, lam), name="rnn_bwd", grid=(N_RNN_HEADS,),
        in_specs=[pl.BlockSpec((T, HEAD), lambda h: (0, COL_RNN + h)), pl.BlockSpec((T, HEAD), lambda h: (0, COL_GATE + h)),
                  col, col, taps, vec, mat, vec, mat, vec, vec],
        out_specs=[col, col, mat, mat, vec, vec, vec, vec, taps],
        out_shape=[jax.ShapeDtypeStruct((T, D_RNN), BF16), jax.ShapeDtypeStruct((T, D_RNN), BF16), mat_out, mat_out,
                   vec_out, vec_out, vec_out, vec_out, jax.ShapeDtypeStruct((CONV_WIDTH, D_RNN), F32)],
        scratch_shapes=[seq_pad, seq_pad, seq_pad, seq, seq, seq, seq, seq, seq_pad, seq],
        semantics=("parallel",), jobs=jobs)


GROUP_FFN = ["w_ffn_out", "w_ffn_in"]
GROUP_MIX = ["w_o", "w_pool_out", "w_rnn_out"]
GROUP_IN = ["w_in"]


def _step(x, target, s, full, conv_w_mine, place):
    T = x.shape[0]
    tall, mid, low = min(T, 2048), min(T, 1024), min(T, 512)
    c1 = place[1:]
    full = dict(full)

    def gathered(names, results):
        full.update(zip(names, results))

    (full["w_in"], conv_w), = _run_jobs([_gather_job(full, ["w_in"], conv_w_mine)], "gather_w_in")
    early = ["w_pool_out", "w_rnn_out", "w_o", "w_ffn_out"]
    (proj, h1), (res,) = _norm_matmul(x, s["norm_mix"], full["w_in"], tm=tall, tn=512, name="in_proj", jobs=[_gather_job(full, early)])
    gathered(early, res)
    pm = _pool_fwd(proj, s["w_pool_grp"], s["pool_scale"])
    (hr, z), (res,) = _rnn_fwd(proj, conv_w, s["conv_b"], s["w_rg_a"], s["b_rg_a"], s["w_rg_x"], s["b_rg_x"], s["lru_lambda"],
                               jobs=[_gather_job(full, ["w_ffn_in"])])
    gathered(["w_ffn_in"], res)
    y_pool, y_rnn, mix = _branch_mix(pm, z, full["w_pool_out"], full["w_rnn_out"], proj, tm=tall, tn=256)
    x2 = _out_proj_residual(mix, full["w_o"], x, tm=mid)
    gate, up, act, h2 = _ffn_in(x2, s["norm_ffn"], full["w_ffn_in"], tm=tall, tn=256)
    dx3, dx3b, sq_cols, g_norm_final = _ffn_out_loss(act, full["w_ffn_out"], x2, s["norm_final"], target, tm=low)

    g = {"norm_final": g_norm_final}

    def chip_sums(names, from_sibling):
        sums = {name: _chip_sum(name, g[name], got, c1) for name, got in zip(names, from_sibling)}
        return {name: v[0] for name, v in sums.items()}, {name: v[1] for name, v in sums.items()}

    def final_sums(names, sums, from_chips):
        return {name: _final_sum(name, sums[name], got, place) for name, got in zip(names, from_chips)}

    dgate, dup = _ffn_out_bwd(dx3b, full["w_ffn_out"], gate, up, tm=tall, tn=256)
    g["w_ffn_out"] = _weight_grad(act, [dx3b], tm=256, tn=D_MODEL, name="w_ffn_out_grad")
    dx2, dx2b, g["norm_ffn"] = _ffn_in_bwd(dgate, dup, full["w_ffn_in"], dx3, x2, s["norm_ffn"], tm=low)
    g["w_ffn_in"] = _weight_grad(h2, [dgate, dup], tm=D_MODEL, tn=256, name="w_ffn_in_grad")
    (dgp, dgr, dyp, dyr), (res,) = _out_proj_bwd(dx2b, full["w_o"], proj, y_pool, y_rnn, tm=tall, tn=256,
                                                 jobs=[_sibling_job(g, GROUP_FFN)])
    sums_ffn, sums_ffn_bf16 = chip_sums(GROUP_FFN, res)
    g["w_o"] = _weight_grad(mix, [dx2b], tm=D_MODEL, tn=256, name="w_o_grad")
    dpm, dz = _branch_bwd(dyp, dyr, full["w_pool_out"], full["w_rnn_out"], tm=mid)
    g["w_pool_out"] = _weight_grad(pm, [dyp], tm=D_POOL, tn=256, name="w_pool_out_grad")
    g["w_rnn_out"] = _weight_grad(z, [dyr], tm=D_RNN, tn=256, name="w_rnn_out_grad")
    (dupool, g["w_pool_grp"], g["pool_scale"]), (res,) = _pool_bwd(proj, dpm, s["w_pool_grp"], s["pool_scale"],
                                                                   jobs=[_sibling_job(g, GROUP_MIX)])
    sums_mix, sums_mix_bf16 = chip_sums(GROUP_MIX, res)
    ((durnn, dugate, g["w_rg_a"], g["w_rg_x"], g["b_rg_a"], g["b_rg_x"], g["lru_lambda"], g["conv_b"], g["conv_w"]),
     (res,)) = _rnn_bwd(proj, hr, dz, conv_w, s["conv_b"], s["w_rg_a"], s["b_rg_a"], s["w_rg_x"], s["b_rg_x"], s["lru_lambda"],
                        jobs=[_chips_job(sums_ffn_bf16, GROUP_FFN)])
    shards = final_sums(GROUP_FFN, sums_ffn, res)
    segs = [dupool, durnn, dugate, dgp, dgr]
    grad_x, g["norm_mix"] = _in_proj_bwd(segs, full["w_in"], dx2, x, s["norm_mix"], tm=low)
    g["w_in"], (res, joined) = _weight_grad(h1, segs, tm=D_MODEL, tn=256, name="w_in_grad",
                                           jobs=[_chips_job(sums_mix_bf16, GROUP_MIX), _join_job(shards, GROUP_FFN)])
    grads = dict(zip(GROUP_FFN, joined))
    shards = final_sums(GROUP_MIX, sums_mix, res)
    (res,) = _run_jobs([_sibling_job(g, GROUP_IN)], "w_in_exchange_sibling")
    sums_in, sums_in_bf16 = chip_sums(GROUP_IN, res)

    vec_rows = [g[name] if name != "pool_scale" else jnp.pad(g[name], ((0, 0), (0, D_MODEL - D_POOL))) for name in VEC_ITEMS]
    vec_rows += [g["conv_w"], sq_cols, jnp.zeros((VEC_ROWS - len(VEC_ITEMS) - CONV_WIDTH - 1, D_MODEL), F32)]
    vec = jnp.concatenate(vec_rows, axis=0).reshape(VEC_ROWS, N_DEV, HEAD).transpose(1, 0, 2)
    mat = jnp.concatenate([g[name].reshape(-1, HEAD) for name in MAT_ITEMS], axis=0).reshape(N_DEV, -1, HEAD)
    (vec, mat), (res, joined) = _all_reduce_small([vec, mat], jobs=[_chips_job(sums_in_bf16, GROUP_IN), _join_job(shards, GROUP_MIX)])
    grads.update(zip(GROUP_MIX, joined))
    (joined,) = _run_jobs([_join_job(final_sums(GROUP_IN, sums_in, res), GROUP_IN)], "w_in_join_halves")
    grads.update(zip(GROUP_IN, joined))

    vec = vec.transpose(1, 0, 2).reshape(VEC_ROWS, D_MODEL)
    mat = mat.reshape(-1, HEAD)
    for k, name in enumerate(VEC_ITEMS):
        grads[name] = vec[k:k + 1, :s[name].shape[1]]
    grads["conv_w"] = vec[len(VEC_ITEMS):len(VEC_ITEMS) + CONV_WIDTH]
    row = 0
    for name in MAT_ITEMS:
        rows = s[name].shape[0] * HEAD
        grads[name] = mat[row:row + rows]
        row += rows
    return vec[len(VEC_ITEMS) + CONV_WIDTH], grad_x, grads


LARGE = {"w_in": "col", "w_pool_out": "col", "w_rnn_out": "row", "w_o": "row", "w_ffn_in": "col", "w_ffn_out": "row"}
LARGE_SHAPE = {"w_in": (D_MODEL, D_IN), "w_pool_out": (D_POOL, D_MODEL), "w_rnn_out": (D_RNN, D_MODEL),
               "w_o": (D_MODEL, D_MODEL), "w_ffn_in": (D_MODEL, 2 * D_FF), "w_ffn_out": (D_FF, D_MODEL)}


def _place():
    x, y, c = lax.axis_index("x"), lax.axis_index("y"), lax.axis_index("c")
    return 2 * x + y, c


def _chip_device(chip, c):
    return (chip // 2, chip % 2, c)


def _chip_window(ref, kind, shape, chip, half=None):
    K, N = shape
    if kind == "col":
        rows = slice(None) if half is None else pl.ds(half * (K // 2), K // 2)
        return ref.at[rows, pl.ds(chip * (N // N_CHIPS), N // N_CHIPS)]
    ks = K // N_CHIPS
    if half is None:
        return ref.at[pl.ds(chip * ks, ks), :]
    return ref.at[pl.ds(chip * ks + half * (ks // 2), ks // 2), :]


def _row_half(ref, half):
    rows = ref.shape[0] // 2
    return ref.at[pl.ds(half * rows, rows), :]


def _remote(win_src, win_dst, send_sems, recv_sems, idx, to):
    return pltpu.make_async_remote_copy(src_ref=win_src, dst_ref=win_dst, send_sem=send_sems.at[idx], recv_sem=recv_sems.at[idx],
                                        device_id=to, device_id_type=MESH)


def _gather_job(full, names, conv_w_full=None):
    n = len(names)
    cw_cols = D_RNN // N_CHIPS

    def windows(refs, chip, half):
        return [_chip_window(refs[k], LARGE[name], LARGE_SHAPE[name], chip, half) for k, name in enumerate(names)]

    def ici_copies(refs, send_sems, recv_sems, src_chip, dst_chip, c, r):
        wins = windows(refs, src_chip, c)
        if conv_w_full is not None:
            wins.append(refs[n].at[:, pl.ds(src_chip * cw_cols, cw_cols)])
        return [_remote(win, win, send_sems, recv_sems, (k, r), _chip_device(dst_chip, c)) for k, win in enumerate(wins)]

    def forwards(refs, send_sems, recv_sems, src_chip, half, to_core, chip, r):
        return [_remote(win, win, send_sems, recv_sems, (k, 3 + r), _chip_device(chip, to_core))
                for k, win in enumerate(windows(refs, src_chip, half))]

    def start(ins, outs, send_sems, recv_sems):
        chip, c = _place()
        for r in range(3):
            for cp in ici_copies(outs, send_sems, recv_sems, chip, chip ^ (r + 1), c, r):
                cp.start()

    def finish(ins, outs, send_sems, recv_sems):
        chip, c = _place()
        for r in range(3):
            for cp in ici_copies(outs, send_sems, recv_sems, chip ^ (r + 1), chip, c, r):
                cp.wait_recv()
            for cp in forwards(outs, send_sems, recv_sems, chip ^ (r + 1), c, 1 - c, chip, r):
                cp.start()
        for r in range(3):
            for cp in forwards(outs, send_sems, recv_sems, chip ^ (r + 1), 1 - c, c, chip, r):
                cp.wait_recv()
            for cp in ici_copies(outs, send_sems, recv_sems, chip, chip ^ (r + 1), c, r):
                cp.wait_send()
            for cp in forwards(outs, send_sems, recv_sems, chip ^ (r + 1), c, 1 - c, chip, r):
                cp.wait_send()

    arrays = [full[name] for name in names] + ([conv_w_full] if conv_w_full is not None else [])
    return _Job(arrays, [jax.ShapeDtypeStruct(a.shape, a.dtype) for a in arrays], {k: k for k in range(len(arrays))},
                (len(arrays), 6), start, finish)


def _core_halves(ref, kind, shape, c):
    return [_chip_window(ref, kind, shape, chip, c) for chip in range(N_CHIPS)]


def _sibling_job(grads, names):
    def start(ins, outs, send_sems, recv_sems):
        chip, c = _place()
        for k, name in enumerate(names):
            kind, shape = LARGE[name], LARGE_SHAPE[name]
            if kind == "col":
                pairs = [(_row_half(ins[k], 1 - c), outs[k])]
            else:
                rows = shape[0] // N_DEV
                pairs = [(win, outs[k].at[pl.ds(j * rows, rows), :]) for j, win in enumerate(_core_halves(ins[k], kind, shape, 1 - c))]
            for src, dst in pairs:
                _remote(src, dst, send_sems, recv_sems, k, _chip_device(chip, 1 - c)).start()

    def finish(ins, outs, send_sems, recv_sems):
        chip, c = _place()
        for k in range(len(names)):
            _remote(outs[k], outs[k], send_sems, recv_sems, k, _chip_device(chip, 1 - c)).wait()

    return _Job([grads[name] for name in names],
                [jax.ShapeDtypeStruct((LARGE_SHAPE[name][0] // 2, LARGE_SHAPE[name][1]), F32) for name in names], {},
                (len(names),), start, finish)


def _chip_sum(name, g, got, c):
    kind, (K, N) = LARGE[name], LARGE_SHAPE[name]
    rows = K // N_DEV

    def body(c_ref, g_ref, got_ref, o_ref, ob_ref):
        total = g_ref[...] + got_ref[...]
        o_ref[...] = total
        ob_ref[...] = total.astype(BF16)

    if kind == "col":
        mine = pl.BlockSpec((rows, N), lambda j, c_ref: (j + N_CHIPS * c_ref[0], 0))
    else:
        mine = pl.BlockSpec((rows, N), lambda j, c_ref: (2 * j + c_ref[0], 0))
    blk = pl.BlockSpec((rows, N), lambda j, c_ref: (j, 0))
    return pl.pallas_call(
        body, name=name + "_chip_sum",
        grid_spec=pltpu.PrefetchScalarGridSpec(num_scalar_prefetch=1, grid=(N_CHIPS,), in_specs=[mine, blk], out_specs=[blk, blk]),
        out_shape=[jax.ShapeDtypeStruct((K // 2, N), F32), jax.ShapeDtypeStruct((K // 2, N), BF16)],
        compiler_params=_params(dimension_semantics=("parallel",)),
    )(c, g, got)


def _piece(ref, kind, shape, chip):
    K, N = shape
    if kind == "col":
        return ref.at[:, pl.ds(chip * (N // N_CHIPS), N // N_CHIPS)]
    return ref.at[pl.ds(chip * (K // N_DEV), K // N_DEV), :]


def _piece_shape(name):
    kind, (K, N) = LARGE[name], LARGE_SHAPE[name]
    return (K // 2, N // N_CHIPS) if kind == "col" else (K // N_DEV, N)


def _chips_job(sums, names):
    def copies(ins, outs, send_sems, recv_sems):
        chip, c = _place()
        return [_remote(_piece(ins[k], LARGE[name], LARGE_SHAPE[name], chip ^ (r + 1)), outs[k].at[r], send_sems, recv_sems, (k, r),
                        _chip_device(chip ^ (r + 1), c)) for k, name in enumerate(names) for r in range(3)]

    def start(*refs):
        for cp in copies(*refs):
            cp.start()

    def finish(*refs):
        for cp in copies(*refs):
            cp.wait()

    return _Job([sums[name] for name in names], [jax.ShapeDtypeStruct((3,) + _piece_shape(name), BF16) for name in names], {},
                (len(names), 3), start, finish)


def _final_sum(name, chip_sum, got, place):
    kind = LARGE[name]
    rows, cols = _piece_shape(name)

    def body(place_ref, s_ref, got_ref, o_ref):
        o_ref[...] = ((s_ref[...] + got_ref[0].astype(F32)) + got_ref[1].astype(F32)) + got_ref[2].astype(F32)

    if kind == "col":
        mine = pl.BlockSpec((rows, cols), lambda i, place_ref: (0, place_ref[0]))
    else:
        mine = pl.BlockSpec((rows, cols), lambda i, place_ref: (place_ref[0], 0))
    return pl.pallas_call(
        body, name=name + "_final_sum",
        grid_spec=pltpu.PrefetchScalarGridSpec(
            num_scalar_prefetch=1, grid=(1,), in_specs=[mine, pl.BlockSpec((3, rows, cols), lambda i, place_ref: (0, 0, 0))],
            out_specs=pl.BlockSpec((rows, cols), lambda i, place_ref: (place_ref[1], 0))),
        out_shape=jax.ShapeDtypeStruct((2 * rows, cols), F32),
        compiler_params=_params(dimension_semantics=("arbitrary",)),
    )(place, chip_sum, got)


def _join_job(shards, names):
    def half_copy(outs, send_sems, recv_sems, k, mine):
        chip, c = _place()
        win = _row_half(outs[k], c if mine else 1 - c)
        return _remote(win, win, send_sems, recv_sems, k, _chip_device(chip, 1 - c))

    def start(ins, outs, send_sems, recv_sems):
        for k in range(len(names)):
            half_copy(outs, send_sems, recv_sems, k, True).start()

    def finish(ins, outs, send_sems, recv_sems):
        for k in range(len(names)):
            half_copy(outs, send_sems, recv_sems, k, True).wait_send()
            half_copy(outs, send_sems, recv_sems, k, False).wait_recv()

    arrays = [shards[name] for name in names]
    return _Job(arrays, [jax.ShapeDtypeStruct(a.shape, F32) for a in arrays], {k: k for k in range(len(arrays))},
                (len(arrays),), start, finish)


VEC_ROWS = 16


def _all_reduce_small(slabs, jobs=()):
    n = len(slabs)

    def body(*refs):
        in_refs, out_refs, got_refs = refs[:n], refs[n:2 * n], refs[2 * n:3 * n]
        send_sems, recv_sems = refs[3 * n:]
        x, y, c = lax.axis_index("x"), lax.axis_index("y"), lax.axis_index("c")
        me = 4 * x + 2 * y + c

        def remote(src, dst, k, phase, r):
            other = me ^ r
            return pltpu.make_async_remote_copy(src_ref=src, dst_ref=dst, send_sem=send_sems.at[k, phase, r],
                                                recv_sem=recv_sems.at[k, phase, r],
                                                device_id=(other // 4, (other // 2) % 2, other % 2), device_id_type=MESH)

        scatter = [remote(in_refs[k].at[me ^ r], got_refs[k].at[r], k, 0, r) for r in range(1, N_DEV) for k in range(n)]
        for cp in scatter:
            cp.start()
        for cp in scatter:
            cp.wait()
        for k in range(n):
            total = in_refs[k][me]
            for r in range(1, N_DEV):
                total = total + got_refs[k][r]
            out_refs[k][me] = total
        gather = [remote(out_refs[k].at[me], out_refs[k].at[me], k, 1, r) for r in range(1, N_DEV) for k in range(n)]
        for cp in gather:
            cp.start()
        for r in range(1, N_DEV):
            for k in range(n):
                remote(out_refs[k].at[me ^ r], out_refs[k].at[me ^ r], k, 1, r).wait_recv()
        for cp in gather:
            cp.wait_send()

    return _pallas(
        body, slabs, name="all_reduce_small", grid=(), in_specs=[VMEM] * n, out_specs=[VMEM] * n,
        out_shape=[jax.ShapeDtypeStruct(s.shape, F32) for s in slabs],
        scratch_shapes=[pltpu.VMEM(s.shape, F32) for s in slabs]
        + [pltpu.SemaphoreType.DMA((n, 2, N_DEV)), pltpu.SemaphoreType.DMA((n, 2, N_DEV))], jobs=jobs)


def _cast_into_whole(w, name, place):
    rows, cols = w.shape
    tr = rows // 2

    def body(place_ref, w_ref, o_ref):
        o_ref[...] = w_ref[...].astype(BF16)

    if LARGE[name] == "col":
        window = pl.BlockSpec((tr, cols), lambda i, place_ref: (i, place_ref[0]))
    else:
        window = pl.BlockSpec((tr, cols), lambda i, place_ref: (2 * place_ref[0] + i, 0))
    return pl.pallas_call(
        body, name=name + "_cast",
        grid_spec=pltpu.PrefetchScalarGridSpec(num_scalar_prefetch=1, grid=(2,),
                                               in_specs=[pl.BlockSpec((tr, cols), lambda i, place_ref: (i, 0))], out_specs=window),
        out_shape=jax.ShapeDtypeStruct(LARGE_SHAPE[name], BF16),
        compiler_params=_params(dimension_semantics=("parallel",)))(place, w)


def _adamw_math(w, g, m, v):
    m = ADAM_B1 * m + (1.0 - ADAM_B1) * g
    v = ADAM_B2 * v + (1.0 - ADAM_B2) * (g * g)
    m_hat = m / (1.0 - ADAM_B1 ** ADAM_STEP)
    v_hat = v / (1.0 - ADAM_B2 ** ADAM_STEP)
    delta = -ADAM_LR * (m_hat / (jnp.sqrt(v_hat) + ADAM_EPS) + ADAM_WD * w)
    return delta, m, v


def _adamw_large(w, g, m, v, name):
    rows, cols = w.shape
    tr = rows // 4

    def body(w_ref, g_ref, m_ref, v_ref, d_ref, mo_ref, vo_ref):
        d_ref[...], mo_ref[...], vo_ref[...] = _adamw_math(w_ref[...], g_ref[...], m_ref[...], v_ref[...])

    blk = pl.BlockSpec((tr, cols), lambda i: (i, 0))
    out = jax.ShapeDtypeStruct(w.shape, F32)
    return pl.pallas_call(body, name=name + "_adamw", grid=(4,), in_specs=[blk] * 4, out_specs=[blk] * 3, out_shape=[out] * 3,
                          compiler_params=_params(dimension_semantics=("parallel",)))(w, g, m, v)


def _adamw_small(ws, gs, ms, vs):
    n = len(ws)

    def body(*refs):
        for k in range(n):
            w_ref, g_ref, m_ref, v_ref = (refs[q * n + k] for q in range(4))
            d_ref, mo_ref, vo_ref = (refs[(4 + q) * n + k] for q in range(3))
            d_ref[...], mo_ref[...], vo_ref[...] = _adamw_math(w_ref[...], g_ref[...], m_ref[...], v_ref[...])

    out = [jax.ShapeDtypeStruct(w.shape, F32) for w in ws]
    res = pl.pallas_call(body, name="small_adamw", in_specs=[VMEM] * (4 * n), out_specs=[VMEM] * (3 * n), out_shape=out * 3,
                         compiler_params=_params())(*ws, *gs, *ms, *vs)
    return res[:n], res[n:2 * n], res[2 * n:]


WEIGHTS = ["norm_mix", "w_in", "w_pool_grp", "pool_scale", "w_pool_out", "conv_w", "conv_b", "w_rg_a", "b_rg_a", "w_rg_x",
           "b_rg_x", "lru_lambda", "w_rnn_out", "w_o", "norm_ffn", "w_ffn_in", "w_ffn_out", "norm_final"]
VEC_ITEMS = ["norm_mix", "norm_ffn", "norm_final", "pool_scale", "conv_b", "lru_lambda", "b_rg_a", "b_rg_x"]
MAT_ITEMS = ["w_pool_grp", "w_rg_a", "w_rg_x"]


def _as2d(name, a):
    if name in MAT_ITEMS:
        return a.reshape(-1, HEAD, HEAD)
    if name == "conv_w":
        return a.reshape(CONV_WIDTH, -1)
    return a.reshape(1, -1)


def kernel(x, norm_mix, w_in, w_pool_grp, pool_scale, w_pool_out, conv_w, conv_b, w_rg_a, b_rg_a, w_rg_x, b_rg_x, lru_lambda, w_rnn_out, w_o, norm_ffn, w_ffn_in, w_ffn_out, norm_final, loss_target, m_norm_mix, m_w_in, m_w_pool_grp, m_pool_scale, m_w_pool_out, m_conv_w, m_conv_b, m_w_rg_a, m_b_rg_a, m_w_rg_x, m_b_rg_x, m_lru_lambda, m_w_rnn_out, m_w_o, m_norm_ffn, m_w_ffn_in, m_w_ffn_out, m_norm_final, v_norm_mix, v_w_in, v_w_pool_grp, v_pool_scale, v_w_pool_out, v_conv_w, v_conv_b, v_w_rg_a, v_b_rg_a, v_w_rg_x, v_b_rg_x, v_lru_lambda, v_w_rnn_out, v_w_o, v_norm_ffn, v_w_ffn_in, v_w_ffn_out, v_norm_final):
    given = dict(locals())
    w = {name: given[name] for name in WEIGHTS}
    m = {name: given["m_" + name] for name in WEIGHTS}
    v = {name: given["v_" + name] for name in WEIGHTS}
    chip, c = _place()

    place = jnp.stack([chip, c]).astype(jnp.int32)
    conv_cols = w["conv_w"].shape[-1]
    conv_w_mine = lax.dynamic_update_slice_in_dim(jnp.zeros((CONV_WIDTH, D_RNN), F32), w["conv_w"][0], chip * conv_cols, axis=1)
    full = {name: _cast_into_whole(w[name][0], name, place) for name in LARGE}
    small = {name: _as2d(name, w[name]) for name in WEIGHTS if name not in LARGE and name != "conv_w"}
    sq_cols, grad_x, grads = _step(x[0], loss_target[0], small, full, conv_w_mine, place)
    loss = 0.5 / D_MODEL * jnp.sum(sq_cols)
    grads["conv_w"] = lax.dynamic_slice_in_dim(grads["conv_w"], chip * conv_cols, conv_cols, axis=1)

    delta, new_m, new_v = {}, {}, {}
    for name in LARGE:
        delta[name], new_m[name], new_v[name] = _adamw_large(w[name][0], grads[name], m[name][0], v[name][0], name)
    small_names = [name for name in WEIGHTS if name not in LARGE]
    flat = lambda d: [d[name].reshape(grads[name].shape) for name in small_names]
    ds, mo, vo = _adamw_small(flat(w), [grads[name] for name in small_names], flat(m), flat(v))
    for k, name in enumerate(small_names):
        delta[name], new_m[name], new_v[name] = ds[k], mo[k], vo[k]

    shaped = lambda d: [d[name].reshape(w[name].shape) for name in WEIGHTS]
    return (loss, grad_x[None], *shaped(grads), *shaped(delta), *shaped(new_m), *shaped(new_v))
```

```python
import functools
import math

import jax
import jax.numpy as jnp
from jax import lax
from jax.experimental import pallas as pl
from jax.experimental.pallas import tpu as pltpu

F32 = jnp.float32
BF16 = jnp.bfloat16

D_MODEL = 1024
D_POOL = 512
N_POOL_GROUPS = 4
D_RNN = 1024
N_RNN_HEADS = 8
HEAD = 128
CONV_WIDTH = 4
LRU_C = 8.0
D_FF = 2816
D_IN = D_POOL + 2 * D_RNN + 2 * D_MODEL
NORM_EPS = 1e-6
COL_RNN = D_POOL // HEAD
COL_GATE = (D_POOL + D_RNN) // HEAD

ADAM_LR = 0.001
ADAM_B1 = 0.9
ADAM_B2 = 0.999
ADAM_EPS = 1e-08
ADAM_WD = 0.01
ADAM_STEP = 10

N_CHIPS = 4
N_DEV = 8
MESH = pl.DeviceIdType.MESH
ANY = pl.BlockSpec(memory_space=pl.ANY)
VMEM = pl.BlockSpec(memory_space=pltpu.VMEM)
VMEM_LIMIT_BYTES = 60 * 1024 * 1024
SUBLANES = 8
POOL_HALO = 16
CHUNK = 256

GELU_C = math.sqrt(2.0 / math.pi)
GELU_A = 0.044715


def _params(**kw):
    return pltpu.CompilerParams(vmem_limit_bytes=VMEM_LIMIT_BYTES, **kw)


def _sigmoid(x):
    return 0.5 * jnp.tanh(0.5 * x) + 0.5


def _log1p(y):
    u = 1.0 + y
    d = u - 1.0
    return jnp.where(d == 0.0, y, jnp.log(u) * (y / jnp.where(d == 0.0, 1.0, d)))


def _gelu_parts(x):
    x2 = x * x
    th = jnp.tanh(GELU_C * (x + GELU_A * x * x2))
    g = 0.5 * x * (1.0 + th)
    dg = 0.5 * (1.0 + th) + 0.5 * x * (1.0 - th * th) * GELU_C * (1.0 + 3.0 * GELU_A * x2)
    return g, dg


def _dot(a, b):
    return jnp.dot(a, b, preferred_element_type=F32)


def _dot_nt(a, b):
    return lax.dot_general(a, b, (((1,), (1,)), ((), ())), preferred_element_type=F32)


def _dot_tn(a, b):
    return lax.dot_general(a, b, (((0,), (0,)), ((), ())), preferred_element_type=F32)


def _rms_scale(xv):
    return lax.rsqrt(jnp.mean(xv * xv, axis=-1, keepdims=True) + NORM_EPS)


def _rms_bwd(dy, xv, g):
    r = _rms_scale(xv)
    xh = xv * r
    dyg = dy * g
    dx = r * (dyg - xh * jnp.mean(dyg * xh, axis=-1, keepdims=True))
    return dx, dy * xh


class _Job:
    def __init__(self, inputs, out_shapes, aliases, sem_shape, start, finish):
        self.inputs, self.out_shapes, self.aliases, self.sem_shape = list(inputs), list(out_shapes), dict(aliases), sem_shape
        self.start, self.finish = start, finish


def _pallas(body, operands, *, name, grid, in_specs, out_specs, out_shape, scratch_shapes=(), semantics=None, jobs=()):
    n_in, n_out, n_scr = len(in_specs), len(out_specs), len(scratch_shapes)
    job_in = [a for job in jobs for a in job.inputs]
    job_out = [s for job in jobs for s in job.out_shapes]
    aliases, i0, o0 = {}, n_in, n_out
    for job in jobs:
        aliases.update({i0 + i: o0 + o for i, o in job.aliases.items()})
        i0, o0 = i0 + len(job.inputs), o0 + len(job.out_shapes)

    def whole(*refs):
        ins, j_ins = refs[:n_in], refs[n_in:n_in + len(job_in)]
        outs = refs[n_in + len(job_in):][:n_out]
        j_outs = refs[n_in + len(job_in) + n_out:][:len(job_out)]
        rest = refs[n_in + len(job_in) + n_out + len(job_out):]
        scr, sems = rest[:n_scr], rest[n_scr:]

        def run(phase):
            i, o = 0, 0
            for k, job in enumerate(jobs):
                getattr(job, phase)(j_ins[i:i + len(job.inputs)], j_outs[o:o + len(job.out_shapes)], sems[2 * k], sems[2 * k + 1])
                i, o = i + len(job.inputs), o + len(job.out_shapes)

        def at(step_of, phase):
            if not jobs:
                return
            if not grid:
                run(phase)
                return
            cond = functools.reduce(jnp.logical_and, [pl.program_id(d) == step_of(d) for d in range(len(grid))])
            pl.when(cond)(functools.partial(run, phase))

        at(lambda d: 0, "start")
        body(*ins, *outs, *scr)
        at(lambda d: grid[d] - 1, "finish")

    res = pl.pallas_call(
        whole, name=name, grid=grid, in_specs=list(in_specs) + [ANY] * len(job_in), out_specs=list(out_specs) + [ANY] * len(job_out),
        out_shape=list(out_shape) + job_out, input_output_aliases=aliases,
        scratch_shapes=list(scratch_shapes) + [pltpu.SemaphoreType.DMA(job.sem_shape) for job in jobs for _ in range(2)],
        compiler_params=_params(dimension_semantics=semantics, has_side_effects=bool(jobs)),
    )(*operands, *job_in)
    per_job, o = [], n_out
    for job in jobs:
        per_job.append(res[o:o + len(job.out_shapes)])
        o += len(job.out_shapes)
    return res[:n_out], per_job


def _run_jobs(jobs, name):
    return _pallas(lambda: None, [], name=name, grid=(), in_specs=[], out_specs=[], out_shape=[], jobs=jobs)[1]


NORM_ROWS = 256


def _norm_rows(x_ref, g_ref, h_ref):
    g = g_ref[...]

    def rows(i, carry):
        r = pl.ds(pl.multiple_of(i * NORM_ROWS, NORM_ROWS), NORM_ROWS)
        xv = x_ref[r, :]
        h_ref[r, :] = (xv * _rms_scale(xv) * g).astype(BF16)
        return carry

    lax.fori_loop(0, x_ref.shape[0] // NORM_ROWS, rows, 0)


def _norm_matmul(x, g, w, *, tm, tn, name, jobs=()):
    T, K = x.shape
    N = w.shape[1]

    def body(x_ref, g_ref, w_ref, o_ref, h_ref):
        @pl.when(pl.program_id(1) == 0)
        def _():
            _norm_rows(x_ref, g_ref, h_ref)

        o_ref[...] = _dot(h_ref[...], w_ref[...])

    return _pallas(
        body, (x, g, w), name=name, grid=(T // tm, N // tn),
        in_specs=[pl.BlockSpec((tm, K), lambda i, j: (i, 0)), pl.BlockSpec((1, K), lambda i, j: (0, 0)),
                  pl.BlockSpec((K, tn), lambda i, j: (0, j))],
        out_specs=[pl.BlockSpec((tm, tn), lambda i, j: (i, j)), pl.BlockSpec((tm, K), lambda i, j: (i, 0))],
        out_shape=[jax.ShapeDtypeStruct((T, N), F32), jax.ShapeDtypeStruct((T, K), BF16)],
        semantics=("parallel", "arbitrary"), jobs=jobs)


def _ffn_in(x2, g, w, *, tm, tn):
    T, K = x2.shape
    nb = D_FF // tn

    def body(x_ref, g_ref, wg_ref, wu_ref, gate_ref, up_ref, act_ref, h_ref):
        @pl.when(pl.program_id(1) == 0)
        def _():
            _norm_rows(x_ref, g_ref, h_ref)

        h = h_ref[...]
        gate = _dot(h, wg_ref[...])
        up = _dot(h, wu_ref[...])
        gate_ref[...] = gate
        up_ref[...] = up
        act_ref[...] = (gate * _sigmoid(gate) * up).astype(BF16)

    blk = pl.BlockSpec((tm, tn), lambda i, j: (i, j))
    return pl.pallas_call(
        body, name="ffn_in", grid=(T // tm, nb),
        in_specs=[pl.BlockSpec((tm, K), lambda i, j: (i, 0)), pl.BlockSpec((1, K), lambda i, j: (0, 0)),
                  pl.BlockSpec((K, tn), lambda i, j: (0, j)), pl.BlockSpec((K, tn), lambda i, j: (0, j + nb))],
        out_specs=[blk, blk, blk, pl.BlockSpec((tm, K), lambda i, j: (i, 0))],
        out_shape=[jax.ShapeDtypeStruct((T, D_FF), F32), jax.ShapeDtypeStruct((T, D_FF), F32),
                   jax.ShapeDtypeStruct((T, D_FF), BF16), jax.ShapeDtypeStruct((T, K), BF16)],
        compiler_params=_params(dimension_semantics=("parallel", "arbitrary")),
    )(x2, g, w, w)


def _branch_mix(pm, z, w_pool_out, w_rnn_out, proj, *, tm, tn):
    T = pm.shape[0]
    col_gp = (D_POOL + 2 * D_RNN) // tn
    col_gr = col_gp + D_MODEL // tn

    def body(pm_ref, z_ref, wp_ref, wr_ref, gp_ref, gr_ref, yp_ref, yr_ref, mix_ref):
        yp = _dot(pm_ref[...], wp_ref[...])
        yr = _dot(z_ref[...], wr_ref[...])
        yp_ref[...] = yp
        yr_ref[...] = yr
        mix_ref[...] = (_sigmoid(gp_ref[...]) * yp + _sigmoid(gr_ref[...]) * yr).astype(BF16)

    blk = pl.BlockSpec((tm, tn), lambda i, j: (i, j))
    return pl.pallas_call(
        body, name="branch_mix", grid=(T // tm, D_MODEL // tn),
        in_specs=[pl.BlockSpec((tm, D_POOL), lambda i, j: (i, 0)), pl.BlockSpec((tm, D_RNN), lambda i, j: (i, 0)),
                  pl.BlockSpec((D_POOL, tn), lambda i, j: (0, j)), pl.BlockSpec((D_RNN, tn), lambda i, j: (0, j)),
                  pl.BlockSpec((tm, tn), lambda i, j: (i, col_gp + j)), pl.BlockSpec((tm, tn), lambda i, j: (i, col_gr + j))],
        out_specs=[blk, blk, blk],
        out_shape=[jax.ShapeDtypeStruct((T, D_MODEL), F32), jax.ShapeDtypeStruct((T, D_MODEL), F32),
                   jax.ShapeDtypeStruct((T, D_MODEL), BF16)],
        compiler_params=_params(dimension_semantics=("parallel", "parallel")),
    )(pm, z, w_pool_out, w_rnn_out, proj, proj)


def _out_proj_residual(mix, w_o, x, *, tm):
    T = x.shape[0]

    def body(mix_ref, w_ref, x_ref, o_ref):
        o_ref[...] = x_ref[...] + _dot(mix_ref[...], w_ref[...])

    row = pl.BlockSpec((tm, D_MODEL), lambda i: (i, 0))
    return pl.pallas_call(
        body, name="out_proj_residual", grid=(T // tm,),
        in_specs=[row, pl.BlockSpec((D_MODEL, D_MODEL), lambda i: (0, 0)), row],
        out_specs=row, out_shape=jax.ShapeDtypeStruct((T, D_MODEL), F32),
        compiler_params=_params(dimension_semantics=("parallel",)),
    )(mix, w_o, x)


def _ffn_out_loss(act, w, x2, g3, target, *, tm):
    T = x2.shape[0]

    def body(act_ref, w_ref, x2_ref, g_ref, t_ref, dx_ref, dxb_ref, sq_ref, dg_ref):
        @pl.when(pl.program_id(0) == 0)
        def _():
            sq_ref[...] = jnp.zeros_like(sq_ref)
            dg_ref[...] = jnp.zeros_like(dg_ref)

        x3 = x2_ref[...] + _dot(act_ref[...], w_ref[...])
        g = g_ref[...]
        err = x3 * _rms_scale(x3) * g - t_ref[...]
        sq_ref[...] += jnp.sum(err * err, axis=0, keepdims=True)
        dx, dgp = _rms_bwd(err * (1.0 / D_MODEL), x3, g)
        dg_ref[...] += jnp.sum(dgp, axis=0, keepdims=True)
        dx_ref[...] = dx
        dxb_ref[...] = dx.astype(BF16)

    row = pl.BlockSpec((tm, D_MODEL), lambda i: (i, 0))
    vec = pl.BlockSpec((1, D_MODEL), lambda i: (0, 0))
    return pl.pallas_call(
        body, name="ffn_out_loss", grid=(T // tm,),
        in_specs=[pl.BlockSpec((tm, D_FF), lambda i: (i, 0)), pl.BlockSpec((D_FF, D_MODEL), lambda i: (0, 0)), row, vec, row],
        out_specs=[row, row, vec, vec],
        out_shape=[jax.ShapeDtypeStruct((T, D_MODEL), F32), jax.ShapeDtypeStruct((T, D_MODEL), BF16),
                   jax.ShapeDtypeStruct((1, D_MODEL), F32), jax.ShapeDtypeStruct((1, D_MODEL), F32)],
        compiler_params=_params(dimension_semantics=("arbitrary",)),
    )(act, w, x2, g3, target)


def _ffn_out_bwd(dx3b, w, gate, up, *, tm, tn):
    T = dx3b.shape[0]

    def body(dx_ref, w_ref, gate_ref, up_ref, dgate_ref, dup_ref):
        dact = _dot_nt(dx_ref[...], w_ref[...])
        gate = gate_ref[...]
        s = _sigmoid(gate)
        dgate_ref[...] = (dact * up_ref[...] * s * (1.0 + gate * (1.0 - s))).astype(BF16)
        dup_ref[...] = (dact * gate * s).astype(BF16)

    blk = pl.BlockSpec((tm, tn), lambda i, j: (i, j))
    return pl.pallas_call(
        body, name="ffn_out_bwd", grid=(T // tm, D_FF // tn),
        in_specs=[pl.BlockSpec((tm, D_MODEL), lambda i, j: (i, 0)), pl.BlockSpec((tn, D_MODEL), lambda i, j: (j, 0)), blk, blk],
        out_specs=[blk, blk],
        out_shape=[jax.ShapeDtypeStruct((T, D_FF), BF16), jax.ShapeDtypeStruct((T, D_FF), BF16)],
        compiler_params=_params(dimension_semantics=("parallel", "parallel")),
    )(dx3b, w, gate, up)


def _ffn_in_bwd(dgate, dup, w, dx3, x2, g2, *, tm):
    T = x2.shape[0]

    def body(dgate_ref, dup_ref, w_ref, dx3_ref, x2_ref, g_ref, dx_ref, dxb_ref, dg_ref):
        @pl.when(pl.program_id(0) == 0)
        def _():
            dg_ref[...] = jnp.zeros_like(dg_ref)

        dh = _dot_nt(dgate_ref[...], w_ref[:, :D_FF]) + _dot_nt(dup_ref[...], w_ref[:, D_FF:])
        dxn, dgp = _rms_bwd(dh, x2_ref[...], g_ref[...])
        dx = dx3_ref[...] + dxn
        dg_ref[...] += jnp.sum(dgp, axis=0, keepdims=True)
        dx_ref[...] = dx
        dxb_ref[...] = dx.astype(BF16)

    row = pl.BlockSpec((tm, D_MODEL), lambda i: (i, 0))
    wide = pl.BlockSpec((tm, D_FF), lambda i: (i, 0))
    vec = pl.BlockSpec((1, D_MODEL), lambda i: (0, 0))
    return pl.pallas_call(
        body, name="ffn_in_bwd", grid=(T // tm,),
        in_specs=[wide, wide, pl.BlockSpec((D_MODEL, 2 * D_FF), lambda i: (0, 0)), row, row, vec],
        out_specs=[row, row, vec],
        out_shape=[jax.ShapeDtypeStruct((T, D_MODEL), F32), jax.ShapeDtypeStruct((T, D_MODEL), BF16),
                   jax.ShapeDtypeStruct((1, D_MODEL), F32)],
        compiler_params=_params(dimension_semantics=("arbitrary",)),
    )(dgate, dup, w, dx3, x2, g2)


def _out_proj_bwd(dx2b, w_o, proj, y_pool, y_rnn, *, tm, tn, jobs=()):
    T = dx2b.shape[0]
    col_gp = (D_POOL + 2 * D_RNN) // tn
    col_gr = col_gp + D_MODEL // tn

    def body(dx_ref, w_ref, gp_ref, gr_ref, yp_ref, yr_ref, dgp_ref, dgr_ref, dyp_ref, dyr_ref):
        dmix = _dot_nt(dx_ref[...], w_ref[...])
        sp = _sigmoid(gp_ref[...])
        sr = _sigmoid(gr_ref[...])
        dgp_ref[...] = (dmix * yp_ref[...] * sp * (1.0 - sp)).astype(BF16)
        dgr_ref[...] = (dmix * yr_ref[...] * sr * (1.0 - sr)).astype(BF16)
        dyp_ref[...] = (dmix * sp).astype(BF16)
        dyr_ref[...] = (dmix * sr).astype(BF16)

    blk = pl.BlockSpec((tm, tn), lambda i, j: (i, j))
    out = jax.ShapeDtypeStruct((T, D_MODEL), BF16)
    return _pallas(
        body, (dx2b, w_o, proj, proj, y_pool, y_rnn), name="out_proj_bwd", grid=(T // tm, D_MODEL // tn),
        in_specs=[pl.BlockSpec((tm, D_MODEL), lambda i, j: (i, 0)), pl.BlockSpec((tn, D_MODEL), lambda i, j: (j, 0)),
                  pl.BlockSpec((tm, tn), lambda i, j: (i, col_gp + j)), pl.BlockSpec((tm, tn), lambda i, j: (i, col_gr + j)), blk, blk],
        out_specs=[blk, blk, blk, blk], out_shape=[out, out, out, out], semantics=("parallel", "parallel"), jobs=jobs)


def _branch_bwd(dyp, dyr, w_pool_out, w_rnn_out, *, tm):
    T = dyp.shape[0]

    def body(dyp_ref, dyr_ref, wp_ref, wr_ref, dpm_ref, dz_ref):
        dpm_ref[...] = _dot_nt(dyp_ref[...], wp_ref[...])
        dz_ref[...] = _dot_nt(dyr_ref[...], wr_ref[...])

    row = pl.BlockSpec((tm, D_MODEL), lambda i: (i, 0))
    return pl.pallas_call(
        body, name="branch_bwd", grid=(T // tm,),
        in_specs=[row, row, pl.BlockSpec((D_POOL, D_MODEL), lambda i: (0, 0)), pl.BlockSpec((D_RNN, D_MODEL), lambda i: (0, 0))],
        out_specs=[pl.BlockSpec((tm, D_POOL), lambda i: (i, 0)), pl.BlockSpec((tm, D_RNN), lambda i: (i, 0))],
        out_shape=[jax.ShapeDtypeStruct((T, D_POOL), F32), jax.ShapeDtypeStruct((T, D_RNN), F32)],
        compiler_params=_params(dimension_semantics=("parallel",)),
    )(dyp, dyr, w_pool_out, w_rnn_out)


def _in_proj_bwd(segs, w, dx2, x, g1, *, tm):
    T = x.shape[0]
    widths = [s.shape[1] for s in segs]
    offs = [sum(widths[:k]) for k in range(len(widths))]
    n = len(segs)

    def body(*refs):
        seg_refs, (w_ref, dx2_ref, x_ref, g_ref, dx_ref, dg_ref) = refs[:n], refs[n:]

        @pl.when(pl.program_id(0) == 0)
        def _():
            dg_ref[...] = jnp.zeros_like(dg_ref)

        dh = _dot_nt(seg_refs[0][...], w_ref[:, offs[0]:offs[0] + widths[0]])
        for k in range(1, n):
            dh += _dot_nt(seg_refs[k][...], w_ref[:, offs[k]:offs[k] + widths[k]])
        dxn, dgp = _rms_bwd(dh, x_ref[...], g_ref[...])
        dg_ref[...] += jnp.sum(dgp, axis=0, keepdims=True)
        dx_ref[...] = dx2_ref[...] + dxn

    row = pl.BlockSpec((tm, D_MODEL), lambda i: (i, 0))
    vec = pl.BlockSpec((1, D_MODEL), lambda i: (0, 0))
    return pl.pallas_call(
        body, name="in_proj_bwd", grid=(T // tm,),
        in_specs=[pl.BlockSpec((tm, wd), lambda i: (i, 0)) for wd in widths]
        + [pl.BlockSpec((D_MODEL, D_IN), lambda i: (0, 0)), row, row, vec],
        out_specs=[row, vec],
        out_shape=[jax.ShapeDtypeStruct((T, D_MODEL), F32), jax.ShapeDtypeStruct((1, D_MODEL), F32)],
        compiler_params=_params(dimension_semantics=("arbitrary",)),
    )(*segs, w, dx2, x, g1)


def _weight_grad(a, segs, *, tm, tn, name, jobs=None):
    T, M = a.shape
    nblk = [s.shape[1] // tn for s in segs]
    first = [sum(nblk[:k]) for k in range(len(segs))]
    n = len(segs)

    def body(a_ref, *refs):
        seg_refs, o_ref = refs[:n], refs[n]
        j = pl.program_id(1)
        for k in range(n):
            @pl.when((j >= first[k]) & (j < first[k] + nblk[k]))
            def _(k=k):
                o_ref[...] = _dot_tn(a_ref[...], seg_refs[k][...])

    def seg_spec(k):
        return pl.BlockSpec((T, tn), lambda i, j: (0, jnp.clip(j - first[k], 0, nblk[k] - 1)))

    (grad,), results = _pallas(
        body, (a, *segs), name=name, grid=(M // tm, sum(nblk)),
        in_specs=[pl.BlockSpec((T, tm), lambda i, j: (0, i))] + [seg_spec(k) for k in range(n)],
        out_specs=[pl.BlockSpec((tm, tn), lambda i, j: (i, j))],
        out_shape=[jax.ShapeDtypeStruct((M, sum(nblk) * tn), F32)],
        semantics=("parallel", "arbitrary"), jobs=jobs or ())
    return grad if jobs is None else (grad, results)


def _pad_front(dst, src, halo):
    dst[pl.ds(0, halo), :] = jnp.zeros((halo, src.shape[1]), F32)

    def fill(i, carry):
        r0 = pl.multiple_of(i * CHUNK, CHUNK)
        dst[pl.ds(r0 + halo, CHUNK), :] = src[pl.ds(r0, CHUNK), :]
        return carry

    lax.fori_loop(0, src.shape[0] // CHUNK, fill, 0)


def _shift_rows(v, k):
    return pltpu.roll(v, k % v.shape[0], axis=0)


def _window_sums(xs, direction):
    s2 = xs + _shift_rows(xs, direction)
    s4 = s2 + _shift_rows(s2, 2 * direction)
    s8 = s4 + _shift_rows(s4, 4 * direction)
    s16 = s8 + _shift_rows(s8, 8 * direction)
    return s2, s4, s8, s16


def _select_window(g, sums):
    s2, s4, s8, s16 = sums
    return jnp.where(g == 0, s2, jnp.where(g == 1, s4, jnp.where(g == 2, s8, s16)))


def _pool_count(g, start, rows):
    t = start + lax.broadcasted_iota(jnp.int32, (rows, 1), 0)
    return jnp.minimum(t + 1, jnp.left_shift(2, g)).astype(F32)


def _pool_fwd(proj, w_grp, scale):
    T = proj.shape[0]
    nchunk = T // CHUNK

    def body(u_ref, w_ref, s_ref, o_ref, upad):
        g = pl.program_id(0)
        _pad_front(upad, u_ref, POOL_HALO)
        w = w_ref[...].astype(BF16)
        scale_row = s_ref[...]

        def chunk(i, carry):
            r0 = pl.multiple_of(i * CHUNK, CHUNK)
            xs = upad[pl.ds(r0, CHUNK + POOL_HALO), :]
            win = _select_window(g, _window_sums(xs, 1))[POOL_HALO:]
            pooled = win / _pool_count(g, r0, CHUNK) - xs[POOL_HALO:]
            o_ref[pl.ds(r0, CHUNK), :] = (_dot(pooled.astype(BF16), w) * scale_row).astype(BF16)
            return carry

        lax.fori_loop(0, nchunk, chunk, 0)

    return pl.pallas_call(
        body, name="pool_fwd", grid=(N_POOL_GROUPS,),
        in_specs=[pl.BlockSpec((T, HEAD), lambda g: (0, g)), pl.BlockSpec((None, HEAD, HEAD), lambda g: (g, 0, 0)),
                  pl.BlockSpec((1, HEAD), lambda g: (0, g))],
        out_specs=pl.BlockSpec((T, HEAD), lambda g: (0, g)),
        out_shape=jax.ShapeDtypeStruct((T, D_POOL), BF16),
        scratch_shapes=[pltpu.VMEM((T + POOL_HALO, HEAD), F32)],
        compiler_params=_params(dimension_semantics=("parallel",)),
    )(proj, w_grp, scale)


def _pool_bwd(proj, dpm, w_grp, scale, jobs=()):
    T = proj.shape[0]
    nchunk = T // CHUNK

    def body(u_ref, dpm_ref, w_ref, s_ref, du_ref, dw_ref, ds_ref, upad, zpad, dpool):
        g = pl.program_id(0)
        _pad_front(upad, u_ref, POOL_HALO)
        zpad[pl.ds(T, POOL_HALO), :] = jnp.zeros((POOL_HALO, HEAD), F32)
        dw_ref[...] = jnp.zeros_like(dw_ref)
        ds_ref[...] = jnp.zeros_like(ds_ref)
        w = w_ref[...].astype(BF16)
        scale_row = s_ref[...]

        def chunk(i, carry):
            r0 = pl.multiple_of(i * CHUNK, CHUNK)
            xs = upad[pl.ds(r0, CHUNK + POOL_HALO), :]
            cnt = _pool_count(g, r0, CHUNK)
            pooled = (_select_window(g, _window_sums(xs, 1))[POOL_HALO:] / cnt - xs[POOL_HALO:]).astype(BF16)
            mixed = _dot(pooled, w)
            d = dpm_ref[pl.ds(r0, CHUNK), :]
            ds_ref[...] += jnp.sum(d * mixed, axis=0, keepdims=True)
            dmixed = (d * scale_row).astype(BF16)
            dw_ref[...] += _dot_tn(pooled, dmixed)
            dp = _dot_nt(dmixed, w)
            dpool[pl.ds(r0, CHUNK), :] = dp
            zpad[pl.ds(r0, CHUNK), :] = dp / cnt
            return carry

        lax.fori_loop(0, nchunk, chunk, 0)

        def chunk2(i, carry):
            r0 = pl.multiple_of(i * CHUNK, CHUNK)
            zs = zpad[pl.ds(r0, CHUNK + POOL_HALO), :]
            win = _select_window(g, _window_sums(zs, -1))[:CHUNK]
            du_ref[pl.ds(r0, CHUNK), :] = (win - dpool[pl.ds(r0, CHUNK), :]).astype(BF16)
            return carry

        lax.fori_loop(0, nchunk, chunk2, 0)

    col = pl.BlockSpec((T, HEAD), lambda g: (0, g))
    return _pallas(
        body, (proj, dpm, w_grp, scale), name="pool_bwd", grid=(N_POOL_GROUPS,),
        in_specs=[col, col, pl.BlockSpec((None, HEAD, HEAD), lambda g: (g, 0, 0)), pl.BlockSpec((1, HEAD), lambda g: (0, g))],
        out_specs=[col, pl.BlockSpec((None, HEAD, HEAD), lambda g: (g, 0, 0)), pl.BlockSpec((1, HEAD), lambda g: (0, g))],
        out_shape=[jax.ShapeDtypeStruct((T, D_POOL), BF16), jax.ShapeDtypeStruct((N_POOL_GROUPS, HEAD, HEAD), F32),
                   jax.ShapeDtypeStruct((1, D_POOL), F32)],
        scratch_shapes=[pltpu.VMEM((T + POOL_HALO, HEAD), F32), pltpu.VMEM((T + POOL_HALO, HEAD), F32), pltpu.VMEM((T, HEAD), F32)],
        semantics=("parallel",), jobs=jobs)


def _conv_taps(xs, cw):
    v = cw[CONV_WIDTH - 1] * xs[SUBLANES:]
    for k in range(CONV_WIDTH - 1):
        v += cw[k] * _shift_rows(xs, CONV_WIDTH - 1 - k)[SUBLANES:]
    return v


def _tap_rows(cw_ref):
    return [cw_ref[k:k + 1, :] for k in range(CONV_WIDTH)]


def _softplus_neg(lam):
    return jnp.maximum(-lam, 0.0) + _log1p(jnp.exp(-jnp.abs(lam)))


def _lru_gates(v, wa, ba, wx, bx, sp):
    vb = v.astype(BF16)
    ra = _sigmoid(_dot(vb, wa) + ba)
    ix = _sigmoid(_dot(vb, wx) + bx)
    log_a = -LRU_C * ra * sp
    a = jnp.exp(log_a)
    sq = jnp.sqrt(-jnp.tanh(log_a) * (a * a + 1.0))
    return ra, ix, a, sq


def _row_bcast(v, r):
    return jnp.broadcast_to(v[r:r + 1, :], v.shape)


def _tile_scans(a, b, direction):
    ri = lax.broadcasted_iota(jnp.int32, a.shape, 0) % SUBLANES
    A, B = a, b
    for s in (1, 2, 4):
        ok = (ri >= s) if direction == 1 else (ri + s < SUBLANES)
        As, Bs = _shift_rows(A, s * direction), _shift_rows(B, s * direction)
        B = jnp.where(ok, A * Bs + B, B)
        A = jnp.where(ok, A * As, A)
    return A, B


TILES_PER_STEP = 8


def _carry_tiles(A_s, B_s, out, ntile, direction):
    out_row = SUBLANES - 1 if direction == 1 else 0

    def step(k, carry):
        for j in range(TILES_PER_STEP):
            t = k * TILES_PER_STEP + j
            r0 = pl.multiple_of((t if direction == 1 else ntile - 1 - t) * SUBLANES, SUBLANES)
            A, B = A_s[pl.ds(r0, SUBLANES), :], B_s[pl.ds(r0, SUBLANES), :]
            out[pl.ds(r0, SUBLANES), :] = A * carry + B
            carry = _row_bcast(A, out_row) * carry + _row_bcast(B, out_row)
        return carry

    lax.fori_loop(0, ntile // TILES_PER_STEP, step, jnp.zeros((SUBLANES, HEAD), F32))


def _rnn_fwd(proj, conv_w, conv_b, w_a, b_a, w_x, b_x, lam, jobs=()):
    T = proj.shape[0]
    nchunk = T // CHUNK
    ntile = T // SUBLANES

    def body(u_ref, ug_ref, cw_ref, cb_ref, wa_ref, ba_ref, wx_ref, bx_ref, lam_ref, h_ref, z_ref, upad, a_s, b_s):
        _pad_front(upad, u_ref, SUBLANES)
        cw, cb = _tap_rows(cw_ref), cb_ref[...]
        wa, wx = wa_ref[...].astype(BF16), wx_ref[...].astype(BF16)
        ba, bx = ba_ref[...], bx_ref[...]
        sp = _softplus_neg(lam_ref[...])

        def chunk(i, carry):
            r0 = pl.multiple_of(i * CHUNK, CHUNK)
            v = _conv_taps(upad[pl.ds(r0, CHUNK + SUBLANES), :], cw) + cb
            _, ix, a, sq = _lru_gates(v, wa, ba, wx, bx, sp)
            a_s[pl.ds(r0, CHUNK), :], b_s[pl.ds(r0, CHUNK), :] = _tile_scans(a, sq * ix * v, 1)
            return carry

        lax.fori_loop(0, nchunk, chunk, 0)
        _carry_tiles(a_s, b_s, h_ref, ntile, 1)

        def chunk3(i, carry):
            r0 = pl.multiple_of(i * CHUNK, CHUNK)
            gl, _ = _gelu_parts(ug_ref[pl.ds(r0, CHUNK), :])
            z_ref[pl.ds(r0, CHUNK), :] = (h_ref[pl.ds(r0, CHUNK), :] * gl).astype(BF16)
            return carry

        lax.fori_loop(0, nchunk, chunk3, 0)

    col = pl.BlockSpec((T, HEAD), lambda h: (0, h))
    vec = pl.BlockSpec((1, HEAD), lambda h: (0, h))
    mat = pl.BlockSpec((None, HEAD, HEAD), lambda h: (h, 0, 0))
    return _pallas(
        body, (proj, proj, conv_w, conv_b, w_a, b_a, w_x, b_x, lam), name="rnn_fwd", grid=(N_RNN_HEADS,),
        in_specs=[pl.BlockSpec((T, HEAD), lambda h: (0, COL_RNN + h)), pl.BlockSpec((T, HEAD), lambda h: (0, COL_GATE + h)),
                  pl.BlockSpec((CONV_WIDTH, HEAD), lambda h: (0, h)), vec, mat, vec, mat, vec, vec],
        out_specs=[col, col],
        out_shape=[jax.ShapeDtypeStruct((T, D_RNN), F32), jax.ShapeDtypeStruct((T, D_RNN), BF16)],
        scratch_shapes=[pltpu.VMEM((T + SUBLANES, HEAD), F32), pltpu.VMEM((T, HEAD), F32), pltpu.VMEM((T, HEAD), F32)],
        semantics=("parallel",), jobs=jobs)


def _rnn_bwd(proj, hr, dz, conv_w, conv_b, w_a, b_a, w_x, b_x, lam, jobs=()):
    T = proj.shape[0]
    nchunk = T // CHUNK
    ntile = T // SUBLANES

    def body(u_ref, ug_ref, h_ref, dz_ref, cw_ref, cb_ref, wa_ref, ba_ref, wx_ref, bx_ref, lam_ref,
             du_ref, dug_ref, dwa_ref, dwx_ref, dba_ref, dbx_ref, dlam_ref, dcb_ref, dcw_ref,
             upad, hpad, apad, v_s, ra_s, ix_s, sq_s, g_s, dvpad, ga_s):
        zero_tile = jnp.zeros((SUBLANES, HEAD), F32)
        _pad_front(upad, u_ref, SUBLANES)
        _pad_front(hpad, h_ref, SUBLANES)
        apad[pl.ds(T, SUBLANES), :] = zero_tile
        dvpad[pl.ds(T, SUBLANES), :] = zero_tile
        for ref in (dwa_ref, dwx_ref, dba_ref, dbx_ref, dlam_ref, dcb_ref, dcw_ref):
            ref[...] = jnp.zeros_like(ref)
        cw, cb = _tap_rows(cw_ref), cb_ref[...]
        wa, wx = wa_ref[...].astype(BF16), wx_ref[...].astype(BF16)
        ba, bx = ba_ref[...], bx_ref[...]
        lam_row = lam_ref[...]
        sp = _softplus_neg(lam_row)

        def chunk(i, carry):
            r0 = pl.multiple_of(i * CHUNK, CHUNK)
            rows = pl.ds(r0, CHUNK)
            h = h_ref[rows, :]
            v = _conv_taps(upad[pl.ds(r0, CHUNK + SUBLANES), :], cw) + cb
            ra, ix, a, sq = _lru_gates(v, wa, ba, wx, bx, sp)
            v_s[rows, :], ra_s[rows, :], ix_s[rows, :], sq_s[rows, :], apad[rows, :] = v, ra, ix, sq, a
            gl, dgl = _gelu_parts(ug_ref[rows, :])
            d = dz_ref[rows, :]
            g_s[rows, :] = d * gl
            dug_ref[rows, :] = (d * h * dgl).astype(BF16)
            return carry

        lax.fori_loop(0, nchunk, chunk, 0)

        def chunk2(i, carry):
            r0 = pl.multiple_of(i * CHUNK, CHUNK)
            rows = pl.ds(r0, CHUNK)
            a_next = _shift_rows(apad[pl.ds(r0, CHUNK + SUBLANES), :], -1)[:CHUNK]
            ga_s[rows, :], g_s[rows, :] = _tile_scans(a_next, g_s[rows, :], -1)
            return carry

        lax.fori_loop(0, nchunk, chunk2, 0)
        _carry_tiles(ga_s, g_s, g_s, ntile, -1)

        def chunk3(i, carry):
            r0 = pl.multiple_of(i * CHUNK, CHUNK)
            rows = pl.ds(r0, CHUNK)
            g = g_s[rows, :]
            h_prev = _shift_rows(hpad[pl.ds(r0, CHUNK + SUBLANES), :], 1)[SUBLANES:]
            v, ra, ix, sq, a = v_s[rows, :], ra_s[rows, :], ix_s[rows, :], sq_s[rows, :], apad[rows, :]
            d_sq = g * ix * v
            d_ix = g * sq * v
            d_la = a * g * h_prev - d_sq * a * a / sq
            dlam_ref[...] += jnp.sum(d_la * ra, axis=0, keepdims=True)
            d_pa = d_la * (-LRU_C) * sp * ra * (1.0 - ra)
            d_px = d_ix * ix * (1.0 - ix)
            vb, d_pab, d_pxb = v.astype(BF16), d_pa.astype(BF16), d_px.astype(BF16)
            dwa_ref[...] += _dot_tn(vb, d_pab)
            dwx_ref[...] += _dot_tn(vb, d_pxb)
            dba_ref[...] += jnp.sum(d_pa, axis=0, keepdims=True)
            dbx_ref[...] += jnp.sum(d_px, axis=0, keepdims=True)
            dv = g * sq * ix + _dot_nt(d_pab, wa) + _dot_nt(d_pxb, wx)
            dvpad[rows, :] = dv
            dcb_ref[...] += jnp.sum(dv, axis=0, keepdims=True)
            xs = upad[pl.ds(r0, CHUNK + SUBLANES), :]
            for k in range(CONV_WIDTH):
                u_k = _shift_rows(xs, CONV_WIDTH - 1 - k)[SUBLANES:] if k < CONV_WIDTH - 1 else xs[SUBLANES:]
                dcw_ref[k:k + 1, :] += jnp.sum(dv * u_k, axis=0, keepdims=True)
            return carry

        lax.fori_loop(0, nchunk, chunk3, 0)
        dlam_ref[...] = dlam_ref[...] * (LRU_C * _sigmoid(-lam_row))

        def chunk4(i, carry):
            r0 = pl.multiple_of(i * CHUNK, CHUNK)
            dvs = dvpad[pl.ds(r0, CHUNK + SUBLANES), :]
            du = cw[CONV_WIDTH - 1] * dvs[:CHUNK]
            for k in range(CONV_WIDTH - 1):
                du += cw[k] * _shift_rows(dvs, -(CONV_WIDTH - 1 - k))[:CHUNK]
            du_ref[pl.ds(r0, CHUNK), :] = du.astype(BF16)
            return carry

        lax.fori_loop(0, nchunk, chunk4, 0)

    col = pl.BlockSpec((T, HEAD), lambda h: (0, h))
    vec = pl.BlockSpec((1, HEAD), lambda h: (0, h))
    mat = pl.BlockSpec((None, HEAD, HEAD), lambda h: (h, 0, 0))
    taps = pl.BlockSpec((CONV_WIDTH, HEAD), lambda h: (0, h))
    vec_out = jax.ShapeDtypeStruct((1, D_RNN), F32)
    mat_out = jax.ShapeDtypeStruct((N_RNN_HEADS, HEAD, HEAD), F32)
    seq = pltpu.VMEM((T, HEAD), F32)
    seq_pad = pltpu.VMEM((T + SUBLANES, HEAD), F32)
    return _pallas(
        body, (proj, proj, hr, dz, conv_w, conv_b, w_a, b_a, w_x, b_x, lam), name="rnn_bwd", grid=(N_RNN_HEADS,),
        in_specs=[pl.BlockSpec((T, HEAD), lambda h: (0, COL_RNN + h)), pl.BlockSpec((T, HEAD), lambda h: (0, COL_GATE + h)),
                  col, col, taps, vec, mat, vec, mat, vec, vec],
        out_specs=[col, col, mat, mat, vec, vec, vec, vec, taps],
        out_shape=[jax.ShapeDtypeStruct((T, D_RNN), BF16), jax.ShapeDtypeStruct((T, D_RNN), BF16), mat_out, mat_out,
                   vec_out, vec_out, vec_out, vec_out, jax.ShapeDtypeStruct((CONV_WIDTH, D_RNN), F32)],
        scratch_shapes=[seq_pad, seq_pad, seq_pad, seq, seq, seq, seq, seq, seq_pad, seq],
        semantics=("parallel",), jobs=jobs)


GROUP_FFN = ["w_ffn_out", "w_ffn_in"]
GROUP_MIX = ["w_o", "w_pool_out", "w_rnn_out"]
GROUP_IN = ["w_in"]


def _step(x, target, s, full, conv_w_mine, place):
    T = x.shape[0]
    tall, mid, low = min(T, 2048), min(T, 1024), min(T, 512)
    c1 = place[1:]
    full = dict(full)

    def gathered(names, results):
        full.update(zip(names, results))

    (full["w_in"], conv_w), = _run_jobs([_gather_job(full, ["w_in"], conv_w_mine)], "gather_w_in")
    early = ["w_pool_out", "w_rnn_out", "w_o", "w_ffn_out"]
    (proj, h1), (res,) = _norm_matmul(x, s["norm_mix"], full["w_in"], tm=tall, tn=512, name="in_proj", jobs=[_gather_job(full, early)])
    gathered(early, res)
    pm = _pool_fwd(proj, s["w_pool_grp"], s["pool_scale"])
    (hr, z), (res,) = _rnn_fwd(proj, conv_w, s["conv_b"], s["w_rg_a"], s["b_rg_a"], s["w_rg_x"], s["b_rg_x"], s["lru_lambda"],
                               jobs=[_gather_job(full, ["w_ffn_in"])])
    gathered(["w_ffn_in"], res)
    y_pool, y_rnn, mix = _branch_mix(pm, z, full["w_pool_out"], full["w_rnn_out"], proj, tm=tall, tn=256)
    x2 = _out_proj_residual(mix, full["w_o"], x, tm=mid)
    gate, up, act, h2 = _ffn_in(x2, s["norm_ffn"], full["w_ffn_in"], tm=tall, tn=256)
    dx3, dx3b, sq_cols, g_norm_final = _ffn_out_loss(act, full["w_ffn_out"], x2, s["norm_final"], target, tm=low)

    g = {"norm_final": g_norm_final}

    def chip_sums(names, from_sibling):
        sums = {name: _chip_sum(name, g[name], got, c1) for name, got in zip(names, from_sibling)}
        return {name: v[0] for name, v in sums.items()}, {name: v[1] for name, v in sums.items()}

    def final_sums(names, sums, from_chips):
        return {name: _final_sum(name, sums[name], got, place) for name, got in zip(names, from_chips)}

    dgate, dup = _ffn_out_bwd(dx3b, full["w_ffn_out"], gate, up, tm=tall, tn=256)
    g["w_ffn_out"] = _weight_grad(act, [dx3b], tm=256, tn=D_MODEL, name="w_ffn_out_grad")
    dx2, dx2b, g["norm_ffn"] = _ffn_in_bwd(dgate, dup, full["w_ffn_in"], dx3, x2, s["norm_ffn"], tm=low)
    g["w_ffn_in"] = _weight_grad(h2, [dgate, dup], tm=D_MODEL, tn=256, name="w_ffn_in_grad")
    (dgp, dgr, dyp, dyr), (res,) = _out_proj_bwd(dx2b, full["w_o"], proj, y_pool, y_rnn, tm=tall, tn=256,
                                                 jobs=[_sibling_job(g, GROUP_FFN)])
    sums_ffn, sums_ffn_bf16 = chip_sums(GROUP_FFN, res)
    g["w_o"] = _weight_grad(mix, [dx2b], tm=D_MODEL, tn=256, name="w_o_grad")
    dpm, dz = _branch_bwd(dyp, dyr, full["w_pool_out"], full["w_rnn_out"], tm=mid)
    g["w_pool_out"] = _weight_grad(pm, [dyp], tm=D_POOL, tn=256, name="w_pool_out_grad")
    g["w_rnn_out"] = _weight_grad(z, [dyr], tm=D_RNN, tn=256, name="w_rnn_out_grad")
    (dupool, g["w_pool_grp"], g["pool_scale"]), (res,) = _pool_bwd(proj, dpm, s["w_pool_grp"], s["pool_scale"],
                                                                   jobs=[_sibling_job(g, GROUP_MIX)])
    sums_mix, sums_mix_bf16 = chip_sums(GROUP_MIX, res)
    ((durnn, dugate, g["w_rg_a"], g["w_rg_x"], g["b_rg_a"], g["b_rg_x"], g["lru_lambda"], g["conv_b"], g["conv_w"]),
     (res,)) = _rnn_bwd(proj, hr, dz, conv_w, s["conv_b"], s["w_rg_a"], s["b_rg_a"], s["w_rg_x"], s["b_rg_x"], s["lru_lambda"],
                        jobs=[_chips_job(sums_ffn_bf16, GROUP_FFN)])
    shards = final_sums(GROUP_FFN, sums_ffn, res)
    segs = [dupool, durnn, dugate, dgp, dgr]
    grad_x, g["norm_mix"] = _in_proj_bwd(segs, full["w_in"], dx2, x, s["norm_mix"], tm=low)
    g["w_in"], (res, joined) = _weight_grad(h1, segs, tm=D_MODEL, tn=256, name="w_in_grad",
                                           jobs=[_chips_job(sums_mix_bf16, GROUP_MIX), _join_job(shards, GROUP_FFN)])
    grads = dict(zip(GROUP_FFN, joined))
    shards = final_sums(GROUP_MIX, sums_mix, res)
    (res,) = _run_jobs([_sibling_job(g, GROUP_IN)], "w_in_exchange_sibling")
    sums_in, sums_in_bf16 = chip_sums(GROUP_IN, res)

    vec_rows = [g[name] if name != "pool_scale" else jnp.pad(g[name], ((0, 0), (0, D_MODEL - D_POOL))) for name in VEC_ITEMS]
    vec_rows += [g["conv_w"], sq_cols, jnp.zeros((VEC_ROWS - len(VEC_ITEMS) - CONV_WIDTH - 1, D_MODEL), F32)]
    vec = jnp.concatenate(vec_rows, axis=0).reshape(VEC_ROWS, N_DEV, HEAD).transpose(1, 0, 2)
    mat = jnp.concatenate([g[name].reshape(-1, HEAD) for name in MAT_ITEMS], axis=0).reshape(N_DEV, -1, HEAD)
    (vec, mat), (res, joined) = _all_reduce_small([vec, mat], jobs=[_chips_job(sums_in_bf16, GROUP_IN), _join_job(shards, GROUP_MIX)])
    grads.update(zip(GROUP_MIX, joined))
    (joined,) = _run_jobs([_join_job(final_sums(GROUP_IN, sums_in, res), GROUP_IN)], "w_in_join_halves")
    grads.update(zip(GROUP_IN, joined))

    vec = vec.transpose(1, 0, 2).reshape(VEC_ROWS, D_MODEL)
    mat = mat.reshape(-1, HEAD)
    for k, name in enumerate(VEC_ITEMS):
        grads[name] = vec[k:k + 1, :s[name].shape[1]]
    grads["conv_w"] = vec[len(VEC_ITEMS):len(VEC_ITEMS) + CONV_WIDTH]
    row = 0
    for name in MAT_ITEMS:
        rows = s[name].shape[0] * HEAD
        grads[name] = mat[row:row + rows]
        row += rows
    return vec[len(VEC_ITEMS) + CONV_WIDTH], grad_x, grads


LARGE = {"w_in": "col", "w_pool_out": "col", "w_rnn_out": "row", "w_o": "row", "w_ffn_in": "col", "w_ffn_out": "row"}
LARGE_SHAPE = {"w_in": (D_MODEL, D_IN), "w_pool_out": (D_POOL, D_MODEL), "w_rnn_out": (D_RNN, D_MODEL),
               "w_o": (D_MODEL, D_MODEL), "w_ffn_in": (D_MODEL, 2 * D_FF), "w_ffn_out": (D_FF, D_MODEL)}


def _place():
    x, y, c = lax.axis_index("x"), lax.axis_index("y"), lax.axis_index("c")
    return 2 * x + y, c


def _chip_device(chip, c):
    return (chip // 2, chip % 2, c)


def _chip_window(ref, kind, shape, chip, half=None):
    K, N = shape
    if kind == "col":
        rows = slice(None) if half is None else pl.ds(half * (K // 2), K // 2)
        return ref.at[rows, pl.ds(chip * (N // N_CHIPS), N // N_CHIPS)]
    ks = K // N_CHIPS
    if half is None:
        return ref.at[pl.ds(chip * ks, ks), :]
    return ref.at[pl.ds(chip * ks + half * (ks // 2), ks // 2), :]


def _row_half(ref, half):
    rows = ref.shape[0] // 2
    return ref.at[pl.ds(half * rows, rows), :]


def _remote(win_src, win_dst, send_sems, recv_sems, idx, to):
    return pltpu.make_async_remote_copy(src_ref=win_src, dst_ref=win_dst, send_sem=send_sems.at[idx], recv_sem=recv_sems.at[idx],
                                        device_id=to, device_id_type=MESH)


def _gather_job(full, names, conv_w_full=None):
    n = len(names)
    cw_cols = D_RNN // N_CHIPS

    def windows(refs, chip, half):
        return [_chip_window(refs[k], LARGE[name], LARGE_SHAPE[name], chip, half) for k, name in enumerate(names)]

    def ici_copies(refs, send_sems, recv_sems, src_chip, dst_chip, c, r):
        wins = windows(refs, src_chip, c)
        if conv_w_full is not None:
            wins.append(refs[n].at[:, pl.ds(src_chip * cw_cols, cw_cols)])
        return [_remote(win, win, send_sems, recv_sems, (k, r), _chip_device(dst_chip, c)) for k, win in enumerate(wins)]

    def forwards(refs, send_sems, recv_sems, src_chip, half, to_core, chip, r):
        return [_remote(win, win, send_sems, recv_sems, (k, 3 + r), _chip_device(chip, to_core))
                for k, win in enumerate(windows(refs, src_chip, half))]

    def start(ins, outs, send_sems, recv_sems):
        chip, c = _place()
        for r in range(3):
            for cp in ici_copies(outs, send_sems, recv_sems, chip, chip ^ (r + 1), c, r):
                cp.start()

    def finish(ins, outs, send_sems, recv_sems):
        chip, c = _place()
        for r in range(3):
            for cp in ici_copies(outs, send_sems, recv_sems, chip ^ (r + 1), chip, c, r):
                cp.wait_recv()
            for cp in forwards(outs, send_sems, recv_sems, chip ^ (r + 1), c, 1 - c, chip, r):
                cp.start()
        for r in range(3):
            for cp in forwards(outs, send_sems, recv_sems, chip ^ (r + 1), 1 - c, c, chip, r):
                cp.wait_recv()
            for cp in ici_copies(outs, send_sems, recv_sems, chip, chip ^ (r + 1), c, r):
                cp.wait_send()
            for cp in forwards(outs, send_sems, recv_sems, chip ^ (r + 1), c, 1 - c, chip, r):
                cp.wait_send()

    arrays = [full[name] for name in names] + ([conv_w_full] if conv_w_full is not None else [])
    return _Job(arrays, [jax.ShapeDtypeStruct(a.shape, a.dtype) for a in arrays], {k: k for k in range(len(arrays))},
                (len(arrays), 6), start, finish)


def _core_halves(ref, kind, shape, c):
    return [_chip_window(ref, kind, shape, chip, c) for chip in range(N_CHIPS)]


def _sibling_job(grads, names):
    def start(ins, outs, send_sems, recv_sems):
        chip, c = _place()
        for k, name in enumerate(names):
            kind, shape = LARGE[name], LARGE_SHAPE[name]
            if kind == "col":
                pairs = [(_row_half(ins[k], 1 - c), outs[k])]
            else:
                rows = shape[0] // N_DEV
                pairs = [(win, outs[k].at[pl.ds(j * rows, rows), :]) for j, win in enumerate(_core_halves(ins[k], kind, shape, 1 - c))]
            for src, dst in pairs:
                _remote(src, dst, send_sems, recv_sems, k, _chip_device(chip, 1 - c)).start()

    def finish(ins, outs, send_sems, recv_sems):
        chip, c = _place()
        for k in range(len(names)):
            _remote(outs[k], outs[k], send_sems, recv_sems, k, _chip_device(chip, 1 - c)).wait()

    return _Job([grads[name] for name in names],
                [jax.ShapeDtypeStruct((LARGE_SHAPE[name][0] // 2, LARGE_SHAPE[name][1]), F32) for name in names], {},
                (len(names),), start, finish)


def _chip_sum(name, g, got, c):
    kind, (K, N) = LARGE[name], LARGE_SHAPE[name]
    rows = K // N_DEV

    def body(c_ref, g_ref, got_ref, o_ref, ob_ref):
        total = g_ref[...] + got_ref[...]
        o_ref[...] = total
        ob_ref[...] = total.astype(BF16)

    if kind == "col":
        mine = pl.BlockSpec((rows, N), lambda j, c_ref: (j + N_CHIPS * c_ref[0], 0))
    else:
        mine = pl.BlockSpec((rows, N), lambda j, c_ref: (2 * j + c_ref[0], 0))
    blk = pl.BlockSpec((rows, N), lambda j, c_ref: (j, 0))
    return pl.pallas_call(
        body, name=name + "_chip_sum",
        grid_spec=pltpu.PrefetchScalarGridSpec(num_scalar_prefetch=1, grid=(N_CHIPS,), in_specs=[mine, blk], out_specs=[blk, blk]),
        out_shape=[jax.ShapeDtypeStruct((K // 2, N), F32), jax.ShapeDtypeStruct((K // 2, N), BF16)],
        compiler_params=_params(dimension_semantics=("parallel",)),
    )(c, g, got)


def _piece(ref, kind, shape, chip):
    K, N = shape
    if kind == "col":
        return ref.at[:, pl.ds(chip * (N // N_CHIPS), N // N_CHIPS)]
    return ref.at[pl.ds(chip * (K // N_DEV), K // N_DEV), :]


def _piece_shape(name):
    kind, (K, N) = LARGE[name], LARGE_SHAPE[name]
    return (K // 2, N // N_CHIPS) if kind == "col" else (K // N_DEV, N)


def _chips_job(sums, names):
    def copies(ins, outs, send_sems, recv_sems):
        chip, c = _place()
        return [_remote(_piece(ins[k], LARGE[name], LARGE_SHAPE[name], chip ^ (r + 1)), outs[k].at[r], send_sems, recv_sems, (k, r),
                        _chip_device(chip ^ (r + 1), c)) for k, name in enumerate(names) for r in range(3)]

    def start(*refs):
        for cp in copies(*refs):
            cp.start()

    def finish(*refs):
        for cp in copies(*refs):
            cp.wait()

    return _Job([sums[name] for name in names], [jax.ShapeDtypeStruct((3,) + _piece_shape(name), BF16) for name in names], {},
                (len(names), 3), start, finish)


def _final_sum(name, chip_sum, got, place):
    kind = LARGE[name]
    rows, cols = _piece_shape(name)

    def body(place_ref, s_ref, got_ref, o_ref):
        o_ref[...] = ((s_ref[...] + got_ref[0].astype(F32)) + got_ref[1].astype(F32)) + got_ref[2].astype(F32)

    if kind == "col":
        mine = pl.BlockSpec((rows, cols), lambda i, place_ref: (0, place_ref[0]))
    else:
        mine = pl.BlockSpec((rows, cols), lambda i, place_ref: (place_ref[0], 0))
    return pl.pallas_call(
        body, name=name + "_final_sum",
        grid_spec=pltpu.PrefetchScalarGridSpec(
            num_scalar_prefetch=1, grid=(1,), in_specs=[mine, pl.BlockSpec((3, rows, cols), lambda i, place_ref: (0, 0, 0))],
            out_specs=pl.BlockSpec((rows, cols), lambda i, place_ref: (place_ref[1], 0))),
        out_shape=jax.ShapeDtypeStruct((2 * rows, cols), F32),
        compiler_params=_params(dimension_semantics=("arbitrary",)),
    )(place, chip_sum, got)


def _join_job(shards, names):
    def half_copy(outs, send_sems, recv_sems, k, mine):
        chip, c = _place()
        win = _row_half(outs[k], c if mine else 1 - c)
        return _remote(win, win, send_sems, recv_sems, k, _chip_device(chip, 1 - c))

    def start(ins, outs, send_sems, recv_sems):
        for k in range(len(names)):
            half_copy(outs, send_sems, recv_sems, k, True).start()

    def finish(ins, outs, send_sems, recv_sems):
        for k in range(len(names)):
            half_copy(outs, send_sems, recv_sems, k, True).wait_send()
            half_copy(outs, send_sems, recv_sems, k, False).wait_recv()

    arrays = [shards[name] for name in names]
    return _Job(arrays, [jax.ShapeDtypeStruct(a.shape, F32) for a in arrays], {k: k for k in range(len(arrays))},
                (len(arrays),), start, finish)


VEC_ROWS = 16


def _all_reduce_small(slabs, jobs=()):
    n = len(slabs)

    def body(*refs):
        in_refs, out_refs, got_refs = refs[:n], refs[n:2 * n], refs[2 * n:3 * n]
        send_sems, recv_sems = refs[3 * n:]
        x, y, c = lax.axis_index("x"), lax.axis_index("y"), lax.axis_index("c")
        me = 4 * x + 2 * y + c

        def remote(src, dst, k, phase, r):
            other = me ^ r
            return pltpu.make_async_remote_copy(src_ref=src, dst_ref=dst, send_sem=send_sems.at[k, phase, r],
                                                recv_sem=recv_sems.at[k, phase, r],
                                                device_id=(other // 4, (other // 2) % 2, other % 2), device_id_type=MESH)

        scatter = [remote(in_refs[k].at[me ^ r], got_refs[k].at[r], k, 0, r) for r in range(1, N_DEV) for k in range(n)]
        for cp in scatter:
            cp.start()
        for cp in scatter:
            cp.wait()
        for k in range(n):
            total = in_refs[k][me]
            for r in range(1, N_DEV):
                total = total + got_refs[k][r]
            out_refs[k][me] = total
        gather = [remote(out_refs[k].at[me], out_refs[k].at[me], k, 1, r) for r in range(1, N_DEV) for k in range(n)]
        for cp in gather:
            cp.start()
        for r in range(1, N_DEV):
            for k in range(n):
                remote(out_refs[k].at[me ^ r], out_refs[k].at[me ^ r], k, 1, r).wait_recv()
        for cp in gather:
            cp.wait_send()

    return _pallas(
        body, slabs, name="all_reduce_small", grid=(), in_specs=[VMEM] * n, out_specs=[VMEM] * n,
        out_shape=[jax.ShapeDtypeStruct(s.shape, F32) for s in slabs],
        scratch_shapes=[pltpu.VMEM(s.shape, F32) for s in slabs]
        + [pltpu.SemaphoreType.DMA((n, 2, N_DEV)), pltpu.SemaphoreType.DMA((n, 2, N_DEV))], jobs=jobs)


def _cast_into_whole(w, name, place):
    rows, cols = w.shape
    tr = rows // 2

    def body(place_ref, w_ref, o_ref):
        o_ref[...] = w_ref[...].astype(BF16)

    if LARGE[name] == "col":
        window = pl.BlockSpec((tr, cols), lambda i, place_ref: (i, place_ref[0]))
    else:
        window = pl.BlockSpec((tr, cols), lambda i, place_ref: (2 * place_ref[0] + i, 0))
    return pl.pallas_call(
        body, name=name + "_cast",
        grid_spec=pltpu.PrefetchScalarGridSpec(num_scalar_prefetch=1, grid=(2,),
                                               in_specs=[pl.BlockSpec((tr, cols), lambda i, place_ref: (i, 0))], out_specs=window),
        out_shape=jax.ShapeDtypeStruct(LARGE_SHAPE[name], BF16),
        compiler_params=_params(dimension_semantics=("parallel",)))(place, w)


def _adamw_math(w, g, m, v):
    m = ADAM_B1 * m + (1.0 - ADAM_B1) * g
    v = ADAM_B2 * v + (1.0 - ADAM_B2) * (g * g)
    m_hat = m / (1.0 - ADAM_B1 ** ADAM_STEP)
    v_hat = v / (1.0 - ADAM_B2 ** ADAM_STEP)
    delta = -ADAM_LR * (m_hat / (jnp.sqrt(v_hat) + ADAM_EPS) + ADAM_WD * w)
    return delta, m, v


def _adamw_large(w, g, m, v, name):
    rows, cols = w.shape
    tr = rows // 4

    def body(w_ref, g_ref, m_ref, v_ref, d_ref, mo_ref, vo_ref):
        d_ref[...], mo_ref[...], vo_ref[...] = _adamw_math(w_ref[...], g_ref[...], m_ref[...], v_ref[...])

    blk = pl.BlockSpec((tr, cols), lambda i: (i, 0))
    out = jax.ShapeDtypeStruct(w.shape, F32)
    return pl.pallas_call(body, name=name + "_adamw", grid=(4,), in_specs=[blk] * 4, out_specs=[blk] * 3, out_shape=[out] * 3,
                          compiler_params=_params(dimension_semantics=("parallel",)))(w, g, m, v)


def _adamw_small(ws, gs, ms, vs):
    n = len(ws)

    def body(*refs):
        for k in range(n):
            w_ref, g_ref, m_ref, v_ref = (refs[q * n + k] for q in range(4))
            d_ref, mo_ref, vo_ref = (refs[(4 + q) * n + k] for q in range(3))
            d_ref[...], mo_ref[...], vo_ref[...] = _adamw_math(w_ref[...], g_ref[...], m_ref[...], v_ref[...])

    out = [jax.ShapeDtypeStruct(w.shape, F32) for w in ws]
    res = pl.pallas_call(body, name="small_adamw", in_specs=[VMEM] * (4 * n), out_specs=[VMEM] * (3 * n), out_shape=out * 3,
                         compiler_params=_params())(*ws, *gs, *ms, *vs)
    return res[:n], res[n:2 * n], res[2 * n:]


WEIGHTS = ["norm_mix", "w_in", "w_pool_grp", "pool_scale", "w_pool_out", "conv_w", "conv_b", "w_rg_a", "b_rg_a", "w_rg_x",
           "b_rg_x", "lru_lambda", "w_rnn_out", "w_o", "norm_ffn", "w_ffn_in", "w_ffn_out", "norm_final"]
VEC_ITEMS = ["norm_mix", "norm_ffn", "norm_final", "pool_scale", "conv_b", "lru_lambda", "b_rg_a", "b_rg_x"]
MAT_ITEMS = ["w_pool_grp", "w_rg_a", "w_rg_x"]


def _as2d(name, a):
    if name in MAT_ITEMS:
        return a.reshape(-1, HEAD, HEAD)
    if name == "conv_w":
        return a.reshape(CONV_WIDTH, -1)
    return a.reshape(1, -1)


def kernel(x, norm_mix, w_in, w_pool_grp, pool_scale, w_pool_out, conv_w, conv_b, w_rg_a, b_rg_a, w_rg_x, b_rg_x, lru_lambda, w_rnn_out, w_o, norm_ffn, w_ffn_in, w_ffn_out, norm_final, loss_target, m_norm_mix, m_w_in, m_w_pool_grp, m_pool_scale, m_w_pool_out, m_conv_w, m_conv_b, m_w_rg_a, m_b_rg_a, m_w_rg_x, m_b_rg_x, m_lru_lambda, m_w_rnn_out, m_w_o, m_norm_ffn, m_w_ffn_in, m_w_ffn_out, m_norm_final, v_norm_mix, v_w_in, v_w_pool_grp, v_pool_scale, v_w_pool_out, v_conv_w, v_conv_b, v_w_rg_a, v_b_rg_a, v_w_rg_x, v_b_rg_x, v_lru_lambda, v_w_rnn_out, v_w_o, v_norm_ffn, v_w_ffn_in, v_w_ffn_out, v_norm_final):
    given = dict(locals())
    w = {name: given[name] for name in WEIGHTS}
    m = {name: given["m_" + name] for name in WEIGHTS}
    v = {name: given["v_" + name] for name in WEIGHTS}
    chip, c = _place()

    place = jnp.stack([chip, c]).astype(jnp.int32)
    conv_cols = w["conv_w"].shape[-1]
    conv_w_mine = lax.dynamic_update_slice_in_dim(jnp.zeros((CONV_WIDTH, D_RNN), F32), w["conv_w"][0], chip * conv_cols, axis=1)
    full = {name: _cast_into_whole(w[name][0], name, place) for name in LARGE}
    small = {name: _as2d(name, w[name]) for name in WEIGHTS if name not in LARGE and name != "conv_w"}
    sq_cols, grad_x, grads = _step(x[0], loss_target[0], small, full, conv_w_mine, place)
    loss = 0.5 / D_MODEL * jnp.sum(sq_cols)
    grads["conv_w"] = lax.dynamic_slice_in_dim(grads["conv_w"], chip * conv_cols, conv_cols, axis=1)

    delta, new_m, new_v = {}, {}, {}
    for name in LARGE:
        delta[name], new_m[name], new_v[name] = _adamw_large(w[name][0], grads[name], m[name][0], v[name][0], name)
    small_names = [name for name in WEIGHTS if name not in LARGE]
    flat = lambda d: [d[name].reshape(grads[name].shape) for name in small_names]
    ds, mo, vo = _adamw_small(flat(w), [grads[name] for name in small_names], flat(m), flat(v))
    for k, name in enumerate(small_names):
        delta[name], new_m[name], new_v[name] = ds[k], mo[k], vo[k]

    shaped = lambda d: [d[name].reshape(w[name].shape) for name in WEIGHTS]
    return (loss, grad_x[None], *shaped(grads), *shaped(delta), *shaped(new_m), *shaped(new_v))
```

```python
import functools
import math

import jax
import jax.numpy as jnp
from jax import lax
from jax.experimental import pallas as pl
from jax.experimental.pallas import tpu as pltpu

F32 = jnp.float32
BF16 = jnp.bfloat16

D_MODEL = 1024
D_POOL = 512
N_POOL_GROUPS = 4
D_RNN = 1024
N_RNN_HEADS = 8
HEAD = 128
CONV_WIDTH = 4
LRU_C = 8.0
D_FF = 2816
D_IN = D_POOL + 2 * D_RNN + 2 * D_MODEL
NORM_EPS = 1e-6
COL_RNN = D_POOL // HEAD
COL_GATE = (D_POOL + D_RNN) // HEAD

ADAM_LR = 0.001
ADAM_B1 = 0.9
ADAM_B2 = 0.999
ADAM_EPS = 1e-08
ADAM_WD = 0.01
ADAM_STEP = 10

N_CHIPS = 4
N_DEV = 8
MESH = pl.DeviceIdType.MESH
ANY = pl.BlockSpec(memory_space=pl.ANY)
VMEM = pl.BlockSpec(memory_space=pltpu.VMEM)
VMEM_LIMIT_BYTES = 60 * 1024 * 1024
SUBLANES = 8
POOL_HALO = 16
CHUNK = 1024

GELU_C = math.sqrt(2.0 / math.pi)
GELU_A = 0.044715


def _params(**kw):
    return pltpu.CompilerParams(vmem_limit_bytes=VMEM_LIMIT_BYTES, **kw)


def _sigmoid(x):
    return 0.5 * jnp.tanh(0.5 * x) + 0.5


def _log1p(y):
    u = 1.0 + y
    d = u - 1.0
    return jnp.where(d == 0.0, y, jnp.log(u) * (y / jnp.where(d == 0.0, 1.0, d)))


def _gelu_parts(x):
    x2 = x * x
    th = jnp.tanh(GELU_C * (x + GELU_A * x * x2))
    g = 0.5 * x * (1.0 + th)
    dg = 0.5 * (1.0 + th) + 0.5 * x * (1.0 - th * th) * GELU_C * (1.0 + 3.0 * GELU_A * x2)
    return g, dg


def _dot(a, b):
    return jnp.dot(a, b, preferred_element_type=F32)


def _dot_nt(a, b):
    return lax.dot_general(a, b, (((1,), (1,)), ((), ())), preferred_element_type=F32)


def _dot_tn(a, b):
    return lax.dot_general(a, b, (((0,), (0,)), ((), ())), preferred_element_type=F32)


def _rms_scale(xv):
    return lax.rsqrt(jnp.mean(xv * xv, axis=-1, keepdims=True) + NORM_EPS)


def _rms_bwd(dy, xv, g):
    r = _rms_scale(xv)
    xh = xv * r
    dyg = dy * g
    dx = r * (dyg - xh * jnp.mean(dyg * xh, axis=-1, keepdims=True))
    return dx, dy * xh


class _Job:
    def __init__(self, inputs, out_shapes, aliases, sem_shape, start, finish):
        self.inputs, self.out_shapes, self.aliases, self.sem_shape = list(inputs), list(out_shapes), dict(aliases), sem_shape
        self.start, self.finish = start, finish


def _pallas(body, operands, *, name, grid, in_specs, out_specs, out_shape, scratch_shapes=(), semantics=None, jobs=()):
    n_in, n_out, n_scr = len(in_specs), len(out_specs), len(scratch_shapes)
    job_in = [a for job in jobs for a in job.inputs]
    job_out = [s for job in jobs for s in job.out_shapes]
    aliases, i0, o0 = {}, n_in, n_out
    for job in jobs:
        aliases.update({i0 + i: o0 + o for i, o in job.aliases.items()})
        i0, o0 = i0 + len(job.inputs), o0 + len(job.out_shapes)

    def whole(*refs):
        ins, j_ins = refs[:n_in], refs[n_in:n_in + len(job_in)]
        outs = refs[n_in + len(job_in):][:n_out]
        j_outs = refs[n_in + len(job_in) + n_out:][:len(job_out)]
        rest = refs[n_in + len(job_in) + n_out + len(job_out):]
        scr, sems = rest[:n_scr], rest[n_scr:]

        def run(phase):
            i, o = 0, 0
            for k, job in enumerate(jobs):
                getattr(job, phase)(j_ins[i:i + len(job.inputs)], j_outs[o:o + len(job.out_shapes)], sems[2 * k], sems[2 * k + 1])
                i, o = i + len(job.inputs), o + len(job.out_shapes)

        def at(step_of, phase):
            if not jobs:
                return
            if not grid:
                run(phase)
                return
            cond = functools.reduce(jnp.logical_and, [pl.program_id(d) == step_of(d) for d in range(len(grid))])
            pl.when(cond)(functools.partial(run, phase))

        at(lambda d: 0, "start")
        body(*ins, *outs, *scr)
        at(lambda d: grid[d] - 1, "finish")

    res = pl.pallas_call(
        whole, name=name, grid=grid, in_specs=list(in_specs) + [ANY] * len(job_in), out_specs=list(out_specs) + [ANY] * len(job_out),
        out_shape=list(out_shape) + job_out, input_output_aliases=aliases,
        scratch_shapes=list(scratch_shapes) + [pltpu.SemaphoreType.DMA(job.sem_shape) for job in jobs for _ in range(2)],
        compiler_params=_params(dimension_semantics=semantics, has_side_effects=bool(jobs)),
    )(*operands, *job_in)
    per_job, o = [], n_out
    for job in jobs:
        per_job.append(res[o:o + len(job.out_shapes)])
        o += len(job.out_shapes)
    return res[:n_out], per_job


def _run_jobs(jobs, name):
    return _pallas(lambda: None, [], name=name, grid=(), in_specs=[], out_specs=[], out_shape=[], jobs=jobs)[1]


NORM_ROWS = 256


def _norm_rows(x_ref, g_ref, h_ref):
    g = g_ref[...]

    def rows(i, carry):
        r = pl.ds(pl.multiple_of(i * NORM_ROWS, NORM_ROWS), NORM_ROWS)
        xv = x_ref[r, :]
        h_ref[r, :] = (xv * _rms_scale(xv) * g).astype(BF16)
        return carry

    lax.fori_loop(0, x_ref.shape[0] // NORM_ROWS, rows, 0)


def _norm_matmul(x, g, w, *, tm, tn, name, jobs=()):
    T, K = x.shape
    N = w.shape[1]

    def body(x_ref, g_ref, w_ref, o_ref, h_ref):
        @pl.when(pl.program_id(1) == 0)
        def _():
            _norm_rows(x_ref, g_ref, h_ref)

        o_ref[...] = _dot(h_ref[...], w_ref[...])

    return _pallas(
        body, (x, g, w), name=name, grid=(T // tm, N // tn),
        in_specs=[pl.BlockSpec((tm, K), lambda i, j: (i, 0)), pl.BlockSpec((1, K), lambda i, j: (0, 0)),
                  pl.BlockSpec((K, tn), lambda i, j: (0, j))],
        out_specs=[pl.BlockSpec((tm, tn), lambda i, j: (i, j)), pl.BlockSpec((tm, K), lambda i, j: (i, 0))],
        out_shape=[jax.ShapeDtypeStruct((T, N), F32), jax.ShapeDtypeStruct((T, K), BF16)],
        semantics=("parallel", "arbitrary"), jobs=jobs)


def _ffn_in(x2, g, w, *, tm, tn):
    T, K = x2.shape
    nb = D_FF // tn

    def body(x_ref, g_ref, wg_ref, wu_ref, gate_ref, up_ref, act_ref, h_ref):
        @pl.when(pl.program_id(1) == 0)
        def _():
            _norm_rows(x_ref, g_ref, h_ref)

        h = h_ref[...]
        gate = _dot(h, wg_ref[...])
        up = _dot(h, wu_ref[...])
        gate_ref[...] = gate
        up_ref[...] = up
        act_ref[...] = (gate * _sigmoid(gate) * up).astype(BF16)

    blk = pl.BlockSpec((tm, tn), lambda i, j: (i, j))
    return pl.pallas_call(
        body, name="ffn_in", grid=(T // tm, nb),
        in_specs=[pl.BlockSpec((tm, K), lambda i, j: (i, 0)), pl.BlockSpec((1, K), lambda i, j: (0, 0)),
                  pl.BlockSpec((K, tn), lambda i, j: (0, j)), pl.BlockSpec((K, tn), lambda i, j: (0, j + nb))],
        out_specs=[blk, blk, blk, pl.BlockSpec((tm, K), lambda i, j: (i, 0))],
        out_shape=[jax.ShapeDtypeStruct((T, D_FF), F32), jax.ShapeDtypeStruct((T, D_FF), F32),
                   jax.ShapeDtypeStruct((T, D_FF), BF16), jax.ShapeDtypeStruct((T, K), BF16)],
        compiler_params=_params(dimension_semantics=("parallel", "arbitrary")),
    )(x2, g, w, w)


def _branch_mix(pm, z, w_pool_out, w_rnn_out, proj, *, tm, tn):
    T = pm.shape[0]
    col_gp = (D_POOL + 2 * D_RNN) // tn
    col_gr = col_gp + D_MODEL // tn

    def body(pm_ref, z_ref, wp_ref, wr_ref, gp_ref, gr_ref, yp_ref, yr_ref, mix_ref):
        yp = _dot(pm_ref[...], wp_ref[...])
        yr = _dot(z_ref[...], wr_ref[...])
        yp_ref[...] = yp
        yr_ref[...] = yr
        mix_ref[...] = (_sigmoid(gp_ref[...]) * yp + _sigmoid(gr_ref[...]) * yr).astype(BF16)

    blk = pl.BlockSpec((tm, tn), lambda i, j: (i, j))
    return pl.pallas_call(
        body, name="branch_mix", grid=(T // tm, D_MODEL // tn),
        in_specs=[pl.BlockSpec((tm, D_POOL), lambda i, j: (i, 0)), pl.BlockSpec((tm, D_RNN), lambda i, j: (i, 0)),
                  pl.BlockSpec((D_POOL, tn), lambda i, j: (0, j)), pl.BlockSpec((D_RNN, tn), lambda i, j: (0, j)),
                  pl.BlockSpec((tm, tn), lambda i, j: (i, col_gp + j)), pl.BlockSpec((tm, tn), lambda i, j: (i, col_gr + j))],
        out_specs=[blk, blk, blk],
        out_shape=[jax.ShapeDtypeStruct((T, D_MODEL), F32), jax.ShapeDtypeStruct((T, D_MODEL), F32),
                   jax.ShapeDtypeStruct((T, D_MODEL), BF16)],
        compiler_params=_params(dimension_semantics=("parallel", "parallel")),
    )(pm, z, w_pool_out, w_rnn_out, proj, proj)


def _out_proj_residual(mix, w_o, x, *, tm):
    T = x.shape[0]

    def body(mix_ref, w_ref, x_ref, o_ref):
        o_ref[...] = x_ref[...] + _dot(mix_ref[...], w_ref[...])

    row = pl.BlockSpec((tm, D_MODEL), lambda i: (i, 0))
    return pl.pallas_call(
        body, name="out_proj_residual", grid=(T // tm,),
        in_specs=[row, pl.BlockSpec((D_MODEL, D_MODEL), lambda i: (0, 0)), row],
        out_specs=row, out_shape=jax.ShapeDtypeStruct((T, D_MODEL), F32),
        compiler_params=_params(dimension_semantics=("parallel",)),
    )(mix, w_o, x)


def _ffn_out_loss(act, w, x2, g3, target, *, tm):
    T = x2.shape[0]

    def body(act_ref, w_ref, x2_ref, g_ref, t_ref, dx_ref, dxb_ref, sq_ref, dg_ref):
        @pl.when(pl.program_id(0) == 0)
        def _():
            sq_ref[...] = jnp.zeros_like(sq_ref)
            dg_ref[...] = jnp.zeros_like(dg_ref)

        x3 = x2_ref[...] + _dot(act_ref[...], w_ref[...])
        g = g_ref[...]
        err = x3 * _rms_scale(x3) * g - t_ref[...]
        sq_ref[...] += jnp.sum(err * err, axis=0, keepdims=True)
        dx, dgp = _rms_bwd(err * (1.0 / D_MODEL), x3, g)
        dg_ref[...] += jnp.sum(dgp, axis=0, keepdims=True)
        dx_ref[...] = dx
        dxb_ref[...] = dx.astype(BF16)

    row = pl.BlockSpec((tm, D_MODEL), lambda i: (i, 0))
    vec = pl.BlockSpec((1, D_MODEL), lambda i: (0, 0))
    return pl.pallas_call(
        body, name="ffn_out_loss", grid=(T // tm,),
        in_specs=[pl.BlockSpec((tm, D_FF), lambda i: (i, 0)), pl.BlockSpec((D_FF, D_MODEL), lambda i: (0, 0)), row, vec, row],
        out_specs=[row, row, vec, vec],
        out_shape=[jax.ShapeDtypeStruct((T, D_MODEL), F32), jax.ShapeDtypeStruct((T, D_MODEL), BF16),
                   jax.ShapeDtypeStruct((1, D_MODEL), F32), jax.ShapeDtypeStruct((1, D_MODEL), F32)],
        compiler_params=_params(dimension_semantics=("arbitrary",)),
    )(act, w, x2, g3, target)


def _ffn_out_bwd(dx3b, w, gate, up, *, tm, tn):
    T = dx3b.shape[0]

    def body(dx_ref, w_ref, gate_ref, up_ref, dgate_ref, dup_ref):
        dact = _dot_nt(dx_ref[...], w_ref[...])
        gate = gate_ref[...]
        s = _sigmoid(gate)
        dgate_ref[...] = (dact * up_ref[...] * s * (1.0 + gate * (1.0 - s))).astype(BF16)
        dup_ref[...] = (dact * gate * s).astype(BF16)

    blk = pl.BlockSpec((tm, tn), lambda i, j: (i, j))
    return pl.pallas_call(
        body, name="ffn_out_bwd", grid=(T // tm, D_FF // tn),
        in_specs=[pl.BlockSpec((tm, D_MODEL), lambda i, j: (i, 0)), pl.BlockSpec((tn, D_MODEL), lambda i, j: (j, 0)), blk, blk],
        out_specs=[blk, blk],
        out_shape=[jax.ShapeDtypeStruct((T, D_FF), BF16), jax.ShapeDtypeStruct((T, D_FF), BF16)],
        compiler_params=_params(dimension_semantics=("parallel", "parallel")),
    )(dx3b, w, gate, up)


def _ffn_in_bwd(dgate, dup, w, dx3, x2, g2, *, tm):
    T = x2.shape[0]

    def body(dgate_ref, dup_ref, w_ref, dx3_ref, x2_ref, g_ref, dx_ref, dxb_ref, dg_ref):
        @pl.when(pl.program_id(0) == 0)
        def _():
            dg_ref[...] = jnp.zeros_like(dg_ref)

        dh = _dot_nt(dgate_ref[...], w_ref[:, :D_FF]) + _dot_nt(dup_ref[...], w_ref[:, D_FF:])
        dxn, dgp = _rms_bwd(dh, x2_ref[...], g_ref[...])
        dx = dx3_ref[...] + dxn
        dg_ref[...] += jnp.sum(dgp, axis=0, keepdims=True)
        dx_ref[...] = dx
        dxb_ref[...] = dx.astype(BF16)

    row = pl.BlockSpec((tm, D_MODEL), lambda i: (i, 0))
    wide = pl.BlockSpec((tm, D_FF), lambda i: (i, 0))
    vec = pl.BlockSpec((1, D_MODEL), lambda i: (0, 0))
    return pl.pallas_call(
        body, name="ffn_in_bwd", grid=(T // tm,),
        in_specs=[wide, wide, pl.BlockSpec((D_MODEL, 2 * D_FF), lambda i: (0, 0)), row, row, vec],
        out_specs=[row, row, vec],
        out_shape=[jax.ShapeDtypeStruct((T, D_MODEL), F32), jax.ShapeDtypeStruct((T, D_MODEL), BF16),
                   jax.ShapeDtypeStruct((1, D_MODEL), F32)],
        compiler_params=_params(dimension_semantics=("arbitrary",)),
    )(dgate, dup, w, dx3, x2, g2)


def _out_proj_bwd(dx2b, w_o, proj, y_pool, y_rnn, *, tm, tn, jobs=()):
    T = dx2b.shape[0]
    col_gp = (D_POOL + 2 * D_RNN) // tn
    col_gr = col_gp + D_MODEL // tn

    def body(dx_ref, w_ref, gp_ref, gr_ref, yp_ref, yr_ref, dgp_ref, dgr_ref, dyp_ref, dyr_ref):
        dmix = _dot_nt(dx_ref[...], w_ref[...])
        sp = _sigmoid(gp_ref[...])
        sr = _sigmoid(gr_ref[...])
        dgp_ref[...] = (dmix * yp_ref[...] * sp * (1.0 - sp)).astype(BF16)
        dgr_ref[...] = (dmix * yr_ref[...] * sr * (1.0 - sr)).astype(BF16)
        dyp_ref[...] = (dmix * sp).astype(BF16)
        dyr_ref[...] = (dmix * sr).astype(BF16)

    blk = pl.BlockSpec((tm, tn), lambda i, j: (i, j))
    out = jax.ShapeDtypeStruct((T, D_MODEL), BF16)
    return _pallas(
        body, (dx2b, w_o, proj, proj, y_pool, y_rnn), name="out_proj_bwd", grid=(T // tm, D_MODEL // tn),
        in_specs=[pl.BlockSpec((tm, D_MODEL), lambda i, j: (i, 0)), pl.BlockSpec((tn, D_MODEL), lambda i, j: (j, 0)),
                  pl.BlockSpec((tm, tn), lambda i, j: (i, col_gp + j)), pl.BlockSpec((tm, tn), lambda i, j: (i, col_gr + j)), blk, blk],
        out_specs=[blk, blk, blk, blk], out_shape=[out, out, out, out], semantics=("parallel", "parallel"), jobs=jobs)


def _branch_bwd(dyp, dyr, w_pool_out, w_rnn_out, *, tm):
    T = dyp.shape[0]

    def body(dyp_ref, dyr_ref, wp_ref, wr_ref, dpm_ref, dz_ref):
        dpm_ref[...] = _dot_nt(dyp_ref[...], wp_ref[...])
        dz_ref[...] = _dot_nt(dyr_ref[...], wr_ref[...])

    row = pl.BlockSpec((tm, D_MODEL), lambda i: (i, 0))
    return pl.pallas_call(
        body, name="branch_bwd", grid=(T // tm,),
        in_specs=[row, row, pl.BlockSpec((D_POOL, D_MODEL), lambda i: (0, 0)), pl.BlockSpec((D_RNN, D_MODEL), lambda i: (0, 0))],
        out_specs=[pl.BlockSpec((tm, D_POOL), lambda i: (i, 0)), pl.BlockSpec((tm, D_RNN), lambda i: (i, 0))],
        out_shape=[jax.ShapeDtypeStruct((T, D_POOL), F32), jax.ShapeDtypeStruct((T, D_RNN), F32)],
        compiler_params=_params(dimension_semantics=("parallel",)),
    )(dyp, dyr, w_pool_out, w_rnn_out)


def _in_proj_bwd(segs, w, dx2, x, g1, *, tm):
    T = x.shape[0]
    widths = [s.shape[1] for s in segs]
    offs = [sum(widths[:k]) for k in range(len(widths))]
    n = len(segs)

    def body(*refs):
        seg_refs, (w_ref, dx2_ref, x_ref, g_ref, dx_ref, dg_ref) = refs[:n], refs[n:]

        @pl.when(pl.program_id(0) == 0)
        def _():
            dg_ref[...] = jnp.zeros_like(dg_ref)

        dh = _dot_nt(seg_refs[0][...], w_ref[:, offs[0]:offs[0] + widths[0]])
        for k in range(1, n):
            dh += _dot_nt(seg_refs[k][...], w_ref[:, offs[k]:offs[k] + widths[k]])
        dxn, dgp = _rms_bwd(dh, x_ref[...], g_ref[...])
        dg_ref[...] += jnp.sum(dgp, axis=0, keepdims=True)
        dx_ref[...] = dx2_ref[...] + dxn

    row = pl.BlockSpec((tm, D_MODEL), lambda i: (i, 0))
    vec = pl.BlockSpec((1, D_MODEL), lambda i: (0, 0))
    return pl.pallas_call(
        body, name="in_proj_bwd", grid=(T // tm,),
        in_specs=[pl.BlockSpec((tm, wd), lambda i: (i, 0)) for wd in widths]
        + [pl.BlockSpec((D_MODEL, D_IN), lambda i: (0, 0)), row, row, vec],
        out_specs=[row, vec],
        out_shape=[jax.ShapeDtypeStruct((T, D_MODEL), F32), jax.ShapeDtypeStruct((1, D_MODEL), F32)],
        compiler_params=_params(dimension_semantics=("arbitrary",)),
    )(*segs, w, dx2, x, g1)


def _weight_grad(a, segs, *, tm, tn, name, jobs=None):
    T, M = a.shape
    nblk = [s.shape[1] // tn for s in segs]
    first = [sum(nblk[:k]) for k in range(len(segs))]
    n = len(segs)

    def body(a_ref, *refs):
        seg_refs, o_ref = refs[:n], refs[n]
        j = pl.program_id(1)
        for k in range(n):
            @pl.when((j >= first[k]) & (j < first[k] + nblk[k]))
            def _(k=k):
                o_ref[...] = _dot_tn(a_ref[...], seg_refs[k][...])

    def seg_spec(k):
        return pl.BlockSpec((T, tn), lambda i, j: (0, jnp.clip(j - first[k], 0, nblk[k] - 1)))

    (grad,), results = _pallas(
        body, (a, *segs), name=name, grid=(M // tm, sum(nblk)),
        in_specs=[pl.BlockSpec((T, tm), lambda i, j: (0, i))] + [seg_spec(k) for k in range(n)],
        out_specs=[pl.BlockSpec((tm, tn), lambda i, j: (i, j))],
        out_shape=[jax.ShapeDtypeStruct((M, sum(nblk) * tn), F32)],
        semantics=("parallel", "arbitrary"), jobs=jobs or ())
    return grad if jobs is None else (grad, results)


def _pad_front(dst, src, halo):
    dst[pl.ds(0, halo), :] = jnp.zeros((halo, src.shape[1]), F32)

    def fill(i, carry):
        r0 = pl.multiple_of(i * CHUNK, CHUNK)
        dst[pl.ds(r0 + halo, CHUNK), :] = src[pl.ds(r0, CHUNK), :]
        return carry

    lax.fori_loop(0, src.shape[0] // CHUNK, fill, 0)


def _shift_rows(v, k):
    return pltpu.roll(v, k % v.shape[0], axis=0)


def _window_sums(xs, direction):
    s2 = xs + _shift_rows(xs, direction)
    s4 = s2 + _shift_rows(s2, 2 * direction)
    s8 = s4 + _shift_rows(s4, 4 * direction)
    s16 = s8 + _shift_rows(s8, 8 * direction)
    return s2, s4, s8, s16


def _select_window(g, sums):
    s2, s4, s8, s16 = sums
    return jnp.where(g == 0, s2, jnp.where(g == 1, s4, jnp.where(g == 2, s8, s16)))


def _pool_count(g, start, rows):
    t = start + lax.broadcasted_iota(jnp.int32, (rows, 1), 0)
    return jnp.minimum(t + 1, jnp.left_shift(2, g)).astype(F32)


def _pool_fwd(proj, w_grp, scale):
    T = proj.shape[0]
    nchunk = T // CHUNK

    def body(u_ref, w_ref, s_ref, o_ref, upad):
        g = pl.program_id(0)
        _pad_front(upad, u_ref, POOL_HALO)
        w = w_ref[...].astype(BF16)
        scale_row = s_ref[...]

        def chunk(i, carry):
            r0 = pl.multiple_of(i * CHUNK, CHUNK)
            xs = upad[pl.ds(r0, CHUNK + POOL_HALO), :]
            win = _select_window(g, _window_sums(xs, 1))[POOL_HALO:]
            pooled = win / _pool_count(g, r0, CHUNK) - xs[POOL_HALO:]
            o_ref[pl.ds(r0, CHUNK), :] = (_dot(pooled.astype(BF16), w) * scale_row).astype(BF16)
            return carry

        lax.fori_loop(0, nchunk, chunk, 0)

    return pl.pallas_call(
        body, name="pool_fwd", grid=(N_POOL_GROUPS,),
        in_specs=[pl.BlockSpec((T, HEAD), lambda g: (0, g)), pl.BlockSpec((None, HEAD, HEAD), lambda g: (g, 0, 0)),
                  pl.BlockSpec((1, HEAD), lambda g: (0, g))],
        out_specs=pl.BlockSpec((T, HEAD), lambda g: (0, g)),
        out_shape=jax.ShapeDtypeStruct((T, D_POOL), BF16),
        scratch_shapes=[pltpu.VMEM((T + POOL_HALO, HEAD), F32)],
        compiler_params=_params(dimension_semantics=("parallel",)),
    )(proj, w_grp, scale)


def _pool_bwd(proj, dpm, w_grp, scale, jobs=()):
    T = proj.shape[0]
    nchunk = T // CHUNK

    def body(u_ref, dpm_ref, w_ref, s_ref, du_ref, dw_ref, ds_ref, upad, zpad, dpool):
        g = pl.program_id(0)
        _pad_front(upad, u_ref, POOL_HALO)
        zpad[pl.ds(T, POOL_HALO), :] = jnp.zeros((POOL_HALO, HEAD), F32)
        dw_ref[...] = jnp.zeros_like(dw_ref)
        ds_ref[...] = jnp.zeros_like(ds_ref)
        w = w_ref[...].astype(BF16)
        scale_row = s_ref[...]

        def chunk(i, carry):
            r0 = pl.multiple_of(i * CHUNK, CHUNK)
            xs = upad[pl.ds(r0, CHUNK + POOL_HALO), :]
            cnt = _pool_count(g, r0, CHUNK)
            pooled = (_select_window(g, _window_sums(xs, 1))[POOL_HALO:] / cnt - xs[POOL_HALO:]).astype(BF16)
            mixed = _dot(pooled, w)
            d = dpm_ref[pl.ds(r0, CHUNK), :]
            ds_ref[...] += jnp.sum(d * mixed, axis=0, keepdims=True)
            dmixed = (d * scale_row).astype(BF16)
            dw_ref[...] += _dot_tn(pooled, dmixed)
            dp = _dot_nt(dmixed, w)
            dpool[pl.ds(r0, CHUNK), :] = dp
            zpad[pl.ds(r0, CHUNK), :] = dp / cnt
            return carry

        lax.fori_loop(0, nchunk, chunk, 0)

        def chunk2(i, carry):
            r0 = pl.multiple_of(i * CHUNK, CHUNK)
            zs = zpad[pl.ds(r0, CHUNK + POOL_HALO), :]
            win = _select_window(g, _window_sums(zs, -1))[:CHUNK]
            du_ref[pl.ds(r0, CHUNK), :] = (win - dpool[pl.ds(r0, CHUNK), :]).astype(BF16)
            return carry

        lax.fori_loop(0, nchunk, chunk2, 0)

    col = pl.BlockSpec((T, HEAD), lambda g: (0, g))
    return _pallas(
        body, (proj, dpm, w_grp, scale), name="pool_bwd", grid=(N_POOL_GROUPS,),
        in_specs=[col, col, pl.BlockSpec((None, HEAD, HEAD), lambda g: (g, 0, 0)), pl.BlockSpec((1, HEAD), lambda g: (0, g))],
        out_specs=[col, pl.BlockSpec((None, HEAD, HEAD), lambda g: (g, 0, 0)), pl.BlockSpec((1, HEAD), lambda g: (0, g))],
        out_shape=[jax.ShapeDtypeStruct((T, D_POOL), BF16), jax.ShapeDtypeStruct((N_POOL_GROUPS, HEAD, HEAD), F32),
                   jax.ShapeDtypeStruct((1, D_POOL), F32)],
        scratch_shapes=[pltpu.VMEM((T + POOL_HALO, HEAD), F32), pltpu.VMEM((T + POOL_HALO, HEAD), F32), pltpu.VMEM((T, HEAD), F32)],
        semantics=("parallel",), jobs=jobs)


def _conv_taps(xs, cw):
    v = cw[CONV_WIDTH - 1] * xs[SUBLANES:]
    for k in range(CONV_WIDTH - 1):
        v += cw[k] * _shift_rows(xs, CONV_WIDTH - 1 - k)[SUBLANES:]
    return v


def _tap_rows(cw_ref):
    return [cw_ref[k:k + 1, :] for k in range(CONV_WIDTH)]


def _softplus_neg(lam):
    return jnp.maximum(-lam, 0.0) + _log1p(jnp.exp(-jnp.abs(lam)))


def _lru_gates(v, wa, ba, wx, bx, sp):
    vb = v.astype(BF16)
    ra = _sigmoid(_dot(vb, wa) + ba)
    ix = _sigmoid(_dot(vb, wx) + bx)
    log_a = -LRU_C * ra * sp
    a = jnp.exp(log_a)
    sq = jnp.sqrt(-jnp.tanh(log_a) * (a * a + 1.0))
    return ra, ix, a, sq


def _row_bcast(v, r):
    return jnp.broadcast_to(v[r:r + 1, :], v.shape)


def _tile_scans(a, b, direction):
    ri = lax.broadcasted_iota(jnp.int32, a.shape, 0) % SUBLANES
    A, B = a, b
    for s in (1, 2, 4):
        ok = (ri >= s) if direction == 1 else (ri + s < SUBLANES)
        As, Bs = _shift_rows(A, s * direction), _shift_rows(B, s * direction)
        B = jnp.where(ok, A * Bs + B, B)
        A = jnp.where(ok, A * As, A)
    return A, B


TILES_PER_STEP = 8


def _carry_tiles(A_s, B_s, out, ntile, direction):
    out_row = SUBLANES - 1 if direction == 1 else 0

    def step(k, carry):
        for j in range(TILES_PER_STEP):
            t = k * TILES_PER_STEP + j
            r0 = pl.multiple_of((t if direction == 1 else ntile - 1 - t) * SUBLANES, SUBLANES)
            A, B = A_s[pl.ds(r0, SUBLANES), :], B_s[pl.ds(r0, SUBLANES), :]
            out[pl.ds(r0, SUBLANES), :] = A * carry + B
            carry = _row_bcast(A, out_row) * carry + _row_bcast(B, out_row)
        return carry

    lax.fori_loop(0, ntile // TILES_PER_STEP, step, jnp.zeros((SUBLANES, HEAD), F32))


def _rnn_fwd(proj, conv_w, conv_b, w_a, b_a, w_x, b_x, lam, jobs=()):
    T = proj.shape[0]
    nchunk = T // CHUNK
    ntile = T // SUBLANES

    def body(u_ref, ug_ref, cw_ref, cb_ref, wa_ref, ba_ref, wx_ref, bx_ref, lam_ref, h_ref, z_ref, upad, a_s, b_s):
        _pad_front(upad, u_ref, SUBLANES)
        cw, cb = _tap_rows(cw_ref), cb_ref[...]
        wa, wx = wa_ref[...].astype(BF16), wx_ref[...].astype(BF16)
        ba, bx = ba_ref[...], bx_ref[...]
        sp = _softplus_neg(lam_ref[...])

        def chunk(i, carry):
            r0 = pl.multiple_of(i * CHUNK, CHUNK)
            v = _conv_taps(upad[pl.ds(r0, CHUNK + SUBLANES), :], cw) + cb
            _, ix, a, sq = _lru_gates(v, wa, ba, wx, bx, sp)
            a_s[pl.ds(r0, CHUNK), :], b_s[pl.ds(r0, CHUNK), :] = _tile_scans(a, sq * ix * v, 1)
            return carry

        lax.fori_loop(0, nchunk, chunk, 0)
        _carry_tiles(a_s, b_s, h_ref, ntile, 1)

        def chunk3(i, carry):
            r0 = pl.multiple_of(i * CHUNK, CHUNK)
            gl, _ = _gelu_parts(ug_ref[pl.ds(r0, CHUNK), :])
            z_ref[pl.ds(r0, CHUNK), :] = (h_ref[pl.ds(r0, CHUNK), :] * gl).astype(BF16)
            return carry

        lax.fori_loop(0, nchunk, chunk3, 0)

    col = pl.BlockSpec((T, HEAD), lambda h: (0, h))
    vec = pl.BlockSpec((1, HEAD), lambda h: (0, h))
    mat = pl.BlockSpec((None, HEAD, HEAD), lambda h: (h, 0, 0))
    return _pallas(
        body, (proj, proj, conv_w, conv_b, w_a, b_a, w_x, b_x, lam), name="rnn_fwd", grid=(N_RNN_HEADS,),
        in_specs=[pl.BlockSpec((T, HEAD), lambda h: (0, COL_RNN + h)), pl.BlockSpec((T, HEAD), lambda h: (0, COL_GATE + h)),
                  pl.BlockSpec((CONV_WIDTH, HEAD), lambda h: (0, h)), vec, mat, vec, mat, vec, vec],
        out_specs=[col, col],
        out_shape=[jax.ShapeDtypeStruct((T, D_RNN), F32), jax.ShapeDtypeStruct((T, D_RNN), BF16)],
        scratch_shapes=[pltpu.VMEM((T + SUBLANES, HEAD), F32), pltpu.VMEM((T, HEAD), F32), pltpu.VMEM((T, HEAD), F32)],
        semantics=("parallel",), jobs=jobs)


def _rnn_bwd(proj, hr, dz, conv_w, conv_b, w_a, b_a, w_x, b_x, lam, jobs=()):
    T = proj.shape[0]
    nchunk = T // CHUNK
    ntile = T // SUBLANES

    def body(u_ref, ug_ref, h_ref, dz_ref, cw_ref, cb_ref, wa_ref, ba_ref, wx_ref, bx_ref, lam_ref,
             du_ref, dug_ref, dwa_ref, dwx_ref, dba_ref, dbx_ref, dlam_ref, dcb_ref, dcw_ref,
             upad, hpad, apad, v_s, ra_s, ix_s, sq_s, g_s, dvpad, ga_s):
        zero_tile = jnp.zeros((SUBLANES, HEAD), F32)
        _pad_front(upad, u_ref, SUBLANES)
        _pad_front(hpad, h_ref, SUBLANES)
        apad[pl.ds(T, SUBLANES), :] = zero_tile
        dvpad[pl.ds(T, SUBLANES), :] = zero_tile
        for ref in (dwa_ref, dwx_ref, dba_ref, dbx_ref, dlam_ref, dcb_ref, dcw_ref):
            ref[...] = jnp.zeros_like(ref)
        cw, cb = _tap_rows(cw_ref), cb_ref[...]
        wa, wx = wa_ref[...].astype(BF16), wx_ref[...].astype(BF16)
        ba, bx = ba_ref[...], bx_ref[...]
        lam_row = lam_ref[...]
        sp = _softplus_neg(lam_row)

        def chunk(i, carry):
            r0 = pl.multiple_of(i * CHUNK, CHUNK)
            rows = pl.ds(r0, CHUNK)
            h = h_ref[rows, :]
            v = _conv_taps(upad[pl.ds(r0, CHUNK + SUBLANES), :], cw) + cb
            ra, ix, a, sq = _lru_gates(v, wa, ba, wx, bx, sp)
            v_s[rows, :], ra_s[rows, :], ix_s[rows, :], sq_s[rows, :], apad[rows, :] = v, ra, ix, sq, a
            gl, dgl = _gelu_parts(ug_ref[rows, :])
            d = dz_ref[rows, :]
            g_s[rows, :] = d * gl
            dug_ref[rows, :] = (d * h * dgl).astype(BF16)
            return carry

        lax.fori_loop(0, nchunk, chunk, 0)

        def chunk2(i, carry):
            r0 = pl.multiple_of(i * CHUNK, CHUNK)
            rows = pl.ds(r0, CHUNK)
            a_next = _shift_rows(apad[pl.ds(r0, CHUNK + SUBLANES), :], -1)[:CHUNK]
            ga_s[rows, :], g_s[rows, :] = _tile_scans(a_next, g_s[rows, :], -1)
            return carry

        lax.fori_loop(0, nchunk, chunk2, 0)
        _carry_tiles(ga_s, g_s, g_s, ntile, -1)

        def chunk3(i, carry):
            r0 = pl.multiple_of(i * CHUNK, CHUNK)
            rows = pl.ds(r0, CHUNK)
            g = g_s[rows, :]
            h_prev = _shift_rows(hpad[pl.ds(r0, CHUNK + SUBLANES), :], 1)[SUBLANES:]
            v, ra, ix, sq, a = v_s[rows, :], ra_s[rows, :], ix_s[rows, :], sq_s[rows, :], apad[rows, :]
            d_sq = g * ix * v
            d_ix = g * sq * v
            d_la = a * g * h_prev - d_sq * a * a / sq
            dlam_ref[...] += jnp.sum(d_la * ra, axis=0, keepdims=True)
            d_pa = d_la * (-LRU_C) * sp * ra * (1.0 - ra)
            d_px = d_ix * ix * (1.0 - ix)
            vb, d_pab, d_pxb = v.astype(BF16), d_pa.astype(BF16), d_px.astype(BF16)
            dwa_ref[...] += _dot_tn(vb, d_pab)
            dwx_ref[...] += _dot_tn(vb, d_pxb)
            dba_ref[...] += jnp.sum(d_pa, axis=0, keepdims=True)
            dbx_ref[...] += jnp.sum(d_px, axis=0, keepdims=True)
            dv = g * sq * ix + _dot_nt(d_pab, wa) + _dot_nt(d_pxb, wx)
            dvpad[rows, :] = dv
            dcb_ref[...] += jnp.sum(dv, axis=0, keepdims=True)
            xs = upad[pl.ds(r0, CHUNK + SUBLANES), :]
            for k in range(CONV_WIDTH):
                u_k = _shift_rows(xs, CONV_WIDTH - 1 - k)[SUBLANES:] if k < CONV_WIDTH - 1 else xs[SUBLANES:]
                dcw_ref[k:k + 1, :] += jnp.sum(dv * u_k, axis=0, keepdims=True)
            return carry

        lax.fori_loop(0, nchunk, chunk3, 0)
        dlam_ref[...] = dlam_ref[...] * (LRU_C * _sigmoid(-lam_row))

        def chunk4(i, carry):
            r0 = pl.multiple_of(i * CHUNK, CHUNK)
            dvs = dvpad[pl.ds(r0, CHUNK + SUBLANES), :]
            du = cw[CONV_WIDTH - 1] * dvs[:CHUNK]
            for k in range(CONV_WIDTH - 1):
                du += cw[k] * _shift_rows(dvs, -(CONV_WIDTH - 1 - k))[:CHUNK]
            du_ref[pl.ds(r0, CHUNK), :] = du.astype(BF16)
            return carry

        lax.fori_loop(0, nchunk, chunk4, 0)

    col = pl.BlockSpec((T, HEAD), lambda h: (0, h))
    vec = pl.BlockSpec((1, HEAD), lambda h: (0, h))
    mat = pl.BlockSpec((None, HEAD, HEAD), lambda h: (h, 0, 0))
    taps = pl.BlockSpec((CONV_WIDTH, HEAD), lambda h: (0, h))
    vec_out = jax.ShapeDtypeStruct((1, D_RNN), F32)
    mat_out = jax.ShapeDtypeStruct((N_RNN_HEADS, HEAD, HEAD), F32)
    seq = pltpu.VMEM((T, HEAD), F32)
    seq_pad = pltpu.VMEM((T + SUBLANES, HEAD), F32)
    return _pallas(
        body, (proj, proj, hr, dz, conv_w, conv_b, w_a, b_a, w_x, b_x, lam), name="rnn_bwd", grid=(N_RNN_HEADS,),
        in_specs=[pl.BlockSpec((T, HEAD), lambda h: (0, COL_RNN + h)), pl.BlockSpec((T, HEAD), lambda h: (0, COL_GATE + h)),
                  col, col, taps, vec, mat, vec, mat, vec, vec],
        out_specs=[col, col, mat, mat, vec, vec, vec, vec, taps],
        out_shape=[jax.ShapeDtypeStruct((T, D_RNN), BF16), jax.ShapeDtypeStruct((T, D_RNN), BF16), mat_out, mat_out,
                   vec_out, vec_out, vec_out, vec_out, jax.ShapeDtypeStruct((CONV_WIDTH, D_RNN), F32)],
        scratch_shapes=[seq_pad, seq_pad, seq_pad, seq, seq, seq, seq, seq, seq_pad, seq],
        semantics=("parallel",), jobs=jobs)


GROUP_FFN = ["w_ffn_out", "w_ffn_in"]
GROUP_MIX = ["w_o", "w_pool_out", "w_rnn_out"]
GROUP_IN = ["w_in"]


def _step(x, target, s, full, conv_w_mine, place):
    T = x.shape[0]
    tall, mid, low = min(T, 2048), min(T, 1024), min(T, 512)
    c1 = place[1:]
    full = dict(full)

    def gathered(names, results):
        full.update(zip(names, results))

    (full["w_in"], conv_w), = _run_jobs([_gather_job(full, ["w_in"], conv_w_mine)], "gather_w_in")
    early = ["w_pool_out", "w_rnn_out", "w_o", "w_ffn_out"]
    (proj, h1), (res,) = _norm_matmul(x, s["norm_mix"], full["w_in"], tm=tall, tn=512, name="in_proj", jobs=[_gather_job(full, early)])
    gathered(early, res)
    pm = _pool_fwd(proj, s["w_pool_grp"], s["pool_scale"])
    (hr, z), (res,) = _rnn_fwd(proj, conv_w, s["conv_b"], s["w_rg_a"], s["b_rg_a"], s["w_rg_x"], s["b_rg_x"], s["lru_lambda"],
                               jobs=[_gather_job(full, ["w_ffn_in"])])
    gathered(["w_ffn_in"], res)
    y_pool, y_rnn, mix = _branch_mix(pm, z, full["w_pool_out"], full["w_rnn_out"], proj, tm=tall, tn=256)
    x2 = _out_proj_residual(mix, full["w_o"], x, tm=mid)
    gate, up, act, h2 = _ffn_in(x2, s["norm_ffn"], full["w_ffn_in"], tm=tall, tn=256)
    dx3, dx3b, sq_cols, g_norm_final = _ffn_out_loss(act, full["w_ffn_out"], x2, s["norm_final"], target, tm=low)

    g = {"norm_final": g_norm_final}

    def chip_sums(names, from_sibling):
        sums = {name: _chip_sum(name, g[name], got, c1) for name, got in zip(names, from_sibling)}
        return {name: v[0] for name, v in sums.items()}, {name: v[1] for name, v in sums.items()}

    def final_sums(names, sums, from_chips):
        return {name: _final_sum(name, sums[name], got, place) for name, got in zip(names, from_chips)}

    dgate, dup = _ffn_out_bwd(dx3b, full["w_ffn_out"], gate, up, tm=tall, tn=256)
    g["w_ffn_out"] = _weight_grad(act, [dx3b], tm=256, tn=D_MODEL, name="w_ffn_out_grad")
    dx2, dx2b, g["norm_ffn"] = _ffn_in_bwd(dgate, dup, full["w_ffn_in"], dx3, x2, s["norm_ffn"], tm=low)
    g["w_ffn_in"] = _weight_grad(h2, [dgate, dup], tm=D_MODEL, tn=256, name="w_ffn_in_grad")
    (dgp, dgr, dyp, dyr), (res,) = _out_proj_bwd(dx2b, full["w_o"], proj, y_pool, y_rnn, tm=tall, tn=256,
                                                 jobs=[_sibling_job(g, GROUP_FFN)])
    sums_ffn, sums_ffn_bf16 = chip_sums(GROUP_FFN, res)
    g["w_o"] = _weight_grad(mix, [dx2b], tm=D_MODEL, tn=256, name="w_o_grad")
    dpm, dz = _branch_bwd(dyp, dyr, full["w_pool_out"], full["w_rnn_out"], tm=mid)
    g["w_pool_out"] = _weight_grad(pm, [dyp], tm=D_POOL, tn=256, name="w_pool_out_grad")
    g["w_rnn_out"] = _weight_grad(z, [dyr], tm=D_RNN, tn=256, name="w_rnn_out_grad")
    (dupool, g["w_pool_grp"], g["pool_scale"]), (res,) = _pool_bwd(proj, dpm, s["w_pool_grp"], s["pool_scale"],
                                                                   jobs=[_sibling_job(g, GROUP_MIX)])
    sums_mix, sums_mix_bf16 = chip_sums(GROUP_MIX, res)
    ((durnn, dugate, g["w_rg_a"], g["w_rg_x"], g["b_rg_a"], g["b_rg_x"], g["lru_lambda"], g["conv_b"], g["conv_w"]),
     (res,)) = _rnn_bwd(proj, hr, dz, conv_w, s["conv_b"], s["w_rg_a"], s["b_rg_a"], s["w_rg_x"], s["b_rg_x"], s["lru_lambda"],
                        jobs=[_chips_job(sums_ffn_bf16, GROUP_FFN)])
    shards = final_sums(GROUP_FFN, sums_ffn, res)
    segs = [dupool, durnn, dugate, dgp, dgr]
    grad_x, g["norm_mix"] = _in_proj_bwd(segs, full["w_in"], dx2, x, s["norm_mix"], tm=low)
    g["w_in"], (res, joined) = _weight_grad(h1, segs, tm=D_MODEL, tn=256, name="w_in_grad",
                                           jobs=[_chips_job(sums_mix_bf16, GROUP_MIX), _join_job(shards, GROUP_FFN)])
    grads = dict(zip(GROUP_FFN, joined))
    shards = final_sums(GROUP_MIX, sums_mix, res)
    (res,) = _run_jobs([_sibling_job(g, GROUP_IN)], "w_in_exchange_sibling")
    sums_in, sums_in_bf16 = chip_sums(GROUP_IN, res)

    vec_rows = [g[name] if name != "pool_scale" else jnp.pad(g[name], ((0, 0), (0, D_MODEL - D_POOL))) for name in VEC_ITEMS]
    vec_rows += [g["conv_w"], sq_cols, jnp.zeros((VEC_ROWS - len(VEC_ITEMS) - CONV_WIDTH - 1, D_MODEL), F32)]
    vec = jnp.concatenate(vec_rows, axis=0).reshape(VEC_ROWS, N_DEV, HEAD).transpose(1, 0, 2)
    mat = jnp.concatenate([g[name].reshape(-1, HEAD) for name in MAT_ITEMS], axis=0).reshape(N_DEV, -1, HEAD)
    (vec, mat), (res, joined) = _all_reduce_small([vec, mat], jobs=[_chips_job(sums_in_bf16, GROUP_IN), _join_job(shards, GROUP_MIX)])
    grads.update(zip(GROUP_MIX, joined))
    (joined,) = _run_jobs([_join_job(final_sums(GROUP_IN, sums_in, res), GROUP_IN)], "w_in_join_halves")
    grads.update(zip(GROUP_IN, joined))

    vec = vec.transpose(1, 0, 2).reshape(VEC_ROWS, D_MODEL)
    mat = mat.reshape(-1, HEAD)
    for k, name in enumerate(VEC_ITEMS):
        grads[name] = vec[k:k + 1, :s[name].shape[1]]
    grads["conv_w"] = vec[len(VEC_ITEMS):len(VEC_ITEMS) + CONV_WIDTH]
    row = 0
    for name in MAT_ITEMS:
        rows = s[name].shape[0] * HEAD
        grads[name] = mat[row:row + rows]
        row += rows
    return vec[len(VEC_ITEMS) + CONV_WIDTH], grad_x, grads


LARGE = {"w_in": "col", "w_pool_out": "col", "w_rnn_out": "row", "w_o": "row", "w_ffn_in": "col", "w_ffn_out": "row"}
LARGE_SHAPE = {"w_in": (D_MODEL, D_IN), "w_pool_out": (D_POOL, D_MODEL), "w_rnn_out": (D_RNN, D_MODEL),
               "w_o": (D_MODEL, D_MODEL), "w_ffn_in": (D_MODEL, 2 * D_FF), "w_ffn_out": (D_FF, D_MODEL)}


def _place():
    x, y, c = lax.axis_index("x"), lax.axis_index("y"), lax.axis_index("c")
    return 2 * x + y, c


def _chip_device(chip, c):
    return (chip // 2, chip % 2, c)


def _chip_window(ref, kind, shape, chip, half=None):
    K, N = shape
    if kind == "col":
        rows = slice(None) if half is None else pl.ds(half * (K // 2), K // 2)
        return ref.at[rows, pl.ds(chip * (N // N_CHIPS), N // N_CHIPS)]
    ks = K // N_CHIPS
    if half is None:
        return ref.at[pl.ds(chip * ks, ks), :]
    return ref.at[pl.ds(chip * ks + half * (ks // 2), ks // 2), :]


def _row_half(ref, half):
    rows = ref.shape[0] // 2
    return ref.at[pl.ds(half * rows, rows), :]


def _remote(win_src, win_dst, send_sems, recv_sems, idx, to):
    return pltpu.make_async_remote_copy(src_ref=win_src, dst_ref=win_dst, send_sem=send_sems.at[idx], recv_sem=recv_sems.at[idx],
                                        device_id=to, device_id_type=MESH)


def _gather_job(full, names, conv_w_full=None):
    n = len(names)
    cw_cols = D_RNN // N_CHIPS

    def windows(refs, chip, half):
        return [_chip_window(refs[k], LARGE[name], LARGE_SHAPE[name], chip, half) for k, name in enumerate(names)]

    def ici_copies(refs, send_sems, recv_sems, src_chip, dst_chip, c, r):
        wins = windows(refs, src_chip, c)
        if conv_w_full is not None:
            wins.append(refs[n].at[:, pl.ds(src_chip * cw_cols, cw_cols)])
        return [_remote(win, win, send_sems, recv_sems, (k, r), _chip_device(dst_chip, c)) for k, win in enumerate(wins)]

    def forwards(refs, send_sems, recv_sems, src_chip, half, to_core, chip, r):
        return [_remote(win, win, send_sems, recv_sems, (k, 3 + r), _chip_device(chip, to_core))
                for k, win in enumerate(windows(refs, src_chip, half))]

    def start(ins, outs, send_sems, recv_sems):
        chip, c = _place()
        for r in range(3):
            for cp in ici_copies(outs, send_sems, recv_sems, chip, chip ^ (r + 1), c, r):
                cp.start()

    def finish(ins, outs, send_sems, recv_sems):
        chip, c = _place()
        for r in range(3):
            for cp in ici_copies(outs, send_sems, recv_sems, chip ^ (r + 1), chip, c, r):
                cp.wait_recv()
            for cp in forwards(outs, send_sems, recv_sems, chip ^ (r + 1), c, 1 - c, chip, r):
                cp.start()
        for r in range(3):
            for cp in forwards(outs, send_sems, recv_sems, chip ^ (r + 1), 1 - c, c, chip, r):
                cp.wait_recv()
            for cp in ici_copies(outs, send_sems, recv_sems, chip, chip ^ (r + 1), c, r):
                cp.wait_send()
            for cp in forwards(outs, send_sems, recv_sems, chip ^ (r + 1), c, 1 - c, chip, r):
                cp.wait_send()

    arrays = [full[name] for name in names] + ([conv_w_full] if conv_w_full is not None else [])
    return _Job(arrays, [jax.ShapeDtypeStruct(a.shape, a.dtype) for a in arrays], {k: k for k in range(len(arrays))},
                (len(arrays), 6), start, finish)


def _core_halves(ref, kind, shape, c):
    return [_chip_window(ref, kind, shape, chip, c) for chip in range(N_CHIPS)]


def _sibling_job(grads, names):
    def start(ins, outs, send_sems, recv_sems):
        chip, c = _place()
        for k, name in enumerate(names):
            kind, shape = LARGE[name], LARGE_SHAPE[name]
            if kind == "col":
                pairs = [(_row_half(ins[k], 1 - c), outs[k])]
            else:
                rows = shape[0] // N_DEV
                pairs = [(win, outs[k].at[pl.ds(j * rows, rows), :]) for j, win in enumerate(_core_halves(ins[k], kind, shape, 1 - c))]
            for src, dst in pairs:
                _remote(src, dst, send_sems, recv_sems, k, _chip_device(chip, 1 - c)).start()

    def finish(ins, outs, send_sems, recv_sems):
        chip, c = _place()
        for k in range(len(names)):
            _remote(outs[k], outs[k], send_sems, recv_sems, k, _chip_device(chip, 1 - c)).wait()

    return _Job([grads[name] for name in names],
                [jax.ShapeDtypeStruct((LARGE_SHAPE[name][0] // 2, LARGE_SHAPE[name][1]), F32) for name in names], {},
                (len(names),), start, finish)


def _chip_sum(name, g, got, c):
    kind, (K, N) = LARGE[name], LARGE_SHAPE[name]
    rows = K // N_DEV

    def body(c_ref, g_ref, got_ref, o_ref, ob_ref):
        total = g_ref[...] + got_ref[...]
        o_ref[...] = total
        ob_ref[...] = total.astype(BF16)

    if kind == "col":
        mine = pl.BlockSpec((rows, N), lambda j, c_ref: (j + N_CHIPS * c_ref[0], 0))
    else:
        mine = pl.BlockSpec((rows, N), lambda j, c_ref: (2 * j + c_ref[0], 0))
    blk = pl.BlockSpec((rows, N), lambda j, c_ref: (j, 0))
    return pl.pallas_call(
        body, name=name + "_chip_sum",
        grid_spec=pltpu.PrefetchScalarGridSpec(num_scalar_prefetch=1, grid=(N_CHIPS,), in_specs=[mine, blk], out_specs=[blk, blk]),
        out_shape=[jax.ShapeDtypeStruct((K // 2, N), F32), jax.ShapeDtypeStruct((K // 2, N), BF16)],
        compiler_params=_params(dimension_semantics=("parallel",)),
    )(c, g, got)


def _piece(ref, kind, shape, chip):
    K, N = shape
    if kind == "col":
        return ref.at[:, pl.ds(chip * (N // N_CHIPS), N // N_CHIPS)]
    return ref.at[pl.ds(chip * (K // N_DEV), K // N_DEV), :]


def _piece_shape(name):
    kind, (K, N) = LARGE[name], LARGE_SHAPE[name]
    return (K // 2, N // N_CHIPS) if kind == "col" else (K // N_DEV, N)


def _chips_job(sums, names):
    def copies(ins, outs, send_sems, recv_sems):
        chip, c = _place()
        return [_remote(_piece(ins[k], LARGE[name], LARGE_SHAPE[name], chip ^ (r + 1)), outs[k].at[r], send_sems, recv_sems, (k, r),
                        _chip_device(chip ^ (r + 1), c)) for k, name in enumerate(names) for r in range(3)]

    def start(*refs):
        for cp in copies(*refs):
            cp.start()

    def finish(*refs):
        for cp in copies(*refs):
            cp.wait()

    return _Job([sums[name] for name in names], [jax.ShapeDtypeStruct((3,) + _piece_shape(name), BF16) for name in names], {},
                (len(names), 3), start, finish)


def _final_sum(name, chip_sum, got, place):
    kind = LARGE[name]
    rows, cols = _piece_shape(name)

    def body(place_ref, s_ref, got_ref, o_ref):
        o_ref[...] = ((s_ref[...] + got_ref[0].astype(F32)) + got_ref[1].astype(F32)) + got_ref[2].astype(F32)

    if kind == "col":
        mine = pl.BlockSpec((rows, cols), lambda i, place_ref: (0, place_ref[0]))
    else:
        mine = pl.BlockSpec((rows, cols), lambda i, place_ref: (place_ref[0], 0))
    return pl.pallas_call(
        body, name=name + "_final_sum",
        grid_spec=pltpu.PrefetchScalarGridSpec(
            num_scalar_prefetch=1, grid=(1,), in_specs=[mine, pl.BlockSpec((3, rows, cols), lambda i, place_ref: (0, 0, 0))],
            out_specs=pl.BlockSpec((rows, cols), lambda i, place_ref: (place_ref[1], 0))),
        out_shape=jax.ShapeDtypeStruct((2 * rows, cols), F32),
        compiler_params=_params(dimension_semantics=("arbitrary",)),
    )(place, chip_sum, got)


def _join_job(shards, names):
    def half_copy(outs, send_sems, recv_sems, k, mine):
        chip, c = _place()
        win = _row_half(outs[k], c if mine else 1 - c)
        return _remote(win, win, send_sems, recv_sems, k, _chip_device(chip, 1 - c))

    def start(ins, outs, send_sems, recv_sems):
        for k in range(len(names)):
            half_copy(outs, send_sems, recv_sems, k, True).start()

    def finish(ins, outs, send_sems, recv_sems):
        for k in range(len(names)):
            half_copy(outs, send_sems, recv_sems, k, True).wait_send()
            half_copy(outs, send_sems, recv_sems, k, False).wait_recv()

    arrays = [shards[name] for name in names]
    return _Job(arrays, [jax.ShapeDtypeStruct(a.shape, F32) for a in arrays], {k: k for k in range(len(arrays))},
                (len(arrays),), start, finish)


VEC_ROWS = 16


def _all_reduce_small(slabs, jobs=()):
    n = len(slabs)

    def body(*refs):
        in_refs, out_refs, got_refs = refs[:n], refs[n:2 * n], refs[2 * n:3 * n]
        send_sems, recv_sems = refs[3 * n:]
        x, y, c = lax.axis_index("x"), lax.axis_index("y"), lax.axis_index("c")
        me = 4 * x + 2 * y + c

        def remote(src, dst, k, phase, r):
            other = me ^ r
            return pltpu.make_async_remote_copy(src_ref=src, dst_ref=dst, send_sem=send_sems.at[k, phase, r],
                                                recv_sem=recv_sems.at[k, phase, r],
                                                device_id=(other // 4, (other // 2) % 2, other % 2), device_id_type=MESH)

        scatter = [remote(in_refs[k].at[me ^ r], got_refs[k].at[r], k, 0, r) for r in range(1, N_DEV) for k in range(n)]
        for cp in scatter:
            cp.start()
        for cp in scatter:
            cp.wait()
        for k in range(n):
            total = in_refs[k][me]
            for r in range(1, N_DEV):
                total = total + got_refs[k][r]
            out_refs[k][me] = total
        gather = [remote(out_refs[k].at[me], out_refs[k].at[me], k, 1, r) for r in range(1, N_DEV) for k in range(n)]
        for cp in gather:
            cp.start()
        for r in range(1, N_DEV):
            for k in range(n):
                remote(out_refs[k].at[me ^ r], out_refs[k].at[me ^ r], k, 1, r).wait_recv()
        for cp in gather:
            cp.wait_send()

    return _pallas(
        body, slabs, name="all_reduce_small", grid=(), in_specs=[VMEM] * n, out_specs=[VMEM] * n,
        out_shape=[jax.ShapeDtypeStruct(s.shape, F32) for s in slabs],
        scratch_shapes=[pltpu.VMEM(s.shape, F32) for s in slabs]
        + [pltpu.SemaphoreType.DMA((n, 2, N_DEV)), pltpu.SemaphoreType.DMA((n, 2, N_DEV))], jobs=jobs)


def _cast_into_whole(w, name, place):
    rows, cols = w.shape
    tr = rows // 2

    def body(place_ref, w_ref, o_ref):
        o_ref[...] = w_ref[...].astype(BF16)

    if LARGE[name] == "col":
        window = pl.BlockSpec((tr, cols), lambda i, place_ref: (i, place_ref[0]))
    else:
        window = pl.BlockSpec((tr, cols), lambda i, place_ref: (2 * place_ref[0] + i, 0))
    return pl.pallas_call(
        body, name=name + "_cast",
        grid_spec=pltpu.PrefetchScalarGridSpec(num_scalar_prefetch=1, grid=(2,),
                                               in_specs=[pl.BlockSpec((tr, cols), lambda i, place_ref: (i, 0))], out_specs=window),
        out_shape=jax.ShapeDtypeStruct(LARGE_SHAPE[name], BF16),
        compiler_params=_params(dimension_semantics=("parallel",)))(place, w)


def _adamw_math(w, g, m, v):
    m = ADAM_B1 * m + (1.0 - ADAM_B1) * g
    v = ADAM_B2 * v + (1.0 - ADAM_B2) * (g * g)
    m_hat = m / (1.0 - ADAM_B1 ** ADAM_STEP)
    v_hat = v / (1.0 - ADAM_B2 ** ADAM_STEP)
    delta = -ADAM_LR * (m_hat / (jnp.sqrt(v_hat) + ADAM_EPS) + ADAM_WD * w)
    return delta, m, v


def _adamw_large(w, g, m, v, name):
    rows, cols = w.shape
    tr = rows // 4

    def body(w_ref, g_ref, m_ref, v_ref, d_ref, mo_ref, vo_ref):
        d_ref[...], mo_ref[...], vo_ref[...] = _adamw_math(w_ref[...], g_ref[...], m_ref[...], v_ref[...])

    blk = pl.BlockSpec((tr, cols), lambda i: (i, 0))
    out = jax.ShapeDtypeStruct(w.shape, F32)
    return pl.pallas_call(body, name=name + "_adamw", grid=(4,), in_specs=[blk] * 4, out_specs=[blk] * 3, out_shape=[out] * 3,
                          compiler_params=_params(dimension_semantics=("parallel",)))(w, g, m, v)


def _adamw_small(ws, gs, ms, vs):
    n = len(ws)

    def body(*refs):
        for k in range(n):
            w_ref, g_ref, m_ref, v_ref = (refs[q * n + k] for q in range(4))
            d_ref, mo_ref, vo_ref = (refs[(4 + q) * n + k] for q in range(3))
            d_ref[...], mo_ref[...], vo_ref[...] = _adamw_math(w_ref[...], g_ref[...], m_ref[...], v_ref[...])

    out = [jax.ShapeDtypeStruct(w.shape, F32) for w in ws]
    res = pl.pallas_call(body, name="small_adamw", in_specs=[VMEM] * (4 * n), out_specs=[VMEM] * (3 * n), out_shape=out * 3,
                         compiler_params=_params())(*ws, *gs, *ms, *vs)
    return res[:n], res[n:2 * n], res[2 * n:]


WEIGHTS = ["norm_mix", "w_in", "w_pool_grp", "pool_scale", "w_pool_out", "conv_w", "conv_b", "w_rg_a", "b_rg_a", "w_rg_x",
           "b_rg_x", "lru_lambda", "w_rnn_out", "w_o", "norm_ffn", "w_ffn_in", "w_ffn_out", "norm_final"]
VEC_ITEMS = ["norm_mix", "norm_ffn", "norm_final", "pool_scale", "conv_b", "lru_lambda", "b_rg_a", "b_rg_x"]
MAT_ITEMS = ["w_pool_grp", "w_rg_a", "w_rg_x"]


def _as2d(name, a):
    if name in MAT_ITEMS:
        return a.reshape(-1, HEAD, HEAD)
    if name == "conv_w":
        return a.reshape(CONV_WIDTH, -1)
    return a.reshape(1, -1)


def kernel(x, norm_mix, w_in, w_pool_grp, pool_scale, w_pool_out, conv_w, conv_b, w_rg_a, b_rg_a, w_rg_x, b_rg_x, lru_lambda, w_rnn_out, w_o, norm_ffn, w_ffn_in, w_ffn_out, norm_final, loss_target, m_norm_mix, m_w_in, m_w_pool_grp, m_pool_scale, m_w_pool_out, m_conv_w, m_conv_b, m_w_rg_a, m_b_rg_a, m_w_rg_x, m_b_rg_x, m_lru_lambda, m_w_rnn_out, m_w_o, m_norm_ffn, m_w_ffn_in, m_w_ffn_out, m_norm_final, v_norm_mix, v_w_in, v_w_pool_grp, v_pool_scale, v_w_pool_out, v_conv_w, v_conv_b, v_w_rg_a, v_b_rg_a, v_w_rg_x, v_b_rg_x, v_lru_lambda, v_w_rnn_out, v_w_o, v_norm_ffn, v_w_ffn_in, v_w_ffn_out, v_norm_final):
    given = dict(locals())
    w = {name: given[name] for name in WEIGHTS}
    m = {name: given["m_" + name] for name in WEIGHTS}
    v = {name: given["v_" + name] for name in WEIGHTS}
    chip, c = _place()

    place = jnp.stack([chip, c]).astype(jnp.int32)
    conv_cols = w["conv_w"].shape[-1]
    conv_w_mine = lax.dynamic_update_slice_in_dim(jnp.zeros((CONV_WIDTH, D_RNN), F32), w["conv_w"][0], chip * conv_cols, axis=1)
    full = {name: _cast_into_whole(w[name][0], name, place) for name in LARGE}
    small = {name: _as2d(name, w[name]) for name in WEIGHTS if name not in LARGE and name != "conv_w"}
    sq_cols, grad_x, grads = _step(x[0], loss_target[0], small, full, conv_w_mine, place)
    loss = 0.5 / D_MODEL * jnp.sum(sq_cols)
    grads["conv_w"] = lax.dynamic_slice_in_dim(grads["conv_w"], chip * conv_cols, conv_cols, axis=1)

    delta, new_m, new_v = {}, {}, {}
    for name in LARGE:
        delta[name], new_m[name], new_v[name] = _adamw_large(w[name][0], grads[name], m[name][0], v[name][0], name)
    small_names = [name for name in WEIGHTS if name not in LARGE]
    flat = lambda d: [d[name].reshape(grads[name].shape) for name in small_names]
    ds, mo, vo = _adamw_small(flat(w), [grads[name] for name in small_names], flat(m), flat(v))
    for k, name in enumerate(small_names):
        delta[name], new_m[name], new_v[name] = ds[k], mo[k], vo[k]

    shaped = lambda d: [d[name].reshape(w[name].shape) for name in WEIGHTS]
    return (loss, grad_x[None], *shaped(grads), *shaped(delta), *shaped(new_m), *shaped(new_v))
```

```python
import functools
import math

import jax
import jax.numpy as jnp
from jax import lax
from jax.experimental import pallas as pl
from jax.experimental.pallas import tpu as pltpu

F32 = jnp.float32
BF16 = jnp.bfloat16

D_MODEL = 1024
D_POOL = 512
N_POOL_GROUPS = 4
D_RNN = 1024
N_RNN_HEADS = 8
HEAD = 128
CONV_WIDTH = 4
LRU_C = 8.0
D_FF = 2816
D_IN = D_POOL + 2 * D_RNN + 2 * D_MODEL
NORM_EPS = 1e-6
COL_RNN = D_POOL // HEAD
COL_GATE = (D_POOL + D_RNN) // HEAD

ADAM_LR = 0.001
ADAM_B1 = 0.9
ADAM_B2 = 0.999
ADAM_EPS = 1e-08
ADAM_WD = 0.01
ADAM_STEP = 10

N_CHIPS = 4
N_DEV = 8
MESH = pl.DeviceIdType.MESH
ANY = pl.BlockSpec(memory_space=pl.ANY)
VMEM = pl.BlockSpec(memory_space=pltpu.VMEM)
VMEM_LIMIT_BYTES = 60 * 1024 * 1024
SUBLANES = 8
POOL_HALO = 16
CHUNK = 1024

GELU_C = math.sqrt(2.0 / math.pi)
GELU_A = 0.044715


def _params(**kw):
    return pltpu.CompilerParams(vmem_limit_bytes=VMEM_LIMIT_BYTES, **kw)


def _sigmoid(x):
    return 0.5 * jnp.tanh(0.5 * x) + 0.5


def _log1p(y):
    u = 1.0 + y
    d = u - 1.0
    return jnp.where(d == 0.0, y, jnp.log(u) * (y / jnp.where(d == 0.0, 1.0, d)))


def _gelu_parts(x):
    x2 = x * x
    th = jnp.tanh(GELU_C * (x + GELU_A * x * x2))
    g = 0.5 * x * (1.0 + th)
    dg = 0.5 * (1.0 + th) + 0.5 * x * (1.0 - th * th) * GELU_C * (1.0 + 3.0 * GELU_A * x2)
    return g, dg


def _dot(a, b):
    return jnp.dot(a, b, preferred_element_type=F32)


def _dot_nt(a, b):
    return lax.dot_general(a, b, (((1,), (1,)), ((), ())), preferred_element_type=F32)


def _dot_tn(a, b):
    return lax.dot_general(a, b, (((0,), (0,)), ((), ())), preferred_element_type=F32)


def _rms_scale(xv):
    return lax.rsqrt(jnp.mean(xv * xv, axis=-1, keepdims=True) + NORM_EPS)


def _rms_bwd(dy, xv, g):
    r = _rms_scale(xv)
    xh = xv * r
    dyg = dy * g
    dx = r * (dyg - xh * jnp.mean(dyg * xh, axis=-1, keepdims=True))
    return dx, dy * xh


class _Job:
    def __init__(self, inputs, out_shapes, aliases, sem_shape, start, finish):
        self.inputs, self.out_shapes, self.aliases, self.sem_shape = list(inputs), list(out_shapes), dict(aliases), sem_shape
        self.start, self.finish = start, finish


def _pallas(body, operands, *, name, grid, in_specs, out_specs, out_shape, scratch_shapes=(), semantics=None, jobs=()):
    n_in, n_out, n_scr = len(in_specs), len(out_specs), len(scratch_shapes)
    job_in = [a for job in jobs for a in job.inputs]
    job_out = [s for job in jobs for s in job.out_shapes]
    aliases, i0, o0 = {}, n_in, n_out
    for job in jobs:
        aliases.update({i0 + i: o0 + o for i, o in job.aliases.items()})
        i0, o0 = i0 + len(job.inputs), o0 + len(job.out_shapes)

    def whole(*refs):
        ins, j_ins = refs[:n_in], refs[n_in:n_in + len(job_in)]
        outs = refs[n_in + len(job_in):][:n_out]
        j_outs = refs[n_in + len(job_in) + n_out:][:len(job_out)]
        rest = refs[n_in + len(job_in) + n_out + len(job_out):]
        scr, sems = rest[:n_scr], rest[n_scr:]

        def run(phase):
            i, o = 0, 0
            for k, job in enumerate(jobs):
                getattr(job, phase)(j_ins[i:i + len(job.inputs)], j_outs[o:o + len(job.out_shapes)], sems[2 * k], sems[2 * k + 1])
                i, o = i + len(job.inputs), o + len(job.out_shapes)

        def at(step_of, phase):
            if not jobs:
                return
            if not grid:
                run(phase)
                return
            cond = functools.reduce(jnp.logical_and, [pl.program_id(d) == step_of(d) for d in range(len(grid))])
            pl.when(cond)(functools.partial(run, phase))

        at(lambda d: 0, "start")
        body(*ins, *outs, *scr)
        at(lambda d: grid[d] - 1, "finish")

    res = pl.pallas_call(
        whole, name=name, grid=grid, in_specs=list(in_specs) + [ANY] * len(job_in), out_specs=list(out_specs) + [ANY] * len(job_out),
        out_shape=list(out_shape) + job_out, input_output_aliases=aliases,
        scratch_shapes=list(scratch_shapes) + [pltpu.SemaphoreType.DMA(job.sem_shape) for job in jobs for _ in range(2)],
        compiler_params=_params(dimension_semantics=semantics, has_side_effects=bool(jobs)),
    )(*operands, *job_in)
    per_job, o = [], n_out
    for job in jobs:
        per_job.append(res[o:o + len(job.out_shapes)])
        o += len(job.out_shapes)
    return res[:n_out], per_job


def _run_jobs(jobs, name):
    return _pallas(lambda: None, [], name=name, grid=(), in_specs=[], out_specs=[], out_shape=[], jobs=jobs)[1]


NORM_ROWS = 256


def _norm_rows(x_ref, g_ref, h_ref):
    g = g_ref[...]

    def rows(i, carry):
        r = pl.ds(pl.multiple_of(i * NORM_ROWS, NORM_ROWS), NORM_ROWS)
        xv = x_ref[r, :]
        h_ref[r, :] = (xv * _rms_scale(xv) * g).astype(BF16)
        return carry

    lax.fori_loop(0, x_ref.shape[0] // NORM_ROWS, rows, 0)


def _norm_matmul(x, g, w, *, tm, tn, name, jobs=()):
    T, K = x.shape
    N = w.shape[1]

    def body(x_ref, g_ref, w_ref, o_ref, h_ref):
        @pl.when(pl.program_id(1) == 0)
        def _():
            _norm_rows(x_ref, g_ref, h_ref)

        o_ref[...] = _dot(h_ref[...], w_ref[...])

    return _pallas(
        body, (x, g, w), name=name, grid=(T // tm, N // tn),
        in_specs=[pl.BlockSpec((tm, K), lambda i, j: (i, 0)), pl.BlockSpec((1, K), lambda i, j: (0, 0)),
                  pl.BlockSpec((K, tn), lambda i, j: (0, j))],
        out_specs=[pl.BlockSpec((tm, tn), lambda i, j: (i, j)), pl.BlockSpec((tm, K), lambda i, j: (i, 0))],
        out_shape=[jax.ShapeDtypeStruct((T, N), F32), jax.ShapeDtypeStruct((T, K), BF16)],
        semantics=("parallel", "arbitrary"), jobs=jobs)


def _ffn_in(x2, g, w, *, tm, tn):
    T, K = x2.shape
    nb = D_FF // tn

    def body(x_ref, g_ref, wg_ref, wu_ref, gate_ref, up_ref, act_ref, h_ref):
        @pl.when(pl.program_id(1) == 0)
        def _():
            _norm_rows(x_ref, g_ref, h_ref)

        h = h_ref[...]
        gate = _dot(h, wg_ref[...])
        up = _dot(h, wu_ref[...])
        gate_ref[...] = gate.astype(BF16)
        up_ref[...] = up.astype(BF16)
        act_ref[...] = (gate * _sigmoid(gate) * up).astype(BF16)

    blk = pl.BlockSpec((tm, tn), lambda i, j: (i, j))
    return pl.pallas_call(
        body, name="ffn_in", grid=(T // tm, nb),
        in_specs=[pl.BlockSpec((tm, K), lambda i, j: (i, 0)), pl.BlockSpec((1, K), lambda i, j: (0, 0)),
                  pl.BlockSpec((K, tn), lambda i, j: (0, j)), pl.BlockSpec((K, tn), lambda i, j: (0, j + nb))],
        out_specs=[blk, blk, blk, pl.BlockSpec((tm, K), lambda i, j: (i, 0))],
        out_shape=[jax.ShapeDtypeStruct((T, D_FF), BF16), jax.ShapeDtypeStruct((T, D_FF), BF16),
                   jax.ShapeDtypeStruct((T, D_FF), BF16), jax.ShapeDtypeStruct((T, K), BF16)],
        compiler_params=_params(dimension_semantics=("parallel", "arbitrary")),
    )(x2, g, w, w)


def _branch_mix(pm, z, w_pool_out, w_rnn_out, proj, *, tm, tn):
    T = pm.shape[0]
    col_gp = (D_POOL + 2 * D_RNN) // tn
    col_gr = col_gp + D_MODEL // tn

    def body(pm_ref, z_ref, wp_ref, wr_ref, gp_ref, gr_ref, yp_ref, yr_ref, mix_ref):
        yp = _dot(pm_ref[...], wp_ref[...])
        yr = _dot(z_ref[...], wr_ref[...])
        yp_ref[...] = yp.astype(BF16)
        yr_ref[...] = yr.astype(BF16)
        mix_ref[...] = (_sigmoid(gp_ref[...]) * yp + _sigmoid(gr_ref[...]) * yr).astype(BF16)

    blk = pl.BlockSpec((tm, tn), lambda i, j: (i, j))
    return pl.pallas_call(
        body, name="branch_mix", grid=(T // tm, D_MODEL // tn),
        in_specs=[pl.BlockSpec((tm, D_POOL), lambda i, j: (i, 0)), pl.BlockSpec((tm, D_RNN), lambda i, j: (i, 0)),
                  pl.BlockSpec((D_POOL, tn), lambda i, j: (0, j)), pl.BlockSpec((D_RNN, tn), lambda i, j: (0, j)),
                  pl.BlockSpec((tm, tn), lambda i, j: (i, col_gp + j)), pl.BlockSpec((tm, tn), lambda i, j: (i, col_gr + j))],
        out_specs=[blk, blk, blk],
        out_shape=[jax.ShapeDtypeStruct((T, D_MODEL), BF16), jax.ShapeDtypeStruct((T, D_MODEL), BF16),
                   jax.ShapeDtypeStruct((T, D_MODEL), BF16)],
        compiler_params=_params(dimension_semantics=("parallel", "parallel")),
    )(pm, z, w_pool_out, w_rnn_out, proj, proj)


def _out_proj_residual(mix, w_o, x, *, tm):
    T = x.shape[0]

    def body(mix_ref, w_ref, x_ref, o_ref):
        o_ref[...] = x_ref[...] + _dot(mix_ref[...], w_ref[...])

    row = pl.BlockSpec((tm, D_MODEL), lambda i: (i, 0))
    return pl.pallas_call(
        body, name="out_proj_residual", grid=(T // tm,),
        in_specs=[row, pl.BlockSpec((D_MODEL, D_MODEL), lambda i: (0, 0)), row],
        out_specs=row, out_shape=jax.ShapeDtypeStruct((T, D_MODEL), F32),
        compiler_params=_params(dimension_semantics=("parallel",)),
    )(mix, w_o, x)


def _ffn_out_loss(act, w, x2, g3, target, *, tm):
    T = x2.shape[0]

    def body(act_ref, w_ref, x2_ref, g_ref, t_ref, dx_ref, dxb_ref, sq_ref, dg_ref):
        @pl.when(pl.program_id(0) == 0)
        def _():
            sq_ref[...] = jnp.zeros_like(sq_ref)
            dg_ref[...] = jnp.zeros_like(dg_ref)

        x3 = x2_ref[...] + _dot(act_ref[...], w_ref[...])
        g = g_ref[...]
        err = x3 * _rms_scale(x3) * g - t_ref[...]
        sq_ref[...] += jnp.sum(err * err, axis=0, keepdims=True)
        dx, dgp = _rms_bwd(err * (1.0 / D_MODEL), x3, g)
        dg_ref[...] += jnp.sum(dgp, axis=0, keepdims=True)
        dx_ref[...] = dx
        dxb_ref[...] = dx.astype(BF16)

    row = pl.BlockSpec((tm, D_MODEL), lambda i: (i, 0))
    vec = pl.BlockSpec((1, D_MODEL), lambda i: (0, 0))
    return pl.pallas_call(
        body, name="ffn_out_loss", grid=(T // tm,),
        in_specs=[pl.BlockSpec((tm, D_FF), lambda i: (i, 0)), pl.BlockSpec((D_FF, D_MODEL), lambda i: (0, 0)), row, vec, row],
        out_specs=[row, row, vec, vec],
        out_shape=[jax.ShapeDtypeStruct((T, D_MODEL), F32), jax.ShapeDtypeStruct((T, D_MODEL), BF16),
                   jax.ShapeDtypeStruct((1, D_MODEL), F32), jax.ShapeDtypeStruct((1, D_MODEL), F32)],
        compiler_params=_params(dimension_semantics=("arbitrary",)),
    )(act, w, x2, g3, target)


def _ffn_out_bwd(dx3b, w, gate, up, *, tm, tn):
    T = dx3b.shape[0]

    def body(dx_ref, w_ref, gate_ref, up_ref, dgate_ref, dup_ref):
        dact = _dot_nt(dx_ref[...], w_ref[...])
        gate = gate_ref[...].astype(F32)
        s = _sigmoid(gate)
        dgate_ref[...] = (dact * up_ref[...].astype(F32) * s * (1.0 + gate * (1.0 - s))).astype(BF16)
        dup_ref[...] = (dact * gate * s).astype(BF16)

    blk = pl.BlockSpec((tm, tn), lambda i, j: (i, j))
    return pl.pallas_call(
        body, name="ffn_out_bwd", grid=(T // tm, D_FF // tn),
        in_specs=[pl.BlockSpec((tm, D_MODEL), lambda i, j: (i, 0)), pl.BlockSpec((tn, D_MODEL), lambda i, j: (j, 0)), blk, blk],
        out_specs=[blk, blk],
        out_shape=[jax.ShapeDtypeStruct((T, D_FF), BF16), jax.ShapeDtypeStruct((T, D_FF), BF16)],
        compiler_params=_params(dimension_semantics=("parallel", "parallel")),
    )(dx3b, w, gate, up)


def _ffn_in_bwd(dgate, dup, w, dx3, x2, g2, *, tm):
    T = x2.shape[0]

    def body(dgate_ref, dup_ref, w_ref, dx3_ref, x2_ref, g_ref, dx_ref, dxb_ref, dg_ref):
        @pl.when(pl.program_id(0) == 0)
        def _():
            dg_ref[...] = jnp.zeros_like(dg_ref)

        dh = _dot_nt(dgate_ref[...], w_ref[:, :D_FF]) + _dot_nt(dup_ref[...], w_ref[:, D_FF:])
        dxn, dgp = _rms_bwd(dh, x2_ref[...], g_ref[...])
        dx = dx3_ref[...] + dxn
        dg_ref[...] += jnp.sum(dgp, axis=0, keepdims=True)
        dx_ref[...] = dx
        dxb_ref[...] = dx.astype(BF16)

    row = pl.BlockSpec((tm, D_MODEL), lambda i: (i, 0))
    wide = pl.BlockSpec((tm, D_FF), lambda i: (i, 0))
    vec = pl.BlockSpec((1, D_MODEL), lambda i: (0, 0))
    return pl.pallas_call(
        body, name="ffn_in_bwd", grid=(T // tm,),
        in_specs=[wide, wide, pl.BlockSpec((D_MODEL, 2 * D_FF), lambda i: (0, 0)), row, row, vec],
        out_specs=[row, row, vec],
        out_shape=[jax.ShapeDtypeStruct((T, D_MODEL), F32), jax.ShapeDtypeStruct((T, D_MODEL), BF16),
                   jax.ShapeDtypeStruct((1, D_MODEL), F32)],
        compiler_params=_params(dimension_semantics=("arbitrary",)),
    )(dgate, dup, w, dx3, x2, g2)


def _out_proj_bwd(dx2b, w_o, proj, y_pool, y_rnn, *, tm, tn, jobs=()):
    T = dx2b.shape[0]
    col_gp = (D_POOL + 2 * D_RNN) // tn
    col_gr = col_gp + D_MODEL // tn

    def body(dx_ref, w_ref, gp_ref, gr_ref, yp_ref, yr_ref, dgp_ref, dgr_ref, dyp_ref, dyr_ref):
        dmix = _dot_nt(dx_ref[...], w_ref[...])
        sp = _sigmoid(gp_ref[...])
        sr = _sigmoid(gr_ref[...])
        dgp_ref[...] = (dmix * yp_ref[...].astype(F32) * sp * (1.0 - sp)).astype(BF16)
        dgr_ref[...] = (dmix * yr_ref[...].astype(F32) * sr * (1.0 - sr)).astype(BF16)
        dyp_ref[...] = (dmix * sp).astype(BF16)
        dyr_ref[...] = (dmix * sr).astype(BF16)

    blk = pl.BlockSpec((tm, tn), lambda i, j: (i, j))
    out = jax.ShapeDtypeStruct((T, D_MODEL), BF16)
    return _pallas(
        body, (dx2b, w_o, proj, proj, y_pool, y_rnn), name="out_proj_bwd", grid=(T // tm, D_MODEL // tn),
        in_specs=[pl.BlockSpec((tm, D_MODEL), lambda i, j: (i, 0)), pl.BlockSpec((tn, D_MODEL), lambda i, j: (j, 0)),
                  pl.BlockSpec((tm, tn), lambda i, j: (i, col_gp + j)), pl.BlockSpec((tm, tn), lambda i, j: (i, col_gr + j)), blk, blk],
        out_specs=[blk, blk, blk, blk], out_shape=[out, out, out, out], semantics=("parallel", "parallel"), jobs=jobs)


def _branch_bwd(dyp, dyr, w_pool_out, w_rnn_out, *, tm):
    T = dyp.shape[0]

    def body(dyp_ref, dyr_ref, wp_ref, wr_ref, dpm_ref, dz_ref):
        dpm_ref[...] = _dot_nt(dyp_ref[...], wp_ref[...])
        dz_ref[...] = _dot_nt(dyr_ref[...], wr_ref[...])

    row = pl.BlockSpec((tm, D_MODEL), lambda i: (i, 0))
    return pl.pallas_call(
        body, name="branch_bwd", grid=(T // tm,),
        in_specs=[row, row, pl.BlockSpec((D_POOL, D_MODEL), lambda i: (0, 0)), pl.BlockSpec((D_RNN, D_MODEL), lambda i: (0, 0))],
        out_specs=[pl.BlockSpec((tm, D_POOL), lambda i: (i, 0)), pl.BlockSpec((tm, D_RNN), lambda i: (i, 0))],
        out_shape=[jax.ShapeDtypeStruct((T, D_POOL), F32), jax.ShapeDtypeStruct((T, D_RNN), F32)],
        compiler_params=_params(dimension_semantics=("parallel",)),
    )(dyp, dyr, w_pool_out, w_rnn_out)


def _in_proj_bwd(segs, w, dx2, x, g1, *, tm):
    T = x.shape[0]
    widths = [s.shape[1] for s in segs]
    offs = [sum(widths[:k]) for k in range(len(widths))]
    n = len(segs)

    def body(*refs):
        seg_refs, (w_ref, dx2_ref, x_ref, g_ref, dx_ref, dg_ref) = refs[:n], refs[n:]

        @pl.when(pl.program_id(0) == 0)
        def _():
            dg_ref[...] = jnp.zeros_like(dg_ref)

        dh = _dot_nt(seg_refs[0][...], w_ref[:, offs[0]:offs[0] + widths[0]])
        for k in range(1, n):
            dh += _dot_nt(seg_refs[k][...], w_ref[:, offs[k]:offs[k] + widths[k]])
        dxn, dgp = _rms_bwd(dh, x_ref[...], g_ref[...])
        dg_ref[...] += jnp.sum(dgp, axis=0, keepdims=True)
        dx_ref[...] = dx2_ref[...] + dxn

    row = pl.BlockSpec((tm, D_MODEL), lambda i: (i, 0))
    vec = pl.BlockSpec((1, D_MODEL), lambda i: (0, 0))
    return pl.pallas_call(
        body, name="in_proj_bwd", grid=(T // tm,),
        in_specs=[pl.BlockSpec((tm, wd), lambda i: (i, 0)) for wd in widths]
        + [pl.BlockSpec((D_MODEL, D_IN), lambda i: (0, 0)), row, row, vec],
        out_specs=[row, vec],
        out_shape=[jax.ShapeDtypeStruct((T, D_MODEL), F32), jax.ShapeDtypeStruct((1, D_MODEL), F32)],
        compiler_params=_params(dimension_semantics=("arbitrary",)),
    )(*segs, w, dx2, x, g1)


def _weight_grad(a, segs, *, tm, tn, name, jobs=None):
    T, M = a.shape
    nblk = [s.shape[1] // tn for s in segs]
    first = [sum(nblk[:k]) for k in range(len(segs))]
    n = len(segs)

    def body(a_ref, *refs):
        seg_refs, o_ref = refs[:n], refs[n]
        j = pl.program_id(1)
        for k in range(n):
            @pl.when((j >= first[k]) & (j < first[k] + nblk[k]))
            def _(k=k):
                o_ref[...] = _dot_tn(a_ref[...], seg_refs[k][...])

    def seg_spec(k):
        return pl.BlockSpec((T, tn), lambda i, j: (0, jnp.clip(j - first[k], 0, nblk[k] - 1)))

    (grad,), results = _pallas(
        body, (a, *segs), name=name, grid=(M // tm, sum(nblk)),
        in_specs=[pl.BlockSpec((T, tm), lambda i, j: (0, i))] + [seg_spec(k) for k in range(n)],
        out_specs=[pl.BlockSpec((tm, tn), lambda i, j: (i, j))],
        out_shape=[jax.ShapeDtypeStruct((M, sum(nblk) * tn), F32)],
        semantics=("parallel", "arbitrary"), jobs=jobs or ())
    return grad if jobs is None else (grad, results)


def _pad_front(dst, src, halo):
    dst[pl.ds(0, halo), :] = jnp.zeros((halo, src.shape[1]), F32)

    def fill(i, carry):
        r0 = pl.multiple_of(i * CHUNK, CHUNK)
        dst[pl.ds(r0 + halo, CHUNK), :] = src[pl.ds(r0, CHUNK), :]
        return carry

    lax.fori_loop(0, src.shape[0] // CHUNK, fill, 0)


def _shift_rows(v, k):
    return pltpu.roll(v, k % v.shape[0], axis=0)


def _window_sums(xs, direction):
    s2 = xs + _shift_rows(xs, direction)
    s4 = s2 + _shift_rows(s2, 2 * direction)
    s8 = s4 + _shift_rows(s4, 4 * direction)
    s16 = s8 + _shift_rows(s8, 8 * direction)
    return s2, s4, s8, s16


def _select_window(g, sums):
    s2, s4, s8, s16 = sums
    return jnp.where(g == 0, s2, jnp.where(g == 1, s4, jnp.where(g == 2, s8, s16)))


def _pool_count(g, start, rows):
    t = start + lax.broadcasted_iota(jnp.int32, (rows, 1), 0)
    return jnp.minimum(t + 1, jnp.left_shift(2, g)).astype(F32)


def _pool_fwd(proj, w_grp, scale):
    T = proj.shape[0]
    nchunk = T // CHUNK

    def body(u_ref, w_ref, s_ref, o_ref, upad):
        g = pl.program_id(0)
        _pad_front(upad, u_ref, POOL_HALO)
        w = w_ref[...].astype(BF16)
        scale_row = s_ref[...]

        def chunk(i, carry):
            r0 = pl.multiple_of(i * CHUNK, CHUNK)
            xs = upad[pl.ds(r0, CHUNK + POOL_HALO), :]
            win = _select_window(g, _window_sums(xs, 1))[POOL_HALO:]
            pooled = win / _pool_count(g, r0, CHUNK) - xs[POOL_HALO:]
            o_ref[pl.ds(r0, CHUNK), :] = (_dot(pooled.astype(BF16), w) * scale_row).astype(BF16)
            return carry

        lax.fori_loop(0, nchunk, chunk, 0)

    return pl.pallas_call(
        body, name="pool_fwd", grid=(N_POOL_GROUPS,),
        in_specs=[pl.BlockSpec((T, HEAD), lambda g: (0, g)), pl.BlockSpec((None, HEAD, HEAD), lambda g: (g, 0, 0)),
                  pl.BlockSpec((1, HEAD), lambda g: (0, g))],
        out_specs=pl.BlockSpec((T, HEAD), lambda g: (0, g)),
        out_shape=jax.ShapeDtypeStruct((T, D_POOL), BF16),
        scratch_shapes=[pltpu.VMEM((T + POOL_HALO, HEAD), F32)],
        compiler_params=_params(dimension_semantics=("parallel",)),
    )(proj, w_grp, scale)


def _pool_bwd(proj, dpm, w_grp, scale, jobs=()):
    T = proj.shape[0]
    nchunk = T // CHUNK

    def body(u_ref, dpm_ref, w_ref, s_ref, du_ref, dw_ref, ds_ref, upad, zpad, dpool):
        g = pl.program_id(0)
        _pad_front(upad, u_ref, POOL_HALO)
        zpad[pl.ds(T, POOL_HALO), :] = jnp.zeros((POOL_HALO, HEAD), F32)
        dw_ref[...] = jnp.zeros_like(dw_ref)
        ds_ref[...] = jnp.zeros_like(ds_ref)
        w = w_ref[...].astype(BF16)
        scale_row = s_ref[...]

        def chunk(i, carry):
            r0 = pl.multiple_of(i * CHUNK, CHUNK)
            xs = upad[pl.ds(r0, CHUNK + POOL_HALO), :]
            cnt = _pool_count(g, r0, CHUNK)
            pooled = (_select_window(g, _window_sums(xs, 1))[POOL_HALO:] / cnt - xs[POOL_HALO:]).astype(BF16)
            mixed = _dot(pooled, w)
            d = dpm_ref[pl.ds(r0, CHUNK), :]
            ds_ref[...] += jnp.sum(d * mixed, axis=0, keepdims=True)
            dmixed = (d * scale_row).astype(BF16)
            dw_ref[...] += _dot_tn(pooled, dmixed)
            dp = _dot_nt(dmixed, w)
            dpool[pl.ds(r0, CHUNK), :] = dp
            zpad[pl.ds(r0, CHUNK), :] = dp / cnt
            return carry

        lax.fori_loop(0, nchunk, chunk, 0)

        def chunk2(i, carry):
            r0 = pl.multiple_of(i * CHUNK, CHUNK)
            zs = zpad[pl.ds(r0, CHUNK + POOL_HALO), :]
            win = _select_window(g, _window_sums(zs, -1))[:CHUNK]
            du_ref[pl.ds(r0, CHUNK), :] = (win - dpool[pl.ds(r0, CHUNK), :]).astype(BF16)
            return carry

        lax.fori_loop(0, nchunk, chunk2, 0)

    col = pl.BlockSpec((T, HEAD), lambda g: (0, g))
    return _pallas(
        body, (proj, dpm, w_grp, scale), name="pool_bwd", grid=(N_POOL_GROUPS,),
        in_specs=[col, col, pl.BlockSpec((None, HEAD, HEAD), lambda g: (g, 0, 0)), pl.BlockSpec((1, HEAD), lambda g: (0, g))],
        out_specs=[col, pl.BlockSpec((None, HEAD, HEAD), lambda g: (g, 0, 0)), pl.BlockSpec((1, HEAD), lambda g: (0, g))],
        out_shape=[jax.ShapeDtypeStruct((T, D_POOL), BF16), jax.ShapeDtypeStruct((N_POOL_GROUPS, HEAD, HEAD), F32),
                   jax.ShapeDtypeStruct((1, D_POOL), F32)],
        scratch_shapes=[pltpu.VMEM((T + POOL_HALO, HEAD), F32), pltpu.VMEM((T + POOL_HALO, HEAD), F32), pltpu.VMEM((T, HEAD), F32)],
        semantics=("parallel",), jobs=jobs)


def _conv_taps(xs, cw):
    v = cw[CONV_WIDTH - 1] * xs[SUBLANES:]
    for k in range(CONV_WIDTH - 1):
        v += cw[k] * _shift_rows(xs, CONV_WIDTH - 1 - k)[SUBLANES:]
    return v


def _tap_rows(cw_ref):
    return [cw_ref[k:k + 1, :] for k in range(CONV_WIDTH)]


def _softplus_neg(lam):
    return jnp.maximum(-lam, 0.0) + _log1p(jnp.exp(-jnp.abs(lam)))


def _lru_gates(v, wa, ba, wx, bx, sp):
    vb = v.astype(BF16)
    ra = _sigmoid(_dot(vb, wa) + ba)
    ix = _sigmoid(_dot(vb, wx) + bx)
    log_a = -LRU_C * ra * sp
    a = jnp.exp(log_a)
    sq = jnp.sqrt(-jnp.tanh(log_a) * (a * a + 1.0))
    return ra, ix, a, sq


def _row_bcast(v, r):
    return jnp.broadcast_to(v[r:r + 1, :], v.shape)


def _tile_scans(a, b, direction):
    ri = lax.broadcasted_iota(jnp.int32, a.shape, 0) % SUBLANES
    A, B = a, b
    for s in (1, 2, 4):
        ok = (ri >= s) if direction == 1 else (ri + s < SUBLANES)
        As, Bs = _shift_rows(A, s * direction), _shift_rows(B, s * direction)
        B = jnp.where(ok, A * Bs + B, B)
        A = jnp.where(ok, A * As, A)
    return A, B


TILES_PER_STEP = 8


def _carry_tiles(A_s, B_s, out, ntile, direction):
    out_row = SUBLANES - 1 if direction == 1 else 0

    def step(k, carry):
        for j in range(TILES_PER_STEP):
            t = k * TILES_PER_STEP + j
            r0 = pl.multiple_of((t if direction == 1 else ntile - 1 - t) * SUBLANES, SUBLANES)
            A, B = A_s[pl.ds(r0, SUBLANES), :], B_s[pl.ds(r0, SUBLANES), :]
            out[pl.ds(r0, SUBLANES), :] = A * carry + B
            carry = _row_bcast(A, out_row) * carry + _row_bcast(B, out_row)
        return carry

    lax.fori_loop(0, ntile // TILES_PER_STEP, step, jnp.zeros((SUBLANES, HEAD), F32))


def _rnn_fwd(proj, conv_w, conv_b, w_a, b_a, w_x, b_x, lam, jobs=()):
    T = proj.shape[0]
    nchunk = T // CHUNK
    ntile = T // SUBLANES

    def body(u_ref, ug_ref, cw_ref, cb_ref, wa_ref, ba_ref, wx_ref, bx_ref, lam_ref, h_ref, z_ref, upad, a_s, b_s):
        _pad_front(upad, u_ref, SUBLANES)
        cw, cb = _tap_rows(cw_ref), cb_ref[...]
        wa, wx = wa_ref[...].astype(BF16), wx_ref[...].astype(BF16)
        ba, bx = ba_ref[...], bx_ref[...]
        sp = _softplus_neg(lam_ref[...])

        def chunk(i, carry):
            r0 = pl.multiple_of(i * CHUNK, CHUNK)
            v = _conv_taps(upad[pl.ds(r0, CHUNK + SUBLANES), :], cw) + cb
            _, ix, a, sq = _lru_gates(v, wa, ba, wx, bx, sp)
            a_s[pl.ds(r0, CHUNK), :], b_s[pl.ds(r0, CHUNK), :] = _tile_scans(a, sq * ix * v, 1)
            return carry

        lax.fori_loop(0, nchunk, chunk, 0)
        _carry_tiles(a_s, b_s, h_ref, ntile, 1)

        def chunk3(i, carry):
            r0 = pl.multiple_of(i * CHUNK, CHUNK)
            gl, _ = _gelu_parts(ug_ref[pl.ds(r0, CHUNK), :])
            z_ref[pl.ds(r0, CHUNK), :] = (h_ref[pl.ds(r0, CHUNK), :] * gl).astype(BF16)
            return carry

        lax.fori_loop(0, nchunk, chunk3, 0)

    col = pl.BlockSpec((T, HEAD), lambda h: (0, h))
    vec = pl.BlockSpec((1, HEAD), lambda h: (0, h))
    mat = pl.BlockSpec((None, HEAD, HEAD), lambda h: (h, 0, 0))
    return _pallas(
        body, (proj, proj, conv_w, conv_b, w_a, b_a, w_x, b_x, lam), name="rnn_fwd", grid=(N_RNN_HEADS,),
        in_specs=[pl.BlockSpec((T, HEAD), lambda h: (0, COL_RNN + h)), pl.BlockSpec((T, HEAD), lambda h: (0, COL_GATE + h)),
                  pl.BlockSpec((CONV_WIDTH, HEAD), lambda h: (0, h)), vec, mat, vec, mat, vec, vec],
        out_specs=[col, col],
        out_shape=[jax.ShapeDtypeStruct((T, D_RNN), F32), jax.ShapeDtypeStruct((T, D_RNN), BF16)],
        scratch_shapes=[pltpu.VMEM((T + SUBLANES, HEAD), F32), pltpu.VMEM((T, HEAD), F32), pltpu.VMEM((T, HEAD), F32)],
        semantics=("parallel",), jobs=jobs)


def _rnn_bwd(proj, hr, dz, conv_w, conv_b, w_a, b_a, w_x, b_x, lam, jobs=()):
    T = proj.shape[0]
    nchunk = T // CHUNK
    ntile = T // SUBLANES

    def body(u_ref, ug_ref, h_ref, dz_ref, cw_ref, cb_ref, wa_ref, ba_ref, wx_ref, bx_ref, lam_ref,
             du_ref, dug_ref, dwa_ref, dwx_ref, dba_ref, dbx_ref, dlam_ref, dcb_ref, dcw_ref,
             upad, hpad, apad, v_s, ra_s, ix_s, sq_s, g_s, dvpad, ga_s):
        zero_tile = jnp.zeros((SUBLANES, HEAD), F32)
        _pad_front(upad, u_ref, SUBLANES)
        _pad_front(hpad, h_ref, SUBLANES)
        apad[pl.ds(T, SUBLANES), :] = zero_tile
        dvpad[pl.ds(T, SUBLANES), :] = zero_tile
        for ref in (dwa_ref, dwx_ref, dba_ref, dbx_ref, dlam_ref, dcb_ref, dcw_ref):
            ref[...] = jnp.zeros_like(ref)
        cw, cb = _tap_rows(cw_ref), cb_ref[...]
        wa, wx = wa_ref[...].astype(BF16), wx_ref[...].astype(BF16)
        ba, bx = ba_ref[...], bx_ref[...]
        lam_row = lam_ref[...]
        sp = _softplus_neg(lam_row)

        def chunk(i, carry):
            r0 = pl.multiple_of(i * CHUNK, CHUNK)
            rows = pl.ds(r0, CHUNK)
            h = h_ref[rows, :]
            v = _conv_taps(upad[pl.ds(r0, CHUNK + SUBLANES), :], cw) + cb
            ra, ix, a, sq = _lru_gates(v, wa, ba, wx, bx, sp)
            v_s[rows, :], ra_s[rows, :], ix_s[rows, :], sq_s[rows, :], apad[rows, :] = v, ra, ix, sq, a
            gl, dgl = _gelu_parts(ug_ref[rows, :])
            d = dz_ref[rows, :]
            g_s[rows, :] = d * gl
            dug_ref[rows, :] = (d * h * dgl).astype(BF16)
            return carry

        lax.fori_loop(0, nchunk, chunk, 0)

        def chunk2(i, carry):
            r0 = pl.multiple_of(i * CHUNK, CHUNK)
            rows = pl.ds(r0, CHUNK)
            a_next = _shift_rows(apad[pl.ds(r0, CHUNK + SUBLANES), :], -1)[:CHUNK]
            ga_s[rows, :], g_s[rows, :] = _tile_scans(a_next, g_s[rows, :], -1)
            return carry

        lax.fori_loop(0, nchunk, chunk2, 0)
        _carry_tiles(ga_s, g_s, g_s, ntile, -1)

        def chunk3(i, carry):
            r0 = pl.multiple_of(i * CHUNK, CHUNK)
            rows = pl.ds(r0, CHUNK)
            g = g_s[rows, :]
            h_prev = _shift_rows(hpad[pl.ds(r0, CHUNK + SUBLANES), :], 1)[SUBLANES:]
            v, ra, ix, sq, a = v_s[rows, :], ra_s[rows, :], ix_s[rows, :], sq_s[rows, :], apad[rows, :]
            d_sq = g * ix * v
            d_ix = g * sq * v
            d_la = a * g * h_prev - d_sq * a * a / sq
            dlam_ref[...] += jnp.sum(d_la * ra, axis=0, keepdims=True)
            d_pa = d_la * (-LRU_C) * sp * ra * (1.0 - ra)
            d_px = d_ix * ix * (1.0 - ix)
            vb, d_pab, d_pxb = v.astype(BF16), d_pa.astype(BF16), d_px.astype(BF16)
            dwa_ref[...] += _dot_tn(vb, d_pab)
            dwx_ref[...] += _dot_tn(vb, d_pxb)
            dba_ref[...] += jnp.sum(d_pa, axis=0, keepdims=True)
            dbx_ref[...] += jnp.sum(d_px, axis=0, keepdims=True)
            dv = g * sq * ix + _dot_nt(d_pab, wa) + _dot_nt(d_pxb, wx)
            dvpad[rows, :] = dv
            dcb_ref[...] += jnp.sum(dv, axis=0, keepdims=True)
            xs = upad[pl.ds(r0, CHUNK + SUBLANES), :]
            for k in range(CONV_WIDTH):
                u_k = _shift_rows(xs, CONV_WIDTH - 1 - k)[SUBLANES:] if k < CONV_WIDTH - 1 else xs[SUBLANES:]
                dcw_ref[k:k + 1, :] += jnp.sum(dv * u_k, axis=0, keepdims=True)
            return carry

        lax.fori_loop(0, nchunk, chunk3, 0)
        dlam_ref[...] = dlam_ref[...] * (LRU_C * _sigmoid(-lam_row))

        def chunk4(i, carry):
            r0 = pl.multiple_of(i * CHUNK, CHUNK)
            dvs = dvpad[pl.ds(r0, CHUNK + SUBLANES), :]
            du = cw[CONV_WIDTH - 1] * dvs[:CHUNK]
            for k in range(CONV_WIDTH - 1):
                du += cw[k] * _shift_rows(dvs, -(CONV_WIDTH - 1 - k))[:CHUNK]
            du_ref[pl.ds(r0, CHUNK), :] = du.astype(BF16)
            return carry

        lax.fori_loop(0, nchunk, chunk4, 0)

    col = pl.BlockSpec((T, HEAD), lambda h: (0, h))
    vec = pl.BlockSpec((1, HEAD), lambda h: (0, h))
    mat = pl.BlockSpec((None, HEAD, HEAD), lambda h: (h, 0, 0))
    taps = pl.BlockSpec((CONV_WIDTH, HEAD), lambda h: (0, h))
    vec_out = jax.ShapeDtypeStruct((1, D_RNN), F32)
    mat_out = jax.ShapeDtypeStruct((N_RNN_HEADS, HEAD, HEAD), F32)
    seq = pltpu.VMEM((T, HEAD), F32)
    seq_pad = pltpu.VMEM((T + SUBLANES, HEAD), F32)
    return _pallas(
        body, (proj, proj, hr, dz, conv_w, conv_b, w_a, b_a, w_x, b_x, lam), name="rnn_bwd", grid=(N_RNN_HEADS,),
        in_specs=[pl.BlockSpec((T, HEAD), lambda h: (0, COL_RNN + h)), pl.BlockSpec((T, HEAD), lambda h: (0, COL_GATE + h)),
                  col, col, taps, vec, mat, vec, mat, vec, vec],
        out_specs=[col, col, mat, mat, vec, vec, vec, vec, taps],
        out_shape=[jax.ShapeDtypeStruct((T, D_RNN), BF16), jax.ShapeDtypeStruct((T, D_RNN), BF16), mat_out, mat_out,
                   vec_out, vec_out, vec_out, vec_out, jax.ShapeDtypeStruct((CONV_WIDTH, D_RNN), F32)],
        scratch_shapes=[seq_pad, seq_pad, seq_pad, seq, seq, seq, seq, seq, seq_pad, seq],
        semantics=("parallel",), jobs=jobs)


GROUP_FFN = ["w_ffn_out", "w_ffn_in"]
GROUP_MIX = ["w_o", "w_pool_out", "w_rnn_out"]
GROUP_IN = ["w_in"]


def _step(x, target, s, full, conv_w_mine, place):
    T = x.shape[0]
    tall, mid, low = min(T, 2048), min(T, 1024), min(T, 512)
    c1 = place[1:]
    full = dict(full)

    def gathered(names, results):
        full.update(zip(names, results))

    (full["w_in"], conv_w), = _run_jobs([_gather_job(full, ["w_in"], conv_w_mine)], "gather_w_in")
    early = ["w_pool_out", "w_rnn_out", "w_o", "w_ffn_out"]
    (proj, h1), (res,) = _norm_matmul(x, s["norm_mix"], full["w_in"], tm=tall, tn=512, name="in_proj", jobs=[_gather_job(full, early)])
    gathered(early, res)
    pm = _pool_fwd(proj, s["w_pool_grp"], s["pool_scale"])
    (hr, z), (res,) = _rnn_fwd(proj, conv_w, s["conv_b"], s["w_rg_a"], s["b_rg_a"], s["w_rg_x"], s["b_rg_x"], s["lru_lambda"],
                               jobs=[_gather_job(full, ["w_ffn_in"])])
    gathered(["w_ffn_in"], res)
    y_pool, y_rnn, mix = _branch_mix(pm, z, full["w_pool_out"], full["w_rnn_out"], proj, tm=tall, tn=256)
    x2 = _out_proj_residual(mix, full["w_o"], x, tm=mid)
    gate, up, act, h2 = _ffn_in(x2, s["norm_ffn"], full["w_ffn_in"], tm=tall, tn=256)
    dx3, dx3b, sq_cols, g_norm_final = _ffn_out_loss(act, full["w_ffn_out"], x2, s["norm_final"], target, tm=low)

    g = {"norm_final": g_norm_final}

    def chip_sums(names, from_sibling):
        sums = {name: _chip_sum(name, g[name], got, c1) for name, got in zip(names, from_sibling)}
        return {name: v[0] for name, v in sums.items()}, {name: v[1] for name, v in sums.items()}

    def final_sums(names, sums, from_chips):
        return {name: _final_sum(name, sums[name], got, place) for name, got in zip(names, from_chips)}

    dgate, dup = _ffn_out_bwd(dx3b, full["w_ffn_out"], gate, up, tm=tall, tn=256)
    g["w_ffn_out"] = _weight_grad(act, [dx3b], tm=256, tn=D_MODEL, name="w_ffn_out_grad")
    dx2, dx2b, g["norm_ffn"] = _ffn_in_bwd(dgate, dup, full["w_ffn_in"], dx3, x2, s["norm_ffn"], tm=low)
    g["w_ffn_in"] = _weight_grad(h2, [dgate, dup], tm=D_MODEL, tn=256, name="w_ffn_in_grad")
    (dgp, dgr, dyp, dyr), (res,) = _out_proj_bwd(dx2b, full["w_o"], proj, y_pool, y_rnn, tm=tall, tn=256,
                                                 jobs=[_sibling_job(g, GROUP_FFN)])
    sums_ffn, sums_ffn_bf16 = chip_sums(GROUP_FFN, res)
    g["w_o"] = _weight_grad(mix, [dx2b], tm=D_MODEL, tn=256, name="w_o_grad")
    dpm, dz = _branch_bwd(dyp, dyr, full["w_pool_out"], full["w_rnn_out"], tm=mid)
    g["w_pool_out"] = _weight_grad(pm, [dyp], tm=D_POOL, tn=256, name="w_pool_out_grad")
    g["w_rnn_out"] = _weight_grad(z, [dyr], tm=D_RNN, tn=256, name="w_rnn_out_grad")
    (dupool, g["w_pool_grp"], g["pool_scale"]), (res,) = _pool_bwd(proj, dpm, s["w_pool_grp"], s["pool_scale"],
                                                                   jobs=[_sibling_job(g, GROUP_MIX)])
    sums_mix, sums_mix_bf16 = chip_sums(GROUP_MIX, res)
    ((durnn, dugate, g["w_rg_a"], g["w_rg_x"], g["b_rg_a"], g["b_rg_x"], g["lru_lambda"], g["conv_b"], g["conv_w"]),
     (res,)) = _rnn_bwd(proj, hr, dz, conv_w, s["conv_b"], s["w_rg_a"], s["b_rg_a"], s["w_rg_x"], s["b_rg_x"], s["lru_lambda"],
                        jobs=[_chips_job(sums_ffn_bf16, GROUP_FFN)])
    shards = final_sums(GROUP_FFN, sums_ffn, res)
    segs = [dupool, durnn, dugate, dgp, dgr]
    grad_x, g["norm_mix"] = _in_proj_bwd(segs, full["w_in"], dx2, x, s["norm_mix"], tm=low)
    g["w_in"], (res, joined) = _weight_grad(h1, segs, tm=D_MODEL, tn=256, name="w_in_grad",
                                           jobs=[_chips_job(sums_mix_bf16, GROUP_MIX), _join_job(shards, GROUP_FFN)])
    grads = dict(zip(GROUP_FFN, joined))
    shards = final_sums(GROUP_MIX, sums_mix, res)
    (res,) = _run_jobs([_sibling_job(g, GROUP_IN)], "w_in_exchange_sibling")
    sums_in, sums_in_bf16 = chip_sums(GROUP_IN, res)

    vec_rows = [g[name] if name != "pool_scale" else jnp.pad(g[name], ((0, 0), (0, D_MODEL - D_POOL))) for name in VEC_ITEMS]
    vec_rows += [g["conv_w"], sq_cols, jnp.zeros((VEC_ROWS - len(VEC_ITEMS) - CONV_WIDTH - 1, D_MODEL), F32)]
    vec = jnp.concatenate(vec_rows, axis=0).reshape(VEC_ROWS, N_DEV, HEAD).transpose(1, 0, 2)
    mat = jnp.concatenate([g[name].reshape(-1, HEAD) for name in MAT_ITEMS], axis=0).reshape(N_DEV, -1, HEAD)
    (vec, mat), (res, joined) = _all_reduce_small([vec, mat], jobs=[_chips_job(sums_in_bf16, GROUP_IN), _join_job(shards, GROUP_MIX)])
    grads.update(zip(GROUP_MIX, joined))
    (joined,) = _run_jobs([_join_job(final_sums(GROUP_IN, sums_in, res), GROUP_IN)], "w_in_join_halves")
    grads.update(zip(GROUP_IN, joined))

    vec = vec.transpose(1, 0, 2).reshape(VEC_ROWS, D_MODEL)
    mat = mat.reshape(-1, HEAD)
    for k, name in enumerate(VEC_ITEMS):
        grads[name] = vec[k:k + 1, :s[name].shape[1]]
    grads["conv_w"] = vec[len(VEC_ITEMS):len(VEC_ITEMS) + CONV_WIDTH]
    row = 0
    for name in MAT_ITEMS:
        rows = s[name].shape[0] * HEAD
        grads[name] = mat[row:row + rows]
        row += rows
    return vec[len(VEC_ITEMS) + CONV_WIDTH], grad_x, grads


LARGE = {"w_in": "col", "w_pool_out": "col", "w_rnn_out": "row", "w_o": "row", "w_ffn_in": "col", "w_ffn_out": "row"}
LARGE_SHAPE = {"w_in": (D_MODEL, D_IN), "w_pool_out": (D_POOL, D_MODEL), "w_rnn_out": (D_RNN, D_MODEL),
               "w_o": (D_MODEL, D_MODEL), "w_ffn_in": (D_MODEL, 2 * D_FF), "w_ffn_out": (D_FF, D_MODEL)}


def _place():
    x, y, c = lax.axis_index("x"), lax.axis_index("y"), lax.axis_index("c")
    return 2 * x + y, c


def _chip_device(chip, c):
    return (chip // 2, chip % 2, c)


def _chip_window(ref, kind, shape, chip, half=None):
    K, N = shape
    if kind == "col":
        rows = slice(None) if half is None else pl.ds(half * (K // 2), K // 2)
        return ref.at[rows, pl.ds(chip * (N // N_CHIPS), N // N_CHIPS)]
    ks = K // N_CHIPS
    if half is None:
        return ref.at[pl.ds(chip * ks, ks), :]
    return ref.at[pl.ds(chip * ks + half * (ks // 2), ks // 2), :]


def _row_half(ref, half):
    rows = ref.shape[0] // 2
    return ref.at[pl.ds(half * rows, rows), :]


def _remote(win_src, win_dst, send_sems, recv_sems, idx, to):
    return pltpu.make_async_remote_copy(src_ref=win_src, dst_ref=win_dst, send_sem=send_sems.at[idx], recv_sem=recv_sems.at[idx],
                                        device_id=to, device_id_type=MESH)


def _gather_job(full, names, conv_w_full=None):
    n = len(names)
    cw_cols = D_RNN // N_CHIPS

    def windows(refs, chip, half):
        return [_chip_window(refs[k], LARGE[name], LARGE_SHAPE[name], chip, half) for k, name in enumerate(names)]

    def ici_copies(refs, send_sems, recv_sems, src_chip, dst_chip, c, r):
        wins = windows(refs, src_chip, c)
        if conv_w_full is not None:
            wins.append(refs[n].at[:, pl.ds(src_chip * cw_cols, cw_cols)])
        return [_remote(win, win, send_sems, recv_sems, (k, r), _chip_device(dst_chip, c)) for k, win in enumerate(wins)]

    def forwards(refs, send_sems, recv_sems, src_chip, half, to_core, chip, r):
        return [_remote(win, win, send_sems, recv_sems, (k, 3 + r), _chip_device(chip, to_core))
                for k, win in enumerate(windows(refs, src_chip, half))]

    def start(ins, outs, send_sems, recv_sems):
        chip, c = _place()
        for r in range(3):
            for cp in ici_copies(outs, send_sems, recv_sems, chip, chip ^ (r + 1), c, r):
                cp.start()

    def finish(ins, outs, send_sems, recv_sems):
        chip, c = _place()
        for r in range(3):
            for cp in ici_copies(outs, send_sems, recv_sems, chip ^ (r + 1), chip, c, r):
                cp.wait_recv()
            for cp in forwards(outs, send_sems, recv_sems, chip ^ (r + 1), c, 1 - c, chip, r):
                cp.start()
        for r in range(3):
            for cp in forwards(outs, send_sems, recv_sems, chip ^ (r + 1), 1 - c, c, chip, r):
                cp.wait_recv()
            for cp in ici_copies(outs, send_sems, recv_sems, chip, chip ^ (r + 1), c, r):
                cp.wait_send()
            for cp in forwards(outs, send_sems, recv_sems, chip ^ (r + 1), c, 1 - c, chip, r):
                cp.wait_send()

    arrays = [full[name] for name in names] + ([conv_w_full] if conv_w_full is not None else [])
    return _Job(arrays, [jax.ShapeDtypeStruct(a.shape, a.dtype) for a in arrays], {k: k for k in range(len(arrays))},
                (len(arrays), 6), start, finish)


def _core_halves(ref, kind, shape, c):
    return [_chip_window(ref, kind, shape, chip, c) for chip in range(N_CHIPS)]


def _sibling_job(grads, names):
    def start(ins, outs, send_sems, recv_sems):
        chip, c = _place()
        for k, name in enumerate(names):
            kind, shape = LARGE[name], LARGE_SHAPE[name]
            if kind == "col":
                pairs = [(_row_half(ins[k], 1 - c), outs[k])]
            else:
                rows = shape[0] // N_DEV
                pairs = [(win, outs[k].at[pl.ds(j * rows, rows), :]) for j, win in enumerate(_core_halves(ins[k], kind, shape, 1 - c))]
            for src, dst in pairs:
                _remote(src, dst, send_sems, recv_sems, k, _chip_device(chip, 1 - c)).start()

    def finish(ins, outs, send_sems, recv_sems):
        chip, c = _place()
        for k in range(len(names)):
            _remote(outs[k], outs[k], send_sems, recv_sems, k, _chip_device(chip, 1 - c)).wait()

    return _Job([grads[name] for name in names],
                [jax.ShapeDtypeStruct((LARGE_SHAPE[name][0] // 2, LARGE_SHAPE[name][1]), F32) for name in names], {},
                (len(names),), start, finish)


def _chip_sum(name, g, got, c):
    kind, (K, N) = LARGE[name], LARGE_SHAPE[name]
    rows = K // N_DEV

    def body(c_ref, g_ref, got_ref, o_ref, ob_ref):
        total = g_ref[...] + got_ref[...]
        o_ref[...] = total
        ob_ref[...] = total.astype(BF16)

    if kind == "col":
        mine = pl.BlockSpec((rows, N), lambda j, c_ref: (j + N_CHIPS * c_ref[0], 0))
    else:
        mine = pl.BlockSpec((rows, N), lambda j, c_ref: (2 * j + c_ref[0], 0))
    blk = pl.BlockSpec((rows, N), lambda j, c_ref: (j, 0))
    return pl.pallas_call(
        body, name=name + "_chip_sum",
        grid_spec=pltpu.PrefetchScalarGridSpec(num_scalar_prefetch=1, grid=(N_CHIPS,), in_specs=[mine, blk], out_specs=[blk, blk]),
        out_shape=[jax.ShapeDtypeStruct((K // 2, N), F32), jax.ShapeDtypeStruct((K // 2, N), BF16)],
        compiler_params=_params(dimension_semantics=("parallel",)),
    )(c, g, got)


def _piece(ref, kind, shape, chip):
    K, N = shape
    if kind == "col":
        return ref.at[:, pl.ds(chip * (N // N_CHIPS), N // N_CHIPS)]
    return ref.at[pl.ds(chip * (K // N_DEV), K // N_DEV), :]


def _piece_shape(name):
    kind, (K, N) = LARGE[name], LARGE_SHAPE[name]
    return (K // 2, N // N_CHIPS) if kind == "col" else (K // N_DEV, N)


def _chips_job(sums, names):
    def copies(ins, outs, send_sems, recv_sems):
        chip, c = _place()
        return [_remote(_piece(ins[k], LARGE[name], LARGE_SHAPE[name], chip ^ (r + 1)), outs[k].at[r], send_sems, recv_sems, (k, r),
                        _chip_device(chip ^ (r + 1), c)) for k, name in enumerate(names) for r in range(3)]

    def start(*refs):
        for cp in copies(*refs):
            cp.start()

    def finish(*refs):
        for cp in copies(*refs):
            cp.wait()

    return _Job([sums[name] for name in names], [jax.ShapeDtypeStruct((3,) + _piece_shape(name), BF16) for name in names], {},
                (len(names), 3), start, finish)


def _final_sum(name, chip_sum, got, place):
    kind = LARGE[name]
    rows, cols = _piece_shape(name)

    def body(place_ref, s_ref, got_ref, o_ref):
        o_ref[...] = ((s_ref[...] + got_ref[0].astype(F32)) + got_ref[1].astype(F32)) + got_ref[2].astype(F32)

    if kind == "col":
        mine = pl.BlockSpec((rows, cols), lambda i, place_ref: (0, place_ref[0]))
    else:
        mine = pl.BlockSpec((rows, cols), lambda i, place_ref: (place_ref[0], 0))
    return pl.pallas_call(
        body, name=name + "_final_sum",
        grid_spec=pltpu.PrefetchScalarGridSpec(
            num_scalar_prefetch=1, grid=(1,), in_specs=[mine, pl.BlockSpec((3, rows, cols), lambda i, place_ref: (0, 0, 0))],
            out_specs=pl.BlockSpec((rows, cols), lambda i, place_ref: (place_ref[1], 0))),
        out_shape=jax.ShapeDtypeStruct((2 * rows, cols), F32),
        compiler_params=_params(dimension_semantics=("arbitrary",)),
    )(place, chip_sum, got)


def _join_job(shards, names):
    def half_copy(outs, send_sems, recv_sems, k, mine):
        chip, c = _place()
        win = _row_half(outs[k], c if mine else 1 - c)
        return _remote(win, win, send_sems, recv_sems, k, _chip_device(chip, 1 - c))

    def start(ins, outs, send_sems, recv_sems):
        for k in range(len(names)):
            half_copy(outs, send_sems, recv_sems, k, True).start()

    def finish(ins, outs, send_sems, recv_sems):
        for k in range(len(names)):
            half_copy(outs, send_sems, recv_sems, k, True).wait_send()
            half_copy(outs, send_sems, recv_sems, k, False).wait_recv()

    arrays = [shards[name] for name in names]
    return _Job(arrays, [jax.ShapeDtypeStruct(a.shape, F32) for a in arrays], {k: k for k in range(len(arrays))},
                (len(arrays),), start, finish)


VEC_ROWS = 16


def _all_reduce_small(slabs, jobs=()):
    n = len(slabs)

    def body(*refs):
        in_refs, out_refs, got_refs = refs[:n], refs[n:2 * n], refs[2 * n:3 * n]
        send_sems, recv_sems = refs[3 * n:]
        x, y, c = lax.axis_index("x"), lax.axis_index("y"), lax.axis_index("c")
        me = 4 * x + 2 * y + c

        def remote(src, dst, k, phase, r):
            other = me ^ r
            return pltpu.make_async_remote_copy(src_ref=src, dst_ref=dst, send_sem=send_sems.at[k, phase, r],
                                                recv_sem=recv_sems.at[k, phase, r],
                                                device_id=(other // 4, (other // 2) % 2, other % 2), device_id_type=MESH)

        scatter = [remote(in_refs[k].at[me ^ r], got_refs[k].at[r], k, 0, r) for r in range(1, N_DEV) for k in range(n)]
        for cp in scatter:
            cp.start()
        for cp in scatter:
            cp.wait()
        for k in range(n):
            total = in_refs[k][me]
            for r in range(1, N_DEV):
                total = total + got_refs[k][r]
            out_refs[k][me] = total
        gather = [remote(out_refs[k].at[me], out_refs[k].at[me], k, 1, r) for r in range(1, N_DEV) for k in range(n)]
        for cp in gather:
            cp.start()
        for r in range(1, N_DEV):
            for k in range(n):
                remote(out_refs[k].at[me ^ r], out_refs[k].at[me ^ r], k, 1, r).wait_recv()
        for cp in gather:
            cp.wait_send()

    return _pallas(
        body, slabs, name="all_reduce_small", grid=(), in_specs=[VMEM] * n, out_specs=[VMEM] * n,
        out_shape=[jax.ShapeDtypeStruct(s.shape, F32) for s in slabs],
        scratch_shapes=[pltpu.VMEM(s.shape, F32) for s in slabs]
        + [pltpu.SemaphoreType.DMA((n, 2, N_DEV)), pltpu.SemaphoreType.DMA((n, 2, N_DEV))], jobs=jobs)


def _cast_into_whole(w, name, place):
    rows, cols = w.shape
    tr = rows // 2

    def body(place_ref, w_ref, o_ref):
        o_ref[...] = w_ref[...].astype(BF16)

    if LARGE[name] == "col":
        window = pl.BlockSpec((tr, cols), lambda i, place_ref: (i, place_ref[0]))
    else:
        window = pl.BlockSpec((tr, cols), lambda i, place_ref: (2 * place_ref[0] + i, 0))
    return pl.pallas_call(
        body, name=name + "_cast",
        grid_spec=pltpu.PrefetchScalarGridSpec(num_scalar_prefetch=1, grid=(2,),
                                               in_specs=[pl.BlockSpec((tr, cols), lambda i, place_ref: (i, 0))], out_specs=window),
        out_shape=jax.ShapeDtypeStruct(LARGE_SHAPE[name], BF16),
        compiler_params=_params(dimension_semantics=("parallel",)))(place, w)


def _adamw_math(w, g, m, v):
    m = ADAM_B1 * m + (1.0 - ADAM_B1) * g
    v = ADAM_B2 * v + (1.0 - ADAM_B2) * (g * g)
    m_hat = m / (1.0 - ADAM_B1 ** ADAM_STEP)
    v_hat = v / (1.0 - ADAM_B2 ** ADAM_STEP)
    delta = -ADAM_LR * (m_hat / (jnp.sqrt(v_hat) + ADAM_EPS) + ADAM_WD * w)
    return delta, m, v


def _adamw_large(w, g, m, v, name):
    rows, cols = w.shape
    tr = rows // 4

    def body(w_ref, g_ref, m_ref, v_ref, d_ref, mo_ref, vo_ref):
        d_ref[...], mo_ref[...], vo_ref[...] = _adamw_math(w_ref[...], g_ref[...], m_ref[...], v_ref[...])

    blk = pl.BlockSpec((tr, cols), lambda i: (i, 0))
    out = jax.ShapeDtypeStruct(w.shape, F32)
    return pl.pallas_call(body, name=name + "_adamw", grid=(4,), in_specs=[blk] * 4, out_specs=[blk] * 3, out_shape=[out] * 3,
                          compiler_params=_params(dimension_semantics=("parallel",)))(w, g, m, v)


def _adamw_small(ws, gs, ms, vs):
    n = len(ws)

    def body(*refs):
        for k in range(n):
            w_ref, g_ref, m_ref, v_ref = (refs[q * n + k] for q in range(4))
            d_ref, mo_ref, vo_ref = (refs[(4 + q) * n + k] for q in range(3))
            d_ref[...], mo_ref[...], vo_ref[...] = _adamw_math(w_ref[...], g_ref[...], m_ref[...], v_ref[...])

    out = [jax.ShapeDtypeStruct(w.shape, F32) for w in ws]
    res = pl.pallas_call(body, name="small_adamw", in_specs=[VMEM] * (4 * n), out_specs=[VMEM] * (3 * n), out_shape=out * 3,
                         compiler_params=_params())(*ws, *gs, *ms, *vs)
    return res[:n], res[n:2 * n], res[2 * n:]


WEIGHTS = ["norm_mix", "w_in", "w_pool_grp", "pool_scale", "w_pool_out", "conv_w", "conv_b", "w_rg_a", "b_rg_a", "w_rg_x",
           "b_rg_x", "lru_lambda", "w_rnn_out", "w_o", "norm_ffn", "w_ffn_in", "w_ffn_out", "norm_final"]
VEC_ITEMS = ["norm_mix", "norm_ffn", "norm_final", "pool_scale", "conv_b", "lru_lambda", "b_rg_a", "b_rg_x"]
MAT_ITEMS = ["w_pool_grp", "w_rg_a", "w_rg_x"]


def _as2d(name, a):
    if name in MAT_ITEMS:
        return a.reshape(-1, HEAD, HEAD)
    if name == "conv_w":
        return a.reshape(CONV_WIDTH, -1)
    return a.reshape(1, -1)


def kernel(x, norm_mix, w_in, w_pool_grp, pool_scale, w_pool_out, conv_w, conv_b, w_rg_a, b_rg_a, w_rg_x, b_rg_x, lru_lambda, w_rnn_out, w_o, norm_ffn, w_ffn_in, w_ffn_out, norm_final, loss_target, m_norm_mix, m_w_in, m_w_pool_grp, m_pool_scale, m_w_pool_out, m_conv_w, m_conv_b, m_w_rg_a, m_b_rg_a, m_w_rg_x, m_b_rg_x, m_lru_lambda, m_w_rnn_out, m_w_o, m_norm_ffn, m_w_ffn_in, m_w_ffn_out, m_norm_final, v_norm_mix, v_w_in, v_w_pool_grp, v_pool_scale, v_w_pool_out, v_conv_w, v_conv_b, v_w_rg_a, v_b_rg_a, v_w_rg_x, v_b_rg_x, v_lru_lambda, v_w_rnn_out, v_w_o, v_norm_ffn, v_w_ffn_in, v_w_ffn_out, v_norm_final):
    given = dict(locals())
    w = {name: given[name] for name in WEIGHTS}
    m = {name: given["m_" + name] for name in WEIGHTS}
    v = {name: given["v_" + name] for name in WEIGHTS}
    chip, c = _place()

    place = jnp.stack([chip, c]).astype(jnp.int32)
    conv_cols = w["conv_w"].shape[-1]
    conv_w_mine = lax.dynamic_update_slice_in_dim(jnp.zeros((CONV_WIDTH, D_RNN), F32), w["conv_w"][0], chip * conv_cols, axis=1)
    full = {name: _cast_into_whole(w[name][0], name, place) for name in LARGE}
    small = {name: _as2d(name, w[name]) for name in WEIGHTS if name not in LARGE and name != "conv_w"}
    sq_cols, grad_x, grads = _step(x[0], loss_target[0], small, full, conv_w_mine, place)
    loss = 0.5 / D_MODEL * jnp.sum(sq_cols)
    grads["conv_w"] = lax.dynamic_slice_in_dim(grads["conv_w"], chip * conv_cols, conv_cols, axis=1)

    delta, new_m, new_v = {}, {}, {}
    for name in LARGE:
        delta[name], new_m[name], new_v[name] = _adamw_large(w[name][0], grads[name], m[name][0], v[name][0], name)
    small_names = [name for name in WEIGHTS if name not in LARGE]
    flat = lambda d: [d[name].reshape(grads[name].shape) for name in small_names]
    ds, mo, vo = _adamw_small(flat(w), [grads[name] for name in small_names], flat(m), flat(v))
    for k, name in enumerate(small_names):
        delta[name], new_m[name], new_v[name] = ds[k], mo[k], vo[k]

    shaped = lambda d: [d[name].reshape(w[name].shape) for name in WEIGHTS]
    return (loss, grad_x[None], *shaped(grads), *shaped(delta), *shaped(new_m), *shaped(new_v))
```

```python
import functools
import math

import jax
import jax.numpy as jnp
from jax import lax
from jax.experimental import pallas as pl
from jax.experimental.pallas import tpu as pltpu

F32 = jnp.float32
BF16 = jnp.bfloat16

D_MODEL = 1024
D_POOL = 512
N_POOL_GROUPS = 4
D_RNN = 1024
N_RNN_HEADS = 8
HEAD = 128
CONV_WIDTH = 4
LRU_C = 8.0
D_FF = 2816
D_IN = D_POOL + 2 * D_RNN + 2 * D_MODEL
NORM_EPS = 1e-6
COL_RNN = D_POOL // HEAD
COL_GATE = (D_POOL + D_RNN) // HEAD

ADAM_LR = 0.001
ADAM_B1 = 0.9
ADAM_B2 = 0.999
ADAM_EPS = 1e-08
ADAM_WD = 0.01
ADAM_STEP = 10

N_CHIPS = 4
N_DEV = 8
MESH = pl.DeviceIdType.MESH
ANY = pl.BlockSpec(memory_space=pl.ANY)
VMEM = pl.BlockSpec(memory_space=pltpu.VMEM)
VMEM_LIMIT_BYTES = 60 * 1024 * 1024
SUBLANES = 8
POOL_HALO = 16
CHUNK = 1024

GELU_C = math.sqrt(2.0 / math.pi)
GELU_A = 0.044715


def _params(**kw):
    return pltpu.CompilerParams(vmem_limit_bytes=VMEM_LIMIT_BYTES, **kw)


def _sigmoid(x):
    return 0.5 * jnp.tanh(0.5 * x) + 0.5


def _log1p(y):
    u = 1.0 + y
    d = u - 1.0
    return jnp.where(d == 0.0, y, jnp.log(u) * (y / jnp.where(d == 0.0, 1.0, d)))


def _gelu_parts(x):
    x2 = x * x
    th = jnp.tanh(GELU_C * (x + GELU_A * x * x2))
    g = 0.5 * x * (1.0 + th)
    dg = 0.5 * (1.0 + th) + 0.5 * x * (1.0 - th * th) * GELU_C * (1.0 + 3.0 * GELU_A * x2)
    return g, dg


def _dot(a, b):
    return jnp.dot(a, b, preferred_element_type=F32)


def _dot_nt(a, b):
    return lax.dot_general(a, b, (((1,), (1,)), ((), ())), preferred_element_type=F32)


def _dot_tn(a, b):
    return lax.dot_general(a, b, (((0,), (0,)), ((), ())), preferred_element_type=F32)


def _rms_scale(xv):
    return lax.rsqrt(jnp.mean(xv * xv, axis=-1, keepdims=True) + NORM_EPS)


def _rms_bwd(dy, xv, g):
    r = _rms_scale(xv)
    xh = xv * r
    dyg = dy * g
    dx = r * (dyg - xh * jnp.mean(dyg * xh, axis=-1, keepdims=True))
    return dx, dy * xh


class _Job:
    def __init__(self, inputs, out_shapes, aliases, sem_shape, start, finish):
        self.inputs, self.out_shapes, self.aliases, self.sem_shape = list(inputs), list(out_shapes), dict(aliases), sem_shape
        self.start, self.finish = start, finish


def _pallas(body, operands, *, name, grid, in_specs, out_specs, out_shape, scratch_shapes=(), semantics=None, jobs=()):
    n_in, n_out, n_scr = len(in_specs), len(out_specs), len(scratch_shapes)
    job_in = [a for job in jobs for a in job.inputs]
    job_out = [s for job in jobs for s in job.out_shapes]
    aliases, i0, o0 = {}, n_in, n_out
    for job in jobs:
        aliases.update({i0 + i: o0 + o for i, o in job.aliases.items()})
        i0, o0 = i0 + len(job.inputs), o0 + len(job.out_shapes)

    def whole(*refs):
        ins, j_ins = refs[:n_in], refs[n_in:n_in + len(job_in)]
        outs = refs[n_in + len(job_in):][:n_out]
        j_outs = refs[n_in + len(job_in) + n_out:][:len(job_out)]
        rest = refs[n_in + len(job_in) + n_out + len(job_out):]
        scr, sems = rest[:n_scr], rest[n_scr:]

        def run(phase):
            i, o = 0, 0
            for k, job in enumerate(jobs):
                getattr(job, phase)(j_ins[i:i + len(job.inputs)], j_outs[o:o + len(job.out_shapes)], sems[2 * k], sems[2 * k + 1])
                i, o = i + len(job.inputs), o + len(job.out_shapes)

        def at(step_of, phase):
            if not jobs:
                return
            if not grid:
                run(phase)
                return
            cond = functools.reduce(jnp.logical_and, [pl.program_id(d) == step_of(d) for d in range(len(grid))])
            pl.when(cond)(functools.partial(run, phase))

        at(lambda d: 0, "start")
        body(*ins, *outs, *scr)
        at(lambda d: grid[d] - 1, "finish")

    res = pl.pallas_call(
        whole, name=name, grid=grid, in_specs=list(in_specs) + [ANY] * len(job_in), out_specs=list(out_specs) + [ANY] * len(job_out),
        out_shape=list(out_shape) + job_out, input_output_aliases=aliases,
        scratch_shapes=list(scratch_shapes) + [pltpu.SemaphoreType.DMA(job.sem_shape) for job in jobs for _ in range(2)],
        compiler_params=_params(dimension_semantics=semantics, has_side_effects=bool(jobs)),
    )(*operands, *job_in)
    per_job, o = [], n_out
    for job in jobs:
        per_job.append(res[o:o + len(job.out_shapes)])
        o += len(job.out_shapes)
    return res[:n_out], per_job


def _run_jobs(jobs, name):
    return _pallas(lambda: None, [], name=name, grid=(), in_specs=[], out_specs=[], out_shape=[], jobs=jobs)[1]


NORM_ROWS = 256


def _norm_rows(x_ref, g_ref, h_ref):
    g = g_ref[...]

    def rows(i, carry):
        r = pl.ds(pl.multiple_of(i * NORM_ROWS, NORM_ROWS), NORM_ROWS)
        xv = x_ref[r, :]
        h_ref[r, :] = (xv * _rms_scale(xv) * g).astype(BF16)
        return carry

    lax.fori_loop(0, x_ref.shape[0] // NORM_ROWS, rows, 0)


def _norm_matmul(x, g, w, *, tm, tn, name, jobs=()):
    T, K = x.shape
    N = w.shape[1]

    def body(x_ref, g_ref, w_ref, o_ref, h_ref):
        @pl.when(pl.program_id(1) == 0)
        def _():
            _norm_rows(x_ref, g_ref, h_ref)

        o_ref[...] = _dot(h_ref[...], w_ref[...])

    return _pallas(
        body, (x, g, w), name=name, grid=(T // tm, N // tn),
        in_specs=[pl.BlockSpec((tm, K), lambda i, j: (i, 0)), pl.BlockSpec((1, K), lambda i, j: (0, 0)),
                  pl.BlockSpec((K, tn), lambda i, j: (0, j))],
        out_specs=[pl.BlockSpec((tm, tn), lambda i, j: (i, j)), pl.BlockSpec((tm, K), lambda i, j: (i, 0))],
        out_shape=[jax.ShapeDtypeStruct((T, N), F32), jax.ShapeDtypeStruct((T, K), BF16)],
        semantics=("parallel", "arbitrary"), jobs=jobs)


def _ffn_in(x2, g, w, *, tm, tn):
    T, K = x2.shape
    nb = D_FF // tn

    def body(x_ref, g_ref, wg_ref, wu_ref, gate_ref, up_ref, act_ref, h_ref):
        @pl.when(pl.program_id(1) == 0)
        def _():
            _norm_rows(x_ref, g_ref, h_ref)

        h = h_ref[...]
        gate = _dot(h, wg_ref[...])
        up = _dot(h, wu_ref[...])
        gate_ref[...] = gate.astype(BF16)
        up_ref[...] = up.astype(BF16)
        act_ref[...] = (gate * _sigmoid(gate) * up).astype(BF16)

    blk = pl.BlockSpec((tm, tn), lambda i, j: (i, j))
    return pl.pallas_call(
        body, name="ffn_in", grid=(T // tm, nb),
        in_specs=[pl.BlockSpec((tm, K), lambda i, j: (i, 0)), pl.BlockSpec((1, K), lambda i, j: (0, 0)),
                  pl.BlockSpec((K, tn), lambda i, j: (0, j)), pl.BlockSpec((K, tn), lambda i, j: (0, j + nb))],
        out_specs=[blk, blk, blk, pl.BlockSpec((tm, K), lambda i, j: (i, 0))],
        out_shape=[jax.ShapeDtypeStruct((T, D_FF), BF16), jax.ShapeDtypeStruct((T, D_FF), BF16),
                   jax.ShapeDtypeStruct((T, D_FF), BF16), jax.ShapeDtypeStruct((T, K), BF16)],
        compiler_params=_params(dimension_semantics=("parallel", "arbitrary")),
    )(x2, g, w, w)


def _branch_mix(pm, z, w_pool_out, w_rnn_out, proj, *, tm, tn):
    T = pm.shape[0]
    col_gp = (D_POOL + 2 * D_RNN) // tn
    col_gr = col_gp + D_MODEL // tn

    def body(pm_ref, z_ref, wp_ref, wr_ref, gp_ref, gr_ref, yp_ref, yr_ref, mix_ref):
        yp = _dot(pm_ref[...], wp_ref[...])
        yr = _dot(z_ref[...], wr_ref[...])
        yp_ref[...] = yp.astype(BF16)
        yr_ref[...] = yr.astype(BF16)
        mix_ref[...] = (_sigmoid(gp_ref[...]) * yp + _sigmoid(gr_ref[...]) * yr).astype(BF16)

    blk = pl.BlockSpec((tm, tn), lambda i, j: (i, j))
    return pl.pallas_call(
        body, name="branch_mix", grid=(T // tm, D_MODEL // tn),
        in_specs=[pl.BlockSpec((tm, D_POOL), lambda i, j: (i, 0)), pl.BlockSpec((tm, D_RNN), lambda i, j: (i, 0)),
                  pl.BlockSpec((D_POOL, tn), lambda i, j: (0, j)), pl.BlockSpec((D_RNN, tn), lambda i, j: (0, j)),
                  pl.BlockSpec((tm, tn), lambda i, j: (i, col_gp + j)), pl.BlockSpec((tm, tn), lambda i, j: (i, col_gr + j))],
        out_specs=[blk, blk, blk],
        out_shape=[jax.ShapeDtypeStruct((T, D_MODEL), BF16), jax.ShapeDtypeStruct((T, D_MODEL), BF16),
                   jax.ShapeDtypeStruct((T, D_MODEL), BF16)],
        compiler_params=_params(dimension_semantics=("parallel", "parallel")),
    )(pm, z, w_pool_out, w_rnn_out, proj, proj)


def _out_proj_residual(mix, w_o, x, *, tm):
    T = x.shape[0]

    def body(mix_ref, w_ref, x_ref, o_ref):
        o_ref[...] = x_ref[...] + _dot(mix_ref[...], w_ref[...])

    row = pl.BlockSpec((tm, D_MODEL), lambda i: (i, 0))
    return pl.pallas_call(
        body, name="out_proj_residual", grid=(T // tm,),
        in_specs=[row, pl.BlockSpec((D_MODEL, D_MODEL), lambda i: (0, 0)), row],
        out_specs=row, out_shape=jax.ShapeDtypeStruct((T, D_MODEL), F32),
        compiler_params=_params(dimension_semantics=("parallel",)),
    )(mix, w_o, x)


def _ffn_out_loss(act, w, x2, g3, target, *, tm):
    T = x2.shape[0]

    def body(act_ref, w_ref, x2_ref, g_ref, t_ref, dx_ref, dxb_ref, sq_ref, dg_ref):
        @pl.when(pl.program_id(0) == 0)
        def _():
            sq_ref[...] = jnp.zeros_like(sq_ref)
            dg_ref[...] = jnp.zeros_like(dg_ref)

        x3 = x2_ref[...] + _dot(act_ref[...], w_ref[...])
        g = g_ref[...]
        err = x3 * _rms_scale(x3) * g - t_ref[...]
        sq_ref[...] += jnp.sum(err * err, axis=0, keepdims=True)
        dx, dgp = _rms_bwd(err * (1.0 / D_MODEL), x3, g)
        dg_ref[...] += jnp.sum(dgp, axis=0, keepdims=True)
        dx_ref[...] = dx
        dxb_ref[...] = dx.astype(BF16)

    row = pl.BlockSpec((tm, D_MODEL), lambda i: (i, 0))
    vec = pl.BlockSpec((1, D_MODEL), lambda i: (0, 0))
    return pl.pallas_call(
        body, name="ffn_out_loss", grid=(T // tm,),
        in_specs=[pl.BlockSpec((tm, D_FF), lambda i: (i, 0)), pl.BlockSpec((D_FF, D_MODEL), lambda i: (0, 0)), row, vec, row],
        out_specs=[row, row, vec, vec],
        out_shape=[jax.ShapeDtypeStruct((T, D_MODEL), F32), jax.ShapeDtypeStruct((T, D_MODEL), BF16),
                   jax.ShapeDtypeStruct((1, D_MODEL), F32), jax.ShapeDtypeStruct((1, D_MODEL), F32)],
        compiler_params=_params(dimension_semantics=("arbitrary",)),
    )(act, w, x2, g3, target)


def _ffn_out_bwd(dx3b, w, gate, up, *, tm, tn):
    T = dx3b.shape[0]

    def body(dx_ref, w_ref, gate_ref, up_ref, dgate_ref, dup_ref):
        dact = _dot_nt(dx_ref[...], w_ref[...])
        gate = gate_ref[...].astype(F32)
        s = _sigmoid(gate)
        dgate_ref[...] = (dact * up_ref[...].astype(F32) * s * (1.0 + gate * (1.0 - s))).astype(BF16)
        dup_ref[...] = (dact * gate * s).astype(BF16)

    blk = pl.BlockSpec((tm, tn), lambda i, j: (i, j))
    return pl.pallas_call(
        body, name="ffn_out_bwd", grid=(T // tm, D_FF // tn),
        in_specs=[pl.BlockSpec((tm, D_MODEL), lambda i, j: (i, 0)), pl.BlockSpec((tn, D_MODEL), lambda i, j: (j, 0)), blk, blk],
        out_specs=[blk, blk],
        out_shape=[jax.ShapeDtypeStruct((T, D_FF), BF16), jax.ShapeDtypeStruct((T, D_FF), BF16)],
        compiler_params=_params(dimension_semantics=("parallel", "parallel")),
    )(dx3b, w, gate, up)


def _ffn_in_bwd(dgate, dup, w, dx3, x2, g2, *, tm):
    T = x2.shape[0]

    def body(dgate_ref, dup_ref, w_ref, dx3_ref, x2_ref, g_ref, dx_ref, dxb_ref, dg_ref):
        @pl.when(pl.program_id(0) == 0)
        def _():
            dg_ref[...] = jnp.zeros_like(dg_ref)

        dh = _dot_nt(dgate_ref[...], w_ref[:, :D_FF]) + _dot_nt(dup_ref[...], w_ref[:, D_FF:])
        dxn, dgp = _rms_bwd(dh, x2_ref[...], g_ref[...])
        dx = dx3_ref[...] + dxn
        dg_ref[...] += jnp.sum(dgp, axis=0, keepdims=True)
        dx_ref[...] = dx
        dxb_ref[...] = dx.astype(BF16)

    row = pl.BlockSpec((tm, D_MODEL), lambda i: (i, 0))
    wide = pl.BlockSpec((tm, D_FF), lambda i: (i, 0))
    vec = pl.BlockSpec((1, D_MODEL), lambda i: (0, 0))
    return pl.pallas_call(
        body, name="ffn_in_bwd", grid=(T // tm,),
        in_specs=[wide, wide, pl.BlockSpec((D_MODEL, 2 * D_FF), lambda i: (0, 0)), row, row, vec],
        out_specs=[row, row, vec],
        out_shape=[jax.ShapeDtypeStruct((T, D_MODEL), F32), jax.ShapeDtypeStruct((T, D_MODEL), BF16),
                   jax.ShapeDtypeStruct((1, D_MODEL), F32)],
        compiler_params=_params(dimension_semantics=("arbitrary",)),
    )(dgate, dup, w, dx3, x2, g2)


def _out_proj_bwd(dx2b, w_o, proj, y_pool, y_rnn, *, tm, tn, jobs=()):
    T = dx2b.shape[0]
    col_gp = (D_POOL + 2 * D_RNN) // tn
    col_gr = col_gp + D_MODEL // tn

    def body(dx_ref, w_ref, gp_ref, gr_ref, yp_ref, yr_ref, dgp_ref, dgr_ref, dyp_ref, dyr_ref):
        dmix = _dot_nt(dx_ref[...], w_ref[...])
        sp = _sigmoid(gp_ref[...])
        sr = _sigmoid(gr_ref[...])
        dgp_ref[...] = (dmix * yp_ref[...].astype(F32) * sp * (1.0 - sp)).astype(BF16)
        dgr_ref[...] = (dmix * yr_ref[...].astype(F32) * sr * (1.0 - sr)).astype(BF16)
        dyp_ref[...] = (dmix * sp).astype(BF16)
        dyr_ref[...] = (dmix * sr).astype(BF16)

    blk = pl.BlockSpec((tm, tn), lambda i, j: (i, j))
    out = jax.ShapeDtypeStruct((T, D_MODEL), BF16)
    return _pallas(
        body, (dx2b, w_o, proj, proj, y_pool, y_rnn), name="out_proj_bwd", grid=(T // tm, D_MODEL // tn),
        in_specs=[pl.BlockSpec((tm, D_MODEL), lambda i, j: (i, 0)), pl.BlockSpec((tn, D_MODEL), lambda i, j: (j, 0)),
                  pl.BlockSpec((tm, tn), lambda i, j: (i, col_gp + j)), pl.BlockSpec((tm, tn), lambda i, j: (i, col_gr + j)), blk, blk],
        out_specs=[blk, blk, blk, blk], out_shape=[out, out, out, out], semantics=("parallel", "parallel"), jobs=jobs)


def _branch_bwd(dyp, dyr, w_pool_out, w_rnn_out, *, tm):
    T = dyp.shape[0]

    def body(dyp_ref, dyr_ref, wp_ref, wr_ref, dpm_ref, dz_ref):
        dpm_ref[...] = _dot_nt(dyp_ref[...], wp_ref[...])
        dz_ref[...] = _dot_nt(dyr_ref[...], wr_ref[...])

    row = pl.BlockSpec((tm, D_MODEL), lambda i: (i, 0))
    return pl.pallas_call(
        body, name="branch_bwd", grid=(T // tm,),
        in_specs=[row, row, pl.BlockSpec((D_POOL, D_MODEL), lambda i: (0, 0)), pl.BlockSpec((D_RNN, D_MODEL), lambda i: (0, 0))],
        out_specs=[pl.BlockSpec((tm, D_POOL), lambda i: (i, 0)), pl.BlockSpec((tm, D_RNN), lambda i: (i, 0))],
        out_shape=[jax.ShapeDtypeStruct((T, D_POOL), F32), jax.ShapeDtypeStruct((T, D_RNN), F32)],
        compiler_params=_params(dimension_semantics=("parallel",)),
    )(dyp, dyr, w_pool_out, w_rnn_out)


def _in_proj_bwd(segs, w, dx2, x, g1, *, tm, jobs=()):
    T = x.shape[0]
    widths = [s.shape[1] for s in segs]
    offs = [sum(widths[:k]) for k in range(len(widths))]
    n = len(segs)

    def body(*refs):
        seg_refs, (w_ref, dx2_ref, x_ref, g_ref, dx_ref, dg_ref) = refs[:n], refs[n:]

        @pl.when(pl.program_id(0) == 0)
        def _():
            dg_ref[...] = jnp.zeros_like(dg_ref)

        dh = _dot_nt(seg_refs[0][...], w_ref[:, offs[0]:offs[0] + widths[0]])
        for k in range(1, n):
            dh += _dot_nt(seg_refs[k][...], w_ref[:, offs[k]:offs[k] + widths[k]])
        dxn, dgp = _rms_bwd(dh, x_ref[...], g_ref[...])
        dg_ref[...] += jnp.sum(dgp, axis=0, keepdims=True)
        dx_ref[...] = dx2_ref[...] + dxn

    row = pl.BlockSpec((tm, D_MODEL), lambda i: (i, 0))
    vec = pl.BlockSpec((1, D_MODEL), lambda i: (0, 0))
    return _pallas(
        body, (*segs, w, dx2, x, g1), name="in_proj_bwd", grid=(T // tm,),
        in_specs=[pl.BlockSpec((tm, wd), lambda i: (i, 0)) for wd in widths]
        + [pl.BlockSpec((D_MODEL, D_IN), lambda i: (0, 0)), row, row, vec],
        out_specs=[row, vec],
        out_shape=[jax.ShapeDtypeStruct((T, D_MODEL), F32), jax.ShapeDtypeStruct((1, D_MODEL), F32)],
        semantics=("arbitrary",), jobs=jobs)


def _weight_grad(a, segs, *, tm, tn, name, jobs=None):
    T, M = a.shape
    nblk = [s.shape[1] // tn for s in segs]
    first = [sum(nblk[:k]) for k in range(len(segs))]
    n = len(segs)

    def body(a_ref, *refs):
        seg_refs, o_ref = refs[:n], refs[n]
        j = pl.program_id(1)
        for k in range(n):
            @pl.when((j >= first[k]) & (j < first[k] + nblk[k]))
            def _(k=k):
                o_ref[...] = _dot_tn(a_ref[...], seg_refs[k][...])

    def seg_spec(k):
        return pl.BlockSpec((T, tn), lambda i, j: (0, jnp.clip(j - first[k], 0, nblk[k] - 1)))

    (grad,), results = _pallas(
        body, (a, *segs), name=name, grid=(M // tm, sum(nblk)),
        in_specs=[pl.BlockSpec((T, tm), lambda i, j: (0, i))] + [seg_spec(k) for k in range(n)],
        out_specs=[pl.BlockSpec((tm, tn), lambda i, j: (i, j))],
        out_shape=[jax.ShapeDtypeStruct((M, sum(nblk) * tn), F32)],
        semantics=("parallel", "arbitrary"), jobs=jobs or ())
    return grad if jobs is None else (grad, results)


def _pad_front(dst, src, halo):
    dst[pl.ds(0, halo), :] = jnp.zeros((halo, src.shape[1]), F32)

    def fill(i, carry):
        r0 = pl.multiple_of(i * CHUNK, CHUNK)
        dst[pl.ds(r0 + halo, CHUNK), :] = src[pl.ds(r0, CHUNK), :]
        return carry

    lax.fori_loop(0, src.shape[0] // CHUNK, fill, 0)


def _shift_rows(v, k):
    return pltpu.roll(v, k % v.shape[0], axis=0)


def _window_sums(xs, direction):
    s2 = xs + _shift_rows(xs, direction)
    s4 = s2 + _shift_rows(s2, 2 * direction)
    s8 = s4 + _shift_rows(s4, 4 * direction)
    s16 = s8 + _shift_rows(s8, 8 * direction)
    return s2, s4, s8, s16


def _select_window(g, sums):
    s2, s4, s8, s16 = sums
    return jnp.where(g == 0, s2, jnp.where(g == 1, s4, jnp.where(g == 2, s8, s16)))


def _pool_count(g, start, rows):
    t = start + lax.broadcasted_iota(jnp.int32, (rows, 1), 0)
    return jnp.minimum(t + 1, jnp.left_shift(2, g)).astype(F32)


def _pool_fwd(proj, w_grp, scale):
    T = proj.shape[0]
    nchunk = T // CHUNK

    def body(u_ref, w_ref, s_ref, o_ref, upad):
        g = pl.program_id(0)
        _pad_front(upad, u_ref, POOL_HALO)
        w = w_ref[...].astype(BF16)
        scale_row = s_ref[...]

        def chunk(i, carry):
            r0 = pl.multiple_of(i * CHUNK, CHUNK)
            xs = upad[pl.ds(r0, CHUNK + POOL_HALO), :]
            win = _select_window(g, _window_sums(xs, 1))[POOL_HALO:]
            pooled = win / _pool_count(g, r0, CHUNK) - xs[POOL_HALO:]
            o_ref[pl.ds(r0, CHUNK), :] = (_dot(pooled.astype(BF16), w) * scale_row).astype(BF16)
            return carry

        lax.fori_loop(0, nchunk, chunk, 0)

    return pl.pallas_call(
        body, name="pool_fwd", grid=(N_POOL_GROUPS,),
        in_specs=[pl.BlockSpec((T, HEAD), lambda g: (0, g)), pl.BlockSpec((None, HEAD, HEAD), lambda g: (g, 0, 0)),
                  pl.BlockSpec((1, HEAD), lambda g: (0, g))],
        out_specs=pl.BlockSpec((T, HEAD), lambda g: (0, g)),
        out_shape=jax.ShapeDtypeStruct((T, D_POOL), BF16),
        scratch_shapes=[pltpu.VMEM((T + POOL_HALO, HEAD), F32)],
        compiler_params=_params(dimension_semantics=("parallel",)),
    )(proj, w_grp, scale)


def _pool_bwd(proj, dpm, w_grp, scale, jobs=()):
    T = proj.shape[0]
    nchunk = T // CHUNK

    def body(u_ref, dpm_ref, w_ref, s_ref, du_ref, dw_ref, ds_ref, upad, zpad, dpool):
        g = pl.program_id(0)
        _pad_front(upad, u_ref, POOL_HALO)
        zpad[pl.ds(T, POOL_HALO), :] = jnp.zeros((POOL_HALO, HEAD), F32)
        dw_ref[...] = jnp.zeros_like(dw_ref)
        ds_ref[...] = jnp.zeros_like(ds_ref)
        w = w_ref[...].astype(BF16)
        scale_row = s_ref[...]

        def chunk(i, carry):
            r0 = pl.multiple_of(i * CHUNK, CHUNK)
            xs = upad[pl.ds(r0, CHUNK + POOL_HALO), :]
            cnt = _pool_count(g, r0, CHUNK)
            pooled = (_select_window(g, _window_sums(xs, 1))[POOL_HALO:] / cnt - xs[POOL_HALO:]).astype(BF16)
            mixed = _dot(pooled, w)
            d = dpm_ref[pl.ds(r0, CHUNK), :]
            ds_ref[...] += jnp.sum(d * mixed, axis=0, keepdims=True)
            dmixed = (d * scale_row).astype(BF16)
            dw_ref[...] += _dot_tn(pooled, dmixed)
            dp = _dot_nt(dmixed, w)
            dpool[pl.ds(r0, CHUNK), :] = dp
            zpad[pl.ds(r0, CHUNK), :] = dp / cnt
            return carry

        lax.fori_loop(0, nchunk, chunk, 0)

        def chunk2(i, carry):
            r0 = pl.multiple_of(i * CHUNK, CHUNK)
            zs = zpad[pl.ds(r0, CHUNK + POOL_HALO), :]
            win = _select_window(g, _window_sums(zs, -1))[:CHUNK]
            du_ref[pl.ds(r0, CHUNK), :] = (win - dpool[pl.ds(r0, CHUNK), :]).astype(BF16)
            return carry

        lax.fori_loop(0, nchunk, chunk2, 0)

    col = pl.BlockSpec((T, HEAD), lambda g: (0, g))
    return _pallas(
        body, (proj, dpm, w_grp, scale), name="pool_bwd", grid=(N_POOL_GROUPS,),
        in_specs=[col, col, pl.BlockSpec((None, HEAD, HEAD), lambda g: (g, 0, 0)), pl.BlockSpec((1, HEAD), lambda g: (0, g))],
        out_specs=[col, pl.BlockSpec((None, HEAD, HEAD), lambda g: (g, 0, 0)), pl.BlockSpec((1, HEAD), lambda g: (0, g))],
        out_shape=[jax.ShapeDtypeStruct((T, D_POOL), BF16), jax.ShapeDtypeStruct((N_POOL_GROUPS, HEAD, HEAD), F32),
                   jax.ShapeDtypeStruct((1, D_POOL), F32)],
        scratch_shapes=[pltpu.VMEM((T + POOL_HALO, HEAD), F32), pltpu.VMEM((T + POOL_HALO, HEAD), F32), pltpu.VMEM((T, HEAD), F32)],
        semantics=("parallel",), jobs=jobs)


def _conv_taps(xs, cw):
    v = cw[CONV_WIDTH - 1] * xs[SUBLANES:]
    for k in range(CONV_WIDTH - 1):
        v += cw[k] * _shift_rows(xs, CONV_WIDTH - 1 - k)[SUBLANES:]
    return v


def _tap_rows(cw_ref):
    return [cw_ref[k:k + 1, :] for k in range(CONV_WIDTH)]


def _softplus_neg(lam):
    return jnp.maximum(-lam, 0.0) + _log1p(jnp.exp(-jnp.abs(lam)))


def _lru_gates(v, wa, ba, wx, bx, sp):
    vb = v.astype(BF16)
    ra = _sigmoid(_dot(vb, wa) + ba)
    ix = _sigmoid(_dot(vb, wx) + bx)
    log_a = -LRU_C * ra * sp
    a = jnp.exp(log_a)
    sq = jnp.sqrt(-jnp.tanh(log_a) * (a * a + 1.0))
    return ra, ix, a, sq


def _row_bcast(v, r):
    return jnp.broadcast_to(v[r:r + 1, :], v.shape)


def _tile_scans(a, b, direction):
    ri = lax.broadcasted_iota(jnp.int32, a.shape, 0) % SUBLANES
    A, B = a, b
    for s in (1, 2, 4):
        ok = (ri >= s) if direction == 1 else (ri + s < SUBLANES)
        As, Bs = _shift_rows(A, s * direction), _shift_rows(B, s * direction)
        B = jnp.where(ok, A * Bs + B, B)
        A = jnp.where(ok, A * As, A)
    return A, B


TILES_PER_STEP = 8


def _carry_tiles(A_s, B_s, out, ntile, direction):
    out_row = SUBLANES - 1 if direction == 1 else 0

    def step(k, carry):
        for j in range(TILES_PER_STEP):
            t = k * TILES_PER_STEP + j
            r0 = pl.multiple_of((t if direction == 1 else ntile - 1 - t) * SUBLANES, SUBLANES)
            A, B = A_s[pl.ds(r0, SUBLANES), :], B_s[pl.ds(r0, SUBLANES), :]
            out[pl.ds(r0, SUBLANES), :] = A * carry + B
            carry = _row_bcast(A, out_row) * carry + _row_bcast(B, out_row)
        return carry

    lax.fori_loop(0, ntile // TILES_PER_STEP, step, jnp.zeros((SUBLANES, HEAD), F32))


def _rnn_fwd(proj, conv_w, conv_b, w_a, b_a, w_x, b_x, lam, jobs=()):
    T = proj.shape[0]
    nchunk = T // CHUNK
    ntile = T // SUBLANES

    def body(u_ref, ug_ref, cw_ref, cb_ref, wa_ref, ba_ref, wx_ref, bx_ref, lam_ref, h_ref, z_ref, upad, a_s, b_s):
        _pad_front(upad, u_ref, SUBLANES)
        cw, cb = _tap_rows(cw_ref), cb_ref[...]
        wa, wx = wa_ref[...].astype(BF16), wx_ref[...].astype(BF16)
        ba, bx = ba_ref[...], bx_ref[...]
        sp = _softplus_neg(lam_ref[...])

        def chunk(i, carry):
            r0 = pl.multiple_of(i * CHUNK, CHUNK)
            v = _conv_taps(upad[pl.ds(r0, CHUNK + SUBLANES), :], cw) + cb
            _, ix, a, sq = _lru_gates(v, wa, ba, wx, bx, sp)
            a_s[pl.ds(r0, CHUNK), :], b_s[pl.ds(r0, CHUNK), :] = _tile_scans(a, sq * ix * v, 1)
            return carry

        lax.fori_loop(0, nchunk, chunk, 0)
        _carry_tiles(a_s, b_s, h_ref, ntile, 1)

        def chunk3(i, carry):
            r0 = pl.multiple_of(i * CHUNK, CHUNK)
            gl, _ = _gelu_parts(ug_ref[pl.ds(r0, CHUNK), :])
            z_ref[pl.ds(r0, CHUNK), :] = (h_ref[pl.ds(r0, CHUNK), :] * gl).astype(BF16)
            return carry

        lax.fori_loop(0, nchunk, chunk3, 0)

    col = pl.BlockSpec((T, HEAD), lambda h: (0, h))
    vec = pl.BlockSpec((1, HEAD), lambda h: (0, h))
    mat = pl.BlockSpec((None, HEAD, HEAD), lambda h: (h, 0, 0))
    return _pallas(
        body, (proj, proj, conv_w, conv_b, w_a, b_a, w_x, b_x, lam), name="rnn_fwd", grid=(N_RNN_HEADS,),
        in_specs=[pl.BlockSpec((T, HEAD), lambda h: (0, COL_RNN + h)), pl.BlockSpec((T, HEAD), lambda h: (0, COL_GATE + h)),
                  pl.BlockSpec((CONV_WIDTH, HEAD), lambda h: (0, h)), vec, mat, vec, mat, vec, vec],
        out_specs=[col, col],
        out_shape=[jax.ShapeDtypeStruct((T, D_RNN), F32), jax.ShapeDtypeStruct((T, D_RNN), BF16)],
        scratch_shapes=[pltpu.VMEM((T + SUBLANES, HEAD), F32), pltpu.VMEM((T, HEAD), F32), pltpu.VMEM((T, HEAD), F32)],
        semantics=("parallel",), jobs=jobs)


def _rnn_bwd(proj, hr, dz, conv_w, conv_b, w_a, b_a, w_x, b_x, lam, jobs=()):
    T = proj.shape[0]
    nchunk = T // CHUNK
    ntile = T // SUBLANES

    def body(u_ref, ug_ref, h_ref, dz_ref, cw_ref, cb_ref, wa_ref, ba_ref, wx_ref, bx_ref, lam_ref,
             du_ref, dug_ref, dwa_ref, dwx_ref, dba_ref, dbx_ref, dlam_ref, dcb_ref, dcw_ref,
             upad, hpad, apad, v_s, ra_s, ix_s, sq_s, g_s, dvpad, ga_s):
        zero_tile = jnp.zeros((SUBLANES, HEAD), F32)
        _pad_front(upad, u_ref, SUBLANES)
        _pad_front(hpad, h_ref, SUBLANES)
        apad[pl.ds(T, SUBLANES), :] = zero_tile
        dvpad[pl.ds(T, SUBLANES), :] = zero_tile
        for ref in (dwa_ref, dwx_ref, dba_ref, dbx_ref, dlam_ref, dcb_ref, dcw_ref):
            ref[...] = jnp.zeros_like(ref)
        cw, cb = _tap_rows(cw_ref), cb_ref[...]
        wa, wx = wa_ref[...].astype(BF16), wx_ref[...].astype(BF16)
        ba, bx = ba_ref[...], bx_ref[...]
        lam_row = lam_ref[...]
        sp = _softplus_neg(lam_row)

        def chunk(i, carry):
            r0 = pl.multiple_of(i * CHUNK, CHUNK)
            rows = pl.ds(r0, CHUNK)
            h = h_ref[rows, :]
            v = _conv_taps(upad[pl.ds(r0, CHUNK + SUBLANES), :], cw) + cb
            ra, ix, a, sq = _lru_gates(v, wa, ba, wx, bx, sp)
            v_s[rows, :], ra_s[rows, :], ix_s[rows, :], sq_s[rows, :], apad[rows, :] = v, ra, ix, sq, a
            gl, dgl = _gelu_parts(ug_ref[rows, :])
            d = dz_ref[rows, :]
            g_s[rows, :] = d * gl
            dug_ref[rows, :] = (d * h * dgl).astype(BF16)
            return carry

        lax.fori_loop(0, nchunk, chunk, 0)

        def chunk2(i, carry):
            r0 = pl.multiple_of(i * CHUNK, CHUNK)
            rows = pl.ds(r0, CHUNK)
            a_next = _shift_rows(apad[pl.ds(r0, CHUNK + SUBLANES), :], -1)[:CHUNK]
            ga_s[rows, :], g_s[rows, :] = _tile_scans(a_next, g_s[rows, :], -1)
            return carry

        lax.fori_loop(0, nchunk, chunk2, 0)
        _carry_tiles(ga_s, g_s, g_s, ntile, -1)

        def chunk3(i, carry):
            r0 = pl.multiple_of(i * CHUNK, CHUNK)
            rows = pl.ds(r0, CHUNK)
            g = g_s[rows, :]
            h_prev = _shift_rows(hpad[pl.ds(r0, CHUNK + SUBLANES), :], 1)[SUBLANES:]
            v, ra, ix, sq, a = v_s[rows, :], ra_s[rows, :], ix_s[rows, :], sq_s[rows, :], apad[rows, :]
            d_sq = g * ix * v
            d_ix = g * sq * v
            d_la = a * g * h_prev - d_sq * a * a / sq
            dlam_ref[...] += jnp.sum(d_la * ra, axis=0, keepdims=True)
            d_pa = d_la * (-LRU_C) * sp * ra * (1.0 - ra)
            d_px = d_ix * ix * (1.0 - ix)
            vb, d_pab, d_pxb = v.astype(BF16), d_pa.astype(BF16), d_px.astype(BF16)
            dwa_ref[...] += _dot_tn(vb, d_pab)
            dwx_ref[...] += _dot_tn(vb, d_pxb)
            dba_ref[...] += jnp.sum(d_pa, axis=0, keepdims=True)
            dbx_ref[...] += jnp.sum(d_px, axis=0, keepdims=True)
            dv = g * sq * ix + _dot_nt(d_pab, wa) + _dot_nt(d_pxb, wx)
            dvpad[rows, :] = dv
            dcb_ref[...] += jnp.sum(dv, axis=0, keepdims=True)
            xs = upad[pl.ds(r0, CHUNK + SUBLANES), :]
            for k in range(CONV_WIDTH):
                u_k = _shift_rows(xs, CONV_WIDTH - 1 - k)[SUBLANES:] if k < CONV_WIDTH - 1 else xs[SUBLANES:]
                dcw_ref[k:k + 1, :] += jnp.sum(dv * u_k, axis=0, keepdims=True)
            return carry

        lax.fori_loop(0, nchunk, chunk3, 0)
        dlam_ref[...] = dlam_ref[...] * (LRU_C * _sigmoid(-lam_row))

        def chunk4(i, carry):
            r0 = pl.multiple_of(i * CHUNK, CHUNK)
            dvs = dvpad[pl.ds(r0, CHUNK + SUBLANES), :]
            du = cw[CONV_WIDTH - 1] * dvs[:CHUNK]
            for k in range(CONV_WIDTH - 1):
                du += cw[k] * _shift_rows(dvs, -(CONV_WIDTH - 1 - k))[:CHUNK]
            du_ref[pl.ds(r0, CHUNK), :] = du.astype(BF16)
            return carry

        lax.fori_loop(0, nchunk, chunk4, 0)

    col = pl.BlockSpec((T, HEAD), lambda h: (0, h))
    vec = pl.BlockSpec((1, HEAD), lambda h: (0, h))
    mat = pl.BlockSpec((None, HEAD, HEAD), lambda h: (h, 0, 0))
    taps = pl.BlockSpec((CONV_WIDTH, HEAD), lambda h: (0, h))
    vec_out = jax.ShapeDtypeStruct((1, D_RNN), F32)
    mat_out = jax.ShapeDtypeStruct((N_RNN_HEADS, HEAD, HEAD), F32)
    seq = pltpu.VMEM((T, HEAD), F32)
    seq_pad = pltpu.VMEM((T + SUBLANES, HEAD), F32)
    return _pallas(
        body, (proj, proj, hr, dz, conv_w, conv_b, w_a, b_a, w_x, b_x, lam), name="rnn_bwd", grid=(N_RNN_HEADS,),
        in_specs=[pl.BlockSpec((T, HEAD), lambda h: (0, COL_RNN + h)), pl.BlockSpec((T, HEAD), lambda h: (0, COL_GATE + h)),
                  col, col, taps, vec, mat, vec, mat, vec, vec],
        out_specs=[col, col, mat, mat, vec, vec, vec, vec, taps],
        out_shape=[jax.ShapeDtypeStruct((T, D_RNN), BF16), jax.ShapeDtypeStruct((T, D_RNN), BF16), mat_out, mat_out,
                   vec_out, vec_out, vec_out, vec_out, jax.ShapeDtypeStruct((CONV_WIDTH, D_RNN), F32)],
        scratch_shapes=[seq_pad, seq_pad, seq_pad, seq, seq, seq, seq, seq, seq_pad, seq],
        semantics=("parallel",), jobs=jobs)


GROUP_FFN = ["w_ffn_out", "w_ffn_in"]
GROUP_MIX = ["w_o", "w_pool_out", "w_rnn_out"]
GROUP_IN = ["w_in"]


def _step(x, target, s, full, conv_w_mine, place):
    T = x.shape[0]
    tall, mid, low = min(T, 2048), min(T, 1024), min(T, 512)
    c1 = place[1:]
    full = dict(full)

    def gathered(names, results):
        full.update(zip(names, results))

    (full["w_in"], conv_w), = _run_jobs([_gather_job(full, ["w_in"], conv_w_mine)], "gather_w_in")
    early = ["w_pool_out", "w_rnn_out", "w_o", "w_ffn_out"]
    (proj, h1), (res,) = _norm_matmul(x, s["norm_mix"], full["w_in"], tm=tall, tn=512, name="in_proj", jobs=[_gather_job(full, early)])
    gathered(early, res)
    pm = _pool_fwd(proj, s["w_pool_grp"], s["pool_scale"])
    (hr, z), (res,) = _rnn_fwd(proj, conv_w, s["conv_b"], s["w_rg_a"], s["b_rg_a"], s["w_rg_x"], s["b_rg_x"], s["lru_lambda"],
                               jobs=[_gather_job(full, ["w_ffn_in"])])
    gathered(["w_ffn_in"], res)
    y_pool, y_rnn, mix = _branch_mix(pm, z, full["w_pool_out"], full["w_rnn_out"], proj, tm=tall, tn=256)
    x2 = _out_proj_residual(mix, full["w_o"], x, tm=mid)
    gate, up, act, h2 = _ffn_in(x2, s["norm_ffn"], full["w_ffn_in"], tm=tall, tn=256)
    dx3, dx3b, sq_cols, g_norm_final = _ffn_out_loss(act, full["w_ffn_out"], x2, s["norm_final"], target, tm=low)

    g = {"norm_final": g_norm_final}

    def chip_sums(names, from_sibling):
        sums = {name: _chip_sum(name, g[name], got, c1) for name, got in zip(names, from_sibling)}
        return {name: v[0] for name, v in sums.items()}, {name: v[1] for name, v in sums.items()}

    def final_sums(names, sums, from_chips):
        return {name: _final_sum(name, sums[name], got, place) for name, got in zip(names, from_chips)}

    dgate, dup = _ffn_out_bwd(dx3b, full["w_ffn_out"], gate, up, tm=tall, tn=256)
    g["w_ffn_out"] = _weight_grad(act, [dx3b], tm=256, tn=D_MODEL, name="w_ffn_out_grad")
    dx2, dx2b, g["norm_ffn"] = _ffn_in_bwd(dgate, dup, full["w_ffn_in"], dx3, x2, s["norm_ffn"], tm=low)
    g["w_ffn_in"] = _weight_grad(h2, [dgate, dup], tm=D_MODEL, tn=256, name="w_ffn_in_grad")
    (dgp, dgr, dyp, dyr), (res,) = _out_proj_bwd(dx2b, full["w_o"], proj, y_pool, y_rnn, tm=tall, tn=256,
                                                 jobs=[_sibling_job(g, GROUP_FFN)])
    sums_ffn, sums_ffn_bf16 = chip_sums(GROUP_FFN, res)
    g["w_o"] = _weight_grad(mix, [dx2b], tm=D_MODEL, tn=256, name="w_o_grad")
    dpm, dz = _branch_bwd(dyp, dyr, full["w_pool_out"], full["w_rnn_out"], tm=mid)
    g["w_pool_out"] = _weight_grad(pm, [dyp], tm=D_POOL, tn=256, name="w_pool_out_grad")
    g["w_rnn_out"] = _weight_grad(z, [dyr], tm=D_RNN, tn=256, name="w_rnn_out_grad")
    (dupool, g["w_pool_grp"], g["pool_scale"]), (res,) = _pool_bwd(proj, dpm, s["w_pool_grp"], s["pool_scale"],
                                                                   jobs=[_sibling_job(g, GROUP_MIX)])
    sums_mix, sums_mix_bf16 = chip_sums(GROUP_MIX, res)
    ((durnn, dugate, g["w_rg_a"], g["w_rg_x"], g["b_rg_a"], g["b_rg_x"], g["lru_lambda"], g["conv_b"], g["conv_w"]),
     (res,)) = _rnn_bwd(proj, hr, dz, conv_w, s["conv_b"], s["w_rg_a"], s["b_rg_a"], s["w_rg_x"], s["b_rg_x"], s["lru_lambda"],
                        jobs=[_chips_job(sums_ffn_bf16, GROUP_FFN)])
    shards = final_sums(GROUP_FFN, sums_ffn, res)
    segs = [dupool, durnn, dugate, dgp, dgr]
    g["w_in"], (res, joined) = _weight_grad(h1, segs, tm=D_MODEL, tn=256, name="w_in_grad",
                                           jobs=[_chips_job(sums_mix_bf16, GROUP_MIX), _join_job(shards, GROUP_FFN)])
    grads = dict(zip(GROUP_FFN, joined))
    shards = final_sums(GROUP_MIX, sums_mix, res)
    (res,) = _run_jobs([_sibling_job(g, GROUP_IN)], "w_in_exchange_sibling")
    sums_in, sums_in_bf16 = chip_sums(GROUP_IN, res)
    (grad_x, g["norm_mix"]), (res,) = _in_proj_bwd(segs, full["w_in"], dx2, x, s["norm_mix"], tm=low,
                                                  jobs=[_chips_job(sums_in_bf16, GROUP_IN)])
    shards.update(final_sums(GROUP_IN, sums_in, res))

    vec_rows = [g[name] if name != "pool_scale" else jnp.pad(g[name], ((0, 0), (0, D_MODEL - D_POOL))) for name in VEC_ITEMS]
    vec_rows += [g["conv_w"], sq_cols, jnp.zeros((VEC_ROWS - len(VEC_ITEMS) - CONV_WIDTH - 1, D_MODEL), F32)]
    vec = jnp.concatenate(vec_rows, axis=0).reshape(VEC_ROWS, N_DEV, HEAD).transpose(1, 0, 2)
    mat = jnp.concatenate([g[name].reshape(-1, HEAD) for name in MAT_ITEMS], axis=0).reshape(N_DEV, -1, HEAD)
    (vec, mat), (joined,) = _all_reduce_small([vec, mat], jobs=[_join_job(shards, GROUP_MIX + GROUP_IN)])
    grads.update(zip(GROUP_MIX + GROUP_IN, joined))

    vec = vec.transpose(1, 0, 2).reshape(VEC_ROWS, D_MODEL)
    mat = mat.reshape(-1, HEAD)
    for k, name in enumerate(VEC_ITEMS):
        grads[name] = vec[k:k + 1, :s[name].shape[1]]
    grads["conv_w"] = vec[len(VEC_ITEMS):len(VEC_ITEMS) + CONV_WIDTH]
    row = 0
    for name in MAT_ITEMS:
        rows = s[name].shape[0] * HEAD
        grads[name] = mat[row:row + rows]
        row += rows
    return vec[len(VEC_ITEMS) + CONV_WIDTH], grad_x, grads


LARGE = {"w_in": "col", "w_pool_out": "col", "w_rnn_out": "row", "w_o": "row", "w_ffn_in": "col", "w_ffn_out": "row"}
LARGE_SHAPE = {"w_in": (D_MODEL, D_IN), "w_pool_out": (D_POOL, D_MODEL), "w_rnn_out": (D_RNN, D_MODEL),
               "w_o": (D_MODEL, D_MODEL), "w_ffn_in": (D_MODEL, 2 * D_FF), "w_ffn_out": (D_FF, D_MODEL)}


def _place():
    x, y, c = lax.axis_index("x"), lax.axis_index("y"), lax.axis_index("c")
    return 2 * x + y, c


def _chip_device(chip, c):
    return (chip // 2, chip % 2, c)


def _chip_window(ref, kind, shape, chip, half=None):
    K, N = shape
    if kind == "col":
        rows = slice(None) if half is None else pl.ds(half * (K // 2), K // 2)
        return ref.at[rows, pl.ds(chip * (N // N_CHIPS), N // N_CHIPS)]
    ks = K // N_CHIPS
    if half is None:
        return ref.at[pl.ds(chip * ks, ks), :]
    return ref.at[pl.ds(chip * ks + half * (ks // 2), ks // 2), :]


def _row_half(ref, half):
    rows = ref.shape[0] // 2
    return ref.at[pl.ds(half * rows, rows), :]


def _remote(win_src, win_dst, send_sems, recv_sems, idx, to):
    return pltpu.make_async_remote_copy(src_ref=win_src, dst_ref=win_dst, send_sem=send_sems.at[idx], recv_sem=recv_sems.at[idx],
                                        device_id=to, device_id_type=MESH)


def _gather_job(full, names, conv_w_full=None):
    n = len(names)
    cw_cols = D_RNN // N_CHIPS

    def windows(refs, chip, half):
        return [_chip_window(refs[k], LARGE[name], LARGE_SHAPE[name], chip, half) for k, name in enumerate(names)]

    def ici_copies(refs, send_sems, recv_sems, src_chip, dst_chip, c, r):
        wins = windows(refs, src_chip, c)
        if conv_w_full is not None:
            wins.append(refs[n].at[:, pl.ds(src_chip * cw_cols, cw_cols)])
        return [_remote(win, win, send_sems, recv_sems, (k, r), _chip_device(dst_chip, c)) for k, win in enumerate(wins)]

    def forwards(refs, send_sems, recv_sems, src_chip, half, to_core, chip, r):
        return [_remote(win, win, send_sems, recv_sems, (k, 3 + r), _chip_device(chip, to_core))
                for k, win in enumerate(windows(refs, src_chip, half))]

    def start(ins, outs, send_sems, recv_sems):
        chip, c = _place()
        for r in range(3):
            for cp in ici_copies(outs, send_sems, recv_sems, chip, chip ^ (r + 1), c, r):
                cp.start()

    def finish(ins, outs, send_sems, recv_sems):
        chip, c = _place()
        for r in range(3):
            for cp in ici_copies(outs, send_sems, recv_sems, chip ^ (r + 1), chip, c, r):
                cp.wait_recv()
            for cp in forwards(outs, send_sems, recv_sems, chip ^ (r + 1), c, 1 - c, chip, r):
                cp.start()
        for r in range(3):
            for cp in forwards(outs, send_sems, recv_sems, chip ^ (r + 1), 1 - c, c, chip, r):
                cp.wait_recv()
            for cp in ici_copies(outs, send_sems, recv_sems, chip, chip ^ (r + 1), c, r):
                cp.wait_send()
            for cp in forwards(outs, send_sems, recv_sems, chip ^ (r + 1), c, 1 - c, chip, r):
                cp.wait_send()

    arrays = [full[name] for name in names] + ([conv_w_full] if conv_w_full is not None else [])
    return _Job(arrays, [jax.ShapeDtypeStruct(a.shape, a.dtype) for a in arrays], {k: k for k in range(len(arrays))},
                (len(arrays), 6), start, finish)


def _core_halves(ref, kind, shape, c):
    return [_chip_window(ref, kind, shape, chip, c) for chip in range(N_CHIPS)]


def _sibling_job(grads, names):
    def start(ins, outs, send_sems, recv_sems):
        chip, c = _place()
        for k, name in enumerate(names):
            kind, shape = LARGE[name], LARGE_SHAPE[name]
            if kind == "col":
                pairs = [(_row_half(ins[k], 1 - c), outs[k])]
            else:
                rows = shape[0] // N_DEV
                pairs = [(win, outs[k].at[pl.ds(j * rows, rows), :]) for j, win in enumerate(_core_halves(ins[k], kind, shape, 1 - c))]
            for src, dst in pairs:
                _remote(src, dst, send_sems, recv_sems, k, _chip_device(chip, 1 - c)).start()

    def finish(ins, outs, send_sems, recv_sems):
        chip, c = _place()
        for k in range(len(names)):
            _remote(outs[k], outs[k], send_sems, recv_sems, k, _chip_device(chip, 1 - c)).wait()

    return _Job([grads[name] for name in names],
                [jax.ShapeDtypeStruct((LARGE_SHAPE[name][0] // 2, LARGE_SHAPE[name][1]), F32) for name in names], {},
                (len(names),), start, finish)


def _chip_sum(name, g, got, c):
    kind, (K, N) = LARGE[name], LARGE_SHAPE[name]
    rows = K // N_DEV

    def body(c_ref, g_ref, got_ref, o_ref, ob_ref):
        total = g_ref[...] + got_ref[...]
        o_ref[...] = total
        ob_ref[...] = total.astype(BF16)

    if kind == "col":
        mine = pl.BlockSpec((rows, N), lambda j, c_ref: (j + N_CHIPS * c_ref[0], 0))
    else:
        mine = pl.BlockSpec((rows, N), lambda j, c_ref: (2 * j + c_ref[0], 0))
    blk = pl.BlockSpec((rows, N), lambda j, c_ref: (j, 0))
    return pl.pallas_call(
        body, name=name + "_chip_sum",
        grid_spec=pltpu.PrefetchScalarGridSpec(num_scalar_prefetch=1, grid=(N_CHIPS,), in_specs=[mine, blk], out_specs=[blk, blk]),
        out_shape=[jax.ShapeDtypeStruct((K // 2, N), F32), jax.ShapeDtypeStruct((K // 2, N), BF16)],
        compiler_params=_params(dimension_semantics=("parallel",)),
    )(c, g, got)


def _piece(ref, kind, shape, chip):
    K, N = shape
    if kind == "col":
        return ref.at[:, pl.ds(chip * (N // N_CHIPS), N // N_CHIPS)]
    return ref.at[pl.ds(chip * (K // N_DEV), K // N_DEV), :]


def _piece_shape(name):
    kind, (K, N) = LARGE[name], LARGE_SHAPE[name]
    return (K // 2, N // N_CHIPS) if kind == "col" else (K // N_DEV, N)


def _chips_job(sums, names):
    def copies(ins, outs, send_sems, recv_sems):
        chip, c = _place()
        return [_remote(_piece(ins[k], LARGE[name], LARGE_SHAPE[name], chip ^ (r + 1)), outs[k].at[r], send_sems, recv_sems, (k, r),
                        _chip_device(chip ^ (r + 1), c)) for k, name in enumerate(names) for r in range(3)]

    def start(*refs):
        for cp in copies(*refs):
            cp.start()

    def finish(*refs):
        for cp in copies(*refs):
            cp.wait()

    return _Job([sums[name] for name in names], [jax.ShapeDtypeStruct((3,) + _piece_shape(name), BF16) for name in names], {},
                (len(names), 3), start, finish)


def _final_sum(name, chip_sum, got, place):
    kind = LARGE[name]
    rows, cols = _piece_shape(name)

    def body(place_ref, s_ref, got_ref, o_ref):
        o_ref[...] = ((s_ref[...] + got_ref[0].astype(F32)) + got_ref[1].astype(F32)) + got_ref[2].astype(F32)

    if kind == "col":
        mine = pl.BlockSpec((rows, cols), lambda i, place_ref: (0, place_ref[0]))
    else:
        mine = pl.BlockSpec((rows, cols), lambda i, place_ref: (place_ref[0], 0))
    return pl.pallas_call(
        body, name=name + "_final_sum",
        grid_spec=pltpu.PrefetchScalarGridSpec(
            num_scalar_prefetch=1, grid=(1,), in_specs=[mine, pl.BlockSpec((3, rows, cols), lambda i, place_ref: (0, 0, 0))],
            out_specs=pl.BlockSpec((rows, cols), lambda i, place_ref: (place_ref[1], 0))),
        out_shape=jax.ShapeDtypeStruct((2 * rows, cols), F32),
        compiler_params=_params(dimension_semantics=("arbitrary",)),
    )(place, chip_sum, got)


def _join_job(shards, names):
    def half_copy(outs, send_sems, recv_sems, k, mine):
        chip, c = _place()
        win = _row_half(outs[k], c if mine else 1 - c)
        return _remote(win, win, send_sems, recv_sems, k, _chip_device(chip, 1 - c))

    def start(ins, outs, send_sems, recv_sems):
        for k in range(len(names)):
            half_copy(outs, send_sems, recv_sems, k, True).start()

    def finish(ins, outs, send_sems, recv_sems):
        for k in range(len(names)):
            half_copy(outs, send_sems, recv_sems, k, True).wait_send()
            half_copy(outs, send_sems, recv_sems, k, False).wait_recv()

    arrays = [shards[name] for name in names]
    return _Job(arrays, [jax.ShapeDtypeStruct(a.shape, F32) for a in arrays], {k: k for k in range(len(arrays))},
                (len(arrays),), start, finish)


VEC_ROWS = 16


def _all_reduce_small(slabs, jobs=()):
    n = len(slabs)

    def body(*refs):
        in_refs, out_refs, got_refs = refs[:n], refs[n:2 * n], refs[2 * n:3 * n]
        send_sems, recv_sems = refs[3 * n:]
        x, y, c = lax.axis_index("x"), lax.axis_index("y"), lax.axis_index("c")
        me = 4 * x + 2 * y + c

        def remote(src, dst, k, phase, r):
            other = me ^ r
            return pltpu.make_async_remote_copy(src_ref=src, dst_ref=dst, send_sem=send_sems.at[k, phase, r],
                                                recv_sem=recv_sems.at[k, phase, r],
                                                device_id=(other // 4, (other // 2) % 2, other % 2), device_id_type=MESH)

        scatter = [remote(in_refs[k].at[me ^ r], got_refs[k].at[r], k, 0, r) for r in range(1, N_DEV) for k in range(n)]
        for cp in scatter:
            cp.start()
        for cp in scatter:
            cp.wait()
        for k in range(n):
            total = in_refs[k][me]
            for r in range(1, N_DEV):
                total = total + got_refs[k][r]
            out_refs[k][me] = total
        gather = [remote(out_refs[k].at[me], out_refs[k].at[me], k, 1, r) for r in range(1, N_DEV) for k in range(n)]
        for cp in gather:
            cp.start()
        for r in range(1, N_DEV):
            for k in range(n):
                remote(out_refs[k].at[me ^ r], out_refs[k].at[me ^ r], k, 1, r).wait_recv()
        for cp in gather:
            cp.wait_send()

    return _pallas(
        body, slabs, name="all_reduce_small", grid=(), in_specs=[VMEM] * n, out_specs=[VMEM] * n,
        out_shape=[jax.ShapeDtypeStruct(s.shape, F32) for s in slabs],
        scratch_shapes=[pltpu.VMEM(s.shape, F32) for s in slabs]
        + [pltpu.SemaphoreType.DMA((n, 2, N_DEV)), pltpu.SemaphoreType.DMA((n, 2, N_DEV))], jobs=jobs)


def _cast_into_whole(w, name, place):
    rows, cols = w.shape
    tr = rows // 2

    def body(place_ref, w_ref, o_ref):
        o_ref[...] = w_ref[...].astype(BF16)

    if LARGE[name] == "col":
        window = pl.BlockSpec((tr, cols), lambda i, place_ref: (i, place_ref[0]))
    else:
        window = pl.BlockSpec((tr, cols), lambda i, place_ref: (2 * place_ref[0] + i, 0))
    return pl.pallas_call(
        body, name=name + "_cast",
        grid_spec=pltpu.PrefetchScalarGridSpec(num_scalar_prefetch=1, grid=(2,),
                                               in_specs=[pl.BlockSpec((tr, cols), lambda i, place_ref: (i, 0))], out_specs=window),
        out_shape=jax.ShapeDtypeStruct(LARGE_SHAPE[name], BF16),
        compiler_params=_params(dimension_semantics=("parallel",)))(place, w)


def _adamw_math(w, g, m, v):
    m = ADAM_B1 * m + (1.0 - ADAM_B1) * g
    v = ADAM_B2 * v + (1.0 - ADAM_B2) * (g * g)
    m_hat = m / (1.0 - ADAM_B1 ** ADAM_STEP)
    v_hat = v / (1.0 - ADAM_B2 ** ADAM_STEP)
    delta = -ADAM_LR * (m_hat / (jnp.sqrt(v_hat) + ADAM_EPS) + ADAM_WD * w)
    return delta, m, v


def _adamw_large(w, g, m, v, name):
    rows, cols = w.shape
    tr = rows // 4

    def body(w_ref, g_ref, m_ref, v_ref, d_ref, mo_ref, vo_ref):
        d_ref[...], mo_ref[...], vo_ref[...] = _adamw_math(w_ref[...], g_ref[...], m_ref[...], v_ref[...])

    blk = pl.BlockSpec((tr, cols), lambda i: (i, 0))
    out = jax.ShapeDtypeStruct(w.shape, F32)
    return pl.pallas_call(body, name=name + "_adamw", grid=(4,), in_specs=[blk] * 4, out_specs=[blk] * 3, out_shape=[out] * 3,
                          compiler_params=_params(dimension_semantics=("parallel",)))(w, g, m, v)


def _adamw_small(ws, gs, ms, vs):
    n = len(ws)

    def body(*refs):
        for k in range(n):
            w_ref, g_ref, m_ref, v_ref = (refs[q * n + k] for q in range(4))
            d_ref, mo_ref, vo_ref = (refs[(4 + q) * n + k] for q in range(3))
            d_ref[...], mo_ref[...], vo_ref[...] = _adamw_math(w_ref[...], g_ref[...], m_ref[...], v_ref[...])

    out = [jax.ShapeDtypeStruct(w.shape, F32) for w in ws]
    res = pl.pallas_call(body, name="small_adamw", in_specs=[VMEM] * (4 * n), out_specs=[VMEM] * (3 * n), out_shape=out * 3,
                         compiler_params=_params())(*ws, *gs, *ms, *vs)
    return res[:n], res[n:2 * n], res[2 * n:]


WEIGHTS = ["norm_mix", "w_in", "w_pool_grp", "pool_scale", "w_pool_out", "conv_w", "conv_b", "w_rg_a", "b_rg_a", "w_rg_x",
           "b_rg_x", "lru_lambda", "w_rnn_out", "w_o", "norm_ffn", "w_ffn_in", "w_ffn_out", "norm_final"]
VEC_ITEMS = ["norm_mix", "norm_ffn", "norm_final", "pool_scale", "conv_b", "lru_lambda", "b_rg_a", "b_rg_x"]
MAT_ITEMS = ["w_pool_grp", "w_rg_a", "w_rg_x"]


def _as2d(name, a):
    if name in MAT_ITEMS:
        return a.reshape(-1, HEAD, HEAD)
    if name == "conv_w":
        return a.reshape(CONV_WIDTH, -1)
    return a.reshape(1, -1)


def kernel(x, norm_mix, w_in, w_pool_grp, pool_scale, w_pool_out, conv_w, conv_b, w_rg_a, b_rg_a, w_rg_x, b_rg_x, lru_lambda, w_rnn_out, w_o, norm_ffn, w_ffn_in, w_ffn_out, norm_final, loss_target, m_norm_mix, m_w_in, m_w_pool_grp, m_pool_scale, m_w_pool_out, m_conv_w, m_conv_b, m_w_rg_a, m_b_rg_a, m_w_rg_x, m_b_rg_x, m_lru_lambda, m_w_rnn_out, m_w_o, m_norm_ffn, m_w_ffn_in, m_w_ffn_out, m_norm_final, v_norm_mix, v_w_in, v_w_pool_grp, v_pool_scale, v_w_pool_out, v_conv_w, v_conv_b, v_w_rg_a, v_b_rg_a, v_w_rg_x, v_b_rg_x, v_lru_lambda, v_w_rnn_out, v_w_o, v_norm_ffn, v_w_ffn_in, v_w_ffn_out, v_norm_final):
    given = dict(locals())
    w = {name: given[name] for name in WEIGHTS}
    m = {name: given["m_" + name] for name in WEIGHTS}
    v = {name: given["v_" + name] for name in WEIGHTS}
    chip, c = _place()

    place = jnp.stack([chip, c]).astype(jnp.int32)
    conv_cols = w["conv_w"].shape[-1]
    conv_w_mine = lax.dynamic_update_slice_in_dim(jnp.zeros((CONV_WIDTH, D_RNN), F32), w["conv_w"][0], chip * conv_cols, axis=1)
    full = {name: _cast_into_whole(w[name][0], name, place) for name in LARGE}
    small = {name: _as2d(name, w[name]) for name in WEIGHTS if name not in LARGE and name != "conv_w"}
    sq_cols, grad_x, grads = _step(x[0], loss_target[0], small, full, conv_w_mine, place)
    loss = 0.5 / D_MODEL * jnp.sum(sq_cols)
    grads["conv_w"] = lax.dynamic_slice_in_dim(grads["conv_w"], chip * conv_cols, conv_cols, axis=1)

    delta, new_m, new_v = {}, {}, {}
    for name in LARGE:
        delta[name], new_m[name], new_v[name] = _adamw_large(w[name][0], grads[name], m[name][0], v[name][0], name)
    small_names = [name for name in WEIGHTS if name not in LARGE]
    flat = lambda d: [d[name].reshape(grads[name].shape) for name in small_names]
    ds, mo, vo = _adamw_small(flat(w), [grads[name] for name in small_names], flat(m), flat(v))
    for k, name in enumerate(small_names):
        delta[name], new_m[name], new_v[name] = ds[k], mo[k], vo[k]

    shaped = lambda d: [d[name].reshape(w[name].shape) for name in WEIGHTS]
    return (loss, grad_x[None], *shaped(grads), *shaped(delta), *shaped(new_m), *shaped(new_v))
```

```python
import functools
import math

import jax
import jax.numpy as jnp
from jax import lax
from jax.experimental import pallas as pl
from jax.experimental.pallas import tpu as pltpu

F32 = jnp.float32
BF16 = jnp.bfloat16

D_MODEL = 1024
D_POOL = 512
N_POOL_GROUPS = 4
D_RNN = 1024
N_RNN_HEADS = 8
HEAD = 128
CONV_WIDTH = 4
LRU_C = 8.0
D_FF = 2816
D_IN = D_POOL + 2 * D_RNN + 2 * D_MODEL
NORM_EPS = 1e-6
COL_RNN = D_POOL // HEAD
COL_GATE = (D_POOL + D_RNN) // HEAD

ADAM_LR = 0.001
ADAM_B1 = 0.9
ADAM_B2 = 0.999
ADAM_EPS = 1e-08
ADAM_WD = 0.01
ADAM_STEP = 10

N_CHIPS = 4
N_DEV = 8
MESH = pl.DeviceIdType.MESH
ANY = pl.BlockSpec(memory_space=pl.ANY)
VMEM = pl.BlockSpec(memory_space=pltpu.VMEM)
VMEM_LIMIT_BYTES = 60 * 1024 * 1024
SUBLANES = 8
POOL_HALO = 16
CHUNK = 1024

GELU_C = math.sqrt(2.0 / math.pi)
GELU_A = 0.044715


def _params(**kw):
    return pltpu.CompilerParams(vmem_limit_bytes=VMEM_LIMIT_BYTES, **kw)


def _sigmoid(x):
    return 0.5 * jnp.tanh(0.5 * x) + 0.5


def _log1p(y):
    u = 1.0 + y
    d = u - 1.0
    return jnp.where(d == 0.0, y, jnp.log(u) * (y / jnp.where(d == 0.0, 1.0, d)))


def _gelu_parts(x):
    x2 = x * x
    th = jnp.tanh(GELU_C * (x + GELU_A * x * x2))
    g = 0.5 * x * (1.0 + th)
    dg = 0.5 * (1.0 + th) + 0.5 * x * (1.0 - th * th) * GELU_C * (1.0 + 3.0 * GELU_A * x2)
    return g, dg


def _dot(a, b):
    return jnp.dot(a, b, preferred_element_type=F32)


def _dot_nt(a, b):
    return lax.dot_general(a, b, (((1,), (1,)), ((), ())), preferred_element_type=F32)


def _dot_tn(a, b):
    return lax.dot_general(a, b, (((0,), (0,)), ((), ())), preferred_element_type=F32)


def _rms_scale(xv):
    return lax.rsqrt(jnp.mean(xv * xv, axis=-1, keepdims=True) + NORM_EPS)


def _rms_bwd(dy, xv, g):
    r = _rms_scale(xv)
    xh = xv * r
    dyg = dy * g
    dx = r * (dyg - xh * jnp.mean(dyg * xh, axis=-1, keepdims=True))
    return dx, dy * xh


class _Job:
    def __init__(self, inputs, out_shapes, aliases, sem_shape, start, finish):
        self.inputs, self.out_shapes, self.aliases, self.sem_shape = list(inputs), list(out_shapes), dict(aliases), sem_shape
        self.start, self.finish = start, finish


def _pallas(body, operands, *, name, grid, in_specs, out_specs, out_shape, scratch_shapes=(), semantics=None, jobs=()):
    n_in, n_out, n_scr = len(in_specs), len(out_specs), len(scratch_shapes)
    job_in = [a for job in jobs for a in job.inputs]
    job_out = [s for job in jobs for s in job.out_shapes]
    aliases, i0, o0 = {}, n_in, n_out
    for job in jobs:
        aliases.update({i0 + i: o0 + o for i, o in job.aliases.items()})
        i0, o0 = i0 + len(job.inputs), o0 + len(job.out_shapes)

    def whole(*refs):
        ins, j_ins = refs[:n_in], refs[n_in:n_in + len(job_in)]
        outs = refs[n_in + len(job_in):][:n_out]
        j_outs = refs[n_in + len(job_in) + n_out:][:len(job_out)]
        rest = refs[n_in + len(job_in) + n_out + len(job_out):]
        scr, sems = rest[:n_scr], rest[n_scr:]

        def run(phase):
            i, o = 0, 0
            for k, job in enumerate(jobs):
                getattr(job, phase)(j_ins[i:i + len(job.inputs)], j_outs[o:o + len(job.out_shapes)], sems[2 * k], sems[2 * k + 1])
                i, o = i + len(job.inputs), o + len(job.out_shapes)

        def at(step_of, phase):
            if not jobs:
                return
            if not grid:
                run(phase)
                return
            cond = functools.reduce(jnp.logical_and, [pl.program_id(d) == step_of(d) for d in range(len(grid))])
            pl.when(cond)(functools.partial(run, phase))

        at(lambda d: 0, "start")
        body(*ins, *outs, *scr)
        at(lambda d: grid[d] - 1, "finish")

    res = pl.pallas_call(
        whole, name=name, grid=grid, in_specs=list(in_specs) + [ANY] * len(job_in), out_specs=list(out_specs) + [ANY] * len(job_out),
        out_shape=list(out_shape) + job_out, input_output_aliases=aliases,
        scratch_shapes=list(scratch_shapes) + [pltpu.SemaphoreType.DMA(job.sem_shape) for job in jobs for _ in range(2)],
        compiler_params=_params(dimension_semantics=semantics, has_side_effects=bool(jobs)),
    )(*operands, *job_in)
    per_job, o = [], n_out
    for job in jobs:
        per_job.append(res[o:o + len(job.out_shapes)])
        o += len(job.out_shapes)
    return res[:n_out], per_job


def _run_jobs(jobs, name):
    return _pallas(lambda: None, [], name=name, grid=(), in_specs=[], out_specs=[], out_shape=[], jobs=jobs)[1]


NORM_ROWS = 256


def _norm_rows(x_ref, g_ref, h_ref):
    g = g_ref[...]

    def rows(i, carry):
        r = pl.ds(pl.multiple_of(i * NORM_ROWS, NORM_ROWS), NORM_ROWS)
        xv = x_ref[r, :]
        h_ref[r, :] = (xv * _rms_scale(xv) * g).astype(BF16)
        return carry

    lax.fori_loop(0, x_ref.shape[0] // NORM_ROWS, rows, 0)


def _norm_matmul(x, g, w, *, tm, tn, name, jobs=()):
    T, K = x.shape
    N = w.shape[1]

    def body(x_ref, g_ref, w_ref, o_ref, h_ref):
        @pl.when(pl.program_id(1) == 0)
        def _():
            _norm_rows(x_ref, g_ref, h_ref)

        o_ref[...] = _dot(h_ref[...], w_ref[...])

    return _pallas(
        body, (x, g, w), name=name, grid=(T // tm, N // tn),
        in_specs=[pl.BlockSpec((tm, K), lambda i, j: (i, 0)), pl.BlockSpec((1, K), lambda i, j: (0, 0)),
                  pl.BlockSpec((K, tn), lambda i, j: (0, j))],
        out_specs=[pl.BlockSpec((tm, tn), lambda i, j: (i, j)), pl.BlockSpec((tm, K), lambda i, j: (i, 0))],
        out_shape=[jax.ShapeDtypeStruct((T, N), F32), jax.ShapeDtypeStruct((T, K), BF16)],
        semantics=("parallel", "arbitrary"), jobs=jobs)


def _ffn_in(x2, g, w, *, tm, tn):
    T, K = x2.shape
    nb = D_FF // tn

    def body(x_ref, g_ref, wg_ref, wu_ref, gate_ref, up_ref, act_ref, h_ref):
        @pl.when(pl.program_id(1) == 0)
        def _():
            _norm_rows(x_ref, g_ref, h_ref)

        h = h_ref[...]
        gate = _dot(h, wg_ref[...])
        up = _dot(h, wu_ref[...])
        gate_ref[...] = gate.astype(BF16)
        up_ref[...] = up.astype(BF16)
        act_ref[...] = (gate * _sigmoid(gate) * up).astype(BF16)

    blk = pl.BlockSpec((tm, tn), lambda i, j: (i, j))
    return pl.pallas_call(
        body, name="ffn_in", grid=(T // tm, nb),
        in_specs=[pl.BlockSpec((tm, K), lambda i, j: (i, 0)), pl.BlockSpec((1, K), lambda i, j: (0, 0)),
                  pl.BlockSpec((K, tn), lambda i, j: (0, j)), pl.BlockSpec((K, tn), lambda i, j: (0, j + nb))],
        out_specs=[blk, blk, blk, pl.BlockSpec((tm, K), lambda i, j: (i, 0))],
        out_shape=[jax.ShapeDtypeStruct((T, D_FF), BF16), jax.ShapeDtypeStruct((T, D_FF), BF16),
                   jax.ShapeDtypeStruct((T, D_FF), BF16), jax.ShapeDtypeStruct((T, K), BF16)],
        compiler_params=_params(dimension_semantics=("parallel", "arbitrary")),
    )(x2, g, w, w)


def _branch_mix(pm, z, w_pool_out, w_rnn_out, proj, *, tm, tn):
    T = pm.shape[0]
    col_gp = (D_POOL + 2 * D_RNN) // tn
    col_gr = col_gp + D_MODEL // tn

    def body(pm_ref, z_ref, wp_ref, wr_ref, gp_ref, gr_ref, yp_ref, yr_ref, mix_ref):
        yp = _dot(pm_ref[...], wp_ref[...])
        yr = _dot(z_ref[...], wr_ref[...])
        yp_ref[...] = yp.astype(BF16)
        yr_ref[...] = yr.astype(BF16)
        mix_ref[...] = (_sigmoid(gp_ref[...]) * yp + _sigmoid(gr_ref[...]) * yr).astype(BF16)

    blk = pl.BlockSpec((tm, tn), lambda i, j: (i, j))
    return pl.pallas_call(
        body, name="branch_mix", grid=(T // tm, D_MODEL // tn),
        in_specs=[pl.BlockSpec((tm, D_POOL), lambda i, j: (i, 0)), pl.BlockSpec((tm, D_RNN), lambda i, j: (i, 0)),
                  pl.BlockSpec((D_POOL, tn), lambda i, j: (0, j)), pl.BlockSpec((D_RNN, tn), lambda i, j: (0, j)),
                  pl.BlockSpec((tm, tn), lambda i, j: (i, col_gp + j)), pl.BlockSpec((tm, tn), lambda i, j: (i, col_gr + j))],
        out_specs=[blk, blk, blk],
        out_shape=[jax.ShapeDtypeStruct((T, D_MODEL), BF16), jax.ShapeDtypeStruct((T, D_MODEL), BF16),
                   jax.ShapeDtypeStruct((T, D_MODEL), BF16)],
        compiler_params=_params(dimension_semantics=("parallel", "parallel")),
    )(pm, z, w_pool_out, w_rnn_out, proj, proj)


def _out_proj_residual(mix, w_o, x, *, tm):
    T = x.shape[0]

    def body(mix_ref, w_ref, x_ref, o_ref):
        o_ref[...] = x_ref[...] + _dot(mix_ref[...], w_ref[...])

    row = pl.BlockSpec((tm, D_MODEL), lambda i: (i, 0))
    return pl.pallas_call(
        body, name="out_proj_residual", grid=(T // tm,),
        in_specs=[row, pl.BlockSpec((D_MODEL, D_MODEL), lambda i: (0, 0)), row],
        out_specs=row, out_shape=jax.ShapeDtypeStruct((T, D_MODEL), F32),
        compiler_params=_params(dimension_semantics=("parallel",)),
    )(mix, w_o, x)


def _ffn_out_loss(act, w, x2, g3, target, *, tm):
    T = x2.shape[0]

    def body(act_ref, w_ref, x2_ref, g_ref, t_ref, dx_ref, dxb_ref, sq_ref, dg_ref):
        @pl.when(pl.program_id(0) == 0)
        def _():
            sq_ref[...] = jnp.zeros_like(sq_ref)
            dg_ref[...] = jnp.zeros_like(dg_ref)

        x3 = x2_ref[...] + _dot(act_ref[...], w_ref[...])
        g = g_ref[...]
        err = x3 * _rms_scale(x3) * g - t_ref[...]
        sq_ref[...] += jnp.sum(err * err, axis=0, keepdims=True)
        dx, dgp = _rms_bwd(err * (1.0 / D_MODEL), x3, g)
        dg_ref[...] += jnp.sum(dgp, axis=0, keepdims=True)
        dx_ref[...] = dx
        dxb_ref[...] = dx.astype(BF16)

    row = pl.BlockSpec((tm, D_MODEL), lambda i: (i, 0))
    vec = pl.BlockSpec((1, D_MODEL), lambda i: (0, 0))
    return pl.pallas_call(
        body, name="ffn_out_loss", grid=(T // tm,),
        in_specs=[pl.BlockSpec((tm, D_FF), lambda i: (i, 0)), pl.BlockSpec((D_FF, D_MODEL), lambda i: (0, 0)), row, vec, row],
        out_specs=[row, row, vec, vec],
        out_shape=[jax.ShapeDtypeStruct((T, D_MODEL), F32), jax.ShapeDtypeStruct((T, D_MODEL), BF16),
                   jax.ShapeDtypeStruct((1, D_MODEL), F32), jax.ShapeDtypeStruct((1, D_MODEL), F32)],
        compiler_params=_params(dimension_semantics=("arbitrary",)),
    )(act, w, x2, g3, target)


def _ffn_out_bwd(dx3b, w, gate, up, *, tm, tn):
    T = dx3b.shape[0]

    def body(dx_ref, w_ref, gate_ref, up_ref, dgate_ref, dup_ref):
        dact = _dot_nt(dx_ref[...], w_ref[...])
        gate = gate_ref[...].astype(F32)
        s = _sigmoid(gate)
        dgate_ref[...] = (dact * up_ref[...].astype(F32) * s * (1.0 + gate * (1.0 - s))).astype(BF16)
        dup_ref[...] = (dact * gate * s).astype(BF16)

    blk = pl.BlockSpec((tm, tn), lambda i, j: (i, j))
    return pl.pallas_call(
        body, name="ffn_out_bwd", grid=(T // tm, D_FF // tn),
        in_specs=[pl.BlockSpec((tm, D_MODEL), lambda i, j: (i, 0)), pl.BlockSpec((tn, D_MODEL), lambda i, j: (j, 0)), blk, blk],
        out_specs=[blk, blk],
        out_shape=[jax.ShapeDtypeStruct((T, D_FF), BF16), jax.ShapeDtypeStruct((T, D_FF), BF16)],
        compiler_params=_params(dimension_semantics=("parallel", "parallel")),
    )(dx3b, w, gate, up)


def _ffn_in_bwd(dgate, dup, w, dx3, x2, g2, *, tm):
    T = x2.shape[0]

    def body(dgate_ref, dup_ref, w_ref, dx3_ref, x2_ref, g_ref, dx_ref, dxb_ref, dg_ref):
        @pl.when(pl.program_id(0) == 0)
        def _():
            dg_ref[...] = jnp.zeros_like(dg_ref)

        dh = _dot_nt(dgate_ref[...], w_ref[:, :D_FF]) + _dot_nt(dup_ref[...], w_ref[:, D_FF:])
        dxn, dgp = _rms_bwd(dh, x2_ref[...], g_ref[...])
        dx = dx3_ref[...] + dxn
        dg_ref[...] += jnp.sum(dgp, axis=0, keepdims=True)
        dx_ref[...] = dx
        dxb_ref[...] = dx.astype(BF16)

    row = pl.BlockSpec((tm, D_MODEL), lambda i: (i, 0))
    wide = pl.BlockSpec((tm, D_FF), lambda i: (i, 0))
    vec = pl.BlockSpec((1, D_MODEL), lambda i: (0, 0))
    return pl.pallas_call(
        body, name="ffn_in_bwd", grid=(T // tm,),
        in_specs=[wide, wide, pl.BlockSpec((D_MODEL, 2 * D_FF), lambda i: (0, 0)), row, row, vec],
        out_specs=[row, row, vec],
        out_shape=[jax.ShapeDtypeStruct((T, D_MODEL), F32), jax.ShapeDtypeStruct((T, D_MODEL), BF16),
                   jax.ShapeDtypeStruct((1, D_MODEL), F32)],
        compiler_params=_params(dimension_semantics=("arbitrary",)),
    )(dgate, dup, w, dx3, x2, g2)


def _out_proj_bwd(dx2b, w_o, proj, y_pool, y_rnn, *, tm, tn, jobs=()):
    T = dx2b.shape[0]
    col_gp = (D_POOL + 2 * D_RNN) // tn
    col_gr = col_gp + D_MODEL // tn

    def body(dx_ref, w_ref, gp_ref, gr_ref, yp_ref, yr_ref, dgp_ref, dgr_ref, dyp_ref, dyr_ref):
        dmix = _dot_nt(dx_ref[...], w_ref[...])
        sp = _sigmoid(gp_ref[...])
        sr = _sigmoid(gr_ref[...])
        dgp_ref[...] = (dmix * yp_ref[...].astype(F32) * sp * (1.0 - sp)).astype(BF16)
        dgr_ref[...] = (dmix * yr_ref[...].astype(F32) * sr * (1.0 - sr)).astype(BF16)
        dyp_ref[...] = (dmix * sp).astype(BF16)
        dyr_ref[...] = (dmix * sr).astype(BF16)

    blk = pl.BlockSpec((tm, tn), lambda i, j: (i, j))
    out = jax.ShapeDtypeStruct((T, D_MODEL), BF16)
    return _pallas(
        body, (dx2b, w_o, proj, proj, y_pool, y_rnn), name="out_proj_bwd", grid=(T // tm, D_MODEL // tn),
        in_specs=[pl.BlockSpec((tm, D_MODEL), lambda i, j: (i, 0)), pl.BlockSpec((tn, D_MODEL), lambda i, j: (j, 0)),
                  pl.BlockSpec((tm, tn), lambda i, j: (i, col_gp + j)), pl.BlockSpec((tm, tn), lambda i, j: (i, col_gr + j)), blk, blk],
        out_specs=[blk, blk, blk, blk], out_shape=[out, out, out, out], semantics=("parallel", "parallel"), jobs=jobs)


def _branch_bwd(dyp, dyr, w_pool_out, w_rnn_out, *, tm):
    T = dyp.shape[0]

    def body(dyp_ref, dyr_ref, wp_ref, wr_ref, dpm_ref, dz_ref):
        dpm_ref[...] = _dot_nt(dyp_ref[...], wp_ref[...])
        dz_ref[...] = _dot_nt(dyr_ref[...], wr_ref[...])

    row = pl.BlockSpec((tm, D_MODEL), lambda i: (i, 0))
    return pl.pallas_call(
        body, name="branch_bwd", grid=(T // tm,),
        in_specs=[row, row, pl.BlockSpec((D_POOL, D_MODEL), lambda i: (0, 0)), pl.BlockSpec((D_RNN, D_MODEL), lambda i: (0, 0))],
        out_specs=[pl.BlockSpec((tm, D_POOL), lambda i: (i, 0)), pl.BlockSpec((tm, D_RNN), lambda i: (i, 0))],
        out_shape=[jax.ShapeDtypeStruct((T, D_POOL), F32), jax.ShapeDtypeStruct((T, D_RNN), F32)],
        compiler_params=_params(dimension_semantics=("parallel",)),
    )(dyp, dyr, w_pool_out, w_rnn_out)


def _in_proj_bwd(segs, w, dx2, x, g1, *, tm, jobs=()):
    T = x.shape[0]
    widths = [s.shape[1] for s in segs]
    offs = [sum(widths[:k]) for k in range(len(widths))]
    n = len(segs)

    def body(*refs):
        seg_refs, (w_ref, dx2_ref, x_ref, g_ref, dx_ref, dg_ref) = refs[:n], refs[n:]

        @pl.when(pl.program_id(0) == 0)
        def _():
            dg_ref[...] = jnp.zeros_like(dg_ref)

        dh = _dot_nt(seg_refs[0][...], w_ref[:, offs[0]:offs[0] + widths[0]])
        for k in range(1, n):
            dh += _dot_nt(seg_refs[k][...], w_ref[:, offs[k]:offs[k] + widths[k]])
        dxn, dgp = _rms_bwd(dh, x_ref[...], g_ref[...])
        dg_ref[...] += jnp.sum(dgp, axis=0, keepdims=True)
        dx_ref[...] = dx2_ref[...] + dxn

    row = pl.BlockSpec((tm, D_MODEL), lambda i: (i, 0))
    vec = pl.BlockSpec((1, D_MODEL), lambda i: (0, 0))
    return _pallas(
        body, (*segs, w, dx2, x, g1), name="in_proj_bwd", grid=(T // tm,),
        in_specs=[pl.BlockSpec((tm, wd), lambda i: (i, 0)) for wd in widths]
        + [pl.BlockSpec((D_MODEL, D_IN), lambda i: (0, 0)), row, row, vec],
        out_specs=[row, vec],
        out_shape=[jax.ShapeDtypeStruct((T, D_MODEL), F32), jax.ShapeDtypeStruct((1, D_MODEL), F32)],
        semantics=("arbitrary",), jobs=jobs)


def _weight_grad(a, segs, *, tm, tn, name, jobs=None):
    T, M = a.shape
    nblk = [s.shape[1] // tn for s in segs]
    first = [sum(nblk[:k]) for k in range(len(segs))]
    n = len(segs)

    def body(a_ref, *refs):
        seg_refs, o_ref = refs[:n], refs[n]
        j = pl.program_id(1)
        for k in range(n):
            @pl.when((j >= first[k]) & (j < first[k] + nblk[k]))
            def _(k=k):
                o_ref[...] = _dot_tn(a_ref[...], seg_refs[k][...])

    def seg_spec(k):
        return pl.BlockSpec((T, tn), lambda i, j: (0, jnp.clip(j - first[k], 0, nblk[k] - 1)))

    (grad,), results = _pallas(
        body, (a, *segs), name=name, grid=(M // tm, sum(nblk)),
        in_specs=[pl.BlockSpec((T, tm), lambda i, j: (0, i))] + [seg_spec(k) for k in range(n)],
        out_specs=[pl.BlockSpec((tm, tn), lambda i, j: (i, j))],
        out_shape=[jax.ShapeDtypeStruct((M, sum(nblk) * tn), F32)],
        semantics=("parallel", "arbitrary"), jobs=jobs or ())
    return grad if jobs is None else (grad, results)


def _pad_front(dst, src, halo):
    dst[pl.ds(0, halo), :] = jnp.zeros((halo, src.shape[1]), F32)

    def fill(i, carry):
        r0 = pl.multiple_of(i * CHUNK, CHUNK)
        dst[pl.ds(r0 + halo, CHUNK), :] = src[pl.ds(r0, CHUNK), :]
        return carry

    lax.fori_loop(0, src.shape[0] // CHUNK, fill, 0)


def _shift_rows(v, k):
    return pltpu.roll(v, k % v.shape[0], axis=0)


def _window_sums(xs, direction):
    s2 = xs + _shift_rows(xs, direction)
    s4 = s2 + _shift_rows(s2, 2 * direction)
    s8 = s4 + _shift_rows(s4, 4 * direction)
    s16 = s8 + _shift_rows(s8, 8 * direction)
    return s2, s4, s8, s16


def _select_window(g, sums):
    s2, s4, s8, s16 = sums
    return jnp.where(g == 0, s2, jnp.where(g == 1, s4, jnp.where(g == 2, s8, s16)))


def _pool_count(g, start, rows):
    t = start + lax.broadcasted_iota(jnp.int32, (rows, 1), 0)
    return jnp.minimum(t + 1, jnp.left_shift(2, g)).astype(F32)


def _pool_fwd(proj, w_grp, scale):
    T = proj.shape[0]
    nchunk = T // CHUNK

    def body(u_ref, w_ref, s_ref, o_ref, upad):
        g = pl.program_id(0)
        _pad_front(upad, u_ref, POOL_HALO)
        w = w_ref[...].astype(BF16)
        scale_row = s_ref[...]

        def chunk(i, carry):
            r0 = pl.multiple_of(i * CHUNK, CHUNK)
            xs = upad[pl.ds(r0, CHUNK + POOL_HALO), :]
            win = _select_window(g, _window_sums(xs, 1))[POOL_HALO:]
            pooled = win / _pool_count(g, r0, CHUNK) - xs[POOL_HALO:]
            o_ref[pl.ds(r0, CHUNK), :] = (_dot(pooled.astype(BF16), w) * scale_row).astype(BF16)
            return carry

        lax.fori_loop(0, nchunk, chunk, 0)

    return pl.pallas_call(
        body, name="pool_fwd", grid=(N_POOL_GROUPS,),
        in_specs=[pl.BlockSpec((T, HEAD), lambda g: (0, g)), pl.BlockSpec((None, HEAD, HEAD), lambda g: (g, 0, 0)),
                  pl.BlockSpec((1, HEAD), lambda g: (0, g))],
        out_specs=pl.BlockSpec((T, HEAD), lambda g: (0, g)),
        out_shape=jax.ShapeDtypeStruct((T, D_POOL), BF16),
        scratch_shapes=[pltpu.VMEM((T + POOL_HALO, HEAD), F32)],
        compiler_params=_params(dimension_semantics=("parallel",)),
    )(proj, w_grp, scale)


def _pool_bwd(proj, dpm, w_grp, scale, jobs=()):
    T = proj.shape[0]
    nchunk = T // CHUNK

    def body(u_ref, dpm_ref, w_ref, s_ref, du_ref, dw_ref, ds_ref, upad, zpad, dpool):
        g = pl.program_id(0)
        _pad_front(upad, u_ref, POOL_HALO)
        zpad[pl.ds(T, POOL_HALO), :] = jnp.zeros((POOL_HALO, HEAD), F32)
        dw_ref[...] = jnp.zeros_like(dw_ref)
        ds_ref[...] = jnp.zeros_like(ds_ref)
        w = w_ref[...].astype(BF16)
        scale_row = s_ref[...]

        def chunk(i, carry):
            r0 = pl.multiple_of(i * CHUNK, CHUNK)
            xs = upad[pl.ds(r0, CHUNK + POOL_HALO), :]
            cnt = _pool_count(g, r0, CHUNK)
            pooled = (_select_window(g, _window_sums(xs, 1))[POOL_HALO:] / cnt - xs[POOL_HALO:]).astype(BF16)
            mixed = _dot(pooled, w)
            d = dpm_ref[pl.ds(r0, CHUNK), :]
            ds_ref[...] += jnp.sum(d * mixed, axis=0, keepdims=True)
            dmixed = (d * scale_row).astype(BF16)
            dw_ref[...] += _dot_tn(pooled, dmixed)
            dp = _dot_nt(dmixed, w)
            dpool[pl.ds(r0, CHUNK), :] = dp
            zpad[pl.ds(r0, CHUNK), :] = dp / cnt
            return carry

        lax.fori_loop(0, nchunk, chunk, 0)

        def chunk2(i, carry):
            r0 = pl.multiple_of(i * CHUNK, CHUNK)
            zs = zpad[pl.ds(r0, CHUNK + POOL_HALO), :]
            win = _select_window(g, _window_sums(zs, -1))[:CHUNK]
            du_ref[pl.ds(r0, CHUNK), :] = (win - dpool[pl.ds(r0, CHUNK), :]).astype(BF16)
            return carry

        lax.fori_loop(0, nchunk, chunk2, 0)

    col = pl.BlockSpec((T, HEAD), lambda g: (0, g))
    return _pallas(
        body, (proj, dpm, w_grp, scale), name="pool_bwd", grid=(N_POOL_GROUPS,),
        in_specs=[col, col, pl.BlockSpec((None, HEAD, HEAD), lambda g: (g, 0, 0)), pl.BlockSpec((1, HEAD), lambda g: (0, g))],
        out_specs=[col, pl.BlockSpec((None, HEAD, HEAD), lambda g: (g, 0, 0)), pl.BlockSpec((1, HEAD), lambda g: (0, g))],
        out_shape=[jax.ShapeDtypeStruct((T, D_POOL), BF16), jax.ShapeDtypeStruct((N_POOL_GROUPS, HEAD, HEAD), F32),
                   jax.ShapeDtypeStruct((1, D_POOL), F32)],
        scratch_shapes=[pltpu.VMEM((T + POOL_HALO, HEAD), F32), pltpu.VMEM((T + POOL_HALO, HEAD), F32), pltpu.VMEM((T, HEAD), F32)],
        semantics=("parallel",), jobs=jobs)


def _conv_taps(xs, cw):
    v = cw[CONV_WIDTH - 1] * xs[SUBLANES:]
    for k in range(CONV_WIDTH - 1):
        v += cw[k] * _shift_rows(xs, CONV_WIDTH - 1 - k)[SUBLANES:]
    return v


def _tap_rows(cw_ref):
    return [cw_ref[k:k + 1, :] for k in range(CONV_WIDTH)]


def _softplus_neg(lam):
    return jnp.maximum(-lam, 0.0) + _log1p(jnp.exp(-jnp.abs(lam)))


def _lru_gates(v, wa, ba, wx, bx, sp):
    vb = v.astype(BF16)
    ra = _sigmoid(_dot(vb, wa) + ba)
    ix = _sigmoid(_dot(vb, wx) + bx)
    log_a = -LRU_C * ra * sp
    a = jnp.exp(log_a)
    sq = jnp.sqrt(-jnp.tanh(log_a) * (a * a + 1.0))
    return ra, ix, a, sq


def _row_bcast(v, r):
    return jnp.broadcast_to(v[r:r + 1, :], v.shape)


TILE_BLOCK = 128


def _scan_in_tiles(coef, coef_shift, A_out, B, T, direction):
    order = list(range(SUBLANES)) if direction == 1 else list(range(SUBLANES - 1, -1, -1))
    tiles = min(TILE_BLOCK, T // SUBLANES)
    for base in range(0, T, tiles * SUBLANES):
        def rows(r, base=base):
            return pl.ds(base + r, tiles, stride=SUBLANES)

        A, Bv = coef[rows(order[0] + coef_shift), :], B[rows(order[0]), :]
        A_out[rows(order[0]), :] = A
        for r in order[1:]:
            a = coef[rows(r + coef_shift), :]
            Bv = a * Bv + B[rows(r), :]
            A = a * A
            A_out[rows(r), :] = A
            B[rows(r), :] = Bv


TILES_PER_STEP = 8


def _carry_tiles(A_s, B_s, out, ntile, direction):
    out_row = SUBLANES - 1 if direction == 1 else 0

    def step(k, carry):
        for j in range(TILES_PER_STEP):
            t = k * TILES_PER_STEP + j
            r0 = pl.multiple_of((t if direction == 1 else ntile - 1 - t) * SUBLANES, SUBLANES)
            A, B = A_s[pl.ds(r0, SUBLANES), :], B_s[pl.ds(r0, SUBLANES), :]
            out[pl.ds(r0, SUBLANES), :] = A * carry + B
            carry = _row_bcast(A, out_row) * carry + _row_bcast(B, out_row)
        return carry

    lax.fori_loop(0, ntile // TILES_PER_STEP, step, jnp.zeros((SUBLANES, HEAD), F32))


def _rnn_fwd(proj, conv_w, conv_b, w_a, b_a, w_x, b_x, lam, jobs=()):
    T = proj.shape[0]
    nchunk = T // CHUNK
    ntile = T // SUBLANES

    def body(u_ref, ug_ref, cw_ref, cb_ref, wa_ref, ba_ref, wx_ref, bx_ref, lam_ref, h_ref, z_ref, upad, a_s, b_s):
        _pad_front(upad, u_ref, SUBLANES)
        cw, cb = _tap_rows(cw_ref), cb_ref[...]
        wa, wx = wa_ref[...].astype(BF16), wx_ref[...].astype(BF16)
        ba, bx = ba_ref[...], bx_ref[...]
        sp = _softplus_neg(lam_ref[...])

        def chunk(i, carry):
            r0 = pl.multiple_of(i * CHUNK, CHUNK)
            v = _conv_taps(upad[pl.ds(r0, CHUNK + SUBLANES), :], cw) + cb
            _, ix, a, sq = _lru_gates(v, wa, ba, wx, bx, sp)
            a_s[pl.ds(r0, CHUNK), :], b_s[pl.ds(r0, CHUNK), :] = a, sq * ix * v
            return carry

        lax.fori_loop(0, nchunk, chunk, 0)
        _scan_in_tiles(a_s, 0, a_s, b_s, T, 1)
        _carry_tiles(a_s, b_s, h_ref, ntile, 1)

        def chunk3(i, carry):
            r0 = pl.multiple_of(i * CHUNK, CHUNK)
            gl, _ = _gelu_parts(ug_ref[pl.ds(r0, CHUNK), :])
            z_ref[pl.ds(r0, CHUNK), :] = (h_ref[pl.ds(r0, CHUNK), :] * gl).astype(BF16)
            return carry

        lax.fori_loop(0, nchunk, chunk3, 0)

    col = pl.BlockSpec((T, HEAD), lambda h: (0, h))
    vec = pl.BlockSpec((1, HEAD), lambda h: (0, h))
    mat = pl.BlockSpec((None, HEAD, HEAD), lambda h: (h, 0, 0))
    return _pallas(
        body, (proj, proj, conv_w, conv_b, w_a, b_a, w_x, b_x, lam), name="rnn_fwd", grid=(N_RNN_HEADS,),
        in_specs=[pl.BlockSpec((T, HEAD), lambda h: (0, COL_RNN + h)), pl.BlockSpec((T, HEAD), lambda h: (0, COL_GATE + h)),
                  pl.BlockSpec((CONV_WIDTH, HEAD), lambda h: (0, h)), vec, mat, vec, mat, vec, vec],
        out_specs=[col, col],
        out_shape=[jax.ShapeDtypeStruct((T, D_RNN), F32), jax.ShapeDtypeStruct((T, D_RNN), BF16)],
        scratch_shapes=[pltpu.VMEM((T + SUBLANES, HEAD), F32), pltpu.VMEM((T, HEAD), F32), pltpu.VMEM((T, HEAD), F32)],
        semantics=("parallel",), jobs=jobs)


def _rnn_bwd(proj, hr, dz, conv_w, conv_b, w_a, b_a, w_x, b_x, lam, jobs=()):
    T = proj.shape[0]
    nchunk = T // CHUNK
    ntile = T // SUBLANES

    def body(u_ref, ug_ref, h_ref, dz_ref, cw_ref, cb_ref, wa_ref, ba_ref, wx_ref, bx_ref, lam_ref,
             du_ref, dug_ref, dwa_ref, dwx_ref, dba_ref, dbx_ref, dlam_ref, dcb_ref, dcw_ref,
             upad, hpad, apad, v_s, ra_s, ix_s, sq_s, g_s, dvpad, ga_s):
        zero_tile = jnp.zeros((SUBLANES, HEAD), F32)
        _pad_front(upad, u_ref, SUBLANES)
        _pad_front(hpad, h_ref, SUBLANES)
        apad[pl.ds(T, SUBLANES), :] = zero_tile
        dvpad[pl.ds(T, SUBLANES), :] = zero_tile
        for ref in (dwa_ref, dwx_ref, dba_ref, dbx_ref, dlam_ref, dcb_ref, dcw_ref):
            ref[...] = jnp.zeros_like(ref)
        cw, cb = _tap_rows(cw_ref), cb_ref[...]
        wa, wx = wa_ref[...].astype(BF16), wx_ref[...].astype(BF16)
        ba, bx = ba_ref[...], bx_ref[...]
        lam_row = lam_ref[...]
        sp = _softplus_neg(lam_row)

        def chunk(i, carry):
            r0 = pl.multiple_of(i * CHUNK, CHUNK)
            rows = pl.ds(r0, CHUNK)
            h = h_ref[rows, :]
            v = _conv_taps(upad[pl.ds(r0, CHUNK + SUBLANES), :], cw) + cb
            ra, ix, a, sq = _lru_gates(v, wa, ba, wx, bx, sp)
            v_s[rows, :], ra_s[rows, :], ix_s[rows, :], sq_s[rows, :], apad[rows, :] = v, ra, ix, sq, a
            gl, dgl = _gelu_parts(ug_ref[rows, :])
            d = dz_ref[rows, :]
            g_s[rows, :] = d * gl
            dug_ref[rows, :] = (d * h * dgl).astype(BF16)
            return carry

        lax.fori_loop(0, nchunk, chunk, 0)

        _scan_in_tiles(apad, 1, ga_s, g_s, T, -1)
        _carry_tiles(ga_s, g_s, g_s, ntile, -1)

        def chunk3(i, carry):
            r0 = pl.multiple_of(i * CHUNK, CHUNK)
            rows = pl.ds(r0, CHUNK)
            g = g_s[rows, :]
            h_prev = _shift_rows(hpad[pl.ds(r0, CHUNK + SUBLANES), :], 1)[SUBLANES:]
            v, ra, ix, sq, a = v_s[rows, :], ra_s[rows, :], ix_s[rows, :], sq_s[rows, :], apad[rows, :]
            d_sq = g * ix * v
            d_ix = g * sq * v
            d_la = a * g * h_prev - d_sq * a * a / sq
            dlam_ref[...] += jnp.sum(d_la * ra, axis=0, keepdims=True)
            d_pa = d_la * (-LRU_C) * sp * ra * (1.0 - ra)
            d_px = d_ix * ix * (1.0 - ix)
            vb, d_pab, d_pxb = v.astype(BF16), d_pa.astype(BF16), d_px.astype(BF16)
            dwa_ref[...] += _dot_tn(vb, d_pab)
            dwx_ref[...] += _dot_tn(vb, d_pxb)
            dba_ref[...] += jnp.sum(d_pa, axis=0, keepdims=True)
            dbx_ref[...] += jnp.sum(d_px, axis=0, keepdims=True)
            dv = g * sq * ix + _dot_nt(d_pab, wa) + _dot_nt(d_pxb, wx)
            dvpad[rows, :] = dv
            dcb_ref[...] += jnp.sum(dv, axis=0, keepdims=True)
            xs = upad[pl.ds(r0, CHUNK + SUBLANES), :]
            for k in range(CONV_WIDTH):
                u_k = _shift_rows(xs, CONV_WIDTH - 1 - k)[SUBLANES:] if k < CONV_WIDTH - 1 else xs[SUBLANES:]
                dcw_ref[k:k + 1, :] += jnp.sum(dv * u_k, axis=0, keepdims=True)
            return carry

        lax.fori_loop(0, nchunk, chunk3, 0)
        dlam_ref[...] = dlam_ref[...] * (LRU_C * _sigmoid(-lam_row))

        def chunk4(i, carry):
            r0 = pl.multiple_of(i * CHUNK, CHUNK)
            dvs = dvpad[pl.ds(r0, CHUNK + SUBLANES), :]
            du = cw[CONV_WIDTH - 1] * dvs[:CHUNK]
            for k in range(CONV_WIDTH - 1):
                du += cw[k] * _shift_rows(dvs, -(CONV_WIDTH - 1 - k))[:CHUNK]
            du_ref[pl.ds(r0, CHUNK), :] = du.astype(BF16)
            return carry

        lax.fori_loop(0, nchunk, chunk4, 0)

    col = pl.BlockSpec((T, HEAD), lambda h: (0, h))
    vec = pl.BlockSpec((1, HEAD), lambda h: (0, h))
    mat = pl.BlockSpec((None, HEAD, HEAD), lambda h: (h, 0, 0))
    taps = pl.BlockSpec((CONV_WIDTH, HEAD), lambda h: (0, h))
    vec_out = jax.ShapeDtypeStruct((1, D_RNN), F32)
    mat_out = jax.ShapeDtypeStruct((N_RNN_HEADS, HEAD, HEAD), F32)
    seq = pltpu.VMEM((T, HEAD), F32)
    seq_pad = pltpu.VMEM((T + SUBLANES, HEAD), F32)
    return _pallas(
        body, (proj, proj, hr, dz, conv_w, conv_b, w_a, b_a, w_x, b_x, lam), name="rnn_bwd", grid=(N_RNN_HEADS,),
        in_specs=[pl.BlockSpec((T, HEAD), lambda h: (0, COL_RNN + h)), pl.BlockSpec((T, HEAD), lambda h: (0, COL_GATE + h)),
                  col, col, taps, vec, mat, vec, mat, vec, vec],
        out_specs=[col, col, mat, mat, vec, vec, vec, vec, taps],
        out_shape=[jax.ShapeDtypeStruct((T, D_RNN), BF16), jax.ShapeDtypeStruct((T, D_RNN), BF16), mat_out, mat_out,
                   vec_out, vec_out, vec_out, vec_out, jax.ShapeDtypeStruct((CONV_WIDTH, D_RNN), F32)],
        scratch_shapes=[seq_pad, seq_pad, seq_pad, seq, seq, seq, seq, seq, seq_pad, seq],
        semantics=("parallel",), jobs=jobs)


GROUP_FFN = ["w_ffn_out", "w_ffn_in"]
GROUP_MIX = ["w_o", "w_pool_out", "w_rnn_out"]
GROUP_IN = ["w_in"]


def _step(x, target, s, full, conv_w_mine, place):
    T = x.shape[0]
    tall, mid, low = min(T, 2048), min(T, 1024), min(T, 512)
    c1 = place[1:]
    full = dict(full)

    def gathered(names, results):
        full.update(zip(names, results))

    (full["w_in"], conv_w), = _run_jobs([_gather_job(full, ["w_in"], conv_w_mine)], "gather_w_in")
    early = ["w_pool_out", "w_rnn_out", "w_o", "w_ffn_out"]
    (proj, h1), (res,) = _norm_matmul(x, s["norm_mix"], full["w_in"], tm=tall, tn=512, name="in_proj", jobs=[_gather_job(full, early)])
    gathered(early, res)
    pm = _pool_fwd(proj, s["w_pool_grp"], s["pool_scale"])
    (hr, z), (res,) = _rnn_fwd(proj, conv_w, s["conv_b"], s["w_rg_a"], s["b_rg_a"], s["w_rg_x"], s["b_rg_x"], s["lru_lambda"],
                               jobs=[_gather_job(full, ["w_ffn_in"])])
    gathered(["w_ffn_in"], res)
    y_pool, y_rnn, mix = _branch_mix(pm, z, full["w_pool_out"], full["w_rnn_out"], proj, tm=tall, tn=256)
    x2 = _out_proj_residual(mix, full["w_o"], x, tm=mid)
    gate, up, act, h2 = _ffn_in(x2, s["norm_ffn"], full["w_ffn_in"], tm=tall, tn=256)
    dx3, dx3b, sq_cols, g_norm_final = _ffn_out_loss(act, full["w_ffn_out"], x2, s["norm_final"], target, tm=low)

    g = {"norm_final": g_norm_final}

    def chip_sums(names, from_sibling):
        sums = {name: _chip_sum(name, g[name], got, c1) for name, got in zip(names, from_sibling)}
        return {name: v[0] for name, v in sums.items()}, {name: v[1] for name, v in sums.items()}

    def final_sums(names, sums, from_chips):
        return {name: _final_sum(name, sums[name], got, place) for name, got in zip(names, from_chips)}

    dgate, dup = _ffn_out_bwd(dx3b, full["w_ffn_out"], gate, up, tm=tall, tn=256)
    g["w_ffn_out"] = _weight_grad(act, [dx3b], tm=256, tn=D_MODEL, name="w_ffn_out_grad")
    dx2, dx2b, g["norm_ffn"] = _ffn_in_bwd(dgate, dup, full["w_ffn_in"], dx3, x2, s["norm_ffn"], tm=low)
    g["w_ffn_in"] = _weight_grad(h2, [dgate, dup], tm=D_MODEL, tn=256, name="w_ffn_in_grad")
    (dgp, dgr, dyp, dyr), (res,) = _out_proj_bwd(dx2b, full["w_o"], proj, y_pool, y_rnn, tm=tall, tn=256,
                                                 jobs=[_sibling_job(g, GROUP_FFN)])
    sums_ffn, sums_ffn_bf16 = chip_sums(GROUP_FFN, res)
    g["w_o"] = _weight_grad(mix, [dx2b], tm=D_MODEL, tn=256, name="w_o_grad")
    dpm, dz = _branch_bwd(dyp, dyr, full["w_pool_out"], full["w_rnn_out"], tm=mid)
    g["w_pool_out"] = _weight_grad(pm, [dyp], tm=D_POOL, tn=256, name="w_pool_out_grad")
    g["w_rnn_out"] = _weight_grad(z, [dyr], tm=D_RNN, tn=256, name="w_rnn_out_grad")
    (dupool, g["w_pool_grp"], g["pool_scale"]), (res,) = _pool_bwd(proj, dpm, s["w_pool_grp"], s["pool_scale"],
                                                                   jobs=[_sibling_job(g, GROUP_MIX)])
    sums_mix, sums_mix_bf16 = chip_sums(GROUP_MIX, res)
    ((durnn, dugate, g["w_rg_a"], g["w_rg_x"], g["b_rg_a"], g["b_rg_x"], g["lru_lambda"], g["conv_b"], g["conv_w"]),
     (res,)) = _rnn_bwd(proj, hr, dz, conv_w, s["conv_b"], s["w_rg_a"], s["b_rg_a"], s["w_rg_x"], s["b_rg_x"], s["lru_lambda"],
                        jobs=[_chips_job(sums_ffn_bf16, GROUP_FFN)])
    shards = final_sums(GROUP_FFN, sums_ffn, res)
    segs = [dupool, durnn, dugate, dgp, dgr]
    g["w_in"], (res, joined) = _weight_grad(h1, segs, tm=D_MODEL, tn=256, name="w_in_grad",
                                           jobs=[_chips_job(sums_mix_bf16, GROUP_MIX), _join_job(shards, GROUP_FFN)])
    grads = dict(zip(GROUP_FFN, joined))
    shards = final_sums(GROUP_MIX, sums_mix, res)
    (res,) = _run_jobs([_sibling_job(g, GROUP_IN)], "w_in_exchange_sibling")
    sums_in, sums_in_bf16 = chip_sums(GROUP_IN, res)
    (grad_x, g["norm_mix"]), (res,) = _in_proj_bwd(segs, full["w_in"], dx2, x, s["norm_mix"], tm=low,
                                                  jobs=[_chips_job(sums_in_bf16, GROUP_IN)])
    shards.update(final_sums(GROUP_IN, sums_in, res))

    vec_rows = [g[name] if name != "pool_scale" else jnp.pad(g[name], ((0, 0), (0, D_MODEL - D_POOL))) for name in VEC_ITEMS]
    vec_rows += [g["conv_w"], sq_cols, jnp.zeros((VEC_ROWS - len(VEC_ITEMS) - CONV_WIDTH - 1, D_MODEL), F32)]
    vec = jnp.concatenate(vec_rows, axis=0).reshape(VEC_ROWS, N_DEV, HEAD).transpose(1, 0, 2)
    mat = jnp.concatenate([g[name].reshape(-1, HEAD) for name in MAT_ITEMS], axis=0).reshape(N_DEV, -1, HEAD)
    (vec, mat), (joined,) = _all_reduce_small([vec, mat], jobs=[_join_job(shards, GROUP_MIX + GROUP_IN)])
    grads.update(zip(GROUP_MIX + GROUP_IN, joined))

    vec = vec.transpose(1, 0, 2).reshape(VEC_ROWS, D_MODEL)
    mat = mat.reshape(-1, HEAD)
    for k, name in enumerate(VEC_ITEMS):
        grads[name] = vec[k:k + 1, :s[name].shape[1]]
    grads["conv_w"] = vec[len(VEC_ITEMS):len(VEC_ITEMS) + CONV_WIDTH]
    row = 0
    for name in MAT_ITEMS:
        rows = s[name].shape[0] * HEAD
        grads[name] = mat[row:row + rows]
        row += rows
    return vec[len(VEC_ITEMS) + CONV_WIDTH], grad_x, grads


LARGE = {"w_in": "col", "w_pool_out": "col", "w_rnn_out": "row", "w_o": "row", "w_ffn_in": "col", "w_ffn_out": "row"}
LARGE_SHAPE = {"w_in": (D_MODEL, D_IN), "w_pool_out": (D_POOL, D_MODEL), "w_rnn_out": (D_RNN, D_MODEL),
               "w_o": (D_MODEL, D_MODEL), "w_ffn_in": (D_MODEL, 2 * D_FF), "w_ffn_out": (D_FF, D_MODEL)}


def _place():
    x, y, c = lax.axis_index("x"), lax.axis_index("y"), lax.axis_index("c")
    return 2 * x + y, c


def _chip_device(chip, c):
    return (chip // 2, chip % 2, c)


def _chip_window(ref, kind, shape, chip, half=None):
    K, N = shape
    if kind == "col":
        rows = slice(None) if half is None else pl.ds(half * (K // 2), K // 2)
        return ref.at[rows, pl.ds(chip * (N // N_CHIPS), N // N_CHIPS)]
    ks = K // N_CHIPS
    if half is None:
        return ref.at[pl.ds(chip * ks, ks), :]
    return ref.at[pl.ds(chip * ks + half * (ks // 2), ks // 2), :]


def _row_half(ref, half):
    rows = ref.shape[0] // 2
    return ref.at[pl.ds(half * rows, rows), :]


def _remote(win_src, win_dst, send_sems, recv_sems, idx, to):
    return pltpu.make_async_remote_copy(src_ref=win_src, dst_ref=win_dst, send_sem=send_sems.at[idx], recv_sem=recv_sems.at[idx],
                                        device_id=to, device_id_type=MESH)


def _gather_job(full, names, conv_w_full=None):
    n = len(names)
    cw_cols = D_RNN // N_CHIPS

    def windows(refs, chip, half):
        return [_chip_window(refs[k], LARGE[name], LARGE_SHAPE[name], chip, half) for k, name in enumerate(names)]

    def ici_copies(refs, send_sems, recv_sems, src_chip, dst_chip, c, r):
        wins = windows(refs, src_chip, c)
        if conv_w_full is not None:
            wins.append(refs[n].at[:, pl.ds(src_chip * cw_cols, cw_cols)])
        return [_remote(win, win, send_sems, recv_sems, (k, r), _chip_device(dst_chip, c)) for k, win in enumerate(wins)]

    def forwards(refs, send_sems, recv_sems, src_chip, half, to_core, chip, r):
        return [_remote(win, win, send_sems, recv_sems, (k, 3 + r), _chip_device(chip, to_core))
                for k, win in enumerate(windows(refs, src_chip, half))]

    def start(ins, outs, send_sems, recv_sems):
        chip, c = _place()
        for r in range(3):
            for cp in ici_copies(outs, send_sems, recv_sems, chip, chip ^ (r + 1), c, r):
                cp.start()

    def finish(ins, outs, send_sems, recv_sems):
        chip, c = _place()
        for r in range(3):
            for cp in ici_copies(outs, send_sems, recv_sems, chip ^ (r + 1), chip, c, r):
                cp.wait_recv()
            for cp in forwards(outs, send_sems, recv_sems, chip ^ (r + 1), c, 1 - c, chip, r):
                cp.start()
        for r in range(3):
            for cp in forwards(outs, send_sems, recv_sems, chip ^ (r + 1), 1 - c, c, chip, r):
                cp.wait_recv()
            for cp in ici_copies(outs, send_sems, recv_sems, chip, chip ^ (r + 1), c, r):
                cp.wait_send()
            for cp in forwards(outs, send_sems, recv_sems, chip ^ (r + 1), c, 1 - c, chip, r):
                cp.wait_send()

    arrays = [full[name] for name in names] + ([conv_w_full] if conv_w_full is not None else [])
    return _Job(arrays, [jax.ShapeDtypeStruct(a.shape, a.dtype) for a in arrays], {k: k for k in range(len(arrays))},
                (len(arrays), 6), start, finish)


def _core_halves(ref, kind, shape, c):
    return [_chip_window(ref, kind, shape, chip, c) for chip in range(N_CHIPS)]


def _sibling_job(grads, names):
    def start(ins, outs, send_sems, recv_sems):
        chip, c = _place()
        for k, name in enumerate(names):
            kind, shape = LARGE[name], LARGE_SHAPE[name]
            if kind == "col":
                pairs = [(_row_half(ins[k], 1 - c), outs[k])]
            else:
                rows = shape[0] // N_DEV
                pairs = [(win, outs[k].at[pl.ds(j * rows, rows), :]) for j, win in enumerate(_core_halves(ins[k], kind, shape, 1 - c))]
            for src, dst in pairs:
                _remote(src, dst, send_sems, recv_sems, k, _chip_device(chip, 1 - c)).start()

    def finish(ins, outs, send_sems, recv_sems):
        chip, c = _place()
        for k in range(len(names)):
            _remote(outs[k], outs[k], send_sems, recv_sems, k, _chip_device(chip, 1 - c)).wait()

    return _Job([grads[name] for name in names],
                [jax.ShapeDtypeStruct((LARGE_SHAPE[name][0] // 2, LARGE_SHAPE[name][1]), F32) for name in names], {},
                (len(names),), start, finish)


def _chip_sum(name, g, got, c):
    kind, (K, N) = LARGE[name], LARGE_SHAPE[name]
    rows = K // N_DEV

    def body(c_ref, g_ref, got_ref, o_ref, ob_ref):
        total = g_ref[...] + got_ref[...]
        o_ref[...] = total
        ob_ref[...] = total.astype(BF16)

    if kind == "col":
        mine = pl.BlockSpec((rows, N), lambda j, c_ref: (j + N_CHIPS * c_ref[0], 0))
    else:
        mine = pl.BlockSpec((rows, N), lambda j, c_ref: (2 * j + c_ref[0], 0))
    blk = pl.BlockSpec((rows, N), lambda j, c_ref: (j, 0))
    return pl.pallas_call(
        body, name=name + "_chip_sum",
        grid_spec=pltpu.PrefetchScalarGridSpec(num_scalar_prefetch=1, grid=(N_CHIPS,), in_specs=[mine, blk], out_specs=[blk, blk]),
        out_shape=[jax.ShapeDtypeStruct((K // 2, N), F32), jax.ShapeDtypeStruct((K // 2, N), BF16)],
        compiler_params=_params(dimension_semantics=("parallel",)),
    )(c, g, got)


def _piece(ref, kind, shape, chip):
    K, N = shape
    if kind == "col":
        return ref.at[:, pl.ds(chip * (N // N_CHIPS), N // N_CHIPS)]
    return ref.at[pl.ds(chip * (K // N_DEV), K // N_DEV), :]


def _piece_shape(name):
    kind, (K, N) = LARGE[name], LARGE_SHAPE[name]
    return (K // 2, N // N_CHIPS) if kind == "col" else (K // N_DEV, N)


def _chips_job(sums, names):
    def copies(ins, outs, send_sems, recv_sems):
        chip, c = _place()
        return [_remote(_piece(ins[k], LARGE[name], LARGE_SHAPE[name], chip ^ (r + 1)), outs[k].at[r], send_sems, recv_sems, (k, r),
                        _chip_device(chip ^ (r + 1), c)) for k, name in enumerate(names) for r in range(3)]

    def start(*refs):
        for cp in copies(*refs):
            cp.start()

    def finish(*refs):
        for cp in copies(*refs):
            cp.wait()

    return _Job([sums[name] for name in names], [jax.ShapeDtypeStruct((3,) + _piece_shape(name), BF16) for name in names], {},
                (len(names), 3), start, finish)


def _final_sum(name, chip_sum, got, place):
    kind = LARGE[name]
    rows, cols = _piece_shape(name)

    def body(place_ref, s_ref, got_ref, o_ref):
        o_ref[...] = ((s_ref[...] + got_ref[0].astype(F32)) + got_ref[1].astype(F32)) + got_ref[2].astype(F32)

    if kind == "col":
        mine = pl.BlockSpec((rows, cols), lambda i, place_ref: (0, place_ref[0]))
    else:
        mine = pl.BlockSpec((rows, cols), lambda i, place_ref: (place_ref[0], 0))
    return pl.pallas_call(
        body, name=name + "_final_sum",
        grid_spec=pltpu.PrefetchScalarGridSpec(
            num_scalar_prefetch=1, grid=(1,), in_specs=[mine, pl.BlockSpec((3, rows, cols), lambda i, place_ref: (0, 0, 0))],
            out_specs=pl.BlockSpec((rows, cols), lambda i, place_ref: (place_ref[1], 0))),
        out_shape=jax.ShapeDtypeStruct((2 * rows, cols), F32),
        compiler_params=_params(dimension_semantics=("arbitrary",)),
    )(place, chip_sum, got)


def _join_job(shards, names):
    def half_copy(outs, send_sems, recv_sems, k, mine):
        chip, c = _place()
        win = _row_half(outs[k], c if mine else 1 - c)
        return _remote(win, win, send_sems, recv_sems, k, _chip_device(chip, 1 - c))

    def start(ins, outs, send_sems, recv_sems):
        for k in range(len(names)):
            half_copy(outs, send_sems, recv_sems, k, True).start()

    def finish(ins, outs, send_sems, recv_sems):
        for k in range(len(names)):
            half_copy(outs, send_sems, recv_sems, k, True).wait_send()
            half_copy(outs, send_sems, recv_sems, k, False).wait_recv()

    arrays = [shards[name] for name in names]
    return _Job(arrays, [jax.ShapeDtypeStruct(a.shape, F32) for a in arrays], {k: k for k in range(len(arrays))},
                (len(arrays),), start, finish)


VEC_ROWS = 16


def _all_reduce_small(slabs, jobs=()):
    n = len(slabs)

    def body(*refs):
        in_refs, out_refs, got_refs = refs[:n], refs[n:2 * n], refs[2 * n:3 * n]
        send_sems, recv_sems = refs[3 * n:]
        x, y, c = lax.axis_index("x"), lax.axis_index("y"), lax.axis_index("c")
        me = 4 * x + 2 * y + c

        def remote(src, dst, k, phase, r):
            other = me ^ r
            return pltpu.make_async_remote_copy(src_ref=src, dst_ref=dst, send_sem=send_sems.at[k, phase, r],
                                                recv_sem=recv_sems.at[k, phase, r],
                                                device_id=(other // 4, (other // 2) % 2, other % 2), device_id_type=MESH)

        scatter = [remote(in_refs[k].at[me ^ r], got_refs[k].at[r], k, 0, r) for r in range(1, N_DEV) for k in range(n)]
        for cp in scatter:
            cp.start()
        for cp in scatter:
            cp.wait()
        for k in range(n):
            total = in_refs[k][me]
            for r in range(1, N_DEV):
                total = total + got_refs[k][r]
            out_refs[k][me] = total
        gather = [remote(out_refs[k].at[me], out_refs[k].at[me], k, 1, r) for r in range(1, N_DEV) for k in range(n)]
        for cp in gather:
            cp.start()
        for r in range(1, N_DEV):
            for k in range(n):
                remote(out_refs[k].at[me ^ r], out_refs[k].at[me ^ r], k, 1, r).wait_recv()
        for cp in gather:
            cp.wait_send()

    return _pallas(
        body, slabs, name="all_reduce_small", grid=(), in_specs=[VMEM] * n, out_specs=[VMEM] * n,
        out_shape=[jax.ShapeDtypeStruct(s.shape, F32) for s in slabs],
        scratch_shapes=[pltpu.VMEM(s.shape, F32) for s in slabs]
        + [pltpu.SemaphoreType.DMA((n, 2, N_DEV)), pltpu.SemaphoreType.DMA((n, 2, N_DEV))], jobs=jobs)


def _cast_into_whole(w, name, place):
    rows, cols = w.shape
    tr = rows // 2

    def body(place_ref, w_ref, o_ref):
        o_ref[...] = w_ref[...].astype(BF16)

    if LARGE[name] == "col":
        window = pl.BlockSpec((tr, cols), lambda i, place_ref: (i, place_ref[0]))
    else:
        window = pl.BlockSpec((tr, cols), lambda i, place_ref: (2 * place_ref[0] + i, 0))
    return pl.pallas_call(
        body, name=name + "_cast",
        grid_spec=pltpu.PrefetchScalarGridSpec(num_scalar_prefetch=1, grid=(2,),
                                               in_specs=[pl.BlockSpec((tr, cols), lambda i, place_ref: (i, 0))], out_specs=window),
        out_shape=jax.ShapeDtypeStruct(LARGE_SHAPE[name], BF16),
        compiler_params=_params(dimension_semantics=("parallel",)))(place, w)


def _adamw_math(w, g, m, v):
    m = ADAM_B1 * m + (1.0 - ADAM_B1) * g
    v = ADAM_B2 * v + (1.0 - ADAM_B2) * (g * g)
    m_hat = m / (1.0 - ADAM_B1 ** ADAM_STEP)
    v_hat = v / (1.0 - ADAM_B2 ** ADAM_STEP)
    delta = -ADAM_LR * (m_hat / (jnp.sqrt(v_hat) + ADAM_EPS) + ADAM_WD * w)
    return delta, m, v


def _adamw_large(w, g, m, v, name):
    rows, cols = w.shape
    tr = rows // 4

    def body(w_ref, g_ref, m_ref, v_ref, d_ref, mo_ref, vo_ref):
        d_ref[...], mo_ref[...], vo_ref[...] = _adamw_math(w_ref[...], g_ref[...], m_ref[...], v_ref[...])

    blk = pl.BlockSpec((tr, cols), lambda i: (i, 0))
    out = jax.ShapeDtypeStruct(w.shape, F32)
    return pl.pallas_call(body, name=name + "_adamw", grid=(4,), in_specs=[blk] * 4, out_specs=[blk] * 3, out_shape=[out] * 3,
                          compiler_params=_params(dimension_semantics=("parallel",)))(w, g, m, v)


def _adamw_small(ws, gs, ms, vs):
    n = len(ws)

    def body(*refs):
        for k in range(n):
            w_ref, g_ref, m_ref, v_ref = (refs[q * n + k] for q in range(4))
            d_ref, mo_ref, vo_ref = (refs[(4 + q) * n + k] for q in range(3))
            d_ref[...], mo_ref[...], vo_ref[...] = _adamw_math(w_ref[...], g_ref[...], m_ref[...], v_ref[...])

    out = [jax.ShapeDtypeStruct(w.shape, F32) for w in ws]
    res = pl.pallas_call(body, name="small_adamw", in_specs=[VMEM] * (4 * n), out_specs=[VMEM] * (3 * n), out_shape=out * 3,
                         compiler_params=_params())(*ws, *gs, *ms, *vs)
    return res[:n], res[n:2 * n], res[2 * n:]


WEIGHTS = ["norm_mix", "w_in", "w_pool_grp", "pool_scale", "w_pool_out", "conv_w", "conv_b", "w_rg_a", "b_rg_a", "w_rg_x",
           "b_rg_x", "lru_lambda", "w_rnn_out", "w_o", "norm_ffn", "w_ffn_in", "w_ffn_out", "norm_final"]
VEC_ITEMS = ["norm_mix", "norm_ffn", "norm_final", "pool_scale", "conv_b", "lru_lambda", "b_rg_a", "b_rg_x"]
MAT_ITEMS = ["w_pool_grp", "w_rg_a", "w_rg_x"]


def _as2d(name, a):
    if name in MAT_ITEMS:
        return a.reshape(-1, HEAD, HEAD)
    if name == "conv_w":
        return a.reshape(CONV_WIDTH, -1)
    return a.reshape(1, -1)


def kernel(x, norm_mix, w_in, w_pool_grp, pool_scale, w_pool_out, conv_w, conv_b, w_rg_a, b_rg_a, w_rg_x, b_rg_x, lru_lambda, w_rnn_out, w_o, norm_ffn, w_ffn_in, w_ffn_out, norm_final, loss_target, m_norm_mix, m_w_in, m_w_pool_grp, m_pool_scale, m_w_pool_out, m_conv_w, m_conv_b, m_w_rg_a, m_b_rg_a, m_w_rg_x, m_b_rg_x, m_lru_lambda, m_w_rnn_out, m_w_o, m_norm_ffn, m_w_ffn_in, m_w_ffn_out, m_norm_final, v_norm_mix, v_w_in, v_w_pool_grp, v_pool_scale, v_w_pool_out, v_conv_w, v_conv_b, v_w_rg_a, v_b_rg_a, v_w_rg_x, v_b_rg_x, v_lru_lambda, v_w_rnn_out, v_w_o, v_norm_ffn, v_w_ffn_in, v_w_ffn_out, v_norm_final):
    given = dict(locals())
    w = {name: given[name] for name in WEIGHTS}
    m = {name: given["m_" + name] for name in WEIGHTS}
    v = {name: given["v_" + name] for name in WEIGHTS}
    chip, c = _place()

    place = jnp.stack([chip, c]).astype(jnp.int32)
    conv_cols = w["conv_w"].shape[-1]
    conv_w_mine = lax.dynamic_update_slice_in_dim(jnp.zeros((CONV_WIDTH, D_RNN), F32), w["conv_w"][0], chip * conv_cols, axis=1)
    full = {name: _cast_into_whole(w[name][0], name, place) for name in LARGE}
    small = {name: _as2d(name, w[name]) for name in WEIGHTS if name not in LARGE and name != "conv_w"}
    sq_cols, grad_x, grads = _step(x[0], loss_target[0], small, full, conv_w_mine, place)
    loss = 0.5 / D_MODEL * jnp.sum(sq_cols)
    grads["conv_w"] = lax.dynamic_slice_in_dim(grads["conv_w"], chip * conv_cols, conv_cols, axis=1)

    delta, new_m, new_v = {}, {}, {}
    for name in LARGE:
        delta[name], new_m[name], new_v[name] = _adamw_large(w[name][0], grads[name], m[name][0], v[name][0], name)
    small_names = [name for name in WEIGHTS if name not in LARGE]
    flat = lambda d: [d[name].reshape(grads[name].shape) for name in small_names]
    ds, mo, vo = _adamw_small(flat(w), [grads[name] for name in small_names], flat(m), flat(v))
    for k, name in enumerate(small_names):
        delta[name], new_m[name], new_v[name] = ds[k], mo[k], vo[k]

    shaped = lambda d: [d[name].reshape(w[name].shape) for name in WEIGHTS]
    return (loss, grad_x[None], *shaped(grads), *shaped(delta), *shaped(new_m), *shaped(new_v))
```

```python
import functools
import math

import jax
import jax.numpy as jnp
from jax import lax
from jax.experimental import pallas as pl
from jax.experimental.pallas import tpu as pltpu

F32 = jnp.float32
BF16 = jnp.bfloat16

D_MODEL = 1024
D_POOL = 512
N_POOL_GROUPS = 4
D_RNN = 1024
N_RNN_HEADS = 8
HEAD = 128
CONV_WIDTH = 4
LRU_C = 8.0
D_FF = 2816
D_IN = D_POOL + 2 * D_RNN + 2 * D_MODEL
NORM_EPS = 1e-6
COL_RNN = D_POOL // HEAD
COL_GATE = (D_POOL + D_RNN) // HEAD

ADAM_LR = 0.001
ADAM_B1 = 0.9
ADAM_B2 = 0.999
ADAM_EPS = 1e-08
ADAM_WD = 0.01
ADAM_STEP = 10

N_CHIPS = 4
N_DEV = 8
MESH = pl.DeviceIdType.MESH
ANY = pl.BlockSpec(memory_space=pl.ANY)
VMEM = pl.BlockSpec(memory_space=pltpu.VMEM)
VMEM_LIMIT_BYTES = 60 * 1024 * 1024
SUBLANES = 8
POOL_HALO = 16
CHUNK = 1024

GELU_C = math.sqrt(2.0 / math.pi)
GELU_A = 0.044715


def _params(**kw):
    return pltpu.CompilerParams(vmem_limit_bytes=VMEM_LIMIT_BYTES, **kw)


def _sigmoid(x):
    return 0.5 * jnp.tanh(0.5 * x) + 0.5


def _log1p(y):
    u = 1.0 + y
    d = u - 1.0
    return jnp.where(d == 0.0, y, jnp.log(u) * (y / jnp.where(d == 0.0, 1.0, d)))


def _gelu_parts(x):
    x2 = x * x
    th = jnp.tanh(GELU_C * (x + GELU_A * x * x2))
    g = 0.5 * x * (1.0 + th)
    dg = 0.5 * (1.0 + th) + 0.5 * x * (1.0 - th * th) * GELU_C * (1.0 + 3.0 * GELU_A * x2)
    return g, dg


def _dot(a, b):
    return jnp.dot(a, b, preferred_element_type=F32)


def _dot_nt(a, b):
    return lax.dot_general(a, b, (((1,), (1,)), ((), ())), preferred_element_type=F32)


def _dot_tn(a, b):
    return lax.dot_general(a, b, (((0,), (0,)), ((), ())), preferred_element_type=F32)


def _rms_scale(xv):
    return lax.rsqrt(jnp.mean(xv * xv, axis=-1, keepdims=True) + NORM_EPS)


def _rms_bwd(dy, xv, g):
    r = _rms_scale(xv)
    xh = xv * r
    dyg = dy * g
    dx = r * (dyg - xh * jnp.mean(dyg * xh, axis=-1, keepdims=True))
    return dx, dy * xh


class _Job:
    def __init__(self, inputs, out_shapes, aliases, sem_shape, start, finish):
        self.inputs, self.out_shapes, self.aliases, self.sem_shape = list(inputs), list(out_shapes), dict(aliases), sem_shape
        self.start, self.finish = start, finish


def _pallas(body, operands, *, name, grid, in_specs, out_specs, out_shape, scratch_shapes=(), semantics=None, jobs=()):
    n_in, n_out, n_scr = len(in_specs), len(out_specs), len(scratch_shapes)
    job_in = [a for job in jobs for a in job.inputs]
    job_out = [s for job in jobs for s in job.out_shapes]
    aliases, i0, o0 = {}, n_in, n_out
    for job in jobs:
        aliases.update({i0 + i: o0 + o for i, o in job.aliases.items()})
        i0, o0 = i0 + len(job.inputs), o0 + len(job.out_shapes)

    def whole(*refs):
        ins, j_ins = refs[:n_in], refs[n_in:n_in + len(job_in)]
        outs = refs[n_in + len(job_in):][:n_out]
        j_outs = refs[n_in + len(job_in) + n_out:][:len(job_out)]
        rest = refs[n_in + len(job_in) + n_out + len(job_out):]
        scr, sems = rest[:n_scr], rest[n_scr:]

        def run(phase):
            i, o = 0, 0
            for k, job in enumerate(jobs):
                getattr(job, phase)(j_ins[i:i + len(job.inputs)], j_outs[o:o + len(job.out_shapes)], sems[2 * k], sems[2 * k + 1])
                i, o = i + len(job.inputs), o + len(job.out_shapes)

        def at(step_of, phase):
            if not jobs:
                return
            if not grid:
                run(phase)
                return
            cond = functools.reduce(jnp.logical_and, [pl.program_id(d) == step_of(d) for d in range(len(grid))])
            pl.when(cond)(functools.partial(run, phase))

        at(lambda d: 0, "start")
        body(*ins, *outs, *scr)
        at(lambda d: grid[d] - 1, "finish")

    res = pl.pallas_call(
        whole, name=name, grid=grid, in_specs=list(in_specs) + [ANY] * len(job_in), out_specs=list(out_specs) + [ANY] * len(job_out),
        out_shape=list(out_shape) + job_out, input_output_aliases=aliases,
        scratch_shapes=list(scratch_shapes) + [pltpu.SemaphoreType.DMA(job.sem_shape) for job in jobs for _ in range(2)],
        compiler_params=_params(dimension_semantics=semantics, has_side_effects=bool(jobs)),
    )(*operands, *job_in)
    per_job, o = [], n_out
    for job in jobs:
        per_job.append(res[o:o + len(job.out_shapes)])
        o += len(job.out_shapes)
    return res[:n_out], per_job


def _run_jobs(jobs, name):
    return _pallas(lambda: None, [], name=name, grid=(), in_specs=[], out_specs=[], out_shape=[], jobs=jobs)[1]


NORM_ROWS = 256


def _norm_rows(x_ref, g_ref, h_ref):
    g = g_ref[...]

    def rows(i, carry):
        r = pl.ds(pl.multiple_of(i * NORM_ROWS, NORM_ROWS), NORM_ROWS)
        xv = x_ref[r, :]
        h_ref[r, :] = (xv * _rms_scale(xv) * g).astype(BF16)
        return carry

    lax.fori_loop(0, x_ref.shape[0] // NORM_ROWS, rows, 0)


def _norm_matmul(x, g, w, *, tm, tn, name, jobs=()):
    T, K = x.shape
    N = w.shape[1]

    def body(x_ref, g_ref, w_ref, o_ref, h_ref):
        @pl.when(pl.program_id(1) == 0)
        def _():
            _norm_rows(x_ref, g_ref, h_ref)

        o_ref[...] = _dot(h_ref[...], w_ref[...])

    return _pallas(
        body, (x, g, w), name=name, grid=(T // tm, N // tn),
        in_specs=[pl.BlockSpec((tm, K), lambda i, j: (i, 0)), pl.BlockSpec((1, K), lambda i, j: (0, 0)),
                  pl.BlockSpec((K, tn), lambda i, j: (0, j))],
        out_specs=[pl.BlockSpec((tm, tn), lambda i, j: (i, j)), pl.BlockSpec((tm, K), lambda i, j: (i, 0))],
        out_shape=[jax.ShapeDtypeStruct((T, N), F32), jax.ShapeDtypeStruct((T, K), BF16)],
        semantics=("parallel", "arbitrary"), jobs=jobs)


def _ffn_in(x2, g, w, *, tm, tn):
    T, K = x2.shape
    nb = D_FF // tn

    def body(x_ref, g_ref, wg_ref, wu_ref, dup_ref, dgate_ref, act_ref, h_ref):
        @pl.when(pl.program_id(1) == 0)
        def _():
            _norm_rows(x_ref, g_ref, h_ref)

        h = h_ref[...]
        gate = _dot(h, wg_ref[...])
        up = _dot(h, wu_ref[...])
        s = _sigmoid(gate)
        silu = gate * s
        dup_ref[...] = silu.astype(BF16)
        dgate_ref[...] = (up * (s + silu * (1.0 - s))).astype(BF16)
        act_ref[...] = (silu * up).astype(BF16)

    blk = pl.BlockSpec((tm, tn), lambda i, j: (i, j))
    return pl.pallas_call(
        body, name="ffn_in", grid=(T // tm, nb),
        in_specs=[pl.BlockSpec((tm, K), lambda i, j: (i, 0)), pl.BlockSpec((1, K), lambda i, j: (0, 0)),
                  pl.BlockSpec((K, tn), lambda i, j: (0, j)), pl.BlockSpec((K, tn), lambda i, j: (0, j + nb))],
        out_specs=[blk, blk, blk, pl.BlockSpec((tm, K), lambda i, j: (i, 0))],
        out_shape=[jax.ShapeDtypeStruct((T, D_FF), BF16), jax.ShapeDtypeStruct((T, D_FF), BF16),
                   jax.ShapeDtypeStruct((T, D_FF), BF16), jax.ShapeDtypeStruct((T, K), BF16)],
        compiler_params=_params(dimension_semantics=("parallel", "arbitrary")),
    )(x2, g, w, w)


def _branch_mix(pm, z, w_pool_out, w_rnn_out, proj, *, tm, tn):
    T = pm.shape[0]
    col_gp = (D_POOL + 2 * D_RNN) // tn
    col_gr = col_gp + D_MODEL // tn

    def body(pm_ref, z_ref, wp_ref, wr_ref, gp_ref, gr_ref, by_gp_ref, by_gr_ref, sp_ref, sr_ref, mix_ref):
        yp = _dot(pm_ref[...], wp_ref[...])
        yr = _dot(z_ref[...], wr_ref[...])
        sp, sr = _sigmoid(gp_ref[...]), _sigmoid(gr_ref[...])
        by_gp_ref[...] = (yp * sp * (1.0 - sp)).astype(BF16)
        by_gr_ref[...] = (yr * sr * (1.0 - sr)).astype(BF16)
        sp_ref[...] = sp.astype(BF16)
        sr_ref[...] = sr.astype(BF16)
        mix_ref[...] = (sp * yp + sr * yr).astype(BF16)

    blk = pl.BlockSpec((tm, tn), lambda i, j: (i, j))
    out = jax.ShapeDtypeStruct((T, D_MODEL), BF16)
    return pl.pallas_call(
        body, name="branch_mix", grid=(T // tm, D_MODEL // tn),
        in_specs=[pl.BlockSpec((tm, D_POOL), lambda i, j: (i, 0)), pl.BlockSpec((tm, D_RNN), lambda i, j: (i, 0)),
                  pl.BlockSpec((D_POOL, tn), lambda i, j: (0, j)), pl.BlockSpec((D_RNN, tn), lambda i, j: (0, j)),
                  pl.BlockSpec((tm, tn), lambda i, j: (i, col_gp + j)), pl.BlockSpec((tm, tn), lambda i, j: (i, col_gr + j))],
        out_specs=[blk] * 5, out_shape=[out] * 5,
        compiler_params=_params(dimension_semantics=("parallel", "parallel")),
    )(pm, z, w_pool_out, w_rnn_out, proj, proj)


def _out_proj_residual(mix, w_o, x, *, tm):
    T = x.shape[0]

    def body(mix_ref, w_ref, x_ref, o_ref):
        o_ref[...] = x_ref[...] + _dot(mix_ref[...], w_ref[...])

    row = pl.BlockSpec((tm, D_MODEL), lambda i: (i, 0))
    return pl.pallas_call(
        body, name="out_proj_residual", grid=(T // tm,),
        in_specs=[row, pl.BlockSpec((D_MODEL, D_MODEL), lambda i: (0, 0)), row],
        out_specs=row, out_shape=jax.ShapeDtypeStruct((T, D_MODEL), F32),
        compiler_params=_params(dimension_semantics=("parallel",)),
    )(mix, w_o, x)


def _ffn_out_loss(act, w, x2, g3, target, *, tm):
    T = x2.shape[0]

    def body(act_ref, w_ref, x2_ref, g_ref, t_ref, dx_ref, dxb_ref, sq_ref, dg_ref):
        @pl.when(pl.program_id(0) == 0)
        def _():
            sq_ref[...] = jnp.zeros_like(sq_ref)
            dg_ref[...] = jnp.zeros_like(dg_ref)

        x3 = x2_ref[...] + _dot(act_ref[...], w_ref[...])
        g = g_ref[...]
        err = x3 * _rms_scale(x3) * g - t_ref[...]
        sq_ref[...] += jnp.sum(err * err, axis=0, keepdims=True)
        dx, dgp = _rms_bwd(err * (1.0 / D_MODEL), x3, g)
        dg_ref[...] += jnp.sum(dgp, axis=0, keepdims=True)
        dx_ref[...] = dx
        dxb_ref[...] = dx.astype(BF16)

    row = pl.BlockSpec((tm, D_MODEL), lambda i: (i, 0))
    vec = pl.BlockSpec((1, D_MODEL), lambda i: (0, 0))
    return pl.pallas_call(
        body, name="ffn_out_loss", grid=(T // tm,),
        in_specs=[pl.BlockSpec((tm, D_FF), lambda i: (i, 0)), pl.BlockSpec((D_FF, D_MODEL), lambda i: (0, 0)), row, vec, row],
        out_specs=[row, row, vec, vec],
        out_shape=[jax.ShapeDtypeStruct((T, D_MODEL), F32), jax.ShapeDtypeStruct((T, D_MODEL), BF16),
                   jax.ShapeDtypeStruct((1, D_MODEL), F32), jax.ShapeDtypeStruct((1, D_MODEL), F32)],
        compiler_params=_params(dimension_semantics=("arbitrary",)),
    )(act, w, x2, g3, target)


def _ffn_out_bwd(dx3b, w, act_by_gate, act_by_up, *, tm, tn):
    T = dx3b.shape[0]

    def body(dx_ref, w_ref, by_gate_ref, by_up_ref, dgate_ref, dup_ref):
        dact = _dot_nt(dx_ref[...], w_ref[...])
        dgate_ref[...] = (dact * by_gate_ref[...].astype(F32)).astype(BF16)
        dup_ref[...] = (dact * by_up_ref[...].astype(F32)).astype(BF16)

    blk = pl.BlockSpec((tm, tn), lambda i, j: (i, j))
    return pl.pallas_call(
        body, name="ffn_out_bwd", grid=(T // tm, D_FF // tn),
        in_specs=[pl.BlockSpec((tm, D_MODEL), lambda i, j: (i, 0)), pl.BlockSpec((tn, D_MODEL), lambda i, j: (j, 0)), blk, blk],
        out_specs=[blk, blk],
        out_shape=[jax.ShapeDtypeStruct((T, D_FF), BF16), jax.ShapeDtypeStruct((T, D_FF), BF16)],
        compiler_params=_params(dimension_semantics=("parallel", "parallel")),
    )(dx3b, w, act_by_gate, act_by_up)


def _ffn_in_bwd(dgate, dup, w, dx3, x2, g2, *, tm):
    T = x2.shape[0]

    def body(dgate_ref, dup_ref, w_ref, dx3_ref, x2_ref, g_ref, dx_ref, dxb_ref, dg_ref):
        @pl.when(pl.program_id(0) == 0)
        def _():
            dg_ref[...] = jnp.zeros_like(dg_ref)

        dh = _dot_nt(dgate_ref[...], w_ref[:, :D_FF]) + _dot_nt(dup_ref[...], w_ref[:, D_FF:])
        dxn, dgp = _rms_bwd(dh, x2_ref[...], g_ref[...])
        dx = dx3_ref[...] + dxn
        dg_ref[...] += jnp.sum(dgp, axis=0, keepdims=True)
        dx_ref[...] = dx
        dxb_ref[...] = dx.astype(BF16)

    row = pl.BlockSpec((tm, D_MODEL), lambda i: (i, 0))
    wide = pl.BlockSpec((tm, D_FF), lambda i: (i, 0))
    vec = pl.BlockSpec((1, D_MODEL), lambda i: (0, 0))
    return pl.pallas_call(
        body, name="ffn_in_bwd", grid=(T // tm,),
        in_specs=[wide, wide, pl.BlockSpec((D_MODEL, 2 * D_FF), lambda i: (0, 0)), row, row, vec],
        out_specs=[row, row, vec],
        out_shape=[jax.ShapeDtypeStruct((T, D_MODEL), F32), jax.ShapeDtypeStruct((T, D_MODEL), BF16),
                   jax.ShapeDtypeStruct((1, D_MODEL), F32)],
        compiler_params=_params(dimension_semantics=("arbitrary",)),
    )(dgate, dup, w, dx3, x2, g2)


def _out_proj_bwd(dx2b, w_o, mix_by, *, tm, tn, jobs=()):
    T = dx2b.shape[0]

    def body(dx_ref, w_ref, *refs):
        dmix = _dot_nt(dx_ref[...], w_ref[...])
        for by_ref, d_ref in zip(refs[:4], refs[4:]):
            d_ref[...] = (dmix * by_ref[...].astype(F32)).astype(BF16)

    blk = pl.BlockSpec((tm, tn), lambda i, j: (i, j))
    out = jax.ShapeDtypeStruct((T, D_MODEL), BF16)
    return _pallas(
        body, (dx2b, w_o, *mix_by), name="out_proj_bwd", grid=(T // tm, D_MODEL // tn),
        in_specs=[pl.BlockSpec((tm, D_MODEL), lambda i, j: (i, 0)), pl.BlockSpec((tn, D_MODEL), lambda i, j: (j, 0))] + [blk] * 4,
        out_specs=[blk] * 4, out_shape=[out] * 4, semantics=("parallel", "parallel"), jobs=jobs)


def _branch_bwd(dyp, dyr, w_pool_out, w_rnn_out, *, tm):
    T = dyp.shape[0]

    def body(dyp_ref, dyr_ref, wp_ref, wr_ref, dpm_ref, dz_ref):
        dpm_ref[...] = _dot_nt(dyp_ref[...], wp_ref[...])
        dz_ref[...] = _dot_nt(dyr_ref[...], wr_ref[...])

    row = pl.BlockSpec((tm, D_MODEL), lambda i: (i, 0))
    return pl.pallas_call(
        body, name="branch_bwd", grid=(T // tm,),
        in_specs=[row, row, pl.BlockSpec((D_POOL, D_MODEL), lambda i: (0, 0)), pl.BlockSpec((D_RNN, D_MODEL), lambda i: (0, 0))],
        out_specs=[pl.BlockSpec((tm, D_POOL), lambda i: (i, 0)), pl.BlockSpec((tm, D_RNN), lambda i: (i, 0))],
        out_shape=[jax.ShapeDtypeStruct((T, D_POOL), F32), jax.ShapeDtypeStruct((T, D_RNN), F32)],
        compiler_params=_params(dimension_semantics=("parallel",)),
    )(dyp, dyr, w_pool_out, w_rnn_out)


def _in_proj_bwd(segs, w, dx2, x, g1, *, tm, jobs=()):
    T = x.shape[0]
    widths = [s.shape[1] for s in segs]
    offs = [sum(widths[:k]) for k in range(len(widths))]
    n = len(segs)

    def body(*refs):
        seg_refs, (w_ref, dx2_ref, x_ref, g_ref, dx_ref, dg_ref) = refs[:n], refs[n:]

        @pl.when(pl.program_id(0) == 0)
        def _():
            dg_ref[...] = jnp.zeros_like(dg_ref)

        dh = _dot_nt(seg_refs[0][...], w_ref[:, offs[0]:offs[0] + widths[0]])
        for k in range(1, n):
            dh += _dot_nt(seg_refs[k][...], w_ref[:, offs[k]:offs[k] + widths[k]])
        dxn, dgp = _rms_bwd(dh, x_ref[...], g_ref[...])
        dg_ref[...] += jnp.sum(dgp, axis=0, keepdims=True)
        dx_ref[...] = dx2_ref[...] + dxn

    row = pl.BlockSpec((tm, D_MODEL), lambda i: (i, 0))
    vec = pl.BlockSpec((1, D_MODEL), lambda i: (0, 0))
    return _pallas(
        body, (*segs, w, dx2, x, g1), name="in_proj_bwd", grid=(T // tm,),
        in_specs=[pl.BlockSpec((tm, wd), lambda i: (i, 0)) for wd in widths]
        + [pl.BlockSpec((D_MODEL, D_IN), lambda i: (0, 0)), row, row, vec],
        out_specs=[row, vec],
        out_shape=[jax.ShapeDtypeStruct((T, D_MODEL), F32), jax.ShapeDtypeStruct((1, D_MODEL), F32)],
        semantics=("arbitrary",), jobs=jobs)


def _weight_grad(a, segs, *, tm, tn, name, jobs=None):
    T, M = a.shape
    nblk = [s.shape[1] // tn for s in segs]
    first = [sum(nblk[:k]) for k in range(len(segs))]
    n = len(segs)

    def body(a_ref, *refs):
        seg_refs, o_ref = refs[:n], refs[n]
        j = pl.program_id(1)
        for k in range(n):
            @pl.when((j >= first[k]) & (j < first[k] + nblk[k]))
            def _(k=k):
                o_ref[...] = _dot_tn(a_ref[...], seg_refs[k][...])

    def seg_spec(k):
        return pl.BlockSpec((T, tn), lambda i, j: (0, jnp.clip(j - first[k], 0, nblk[k] - 1)))

    (grad,), results = _pallas(
        body, (a, *segs), name=name, grid=(M // tm, sum(nblk)),
        in_specs=[pl.BlockSpec((T, tm), lambda i, j: (0, i))] + [seg_spec(k) for k in range(n)],
        out_specs=[pl.BlockSpec((tm, tn), lambda i, j: (i, j))],
        out_shape=[jax.ShapeDtypeStruct((M, sum(nblk) * tn), F32)],
        semantics=("parallel", "arbitrary"), jobs=jobs or ())
    return grad if jobs is None else (grad, results)


def _pad_front(dst, src, halo):
    dst[pl.ds(0, halo), :] = jnp.zeros((halo, src.shape[1]), F32)

    def fill(i, carry):
        r0 = pl.multiple_of(i * CHUNK, CHUNK)
        dst[pl.ds(r0 + halo, CHUNK), :] = src[pl.ds(r0, CHUNK), :]
        return carry

    lax.fori_loop(0, src.shape[0] // CHUNK, fill, 0)


def _shift_rows(v, k):
    return pltpu.roll(v, k % v.shape[0], axis=0)


def _window_sums(xs, direction):
    s2 = xs + _shift_rows(xs, direction)
    s4 = s2 + _shift_rows(s2, 2 * direction)
    s8 = s4 + _shift_rows(s4, 4 * direction)
    s16 = s8 + _shift_rows(s8, 8 * direction)
    return s2, s4, s8, s16


def _select_window(g, sums):
    s2, s4, s8, s16 = sums
    return jnp.where(g == 0, s2, jnp.where(g == 1, s4, jnp.where(g == 2, s8, s16)))


def _pool_count(g, start, rows):
    t = start + lax.broadcasted_iota(jnp.int32, (rows, 1), 0)
    return jnp.minimum(t + 1, jnp.left_shift(2, g)).astype(F32)


def _pool_fwd(proj, w_grp, scale):
    T = proj.shape[0]
    nchunk = T // CHUNK

    def body(u_ref, w_ref, s_ref, o_ref, upad):
        g = pl.program_id(0)
        _pad_front(upad, u_ref, POOL_HALO)
        w = w_ref[...].astype(BF16)
        scale_row = s_ref[...]

        def chunk(i, carry):
            r0 = pl.multiple_of(i * CHUNK, CHUNK)
            xs = upad[pl.ds(r0, CHUNK + POOL_HALO), :]
            win = _select_window(g, _window_sums(xs, 1))[POOL_HALO:]
            pooled = win / _pool_count(g, r0, CHUNK) - xs[POOL_HALO:]
            o_ref[pl.ds(r0, CHUNK), :] = (_dot(pooled.astype(BF16), w) * scale_row).astype(BF16)
            return carry

        lax.fori_loop(0, nchunk, chunk, 0)

    return pl.pallas_call(
        body, name="pool_fwd", grid=(N_POOL_GROUPS,),
        in_specs=[pl.BlockSpec((T, HEAD), lambda g: (0, g)), pl.BlockSpec((None, HEAD, HEAD), lambda g: (g, 0, 0)),
                  pl.BlockSpec((1, HEAD), lambda g: (0, g))],
        out_specs=pl.BlockSpec((T, HEAD), lambda g: (0, g)),
        out_shape=jax.ShapeDtypeStruct((T, D_POOL), BF16),
        scratch_shapes=[pltpu.VMEM((T + POOL_HALO, HEAD), F32)],
        compiler_params=_params(dimension_semantics=("parallel",)),
    )(proj, w_grp, scale)


def _pool_bwd(proj, dpm, w_grp, scale, jobs=()):
    T = proj.shape[0]
    nchunk = T // CHUNK

    def body(u_ref, dpm_ref, w_ref, s_ref, du_ref, dw_ref, ds_ref, upad, zpad, dpool):
        g = pl.program_id(0)
        _pad_front(upad, u_ref, POOL_HALO)
        zpad[pl.ds(T, POOL_HALO), :] = jnp.zeros((POOL_HALO, HEAD), F32)
        dw_ref[...] = jnp.zeros_like(dw_ref)
        ds_ref[...] = jnp.zeros_like(ds_ref)
        w = w_ref[...].astype(BF16)
        scale_row = s_ref[...]

        def chunk(i, carry):
            r0 = pl.multiple_of(i * CHUNK, CHUNK)
            xs = upad[pl.ds(r0, CHUNK + POOL_HALO), :]
            cnt = _pool_count(g, r0, CHUNK)
            pooled = (_select_window(g, _window_sums(xs, 1))[POOL_HALO:] / cnt - xs[POOL_HALO:]).astype(BF16)
            mixed = _dot(pooled, w)
            d = dpm_ref[pl.ds(r0, CHUNK), :]
            ds_ref[...] += jnp.sum(d * mixed, axis=0, keepdims=True)
            dmixed = (d * scale_row).astype(BF16)
            dw_ref[...] += _dot_tn(pooled, dmixed)
            dp = _dot_nt(dmixed, w)
            dpool[pl.ds(r0, CHUNK), :] = dp
            zpad[pl.ds(r0, CHUNK), :] = dp / cnt
            return carry

        lax.fori_loop(0, nchunk, chunk, 0)

        def chunk2(i, carry):
            r0 = pl.multiple_of(i * CHUNK, CHUNK)
            zs = zpad[pl.ds(r0, CHUNK + POOL_HALO), :]
            win = _select_window(g, _window_sums(zs, -1))[:CHUNK]
            du_ref[pl.ds(r0, CHUNK), :] = (win - dpool[pl.ds(r0, CHUNK), :]).astype(BF16)
            return carry

        lax.fori_loop(0, nchunk, chunk2, 0)

    col = pl.BlockSpec((T, HEAD), lambda g: (0, g))
    return _pallas(
        body, (proj, dpm, w_grp, scale), name="pool_bwd", grid=(N_POOL_GROUPS,),
        in_specs=[col, col, pl.BlockSpec((None, HEAD, HEAD), lambda g: (g, 0, 0)), pl.BlockSpec((1, HEAD), lambda g: (0, g))],
        out_specs=[col, pl.BlockSpec((None, HEAD, HEAD), lambda g: (g, 0, 0)), pl.BlockSpec((1, HEAD), lambda g: (0, g))],
        out_shape=[jax.ShapeDtypeStruct((T, D_POOL), BF16), jax.ShapeDtypeStruct((N_POOL_GROUPS, HEAD, HEAD), F32),
                   jax.ShapeDtypeStruct((1, D_POOL), F32)],
        scratch_shapes=[pltpu.VMEM((T + POOL_HALO, HEAD), F32), pltpu.VMEM((T + POOL_HALO, HEAD), F32), pltpu.VMEM((T, HEAD), F32)],
        semantics=("parallel",), jobs=jobs)


def _conv_taps(xs, cw):
    v = cw[CONV_WIDTH - 1] * xs[SUBLANES:]
    for k in range(CONV_WIDTH - 1):
        v += cw[k] * _shift_rows(xs, CONV_WIDTH - 1 - k)[SUBLANES:]
    return v


def _tap_rows(cw_ref):
    return [cw_ref[k:k + 1, :] for k in range(CONV_WIDTH)]


def _softplus_neg(lam):
    return jnp.maximum(-lam, 0.0) + _log1p(jnp.exp(-jnp.abs(lam)))


def _lru_gates(v, wa, ba, wx, bx, sp):
    vb = v.astype(BF16)
    ra = _sigmoid(_dot(vb, wa) + ba)
    ix = _sigmoid(_dot(vb, wx) + bx)
    log_a = -LRU_C * ra * sp
    a = jnp.exp(log_a)
    sq = jnp.sqrt(-jnp.tanh(log_a) * (a * a + 1.0))
    return ra, ix, a, sq


def _row_bcast(v, r):
    return jnp.broadcast_to(v[r:r + 1, :], v.shape)


TILE_BLOCK = 128


def _scan_in_tiles(coef, coef_shift, A_out, B, T, direction):
    order = list(range(SUBLANES)) if direction == 1 else list(range(SUBLANES - 1, -1, -1))
    tiles = min(TILE_BLOCK, T // SUBLANES)
    for base in range(0, T, tiles * SUBLANES):
        def rows(r, base=base):
            return pl.ds(base + r, tiles, stride=SUBLANES)

        A, Bv = coef[rows(order[0] + coef_shift), :], B[rows(order[0]), :]
        A_out[rows(order[0]), :] = A
        for r in order[1:]:
            a = coef[rows(r + coef_shift), :]
            Bv = a * Bv + B[rows(r), :]
            A = a * A
            A_out[rows(r), :] = A
            B[rows(r), :] = Bv


TILES_PER_STEP = 8


def _carry_tiles(A_s, B_s, out, ntile, direction):
    out_row = SUBLANES - 1 if direction == 1 else 0

    def step(k, carry):
        for j in range(TILES_PER_STEP):
            t = k * TILES_PER_STEP + j
            r0 = pl.multiple_of((t if direction == 1 else ntile - 1 - t) * SUBLANES, SUBLANES)
            A, B = A_s[pl.ds(r0, SUBLANES), :], B_s[pl.ds(r0, SUBLANES), :]
            out[pl.ds(r0, SUBLANES), :] = A * carry + B
            carry = _row_bcast(A, out_row) * carry + _row_bcast(B, out_row)
        return carry

    lax.fori_loop(0, ntile // TILES_PER_STEP, step, jnp.zeros((SUBLANES, HEAD), F32))


def _rnn_fwd(proj, conv_w, conv_b, w_a, b_a, w_x, b_x, lam, jobs=()):
    T = proj.shape[0]
    nchunk = T // CHUNK
    ntile = T // SUBLANES

    def body(u_ref, ug_ref, cw_ref, cb_ref, wa_ref, ba_ref, wx_ref, bx_ref, lam_ref, h_ref, z_ref, upad, a_s, b_s):
        _pad_front(upad, u_ref, SUBLANES)
        cw, cb = _tap_rows(cw_ref), cb_ref[...]
        wa, wx = wa_ref[...].astype(BF16), wx_ref[...].astype(BF16)
        ba, bx = ba_ref[...], bx_ref[...]
        sp = _softplus_neg(lam_ref[...])

        def chunk(i, carry):
            r0 = pl.multiple_of(i * CHUNK, CHUNK)
            v = _conv_taps(upad[pl.ds(r0, CHUNK + SUBLANES), :], cw) + cb
            _, ix, a, sq = _lru_gates(v, wa, ba, wx, bx, sp)
            a_s[pl.ds(r0, CHUNK), :], b_s[pl.ds(r0, CHUNK), :] = a, sq * ix * v
            return carry

        lax.fori_loop(0, nchunk, chunk, 0)
        _scan_in_tiles(a_s, 0, a_s, b_s, T, 1)
        _carry_tiles(a_s, b_s, h_ref, ntile, 1)

        def chunk3(i, carry):
            r0 = pl.multiple_of(i * CHUNK, CHUNK)
            gl, _ = _gelu_parts(ug_ref[pl.ds(r0, CHUNK), :])
            z_ref[pl.ds(r0, CHUNK), :] = (h_ref[pl.ds(r0, CHUNK), :] * gl).astype(BF16)
            return carry

        lax.fori_loop(0, nchunk, chunk3, 0)

    col = pl.BlockSpec((T, HEAD), lambda h: (0, h))
    vec = pl.BlockSpec((1, HEAD), lambda h: (0, h))
    mat = pl.BlockSpec((None, HEAD, HEAD), lambda h: (h, 0, 0))
    return _pallas(
        body, (proj, proj, conv_w, conv_b, w_a, b_a, w_x, b_x, lam), name="rnn_fwd", grid=(N_RNN_HEADS,),
        in_specs=[pl.BlockSpec((T, HEAD), lambda h: (0, COL_RNN + h)), pl.BlockSpec((T, HEAD), lambda h: (0, COL_GATE + h)),
                  pl.BlockSpec((CONV_WIDTH, HEAD), lambda h: (0, h)), vec, mat, vec, mat, vec, vec],
        out_specs=[col, col],
        out_shape=[jax.ShapeDtypeStruct((T, D_RNN), F32), jax.ShapeDtypeStruct((T, D_RNN), BF16)],
        scratch_shapes=[pltpu.VMEM((T + SUBLANES, HEAD), F32), pltpu.VMEM((T, HEAD), F32), pltpu.VMEM((T, HEAD), F32)],
        semantics=("parallel",), jobs=jobs)


def _rnn_bwd(proj, hr, dz, conv_w, conv_b, w_a, b_a, w_x, b_x, lam, jobs=()):
    T = proj.shape[0]
    nchunk = T // CHUNK
    ntile = T // SUBLANES

    def body(u_ref, ug_ref, h_ref, dz_ref, cw_ref, cb_ref, wa_ref, ba_ref, wx_ref, bx_ref, lam_ref,
             du_ref, dug_ref, dwa_ref, dwx_ref, dba_ref, dbx_ref, dlam_ref, dcb_ref, dcw_ref,
             upad, hpad, apad, v_s, ra_s, ix_s, sq_s, g_s, dvpad, ga_s):
        zero_tile = jnp.zeros((SUBLANES, HEAD), F32)
        _pad_front(upad, u_ref, SUBLANES)
        _pad_front(hpad, h_ref, SUBLANES)
        apad[pl.ds(T, SUBLANES), :] = zero_tile
        dvpad[pl.ds(T, SUBLANES), :] = zero_tile
        for ref in (dwa_ref, dwx_ref, dba_ref, dbx_ref, dlam_ref, dcb_ref, dcw_ref):
            ref[...] = jnp.zeros_like(ref)
        cw, cb = _tap_rows(cw_ref), cb_ref[...]
        wa, wx = wa_ref[...].astype(BF16), wx_ref[...].astype(BF16)
        ba, bx = ba_ref[...], bx_ref[...]
        lam_row = lam_ref[...]
        sp = _softplus_neg(lam_row)

        def chunk(i, carry):
            r0 = pl.multiple_of(i * CHUNK, CHUNK)
            rows = pl.ds(r0, CHUNK)
            h = h_ref[rows, :]
            v = _conv_taps(upad[pl.ds(r0, CHUNK + SUBLANES), :], cw) + cb
            ra, ix, a, sq = _lru_gates(v, wa, ba, wx, bx, sp)
            v_s[rows, :], ra_s[rows, :], ix_s[rows, :], sq_s[rows, :], apad[rows, :] = v, ra, ix, sq, a
            gl, dgl = _gelu_parts(ug_ref[rows, :])
            d = dz_ref[rows, :]
            g_s[rows, :] = d * gl
            dug_ref[rows, :] = (d * h * dgl).astype(BF16)
            return carry

        lax.fori_loop(0, nchunk, chunk, 0)

        _scan_in_tiles(apad, 1, ga_s, g_s, T, -1)
        _carry_tiles(ga_s, g_s, g_s, ntile, -1)

        def chunk3(i, carry):
            r0 = pl.multiple_of(i * CHUNK, CHUNK)
            rows = pl.ds(r0, CHUNK)
            g = g_s[rows, :]
            h_prev = _shift_rows(hpad[pl.ds(r0, CHUNK + SUBLANES), :], 1)[SUBLANES:]
            v, ra, ix, sq, a = v_s[rows, :], ra_s[rows, :], ix_s[rows, :], sq_s[rows, :], apad[rows, :]
            d_sq = g * ix * v
            d_ix = g * sq * v
            d_la = a * g * h_prev - d_sq * a * a / sq
            dlam_ref[...] += jnp.sum(d_la * ra, axis=0, keepdims=True)
            d_pa = d_la * (-LRU_C) * sp * ra * (1.0 - ra)
            d_px = d_ix * ix * (1.0 - ix)
            vb, d_pab, d_pxb = v.astype(BF16), d_pa.astype(BF16), d_px.astype(BF16)
            dwa_ref[...] += _dot_tn(vb, d_pab)
            dwx_ref[...] += _dot_tn(vb, d_pxb)
            dba_ref[...] += jnp.sum(d_pa, axis=0, keepdims=True)
            dbx_ref[...] += jnp.sum(d_px, axis=0, keepdims=True)
            dv = g * sq * ix + _dot_nt(d_pab, wa) + _dot_nt(d_pxb, wx)
            dvpad[rows, :] = dv
            dcb_ref[...] += jnp.sum(dv, axis=0, keepdims=True)
            xs = upad[pl.ds(r0, CHUNK + SUBLANES), :]
            for k in range(CONV_WIDTH):
                u_k = _shift_rows(xs, CONV_WIDTH - 1 - k)[SUBLANES:] if k < CONV_WIDTH - 1 else xs[SUBLANES:]
                dcw_ref[k:k + 1, :] += jnp.sum(dv * u_k, axis=0, keepdims=True)
            return carry

        lax.fori_loop(0, nchunk, chunk3, 0)
        dlam_ref[...] = dlam_ref[...] * (LRU_C * _sigmoid(-lam_row))

        def chunk4(i, carry):
            r0 = pl.multiple_of(i * CHUNK, CHUNK)
            dvs = dvpad[pl.ds(r0, CHUNK + SUBLANES), :]
            du = cw[CONV_WIDTH - 1] * dvs[:CHUNK]
            for k in range(CONV_WIDTH - 1):
                du += cw[k] * _shift_rows(dvs, -(CONV_WIDTH - 1 - k))[:CHUNK]
            du_ref[pl.ds(r0, CHUNK), :] = du.astype(BF16)
            return carry

        lax.fori_loop(0, nchunk, chunk4, 0)

    col = pl.BlockSpec((T, HEAD), lambda h: (0, h))
    vec = pl.BlockSpec((1, HEAD), lambda h: (0, h))
    mat = pl.BlockSpec((None, HEAD, HEAD), lambda h: (h, 0, 0))
    taps = pl.BlockSpec((CONV_WIDTH, HEAD), lambda h: (0, h))
    vec_out = jax.ShapeDtypeStruct((1, D_RNN), F32)
    mat_out = jax.ShapeDtypeStruct((N_RNN_HEADS, HEAD, HEAD), F32)
    seq = pltpu.VMEM((T, HEAD), F32)
    seq_pad = pltpu.VMEM((T + SUBLANES, HEAD), F32)
    return _pallas(
        body, (proj, proj, hr, dz, conv_w, conv_b, w_a, b_a, w_x, b_x, lam), name="rnn_bwd", grid=(N_RNN_HEADS,),
        in_specs=[pl.BlockSpec((T, HEAD), lambda h: (0, COL_RNN + h)), pl.BlockSpec((T, HEAD), lambda h: (0, COL_GATE + h)),
                  col, col, taps, vec, mat, vec, mat, vec, vec],
        out_specs=[col, col, mat, mat, vec, vec, vec, vec, taps],
        out_shape=[jax.ShapeDtypeStruct((T, D_RNN), BF16), jax.ShapeDtypeStruct((T, D_RNN), BF16), mat_out, mat_out,
                   vec_out, vec_out, vec_out, vec_out, jax.ShapeDtypeStruct((CONV_WIDTH, D_RNN), F32)],
        scratch_shapes=[seq_pad, seq_pad, seq_pad, seq, seq, seq, seq, seq, seq_pad, seq],
        semantics=("parallel",), jobs=jobs)


GROUP_FFN = ["w_ffn_out", "w_ffn_in"]
GROUP_MIX = ["w_o", "w_pool_out", "w_rnn_out"]
GROUP_IN = ["w_in"]


def _step(x, target, s, full, conv_w_mine, place):
    T = x.shape[0]
    tall, mid, low = min(T, 2048), min(T, 1024), min(T, 512)
    c1 = place[1:]
    full = dict(full)

    def gathered(names, results):
        full.update(zip(names, results))

    (full["w_in"], conv_w), = _run_jobs([_gather_job(full, ["w_in"], conv_w_mine)], "gather_w_in")
    early = ["w_pool_out", "w_rnn_out", "w_o", "w_ffn_out"]
    (proj, h1), (res,) = _norm_matmul(x, s["norm_mix"], full["w_in"], tm=tall, tn=512, name="in_proj", jobs=[_gather_job(full, early)])
    gathered(early, res)
    pm = _pool_fwd(proj, s["w_pool_grp"], s["pool_scale"])
    (hr, z), (res,) = _rnn_fwd(proj, conv_w, s["conv_b"], s["w_rg_a"], s["b_rg_a"], s["w_rg_x"], s["b_rg_x"], s["lru_lambda"],
                               jobs=[_gather_job(full, ["w_ffn_in"])])
    gathered(["w_ffn_in"], res)
    *mix_by, mix = _branch_mix(pm, z, full["w_pool_out"], full["w_rnn_out"], proj, tm=tall, tn=256)
    x2 = _out_proj_residual(mix, full["w_o"], x, tm=mid)
    act_by_up, act_by_gate, act, h2 = _ffn_in(x2, s["norm_ffn"], full["w_ffn_in"], tm=tall, tn=256)
    dx3, dx3b, sq_cols, g_norm_final = _ffn_out_loss(act, full["w_ffn_out"], x2, s["norm_final"], target, tm=low)

    g = {"norm_final": g_norm_final}

    def chip_sums(names, from_sibling):
        sums = {name: _chip_sum(name, g[name], got, c1) for name, got in zip(names, from_sibling)}
        return {name: v[0] for name, v in sums.items()}, {name: v[1] for name, v in sums.items()}

    def final_sums(names, sums, from_chips):
        return {name: _final_sum(name, sums[name], got, place) for name, got in zip(names, from_chips)}

    dgate, dup = _ffn_out_bwd(dx3b, full["w_ffn_out"], act_by_gate, act_by_up, tm=tall, tn=256)
    g["w_ffn_out"] = _weight_grad(act, [dx3b], tm=256, tn=D_MODEL, name="w_ffn_out_grad")
    dx2, dx2b, g["norm_ffn"] = _ffn_in_bwd(dgate, dup, full["w_ffn_in"], dx3, x2, s["norm_ffn"], tm=low)
    g["w_ffn_in"] = _weight_grad(h2, [dgate, dup], tm=D_MODEL, tn=256, name="w_ffn_in_grad")
    (dgp, dgr, dyp, dyr), (res,) = _out_proj_bwd(dx2b, full["w_o"], mix_by, tm=tall, tn=256,
                                                 jobs=[_sibling_job(g, GROUP_FFN)])
    sums_ffn, sums_ffn_bf16 = chip_sums(GROUP_FFN, res)
    g["w_o"] = _weight_grad(mix, [dx2b], tm=D_MODEL, tn=256, name="w_o_grad")
    dpm, dz = _branch_bwd(dyp, dyr, full["w_pool_out"], full["w_rnn_out"], tm=mid)
    g["w_pool_out"] = _weight_grad(pm, [dyp], tm=D_POOL, tn=256, name="w_pool_out_grad")
    g["w_rnn_out"] = _weight_grad(z, [dyr], tm=D_RNN, tn=256, name="w_rnn_out_grad")
    (dupool, g["w_pool_grp"], g["pool_scale"]), (res,) = _pool_bwd(proj, dpm, s["w_pool_grp"], s["pool_scale"],
                                                                   jobs=[_sibling_job(g, GROUP_MIX)])
    sums_mix, sums_mix_bf16 = chip_sums(GROUP_MIX, res)
    ((durnn, dugate, g["w_rg_a"], g["w_rg_x"], g["b_rg_a"], g["b_rg_x"], g["lru_lambda"], g["conv_b"], g["conv_w"]),
     (res,)) = _rnn_bwd(proj, hr, dz, conv_w, s["conv_b"], s["w_rg_a"], s["b_rg_a"], s["w_rg_x"], s["b_rg_x"], s["lru_lambda"],
                        jobs=[_chips_job(sums_ffn_bf16, GROUP_FFN)])
    shards = final_sums(GROUP_FFN, sums_ffn, res)
    segs = [dupool, durnn, dugate, dgp, dgr]
    g["w_in"], (res, joined) = _weight_grad(h1, segs, tm=D_MODEL, tn=256, name="w_in_grad",
                                           jobs=[_chips_job(sums_mix_bf16, GROUP_MIX), _join_job(shards, GROUP_FFN)])
    grads = dict(zip(GROUP_FFN, joined))
    shards = final_sums(GROUP_MIX, sums_mix, res)
    (res,) = _run_jobs([_sibling_job(g, GROUP_IN)], "w_in_exchange_sibling")
    sums_in, sums_in_bf16 = chip_sums(GROUP_IN, res)
    (grad_x, g["norm_mix"]), (res,) = _in_proj_bwd(segs, full["w_in"], dx2, x, s["norm_mix"], tm=low,
                                                  jobs=[_chips_job(sums_in_bf16, GROUP_IN)])
    shards.update(final_sums(GROUP_IN, sums_in, res))

    vec_rows = [g[name] if name != "pool_scale" else jnp.pad(g[name], ((0, 0), (0, D_MODEL - D_POOL))) for name in VEC_ITEMS]
    vec_rows += [g["conv_w"], sq_cols, jnp.zeros((VEC_ROWS - len(VEC_ITEMS) - CONV_WIDTH - 1, D_MODEL), F32)]
    vec = jnp.concatenate(vec_rows, axis=0).reshape(VEC_ROWS, N_DEV, HEAD).transpose(1, 0, 2)
    mat = jnp.concatenate([g[name].reshape(-1, HEAD) for name in MAT_ITEMS], axis=0).reshape(N_DEV, -1, HEAD)
    (vec, mat), (joined,) = _all_reduce_small([vec, mat], jobs=[_join_job(shards, GROUP_MIX + GROUP_IN)])
    grads.update(zip(GROUP_MIX + GROUP_IN, joined))

    vec = vec.transpose(1, 0, 2).reshape(VEC_ROWS, D_MODEL)
    mat = mat.reshape(-1, HEAD)
    for k, name in enumerate(VEC_ITEMS):
        grads[name] = vec[k:k + 1, :s[name].shape[1]]
    grads["conv_w"] = vec[len(VEC_ITEMS):len(VEC_ITEMS) + CONV_WIDTH]
    row = 0
    for name in MAT_ITEMS:
        rows = s[name].shape[0] * HEAD
        grads[name] = mat[row:row + rows]
        row += rows
    return vec[len(VEC_ITEMS) + CONV_WIDTH], grad_x, grads


LARGE = {"w_in": "col", "w_pool_out": "col", "w_rnn_out": "row", "w_o": "row", "w_ffn_in": "col", "w_ffn_out": "row"}
LARGE_SHAPE = {"w_in": (D_MODEL, D_IN), "w_pool_out": (D_POOL, D_MODEL), "w_rnn_out": (D_RNN, D_MODEL),
               "w_o": (D_MODEL, D_MODEL), "w_ffn_in": (D_MODEL, 2 * D_FF), "w_ffn_out": (D_FF, D_MODEL)}


def _place():
    x, y, c = lax.axis_index("x"), lax.axis_index("y"), lax.axis_index("c")
    return 2 * x + y, c


def _chip_device(chip, c):
    return (chip // 2, chip % 2, c)


def _chip_window(ref, kind, shape, chip, half=None):
    K, N = shape
    if kind == "col":
        rows = slice(None) if half is None else pl.ds(half * (K // 2), K // 2)
        return ref.at[rows, pl.ds(chip * (N // N_CHIPS), N // N_CHIPS)]
    ks = K // N_CHIPS
    if half is None:
        return ref.at[pl.ds(chip * ks, ks), :]
    return ref.at[pl.ds(chip * ks + half * (ks // 2), ks // 2), :]


def _row_half(ref, half):
    rows = ref.shape[0] // 2
    return ref.at[pl.ds(half * rows, rows), :]


def _remote(win_src, win_dst, send_sems, recv_sems, idx, to):
    return pltpu.make_async_remote_copy(src_ref=win_src, dst_ref=win_dst, send_sem=send_sems.at[idx], recv_sem=recv_sems.at[idx],
                                        device_id=to, device_id_type=MESH)


def _gather_job(full, names, conv_w_full=None):
    n = len(names)
    cw_cols = D_RNN // N_CHIPS

    def windows(refs, chip, half):
        return [_chip_window(refs[k], LARGE[name], LARGE_SHAPE[name], chip, half) for k, name in enumerate(names)]

    def ici_copies(refs, send_sems, recv_sems, src_chip, dst_chip, c, r):
        wins = windows(refs, src_chip, c)
        if conv_w_full is not None:
            wins.append(refs[n].at[:, pl.ds(src_chip * cw_cols, cw_cols)])
        return [_remote(win, win, send_sems, recv_sems, (k, r), _chip_device(dst_chip, c)) for k, win in enumerate(wins)]

    def forwards(refs, send_sems, recv_sems, src_chip, half, to_core, chip, r):
        return [_remote(win, win, send_sems, recv_sems, (k, 3 + r), _chip_device(chip, to_core))
                for k, win in enumerate(windows(refs, src_chip, half))]

    def start(ins, outs, send_sems, recv_sems):
        chip, c = _place()
        for r in range(3):
            for cp in ici_copies(outs, send_sems, recv_sems, chip, chip ^ (r + 1), c, r):
                cp.start()

    def finish(ins, outs, send_sems, recv_sems):
        chip, c = _place()
        for r in range(3):
            for cp in ici_copies(outs, send_sems, recv_sems, chip ^ (r + 1), chip, c, r):
                cp.wait_recv()
            for cp in forwards(outs, send_sems, recv_sems, chip ^ (r + 1), c, 1 - c, chip, r):
                cp.start()
        for r in range(3):
            for cp in forwards(outs, send_sems, recv_sems, chip ^ (r + 1), 1 - c, c, chip, r):
                cp.wait_recv()
            for cp in ici_copies(outs, send_sems, recv_sems, chip, chip ^ (r + 1), c, r):
                cp.wait_send()
            for cp in forwards(outs, send_sems, recv_sems, chip ^ (r + 1), c, 1 - c, chip, r):
                cp.wait_send()

    arrays = [full[name] for name in names] + ([conv_w_full] if conv_w_full is not None else [])
    return _Job(arrays, [jax.ShapeDtypeStruct(a.shape, a.dtype) for a in arrays], {k: k for k in range(len(arrays))},
                (len(arrays), 6), start, finish)


def _core_halves(ref, kind, shape, c):
    return [_chip_window(ref, kind, shape, chip, c) for chip in range(N_CHIPS)]


def _sibling_job(grads, names):
    def start(ins, outs, send_sems, recv_sems):
        chip, c = _place()
        for k, name in enumerate(names):
            kind, shape = LARGE[name], LARGE_SHAPE[name]
            if kind == "col":
                pairs = [(_row_half(ins[k], 1 - c), outs[k])]
            else:
                rows = shape[0] // N_DEV
                pairs = [(win, outs[k].at[pl.ds(j * rows, rows), :]) for j, win in enumerate(_core_halves(ins[k], kind, shape, 1 - c))]
            for src, dst in pairs:
                _remote(src, dst, send_sems, recv_sems, k, _chip_device(chip, 1 - c)).start()

    def finish(ins, outs, send_sems, recv_sems):
        chip, c = _place()
        for k in range(len(names)):
            _remote(outs[k], outs[k], send_sems, recv_sems, k, _chip_device(chip, 1 - c)).wait()

    return _Job([grads[name] for name in names],
                [jax.ShapeDtypeStruct((LARGE_SHAPE[name][0] // 2, LARGE_SHAPE[name][1]), F32) for name in names], {},
                (len(names),), start, finish)


def _chip_sum(name, g, got, c):
    kind, (K, N) = LARGE[name], LARGE_SHAPE[name]
    rows = K // N_DEV

    def body(c_ref, g_ref, got_ref, o_ref, ob_ref):
        total = g_ref[...] + got_ref[...]
        o_ref[...] = total
        ob_ref[...] = total.astype(BF16)

    if kind == "col":
        mine = pl.BlockSpec((rows, N), lambda j, c_ref: (j + N_CHIPS * c_ref[0], 0))
    else:
        mine = pl.BlockSpec((rows, N), lambda j, c_ref: (2 * j + c_ref[0], 0))
    blk = pl.BlockSpec((rows, N), lambda j, c_ref: (j, 0))
    return pl.pallas_call(
        body, name=name + "_chip_sum",
        grid_spec=pltpu.PrefetchScalarGridSpec(num_scalar_prefetch=1, grid=(N_CHIPS,), in_specs=[mine, blk], out_specs=[blk, blk]),
        out_shape=[jax.ShapeDtypeStruct((K // 2, N), F32), jax.ShapeDtypeStruct((K // 2, N), BF16)],
        compiler_params=_params(dimension_semantics=("parallel",)),
    )(c, g, got)


def _piece(ref, kind, shape, chip):
    K, N = shape
    if kind == "col":
        return ref.at[:, pl.ds(chip * (N // N_CHIPS), N // N_CHIPS)]
    return ref.at[pl.ds(chip * (K // N_DEV), K // N_DEV), :]


def _piece_shape(name):
    kind, (K, N) = LARGE[name], LARGE_SHAPE[name]
    return (K // 2, N // N_CHIPS) if kind == "col" else (K // N_DEV, N)


def _chips_job(sums, names):
    def copies(ins, outs, send_sems, recv_sems):
        chip, c = _place()
        return [_remote(_piece(ins[k], LARGE[name], LARGE_SHAPE[name], chip ^ (r + 1)), outs[k].at[r], send_sems, recv_sems, (k, r),
                        _chip_device(chip ^ (r + 1), c)) for k, name in enumerate(names) for r in range(3)]

    def start(*refs):
        for cp in copies(*refs):
            cp.start()

    def finish(*refs):
        for cp in copies(*refs):
            cp.wait()

    return _Job([sums[name] for name in names], [jax.ShapeDtypeStruct((3,) + _piece_shape(name), BF16) for name in names], {},
                (len(names), 3), start, finish)


def _final_sum(name, chip_sum, got, place):
    kind = LARGE[name]
    rows, cols = _piece_shape(name)

    def body(place_ref, s_ref, got_ref, o_ref):
        o_ref[...] = ((s_ref[...] + got_ref[0].astype(F32)) + got_ref[1].astype(F32)) + got_ref[2].astype(F32)

    if kind == "col":
        mine = pl.BlockSpec((rows, cols), lambda i, place_ref: (0, place_ref[0]))
    else:
        mine = pl.BlockSpec((rows, cols), lambda i, place_ref: (place_ref[0], 0))
    return pl.pallas_call(
        body, name=name + "_final_sum",
        grid_spec=pltpu.PrefetchScalarGridSpec(
            num_scalar_prefetch=1, grid=(1,), in_specs=[mine, pl.BlockSpec((3, rows, cols), lambda i, place_ref: (0, 0, 0))],
            out_specs=pl.BlockSpec((rows, cols), lambda i, place_ref: (place_ref[1], 0))),
        out_shape=jax.ShapeDtypeStruct((2 * rows, cols), F32),
        compiler_params=_params(dimension_semantics=("arbitrary",)),
    )(place, chip_sum, got)


def _join_job(shards, names):
    def half_copy(outs, send_sems, recv_sems, k, mine):
        chip, c = _place()
        win = _row_half(outs[k], c if mine else 1 - c)
        return _remote(win, win, send_sems, recv_sems, k, _chip_device(chip, 1 - c))

    def start(ins, outs, send_sems, recv_sems):
        for k in range(len(names)):
            half_copy(outs, send_sems, recv_sems, k, True).start()

    def finish(ins, outs, send_sems, recv_sems):
        for k in range(len(names)):
            half_copy(outs, send_sems, recv_sems, k, True).wait_send()
            half_copy(outs, send_sems, recv_sems, k, False).wait_recv()

    arrays = [shards[name] for name in names]
    return _Job(arrays, [jax.ShapeDtypeStruct(a.shape, F32) for a in arrays], {k: k for k in range(len(arrays))},
                (len(arrays),), start, finish)


VEC_ROWS = 16


def _all_reduce_small(slabs, jobs=()):
    n = len(slabs)

    def body(*refs):
        in_refs, out_refs, got_refs = refs[:n], refs[n:2 * n], refs[2 * n:3 * n]
        send_sems, recv_sems = refs[3 * n:]
        x, y, c = lax.axis_index("x"), lax.axis_index("y"), lax.axis_index("c")
        me = 4 * x + 2 * y + c

        def remote(src, dst, k, phase, r):
            other = me ^ r
            return pltpu.make_async_remote_copy(src_ref=src, dst_ref=dst, send_sem=send_sems.at[k, phase, r],
                                                recv_sem=recv_sems.at[k, phase, r],
                                                device_id=(other // 4, (other // 2) % 2, other % 2), device_id_type=MESH)

        scatter = [remote(in_refs[k].at[me ^ r], got_refs[k].at[r], k, 0, r) for r in range(1, N_DEV) for k in range(n)]
        for cp in scatter:
            cp.start()
        for cp in scatter:
            cp.wait()
        for k in range(n):
            total = in_refs[k][me]
            for r in range(1, N_DEV):
                total = total + got_refs[k][r]
            out_refs[k][me] = total
        gather = [remote(out_refs[k].at[me], out_refs[k].at[me], k, 1, r) for r in range(1, N_DEV) for k in range(n)]
        for cp in gather:
            cp.start()
        for r in range(1, N_DEV):
            for k in range(n):
                remote(out_refs[k].at[me ^ r], out_refs[k].at[me ^ r], k, 1, r).wait_recv()
        for cp in gather:
            cp.wait_send()

    return _pallas(
        body, slabs, name="all_reduce_small", grid=(), in_specs=[VMEM] * n, out_specs=[VMEM] * n,
        out_shape=[jax.ShapeDtypeStruct(s.shape, F32) for s in slabs],
        scratch_shapes=[pltpu.VMEM(s.shape, F32) for s in slabs]
        + [pltpu.SemaphoreType.DMA((n, 2, N_DEV)), pltpu.SemaphoreType.DMA((n, 2, N_DEV))], jobs=jobs)


def _cast_into_whole(w, name, place):
    rows, cols = w.shape
    tr = rows // 2

    def body(place_ref, w_ref, o_ref):
        o_ref[...] = w_ref[...].astype(BF16)

    if LARGE[name] == "col":
        window = pl.BlockSpec((tr, cols), lambda i, place_ref: (i, place_ref[0]))
    else:
        window = pl.BlockSpec((tr, cols), lambda i, place_ref: (2 * place_ref[0] + i, 0))
    return pl.pallas_call(
        body, name=name + "_cast",
        grid_spec=pltpu.PrefetchScalarGridSpec(num_scalar_prefetch=1, grid=(2,),
                                               in_specs=[pl.BlockSpec((tr, cols), lambda i, place_ref: (i, 0))], out_specs=window),
        out_shape=jax.ShapeDtypeStruct(LARGE_SHAPE[name], BF16),
        compiler_params=_params(dimension_semantics=("parallel",)))(place, w)


def _adamw_math(w, g, m, v):
    m = ADAM_B1 * m + (1.0 - ADAM_B1) * g
    v = ADAM_B2 * v + (1.0 - ADAM_B2) * (g * g)
    m_hat = m / (1.0 - ADAM_B1 ** ADAM_STEP)
    v_hat = v / (1.0 - ADAM_B2 ** ADAM_STEP)
    delta = -ADAM_LR * (m_hat / (jnp.sqrt(v_hat) + ADAM_EPS) + ADAM_WD * w)
    return delta, m, v


def _adamw_large(w, g, m, v, name):
    rows, cols = w.shape
    tr = rows // 4

    def body(w_ref, g_ref, m_ref, v_ref, d_ref, mo_ref, vo_ref):
        d_ref[...], mo_ref[...], vo_ref[...] = _adamw_math(w_ref[...], g_ref[...], m_ref[...], v_ref[...])

    blk = pl.BlockSpec((tr, cols), lambda i: (i, 0))
    out = jax.ShapeDtypeStruct(w.shape, F32)
    return pl.pallas_call(body, name=name + "_adamw", grid=(4,), in_specs=[blk] * 4, out_specs=[blk] * 3, out_shape=[out] * 3,
                          compiler_params=_params(dimension_semantics=("parallel",)))(w, g, m, v)


def _adamw_small(ws, gs, ms, vs):
    n = len(ws)

    def body(*refs):
        for k in range(n):
            w_ref, g_ref, m_ref, v_ref = (refs[q * n + k] for q in range(4))
            d_ref, mo_ref, vo_ref = (refs[(4 + q) * n + k] for q in range(3))
            d_ref[...], mo_ref[...], vo_ref[...] = _adamw_math(w_ref[...], g_ref[...], m_ref[...], v_ref[...])

    out = [jax.ShapeDtypeStruct(w.shape, F32) for w in ws]
    res = pl.pallas_call(body, name="small_adamw", in_specs=[VMEM] * (4 * n), out_specs=[VMEM] * (3 * n), out_shape=out * 3,
                         compiler_params=_params())(*ws, *gs, *ms, *vs)
    return res[:n], res[n:2 * n], res[2 * n:]


WEIGHTS = ["norm_mix", "w_in", "w_pool_grp", "pool_scale", "w_pool_out", "conv_w", "conv_b", "w_rg_a", "b_rg_a", "w_rg_x",
           "b_rg_x", "lru_lambda", "w_rnn_out", "w_o", "norm_ffn", "w_ffn_in", "w_ffn_out", "norm_final"]
VEC_ITEMS = ["norm_mix", "norm_ffn", "norm_final", "pool_scale", "conv_b", "lru_lambda", "b_rg_a", "b_rg_x"]
MAT_ITEMS = ["w_pool_grp", "w_rg_a", "w_rg_x"]


def _as2d(name, a):
    if name in MAT_ITEMS:
        return a.reshape(-1, HEAD, HEAD)
    if name == "conv_w":
        return a.reshape(CONV_WIDTH, -1)
    return a.reshape(1, -1)


def kernel(x, norm_mix, w_in, w_pool_grp, pool_scale, w_pool_out, conv_w, conv_b, w_rg_a, b_rg_a, w_rg_x, b_rg_x, lru_lambda, w_rnn_out, w_o, norm_ffn, w_ffn_in, w_ffn_out, norm_final, loss_target, m_norm_mix, m_w_in, m_w_pool_grp, m_pool_scale, m_w_pool_out, m_conv_w, m_conv_b, m_w_rg_a, m_b_rg_a, m_w_rg_x, m_b_rg_x, m_lru_lambda, m_w_rnn_out, m_w_o, m_norm_ffn, m_w_ffn_in, m_w_ffn_out, m_norm_final, v_norm_mix, v_w_in, v_w_pool_grp, v_pool_scale, v_w_pool_out, v_conv_w, v_conv_b, v_w_rg_a, v_b_rg_a, v_w_rg_x, v_b_rg_x, v_lru_lambda, v_w_rnn_out, v_w_o, v_norm_ffn, v_w_ffn_in, v_w_ffn_out, v_norm_final):
    given = dict(locals())
    w = {name: given[name] for name in WEIGHTS}
    m = {name: given["m_" + name] for name in WEIGHTS}
    v = {name: given["v_" + name] for name in WEIGHTS}
    chip, c = _place()

    place = jnp.stack([chip, c]).astype(jnp.int32)
    conv_cols = w["conv_w"].shape[-1]
    conv_w_mine = lax.dynamic_update_slice_in_dim(jnp.zeros((CONV_WIDTH, D_RNN), F32), w["conv_w"][0], chip * conv_cols, axis=1)
    full = {name: _cast_into_whole(w[name][0], name, place) for name in LARGE}
    small = {name: _as2d(name, w[name]) for name in WEIGHTS if name not in LARGE and name != "conv_w"}
    sq_cols, grad_x, grads = _step(x[0], loss_target[0], small, full, conv_w_mine, place)
    loss = 0.5 / D_MODEL * jnp.sum(sq_cols)
    grads["conv_w"] = lax.dynamic_slice_in_dim(grads["conv_w"], chip * conv_cols, conv_cols, axis=1)

    delta, new_m, new_v = {}, {}, {}
    for name in LARGE:
        delta[name], new_m[name], new_v[name] = _adamw_large(w[name][0], grads[name], m[name][0], v[name][0], name)
    small_names = [name for name in WEIGHTS if name not in LARGE]
    flat = lambda d: [d[name].reshape(grads[name].shape) for name in small_names]
    ds, mo, vo = _adamw_small(flat(w), [grads[name] for name in small_names], flat(m), flat(v))
    for k, name in enumerate(small_names):
        delta[name], new_m[name], new_v[name] = ds[k], mo[k], vo[k]

    shaped = lambda d: [d[name].reshape(w[name].shape) for name in WEIGHTS]
    return (loss, grad_x[None], *shaped(grads), *shaped(delta), *shaped(new_m), *shaped(new_v))
```

```python
import functools
import math

import jax
import jax.numpy as jnp
from jax import lax
from jax.experimental import pallas as pl
from jax.experimental.pallas import tpu as pltpu

F32 = jnp.float32
BF16 = jnp.bfloat16

D_MODEL = 1024
D_POOL = 512
N_POOL_GROUPS = 4
D_RNN = 1024
N_RNN_HEADS = 8
HEAD = 128
CONV_WIDTH = 4
LRU_C = 8.0
D_FF = 2816
D_IN = D_POOL + 2 * D_RNN + 2 * D_MODEL
NORM_EPS = 1e-6
COL_RNN = D_POOL // HEAD
COL_GATE = (D_POOL + D_RNN) // HEAD

ADAM_LR = 0.001
ADAM_B1 = 0.9
ADAM_B2 = 0.999
ADAM_EPS = 1e-08
ADAM_WD = 0.01
ADAM_STEP = 10

N_CHIPS = 4
N_DEV = 8
MESH = pl.DeviceIdType.MESH
ANY = pl.BlockSpec(memory_space=pl.ANY)
VMEM = pl.BlockSpec(memory_space=pltpu.VMEM)
VMEM_LIMIT_BYTES = 60 * 1024 * 1024
SUBLANES = 8
POOL_HALO = 16
CHUNK = 1024

GELU_C = math.sqrt(2.0 / math.pi)
GELU_A = 0.044715


def _params(**kw):
    return pltpu.CompilerParams(vmem_limit_bytes=VMEM_LIMIT_BYTES, **kw)


def _sigmoid(x):
    return 0.5 * jnp.tanh(0.5 * x) + 0.5


def _log1p(y):
    u = 1.0 + y
    d = u - 1.0
    return jnp.where(d == 0.0, y, jnp.log(u) * (y / jnp.where(d == 0.0, 1.0, d)))


def _gelu_parts(x):
    x2 = x * x
    th = jnp.tanh(GELU_C * (x + GELU_A * x * x2))
    g = 0.5 * x * (1.0 + th)
    dg = 0.5 * (1.0 + th) + 0.5 * x * (1.0 - th * th) * GELU_C * (1.0 + 3.0 * GELU_A * x2)
    return g, dg


def _dot(a, b):
    return jnp.dot(a, b, preferred_element_type=F32)


def _dot_nt(a, b):
    return lax.dot_general(a, b, (((1,), (1,)), ((), ())), preferred_element_type=F32)


def _dot_tn(a, b):
    return lax.dot_general(a, b, (((0,), (0,)), ((), ())), preferred_element_type=F32)


def _rms_scale(xv):
    return lax.rsqrt(jnp.mean(xv * xv, axis=-1, keepdims=True) + NORM_EPS)


def _rms_bwd(dy, xv, g):
    r = _rms_scale(xv)
    xh = xv * r
    dyg = dy * g
    dx = r * (dyg - xh * jnp.mean(dyg * xh, axis=-1, keepdims=True))
    return dx, dy * xh


class _Job:
    def __init__(self, inputs, out_shapes, aliases, sem_shape, start, finish):
        self.inputs, self.out_shapes, self.aliases, self.sem_shape = list(inputs), list(out_shapes), dict(aliases), sem_shape
        self.start, self.finish = start, finish


def _pallas(body, operands, *, name, grid, in_specs, out_specs, out_shape, scratch_shapes=(), semantics=None, jobs=()):
    n_in, n_out, n_scr = len(in_specs), len(out_specs), len(scratch_shapes)
    job_in = [a for job in jobs for a in job.inputs]
    job_out = [s for job in jobs for s in job.out_shapes]
    aliases, i0, o0 = {}, n_in, n_out
    for job in jobs:
        aliases.update({i0 + i: o0 + o for i, o in job.aliases.items()})
        i0, o0 = i0 + len(job.inputs), o0 + len(job.out_shapes)

    def whole(*refs):
        ins, j_ins = refs[:n_in], refs[n_in:n_in + len(job_in)]
        outs = refs[n_in + len(job_in):][:n_out]
        j_outs = refs[n_in + len(job_in) + n_out:][:len(job_out)]
        rest = refs[n_in + len(job_in) + n_out + len(job_out):]
        scr, sems = rest[:n_scr], rest[n_scr:]

        def run(phase):
            i, o = 0, 0
            for k, job in enumerate(jobs):
                getattr(job, phase)(j_ins[i:i + len(job.inputs)], j_outs[o:o + len(job.out_shapes)], sems[2 * k], sems[2 * k + 1])
                i, o = i + len(job.inputs), o + len(job.out_shapes)

        def at(step_of, phase):
            if not jobs:
                return
            if not grid:
                run(phase)
                return
            cond = functools.reduce(jnp.logical_and, [pl.program_id(d) == step_of(d) for d in range(len(grid))])
            pl.when(cond)(functools.partial(run, phase))

        at(lambda d: 0, "start")
        body(*ins, *outs, *scr)
        at(lambda d: grid[d] - 1, "finish")

    res = pl.pallas_call(
        whole, name=name, grid=grid, in_specs=list(in_specs) + [ANY] * len(job_in), out_specs=list(out_specs) + [ANY] * len(job_out),
        out_shape=list(out_shape) + job_out, input_output_aliases=aliases,
        scratch_shapes=list(scratch_shapes) + [pltpu.SemaphoreType.DMA(job.sem_shape) for job in jobs for _ in range(2)],
        compiler_params=_params(dimension_semantics=semantics, has_side_effects=bool(jobs)),
    )(*operands, *job_in)
    per_job, o = [], n_out
    for job in jobs:
        per_job.append(res[o:o + len(job.out_shapes)])
        o += len(job.out_shapes)
    return res[:n_out], per_job


def _run_jobs(jobs, name):
    return _pallas(lambda: None, [], name=name, grid=(), in_specs=[], out_specs=[], out_shape=[], jobs=jobs)[1]


NORM_ROWS = 256


def _norm_rows(x_ref, g_ref, h_ref):
    g = g_ref[...]

    def rows(i, carry):
        r = pl.ds(pl.multiple_of(i * NORM_ROWS, NORM_ROWS), NORM_ROWS)
        xv = x_ref[r, :]
        h_ref[r, :] = (xv * _rms_scale(xv) * g).astype(BF16)
        return carry

    lax.fori_loop(0, x_ref.shape[0] // NORM_ROWS, rows, 0)


def _norm_matmul(x, g, w, *, tm, tn, name, jobs=()):
    T, K = x.shape
    N = w.shape[1]

    def body(x_ref, g_ref, w_ref, o_ref, h_ref):
        @pl.when(pl.program_id(1) == 0)
        def _():
            _norm_rows(x_ref, g_ref, h_ref)

        o_ref[...] = _dot(h_ref[...], w_ref[...])

    return _pallas(
        body, (x, g, w), name=name, grid=(T // tm, N // tn),
        in_specs=[pl.BlockSpec((tm, K), lambda i, j: (i, 0)), pl.BlockSpec((1, K), lambda i, j: (0, 0)),
                  pl.BlockSpec((K, tn), lambda i, j: (0, j))],
        out_specs=[pl.BlockSpec((tm, tn), lambda i, j: (i, j)), pl.BlockSpec((tm, K), lambda i, j: (i, 0))],
        out_shape=[jax.ShapeDtypeStruct((T, N), F32), jax.ShapeDtypeStruct((T, K), BF16)],
        semantics=("parallel", "arbitrary"), jobs=jobs)


def _ffn_in(x2, g, w, *, tm, tn):
    T, K = x2.shape
    nb = D_FF // tn

    def body(x_ref, g_ref, wg_ref, wu_ref, dup_ref, dgate_ref, act_ref, h_ref):
        @pl.when(pl.program_id(1) == 0)
        def _():
            _norm_rows(x_ref, g_ref, h_ref)

        h = h_ref[...]
        gate = _dot(h, wg_ref[...])
        up = _dot(h, wu_ref[...])
        s = _sigmoid(gate)
        silu = gate * s
        dup_ref[...] = silu.astype(BF16)
        dgate_ref[...] = (up * (s + silu * (1.0 - s))).astype(BF16)
        act_ref[...] = (silu * up).astype(BF16)

    blk = pl.BlockSpec((tm, tn), lambda i, j: (i, j))
    return pl.pallas_call(
        body, name="ffn_in", grid=(T // tm, nb),
        in_specs=[pl.BlockSpec((tm, K), lambda i, j: (i, 0)), pl.BlockSpec((1, K), lambda i, j: (0, 0)),
                  pl.BlockSpec((K, tn), lambda i, j: (0, j)), pl.BlockSpec((K, tn), lambda i, j: (0, j + nb))],
        out_specs=[blk, blk, blk, pl.BlockSpec((tm, K), lambda i, j: (i, 0))],
        out_shape=[jax.ShapeDtypeStruct((T, D_FF), BF16), jax.ShapeDtypeStruct((T, D_FF), BF16),
                   jax.ShapeDtypeStruct((T, D_FF), BF16), jax.ShapeDtypeStruct((T, K), BF16)],
        compiler_params=_params(dimension_semantics=("parallel", "arbitrary")),
    )(x2, g, w, w)


def _branch_mix(pm, z, w_pool_out, w_rnn_out, proj, *, tm, tn):
    T = pm.shape[0]
    col_gp = (D_POOL + 2 * D_RNN) // tn
    col_gr = col_gp + D_MODEL // tn

    def body(pm_ref, z_ref, wp_ref, wr_ref, gp_ref, gr_ref, by_gp_ref, by_gr_ref, sp_ref, sr_ref, mix_ref):
        yp = _dot(pm_ref[...], wp_ref[...])
        yr = _dot(z_ref[...], wr_ref[...])
        sp, sr = _sigmoid(gp_ref[...]), _sigmoid(gr_ref[...])
        by_gp_ref[...] = (yp * sp * (1.0 - sp)).astype(BF16)
        by_gr_ref[...] = (yr * sr * (1.0 - sr)).astype(BF16)
        sp_ref[...] = sp.astype(BF16)
        sr_ref[...] = sr.astype(BF16)
        mix_ref[...] = (sp * yp + sr * yr).astype(BF16)

    blk = pl.BlockSpec((tm, tn), lambda i, j: (i, j))
    out = jax.ShapeDtypeStruct((T, D_MODEL), BF16)
    return pl.pallas_call(
        body, name="branch_mix", grid=(T // tm, D_MODEL // tn),
        in_specs=[pl.BlockSpec((tm, D_POOL), lambda i, j: (i, 0)), pl.BlockSpec((tm, D_RNN), lambda i, j: (i, 0)),
                  pl.BlockSpec((D_POOL, tn), lambda i, j: (0, j)), pl.BlockSpec((D_RNN, tn), lambda i, j: (0, j)),
                  pl.BlockSpec((tm, tn), lambda i, j: (i, col_gp + j)), pl.BlockSpec((tm, tn), lambda i, j: (i, col_gr + j))],
        out_specs=[blk] * 5, out_shape=[out] * 5,
        compiler_params=_params(dimension_semantics=("parallel", "parallel")),
    )(pm, z, w_pool_out, w_rnn_out, proj, proj)


def _out_proj_residual(mix, w_o, x, *, tm):
    T = x.shape[0]

    def body(mix_ref, w_ref, x_ref, o_ref):
        o_ref[...] = x_ref[...] + _dot(mix_ref[...], w_ref[...])

    row = pl.BlockSpec((tm, D_MODEL), lambda i: (i, 0))
    return pl.pallas_call(
        body, name="out_proj_residual", grid=(T // tm,),
        in_specs=[row, pl.BlockSpec((D_MODEL, D_MODEL), lambda i: (0, 0)), row],
        out_specs=row, out_shape=jax.ShapeDtypeStruct((T, D_MODEL), F32),
        compiler_params=_params(dimension_semantics=("parallel",)),
    )(mix, w_o, x)


def _ffn_out_loss(act, w, x2, g3, target, *, tm):
    T = x2.shape[0]

    def body(act_ref, w_ref, x2_ref, g_ref, t_ref, dx_ref, dxb_ref, sq_ref, dg_ref):
        @pl.when(pl.program_id(0) == 0)
        def _():
            sq_ref[...] = jnp.zeros_like(sq_ref)
            dg_ref[...] = jnp.zeros_like(dg_ref)

        x3 = x2_ref[...] + _dot(act_ref[...], w_ref[...])
        g = g_ref[...]
        err = x3 * _rms_scale(x3) * g - t_ref[...]
        sq_ref[...] += jnp.sum(err * err, axis=0, keepdims=True)
        dx, dgp = _rms_bwd(err * (1.0 / D_MODEL), x3, g)
        dg_ref[...] += jnp.sum(dgp, axis=0, keepdims=True)
        dx_ref[...] = dx
        dxb_ref[...] = dx.astype(BF16)

    row = pl.BlockSpec((tm, D_MODEL), lambda i: (i, 0))
    vec = pl.BlockSpec((1, D_MODEL), lambda i: (0, 0))
    return pl.pallas_call(
        body, name="ffn_out_loss", grid=(T // tm,),
        in_specs=[pl.BlockSpec((tm, D_FF), lambda i: (i, 0)), pl.BlockSpec((D_FF, D_MODEL), lambda i: (0, 0)), row, vec, row],
        out_specs=[row, row, vec, vec],
        out_shape=[jax.ShapeDtypeStruct((T, D_MODEL), F32), jax.ShapeDtypeStruct((T, D_MODEL), BF16),
                   jax.ShapeDtypeStruct((1, D_MODEL), F32), jax.ShapeDtypeStruct((1, D_MODEL), F32)],
        compiler_params=_params(dimension_semantics=("arbitrary",)),
    )(act, w, x2, g3, target)


def _ffn_out_bwd(dx3b, w, act_by_gate, act_by_up, *, tm, tn):
    T = dx3b.shape[0]

    def body(dx_ref, w_ref, by_gate_ref, by_up_ref, dgate_ref, dup_ref):
        dact = _dot_nt(dx_ref[...], w_ref[...])
        dgate_ref[...] = (dact * by_gate_ref[...].astype(F32)).astype(BF16)
        dup_ref[...] = (dact * by_up_ref[...].astype(F32)).astype(BF16)

    blk = pl.BlockSpec((tm, tn), lambda i, j: (i, j))
    return pl.pallas_call(
        body, name="ffn_out_bwd", grid=(T // tm, D_FF // tn),
        in_specs=[pl.BlockSpec((tm, D_MODEL), lambda i, j: (i, 0)), pl.BlockSpec((tn, D_MODEL), lambda i, j: (j, 0)), blk, blk],
        out_specs=[blk, blk],
        out_shape=[jax.ShapeDtypeStruct((T, D_FF), BF16), jax.ShapeDtypeStruct((T, D_FF), BF16)],
        compiler_params=_params(dimension_semantics=("parallel", "parallel")),
    )(dx3b, w, act_by_gate, act_by_up)


def _ffn_in_bwd(dgate, dup, w, dx3, x2, g2, *, tm):
    T = x2.shape[0]

    def body(dgate_ref, dup_ref, w_ref, dx3_ref, x2_ref, g_ref, dx_ref, dxb_ref, dg_ref):
        @pl.when(pl.program_id(0) == 0)
        def _():
            dg_ref[...] = jnp.zeros_like(dg_ref)

        dh = _dot_nt(dgate_ref[...], w_ref[:, :D_FF]) + _dot_nt(dup_ref[...], w_ref[:, D_FF:])
        dxn, dgp = _rms_bwd(dh, x2_ref[...], g_ref[...])
        dx = dx3_ref[...] + dxn
        dg_ref[...] += jnp.sum(dgp, axis=0, keepdims=True)
        dx_ref[...] = dx
        dxb_ref[...] = dx.astype(BF16)

    row = pl.BlockSpec((tm, D_MODEL), lambda i: (i, 0))
    wide = pl.BlockSpec((tm, D_FF), lambda i: (i, 0))
    vec = pl.BlockSpec((1, D_MODEL), lambda i: (0, 0))
    return pl.pallas_call(
        body, name="ffn_in_bwd", grid=(T // tm,),
        in_specs=[wide, wide, pl.BlockSpec((D_MODEL, 2 * D_FF), lambda i: (0, 0)), row, row, vec],
        out_specs=[row, row, vec],
        out_shape=[jax.ShapeDtypeStruct((T, D_MODEL), F32), jax.ShapeDtypeStruct((T, D_MODEL), BF16),
                   jax.ShapeDtypeStruct((1, D_MODEL), F32)],
        compiler_params=_params(dimension_semantics=("arbitrary",)),
    )(dgate, dup, w, dx3, x2, g2)


def _out_proj_bwd(dx2b, w_o, mix_by, *, tm, tn, jobs=()):
    T = dx2b.shape[0]

    def body(dx_ref, w_ref, *refs):
        dmix = _dot_nt(dx_ref[...], w_ref[...])
        for by_ref, d_ref in zip(refs[:4], refs[4:]):
            d_ref[...] = (dmix * by_ref[...].astype(F32)).astype(BF16)

    blk = pl.BlockSpec((tm, tn), lambda i, j: (i, j))
    out = jax.ShapeDtypeStruct((T, D_MODEL), BF16)
    return _pallas(
        body, (dx2b, w_o, *mix_by), name="out_proj_bwd", grid=(T // tm, D_MODEL // tn),
        in_specs=[pl.BlockSpec((tm, D_MODEL), lambda i, j: (i, 0)), pl.BlockSpec((tn, D_MODEL), lambda i, j: (j, 0))] + [blk] * 4,
        out_specs=[blk] * 4, out_shape=[out] * 4, semantics=("parallel", "parallel"), jobs=jobs)


def _branch_bwd(dyp, dyr, w_pool_out, w_rnn_out, *, tm):
    T = dyp.shape[0]

    def body(dyp_ref, dyr_ref, wp_ref, wr_ref, dpm_ref, dz_ref):
        dpm_ref[...] = _dot_nt(dyp_ref[...], wp_ref[...])
        dz_ref[...] = _dot_nt(dyr_ref[...], wr_ref[...])

    row = pl.BlockSpec((tm, D_MODEL), lambda i: (i, 0))
    return pl.pallas_call(
        body, name="branch_bwd", grid=(T // tm,),
        in_specs=[row, row, pl.BlockSpec((D_POOL, D_MODEL), lambda i: (0, 0)), pl.BlockSpec((D_RNN, D_MODEL), lambda i: (0, 0))],
        out_specs=[pl.BlockSpec((tm, D_POOL), lambda i: (i, 0)), pl.BlockSpec((tm, D_RNN), lambda i: (i, 0))],
        out_shape=[jax.ShapeDtypeStruct((T, D_POOL), F32), jax.ShapeDtypeStruct((T, D_RNN), F32)],
        compiler_params=_params(dimension_semantics=("parallel",)),
    )(dyp, dyr, w_pool_out, w_rnn_out)


def _in_proj_bwd(segs, w, dx2, x, g1, *, tm, jobs=()):
    T = x.shape[0]
    widths = [s.shape[1] for s in segs]
    offs = [sum(widths[:k]) for k in range(len(widths))]
    n = len(segs)

    def body(*refs):
        seg_refs, (w_ref, dx2_ref, x_ref, g_ref, dx_ref, dg_ref) = refs[:n], refs[n:]

        @pl.when(pl.program_id(0) == 0)
        def _():
            dg_ref[...] = jnp.zeros_like(dg_ref)

        dh = _dot_nt(seg_refs[0][...], w_ref[:, offs[0]:offs[0] + widths[0]])
        for k in range(1, n):
            dh += _dot_nt(seg_refs[k][...], w_ref[:, offs[k]:offs[k] + widths[k]])
        dxn, dgp = _rms_bwd(dh, x_ref[...], g_ref[...])
        dg_ref[...] += jnp.sum(dgp, axis=0, keepdims=True)
        dx_ref[...] = dx2_ref[...] + dxn

    row = pl.BlockSpec((tm, D_MODEL), lambda i: (i, 0))
    vec = pl.BlockSpec((1, D_MODEL), lambda i: (0, 0))
    return _pallas(
        body, (*segs, w, dx2, x, g1), name="in_proj_bwd", grid=(T // tm,),
        in_specs=[pl.BlockSpec((tm, wd), lambda i: (i, 0)) for wd in widths]
        + [pl.BlockSpec((D_MODEL, D_IN), lambda i: (0, 0)), row, row, vec],
        out_specs=[row, vec],
        out_shape=[jax.ShapeDtypeStruct((T, D_MODEL), F32), jax.ShapeDtypeStruct((1, D_MODEL), F32)],
        semantics=("arbitrary",), jobs=jobs)


def _weight_grad(a, segs, *, tm, tn, name, jobs=None):
    T, M = a.shape
    nblk = [s.shape[1] // tn for s in segs]
    first = [sum(nblk[:k]) for k in range(len(segs))]
    n = len(segs)

    def body(a_ref, *refs):
        seg_refs, o_ref = refs[:n], refs[n]
        j = pl.program_id(1)
        for k in range(n):
            @pl.when((j >= first[k]) & (j < first[k] + nblk[k]))
            def _(k=k):
                o_ref[...] = _dot_tn(a_ref[...], seg_refs[k][...])

    def seg_spec(k):
        return pl.BlockSpec((T, tn), lambda i, j: (0, jnp.clip(j - first[k], 0, nblk[k] - 1)))

    (grad,), results = _pallas(
        body, (a, *segs), name=name, grid=(M // tm, sum(nblk)),
        in_specs=[pl.BlockSpec((T, tm), lambda i, j: (0, i))] + [seg_spec(k) for k in range(n)],
        out_specs=[pl.BlockSpec((tm, tn), lambda i, j: (i, j))],
        out_shape=[jax.ShapeDtypeStruct((M, sum(nblk) * tn), F32)],
        semantics=("parallel", "arbitrary"), jobs=jobs or ())
    return grad if jobs is None else (grad, results)


def _pad_front(dst, src, halo):
    dst[pl.ds(0, halo), :] = jnp.zeros((halo, src.shape[1]), F32)

    def fill(i, carry):
        r0 = pl.multiple_of(i * CHUNK, CHUNK)
        dst[pl.ds(r0 + halo, CHUNK), :] = src[pl.ds(r0, CHUNK), :]
        return carry

    lax.fori_loop(0, src.shape[0] // CHUNK, fill, 0)


def _shift_rows(v, k):
    return pltpu.roll(v, k % v.shape[0], axis=0)


def _window_sums(xs, direction):
    s2 = xs + _shift_rows(xs, direction)
    s4 = s2 + _shift_rows(s2, 2 * direction)
    s8 = s4 + _shift_rows(s4, 4 * direction)
    s16 = s8 + _shift_rows(s8, 8 * direction)
    return s2, s4, s8, s16


def _select_window(g, sums):
    s2, s4, s8, s16 = sums
    return jnp.where(g == 0, s2, jnp.where(g == 1, s4, jnp.where(g == 2, s8, s16)))


def _pool_count(g, start, rows):
    t = start + lax.broadcasted_iota(jnp.int32, (rows, 1), 0)
    return jnp.minimum(t + 1, jnp.left_shift(2, g)).astype(F32)


def _pool_fwd(proj, w_grp, scale):
    T = proj.shape[0]
    nchunk = T // CHUNK

    def body(u_ref, w_ref, s_ref, o_ref, upad):
        g = pl.program_id(0)
        _pad_front(upad, u_ref, POOL_HALO)
        w = w_ref[...].astype(BF16)
        scale_row = s_ref[...]

        def chunk(i, carry):
            r0 = pl.multiple_of(i * CHUNK, CHUNK)
            xs = upad[pl.ds(r0, CHUNK + POOL_HALO), :]
            win = _select_window(g, _window_sums(xs, 1))[POOL_HALO:]
            pooled = win / _pool_count(g, r0, CHUNK) - xs[POOL_HALO:]
            o_ref[pl.ds(r0, CHUNK), :] = (_dot(pooled.astype(BF16), w) * scale_row).astype(BF16)
            return carry

        lax.fori_loop(0, nchunk, chunk, 0)

    return pl.pallas_call(
        body, name="pool_fwd", grid=(N_POOL_GROUPS,),
        in_specs=[pl.BlockSpec((T, HEAD), lambda g: (0, g)), pl.BlockSpec((None, HEAD, HEAD), lambda g: (g, 0, 0)),
                  pl.BlockSpec((1, HEAD), lambda g: (0, g))],
        out_specs=pl.BlockSpec((T, HEAD), lambda g: (0, g)),
        out_shape=jax.ShapeDtypeStruct((T, D_POOL), BF16),
        scratch_shapes=[pltpu.VMEM((T + POOL_HALO, HEAD), F32)],
        compiler_params=_params(dimension_semantics=("parallel",)),
    )(proj, w_grp, scale)


def _pool_bwd(proj, dpm, w_grp, scale, jobs=()):
    T = proj.shape[0]
    nchunk = T // CHUNK

    def body(u_ref, dpm_ref, w_ref, s_ref, du_ref, dw_ref, ds_ref, upad, zpad, dpool):
        g = pl.program_id(0)
        _pad_front(upad, u_ref, POOL_HALO)
        zpad[pl.ds(T, POOL_HALO), :] = jnp.zeros((POOL_HALO, HEAD), F32)
        dw_ref[...] = jnp.zeros_like(dw_ref)
        ds_ref[...] = jnp.zeros_like(ds_ref)
        w = w_ref[...].astype(BF16)
        scale_row = s_ref[...]

        def chunk(i, carry):
            r0 = pl.multiple_of(i * CHUNK, CHUNK)
            xs = upad[pl.ds(r0, CHUNK + POOL_HALO), :]
            cnt = _pool_count(g, r0, CHUNK)
            pooled = (_select_window(g, _window_sums(xs, 1))[POOL_HALO:] / cnt - xs[POOL_HALO:]).astype(BF16)
            mixed = _dot(pooled, w)
            d = dpm_ref[pl.ds(r0, CHUNK), :]
            ds_ref[...] += jnp.sum(d * mixed, axis=0, keepdims=True)
            dmixed = (d * scale_row).astype(BF16)
            dw_ref[...] += _dot_tn(pooled, dmixed)
            dp = _dot_nt(dmixed, w)
            dpool[pl.ds(r0, CHUNK), :] = dp
            zpad[pl.ds(r0, CHUNK), :] = dp / cnt
            return carry

        lax.fori_loop(0, nchunk, chunk, 0)

        def chunk2(i, carry):
            r0 = pl.multiple_of(i * CHUNK, CHUNK)
            zs = zpad[pl.ds(r0, CHUNK + POOL_HALO), :]
            win = _select_window(g, _window_sums(zs, -1))[:CHUNK]
            du_ref[pl.ds(r0, CHUNK), :] = (win - dpool[pl.ds(r0, CHUNK), :]).astype(BF16)
            return carry

        lax.fori_loop(0, nchunk, chunk2, 0)

    col = pl.BlockSpec((T, HEAD), lambda g: (0, g))
    return _pallas(
        body, (proj, dpm, w_grp, scale), name="pool_bwd", grid=(N_POOL_GROUPS,),
        in_specs=[col, col, pl.BlockSpec((None, HEAD, HEAD), lambda g: (g, 0, 0)), pl.BlockSpec((1, HEAD), lambda g: (0, g))],
        out_specs=[col, pl.BlockSpec((None, HEAD, HEAD), lambda g: (g, 0, 0)), pl.BlockSpec((1, HEAD), lambda g: (0, g))],
        out_shape=[jax.ShapeDtypeStruct((T, D_POOL), BF16), jax.ShapeDtypeStruct((N_POOL_GROUPS, HEAD, HEAD), F32),
                   jax.ShapeDtypeStruct((1, D_POOL), F32)],
        scratch_shapes=[pltpu.VMEM((T + POOL_HALO, HEAD), F32), pltpu.VMEM((T + POOL_HALO, HEAD), F32), pltpu.VMEM((T, HEAD), F32)],
        semantics=("parallel",), jobs=jobs)


def _conv_taps(xs, cw):
    v = cw[CONV_WIDTH - 1] * xs[SUBLANES:]
    for k in range(CONV_WIDTH - 1):
        v += cw[k] * _shift_rows(xs, CONV_WIDTH - 1 - k)[SUBLANES:]
    return v


def _tap_rows(cw_ref):
    return [cw_ref[k:k + 1, :] for k in range(CONV_WIDTH)]


def _softplus_neg(lam):
    return jnp.maximum(-lam, 0.0) + _log1p(jnp.exp(-jnp.abs(lam)))


def _lru_gates(v, wa, ba, wx, bx, sp):
    vb = v.astype(BF16)
    ra = _sigmoid(_dot(vb, wa) + ba)
    ix = _sigmoid(_dot(vb, wx) + bx)
    log_a = -LRU_C * ra * sp
    a = jnp.exp(log_a)
    sq = jnp.sqrt(-jnp.tanh(log_a) * (a * a + 1.0))
    return ra, ix, a, sq


def _row_bcast(v, r):
    return jnp.broadcast_to(v[r:r + 1, :], v.shape)


TILE_BLOCK = 128


def _scan_in_tiles(coef, coef_shift, A_out, B, T, direction):
    order = list(range(SUBLANES)) if direction == 1 else list(range(SUBLANES - 1, -1, -1))
    tiles = min(TILE_BLOCK, T // SUBLANES)
    for base in range(0, T, tiles * SUBLANES):
        def rows(r, base=base):
            return pl.ds(base + r, tiles, stride=SUBLANES)

        A, Bv = coef[rows(order[0] + coef_shift), :], B[rows(order[0]), :]
        A_out[rows(order[0]), :] = A
        for r in order[1:]:
            a = coef[rows(r + coef_shift), :]
            Bv = a * Bv + B[rows(r), :]
            A = a * A
            A_out[rows(r), :] = A
            B[rows(r), :] = Bv


TILES_PER_STEP = 8


def _carry_tiles(A_s, B_s, out, ntile, direction):
    out_row = SUBLANES - 1 if direction == 1 else 0

    def step(k, carry):
        for j in range(TILES_PER_STEP):
            t = k * TILES_PER_STEP + j
            r0 = pl.multiple_of((t if direction == 1 else ntile - 1 - t) * SUBLANES, SUBLANES)
            A, B = A_s[pl.ds(r0, SUBLANES), :], B_s[pl.ds(r0, SUBLANES), :]
            out[pl.ds(r0, SUBLANES), :] = A * carry + B
            carry = _row_bcast(A, out_row) * carry + _row_bcast(B, out_row)
        return carry

    lax.fori_loop(0, ntile // TILES_PER_STEP, step, jnp.zeros((SUBLANES, HEAD), F32))


def _rnn_fwd(proj, conv_w, conv_b, w_a, b_a, w_x, b_x, lam, jobs=()):
    T = proj.shape[0]
    nchunk = T // CHUNK
    ntile = T // SUBLANES

    def body(u_ref, ug_ref, cw_ref, cb_ref, wa_ref, ba_ref, wx_ref, bx_ref, lam_ref, h_ref, z_ref, upad, a_s, b_s):
        _pad_front(upad, u_ref, SUBLANES)
        cw, cb = _tap_rows(cw_ref), cb_ref[...]
        wa, wx = wa_ref[...].astype(BF16), wx_ref[...].astype(BF16)
        ba, bx = ba_ref[...], bx_ref[...]
        sp = _softplus_neg(lam_ref[...])

        def chunk(i, carry):
            r0 = pl.multiple_of(i * CHUNK, CHUNK)
            v = _conv_taps(upad[pl.ds(r0, CHUNK + SUBLANES), :], cw) + cb
            _, ix, a, sq = _lru_gates(v, wa, ba, wx, bx, sp)
            a_s[pl.ds(r0, CHUNK), :], b_s[pl.ds(r0, CHUNK), :] = a, sq * ix * v
            return carry

        lax.fori_loop(0, nchunk, chunk, 0)
        _scan_in_tiles(a_s, 0, a_s, b_s, T, 1)
        _carry_tiles(a_s, b_s, h_ref, ntile, 1)

        def chunk3(i, carry):
            r0 = pl.multiple_of(i * CHUNK, CHUNK)
            gl, _ = _gelu_parts(ug_ref[pl.ds(r0, CHUNK), :])
            z_ref[pl.ds(r0, CHUNK), :] = (h_ref[pl.ds(r0, CHUNK), :] * gl).astype(BF16)
            return carry

        lax.fori_loop(0, nchunk, chunk3, 0)

    col = pl.BlockSpec((T, HEAD), lambda h: (0, h))
    vec = pl.BlockSpec((1, HEAD), lambda h: (0, h))
    mat = pl.BlockSpec((None, HEAD, HEAD), lambda h: (h, 0, 0))
    return _pallas(
        body, (proj, proj, conv_w, conv_b, w_a, b_a, w_x, b_x, lam), name="rnn_fwd", grid=(N_RNN_HEADS,),
        in_specs=[pl.BlockSpec((T, HEAD), lambda h: (0, COL_RNN + h)), pl.BlockSpec((T, HEAD), lambda h: (0, COL_GATE + h)),
                  pl.BlockSpec((CONV_WIDTH, HEAD), lambda h: (0, h)), vec, mat, vec, mat, vec, vec],
        out_specs=[col, col],
        out_shape=[jax.ShapeDtypeStruct((T, D_RNN), F32), jax.ShapeDtypeStruct((T, D_RNN), BF16)],
        scratch_shapes=[pltpu.VMEM((T + SUBLANES, HEAD), F32), pltpu.VMEM((T, HEAD), F32), pltpu.VMEM((T, HEAD), F32)],
        semantics=("parallel",), jobs=jobs)


def _rnn_bwd(proj, hr, dz, conv_w, conv_b, w_a, b_a, w_x, b_x, lam, jobs=()):
    T = proj.shape[0]
    nchunk = T // CHUNK
    ntile = T // SUBLANES

    def body(u_ref, ug_ref, h_ref, dz_ref, cw_ref, cb_ref, wa_ref, ba_ref, wx_ref, bx_ref, lam_ref,
             du_ref, dug_ref, dwa_ref, dwx_ref, dba_ref, dbx_ref, dlam_ref, dcb_ref, dcw_ref,
             upad, hpad, apad, v_s, ra_s, ix_s, sq_s, g_s, dvpad, ga_s):
        zero_tile = jnp.zeros((SUBLANES, HEAD), F32)
        _pad_front(upad, u_ref, SUBLANES)
        _pad_front(hpad, h_ref, SUBLANES)
        apad[pl.ds(T, SUBLANES), :] = zero_tile
        dvpad[pl.ds(T, SUBLANES), :] = zero_tile
        for ref in (dwa_ref, dwx_ref, dba_ref, dbx_ref, dlam_ref, dcb_ref, dcw_ref):
            ref[...] = jnp.zeros_like(ref)
        cw, cb = _tap_rows(cw_ref), cb_ref[...]
        wa, wx = wa_ref[...].astype(BF16), wx_ref[...].astype(BF16)
        ba, bx = ba_ref[...], bx_ref[...]
        lam_row = lam_ref[...]
        sp = _softplus_neg(lam_row)

        def chunk(i, carry):
            r0 = pl.multiple_of(i * CHUNK, CHUNK)
            rows = pl.ds(r0, CHUNK)
            h = h_ref[rows, :]
            v = _conv_taps(upad[pl.ds(r0, CHUNK + SUBLANES), :], cw) + cb
            ra, ix, a, sq = _lru_gates(v, wa, ba, wx, bx, sp)
            v_s[rows, :], ra_s[rows, :], ix_s[rows, :], sq_s[rows, :], apad[rows, :] = v, ra, ix, sq, a
            gl, dgl = _gelu_parts(ug_ref[rows, :])
            d = dz_ref[rows, :]
            g_s[rows, :] = d * gl
            dug_ref[rows, :] = (d * h * dgl).astype(BF16)
            return carry

        lax.fori_loop(0, nchunk, chunk, 0)

        _scan_in_tiles(apad, 1, ga_s, g_s, T, -1)
        _carry_tiles(ga_s, g_s, g_s, ntile, -1)

        def chunk3(i, carry):
            r0 = pl.multiple_of(i * CHUNK, CHUNK)
            rows = pl.ds(r0, CHUNK)
            g = g_s[rows, :]
            h_prev = _shift_rows(hpad[pl.ds(r0, CHUNK + SUBLANES), :], 1)[SUBLANES:]
            v, ra, ix, sq, a = v_s[rows, :], ra_s[rows, :], ix_s[rows, :], sq_s[rows, :], apad[rows, :]
            d_sq = g * ix * v
            d_ix = g * sq * v
            d_la = a * g * h_prev - d_sq * a * a / sq
            dlam_ref[...] += jnp.sum(d_la * ra, axis=0, keepdims=True)
            d_pa = d_la * (-LRU_C) * sp * ra * (1.0 - ra)
            d_px = d_ix * ix * (1.0 - ix)
            vb, d_pab, d_pxb = v.astype(BF16), d_pa.astype(BF16), d_px.astype(BF16)
            dwa_ref[...] += _dot_tn(vb, d_pab)
            dwx_ref[...] += _dot_tn(vb, d_pxb)
            dba_ref[...] += jnp.sum(d_pa, axis=0, keepdims=True)
            dbx_ref[...] += jnp.sum(d_px, axis=0, keepdims=True)
            dv = g * sq * ix + _dot_nt(d_pab, wa) + _dot_nt(d_pxb, wx)
            dvpad[rows, :] = dv
            dcb_ref[...] += jnp.sum(dv, axis=0, keepdims=True)
            xs = upad[pl.ds(r0, CHUNK + SUBLANES), :]
            for k in range(CONV_WIDTH):
                u_k = _shift_rows(xs, CONV_WIDTH - 1 - k)[SUBLANES:] if k < CONV_WIDTH - 1 else xs[SUBLANES:]
                dcw_ref[k:k + 1, :] += jnp.sum(dv * u_k, axis=0, keepdims=True)
            return carry

        lax.fori_loop(0, nchunk, chunk3, 0)
        dlam_ref[...] = dlam_ref[...] * (LRU_C * _sigmoid(-lam_row))

        def chunk4(i, carry):
            r0 = pl.multiple_of(i * CHUNK, CHUNK)
            dvs = dvpad[pl.ds(r0, CHUNK + SUBLANES), :]
            du = cw[CONV_WIDTH - 1] * dvs[:CHUNK]
            for k in range(CONV_WIDTH - 1):
                du += cw[k] * _shift_rows(dvs, -(CONV_WIDTH - 1 - k))[:CHUNK]
            du_ref[pl.ds(r0, CHUNK), :] = du.astype(BF16)
            return carry

        lax.fori_loop(0, nchunk, chunk4, 0)

    col = pl.BlockSpec((T, HEAD), lambda h: (0, h))
    vec = pl.BlockSpec((1, HEAD), lambda h: (0, h))
    mat = pl.BlockSpec((None, HEAD, HEAD), lambda h: (h, 0, 0))
    taps = pl.BlockSpec((CONV_WIDTH, HEAD), lambda h: (0, h))
    vec_out = jax.ShapeDtypeStruct((1, D_RNN), F32)
    mat_out = jax.ShapeDtypeStruct((N_RNN_HEADS, HEAD, HEAD), F32)
    seq = pltpu.VMEM((T, HEAD), F32)
    seq_pad = pltpu.VMEM((T + SUBLANES, HEAD), F32)
    return _pallas(
        body, (proj, proj, hr, dz, conv_w, conv_b, w_a, b_a, w_x, b_x, lam), name="rnn_bwd", grid=(N_RNN_HEADS,),
        in_specs=[pl.BlockSpec((T, HEAD), lambda h: (0, COL_RNN + h)), pl.BlockSpec((T, HEAD), lambda h: (0, COL_GATE + h)),
                  col, col, taps, vec, mat, vec, mat, vec, vec],
        out_specs=[col, col, mat, mat, vec, vec, vec, vec, taps],
        out_shape=[jax.ShapeDtypeStruct((T, D_RNN), BF16), jax.ShapeDtypeStruct((T, D_RNN), BF16), mat_out, mat_out,
                   vec_out, vec_out, vec_out, vec_out, jax.ShapeDtypeStruct((CONV_WIDTH, D_RNN), F32)],
        scratch_shapes=[seq_pad, seq_pad, seq_pad, seq, seq, seq, seq, seq, seq_pad, seq],
        semantics=("parallel",), jobs=jobs)


GROUP_FFN = ["w_ffn_out", "w_ffn_in"]
GROUP_MIX = ["w_o", "w_pool_out", "w_rnn_out"]
GROUP_IN = ["w_in"]


def _step(x, target, s, full, conv_w_mine, place):
    T = x.shape[0]
    tall, mid, low = min(T, 2048), min(T, 1024), min(T, 512)
    c1 = place[1:]
    full = dict(full)

    def gathered(names, results):
        full.update(zip(names, results))

    (full["w_in"], conv_w), = _run_jobs([_gather_job(full, ["w_in"], conv_w_mine)], "gather_w_in")
    early = ["w_pool_out", "w_rnn_out", "w_o", "w_ffn_out"]
    (proj, h1), (res,) = _norm_matmul(x, s["norm_mix"], full["w_in"], tm=tall, tn=512, name="in_proj", jobs=[_gather_job(full, early)])
    gathered(early, res)
    pm = _pool_fwd(proj, s["w_pool_grp"], s["pool_scale"])
    (hr, z), (res,) = _rnn_fwd(proj, conv_w, s["conv_b"], s["w_rg_a"], s["b_rg_a"], s["w_rg_x"], s["b_rg_x"], s["lru_lambda"],
                               jobs=[_gather_job(full, ["w_ffn_in"])])
    gathered(["w_ffn_in"], res)
    *mix_by, mix = _branch_mix(pm, z, full["w_pool_out"], full["w_rnn_out"], proj, tm=mid, tn=512)
    x2 = _out_proj_residual(mix, full["w_o"], x, tm=mid)
    act_by_up, act_by_gate, act, h2 = _ffn_in(x2, s["norm_ffn"], full["w_ffn_in"], tm=tall, tn=256)
    dx3, dx3b, sq_cols, g_norm_final = _ffn_out_loss(act, full["w_ffn_out"], x2, s["norm_final"], target, tm=low)

    g = {"norm_final": g_norm_final}

    def chip_sums(names, from_sibling):
        sums = {name: _chip_sum(name, g[name], got, c1) for name, got in zip(names, from_sibling)}
        return {name: v[0] for name, v in sums.items()}, {name: v[1] for name, v in sums.items()}

    def final_sums(names, sums, from_chips):
        return {name: _final_sum(name, sums[name], got, place) for name, got in zip(names, from_chips)}

    dgate, dup = _ffn_out_bwd(dx3b, full["w_ffn_out"], act_by_gate, act_by_up, tm=mid, tn=1408)
    g["w_ffn_out"] = _weight_grad(act, [dx3b], tm=256, tn=D_MODEL, name="w_ffn_out_grad")
    dx2, dx2b, g["norm_ffn"] = _ffn_in_bwd(dgate, dup, full["w_ffn_in"], dx3, x2, s["norm_ffn"], tm=low)
    g["w_ffn_in"] = _weight_grad(h2, [dgate, dup], tm=D_MODEL, tn=256, name="w_ffn_in_grad")
    (dgp, dgr, dyp, dyr), (res,) = _out_proj_bwd(dx2b, full["w_o"], mix_by, tm=mid, tn=D_MODEL,
                                                 jobs=[_sibling_job(g, GROUP_FFN)])
    sums_ffn, sums_ffn_bf16 = chip_sums(GROUP_FFN, res)
    g["w_o"] = _weight_grad(mix, [dx2b], tm=D_MODEL, tn=256, name="w_o_grad")
    dpm, dz = _branch_bwd(dyp, dyr, full["w_pool_out"], full["w_rnn_out"], tm=mid)
    g["w_pool_out"] = _weight_grad(pm, [dyp], tm=D_POOL, tn=256, name="w_pool_out_grad")
    g["w_rnn_out"] = _weight_grad(z, [dyr], tm=D_RNN, tn=256, name="w_rnn_out_grad")
    (dupool, g["w_pool_grp"], g["pool_scale"]), (res,) = _pool_bwd(proj, dpm, s["w_pool_grp"], s["pool_scale"],
                                                                   jobs=[_sibling_job(g, GROUP_MIX)])
    sums_mix, sums_mix_bf16 = chip_sums(GROUP_MIX, res)
    ((durnn, dugate, g["w_rg_a"], g["w_rg_x"], g["b_rg_a"], g["b_rg_x"], g["lru_lambda"], g["conv_b"], g["conv_w"]),
     (res,)) = _rnn_bwd(proj, hr, dz, conv_w, s["conv_b"], s["w_rg_a"], s["b_rg_a"], s["w_rg_x"], s["b_rg_x"], s["lru_lambda"],
                        jobs=[_chips_job(sums_ffn_bf16, GROUP_FFN)])
    shards = final_sums(GROUP_FFN, sums_ffn, res)
    segs = [dupool, durnn, dugate, dgp, dgr]
    g["w_in"], (res, joined) = _weight_grad(h1, segs, tm=D_MODEL, tn=256, name="w_in_grad",
                                           jobs=[_chips_job(sums_mix_bf16, GROUP_MIX), _join_job(shards, GROUP_FFN)])
    grads = dict(zip(GROUP_FFN, joined))
    shards = final_sums(GROUP_MIX, sums_mix, res)
    (res,) = _run_jobs([_sibling_job(g, GROUP_IN)], "w_in_exchange_sibling")
    sums_in, sums_in_bf16 = chip_sums(GROUP_IN, res)
    (grad_x, g["norm_mix"]), (res,) = _in_proj_bwd(segs, full["w_in"], dx2, x, s["norm_mix"], tm=low,
                                                  jobs=[_chips_job(sums_in_bf16, GROUP_IN)])
    shards.update(final_sums(GROUP_IN, sums_in, res))

    vec_rows = [g[name] if name != "pool_scale" else jnp.pad(g[name], ((0, 0), (0, D_MODEL - D_POOL))) for name in VEC_ITEMS]
    vec_rows += [g["conv_w"], sq_cols, jnp.zeros((VEC_ROWS - len(VEC_ITEMS) - CONV_WIDTH - 1, D_MODEL), F32)]
    vec = jnp.concatenate(vec_rows, axis=0).reshape(VEC_ROWS, N_DEV, HEAD).transpose(1, 0, 2)
    mat = jnp.concatenate([g[name].reshape(-1, HEAD) for name in MAT_ITEMS], axis=0).reshape(N_DEV, -1, HEAD)
    (vec, mat), (joined,) = _all_reduce_small([vec, mat], jobs=[_join_job(shards, GROUP_MIX + GROUP_IN)])
    grads.update(zip(GROUP_MIX + GROUP_IN, joined))

    vec = vec.transpose(1, 0, 2).reshape(VEC_ROWS, D_MODEL)
    mat = mat.reshape(-1, HEAD)
    for k, name in enumerate(VEC_ITEMS):
        grads[name] = vec[k:k + 1, :s[name].shape[1]]
    grads["conv_w"] = vec[len(VEC_ITEMS):len(VEC_ITEMS) + CONV_WIDTH]
    row = 0
    for name in MAT_ITEMS:
        rows = s[name].shape[0] * HEAD
        grads[name] = mat[row:row + rows]
        row += rows
    return vec[len(VEC_ITEMS) + CONV_WIDTH], grad_x, grads


LARGE = {"w_in": "col", "w_pool_out": "col", "w_rnn_out": "row", "w_o": "row", "w_ffn_in": "col", "w_ffn_out": "row"}
LARGE_SHAPE = {"w_in": (D_MODEL, D_IN), "w_pool_out": (D_POOL, D_MODEL), "w_rnn_out": (D_RNN, D_MODEL),
               "w_o": (D_MODEL, D_MODEL), "w_ffn_in": (D_MODEL, 2 * D_FF), "w_ffn_out": (D_FF, D_MODEL)}


def _place():
    x, y, c = lax.axis_index("x"), lax.axis_index("y"), lax.axis_index("c")
    return 2 * x + y, c


def _chip_device(chip, c):
    return (chip // 2, chip % 2, c)


def _chip_window(ref, kind, shape, chip, half=None):
    K, N = shape
    if kind == "col":
        rows = slice(None) if half is None else pl.ds(half * (K // 2), K // 2)
        return ref.at[rows, pl.ds(chip * (N // N_CHIPS), N // N_CHIPS)]
    ks = K // N_CHIPS
    if half is None:
        return ref.at[pl.ds(chip * ks, ks), :]
    return ref.at[pl.ds(chip * ks + half * (ks // 2), ks // 2), :]


def _row_half(ref, half):
    rows = ref.shape[0] // 2
    return ref.at[pl.ds(half * rows, rows), :]


def _remote(win_src, win_dst, send_sems, recv_sems, idx, to):
    return pltpu.make_async_remote_copy(src_ref=win_src, dst_ref=win_dst, send_sem=send_sems.at[idx], recv_sem=recv_sems.at[idx],
                                        device_id=to, device_id_type=MESH)


def _gather_job(full, names, conv_w_full=None):
    n = len(names)
    cw_cols = D_RNN // N_CHIPS

    def windows(refs, chip, half):
        return [_chip_window(refs[k], LARGE[name], LARGE_SHAPE[name], chip, half) for k, name in enumerate(names)]

    def ici_copies(refs, send_sems, recv_sems, src_chip, dst_chip, c, r):
        wins = windows(refs, src_chip, c)
        if conv_w_full is not None:
            wins.append(refs[n].at[:, pl.ds(src_chip * cw_cols, cw_cols)])
        return [_remote(win, win, send_sems, recv_sems, (k, r), _chip_device(dst_chip, c)) for k, win in enumerate(wins)]

    def forwards(refs, send_sems, recv_sems, src_chip, half, to_core, chip, r):
        return [_remote(win, win, send_sems, recv_sems, (k, 3 + r), _chip_device(chip, to_core))
                for k, win in enumerate(windows(refs, src_chip, half))]

    def start(ins, outs, send_sems, recv_sems):
        chip, c = _place()
        for r in range(3):
            for cp in ici_copies(outs, send_sems, recv_sems, chip, chip ^ (r + 1), c, r):
                cp.start()

    def finish(ins, outs, send_sems, recv_sems):
        chip, c = _place()
        for r in range(3):
            for cp in ici_copies(outs, send_sems, recv_sems, chip ^ (r + 1), chip, c, r):
                cp.wait_recv()
            for cp in forwards(outs, send_sems, recv_sems, chip ^ (r + 1), c, 1 - c, chip, r):
                cp.start()
        for r in range(3):
            for cp in forwards(outs, send_sems, recv_sems, chip ^ (r + 1), 1 - c, c, chip, r):
                cp.wait_recv()
            for cp in ici_copies(outs, send_sems, recv_sems, chip, chip ^ (r + 1), c, r):
                cp.wait_send()
            for cp in forwards(outs, send_sems, recv_sems, chip ^ (r + 1), c, 1 - c, chip, r):
                cp.wait_send()

    arrays = [full[name] for name in names] + ([conv_w_full] if conv_w_full is not None else [])
    return _Job(arrays, [jax.ShapeDtypeStruct(a.shape, a.dtype) for a in arrays], {k: k for k in range(len(arrays))},
                (len(arrays), 6), start, finish)


def _core_halves(ref, kind, shape, c):
    return [_chip_window(ref, kind, shape, chip, c) for chip in range(N_CHIPS)]


def _sibling_job(grads, names):
    def start(ins, outs, send_sems, recv_sems):
        chip, c = _place()
        for k, name in enumerate(names):
            kind, shape = LARGE[name], LARGE_SHAPE[name]
            if kind == "col":
                pairs = [(_row_half(ins[k], 1 - c), outs[k])]
            else:
                rows = shape[0] // N_DEV
                pairs = [(win, outs[k].at[pl.ds(j * rows, rows), :]) for j, win in enumerate(_core_halves(ins[k], kind, shape, 1 - c))]
            for src, dst in pairs:
                _remote(src, dst, send_sems, recv_sems, k, _chip_device(chip, 1 - c)).start()

    def finish(ins, outs, send_sems, recv_sems):
        chip, c = _place()
        for k in range(len(names)):
            _remote(outs[k], outs[k], send_sems, recv_sems, k, _chip_device(chip, 1 - c)).wait()

    return _Job([grads[name] for name in names],
                [jax.ShapeDtypeStruct((LARGE_SHAPE[name][0] // 2, LARGE_SHAPE[name][1]), F32) for name in names], {},
                (len(names),), start, finish)


def _chip_sum(name, g, got, c):
    kind, (K, N) = LARGE[name], LARGE_SHAPE[name]
    rows = K // N_DEV

    def body(c_ref, g_ref, got_ref, o_ref, ob_ref):
        total = g_ref[...] + got_ref[...]
        o_ref[...] = total
        ob_ref[...] = total.astype(BF16)

    if kind == "col":
        mine = pl.BlockSpec((rows, N), lambda j, c_ref: (j + N_CHIPS * c_ref[0], 0))
    else:
        mine = pl.BlockSpec((rows, N), lambda j, c_ref: (2 * j + c_ref[0], 0))
    blk = pl.BlockSpec((rows, N), lambda j, c_ref: (j, 0))
    return pl.pallas_call(
        body, name=name + "_chip_sum",
        grid_spec=pltpu.PrefetchScalarGridSpec(num_scalar_prefetch=1, grid=(N_CHIPS,), in_specs=[mine, blk], out_specs=[blk, blk]),
        out_shape=[jax.ShapeDtypeStruct((K // 2, N), F32), jax.ShapeDtypeStruct((K // 2, N), BF16)],
        compiler_params=_params(dimension_semantics=("parallel",)),
    )(c, g, got)


def _piece(ref, kind, shape, chip):
    K, N = shape
    if kind == "col":
        return ref.at[:, pl.ds(chip * (N // N_CHIPS), N // N_CHIPS)]
    return ref.at[pl.ds(chip * (K // N_DEV), K // N_DEV), :]


def _piece_shape(name):
    kind, (K, N) = LARGE[name], LARGE_SHAPE[name]
    return (K // 2, N // N_CHIPS) if kind == "col" else (K // N_DEV, N)


def _chips_job(sums, names):
    def copies(ins, outs, send_sems, recv_sems):
        chip, c = _place()
        return [_remote(_piece(ins[k], LARGE[name], LARGE_SHAPE[name], chip ^ (r + 1)), outs[k].at[r], send_sems, recv_sems, (k, r),
                        _chip_device(chip ^ (r + 1), c)) for k, name in enumerate(names) for r in range(3)]

    def start(*refs):
        for cp in copies(*refs):
            cp.start()

    def finish(*refs):
        for cp in copies(*refs):
            cp.wait()

    return _Job([sums[name] for name in names], [jax.ShapeDtypeStruct((3,) + _piece_shape(name), BF16) for name in names], {},
                (len(names), 3), start, finish)


def _final_sum(name, chip_sum, got, place):
    kind = LARGE[name]
    rows, cols = _piece_shape(name)

    def body(place_ref, s_ref, got_ref, o_ref):
        o_ref[...] = ((s_ref[...] + got_ref[0].astype(F32)) + got_ref[1].astype(F32)) + got_ref[2].astype(F32)

    if kind == "col":
        mine = pl.BlockSpec((rows, cols), lambda i, place_ref: (0, place_ref[0]))
    else:
        mine = pl.BlockSpec((rows, cols), lambda i, place_ref: (place_ref[0], 0))
    return pl.pallas_call(
        body, name=name + "_final_sum",
        grid_spec=pltpu.PrefetchScalarGridSpec(
            num_scalar_prefetch=1, grid=(1,), in_specs=[mine, pl.BlockSpec((3, rows, cols), lambda i, place_ref: (0, 0, 0))],
            out_specs=pl.BlockSpec((rows, cols), lambda i, place_ref: (place_ref[1], 0))),
        out_shape=jax.ShapeDtypeStruct((2 * rows, cols), F32),
        compiler_params=_params(dimension_semantics=("arbitrary",)),
    )(place, chip_sum, got)


def _join_job(shards, names):
    def half_copy(outs, send_sems, recv_sems, k, mine):
        chip, c = _place()
        win = _row_half(outs[k], c if mine else 1 - c)
        return _remote(win, win, send_sems, recv_sems, k, _chip_device(chip, 1 - c))

    def start(ins, outs, send_sems, recv_sems):
        for k in range(len(names)):
            half_copy(outs, send_sems, recv_sems, k, True).start()

    def finish(ins, outs, send_sems, recv_sems):
        for k in range(len(names)):
            half_copy(outs, send_sems, recv_sems, k, True).wait_send()
            half_copy(outs, send_sems, recv_sems, k, False).wait_recv()

    arrays = [shards[name] for name in names]
    return _Job(arrays, [jax.ShapeDtypeStruct(a.shape, F32) for a in arrays], {k: k for k in range(len(arrays))},
                (len(arrays),), start, finish)


VEC_ROWS = 16


def _all_reduce_small(slabs, jobs=()):
    n = len(slabs)

    def body(*refs):
        in_refs, out_refs, got_refs = refs[:n], refs[n:2 * n], refs[2 * n:3 * n]
        send_sems, recv_sems = refs[3 * n:]
        x, y, c = lax.axis_index("x"), lax.axis_index("y"), lax.axis_index("c")
        me = 4 * x + 2 * y + c

        def remote(src, dst, k, phase, r):
            other = me ^ r
            return pltpu.make_async_remote_copy(src_ref=src, dst_ref=dst, send_sem=send_sems.at[k, phase, r],
                                                recv_sem=recv_sems.at[k, phase, r],
                                                device_id=(other // 4, (other // 2) % 2, other % 2), device_id_type=MESH)

        scatter = [remote(in_refs[k].at[me ^ r], got_refs[k].at[r], k, 0, r) for r in range(1, N_DEV) for k in range(n)]
        for cp in scatter:
            cp.start()
        for cp in scatter:
            cp.wait()
        for k in range(n):
            total = in_refs[k][me]
            for r in range(1, N_DEV):
                total = total + got_refs[k][r]
            out_refs[k][me] = total
        gather = [remote(out_refs[k].at[me], out_refs[k].at[me], k, 1, r) for r in range(1, N_DEV) for k in range(n)]
        for cp in gather:
            cp.start()
        for r in range(1, N_DEV):
            for k in range(n):
                remote(out_refs[k].at[me ^ r], out_refs[k].at[me ^ r], k, 1, r).wait_recv()
        for cp in gather:
            cp.wait_send()

    return _pallas(
        body, slabs, name="all_reduce_small", grid=(), in_specs=[VMEM] * n, out_specs=[VMEM] * n,
        out_shape=[jax.ShapeDtypeStruct(s.shape, F32) for s in slabs],
        scratch_shapes=[pltpu.VMEM(s.shape, F32) for s in slabs]
        + [pltpu.SemaphoreType.DMA((n, 2, N_DEV)), pltpu.SemaphoreType.DMA((n, 2, N_DEV))], jobs=jobs)


def _cast_into_whole(w, name, place):
    rows, cols = w.shape
    tr = rows // 2

    def body(place_ref, w_ref, o_ref):
        o_ref[...] = w_ref[...].astype(BF16)

    if LARGE[name] == "col":
        window = pl.BlockSpec((tr, cols), lambda i, place_ref: (i, place_ref[0]))
    else:
        window = pl.BlockSpec((tr, cols), lambda i, place_ref: (2 * place_ref[0] + i, 0))
    return pl.pallas_call(
        body, name=name + "_cast",
        grid_spec=pltpu.PrefetchScalarGridSpec(num_scalar_prefetch=1, grid=(2,),
                                               in_specs=[pl.BlockSpec((tr, cols), lambda i, place_ref: (i, 0))], out_specs=window),
        out_shape=jax.ShapeDtypeStruct(LARGE_SHAPE[name], BF16),
        compiler_params=_params(dimension_semantics=("parallel",)))(place, w)


def _adamw_math(w, g, m, v):
    m = ADAM_B1 * m + (1.0 - ADAM_B1) * g
    v = ADAM_B2 * v + (1.0 - ADAM_B2) * (g * g)
    m_hat = m / (1.0 - ADAM_B1 ** ADAM_STEP)
    v_hat = v / (1.0 - ADAM_B2 ** ADAM_STEP)
    delta = -ADAM_LR * (m_hat / (jnp.sqrt(v_hat) + ADAM_EPS) + ADAM_WD * w)
    return delta, m, v


def _adamw_large(w, g, m, v, name):
    rows, cols = w.shape
    tr = rows // 4

    def body(w_ref, g_ref, m_ref, v_ref, d_ref, mo_ref, vo_ref):
        d_ref[...], mo_ref[...], vo_ref[...] = _adamw_math(w_ref[...], g_ref[...], m_ref[...], v_ref[...])

    blk = pl.BlockSpec((tr, cols), lambda i: (i, 0))
    out = jax.ShapeDtypeStruct(w.shape, F32)
    return pl.pallas_call(body, name=name + "_adamw", grid=(4,), in_specs=[blk] * 4, out_specs=[blk] * 3, out_shape=[out] * 3,
                          compiler_params=_params(dimension_semantics=("parallel",)))(w, g, m, v)


def _adamw_small(ws, gs, ms, vs):
    n = len(ws)

    def body(*refs):
        for k in range(n):
            w_ref, g_ref, m_ref, v_ref = (refs[q * n + k] for q in range(4))
            d_ref, mo_ref, vo_ref = (refs[(4 + q) * n + k] for q in range(3))
            d_ref[...], mo_ref[...], vo_ref[...] = _adamw_math(w_ref[...], g_ref[...], m_ref[...], v_ref[...])

    out = [jax.ShapeDtypeStruct(w.shape, F32) for w in ws]
    res = pl.pallas_call(body, name="small_adamw", in_specs=[VMEM] * (4 * n), out_specs=[VMEM] * (3 * n), out_shape=out * 3,
                         compiler_params=_params())(*ws, *gs, *ms, *vs)
    return res[:n], res[n:2 * n], res[2 * n:]


WEIGHTS = ["norm_mix", "w_in", "w_pool_grp", "pool_scale", "w_pool_out", "conv_w", "conv_b", "w_rg_a", "b_rg_a", "w_rg_x",
           "b_rg_x", "lru_lambda", "w_rnn_out", "w_o", "norm_ffn", "w_ffn_in", "w_ffn_out", "norm_final"]
VEC_ITEMS = ["norm_mix", "norm_ffn", "norm_final", "pool_scale", "conv_b", "lru_lambda", "b_rg_a", "b_rg_x"]
MAT_ITEMS = ["w_pool_grp", "w_rg_a", "w_rg_x"]


def _as2d(name, a):
    if name in MAT_ITEMS:
        return a.reshape(-1, HEAD, HEAD)
    if name == "conv_w":
        return a.reshape(CONV_WIDTH, -1)
    return a.reshape(1, -1)


def kernel(x, norm_mix, w_in, w_pool_grp, pool_scale, w_pool_out, conv_w, conv_b, w_rg_a, b_rg_a, w_rg_x, b_rg_x, lru_lambda, w_rnn_out, w_o, norm_ffn, w_ffn_in, w_ffn_out, norm_final, loss_target, m_norm_mix, m_w_in, m_w_pool_grp, m_pool_scale, m_w_pool_out, m_conv_w, m_conv_b, m_w_rg_a, m_b_rg_a, m_w_rg_x, m_b_rg_x, m_lru_lambda, m_w_rnn_out, m_w_o, m_norm_ffn, m_w_ffn_in, m_w_ffn_out, m_norm_final, v_norm_mix, v_w_in, v_w_pool_grp, v_pool_scale, v_w_pool_out, v_conv_w, v_conv_b, v_w_rg_a, v_b_rg_a, v_w_rg_x, v_b_rg_x, v_lru_lambda, v_w_rnn_out, v_w_o, v_norm_ffn, v_w_ffn_in, v_w_ffn_out, v_norm_final):
    given = dict(locals())
    w = {name: given[name] for name in WEIGHTS}
    m = {name: given["m_" + name] for name in WEIGHTS}
    v = {name: given["v_" + name] for name in WEIGHTS}
    chip, c = _place()

    place = jnp.stack([chip, c]).astype(jnp.int32)
    conv_cols = w["conv_w"].shape[-1]
    conv_w_mine = lax.dynamic_update_slice_in_dim(jnp.zeros((CONV_WIDTH, D_RNN), F32), w["conv_w"][0], chip * conv_cols, axis=1)
    full = {name: _cast_into_whole(w[name][0], name, place) for name in LARGE}
    small = {name: _as2d(name, w[name]) for name in WEIGHTS if name not in LARGE and name != "conv_w"}
    sq_cols, grad_x, grads = _step(x[0], loss_target[0], small, full, conv_w_mine, place)
    loss = 0.5 / D_MODEL * jnp.sum(sq_cols)
    grads["conv_w"] = lax.dynamic_slice_in_dim(grads["conv_w"], chip * conv_cols, conv_cols, axis=1)

    delta, new_m, new_v = {}, {}, {}
    for name in LARGE:
        delta[name], new_m[name], new_v[name] = _adamw_large(w[name][0], grads[name], m[name][0], v[name][0], name)
    small_names = [name for name in WEIGHTS if name not in LARGE]
    flat = lambda d: [d[name].reshape(grads[name].shape) for name in small_names]
    ds, mo, vo = _adamw_small(flat(w), [grads[name] for name in small_names], flat(m), flat(v))
    for k, name in enumerate(small_names):
        delta[name], new_m[name], new_v[name] = ds[k], mo[k], vo[k]

    shaped = lambda d: [d[name].reshape(w[name].shape) for name in WEIGHTS]
    return (loss, grad_x[None], *shaped(grads), *shaped(delta), *shaped(new_m), *shaped(new_v))
```

```python
import functools
import math

import jax
import jax.numpy as jnp
from jax import lax
from jax.experimental import pallas as pl
from jax.experimental.pallas import tpu as pltpu

F32 = jnp.float32
BF16 = jnp.bfloat16

D_MODEL = 1024
D_POOL = 512
N_POOL_GROUPS = 4
D_RNN = 1024
N_RNN_HEADS = 8
HEAD = 128
CONV_WIDTH = 4
LRU_C = 8.0
D_FF = 2816
D_IN = D_POOL + 2 * D_RNN + 2 * D_MODEL
NORM_EPS = 1e-6
COL_RNN = D_POOL // HEAD
COL_GATE = (D_POOL + D_RNN) // HEAD

ADAM_LR = 0.001
ADAM_B1 = 0.9
ADAM_B2 = 0.999
ADAM_EPS = 1e-08
ADAM_WD = 0.01
ADAM_STEP = 10

N_CHIPS = 4
N_DEV = 8
MESH = pl.DeviceIdType.MESH
ANY = pl.BlockSpec(memory_space=pl.ANY)
VMEM = pl.BlockSpec(memory_space=pltpu.VMEM)
VMEM_LIMIT_BYTES = 60 * 1024 * 1024
SUBLANES = 8
POOL_HALO = 16
CHUNK = 1024

GELU_C = math.sqrt(2.0 / math.pi)
GELU_A = 0.044715


def _params(**kw):
    return pltpu.CompilerParams(vmem_limit_bytes=VMEM_LIMIT_BYTES, **kw)


def _sigmoid(x):
    return 0.5 * jnp.tanh(0.5 * x) + 0.5


def _log1p(y):
    u = 1.0 + y
    d = u - 1.0
    return jnp.where(d == 0.0, y, jnp.log(u) * (y / jnp.where(d == 0.0, 1.0, d)))


def _gelu_parts(x):
    x2 = x * x
    th = jnp.tanh(GELU_C * (x + GELU_A * x * x2))
    g = 0.5 * x * (1.0 + th)
    dg = 0.5 * (1.0 + th) + 0.5 * x * (1.0 - th * th) * GELU_C * (1.0 + 3.0 * GELU_A * x2)
    return g, dg


def _dot(a, b):
    return jnp.dot(a, b, preferred_element_type=F32)


def _dot_nt(a, b):
    return lax.dot_general(a, b, (((1,), (1,)), ((), ())), preferred_element_type=F32)


def _dot_tn(a, b):
    return lax.dot_general(a, b, (((0,), (0,)), ((), ())), preferred_element_type=F32)


def _rms_scale(xv):
    return lax.rsqrt(jnp.mean(xv * xv, axis=-1, keepdims=True) + NORM_EPS)


def _rms_bwd(dy, xv, g):
    r = _rms_scale(xv)
    xh = xv * r
    dyg = dy * g
    dx = r * (dyg - xh * jnp.mean(dyg * xh, axis=-1, keepdims=True))
    return dx, dy * xh


class _Job:
    def __init__(self, inputs, out_shapes, aliases, sem_shape, start, finish):
        self.inputs, self.out_shapes, self.aliases, self.sem_shape = list(inputs), list(out_shapes), dict(aliases), sem_shape
        self.start, self.finish = start, finish


def _pallas(body, operands, *, name, grid, in_specs, out_specs, out_shape, scratch_shapes=(), semantics=None, jobs=()):
    n_in, n_out, n_scr = len(in_specs), len(out_specs), len(scratch_shapes)
    job_in = [a for job in jobs for a in job.inputs]
    job_out = [s for job in jobs for s in job.out_shapes]
    aliases, i0, o0 = {}, n_in, n_out
    for job in jobs:
        aliases.update({i0 + i: o0 + o for i, o in job.aliases.items()})
        i0, o0 = i0 + len(job.inputs), o0 + len(job.out_shapes)

    def whole(*refs):
        ins, j_ins = refs[:n_in], refs[n_in:n_in + len(job_in)]
        outs = refs[n_in + len(job_in):][:n_out]
        j_outs = refs[n_in + len(job_in) + n_out:][:len(job_out)]
        rest = refs[n_in + len(job_in) + n_out + len(job_out):]
        scr, sems = rest[:n_scr], rest[n_scr:]

        def run(phase):
            i, o = 0, 0
            for k, job in enumerate(jobs):
                getattr(job, phase)(j_ins[i:i + len(job.inputs)], j_outs[o:o + len(job.out_shapes)], sems[2 * k], sems[2 * k + 1])
                i, o = i + len(job.inputs), o + len(job.out_shapes)

        def at(step_of, phase):
            if not jobs:
                return
            if not grid:
                run(phase)
                return
            cond = functools.reduce(jnp.logical_and, [pl.program_id(d) == step_of(d) for d in range(len(grid))])
            pl.when(cond)(functools.partial(run, phase))

        at(lambda d: 0, "start")
        body(*ins, *outs, *scr)
        at(lambda d: grid[d] - 1, "finish")

    res = pl.pallas_call(
        whole, name=name, grid=grid, in_specs=list(in_specs) + [ANY] * len(job_in), out_specs=list(out_specs) + [ANY] * len(job_out),
        out_shape=list(out_shape) + job_out, input_output_aliases=aliases,
        scratch_shapes=list(scratch_shapes) + [pltpu.SemaphoreType.DMA(job.sem_shape) for job in jobs for _ in range(2)],
        compiler_params=_params(dimension_semantics=semantics, has_side_effects=bool(jobs)),
    )(*operands, *job_in)
    per_job, o = [], n_out
    for job in jobs:
        per_job.append(res[o:o + len(job.out_shapes)])
        o += len(job.out_shapes)
    return res[:n_out], per_job


def _run_jobs(jobs, name):
    return _pallas(lambda: None, [], name=name, grid=(), in_specs=[], out_specs=[], out_shape=[], jobs=jobs)[1]


NORM_ROWS = 256


def _norm_rows(x_ref, g_ref, h_ref):
    g = g_ref[...]

    def rows(i, carry):
        r = pl.ds(pl.multiple_of(i * NORM_ROWS, NORM_ROWS), NORM_ROWS)
        xv = x_ref[r, :]
        h_ref[r, :] = (xv * _rms_scale(xv) * g).astype(BF16)
        return carry

    lax.fori_loop(0, x_ref.shape[0] // NORM_ROWS, rows, 0)


def _norm_matmul(x, g, w, *, tm, tn, name, jobs=()):
    T, K = x.shape
    N = w.shape[1]

    def body(x_ref, g_ref, w_ref, o_ref, h_ref):
        @pl.when(pl.program_id(1) == 0)
        def _():
            _norm_rows(x_ref, g_ref, h_ref)

        o_ref[...] = _dot(h_ref[...], w_ref[...])

    return _pallas(
        body, (x, g, w), name=name, grid=(T // tm, N // tn),
        in_specs=[pl.BlockSpec((tm, K), lambda i, j: (i, 0)), pl.BlockSpec((1, K), lambda i, j: (0, 0)),
                  pl.BlockSpec((K, tn), lambda i, j: (0, j))],
        out_specs=[pl.BlockSpec((tm, tn), lambda i, j: (i, j)), pl.BlockSpec((tm, K), lambda i, j: (i, 0))],
        out_shape=[jax.ShapeDtypeStruct((T, N), F32), jax.ShapeDtypeStruct((T, K), BF16)],
        semantics=("parallel", "arbitrary"), jobs=jobs)


def _ffn_in(x2, g, w, *, tm, tn):
    T, K = x2.shape
    nb = D_FF // tn

    def body(x_ref, g_ref, wg_ref, wu_ref, dup_ref, dgate_ref, act_ref, h_ref):
        @pl.when(pl.program_id(1) == 0)
        def _():
            _norm_rows(x_ref, g_ref, h_ref)

        h = h_ref[...]
        gate = _dot(h, wg_ref[...])
        up = _dot(h, wu_ref[...])
        s = _sigmoid(gate)
        silu = gate * s
        dup_ref[...] = silu.astype(BF16)
        dgate_ref[...] = (up * (s + silu * (1.0 - s))).astype(BF16)
        act_ref[...] = (silu * up).astype(BF16)

    blk = pl.BlockSpec((tm, tn), lambda i, j: (i, j))
    return pl.pallas_call(
        body, name="ffn_in", grid=(T // tm, nb),
        in_specs=[pl.BlockSpec((tm, K), lambda i, j: (i, 0)), pl.BlockSpec((1, K), lambda i, j: (0, 0)),
                  pl.BlockSpec((K, tn), lambda i, j: (0, j)), pl.BlockSpec((K, tn), lambda i, j: (0, j + nb))],
        out_specs=[blk, blk, blk, pl.BlockSpec((tm, K), lambda i, j: (i, 0))],
        out_shape=[jax.ShapeDtypeStruct((T, D_FF), BF16), jax.ShapeDtypeStruct((T, D_FF), BF16),
                   jax.ShapeDtypeStruct((T, D_FF), BF16), jax.ShapeDtypeStruct((T, K), BF16)],
        compiler_params=_params(dimension_semantics=("parallel", "arbitrary")),
    )(x2, g, w, w)


def _branch_mix(pm, z, w_pool_out, w_rnn_out, proj, *, tm, tn):
    T = pm.shape[0]
    col_gp = (D_POOL + 2 * D_RNN) // tn
    col_gr = col_gp + D_MODEL // tn

    def body(pm_ref, z_ref, wp_ref, wr_ref, gp_ref, gr_ref, by_gp_ref, by_gr_ref, sp_ref, sr_ref, mix_ref):
        yp = _dot(pm_ref[...], wp_ref[...])
        yr = _dot(z_ref[...], wr_ref[...])
        sp, sr = _sigmoid(gp_ref[...]), _sigmoid(gr_ref[...])
        by_gp_ref[...] = (yp * sp * (1.0 - sp)).astype(BF16)
        by_gr_ref[...] = (yr * sr * (1.0 - sr)).astype(BF16)
        sp_ref[...] = sp.astype(BF16)
        sr_ref[...] = sr.astype(BF16)
        mix_ref[...] = (sp * yp + sr * yr).astype(BF16)

    blk = pl.BlockSpec((tm, tn), lambda i, j: (i, j))
    out = jax.ShapeDtypeStruct((T, D_MODEL), BF16)
    return pl.pallas_call(
        body, name="branch_mix", grid=(T // tm, D_MODEL // tn),
        in_specs=[pl.BlockSpec((tm, D_POOL), lambda i, j: (i, 0)), pl.BlockSpec((tm, D_RNN), lambda i, j: (i, 0)),
                  pl.BlockSpec((D_POOL, tn), lambda i, j: (0, j)), pl.BlockSpec((D_RNN, tn), lambda i, j: (0, j)),
                  pl.BlockSpec((tm, tn), lambda i, j: (i, col_gp + j)), pl.BlockSpec((tm, tn), lambda i, j: (i, col_gr + j))],
        out_specs=[blk] * 5, out_shape=[out] * 5,
        compiler_params=_params(dimension_semantics=("parallel", "parallel")),
    )(pm, z, w_pool_out, w_rnn_out, proj, proj)


def _out_proj_residual(mix, w_o, x, *, tm):
    T = x.shape[0]

    def body(mix_ref, w_ref, x_ref, o_ref):
        o_ref[...] = x_ref[...] + _dot(mix_ref[...], w_ref[...])

    row = pl.BlockSpec((tm, D_MODEL), lambda i: (i, 0))
    return pl.pallas_call(
        body, name="out_proj_residual", grid=(T // tm,),
        in_specs=[row, pl.BlockSpec((D_MODEL, D_MODEL), lambda i: (0, 0)), row],
        out_specs=row, out_shape=jax.ShapeDtypeStruct((T, D_MODEL), F32),
        compiler_params=_params(dimension_semantics=("parallel",)),
    )(mix, w_o, x)


def _ffn_out_loss(act, w, x2, g3, target, *, tm):
    T = x2.shape[0]

    def body(act_ref, w_ref, x2_ref, g_ref, t_ref, dx_ref, dxb_ref, sq_ref, dg_ref):
        @pl.when(pl.program_id(0) == 0)
        def _():
            sq_ref[...] = jnp.zeros_like(sq_ref)
            dg_ref[...] = jnp.zeros_like(dg_ref)

        x3 = x2_ref[...] + _dot(act_ref[...], w_ref[...])
        g = g_ref[...]
        err = x3 * _rms_scale(x3) * g - t_ref[...]
        sq_ref[...] += jnp.sum(err * err, axis=0, keepdims=True)
        dx, dgp = _rms_bwd(err * (1.0 / D_MODEL), x3, g)
        dg_ref[...] += jnp.sum(dgp, axis=0, keepdims=True)
        dx_ref[...] = dx
        dxb_ref[...] = dx.astype(BF16)

    row = pl.BlockSpec((tm, D_MODEL), lambda i: (i, 0))
    vec = pl.BlockSpec((1, D_MODEL), lambda i: (0, 0))
    return pl.pallas_call(
        body, name="ffn_out_loss", grid=(T // tm,),
        in_specs=[pl.BlockSpec((tm, D_FF), lambda i: (i, 0)), pl.BlockSpec((D_FF, D_MODEL), lambda i: (0, 0)), row, vec, row],
        out_specs=[row, row, vec, vec],
        out_shape=[jax.ShapeDtypeStruct((T, D_MODEL), F32), jax.ShapeDtypeStruct((T, D_MODEL), BF16),
                   jax.ShapeDtypeStruct((1, D_MODEL), F32), jax.ShapeDtypeStruct((1, D_MODEL), F32)],
        compiler_params=_params(dimension_semantics=("arbitrary",)),
    )(act, w, x2, g3, target)


def _ffn_out_bwd(dx3b, w, act_by_gate, act_by_up, *, tm, tn):
    T = dx3b.shape[0]

    def body(dx_ref, w_ref, by_gate_ref, by_up_ref, dgate_ref, dup_ref):
        dact = _dot_nt(dx_ref[...], w_ref[...])
        dgate_ref[...] = (dact * by_gate_ref[...].astype(F32)).astype(BF16)
        dup_ref[...] = (dact * by_up_ref[...].astype(F32)).astype(BF16)

    blk = pl.BlockSpec((tm, tn), lambda i, j: (i, j))
    return pl.pallas_call(
        body, name="ffn_out_bwd", grid=(T // tm, D_FF // tn),
        in_specs=[pl.BlockSpec((tm, D_MODEL), lambda i, j: (i, 0)), pl.BlockSpec((tn, D_MODEL), lambda i, j: (j, 0)), blk, blk],
        out_specs=[blk, blk],
        out_shape=[jax.ShapeDtypeStruct((T, D_FF), BF16), jax.ShapeDtypeStruct((T, D_FF), BF16)],
        compiler_params=_params(dimension_semantics=("parallel", "parallel")),
    )(dx3b, w, act_by_gate, act_by_up)


def _ffn_in_bwd(dgate, dup, w, dx3, x2, g2, *, tm):
    T = x2.shape[0]

    def body(dgate_ref, dup_ref, w_ref, dx3_ref, x2_ref, g_ref, dx_ref, dxb_ref, dg_ref):
        @pl.when(pl.program_id(0) == 0)
        def _():
            dg_ref[...] = jnp.zeros_like(dg_ref)

        dh = _dot_nt(dgate_ref[...], w_ref[:, :D_FF]) + _dot_nt(dup_ref[...], w_ref[:, D_FF:])
        dxn, dgp = _rms_bwd(dh, x2_ref[...], g_ref[...])
        dx = dx3_ref[...] + dxn
        dg_ref[...] += jnp.sum(dgp, axis=0, keepdims=True)
        dx_ref[...] = dx
        dxb_ref[...] = dx.astype(BF16)

    row = pl.BlockSpec((tm, D_MODEL), lambda i: (i, 0))
    wide = pl.BlockSpec((tm, D_FF), lambda i: (i, 0))
    vec = pl.BlockSpec((1, D_MODEL), lambda i: (0, 0))
    return pl.pallas_call(
        body, name="ffn_in_bwd", grid=(T // tm,),
        in_specs=[wide, wide, pl.BlockSpec((D_MODEL, 2 * D_FF), lambda i: (0, 0)), row, row, vec],
        out_specs=[row, row, vec],
        out_shape=[jax.ShapeDtypeStruct((T, D_MODEL), F32), jax.ShapeDtypeStruct((T, D_MODEL), BF16),
                   jax.ShapeDtypeStruct((1, D_MODEL), F32)],
        compiler_params=_params(dimension_semantics=("arbitrary",)),
    )(dgate, dup, w, dx3, x2, g2)


def _out_proj_bwd(dx2b, w_o, mix_by, *, tm, tn, jobs=()):
    T = dx2b.shape[0]

    def body(dx_ref, w_ref, *refs):
        dmix = _dot_nt(dx_ref[...], w_ref[...])
        for by_ref, d_ref in zip(refs[:4], refs[4:]):
            d_ref[...] = (dmix * by_ref[...].astype(F32)).astype(BF16)

    blk = pl.BlockSpec((tm, tn), lambda i, j: (i, j))
    out = jax.ShapeDtypeStruct((T, D_MODEL), BF16)
    return _pallas(
        body, (dx2b, w_o, *mix_by), name="out_proj_bwd", grid=(T // tm, D_MODEL // tn),
        in_specs=[pl.BlockSpec((tm, D_MODEL), lambda i, j: (i, 0)), pl.BlockSpec((tn, D_MODEL), lambda i, j: (j, 0))] + [blk] * 4,
        out_specs=[blk] * 4, out_shape=[out] * 4, semantics=("parallel", "parallel"), jobs=jobs)


def _branch_bwd(dyp, dyr, w_pool_out, w_rnn_out, *, tm):
    T = dyp.shape[0]

    def body(dyp_ref, dyr_ref, wp_ref, wr_ref, dpm_ref, dz_ref):
        dpm_ref[...] = _dot_nt(dyp_ref[...], wp_ref[...])
        dz_ref[...] = _dot_nt(dyr_ref[...], wr_ref[...])

    row = pl.BlockSpec((tm, D_MODEL), lambda i: (i, 0))
    return pl.pallas_call(
        body, name="branch_bwd", grid=(T // tm,),
        in_specs=[row, row, pl.BlockSpec((D_POOL, D_MODEL), lambda i: (0, 0)), pl.BlockSpec((D_RNN, D_MODEL), lambda i: (0, 0))],
        out_specs=[pl.BlockSpec((tm, D_POOL), lambda i: (i, 0)), pl.BlockSpec((tm, D_RNN), lambda i: (i, 0))],
        out_shape=[jax.ShapeDtypeStruct((T, D_POOL), F32), jax.ShapeDtypeStruct((T, D_RNN), F32)],
        compiler_params=_params(dimension_semantics=("parallel",)),
    )(dyp, dyr, w_pool_out, w_rnn_out)


def _in_proj_bwd(segs, w, dx2, x, g1, *, tm, jobs=()):
    T = x.shape[0]
    widths = [s.shape[1] for s in segs]
    offs = [sum(widths[:k]) for k in range(len(widths))]
    n = len(segs)

    def body(*refs):
        seg_refs, (w_ref, dx2_ref, x_ref, g_ref, dx_ref, dg_ref) = refs[:n], refs[n:]

        @pl.when(pl.program_id(0) == 0)
        def _():
            dg_ref[...] = jnp.zeros_like(dg_ref)

        dh = _dot_nt(seg_refs[0][...], w_ref[:, offs[0]:offs[0] + widths[0]])
        for k in range(1, n):
            dh += _dot_nt(seg_refs[k][...], w_ref[:, offs[k]:offs[k] + widths[k]])
        dxn, dgp = _rms_bwd(dh, x_ref[...], g_ref[...])
        dg_ref[...] += jnp.sum(dgp, axis=0, keepdims=True)
        dx_ref[...] = dx2_ref[...] + dxn

    row = pl.BlockSpec((tm, D_MODEL), lambda i: (i, 0))
    vec = pl.BlockSpec((1, D_MODEL), lambda i: (0, 0))
    return _pallas(
        body, (*segs, w, dx2, x, g1), name="in_proj_bwd", grid=(T // tm,),
        in_specs=[pl.BlockSpec((tm, wd), lambda i: (i, 0)) for wd in widths]
        + [pl.BlockSpec((D_MODEL, D_IN), lambda i: (0, 0)), row, row, vec],
        out_specs=[row, vec],
        out_shape=[jax.ShapeDtypeStruct((T, D_MODEL), F32), jax.ShapeDtypeStruct((1, D_MODEL), F32)],
        semantics=("arbitrary",), jobs=jobs)


def _weight_grad(a, segs, *, tm, tn, name, jobs=None):
    T, M = a.shape
    nblk = [s.shape[1] // tn for s in segs]
    first = [sum(nblk[:k]) for k in range(len(segs))]
    n = len(segs)

    def body(a_ref, *refs):
        seg_refs, o_ref = refs[:n], refs[n]
        j = pl.program_id(1)
        for k in range(n):
            @pl.when((j >= first[k]) & (j < first[k] + nblk[k]))
            def _(k=k):
                o_ref[...] = _dot_tn(a_ref[...], seg_refs[k][...])

    def seg_spec(k):
        return pl.BlockSpec((T, tn), lambda i, j: (0, jnp.clip(j - first[k], 0, nblk[k] - 1)))

    (grad,), results = _pallas(
        body, (a, *segs), name=name, grid=(M // tm, sum(nblk)),
        in_specs=[pl.BlockSpec((T, tm), lambda i, j: (0, i))] + [seg_spec(k) for k in range(n)],
        out_specs=[pl.BlockSpec((tm, tn), lambda i, j: (i, j))],
        out_shape=[jax.ShapeDtypeStruct((M, sum(nblk) * tn), F32)],
        semantics=("parallel", "arbitrary"), jobs=jobs or ())
    return grad if jobs is None else (grad, results)


def _pad_front(dst, src, halo):
    dst[pl.ds(0, halo), :] = jnp.zeros((halo, src.shape[1]), F32)

    def fill(i, carry):
        r0 = pl.multiple_of(i * CHUNK, CHUNK)
        dst[pl.ds(r0 + halo, CHUNK), :] = src[pl.ds(r0, CHUNK), :]
        return carry

    lax.fori_loop(0, src.shape[0] // CHUNK, fill, 0)


def _shift_rows(v, k):
    return pltpu.roll(v, k % v.shape[0], axis=0)


def _window_sums(xs, direction):
    s2 = xs + _shift_rows(xs, direction)
    s4 = s2 + _shift_rows(s2, 2 * direction)
    s8 = s4 + _shift_rows(s4, 4 * direction)
    s16 = s8 + _shift_rows(s8, 8 * direction)
    return s2, s4, s8, s16


def _select_window(g, sums):
    s2, s4, s8, s16 = sums
    return jnp.where(g == 0, s2, jnp.where(g == 1, s4, jnp.where(g == 2, s8, s16)))


def _pool_count(g, start, rows):
    t = start + lax.broadcasted_iota(jnp.int32, (rows, 1), 0)
    return jnp.minimum(t + 1, jnp.left_shift(2, g)).astype(F32)


def _pool_fwd(proj, w_grp, scale):
    T = proj.shape[0]
    nchunk = T // CHUNK

    def body(u_ref, w_ref, s_ref, o_ref, upad):
        g = pl.program_id(0)
        _pad_front(upad, u_ref, POOL_HALO)
        w = w_ref[...].astype(BF16)
        scale_row = s_ref[...]

        def chunk(i, carry):
            r0 = pl.multiple_of(i * CHUNK, CHUNK)
            xs = upad[pl.ds(r0, CHUNK + POOL_HALO), :]
            win = _select_window(g, _window_sums(xs, 1))[POOL_HALO:]
            pooled = win / _pool_count(g, r0, CHUNK) - xs[POOL_HALO:]
            o_ref[pl.ds(r0, CHUNK), :] = (_dot(pooled.astype(BF16), w) * scale_row).astype(BF16)
            return carry

        lax.fori_loop(0, nchunk, chunk, 0)

    return pl.pallas_call(
        body, name="pool_fwd", grid=(N_POOL_GROUPS,),
        in_specs=[pl.BlockSpec((T, HEAD), lambda g: (0, g)), pl.BlockSpec((None, HEAD, HEAD), lambda g: (g, 0, 0)),
                  pl.BlockSpec((1, HEAD), lambda g: (0, g))],
        out_specs=pl.BlockSpec((T, HEAD), lambda g: (0, g)),
        out_shape=jax.ShapeDtypeStruct((T, D_POOL), BF16),
        scratch_shapes=[pltpu.VMEM((T + POOL_HALO, HEAD), F32)],
        compiler_params=_params(dimension_semantics=("parallel",)),
    )(proj, w_grp, scale)


def _pool_bwd(proj, dpm, w_grp, scale, jobs=()):
    T = proj.shape[0]
    nchunk = T // CHUNK

    def body(u_ref, dpm_ref, w_ref, s_ref, du_ref, dw_ref, ds_ref, upad, zpad, dpool):
        g = pl.program_id(0)
        _pad_front(upad, u_ref, POOL_HALO)
        zpad[pl.ds(T, POOL_HALO), :] = jnp.zeros((POOL_HALO, HEAD), F32)
        dw_ref[...] = jnp.zeros_like(dw_ref)
        ds_ref[...] = jnp.zeros_like(ds_ref)
        w = w_ref[...].astype(BF16)
        scale_row = s_ref[...]

        def chunk(i, carry):
            r0 = pl.multiple_of(i * CHUNK, CHUNK)
            xs = upad[pl.ds(r0, CHUNK + POOL_HALO), :]
            cnt = _pool_count(g, r0, CHUNK)
            pooled = (_select_window(g, _window_sums(xs, 1))[POOL_HALO:] / cnt - xs[POOL_HALO:]).astype(BF16)
            mixed = _dot(pooled, w)
            d = dpm_ref[pl.ds(r0, CHUNK), :]
            ds_ref[...] += jnp.sum(d * mixed, axis=0, keepdims=True)
            dmixed = (d * scale_row).astype(BF16)
            dw_ref[...] += _dot_tn(pooled, dmixed)
            dp = _dot_nt(dmixed, w)
            dpool[pl.ds(r0, CHUNK), :] = dp
            zpad[pl.ds(r0, CHUNK), :] = dp / cnt
            return carry

        lax.fori_loop(0, nchunk, chunk, 0)

        def chunk2(i, carry):
            r0 = pl.multiple_of(i * CHUNK, CHUNK)
            zs = zpad[pl.ds(r0, CHUNK + POOL_HALO), :]
            win = _select_window(g, _window_sums(zs, -1))[:CHUNK]
            du_ref[pl.ds(r0, CHUNK), :] = (win - dpool[pl.ds(r0, CHUNK), :]).astype(BF16)
            return carry

        lax.fori_loop(0, nchunk, chunk2, 0)

    col = pl.BlockSpec((T, HEAD), lambda g: (0, g))
    return _pallas(
        body, (proj, dpm, w_grp, scale), name="pool_bwd", grid=(N_POOL_GROUPS,),
        in_specs=[col, col, pl.BlockSpec((None, HEAD, HEAD), lambda g: (g, 0, 0)), pl.BlockSpec((1, HEAD), lambda g: (0, g))],
        out_specs=[col, pl.BlockSpec((None, HEAD, HEAD), lambda g: (g, 0, 0)), pl.BlockSpec((1, HEAD), lambda g: (0, g))],
        out_shape=[jax.ShapeDtypeStruct((T, D_POOL), BF16), jax.ShapeDtypeStruct((N_POOL_GROUPS, HEAD, HEAD), F32),
                   jax.ShapeDtypeStruct((1, D_POOL), F32)],
        scratch_shapes=[pltpu.VMEM((T + POOL_HALO, HEAD), F32), pltpu.VMEM((T + POOL_HALO, HEAD), F32), pltpu.VMEM((T, HEAD), F32)],
        semantics=("parallel",), jobs=jobs)


def _conv_taps(xs, cw):
    v = cw[CONV_WIDTH - 1] * xs[SUBLANES:]
    for k in range(CONV_WIDTH - 1):
        v += cw[k] * _shift_rows(xs, CONV_WIDTH - 1 - k)[SUBLANES:]
    return v


def _tap_rows(cw_ref):
    return [cw_ref[k:k + 1, :] for k in range(CONV_WIDTH)]


def _softplus_neg(lam):
    return jnp.maximum(-lam, 0.0) + _log1p(jnp.exp(-jnp.abs(lam)))


def _lru_gates(v, wa, ba, wx, bx, sp):
    vb = v.astype(BF16)
    ra = _sigmoid(_dot(vb, wa) + ba)
    ix = _sigmoid(_dot(vb, wx) + bx)
    log_a = -LRU_C * ra * sp
    a = jnp.exp(log_a)
    sq = jnp.sqrt(-jnp.tanh(log_a) * (a * a + 1.0))
    return ra, ix, a, sq


def _row_bcast(v, r):
    return jnp.broadcast_to(v[r:r + 1, :], v.shape)


TILE_BLOCK = 128


def _scan_in_tiles(coef, coef_shift, A_out, B, T, direction):
    order = list(range(SUBLANES)) if direction == 1 else list(range(SUBLANES - 1, -1, -1))
    tiles = min(TILE_BLOCK, T // SUBLANES)
    for base in range(0, T, tiles * SUBLANES):
        def rows(r, base=base):
            return pl.ds(base + r, tiles, stride=SUBLANES)

        A, Bv = coef[rows(order[0] + coef_shift), :], B[rows(order[0]), :]
        A_out[rows(order[0]), :] = A
        for r in order[1:]:
            a = coef[rows(r + coef_shift), :]
            Bv = a * Bv + B[rows(r), :]
            A = a * A
            A_out[rows(r), :] = A
            B[rows(r), :] = Bv


TILES_PER_STEP = 8


def _carry_tiles(A_s, B_s, out, ntile, direction):
    out_row = SUBLANES - 1 if direction == 1 else 0

    def step(k, carry):
        for j in range(TILES_PER_STEP):
            t = k * TILES_PER_STEP + j
            r0 = pl.multiple_of((t if direction == 1 else ntile - 1 - t) * SUBLANES, SUBLANES)
            A, B = A_s[pl.ds(r0, SUBLANES), :], B_s[pl.ds(r0, SUBLANES), :]
            out[pl.ds(r0, SUBLANES), :] = A * carry + B
            carry = _row_bcast(A, out_row) * carry + _row_bcast(B, out_row)
        return carry

    lax.fori_loop(0, ntile // TILES_PER_STEP, step, jnp.zeros((SUBLANES, HEAD), F32))


def _rnn_fwd(proj, conv_w, conv_b, w_a, b_a, w_x, b_x, lam, jobs=()):
    T = proj.shape[0]
    nchunk = T // CHUNK
    ntile = T // SUBLANES

    def body(u_ref, ug_ref, cw_ref, cb_ref, wa_ref, ba_ref, wx_ref, bx_ref, lam_ref,
             h_ref, z_ref, v_ref, ra_ref, ix_ref, a_ref, sq_ref, upad, a_s, b_s):
        _pad_front(upad, u_ref, SUBLANES)
        cw, cb = _tap_rows(cw_ref), cb_ref[...]
        wa, wx = wa_ref[...].astype(BF16), wx_ref[...].astype(BF16)
        ba, bx = ba_ref[...], bx_ref[...]
        sp = _softplus_neg(lam_ref[...])

        def chunk(i, carry):
            rows = pl.ds(pl.multiple_of(i * CHUNK, CHUNK), CHUNK)
            v = _conv_taps(upad[pl.ds(pl.multiple_of(i * CHUNK, CHUNK), CHUNK + SUBLANES), :], cw) + cb
            ra, ix, a, sq = _lru_gates(v, wa, ba, wx, bx, sp)
            v_ref[rows, :], ra_ref[rows, :], ix_ref[rows, :], a_ref[rows, :], sq_ref[rows, :] = v, ra, ix, a, sq
            a_s[rows, :], b_s[rows, :] = a, sq * ix * v
            return carry

        lax.fori_loop(0, nchunk, chunk, 0)
        _scan_in_tiles(a_s, 0, a_s, b_s, T, 1)
        _carry_tiles(a_s, b_s, h_ref, ntile, 1)

        def chunk3(i, carry):
            r0 = pl.multiple_of(i * CHUNK, CHUNK)
            gl, _ = _gelu_parts(ug_ref[pl.ds(r0, CHUNK), :])
            z_ref[pl.ds(r0, CHUNK), :] = (h_ref[pl.ds(r0, CHUNK), :] * gl).astype(BF16)
            return carry

        lax.fori_loop(0, nchunk, chunk3, 0)

    col = pl.BlockSpec((T, HEAD), lambda h: (0, h))
    vec = pl.BlockSpec((1, HEAD), lambda h: (0, h))
    mat = pl.BlockSpec((None, HEAD, HEAD), lambda h: (h, 0, 0))
    return _pallas(
        body, (proj, proj, conv_w, conv_b, w_a, b_a, w_x, b_x, lam), name="rnn_fwd", grid=(N_RNN_HEADS,),
        in_specs=[pl.BlockSpec((T, HEAD), lambda h: (0, COL_RNN + h)), pl.BlockSpec((T, HEAD), lambda h: (0, COL_GATE + h)),
                  pl.BlockSpec((CONV_WIDTH, HEAD), lambda h: (0, h)), vec, mat, vec, mat, vec, vec],
        out_specs=[col] * 7,
        out_shape=[jax.ShapeDtypeStruct((T, D_RNN), F32), jax.ShapeDtypeStruct((T, D_RNN), BF16)]
        + [jax.ShapeDtypeStruct((T, D_RNN), F32)] * 5,
        scratch_shapes=[pltpu.VMEM((T + SUBLANES, HEAD), F32), pltpu.VMEM((T, HEAD), F32), pltpu.VMEM((T, HEAD), F32)],
        semantics=("parallel",), jobs=jobs)


def _rnn_bwd(proj, hr, dz, gates, conv_w, w_a, w_x, lam, jobs=()):
    T = proj.shape[0]
    nchunk = T // CHUNK
    ntile = T // SUBLANES

    def body(u_ref, ug_ref, h_ref, dz_ref, v_ref, ra_ref, ix_ref, a_ref, sq_ref, cw_ref, wa_ref, wx_ref, lam_ref,
             du_ref, dug_ref, dwa_ref, dwx_ref, dba_ref, dbx_ref, dlam_ref, dcb_ref, dcw_ref,
             upad, hpad, apad, g_s, dvpad, ga_s):
        zero_tile = jnp.zeros((SUBLANES, HEAD), F32)
        _pad_front(upad, u_ref, SUBLANES)
        _pad_front(hpad, h_ref, SUBLANES)
        apad[pl.ds(T, SUBLANES), :] = zero_tile
        dvpad[pl.ds(T, SUBLANES), :] = zero_tile
        for ref in (dwa_ref, dwx_ref, dba_ref, dbx_ref, dlam_ref, dcb_ref, dcw_ref):
            ref[...] = jnp.zeros_like(ref)
        cw = _tap_rows(cw_ref)
        wa, wx = wa_ref[...].astype(BF16), wx_ref[...].astype(BF16)
        lam_row = lam_ref[...]
        sp = _softplus_neg(lam_row)

        def chunk(i, carry):
            rows = pl.ds(pl.multiple_of(i * CHUNK, CHUNK), CHUNK)
            apad[rows, :] = a_ref[rows, :]
            gl, dgl = _gelu_parts(ug_ref[rows, :])
            d = dz_ref[rows, :]
            g_s[rows, :] = d * gl
            dug_ref[rows, :] = (d * h_ref[rows, :] * dgl).astype(BF16)
            return carry

        lax.fori_loop(0, nchunk, chunk, 0)

        _scan_in_tiles(apad, 1, ga_s, g_s, T, -1)
        _carry_tiles(ga_s, g_s, g_s, ntile, -1)

        def chunk3(i, carry):
            r0 = pl.multiple_of(i * CHUNK, CHUNK)
            rows = pl.ds(r0, CHUNK)
            g = g_s[rows, :]
            h_prev = _shift_rows(hpad[pl.ds(r0, CHUNK + SUBLANES), :], 1)[SUBLANES:]
            v, ra, ix, sq, a = v_ref[rows, :], ra_ref[rows, :], ix_ref[rows, :], sq_ref[rows, :], a_ref[rows, :]
            d_sq = g * ix * v
            d_ix = g * sq * v
            d_la = a * g * h_prev - d_sq * a * a / sq
            dlam_ref[...] += jnp.sum(d_la * ra, axis=0, keepdims=True)
            d_pa = d_la * (-LRU_C) * sp * ra * (1.0 - ra)
            d_px = d_ix * ix * (1.0 - ix)
            vb, d_pab, d_pxb = v.astype(BF16), d_pa.astype(BF16), d_px.astype(BF16)
            dwa_ref[...] += _dot_tn(vb, d_pab)
            dwx_ref[...] += _dot_tn(vb, d_pxb)
            dba_ref[...] += jnp.sum(d_pa, axis=0, keepdims=True)
            dbx_ref[...] += jnp.sum(d_px, axis=0, keepdims=True)
            dv = g * sq * ix + _dot_nt(d_pab, wa) + _dot_nt(d_pxb, wx)
            dvpad[rows, :] = dv
            dcb_ref[...] += jnp.sum(dv, axis=0, keepdims=True)
            xs = upad[pl.ds(r0, CHUNK + SUBLANES), :]
            for k in range(CONV_WIDTH):
                u_k = _shift_rows(xs, CONV_WIDTH - 1 - k)[SUBLANES:] if k < CONV_WIDTH - 1 else xs[SUBLANES:]
                dcw_ref[k:k + 1, :] += jnp.sum(dv * u_k, axis=0, keepdims=True)
            return carry

        lax.fori_loop(0, nchunk, chunk3, 0)
        dlam_ref[...] = dlam_ref[...] * (LRU_C * _sigmoid(-lam_row))

        def chunk4(i, carry):
            r0 = pl.multiple_of(i * CHUNK, CHUNK)
            dvs = dvpad[pl.ds(r0, CHUNK + SUBLANES), :]
            du = cw[CONV_WIDTH - 1] * dvs[:CHUNK]
            for k in range(CONV_WIDTH - 1):
                du += cw[k] * _shift_rows(dvs, -(CONV_WIDTH - 1 - k))[:CHUNK]
            du_ref[pl.ds(r0, CHUNK), :] = du.astype(BF16)
            return carry

        lax.fori_loop(0, nchunk, chunk4, 0)

    col = pl.BlockSpec((T, HEAD), lambda h: (0, h))
    vec = pl.BlockSpec((1, HEAD), lambda h: (0, h))
    mat = pl.BlockSpec((None, HEAD, HEAD), lambda h: (h, 0, 0))
    taps = pl.BlockSpec((CONV_WIDTH, HEAD), lambda h: (0, h))
    vec_out = jax.ShapeDtypeStruct((1, D_RNN), F32)
    mat_out = jax.ShapeDtypeStruct((N_RNN_HEADS, HEAD, HEAD), F32)
    seq = pltpu.VMEM((T, HEAD), F32)
    seq_pad = pltpu.VMEM((T + SUBLANES, HEAD), F32)
    return _pallas(
        body, (proj, proj, hr, dz, *gates, conv_w, w_a, w_x, lam), name="rnn_bwd", grid=(N_RNN_HEADS,),
        in_specs=[pl.BlockSpec((T, HEAD), lambda h: (0, COL_RNN + h)), pl.BlockSpec((T, HEAD), lambda h: (0, COL_GATE + h))]
        + [col] * 7 + [taps, mat, mat, vec],
        out_specs=[col, col, mat, mat, vec, vec, vec, vec, taps],
        out_shape=[jax.ShapeDtypeStruct((T, D_RNN), BF16), jax.ShapeDtypeStruct((T, D_RNN), BF16), mat_out, mat_out,
                   vec_out, vec_out, vec_out, vec_out, jax.ShapeDtypeStruct((CONV_WIDTH, D_RNN), F32)],
        scratch_shapes=[seq_pad, seq_pad, seq_pad, seq, seq_pad, seq],
        semantics=("parallel",), jobs=jobs)


GROUP_FFN = ["w_ffn_out", "w_ffn_in"]
GROUP_MIX = ["w_o", "w_pool_out", "w_rnn_out"]
GROUP_IN = ["w_in"]


def _step(x, target, s, full, conv_w_mine, place):
    T = x.shape[0]
    tall, mid, low = min(T, 2048), min(T, 1024), min(T, 512)
    c1 = place[1:]
    full = dict(full)

    def gathered(names, results):
        full.update(zip(names, results))

    (full["w_in"], conv_w), = _run_jobs([_gather_job(full, ["w_in"], conv_w_mine)], "gather_w_in")
    early = ["w_pool_out", "w_rnn_out", "w_o", "w_ffn_out"]
    (proj, h1), (res,) = _norm_matmul(x, s["norm_mix"], full["w_in"], tm=tall, tn=512, name="in_proj", jobs=[_gather_job(full, early)])
    gathered(early, res)
    pm = _pool_fwd(proj, s["w_pool_grp"], s["pool_scale"])
    (hr, z, *gates), (res,) = _rnn_fwd(proj, conv_w, s["conv_b"], s["w_rg_a"], s["b_rg_a"], s["w_rg_x"], s["b_rg_x"],
                                       s["lru_lambda"], jobs=[_gather_job(full, ["w_ffn_in"])])
    gathered(["w_ffn_in"], res)
    *mix_by, mix = _branch_mix(pm, z, full["w_pool_out"], full["w_rnn_out"], proj, tm=tall, tn=256)
    x2 = _out_proj_residual(mix, full["w_o"], x, tm=mid)
    act_by_up, act_by_gate, act, h2 = _ffn_in(x2, s["norm_ffn"], full["w_ffn_in"], tm=tall, tn=256)
    dx3, dx3b, sq_cols, g_norm_final = _ffn_out_loss(act, full["w_ffn_out"], x2, s["norm_final"], target, tm=low)

    g = {"norm_final": g_norm_final}

    def chip_sums(names, from_sibling):
        sums = {name: _chip_sum(name, g[name], got, c1) for name, got in zip(names, from_sibling)}
        return {name: v[0] for name, v in sums.items()}, {name: v[1] for name, v in sums.items()}

    def final_sums(names, sums, from_chips):
        return {name: _final_sum(name, sums[name], got, place) for name, got in zip(names, from_chips)}

    dgate, dup = _ffn_out_bwd(dx3b, full["w_ffn_out"], act_by_gate, act_by_up, tm=tall, tn=256)
    g["w_ffn_out"] = _weight_grad(act, [dx3b], tm=256, tn=D_MODEL, name="w_ffn_out_grad")
    dx2, dx2b, g["norm_ffn"] = _ffn_in_bwd(dgate, dup, full["w_ffn_in"], dx3, x2, s["norm_ffn"], tm=low)
    g["w_ffn_in"] = _weight_grad(h2, [dgate, dup], tm=D_MODEL, tn=256, name="w_ffn_in_grad")
    (dgp, dgr, dyp, dyr), (res,) = _out_proj_bwd(dx2b, full["w_o"], mix_by, tm=tall, tn=256,
                                                 jobs=[_sibling_job(g, GROUP_FFN)])
    sums_ffn, sums_ffn_bf16 = chip_sums(GROUP_FFN, res)
    g["w_o"] = _weight_grad(mix, [dx2b], tm=D_MODEL, tn=256, name="w_o_grad")
    dpm, dz = _branch_bwd(dyp, dyr, full["w_pool_out"], full["w_rnn_out"], tm=mid)
    g["w_pool_out"] = _weight_grad(pm, [dyp], tm=D_POOL, tn=256, name="w_pool_out_grad")
    g["w_rnn_out"] = _weight_grad(z, [dyr], tm=D_RNN, tn=256, name="w_rnn_out_grad")
    (dupool, g["w_pool_grp"], g["pool_scale"]), (res,) = _pool_bwd(proj, dpm, s["w_pool_grp"], s["pool_scale"],
                                                                   jobs=[_sibling_job(g, GROUP_MIX)])
    sums_mix, sums_mix_bf16 = chip_sums(GROUP_MIX, res)
    ((durnn, dugate, g["w_rg_a"], g["w_rg_x"], g["b_rg_a"], g["b_rg_x"], g["lru_lambda"], g["conv_b"], g["conv_w"]),
     (res,)) = _rnn_bwd(proj, hr, dz, gates, conv_w, s["w_rg_a"], s["w_rg_x"], s["lru_lambda"],
                        jobs=[_chips_job(sums_ffn_bf16, GROUP_FFN)])
    shards = final_sums(GROUP_FFN, sums_ffn, res)
    segs = [dupool, durnn, dugate, dgp, dgr]
    g["w_in"], (res, joined) = _weight_grad(h1, segs, tm=D_MODEL, tn=256, name="w_in_grad",
                                           jobs=[_chips_job(sums_mix_bf16, GROUP_MIX), _join_job(shards, GROUP_FFN)])
    grads = dict(zip(GROUP_FFN, joined))
    shards = final_sums(GROUP_MIX, sums_mix, res)
    (res,) = _run_jobs([_sibling_job(g, GROUP_IN)], "w_in_exchange_sibling")
    sums_in, sums_in_bf16 = chip_sums(GROUP_IN, res)
    (grad_x, g["norm_mix"]), (res,) = _in_proj_bwd(segs, full["w_in"], dx2, x, s["norm_mix"], tm=low,
                                                  jobs=[_chips_job(sums_in_bf16, GROUP_IN)])
    shards.update(final_sums(GROUP_IN, sums_in, res))

    vec_rows = [g[name] if name != "pool_scale" else jnp.pad(g[name], ((0, 0), (0, D_MODEL - D_POOL))) for name in VEC_ITEMS]
    vec_rows += [g["conv_w"], sq_cols, jnp.zeros((VEC_ROWS - len(VEC_ITEMS) - CONV_WIDTH - 1, D_MODEL), F32)]
    vec = jnp.concatenate(vec_rows, axis=0).reshape(VEC_ROWS, N_DEV, HEAD).transpose(1, 0, 2)
    mat = jnp.concatenate([g[name].reshape(-1, HEAD) for name in MAT_ITEMS], axis=0).reshape(N_DEV, -1, HEAD)
    (vec, mat), (joined,) = _all_reduce_small([vec, mat], jobs=[_join_job(shards, GROUP_MIX + GROUP_IN)])
    grads.update(zip(GROUP_MIX + GROUP_IN, joined))

    vec = vec.transpose(1, 0, 2).reshape(VEC_ROWS, D_MODEL)
    mat = mat.reshape(-1, HEAD)
    for k, name in enumerate(VEC_ITEMS):
        grads[name] = vec[k:k + 1, :s[name].shape[1]]
    grads["conv_w"] = vec[len(VEC_ITEMS):len(VEC_ITEMS) + CONV_WIDTH]
    row = 0
    for name in MAT_ITEMS:
        rows = s[name].shape[0] * HEAD
        grads[name] = mat[row:row + rows]
        row += rows
    return vec[len(VEC_ITEMS) + CONV_WIDTH], grad_x, grads


LARGE = {"w_in": "col", "w_pool_out": "col", "w_rnn_out": "row", "w_o": "row", "w_ffn_in": "col", "w_ffn_out": "row"}
LARGE_SHAPE = {"w_in": (D_MODEL, D_IN), "w_pool_out": (D_POOL, D_MODEL), "w_rnn_out": (D_RNN, D_MODEL),
               "w_o": (D_MODEL, D_MODEL), "w_ffn_in": (D_MODEL, 2 * D_FF), "w_ffn_out": (D_FF, D_MODEL)}


def _place():
    x, y, c = lax.axis_index("x"), lax.axis_index("y"), lax.axis_index("c")
    return 2 * x + y, c


def _chip_device(chip, c):
    return (chip // 2, chip % 2, c)


def _chip_window(ref, kind, shape, chip, half=None):
    K, N = shape
    if kind == "col":
        rows = slice(None) if half is None else pl.ds(half * (K // 2), K // 2)
        return ref.at[rows, pl.ds(chip * (N // N_CHIPS), N // N_CHIPS)]
    ks = K // N_CHIPS
    if half is None:
        return ref.at[pl.ds(chip * ks, ks), :]
    return ref.at[pl.ds(chip * ks + half * (ks // 2), ks // 2), :]


def _row_half(ref, half):
    rows = ref.shape[0] // 2
    return ref.at[pl.ds(half * rows, rows), :]


def _remote(win_src, win_dst, send_sems, recv_sems, idx, to):
    return pltpu.make_async_remote_copy(src_ref=win_src, dst_ref=win_dst, send_sem=send_sems.at[idx], recv_sem=recv_sems.at[idx],
                                        device_id=to, device_id_type=MESH)


def _gather_job(full, names, conv_w_full=None):
    n = len(names)
    cw_cols = D_RNN // N_CHIPS

    def windows(refs, chip, half):
        return [_chip_window(refs[k], LARGE[name], LARGE_SHAPE[name], chip, half) for k, name in enumerate(names)]

    def ici_copies(refs, send_sems, recv_sems, src_chip, dst_chip, c, r):
        wins = windows(refs, src_chip, c)
        if conv_w_full is not None:
            wins.append(refs[n].at[:, pl.ds(src_chip * cw_cols, cw_cols)])
        return [_remote(win, win, send_sems, recv_sems, (k, r), _chip_device(dst_chip, c)) for k, win in enumerate(wins)]

    def forwards(refs, send_sems, recv_sems, src_chip, half, to_core, chip, r):
        return [_remote(win, win, send_sems, recv_sems, (k, 3 + r), _chip_device(chip, to_core))
                for k, win in enumerate(windows(refs, src_chip, half))]

    def start(ins, outs, send_sems, recv_sems):
        chip, c = _place()
        for r in range(3):
            for cp in ici_copies(outs, send_sems, recv_sems, chip, chip ^ (r + 1), c, r):
                cp.start()

    def finish(ins, outs, send_sems, recv_sems):
        chip, c = _place()
        for r in range(3):
            for cp in ici_copies(outs, send_sems, recv_sems, chip ^ (r + 1), chip, c, r):
                cp.wait_recv()
            for cp in forwards(outs, send_sems, recv_sems, chip ^ (r + 1), c, 1 - c, chip, r):
                cp.start()
        for r in range(3):
            for cp in forwards(outs, send_sems, recv_sems, chip ^ (r + 1), 1 - c, c, chip, r):
                cp.wait_recv()
            for cp in ici_copies(outs, send_sems, recv_sems, chip, chip ^ (r + 1), c, r):
                cp.wait_send()
            for cp in forwards(outs, send_sems, recv_sems, chip ^ (r + 1), c, 1 - c, chip, r):
                cp.wait_send()

    arrays = [full[name] for name in names] + ([conv_w_full] if conv_w_full is not None else [])
    return _Job(arrays, [jax.ShapeDtypeStruct(a.shape, a.dtype) for a in arrays], {k: k for k in range(len(arrays))},
                (len(arrays), 6), start, finish)


def _core_halves(ref, kind, shape, c):
    return [_chip_window(ref, kind, shape, chip, c) for chip in range(N_CHIPS)]


def _sibling_job(grads, names):
    def start(ins, outs, send_sems, recv_sems):
        chip, c = _place()
        for k, name in enumerate(names):
            kind, shape = LARGE[name], LARGE_SHAPE[name]
            if kind == "col":
                pairs = [(_row_half(ins[k], 1 - c), outs[k])]
            else:
                rows = shape[0] // N_DEV
                pairs = [(win, outs[k].at[pl.ds(j * rows, rows), :]) for j, win in enumerate(_core_halves(ins[k], kind, shape, 1 - c))]
            for src, dst in pairs:
                _remote(src, dst, send_sems, recv_sems, k, _chip_device(chip, 1 - c)).start()

    def finish(ins, outs, send_sems, recv_sems):
        chip, c = _place()
        for k in range(len(names)):
            _remote(outs[k], outs[k], send_sems, recv_sems, k, _chip_device(chip, 1 - c)).wait()

    return _Job([grads[name] for name in names],
                [jax.ShapeDtypeStruct((LARGE_SHAPE[name][0] // 2, LARGE_SHAPE[name][1]), F32) for name in names], {},
                (len(names),), start, finish)


def _chip_sum(name, g, got, c):
    kind, (K, N) = LARGE[name], LARGE_SHAPE[name]
    rows = K // N_DEV

    def body(c_ref, g_ref, got_ref, o_ref, ob_ref):
        total = g_ref[...] + got_ref[...]
        o_ref[...] = total
        ob_ref[...] = total.astype(BF16)

    if kind == "col":
        mine = pl.BlockSpec((rows, N), lambda j, c_ref: (j + N_CHIPS * c_ref[0], 0))
    else:
        mine = pl.BlockSpec((rows, N), lambda j, c_ref: (2 * j + c_ref[0], 0))
    blk = pl.BlockSpec((rows, N), lambda j, c_ref: (j, 0))
    return pl.pallas_call(
        body, name=name + "_chip_sum",
        grid_spec=pltpu.PrefetchScalarGridSpec(num_scalar_prefetch=1, grid=(N_CHIPS,), in_specs=[mine, blk], out_specs=[blk, blk]),
        out_shape=[jax.ShapeDtypeStruct((K // 2, N), F32), jax.ShapeDtypeStruct((K // 2, N), BF16)],
        compiler_params=_params(dimension_semantics=("parallel",)),
    )(c, g, got)


def _piece(ref, kind, shape, chip):
    K, N = shape
    if kind == "col":
        return ref.at[:, pl.ds(chip * (N // N_CHIPS), N // N_CHIPS)]
    return ref.at[pl.ds(chip * (K // N_DEV), K // N_DEV), :]


def _piece_shape(name):
    kind, (K, N) = LARGE[name], LARGE_SHAPE[name]
    return (K // 2, N // N_CHIPS) if kind == "col" else (K // N_DEV, N)


def _chips_job(sums, names):
    def copies(ins, outs, send_sems, recv_sems):
        chip, c = _place()
        return [_remote(_piece(ins[k], LARGE[name], LARGE_SHAPE[name], chip ^ (r + 1)), outs[k].at[r], send_sems, recv_sems, (k, r),
                        _chip_device(chip ^ (r + 1), c)) for k, name in enumerate(names) for r in range(3)]

    def start(*refs):
        for cp in copies(*refs):
            cp.start()

    def finish(*refs):
        for cp in copies(*refs):
            cp.wait()

    return _Job([sums[name] for name in names], [jax.ShapeDtypeStruct((3,) + _piece_shape(name), BF16) for name in names], {},
                (len(names), 3), start, finish)


def _final_sum(name, chip_sum, got, place):
    kind = LARGE[name]
    rows, cols = _piece_shape(name)

    def body(place_ref, s_ref, got_ref, o_ref):
        o_ref[...] = ((s_ref[...] + got_ref[0].astype(F32)) + got_ref[1].astype(F32)) + got_ref[2].astype(F32)

    if kind == "col":
        mine = pl.BlockSpec((rows, cols), lambda i, place_ref: (0, place_ref[0]))
    else:
        mine = pl.BlockSpec((rows, cols), lambda i, place_ref: (place_ref[0], 0))
    return pl.pallas_call(
        body, name=name + "_final_sum",
        grid_spec=pltpu.PrefetchScalarGridSpec(
            num_scalar_prefetch=1, grid=(1,), in_specs=[mine, pl.BlockSpec((3, rows, cols), lambda i, place_ref: (0, 0, 0))],
            out_specs=pl.BlockSpec((rows, cols), lambda i, place_ref: (place_ref[1], 0))),
        out_shape=jax.ShapeDtypeStruct((2 * rows, cols), F32),
        compiler_params=_params(dimension_semantics=("arbitrary",)),
    )(place, chip_sum, got)


def _join_job(shards, names):
    def half_copy(outs, send_sems, recv_sems, k, mine):
        chip, c = _place()
        win = _row_half(outs[k], c if mine else 1 - c)
        return _remote(win, win, send_sems, recv_sems, k, _chip_device(chip, 1 - c))

    def start(ins, outs, send_sems, recv_sems):
        for k in range(len(names)):
            half_copy(outs, send_sems, recv_sems, k, True).start()

    def finish(ins, outs, send_sems, recv_sems):
        for k in range(len(names)):
            half_copy(outs, send_sems, recv_sems, k, True).wait_send()
            half_copy(outs, send_sems, recv_sems, k, False).wait_recv()

    arrays = [shards[name] for name in names]
    return _Job(arrays, [jax.ShapeDtypeStruct(a.shape, F32) for a in arrays], {k: k for k in range(len(arrays))},
                (len(arrays),), start, finish)


VEC_ROWS = 16


def _all_reduce_small(slabs, jobs=()):
    n = len(slabs)

    def body(*refs):
        in_refs, out_refs, got_refs = refs[:n], refs[n:2 * n], refs[2 * n:3 * n]
        send_sems, recv_sems = refs[3 * n:]
        x, y, c = lax.axis_index("x"), lax.axis_index("y"), lax.axis_index("c")
        me = 4 * x + 2 * y + c

        def remote(src, dst, k, phase, r):
            other = me ^ r
            return pltpu.make_async_remote_copy(src_ref=src, dst_ref=dst, send_sem=send_sems.at[k, phase, r],
                                                recv_sem=recv_sems.at[k, phase, r],
                                                device_id=(other // 4, (other // 2) % 2, other % 2), device_id_type=MESH)

        scatter = [remote(in_refs[k].at[me ^ r], got_refs[k].at[r], k, 0, r) for r in range(1, N_DEV) for k in range(n)]
        for cp in scatter:
            cp.start()
        for cp in scatter:
            cp.wait()
        for k in range(n):
            total = in_refs[k][me]
            for r in range(1, N_DEV):
                total = total + got_refs[k][r]
            out_refs[k][me] = total
        gather = [remote(out_refs[k].at[me], out_refs[k].at[me], k, 1, r) for r in range(1, N_DEV) for k in range(n)]
        for cp in gather:
            cp.start()
        for r in range(1, N_DEV):
            for k in range(n):
                remote(out_refs[k].at[me ^ r], out_refs[k].at[me ^ r], k, 1, r).wait_recv()
        for cp in gather:
            cp.wait_send()

    return _pallas(
        body, slabs, name="all_reduce_small", grid=(), in_specs=[VMEM] * n, out_specs=[VMEM] * n,
        out_shape=[jax.ShapeDtypeStruct(s.shape, F32) for s in slabs],
        scratch_shapes=[pltpu.VMEM(s.shape, F32) for s in slabs]
        + [pltpu.SemaphoreType.DMA((n, 2, N_DEV)), pltpu.SemaphoreType.DMA((n, 2, N_DEV))], jobs=jobs)


def _cast_into_whole(w, name, place):
    rows, cols = w.shape
    tr = rows // 2

    def body(place_ref, w_ref, o_ref):
        o_ref[...] = w_ref[...].astype(BF16)

    if LARGE[name] == "col":
        window = pl.BlockSpec((tr, cols), lambda i, place_ref: (i, place_ref[0]))
    else:
        window = pl.BlockSpec((tr, cols), lambda i, place_ref: (2 * place_ref[0] + i, 0))
    return pl.pallas_call(
        body, name=name + "_cast",
        grid_spec=pltpu.PrefetchScalarGridSpec(num_scalar_prefetch=1, grid=(2,),
                                               in_specs=[pl.BlockSpec((tr, cols), lambda i, place_ref: (i, 0))], out_specs=window),
        out_shape=jax.ShapeDtypeStruct(LARGE_SHAPE[name], BF16),
        compiler_params=_params(dimension_semantics=("parallel",)))(place, w)


def _adamw_math(w, g, m, v):
    m = ADAM_B1 * m + (1.0 - ADAM_B1) * g
    v = ADAM_B2 * v + (1.0 - ADAM_B2) * (g * g)
    m_hat = m / (1.0 - ADAM_B1 ** ADAM_STEP)
    v_hat = v / (1.0 - ADAM_B2 ** ADAM_STEP)
    delta = -ADAM_LR * (m_hat / (jnp.sqrt(v_hat) + ADAM_EPS) + ADAM_WD * w)
    return delta, m, v


def _adamw_large(w, g, m, v, name):
    rows, cols = w.shape
    tr = rows // 4

    def body(w_ref, g_ref, m_ref, v_ref, d_ref, mo_ref, vo_ref):
        d_ref[...], mo_ref[...], vo_ref[...] = _adamw_math(w_ref[...], g_ref[...], m_ref[...], v_ref[...])

    blk = pl.BlockSpec((tr, cols), lambda i: (i, 0))
    out = jax.ShapeDtypeStruct(w.shape, F32)
    return pl.pallas_call(body, name=name + "_adamw", grid=(4,), in_specs=[blk] * 4, out_specs=[blk] * 3, out_shape=[out] * 3,
                          compiler_params=_params(dimension_semantics=("parallel",)))(w, g, m, v)


def _adamw_small(ws, gs, ms, vs):
    n = len(ws)

    def body(*refs):
        for k in range(n):
            w_ref, g_ref, m_ref, v_ref = (refs[q * n + k] for q in range(4))
            d_ref, mo_ref, vo_ref = (refs[(4 + q) * n + k] for q in range(3))
            d_ref[...], mo_ref[...], vo_ref[...] = _adamw_math(w_ref[...], g_ref[...], m_ref[...], v_ref[...])

    out = [jax.ShapeDtypeStruct(w.shape, F32) for w in ws]
    res = pl.pallas_call(body, name="small_adamw", in_specs=[VMEM] * (4 * n), out_specs=[VMEM] * (3 * n), out_shape=out * 3,
                         compiler_params=_params())(*ws, *gs, *ms, *vs)
    return res[:n], res[n:2 * n], res[2 * n:]


WEIGHTS = ["norm_mix", "w_in", "w_pool_grp", "pool_scale", "w_pool_out", "conv_w", "conv_b", "w_rg_a", "b_rg_a", "w_rg_x",
           "b_rg_x", "lru_lambda", "w_rnn_out", "w_o", "norm_ffn", "w_ffn_in", "w_ffn_out", "norm_final"]
VEC_ITEMS = ["norm_mix", "norm_ffn", "norm_final", "pool_scale", "conv_b", "lru_lambda", "b_rg_a", "b_rg_x"]
MAT_ITEMS = ["w_pool_grp", "w_rg_a", "w_rg_x"]


def _as2d(name, a):
    if name in MAT_ITEMS:
        return a.reshape(-1, HEAD, HEAD)
    if name == "conv_w":
        return a.reshape(CONV_WIDTH, -1)
    return a.reshape(1, -1)


def kernel(x, norm_mix, w_in, w_pool_grp, pool_scale, w_pool_out, conv_w, conv_b, w_rg_a, b_rg_a, w_rg_x, b_rg_x, lru_lambda, w_rnn_out, w_o, norm_ffn, w_ffn_in, w_ffn_out, norm_final, loss_target, m_norm_mix, m_w_in, m_w_pool_grp, m_pool_scale, m_w_pool_out, m_conv_w, m_conv_b, m_w_rg_a, m_b_rg_a, m_w_rg_x, m_b_rg_x, m_lru_lambda, m_w_rnn_out, m_w_o, m_norm_ffn, m_w_ffn_in, m_w_ffn_out, m_norm_final, v_norm_mix, v_w_in, v_w_pool_grp, v_pool_scale, v_w_pool_out, v_conv_w, v_conv_b, v_w_rg_a, v_b_rg_a, v_w_rg_x, v_b_rg_x, v_lru_lambda, v_w_rnn_out, v_w_o, v_norm_ffn, v_w_ffn_in, v_w_ffn_out, v_norm_final):
    given = dict(locals())
    w = {name: given[name] for name in WEIGHTS}
    m = {name: given["m_" + name] for name in WEIGHTS}
    v = {name: given["v_" + name] for name in WEIGHTS}
    chip, c = _place()

    place = jnp.stack([chip, c]).astype(jnp.int32)
    conv_cols = w["conv_w"].shape[-1]
    conv_w_mine = lax.dynamic_update_slice_in_dim(jnp.zeros((CONV_WIDTH, D_RNN), F32), w["conv_w"][0], chip * conv_cols, axis=1)
    full = {name: _cast_into_whole(w[name][0], name, place) for name in LARGE}
    small = {name: _as2d(name, w[name]) for name in WEIGHTS if name not in LARGE and name != "conv_w"}
    sq_cols, grad_x, grads = _step(x[0], loss_target[0], small, full, conv_w_mine, place)
    loss = 0.5 / D_MODEL * jnp.sum(sq_cols)
    grads["conv_w"] = lax.dynamic_slice_in_dim(grads["conv_w"], chip * conv_cols, conv_cols, axis=1)

    delta, new_m, new_v = {}, {}, {}
    for name in LARGE:
        delta[name], new_m[name], new_v[name] = _adamw_large(w[name][0], grads[name], m[name][0], v[name][0], name)
    small_names = [name for name in WEIGHTS if name not in LARGE]
    flat = lambda d: [d[name].reshape(grads[name].shape) for name in small_names]
    ds, mo, vo = _adamw_small(flat(w), [grads[name] for name in small_names], flat(m), flat(v))
    for k, name in enumerate(small_names):
        delta[name], new_m[name], new_v[name] = ds[k], mo[k], vo[k]

    shaped = lambda d: [d[name].reshape(w[name].shape) for name in WEIGHTS]
    return (loss, grad_x[None], *shaped(grads), *shaped(delta), *shaped(new_m), *shaped(new_v))
```

```python
import functools
import math

import jax
import jax.numpy as jnp
from jax import lax
from jax.experimental import pallas as pl
from jax.experimental.pallas import tpu as pltpu

F32 = jnp.float32
BF16 = jnp.bfloat16

D_MODEL = 1024
D_POOL = 512
N_POOL_GROUPS = 4
D_RNN = 1024
N_RNN_HEADS = 8
HEAD = 128
CONV_WIDTH = 4
LRU_C = 8.0
D_FF = 2816
D_IN = D_POOL + 2 * D_RNN + 2 * D_MODEL
NORM_EPS = 1e-6
COL_RNN = D_POOL // HEAD
COL_GATE = (D_POOL + D_RNN) // HEAD

ADAM_LR = 0.001
ADAM_B1 = 0.9
ADAM_B2 = 0.999
ADAM_EPS = 1e-08
ADAM_WD = 0.01
ADAM_STEP = 10

N_CHIPS = 4
N_DEV = 8
MESH = pl.DeviceIdType.MESH
ANY = pl.BlockSpec(memory_space=pl.ANY)
VMEM = pl.BlockSpec(memory_space=pltpu.VMEM)
VMEM_LIMIT_BYTES = 60 * 1024 * 1024
SUBLANES = 8
POOL_HALO = 16
CHUNK = 1024

GELU_C = math.sqrt(2.0 / math.pi)
GELU_A = 0.044715


def _params(**kw):
    return pltpu.CompilerParams(vmem_limit_bytes=VMEM_LIMIT_BYTES, **kw)


def _sigmoid(x):
    return 0.5 * jnp.tanh(0.5 * x) + 0.5


def _log1p(y):
    u = 1.0 + y
    d = u - 1.0
    return jnp.where(d == 0.0, y, jnp.log(u) * (y / jnp.where(d == 0.0, 1.0, d)))


def _gelu_parts(x):
    x2 = x * x
    th = jnp.tanh(GELU_C * (x + GELU_A * x * x2))
    g = 0.5 * x * (1.0 + th)
    dg = 0.5 * (1.0 + th) + 0.5 * x * (1.0 - th * th) * GELU_C * (1.0 + 3.0 * GELU_A * x2)
    return g, dg


def _dot(a, b):
    return jnp.dot(a, b, preferred_element_type=F32)


def _dot_nt(a, b):
    return lax.dot_general(a, b, (((1,), (1,)), ((), ())), preferred_element_type=F32)


def _dot_tn(a, b):
    return lax.dot_general(a, b, (((0,), (0,)), ((), ())), preferred_element_type=F32)


def _rms_scale(xv):
    return lax.rsqrt(jnp.mean(xv * xv, axis=-1, keepdims=True) + NORM_EPS)


def _rms_bwd(dy, xv, g):
    r = _rms_scale(xv)
    xh = xv * r
    dyg = dy * g
    dx = r * (dyg - xh * jnp.mean(dyg * xh, axis=-1, keepdims=True))
    return dx, dy * xh


class _Job:
    def __init__(self, inputs, out_shapes, aliases, sem_shape, start, finish):
        self.inputs, self.out_shapes, self.aliases, self.sem_shape = list(inputs), list(out_shapes), dict(aliases), sem_shape
        self.start, self.finish = start, finish


def _pallas(body, operands, *, name, grid, in_specs, out_specs, out_shape, scratch_shapes=(), semantics=None, jobs=()):
    n_in, n_out, n_scr = len(in_specs), len(out_specs), len(scratch_shapes)
    job_in = [a for job in jobs for a in job.inputs]
    job_out = [s for job in jobs for s in job.out_shapes]
    aliases, i0, o0 = {}, n_in, n_out
    for job in jobs:
        aliases.update({i0 + i: o0 + o for i, o in job.aliases.items()})
        i0, o0 = i0 + len(job.inputs), o0 + len(job.out_shapes)

    def whole(*refs):
        ins, j_ins = refs[:n_in], refs[n_in:n_in + len(job_in)]
        outs = refs[n_in + len(job_in):][:n_out]
        j_outs = refs[n_in + len(job_in) + n_out:][:len(job_out)]
        rest = refs[n_in + len(job_in) + n_out + len(job_out):]
        scr, sems = rest[:n_scr], rest[n_scr:]

        def run(phase):
            i, o = 0, 0
            for k, job in enumerate(jobs):
                getattr(job, phase)(j_ins[i:i + len(job.inputs)], j_outs[o:o + len(job.out_shapes)], sems[2 * k], sems[2 * k + 1])
                i, o = i + len(job.inputs), o + len(job.out_shapes)

        def at(step_of, phase):
            if not jobs:
                return
            if not grid:
                run(phase)
                return
            cond = functools.reduce(jnp.logical_and, [pl.program_id(d) == step_of(d) for d in range(len(grid))])
            pl.when(cond)(functools.partial(run, phase))

        at(lambda d: 0, "start")
        body(*ins, *outs, *scr)
        at(lambda d: grid[d] - 1, "finish")

    res = pl.pallas_call(
        whole, name=name, grid=grid, in_specs=list(in_specs) + [ANY] * len(job_in), out_specs=list(out_specs) + [ANY] * len(job_out),
        out_shape=list(out_shape) + job_out, input_output_aliases=aliases,
        scratch_shapes=list(scratch_shapes) + [pltpu.SemaphoreType.DMA(job.sem_shape) for job in jobs for _ in range(2)],
        compiler_params=_params(dimension_semantics=semantics, has_side_effects=bool(jobs)),
    )(*operands, *job_in)
    per_job, o = [], n_out
    for job in jobs:
        per_job.append(res[o:o + len(job.out_shapes)])
        o += len(job.out_shapes)
    return res[:n_out], per_job


def _run_jobs(jobs, name):
    return _pallas(lambda: None, [], name=name, grid=(), in_specs=[], out_specs=[], out_shape=[], jobs=jobs)[1]


NORM_ROWS = 256


def _norm_rows(x_ref, g_ref, h_ref):
    g = g_ref[...]

    def rows(i, carry):
        r = pl.ds(pl.multiple_of(i * NORM_ROWS, NORM_ROWS), NORM_ROWS)
        xv = x_ref[r, :]
        h_ref[r, :] = (xv * _rms_scale(xv) * g).astype(BF16)
        return carry

    lax.fori_loop(0, x_ref.shape[0] // NORM_ROWS, rows, 0)


def _norm_matmul(x, g, w, *, tm, tn, name, jobs=()):
    T, K = x.shape
    N = w.shape[1]

    def body(x_ref, g_ref, w_ref, o_ref, h_ref):
        @pl.when(pl.program_id(1) == 0)
        def _():
            _norm_rows(x_ref, g_ref, h_ref)

        o_ref[...] = _dot(h_ref[...], w_ref[...])

    return _pallas(
        body, (x, g, w), name=name, grid=(T // tm, N // tn),
        in_specs=[pl.BlockSpec((tm, K), lambda i, j: (i, 0)), pl.BlockSpec((1, K), lambda i, j: (0, 0)),
                  pl.BlockSpec((K, tn), lambda i, j: (0, j))],
        out_specs=[pl.BlockSpec((tm, tn), lambda i, j: (i, j)), pl.BlockSpec((tm, K), lambda i, j: (i, 0))],
        out_shape=[jax.ShapeDtypeStruct((T, N), F32), jax.ShapeDtypeStruct((T, K), BF16)],
        semantics=("parallel", "arbitrary"), jobs=jobs)


def _ffn_in(x2, g, w, *, tm, tn):
    T, K = x2.shape
    nb = D_FF // tn

    def body(x_ref, g_ref, wg_ref, wu_ref, dup_ref, dgate_ref, act_ref, h_ref):
        @pl.when(pl.program_id(1) == 0)
        def _():
            _norm_rows(x_ref, g_ref, h_ref)

        h = h_ref[...]
        gate = _dot(h, wg_ref[...])
        up = _dot(h, wu_ref[...])
        s = _sigmoid(gate)
        silu = gate * s
        dup_ref[...] = silu.astype(BF16)
        dgate_ref[...] = (up * (s + silu * (1.0 - s))).astype(BF16)
        act_ref[...] = (silu * up).astype(BF16)

    blk = pl.BlockSpec((tm, tn), lambda i, j: (i, j))
    return pl.pallas_call(
        body, name="ffn_in", grid=(T // tm, nb),
        in_specs=[pl.BlockSpec((tm, K), lambda i, j: (i, 0)), pl.BlockSpec((1, K), lambda i, j: (0, 0)),
                  pl.BlockSpec((K, tn), lambda i, j: (0, j)), pl.BlockSpec((K, tn), lambda i, j: (0, j + nb))],
        out_specs=[blk, blk, blk, pl.BlockSpec((tm, K), lambda i, j: (i, 0))],
        out_shape=[jax.ShapeDtypeStruct((T, D_FF), BF16), jax.ShapeDtypeStruct((T, D_FF), BF16),
                   jax.ShapeDtypeStruct((T, D_FF), BF16), jax.ShapeDtypeStruct((T, K), BF16)],
        compiler_params=_params(dimension_semantics=("parallel", "arbitrary")),
    )(x2, g, w, w)


def _branch_mix(pm, z, w_pool_out, w_rnn_out, proj, *, tm, tn):
    T = pm.shape[0]
    col_gp = (D_POOL + 2 * D_RNN) // tn
    col_gr = col_gp + D_MODEL // tn

    def body(pm_ref, z_ref, wp_ref, wr_ref, gp_ref, gr_ref, by_gp_ref, by_gr_ref, sp_ref, sr_ref, mix_ref):
        yp = _dot(pm_ref[...], wp_ref[...])
        yr = _dot(z_ref[...], wr_ref[...])
        sp, sr = _sigmoid(gp_ref[...]), _sigmoid(gr_ref[...])
        by_gp_ref[...] = (yp * sp * (1.0 - sp)).astype(BF16)
        by_gr_ref[...] = (yr * sr * (1.0 - sr)).astype(BF16)
        sp_ref[...] = sp.astype(BF16)
        sr_ref[...] = sr.astype(BF16)
        mix_ref[...] = (sp * yp + sr * yr).astype(BF16)

    blk = pl.BlockSpec((tm, tn), lambda i, j: (i, j))
    out = jax.ShapeDtypeStruct((T, D_MODEL), BF16)
    return pl.pallas_call(
        body, name="branch_mix", grid=(T // tm, D_MODEL // tn),
        in_specs=[pl.BlockSpec((tm, D_POOL), lambda i, j: (i, 0)), pl.BlockSpec((tm, D_RNN), lambda i, j: (i, 0)),
                  pl.BlockSpec((D_POOL, tn), lambda i, j: (0, j)), pl.BlockSpec((D_RNN, tn), lambda i, j: (0, j)),
                  pl.BlockSpec((tm, tn), lambda i, j: (i, col_gp + j)), pl.BlockSpec((tm, tn), lambda i, j: (i, col_gr + j))],
        out_specs=[blk] * 5, out_shape=[out] * 5,
        compiler_params=_params(dimension_semantics=("parallel", "parallel")),
    )(pm, z, w_pool_out, w_rnn_out, proj, proj)


def _out_proj_residual(mix, w_o, x, *, tm):
    T = x.shape[0]

    def body(mix_ref, w_ref, x_ref, o_ref):
        o_ref[...] = x_ref[...] + _dot(mix_ref[...], w_ref[...])

    row = pl.BlockSpec((tm, D_MODEL), lambda i: (i, 0))
    return pl.pallas_call(
        body, name="out_proj_residual", grid=(T // tm,),
        in_specs=[row, pl.BlockSpec((D_MODEL, D_MODEL), lambda i: (0, 0)), row],
        out_specs=row, out_shape=jax.ShapeDtypeStruct((T, D_MODEL), F32),
        compiler_params=_params(dimension_semantics=("parallel",)),
    )(mix, w_o, x)


def _ffn_out_loss(act, w, x2, g3, target, *, tm):
    T = x2.shape[0]

    def body(act_ref, w_ref, x2_ref, g_ref, t_ref, dx_ref, dxb_ref, sq_ref, dg_ref):
        @pl.when(pl.program_id(0) == 0)
        def _():
            sq_ref[...] = jnp.zeros_like(sq_ref)
            dg_ref[...] = jnp.zeros_like(dg_ref)

        x3 = x2_ref[...] + _dot(act_ref[...], w_ref[...])
        g = g_ref[...]
        err = x3 * _rms_scale(x3) * g - t_ref[...]
        sq_ref[...] += jnp.sum(err * err, axis=0, keepdims=True)
        dx, dgp = _rms_bwd(err * (1.0 / D_MODEL), x3, g)
        dg_ref[...] += jnp.sum(dgp, axis=0, keepdims=True)
        dx_ref[...] = dx
        dxb_ref[...] = dx.astype(BF16)

    row = pl.BlockSpec((tm, D_MODEL), lambda i: (i, 0))
    vec = pl.BlockSpec((1, D_MODEL), lambda i: (0, 0))
    return pl.pallas_call(
        body, name="ffn_out_loss", grid=(T // tm,),
        in_specs=[pl.BlockSpec((tm, D_FF), lambda i: (i, 0)), pl.BlockSpec((D_FF, D_MODEL), lambda i: (0, 0)), row, vec, row],
        out_specs=[row, row, vec, vec],
        out_shape=[jax.ShapeDtypeStruct((T, D_MODEL), F32), jax.ShapeDtypeStruct((T, D_MODEL), BF16),
                   jax.ShapeDtypeStruct((1, D_MODEL), F32), jax.ShapeDtypeStruct((1, D_MODEL), F32)],
        compiler_params=_params(dimension_semantics=("arbitrary",)),
    )(act, w, x2, g3, target)


def _ffn_out_bwd(dx3b, w, act_by_gate, act_by_up, *, tm, tn):
    T = dx3b.shape[0]

    def body(dx_ref, w_ref, by_gate_ref, by_up_ref, dgate_ref, dup_ref):
        dact = _dot_nt(dx_ref[...], w_ref[...])
        dgate_ref[...] = (dact * by_gate_ref[...].astype(F32)).astype(BF16)
        dup_ref[...] = (dact * by_up_ref[...].astype(F32)).astype(BF16)

    blk = pl.BlockSpec((tm, tn), lambda i, j: (i, j))
    return pl.pallas_call(
        body, name="ffn_out_bwd", grid=(T // tm, D_FF // tn),
        in_specs=[pl.BlockSpec((tm, D_MODEL), lambda i, j: (i, 0)), pl.BlockSpec((tn, D_MODEL), lambda i, j: (j, 0)), blk, blk],
        out_specs=[blk, blk],
        out_shape=[jax.ShapeDtypeStruct((T, D_FF), BF16), jax.ShapeDtypeStruct((T, D_FF), BF16)],
        compiler_params=_params(dimension_semantics=("parallel", "parallel")),
    )(dx3b, w, act_by_gate, act_by_up)


def _ffn_in_bwd(dgate, dup, w, dx3, x2, g2, *, tm, jobs=()):
    T = x2.shape[0]

    def body(dgate_ref, dup_ref, w_ref, dx3_ref, x2_ref, g_ref, dx_ref, dxb_ref, dg_ref):
        @pl.when(pl.program_id(0) == 0)
        def _():
            dg_ref[...] = jnp.zeros_like(dg_ref)

        dh = _dot_nt(dgate_ref[...], w_ref[:, :D_FF]) + _dot_nt(dup_ref[...], w_ref[:, D_FF:])
        dxn, dgp = _rms_bwd(dh, x2_ref[...], g_ref[...])
        dx = dx3_ref[...] + dxn
        dg_ref[...] += jnp.sum(dgp, axis=0, keepdims=True)
        dx_ref[...] = dx
        dxb_ref[...] = dx.astype(BF16)

    row = pl.BlockSpec((tm, D_MODEL), lambda i: (i, 0))
    wide = pl.BlockSpec((tm, D_FF), lambda i: (i, 0))
    vec = pl.BlockSpec((1, D_MODEL), lambda i: (0, 0))
    return _pallas(
        body, (dgate, dup, w, dx3, x2, g2), name="ffn_in_bwd", grid=(T // tm,),
        in_specs=[wide, wide, pl.BlockSpec((D_MODEL, 2 * D_FF), lambda i: (0, 0)), row, row, vec],
        out_specs=[row, row, vec],
        out_shape=[jax.ShapeDtypeStruct((T, D_MODEL), F32), jax.ShapeDtypeStruct((T, D_MODEL), BF16),
                   jax.ShapeDtypeStruct((1, D_MODEL), F32)],
        semantics=("arbitrary",), jobs=jobs)


def _out_proj_bwd(dx2b, w_o, mix_by, *, tm, tn, jobs=()):
    T = dx2b.shape[0]

    def body(dx_ref, w_ref, *refs):
        dmix = _dot_nt(dx_ref[...], w_ref[...])
        for by_ref, d_ref in zip(refs[:4], refs[4:]):
            d_ref[...] = (dmix * by_ref[...].astype(F32)).astype(BF16)

    blk = pl.BlockSpec((tm, tn), lambda i, j: (i, j))
    out = jax.ShapeDtypeStruct((T, D_MODEL), BF16)
    return _pallas(
        body, (dx2b, w_o, *mix_by), name="out_proj_bwd", grid=(T // tm, D_MODEL // tn),
        in_specs=[pl.BlockSpec((tm, D_MODEL), lambda i, j: (i, 0)), pl.BlockSpec((tn, D_MODEL), lambda i, j: (j, 0))] + [blk] * 4,
        out_specs=[blk] * 4, out_shape=[out] * 4, semantics=("parallel", "parallel"), jobs=jobs)


def _branch_bwd(dyp, dyr, w_pool_out, w_rnn_out, *, tm):
    T = dyp.shape[0]

    def body(dyp_ref, dyr_ref, wp_ref, wr_ref, dpm_ref, dz_ref):
        dpm_ref[...] = _dot_nt(dyp_ref[...], wp_ref[...])
        dz_ref[...] = _dot_nt(dyr_ref[...], wr_ref[...])

    row = pl.BlockSpec((tm, D_MODEL), lambda i: (i, 0))
    return pl.pallas_call(
        body, name="branch_bwd", grid=(T // tm,),
        in_specs=[row, row, pl.BlockSpec((D_POOL, D_MODEL), lambda i: (0, 0)), pl.BlockSpec((D_RNN, D_MODEL), lambda i: (0, 0))],
        out_specs=[pl.BlockSpec((tm, D_POOL), lambda i: (i, 0)), pl.BlockSpec((tm, D_RNN), lambda i: (i, 0))],
        out_shape=[jax.ShapeDtypeStruct((T, D_POOL), F32), jax.ShapeDtypeStruct((T, D_RNN), F32)],
        compiler_params=_params(dimension_semantics=("parallel",)),
    )(dyp, dyr, w_pool_out, w_rnn_out)


def _in_proj_bwd(segs, w, dx2, x, g1, *, tm, jobs=()):
    T = x.shape[0]
    widths = [s.shape[1] for s in segs]
    offs = [sum(widths[:k]) for k in range(len(widths))]
    n = len(segs)

    def body(*refs):
        seg_refs, (w_ref, dx2_ref, x_ref, g_ref, dx_ref, dg_ref) = refs[:n], refs[n:]

        @pl.when(pl.program_id(0) == 0)
        def _():
            dg_ref[...] = jnp.zeros_like(dg_ref)

        dh = _dot_nt(seg_refs[0][...], w_ref[:, offs[0]:offs[0] + widths[0]])
        for k in range(1, n):
            dh += _dot_nt(seg_refs[k][...], w_ref[:, offs[k]:offs[k] + widths[k]])
        dxn, dgp = _rms_bwd(dh, x_ref[...], g_ref[...])
        dg_ref[...] += jnp.sum(dgp, axis=0, keepdims=True)
        dx_ref[...] = dx2_ref[...] + dxn

    row = pl.BlockSpec((tm, D_MODEL), lambda i: (i, 0))
    vec = pl.BlockSpec((1, D_MODEL), lambda i: (0, 0))
    return _pallas(
        body, (*segs, w, dx2, x, g1), name="in_proj_bwd", grid=(T // tm,),
        in_specs=[pl.BlockSpec((tm, wd), lambda i: (i, 0)) for wd in widths]
        + [pl.BlockSpec((D_MODEL, D_IN), lambda i: (0, 0)), row, row, vec],
        out_specs=[row, vec],
        out_shape=[jax.ShapeDtypeStruct((T, D_MODEL), F32), jax.ShapeDtypeStruct((1, D_MODEL), F32)],
        semantics=("arbitrary",), jobs=jobs)


def _weight_grad(a, segs, *, tm, tn, name, jobs=None):
    T, M = a.shape
    nblk = [s.shape[1] // tn for s in segs]
    first = [sum(nblk[:k]) for k in range(len(segs))]
    n = len(segs)

    def body(a_ref, *refs):
        seg_refs, o_ref = refs[:n], refs[n]
        j = pl.program_id(1)
        for k in range(n):
            @pl.when((j >= first[k]) & (j < first[k] + nblk[k]))
            def _(k=k):
                o_ref[...] = _dot_tn(a_ref[...], seg_refs[k][...])

    def seg_spec(k):
        return pl.BlockSpec((T, tn), lambda i, j: (0, jnp.clip(j - first[k], 0, nblk[k] - 1)))

    (grad,), results = _pallas(
        body, (a, *segs), name=name, grid=(M // tm, sum(nblk)),
        in_specs=[pl.BlockSpec((T, tm), lambda i, j: (0, i))] + [seg_spec(k) for k in range(n)],
        out_specs=[pl.BlockSpec((tm, tn), lambda i, j: (i, j))],
        out_shape=[jax.ShapeDtypeStruct((M, sum(nblk) * tn), F32)],
        semantics=("parallel", "arbitrary"), jobs=jobs or ())
    return grad if jobs is None else (grad, results)


def _pad_front(dst, src, halo):
    dst[pl.ds(0, halo), :] = jnp.zeros((halo, src.shape[1]), F32)

    def fill(i, carry):
        r0 = pl.multiple_of(i * CHUNK, CHUNK)
        dst[pl.ds(r0 + halo, CHUNK), :] = src[pl.ds(r0, CHUNK), :]
        return carry

    lax.fori_loop(0, src.shape[0] // CHUNK, fill, 0)


def _shift_rows(v, k):
    return pltpu.roll(v, k % v.shape[0], axis=0)


def _window_sums(xs, direction):
    s2 = xs + _shift_rows(xs, direction)
    s4 = s2 + _shift_rows(s2, 2 * direction)
    s8 = s4 + _shift_rows(s4, 4 * direction)
    s16 = s8 + _shift_rows(s8, 8 * direction)
    return s2, s4, s8, s16


def _select_window(g, sums):
    s2, s4, s8, s16 = sums
    return jnp.where(g == 0, s2, jnp.where(g == 1, s4, jnp.where(g == 2, s8, s16)))


def _pool_count(g, start, rows):
    t = start + lax.broadcasted_iota(jnp.int32, (rows, 1), 0)
    return jnp.minimum(t + 1, jnp.left_shift(2, g)).astype(F32)


def _pool_fwd(proj, w_grp, scale):
    T = proj.shape[0]
    nchunk = T // CHUNK

    def body(u_ref, w_ref, s_ref, o_ref, upad):
        g = pl.program_id(0)
        _pad_front(upad, u_ref, POOL_HALO)
        w = w_ref[...].astype(BF16)
        scale_row = s_ref[...]

        def chunk(i, carry):
            r0 = pl.multiple_of(i * CHUNK, CHUNK)
            xs = upad[pl.ds(r0, CHUNK + POOL_HALO), :]
            win = _select_window(g, _window_sums(xs, 1))[POOL_HALO:]
            pooled = win / _pool_count(g, r0, CHUNK) - xs[POOL_HALO:]
            o_ref[pl.ds(r0, CHUNK), :] = (_dot(pooled.astype(BF16), w) * scale_row).astype(BF16)
            return carry

        lax.fori_loop(0, nchunk, chunk, 0)

    return pl.pallas_call(
        body, name="pool_fwd", grid=(N_POOL_GROUPS,),
        in_specs=[pl.BlockSpec((T, HEAD), lambda g: (0, g)), pl.BlockSpec((None, HEAD, HEAD), lambda g: (g, 0, 0)),
                  pl.BlockSpec((1, HEAD), lambda g: (0, g))],
        out_specs=pl.BlockSpec((T, HEAD), lambda g: (0, g)),
        out_shape=jax.ShapeDtypeStruct((T, D_POOL), BF16),
        scratch_shapes=[pltpu.VMEM((T + POOL_HALO, HEAD), F32)],
        compiler_params=_params(dimension_semantics=("parallel",)),
    )(proj, w_grp, scale)


def _pool_bwd(proj, dpm, w_grp, scale, jobs=()):
    T = proj.shape[0]
    nchunk = T // CHUNK

    def body(u_ref, dpm_ref, w_ref, s_ref, du_ref, dw_ref, ds_ref, upad, zpad, dpool):
        g = pl.program_id(0)
        _pad_front(upad, u_ref, POOL_HALO)
        zpad[pl.ds(T, POOL_HALO), :] = jnp.zeros((POOL_HALO, HEAD), F32)
        dw_ref[...] = jnp.zeros_like(dw_ref)
        ds_ref[...] = jnp.zeros_like(ds_ref)
        w = w_ref[...].astype(BF16)
        scale_row = s_ref[...]

        def chunk(i, carry):
            r0 = pl.multiple_of(i * CHUNK, CHUNK)
            xs = upad[pl.ds(r0, CHUNK + POOL_HALO), :]
            cnt = _pool_count(g, r0, CHUNK)
            pooled = (_select_window(g, _window_sums(xs, 1))[POOL_HALO:] / cnt - xs[POOL_HALO:]).astype(BF16)
            mixed = _dot(pooled, w)
            d = dpm_ref[pl.ds(r0, CHUNK), :]
            ds_ref[...] += jnp.sum(d * mixed, axis=0, keepdims=True)
            dmixed = (d * scale_row).astype(BF16)
            dw_ref[...] += _dot_tn(pooled, dmixed)
            dp = _dot_nt(dmixed, w)
            dpool[pl.ds(r0, CHUNK), :] = dp
            zpad[pl.ds(r0, CHUNK), :] = dp / cnt
            return carry

        lax.fori_loop(0, nchunk, chunk, 0)

        def chunk2(i, carry):
            r0 = pl.multiple_of(i * CHUNK, CHUNK)
            zs = zpad[pl.ds(r0, CHUNK + POOL_HALO), :]
            win = _select_window(g, _window_sums(zs, -1))[:CHUNK]
            du_ref[pl.ds(r0, CHUNK), :] = (win - dpool[pl.ds(r0, CHUNK), :]).astype(BF16)
            return carry

        lax.fori_loop(0, nchunk, chunk2, 0)

    col = pl.BlockSpec((T, HEAD), lambda g: (0, g))
    return _pallas(
        body, (proj, dpm, w_grp, scale), name="pool_bwd", grid=(N_POOL_GROUPS,),
        in_specs=[col, col, pl.BlockSpec((None, HEAD, HEAD), lambda g: (g, 0, 0)), pl.BlockSpec((1, HEAD), lambda g: (0, g))],
        out_specs=[col, pl.BlockSpec((None, HEAD, HEAD), lambda g: (g, 0, 0)), pl.BlockSpec((1, HEAD), lambda g: (0, g))],
        out_shape=[jax.ShapeDtypeStruct((T, D_POOL), BF16), jax.ShapeDtypeStruct((N_POOL_GROUPS, HEAD, HEAD), F32),
                   jax.ShapeDtypeStruct((1, D_POOL), F32)],
        scratch_shapes=[pltpu.VMEM((T + POOL_HALO, HEAD), F32), pltpu.VMEM((T + POOL_HALO, HEAD), F32), pltpu.VMEM((T, HEAD), F32)],
        semantics=("parallel",), jobs=jobs)


def _conv_taps(xs, cw):
    v = cw[CONV_WIDTH - 1] * xs[SUBLANES:]
    for k in range(CONV_WIDTH - 1):
        v += cw[k] * _shift_rows(xs, CONV_WIDTH - 1 - k)[SUBLANES:]
    return v


def _tap_rows(cw_ref):
    return [cw_ref[k:k + 1, :] for k in range(CONV_WIDTH)]


def _softplus_neg(lam):
    return jnp.maximum(-lam, 0.0) + _log1p(jnp.exp(-jnp.abs(lam)))


def _lru_gates(v, wa, ba, wx, bx, sp):
    vb = v.astype(BF16)
    ra = _sigmoid(_dot(vb, wa) + ba)
    ix = _sigmoid(_dot(vb, wx) + bx)
    log_a = -LRU_C * ra * sp
    a = jnp.exp(log_a)
    sq = jnp.sqrt(-jnp.tanh(log_a) * (a * a + 1.0))
    return ra, ix, a, sq


def _row_bcast(v, r):
    return jnp.broadcast_to(v[r:r + 1, :], v.shape)


TILE_BLOCK = 128


def _scan_in_tiles(coef, coef_shift, A_out, B, T, direction):
    order = list(range(SUBLANES)) if direction == 1 else list(range(SUBLANES - 1, -1, -1))
    tiles = min(TILE_BLOCK, T // SUBLANES)
    for base in range(0, T, tiles * SUBLANES):
        def rows(r, base=base):
            return pl.ds(base + r, tiles, stride=SUBLANES)

        A, Bv = coef[rows(order[0] + coef_shift), :], B[rows(order[0]), :]
        A_out[rows(order[0]), :] = A
        for r in order[1:]:
            a = coef[rows(r + coef_shift), :]
            Bv = a * Bv + B[rows(r), :]
            A = a * A
            A_out[rows(r), :] = A
            B[rows(r), :] = Bv


TILES_PER_STEP = 8


def _carry_tiles(A_s, B_s, out, ntile, direction):
    out_row = SUBLANES - 1 if direction == 1 else 0

    def step(k, carry):
        for j in range(TILES_PER_STEP):
            t = k * TILES_PER_STEP + j
            r0 = pl.multiple_of((t if direction == 1 else ntile - 1 - t) * SUBLANES, SUBLANES)
            A, B = A_s[pl.ds(r0, SUBLANES), :], B_s[pl.ds(r0, SUBLANES), :]
            out[pl.ds(r0, SUBLANES), :] = A * carry + B
            carry = _row_bcast(A, out_row) * carry + _row_bcast(B, out_row)
        return carry

    lax.fori_loop(0, ntile // TILES_PER_STEP, step, jnp.zeros((SUBLANES, HEAD), F32))


def _rnn_fwd(proj, conv_w, conv_b, w_a, b_a, w_x, b_x, lam, jobs=()):
    T = proj.shape[0]
    nchunk = T // CHUNK
    ntile = T // SUBLANES

    def body(u_ref, ug_ref, cw_ref, cb_ref, wa_ref, ba_ref, wx_ref, bx_ref, lam_ref,
             h_ref, z_ref, v_ref, ra_ref, ix_ref, a_ref, sq_ref, upad, a_s, b_s):
        _pad_front(upad, u_ref, SUBLANES)
        cw, cb = _tap_rows(cw_ref), cb_ref[...]
        wa, wx = wa_ref[...].astype(BF16), wx_ref[...].astype(BF16)
        ba, bx = ba_ref[...], bx_ref[...]
        sp = _softplus_neg(lam_ref[...])

        def chunk(i, carry):
            rows = pl.ds(pl.multiple_of(i * CHUNK, CHUNK), CHUNK)
            v = _conv_taps(upad[pl.ds(pl.multiple_of(i * CHUNK, CHUNK), CHUNK + SUBLANES), :], cw) + cb
            ra, ix, a, sq = _lru_gates(v, wa, ba, wx, bx, sp)
            v_ref[rows, :], ra_ref[rows, :], ix_ref[rows, :], a_ref[rows, :], sq_ref[rows, :] = v, ra, ix, a, sq
            a_s[rows, :], b_s[rows, :] = a, sq * ix * v
            return carry

        lax.fori_loop(0, nchunk, chunk, 0)
        _scan_in_tiles(a_s, 0, a_s, b_s, T, 1)
        _carry_tiles(a_s, b_s, h_ref, ntile, 1)

        def chunk3(i, carry):
            r0 = pl.multiple_of(i * CHUNK, CHUNK)
            gl, _ = _gelu_parts(ug_ref[pl.ds(r0, CHUNK), :])
            z_ref[pl.ds(r0, CHUNK), :] = (h_ref[pl.ds(r0, CHUNK), :] * gl).astype(BF16)
            return carry

        lax.fori_loop(0, nchunk, chunk3, 0)

    col = pl.BlockSpec((T, HEAD), lambda h: (0, h))
    vec = pl.BlockSpec((1, HEAD), lambda h: (0, h))
    mat = pl.BlockSpec((None, HEAD, HEAD), lambda h: (h, 0, 0))
    return _pallas(
        body, (proj, proj, conv_w, conv_b, w_a, b_a, w_x, b_x, lam), name="rnn_fwd", grid=(N_RNN_HEADS,),
        in_specs=[pl.BlockSpec((T, HEAD), lambda h: (0, COL_RNN + h)), pl.BlockSpec((T, HEAD), lambda h: (0, COL_GATE + h)),
                  pl.BlockSpec((CONV_WIDTH, HEAD), lambda h: (0, h)), vec, mat, vec, mat, vec, vec],
        out_specs=[col] * 7,
        out_shape=[jax.ShapeDtypeStruct((T, D_RNN), F32), jax.ShapeDtypeStruct((T, D_RNN), BF16)]
        + [jax.ShapeDtypeStruct((T, D_RNN), F32)] * 5,
        scratch_shapes=[pltpu.VMEM((T + SUBLANES, HEAD), F32), pltpu.VMEM((T, HEAD), F32), pltpu.VMEM((T, HEAD), F32)],
        semantics=("parallel",), jobs=jobs)


def _rnn_bwd(proj, hr, dz, gates, conv_w, w_a, w_x, lam, jobs=()):
    T = proj.shape[0]
    nchunk = T // CHUNK
    ntile = T // SUBLANES

    def body(u_ref, ug_ref, h_ref, dz_ref, v_ref, ra_ref, ix_ref, a_ref, sq_ref, cw_ref, wa_ref, wx_ref, lam_ref,
             du_ref, dug_ref, dwa_ref, dwx_ref, dba_ref, dbx_ref, dlam_ref, dcb_ref, dcw_ref,
             upad, hpad, apad, g_s, dvpad, ga_s):
        zero_tile = jnp.zeros((SUBLANES, HEAD), F32)
        _pad_front(upad, u_ref, SUBLANES)
        _pad_front(hpad, h_ref, SUBLANES)
        apad[pl.ds(T, SUBLANES), :] = zero_tile
        dvpad[pl.ds(T, SUBLANES), :] = zero_tile
        for ref in (dwa_ref, dwx_ref, dba_ref, dbx_ref, dlam_ref, dcb_ref, dcw_ref):
            ref[...] = jnp.zeros_like(ref)
        cw = _tap_rows(cw_ref)
        wa, wx = wa_ref[...].astype(BF16), wx_ref[...].astype(BF16)
        lam_row = lam_ref[...]
        sp = _softplus_neg(lam_row)

        def chunk(i, carry):
            rows = pl.ds(pl.multiple_of(i * CHUNK, CHUNK), CHUNK)
            apad[rows, :] = a_ref[rows, :]
            gl, dgl = _gelu_parts(ug_ref[rows, :])
            d = dz_ref[rows, :]
            g_s[rows, :] = d * gl
            dug_ref[rows, :] = (d * h_ref[rows, :] * dgl).astype(BF16)
            return carry

        lax.fori_loop(0, nchunk, chunk, 0)

        _scan_in_tiles(apad, 1, ga_s, g_s, T, -1)
        _carry_tiles(ga_s, g_s, g_s, ntile, -1)

        def chunk3(i, carry):
            r0 = pl.multiple_of(i * CHUNK, CHUNK)
            rows = pl.ds(r0, CHUNK)
            g = g_s[rows, :]
            h_prev = _shift_rows(hpad[pl.ds(r0, CHUNK + SUBLANES), :], 1)[SUBLANES:]
            v, ra, ix, sq, a = v_ref[rows, :], ra_ref[rows, :], ix_ref[rows, :], sq_ref[rows, :], a_ref[rows, :]
            d_sq = g * ix * v
            d_ix = g * sq * v
            d_la = a * g * h_prev - d_sq * a * a / sq
            dlam_ref[...] += jnp.sum(d_la * ra, axis=0, keepdims=True)
            d_pa = d_la * (-LRU_C) * sp * ra * (1.0 - ra)
            d_px = d_ix * ix * (1.0 - ix)
            vb, d_pab, d_pxb = v.astype(BF16), d_pa.astype(BF16), d_px.astype(BF16)
            dwa_ref[...] += _dot_tn(vb, d_pab)
            dwx_ref[...] += _dot_tn(vb, d_pxb)
            dba_ref[...] += jnp.sum(d_pa, axis=0, keepdims=True)
            dbx_ref[...] += jnp.sum(d_px, axis=0, keepdims=True)
            dv = g * sq * ix + _dot_nt(d_pab, wa) + _dot_nt(d_pxb, wx)
            dvpad[rows, :] = dv
            dcb_ref[...] += jnp.sum(dv, axis=0, keepdims=True)
            xs = upad[pl.ds(r0, CHUNK + SUBLANES), :]
            for k in range(CONV_WIDTH):
                u_k = _shift_rows(xs, CONV_WIDTH - 1 - k)[SUBLANES:] if k < CONV_WIDTH - 1 else xs[SUBLANES:]
                dcw_ref[k:k + 1, :] += jnp.sum(dv * u_k, axis=0, keepdims=True)
            return carry

        lax.fori_loop(0, nchunk, chunk3, 0)
        dlam_ref[...] = dlam_ref[...] * (LRU_C * _sigmoid(-lam_row))

        def chunk4(i, carry):
            r0 = pl.multiple_of(i * CHUNK, CHUNK)
            dvs = dvpad[pl.ds(r0, CHUNK + SUBLANES), :]
            du = cw[CONV_WIDTH - 1] * dvs[:CHUNK]
            for k in range(CONV_WIDTH - 1):
                du += cw[k] * _shift_rows(dvs, -(CONV_WIDTH - 1 - k))[:CHUNK]
            du_ref[pl.ds(r0, CHUNK), :] = du.astype(BF16)
            return carry

        lax.fori_loop(0, nchunk, chunk4, 0)

    col = pl.BlockSpec((T, HEAD), lambda h: (0, h))
    vec = pl.BlockSpec((1, HEAD), lambda h: (0, h))
    mat = pl.BlockSpec((None, HEAD, HEAD), lambda h: (h, 0, 0))
    taps = pl.BlockSpec((CONV_WIDTH, HEAD), lambda h: (0, h))
    vec_out = jax.ShapeDtypeStruct((1, D_RNN), F32)
    mat_out = jax.ShapeDtypeStruct((N_RNN_HEADS, HEAD, HEAD), F32)
    seq = pltpu.VMEM((T, HEAD), F32)
    seq_pad = pltpu.VMEM((T + SUBLANES, HEAD), F32)
    return _pallas(
        body, (proj, proj, hr, dz, *gates, conv_w, w_a, w_x, lam), name="rnn_bwd", grid=(N_RNN_HEADS,),
        in_specs=[pl.BlockSpec((T, HEAD), lambda h: (0, COL_RNN + h)), pl.BlockSpec((T, HEAD), lambda h: (0, COL_GATE + h))]
        + [col] * 7 + [taps, mat, mat, vec],
        out_specs=[col, col, mat, mat, vec, vec, vec, vec, taps],
        out_shape=[jax.ShapeDtypeStruct((T, D_RNN), BF16), jax.ShapeDtypeStruct((T, D_RNN), BF16), mat_out, mat_out,
                   vec_out, vec_out, vec_out, vec_out, jax.ShapeDtypeStruct((CONV_WIDTH, D_RNN), F32)],
        scratch_shapes=[seq_pad, seq_pad, seq_pad, seq, seq_pad, seq],
        semantics=("parallel",), jobs=jobs)


GROUP_FFN_OUT = ["w_ffn_out"]
GROUP_FFN_IN = ["w_ffn_in"]
GROUP_MIX = ["w_o", "w_pool_out", "w_rnn_out"]
GROUP_IN = ["w_in"]


def _step(x, target, s, full, conv_w_mine, place):
    T = x.shape[0]
    tall, mid, low = min(T, 2048), min(T, 1024), min(T, 512)
    c1 = place[1:]
    full = dict(full)

    def gathered(names, results):
        full.update(zip(names, results))

    (full["w_in"], conv_w), = _run_jobs([_gather_job(full, ["w_in"], conv_w_mine)], "gather_w_in")
    early = ["w_pool_out", "w_rnn_out", "w_o", "w_ffn_out"]
    (proj, h1), (res,) = _norm_matmul(x, s["norm_mix"], full["w_in"], tm=tall, tn=512, name="in_proj", jobs=[_gather_job(full, early)])
    gathered(early, res)
    pm = _pool_fwd(proj, s["w_pool_grp"], s["pool_scale"])
    (hr, z, *gates), (res,) = _rnn_fwd(proj, conv_w, s["conv_b"], s["w_rg_a"], s["b_rg_a"], s["w_rg_x"], s["b_rg_x"],
                                       s["lru_lambda"], jobs=[_gather_job(full, ["w_ffn_in"])])
    gathered(["w_ffn_in"], res)
    *mix_by, mix = _branch_mix(pm, z, full["w_pool_out"], full["w_rnn_out"], proj, tm=tall, tn=256)
    x2 = _out_proj_residual(mix, full["w_o"], x, tm=mid)
    act_by_up, act_by_gate, act, h2 = _ffn_in(x2, s["norm_ffn"], full["w_ffn_in"], tm=tall, tn=256)
    dx3, dx3b, sq_cols, g_norm_final = _ffn_out_loss(act, full["w_ffn_out"], x2, s["norm_final"], target, tm=low)

    g = {"norm_final": g_norm_final}

    def chip_sums(names, from_sibling):
        sums = {name: _chip_sum(name, g[name], got, c1) for name, got in zip(names, from_sibling)}
        return {name: v[0] for name, v in sums.items()}, {name: v[1] for name, v in sums.items()}

    def final_sums(names, sums, from_chips):
        return {name: _final_sum(name, sums[name], got, place) for name, got in zip(names, from_chips)}

    dgate, dup = _ffn_out_bwd(dx3b, full["w_ffn_out"], act_by_gate, act_by_up, tm=tall, tn=256)
    g["w_ffn_out"] = _weight_grad(act, [dx3b], tm=256, tn=D_MODEL, name="w_ffn_out_grad")
    (dx2, dx2b, g["norm_ffn"]), (res,) = _ffn_in_bwd(dgate, dup, full["w_ffn_in"], dx3, x2, s["norm_ffn"], tm=low,
                                                     jobs=[_sibling_job(g, GROUP_FFN_OUT)])
    sums_ffn, sums_ffn_bf16 = chip_sums(GROUP_FFN_OUT, res)
    g["w_ffn_in"], (res,) = _weight_grad(h2, [dgate, dup], tm=D_MODEL, tn=256, name="w_ffn_in_grad",
                                         jobs=[_chips_job(sums_ffn_bf16, GROUP_FFN_OUT)])
    shards_ffn = final_sums(GROUP_FFN_OUT, sums_ffn, res)
    (dgp, dgr, dyp, dyr), (res,) = _out_proj_bwd(dx2b, full["w_o"], mix_by, tm=tall, tn=256,
                                                 jobs=[_sibling_job(g, GROUP_FFN_IN)])
    sums_ffn, sums_ffn_bf16 = chip_sums(GROUP_FFN_IN, res)
    g["w_o"] = _weight_grad(mix, [dx2b], tm=D_MODEL, tn=256, name="w_o_grad")
    dpm, dz = _branch_bwd(dyp, dyr, full["w_pool_out"], full["w_rnn_out"], tm=mid)
    g["w_pool_out"] = _weight_grad(pm, [dyp], tm=D_POOL, tn=256, name="w_pool_out_grad")
    g["w_rnn_out"] = _weight_grad(z, [dyr], tm=D_RNN, tn=256, name="w_rnn_out_grad")
    (dupool, g["w_pool_grp"], g["pool_scale"]), (res,) = _pool_bwd(proj, dpm, s["w_pool_grp"], s["pool_scale"],
                                                                   jobs=[_sibling_job(g, GROUP_MIX)])
    sums_mix, sums_mix_bf16 = chip_sums(GROUP_MIX, res)
    ((durnn, dugate, g["w_rg_a"], g["w_rg_x"], g["b_rg_a"], g["b_rg_x"], g["lru_lambda"], g["conv_b"], g["conv_w"]),
     (res,)) = _rnn_bwd(proj, hr, dz, gates, conv_w, s["w_rg_a"], s["w_rg_x"], s["lru_lambda"],
                        jobs=[_chips_job(sums_ffn_bf16, GROUP_FFN_IN)])
    shards_ffn.update(final_sums(GROUP_FFN_IN, sums_ffn, res))
    segs = [dupool, durnn, dugate, dgp, dgr]
    ffn = GROUP_FFN_OUT + GROUP_FFN_IN
    g["w_in"], (res, joined) = _weight_grad(h1, segs, tm=D_MODEL, tn=256, name="w_in_grad",
                                           jobs=[_chips_job(sums_mix_bf16, GROUP_MIX), _join_job(shards_ffn, ffn)])
    grads = dict(zip(ffn, joined))
    shards = final_sums(GROUP_MIX, sums_mix, res)
    (res,) = _run_jobs([_sibling_job(g, GROUP_IN)], "w_in_exchange_sibling")
    sums_in, sums_in_bf16 = chip_sums(GROUP_IN, res)
    (grad_x, g["norm_mix"]), (res,) = _in_proj_bwd(segs, full["w_in"], dx2, x, s["norm_mix"], tm=low,
                                                  jobs=[_chips_job(sums_in_bf16, GROUP_IN)])
    shards.update(final_sums(GROUP_IN, sums_in, res))

    vec_rows = [g[name] if name != "pool_scale" else jnp.pad(g[name], ((0, 0), (0, D_MODEL - D_POOL))) for name in VEC_ITEMS]
    vec_rows += [g["conv_w"], sq_cols, jnp.zeros((VEC_ROWS - len(VEC_ITEMS) - CONV_WIDTH - 1, D_MODEL), F32)]
    vec = jnp.concatenate(vec_rows, axis=0).reshape(VEC_ROWS, N_DEV, HEAD).transpose(1, 0, 2)
    mat = jnp.concatenate([g[name].reshape(-1, HEAD) for name in MAT_ITEMS], axis=0).reshape(N_DEV, -1, HEAD)
    (vec, mat), (joined,) = _all_reduce_small([vec, mat], jobs=[_join_job(shards, GROUP_MIX + GROUP_IN)])
    grads.update(zip(GROUP_MIX + GROUP_IN, joined))

    vec = vec.transpose(1, 0, 2).reshape(VEC_ROWS, D_MODEL)
    mat = mat.reshape(-1, HEAD)
    for k, name in enumerate(VEC_ITEMS):
        grads[name] = vec[k:k + 1, :s[name].shape[1]]
    grads["conv_w"] = vec[len(VEC_ITEMS):len(VEC_ITEMS) + CONV_WIDTH]
    row = 0
    for name in MAT_ITEMS:
        rows = s[name].shape[0] * HEAD
        grads[name] = mat[row:row + rows]
        row += rows
    return vec[len(VEC_ITEMS) + CONV_WIDTH], grad_x, grads


LARGE = {"w_in": "col", "w_pool_out": "col", "w_rnn_out": "row", "w_o": "row", "w_ffn_in": "col", "w_ffn_out": "row"}
LARGE_SHAPE = {"w_in": (D_MODEL, D_IN), "w_pool_out": (D_POOL, D_MODEL), "w_rnn_out": (D_RNN, D_MODEL),
               "w_o": (D_MODEL, D_MODEL), "w_ffn_in": (D_MODEL, 2 * D_FF), "w_ffn_out": (D_FF, D_MODEL)}


def _place():
    x, y, c = lax.axis_index("x"), lax.axis_index("y"), lax.axis_index("c")
    return 2 * x + y, c


def _chip_device(chip, c):
    return (chip // 2, chip % 2, c)


def _chip_window(ref, kind, shape, chip, half=None):
    K, N = shape
    if kind == "col":
        rows = slice(None) if half is None else pl.ds(half * (K // 2), K // 2)
        return ref.at[rows, pl.ds(chip * (N // N_CHIPS), N // N_CHIPS)]
    ks = K // N_CHIPS
    if half is None:
        return ref.at[pl.ds(chip * ks, ks), :]
    return ref.at[pl.ds(chip * ks + half * (ks // 2), ks // 2), :]


def _row_half(ref, half):
    rows = ref.shape[0] // 2
    return ref.at[pl.ds(half * rows, rows), :]


def _remote(win_src, win_dst, send_sems, recv_sems, idx, to):
    return pltpu.make_async_remote_copy(src_ref=win_src, dst_ref=win_dst, send_sem=send_sems.at[idx], recv_sem=recv_sems.at[idx],
                                        device_id=to, device_id_type=MESH)


def _gather_job(full, names, conv_w_full=None):
    n = len(names)
    cw_cols = D_RNN // N_CHIPS

    def windows(refs, chip, half):
        return [_chip_window(refs[k], LARGE[name], LARGE_SHAPE[name], chip, half) for k, name in enumerate(names)]

    def ici_copies(refs, send_sems, recv_sems, src_chip, dst_chip, c, r):
        wins = windows(refs, src_chip, c)
        if conv_w_full is not None:
            wins.append(refs[n].at[:, pl.ds(src_chip * cw_cols, cw_cols)])
        return [_remote(win, win, send_sems, recv_sems, (k, r), _chip_device(dst_chip, c)) for k, win in enumerate(wins)]

    def forwards(refs, send_sems, recv_sems, src_chip, half, to_core, chip, r):
        return [_remote(win, win, send_sems, recv_sems, (k, 3 + r), _chip_device(chip, to_core))
                for k, win in enumerate(windows(refs, src_chip, half))]

    def start(ins, outs, send_sems, recv_sems):
        chip, c = _place()
        for r in range(3):
            for cp in ici_copies(outs, send_sems, recv_sems, chip, chip ^ (r + 1), c, r):
                cp.start()

    def finish(ins, outs, send_sems, recv_sems):
        chip, c = _place()
        for r in range(3):
            for cp in ici_copies(outs, send_sems, recv_sems, chip ^ (r + 1), chip, c, r):
                cp.wait_recv()
            for cp in forwards(outs, send_sems, recv_sems, chip ^ (r + 1), c, 1 - c, chip, r):
                cp.start()
        for r in range(3):
            for cp in forwards(outs, send_sems, recv_sems, chip ^ (r + 1), 1 - c, c, chip, r):
                cp.wait_recv()
            for cp in ici_copies(outs, send_sems, recv_sems, chip, chip ^ (r + 1), c, r):
                cp.wait_send()
            for cp in forwards(outs, send_sems, recv_sems, chip ^ (r + 1), c, 1 - c, chip, r):
                cp.wait_send()

    arrays = [full[name] for name in names] + ([conv_w_full] if conv_w_full is not None else [])
    return _Job(arrays, [jax.ShapeDtypeStruct(a.shape, a.dtype) for a in arrays], {k: k for k in range(len(arrays))},
                (len(arrays), 6), start, finish)


def _core_halves(ref, kind, shape, c):
    return [_chip_window(ref, kind, shape, chip, c) for chip in range(N_CHIPS)]


def _sibling_job(grads, names):
    def start(ins, outs, send_sems, recv_sems):
        chip, c = _place()
        for k, name in enumerate(names):
            kind, shape = LARGE[name], LARGE_SHAPE[name]
            if kind == "col":
                pairs = [(_row_half(ins[k], 1 - c), outs[k])]
            else:
                rows = shape[0] // N_DEV
                pairs = [(win, outs[k].at[pl.ds(j * rows, rows), :]) for j, win in enumerate(_core_halves(ins[k], kind, shape, 1 - c))]
            for src, dst in pairs:
                _remote(src, dst, send_sems, recv_sems, k, _chip_device(chip, 1 - c)).start()

    def finish(ins, outs, send_sems, recv_sems):
        chip, c = _place()
        for k in range(len(names)):
            _remote(outs[k], outs[k], send_sems, recv_sems, k, _chip_device(chip, 1 - c)).wait()

    return _Job([grads[name] for name in names],
                [jax.ShapeDtypeStruct((LARGE_SHAPE[name][0] // 2, LARGE_SHAPE[name][1]), F32) for name in names], {},
                (len(names),), start, finish)


def _chip_sum(name, g, got, c):
    kind, (K, N) = LARGE[name], LARGE_SHAPE[name]
    rows = K // N_DEV

    def body(c_ref, g_ref, got_ref, o_ref, ob_ref):
        total = g_ref[...] + got_ref[...]
        o_ref[...] = total
        ob_ref[...] = total.astype(BF16)

    if kind == "col":
        mine = pl.BlockSpec((rows, N), lambda j, c_ref: (j + N_CHIPS * c_ref[0], 0))
    else:
        mine = pl.BlockSpec((rows, N), lambda j, c_ref: (2 * j + c_ref[0], 0))
    blk = pl.BlockSpec((rows, N), lambda j, c_ref: (j, 0))
    return pl.pallas_call(
        body, name=name + "_chip_sum",
        grid_spec=pltpu.PrefetchScalarGridSpec(num_scalar_prefetch=1, grid=(N_CHIPS,), in_specs=[mine, blk], out_specs=[blk, blk]),
        out_shape=[jax.ShapeDtypeStruct((K // 2, N), F32), jax.ShapeDtypeStruct((K // 2, N), BF16)],
        compiler_params=_params(dimension_semantics=("parallel",)),
    )(c, g, got)


def _piece(ref, kind, shape, chip):
    K, N = shape
    if kind == "col":
        return ref.at[:, pl.ds(chip * (N // N_CHIPS), N // N_CHIPS)]
    return ref.at[pl.ds(chip * (K // N_DEV), K // N_DEV), :]


def _piece_shape(name):
    kind, (K, N) = LARGE[name], LARGE_SHAPE[name]
    return (K // 2, N // N_CHIPS) if kind == "col" else (K // N_DEV, N)


def _chips_job(sums, names):
    def copies(ins, outs, send_sems, recv_sems):
        chip, c = _place()
        return [_remote(_piece(ins[k], LARGE[name], LARGE_SHAPE[name], chip ^ (r + 1)), outs[k].at[r], send_sems, recv_sems, (k, r),
                        _chip_device(chip ^ (r + 1), c)) for k, name in enumerate(names) for r in range(3)]

    def start(*refs):
        for cp in copies(*refs):
            cp.start()

    def finish(*refs):
        for cp in copies(*refs):
            cp.wait()

    return _Job([sums[name] for name in names], [jax.ShapeDtypeStruct((3,) + _piece_shape(name), BF16) for name in names], {},
                (len(names), 3), start, finish)


def _final_sum(name, chip_sum, got, place):
    kind = LARGE[name]
    rows, cols = _piece_shape(name)

    def body(place_ref, s_ref, got_ref, o_ref):
        o_ref[...] = ((s_ref[...] + got_ref[0].astype(F32)) + got_ref[1].astype(F32)) + got_ref[2].astype(F32)

    if kind == "col":
        mine = pl.BlockSpec((rows, cols), lambda i, place_ref: (0, place_ref[0]))
    else:
        mine = pl.BlockSpec((rows, cols), lambda i, place_ref: (place_ref[0], 0))
    return pl.pallas_call(
        body, name=name + "_final_sum",
        grid_spec=pltpu.PrefetchScalarGridSpec(
            num_scalar_prefetch=1, grid=(1,), in_specs=[mine, pl.BlockSpec((3, rows, cols), lambda i, place_ref: (0, 0, 0))],
            out_specs=pl.BlockSpec((rows, cols), lambda i, place_ref: (place_ref[1], 0))),
        out_shape=jax.ShapeDtypeStruct((2 * rows, cols), F32),
        compiler_params=_params(dimension_semantics=("arbitrary",)),
    )(place, chip_sum, got)


def _join_job(shards, names):
    def half_copy(outs, send_sems, recv_sems, k, mine):
        chip, c = _place()
        win = _row_half(outs[k], c if mine else 1 - c)
        return _remote(win, win, send_sems, recv_sems, k, _chip_device(chip, 1 - c))

    def start(ins, outs, send_sems, recv_sems):
        for k in range(len(names)):
            half_copy(outs, send_sems, recv_sems, k, True).start()

    def finish(ins, outs, send_sems, recv_sems):
        for k in range(len(names)):
            half_copy(outs, send_sems, recv_sems, k, True).wait_send()
            half_copy(outs, send_sems, recv_sems, k, False).wait_recv()

    arrays = [shards[name] for name in names]
    return _Job(arrays, [jax.ShapeDtypeStruct(a.shape, F32) for a in arrays], {k: k for k in range(len(arrays))},
                (len(arrays),), start, finish)


VEC_ROWS = 16


def _all_reduce_small(slabs, jobs=()):
    n = len(slabs)

    def body(*refs):
        in_refs, out_refs, got_refs = refs[:n], refs[n:2 * n], refs[2 * n:3 * n]
        send_sems, recv_sems = refs[3 * n:]
        x, y, c = lax.axis_index("x"), lax.axis_index("y"), lax.axis_index("c")
        me = 4 * x + 2 * y + c

        def remote(src, dst, k, phase, r):
            other = me ^ r
            return pltpu.make_async_remote_copy(src_ref=src, dst_ref=dst, send_sem=send_sems.at[k, phase, r],
                                                recv_sem=recv_sems.at[k, phase, r],
                                                device_id=(other // 4, (other // 2) % 2, other % 2), device_id_type=MESH)

        scatter = [remote(in_refs[k].at[me ^ r], got_refs[k].at[r], k, 0, r) for r in range(1, N_DEV) for k in range(n)]
        for cp in scatter:
            cp.start()
        for cp in scatter:
            cp.wait()
        for k in range(n):
            total = in_refs[k][me]
            for r in range(1, N_DEV):
                total = total + got_refs[k][r]
            out_refs[k][me] = total
        gather = [remote(out_refs[k].at[me], out_refs[k].at[me], k, 1, r) for r in range(1, N_DEV) for k in range(n)]
        for cp in gather:
            cp.start()
        for r in range(1, N_DEV):
            for k in range(n):
                remote(out_refs[k].at[me ^ r], out_refs[k].at[me ^ r], k, 1, r).wait_recv()
        for cp in gather:
            cp.wait_send()

    return _pallas(
        body, slabs, name="all_reduce_small", grid=(), in_specs=[VMEM] * n, out_specs=[VMEM] * n,
        out_shape=[jax.ShapeDtypeStruct(s.shape, F32) for s in slabs],
        scratch_shapes=[pltpu.VMEM(s.shape, F32) for s in slabs]
        + [pltpu.SemaphoreType.DMA((n, 2, N_DEV)), pltpu.SemaphoreType.DMA((n, 2, N_DEV))], jobs=jobs)


def _cast_into_whole(w, name, place):
    rows, cols = w.shape
    tr = rows // 2

    def body(place_ref, w_ref, o_ref):
        o_ref[...] = w_ref[...].astype(BF16)

    if LARGE[name] == "col":
        window = pl.BlockSpec((tr, cols), lambda i, place_ref: (i, place_ref[0]))
    else:
        window = pl.BlockSpec((tr, cols), lambda i, place_ref: (2 * place_ref[0] + i, 0))
    return pl.pallas_call(
        body, name=name + "_cast",
        grid_spec=pltpu.PrefetchScalarGridSpec(num_scalar_prefetch=1, grid=(2,),
                                               in_specs=[pl.BlockSpec((tr, cols), lambda i, place_ref: (i, 0))], out_specs=window),
        out_shape=jax.ShapeDtypeStruct(LARGE_SHAPE[name], BF16),
        compiler_params=_params(dimension_semantics=("parallel",)))(place, w)


def _adamw_math(w, g, m, v):
    m = ADAM_B1 * m + (1.0 - ADAM_B1) * g
    v = ADAM_B2 * v + (1.0 - ADAM_B2) * (g * g)
    m_hat = m / (1.0 - ADAM_B1 ** ADAM_STEP)
    v_hat = v / (1.0 - ADAM_B2 ** ADAM_STEP)
    delta = -ADAM_LR * (m_hat / (jnp.sqrt(v_hat) + ADAM_EPS) + ADAM_WD * w)
    return delta, m, v


def _adamw_large(w, g, m, v, name):
    rows, cols = w.shape
    tr = rows // 4

    def body(w_ref, g_ref, m_ref, v_ref, d_ref, mo_ref, vo_ref):
        d_ref[...], mo_ref[...], vo_ref[...] = _adamw_math(w_ref[...], g_ref[...], m_ref[...], v_ref[...])

    blk = pl.BlockSpec((tr, cols), lambda i: (i, 0))
    out = jax.ShapeDtypeStruct(w.shape, F32)
    return pl.pallas_call(body, name=name + "_adamw", grid=(4,), in_specs=[blk] * 4, out_specs=[blk] * 3, out_shape=[out] * 3,
                          compiler_params=_params(dimension_semantics=("parallel",)))(w, g, m, v)


def _adamw_small(ws, gs, ms, vs):
    n = len(ws)

    def body(*refs):
        for k in range(n):
            w_ref, g_ref, m_ref, v_ref = (refs[q * n + k] for q in range(4))
            d_ref, mo_ref, vo_ref = (refs[(4 + q) * n + k] for q in range(3))
            d_ref[...], mo_ref[...], vo_ref[...] = _adamw_math(w_ref[...], g_ref[...], m_ref[...], v_ref[...])

    out = [jax.ShapeDtypeStruct(w.shape, F32) for w in ws]
    res = pl.pallas_call(body, name="small_adamw", in_specs=[VMEM] * (4 * n), out_specs=[VMEM] * (3 * n), out_shape=out * 3,
                         compiler_params=_params())(*ws, *gs, *ms, *vs)
    return res[:n], res[n:2 * n], res[2 * n:]


WEIGHTS = ["norm_mix", "w_in", "w_pool_grp", "pool_scale", "w_pool_out", "conv_w", "conv_b", "w_rg_a", "b_rg_a", "w_rg_x",
           "b_rg_x", "lru_lambda", "w_rnn_out", "w_o", "norm_ffn", "w_ffn_in", "w_ffn_out", "norm_final"]
VEC_ITEMS = ["norm_mix", "norm_ffn", "norm_final", "pool_scale", "conv_b", "lru_lambda", "b_rg_a", "b_rg_x"]
MAT_ITEMS = ["w_pool_grp", "w_rg_a", "w_rg_x"]


def _as2d(name, a):
    if name in MAT_ITEMS:
        return a.reshape(-1, HEAD, HEAD)
    if name == "conv_w":
        return a.reshape(CONV_WIDTH, -1)
    return a.reshape(1, -1)


def kernel(x, norm_mix, w_in, w_pool_grp, pool_scale, w_pool_out, conv_w, conv_b, w_rg_a, b_rg_a, w_rg_x, b_rg_x, lru_lambda, w_rnn_out, w_o, norm_ffn, w_ffn_in, w_ffn_out, norm_final, loss_target, m_norm_mix, m_w_in, m_w_pool_grp, m_pool_scale, m_w_pool_out, m_conv_w, m_conv_b, m_w_rg_a, m_b_rg_a, m_w_rg_x, m_b_rg_x, m_lru_lambda, m_w_rnn_out, m_w_o, m_norm_ffn, m_w_ffn_in, m_w_ffn_out, m_norm_final, v_norm_mix, v_w_in, v_w_pool_grp, v_pool_scale, v_w_pool_out, v_conv_w, v_conv_b, v_w_rg_a, v_b_rg_a, v_w_rg_x, v_b_rg_x, v_lru_lambda, v_w_rnn_out, v_w_o, v_norm_ffn, v_w_ffn_in, v_w_ffn_out, v_norm_final):
    given = dict(locals())
    w = {name: given[name] for name in WEIGHTS}
    m = {name: given["m_" + name] for name in WEIGHTS}
    v = {name: given["v_" + name] for name in WEIGHTS}
    chip, c = _place()

    place = jnp.stack([chip, c]).astype(jnp.int32)
    conv_cols = w["conv_w"].shape[-1]
    conv_w_mine = lax.dynamic_update_slice_in_dim(jnp.zeros((CONV_WIDTH, D_RNN), F32), w["conv_w"][0], chip * conv_cols, axis=1)
    full = {name: _cast_into_whole(w[name][0], name, place) for name in LARGE}
    small = {name: _as2d(name, w[name]) for name in WEIGHTS if name not in LARGE and name != "conv_w"}
    sq_cols, grad_x, grads = _step(x[0], loss_target[0], small, full, conv_w_mine, place)
    loss = 0.5 / D_MODEL * jnp.sum(sq_cols)
    grads["conv_w"] = lax.dynamic_slice_in_dim(grads["conv_w"], chip * conv_cols, conv_cols, axis=1)

    delta, new_m, new_v = {}, {}, {}
    for name in LARGE:
        delta[name], new_m[name], new_v[name] = _adamw_large(w[name][0], grads[name], m[name][0], v[name][0], name)
    small_names = [name for name in WEIGHTS if name not in LARGE]
    flat = lambda d: [d[name].reshape(grads[name].shape) for name in small_names]
    ds, mo, vo = _adamw_small(flat(w), [grads[name] for name in small_names], flat(m), flat(v))
    for k, name in enumerate(small_names):
        delta[name], new_m[name], new_v[name] = ds[k], mo[k], vo[k]

    shaped = lambda d: [d[name].reshape(w[name].shape) for name in WEIGHTS]
    return (loss, grad_x[None], *shaped(grads), *shaped(delta), *shaped(new_m), *shaped(new_v))
```

```python
import functools
import math

import jax
import jax.numpy as jnp
from jax import lax
from jax.experimental import pallas as pl
from jax.experimental.pallas import tpu as pltpu

F32 = jnp.float32
BF16 = jnp.bfloat16

D_MODEL = 1024
D_POOL = 512
N_POOL_GROUPS = 4
D_RNN = 1024
N_RNN_HEADS = 8
HEAD = 128
CONV_WIDTH = 4
LRU_C = 8.0
D_FF = 2816
D_IN = D_POOL + 2 * D_RNN + 2 * D_MODEL
NORM_EPS = 1e-6
COL_RNN = D_POOL // HEAD
COL_GATE = (D_POOL + D_RNN) // HEAD

ADAM_LR = 0.001
ADAM_B1 = 0.9
ADAM_B2 = 0.999
ADAM_EPS = 1e-08
ADAM_WD = 0.01
ADAM_STEP = 10

N_CHIPS = 4
N_DEV = 8
MESH = pl.DeviceIdType.MESH
ANY = pl.BlockSpec(memory_space=pl.ANY)
VMEM = pl.BlockSpec(memory_space=pltpu.VMEM)
VMEM_LIMIT_BYTES = 60 * 1024 * 1024
SUBLANES = 8
POOL_HALO = 16
CHUNK = 1024

GELU_C = math.sqrt(2.0 / math.pi)
GELU_A = 0.044715


def _params(**kw):
    return pltpu.CompilerParams(vmem_limit_bytes=VMEM_LIMIT_BYTES, **kw)


def _sigmoid(x):
    return 0.5 * jnp.tanh(0.5 * x) + 0.5


def _log1p(y):
    u = 1.0 + y
    d = u - 1.0
    return jnp.where(d == 0.0, y, jnp.log(u) * (y / jnp.where(d == 0.0, 1.0, d)))


def _gelu_parts(x):
    x2 = x * x
    th = jnp.tanh(GELU_C * (x + GELU_A * x * x2))
    g = 0.5 * x * (1.0 + th)
    dg = 0.5 * (1.0 + th) + 0.5 * x * (1.0 - th * th) * GELU_C * (1.0 + 3.0 * GELU_A * x2)
    return g, dg


def _dot(a, b):
    return jnp.dot(a, b, preferred_element_type=F32)


def _dot_nt(a, b):
    return lax.dot_general(a, b, (((1,), (1,)), ((), ())), preferred_element_type=F32)


def _dot_tn(a, b):
    return lax.dot_general(a, b, (((0,), (0,)), ((), ())), preferred_element_type=F32)


def _rms_scale(xv):
    return lax.rsqrt(jnp.mean(xv * xv, axis=-1, keepdims=True) + NORM_EPS)


def _rms_bwd(dy, xv, g):
    r = _rms_scale(xv)
    xh = xv * r
    dyg = dy * g
    dx = r * (dyg - xh * jnp.mean(dyg * xh, axis=-1, keepdims=True))
    return dx, dy * xh


class _Job:
    def __init__(self, inputs, out_shapes, aliases, sem_shape, start, finish):
        self.inputs, self.out_shapes, self.aliases, self.sem_shape = list(inputs), list(out_shapes), dict(aliases), sem_shape
        self.start, self.finish = start, finish


def _pallas(body, operands, *, name, grid, in_specs, out_specs, out_shape, scratch_shapes=(), semantics=None, jobs=()):
    n_in, n_out, n_scr = len(in_specs), len(out_specs), len(scratch_shapes)
    job_in = [a for job in jobs for a in job.inputs]
    job_out = [s for job in jobs for s in job.out_shapes]
    aliases, i0, o0 = {}, n_in, n_out
    for job in jobs:
        aliases.update({i0 + i: o0 + o for i, o in job.aliases.items()})
        i0, o0 = i0 + len(job.inputs), o0 + len(job.out_shapes)

    def whole(*refs):
        ins, j_ins = refs[:n_in], refs[n_in:n_in + len(job_in)]
        outs = refs[n_in + len(job_in):][:n_out]
        j_outs = refs[n_in + len(job_in) + n_out:][:len(job_out)]
        rest = refs[n_in + len(job_in) + n_out + len(job_out):]
        scr, sems = rest[:n_scr], rest[n_scr:]

        def run(phase):
            i, o = 0, 0
            for k, job in enumerate(jobs):
                getattr(job, phase)(j_ins[i:i + len(job.inputs)], j_outs[o:o + len(job.out_shapes)], sems[2 * k], sems[2 * k + 1])
                i, o = i + len(job.inputs), o + len(job.out_shapes)

        def at(step_of, phase):
            if not jobs:
                return
            if not grid:
                run(phase)
                return
            cond = functools.reduce(jnp.logical_and, [pl.program_id(d) == step_of(d) for d in range(len(grid))])
            pl.when(cond)(functools.partial(run, phase))

        at(lambda d: 0, "start")
        body(*ins, *outs, *scr)
        at(lambda d: grid[d] - 1, "finish")

    res = pl.pallas_call(
        whole, name=name, grid=grid, in_specs=list(in_specs) + [ANY] * len(job_in), out_specs=list(out_specs) + [ANY] * len(job_out),
        out_shape=list(out_shape) + job_out, input_output_aliases=aliases,
        scratch_shapes=list(scratch_shapes) + [pltpu.SemaphoreType.DMA(job.sem_shape) for job in jobs for _ in range(2)],
        compiler_params=_params(dimension_semantics=semantics, has_side_effects=bool(jobs)),
    )(*operands, *job_in)
    per_job, o = [], n_out
    for job in jobs:
        per_job.append(res[o:o + len(job.out_shapes)])
        o += len(job.out_shapes)
    return res[:n_out], per_job


def _run_jobs(jobs, name):
    return _pallas(lambda: None, [], name=name, grid=(), in_specs=[], out_specs=[], out_shape=[], jobs=jobs)[1]


NORM_ROWS = 256


def _norm_rows(x_ref, g_ref, h_ref):
    g = g_ref[...]

    def rows(i, carry):
        r = pl.ds(pl.multiple_of(i * NORM_ROWS, NORM_ROWS), NORM_ROWS)
        xv = x_ref[r, :]
        h_ref[r, :] = (xv * _rms_scale(xv) * g).astype(BF16)
        return carry

    lax.fori_loop(0, x_ref.shape[0] // NORM_ROWS, rows, 0)


def _norm_matmul(x, g, w, *, tm, tn, name, jobs=()):
    T, K = x.shape
    N = w.shape[1]

    def body(x_ref, g_ref, w_ref, o_ref, h_ref):
        @pl.when(pl.program_id(1) == 0)
        def _():
            _norm_rows(x_ref, g_ref, h_ref)

        o_ref[...] = _dot(h_ref[...], w_ref[...])

    return _pallas(
        body, (x, g, w), name=name, grid=(T // tm, N // tn),
        in_specs=[pl.BlockSpec((tm, K), lambda i, j: (i, 0)), pl.BlockSpec((1, K), lambda i, j: (0, 0)),
                  pl.BlockSpec((K, tn), lambda i, j: (0, j))],
        out_specs=[pl.BlockSpec((tm, tn), lambda i, j: (i, j)), pl.BlockSpec((tm, K), lambda i, j: (i, 0))],
        out_shape=[jax.ShapeDtypeStruct((T, N), F32), jax.ShapeDtypeStruct((T, K), BF16)],
        semantics=("parallel", "arbitrary"), jobs=jobs)


def _ffn_in(x2, g, w, *, tm, tn):
    T, K = x2.shape
    nb = D_FF // tn

    def body(x_ref, g_ref, wg_ref, wu_ref, dup_ref, dgate_ref, act_ref, h_ref):
        @pl.when(pl.program_id(1) == 0)
        def _():
            _norm_rows(x_ref, g_ref, h_ref)

        h = h_ref[...]
        gate = _dot(h, wg_ref[...])
        up = _dot(h, wu_ref[...])
        s = _sigmoid(gate)
        silu = gate * s
        dup_ref[...] = silu.astype(BF16)
        dgate_ref[...] = (up * (s + silu * (1.0 - s))).astype(BF16)
        act_ref[...] = (silu * up).astype(BF16)

    blk = pl.BlockSpec((tm, tn), lambda i, j: (i, j))
    return pl.pallas_call(
        body, name="ffn_in", grid=(T // tm, nb),
        in_specs=[pl.BlockSpec((tm, K), lambda i, j: (i, 0)), pl.BlockSpec((1, K), lambda i, j: (0, 0)),
                  pl.BlockSpec((K, tn), lambda i, j: (0, j)), pl.BlockSpec((K, tn), lambda i, j: (0, j + nb))],
        out_specs=[blk, blk, blk, pl.BlockSpec((tm, K), lambda i, j: (i, 0))],
        out_shape=[jax.ShapeDtypeStruct((T, D_FF), BF16), jax.ShapeDtypeStruct((T, D_FF), BF16),
                   jax.ShapeDtypeStruct((T, D_FF), BF16), jax.ShapeDtypeStruct((T, K), BF16)],
        compiler_params=_params(dimension_semantics=("parallel", "arbitrary")),
    )(x2, g, w, w)


def _branch_mix(pm, z, w_pool_out, w_rnn_out, proj, *, tm, tn):
    T = pm.shape[0]
    col_gp = (D_POOL + 2 * D_RNN) // tn
    col_gr = col_gp + D_MODEL // tn

    def body(pm_ref, z_ref, wp_ref, wr_ref, gp_ref, gr_ref, by_gp_ref, by_gr_ref, sp_ref, sr_ref, mix_ref):
        yp = _dot(pm_ref[...], wp_ref[...])
        yr = _dot(z_ref[...], wr_ref[...])
        sp, sr = _sigmoid(gp_ref[...]), _sigmoid(gr_ref[...])
        by_gp_ref[...] = (yp * sp * (1.0 - sp)).astype(BF16)
        by_gr_ref[...] = (yr * sr * (1.0 - sr)).astype(BF16)
        sp_ref[...] = sp.astype(BF16)
        sr_ref[...] = sr.astype(BF16)
        mix_ref[...] = (sp * yp + sr * yr).astype(BF16)

    blk = pl.BlockSpec((tm, tn), lambda i, j: (i, j))
    out = jax.ShapeDtypeStruct((T, D_MODEL), BF16)
    return pl.pallas_call(
        body, name="branch_mix", grid=(T // tm, D_MODEL // tn),
        in_specs=[pl.BlockSpec((tm, D_POOL), lambda i, j: (i, 0)), pl.BlockSpec((tm, D_RNN), lambda i, j: (i, 0)),
                  pl.BlockSpec((D_POOL, tn), lambda i, j: (0, j)), pl.BlockSpec((D_RNN, tn), lambda i, j: (0, j)),
                  pl.BlockSpec((tm, tn), lambda i, j: (i, col_gp + j)), pl.BlockSpec((tm, tn), lambda i, j: (i, col_gr + j))],
        out_specs=[blk] * 5, out_shape=[out] * 5,
        compiler_params=_params(dimension_semantics=("parallel", "parallel")),
    )(pm, z, w_pool_out, w_rnn_out, proj, proj)


def _out_proj_residual(mix, w_o, x, *, tm):
    T = x.shape[0]

    def body(mix_ref, w_ref, x_ref, o_ref):
        o_ref[...] = x_ref[...] + _dot(mix_ref[...], w_ref[...])

    row = pl.BlockSpec((tm, D_MODEL), lambda i: (i, 0))
    return pl.pallas_call(
        body, name="out_proj_residual", grid=(T // tm,),
        in_specs=[row, pl.BlockSpec((D_MODEL, D_MODEL), lambda i: (0, 0)), row],
        out_specs=row, out_shape=jax.ShapeDtypeStruct((T, D_MODEL), F32),
        compiler_params=_params(dimension_semantics=("parallel",)),
    )(mix, w_o, x)


def _ffn_out_loss(act, w, x2, g3, target, *, tm):
    T = x2.shape[0]

    def body(act_ref, w_ref, x2_ref, g_ref, t_ref, dx_ref, dxb_ref, sq_ref, dg_ref):
        @pl.when(pl.program_id(0) == 0)
        def _():
            sq_ref[...] = jnp.zeros_like(sq_ref)
            dg_ref[...] = jnp.zeros_like(dg_ref)

        x3 = x2_ref[...] + _dot(act_ref[...], w_ref[...])
        g = g_ref[...]
        err = x3 * _rms_scale(x3) * g - t_ref[...]
        sq_ref[...] += jnp.sum(err * err, axis=0, keepdims=True)
        dx, dgp = _rms_bwd(err * (1.0 / D_MODEL), x3, g)
        dg_ref[...] += jnp.sum(dgp, axis=0, keepdims=True)
        dx_ref[...] = dx
        dxb_ref[...] = dx.astype(BF16)

    row = pl.BlockSpec((tm, D_MODEL), lambda i: (i, 0))
    vec = pl.BlockSpec((1, D_MODEL), lambda i: (0, 0))
    return pl.pallas_call(
        body, name="ffn_out_loss", grid=(T // tm,),
        in_specs=[pl.BlockSpec((tm, D_FF), lambda i: (i, 0)), pl.BlockSpec((D_FF, D_MODEL), lambda i: (0, 0)), row, vec, row],
        out_specs=[row, row, vec, vec],
        out_shape=[jax.ShapeDtypeStruct((T, D_MODEL), F32), jax.ShapeDtypeStruct((T, D_MODEL), BF16),
                   jax.ShapeDtypeStruct((1, D_MODEL), F32), jax.ShapeDtypeStruct((1, D_MODEL), F32)],
        compiler_params=_params(dimension_semantics=("arbitrary",)),
    )(act, w, x2, g3, target)


def _ffn_out_bwd(dx3b, w, act_by_gate, act_by_up, *, tm, tn):
    T = dx3b.shape[0]

    def body(dx_ref, w_ref, by_gate_ref, by_up_ref, dgate_ref, dup_ref):
        dact = _dot_nt(dx_ref[...], w_ref[...])
        dgate_ref[...] = (dact * by_gate_ref[...].astype(F32)).astype(BF16)
        dup_ref[...] = (dact * by_up_ref[...].astype(F32)).astype(BF16)

    blk = pl.BlockSpec((tm, tn), lambda i, j: (i, j))
    return pl.pallas_call(
        body, name="ffn_out_bwd", grid=(T // tm, D_FF // tn),
        in_specs=[pl.BlockSpec((tm, D_MODEL), lambda i, j: (i, 0)), pl.BlockSpec((tn, D_MODEL), lambda i, j: (j, 0)), blk, blk],
        out_specs=[blk, blk],
        out_shape=[jax.ShapeDtypeStruct((T, D_FF), BF16), jax.ShapeDtypeStruct((T, D_FF), BF16)],
        compiler_params=_params(dimension_semantics=("parallel", "parallel")),
    )(dx3b, w, act_by_gate, act_by_up)


def _ffn_in_bwd(dgate, dup, w, dx3, x2, g2, *, tm, jobs=()):
    T = x2.shape[0]

    def body(dgate_ref, dup_ref, w_ref, dx3_ref, x2_ref, g_ref, dx_ref, dxb_ref, dg_ref):
        @pl.when(pl.program_id(0) == 0)
        def _():
            dg_ref[...] = jnp.zeros_like(dg_ref)

        dh = _dot_nt(dgate_ref[...], w_ref[:, :D_FF]) + _dot_nt(dup_ref[...], w_ref[:, D_FF:])
        dxn, dgp = _rms_bwd(dh, x2_ref[...], g_ref[...])
        dx = dx3_ref[...] + dxn
        dg_ref[...] += jnp.sum(dgp, axis=0, keepdims=True)
        dx_ref[...] = dx
        dxb_ref[...] = dx.astype(BF16)

    row = pl.BlockSpec((tm, D_MODEL), lambda i: (i, 0))
    wide = pl.BlockSpec((tm, D_FF), lambda i: (i, 0))
    vec = pl.BlockSpec((1, D_MODEL), lambda i: (0, 0))
    return _pallas(
        body, (dgate, dup, w, dx3, x2, g2), name="ffn_in_bwd", grid=(T // tm,),
        in_specs=[wide, wide, pl.BlockSpec((D_MODEL, 2 * D_FF), lambda i: (0, 0)), row, row, vec],
        out_specs=[row, row, vec],
        out_shape=[jax.ShapeDtypeStruct((T, D_MODEL), F32), jax.ShapeDtypeStruct((T, D_MODEL), BF16),
                   jax.ShapeDtypeStruct((1, D_MODEL), F32)],
        semantics=("arbitrary",), jobs=jobs)


def _out_proj_bwd(dx2b, w_o, mix_by, *, tm, tn, jobs=()):
    T = dx2b.shape[0]

    def body(dx_ref, w_ref, *refs):
        dmix = _dot_nt(dx_ref[...], w_ref[...])
        for by_ref, d_ref in zip(refs[:4], refs[4:]):
            d_ref[...] = (dmix * by_ref[...].astype(F32)).astype(BF16)

    blk = pl.BlockSpec((tm, tn), lambda i, j: (i, j))
    out = jax.ShapeDtypeStruct((T, D_MODEL), BF16)
    return _pallas(
        body, (dx2b, w_o, *mix_by), name="out_proj_bwd", grid=(T // tm, D_MODEL // tn),
        in_specs=[pl.BlockSpec((tm, D_MODEL), lambda i, j: (i, 0)), pl.BlockSpec((tn, D_MODEL), lambda i, j: (j, 0))] + [blk] * 4,
        out_specs=[blk] * 4, out_shape=[out] * 4, semantics=("parallel", "parallel"), jobs=jobs)


def _branch_bwd(dyp, dyr, w_pool_out, w_rnn_out, *, tm):
    T = dyp.shape[0]

    def body(dyp_ref, dyr_ref, wp_ref, wr_ref, dpm_ref, dz_ref):
        dpm_ref[...] = _dot_nt(dyp_ref[...], wp_ref[...])
        dz_ref[...] = _dot_nt(dyr_ref[...], wr_ref[...])

    row = pl.BlockSpec((tm, D_MODEL), lambda i: (i, 0))
    return pl.pallas_call(
        body, name="branch_bwd", grid=(T // tm,),
        in_specs=[row, row, pl.BlockSpec((D_POOL, D_MODEL), lambda i: (0, 0)), pl.BlockSpec((D_RNN, D_MODEL), lambda i: (0, 0))],
        out_specs=[pl.BlockSpec((tm, D_POOL), lambda i: (i, 0)), pl.BlockSpec((tm, D_RNN), lambda i: (i, 0))],
        out_shape=[jax.ShapeDtypeStruct((T, D_POOL), F32), jax.ShapeDtypeStruct((T, D_RNN), F32)],
        compiler_params=_params(dimension_semantics=("parallel",)),
    )(dyp, dyr, w_pool_out, w_rnn_out)


def _in_proj_bwd(segs, w, dx2, x, g1, *, tm, jobs=()):
    T = x.shape[0]
    widths = [s.shape[1] for s in segs]
    offs = [sum(widths[:k]) for k in range(len(widths))]
    n = len(segs)

    def body(*refs):
        seg_refs, (w_ref, dx2_ref, x_ref, g_ref, dx_ref, dg_ref) = refs[:n], refs[n:]

        @pl.when(pl.program_id(0) == 0)
        def _():
            dg_ref[...] = jnp.zeros_like(dg_ref)

        dh = _dot_nt(seg_refs[0][...], w_ref[:, offs[0]:offs[0] + widths[0]])
        for k in range(1, n):
            dh += _dot_nt(seg_refs[k][...], w_ref[:, offs[k]:offs[k] + widths[k]])
        dxn, dgp = _rms_bwd(dh, x_ref[...], g_ref[...])
        dg_ref[...] += jnp.sum(dgp, axis=0, keepdims=True)
        dx_ref[...] = dx2_ref[...] + dxn

    row = pl.BlockSpec((tm, D_MODEL), lambda i: (i, 0))
    vec = pl.BlockSpec((1, D_MODEL), lambda i: (0, 0))
    return _pallas(
        body, (*segs, w, dx2, x, g1), name="in_proj_bwd", grid=(T // tm,),
        in_specs=[pl.BlockSpec((tm, wd), lambda i: (i, 0)) for wd in widths]
        + [pl.BlockSpec((D_MODEL, D_IN), lambda i: (0, 0)), row, row, vec],
        out_specs=[row, vec],
        out_shape=[jax.ShapeDtypeStruct((T, D_MODEL), F32), jax.ShapeDtypeStruct((1, D_MODEL), F32)],
        semantics=("arbitrary",), jobs=jobs)


def _weight_grad(a, segs, *, tm, tn, name, jobs=None):
    T, M = a.shape
    nblk = [s.shape[1] // tn for s in segs]
    first = [sum(nblk[:k]) for k in range(len(segs))]
    n = len(segs)

    def body(a_ref, *refs):
        seg_refs, o_ref = refs[:n], refs[n]
        j = pl.program_id(1)
        for k in range(n):
            @pl.when((j >= first[k]) & (j < first[k] + nblk[k]))
            def _(k=k):
                o_ref[...] = _dot_tn(a_ref[...], seg_refs[k][...])

    def seg_spec(k):
        return pl.BlockSpec((T, tn), lambda i, j: (0, jnp.clip(j - first[k], 0, nblk[k] - 1)))

    (grad,), results = _pallas(
        body, (a, *segs), name=name, grid=(M // tm, sum(nblk)),
        in_specs=[pl.BlockSpec((T, tm), lambda i, j: (0, i))] + [seg_spec(k) for k in range(n)],
        out_specs=[pl.BlockSpec((tm, tn), lambda i, j: (i, j))],
        out_shape=[jax.ShapeDtypeStruct((M, sum(nblk) * tn), F32)],
        semantics=("parallel", "arbitrary"), jobs=jobs or ())
    return grad if jobs is None else (grad, results)


def _pad_front(dst, src, halo):
    dst[pl.ds(0, halo), :] = jnp.zeros((halo, src.shape[1]), F32)

    def fill(i, carry):
        r0 = pl.multiple_of(i * CHUNK, CHUNK)
        dst[pl.ds(r0 + halo, CHUNK), :] = src[pl.ds(r0, CHUNK), :]
        return carry

    lax.fori_loop(0, src.shape[0] // CHUNK, fill, 0)


def _shift_rows(v, k):
    return pltpu.roll(v, k % v.shape[0], axis=0)


def _window_sums(xs, direction):
    s2 = xs + _shift_rows(xs, direction)
    s4 = s2 + _shift_rows(s2, 2 * direction)
    s8 = s4 + _shift_rows(s4, 4 * direction)
    s16 = s8 + _shift_rows(s8, 8 * direction)
    return s2, s4, s8, s16


def _select_window(g, sums):
    s2, s4, s8, s16 = sums
    return jnp.where(g == 0, s2, jnp.where(g == 1, s4, jnp.where(g == 2, s8, s16)))


def _pool_count(g, start, rows):
    t = start + lax.broadcasted_iota(jnp.int32, (rows, 1), 0)
    return jnp.minimum(t + 1, jnp.left_shift(2, g)).astype(F32)


def _pool_fwd(proj, w_grp, scale):
    T = proj.shape[0]
    nchunk = T // CHUNK

    def body(u_ref, w_ref, s_ref, o_ref, upad):
        g = pl.program_id(0)
        _pad_front(upad, u_ref, POOL_HALO)
        w = w_ref[...].astype(BF16)
        scale_row = s_ref[...]

        def chunk(i, carry):
            r0 = pl.multiple_of(i * CHUNK, CHUNK)
            xs = upad[pl.ds(r0, CHUNK + POOL_HALO), :]
            win = _select_window(g, _window_sums(xs, 1))[POOL_HALO:]
            pooled = win / _pool_count(g, r0, CHUNK) - xs[POOL_HALO:]
            o_ref[pl.ds(r0, CHUNK), :] = (_dot(pooled.astype(BF16), w) * scale_row).astype(BF16)
            return carry

        lax.fori_loop(0, nchunk, chunk, 0)

    return pl.pallas_call(
        body, name="pool_fwd", grid=(N_POOL_GROUPS,),
        in_specs=[pl.BlockSpec((T, HEAD), lambda g: (0, g)), pl.BlockSpec((None, HEAD, HEAD), lambda g: (g, 0, 0)),
                  pl.BlockSpec((1, HEAD), lambda g: (0, g))],
        out_specs=pl.BlockSpec((T, HEAD), lambda g: (0, g)),
        out_shape=jax.ShapeDtypeStruct((T, D_POOL), BF16),
        scratch_shapes=[pltpu.VMEM((T + POOL_HALO, HEAD), F32)],
        compiler_params=_params(dimension_semantics=("parallel",)),
    )(proj, w_grp, scale)


def _pool_bwd(proj, dpm, w_grp, scale, jobs=()):
    T = proj.shape[0]
    nchunk = T // CHUNK

    def body(u_ref, dpm_ref, w_ref, s_ref, du_ref, dw_ref, ds_ref, upad, zpad, dpool):
        g = pl.program_id(0)
        _pad_front(upad, u_ref, POOL_HALO)
        zpad[pl.ds(T, POOL_HALO), :] = jnp.zeros((POOL_HALO, HEAD), F32)
        dw_ref[...] = jnp.zeros_like(dw_ref)
        ds_ref[...] = jnp.zeros_like(ds_ref)
        w = w_ref[...].astype(BF16)
        scale_row = s_ref[...]

        def chunk(i, carry):
            r0 = pl.multiple_of(i * CHUNK, CHUNK)
            xs = upad[pl.ds(r0, CHUNK + POOL_HALO), :]
            cnt = _pool_count(g, r0, CHUNK)
            pooled = (_select_window(g, _window_sums(xs, 1))[POOL_HALO:] / cnt - xs[POOL_HALO:]).astype(BF16)
            mixed = _dot(pooled, w)
            d = dpm_ref[pl.ds(r0, CHUNK), :]
            ds_ref[...] += jnp.sum(d * mixed, axis=0, keepdims=True)
            dmixed = (d * scale_row).astype(BF16)
            dw_ref[...] += _dot_tn(pooled, dmixed)
            dp = _dot_nt(dmixed, w)
            dpool[pl.ds(r0, CHUNK), :] = dp
            zpad[pl.ds(r0, CHUNK), :] = dp / cnt
            return carry

        lax.fori_loop(0, nchunk, chunk, 0)

        def chunk2(i, carry):
            r0 = pl.multiple_of(i * CHUNK, CHUNK)
            zs = zpad[pl.ds(r0, CHUNK + POOL_HALO), :]
            win = _select_window(g, _window_sums(zs, -1))[:CHUNK]
            du_ref[pl.ds(r0, CHUNK), :] = (win - dpool[pl.ds(r0, CHUNK), :]).astype(BF16)
            return carry

        lax.fori_loop(0, nchunk, chunk2, 0)

    col = pl.BlockSpec((T, HEAD), lambda g: (0, g))
    return _pallas(
        body, (proj, dpm, w_grp, scale), name="pool_bwd", grid=(N_POOL_GROUPS,),
        in_specs=[col, col, pl.BlockSpec((None, HEAD, HEAD), lambda g: (g, 0, 0)), pl.BlockSpec((1, HEAD), lambda g: (0, g))],
        out_specs=[col, pl.BlockSpec((None, HEAD, HEAD), lambda g: (g, 0, 0)), pl.BlockSpec((1, HEAD), lambda g: (0, g))],
        out_shape=[jax.ShapeDtypeStruct((T, D_POOL), BF16), jax.ShapeDtypeStruct((N_POOL_GROUPS, HEAD, HEAD), F32),
                   jax.ShapeDtypeStruct((1, D_POOL), F32)],
        scratch_shapes=[pltpu.VMEM((T + POOL_HALO, HEAD), F32), pltpu.VMEM((T + POOL_HALO, HEAD), F32), pltpu.VMEM((T, HEAD), F32)],
        semantics=("parallel",), jobs=jobs)


def _conv_taps(xs, cw):
    v = cw[CONV_WIDTH - 1] * xs[SUBLANES:]
    for k in range(CONV_WIDTH - 1):
        v += cw[k] * _shift_rows(xs, CONV_WIDTH - 1 - k)[SUBLANES:]
    return v


def _tap_rows(cw_ref):
    return [cw_ref[k:k + 1, :] for k in range(CONV_WIDTH)]


def _softplus_neg(lam):
    return jnp.maximum(-lam, 0.0) + _log1p(jnp.exp(-jnp.abs(lam)))


def _lru_gates(v, wa, ba, wx, bx, sp):
    vb = v.astype(BF16)
    ra = _sigmoid(_dot(vb, wa) + ba)
    ix = _sigmoid(_dot(vb, wx) + bx)
    log_a = -LRU_C * ra * sp
    a = jnp.exp(log_a)
    sq = jnp.sqrt(-jnp.tanh(log_a) * (a * a + 1.0))
    return ra, ix, a, sq


def _row_bcast(v, r):
    return jnp.broadcast_to(v[r:r + 1, :], v.shape)


TILE_BLOCK = 128


def _scan_in_tiles(coef, coef_shift, A_out, B, T, direction):
    order = list(range(SUBLANES)) if direction == 1 else list(range(SUBLANES - 1, -1, -1))
    tiles = min(TILE_BLOCK, T // SUBLANES)
    for base in range(0, T, tiles * SUBLANES):
        def rows(r, base=base):
            return pl.ds(base + r, tiles, stride=SUBLANES)

        A, Bv = coef[rows(order[0] + coef_shift), :], B[rows(order[0]), :]
        A_out[rows(order[0]), :] = A
        for r in order[1:]:
            a = coef[rows(r + coef_shift), :]
            Bv = a * Bv + B[rows(r), :]
            A = a * A
            A_out[rows(r), :] = A
            B[rows(r), :] = Bv


TILES_PER_STEP = 8


def _carry_tiles(A_s, B_s, out, ntile, direction):
    out_row = SUBLANES - 1 if direction == 1 else 0

    def step(k, carry):
        for j in range(TILES_PER_STEP):
            t = k * TILES_PER_STEP + j
            r0 = pl.multiple_of((t if direction == 1 else ntile - 1 - t) * SUBLANES, SUBLANES)
            A, B = A_s[pl.ds(r0, SUBLANES), :], B_s[pl.ds(r0, SUBLANES), :]
            out[pl.ds(r0, SUBLANES), :] = A * carry + B
            carry = _row_bcast(A, out_row) * carry + _row_bcast(B, out_row)
        return carry

    lax.fori_loop(0, ntile // TILES_PER_STEP, step, jnp.zeros((SUBLANES, HEAD), F32))


def _rnn_fwd(proj, conv_w, conv_b, w_a, b_a, w_x, b_x, lam, jobs=()):
    T = proj.shape[0]
    nchunk = T // CHUNK
    ntile = T // SUBLANES

    def body(u_ref, ug_ref, cw_ref, cb_ref, wa_ref, ba_ref, wx_ref, bx_ref, lam_ref,
             h_ref, z_ref, v_ref, ra_ref, ix_ref, a_ref, sq_ref, upad, a_s, b_s):
        _pad_front(upad, u_ref, SUBLANES)
        cw, cb = _tap_rows(cw_ref), cb_ref[...]
        wa, wx = wa_ref[...].astype(BF16), wx_ref[...].astype(BF16)
        ba, bx = ba_ref[...], bx_ref[...]
        sp = _softplus_neg(lam_ref[...])

        def chunk(i, carry):
            rows = pl.ds(pl.multiple_of(i * CHUNK, CHUNK), CHUNK)
            v = _conv_taps(upad[pl.ds(pl.multiple_of(i * CHUNK, CHUNK), CHUNK + SUBLANES), :], cw) + cb
            ra, ix, a, sq = _lru_gates(v, wa, ba, wx, bx, sp)
            v_ref[rows, :], ra_ref[rows, :], ix_ref[rows, :], a_ref[rows, :], sq_ref[rows, :] = v, ra, ix, a, sq
            a_s[rows, :], b_s[rows, :] = a, sq * ix * v
            return carry

        lax.fori_loop(0, nchunk, chunk, 0)
        _scan_in_tiles(a_s, 0, a_s, b_s, T, 1)
        _carry_tiles(a_s, b_s, h_ref, ntile, 1)

        def chunk3(i, carry):
            r0 = pl.multiple_of(i * CHUNK, CHUNK)
            gl, _ = _gelu_parts(ug_ref[pl.ds(r0, CHUNK), :])
            z_ref[pl.ds(r0, CHUNK), :] = (h_ref[pl.ds(r0, CHUNK), :] * gl).astype(BF16)
            return carry

        lax.fori_loop(0, nchunk, chunk3, 0)

    col = pl.BlockSpec((T, HEAD), lambda h: (0, h))
    vec = pl.BlockSpec((1, HEAD), lambda h: (0, h))
    mat = pl.BlockSpec((None, HEAD, HEAD), lambda h: (h, 0, 0))
    return _pallas(
        body, (proj, proj, conv_w, conv_b, w_a, b_a, w_x, b_x, lam), name="rnn_fwd", grid=(N_RNN_HEADS,),
        in_specs=[pl.BlockSpec((T, HEAD), lambda h: (0, COL_RNN + h)), pl.BlockSpec((T, HEAD), lambda h: (0, COL_GATE + h)),
                  pl.BlockSpec((CONV_WIDTH, HEAD), lambda h: (0, h)), vec, mat, vec, mat, vec, vec],
        out_specs=[col] * 7,
        out_shape=[jax.ShapeDtypeStruct((T, D_RNN), F32), jax.ShapeDtypeStruct((T, D_RNN), BF16)]
        + [jax.ShapeDtypeStruct((T, D_RNN), F32)] * 5,
        scratch_shapes=[pltpu.VMEM((T + SUBLANES, HEAD), F32), pltpu.VMEM((T, HEAD), F32), pltpu.VMEM((T, HEAD), F32)],
        semantics=("parallel",), jobs=jobs)


def _rnn_bwd(proj, hr, dz, gates, conv_w, w_a, w_x, lam, jobs=()):
    T = proj.shape[0]
    nchunk = T // CHUNK
    ntile = T // SUBLANES

    def body(u_ref, ug_ref, h_ref, dz_ref, v_ref, ra_ref, ix_ref, a_ref, sq_ref, cw_ref, wa_ref, wx_ref, lam_ref,
             du_ref, dug_ref, dwa_ref, dwx_ref, dba_ref, dbx_ref, dlam_ref, dcb_ref, dcw_ref,
             upad, hpad, apad, g_s, dvpad, ga_s):
        zero_tile = jnp.zeros((SUBLANES, HEAD), F32)
        _pad_front(upad, u_ref, SUBLANES)
        _pad_front(hpad, h_ref, SUBLANES)
        apad[pl.ds(T, SUBLANES), :] = zero_tile
        dvpad[pl.ds(T, SUBLANES), :] = zero_tile
        for ref in (dwa_ref, dwx_ref, dba_ref, dbx_ref, dlam_ref, dcb_ref, dcw_ref):
            ref[...] = jnp.zeros_like(ref)
        cw = _tap_rows(cw_ref)
        wa, wx = wa_ref[...].astype(BF16), wx_ref[...].astype(BF16)
        lam_row = lam_ref[...]
        sp = _softplus_neg(lam_row)

        def chunk(i, carry):
            rows = pl.ds(pl.multiple_of(i * CHUNK, CHUNK), CHUNK)
            apad[rows, :] = a_ref[rows, :]
            gl, dgl = _gelu_parts(ug_ref[rows, :])
            d = dz_ref[rows, :]
            g_s[rows, :] = d * gl
            dug_ref[rows, :] = (d * h_ref[rows, :] * dgl).astype(BF16)
            return carry

        lax.fori_loop(0, nchunk, chunk, 0)

        _scan_in_tiles(apad, 1, ga_s, g_s, T, -1)
        _carry_tiles(ga_s, g_s, g_s, ntile, -1)

        def chunk3(i, carry):
            r0 = pl.multiple_of(i * CHUNK, CHUNK)
            rows = pl.ds(r0, CHUNK)
            g = g_s[rows, :]
            h_prev = _shift_rows(hpad[pl.ds(r0, CHUNK + SUBLANES), :], 1)[SUBLANES:]
            v, ra, ix, sq, a = v_ref[rows, :], ra_ref[rows, :], ix_ref[rows, :], sq_ref[rows, :], a_ref[rows, :]
            d_sq = g * ix * v
            d_ix = g * sq * v
            d_la = a * g * h_prev - d_sq * a * a / sq
            dlam_ref[...] += jnp.sum(d_la * ra, axis=0, keepdims=True)
            d_pa = d_la * (-LRU_C) * sp * ra * (1.0 - ra)
            d_px = d_ix * ix * (1.0 - ix)
            vb, d_pab, d_pxb = v.astype(BF16), d_pa.astype(BF16), d_px.astype(BF16)
            dwa_ref[...] += _dot_tn(vb, d_pab)
            dwx_ref[...] += _dot_tn(vb, d_pxb)
            dba_ref[...] += jnp.sum(d_pa, axis=0, keepdims=True)
            dbx_ref[...] += jnp.sum(d_px, axis=0, keepdims=True)
            dv = g * sq * ix + _dot_nt(d_pab, wa) + _dot_nt(d_pxb, wx)
            dvpad[rows, :] = dv
            dcb_ref[...] += jnp.sum(dv, axis=0, keepdims=True)
            xs = upad[pl.ds(r0, CHUNK + SUBLANES), :]
            for k in range(CONV_WIDTH):
                u_k = _shift_rows(xs, CONV_WIDTH - 1 - k)[SUBLANES:] if k < CONV_WIDTH - 1 else xs[SUBLANES:]
                dcw_ref[k:k + 1, :] += jnp.sum(dv * u_k, axis=0, keepdims=True)
            return carry

        lax.fori_loop(0, nchunk, chunk3, 0)
        dlam_ref[...] = dlam_ref[...] * (LRU_C * _sigmoid(-lam_row))

        def chunk4(i, carry):
            r0 = pl.multiple_of(i * CHUNK, CHUNK)
            dvs = dvpad[pl.ds(r0, CHUNK + SUBLANES), :]
            du = cw[CONV_WIDTH - 1] * dvs[:CHUNK]
            for k in range(CONV_WIDTH - 1):
                du += cw[k] * _shift_rows(dvs, -(CONV_WIDTH - 1 - k))[:CHUNK]
            du_ref[pl.ds(r0, CHUNK), :] = du.astype(BF16)
            return carry

        lax.fori_loop(0, nchunk, chunk4, 0)

    col = pl.BlockSpec((T, HEAD), lambda h: (0, h))
    vec = pl.BlockSpec((1, HEAD), lambda h: (0, h))
    mat = pl.BlockSpec((None, HEAD, HEAD), lambda h: (h, 0, 0))
    taps = pl.BlockSpec((CONV_WIDTH, HEAD), lambda h: (0, h))
    vec_out = jax.ShapeDtypeStruct((1, D_RNN), F32)
    mat_out = jax.ShapeDtypeStruct((N_RNN_HEADS, HEAD, HEAD), F32)
    seq = pltpu.VMEM((T, HEAD), F32)
    seq_pad = pltpu.VMEM((T + SUBLANES, HEAD), F32)
    return _pallas(
        body, (proj, proj, hr, dz, *gates, conv_w, w_a, w_x, lam), name="rnn_bwd", grid=(N_RNN_HEADS,),
        in_specs=[pl.BlockSpec((T, HEAD), lambda h: (0, COL_RNN + h)), pl.BlockSpec((T, HEAD), lambda h: (0, COL_GATE + h))]
        + [col] * 7 + [taps, mat, mat, vec],
        out_specs=[col, col, mat, mat, vec, vec, vec, vec, taps],
        out_shape=[jax.ShapeDtypeStruct((T, D_RNN), BF16), jax.ShapeDtypeStruct((T, D_RNN), BF16), mat_out, mat_out,
                   vec_out, vec_out, vec_out, vec_out, jax.ShapeDtypeStruct((CONV_WIDTH, D_RNN), F32)],
        scratch_shapes=[seq_pad, seq_pad, seq_pad, seq, seq_pad, seq],
        semantics=("parallel",), jobs=jobs)


GROUP_FFN_OUT = ["w_ffn_out"]
GROUP_FFN_IN = ["w_ffn_in"]
GROUP_MIX = ["w_o", "w_pool_out", "w_rnn_out"]
GROUP_IN = ["w_in"]


def _step(x, target, s, full, conv_w_mine, place):
    T = x.shape[0]
    tall, mid, low = min(T, 2048), min(T, 1024), min(T, 512)
    full = dict(full)

    def gathered(names, results):
        full.update(zip(names, results))

    (full["w_in"], conv_w), = _run_jobs([_gather_job(full, ["w_in"], conv_w_mine)], "gather_w_in")
    early = ["w_pool_out", "w_rnn_out", "w_o", "w_ffn_out"]
    (proj, h1), (res,) = _norm_matmul(x, s["norm_mix"], full["w_in"], tm=tall, tn=512, name="in_proj", jobs=[_gather_job(full, early)])
    gathered(early, res)
    pm = _pool_fwd(proj, s["w_pool_grp"], s["pool_scale"])
    (hr, z, *gates), (res,) = _rnn_fwd(proj, conv_w, s["conv_b"], s["w_rg_a"], s["b_rg_a"], s["w_rg_x"], s["b_rg_x"],
                                       s["lru_lambda"], jobs=[_gather_job(full, ["w_ffn_in"])])
    gathered(["w_ffn_in"], res)
    *mix_by, mix = _branch_mix(pm, z, full["w_pool_out"], full["w_rnn_out"], proj, tm=tall, tn=256)
    x2 = _out_proj_residual(mix, full["w_o"], x, tm=mid)
    act_by_up, act_by_gate, act, h2 = _ffn_in(x2, s["norm_ffn"], full["w_ffn_in"], tm=tall, tn=256)
    dx3, dx3b, sq_cols, g_norm_final = _ffn_out_loss(act, full["w_ffn_out"], x2, s["norm_final"], target, tm=low)

    g = {"norm_final": g_norm_final}

    def chip_sums(names, from_sibling):
        sums = {name: _chip_sum(name, g[name], got, place) for name, got in zip(names, from_sibling)}
        return {name: v[0] for name, v in sums.items()}, {name: v[1] for name, v in sums.items()}

    def final_sums(names, sums, from_chips):
        return {name: _final_sum(name, sums[name], got, place) for name, got in zip(names, from_chips)}

    dgate, dup = _ffn_out_bwd(dx3b, full["w_ffn_out"], act_by_gate, act_by_up, tm=tall, tn=256)
    g["w_ffn_out"] = _weight_grad(act, [dx3b], tm=256, tn=D_MODEL, name="w_ffn_out_grad")
    (dx2, dx2b, g["norm_ffn"]), (res,) = _ffn_in_bwd(dgate, dup, full["w_ffn_in"], dx3, x2, s["norm_ffn"], tm=low,
                                                     jobs=[_sibling_job(g, GROUP_FFN_OUT)])
    sums_ffn, sums_ffn_bf16 = chip_sums(GROUP_FFN_OUT, res)
    g["w_ffn_in"], (res,) = _weight_grad(h2, [dgate, dup], tm=D_MODEL, tn=256, name="w_ffn_in_grad",
                                         jobs=[_chips_job(sums_ffn_bf16, GROUP_FFN_OUT)])
    shards_ffn = final_sums(GROUP_FFN_OUT, sums_ffn, res)
    (dgp, dgr, dyp, dyr), (res,) = _out_proj_bwd(dx2b, full["w_o"], mix_by, tm=tall, tn=256,
                                                 jobs=[_sibling_job(g, GROUP_FFN_IN)])
    sums_ffn, sums_ffn_bf16 = chip_sums(GROUP_FFN_IN, res)
    g["w_o"] = _weight_grad(mix, [dx2b], tm=D_MODEL, tn=256, name="w_o_grad")
    dpm, dz = _branch_bwd(dyp, dyr, full["w_pool_out"], full["w_rnn_out"], tm=mid)
    g["w_pool_out"] = _weight_grad(pm, [dyp], tm=D_POOL, tn=256, name="w_pool_out_grad")
    g["w_rnn_out"] = _weight_grad(z, [dyr], tm=D_RNN, tn=256, name="w_rnn_out_grad")
    (dupool, g["w_pool_grp"], g["pool_scale"]), (res,) = _pool_bwd(proj, dpm, s["w_pool_grp"], s["pool_scale"],
                                                                   jobs=[_sibling_job(g, GROUP_MIX)])
    sums_mix, sums_mix_bf16 = chip_sums(GROUP_MIX, res)
    ((durnn, dugate, g["w_rg_a"], g["w_rg_x"], g["b_rg_a"], g["b_rg_x"], g["lru_lambda"], g["conv_b"], g["conv_w"]),
     (res,)) = _rnn_bwd(proj, hr, dz, gates, conv_w, s["w_rg_a"], s["w_rg_x"], s["lru_lambda"],
                        jobs=[_chips_job(sums_ffn_bf16, GROUP_FFN_IN)])
    shards_ffn.update(final_sums(GROUP_FFN_IN, sums_ffn, res))
    segs = [dupool, durnn, dugate, dgp, dgr]
    ffn = GROUP_FFN_OUT + GROUP_FFN_IN
    g["w_in"], (res, joined) = _weight_grad(h1, segs, tm=D_MODEL, tn=256, name="w_in_grad",
                                           jobs=[_chips_job(sums_mix_bf16, GROUP_MIX), _join_job(shards_ffn, ffn)])
    grads = dict(zip(ffn, joined))
    shards = final_sums(GROUP_MIX, sums_mix, res)
    (res,) = _run_jobs([_sibling_job(g, GROUP_IN)], "w_in_exchange_sibling")
    sums_in, sums_in_bf16 = chip_sums(GROUP_IN, res)
    (grad_x, g["norm_mix"]), (res,) = _in_proj_bwd(segs, full["w_in"], dx2, x, s["norm_mix"], tm=low,
                                                  jobs=[_chips_job(sums_in_bf16, GROUP_IN)])
    shards.update(final_sums(GROUP_IN, sums_in, res))

    vec_rows = [g[name] if name != "pool_scale" else jnp.pad(g[name], ((0, 0), (0, D_MODEL - D_POOL))) for name in VEC_ITEMS]
    vec_rows += [g["conv_w"], sq_cols, jnp.zeros((VEC_ROWS - len(VEC_ITEMS) - CONV_WIDTH - 1, D_MODEL), F32)]
    vec = jnp.concatenate(vec_rows, axis=0).reshape(VEC_ROWS, N_DEV, HEAD).transpose(1, 0, 2)
    mat = jnp.concatenate([g[name].reshape(-1, HEAD) for name in MAT_ITEMS], axis=0).reshape(N_DEV, -1, HEAD)
    (vec, mat), (joined,) = _all_reduce_small([vec, mat], jobs=[_join_job(shards, GROUP_MIX + GROUP_IN)])
    grads.update(zip(GROUP_MIX + GROUP_IN, joined))

    vec = vec.transpose(1, 0, 2).reshape(VEC_ROWS, D_MODEL)
    mat = mat.reshape(-1, HEAD)
    for k, name in enumerate(VEC_ITEMS):
        grads[name] = vec[k:k + 1, :s[name].shape[1]]
    grads["conv_w"] = vec[len(VEC_ITEMS):len(VEC_ITEMS) + CONV_WIDTH]
    row = 0
    for name in MAT_ITEMS:
        rows = s[name].shape[0] * HEAD
        grads[name] = mat[row:row + rows]
        row += rows
    return vec[len(VEC_ITEMS) + CONV_WIDTH], grad_x, grads


LARGE = {"w_in": "col", "w_pool_out": "col", "w_rnn_out": "row", "w_o": "row", "w_ffn_in": "col", "w_ffn_out": "row"}
LARGE_SHAPE = {"w_in": (D_MODEL, D_IN), "w_pool_out": (D_POOL, D_MODEL), "w_rnn_out": (D_RNN, D_MODEL),
               "w_o": (D_MODEL, D_MODEL), "w_ffn_in": (D_MODEL, 2 * D_FF), "w_ffn_out": (D_FF, D_MODEL)}


def _place():
    x, y, c = lax.axis_index("x"), lax.axis_index("y"), lax.axis_index("c")
    return 2 * x + y, c


def _chip_device(chip, c):
    return (chip // 2, chip % 2, c)


def _chip_window(ref, kind, shape, chip, half=None):
    K, N = shape
    if kind == "col":
        rows = slice(None) if half is None else pl.ds(half * (K // 2), K // 2)
        return ref.at[rows, pl.ds(chip * (N // N_CHIPS), N // N_CHIPS)]
    ks = K // N_CHIPS
    if half is None:
        return ref.at[pl.ds(chip * ks, ks), :]
    return ref.at[pl.ds(chip * ks + half * (ks // 2), ks // 2), :]


def _row_half(ref, half):
    rows = ref.shape[0] // 2
    return ref.at[pl.ds(half * rows, rows), :]


def _remote(win_src, win_dst, send_sems, recv_sems, idx, to):
    return pltpu.make_async_remote_copy(src_ref=win_src, dst_ref=win_dst, send_sem=send_sems.at[idx], recv_sem=recv_sems.at[idx],
                                        device_id=to, device_id_type=MESH)


def _gather_job(full, names, conv_w_full=None):
    n = len(names)
    cw_cols = D_RNN // N_CHIPS

    def windows(refs, chip, half):
        return [_chip_window(refs[k], LARGE[name], LARGE_SHAPE[name], chip, half) for k, name in enumerate(names)]

    def ici_copies(refs, send_sems, recv_sems, src_chip, dst_chip, c, r):
        wins = windows(refs, src_chip, c)
        if conv_w_full is not None:
            wins.append(refs[n].at[:, pl.ds(src_chip * cw_cols, cw_cols)])
        return [_remote(win, win, send_sems, recv_sems, (k, r), _chip_device(dst_chip, c)) for k, win in enumerate(wins)]

    def forwards(refs, send_sems, recv_sems, src_chip, half, to_core, chip, r):
        return [_remote(win, win, send_sems, recv_sems, (k, 3 + r), _chip_device(chip, to_core))
                for k, win in enumerate(windows(refs, src_chip, half))]

    def start(ins, outs, send_sems, recv_sems):
        chip, c = _place()
        for r in range(3):
            for cp in ici_copies(outs, send_sems, recv_sems, chip, chip ^ (r + 1), c, r):
                cp.start()

    def finish(ins, outs, send_sems, recv_sems):
        chip, c = _place()
        for r in range(3):
            for cp in ici_copies(outs, send_sems, recv_sems, chip ^ (r + 1), chip, c, r):
                cp.wait_recv()
            for cp in forwards(outs, send_sems, recv_sems, chip ^ (r + 1), c, 1 - c, chip, r):
                cp.start()
        for r in range(3):
            for cp in forwards(outs, send_sems, recv_sems, chip ^ (r + 1), 1 - c, c, chip, r):
                cp.wait_recv()
            for cp in ici_copies(outs, send_sems, recv_sems, chip, chip ^ (r + 1), c, r):
                cp.wait_send()
            for cp in forwards(outs, send_sems, recv_sems, chip ^ (r + 1), c, 1 - c, chip, r):
                cp.wait_send()

    arrays = [full[name] for name in names] + ([conv_w_full] if conv_w_full is not None else [])
    return _Job(arrays, [jax.ShapeDtypeStruct(a.shape, a.dtype) for a in arrays], {k: k for k in range(len(arrays))},
                (len(arrays), 6), start, finish)


def _core_halves(ref, kind, shape, c):
    return [_chip_window(ref, kind, shape, chip, c) for chip in range(N_CHIPS)]


def _sibling_job(grads, names):
    def start(ins, outs, send_sems, recv_sems):
        chip, c = _place()
        for k, name in enumerate(names):
            kind, shape = LARGE[name], LARGE_SHAPE[name]
            if kind == "col":
                pairs = [(_row_half(ins[k], 1 - c), outs[k])]
            else:
                rows = shape[0] // N_DEV
                pairs = [(win, outs[k].at[pl.ds(j * rows, rows), :]) for j, win in enumerate(_core_halves(ins[k], kind, shape, 1 - c))]
            for src, dst in pairs:
                _remote(src, dst, send_sems, recv_sems, k, _chip_device(chip, 1 - c)).start()

    def finish(ins, outs, send_sems, recv_sems):
        chip, c = _place()
        for k in range(len(names)):
            _remote(outs[k], outs[k], send_sems, recv_sems, k, _chip_device(chip, 1 - c)).wait()

    return _Job([grads[name] for name in names],
                [jax.ShapeDtypeStruct((LARGE_SHAPE[name][0] // 2, LARGE_SHAPE[name][1]), F32) for name in names], {},
                (len(names),), start, finish)


def _chip_sum(name, g, got, place):
    kind, (K, N) = LARGE[name], LARGE_SHAPE[name]
    rows = K // N_DEV
    piece_cols = N // N_CHIPS

    def body(place_ref, g_ref, got_ref, o_ref, ob_ref):
        total = g_ref[...] + got_ref[...]
        ob_ref[...] = total.astype(BF16)
        if kind == "col":
            for chip in range(N_CHIPS):
                @pl.when(place_ref[0] == chip)
                def _(chip=chip):
                    o_ref[...] = total[:, chip * piece_cols:(chip + 1) * piece_cols]
        else:
            @pl.when(pl.program_id(0) == place_ref[0])
            def _():
                o_ref[...] = total

    if kind == "col":
        mine = pl.BlockSpec((rows, N), lambda j, place_ref: (j + N_CHIPS * place_ref[1], 0))
        own = pl.BlockSpec((rows, piece_cols), lambda j, place_ref: (j, 0))
    else:
        mine = pl.BlockSpec((rows, N), lambda j, place_ref: (2 * j + place_ref[1], 0))
        own = pl.BlockSpec((rows, N), lambda j, place_ref: (0, 0))
    blk = pl.BlockSpec((rows, N), lambda j, place_ref: (j, 0))
    return pl.pallas_call(
        body, name=name + "_chip_sum",
        grid_spec=pltpu.PrefetchScalarGridSpec(num_scalar_prefetch=1, grid=(N_CHIPS,), in_specs=[mine, blk], out_specs=[own, blk]),
        out_shape=[jax.ShapeDtypeStruct(_piece_shape(name), F32), jax.ShapeDtypeStruct((K // 2, N), BF16)],
        compiler_params=_params(dimension_semantics=("arbitrary",)),
    )(place, g, got)


def _piece(ref, kind, shape, chip):
    K, N = shape
    if kind == "col":
        return ref.at[:, pl.ds(chip * (N // N_CHIPS), N // N_CHIPS)]
    return ref.at[pl.ds(chip * (K // N_DEV), K // N_DEV), :]


def _piece_shape(name):
    kind, (K, N) = LARGE[name], LARGE_SHAPE[name]
    return (K // 2, N // N_CHIPS) if kind == "col" else (K // N_DEV, N)


def _chips_job(sums, names):
    def copies(ins, outs, send_sems, recv_sems):
        chip, c = _place()
        return [_remote(_piece(ins[k], LARGE[name], LARGE_SHAPE[name], chip ^ (r + 1)), outs[k].at[r], send_sems, recv_sems, (k, r),
                        _chip_device(chip ^ (r + 1), c)) for k, name in enumerate(names) for r in range(3)]

    def start(*refs):
        for cp in copies(*refs):
            cp.start()

    def finish(*refs):
        for cp in copies(*refs):
            cp.wait()

    return _Job([sums[name] for name in names], [jax.ShapeDtypeStruct((3,) + _piece_shape(name), BF16) for name in names], {},
                (len(names), 3), start, finish)


def _final_sum(name, chip_sum, got, place):
    rows, cols = _piece_shape(name)

    def body(place_ref, s_ref, got_ref, o_ref):
        o_ref[...] = ((s_ref[...] + got_ref[0].astype(F32)) + got_ref[1].astype(F32)) + got_ref[2].astype(F32)

    mine = pl.BlockSpec((rows, cols), lambda i, place_ref: (0, 0))
    return pl.pallas_call(
        body, name=name + "_final_sum",
        grid_spec=pltpu.PrefetchScalarGridSpec(
            num_scalar_prefetch=1, grid=(1,), in_specs=[mine, pl.BlockSpec((3, rows, cols), lambda i, place_ref: (0, 0, 0))],
            out_specs=pl.BlockSpec((rows, cols), lambda i, place_ref: (place_ref[1], 0))),
        out_shape=jax.ShapeDtypeStruct((2 * rows, cols), F32),
        compiler_params=_params(dimension_semantics=("arbitrary",)),
    )(place, chip_sum, got)


def _join_job(shards, names):
    def half_copy(outs, send_sems, recv_sems, k, mine):
        chip, c = _place()
        win = _row_half(outs[k], c if mine else 1 - c)
        return _remote(win, win, send_sems, recv_sems, k, _chip_device(chip, 1 - c))

    def start(ins, outs, send_sems, recv_sems):
        for k in range(len(names)):
            half_copy(outs, send_sems, recv_sems, k, True).start()

    def finish(ins, outs, send_sems, recv_sems):
        for k in range(len(names)):
            half_copy(outs, send_sems, recv_sems, k, True).wait_send()
            half_copy(outs, send_sems, recv_sems, k, False).wait_recv()

    arrays = [shards[name] for name in names]
    return _Job(arrays, [jax.ShapeDtypeStruct(a.shape, F32) for a in arrays], {k: k for k in range(len(arrays))},
                (len(arrays),), start, finish)


VEC_ROWS = 16


def _all_reduce_small(slabs, jobs=()):
    n = len(slabs)

    def body(*refs):
        in_refs, out_refs, got_refs = refs[:n], refs[n:2 * n], refs[2 * n:3 * n]
        send_sems, recv_sems = refs[3 * n:]
        x, y, c = lax.axis_index("x"), lax.axis_index("y"), lax.axis_index("c")
        me = 4 * x + 2 * y + c

        def remote(src, dst, k, phase, r):
            other = me ^ r
            return pltpu.make_async_remote_copy(src_ref=src, dst_ref=dst, send_sem=send_sems.at[k, phase, r],
                                                recv_sem=recv_sems.at[k, phase, r],
                                                device_id=(other // 4, (other // 2) % 2, other % 2), device_id_type=MESH)

        scatter = [remote(in_refs[k].at[me ^ r], got_refs[k].at[r], k, 0, r) for r in range(1, N_DEV) for k in range(n)]
        for cp in scatter:
            cp.start()
        for cp in scatter:
            cp.wait()
        for k in range(n):
            total = in_refs[k][me]
            for r in range(1, N_DEV):
                total = total + got_refs[k][r]
            out_refs[k][me] = total
        gather = [remote(out_refs[k].at[me], out_refs[k].at[me], k, 1, r) for r in range(1, N_DEV) for k in range(n)]
        for cp in gather:
            cp.start()
        for r in range(1, N_DEV):
            for k in range(n):
                remote(out_refs[k].at[me ^ r], out_refs[k].at[me ^ r], k, 1, r).wait_recv()
        for cp in gather:
            cp.wait_send()

    return _pallas(
        body, slabs, name="all_reduce_small", grid=(), in_specs=[VMEM] * n, out_specs=[VMEM] * n,
        out_shape=[jax.ShapeDtypeStruct(s.shape, F32) for s in slabs],
        scratch_shapes=[pltpu.VMEM(s.shape, F32) for s in slabs]
        + [pltpu.SemaphoreType.DMA((n, 2, N_DEV)), pltpu.SemaphoreType.DMA((n, 2, N_DEV))], jobs=jobs)


def _cast_into_whole(w, name, place):
    rows, cols = w.shape
    tr = rows // 2

    def body(place_ref, w_ref, o_ref):
        o_ref[...] = w_ref[...].astype(BF16)

    if LARGE[name] == "col":
        window = pl.BlockSpec((tr, cols), lambda i, place_ref: (i, place_ref[0]))
    else:
        window = pl.BlockSpec((tr, cols), lambda i, place_ref: (2 * place_ref[0] + i, 0))
    return pl.pallas_call(
        body, name=name + "_cast",
        grid_spec=pltpu.PrefetchScalarGridSpec(num_scalar_prefetch=1, grid=(2,),
                                               in_specs=[pl.BlockSpec((tr, cols), lambda i, place_ref: (i, 0))], out_specs=window),
        out_shape=jax.ShapeDtypeStruct(LARGE_SHAPE[name], BF16),
        compiler_params=_params(dimension_semantics=("parallel",)))(place, w)


def _adamw_math(w, g, m, v):
    m = ADAM_B1 * m + (1.0 - ADAM_B1) * g
    v = ADAM_B2 * v + (1.0 - ADAM_B2) * (g * g)
    m_hat = m / (1.0 - ADAM_B1 ** ADAM_STEP)
    v_hat = v / (1.0 - ADAM_B2 ** ADAM_STEP)
    delta = -ADAM_LR * (m_hat / (jnp.sqrt(v_hat) + ADAM_EPS) + ADAM_WD * w)
    return delta, m, v


def _adamw_large(w, g, m, v, name):
    rows, cols = w.shape
    tr = rows // 4

    def body(w_ref, g_ref, m_ref, v_ref, d_ref, mo_ref, vo_ref):
        d_ref[...], mo_ref[...], vo_ref[...] = _adamw_math(w_ref[...], g_ref[...], m_ref[...], v_ref[...])

    blk = pl.BlockSpec((tr, cols), lambda i: (i, 0))
    out = jax.ShapeDtypeStruct(w.shape, F32)
    return pl.pallas_call(body, name=name + "_adamw", grid=(4,), in_specs=[blk] * 4, out_specs=[blk] * 3, out_shape=[out] * 3,
                          compiler_params=_params(dimension_semantics=("parallel",)))(w, g, m, v)


def _adamw_small(ws, gs, ms, vs):
    n = len(ws)

    def body(*refs):
        for k in range(n):
            w_ref, g_ref, m_ref, v_ref = (refs[q * n + k] for q in range(4))
            d_ref, mo_ref, vo_ref = (refs[(4 + q) * n + k] for q in range(3))
            d_ref[...], mo_ref[...], vo_ref[...] = _adamw_math(w_ref[...], g_ref[...], m_ref[...], v_ref[...])

    out = [jax.ShapeDtypeStruct(w.shape, F32) for w in ws]
    res = pl.pallas_call(body, name="small_adamw", in_specs=[VMEM] * (4 * n), out_specs=[VMEM] * (3 * n), out_shape=out * 3,
                         compiler_params=_params())(*ws, *gs, *ms, *vs)
    return res[:n], res[n:2 * n], res[2 * n:]


WEIGHTS = ["norm_mix", "w_in", "w_pool_grp", "pool_scale", "w_pool_out", "conv_w", "conv_b", "w_rg_a", "b_rg_a", "w_rg_x",
           "b_rg_x", "lru_lambda", "w_rnn_out", "w_o", "norm_ffn", "w_ffn_in", "w_ffn_out", "norm_final"]
VEC_ITEMS = ["norm_mix", "norm_ffn", "norm_final", "pool_scale", "conv_b", "lru_lambda", "b_rg_a", "b_rg_x"]
MAT_ITEMS = ["w_pool_grp", "w_rg_a", "w_rg_x"]


def _as2d(name, a):
    if name in MAT_ITEMS:
        return a.reshape(-1, HEAD, HEAD)
    if name == "conv_w":
        return a.reshape(CONV_WIDTH, -1)
    return a.reshape(1, -1)


def kernel(x, norm_mix, w_in, w_pool_grp, pool_scale, w_pool_out, conv_w, conv_b, w_rg_a, b_rg_a, w_rg_x, b_rg_x, lru_lambda, w_rnn_out, w_o, norm_ffn, w_ffn_in, w_ffn_out, norm_final, loss_target, m_norm_mix, m_w_in, m_w_pool_grp, m_pool_scale, m_w_pool_out, m_conv_w, m_conv_b, m_w_rg_a, m_b_rg_a, m_w_rg_x, m_b_rg_x, m_lru_lambda, m_w_rnn_out, m_w_o, m_norm_ffn, m_w_ffn_in, m_w_ffn_out, m_norm_final, v_norm_mix, v_w_in, v_w_pool_grp, v_pool_scale, v_w_pool_out, v_conv_w, v_conv_b, v_w_rg_a, v_b_rg_a, v_w_rg_x, v_b_rg_x, v_lru_lambda, v_w_rnn_out, v_w_o, v_norm_ffn, v_w_ffn_in, v_w_ffn_out, v_norm_final):
    given = dict(locals())
    w = {name: given[name] for name in WEIGHTS}
    m = {name: given["m_" + name] for name in WEIGHTS}
    v = {name: given["v_" + name] for name in WEIGHTS}
    chip, c = _place()

    place = jnp.stack([chip, c]).astype(jnp.int32)
    conv_cols = w["conv_w"].shape[-1]
    conv_w_mine = lax.dynamic_update_slice_in_dim(jnp.zeros((CONV_WIDTH, D_RNN), F32), w["conv_w"][0], chip * conv_cols, axis=1)
    full = {name: _cast_into_whole(w[name][0], name, place) for name in LARGE}
    small = {name: _as2d(name, w[name]) for name in WEIGHTS if name not in LARGE and name != "conv_w"}
    sq_cols, grad_x, grads = _step(x[0], loss_target[0], small, full, conv_w_mine, place)
    loss = 0.5 / D_MODEL * jnp.sum(sq_cols)
    grads["conv_w"] = lax.dynamic_slice_in_dim(grads["conv_w"], chip * conv_cols, conv_cols, axis=1)

    delta, new_m, new_v = {}, {}, {}
    for name in LARGE:
        delta[name], new_m[name], new_v[name] = _adamw_large(w[name][0], grads[name], m[name][0], v[name][0], name)
    small_names = [name for name in WEIGHTS if name not in LARGE]
    flat = lambda d: [d[name].reshape(grads[name].shape) for name in small_names]
    ds, mo, vo = _adamw_small(flat(w), [grads[name] for name in small_names], flat(m), flat(v))
    for k, name in enumerate(small_names):
        delta[name], new_m[name], new_v[name] = ds[k], mo[k], vo[k]

    shaped = lambda d: [d[name].reshape(w[name].shape) for name in WEIGHTS]
    return (loss, grad_x[None], *shaped(grads), *shaped(delta), *shaped(new_m), *shaped(new_v))
```

```python
import functools
import math

import jax
import jax.numpy as jnp
from jax import lax
from jax.experimental import pallas as pl
from jax.experimental.pallas import tpu as pltpu

F32 = jnp.float32
BF16 = jnp.bfloat16

D_MODEL = 1024
D_POOL = 512
N_POOL_GROUPS = 4
D_RNN = 1024
N_RNN_HEADS = 8
HEAD = 128
CONV_WIDTH = 4
LRU_C = 8.0
D_FF = 2816
D_IN = D_POOL + 2 * D_RNN + 2 * D_MODEL
NORM_EPS = 1e-6
COL_RNN = D_POOL // HEAD
COL_GATE = (D_POOL + D_RNN) // HEAD

ADAM_LR = 0.001
ADAM_B1 = 0.9
ADAM_B2 = 0.999
ADAM_EPS = 1e-08
ADAM_WD = 0.01
ADAM_STEP = 10

N_CHIPS = 4
N_DEV = 8
MESH = pl.DeviceIdType.MESH
ANY = pl.BlockSpec(memory_space=pl.ANY)
VMEM = pl.BlockSpec(memory_space=pltpu.VMEM)
VMEM_LIMIT_BYTES = 60 * 1024 * 1024
SUBLANES = 8
POOL_HALO = 16
CHUNK = 1024

GELU_C = math.sqrt(2.0 / math.pi)
GELU_A = 0.044715


def _params(**kw):
    return pltpu.CompilerParams(vmem_limit_bytes=VMEM_LIMIT_BYTES, **kw)


def _sigmoid(x):
    return 0.5 * jnp.tanh(0.5 * x) + 0.5


def _log1p(y):
    u = 1.0 + y
    d = u - 1.0
    return jnp.where(d == 0.0, y, jnp.log(u) * (y / jnp.where(d == 0.0, 1.0, d)))


def _gelu_parts(x):
    x2 = x * x
    th = jnp.tanh(GELU_C * (x + GELU_A * x * x2))
    g = 0.5 * x * (1.0 + th)
    dg = 0.5 * (1.0 + th) + 0.5 * x * (1.0 - th * th) * GELU_C * (1.0 + 3.0 * GELU_A * x2)
    return g, dg


def _dot(a, b):
    return jnp.dot(a, b, preferred_element_type=F32)


def _dot_nt(a, b):
    return lax.dot_general(a, b, (((1,), (1,)), ((), ())), preferred_element_type=F32)


def _dot_tn(a, b):
    return lax.dot_general(a, b, (((0,), (0,)), ((), ())), preferred_element_type=F32)


def _rms_scale(xv):
    return lax.rsqrt(jnp.mean(xv * xv, axis=-1, keepdims=True) + NORM_EPS)


def _rms_bwd(dy, xv, g):
    r = _rms_scale(xv)
    xh = xv * r
    dyg = dy * g
    dx = r * (dyg - xh * jnp.mean(dyg * xh, axis=-1, keepdims=True))
    return dx, dy * xh


class _Job:
    def __init__(self, inputs, out_shapes, aliases, sem_shape, start, finish):
        self.inputs, self.out_shapes, self.aliases, self.sem_shape = list(inputs), list(out_shapes), dict(aliases), sem_shape
        self.start, self.finish = start, finish


def _pallas(body, operands, *, name, grid, in_specs, out_specs, out_shape, scratch_shapes=(), semantics=None, jobs=(),
            prefetch=None):
    n_in, n_out, n_scr = len(in_specs), len(out_specs), len(scratch_shapes)
    n_pre = 0 if prefetch is None else 1
    job_in = [a for job in jobs for a in job.inputs]
    job_out = [s for job in jobs for s in job.out_shapes]
    aliases, i0, o0 = {}, n_pre + n_in, n_out
    for job in jobs:
        aliases.update({i0 + i: o0 + o for i, o in job.aliases.items()})
        i0, o0 = i0 + len(job.inputs), o0 + len(job.out_shapes)

    def whole(*refs):
        pre, refs = refs[:n_pre], refs[n_pre:]
        ins, j_ins = refs[:n_in], refs[n_in:n_in + len(job_in)]
        outs = refs[n_in + len(job_in):][:n_out]
        j_outs = refs[n_in + len(job_in) + n_out:][:len(job_out)]
        rest = refs[n_in + len(job_in) + n_out + len(job_out):]
        scr, sems = rest[:n_scr], rest[n_scr:]

        def run(phase):
            i, o = 0, 0
            for k, job in enumerate(jobs):
                getattr(job, phase)(j_ins[i:i + len(job.inputs)], j_outs[o:o + len(job.out_shapes)], sems[2 * k], sems[2 * k + 1])
                i, o = i + len(job.inputs), o + len(job.out_shapes)

        def at(step_of, phase):
            if not jobs:
                return
            if not grid:
                run(phase)
                return
            cond = functools.reduce(jnp.logical_and, [pl.program_id(d) == step_of(d) for d in range(len(grid))])
            pl.when(cond)(functools.partial(run, phase))

        at(lambda d: 0, "start")
        body(*pre, *ins, *outs, *scr)
        at(lambda d: grid[d] - 1, "finish")

    layout = dict(grid=grid, in_specs=list(in_specs) + [ANY] * len(job_in), out_specs=list(out_specs) + [ANY] * len(job_out),
                  scratch_shapes=list(scratch_shapes) + [pltpu.SemaphoreType.DMA(job.sem_shape) for job in jobs for _ in range(2)])
    if prefetch is not None:
        layout = dict(grid_spec=pltpu.PrefetchScalarGridSpec(num_scalar_prefetch=1, **layout))
    res = pl.pallas_call(
        whole, name=name, out_shape=list(out_shape) + job_out, input_output_aliases=aliases,
        compiler_params=_params(dimension_semantics=semantics, has_side_effects=bool(jobs)), **layout,
    )(*([] if prefetch is None else [prefetch]), *operands, *job_in)
    per_job, o = [], n_out
    for job in jobs:
        per_job.append(res[o:o + len(job.out_shapes)])
        o += len(job.out_shapes)
    return res[:n_out], per_job


def _run_jobs(jobs, name):
    return _pallas(lambda: None, [], name=name, grid=(), in_specs=[], out_specs=[], out_shape=[], jobs=jobs)[1]


NORM_ROWS = 256


def _norm_rows(x_ref, g_ref, h_ref):
    g = g_ref[...]

    def rows(i, carry):
        r = pl.ds(pl.multiple_of(i * NORM_ROWS, NORM_ROWS), NORM_ROWS)
        xv = x_ref[r, :]
        h_ref[r, :] = (xv * _rms_scale(xv) * g).astype(BF16)
        return carry

    lax.fori_loop(0, x_ref.shape[0] // NORM_ROWS, rows, 0)


def _norm_matmul(x, g, w, *, tm, tn, name, jobs=()):
    T, K = x.shape
    N = w.shape[1]

    def body(x_ref, g_ref, w_ref, o_ref, h_ref):
        @pl.when(pl.program_id(1) == 0)
        def _():
            _norm_rows(x_ref, g_ref, h_ref)

        o_ref[...] = _dot(h_ref[...], w_ref[...])

    return _pallas(
        body, (x, g, w), name=name, grid=(T // tm, N // tn),
        in_specs=[pl.BlockSpec((tm, K), lambda i, j: (i, 0)), pl.BlockSpec((1, K), lambda i, j: (0, 0)),
                  pl.BlockSpec((K, tn), lambda i, j: (0, j))],
        out_specs=[pl.BlockSpec((tm, tn), lambda i, j: (i, j)), pl.BlockSpec((tm, K), lambda i, j: (i, 0))],
        out_shape=[jax.ShapeDtypeStruct((T, N), F32), jax.ShapeDtypeStruct((T, K), BF16)],
        semantics=("parallel", "arbitrary"), jobs=jobs)


def _ffn_in(x2, g, w, *, tm, tn):
    T, K = x2.shape
    nb = D_FF // tn

    def body(x_ref, g_ref, wg_ref, wu_ref, dup_ref, dgate_ref, act_ref, h_ref):
        @pl.when(pl.program_id(1) == 0)
        def _():
            _norm_rows(x_ref, g_ref, h_ref)

        h = h_ref[...]
        gate = _dot(h, wg_ref[...])
        up = _dot(h, wu_ref[...])
        s = _sigmoid(gate)
        silu = gate * s
        dup_ref[...] = silu.astype(BF16)
        dgate_ref[...] = (up * (s + silu * (1.0 - s))).astype(BF16)
        act_ref[...] = (silu * up).astype(BF16)

    blk = pl.BlockSpec((tm, tn), lambda i, j: (i, j))
    return pl.pallas_call(
        body, name="ffn_in", grid=(T // tm, nb),
        in_specs=[pl.BlockSpec((tm, K), lambda i, j: (i, 0)), pl.BlockSpec((1, K), lambda i, j: (0, 0)),
                  pl.BlockSpec((K, tn), lambda i, j: (0, j)), pl.BlockSpec((K, tn), lambda i, j: (0, j + nb))],
        out_specs=[blk, blk, blk, pl.BlockSpec((tm, K), lambda i, j: (i, 0))],
        out_shape=[jax.ShapeDtypeStruct((T, D_FF), BF16), jax.ShapeDtypeStruct((T, D_FF), BF16),
                   jax.ShapeDtypeStruct((T, D_FF), BF16), jax.ShapeDtypeStruct((T, K), BF16)],
        compiler_params=_params(dimension_semantics=("parallel", "arbitrary")),
    )(x2, g, w, w)


def _branch_mix(pm, z, w_pool_out, w_rnn_out, proj, *, tm, tn):
    T = pm.shape[0]
    col_gp = (D_POOL + 2 * D_RNN) // tn
    col_gr = col_gp + D_MODEL // tn

    def body(pm_ref, z_ref, wp_ref, wr_ref, gp_ref, gr_ref, by_gp_ref, by_gr_ref, sp_ref, sr_ref, mix_ref):
        yp = _dot(pm_ref[...], wp_ref[...])
        yr = _dot(z_ref[...], wr_ref[...])
        sp, sr = _sigmoid(gp_ref[...]), _sigmoid(gr_ref[...])
        by_gp_ref[...] = (yp * sp * (1.0 - sp)).astype(BF16)
        by_gr_ref[...] = (yr * sr * (1.0 - sr)).astype(BF16)
        sp_ref[...] = sp.astype(BF16)
        sr_ref[...] = sr.astype(BF16)
        mix_ref[...] = (sp * yp + sr * yr).astype(BF16)

    blk = pl.BlockSpec((tm, tn), lambda i, j: (i, j))
    out = jax.ShapeDtypeStruct((T, D_MODEL), BF16)
    return pl.pallas_call(
        body, name="branch_mix", grid=(T // tm, D_MODEL // tn),
        in_specs=[pl.BlockSpec((tm, D_POOL), lambda i, j: (i, 0)), pl.BlockSpec((tm, D_RNN), lambda i, j: (i, 0)),
                  pl.BlockSpec((D_POOL, tn), lambda i, j: (0, j)), pl.BlockSpec((D_RNN, tn), lambda i, j: (0, j)),
                  pl.BlockSpec((tm, tn), lambda i, j: (i, col_gp + j)), pl.BlockSpec((tm, tn), lambda i, j: (i, col_gr + j))],
        out_specs=[blk] * 5, out_shape=[out] * 5,
        compiler_params=_params(dimension_semantics=("parallel", "parallel")),
    )(pm, z, w_pool_out, w_rnn_out, proj, proj)


def _out_proj_residual(mix, w_o, x, *, tm):
    T = x.shape[0]

    def body(mix_ref, w_ref, x_ref, o_ref):
        o_ref[...] = x_ref[...] + _dot(mix_ref[...], w_ref[...])

    row = pl.BlockSpec((tm, D_MODEL), lambda i: (i, 0))
    return pl.pallas_call(
        body, name="out_proj_residual", grid=(T // tm,),
        in_specs=[row, pl.BlockSpec((D_MODEL, D_MODEL), lambda i: (0, 0)), row],
        out_specs=row, out_shape=jax.ShapeDtypeStruct((T, D_MODEL), F32),
        compiler_params=_params(dimension_semantics=("parallel",)),
    )(mix, w_o, x)


def _ffn_out_loss(act, w, x2, g3, target, *, tm):
    T = x2.shape[0]

    def body(act_ref, w_ref, x2_ref, g_ref, t_ref, dx_ref, dxb_ref, sq_ref, dg_ref):
        @pl.when(pl.program_id(0) == 0)
        def _():
            sq_ref[...] = jnp.zeros_like(sq_ref)
            dg_ref[...] = jnp.zeros_like(dg_ref)

        x3 = x2_ref[...] + _dot(act_ref[...], w_ref[...])
        g = g_ref[...]
        err = x3 * _rms_scale(x3) * g - t_ref[...]
        sq_ref[...] += jnp.sum(err * err, axis=0, keepdims=True)
        dx, dgp = _rms_bwd(err * (1.0 / D_MODEL), x3, g)
        dg_ref[...] += jnp.sum(dgp, axis=0, keepdims=True)
        dx_ref[...] = dx
        dxb_ref[...] = dx.astype(BF16)

    row = pl.BlockSpec((tm, D_MODEL), lambda i: (i, 0))
    vec = pl.BlockSpec((1, D_MODEL), lambda i: (0, 0))
    return pl.pallas_call(
        body, name="ffn_out_loss", grid=(T // tm,),
        in_specs=[pl.BlockSpec((tm, D_FF), lambda i: (i, 0)), pl.BlockSpec((D_FF, D_MODEL), lambda i: (0, 0)), row, vec, row],
        out_specs=[row, row, vec, vec],
        out_shape=[jax.ShapeDtypeStruct((T, D_MODEL), F32), jax.ShapeDtypeStruct((T, D_MODEL), BF16),
                   jax.ShapeDtypeStruct((1, D_MODEL), F32), jax.ShapeDtypeStruct((1, D_MODEL), F32)],
        compiler_params=_params(dimension_semantics=("arbitrary",)),
    )(act, w, x2, g3, target)


def _ffn_out_bwd(dx3b, w, act_by_gate, act_by_up, *, tm, tn):
    T = dx3b.shape[0]

    def body(dx_ref, w_ref, by_gate_ref, by_up_ref, dgate_ref, dup_ref):
        dact = _dot_nt(dx_ref[...], w_ref[...])
        dgate_ref[...] = (dact * by_gate_ref[...].astype(F32)).astype(BF16)
        dup_ref[...] = (dact * by_up_ref[...].astype(F32)).astype(BF16)

    blk = pl.BlockSpec((tm, tn), lambda i, j: (i, j))
    return pl.pallas_call(
        body, name="ffn_out_bwd", grid=(T // tm, D_FF // tn),
        in_specs=[pl.BlockSpec((tm, D_MODEL), lambda i, j: (i, 0)), pl.BlockSpec((tn, D_MODEL), lambda i, j: (j, 0)), blk, blk],
        out_specs=[blk, blk],
        out_shape=[jax.ShapeDtypeStruct((T, D_FF), BF16), jax.ShapeDtypeStruct((T, D_FF), BF16)],
        compiler_params=_params(dimension_semantics=("parallel", "parallel")),
    )(dx3b, w, act_by_gate, act_by_up)


def _ffn_in_bwd(dgate, dup, w, dx3, x2, g2, *, tm, jobs=()):
    T = x2.shape[0]

    def body(dgate_ref, dup_ref, w_ref, dx3_ref, x2_ref, g_ref, dx_ref, dxb_ref, dg_ref):
        @pl.when(pl.program_id(0) == 0)
        def _():
            dg_ref[...] = jnp.zeros_like(dg_ref)

        dh = _dot_nt(dgate_ref[...], w_ref[:, :D_FF]) + _dot_nt(dup_ref[...], w_ref[:, D_FF:])
        dxn, dgp = _rms_bwd(dh, x2_ref[...], g_ref[...])
        dx = dx3_ref[...] + dxn
        dg_ref[...] += jnp.sum(dgp, axis=0, keepdims=True)
        dx_ref[...] = dx
        dxb_ref[...] = dx.astype(BF16)

    row = pl.BlockSpec((tm, D_MODEL), lambda i: (i, 0))
    wide = pl.BlockSpec((tm, D_FF), lambda i: (i, 0))
    vec = pl.BlockSpec((1, D_MODEL), lambda i: (0, 0))
    return _pallas(
        body, (dgate, dup, w, dx3, x2, g2), name="ffn_in_bwd", grid=(T // tm,),
        in_specs=[wide, wide, pl.BlockSpec((D_MODEL, 2 * D_FF), lambda i: (0, 0)), row, row, vec],
        out_specs=[row, row, vec],
        out_shape=[jax.ShapeDtypeStruct((T, D_MODEL), F32), jax.ShapeDtypeStruct((T, D_MODEL), BF16),
                   jax.ShapeDtypeStruct((1, D_MODEL), F32)],
        semantics=("arbitrary",), jobs=jobs)


def _out_proj_bwd(dx2b, w_o, mix_by, *, tm, tn, jobs=()):
    T = dx2b.shape[0]

    def body(dx_ref, w_ref, *refs):
        dmix = _dot_nt(dx_ref[...], w_ref[...])
        for by_ref, d_ref in zip(refs[:4], refs[4:]):
            d_ref[...] = (dmix * by_ref[...].astype(F32)).astype(BF16)

    blk = pl.BlockSpec((tm, tn), lambda i, j: (i, j))
    out = jax.ShapeDtypeStruct((T, D_MODEL), BF16)
    return _pallas(
        body, (dx2b, w_o, *mix_by), name="out_proj_bwd", grid=(T // tm, D_MODEL // tn),
        in_specs=[pl.BlockSpec((tm, D_MODEL), lambda i, j: (i, 0)), pl.BlockSpec((tn, D_MODEL), lambda i, j: (j, 0))] + [blk] * 4,
        out_specs=[blk] * 4, out_shape=[out] * 4, semantics=("parallel", "parallel"), jobs=jobs)


def _branch_bwd(dyp, dyr, w_pool_out, w_rnn_out, *, tm):
    T = dyp.shape[0]

    def body(dyp_ref, dyr_ref, wp_ref, wr_ref, dpm_ref, dz_ref):
        dpm_ref[...] = _dot_nt(dyp_ref[...], wp_ref[...])
        dz_ref[...] = _dot_nt(dyr_ref[...], wr_ref[...])

    row = pl.BlockSpec((tm, D_MODEL), lambda i: (i, 0))
    return pl.pallas_call(
        body, name="branch_bwd", grid=(T // tm,),
        in_specs=[row, row, pl.BlockSpec((D_POOL, D_MODEL), lambda i: (0, 0)), pl.BlockSpec((D_RNN, D_MODEL), lambda i: (0, 0))],
        out_specs=[pl.BlockSpec((tm, D_POOL), lambda i: (i, 0)), pl.BlockSpec((tm, D_RNN), lambda i: (i, 0))],
        out_shape=[jax.ShapeDtypeStruct((T, D_POOL), F32), jax.ShapeDtypeStruct((T, D_RNN), F32)],
        compiler_params=_params(dimension_semantics=("parallel",)),
    )(dyp, dyr, w_pool_out, w_rnn_out)


def _in_proj_bwd(segs, w, dx2, x, g1, *, tm, jobs=()):
    T = x.shape[0]
    widths = [s.shape[1] for s in segs]
    offs = [sum(widths[:k]) for k in range(len(widths))]
    n = len(segs)

    def body(*refs):
        seg_refs, (w_ref, dx2_ref, x_ref, g_ref, dx_ref, dg_ref) = refs[:n], refs[n:]

        @pl.when(pl.program_id(0) == 0)
        def _():
            dg_ref[...] = jnp.zeros_like(dg_ref)

        dh = _dot_nt(seg_refs[0][...], w_ref[:, offs[0]:offs[0] + widths[0]])
        for k in range(1, n):
            dh += _dot_nt(seg_refs[k][...], w_ref[:, offs[k]:offs[k] + widths[k]])
        dxn, dgp = _rms_bwd(dh, x_ref[...], g_ref[...])
        dg_ref[...] += jnp.sum(dgp, axis=0, keepdims=True)
        dx_ref[...] = dx2_ref[...] + dxn

    row = pl.BlockSpec((tm, D_MODEL), lambda i: (i, 0))
    vec = pl.BlockSpec((1, D_MODEL), lambda i: (0, 0))
    return _pallas(
        body, (*segs, w, dx2, x, g1), name="in_proj_bwd", grid=(T // tm,),
        in_specs=[pl.BlockSpec((tm, wd), lambda i: (i, 0)) for wd in widths]
        + [pl.BlockSpec((D_MODEL, D_IN), lambda i: (0, 0)), row, row, vec],
        out_specs=[row, vec],
        out_shape=[jax.ShapeDtypeStruct((T, D_MODEL), F32), jax.ShapeDtypeStruct((1, D_MODEL), F32)],
        semantics=("arbitrary",), jobs=jobs)


def _weight_grad(a, segs, *, tm, tn, name, jobs=None):
    T, M = a.shape
    nblk = [s.shape[1] // tn for s in segs]
    first = [sum(nblk[:k]) for k in range(len(segs))]
    n = len(segs)

    def body(a_ref, *refs):
        seg_refs, o_ref = refs[:n], refs[n]
        j = pl.program_id(1)
        for k in range(n):
            @pl.when((j >= first[k]) & (j < first[k] + nblk[k]))
            def _(k=k):
                o_ref[...] = _dot_tn(a_ref[...], seg_refs[k][...])

    def seg_spec(k):
        return pl.BlockSpec((T, tn), lambda i, j: (0, jnp.clip(j - first[k], 0, nblk[k] - 1)))

    (grad,), results = _pallas(
        body, (a, *segs), name=name, grid=(M // tm, sum(nblk)),
        in_specs=[pl.BlockSpec((T, tm), lambda i, j: (0, i))] + [seg_spec(k) for k in range(n)],
        out_specs=[pl.BlockSpec((tm, tn), lambda i, j: (i, j))],
        out_shape=[jax.ShapeDtypeStruct((M, sum(nblk) * tn), F32)],
        semantics=("parallel", "arbitrary"), jobs=jobs or ())
    return grad if jobs is None else (grad, results)


def _pad_front(dst, src, halo):
    dst[pl.ds(0, halo), :] = jnp.zeros((halo, src.shape[1]), F32)

    def fill(i, carry):
        r0 = pl.multiple_of(i * CHUNK, CHUNK)
        dst[pl.ds(r0 + halo, CHUNK), :] = src[pl.ds(r0, CHUNK), :]
        return carry

    lax.fori_loop(0, src.shape[0] // CHUNK, fill, 0)


def _shift_rows(v, k):
    return pltpu.roll(v, k % v.shape[0], axis=0)


def _window_sums(xs, direction):
    s2 = xs + _shift_rows(xs, direction)
    s4 = s2 + _shift_rows(s2, 2 * direction)
    s8 = s4 + _shift_rows(s4, 4 * direction)
    s16 = s8 + _shift_rows(s8, 8 * direction)
    return s2, s4, s8, s16


def _select_window(g, sums):
    s2, s4, s8, s16 = sums
    return jnp.where(g == 0, s2, jnp.where(g == 1, s4, jnp.where(g == 2, s8, s16)))


def _pool_count(g, start, rows):
    t = start + lax.broadcasted_iota(jnp.int32, (rows, 1), 0)
    return jnp.minimum(t + 1, jnp.left_shift(2, g)).astype(F32)


def _pool_fwd(proj, w_grp, scale):
    T = proj.shape[0]
    nchunk = T // CHUNK

    def body(u_ref, w_ref, s_ref, o_ref, upad):
        g = pl.program_id(0)
        _pad_front(upad, u_ref, POOL_HALO)
        w = w_ref[...].astype(BF16)
        scale_row = s_ref[...]

        def chunk(i, carry):
            r0 = pl.multiple_of(i * CHUNK, CHUNK)
            xs = upad[pl.ds(r0, CHUNK + POOL_HALO), :]
            win = _select_window(g, _window_sums(xs, 1))[POOL_HALO:]
            pooled = win / _pool_count(g, r0, CHUNK) - xs[POOL_HALO:]
            o_ref[pl.ds(r0, CHUNK), :] = (_dot(pooled.astype(BF16), w) * scale_row).astype(BF16)
            return carry

        lax.fori_loop(0, nchunk, chunk, 0)

    return pl.pallas_call(
        body, name="pool_fwd", grid=(N_POOL_GROUPS,),
        in_specs=[pl.BlockSpec((T, HEAD), lambda g: (0, g)), pl.BlockSpec((None, HEAD, HEAD), lambda g: (g, 0, 0)),
                  pl.BlockSpec((1, HEAD), lambda g: (0, g))],
        out_specs=pl.BlockSpec((T, HEAD), lambda g: (0, g)),
        out_shape=jax.ShapeDtypeStruct((T, D_POOL), BF16),
        scratch_shapes=[pltpu.VMEM((T + POOL_HALO, HEAD), F32)],
        compiler_params=_params(dimension_semantics=("parallel",)),
    )(proj, w_grp, scale)


def _pool_bwd(proj, dpm, w_grp, scale, jobs=()):
    T = proj.shape[0]
    nchunk = T // CHUNK

    def body(u_ref, dpm_ref, w_ref, s_ref, du_ref, dw_ref, ds_ref, upad, zpad, dpool):
        g = pl.program_id(0)
        _pad_front(upad, u_ref, POOL_HALO)
        zpad[pl.ds(T, POOL_HALO), :] = jnp.zeros((POOL_HALO, HEAD), F32)
        dw_ref[...] = jnp.zeros_like(dw_ref)
        ds_ref[...] = jnp.zeros_like(ds_ref)
        w = w_ref[...].astype(BF16)
        scale_row = s_ref[...]

        def chunk(i, carry):
            r0 = pl.multiple_of(i * CHUNK, CHUNK)
            xs = upad[pl.ds(r0, CHUNK + POOL_HALO), :]
            cnt = _pool_count(g, r0, CHUNK)
            pooled = (_select_window(g, _window_sums(xs, 1))[POOL_HALO:] / cnt - xs[POOL_HALO:]).astype(BF16)
            mixed = _dot(pooled, w)
            d = dpm_ref[pl.ds(r0, CHUNK), :]
            ds_ref[...] += jnp.sum(d * mixed, axis=0, keepdims=True)
            dmixed = (d * scale_row).astype(BF16)
            dw_ref[...] += _dot_tn(pooled, dmixed)
            dp = _dot_nt(dmixed, w)
            dpool[pl.ds(r0, CHUNK), :] = dp
            zpad[pl.ds(r0, CHUNK), :] = dp / cnt
            return carry

        lax.fori_loop(0, nchunk, chunk, 0)

        def chunk2(i, carry):
            r0 = pl.multiple_of(i * CHUNK, CHUNK)
            zs = zpad[pl.ds(r0, CHUNK + POOL_HALO), :]
            win = _select_window(g, _window_sums(zs, -1))[:CHUNK]
            du_ref[pl.ds(r0, CHUNK), :] = (win - dpool[pl.ds(r0, CHUNK), :]).astype(BF16)
            return carry

        lax.fori_loop(0, nchunk, chunk2, 0)

    col = pl.BlockSpec((T, HEAD), lambda g: (0, g))
    return _pallas(
        body, (proj, dpm, w_grp, scale), name="pool_bwd", grid=(N_POOL_GROUPS,),
        in_specs=[col, col, pl.BlockSpec((None, HEAD, HEAD), lambda g: (g, 0, 0)), pl.BlockSpec((1, HEAD), lambda g: (0, g))],
        out_specs=[col, pl.BlockSpec((None, HEAD, HEAD), lambda g: (g, 0, 0)), pl.BlockSpec((1, HEAD), lambda g: (0, g))],
        out_shape=[jax.ShapeDtypeStruct((T, D_POOL), BF16), jax.ShapeDtypeStruct((N_POOL_GROUPS, HEAD, HEAD), F32),
                   jax.ShapeDtypeStruct((1, D_POOL), F32)],
        scratch_shapes=[pltpu.VMEM((T + POOL_HALO, HEAD), F32), pltpu.VMEM((T + POOL_HALO, HEAD), F32), pltpu.VMEM((T, HEAD), F32)],
        semantics=("parallel",), jobs=jobs)


def _conv_taps(xs, cw):
    v = cw[CONV_WIDTH - 1] * xs[SUBLANES:]
    for k in range(CONV_WIDTH - 1):
        v += cw[k] * _shift_rows(xs, CONV_WIDTH - 1 - k)[SUBLANES:]
    return v


def _tap_rows(cw_ref):
    return [cw_ref[k:k + 1, :] for k in range(CONV_WIDTH)]


def _softplus_neg(lam):
    return jnp.maximum(-lam, 0.0) + _log1p(jnp.exp(-jnp.abs(lam)))


def _lru_gates(v, wa, ba, wx, bx, sp):
    vb = v.astype(BF16)
    ra = _sigmoid(_dot(vb, wa) + ba)
    ix = _sigmoid(_dot(vb, wx) + bx)
    log_a = -LRU_C * ra * sp
    a = jnp.exp(log_a)
    sq = jnp.sqrt(-jnp.tanh(log_a) * (a * a + 1.0))
    return ra, ix, a, sq


def _row_bcast(v, r):
    return jnp.broadcast_to(v[r:r + 1, :], v.shape)


TILE_BLOCK = 128


def _scan_in_tiles(coef, coef_shift, A_out, B, T, direction):
    order = list(range(SUBLANES)) if direction == 1 else list(range(SUBLANES - 1, -1, -1))
    tiles = min(TILE_BLOCK, T // SUBLANES)
    for base in range(0, T, tiles * SUBLANES):
        def rows(r, base=base):
            return pl.ds(base + r, tiles, stride=SUBLANES)

        A, Bv = coef[rows(order[0] + coef_shift), :], B[rows(order[0]), :]
        A_out[rows(order[0]), :] = A
        for r in order[1:]:
            a = coef[rows(r + coef_shift), :]
            Bv = a * Bv + B[rows(r), :]
            A = a * A
            A_out[rows(r), :] = A
            B[rows(r), :] = Bv


TILES_PER_STEP = 8


def _carry_tiles(A_s, B_s, out, ntile, direction):
    out_row = SUBLANES - 1 if direction == 1 else 0

    def step(k, carry):
        for j in range(TILES_PER_STEP):
            t = k * TILES_PER_STEP + j
            r0 = pl.multiple_of((t if direction == 1 else ntile - 1 - t) * SUBLANES, SUBLANES)
            A, B = A_s[pl.ds(r0, SUBLANES), :], B_s[pl.ds(r0, SUBLANES), :]
            out[pl.ds(r0, SUBLANES), :] = A * carry + B
            carry = _row_bcast(A, out_row) * carry + _row_bcast(B, out_row)
        return carry

    lax.fori_loop(0, ntile // TILES_PER_STEP, step, jnp.zeros((SUBLANES, HEAD), F32))


def _rnn_fwd(proj, conv_w, conv_b, w_a, b_a, w_x, b_x, lam, jobs=()):
    T = proj.shape[0]
    nchunk = T // CHUNK
    ntile = T // SUBLANES

    def body(u_ref, ug_ref, cw_ref, cb_ref, wa_ref, ba_ref, wx_ref, bx_ref, lam_ref,
             h_ref, z_ref, v_ref, ra_ref, ix_ref, a_ref, sq_ref, upad, a_s, b_s):
        _pad_front(upad, u_ref, SUBLANES)
        cw, cb = _tap_rows(cw_ref), cb_ref[...]
        wa, wx = wa_ref[...].astype(BF16), wx_ref[...].astype(BF16)
        ba, bx = ba_ref[...], bx_ref[...]
        sp = _softplus_neg(lam_ref[...])

        def chunk(i, carry):
            rows = pl.ds(pl.multiple_of(i * CHUNK, CHUNK), CHUNK)
            v = _conv_taps(upad[pl.ds(pl.multiple_of(i * CHUNK, CHUNK), CHUNK + SUBLANES), :], cw) + cb
            ra, ix, a, sq = _lru_gates(v, wa, ba, wx, bx, sp)
            v_ref[rows, :], ra_ref[rows, :], ix_ref[rows, :], a_ref[rows, :], sq_ref[rows, :] = v, ra, ix, a, sq
            a_s[rows, :], b_s[rows, :] = a, sq * ix * v
            return carry

        lax.fori_loop(0, nchunk, chunk, 0)
        _scan_in_tiles(a_s, 0, a_s, b_s, T, 1)
        _carry_tiles(a_s, b_s, h_ref, ntile, 1)

        def chunk3(i, carry):
            r0 = pl.multiple_of(i * CHUNK, CHUNK)
            gl, _ = _gelu_parts(ug_ref[pl.ds(r0, CHUNK), :])
            z_ref[pl.ds(r0, CHUNK), :] = (h_ref[pl.ds(r0, CHUNK), :] * gl).astype(BF16)
            return carry

        lax.fori_loop(0, nchunk, chunk3, 0)

    col = pl.BlockSpec((T, HEAD), lambda h: (0, h))
    vec = pl.BlockSpec((1, HEAD), lambda h: (0, h))
    mat = pl.BlockSpec((None, HEAD, HEAD), lambda h: (h, 0, 0))
    return _pallas(
        body, (proj, proj, conv_w, conv_b, w_a, b_a, w_x, b_x, lam), name="rnn_fwd", grid=(N_RNN_HEADS,),
        in_specs=[pl.BlockSpec((T, HEAD), lambda h: (0, COL_RNN + h)), pl.BlockSpec((T, HEAD), lambda h: (0, COL_GATE + h)),
                  pl.BlockSpec((CONV_WIDTH, HEAD), lambda h: (0, h)), vec, mat, vec, mat, vec, vec],
        out_specs=[col] * 7,
        out_shape=[jax.ShapeDtypeStruct((T, D_RNN), F32), jax.ShapeDtypeStruct((T, D_RNN), BF16)]
        + [jax.ShapeDtypeStruct((T, D_RNN), F32)] * 5,
        scratch_shapes=[pltpu.VMEM((T + SUBLANES, HEAD), F32), pltpu.VMEM((T, HEAD), F32), pltpu.VMEM((T, HEAD), F32)],
        semantics=("parallel",), jobs=jobs)


def _rnn_bwd(proj, hr, dz, gates, conv_w, w_a, w_x, lam, jobs=()):
    T = proj.shape[0]
    nchunk = T // CHUNK
    ntile = T // SUBLANES

    def body(u_ref, ug_ref, h_ref, dz_ref, v_ref, ra_ref, ix_ref, a_ref, sq_ref, cw_ref, wa_ref, wx_ref, lam_ref,
             du_ref, dug_ref, dwa_ref, dwx_ref, dba_ref, dbx_ref, dlam_ref, dcb_ref, dcw_ref,
             upad, hpad, apad, g_s, dvpad, ga_s):
        zero_tile = jnp.zeros((SUBLANES, HEAD), F32)
        _pad_front(upad, u_ref, SUBLANES)
        _pad_front(hpad, h_ref, SUBLANES)
        apad[pl.ds(T, SUBLANES), :] = zero_tile
        dvpad[pl.ds(T, SUBLANES), :] = zero_tile
        for ref in (dwa_ref, dwx_ref, dba_ref, dbx_ref, dlam_ref, dcb_ref, dcw_ref):
            ref[...] = jnp.zeros_like(ref)
        cw = _tap_rows(cw_ref)
        wa, wx = wa_ref[...].astype(BF16), wx_ref[...].astype(BF16)
        lam_row = lam_ref[...]
        sp = _softplus_neg(lam_row)

        def chunk(i, carry):
            rows = pl.ds(pl.multiple_of(i * CHUNK, CHUNK), CHUNK)
            apad[rows, :] = a_ref[rows, :]
            gl, dgl = _gelu_parts(ug_ref[rows, :])
            d = dz_ref[rows, :]
            g_s[rows, :] = d * gl
            dug_ref[rows, :] = (d * h_ref[rows, :] * dgl).astype(BF16)
            return carry

        lax.fori_loop(0, nchunk, chunk, 0)

        _scan_in_tiles(apad, 1, ga_s, g_s, T, -1)
        _carry_tiles(ga_s, g_s, g_s, ntile, -1)

        def chunk3(i, carry):
            r0 = pl.multiple_of(i * CHUNK, CHUNK)
            rows = pl.ds(r0, CHUNK)
            g = g_s[rows, :]
            h_prev = _shift_rows(hpad[pl.ds(r0, CHUNK + SUBLANES), :], 1)[SUBLANES:]
            v, ra, ix, sq, a = v_ref[rows, :], ra_ref[rows, :], ix_ref[rows, :], sq_ref[rows, :], a_ref[rows, :]
            d_sq = g * ix * v
            d_ix = g * sq * v
            d_la = a * g * h_prev - d_sq * a * a / sq
            dlam_ref[...] += jnp.sum(d_la * ra, axis=0, keepdims=True)
            d_pa = d_la * (-LRU_C) * sp * ra * (1.0 - ra)
            d_px = d_ix * ix * (1.0 - ix)
            vb, d_pab, d_pxb = v.astype(BF16), d_pa.astype(BF16), d_px.astype(BF16)
            dwa_ref[...] += _dot_tn(vb, d_pab)
            dwx_ref[...] += _dot_tn(vb, d_pxb)
            dba_ref[...] += jnp.sum(d_pa, axis=0, keepdims=True)
            dbx_ref[...] += jnp.sum(d_px, axis=0, keepdims=True)
            dv = g * sq * ix + _dot_nt(d_pab, wa) + _dot_nt(d_pxb, wx)
            dvpad[rows, :] = dv
            dcb_ref[...] += jnp.sum(dv, axis=0, keepdims=True)
            xs = upad[pl.ds(r0, CHUNK + SUBLANES), :]
            for k in range(CONV_WIDTH):
                u_k = _shift_rows(xs, CONV_WIDTH - 1 - k)[SUBLANES:] if k < CONV_WIDTH - 1 else xs[SUBLANES:]
                dcw_ref[k:k + 1, :] += jnp.sum(dv * u_k, axis=0, keepdims=True)
            return carry

        lax.fori_loop(0, nchunk, chunk3, 0)
        dlam_ref[...] = dlam_ref[...] * (LRU_C * _sigmoid(-lam_row))

        def chunk4(i, carry):
            r0 = pl.multiple_of(i * CHUNK, CHUNK)
            dvs = dvpad[pl.ds(r0, CHUNK + SUBLANES), :]
            du = cw[CONV_WIDTH - 1] * dvs[:CHUNK]
            for k in range(CONV_WIDTH - 1):
                du += cw[k] * _shift_rows(dvs, -(CONV_WIDTH - 1 - k))[:CHUNK]
            du_ref[pl.ds(r0, CHUNK), :] = du.astype(BF16)
            return carry

        lax.fori_loop(0, nchunk, chunk4, 0)

    col = pl.BlockSpec((T, HEAD), lambda h: (0, h))
    vec = pl.BlockSpec((1, HEAD), lambda h: (0, h))
    mat = pl.BlockSpec((None, HEAD, HEAD), lambda h: (h, 0, 0))
    taps = pl.BlockSpec((CONV_WIDTH, HEAD), lambda h: (0, h))
    vec_out = jax.ShapeDtypeStruct((1, D_RNN), F32)
    mat_out = jax.ShapeDtypeStruct((N_RNN_HEADS, HEAD, HEAD), F32)
    seq = pltpu.VMEM((T, HEAD), F32)
    seq_pad = pltpu.VMEM((T + SUBLANES, HEAD), F32)
    return _pallas(
        body, (proj, proj, hr, dz, *gates, conv_w, w_a, w_x, lam), name="rnn_bwd", grid=(N_RNN_HEADS,),
        in_specs=[pl.BlockSpec((T, HEAD), lambda h: (0, COL_RNN + h)), pl.BlockSpec((T, HEAD), lambda h: (0, COL_GATE + h))]
        + [col] * 7 + [taps, mat, mat, vec],
        out_specs=[col, col, mat, mat, vec, vec, vec, vec, taps],
        out_shape=[jax.ShapeDtypeStruct((T, D_RNN), BF16), jax.ShapeDtypeStruct((T, D_RNN), BF16), mat_out, mat_out,
                   vec_out, vec_out, vec_out, vec_out, jax.ShapeDtypeStruct((CONV_WIDTH, D_RNN), F32)],
        scratch_shapes=[seq_pad, seq_pad, seq_pad, seq, seq_pad, seq],
        semantics=("parallel",), jobs=jobs)


GROUP_FFN_OUT = ["w_ffn_out"]
GROUP_FFN_IN = ["w_ffn_in"]
GROUP_MIX = ["w_o", "w_pool_out", "w_rnn_out"]
GROUP_IN = ["w_in"]


def _step(x, target, s, full, conv_w, place):
    T = x.shape[0]
    tall, mid, low = min(T, 2048), min(T, 1024), min(T, 512)
    full = dict(full)

    def gathered(names, results):
        full.update(zip(names, results))

    early =["w_pool_out", "w_rnn_out", "w_o", "w_ffn_out"]
    (proj, h1), (res,) = _norm_matmul(x, s["norm_mix"], full["w_in"], tm=tall, tn=512, name="in_proj", jobs=[_gather_job(full, early)])
    gathered(early, res)
    pm = _pool_fwd(proj, s["w_pool_grp"], s["pool_scale"])
    (hr, z, *gates), (res,) = _rnn_fwd(proj, conv_w, s["conv_b"], s["w_rg_a"], s["b_rg_a"], s["w_rg_x"], s["b_rg_x"],
                                       s["lru_lambda"], jobs=[_gather_job(full, ["w_ffn_in"])])
    gathered(["w_ffn_in"], res)
    *mix_by, mix = _branch_mix(pm, z, full["w_pool_out"], full["w_rnn_out"], proj, tm=tall, tn=256)
    x2 = _out_proj_residual(mix, full["w_o"], x, tm=mid)
    act_by_up, act_by_gate, act, h2 = _ffn_in(x2, s["norm_ffn"], full["w_ffn_in"], tm=tall, tn=256)
    dx3, dx3b, sq_cols, g_norm_final = _ffn_out_loss(act, full["w_ffn_out"], x2, s["norm_final"], target, tm=low)

    g = {"norm_final": g_norm_final}

    def chip_sums(names, from_sibling):
        sums = {name: _chip_sum(name, g[name], got, place) for name, got in zip(names, from_sibling)}
        return {name: v[0] for name, v in sums.items()}, {name: v[1] for name, v in sums.items()}

    def final_sums(names, sums, from_chips):
        return {name: _final_sum(name, sums[name], got, place) for name, got in zip(names, from_chips)}

    dgate, dup = _ffn_out_bwd(dx3b, full["w_ffn_out"], act_by_gate, act_by_up, tm=tall, tn=256)
    g["w_ffn_out"] = _weight_grad(act, [dx3b], tm=256, tn=D_MODEL, name="w_ffn_out_grad")
    (dx2, dx2b, g["norm_ffn"]), (res,) = _ffn_in_bwd(dgate, dup, full["w_ffn_in"], dx3, x2, s["norm_ffn"], tm=low,
                                                     jobs=[_sibling_job(g, GROUP_FFN_OUT)])
    sums_ffn, sums_ffn_bf16 = chip_sums(GROUP_FFN_OUT, res)
    g["w_ffn_in"], (res,) = _weight_grad(h2, [dgate, dup], tm=D_MODEL, tn=256, name="w_ffn_in_grad",
                                         jobs=[_chips_job(sums_ffn_bf16, GROUP_FFN_OUT)])
    shards_ffn = final_sums(GROUP_FFN_OUT, sums_ffn, res)
    (dgp, dgr, dyp, dyr), (res,) = _out_proj_bwd(dx2b, full["w_o"], mix_by, tm=tall, tn=256,
                                                 jobs=[_sibling_job(g, GROUP_FFN_IN)])
    sums_ffn, sums_ffn_bf16 = chip_sums(GROUP_FFN_IN, res)
    g["w_o"] = _weight_grad(mix, [dx2b], tm=D_MODEL, tn=256, name="w_o_grad")
    dpm, dz = _branch_bwd(dyp, dyr, full["w_pool_out"], full["w_rnn_out"], tm=mid)
    g["w_pool_out"] = _weight_grad(pm, [dyp], tm=D_POOL, tn=256, name="w_pool_out_grad")
    g["w_rnn_out"] = _weight_grad(z, [dyr], tm=D_RNN, tn=256, name="w_rnn_out_grad")
    (dupool, g["w_pool_grp"], g["pool_scale"]), (res,) = _pool_bwd(proj, dpm, s["w_pool_grp"], s["pool_scale"],
                                                                   jobs=[_sibling_job(g, GROUP_MIX)])
    sums_mix, sums_mix_bf16 = chip_sums(GROUP_MIX, res)
    ((durnn, dugate, g["w_rg_a"], g["w_rg_x"], g["b_rg_a"], g["b_rg_x"], g["lru_lambda"], g["conv_b"], g["conv_w"]),
     (res,)) = _rnn_bwd(proj, hr, dz, gates, conv_w, s["w_rg_a"], s["w_rg_x"], s["lru_lambda"],
                        jobs=[_chips_job(sums_ffn_bf16, GROUP_FFN_IN)])
    shards_ffn.update(final_sums(GROUP_FFN_IN, sums_ffn, res))
    segs = [dupool, durnn, dugate, dgp, dgr]
    ffn = GROUP_FFN_OUT + GROUP_FFN_IN
    g["w_in"], (res, joined) = _weight_grad(h1, segs, tm=D_MODEL, tn=256, name="w_in_grad",
                                           jobs=[_chips_job(sums_mix_bf16, GROUP_MIX), _join_job(shards_ffn, ffn)])
    grads = dict(zip(ffn, joined))
    shards = final_sums(GROUP_MIX, sums_mix, res)
    (res,) = _run_jobs([_sibling_job(g, GROUP_IN)], "w_in_exchange_sibling")
    sums_in, sums_in_bf16 = chip_sums(GROUP_IN, res)
    (grad_x, g["norm_mix"]), (res,) = _in_proj_bwd(segs, full["w_in"], dx2, x, s["norm_mix"], tm=low,
                                                  jobs=[_chips_job(sums_in_bf16, GROUP_IN)])
    shards.update(final_sums(GROUP_IN, sums_in, res))

    vec_rows = [g[name] if name != "pool_scale" else jnp.pad(g[name], ((0, 0), (0, D_MODEL - D_POOL))) for name in VEC_ITEMS]
    vec_rows += [g["conv_w"], sq_cols, jnp.zeros((VEC_ROWS - len(VEC_ITEMS) - CONV_WIDTH - 1, D_MODEL), F32)]
    vec = jnp.concatenate(vec_rows, axis=0).reshape(VEC_ROWS, N_DEV, HEAD).transpose(1, 0, 2)
    mat = jnp.concatenate([g[name].reshape(-1, HEAD) for name in MAT_ITEMS], axis=0).reshape(N_DEV, -1, HEAD)
    (vec, mat), (joined,) = _all_reduce_small([vec, mat], jobs=[_join_job(shards, GROUP_MIX + GROUP_IN)])
    grads.update(zip(GROUP_MIX + GROUP_IN, joined))

    vec = vec.transpose(1, 0, 2).reshape(VEC_ROWS, D_MODEL)
    mat = mat.reshape(-1, HEAD)
    for k, name in enumerate(VEC_ITEMS):
        grads[name] = vec[k:k + 1, :s[name].shape[1]]
    grads["conv_w"] = vec[len(VEC_ITEMS):len(VEC_ITEMS) + CONV_WIDTH]
    row = 0
    for name in MAT_ITEMS:
        rows = s[name].shape[0] * HEAD
        grads[name] = mat[row:row + rows]
        row += rows
    return vec[len(VEC_ITEMS) + CONV_WIDTH], grad_x, grads


LARGE = {"w_in": "col", "w_pool_out": "col", "w_rnn_out": "row", "w_o": "row", "w_ffn_in": "col", "w_ffn_out": "row"}
LARGE_SHAPE = {"w_in": (D_MODEL, D_IN), "w_pool_out": (D_POOL, D_MODEL), "w_rnn_out": (D_RNN, D_MODEL),
               "w_o": (D_MODEL, D_MODEL), "w_ffn_in": (D_MODEL, 2 * D_FF), "w_ffn_out": (D_FF, D_MODEL)}


def _place():
    x, y, c = lax.axis_index("x"), lax.axis_index("y"), lax.axis_index("c")
    return 2 * x + y, c


def _chip_device(chip, c):
    return (chip // 2, chip % 2, c)


def _chip_window(ref, kind, shape, chip, half=None):
    K, N = shape
    if kind == "col":
        rows = slice(None) if half is None else pl.ds(half * (K // 2), K // 2)
        return ref.at[rows, pl.ds(chip * (N // N_CHIPS), N // N_CHIPS)]
    ks = K // N_CHIPS
    if half is None:
        return ref.at[pl.ds(chip * ks, ks), :]
    return ref.at[pl.ds(chip * ks + half * (ks // 2), ks // 2), :]


def _row_half(ref, half):
    rows = ref.shape[0] // 2
    return ref.at[pl.ds(half * rows, rows), :]


def _remote(win_src, win_dst, send_sems, recv_sems, idx, to):
    return pltpu.make_async_remote_copy(src_ref=win_src, dst_ref=win_dst, send_sem=send_sems.at[idx], recv_sem=recv_sems.at[idx],
                                        device_id=to, device_id_type=MESH)


def _gather_job(full, names, conv_w_full=None):
    n = len(names)
    cw_cols = D_RNN // N_CHIPS

    def windows(refs, chip, half):
        return [_chip_window(refs[k], LARGE[name], LARGE_SHAPE[name], chip, half) for k, name in enumerate(names)]

    def ici_copies(refs, send_sems, recv_sems, src_chip, dst_chip, c, r):
        wins = windows(refs, src_chip, c)
        if conv_w_full is not None:
            wins.append(refs[n].at[:, pl.ds(src_chip * cw_cols, cw_cols)])
        return [_remote(win, win, send_sems, recv_sems, (k, r), _chip_device(dst_chip, c)) for k, win in enumerate(wins)]

    def forwards(refs, send_sems, recv_sems, src_chip, half, to_core, chip, r):
        return [_remote(win, win, send_sems, recv_sems, (k, 3 + r), _chip_device(chip, to_core))
                for k, win in enumerate(windows(refs, src_chip, half))]

    def start(ins, outs, send_sems, recv_sems):
        chip, c = _place()
        for r in range(3):
            for cp in ici_copies(outs, send_sems, recv_sems, chip, chip ^ (r + 1), c, r):
                cp.start()

    def finish(ins, outs, send_sems, recv_sems):
        chip, c = _place()
        for r in range(3):
            for cp in ici_copies(outs, send_sems, recv_sems, chip ^ (r + 1), chip, c, r):
                cp.wait_recv()
            for cp in forwards(outs, send_sems, recv_sems, chip ^ (r + 1), c, 1 - c, chip, r):
                cp.start()
        for r in range(3):
            for cp in forwards(outs, send_sems, recv_sems, chip ^ (r + 1), 1 - c, c, chip, r):
                cp.wait_recv()
            for cp in ici_copies(outs, send_sems, recv_sems, chip, chip ^ (r + 1), c, r):
                cp.wait_send()
            for cp in forwards(outs, send_sems, recv_sems, chip ^ (r + 1), c, 1 - c, chip, r):
                cp.wait_send()

    arrays = [full[name] for name in names] + ([conv_w_full] if conv_w_full is not None else [])
    return _Job(arrays, [jax.ShapeDtypeStruct(a.shape, a.dtype) for a in arrays], {k: k for k in range(len(arrays))},
                (len(arrays), 6), start, finish)


def _core_halves(ref, kind, shape, c):
    return [_chip_window(ref, kind, shape, chip, c) for chip in range(N_CHIPS)]


def _sibling_job(grads, names):
    def start(ins, outs, send_sems, recv_sems):
        chip, c = _place()
        for k, name in enumerate(names):
            kind, shape = LARGE[name], LARGE_SHAPE[name]
            if kind == "col":
                pairs = [(_row_half(ins[k], 1 - c), outs[k])]
            else:
                rows = shape[0] // N_DEV
                pairs = [(win, outs[k].at[pl.ds(j * rows, rows), :]) for j, win in enumerate(_core_halves(ins[k], kind, shape, 1 - c))]
            for src, dst in pairs:
                _remote(src, dst, send_sems, recv_sems, k, _chip_device(chip, 1 - c)).start()

    def finish(ins, outs, send_sems, recv_sems):
        chip, c = _place()
        for k in range(len(names)):
            _remote(outs[k], outs[k], send_sems, recv_sems, k, _chip_device(chip, 1 - c)).wait()

    return _Job([grads[name] for name in names],
                [jax.ShapeDtypeStruct((LARGE_SHAPE[name][0] // 2, LARGE_SHAPE[name][1]), F32) for name in names], {},
                (len(names),), start, finish)


def _chip_sum(name, g, got, place):
    kind, (K, N) = LARGE[name], LARGE_SHAPE[name]
    rows = K // N_DEV
    piece_cols = N // N_CHIPS

    def body(place_ref, g_ref, got_ref, o_ref, ob_ref):
        total = g_ref[...] + got_ref[...]
        ob_ref[...] = total.astype(BF16)
        if kind == "col":
            for chip in range(N_CHIPS):
                @pl.when(place_ref[0] == chip)
                def _(chip=chip):
                    o_ref[...] = total[:, chip * piece_cols:(chip + 1) * piece_cols]
        else:
            @pl.when(pl.program_id(0) == place_ref[0])
            def _():
                o_ref[...] = total

    if kind == "col":
        mine = pl.BlockSpec((rows, N), lambda j, place_ref: (j + N_CHIPS * place_ref[1], 0))
        own = pl.BlockSpec((rows, piece_cols), lambda j, place_ref: (j, 0))
    else:
        mine = pl.BlockSpec((rows, N), lambda j, place_ref: (2 * j + place_ref[1], 0))
        own = pl.BlockSpec((rows, N), lambda j, place_ref: (0, 0))
    blk = pl.BlockSpec((rows, N), lambda j, place_ref: (j, 0))
    return pl.pallas_call(
        body, name=name + "_chip_sum",
        grid_spec=pltpu.PrefetchScalarGridSpec(num_scalar_prefetch=1, grid=(N_CHIPS,), in_specs=[mine, blk], out_specs=[own, blk]),
        out_shape=[jax.ShapeDtypeStruct(_piece_shape(name), F32), jax.ShapeDtypeStruct((K // 2, N), BF16)],
        compiler_params=_params(dimension_semantics=("arbitrary",)),
    )(place, g, got)


def _piece(ref, kind, shape, chip):
    K, N = shape
    if kind == "col":
        return ref.at[:, pl.ds(chip * (N // N_CHIPS), N // N_CHIPS)]
    return ref.at[pl.ds(chip * (K // N_DEV), K // N_DEV), :]


def _piece_shape(name):
    kind, (K, N) = LARGE[name], LARGE_SHAPE[name]
    return (K // 2, N // N_CHIPS) if kind == "col" else (K // N_DEV, N)


def _chips_job(sums, names):
    def copies(ins, outs, send_sems, recv_sems):
        chip, c = _place()
        return [_remote(_piece(ins[k], LARGE[name], LARGE_SHAPE[name], chip ^ (r + 1)), outs[k].at[r], send_sems, recv_sems, (k, r),
                        _chip_device(chip ^ (r + 1), c)) for k, name in enumerate(names) for r in range(3)]

    def start(*refs):
        for cp in copies(*refs):
            cp.start()

    def finish(*refs):
        for cp in copies(*refs):
            cp.wait()

    return _Job([sums[name] for name in names], [jax.ShapeDtypeStruct((3,) + _piece_shape(name), BF16) for name in names], {},
                (len(names), 3), start, finish)


def _final_sum(name, chip_sum, got, place):
    rows, cols = _piece_shape(name)

    def body(place_ref, s_ref, got_ref, o_ref):
        o_ref[...] = ((s_ref[...] + got_ref[0].astype(F32)) + got_ref[1].astype(F32)) + got_ref[2].astype(F32)

    mine = pl.BlockSpec((rows, cols), lambda i, place_ref: (0, 0))
    return pl.pallas_call(
        body, name=name + "_final_sum",
        grid_spec=pltpu.PrefetchScalarGridSpec(
            num_scalar_prefetch=1, grid=(1,), in_specs=[mine, pl.BlockSpec((3, rows, cols), lambda i, place_ref: (0, 0, 0))],
            out_specs=pl.BlockSpec((rows, cols), lambda i, place_ref: (place_ref[1], 0))),
        out_shape=jax.ShapeDtypeStruct((2 * rows, cols), F32),
        compiler_params=_params(dimension_semantics=("arbitrary",)),
    )(place, chip_sum, got)


def _join_job(shards, names):
    def half_copy(outs, send_sems, recv_sems, k, mine):
        chip, c = _place()
        win = _row_half(outs[k], c if mine else 1 - c)
        return _remote(win, win, send_sems, recv_sems, k, _chip_device(chip, 1 - c))

    def start(ins, outs, send_sems, recv_sems):
        for k in range(len(names)):
            half_copy(outs, send_sems, recv_sems, k, True).start()

    def finish(ins, outs, send_sems, recv_sems):
        for k in range(len(names)):
            half_copy(outs, send_sems, recv_sems, k, True).wait_send()
            half_copy(outs, send_sems, recv_sems, k, False).wait_recv()

    arrays = [shards[name] for name in names]
    return _Job(arrays, [jax.ShapeDtypeStruct(a.shape, F32) for a in arrays], {k: k for k in range(len(arrays))},
                (len(arrays),), start, finish)


VEC_ROWS = 16


def _all_reduce_small(slabs, jobs=()):
    n = len(slabs)

    def body(*refs):
        in_refs, out_refs, got_refs = refs[:n], refs[n:2 * n], refs[2 * n:3 * n]
        send_sems, recv_sems = refs[3 * n:]
        x, y, c = lax.axis_index("x"), lax.axis_index("y"), lax.axis_index("c")
        me = 4 * x + 2 * y + c

        def remote(src, dst, k, phase, r):
            other = me ^ r
            return pltpu.make_async_remote_copy(src_ref=src, dst_ref=dst, send_sem=send_sems.at[k, phase, r],
                                                recv_sem=recv_sems.at[k, phase, r],
                                                device_id=(other // 4, (other // 2) % 2, other % 2), device_id_type=MESH)

        scatter = [remote(in_refs[k].at[me ^ r], got_refs[k].at[r], k, 0, r) for r in range(1, N_DEV) for k in range(n)]
        for cp in scatter:
            cp.start()
        for cp in scatter:
            cp.wait()
        for k in range(n):
            total = in_refs[k][me]
            for r in range(1, N_DEV):
                total = total + got_refs[k][r]
            out_refs[k][me] = total
        gather = [remote(out_refs[k].at[me], out_refs[k].at[me], k, 1, r) for r in range(1, N_DEV) for k in range(n)]
        for cp in gather:
            cp.start()
        for r in range(1, N_DEV):
            for k in range(n):
                remote(out_refs[k].at[me ^ r], out_refs[k].at[me ^ r], k, 1, r).wait_recv()
        for cp in gather:
            cp.wait_send()

    return _pallas(
        body, slabs, name="all_reduce_small", grid=(), in_specs=[VMEM] * n, out_specs=[VMEM] * n,
        out_shape=[jax.ShapeDtypeStruct(s.shape, F32) for s in slabs],
        scratch_shapes=[pltpu.VMEM(s.shape, F32) for s in slabs]
        + [pltpu.SemaphoreType.DMA((n, 2, N_DEV)), pltpu.SemaphoreType.DMA((n, 2, N_DEV))], jobs=jobs)


def _cast_into_whole(w, name, place):
    rows, cols = w.shape
    tr = rows // 2

    def body(place_ref, w_ref, o_ref):
        o_ref[...] = w_ref[...].astype(BF16)

    if LARGE[name] == "col":
        window = pl.BlockSpec((tr, cols), lambda i, place_ref: (i, place_ref[0]))
    else:
        window = pl.BlockSpec((tr, cols), lambda i, place_ref: (2 * place_ref[0] + i, 0))
    return pl.pallas_call(
        body, name=name + "_cast",
        grid_spec=pltpu.PrefetchScalarGridSpec(num_scalar_prefetch=1, grid=(2,),
                                               in_specs=[pl.BlockSpec((tr, cols), lambda i, place_ref: (i, 0))], out_specs=window),
        out_shape=jax.ShapeDtypeStruct(LARGE_SHAPE[name], BF16),
        compiler_params=_params(dimension_semantics=("parallel",)))(place, w)


def _cast_many_into_whole(shards, place, jobs):
    names = list(shards)
    n = len(names)

    def body(place_ref, *refs):
        for w_ref, o_ref in zip(refs[:n], refs[n:]):
            o_ref[...] = w_ref[...].astype(BF16)

    def window(name):
        rows, cols = shards[name].shape
        if LARGE[name] == "col":
            return pl.BlockSpec((rows // 2, cols), lambda i, place_ref: (i, place_ref[0]))
        return pl.BlockSpec((rows // 2, cols), lambda i, place_ref: (2 * place_ref[0] + i, 0))

    def half(name):
        rows, cols = shards[name].shape
        return pl.BlockSpec((rows // 2, cols), lambda i, place_ref: (i, 0))

    return _pallas(body, [shards[name] for name in names], name="cast_weights", grid=(2,),
                   in_specs=[half(name) for name in names], out_specs=[window(name) for name in names],
                   out_shape=[jax.ShapeDtypeStruct(LARGE_SHAPE[name], BF16) for name in names],
                   semantics=("arbitrary",), jobs=jobs, prefetch=place)


def _adamw_math(w, g, m, v):
    m = ADAM_B1 * m + (1.0 - ADAM_B1) * g
    v = ADAM_B2 * v + (1.0 - ADAM_B2) * (g * g)
    m_hat = m / (1.0 - ADAM_B1 ** ADAM_STEP)
    v_hat = v / (1.0 - ADAM_B2 ** ADAM_STEP)
    delta = -ADAM_LR * (m_hat / (jnp.sqrt(v_hat) + ADAM_EPS) + ADAM_WD * w)
    return delta, m, v


def _adamw_large(w, g, m, v, name):
    rows, cols = w.shape
    steps = 8
    tr = rows // steps

    def body(w_ref, g_ref, m_ref, v_ref, d_ref, mo_ref, vo_ref):
        d_ref[...], mo_ref[...], vo_ref[...] = _adamw_math(w_ref[...], g_ref[...], m_ref[...], v_ref[...])

    blk = pl.BlockSpec((tr, cols), lambda i: (i, 0))
    out = jax.ShapeDtypeStruct(w.shape, F32)
    return pl.pallas_call(body, name=name + "_adamw", grid=(steps,), in_specs=[blk] * 4, out_specs=[blk] * 3, out_shape=[out] * 3,
                          compiler_params=_params(dimension_semantics=("parallel",)))(w, g, m, v)


def _adamw_small(ws, gs, ms, vs):
    n = len(ws)

    def body(*refs):
        for k in range(n):
            w_ref, g_ref, m_ref, v_ref = (refs[q * n + k] for q in range(4))
            d_ref, mo_ref, vo_ref = (refs[(4 + q) * n + k] for q in range(3))
            d_ref[...], mo_ref[...], vo_ref[...] = _adamw_math(w_ref[...], g_ref[...], m_ref[...], v_ref[...])

    out = [jax.ShapeDtypeStruct(w.shape, F32) for w in ws]
    res = pl.pallas_call(body, name="small_adamw", in_specs=[VMEM] * (4 * n), out_specs=[VMEM] * (3 * n), out_shape=out * 3,
                         compiler_params=_params())(*ws, *gs, *ms, *vs)
    return res[:n], res[n:2 * n], res[2 * n:]


WEIGHTS = ["norm_mix", "w_in", "w_pool_grp", "pool_scale", "w_pool_out", "conv_w", "conv_b", "w_rg_a", "b_rg_a", "w_rg_x",
           "b_rg_x", "lru_lambda", "w_rnn_out", "w_o", "norm_ffn", "w_ffn_in", "w_ffn_out", "norm_final"]
VEC_ITEMS = ["norm_mix", "norm_ffn", "norm_final", "pool_scale", "conv_b", "lru_lambda", "b_rg_a", "b_rg_x"]
MAT_ITEMS = ["w_pool_grp", "w_rg_a", "w_rg_x"]


def _as2d(name, a):
    if name in MAT_ITEMS:
        return a.reshape(-1, HEAD, HEAD)
    if name == "conv_w":
        return a.reshape(CONV_WIDTH, -1)
    return a.reshape(1, -1)


def kernel(x, norm_mix, w_in, w_pool_grp, pool_scale, w_pool_out, conv_w, conv_b, w_rg_a, b_rg_a, w_rg_x, b_rg_x, lru_lambda, w_rnn_out, w_o, norm_ffn, w_ffn_in, w_ffn_out, norm_final, loss_target, m_norm_mix, m_w_in, m_w_pool_grp, m_pool_scale, m_w_pool_out, m_conv_w, m_conv_b, m_w_rg_a, m_b_rg_a, m_w_rg_x, m_b_rg_x, m_lru_lambda, m_w_rnn_out, m_w_o, m_norm_ffn, m_w_ffn_in, m_w_ffn_out, m_norm_final, v_norm_mix, v_w_in, v_w_pool_grp, v_pool_scale, v_w_pool_out, v_conv_w, v_conv_b, v_w_rg_a, v_b_rg_a, v_w_rg_x, v_b_rg_x, v_lru_lambda, v_w_rnn_out, v_w_o, v_norm_ffn, v_w_ffn_in, v_w_ffn_out, v_norm_final):
    given = dict(locals())
    w = {name: given[name] for name in WEIGHTS}
    m = {name: given["m_" + name] for name in WEIGHTS}
    v = {name: given["v_" + name] for name in WEIGHTS}
    chip, c = _place()

    place = jnp.stack([chip, c]).astype(jnp.int32)
    conv_cols = w["conv_w"].shape[-1]
    conv_w_mine = lax.dynamic_update_slice_in_dim(jnp.zeros((CONV_WIDTH, D_RNN), F32), w["conv_w"][0], chip * conv_cols, axis=1)
    w_in_mine = _cast_into_whole(w["w_in"][0], "w_in", place)
    later = [name for name in LARGE if name != "w_in"]
    casts, ((w_in_full, conv_w_full),) = _cast_many_into_whole(
        {name: w[name][0] for name in later}, place, jobs=[_gather_job({"w_in": w_in_mine}, ["w_in"], conv_w_mine)])
    full = dict(zip(later, casts), w_in=w_in_full)
    small = {name: _as2d(name, w[name]) for name in WEIGHTS if name not in LARGE and name != "conv_w"}
    sq_cols, grad_x, grads = _step(x[0], loss_target[0], small, full, conv_w_full, place)
    loss = 0.5 / D_MODEL * jnp.sum(sq_cols)
    grads["conv_w"] = lax.dynamic_slice_in_dim(grads["conv_w"], chip * conv_cols, conv_cols, axis=1)

    delta, new_m, new_v = {}, {}, {}
    for name in LARGE:
        delta[name], new_m[name], new_v[name] = _adamw_large(w[name][0], grads[name], m[name][0], v[name][0], name)
    small_names = [name for name in WEIGHTS if name not in LARGE]
    flat = lambda d: [d[name].reshape(grads[name].shape) for name in small_names]
    ds, mo, vo = _adamw_small(flat(w), [grads[name] for name in small_names], flat(m), flat(v))
    for k, name in enumerate(small_names):
        delta[name], new_m[name], new_v[name] = ds[k], mo[k], vo[k]

    shaped = lambda d: [d[name].reshape(w[name].shape) for name in WEIGHTS]
    return (loss, grad_x[None], *shaped(grads), *shaped(delta), *shaped(new_m), *shaped(new_v))
```

```python
import functools
import math

import jax
import jax.numpy as jnp
from jax import lax
from jax.experimental import pallas as pl
from jax.experimental.pallas import tpu as pltpu

F32 = jnp.float32
BF16 = jnp.bfloat16

D_MODEL = 1024
D_POOL = 512
N_POOL_GROUPS = 4
D_RNN = 1024
N_RNN_HEADS = 8
HEAD = 128
CONV_WIDTH = 4
LRU_C = 8.0
D_FF = 2816
D_IN = D_POOL + 2 * D_RNN + 2 * D_MODEL
NORM_EPS = 1e-6
COL_RNN = D_POOL // HEAD
COL_GATE = (D_POOL + D_RNN) // HEAD

ADAM_LR = 0.001
ADAM_B1 = 0.9
ADAM_B2 = 0.999
ADAM_EPS = 1e-08
ADAM_WD = 0.01
ADAM_STEP = 10

N_CHIPS = 4
N_DEV = 8
MESH = pl.DeviceIdType.MESH
ANY = pl.BlockSpec(memory_space=pl.ANY)
VMEM = pl.BlockSpec(memory_space=pltpu.VMEM)
VMEM_LIMIT_BYTES = 60 * 1024 * 1024
SUBLANES = 8
POOL_HALO = 16
CHUNK = 1024

GELU_C = math.sqrt(2.0 / math.pi)
GELU_A = 0.044715


def _params(**kw):
    return pltpu.CompilerParams(vmem_limit_bytes=VMEM_LIMIT_BYTES, **kw)


def _sigmoid(x):
    return 0.5 * jnp.tanh(0.5 * x) + 0.5


def _log1p(y):
    u = 1.0 + y
    d = u - 1.0
    return jnp.where(d == 0.0, y, jnp.log(u) * (y / jnp.where(d == 0.0, 1.0, d)))


def _gelu_parts(x):
    x2 = x * x
    th = jnp.tanh(GELU_C * (x + GELU_A * x * x2))
    g = 0.5 * x * (1.0 + th)
    dg = 0.5 * (1.0 + th) + 0.5 * x * (1.0 - th * th) * GELU_C * (1.0 + 3.0 * GELU_A * x2)
    return g, dg


def _dot(a, b):
    return jnp.dot(a, b, preferred_element_type=F32)


def _dot_nt(a, b):
    return lax.dot_general(a, b, (((1,), (1,)), ((), ())), preferred_element_type=F32)


def _dot_tn(a, b):
    return lax.dot_general(a, b, (((0,), (0,)), ((), ())), preferred_element_type=F32)


def _rms_scale(xv):
    return lax.rsqrt(jnp.mean(xv * xv, axis=-1, keepdims=True) + NORM_EPS)


def _rms_bwd(dy, xv, g):
    r = _rms_scale(xv)
    xh = xv * r
    dyg = dy * g
    dx = r * (dyg - xh * jnp.mean(dyg * xh, axis=-1, keepdims=True))
    return dx, dy * xh


class _Job:
    def __init__(self, inputs, out_shapes, aliases, sem_shape, start, finish):
        self.inputs, self.out_shapes, self.aliases, self.sem_shape = list(inputs), list(out_shapes), dict(aliases), sem_shape
        self.start, self.finish = start, finish


def _pallas(body, operands, *, name, grid, in_specs, out_specs, out_shape, scratch_shapes=(), semantics=None, jobs=(),
            prefetch=None):
    n_in, n_out, n_scr = len(in_specs), len(out_specs), len(scratch_shapes)
    n_pre = 0 if prefetch is None else 1
    job_in = [a for job in jobs for a in job.inputs]
    job_out = [s for job in jobs for s in job.out_shapes]
    aliases, i0, o0 = {}, n_pre + n_in, n_out
    for job in jobs:
        aliases.update({i0 + i: o0 + o for i, o in job.aliases.items()})
        i0, o0 = i0 + len(job.inputs), o0 + len(job.out_shapes)

    def whole(*refs):
        pre, refs = refs[:n_pre], refs[n_pre:]
        ins, j_ins = refs[:n_in], refs[n_in:n_in + len(job_in)]
        outs = refs[n_in + len(job_in):][:n_out]
        j_outs = refs[n_in + len(job_in) + n_out:][:len(job_out)]
        rest = refs[n_in + len(job_in) + n_out + len(job_out):]
        scr, sems = rest[:n_scr], rest[n_scr:]

        def run(phase):
            i, o = 0, 0
            for k, job in enumerate(jobs):
                getattr(job, phase)(j_ins[i:i + len(job.inputs)], j_outs[o:o + len(job.out_shapes)], sems[2 * k], sems[2 * k + 1])
                i, o = i + len(job.inputs), o + len(job.out_shapes)

        def at(step_of, phase):
            if not jobs:
                return
            if not grid:
                run(phase)
                return
            cond = functools.reduce(jnp.logical_and, [pl.program_id(d) == step_of(d) for d in range(len(grid))])
            pl.when(cond)(functools.partial(run, phase))

        at(lambda d: 0, "start")
        body(*pre, *ins, *outs, *scr)
        at(lambda d: grid[d] - 1, "finish")

    layout = dict(grid=grid, in_specs=list(in_specs) + [ANY] * len(job_in), out_specs=list(out_specs) + [ANY] * len(job_out),
                  scratch_shapes=list(scratch_shapes) + [pltpu.SemaphoreType.DMA(job.sem_shape) for job in jobs for _ in range(2)])
    if prefetch is not None:
        layout = dict(grid_spec=pltpu.PrefetchScalarGridSpec(num_scalar_prefetch=1, **layout))
    res = pl.pallas_call(
        whole, name=name, out_shape=list(out_shape) + job_out, input_output_aliases=aliases,
        compiler_params=_params(dimension_semantics=semantics, has_side_effects=bool(jobs)), **layout,
    )(*([] if prefetch is None else [prefetch]), *operands, *job_in)
    per_job, o = [], n_out
    for job in jobs:
        per_job.append(res[o:o + len(job.out_shapes)])
        o += len(job.out_shapes)
    return res[:n_out], per_job


def _run_jobs(jobs, name):
    return _pallas(lambda: None, [], name=name, grid=(), in_specs=[], out_specs=[], out_shape=[], jobs=jobs)[1]


NORM_ROWS = 256


def _norm_rows(x_ref, g_ref, h_ref):
    g = g_ref[...]

    def rows(i, carry):
        r = pl.ds(pl.multiple_of(i * NORM_ROWS, NORM_ROWS), NORM_ROWS)
        xv = x_ref[r, :]
        h_ref[r, :] = (xv * _rms_scale(xv) * g).astype(BF16)
        return carry

    lax.fori_loop(0, x_ref.shape[0] // NORM_ROWS, rows, 0)


def _norm_matmul(x, g, w, *, tm, tn, name, jobs=()):
    T, K = x.shape
    N = w.shape[1]

    def body(x_ref, g_ref, w_ref, o_ref, h_ref):
        @pl.when(pl.program_id(1) == 0)
        def _():
            _norm_rows(x_ref, g_ref, h_ref)

        o_ref[...] = _dot(h_ref[...], w_ref[...])

    return _pallas(
        body, (x, g, w), name=name, grid=(T // tm, N // tn),
        in_specs=[pl.BlockSpec((tm, K), lambda i, j: (i, 0)), pl.BlockSpec((1, K), lambda i, j: (0, 0)),
                  pl.BlockSpec((K, tn), lambda i, j: (0, j))],
        out_specs=[pl.BlockSpec((tm, tn), lambda i, j: (i, j)), pl.BlockSpec((tm, K), lambda i, j: (i, 0))],
        out_shape=[jax.ShapeDtypeStruct((T, N), F32), jax.ShapeDtypeStruct((T, K), BF16)],
        semantics=("parallel", "arbitrary"), jobs=jobs)


def _ffn_in(x2, g, w, *, tm, tn):
    T, K = x2.shape
    nb = D_FF // tn

    def body(x_ref, g_ref, wg_ref, wu_ref, dup_ref, dgate_ref, act_ref, h_ref):
        @pl.when(pl.program_id(1) == 0)
        def _():
            _norm_rows(x_ref, g_ref, h_ref)

        h = h_ref[...]
        gate = _dot(h, wg_ref[...])
        up = _dot(h, wu_ref[...])
        s = _sigmoid(gate)
        silu = gate * s
        dup_ref[...] = silu.astype(BF16)
        dgate_ref[...] = (up * (s + silu * (1.0 - s))).astype(BF16)
        act_ref[...] = (silu * up).astype(BF16)

    blk = pl.BlockSpec((tm, tn), lambda i, j: (i, j))
    return pl.pallas_call(
        body, name="ffn_in", grid=(T // tm, nb),
        in_specs=[pl.BlockSpec((tm, K), lambda i, j: (i, 0)), pl.BlockSpec((1, K), lambda i, j: (0, 0)),
                  pl.BlockSpec((K, tn), lambda i, j: (0, j)), pl.BlockSpec((K, tn), lambda i, j: (0, j + nb))],
        out_specs=[blk, blk, blk, pl.BlockSpec((tm, K), lambda i, j: (i, 0))],
        out_shape=[jax.ShapeDtypeStruct((T, D_FF), BF16), jax.ShapeDtypeStruct((T, D_FF), BF16),
                   jax.ShapeDtypeStruct((T, D_FF), BF16), jax.ShapeDtypeStruct((T, K), BF16)],
        compiler_params=_params(dimension_semantics=("parallel", "arbitrary")),
    )(x2, g, w, w)


def _branch_mix(pm, z, w_pool_out, w_rnn_out, proj, *, tm, tn):
    T = pm.shape[0]
    col_gp = (D_POOL + 2 * D_RNN) // tn
    col_gr = col_gp + D_MODEL // tn

    def body(pm_ref, z_ref, wp_ref, wr_ref, gp_ref, gr_ref, by_gp_ref, by_gr_ref, sp_ref, sr_ref, mix_ref):
        yp = _dot(pm_ref[...], wp_ref[...])
        yr = _dot(z_ref[...], wr_ref[...])
        sp, sr = _sigmoid(gp_ref[...]), _sigmoid(gr_ref[...])
        by_gp_ref[...] = (yp * sp * (1.0 - sp)).astype(BF16)
        by_gr_ref[...] = (yr * sr * (1.0 - sr)).astype(BF16)
        sp_ref[...] = sp.astype(BF16)
        sr_ref[...] = sr.astype(BF16)
        mix_ref[...] = (sp * yp + sr * yr).astype(BF16)

    blk = pl.BlockSpec((tm, tn), lambda i, j: (i, j))
    out = jax.ShapeDtypeStruct((T, D_MODEL), BF16)
    return pl.pallas_call(
        body, name="branch_mix", grid=(T // tm, D_MODEL // tn),
        in_specs=[pl.BlockSpec((tm, D_POOL), lambda i, j: (i, 0)), pl.BlockSpec((tm, D_RNN), lambda i, j: (i, 0)),
                  pl.BlockSpec((D_POOL, tn), lambda i, j: (0, j)), pl.BlockSpec((D_RNN, tn), lambda i, j: (0, j)),
                  pl.BlockSpec((tm, tn), lambda i, j: (i, col_gp + j)), pl.BlockSpec((tm, tn), lambda i, j: (i, col_gr + j))],
        out_specs=[blk] * 5, out_shape=[out] * 5,
        compiler_params=_params(dimension_semantics=("parallel", "parallel")),
    )(pm, z, w_pool_out, w_rnn_out, proj, proj)


def _out_proj_residual(mix, w_o, x, *, tm):
    T = x.shape[0]

    def body(mix_ref, w_ref, x_ref, o_ref):
        o_ref[...] = x_ref[...] + _dot(mix_ref[...], w_ref[...])

    row = pl.BlockSpec((tm, D_MODEL), lambda i: (i, 0))
    return pl.pallas_call(
        body, name="out_proj_residual", grid=(T // tm,),
        in_specs=[row, pl.BlockSpec((D_MODEL, D_MODEL), lambda i: (0, 0)), row],
        out_specs=row, out_shape=jax.ShapeDtypeStruct((T, D_MODEL), F32),
        compiler_params=_params(dimension_semantics=("parallel",)),
    )(mix, w_o, x)


def _ffn_out_loss(act, w, x2, g3, target, *, tm):
    T = x2.shape[0]

    def body(act_ref, w_ref, x2_ref, g_ref, t_ref, dx_ref, dxb_ref, sq_ref, dg_ref):
        @pl.when(pl.program_id(0) == 0)
        def _():
            sq_ref[...] = jnp.zeros_like(sq_ref)
            dg_ref[...] = jnp.zeros_like(dg_ref)

        x3 = x2_ref[...] + _dot(act_ref[...], w_ref[...])
        g = g_ref[...]
        err = x3 * _rms_scale(x3) * g - t_ref[...]
        sq_ref[...] += jnp.sum(err * err, axis=0, keepdims=True)
        dx, dgp = _rms_bwd(err * (1.0 / D_MODEL), x3, g)
        dg_ref[...] += jnp.sum(dgp, axis=0, keepdims=True)
        dx_ref[...] = dx
        dxb_ref[...] = dx.astype(BF16)

    row = pl.BlockSpec((tm, D_MODEL), lambda i: (i, 0))
    vec = pl.BlockSpec((1, D_MODEL), lambda i: (0, 0))
    return pl.pallas_call(
        body, name="ffn_out_loss", grid=(T // tm,),
        in_specs=[pl.BlockSpec((tm, D_FF), lambda i: (i, 0)), pl.BlockSpec((D_FF, D_MODEL), lambda i: (0, 0)), row, vec, row],
        out_specs=[row, row, vec, vec],
        out_shape=[jax.ShapeDtypeStruct((T, D_MODEL), F32), jax.ShapeDtypeStruct((T, D_MODEL), BF16),
                   jax.ShapeDtypeStruct((1, D_MODEL), F32), jax.ShapeDtypeStruct((1, D_MODEL), F32)],
        compiler_params=_params(dimension_semantics=("arbitrary",)),
    )(act, w, x2, g3, target)


def _ffn_out_bwd(dx3b, w, act_by_gate, act_by_up, *, tm, tn):
    T = dx3b.shape[0]

    def body(dx_ref, w_ref, by_gate_ref, by_up_ref, dgate_ref, dup_ref):
        dact = _dot_nt(dx_ref[...], w_ref[...])
        dgate_ref[...] = (dact * by_gate_ref[...].astype(F32)).astype(BF16)
        dup_ref[...] = (dact * by_up_ref[...].astype(F32)).astype(BF16)

    blk = pl.BlockSpec((tm, tn), lambda i, j: (i, j))
    return pl.pallas_call(
        body, name="ffn_out_bwd", grid=(T // tm, D_FF // tn),
        in_specs=[pl.BlockSpec((tm, D_MODEL), lambda i, j: (i, 0)), pl.BlockSpec((tn, D_MODEL), lambda i, j: (j, 0)), blk, blk],
        out_specs=[blk, blk],
        out_shape=[jax.ShapeDtypeStruct((T, D_FF), BF16), jax.ShapeDtypeStruct((T, D_FF), BF16)],
        compiler_params=_params(dimension_semantics=("parallel", "parallel")),
    )(dx3b, w, act_by_gate, act_by_up)


def _ffn_in_bwd(dgate, dup, w, dx3, x2, g2, *, tm, jobs=()):
    T = x2.shape[0]

    def body(dgate_ref, dup_ref, w_ref, dx3_ref, x2_ref, g_ref, dx_ref, dxb_ref, dg_ref):
        @pl.when(pl.program_id(0) == 0)
        def _():
            dg_ref[...] = jnp.zeros_like(dg_ref)

        dh = _dot_nt(dgate_ref[...], w_ref[:, :D_FF]) + _dot_nt(dup_ref[...], w_ref[:, D_FF:])
        dxn, dgp = _rms_bwd(dh, x2_ref[...], g_ref[...])
        dx = dx3_ref[...] + dxn
        dg_ref[...] += jnp.sum(dgp, axis=0, keepdims=True)
        dx_ref[...] = dx
        dxb_ref[...] = dx.astype(BF16)

    row = pl.BlockSpec((tm, D_MODEL), lambda i: (i, 0))
    wide = pl.BlockSpec((tm, D_FF), lambda i: (i, 0))
    vec = pl.BlockSpec((1, D_MODEL), lambda i: (0, 0))
    return _pallas(
        body, (dgate, dup, w, dx3, x2, g2), name="ffn_in_bwd", grid=(T // tm,),
        in_specs=[wide, wide, pl.BlockSpec((D_MODEL, 2 * D_FF), lambda i: (0, 0)), row, row, vec],
        out_specs=[row, row, vec],
        out_shape=[jax.ShapeDtypeStruct((T, D_MODEL), F32), jax.ShapeDtypeStruct((T, D_MODEL), BF16),
                   jax.ShapeDtypeStruct((1, D_MODEL), F32)],
        semantics=("arbitrary",), jobs=jobs)


def _out_proj_bwd(dx2b, w_o, mix_by, *, tm, tn, jobs=()):
    T = dx2b.shape[0]

    def body(dx_ref, w_ref, *refs):
        dmix = _dot_nt(dx_ref[...], w_ref[...])
        for by_ref, d_ref in zip(refs[:4], refs[4:]):
            d_ref[...] = (dmix * by_ref[...].astype(F32)).astype(BF16)

    blk = pl.BlockSpec((tm, tn), lambda i, j: (i, j))
    out = jax.ShapeDtypeStruct((T, D_MODEL), BF16)
    return _pallas(
        body, (dx2b, w_o, *mix_by), name="out_proj_bwd", grid=(T // tm, D_MODEL // tn),
        in_specs=[pl.BlockSpec((tm, D_MODEL), lambda i, j: (i, 0)), pl.BlockSpec((tn, D_MODEL), lambda i, j: (j, 0))] + [blk] * 4,
        out_specs=[blk] * 4, out_shape=[out] * 4, semantics=("parallel", "parallel"), jobs=jobs)


def _branch_bwd(dyp, dyr, w_pool_out, w_rnn_out, *, tm):
    T = dyp.shape[0]

    def body(dyp_ref, dyr_ref, wp_ref, wr_ref, dpm_ref, dz_ref):
        dpm_ref[...] = _dot_nt(dyp_ref[...], wp_ref[...])
        dz_ref[...] = _dot_nt(dyr_ref[...], wr_ref[...])

    row = pl.BlockSpec((tm, D_MODEL), lambda i: (i, 0))
    return pl.pallas_call(
        body, name="branch_bwd", grid=(T // tm,),
        in_specs=[row, row, pl.BlockSpec((D_POOL, D_MODEL), lambda i: (0, 0)), pl.BlockSpec((D_RNN, D_MODEL), lambda i: (0, 0))],
        out_specs=[pl.BlockSpec((tm, D_POOL), lambda i: (i, 0)), pl.BlockSpec((tm, D_RNN), lambda i: (i, 0))],
        out_shape=[jax.ShapeDtypeStruct((T, D_POOL), F32), jax.ShapeDtypeStruct((T, D_RNN), F32)],
        compiler_params=_params(dimension_semantics=("parallel",)),
    )(dyp, dyr, w_pool_out, w_rnn_out)


def _in_proj_bwd(segs, w, dx2, x, g1, *, tm, jobs=()):
    T = x.shape[0]
    widths = [s.shape[1] for s in segs]
    offs = [sum(widths[:k]) for k in range(len(widths))]
    n = len(segs)

    def body(*refs):
        seg_refs, (w_ref, dx2_ref, x_ref, g_ref, dx_ref, dg_ref) = refs[:n], refs[n:]

        @pl.when(pl.program_id(0) == 0)
        def _():
            dg_ref[...] = jnp.zeros_like(dg_ref)

        dh = _dot_nt(seg_refs[0][...], w_ref[:, offs[0]:offs[0] + widths[0]])
        for k in range(1, n):
            dh += _dot_nt(seg_refs[k][...], w_ref[:, offs[k]:offs[k] + widths[k]])
        dxn, dgp = _rms_bwd(dh, x_ref[...], g_ref[...])
        dg_ref[...] += jnp.sum(dgp, axis=0, keepdims=True)
        dx_ref[...] = dx2_ref[...] + dxn

    row = pl.BlockSpec((tm, D_MODEL), lambda i: (i, 0))
    vec = pl.BlockSpec((1, D_MODEL), lambda i: (0, 0))
    return _pallas(
        body, (*segs, w, dx2, x, g1), name="in_proj_bwd", grid=(T // tm,),
        in_specs=[pl.BlockSpec((tm, wd), lambda i: (i, 0)) for wd in widths]
        + [pl.BlockSpec((D_MODEL, D_IN), lambda i: (0, 0)), row, row, vec],
        out_specs=[row, vec],
        out_shape=[jax.ShapeDtypeStruct((T, D_MODEL), F32), jax.ShapeDtypeStruct((1, D_MODEL), F32)],
        semantics=("arbitrary",), jobs=jobs)


def _weight_grad(a, segs, *, tm, tn, name, jobs=None):
    T, M = a.shape
    nblk = [s.shape[1] // tn for s in segs]
    first = [sum(nblk[:k]) for k in range(len(segs))]
    n = len(segs)

    def body(a_ref, *refs):
        seg_refs, o_ref = refs[:n], refs[n]
        j = pl.program_id(1)
        for k in range(n):
            @pl.when((j >= first[k]) & (j < first[k] + nblk[k]))
            def _(k=k):
                o_ref[...] = _dot_tn(a_ref[...], seg_refs[k][...])

    def seg_spec(k):
        return pl.BlockSpec((T, tn), lambda i, j: (0, jnp.clip(j - first[k], 0, nblk[k] - 1)))

    (grad,), results = _pallas(
        body, (a, *segs), name=name, grid=(M // tm, sum(nblk)),
        in_specs=[pl.BlockSpec((T, tm), lambda i, j: (0, i))] + [seg_spec(k) for k in range(n)],
        out_specs=[pl.BlockSpec((tm, tn), lambda i, j: (i, j))],
        out_shape=[jax.ShapeDtypeStruct((M, sum(nblk) * tn), F32)],
        semantics=("parallel", "arbitrary"), jobs=jobs or ())
    return grad if jobs is None else (grad, results)


def _pad_front(dst, src, halo):
    dst[pl.ds(0, halo), :] = jnp.zeros((halo, src.shape[1]), F32)

    def fill(i, carry):
        r0 = pl.multiple_of(i * CHUNK, CHUNK)
        dst[pl.ds(r0 + halo, CHUNK), :] = src[pl.ds(r0, CHUNK), :]
        return carry

    lax.fori_loop(0, src.shape[0] // CHUNK, fill, 0)


def _shift_rows(v, k):
    return pltpu.roll(v, k % v.shape[0], axis=0)


def _window_sums(xs, direction):
    s2 = xs + _shift_rows(xs, direction)
    s4 = s2 + _shift_rows(s2, 2 * direction)
    s8 = s4 + _shift_rows(s4, 4 * direction)
    s16 = s8 + _shift_rows(s8, 8 * direction)
    return s2, s4, s8, s16


def _select_window(g, sums):
    s2, s4, s8, s16 = sums
    return jnp.where(g == 0, s2, jnp.where(g == 1, s4, jnp.where(g == 2, s8, s16)))


def _pool_count(g, start, rows):
    t = start + lax.broadcasted_iota(jnp.int32, (rows, 1), 0)
    return jnp.minimum(t + 1, jnp.left_shift(2, g)).astype(F32)


def _pool_fwd(proj, w_grp, scale):
    T = proj.shape[0]
    nchunk = T // CHUNK

    def body(u_ref, w_ref, s_ref, o_ref, upad):
        g = pl.program_id(0)
        _pad_front(upad, u_ref, POOL_HALO)
        w = w_ref[...].astype(BF16)
        scale_row = s_ref[...]

        def chunk(i, carry):
            r0 = pl.multiple_of(i * CHUNK, CHUNK)
            xs = upad[pl.ds(r0, CHUNK + POOL_HALO), :]
            win = _select_window(g, _window_sums(xs, 1))[POOL_HALO:]
            pooled = win / _pool_count(g, r0, CHUNK) - xs[POOL_HALO:]
            o_ref[pl.ds(r0, CHUNK), :] = (_dot(pooled.astype(BF16), w) * scale_row).astype(BF16)
            return carry

        lax.fori_loop(0, nchunk, chunk, 0)

    return pl.pallas_call(
        body, name="pool_fwd", grid=(N_POOL_GROUPS,),
        in_specs=[pl.BlockSpec((T, HEAD), lambda g: (0, g)), pl.BlockSpec((None, HEAD, HEAD), lambda g: (g, 0, 0)),
                  pl.BlockSpec((1, HEAD), lambda g: (0, g))],
        out_specs=pl.BlockSpec((T, HEAD), lambda g: (0, g)),
        out_shape=jax.ShapeDtypeStruct((T, D_POOL), BF16),
        scratch_shapes=[pltpu.VMEM((T + POOL_HALO, HEAD), F32)],
        compiler_params=_params(dimension_semantics=("parallel",)),
    )(proj, w_grp, scale)


def _pool_bwd(proj, dpm, w_grp, scale, jobs=()):
    T = proj.shape[0]
    nchunk = T // CHUNK

    def body(u_ref, dpm_ref, w_ref, s_ref, du_ref, dw_ref, ds_ref, upad, zpad, dpool):
        g = pl.program_id(0)
        _pad_front(upad, u_ref, POOL_HALO)
        zpad[pl.ds(T, POOL_HALO), :] = jnp.zeros((POOL_HALO, HEAD), F32)
        dw_ref[...] = jnp.zeros_like(dw_ref)
        ds_ref[...] = jnp.zeros_like(ds_ref)
        w = w_ref[...].astype(BF16)
        scale_row = s_ref[...]

        def chunk(i, carry):
            r0 = pl.multiple_of(i * CHUNK, CHUNK)
            xs = upad[pl.ds(r0, CHUNK + POOL_HALO), :]
            cnt = _pool_count(g, r0, CHUNK)
            pooled = (_select_window(g, _window_sums(xs, 1))[POOL_HALO:] / cnt - xs[POOL_HALO:]).astype(BF16)
            mixed = _dot(pooled, w)
            d = dpm_ref[pl.ds(r0, CHUNK), :]
            ds_ref[...] += jnp.sum(d * mixed, axis=0, keepdims=True)
            dmixed = (d * scale_row).astype(BF16)
            dw_ref[...] += _dot_tn(pooled, dmixed)
            dp = _dot_nt(dmixed, w)
            dpool[pl.ds(r0, CHUNK), :] = dp
            zpad[pl.ds(r0, CHUNK), :] = dp / cnt
            return carry

        lax.fori_loop(0, nchunk, chunk, 0)

        def chunk2(i, carry):
            r0 = pl.multiple_of(i * CHUNK, CHUNK)
            zs = zpad[pl.ds(r0, CHUNK + POOL_HALO), :]
            win = _select_window(g, _window_sums(zs, -1))[:CHUNK]
            du_ref[pl.ds(r0, CHUNK), :] = (win - dpool[pl.ds(r0, CHUNK), :]).astype(BF16)
            return carry

        lax.fori_loop(0, nchunk, chunk2, 0)

    col = pl.BlockSpec((T, HEAD), lambda g: (0, g))
    return _pallas(
        body, (proj, dpm, w_grp, scale), name="pool_bwd", grid=(N_POOL_GROUPS,),
        in_specs=[col, col, pl.BlockSpec((None, HEAD, HEAD), lambda g: (g, 0, 0)), pl.BlockSpec((1, HEAD), lambda g: (0, g))],
        out_specs=[col, pl.BlockSpec((None, HEAD, HEAD), lambda g: (g, 0, 0)), pl.BlockSpec((1, HEAD), lambda g: (0, g))],
        out_shape=[jax.ShapeDtypeStruct((T, D_POOL), BF16), jax.ShapeDtypeStruct((N_POOL_GROUPS, HEAD, HEAD), F32),
                   jax.ShapeDtypeStruct((1, D_POOL), F32)],
        scratch_shapes=[pltpu.VMEM((T + POOL_HALO, HEAD), F32), pltpu.VMEM((T + POOL_HALO, HEAD), F32), pltpu.VMEM((T, HEAD), F32)],
        semantics=("parallel",), jobs=jobs)


def _conv_taps(xs, cw):
    v = cw[CONV_WIDTH - 1] * xs[SUBLANES:]
    for k in range(CONV_WIDTH - 1):
        v += cw[k] * _shift_rows(xs, CONV_WIDTH - 1 - k)[SUBLANES:]
    return v


def _tap_rows(cw_ref):
    return [cw_ref[k:k + 1, :] for k in range(CONV_WIDTH)]


def _softplus_neg(lam):
    return jnp.maximum(-lam, 0.0) + _log1p(jnp.exp(-jnp.abs(lam)))


def _lru_gates(v, wa, ba, wx, bx, sp):
    vb = v.astype(BF16)
    ra = _sigmoid(_dot(vb, wa) + ba)
    ix = _sigmoid(_dot(vb, wx) + bx)
    log_a = -LRU_C * ra * sp
    a = jnp.exp(log_a)
    sq = jnp.sqrt(-jnp.tanh(log_a) * (a * a + 1.0))
    return ra, ix, a, sq


def _row_bcast(v, r):
    return jnp.broadcast_to(v[r:r + 1, :], v.shape)


TILE_BLOCK = 128


def _scan_in_tiles(coef, coef_shift, A_out, B, T, direction):
    order = list(range(SUBLANES)) if direction == 1 else list(range(SUBLANES - 1, -1, -1))
    tiles = min(TILE_BLOCK, T // SUBLANES)
    for base in range(0, T, tiles * SUBLANES):
        def rows(r, base=base):
            return pl.ds(base + r, tiles, stride=SUBLANES)

        A, Bv = coef[rows(order[0] + coef_shift), :], B[rows(order[0]), :]
        A_out[rows(order[0]), :] = A
        for r in order[1:]:
            a = coef[rows(r + coef_shift), :]
            Bv = a * Bv + B[rows(r), :]
            A = a * A
            A_out[rows(r), :] = A
            B[rows(r), :] = Bv


TILES_PER_STEP = 8


def _carry_tiles(A_s, B_s, out, ntile, direction):
    out_row = SUBLANES - 1 if direction == 1 else 0

    def step(k, carry):
        for j in range(TILES_PER_STEP):
            t = k * TILES_PER_STEP + j
            r0 = pl.multiple_of((t if direction == 1 else ntile - 1 - t) * SUBLANES, SUBLANES)
            A, B = A_s[pl.ds(r0, SUBLANES), :], B_s[pl.ds(r0, SUBLANES), :]
            out[pl.ds(r0, SUBLANES), :] = A * carry + B
            carry = _row_bcast(A, out_row) * carry + _row_bcast(B, out_row)
        return carry

    lax.fori_loop(0, ntile // TILES_PER_STEP, step, jnp.zeros((SUBLANES, HEAD), F32))


def _rnn_fwd(proj, conv_w, conv_b, w_a, b_a, w_x, b_x, lam, jobs=()):
    T = proj.shape[0]
    nchunk = T // CHUNK
    ntile = T // SUBLANES

    def body(u_ref, ug_ref, cw_ref, cb_ref, wa_ref, ba_ref, wx_ref, bx_ref, lam_ref,
             h_ref, z_ref, v_ref, ra_ref, ix_ref, a_ref, sq_ref, upad, a_s, b_s):
        _pad_front(upad, u_ref, SUBLANES)
        cw, cb = _tap_rows(cw_ref), cb_ref[...]
        wa, wx = wa_ref[...].astype(BF16), wx_ref[...].astype(BF16)
        ba, bx = ba_ref[...], bx_ref[...]
        sp = _softplus_neg(lam_ref[...])

        def chunk(i, carry):
            rows = pl.ds(pl.multiple_of(i * CHUNK, CHUNK), CHUNK)
            v = _conv_taps(upad[pl.ds(pl.multiple_of(i * CHUNK, CHUNK), CHUNK + SUBLANES), :], cw) + cb
            ra, ix, a, sq = _lru_gates(v, wa, ba, wx, bx, sp)
            v_ref[rows, :], ra_ref[rows, :], ix_ref[rows, :], a_ref[rows, :], sq_ref[rows, :] = v, ra, ix, a, sq
            a_s[rows, :], b_s[rows, :] = a, sq * ix * v
            return carry

        lax.fori_loop(0, nchunk, chunk, 0)
        _scan_in_tiles(a_s, 0, a_s, b_s, T, 1)
        _carry_tiles(a_s, b_s, h_ref, ntile, 1)

        def chunk3(i, carry):
            r0 = pl.multiple_of(i * CHUNK, CHUNK)
            gl, _ = _gelu_parts(ug_ref[pl.ds(r0, CHUNK), :])
            z_ref[pl.ds(r0, CHUNK), :] = (h_ref[pl.ds(r0, CHUNK), :] * gl).astype(BF16)
            return carry

        lax.fori_loop(0, nchunk, chunk3, 0)

    col = pl.BlockSpec((T, HEAD), lambda h: (0, h))
    vec = pl.BlockSpec((1, HEAD), lambda h: (0, h))
    mat = pl.BlockSpec((None, HEAD, HEAD), lambda h: (h, 0, 0))
    return _pallas(
        body, (proj, proj, conv_w, conv_b, w_a, b_a, w_x, b_x, lam), name="rnn_fwd", grid=(N_RNN_HEADS,),
        in_specs=[pl.BlockSpec((T, HEAD), lambda h: (0, COL_RNN + h)), pl.BlockSpec((T, HEAD), lambda h: (0, COL_GATE + h)),
                  pl.BlockSpec((CONV_WIDTH, HEAD), lambda h: (0, h)), vec, mat, vec, mat, vec, vec],
        out_specs=[col] * 7,
        out_shape=[jax.ShapeDtypeStruct((T, D_RNN), F32), jax.ShapeDtypeStruct((T, D_RNN), BF16)]
        + [jax.ShapeDtypeStruct((T, D_RNN), F32)] * 5,
        scratch_shapes=[pltpu.VMEM((T + SUBLANES, HEAD), F32), pltpu.VMEM((T, HEAD), F32), pltpu.VMEM((T, HEAD), F32)],
        semantics=("parallel",), jobs=jobs)


def _rnn_bwd(proj, hr, dz, gates, conv_w, w_a, w_x, lam, jobs=()):
    T = proj.shape[0]
    nchunk = T // CHUNK
    ntile = T // SUBLANES

    def body(u_ref, ug_ref, h_ref, dz_ref, v_ref, ra_ref, ix_ref, a_ref, sq_ref, cw_ref, wa_ref, wx_ref, lam_ref,
             du_ref, dug_ref, dwa_ref, dwx_ref, dba_ref, dbx_ref, dlam_ref, dcb_ref, dcw_ref,
             upad, hpad, apad, g_s, dvpad, ga_s):
        zero_tile = jnp.zeros((SUBLANES, HEAD), F32)
        _pad_front(upad, u_ref, SUBLANES)
        _pad_front(hpad, h_ref, SUBLANES)
        apad[pl.ds(T, SUBLANES), :] = zero_tile
        dvpad[pl.ds(T, SUBLANES), :] = zero_tile
        for ref in (dwa_ref, dwx_ref, dba_ref, dbx_ref, dlam_ref, dcb_ref, dcw_ref):
            ref[...] = jnp.zeros_like(ref)
        cw = _tap_rows(cw_ref)
        wa, wx = wa_ref[...].astype(BF16), wx_ref[...].astype(BF16)
        lam_row = lam_ref[...]
        sp = _softplus_neg(lam_row)

        def chunk(i, carry):
            rows = pl.ds(pl.multiple_of(i * CHUNK, CHUNK), CHUNK)
            apad[rows, :] = a_ref[rows, :]
            gl, dgl = _gelu_parts(ug_ref[rows, :])
            d = dz_ref[rows, :]
            g_s[rows, :] = d * gl
            dug_ref[rows, :] = (d * h_ref[rows, :] * dgl).astype(BF16)
            return carry

        lax.fori_loop(0, nchunk, chunk, 0)

        _scan_in_tiles(apad, 1, ga_s, g_s, T, -1)
        _carry_tiles(ga_s, g_s, g_s, ntile, -1)

        def chunk3(i, carry):
            r0 = pl.multiple_of(i * CHUNK, CHUNK)
            rows = pl.ds(r0, CHUNK)
            g = g_s[rows, :]
            h_prev = _shift_rows(hpad[pl.ds(r0, CHUNK + SUBLANES), :], 1)[SUBLANES:]
            v, ra, ix, sq, a = v_ref[rows, :], ra_ref[rows, :], ix_ref[rows, :], sq_ref[rows, :], a_ref[rows, :]
            d_sq = g * ix * v
            d_ix = g * sq * v
            d_la = a * g * h_prev - d_sq * a * a / sq
            dlam_ref[...] += jnp.sum(d_la * ra, axis=0, keepdims=True)
            d_pa = d_la * (-LRU_C) * sp * ra * (1.0 - ra)
            d_px = d_ix * ix * (1.0 - ix)
            vb, d_pab, d_pxb = v.astype(BF16), d_pa.astype(BF16), d_px.astype(BF16)
            dwa_ref[...] += _dot_tn(vb, d_pab)
            dwx_ref[...] += _dot_tn(vb, d_pxb)
            dba_ref[...] += jnp.sum(d_pa, axis=0, keepdims=True)
            dbx_ref[...] += jnp.sum(d_px, axis=0, keepdims=True)
            dv = g * sq * ix + _dot_nt(d_pab, wa) + _dot_nt(d_pxb, wx)
            dvpad[rows, :] = dv
            dcb_ref[...] += jnp.sum(dv, axis=0, keepdims=True)
            xs = upad[pl.ds(r0, CHUNK + SUBLANES), :]
            for k in range(CONV_WIDTH):
                u_k = _shift_rows(xs, CONV_WIDTH - 1 - k)[SUBLANES:] if k < CONV_WIDTH - 1 else xs[SUBLANES:]
                dcw_ref[k:k + 1, :] += jnp.sum(dv * u_k, axis=0, keepdims=True)
            return carry

        lax.fori_loop(0, nchunk, chunk3, 0)
        dlam_ref[...] = dlam_ref[...] * (LRU_C * _sigmoid(-lam_row))

        def chunk4(i, carry):
            r0 = pl.multiple_of(i * CHUNK, CHUNK)
            dvs = dvpad[pl.ds(r0, CHUNK + SUBLANES), :]
            du = cw[CONV_WIDTH - 1] * dvs[:CHUNK]
            for k in range(CONV_WIDTH - 1):
                du += cw[k] * _shift_rows(dvs, -(CONV_WIDTH - 1 - k))[:CHUNK]
            du_ref[pl.ds(r0, CHUNK), :] = du.astype(BF16)
            return carry

        lax.fori_loop(0, nchunk, chunk4, 0)

    col = pl.BlockSpec((T, HEAD), lambda h: (0, h))
    vec = pl.BlockSpec((1, HEAD), lambda h: (0, h))
    mat = pl.BlockSpec((None, HEAD, HEAD), lambda h: (h, 0, 0))
    taps = pl.BlockSpec((CONV_WIDTH, HEAD), lambda h: (0, h))
    vec_out = jax.ShapeDtypeStruct((1, D_RNN), F32)
    mat_out = jax.ShapeDtypeStruct((N_RNN_HEADS, HEAD, HEAD), F32)
    seq = pltpu.VMEM((T, HEAD), F32)
    seq_pad = pltpu.VMEM((T + SUBLANES, HEAD), F32)
    return _pallas(
        body, (proj, proj, hr, dz, *gates, conv_w, w_a, w_x, lam), name="rnn_bwd", grid=(N_RNN_HEADS,),
        in_specs=[pl.BlockSpec((T, HEAD), lambda h: (0, COL_RNN + h)), pl.BlockSpec((T, HEAD), lambda h: (0, COL_GATE + h))]
        + [col] * 7 + [taps, mat, mat, vec],
        out_specs=[col, col, mat, mat, vec, vec, vec, vec, taps],
        out_shape=[jax.ShapeDtypeStruct((T, D_RNN), BF16), jax.ShapeDtypeStruct((T, D_RNN), BF16), mat_out, mat_out,
                   vec_out, vec_out, vec_out, vec_out, jax.ShapeDtypeStruct((CONV_WIDTH, D_RNN), F32)],
        scratch_shapes=[seq_pad, seq_pad, seq_pad, seq, seq_pad, seq],
        semantics=("parallel",), jobs=jobs)


GROUP_FFN_OUT = ["w_ffn_out"]
GROUP_FFN_IN = ["w_ffn_in"]
GROUP_MIX = ["w_o", "w_pool_out", "w_rnn_out"]
GROUP_IN = ["w_in"]


def _step(x, target, s, full, conv_w, place):
    T = x.shape[0]
    tall, mid, low = min(T, 2048), min(T, 1024), min(T, 512)
    full = dict(full)

    def gathered(names, results):
        full.update(zip(names, results))

    early =["w_pool_out", "w_rnn_out", "w_o", "w_ffn_out"]
    (proj, h1), (res,) = _norm_matmul(x, s["norm_mix"], full["w_in"], tm=tall, tn=512, name="in_proj", jobs=[_gather_job(full, early)])
    gathered(early, res)
    pm = _pool_fwd(proj, s["w_pool_grp"], s["pool_scale"])
    (hr, z, *gates), (res,) = _rnn_fwd(proj, conv_w, s["conv_b"], s["w_rg_a"], s["b_rg_a"], s["w_rg_x"], s["b_rg_x"],
                                       s["lru_lambda"], jobs=[_gather_job(full, ["w_ffn_in"])])
    gathered(["w_ffn_in"], res)
    *mix_by, mix = _branch_mix(pm, z, full["w_pool_out"], full["w_rnn_out"], proj, tm=tall, tn=256)
    x2 = _out_proj_residual(mix, full["w_o"], x, tm=mid)
    act_by_up, act_by_gate, act, h2 = _ffn_in(x2, s["norm_ffn"], full["w_ffn_in"], tm=tall, tn=256)
    dx3, dx3b, sq_cols, g_norm_final = _ffn_out_loss(act, full["w_ffn_out"], x2, s["norm_final"], target, tm=low)

    g = {"norm_final": g_norm_final}

    def chip_sums(names, from_sibling):
        sums = {name: _chip_sum(name, g[name], got, place) for name, got in zip(names, from_sibling)}
        return {name: v[0] for name, v in sums.items()}, {name: v[1] for name, v in sums.items()}

    def final_sums(names, sums, from_chips):
        return {name: _final_sum(name, sums[name], got, place) for name, got in zip(names, from_chips)}

    dgate, dup = _ffn_out_bwd(dx3b, full["w_ffn_out"], act_by_gate, act_by_up, tm=tall, tn=256)
    g["w_ffn_out"] = _weight_grad(act, [dx3b], tm=256, tn=D_MODEL, name="w_ffn_out_grad")
    (dx2, dx2b, g["norm_ffn"]), (res,) = _ffn_in_bwd(dgate, dup, full["w_ffn_in"], dx3, x2, s["norm_ffn"], tm=low,
                                                     jobs=[_sibling_job(g, GROUP_FFN_OUT)])
    sums_ffn, sums_ffn_bf16 = chip_sums(GROUP_FFN_OUT, res)
    g["w_ffn_in"], (res,) = _weight_grad(h2, [dgate, dup], tm=D_MODEL, tn=256, name="w_ffn_in_grad",
                                         jobs=[_chips_job(sums_ffn_bf16, GROUP_FFN_OUT)])
    shards_ffn = final_sums(GROUP_FFN_OUT, sums_ffn, res)
    (dgp, dgr, dyp, dyr), (res,) = _out_proj_bwd(dx2b, full["w_o"], mix_by, tm=tall, tn=256,
                                                 jobs=[_sibling_job(g, GROUP_FFN_IN)])
    sums_ffn, sums_ffn_bf16 = chip_sums(GROUP_FFN_IN, res)
    g["w_o"] = _weight_grad(mix, [dx2b], tm=D_MODEL, tn=256, name="w_o_grad")
    dpm, dz = _branch_bwd(dyp, dyr, full["w_pool_out"], full["w_rnn_out"], tm=mid)
    g["w_pool_out"] = _weight_grad(pm, [dyp], tm=D_POOL, tn=256, name="w_pool_out_grad")
    g["w_rnn_out"] = _weight_grad(z, [dyr], tm=D_RNN, tn=256, name="w_rnn_out_grad")
    (dupool, g["w_pool_grp"], g["pool_scale"]), (res,) = _pool_bwd(proj, dpm, s["w_pool_grp"], s["pool_scale"],
                                                                   jobs=[_sibling_job(g, GROUP_MIX)])
    sums_mix, sums_mix_bf16 = chip_sums(GROUP_MIX, res)
    ((durnn, dugate, g["w_rg_a"], g["w_rg_x"], g["b_rg_a"], g["b_rg_x"], g["lru_lambda"], g["conv_b"], g["conv_w"]),
     (res,)) = _rnn_bwd(proj, hr, dz, gates, conv_w, s["w_rg_a"], s["w_rg_x"], s["lru_lambda"],
                        jobs=[_chips_job(sums_ffn_bf16, GROUP_FFN_IN)])
    shards_ffn.update(final_sums(GROUP_FFN_IN, sums_ffn, res))
    segs = [dupool, durnn, dugate, dgp, dgr]
    ffn = GROUP_FFN_OUT + GROUP_FFN_IN
    g["w_in"], (res, joined) = _weight_grad(h1, segs, tm=D_MODEL, tn=256, name="w_in_grad",
                                           jobs=[_chips_job(sums_mix_bf16, GROUP_MIX), _join_job(shards_ffn, ffn)])
    grads = dict(zip(ffn, joined))
    shards = final_sums(GROUP_MIX, sums_mix, res)
    (res,) = _run_jobs([_sibling_job(g, GROUP_IN)], "w_in_exchange_sibling")
    sums_in, sums_in_bf16 = chip_sums(GROUP_IN, res)
    (grad_x, g["norm_mix"]), (res,) = _in_proj_bwd(segs, full["w_in"], dx2, x, s["norm_mix"], tm=low,
                                                  jobs=[_chips_job(sums_in_bf16, GROUP_IN)])
    shards.update(final_sums(GROUP_IN, sums_in, res))

    vec_rows = [g[name] if name != "pool_scale" else jnp.pad(g[name], ((0, 0), (0, D_MODEL - D_POOL))) for name in VEC_ITEMS]
    vec_rows += [g["conv_w"], sq_cols, jnp.zeros((VEC_ROWS - len(VEC_ITEMS) - CONV_WIDTH - 1, D_MODEL), F32)]
    vec = jnp.concatenate(vec_rows, axis=0).reshape(VEC_ROWS, N_DEV, HEAD).transpose(1, 0, 2)
    mat = jnp.concatenate([g[name].reshape(-1, HEAD) for name in MAT_ITEMS], axis=0).reshape(N_DEV, -1, HEAD)
    (vec, mat), (joined,) = _all_reduce_small([vec, mat], jobs=[_join_job(shards, GROUP_MIX + GROUP_IN)])
    grads.update(zip(GROUP_MIX + GROUP_IN, joined))

    vec = vec.transpose(1, 0, 2).reshape(VEC_ROWS, D_MODEL)
    mat = mat.reshape(-1, HEAD)
    for k, name in enumerate(VEC_ITEMS):
        grads[name] = vec[k:k + 1, :s[name].shape[1]]
    grads["conv_w"] = vec[len(VEC_ITEMS):len(VEC_ITEMS) + CONV_WIDTH]
    row = 0
    for name in MAT_ITEMS:
        rows = s[name].shape[0] * HEAD
        grads[name] = mat[row:row + rows]
        row += rows
    return vec[len(VEC_ITEMS) + CONV_WIDTH], grad_x, grads


LARGE = {"w_in": "col", "w_pool_out": "col", "w_rnn_out": "row", "w_o": "row", "w_ffn_in": "col", "w_ffn_out": "row"}
LARGE_SHAPE = {"w_in": (D_MODEL, D_IN), "w_pool_out": (D_POOL, D_MODEL), "w_rnn_out": (D_RNN, D_MODEL),
               "w_o": (D_MODEL, D_MODEL), "w_ffn_in": (D_MODEL, 2 * D_FF), "w_ffn_out": (D_FF, D_MODEL)}


def _place():
    x, y, c = lax.axis_index("x"), lax.axis_index("y"), lax.axis_index("c")
    return 2 * x + y, c


def _chip_device(chip, c):
    return (chip // 2, chip % 2, c)


def _chip_window(ref, kind, shape, chip, half=None):
    K, N = shape
    if kind == "col":
        rows = slice(None) if half is None else pl.ds(half * (K // 2), K // 2)
        return ref.at[rows, pl.ds(chip * (N // N_CHIPS), N // N_CHIPS)]
    ks = K // N_CHIPS
    if half is None:
        return ref.at[pl.ds(chip * ks, ks), :]
    return ref.at[pl.ds(chip * ks + half * (ks // 2), ks // 2), :]


def _row_half(ref, half):
    rows = ref.shape[0] // 2
    return ref.at[pl.ds(half * rows, rows), :]


def _remote(win_src, win_dst, send_sems, recv_sems, idx, to):
    return pltpu.make_async_remote_copy(src_ref=win_src, dst_ref=win_dst, send_sem=send_sems.at[idx], recv_sem=recv_sems.at[idx],
                                        device_id=to, device_id_type=MESH)


def _gather_job(full, names, conv_w_full=None):
    n = len(names)
    cw_cols = D_RNN // N_CHIPS

    def windows(refs, chip, half):
        return [_chip_window(refs[k], LARGE[name], LARGE_SHAPE[name], chip, half) for k, name in enumerate(names)]

    def ici_copies(refs, send_sems, recv_sems, src_chip, dst_chip, c, r):
        wins = windows(refs, src_chip, c)
        if conv_w_full is not None:
            wins.append(refs[n].at[:, pl.ds(src_chip * cw_cols, cw_cols)])
        return [_remote(win, win, send_sems, recv_sems, (k, r), _chip_device(dst_chip, c)) for k, win in enumerate(wins)]

    def forwards(refs, send_sems, recv_sems, src_chip, half, to_core, chip, r):
        return [_remote(win, win, send_sems, recv_sems, (k, 3 + r), _chip_device(chip, to_core))
                for k, win in enumerate(windows(refs, src_chip, half))]

    def start(ins, outs, send_sems, recv_sems):
        chip, c = _place()
        for r in range(3):
            for cp in ici_copies(outs, send_sems, recv_sems, chip, chip ^ (r + 1), c, r):
                cp.start()

    def finish(ins, outs, send_sems, recv_sems):
        chip, c = _place()
        for r in range(3):
            for cp in ici_copies(outs, send_sems, recv_sems, chip ^ (r + 1), chip, c, r):
                cp.wait_recv()
            for cp in forwards(outs, send_sems, recv_sems, chip ^ (r + 1), c, 1 - c, chip, r):
                cp.start()
        for r in range(3):
            for cp in forwards(outs, send_sems, recv_sems, chip ^ (r + 1), 1 - c, c, chip, r):
                cp.wait_recv()
            for cp in ici_copies(outs, send_sems, recv_sems, chip, chip ^ (r + 1), c, r):
                cp.wait_send()
            for cp in forwards(outs, send_sems, recv_sems, chip ^ (r + 1), c, 1 - c, chip, r):
                cp.wait_send()

    arrays = [full[name] for name in names] + ([conv_w_full] if conv_w_full is not None else [])
    return _Job(arrays, [jax.ShapeDtypeStruct(a.shape, a.dtype) for a in arrays], {k: k for k in range(len(arrays))},
                (len(arrays), 6), start, finish)


def _core_halves(ref, kind, shape, c):
    return [_chip_window(ref, kind, shape, chip, c) for chip in range(N_CHIPS)]


def _sibling_job(grads, names):
    def start(ins, outs, send_sems, recv_sems):
        chip, c = _place()
        for k, name in enumerate(names):
            kind, shape = LARGE[name], LARGE_SHAPE[name]
            if kind == "col":
                pairs = [(_row_half(ins[k], 1 - c), outs[k])]
            else:
                rows = shape[0] // N_DEV
                pairs = [(win, outs[k].at[pl.ds(j * rows, rows), :]) for j, win in enumerate(_core_halves(ins[k], kind, shape, 1 - c))]
            for src, dst in pairs:
                _remote(src, dst, send_sems, recv_sems, k, _chip_device(chip, 1 - c)).start()

    def finish(ins, outs, send_sems, recv_sems):
        chip, c = _place()
        for k in range(len(names)):
            _remote(outs[k], outs[k], send_sems, recv_sems, k, _chip_device(chip, 1 - c)).wait()

    return _Job([grads[name] for name in names],
                [jax.ShapeDtypeStruct((LARGE_SHAPE[name][0] // 2, LARGE_SHAPE[name][1]), F32) for name in names], {},
                (len(names),), start, finish)


def _chip_sum(name, g, got, place):
    kind, (K, N) = LARGE[name], LARGE_SHAPE[name]
    rows = K // N_DEV
    piece_cols = N // N_CHIPS

    def body(place_ref, g_ref, got_ref, o_ref, ob_ref):
        total = g_ref[...] + got_ref[...]
        ob_ref[...] = total.astype(BF16)
        if kind == "col":
            for chip in range(N_CHIPS):
                @pl.when(place_ref[0] == chip)
                def _(chip=chip):
                    o_ref[...] = total[:, chip * piece_cols:(chip + 1) * piece_cols]
        else:
            @pl.when(pl.program_id(0) == place_ref[0])
            def _():
                o_ref[...] = total

    if kind == "col":
        mine = pl.BlockSpec((rows, N), lambda j, place_ref: (j + N_CHIPS * place_ref[1], 0))
        own = pl.BlockSpec((rows, piece_cols), lambda j, place_ref: (j, 0))
    else:
        mine = pl.BlockSpec((rows, N), lambda j, place_ref: (2 * j + place_ref[1], 0))
        own = pl.BlockSpec((rows, N), lambda j, place_ref: (0, 0))
    blk = pl.BlockSpec((rows, N), lambda j, place_ref: (j, 0))
    return pl.pallas_call(
        body, name=name + "_chip_sum",
        grid_spec=pltpu.PrefetchScalarGridSpec(num_scalar_prefetch=1, grid=(N_CHIPS,), in_specs=[mine, blk], out_specs=[own, blk]),
        out_shape=[jax.ShapeDtypeStruct(_piece_shape(name), F32), jax.ShapeDtypeStruct((K // 2, N), BF16)],
        compiler_params=_params(dimension_semantics=("arbitrary",)),
    )(place, g, got)


def _piece(ref, kind, shape, chip):
    K, N = shape
    if kind == "col":
        return ref.at[:, pl.ds(chip * (N // N_CHIPS), N // N_CHIPS)]
    return ref.at[pl.ds(chip * (K // N_DEV), K // N_DEV), :]


def _piece_shape(name):
    kind, (K, N) = LARGE[name], LARGE_SHAPE[name]
    return (K // 2, N // N_CHIPS) if kind == "col" else (K // N_DEV, N)


def _chips_job(sums, names):
    def copies(ins, outs, send_sems, recv_sems):
        chip, c = _place()
        return [_remote(_piece(ins[k], LARGE[name], LARGE_SHAPE[name], chip ^ (r + 1)), outs[k].at[r], send_sems, recv_sems, (k, r),
                        _chip_device(chip ^ (r + 1), c)) for k, name in enumerate(names) for r in range(3)]

    def start(*refs):
        for cp in copies(*refs):
            cp.start()

    def finish(*refs):
        for cp in copies(*refs):
            cp.wait()

    return _Job([sums[name] for name in names], [jax.ShapeDtypeStruct((3,) + _piece_shape(name), BF16) for name in names], {},
                (len(names), 3), start, finish)


def _final_sum(name, chip_sum, got, place):
    rows, cols = _piece_shape(name)

    def body(place_ref, s_ref, got_ref, o_ref):
        o_ref[...] = ((s_ref[...] + got_ref[0].astype(F32)) + got_ref[1].astype(F32)) + got_ref[2].astype(F32)

    mine = pl.BlockSpec((rows, cols), lambda i, place_ref: (0, 0))
    return pl.pallas_call(
        body, name=name + "_final_sum",
        grid_spec=pltpu.PrefetchScalarGridSpec(
            num_scalar_prefetch=1, grid=(1,), in_specs=[mine, pl.BlockSpec((3, rows, cols), lambda i, place_ref: (0, 0, 0))],
            out_specs=pl.BlockSpec((rows, cols), lambda i, place_ref: (place_ref[1], 0))),
        out_shape=jax.ShapeDtypeStruct((2 * rows, cols), F32),
        compiler_params=_params(dimension_semantics=("arbitrary",)),
    )(place, chip_sum, got)


def _join_job(shards, names):
    def half_copy(outs, send_sems, recv_sems, k, mine):
        chip, c = _place()
        win = _row_half(outs[k], c if mine else 1 - c)
        return _remote(win, win, send_sems, recv_sems, k, _chip_device(chip, 1 - c))

    def start(ins, outs, send_sems, recv_sems):
        for k in range(len(names)):
            half_copy(outs, send_sems, recv_sems, k, True).start()

    def finish(ins, outs, send_sems, recv_sems):
        for k in range(len(names)):
            half_copy(outs, send_sems, recv_sems, k, True).wait_send()
            half_copy(outs, send_sems, recv_sems, k, False).wait_recv()

    arrays = [shards[name] for name in names]
    return _Job(arrays, [jax.ShapeDtypeStruct(a.shape, F32) for a in arrays], {k: k for k in range(len(arrays))},
                (len(arrays),), start, finish)


VEC_ROWS = 16


def _all_reduce_small(slabs, jobs=()):
    n = len(slabs)

    def body(*refs):
        in_refs, out_refs, got_refs = refs[:n], refs[n:2 * n], refs[2 * n:3 * n]
        send_sems, recv_sems = refs[3 * n:]
        x, y, c = lax.axis_index("x"), lax.axis_index("y"), lax.axis_index("c")
        me = 4 * x + 2 * y + c

        def remote(src, dst, k, phase, r):
            other = me ^ r
            return pltpu.make_async_remote_copy(src_ref=src, dst_ref=dst, send_sem=send_sems.at[k, phase, r],
                                                recv_sem=recv_sems.at[k, phase, r],
                                                device_id=(other // 4, (other // 2) % 2, other % 2), device_id_type=MESH)

        scatter = [remote(in_refs[k].at[me ^ r], got_refs[k].at[r], k, 0, r) for r in range(1, N_DEV) for k in range(n)]
        for cp in scatter:
            cp.start()
        for cp in scatter:
            cp.wait()
        for k in range(n):
            total = in_refs[k][me]
            for r in range(1, N_DEV):
                total = total + got_refs[k][r]
            out_refs[k][me] = total
        gather = [remote(out_refs[k].at[me], out_refs[k].at[me], k, 1, r) for r in range(1, N_DEV) for k in range(n)]
        for cp in gather:
            cp.start()
        for r in range(1, N_DEV):
            for k in range(n):
                remote(out_refs[k].at[me ^ r], out_refs[k].at[me ^ r], k, 1, r).wait_recv()
        for cp in gather:
            cp.wait_send()

    return _pallas(
        body, slabs, name="all_reduce_small", grid=(), in_specs=[VMEM] * n, out_specs=[VMEM] * n,
        out_shape=[jax.ShapeDtypeStruct(s.shape, F32) for s in slabs],
        scratch_shapes=[pltpu.VMEM(s.shape, F32) for s in slabs]
        + [pltpu.SemaphoreType.DMA((n, 2, N_DEV)), pltpu.SemaphoreType.DMA((n, 2, N_DEV))], jobs=jobs)


def _cast_into_whole(w, name, place):
    rows, cols = w.shape
    tr = rows // 2

    def body(place_ref, w_ref, o_ref):
        o_ref[...] = w_ref[...].astype(BF16)

    if LARGE[name] == "col":
        window = pl.BlockSpec((tr, cols), lambda i, place_ref: (i, place_ref[0]))
    else:
        window = pl.BlockSpec((tr, cols), lambda i, place_ref: (2 * place_ref[0] + i, 0))
    return pl.pallas_call(
        body, name=name + "_cast",
        grid_spec=pltpu.PrefetchScalarGridSpec(num_scalar_prefetch=1, grid=(2,),
                                               in_specs=[pl.BlockSpec((tr, cols), lambda i, place_ref: (i, 0))], out_specs=window),
        out_shape=jax.ShapeDtypeStruct(LARGE_SHAPE[name], BF16),
        compiler_params=_params(dimension_semantics=("parallel",)))(place, w)


def _cast_many_into_whole(shards, place, jobs):
    names = list(shards)
    n = len(names)

    def body(place_ref, *refs):
        for w_ref, o_ref in zip(refs[:n], refs[n:]):
            o_ref[...] = w_ref[...].astype(BF16)

    def window(name):
        rows, cols = shards[name].shape
        if LARGE[name] == "col":
            return pl.BlockSpec((rows // 2, cols), lambda i, place_ref: (i, place_ref[0]))
        return pl.BlockSpec((rows // 2, cols), lambda i, place_ref: (2 * place_ref[0] + i, 0))

    def half(name):
        rows, cols = shards[name].shape
        return pl.BlockSpec((rows // 2, cols), lambda i, place_ref: (i, 0))

    return _pallas(body, [shards[name] for name in names], name="cast_weights", grid=(2,),
                   in_specs=[half(name) for name in names], out_specs=[window(name) for name in names],
                   out_shape=[jax.ShapeDtypeStruct(LARGE_SHAPE[name], BF16) for name in names],
                   semantics=("arbitrary",), jobs=jobs, prefetch=place)


def _adamw_math(w, g, m, v):
    m = ADAM_B1 * m + (1.0 - ADAM_B1) * g
    v = ADAM_B2 * v + (1.0 - ADAM_B2) * (g * g)
    m_hat = m / (1.0 - ADAM_B1 ** ADAM_STEP)
    v_hat = v / (1.0 - ADAM_B2 ** ADAM_STEP)
    delta = -ADAM_LR * (m_hat / (jnp.sqrt(v_hat) + ADAM_EPS) + ADAM_WD * w)
    return delta, m, v


def _adamw_large(w, g, m, v, name):
    rows, cols = w.shape
    steps = 2
    tr = rows // steps

    def body(w_ref, g_ref, m_ref, v_ref, d_ref, mo_ref, vo_ref):
        d_ref[...], mo_ref[...], vo_ref[...] = _adamw_math(w_ref[...], g_ref[...], m_ref[...], v_ref[...])

    blk = pl.BlockSpec((tr, cols), lambda i: (i, 0))
    out = jax.ShapeDtypeStruct(w.shape, F32)
    return pl.pallas_call(body, name=name + "_adamw", grid=(steps,), in_specs=[blk] * 4, out_specs=[blk] * 3, out_shape=[out] * 3,
                          compiler_params=_params(dimension_semantics=("parallel",)))(w, g, m, v)


def _adamw_small(ws, gs, ms, vs):
    n = len(ws)

    def body(*refs):
        for k in range(n):
            w_ref, g_ref, m_ref, v_ref = (refs[q * n + k] for q in range(4))
            d_ref, mo_ref, vo_ref = (refs[(4 + q) * n + k] for q in range(3))
            d_ref[...], mo_ref[...], vo_ref[...] = _adamw_math(w_ref[...], g_ref[...], m_ref[...], v_ref[...])

    out = [jax.ShapeDtypeStruct(w.shape, F32) for w in ws]
    res = pl.pallas_call(body, name="small_adamw", in_specs=[VMEM] * (4 * n), out_specs=[VMEM] * (3 * n), out_shape=out * 3,
                         compiler_params=_params())(*ws, *gs, *ms, *vs)
    return res[:n], res[n:2 * n], res[2 * n:]


WEIGHTS = ["norm_mix", "w_in", "w_pool_grp", "pool_scale", "w_pool_out", "conv_w", "conv_b", "w_rg_a", "b_rg_a", "w_rg_x",
           "b_rg_x", "lru_lambda", "w_rnn_out", "w_o", "norm_ffn", "w_ffn_in", "w_ffn_out", "norm_final"]
VEC_ITEMS = ["norm_mix", "norm_ffn", "norm_final", "pool_scale", "conv_b", "lru_lambda", "b_rg_a", "b_rg_x"]
MAT_ITEMS = ["w_pool_grp", "w_rg_a", "w_rg_x"]


def _as2d(name, a):
    if name in MAT_ITEMS:
        return a.reshape(-1, HEAD, HEAD)
    if name == "conv_w":
        return a.reshape(CONV_WIDTH, -1)
    return a.reshape(1, -1)


def kernel(x, norm_mix, w_in, w_pool_grp, pool_scale, w_pool_out, conv_w, conv_b, w_rg_a, b_rg_a, w_rg_x, b_rg_x, lru_lambda, w_rnn_out, w_o, norm_ffn, w_ffn_in, w_ffn_out, norm_final, loss_target, m_norm_mix, m_w_in, m_w_pool_grp, m_pool_scale, m_w_pool_out, m_conv_w, m_conv_b, m_w_rg_a, m_b_rg_a, m_w_rg_x, m_b_rg_x, m_lru_lambda, m_w_rnn_out, m_w_o, m_norm_ffn, m_w_ffn_in, m_w_ffn_out, m_norm_final, v_norm_mix, v_w_in, v_w_pool_grp, v_pool_scale, v_w_pool_out, v_conv_w, v_conv_b, v_w_rg_a, v_b_rg_a, v_w_rg_x, v_b_rg_x, v_lru_lambda, v_w_rnn_out, v_w_o, v_norm_ffn, v_w_ffn_in, v_w_ffn_out, v_norm_final):
    given = dict(locals())
    w = {name: given[name] for name in WEIGHTS}
    m = {name: given["m_" + name] for name in WEIGHTS}
    v = {name: given["v_" + name] for name in WEIGHTS}
    chip, c = _place()

    place = jnp.stack([chip, c]).astype(jnp.int32)
    conv_cols = w["conv_w"].shape[-1]
    conv_w_mine = lax.dynamic_update_slice_in_dim(jnp.zeros((CONV_WIDTH, D_RNN), F32), w["conv_w"][0], chip * conv_cols, axis=1)
    w_in_mine = _cast_into_whole(w["w_in"][0], "w_in", place)
    later = [name for name in LARGE if name != "w_in"]
    casts, ((w_in_full, conv_w_full),) = _cast_many_into_whole(
        {name: w[name][0] for name in later}, place, jobs=[_gather_job({"w_in": w_in_mine}, ["w_in"], conv_w_mine)])
    full = dict(zip(later, casts), w_in=w_in_full)
    small = {name: _as2d(name, w[name]) for name in WEIGHTS if name not in LARGE and name != "conv_w"}
    sq_cols, grad_x, grads = _step(x[0], loss_target[0], small, full, conv_w_full, place)
    loss = 0.5 / D_MODEL * jnp.sum(sq_cols)
    grads["conv_w"] = lax.dynamic_slice_in_dim(grads["conv_w"], chip * conv_cols, conv_cols, axis=1)

    delta, new_m, new_v = {}, {}, {}
    for name in LARGE:
        delta[name], new_m[name], new_v[name] = _adamw_large(w[name][0], grads[name], m[name][0], v[name][0], name)
    small_names = [name for name in WEIGHTS if name not in LARGE]
    flat = lambda d: [d[name].reshape(grads[name].shape) for name in small_names]
    ds, mo, vo = _adamw_small(flat(w), [grads[name] for name in small_names], flat(m), flat(v))
    for k, name in enumerate(small_names):
        delta[name], new_m[name], new_v[name] = ds[k], mo[k], vo[k]

    shaped = lambda d: [d[name].reshape(w[name].shape) for name in WEIGHTS]
    return (loss, grad_x[None], *shaped(grads), *shaped(delta), *shaped(new_m), *shaped(new_v))
```

```python
import functools
import math

import jax
import jax.numpy as jnp
from jax import lax
from jax.experimental import pallas as pl
from jax.experimental.pallas import tpu as pltpu

F32 = jnp.float32
BF16 = jnp.bfloat16

D_MODEL = 1024
D_POOL = 512
N_POOL_GROUPS = 4
D_RNN = 1024
N_RNN_HEADS = 8
HEAD = 128
CONV_WIDTH = 4
LRU_C = 8.0
D_FF = 2816
D_IN = D_POOL + 2 * D_RNN + 2 * D_MODEL
NORM_EPS = 1e-6
COL_RNN = D_POOL // HEAD
COL_GATE = (D_POOL + D_RNN) // HEAD

ADAM_LR = 0.001
ADAM_B1 = 0.9
ADAM_B2 = 0.999
ADAM_EPS = 1e-08
ADAM_WD = 0.01
ADAM_STEP = 10

N_CHIPS = 4
N_DEV = 8
MESH = pl.DeviceIdType.MESH
ANY = pl.BlockSpec(memory_space=pl.ANY)
VMEM = pl.BlockSpec(memory_space=pltpu.VMEM)
VMEM_LIMIT_BYTES = 60 * 1024 * 1024
SUBLANES = 8
POOL_HALO = 16
CHUNK = 1024

GELU_C = math.sqrt(2.0 / math.pi)
GELU_A = 0.044715


def _params(**kw):
    return pltpu.CompilerParams(vmem_limit_bytes=VMEM_LIMIT_BYTES, **kw)


def _sigmoid(x):
    return 0.5 * jnp.tanh(0.5 * x) + 0.5


def _log1p(y):
    u = 1.0 + y
    d = u - 1.0
    return jnp.where(d == 0.0, y, jnp.log(u) * (y / jnp.where(d == 0.0, 1.0, d)))


def _gelu_parts(x):
    x2 = x * x
    th = jnp.tanh(GELU_C * (x + GELU_A * x * x2))
    g = 0.5 * x * (1.0 + th)
    dg = 0.5 * (1.0 + th) + 0.5 * x * (1.0 - th * th) * GELU_C * (1.0 + 3.0 * GELU_A * x2)
    return g, dg


def _dot(a, b):
    return jnp.dot(a, b, preferred_element_type=F32)


def _dot_nt(a, b):
    return lax.dot_general(a, b, (((1,), (1,)), ((), ())), preferred_element_type=F32)


def _dot_tn(a, b):
    return lax.dot_general(a, b, (((0,), (0,)), ((), ())), preferred_element_type=F32)


def _rms_scale(xv):
    return lax.rsqrt(jnp.mean(xv * xv, axis=-1, keepdims=True) + NORM_EPS)


def _rms_bwd(dy, xv, g):
    r = _rms_scale(xv)
    xh = xv * r
    dyg = dy * g
    dx = r * (dyg - xh * jnp.mean(dyg * xh, axis=-1, keepdims=True))
    return dx, dy * xh


class _Job:
    def __init__(self, inputs, out_shapes, aliases, sem_shape, start, finish):
        self.inputs, self.out_shapes, self.aliases, self.sem_shape = list(inputs), list(out_shapes), dict(aliases), sem_shape
        self.start, self.finish = start, finish


def _pallas(body, operands, *, name, grid, in_specs, out_specs, out_shape, scratch_shapes=(), semantics=None, jobs=(),
            prefetch=None):
    n_in, n_out, n_scr = len(in_specs), len(out_specs), len(scratch_shapes)
    n_pre = 0 if prefetch is None else 1
    job_in = [a for job in jobs for a in job.inputs]
    job_out = [s for job in jobs for s in job.out_shapes]
    aliases, i0, o0 = {}, n_pre + n_in, n_out
    for job in jobs:
        aliases.update({i0 + i: o0 + o for i, o in job.aliases.items()})
        i0, o0 = i0 + len(job.inputs), o0 + len(job.out_shapes)

    def whole(*refs):
        pre, refs = refs[:n_pre], refs[n_pre:]
        ins, j_ins = refs[:n_in], refs[n_in:n_in + len(job_in)]
        outs = refs[n_in + len(job_in):][:n_out]
        j_outs = refs[n_in + len(job_in) + n_out:][:len(job_out)]
        rest = refs[n_in + len(job_in) + n_out + len(job_out):]
        scr, sems = rest[:n_scr], rest[n_scr:]

        def run(phase):
            i, o = 0, 0
            for k, job in enumerate(jobs):
                getattr(job, phase)(j_ins[i:i + len(job.inputs)], j_outs[o:o + len(job.out_shapes)], sems[2 * k], sems[2 * k + 1])
                i, o = i + len(job.inputs), o + len(job.out_shapes)

        def at(step_of, phase):
            if not jobs:
                return
            if not grid:
                run(phase)
                return
            cond = functools.reduce(jnp.logical_and, [pl.program_id(d) == step_of(d) for d in range(len(grid))])
            pl.when(cond)(functools.partial(run, phase))

        at(lambda d: 0, "start")
        body(*pre, *ins, *outs, *scr)
        at(lambda d: grid[d] - 1, "finish")

    layout = dict(grid=grid, in_specs=list(in_specs) + [ANY] * len(job_in), out_specs=list(out_specs) + [ANY] * len(job_out),
                  scratch_shapes=list(scratch_shapes) + [pltpu.SemaphoreType.DMA(job.sem_shape) for job in jobs for _ in range(2)])
    if prefetch is not None:
        layout = dict(grid_spec=pltpu.PrefetchScalarGridSpec(num_scalar_prefetch=1, **layout))
    res = pl.pallas_call(
        whole, name=name, out_shape=list(out_shape) + job_out, input_output_aliases=aliases,
        compiler_params=_params(dimension_semantics=semantics, has_side_effects=bool(jobs)), **layout,
    )(*([] if prefetch is None else [prefetch]), *operands, *job_in)
    per_job, o = [], n_out
    for job in jobs:
        per_job.append(res[o:o + len(job.out_shapes)])
        o += len(job.out_shapes)
    return res[:n_out], per_job


def _run_jobs(jobs, name):
    return _pallas(lambda: None, [], name=name, grid=(), in_specs=[], out_specs=[], out_shape=[], jobs=jobs)[1]


NORM_ROWS = 256


def _norm_rows(x_ref, g_ref, h_ref):
    g = g_ref[...]

    def rows(i, carry):
        r = pl.ds(pl.multiple_of(i * NORM_ROWS, NORM_ROWS), NORM_ROWS)
        xv = x_ref[r, :]
        h_ref[r, :] = (xv * _rms_scale(xv) * g).astype(BF16)
        return carry

    lax.fori_loop(0, x_ref.shape[0] // NORM_ROWS, rows, 0)


def _norm_matmul(x, g, w, *, tm, tn, name, jobs=()):
    T, K = x.shape
    N = w.shape[1]

    def body(x_ref, g_ref, w_ref, o_ref, h_ref):
        @pl.when(pl.program_id(1) == 0)
        def _():
            _norm_rows(x_ref, g_ref, h_ref)

        o_ref[...] = _dot(h_ref[...], w_ref[...])

    return _pallas(
        body, (x, g, w), name=name, grid=(T // tm, N // tn),
        in_specs=[pl.BlockSpec((tm, K), lambda i, j: (i, 0)), pl.BlockSpec((1, K), lambda i, j: (0, 0)),
                  pl.BlockSpec((K, tn), lambda i, j: (0, j))],
        out_specs=[pl.BlockSpec((tm, tn), lambda i, j: (i, j)), pl.BlockSpec((tm, K), lambda i, j: (i, 0))],
        out_shape=[jax.ShapeDtypeStruct((T, N), F32), jax.ShapeDtypeStruct((T, K), BF16)],
        semantics=("parallel", "arbitrary"), jobs=jobs)


def _ffn_in(x2, g, w, *, tm, tn, jobs=()):
    T, K = x2.shape
    nb = D_FF // tn

    def body(x_ref, g_ref, wg_ref, wu_ref, dup_ref, dgate_ref, act_ref, h_ref):
        @pl.when(pl.program_id(1) == 0)
        def _():
            _norm_rows(x_ref, g_ref, h_ref)

        h = h_ref[...]
        gate = _dot(h, wg_ref[...])
        up = _dot(h, wu_ref[...])
        s = _sigmoid(gate)
        silu = gate * s
        dup_ref[...] = silu.astype(BF16)
        dgate_ref[...] = (up * (s + silu * (1.0 - s))).astype(BF16)
        act_ref[...] = (silu * up).astype(BF16)

    blk = pl.BlockSpec((tm, tn), lambda i, j: (i, j))
    return _pallas(
        body, (x2, g, w, w), name="ffn_in", grid=(T // tm, nb),
        in_specs=[pl.BlockSpec((tm, K), lambda i, j: (i, 0)), pl.BlockSpec((1, K), lambda i, j: (0, 0)),
                  pl.BlockSpec((K, tn), lambda i, j: (0, j)), pl.BlockSpec((K, tn), lambda i, j: (0, j + nb))],
        out_specs=[blk, blk, blk, pl.BlockSpec((tm, K), lambda i, j: (i, 0))],
        out_shape=[jax.ShapeDtypeStruct((T, D_FF), BF16), jax.ShapeDtypeStruct((T, D_FF), BF16),
                   jax.ShapeDtypeStruct((T, D_FF), BF16), jax.ShapeDtypeStruct((T, K), BF16)],
        semantics=("parallel", "arbitrary"), jobs=jobs)


def _branch_mix(pm, z, w_pool_out, w_rnn_out, proj, *, tm, tn):
    T = pm.shape[0]
    col_gp = (D_POOL + 2 * D_RNN) // tn
    col_gr = col_gp + D_MODEL // tn

    def body(pm_ref, z_ref, wp_ref, wr_ref, gp_ref, gr_ref, by_gp_ref, by_gr_ref, sp_ref, sr_ref, mix_ref):
        yp = _dot(pm_ref[...], wp_ref[...])
        yr = _dot(z_ref[...], wr_ref[...])
        sp, sr = _sigmoid(gp_ref[...]), _sigmoid(gr_ref[...])
        by_gp_ref[...] = (yp * sp * (1.0 - sp)).astype(BF16)
        by_gr_ref[...] = (yr * sr * (1.0 - sr)).astype(BF16)
        sp_ref[...] = sp.astype(BF16)
        sr_ref[...] = sr.astype(BF16)
        mix_ref[...] = (sp * yp + sr * yr).astype(BF16)

    blk = pl.BlockSpec((tm, tn), lambda i, j: (i, j))
    out = jax.ShapeDtypeStruct((T, D_MODEL), BF16)
    return pl.pallas_call(
        body, name="branch_mix", grid=(T // tm, D_MODEL // tn),
        in_specs=[pl.BlockSpec((tm, D_POOL), lambda i, j: (i, 0)), pl.BlockSpec((tm, D_RNN), lambda i, j: (i, 0)),
                  pl.BlockSpec((D_POOL, tn), lambda i, j: (0, j)), pl.BlockSpec((D_RNN, tn), lambda i, j: (0, j)),
                  pl.BlockSpec((tm, tn), lambda i, j: (i, col_gp + j)), pl.BlockSpec((tm, tn), lambda i, j: (i, col_gr + j))],
        out_specs=[blk] * 5, out_shape=[out] * 5,
        compiler_params=_params(dimension_semantics=("parallel", "parallel")),
    )(pm, z, w_pool_out, w_rnn_out, proj, proj)


def _out_proj_residual(mix, w_o, x, *, tm):
    T = x.shape[0]

    def body(mix_ref, w_ref, x_ref, o_ref):
        o_ref[...] = x_ref[...] + _dot(mix_ref[...], w_ref[...])

    row = pl.BlockSpec((tm, D_MODEL), lambda i: (i, 0))
    return pl.pallas_call(
        body, name="out_proj_residual", grid=(T // tm,),
        in_specs=[row, pl.BlockSpec((D_MODEL, D_MODEL), lambda i: (0, 0)), row],
        out_specs=row, out_shape=jax.ShapeDtypeStruct((T, D_MODEL), F32),
        compiler_params=_params(dimension_semantics=("parallel",)),
    )(mix, w_o, x)


def _ffn_out_loss(act, w, x2, g3, target, *, tm):
    T = x2.shape[0]

    def body(act_ref, w_ref, x2_ref, g_ref, t_ref, dx_ref, dxb_ref, sq_ref, dg_ref):
        @pl.when(pl.program_id(0) == 0)
        def _():
            sq_ref[...] = jnp.zeros_like(sq_ref)
            dg_ref[...] = jnp.zeros_like(dg_ref)

        x3 = x2_ref[...] + _dot(act_ref[...], w_ref[...])
        g = g_ref[...]
        err = x3 * _rms_scale(x3) * g - t_ref[...]
        sq_ref[...] += jnp.sum(err * err, axis=0, keepdims=True)
        dx, dgp = _rms_bwd(err * (1.0 / D_MODEL), x3, g)
        dg_ref[...] += jnp.sum(dgp, axis=0, keepdims=True)
        dx_ref[...] = dx
        dxb_ref[...] = dx.astype(BF16)

    row = pl.BlockSpec((tm, D_MODEL), lambda i: (i, 0))
    vec = pl.BlockSpec((1, D_MODEL), lambda i: (0, 0))
    return pl.pallas_call(
        body, name="ffn_out_loss", grid=(T // tm,),
        in_specs=[pl.BlockSpec((tm, D_FF), lambda i: (i, 0)), pl.BlockSpec((D_FF, D_MODEL), lambda i: (0, 0)), row, vec, row],
        out_specs=[row, row, vec, vec],
        out_shape=[jax.ShapeDtypeStruct((T, D_MODEL), F32), jax.ShapeDtypeStruct((T, D_MODEL), BF16),
                   jax.ShapeDtypeStruct((1, D_MODEL), F32), jax.ShapeDtypeStruct((1, D_MODEL), F32)],
        compiler_params=_params(dimension_semantics=("arbitrary",)),
    )(act, w, x2, g3, target)


def _ffn_out_bwd(dx3b, w, act_by_gate, act_by_up, *, tm, tn):
    T = dx3b.shape[0]

    def body(dx_ref, w_ref, by_gate_ref, by_up_ref, dgate_ref, dup_ref):
        dact = _dot_nt(dx_ref[...], w_ref[...])
        dgate_ref[...] = (dact * by_gate_ref[...].astype(F32)).astype(BF16)
        dup_ref[...] = (dact * by_up_ref[...].astype(F32)).astype(BF16)

    blk = pl.BlockSpec((tm, tn), lambda i, j: (i, j))
    return pl.pallas_call(
        body, name="ffn_out_bwd", grid=(T // tm, D_FF // tn),
        in_specs=[pl.BlockSpec((tm, D_MODEL), lambda i, j: (i, 0)), pl.BlockSpec((tn, D_MODEL), lambda i, j: (j, 0)), blk, blk],
        out_specs=[blk, blk],
        out_shape=[jax.ShapeDtypeStruct((T, D_FF), BF16), jax.ShapeDtypeStruct((T, D_FF), BF16)],
        compiler_params=_params(dimension_semantics=("parallel", "parallel")),
    )(dx3b, w, act_by_gate, act_by_up)


def _ffn_in_bwd(dgate, dup, w, dx3, x2, g2, *, tm, jobs=()):
    T = x2.shape[0]

    def body(dgate_ref, dup_ref, w_ref, dx3_ref, x2_ref, g_ref, dx_ref, dxb_ref, dg_ref):
        @pl.when(pl.program_id(0) == 0)
        def _():
            dg_ref[...] = jnp.zeros_like(dg_ref)

        dh = _dot_nt(dgate_ref[...], w_ref[:, :D_FF]) + _dot_nt(dup_ref[...], w_ref[:, D_FF:])
        dxn, dgp = _rms_bwd(dh, x2_ref[...], g_ref[...])
        dx = dx3_ref[...] + dxn
        dg_ref[...] += jnp.sum(dgp, axis=0, keepdims=True)
        dx_ref[...] = dx
        dxb_ref[...] = dx.astype(BF16)

    row = pl.BlockSpec((tm, D_MODEL), lambda i: (i, 0))
    wide = pl.BlockSpec((tm, D_FF), lambda i: (i, 0))
    vec = pl.BlockSpec((1, D_MODEL), lambda i: (0, 0))
    return _pallas(
        body, (dgate, dup, w, dx3, x2, g2), name="ffn_in_bwd", grid=(T // tm,),
        in_specs=[wide, wide, pl.BlockSpec((D_MODEL, 2 * D_FF), lambda i: (0, 0)), row, row, vec],
        out_specs=[row, row, vec],
        out_shape=[jax.ShapeDtypeStruct((T, D_MODEL), F32), jax.ShapeDtypeStruct((T, D_MODEL), BF16),
                   jax.ShapeDtypeStruct((1, D_MODEL), F32)],
        semantics=("arbitrary",), jobs=jobs)


def _out_proj_bwd(dx2b, w_o, mix_by, *, tm, tn, jobs=()):
    T = dx2b.shape[0]

    def body(dx_ref, w_ref, *refs):
        dmix = _dot_nt(dx_ref[...], w_ref[...])
        for by_ref, d_ref in zip(refs[:4], refs[4:]):
            d_ref[...] = (dmix * by_ref[...].astype(F32)).astype(BF16)

    blk = pl.BlockSpec((tm, tn), lambda i, j: (i, j))
    out = jax.ShapeDtypeStruct((T, D_MODEL), BF16)
    return _pallas(
        body, (dx2b, w_o, *mix_by), name="out_proj_bwd", grid=(T // tm, D_MODEL // tn),
        in_specs=[pl.BlockSpec((tm, D_MODEL), lambda i, j: (i, 0)), pl.BlockSpec((tn, D_MODEL), lambda i, j: (j, 0))] + [blk] * 4,
        out_specs=[blk] * 4, out_shape=[out] * 4, semantics=("parallel", "parallel"), jobs=jobs)


def _branch_bwd(dyp, dyr, w_pool_out, w_rnn_out, *, tm):
    T = dyp.shape[0]

    def body(dyp_ref, dyr_ref, wp_ref, wr_ref, dpm_ref, dz_ref):
        dpm_ref[...] = _dot_nt(dyp_ref[...], wp_ref[...])
        dz_ref[...] = _dot_nt(dyr_ref[...], wr_ref[...])

    row = pl.BlockSpec((tm, D_MODEL), lambda i: (i, 0))
    return pl.pallas_call(
        body, name="branch_bwd", grid=(T // tm,),
        in_specs=[row, row, pl.BlockSpec((D_POOL, D_MODEL), lambda i: (0, 0)), pl.BlockSpec((D_RNN, D_MODEL), lambda i: (0, 0))],
        out_specs=[pl.BlockSpec((tm, D_POOL), lambda i: (i, 0)), pl.BlockSpec((tm, D_RNN), lambda i: (i, 0))],
        out_shape=[jax.ShapeDtypeStruct((T, D_POOL), F32), jax.ShapeDtypeStruct((T, D_RNN), F32)],
        compiler_params=_params(dimension_semantics=("parallel",)),
    )(dyp, dyr, w_pool_out, w_rnn_out)


def _in_proj_bwd(segs, w, dx2, x, g1, *, tm, jobs=()):
    T = x.shape[0]
    widths = [s.shape[1] for s in segs]
    offs = [sum(widths[:k]) for k in range(len(widths))]
    n = len(segs)

    def body(*refs):
        seg_refs, (w_ref, dx2_ref, x_ref, g_ref, dx_ref, dg_ref) = refs[:n], refs[n:]

        @pl.when(pl.program_id(0) == 0)
        def _():
            dg_ref[...] = jnp.zeros_like(dg_ref)

        dh = _dot_nt(seg_refs[0][...], w_ref[:, offs[0]:offs[0] + widths[0]])
        for k in range(1, n):
            dh += _dot_nt(seg_refs[k][...], w_ref[:, offs[k]:offs[k] + widths[k]])
        dxn, dgp = _rms_bwd(dh, x_ref[...], g_ref[...])
        dg_ref[...] += jnp.sum(dgp, axis=0, keepdims=True)
        dx_ref[...] = dx2_ref[...] + dxn

    row = pl.BlockSpec((tm, D_MODEL), lambda i: (i, 0))
    vec = pl.BlockSpec((1, D_MODEL), lambda i: (0, 0))
    return _pallas(
        body, (*segs, w, dx2, x, g1), name="in_proj_bwd", grid=(T // tm,),
        in_specs=[pl.BlockSpec((tm, wd), lambda i: (i, 0)) for wd in widths]
        + [pl.BlockSpec((D_MODEL, D_IN), lambda i: (0, 0)), row, row, vec],
        out_specs=[row, vec],
        out_shape=[jax.ShapeDtypeStruct((T, D_MODEL), F32), jax.ShapeDtypeStruct((1, D_MODEL), F32)],
        semantics=("arbitrary",), jobs=jobs)


def _weight_grad(a, segs, *, tm, tn, name, jobs=None):
    T, M = a.shape
    nblk = [s.shape[1] // tn for s in segs]
    first = [sum(nblk[:k]) for k in range(len(segs))]
    n = len(segs)

    def body(a_ref, *refs):
        seg_refs, o_ref = refs[:n], refs[n]
        j = pl.program_id(1)
        for k in range(n):
            @pl.when((j >= first[k]) & (j < first[k] + nblk[k]))
            def _(k=k):
                o_ref[...] = _dot_tn(a_ref[...], seg_refs[k][...])

    def seg_spec(k):
        return pl.BlockSpec((T, tn), lambda i, j: (0, jnp.clip(j - first[k], 0, nblk[k] - 1)))

    (grad,), results = _pallas(
        body, (a, *segs), name=name, grid=(M // tm, sum(nblk)),
        in_specs=[pl.BlockSpec((T, tm), lambda i, j: (0, i))] + [seg_spec(k) for k in range(n)],
        out_specs=[pl.BlockSpec((tm, tn), lambda i, j: (i, j))],
        out_shape=[jax.ShapeDtypeStruct((M, sum(nblk) * tn), F32)],
        semantics=("parallel", "arbitrary"), jobs=jobs or ())
    return grad if jobs is None else (grad, results)


def _pad_front(dst, src, halo):
    dst[pl.ds(0, halo), :] = jnp.zeros((halo, src.shape[1]), F32)

    def fill(i, carry):
        r0 = pl.multiple_of(i * CHUNK, CHUNK)
        dst[pl.ds(r0 + halo, CHUNK), :] = src[pl.ds(r0, CHUNK), :]
        return carry

    lax.fori_loop(0, src.shape[0] // CHUNK, fill, 0)


def _shift_rows(v, k):
    return pltpu.roll(v, k % v.shape[0], axis=0)


def _window_sums(xs, direction):
    s2 = xs + _shift_rows(xs, direction)
    s4 = s2 + _shift_rows(s2, 2 * direction)
    s8 = s4 + _shift_rows(s4, 4 * direction)
    s16 = s8 + _shift_rows(s8, 8 * direction)
    return s2, s4, s8, s16


def _select_window(g, sums):
    s2, s4, s8, s16 = sums
    return jnp.where(g == 0, s2, jnp.where(g == 1, s4, jnp.where(g == 2, s8, s16)))


def _pool_count(g, start, rows):
    t = start + lax.broadcasted_iota(jnp.int32, (rows, 1), 0)
    return jnp.minimum(t + 1, jnp.left_shift(2, g)).astype(F32)


def _pool_fwd(proj, w_grp, scale):
    T = proj.shape[0]
    nchunk = T // CHUNK

    def body(u_ref, w_ref, s_ref, o_ref, upad):
        g = pl.program_id(0)
        _pad_front(upad, u_ref, POOL_HALO)
        w = w_ref[...].astype(BF16)
        scale_row = s_ref[...]

        def chunk(i, carry):
            r0 = pl.multiple_of(i * CHUNK, CHUNK)
            xs = upad[pl.ds(r0, CHUNK + POOL_HALO), :]
            win = _select_window(g, _window_sums(xs, 1))[POOL_HALO:]
            pooled = win / _pool_count(g, r0, CHUNK) - xs[POOL_HALO:]
            o_ref[pl.ds(r0, CHUNK), :] = (_dot(pooled.astype(BF16), w) * scale_row).astype(BF16)
            return carry

        lax.fori_loop(0, nchunk, chunk, 0)

    return pl.pallas_call(
        body, name="pool_fwd", grid=(N_POOL_GROUPS,),
        in_specs=[pl.BlockSpec((T, HEAD), lambda g: (0, g)), pl.BlockSpec((None, HEAD, HEAD), lambda g: (g, 0, 0)),
                  pl.BlockSpec((1, HEAD), lambda g: (0, g))],
        out_specs=pl.BlockSpec((T, HEAD), lambda g: (0, g)),
        out_shape=jax.ShapeDtypeStruct((T, D_POOL), BF16),
        scratch_shapes=[pltpu.VMEM((T + POOL_HALO, HEAD), F32)],
        compiler_params=_params(dimension_semantics=("parallel",)),
    )(proj, w_grp, scale)


def _pool_bwd(proj, dpm, w_grp, scale, jobs=()):
    T = proj.shape[0]
    nchunk = T // CHUNK

    def body(u_ref, dpm_ref, w_ref, s_ref, du_ref, dw_ref, ds_ref, upad, zpad, dpool):
        g = pl.program_id(0)
        _pad_front(upad, u_ref, POOL_HALO)
        zpad[pl.ds(T, POOL_HALO), :] = jnp.zeros((POOL_HALO, HEAD), F32)
        dw_ref[...] = jnp.zeros_like(dw_ref)
        ds_ref[...] = jnp.zeros_like(ds_ref)
        w = w_ref[...].astype(BF16)
        scale_row = s_ref[...]

        def chunk(i, carry):
            r0 = pl.multiple_of(i * CHUNK, CHUNK)
            xs = upad[pl.ds(r0, CHUNK + POOL_HALO), :]
            cnt = _pool_count(g, r0, CHUNK)
            pooled = (_select_window(g, _window_sums(xs, 1))[POOL_HALO:] / cnt - xs[POOL_HALO:]).astype(BF16)
            mixed = _dot(pooled, w)
            d = dpm_ref[pl.ds(r0, CHUNK), :]
            ds_ref[...] += jnp.sum(d * mixed, axis=0, keepdims=True)
            dmixed = (d * scale_row).astype(BF16)
            dw_ref[...] += _dot_tn(pooled, dmixed)
            dp = _dot_nt(dmixed, w)
            dpool[pl.ds(r0, CHUNK), :] = dp
            zpad[pl.ds(r0, CHUNK), :] = dp / cnt
            return carry

        lax.fori_loop(0, nchunk, chunk, 0)

        def chunk2(i, carry):
            r0 = pl.multiple_of(i * CHUNK, CHUNK)
            zs = zpad[pl.ds(r0, CHUNK + POOL_HALO), :]
            win = _select_window(g, _window_sums(zs, -1))[:CHUNK]
            du_ref[pl.ds(r0, CHUNK), :] = (win - dpool[pl.ds(r0, CHUNK), :]).astype(BF16)
            return carry

        lax.fori_loop(0, nchunk, chunk2, 0)

    col = pl.BlockSpec((T, HEAD), lambda g: (0, g))
    return _pallas(
        body, (proj, dpm, w_grp, scale), name="pool_bwd", grid=(N_POOL_GROUPS,),
        in_specs=[col, col, pl.BlockSpec((None, HEAD, HEAD), lambda g: (g, 0, 0)), pl.BlockSpec((1, HEAD), lambda g: (0, g))],
        out_specs=[col, pl.BlockSpec((None, HEAD, HEAD), lambda g: (g, 0, 0)), pl.BlockSpec((1, HEAD), lambda g: (0, g))],
        out_shape=[jax.ShapeDtypeStruct((T, D_POOL), BF16), jax.ShapeDtypeStruct((N_POOL_GROUPS, HEAD, HEAD), F32),
                   jax.ShapeDtypeStruct((1, D_POOL), F32)],
        scratch_shapes=[pltpu.VMEM((T + POOL_HALO, HEAD), F32), pltpu.VMEM((T + POOL_HALO, HEAD), F32), pltpu.VMEM((T, HEAD), F32)],
        semantics=("parallel",), jobs=jobs)


def _conv_taps(xs, cw):
    v = cw[CONV_WIDTH - 1] * xs[SUBLANES:]
    for k in range(CONV_WIDTH - 1):
        v += cw[k] * _shift_rows(xs, CONV_WIDTH - 1 - k)[SUBLANES:]
    return v


def _tap_rows(cw_ref):
    return [cw_ref[k:k + 1, :] for k in range(CONV_WIDTH)]


def _softplus_neg(lam):
    return jnp.maximum(-lam, 0.0) + _log1p(jnp.exp(-jnp.abs(lam)))


def _lru_gates(v, wa, ba, wx, bx, sp):
    vb = v.astype(BF16)
    ra = _sigmoid(_dot(vb, wa) + ba)
    ix = _sigmoid(_dot(vb, wx) + bx)
    log_a = -LRU_C * ra * sp
    a = jnp.exp(log_a)
    sq = jnp.sqrt(-jnp.tanh(log_a) * (a * a + 1.0))
    return ra, ix, a, sq


def _row_bcast(v, r):
    return jnp.broadcast_to(v[r:r + 1, :], v.shape)


TILE_BLOCK = 128


def _scan_in_tiles(coef, coef_shift, A_out, B, T, direction):
    order = list(range(SUBLANES)) if direction == 1 else list(range(SUBLANES - 1, -1, -1))
    tiles = min(TILE_BLOCK, T // SUBLANES)
    for base in range(0, T, tiles * SUBLANES):
        def rows(r, base=base):
            return pl.ds(base + r, tiles, stride=SUBLANES)

        A, Bv = coef[rows(order[0] + coef_shift), :], B[rows(order[0]), :]
        A_out[rows(order[0]), :] = A
        for r in order[1:]:
            a = coef[rows(r + coef_shift), :]
            Bv = a * Bv + B[rows(r), :]
            A = a * A
            A_out[rows(r), :] = A
            B[rows(r), :] = Bv


TILES_PER_STEP = 8


def _carry_tiles(A_s, B_s, out, ntile, direction):
    out_row = SUBLANES - 1 if direction == 1 else 0

    def step(k, carry):
        for j in range(TILES_PER_STEP):
            t = k * TILES_PER_STEP + j
            r0 = pl.multiple_of((t if direction == 1 else ntile - 1 - t) * SUBLANES, SUBLANES)
            A, B = A_s[pl.ds(r0, SUBLANES), :], B_s[pl.ds(r0, SUBLANES), :]
            out[pl.ds(r0, SUBLANES), :] = A * carry + B
            carry = _row_bcast(A, out_row) * carry + _row_bcast(B, out_row)
        return carry

    lax.fori_loop(0, ntile // TILES_PER_STEP, step, jnp.zeros((SUBLANES, HEAD), F32))


def _rnn_fwd(proj, conv_w, conv_b, w_a, b_a, w_x, b_x, lam, jobs=()):
    T = proj.shape[0]
    nchunk = T // CHUNK
    ntile = T // SUBLANES

    def body(u_ref, ug_ref, cw_ref, cb_ref, wa_ref, ba_ref, wx_ref, bx_ref, lam_ref,
             h_ref, z_ref, v_ref, ra_ref, ix_ref, a_ref, sq_ref, upad, a_s, b_s):
        _pad_front(upad, u_ref, SUBLANES)
        cw, cb = _tap_rows(cw_ref), cb_ref[...]
        wa, wx = wa_ref[...].astype(BF16), wx_ref[...].astype(BF16)
        ba, bx = ba_ref[...], bx_ref[...]
        sp = _softplus_neg(lam_ref[...])

        def chunk(i, carry):
            rows = pl.ds(pl.multiple_of(i * CHUNK, CHUNK), CHUNK)
            v = _conv_taps(upad[pl.ds(pl.multiple_of(i * CHUNK, CHUNK), CHUNK + SUBLANES), :], cw) + cb
            ra, ix, a, sq = _lru_gates(v, wa, ba, wx, bx, sp)
            v_ref[rows, :], ra_ref[rows, :], ix_ref[rows, :], a_ref[rows, :], sq_ref[rows, :] = v, ra, ix, a, sq
            a_s[rows, :], b_s[rows, :] = a, sq * ix * v
            return carry

        lax.fori_loop(0, nchunk, chunk, 0)
        _scan_in_tiles(a_s, 0, a_s, b_s, T, 1)
        _carry_tiles(a_s, b_s, h_ref, ntile, 1)

        def chunk3(i, carry):
            r0 = pl.multiple_of(i * CHUNK, CHUNK)
            gl, _ = _gelu_parts(ug_ref[pl.ds(r0, CHUNK), :])
            z_ref[pl.ds(r0, CHUNK), :] = (h_ref[pl.ds(r0, CHUNK), :] * gl).astype(BF16)
            return carry

        lax.fori_loop(0, nchunk, chunk3, 0)

    col = pl.BlockSpec((T, HEAD), lambda h: (0, h))
    vec = pl.BlockSpec((1, HEAD), lambda h: (0, h))
    mat = pl.BlockSpec((None, HEAD, HEAD), lambda h: (h, 0, 0))
    return _pallas(
        body, (proj, proj, conv_w, conv_b, w_a, b_a, w_x, b_x, lam), name="rnn_fwd", grid=(N_RNN_HEADS,),
        in_specs=[pl.BlockSpec((T, HEAD), lambda h: (0, COL_RNN + h)), pl.BlockSpec((T, HEAD), lambda h: (0, COL_GATE + h)),
                  pl.BlockSpec((CONV_WIDTH, HEAD), lambda h: (0, h)), vec, mat, vec, mat, vec, vec],
        out_specs=[col] * 7,
        out_shape=[jax.ShapeDtypeStruct((T, D_RNN), F32), jax.ShapeDtypeStruct((T, D_RNN), BF16)]
        + [jax.ShapeDtypeStruct((T, D_RNN), F32)] * 5,
        scratch_shapes=[pltpu.VMEM((T + SUBLANES, HEAD), F32), pltpu.VMEM((T, HEAD), F32), pltpu.VMEM((T, HEAD), F32)],
        semantics=("parallel",), jobs=jobs)


def _rnn_bwd(proj, hr, dz, gates, conv_w, w_a, w_x, lam, jobs=()):
    T = proj.shape[0]
    nchunk = T // CHUNK
    ntile = T // SUBLANES

    def body(u_ref, ug_ref, h_ref, dz_ref, v_ref, ra_ref, ix_ref, a_ref, sq_ref, cw_ref, wa_ref, wx_ref, lam_ref,
             du_ref, dug_ref, dwa_ref, dwx_ref, dba_ref, dbx_ref, dlam_ref, dcb_ref, dcw_ref,
             upad, hpad, apad, g_s, dvpad, ga_s):
        zero_tile = jnp.zeros((SUBLANES, HEAD), F32)
        _pad_front(upad, u_ref, SUBLANES)
        _pad_front(hpad, h_ref, SUBLANES)
        apad[pl.ds(T, SUBLANES), :] = zero_tile
        dvpad[pl.ds(T, SUBLANES), :] = zero_tile
        for ref in (dwa_ref, dwx_ref, dba_ref, dbx_ref, dlam_ref, dcb_ref, dcw_ref):
            ref[...] = jnp.zeros_like(ref)
        cw = _tap_rows(cw_ref)
        wa, wx = wa_ref[...].astype(BF16), wx_ref[...].astype(BF16)
        lam_row = lam_ref[...]
        sp = _softplus_neg(lam_row)

        def chunk(i, carry):
            rows = pl.ds(pl.multiple_of(i * CHUNK, CHUNK), CHUNK)
            apad[rows, :] = a_ref[rows, :]
            gl, dgl = _gelu_parts(ug_ref[rows, :])
            d = dz_ref[rows, :]
            g_s[rows, :] = d * gl
            dug_ref[rows, :] = (d * h_ref[rows, :] * dgl).astype(BF16)
            return carry

        lax.fori_loop(0, nchunk, chunk, 0)

        _scan_in_tiles(apad, 1, ga_s, g_s, T, -1)
        _carry_tiles(ga_s, g_s, g_s, ntile, -1)

        def chunk3(i, carry):
            r0 = pl.multiple_of(i * CHUNK, CHUNK)
            rows = pl.ds(r0, CHUNK)
            g = g_s[rows, :]
            h_prev = _shift_rows(hpad[pl.ds(r0, CHUNK + SUBLANES), :], 1)[SUBLANES:]
            v, ra, ix, sq, a = v_ref[rows, :], ra_ref[rows, :], ix_ref[rows, :], sq_ref[rows, :], a_ref[rows, :]
            d_sq = g * ix * v
            d_ix = g * sq * v
            d_la = a * g * h_prev - d_sq * a * a / sq
            dlam_ref[...] += jnp.sum(d_la * ra, axis=0, keepdims=True)
            d_pa = d_la * (-LRU_C) * sp * ra * (1.0 - ra)
            d_px = d_ix * ix * (1.0 - ix)
            vb, d_pab, d_pxb = v.astype(BF16), d_pa.astype(BF16), d_px.astype(BF16)
            dwa_ref[...] += _dot_tn(vb, d_pab)
            dwx_ref[...] += _dot_tn(vb, d_pxb)
            dba_ref[...] += jnp.sum(d_pa, axis=0, keepdims=True)
            dbx_ref[...] += jnp.sum(d_px, axis=0, keepdims=True)
            dv = g * sq * ix + _dot_nt(d_pab, wa) + _dot_nt(d_pxb, wx)
            dvpad[rows, :] = dv
            dcb_ref[...] += jnp.sum(dv, axis=0, keepdims=True)
            xs = upad[pl.ds(r0, CHUNK + SUBLANES), :]
            for k in range(CONV_WIDTH):
                u_k = _shift_rows(xs, CONV_WIDTH - 1 - k)[SUBLANES:] if k < CONV_WIDTH - 1 else xs[SUBLANES:]
                dcw_ref[k:k + 1, :] += jnp.sum(dv * u_k, axis=0, keepdims=True)
            return carry

        lax.fori_loop(0, nchunk, chunk3, 0)
        dlam_ref[...] = dlam_ref[...] * (LRU_C * _sigmoid(-lam_row))

        def chunk4(i, carry):
            r0 = pl.multiple_of(i * CHUNK, CHUNK)
            dvs = dvpad[pl.ds(r0, CHUNK + SUBLANES), :]
            du = cw[CONV_WIDTH - 1] * dvs[:CHUNK]
            for k in range(CONV_WIDTH - 1):
                du += cw[k] * _shift_rows(dvs, -(CONV_WIDTH - 1 - k))[:CHUNK]
            du_ref[pl.ds(r0, CHUNK), :] = du.astype(BF16)
            return carry

        lax.fori_loop(0, nchunk, chunk4, 0)

    col = pl.BlockSpec((T, HEAD), lambda h: (0, h))
    vec = pl.BlockSpec((1, HEAD), lambda h: (0, h))
    mat = pl.BlockSpec((None, HEAD, HEAD), lambda h: (h, 0, 0))
    taps = pl.BlockSpec((CONV_WIDTH, HEAD), lambda h: (0, h))
    vec_out = jax.ShapeDtypeStruct((1, D_RNN), F32)
    mat_out = jax.ShapeDtypeStruct((N_RNN_HEADS, HEAD, HEAD), F32)
    seq = pltpu.VMEM((T, HEAD), F32)
    seq_pad = pltpu.VMEM((T + SUBLANES, HEAD), F32)
    return _pallas(
        body, (proj, proj, hr, dz, *gates, conv_w, w_a, w_x, lam), name="rnn_bwd", grid=(N_RNN_HEADS,),
        in_specs=[pl.BlockSpec((T, HEAD), lambda h: (0, COL_RNN + h)), pl.BlockSpec((T, HEAD), lambda h: (0, COL_GATE + h))]
        + [col] * 7 + [taps, mat, mat, vec],
        out_specs=[col, col, mat, mat, vec, vec, vec, vec, taps],
        out_shape=[jax.ShapeDtypeStruct((T, D_RNN), BF16), jax.ShapeDtypeStruct((T, D_RNN), BF16), mat_out, mat_out,
                   vec_out, vec_out, vec_out, vec_out, jax.ShapeDtypeStruct((CONV_WIDTH, D_RNN), F32)],
        scratch_shapes=[seq_pad, seq_pad, seq_pad, seq, seq_pad, seq],
        semantics=("parallel",), jobs=jobs)


GROUP_FFN_OUT = ["w_ffn_out"]
GROUP_FFN_IN = ["w_ffn_in"]
GROUP_MIX = ["w_o", "w_pool_out", "w_rnn_out"]
GROUP_IN = ["w_in"]


def _step(x, target, s, full, conv_w, place):
    T = x.shape[0]
    tall, mid, low = min(T, 2048), min(T, 1024), min(T, 512)
    full = dict(full)

    def gathered(names, results):
        full.update(zip(names, results))

    early = ["w_pool_out", "w_rnn_out", "w_o"]
    (proj, h1), (res,) = _norm_matmul(x, s["norm_mix"], full["w_in"], tm=tall, tn=512, name="in_proj", jobs=[_gather_job(full, early)])
    gathered(early, res)
    pm = _pool_fwd(proj, s["w_pool_grp"], s["pool_scale"])
    (hr, z, *gates), (res,) = _rnn_fwd(proj, conv_w, s["conv_b"], s["w_rg_a"], s["b_rg_a"], s["w_rg_x"], s["b_rg_x"],
                                       s["lru_lambda"], jobs=[_gather_job(full, ["w_ffn_in"])])
    gathered(["w_ffn_in"], res)
    *mix_by, mix = _branch_mix(pm, z, full["w_pool_out"], full["w_rnn_out"], proj, tm=tall, tn=256)
    x2 = _out_proj_residual(mix, full["w_o"], x, tm=mid)
    (act_by_up, act_by_gate, act, h2), (res,) = _ffn_in(x2, s["norm_ffn"], full["w_ffn_in"], tm=tall, tn=256,
                                                        jobs=[_gather_job(full, ["w_ffn_out"])])
    gathered(["w_ffn_out"], res)
    dx3, dx3b, sq_cols, g_norm_final = _ffn_out_loss(act, full["w_ffn_out"], x2, s["norm_final"], target, tm=low)

    g = {"norm_final": g_norm_final}

    def chip_sums(names, from_sibling):
        sums = {name: _chip_sum(name, g[name], got, place) for name, got in zip(names, from_sibling)}
        return {name: v[0] for name, v in sums.items()}, {name: v[1] for name, v in sums.items()}

    def final_sums(names, sums, from_chips):
        return {name: _final_sum(name, sums[name], got, place) for name, got in zip(names, from_chips)}

    dgate, dup = _ffn_out_bwd(dx3b, full["w_ffn_out"], act_by_gate, act_by_up, tm=tall, tn=256)
    g["w_ffn_out"] = _weight_grad(act, [dx3b], tm=256, tn=D_MODEL, name="w_ffn_out_grad")
    (dx2, dx2b, g["norm_ffn"]), (res,) = _ffn_in_bwd(dgate, dup, full["w_ffn_in"], dx3, x2, s["norm_ffn"], tm=low,
                                                     jobs=[_sibling_job(g, GROUP_FFN_OUT)])
    sums_ffn, sums_ffn_bf16 = chip_sums(GROUP_FFN_OUT, res)
    g["w_ffn_in"], (res,) = _weight_grad(h2, [dgate, dup], tm=D_MODEL, tn=256, name="w_ffn_in_grad",
                                         jobs=[_chips_job(sums_ffn_bf16, GROUP_FFN_OUT)])
    shards_ffn = final_sums(GROUP_FFN_OUT, sums_ffn, res)
    (dgp, dgr, dyp, dyr), (res,) = _out_proj_bwd(dx2b, full["w_o"], mix_by, tm=tall, tn=256,
                                                 jobs=[_sibling_job(g, GROUP_FFN_IN)])
    sums_ffn, sums_ffn_bf16 = chip_sums(GROUP_FFN_IN, res)
    g["w_o"] = _weight_grad(mix, [dx2b], tm=D_MODEL, tn=256, name="w_o_grad")
    dpm, dz = _branch_bwd(dyp, dyr, full["w_pool_out"], full["w_rnn_out"], tm=mid)
    g["w_pool_out"] = _weight_grad(pm, [dyp], tm=D_POOL, tn=256, name="w_pool_out_grad")
    g["w_rnn_out"] = _weight_grad(z, [dyr], tm=D_RNN, tn=256, name="w_rnn_out_grad")
    (dupool, g["w_pool_grp"], g["pool_scale"]), (res,) = _pool_bwd(proj, dpm, s["w_pool_grp"], s["pool_scale"],
                                                                   jobs=[_sibling_job(g, GROUP_MIX)])
    sums_mix, sums_mix_bf16 = chip_sums(GROUP_MIX, res)
    ((durnn, dugate, g["w_rg_a"], g["w_rg_x"], g["b_rg_a"], g["b_rg_x"], g["lru_lambda"], g["conv_b"], g["conv_w"]),
     (res,)) = _rnn_bwd(proj, hr, dz, gates, conv_w, s["w_rg_a"], s["w_rg_x"], s["lru_lambda"],
                        jobs=[_chips_job(sums_ffn_bf16, GROUP_FFN_IN)])
    shards_ffn.update(final_sums(GROUP_FFN_IN, sums_ffn, res))
    segs = [dupool, durnn, dugate, dgp, dgr]
    ffn = GROUP_FFN_OUT + GROUP_FFN_IN
    g["w_in"], (res, joined) = _weight_grad(h1, segs, tm=D_MODEL, tn=256, name="w_in_grad",
                                           jobs=[_chips_job(sums_mix_bf16, GROUP_MIX), _join_job(shards_ffn, ffn)])
    grads = dict(zip(ffn, joined))
    shards = final_sums(GROUP_MIX, sums_mix, res)
    (res,) = _run_jobs([_sibling_job(g, GROUP_IN)], "w_in_exchange_sibling")
    sums_in, sums_in_bf16 = chip_sums(GROUP_IN, res)
    (grad_x, g["norm_mix"]), (res,) = _in_proj_bwd(segs, full["w_in"], dx2, x, s["norm_mix"], tm=low,
                                                  jobs=[_chips_job(sums_in_bf16, GROUP_IN)])
    shards.update(final_sums(GROUP_IN, sums_in, res))

    vec_rows = [g[name] if name != "pool_scale" else jnp.pad(g[name], ((0, 0), (0, D_MODEL - D_POOL))) for name in VEC_ITEMS]
    vec_rows += [g["conv_w"], sq_cols, jnp.zeros((VEC_ROWS - len(VEC_ITEMS) - CONV_WIDTH - 1, D_MODEL), F32)]
    vec = jnp.concatenate(vec_rows, axis=0).reshape(VEC_ROWS, N_DEV, HEAD).transpose(1, 0, 2)
    mat = jnp.concatenate([g[name].reshape(-1, HEAD) for name in MAT_ITEMS], axis=0).reshape(N_DEV, -1, HEAD)
    (vec, mat), (joined,) = _all_reduce_small([vec, mat], jobs=[_join_job(shards, GROUP_MIX + GROUP_IN)])
    grads.update(zip(GROUP_MIX + GROUP_IN, joined))

    vec = vec.transpose(1, 0, 2).reshape(VEC_ROWS, D_MODEL)
    mat = mat.reshape(-1, HEAD)
    for k, name in enumerate(VEC_ITEMS):
        grads[name] = vec[k:k + 1, :s[name].shape[1]]
    grads["conv_w"] = vec[len(VEC_ITEMS):len(VEC_ITEMS) + CONV_WIDTH]
    row = 0
    for name in MAT_ITEMS:
        rows = s[name].shape[0] * HEAD
        grads[name] = mat[row:row + rows]
        row += rows
    return vec[len(VEC_ITEMS) + CONV_WIDTH], grad_x, grads


LARGE = {"w_in": "col", "w_pool_out": "col", "w_rnn_out": "row", "w_o": "row", "w_ffn_in": "col", "w_ffn_out": "row"}
LARGE_SHAPE = {"w_in": (D_MODEL, D_IN), "w_pool_out": (D_POOL, D_MODEL), "w_rnn_out": (D_RNN, D_MODEL),
               "w_o": (D_MODEL, D_MODEL), "w_ffn_in": (D_MODEL, 2 * D_FF), "w_ffn_out": (D_FF, D_MODEL)}


def _place():
    x, y, c = lax.axis_index("x"), lax.axis_index("y"), lax.axis_index("c")
    return 2 * x + y, c


def _chip_device(chip, c):
    return (chip // 2, chip % 2, c)


def _chip_window(ref, kind, shape, chip, half=None):
    K, N = shape
    if kind == "col":
        rows = slice(None) if half is None else pl.ds(half * (K // 2), K // 2)
        return ref.at[rows, pl.ds(chip * (N // N_CHIPS), N // N_CHIPS)]
    ks = K // N_CHIPS
    if half is None:
        return ref.at[pl.ds(chip * ks, ks), :]
    return ref.at[pl.ds(chip * ks + half * (ks // 2), ks // 2), :]


def _row_half(ref, half):
    rows = ref.shape[0] // 2
    return ref.at[pl.ds(half * rows, rows), :]


def _remote(win_src, win_dst, send_sems, recv_sems, idx, to):
    return pltpu.make_async_remote_copy(src_ref=win_src, dst_ref=win_dst, send_sem=send_sems.at[idx], recv_sem=recv_sems.at[idx],
                                        device_id=to, device_id_type=MESH)


def _gather_job(full, names, conv_w_full=None):
    n = len(names)
    cw_cols = D_RNN // N_CHIPS

    def windows(refs, chip, half):
        return [_chip_window(refs[k], LARGE[name], LARGE_SHAPE[name], chip, half) for k, name in enumerate(names)]

    def ici_copies(refs, send_sems, recv_sems, src_chip, dst_chip, c, r):
        wins = windows(refs, src_chip, c)
        if conv_w_full is not None:
            wins.append(refs[n].at[:, pl.ds(src_chip * cw_cols, cw_cols)])
        return [_remote(win, win, send_sems, recv_sems, (k, r), _chip_device(dst_chip, c)) for k, win in enumerate(wins)]

    def forwards(refs, send_sems, recv_sems, src_chip, half, to_core, chip, r):
        return [_remote(win, win, send_sems, recv_sems, (k, 3 + r), _chip_device(chip, to_core))
                for k, win in enumerate(windows(refs, src_chip, half))]

    def start(ins, outs, send_sems, recv_sems):
        chip, c = _place()
        for r in range(3):
            for cp in ici_copies(outs, send_sems, recv_sems, chip, chip ^ (r + 1), c, r):
                cp.start()

    def finish(ins, outs, send_sems, recv_sems):
        chip, c = _place()
        for r in range(3):
            for cp in ici_copies(outs, send_sems, recv_sems, chip ^ (r + 1), chip, c, r):
                cp.wait_recv()
            for cp in forwards(outs, send_sems, recv_sems, chip ^ (r + 1), c, 1 - c, chip, r):
                cp.start()
        for r in range(3):
            for cp in forwards(outs, send_sems, recv_sems, chip ^ (r + 1), 1 - c, c, chip, r):
                cp.wait_recv()
            for cp in ici_copies(outs, send_sems, recv_sems, chip, chip ^ (r + 1), c, r):
                cp.wait_send()
            for cp in forwards(outs, send_sems, recv_sems, chip ^ (r + 1), c, 1 - c, chip, r):
                cp.wait_send()

    arrays = [full[name] for name in names] + ([conv_w_full] if conv_w_full is not None else [])
    return _Job(arrays, [jax.ShapeDtypeStruct(a.shape, a.dtype) for a in arrays], {k: k for k in range(len(arrays))},
                (len(arrays), 6), start, finish)


def _core_halves(ref, kind, shape, c):
    return [_chip_window(ref, kind, shape, chip, c) for chip in range(N_CHIPS)]


def _sibling_job(grads, names):
    def start(ins, outs, send_sems, recv_sems):
        chip, c = _place()
        for k, name in enumerate(names):
            kind, shape = LARGE[name], LARGE_SHAPE[name]
            if kind == "col":
                pairs = [(_row_half(ins[k], 1 - c), outs[k])]
            else:
                rows = shape[0] // N_DEV
                pairs = [(win, outs[k].at[pl.ds(j * rows, rows), :]) for j, win in enumerate(_core_halves(ins[k], kind, shape, 1 - c))]
            for src, dst in pairs:
                _remote(src, dst, send_sems, recv_sems, k, _chip_device(chip, 1 - c)).start()

    def finish(ins, outs, send_sems, recv_sems):
        chip, c = _place()
        for k in range(len(names)):
            _remote(outs[k], outs[k], send_sems, recv_sems, k, _chip_device(chip, 1 - c)).wait()

    return _Job([grads[name] for name in names],
                [jax.ShapeDtypeStruct((LARGE_SHAPE[name][0] // 2, LARGE_SHAPE[name][1]), F32) for name in names], {},
                (len(names),), start, finish)


def _chip_sum(name, g, got, place):
    kind, (K, N) = LARGE[name], LARGE_SHAPE[name]
    rows = K // N_DEV
    piece_cols = N // N_CHIPS

    def body(place_ref, g_ref, got_ref, o_ref, ob_ref):
        total = g_ref[...] + got_ref[...]
        ob_ref[...] = total.astype(BF16)
        if kind == "col":
            for chip in range(N_CHIPS):
                @pl.when(place_ref[0] == chip)
                def _(chip=chip):
                    o_ref[...] = total[:, chip * piece_cols:(chip + 1) * piece_cols]
        else:
            @pl.when(pl.program_id(0) == place_ref[0])
            def _():
                o_ref[...] = total

    if kind == "col":
        mine = pl.BlockSpec((rows, N), lambda j, place_ref: (j + N_CHIPS * place_ref[1], 0))
        own = pl.BlockSpec((rows, piece_cols), lambda j, place_ref: (j, 0))
    else:
        mine = pl.BlockSpec((rows, N), lambda j, place_ref: (2 * j + place_ref[1], 0))
        own = pl.BlockSpec((rows, N), lambda j, place_ref: (0, 0))
    blk = pl.BlockSpec((rows, N), lambda j, place_ref: (j, 0))
    return pl.pallas_call(
        body, name=name + "_chip_sum",
        grid_spec=pltpu.PrefetchScalarGridSpec(num_scalar_prefetch=1, grid=(N_CHIPS,), in_specs=[mine, blk], out_specs=[own, blk]),
        out_shape=[jax.ShapeDtypeStruct(_piece_shape(name), F32), jax.ShapeDtypeStruct((K // 2, N), BF16)],
        compiler_params=_params(dimension_semantics=("arbitrary",)),
    )(place, g, got)


def _piece(ref, kind, shape, chip):
    K, N = shape
    if kind == "col":
        return ref.at[:, pl.ds(chip * (N // N_CHIPS), N // N_CHIPS)]
    return ref.at[pl.ds(chip * (K // N_DEV), K // N_DEV), :]


def _piece_shape(name):
    kind, (K, N) = LARGE[name], LARGE_SHAPE[name]
    return (K // 2, N // N_CHIPS) if kind == "col" else (K // N_DEV, N)


def _chips_job(sums, names):
    def copies(ins, outs, send_sems, recv_sems):
        chip, c = _place()
        return [_remote(_piece(ins[k], LARGE[name], LARGE_SHAPE[name], chip ^ (r + 1)), outs[k].at[r], send_sems, recv_sems, (k, r),
                        _chip_device(chip ^ (r + 1), c)) for k, name in enumerate(names) for r in range(3)]

    def start(*refs):
        for cp in copies(*refs):
            cp.start()

    def finish(*refs):
        for cp in copies(*refs):
            cp.wait()

    return _Job([sums[name] for name in names], [jax.ShapeDtypeStruct((3,) + _piece_shape(name), BF16) for name in names], {},
                (len(names), 3), start, finish)


def _final_sum(name, chip_sum, got, place):
    rows, cols = _piece_shape(name)

    def body(place_ref, s_ref, got_ref, o_ref):
        o_ref[...] = ((s_ref[...] + got_ref[0].astype(F32)) + got_ref[1].astype(F32)) + got_ref[2].astype(F32)

    mine = pl.BlockSpec((rows, cols), lambda i, place_ref: (0, 0))
    return pl.pallas_call(
        body, name=name + "_final_sum",
        grid_spec=pltpu.PrefetchScalarGridSpec(
            num_scalar_prefetch=1, grid=(1,), in_specs=[mine, pl.BlockSpec((3, rows, cols), lambda i, place_ref: (0, 0, 0))],
            out_specs=pl.BlockSpec((rows, cols), lambda i, place_ref: (place_ref[1], 0))),
        out_shape=jax.ShapeDtypeStruct((2 * rows, cols), F32),
        compiler_params=_params(dimension_semantics=("arbitrary",)),
    )(place, chip_sum, got)


def _join_job(shards, names):
    def half_copy(outs, send_sems, recv_sems, k, mine):
        chip, c = _place()
        win = _row_half(outs[k], c if mine else 1 - c)
        return _remote(win, win, send_sems, recv_sems, k, _chip_device(chip, 1 - c))

    def start(ins, outs, send_sems, recv_sems):
        for k in range(len(names)):
            half_copy(outs, send_sems, recv_sems, k, True).start()

    def finish(ins, outs, send_sems, recv_sems):
        for k in range(len(names)):
            half_copy(outs, send_sems, recv_sems, k, True).wait_send()
            half_copy(outs, send_sems, recv_sems, k, False).wait_recv()

    arrays = [shards[name] for name in names]
    return _Job(arrays, [jax.ShapeDtypeStruct(a.shape, F32) for a in arrays], {k: k for k in range(len(arrays))},
                (len(arrays),), start, finish)


VEC_ROWS = 16


def _all_reduce_small(slabs, jobs=()):
    n = len(slabs)

    def body(*refs):
        in_refs, out_refs, got_refs = refs[:n], refs[n:2 * n], refs[2 * n:3 * n]
        send_sems, recv_sems = refs[3 * n:]
        x, y, c = lax.axis_index("x"), lax.axis_index("y"), lax.axis_index("c")
        me = 4 * x + 2 * y + c

        def remote(src, dst, k, phase, r):
            other = me ^ r
            return pltpu.make_async_remote_copy(src_ref=src, dst_ref=dst, send_sem=send_sems.at[k, phase, r],
                                                recv_sem=recv_sems.at[k, phase, r],
                                                device_id=(other // 4, (other // 2) % 2, other % 2), device_id_type=MESH)

        scatter = [remote(in_refs[k].at[me ^ r], got_refs[k].at[r], k, 0, r) for r in range(1, N_DEV) for k in range(n)]
        for cp in scatter:
            cp.start()
        for cp in scatter:
            cp.wait()
        for k in range(n):
            total = in_refs[k][me]
            for r in range(1, N_DEV):
                total = total + got_refs[k][r]
            out_refs[k][me] = total
        gather = [remote(out_refs[k].at[me], out_refs[k].at[me], k, 1, r) for r in range(1, N_DEV) for k in range(n)]
        for cp in gather:
            cp.start()
        for r in range(1, N_DEV):
            for k in range(n):
                remote(out_refs[k].at[me ^ r], out_refs[k].at[me ^ r], k, 1, r).wait_recv()
        for cp in gather:
            cp.wait_send()

    return _pallas(
        body, slabs, name="all_reduce_small", grid=(), in_specs=[VMEM] * n, out_specs=[VMEM] * n,
        out_shape=[jax.ShapeDtypeStruct(s.shape, F32) for s in slabs],
        scratch_shapes=[pltpu.VMEM(s.shape, F32) for s in slabs]
        + [pltpu.SemaphoreType.DMA((n, 2, N_DEV)), pltpu.SemaphoreType.DMA((n, 2, N_DEV))], jobs=jobs)


def _cast_into_whole(w, name, place):
    rows, cols = w.shape
    tr = rows // 2

    def body(place_ref, w_ref, o_ref):
        o_ref[...] = w_ref[...].astype(BF16)

    if LARGE[name] == "col":
        window = pl.BlockSpec((tr, cols), lambda i, place_ref: (i, place_ref[0]))
    else:
        window = pl.BlockSpec((tr, cols), lambda i, place_ref: (2 * place_ref[0] + i, 0))
    return pl.pallas_call(
        body, name=name + "_cast",
        grid_spec=pltpu.PrefetchScalarGridSpec(num_scalar_prefetch=1, grid=(2,),
                                               in_specs=[pl.BlockSpec((tr, cols), lambda i, place_ref: (i, 0))], out_specs=window),
        out_shape=jax.ShapeDtypeStruct(LARGE_SHAPE[name], BF16),
        compiler_params=_params(dimension_semantics=("parallel",)))(place, w)


def _cast_many_into_whole(shards, place, jobs):
    names = list(shards)
    n = len(names)

    def body(place_ref, *refs):
        for w_ref, o_ref in zip(refs[:n], refs[n:]):
            o_ref[...] = w_ref[...].astype(BF16)

    def window(name):
        rows, cols = shards[name].shape
        if LARGE[name] == "col":
            return pl.BlockSpec((rows // 2, cols), lambda i, place_ref: (i, place_ref[0]))
        return pl.BlockSpec((rows // 2, cols), lambda i, place_ref: (2 * place_ref[0] + i, 0))

    def half(name):
        rows, cols = shards[name].shape
        return pl.BlockSpec((rows // 2, cols), lambda i, place_ref: (i, 0))

    return _pallas(body, [shards[name] for name in names], name="cast_weights", grid=(2,),
                   in_specs=[half(name) for name in names], out_specs=[window(name) for name in names],
                   out_shape=[jax.ShapeDtypeStruct(LARGE_SHAPE[name], BF16) for name in names],
                   semantics=("arbitrary",), jobs=jobs, prefetch=place)


def _adamw_math(w, g, m, v):
    m = ADAM_B1 * m + (1.0 - ADAM_B1) * g
    v = ADAM_B2 * v + (1.0 - ADAM_B2) * (g * g)
    m_hat = m / (1.0 - ADAM_B1 ** ADAM_STEP)
    v_hat = v / (1.0 - ADAM_B2 ** ADAM_STEP)
    delta = -ADAM_LR * (m_hat / (jnp.sqrt(v_hat) + ADAM_EPS) + ADAM_WD * w)
    return delta, m, v


def _adamw_large(w, g, m, v, name):
    rows, cols = w.shape
    steps = 2
    tr = rows // steps

    def body(w_ref, g_ref, m_ref, v_ref, d_ref, mo_ref, vo_ref):
        d_ref[...], mo_ref[...], vo_ref[...] = _adamw_math(w_ref[...], g_ref[...], m_ref[...], v_ref[...])

    blk = pl.BlockSpec((tr, cols), lambda i: (i, 0))
    out = jax.ShapeDtypeStruct(w.shape, F32)
    return pl.pallas_call(body, name=name + "_adamw", grid=(steps,), in_specs=[blk] * 4, out_specs=[blk] * 3, out_shape=[out] * 3,
                          compiler_params=_params(dimension_semantics=("parallel",)))(w, g, m, v)


def _adamw_small(ws, gs, ms, vs):
    n = len(ws)

    def body(*refs):
        for k in range(n):
            w_ref, g_ref, m_ref, v_ref = (refs[q * n + k] for q in range(4))
            d_ref, mo_ref, vo_ref = (refs[(4 + q) * n + k] for q in range(3))
            d_ref[...], mo_ref[...], vo_ref[...] = _adamw_math(w_ref[...], g_ref[...], m_ref[...], v_ref[...])

    out = [jax.ShapeDtypeStruct(w.shape, F32) for w in ws]
    res = pl.pallas_call(body, name="small_adamw", in_specs=[VMEM] * (4 * n), out_specs=[VMEM] * (3 * n), out_shape=out * 3,
                         compiler_params=_params())(*ws, *gs, *ms, *vs)
    return res[:n], res[n:2 * n], res[2 * n:]


WEIGHTS = ["norm_mix", "w_in", "w_pool_grp", "pool_scale", "w_pool_out", "conv_w", "conv_b", "w_rg_a", "b_rg_a", "w_rg_x",
           "b_rg_x", "lru_lambda", "w_rnn_out", "w_o", "norm_ffn", "w_ffn_in", "w_ffn_out", "norm_final"]
VEC_ITEMS = ["norm_mix", "norm_ffn", "norm_final", "pool_scale", "conv_b", "lru_lambda", "b_rg_a", "b_rg_x"]
MAT_ITEMS = ["w_pool_grp", "w_rg_a", "w_rg_x"]


def _as2d(name, a):
    if name in MAT_ITEMS:
        return a.reshape(-1, HEAD, HEAD)
    if name == "conv_w":
        return a.reshape(CONV_WIDTH, -1)
    return a.reshape(1, -1)


def kernel(x, norm_mix, w_in, w_pool_grp, pool_scale, w_pool_out, conv_w, conv_b, w_rg_a, b_rg_a, w_rg_x, b_rg_x, lru_lambda, w_rnn_out, w_o, norm_ffn, w_ffn_in, w_ffn_out, norm_final, loss_target, m_norm_mix, m_w_in, m_w_pool_grp, m_pool_scale, m_w_pool_out, m_conv_w, m_conv_b, m_w_rg_a, m_b_rg_a, m_w_rg_x, m_b_rg_x, m_lru_lambda, m_w_rnn_out, m_w_o, m_norm_ffn, m_w_ffn_in, m_w_ffn_out, m_norm_final, v_norm_mix, v_w_in, v_w_pool_grp, v_pool_scale, v_w_pool_out, v_conv_w, v_conv_b, v_w_rg_a, v_b_rg_a, v_w_rg_x, v_b_rg_x, v_lru_lambda, v_w_rnn_out, v_w_o, v_norm_ffn, v_w_ffn_in, v_w_ffn_out, v_norm_final):
    given = dict(locals())
    w = {name: given[name] for name in WEIGHTS}
    m = {name: given["m_" + name] for name in WEIGHTS}
    v = {name: given["v_" + name] for name in WEIGHTS}
    chip, c = _place()

    place = jnp.stack([chip, c]).astype(jnp.int32)
    conv_cols = w["conv_w"].shape[-1]
    conv_w_mine = lax.dynamic_update_slice_in_dim(jnp.zeros((CONV_WIDTH, D_RNN), F32), w["conv_w"][0], chip * conv_cols, axis=1)
    w_in_mine = _cast_into_whole(w["w_in"][0], "w_in", place)
    later = [name for name in LARGE if name != "w_in"]
    casts, ((w_in_full, conv_w_full),) = _cast_many_into_whole(
        {name: w[name][0] for name in later}, place, jobs=[_gather_job({"w_in": w_in_mine}, ["w_in"], conv_w_mine)])
    full = dict(zip(later, casts), w_in=w_in_full)
    small = {name: _as2d(name, w[name]) for name in WEIGHTS if name not in LARGE and name != "conv_w"}
    sq_cols, grad_x, grads = _step(x[0], loss_target[0], small, full, conv_w_full, place)
    loss = 0.5 / D_MODEL * jnp.sum(sq_cols)
    grads["conv_w"] = lax.dynamic_slice_in_dim(grads["conv_w"], chip * conv_cols, conv_cols, axis=1)

    delta, new_m, new_v = {}, {}, {}
    for name in LARGE:
        delta[name], new_m[name], new_v[name] = _adamw_large(w[name][0], grads[name], m[name][0], v[name][0], name)
    small_names = [name for name in WEIGHTS if name not in LARGE]
    flat = lambda d: [d[name].reshape(grads[name].shape) for name in small_names]
    ds, mo, vo = _adamw_small(flat(w), [grads[name] for name in small_names], flat(m), flat(v))
    for k, name in enumerate(small_names):
        delta[name], new_m[name], new_v[name] = ds[k], mo[k], vo[k]

    shaped = lambda d: [d[name].reshape(w[name].shape) for name in WEIGHTS]
    return (loss, grad_x[None], *shaped(grads), *shaped(delta), *shaped(new_m), *shaped(new_v))
```

```python
import functools
import math

import jax
import jax.numpy as jnp
from jax import lax
from jax.experimental import pallas as pl
from jax.experimental.pallas import tpu as pltpu

F32 = jnp.float32
BF16 = jnp.bfloat16

D_MODEL = 1024
D_POOL = 512
N_POOL_GROUPS = 4
D_RNN = 1024
N_RNN_HEADS = 8
HEAD = 128
CONV_WIDTH = 4
LRU_C = 8.0
D_FF = 2816
D_IN = D_POOL + 2 * D_RNN + 2 * D_MODEL
NORM_EPS = 1e-6
COL_RNN = D_POOL // HEAD
COL_GATE = (D_POOL + D_RNN) // HEAD

ADAM_LR = 0.001
ADAM_B1 = 0.9
ADAM_B2 = 0.999
ADAM_EPS = 1e-08
ADAM_WD = 0.01
ADAM_STEP = 10

N_CHIPS = 4
N_DEV = 8
MESH = pl.DeviceIdType.MESH
ANY = pl.BlockSpec(memory_space=pl.ANY)
VMEM = pl.BlockSpec(memory_space=pltpu.VMEM)
VMEM_LIMIT_BYTES = 60 * 1024 * 1024
SUBLANES = 8
POOL_HALO = 16
CHUNK = 1024

GELU_C = math.sqrt(2.0 / math.pi)
GELU_A = 0.044715


def _params(**kw):
    return pltpu.CompilerParams(vmem_limit_bytes=VMEM_LIMIT_BYTES, **kw)


def _sigmoid(x):
    return 0.5 * jnp.tanh(0.5 * x) + 0.5


def _log1p(y):
    u = 1.0 + y
    d = u - 1.0
    return jnp.where(d == 0.0, y, jnp.log(u) * (y / jnp.where(d == 0.0, 1.0, d)))


def _gelu_parts(x):
    x2 = x * x
    th = jnp.tanh(GELU_C * (x + GELU_A * x * x2))
    g = 0.5 * x * (1.0 + th)
    dg = 0.5 * (1.0 + th) + 0.5 * x * (1.0 - th * th) * GELU_C * (1.0 + 3.0 * GELU_A * x2)
    return g, dg


def _dot(a, b):
    return jnp.dot(a, b, preferred_element_type=F32)


def _dot_nt(a, b):
    return lax.dot_general(a, b, (((1,), (1,)), ((), ())), preferred_element_type=F32)


def _dot_tn(a, b):
    return lax.dot_general(a, b, (((0,), (0,)), ((), ())), preferred_element_type=F32)


def _rms_scale(xv):
    return lax.rsqrt(jnp.mean(xv * xv, axis=-1, keepdims=True) + NORM_EPS)


def _rms_bwd(dy, xv, g):
    r = _rms_scale(xv)
    xh = xv * r
    dyg = dy * g
    dx = r * (dyg - xh * jnp.mean(dyg * xh, axis=-1, keepdims=True))
    return dx, dy * xh


class _Job:
    def __init__(self, inputs, out_shapes, aliases, sem_shape, start, finish):
        self.inputs, self.out_shapes, self.aliases, self.sem_shape = list(inputs), list(out_shapes), dict(aliases), sem_shape
        self.start, self.finish = start, finish


def _pallas(body, operands, *, name, grid, in_specs, out_specs, out_shape, scratch_shapes=(), semantics=None, jobs=(),
            prefetch=None):
    n_in, n_out, n_scr = len(in_specs), len(out_specs), len(scratch_shapes)
    n_pre = 0 if prefetch is None else 1
    job_in = [a for job in jobs for a in job.inputs]
    job_out = [s for job in jobs for s in job.out_shapes]
    aliases, i0, o0 = {}, n_pre + n_in, n_out
    for job in jobs:
        aliases.update({i0 + i: o0 + o for i, o in job.aliases.items()})
        i0, o0 = i0 + len(job.inputs), o0 + len(job.out_shapes)

    def whole(*refs):
        pre, refs = refs[:n_pre], refs[n_pre:]
        ins, j_ins = refs[:n_in], refs[n_in:n_in + len(job_in)]
        outs = refs[n_in + len(job_in):][:n_out]
        j_outs = refs[n_in + len(job_in) + n_out:][:len(job_out)]
        rest = refs[n_in + len(job_in) + n_out + len(job_out):]
        scr, sems = rest[:n_scr], rest[n_scr:]

        def run(phase):
            i, o = 0, 0
            for k, job in enumerate(jobs):
                getattr(job, phase)(j_ins[i:i + len(job.inputs)], j_outs[o:o + len(job.out_shapes)], sems[2 * k], sems[2 * k + 1])
                i, o = i + len(job.inputs), o + len(job.out_shapes)

        def at(step_of, phase):
            if not jobs:
                return
            if not grid:
                run(phase)
                return
            cond = functools.reduce(jnp.logical_and, [pl.program_id(d) == step_of(d) for d in range(len(grid))])
            pl.when(cond)(functools.partial(run, phase))

        at(lambda d: 0, "start")
        body(*pre, *ins, *outs, *scr)
        at(lambda d: grid[d] - 1, "finish")

    layout = dict(grid=grid, in_specs=list(in_specs) + [ANY] * len(job_in), out_specs=list(out_specs) + [ANY] * len(job_out),
                  scratch_shapes=list(scratch_shapes) + [pltpu.SemaphoreType.DMA(job.sem_shape) for job in jobs for _ in range(2)])
    if prefetch is not None:
        layout = dict(grid_spec=pltpu.PrefetchScalarGridSpec(num_scalar_prefetch=1, **layout))
    res = pl.pallas_call(
        whole, name=name, out_shape=list(out_shape) + job_out, input_output_aliases=aliases,
        compiler_params=_params(dimension_semantics=semantics, has_side_effects=bool(jobs)), **layout,
    )(*([] if prefetch is None else [prefetch]), *operands, *job_in)
    per_job, o = [], n_out
    for job in jobs:
        per_job.append(res[o:o + len(job.out_shapes)])
        o += len(job.out_shapes)
    return res[:n_out], per_job


def _run_jobs(jobs, name):
    return _pallas(lambda: None, [], name=name, grid=(), in_specs=[], out_specs=[], out_shape=[], jobs=jobs)[1]


NORM_ROWS = 256


def _norm_rows(x_ref, g_ref, h_ref):
    g = g_ref[...]

    def rows(i, carry):
        r = pl.ds(pl.multiple_of(i * NORM_ROWS, NORM_ROWS), NORM_ROWS)
        xv = x_ref[r, :]
        h_ref[r, :] = (xv * _rms_scale(xv) * g).astype(BF16)
        return carry

    lax.fori_loop(0, x_ref.shape[0] // NORM_ROWS, rows, 0)


def _norm_matmul(x, g, w, *, tm, tn, name, jobs=()):
    T, K = x.shape
    N = w.shape[1]

    def body(x_ref, g_ref, w_ref, o_ref, h_ref):
        @pl.when(pl.program_id(1) == 0)
        def _():
            _norm_rows(x_ref, g_ref, h_ref)

        o_ref[...] = _dot(h_ref[...], w_ref[...])

    return _pallas(
        body, (x, g, w), name=name, grid=(T // tm, N // tn),
        in_specs=[pl.BlockSpec((tm, K), lambda i, j: (i, 0)), pl.BlockSpec((1, K), lambda i, j: (0, 0)),
                  pl.BlockSpec((K, tn), lambda i, j: (0, j))],
        out_specs=[pl.BlockSpec((tm, tn), lambda i, j: (i, j)), pl.BlockSpec((tm, K), lambda i, j: (i, 0))],
        out_shape=[jax.ShapeDtypeStruct((T, N), F32), jax.ShapeDtypeStruct((T, K), BF16)],
        semantics=("parallel", "arbitrary"), jobs=jobs)


def _ffn_in(x2, g, w, *, tm, tn, jobs=()):
    T, K = x2.shape
    nb = D_FF // tn

    def body(x_ref, g_ref, wg_ref, wu_ref, dup_ref, dgate_ref, act_ref, h_ref):
        @pl.when(pl.program_id(1) == 0)
        def _():
            _norm_rows(x_ref, g_ref, h_ref)

        h = h_ref[...]
        gate = _dot(h, wg_ref[...])
        up = _dot(h, wu_ref[...])
        s = _sigmoid(gate)
        silu = gate * s
        dup_ref[...] = silu.astype(BF16)
        dgate_ref[...] = (up * (s + silu * (1.0 - s))).astype(BF16)
        act_ref[...] = (silu * up).astype(BF16)

    blk = pl.BlockSpec((tm, tn), lambda i, j: (i, j))
    return _pallas(
        body, (x2, g, w, w), name="ffn_in", grid=(T // tm, nb),
        in_specs=[pl.BlockSpec((tm, K), lambda i, j: (i, 0)), pl.BlockSpec((1, K), lambda i, j: (0, 0)),
                  pl.BlockSpec((K, tn), lambda i, j: (0, j)), pl.BlockSpec((K, tn), lambda i, j: (0, j + nb))],
        out_specs=[blk, blk, blk, pl.BlockSpec((tm, K), lambda i, j: (i, 0))],
        out_shape=[jax.ShapeDtypeStruct((T, D_FF), BF16), jax.ShapeDtypeStruct((T, D_FF), BF16),
                   jax.ShapeDtypeStruct((T, D_FF), BF16), jax.ShapeDtypeStruct((T, K), BF16)],
        semantics=("parallel", "arbitrary"), jobs=jobs)


def _branch_mix(pm, z, w_pool_out, w_rnn_out, proj, *, tm, tn):
    T = pm.shape[0]
    col_gp = (D_POOL + 2 * D_RNN) // tn
    col_gr = col_gp + D_MODEL // tn

    def body(pm_ref, z_ref, wp_ref, wr_ref, gp_ref, gr_ref, by_gp_ref, by_gr_ref, sp_ref, sr_ref, mix_ref):
        yp = _dot(pm_ref[...], wp_ref[...])
        yr = _dot(z_ref[...], wr_ref[...])
        sp, sr = _sigmoid(gp_ref[...]), _sigmoid(gr_ref[...])
        by_gp_ref[...] = (yp * sp * (1.0 - sp)).astype(BF16)
        by_gr_ref[...] = (yr * sr * (1.0 - sr)).astype(BF16)
        sp_ref[...] = sp.astype(BF16)
        sr_ref[...] = sr.astype(BF16)
        mix_ref[...] = (sp * yp + sr * yr).astype(BF16)

    blk = pl.BlockSpec((tm, tn), lambda i, j: (i, j))
    out = jax.ShapeDtypeStruct((T, D_MODEL), BF16)
    return pl.pallas_call(
        body, name="branch_mix", grid=(T // tm, D_MODEL // tn),
        in_specs=[pl.BlockSpec((tm, D_POOL), lambda i, j: (i, 0)), pl.BlockSpec((tm, D_RNN), lambda i, j: (i, 0)),
                  pl.BlockSpec((D_POOL, tn), lambda i, j: (0, j)), pl.BlockSpec((D_RNN, tn), lambda i, j: (0, j)),
                  pl.BlockSpec((tm, tn), lambda i, j: (i, col_gp + j)), pl.BlockSpec((tm, tn), lambda i, j: (i, col_gr + j))],
        out_specs=[blk] * 5, out_shape=[out] * 5,
        compiler_params=_params(dimension_semantics=("parallel", "parallel")),
    )(pm, z, w_pool_out, w_rnn_out, proj, proj)


def _out_proj_residual(mix, w_o, x, *, tm):
    T = x.shape[0]

    def body(mix_ref, w_ref, x_ref, o_ref):
        o_ref[...] = x_ref[...] + _dot(mix_ref[...], w_ref[...])

    row = pl.BlockSpec((tm, D_MODEL), lambda i: (i, 0))
    return pl.pallas_call(
        body, name="out_proj_residual", grid=(T // tm,),
        in_specs=[row, pl.BlockSpec((D_MODEL, D_MODEL), lambda i: (0, 0)), row],
        out_specs=row, out_shape=jax.ShapeDtypeStruct((T, D_MODEL), F32),
        compiler_params=_params(dimension_semantics=("parallel",)),
    )(mix, w_o, x)


def _ffn_out_loss(act, w, x2, g3, target, *, tm):
    T = x2.shape[0]

    def body(act_ref, w_ref, x2_ref, g_ref, t_ref, dx_ref, dxb_ref, sq_ref, dg_ref):
        @pl.when(pl.program_id(0) == 0)
        def _():
            sq_ref[...] = jnp.zeros_like(sq_ref)
            dg_ref[...] = jnp.zeros_like(dg_ref)

        x3 = x2_ref[...] + _dot(act_ref[...], w_ref[...])
        g = g_ref[...]
        err = x3 * _rms_scale(x3) * g - t_ref[...]
        sq_ref[...] += jnp.sum(err * err, axis=0, keepdims=True)
        dx, dgp = _rms_bwd(err * (1.0 / D_MODEL), x3, g)
        dg_ref[...] += jnp.sum(dgp, axis=0, keepdims=True)
        dx_ref[...] = dx
        dxb_ref[...] = dx.astype(BF16)

    row = pl.BlockSpec((tm, D_MODEL), lambda i: (i, 0))
    vec = pl.BlockSpec((1, D_MODEL), lambda i: (0, 0))
    return pl.pallas_call(
        body, name="ffn_out_loss", grid=(T // tm,),
        in_specs=[pl.BlockSpec((tm, D_FF), lambda i: (i, 0)), pl.BlockSpec((D_FF, D_MODEL), lambda i: (0, 0)), row, vec, row],
        out_specs=[row, row, vec, vec],
        out_shape=[jax.ShapeDtypeStruct((T, D_MODEL), F32), jax.ShapeDtypeStruct((T, D_MODEL), BF16),
                   jax.ShapeDtypeStruct((1, D_MODEL), F32), jax.ShapeDtypeStruct((1, D_MODEL), F32)],
        compiler_params=_params(dimension_semantics=("arbitrary",)),
    )(act, w, x2, g3, target)


def _ffn_out_bwd(dx3b, w, act_by_gate, act_by_up, *, tm, tn):
    T = dx3b.shape[0]

    def body(dx_ref, w_ref, by_gate_ref, by_up_ref, dgate_ref, dup_ref):
        dact = _dot_nt(dx_ref[...], w_ref[...])
        dgate_ref[...] = (dact * by_gate_ref[...].astype(F32)).astype(BF16)
        dup_ref[...] = (dact * by_up_ref[...].astype(F32)).astype(BF16)

    blk = pl.BlockSpec((tm, tn), lambda i, j: (i, j))
    return pl.pallas_call(
        body, name="ffn_out_bwd", grid=(T // tm, D_FF // tn),
        in_specs=[pl.BlockSpec((tm, D_MODEL), lambda i, j: (i, 0)), pl.BlockSpec((tn, D_MODEL), lambda i, j: (j, 0)), blk, blk],
        out_specs=[blk, blk],
        out_shape=[jax.ShapeDtypeStruct((T, D_FF), BF16), jax.ShapeDtypeStruct((T, D_FF), BF16)],
        compiler_params=_params(dimension_semantics=("parallel", "parallel")),
    )(dx3b, w, act_by_gate, act_by_up)


def _ffn_in_bwd(dgate, dup, w, dx3, x2, g2, *, tm, jobs=()):
    T = x2.shape[0]

    def body(dgate_ref, dup_ref, w_ref, dx3_ref, x2_ref, g_ref, dx_ref, dxb_ref, dg_ref):
        @pl.when(pl.program_id(0) == 0)
        def _():
            dg_ref[...] = jnp.zeros_like(dg_ref)

        dh = _dot_nt(dgate_ref[...], w_ref[:, :D_FF]) + _dot_nt(dup_ref[...], w_ref[:, D_FF:])
        dxn, dgp = _rms_bwd(dh, x2_ref[...], g_ref[...])
        dx = dx3_ref[...] + dxn
        dg_ref[...] += jnp.sum(dgp, axis=0, keepdims=True)
        dx_ref[...] = dx
        dxb_ref[...] = dx.astype(BF16)

    row = pl.BlockSpec((tm, D_MODEL), lambda i: (i, 0))
    wide = pl.BlockSpec((tm, D_FF), lambda i: (i, 0))
    vec = pl.BlockSpec((1, D_MODEL), lambda i: (0, 0))
    return _pallas(
        body, (dgate, dup, w, dx3, x2, g2), name="ffn_in_bwd", grid=(T // tm,),
        in_specs=[wide, wide, pl.BlockSpec((D_MODEL, 2 * D_FF), lambda i: (0, 0)), row, row, vec],
        out_specs=[row, row, vec],
        out_shape=[jax.ShapeDtypeStruct((T, D_MODEL), F32), jax.ShapeDtypeStruct((T, D_MODEL), BF16),
                   jax.ShapeDtypeStruct((1, D_MODEL), F32)],
        semantics=("arbitrary",), jobs=jobs)


def _out_proj_bwd(dx2b, w_o, mix_by, *, tm, tn, jobs=()):
    T = dx2b.shape[0]

    def body(dx_ref, w_ref, *refs):
        dmix = _dot_nt(dx_ref[...], w_ref[...])
        for by_ref, d_ref in zip(refs[:4], refs[4:]):
            d_ref[...] = (dmix * by_ref[...].astype(F32)).astype(BF16)

    blk = pl.BlockSpec((tm, tn), lambda i, j: (i, j))
    out = jax.ShapeDtypeStruct((T, D_MODEL), BF16)
    return _pallas(
        body, (dx2b, w_o, *mix_by), name="out_proj_bwd", grid=(T // tm, D_MODEL // tn),
        in_specs=[pl.BlockSpec((tm, D_MODEL), lambda i, j: (i, 0)), pl.BlockSpec((tn, D_MODEL), lambda i, j: (j, 0))] + [blk] * 4,
        out_specs=[blk] * 4, out_shape=[out] * 4, semantics=("parallel", "parallel"), jobs=jobs)


def _branch_bwd(dyp, dyr, w_pool_out, w_rnn_out, *, tm):
    T = dyp.shape[0]

    def body(dyp_ref, dyr_ref, wp_ref, wr_ref, dpm_ref, dz_ref):
        dpm_ref[...] = _dot_nt(dyp_ref[...], wp_ref[...])
        dz_ref[...] = _dot_nt(dyr_ref[...], wr_ref[...])

    row = pl.BlockSpec((tm, D_MODEL), lambda i: (i, 0))
    return pl.pallas_call(
        body, name="branch_bwd", grid=(T // tm,),
        in_specs=[row, row, pl.BlockSpec((D_POOL, D_MODEL), lambda i: (0, 0)), pl.BlockSpec((D_RNN, D_MODEL), lambda i: (0, 0))],
        out_specs=[pl.BlockSpec((tm, D_POOL), lambda i: (i, 0)), pl.BlockSpec((tm, D_RNN), lambda i: (i, 0))],
        out_shape=[jax.ShapeDtypeStruct((T, D_POOL), F32), jax.ShapeDtypeStruct((T, D_RNN), F32)],
        compiler_params=_params(dimension_semantics=("parallel",)),
    )(dyp, dyr, w_pool_out, w_rnn_out)


def _in_proj_bwd(segs, w, dx2, x, g1, *, tm, jobs=()):
    T = x.shape[0]
    widths = [s.shape[1] for s in segs]
    offs = [sum(widths[:k]) for k in range(len(widths))]
    n = len(segs)

    def body(*refs):
        seg_refs, (w_ref, dx2_ref, x_ref, g_ref, dx_ref, dg_ref) = refs[:n], refs[n:]

        @pl.when(pl.program_id(0) == 0)
        def _():
            dg_ref[...] = jnp.zeros_like(dg_ref)

        dh = _dot_nt(seg_refs[0][...], w_ref[:, offs[0]:offs[0] + widths[0]])
        for k in range(1, n):
            dh += _dot_nt(seg_refs[k][...], w_ref[:, offs[k]:offs[k] + widths[k]])
        dxn, dgp = _rms_bwd(dh, x_ref[...], g_ref[...])
        dg_ref[...] += jnp.sum(dgp, axis=0, keepdims=True)
        dx_ref[...] = dx2_ref[...] + dxn

    row = pl.BlockSpec((tm, D_MODEL), lambda i: (i, 0))
    vec = pl.BlockSpec((1, D_MODEL), lambda i: (0, 0))
    return _pallas(
        body, (*segs, w, dx2, x, g1), name="in_proj_bwd", grid=(T // tm,),
        in_specs=[pl.BlockSpec((tm, wd), lambda i: (i, 0)) for wd in widths]
        + [pl.BlockSpec((D_MODEL, D_IN), lambda i: (0, 0)), row, row, vec],
        out_specs=[row, vec],
        out_shape=[jax.ShapeDtypeStruct((T, D_MODEL), F32), jax.ShapeDtypeStruct((1, D_MODEL), F32)],
        semantics=("arbitrary",), jobs=jobs)


def _weight_grad(a, segs, *, tm, tn, name, jobs=None):
    T, M = a.shape
    nblk = [s.shape[1] // tn for s in segs]
    first = [sum(nblk[:k]) for k in range(len(segs))]
    n = len(segs)

    def body(a_ref, *refs):
        seg_refs, o_ref = refs[:n], refs[n]
        j = pl.program_id(1)
        for k in range(n):
            @pl.when((j >= first[k]) & (j < first[k] + nblk[k]))
            def _(k=k):
                o_ref[...] = _dot_tn(a_ref[...], seg_refs[k][...])

    def seg_spec(k):
        return pl.BlockSpec((T, tn), lambda i, j: (0, jnp.clip(j - first[k], 0, nblk[k] - 1)))

    (grad,), results = _pallas(
        body, (a, *segs), name=name, grid=(M // tm, sum(nblk)),
        in_specs=[pl.BlockSpec((T, tm), lambda i, j: (0, i))] + [seg_spec(k) for k in range(n)],
        out_specs=[pl.BlockSpec((tm, tn), lambda i, j: (i, j))],
        out_shape=[jax.ShapeDtypeStruct((M, sum(nblk) * tn), F32)],
        semantics=("parallel", "arbitrary"), jobs=jobs or ())
    return grad if jobs is None else (grad, results)


def _pad_front(dst, src, halo):
    dst[pl.ds(0, halo), :] = jnp.zeros((halo, src.shape[1]), F32)

    def fill(i, carry):
        r0 = pl.multiple_of(i * CHUNK, CHUNK)
        dst[pl.ds(r0 + halo, CHUNK), :] = src[pl.ds(r0, CHUNK), :]
        return carry

    lax.fori_loop(0, src.shape[0] // CHUNK, fill, 0)


def _shift_rows(v, k):
    return pltpu.roll(v, k % v.shape[0], axis=0)


def _window_sums(xs, direction):
    s2 = xs + _shift_rows(xs, direction)
    s4 = s2 + _shift_rows(s2, 2 * direction)
    s8 = s4 + _shift_rows(s4, 4 * direction)
    s16 = s8 + _shift_rows(s8, 8 * direction)
    return s2, s4, s8, s16


def _select_window(g, sums):
    s2, s4, s8, s16 = sums
    return jnp.where(g == 0, s2, jnp.where(g == 1, s4, jnp.where(g == 2, s8, s16)))


def _pool_count(g, start, rows):
    t = start + lax.broadcasted_iota(jnp.int32, (rows, 1), 0)
    return jnp.minimum(t + 1, jnp.left_shift(2, g)).astype(F32)


def _pool_fwd(proj, w_grp, scale):
    T = proj.shape[0]
    nchunk = T // CHUNK

    def body(u_ref, w_ref, s_ref, o_ref, upad):
        g = pl.program_id(0)
        _pad_front(upad, u_ref, POOL_HALO)
        w = w_ref[...].astype(BF16)
        scale_row = s_ref[...]

        def chunk(i, carry):
            r0 = pl.multiple_of(i * CHUNK, CHUNK)
            xs = upad[pl.ds(r0, CHUNK + POOL_HALO), :]
            win = _select_window(g, _window_sums(xs, 1))[POOL_HALO:]
            pooled = win / _pool_count(g, r0, CHUNK) - xs[POOL_HALO:]
            o_ref[pl.ds(r0, CHUNK), :] = (_dot(pooled.astype(BF16), w) * scale_row).astype(BF16)
            return carry

        lax.fori_loop(0, nchunk, chunk, 0)

    return pl.pallas_call(
        body, name="pool_fwd", grid=(N_POOL_GROUPS,),
        in_specs=[pl.BlockSpec((T, HEAD), lambda g: (0, g)), pl.BlockSpec((None, HEAD, HEAD), lambda g: (g, 0, 0)),
                  pl.BlockSpec((1, HEAD), lambda g: (0, g))],
        out_specs=pl.BlockSpec((T, HEAD), lambda g: (0, g)),
        out_shape=jax.ShapeDtypeStruct((T, D_POOL), BF16),
        scratch_shapes=[pltpu.VMEM((T + POOL_HALO, HEAD), F32)],
        compiler_params=_params(dimension_semantics=("parallel",)),
    )(proj, w_grp, scale)


def _pool_bwd(proj, dpm, w_grp, scale, jobs=()):
    T = proj.shape[0]
    nchunk = T // CHUNK

    def body(u_ref, dpm_ref, w_ref, s_ref, du_ref, dw_ref, ds_ref, upad, zpad, dpool):
        g = pl.program_id(0)
        _pad_front(upad, u_ref, POOL_HALO)
        zpad[pl.ds(T, POOL_HALO), :] = jnp.zeros((POOL_HALO, HEAD), F32)
        dw_ref[...] = jnp.zeros_like(dw_ref)
        ds_ref[...] = jnp.zeros_like(ds_ref)
        w = w_ref[...].astype(BF16)
        scale_row = s_ref[...]

        def chunk(i, carry):
            r0 = pl.multiple_of(i * CHUNK, CHUNK)
            xs = upad[pl.ds(r0, CHUNK + POOL_HALO), :]
            cnt = _pool_count(g, r0, CHUNK)
            pooled = (_select_window(g, _window_sums(xs, 1))[POOL_HALO:] / cnt - xs[POOL_HALO:]).astype(BF16)
            mixed = _dot(pooled, w)
            d = dpm_ref[pl.ds(r0, CHUNK), :]
            ds_ref[...] += jnp.sum(d * mixed, axis=0, keepdims=True)
            dmixed = (d * scale_row).astype(BF16)
            dw_ref[...] += _dot_tn(pooled, dmixed)
            dp = _dot_nt(dmixed, w)
            dpool[pl.ds(r0, CHUNK), :] = dp
            zpad[pl.ds(r0, CHUNK), :] = dp / cnt
            return carry

        lax.fori_loop(0, nchunk, chunk, 0)

        def chunk2(i, carry):
            r0 = pl.multiple_of(i * CHUNK, CHUNK)
            zs = zpad[pl.ds(r0, CHUNK + POOL_HALO), :]
            win = _select_window(g, _window_sums(zs, -1))[:CHUNK]
            du_ref[pl.ds(r0, CHUNK), :] = (win - dpool[pl.ds(r0, CHUNK), :]).astype(BF16)
            return carry

        lax.fori_loop(0, nchunk, chunk2, 0)

    col = pl.BlockSpec((T, HEAD), lambda g: (0, g))
    return _pallas(
        body, (proj, dpm, w_grp, scale), name="pool_bwd", grid=(N_POOL_GROUPS,),
        in_specs=[col, col, pl.BlockSpec((None, HEAD, HEAD), lambda g: (g, 0, 0)), pl.BlockSpec((1, HEAD), lambda g: (0, g))],
        out_specs=[col, pl.BlockSpec((None, HEAD, HEAD), lambda g: (g, 0, 0)), pl.BlockSpec((1, HEAD), lambda g: (0, g))],
        out_shape=[jax.ShapeDtypeStruct((T, D_POOL), BF16), jax.ShapeDtypeStruct((N_POOL_GROUPS, HEAD, HEAD), F32),
                   jax.ShapeDtypeStruct((1, D_POOL), F32)],
        scratch_shapes=[pltpu.VMEM((T + POOL_HALO, HEAD), F32), pltpu.VMEM((T + POOL_HALO, HEAD), F32), pltpu.VMEM((T, HEAD), F32)],
        semantics=("parallel",), jobs=jobs)


def _conv_taps(xs, cw):
    v = cw[CONV_WIDTH - 1] * xs[SUBLANES:]
    for k in range(CONV_WIDTH - 1):
        v += cw[k] * _shift_rows(xs, CONV_WIDTH - 1 - k)[SUBLANES:]
    return v


def _tap_rows(cw_ref):
    return [cw_ref[k:k + 1, :] for k in range(CONV_WIDTH)]


def _softplus_neg(lam):
    return jnp.maximum(-lam, 0.0) + _log1p(jnp.exp(-jnp.abs(lam)))


def _lru_gates(v, wa, ba, wx, bx, sp):
    vb = v.astype(BF16)
    ra = _sigmoid(_dot(vb, wa) + ba)
    ix = _sigmoid(_dot(vb, wx) + bx)
    log_a = -LRU_C * ra * sp
    a = jnp.exp(log_a)
    sq = jnp.sqrt(-jnp.tanh(log_a) * (a * a + 1.0))
    return ra, ix, a, sq


def _row_bcast(v, r):
    return jnp.broadcast_to(v[r:r + 1, :], v.shape)


TILE_BLOCK = 128


def _scan_in_tiles(coef, coef_shift, A_out, B, T, direction):
    order = list(range(SUBLANES)) if direction == 1 else list(range(SUBLANES - 1, -1, -1))
    tiles = min(TILE_BLOCK, T // SUBLANES)
    for base in range(0, T, tiles * SUBLANES):
        def rows(r, base=base):
            return pl.ds(base + r, tiles, stride=SUBLANES)

        A, Bv = coef[rows(order[0] + coef_shift), :], B[rows(order[0]), :]
        A_out[rows(order[0]), :] = A
        for r in order[1:]:
            a = coef[rows(r + coef_shift), :]
            Bv = a * Bv + B[rows(r), :]
            A = a * A
            A_out[rows(r), :] = A
            B[rows(r), :] = Bv


TILES_PER_STEP = 8


def _carry_tiles(A_s, B_s, out, ntile, direction):
    out_row = SUBLANES - 1 if direction == 1 else 0

    def step(k, carry):
        for j in range(TILES_PER_STEP):
            t = k * TILES_PER_STEP + j
            r0 = pl.multiple_of((t if direction == 1 else ntile - 1 - t) * SUBLANES, SUBLANES)
            A, B = A_s[pl.ds(r0, SUBLANES), :], B_s[pl.ds(r0, SUBLANES), :]
            out[pl.ds(r0, SUBLANES), :] = A * carry + B
            carry = _row_bcast(A, out_row) * carry + _row_bcast(B, out_row)
        return carry

    lax.fori_loop(0, ntile // TILES_PER_STEP, step, jnp.zeros((SUBLANES, HEAD), F32))


def _rnn_fwd(proj, conv_w, conv_b, w_a, b_a, w_x, b_x, lam, jobs=()):
    T = proj.shape[0]
    nchunk = T // CHUNK
    ntile = T // SUBLANES

    def body(u_ref, ug_ref, cw_ref, cb_ref, wa_ref, ba_ref, wx_ref, bx_ref, lam_ref,
             h_ref, z_ref, v_ref, ra_ref, ix_ref, a_ref, sq_ref, upad, a_s, b_s):
        _pad_front(upad, u_ref, SUBLANES)
        cw, cb = _tap_rows(cw_ref), cb_ref[...]
        wa, wx = wa_ref[...].astype(BF16), wx_ref[...].astype(BF16)
        ba, bx = ba_ref[...], bx_ref[...]
        sp = _softplus_neg(lam_ref[...])

        def chunk(i, carry):
            rows = pl.ds(pl.multiple_of(i * CHUNK, CHUNK), CHUNK)
            v = _conv_taps(upad[pl.ds(pl.multiple_of(i * CHUNK, CHUNK), CHUNK + SUBLANES), :], cw) + cb
            ra, ix, a, sq = _lru_gates(v, wa, ba, wx, bx, sp)
            v_ref[rows, :], ra_ref[rows, :], ix_ref[rows, :], a_ref[rows, :], sq_ref[rows, :] = v, ra, ix, a, sq
            a_s[rows, :], b_s[rows, :] = a, sq * ix * v
            return carry

        lax.fori_loop(0, nchunk, chunk, 0)
        _scan_in_tiles(a_s, 0, a_s, b_s, T, 1)
        _carry_tiles(a_s, b_s, h_ref, ntile, 1)

        def chunk3(i, carry):
            r0 = pl.multiple_of(i * CHUNK, CHUNK)
            gl, _ = _gelu_parts(ug_ref[pl.ds(r0, CHUNK), :])
            z_ref[pl.ds(r0, CHUNK), :] = (h_ref[pl.ds(r0, CHUNK), :] * gl).astype(BF16)
            return carry

        lax.fori_loop(0, nchunk, chunk3, 0)

    col = pl.BlockSpec((T, HEAD), lambda h: (0, h))
    vec = pl.BlockSpec((1, HEAD), lambda h: (0, h))
    mat = pl.BlockSpec((None, HEAD, HEAD), lambda h: (h, 0, 0))
    return _pallas(
        body, (proj, proj, conv_w, conv_b, w_a, b_a, w_x, b_x, lam), name="rnn_fwd", grid=(N_RNN_HEADS,),
        in_specs=[pl.BlockSpec((T, HEAD), lambda h: (0, COL_RNN + h)), pl.BlockSpec((T, HEAD), lambda h: (0, COL_GATE + h)),
                  pl.BlockSpec((CONV_WIDTH, HEAD), lambda h: (0, h)), vec, mat, vec, mat, vec, vec],
        out_specs=[col] * 7,
        out_shape=[jax.ShapeDtypeStruct((T, D_RNN), F32), jax.ShapeDtypeStruct((T, D_RNN), BF16)]
        + [jax.ShapeDtypeStruct((T, D_RNN), F32)] * 5,
        scratch_shapes=[pltpu.VMEM((T + SUBLANES, HEAD), F32), pltpu.VMEM((T, HEAD), F32), pltpu.VMEM((T, HEAD), F32)],
        semantics=("parallel",), jobs=jobs)


def _rnn_bwd(proj, hr, dz, gates, conv_w, w_a, w_x, lam, jobs=()):
    T = proj.shape[0]
    nchunk = T // CHUNK
    ntile = T // SUBLANES

    def body(u_ref, ug_ref, h_ref, dz_ref, v_ref, ra_ref, ix_ref, a_ref, sq_ref, cw_ref, wa_ref, wx_ref, lam_ref,
             du_ref, dug_ref, dwa_ref, dwx_ref, dba_ref, dbx_ref, dlam_ref, dcb_ref, dcw_ref,
             upad, hpad, apad, g_s, dvpad, ga_s):
        zero_tile = jnp.zeros((SUBLANES, HEAD), F32)
        _pad_front(upad, u_ref, SUBLANES)
        _pad_front(hpad, h_ref, SUBLANES)
        apad[pl.ds(T, SUBLANES), :] = zero_tile
        dvpad[pl.ds(T, SUBLANES), :] = zero_tile
        for ref in (dwa_ref, dwx_ref, dba_ref, dbx_ref, dlam_ref, dcb_ref, dcw_ref):
            ref[...] = jnp.zeros_like(ref)
        cw = _tap_rows(cw_ref)
        wa, wx = wa_ref[...].astype(BF16), wx_ref[...].astype(BF16)
        lam_row = lam_ref[...]
        sp = _softplus_neg(lam_row)

        def chunk(i, carry):
            rows = pl.ds(pl.multiple_of(i * CHUNK, CHUNK), CHUNK)
            apad[rows, :] = a_ref[rows, :]
            gl, dgl = _gelu_parts(ug_ref[rows, :])
            d = dz_ref[rows, :]
            g_s[rows, :] = d * gl
            dug_ref[rows, :] = (d * h_ref[rows, :] * dgl).astype(BF16)
            return carry

        lax.fori_loop(0, nchunk, chunk, 0)

        _scan_in_tiles(apad, 1, ga_s, g_s, T, -1)
        _carry_tiles(ga_s, g_s, g_s, ntile, -1)

        def chunk3(i, carry):
            r0 = pl.multiple_of(i * CHUNK, CHUNK)
            rows = pl.ds(r0, CHUNK)
            g = g_s[rows, :]
            h_prev = _shift_rows(hpad[pl.ds(r0, CHUNK + SUBLANES), :], 1)[SUBLANES:]
            v, ra, ix, sq, a = v_ref[rows, :], ra_ref[rows, :], ix_ref[rows, :], sq_ref[rows, :], a_ref[rows, :]
            d_sq = g * ix * v
            d_ix = g * sq * v
            d_la = a * g * h_prev - d_sq * a * a / sq
            dlam_ref[...] += jnp.sum(d_la * ra, axis=0, keepdims=True)
            d_pa = d_la * (-LRU_C) * sp * ra * (1.0 - ra)
            d_px = d_ix * ix * (1.0 - ix)
            vb, d_pab, d_pxb = v.astype(BF16), d_pa.astype(BF16), d_px.astype(BF16)
            dwa_ref[...] += _dot_tn(vb, d_pab)
            dwx_ref[...] += _dot_tn(vb, d_pxb)
            dba_ref[...] += jnp.sum(d_pa, axis=0, keepdims=True)
            dbx_ref[...] += jnp.sum(d_px, axis=0, keepdims=True)
            dv = g * sq * ix + _dot_nt(d_pab, wa) + _dot_nt(d_pxb, wx)
            dvpad[rows, :] = dv
            dcb_ref[...] += jnp.sum(dv, axis=0, keepdims=True)
            xs = upad[pl.ds(r0, CHUNK + SUBLANES), :]
            for k in range(CONV_WIDTH):
                u_k = _shift_rows(xs, CONV_WIDTH - 1 - k)[SUBLANES:] if k < CONV_WIDTH - 1 else xs[SUBLANES:]
                dcw_ref[k:k + 1, :] += jnp.sum(dv * u_k, axis=0, keepdims=True)
            return carry

        lax.fori_loop(0, nchunk, chunk3, 0)
        dlam_ref[...] = dlam_ref[...] * (LRU_C * _sigmoid(-lam_row))

        def chunk4(i, carry):
            r0 = pl.multiple_of(i * CHUNK, CHUNK)
            dvs = dvpad[pl.ds(r0, CHUNK + SUBLANES), :]
            du = cw[CONV_WIDTH - 1] * dvs[:CHUNK]
            for k in range(CONV_WIDTH - 1):
                du += cw[k] * _shift_rows(dvs, -(CONV_WIDTH - 1 - k))[:CHUNK]
            du_ref[pl.ds(r0, CHUNK), :] = du.astype(BF16)
            return carry

        lax.fori_loop(0, nchunk, chunk4, 0)

    col = pl.BlockSpec((T, HEAD), lambda h: (0, h))
    vec = pl.BlockSpec((1, HEAD), lambda h: (0, h))
    mat = pl.BlockSpec((None, HEAD, HEAD), lambda h: (h, 0, 0))
    taps = pl.BlockSpec((CONV_WIDTH, HEAD), lambda h: (0, h))
    vec_out = jax.ShapeDtypeStruct((1, D_RNN), F32)
    mat_out = jax.ShapeDtypeStruct((N_RNN_HEADS, HEAD, HEAD), F32)
    seq = pltpu.VMEM((T, HEAD), F32)
    seq_pad = pltpu.VMEM((T + SUBLANES, HEAD), F32)
    return _pallas(
        body, (proj, proj, hr, dz, *gates, conv_w, w_a, w_x, lam), name="rnn_bwd", grid=(N_RNN_HEADS,),
        in_specs=[pl.BlockSpec((T, HEAD), lambda h: (0, COL_RNN + h)), pl.BlockSpec((T, HEAD), lambda h: (0, COL_GATE + h))]
        + [col] * 7 + [taps, mat, mat, vec],
        out_specs=[col, col, mat, mat, vec, vec, vec, vec, taps],
        out_shape=[jax.ShapeDtypeStruct((T, D_RNN), BF16), jax.ShapeDtypeStruct((T, D_RNN), BF16), mat_out, mat_out,
                   vec_out, vec_out, vec_out, vec_out, jax.ShapeDtypeStruct((CONV_WIDTH, D_RNN), F32)],
        scratch_shapes=[seq_pad, seq_pad, seq_pad, seq, seq_pad, seq],
        semantics=("parallel",), jobs=jobs)


GROUP_FFN_OUT = ["w_ffn_out"]
GROUP_FFN_IN = ["w_ffn_in"]
GROUP_MIX = ["w_o", "w_pool_out", "w_rnn_out"]
GROUP_IN = ["w_in"]


def _step(x, target, s, full, conv_w, place):
    T = x.shape[0]
    tall, mid, low = min(T, 2048), min(T, 1024), min(T, 512)
    full = dict(full)

    def gathered(names, results):
        full.update(zip(names, results))

    early = ["w_pool_out", "w_rnn_out", "w_o", "w_ffn_out"]
    (proj, h1), (res,) = _norm_matmul(x, s["norm_mix"], full["w_in"], tm=tall, tn=512, name="in_proj", jobs=[_gather_job(full, early)])
    gathered(early, res)
    pm = _pool_fwd(proj, s["w_pool_grp"], s["pool_scale"])
    (hr, z, *gates), (res,) = _rnn_fwd(proj, conv_w, s["conv_b"], s["w_rg_a"], s["b_rg_a"], s["w_rg_x"], s["b_rg_x"],
                                       s["lru_lambda"], jobs=[_gather_job(full, ["w_ffn_in"])])
    gathered(["w_ffn_in"], res)
    *mix_by, mix = _branch_mix(pm, z, full["w_pool_out"], full["w_rnn_out"], proj, tm=tall, tn=256)
    x2 = _out_proj_residual(mix, full["w_o"], x, tm=mid)
    (act_by_up, act_by_gate, act, h2), _ = _ffn_in(x2, s["norm_ffn"], full["w_ffn_in"], tm=tall, tn=256)
    dx3, dx3b, sq_cols, g_norm_final = _ffn_out_loss(act, full["w_ffn_out"], x2, s["norm_final"], target, tm=low)

    g = {"norm_final": g_norm_final}

    def chip_sums(names, from_sibling):
        sums = {name: _chip_sum(name, g[name], got, place) for name, got in zip(names, from_sibling)}
        return {name: v[0] for name, v in sums.items()}, {name: v[1] for name, v in sums.items()}

    def final_sums(names, sums, from_chips):
        return {name: _final_sum(name, sums[name], got, place) for name, got in zip(names, from_chips)}

    dgate, dup = _ffn_out_bwd(dx3b, full["w_ffn_out"], act_by_gate, act_by_up, tm=tall, tn=256)
    g["w_ffn_out"] = _weight_grad(act, [dx3b], tm=D_FF // 2, tn=D_MODEL, name="w_ffn_out_grad")
    (dx2, dx2b, g["norm_ffn"]), (res,) = _ffn_in_bwd(dgate, dup, full["w_ffn_in"], dx3, x2, s["norm_ffn"], tm=low,
                                                     jobs=[_sibling_job(g, GROUP_FFN_OUT)])
    sums_ffn, sums_ffn_bf16 = chip_sums(GROUP_FFN_OUT, res)
    g["w_ffn_in"], (res,) = _weight_grad(h2, [dgate, dup], tm=D_MODEL, tn=256, name="w_ffn_in_grad",
                                         jobs=[_chips_job(sums_ffn_bf16, GROUP_FFN_OUT)])
    shards_ffn = final_sums(GROUP_FFN_OUT, sums_ffn, res)
    (dgp, dgr, dyp, dyr), (res,) = _out_proj_bwd(dx2b, full["w_o"], mix_by, tm=tall, tn=256,
                                                 jobs=[_sibling_job(g, GROUP_FFN_IN)])
    sums_ffn, sums_ffn_bf16 = chip_sums(GROUP_FFN_IN, res)
    g["w_o"] = _weight_grad(mix, [dx2b], tm=D_MODEL, tn=256, name="w_o_grad")
    dpm, dz = _branch_bwd(dyp, dyr, full["w_pool_out"], full["w_rnn_out"], tm=mid)
    g["w_pool_out"] = _weight_grad(pm, [dyp], tm=D_POOL, tn=256, name="w_pool_out_grad")
    g["w_rnn_out"] = _weight_grad(z, [dyr], tm=D_RNN, tn=256, name="w_rnn_out_grad")
    (dupool, g["w_pool_grp"], g["pool_scale"]), (res,) = _pool_bwd(proj, dpm, s["w_pool_grp"], s["pool_scale"],
                                                                   jobs=[_sibling_job(g, GROUP_MIX)])
    sums_mix, sums_mix_bf16 = chip_sums(GROUP_MIX, res)
    ((durnn, dugate, g["w_rg_a"], g["w_rg_x"], g["b_rg_a"], g["b_rg_x"], g["lru_lambda"], g["conv_b"], g["conv_w"]),
     (res,)) = _rnn_bwd(proj, hr, dz, gates, conv_w, s["w_rg_a"], s["w_rg_x"], s["lru_lambda"],
                        jobs=[_chips_job(sums_ffn_bf16, GROUP_FFN_IN)])
    shards_ffn.update(final_sums(GROUP_FFN_IN, sums_ffn, res))
    segs = [dupool, durnn, dugate, dgp, dgr]
    ffn = GROUP_FFN_OUT + GROUP_FFN_IN
    g["w_in"], (res, joined) = _weight_grad(h1, segs, tm=D_MODEL, tn=256, name="w_in_grad",
                                           jobs=[_chips_job(sums_mix_bf16, GROUP_MIX), _join_job(shards_ffn, ffn)])
    grads = dict(zip(ffn, joined))
    shards = final_sums(GROUP_MIX, sums_mix, res)
    (res,) = _run_jobs([_sibling_job(g, GROUP_IN)], "w_in_exchange_sibling")
    sums_in, sums_in_bf16 = chip_sums(GROUP_IN, res)
    (grad_x, g["norm_mix"]), (res,) = _in_proj_bwd(segs, full["w_in"], dx2, x, s["norm_mix"], tm=low,
                                                  jobs=[_chips_job(sums_in_bf16, GROUP_IN)])
    shards.update(final_sums(GROUP_IN, sums_in, res))

    vec_rows = [g[name] if name != "pool_scale" else jnp.pad(g[name], ((0, 0), (0, D_MODEL - D_POOL))) for name in VEC_ITEMS]
    vec_rows += [g["conv_w"], sq_cols, jnp.zeros((VEC_ROWS - len(VEC_ITEMS) - CONV_WIDTH - 1, D_MODEL), F32)]
    vec = jnp.concatenate(vec_rows, axis=0).reshape(VEC_ROWS, N_DEV, HEAD).transpose(1, 0, 2)
    mat = jnp.concatenate([g[name].reshape(-1, HEAD) for name in MAT_ITEMS], axis=0).reshape(N_DEV, -1, HEAD)
    (vec, mat), (joined,) = _all_reduce_small([vec, mat], jobs=[_join_job(shards, GROUP_MIX + GROUP_IN)])
    grads.update(zip(GROUP_MIX + GROUP_IN, joined))

    vec = vec.transpose(1, 0, 2).reshape(VEC_ROWS, D_MODEL)
    mat = mat.reshape(-1, HEAD)
    for k, name in enumerate(VEC_ITEMS):
        grads[name] = vec[k:k + 1, :s[name].shape[1]]
    grads["conv_w"] = vec[len(VEC_ITEMS):len(VEC_ITEMS) + CONV_WIDTH]
    row = 0
    for name in MAT_ITEMS:
        rows = s[name].shape[0] * HEAD
        grads[name] = mat[row:row + rows]
        row += rows
    return vec[len(VEC_ITEMS) + CONV_WIDTH], grad_x, grads


LARGE = {"w_in": "col", "w_pool_out": "col", "w_rnn_out": "row", "w_o": "row", "w_ffn_in": "col", "w_ffn_out": "row"}
LARGE_SHAPE = {"w_in": (D_MODEL, D_IN), "w_pool_out": (D_POOL, D_MODEL), "w_rnn_out": (D_RNN, D_MODEL),
               "w_o": (D_MODEL, D_MODEL), "w_ffn_in": (D_MODEL, 2 * D_FF), "w_ffn_out": (D_FF, D_MODEL)}


def _place():
    x, y, c = lax.axis_index("x"), lax.axis_index("y"), lax.axis_index("c")
    return 2 * x + y, c


def _chip_device(chip, c):
    return (chip // 2, chip % 2, c)


def _chip_window(ref, kind, shape, chip, half=None):
    K, N = shape
    if kind == "col":
        rows = slice(None) if half is None else pl.ds(half * (K // 2), K // 2)
        return ref.at[rows, pl.ds(chip * (N // N_CHIPS), N // N_CHIPS)]
    ks = K // N_CHIPS
    if half is None:
        return ref.at[pl.ds(chip * ks, ks), :]
    return ref.at[pl.ds(chip * ks + half * (ks // 2), ks // 2), :]


def _row_half(ref, half):
    rows = ref.shape[0] // 2
    return ref.at[pl.ds(half * rows, rows), :]


def _remote(win_src, win_dst, send_sems, recv_sems, idx, to):
    return pltpu.make_async_remote_copy(src_ref=win_src, dst_ref=win_dst, send_sem=send_sems.at[idx], recv_sem=recv_sems.at[idx],
                                        device_id=to, device_id_type=MESH)


def _gather_job(full, names, conv_w_full=None):
    n = len(names)
    cw_cols = D_RNN // N_CHIPS

    def windows(refs, chip, half):
        return [_chip_window(refs[k], LARGE[name], LARGE_SHAPE[name], chip, half) for k, name in enumerate(names)]

    def ici_copies(refs, send_sems, recv_sems, src_chip, dst_chip, c, r):
        wins = windows(refs, src_chip, c)
        if conv_w_full is not None:
            wins.append(refs[n].at[:, pl.ds(src_chip * cw_cols, cw_cols)])
        return [_remote(win, win, send_sems, recv_sems, (k, r), _chip_device(dst_chip, c)) for k, win in enumerate(wins)]

    def forwards(refs, send_sems, recv_sems, src_chip, half, to_core, chip, r):
        return [_remote(win, win, send_sems, recv_sems, (k, 3 + r), _chip_device(chip, to_core))
                for k, win in enumerate(windows(refs, src_chip, half))]

    def start(ins, outs, send_sems, recv_sems):
        chip, c = _place()
        for r in range(3):
            for cp in ici_copies(outs, send_sems, recv_sems, chip, chip ^ (r + 1), c, r):
                cp.start()

    def finish(ins, outs, send_sems, recv_sems):
        chip, c = _place()
        for r in range(3):
            for cp in ici_copies(outs, send_sems, recv_sems, chip ^ (r + 1), chip, c, r):
                cp.wait_recv()
            for cp in forwards(outs, send_sems, recv_sems, chip ^ (r + 1), c, 1 - c, chip, r):
                cp.start()
        for r in range(3):
            for cp in forwards(outs, send_sems, recv_sems, chip ^ (r + 1), 1 - c, c, chip, r):
                cp.wait_recv()
            for cp in ici_copies(outs, send_sems, recv_sems, chip, chip ^ (r + 1), c, r):
                cp.wait_send()
            for cp in forwards(outs, send_sems, recv_sems, chip ^ (r + 1), c, 1 - c, chip, r):
                cp.wait_send()

    arrays = [full[name] for name in names] + ([conv_w_full] if conv_w_full is not None else [])
    return _Job(arrays, [jax.ShapeDtypeStruct(a.shape, a.dtype) for a in arrays], {k: k for k in range(len(arrays))},
                (len(arrays), 6), start, finish)


def _core_halves(ref, kind, shape, c):
    return [_chip_window(ref, kind, shape, chip, c) for chip in range(N_CHIPS)]


def _sibling_job(grads, names):
    def start(ins, outs, send_sems, recv_sems):
        chip, c = _place()
        for k, name in enumerate(names):
            kind, shape = LARGE[name], LARGE_SHAPE[name]
            if kind == "col":
                pairs = [(_row_half(ins[k], 1 - c), outs[k])]
            else:
                rows = shape[0] // N_DEV
                pairs = [(win, outs[k].at[pl.ds(j * rows, rows), :]) for j, win in enumerate(_core_halves(ins[k], kind, shape, 1 - c))]
            for src, dst in pairs:
                _remote(src, dst, send_sems, recv_sems, k, _chip_device(chip, 1 - c)).start()

    def finish(ins, outs, send_sems, recv_sems):
        chip, c = _place()
        for k in range(len(names)):
            _remote(outs[k], outs[k], send_sems, recv_sems, k, _chip_device(chip, 1 - c)).wait()

    return _Job([grads[name] for name in names],
                [jax.ShapeDtypeStruct((LARGE_SHAPE[name][0] // 2, LARGE_SHAPE[name][1]), F32) for name in names], {},
                (len(names),), start, finish)


def _chip_sum(name, g, got, place):
    kind, (K, N) = LARGE[name], LARGE_SHAPE[name]
    rows = K // N_DEV
    piece_cols = N // N_CHIPS

    def body(place_ref, g_ref, got_ref, o_ref, ob_ref):
        total = g_ref[...] + got_ref[...]
        ob_ref[...] = total.astype(BF16)
        if kind == "col":
            for chip in range(N_CHIPS):
                @pl.when(place_ref[0] == chip)
                def _(chip=chip):
                    o_ref[...] = total[:, chip * piece_cols:(chip + 1) * piece_cols]
        else:
            @pl.when(pl.program_id(0) == place_ref[0])
            def _():
                o_ref[...] = total

    if kind == "col":
        mine = pl.BlockSpec((rows, N), lambda j, place_ref: (j + N_CHIPS * place_ref[1], 0))
        own = pl.BlockSpec((rows, piece_cols), lambda j, place_ref: (j, 0))
    else:
        mine = pl.BlockSpec((rows, N), lambda j, place_ref: (2 * j + place_ref[1], 0))
        own = pl.BlockSpec((rows, N), lambda j, place_ref: (0, 0))
    blk = pl.BlockSpec((rows, N), lambda j, place_ref: (j, 0))
    return pl.pallas_call(
        body, name=name + "_chip_sum",
        grid_spec=pltpu.PrefetchScalarGridSpec(num_scalar_prefetch=1, grid=(N_CHIPS,), in_specs=[mine, blk], out_specs=[own, blk]),
        out_shape=[jax.ShapeDtypeStruct(_piece_shape(name), F32), jax.ShapeDtypeStruct((K // 2, N), BF16)],
        compiler_params=_params(dimension_semantics=("arbitrary",)),
    )(place, g, got)


def _piece(ref, kind, shape, chip):
    K, N = shape
    if kind == "col":
        return ref.at[:, pl.ds(chip * (N // N_CHIPS), N // N_CHIPS)]
    return ref.at[pl.ds(chip * (K // N_DEV), K // N_DEV), :]


def _piece_shape(name):
    kind, (K, N) = LARGE[name], LARGE_SHAPE[name]
    return (K // 2, N // N_CHIPS) if kind == "col" else (K // N_DEV, N)


def _chips_job(sums, names):
    def copies(ins, outs, send_sems, recv_sems):
        chip, c = _place()
        return [_remote(_piece(ins[k], LARGE[name], LARGE_SHAPE[name], chip ^ (r + 1)), outs[k].at[r], send_sems, recv_sems, (k, r),
                        _chip_device(chip ^ (r + 1), c)) for k, name in enumerate(names) for r in range(3)]

    def start(*refs):
        for cp in copies(*refs):
            cp.start()

    def finish(*refs):
        for cp in copies(*refs):
            cp.wait()

    return _Job([sums[name] for name in names], [jax.ShapeDtypeStruct((3,) + _piece_shape(name), BF16) for name in names], {},
                (len(names), 3), start, finish)


def _final_sum(name, chip_sum, got, place):
    rows, cols = _piece_shape(name)

    def body(place_ref, s_ref, got_ref, o_ref):
        o_ref[...] = ((s_ref[...] + got_ref[0].astype(F32)) + got_ref[1].astype(F32)) + got_ref[2].astype(F32)

    mine = pl.BlockSpec((rows, cols), lambda i, place_ref: (0, 0))
    return pl.pallas_call(
        body, name=name + "_final_sum",
        grid_spec=pltpu.PrefetchScalarGridSpec(
            num_scalar_prefetch=1, grid=(1,), in_specs=[mine, pl.BlockSpec((3, rows, cols), lambda i, place_ref: (0, 0, 0))],
            out_specs=pl.BlockSpec((rows, cols), lambda i, place_ref: (place_ref[1], 0))),
        out_shape=jax.ShapeDtypeStruct((2 * rows, cols), F32),
        compiler_params=_params(dimension_semantics=("arbitrary",)),
    )(place, chip_sum, got)


def _join_job(shards, names):
    def half_copy(outs, send_sems, recv_sems, k, mine):
        chip, c = _place()
        win = _row_half(outs[k], c if mine else 1 - c)
        return _remote(win, win, send_sems, recv_sems, k, _chip_device(chip, 1 - c))

    def start(ins, outs, send_sems, recv_sems):
        for k in range(len(names)):
            half_copy(outs, send_sems, recv_sems, k, True).start()

    def finish(ins, outs, send_sems, recv_sems):
        for k in range(len(names)):
            half_copy(outs, send_sems, recv_sems, k, True).wait_send()
            half_copy(outs, send_sems, recv_sems, k, False).wait_recv()

    arrays = [shards[name] for name in names]
    return _Job(arrays, [jax.ShapeDtypeStruct(a.shape, F32) for a in arrays], {k: k for k in range(len(arrays))},
                (len(arrays),), start, finish)


VEC_ROWS = 16


def _all_reduce_small(slabs, jobs=()):
    n = len(slabs)

    def body(*refs):
        in_refs, out_refs, got_refs = refs[:n], refs[n:2 * n], refs[2 * n:3 * n]
        send_sems, recv_sems = refs[3 * n:]
        x, y, c = lax.axis_index("x"), lax.axis_index("y"), lax.axis_index("c")
        me = 4 * x + 2 * y + c

        def remote(src, dst, k, phase, r):
            other = me ^ r
            return pltpu.make_async_remote_copy(src_ref=src, dst_ref=dst, send_sem=send_sems.at[k, phase, r],
                                                recv_sem=recv_sems.at[k, phase, r],
                                                device_id=(other // 4, (other // 2) % 2, other % 2), device_id_type=MESH)

        scatter = [remote(in_refs[k].at[me ^ r], got_refs[k].at[r], k, 0, r) for r in range(1, N_DEV) for k in range(n)]
        for cp in scatter:
            cp.start()
        for cp in scatter:
            cp.wait()
        for k in range(n):
            total = in_refs[k][me]
            for r in range(1, N_DEV):
                total = total + got_refs[k][r]
            out_refs[k][me] = total
        gather = [remote(out_refs[k].at[me], out_refs[k].at[me], k, 1, r) for r in range(1, N_DEV) for k in range(n)]
        for cp in gather:
            cp.start()
        for r in range(1, N_DEV):
            for k in range(n):
                remote(out_refs[k].at[me ^ r], out_refs[k].at[me ^ r], k, 1, r).wait_recv()
        for cp in gather:
            cp.wait_send()

    return _pallas(
        body, slabs, name="all_reduce_small", grid=(), in_specs=[VMEM] * n, out_specs=[VMEM] * n,
        out_shape=[jax.ShapeDtypeStruct(s.shape, F32) for s in slabs],
        scratch_shapes=[pltpu.VMEM(s.shape, F32) for s in slabs]
        + [pltpu.SemaphoreType.DMA((n, 2, N_DEV)), pltpu.SemaphoreType.DMA((n, 2, N_DEV))], jobs=jobs)


def _cast_into_whole(w, name, place):
    rows, cols = w.shape
    tr = rows // 2

    def body(place_ref, w_ref, o_ref):
        o_ref[...] = w_ref[...].astype(BF16)

    if LARGE[name] == "col":
        window = pl.BlockSpec((tr, cols), lambda i, place_ref: (i, place_ref[0]))
    else:
        window = pl.BlockSpec((tr, cols), lambda i, place_ref: (2 * place_ref[0] + i, 0))
    return pl.pallas_call(
        body, name=name + "_cast",
        grid_spec=pltpu.PrefetchScalarGridSpec(num_scalar_prefetch=1, grid=(2,),
                                               in_specs=[pl.BlockSpec((tr, cols), lambda i, place_ref: (i, 0))], out_specs=window),
        out_shape=jax.ShapeDtypeStruct(LARGE_SHAPE[name], BF16),
        compiler_params=_params(dimension_semantics=("parallel",)))(place, w)


def _cast_many_into_whole(shards, place, jobs):
    names = list(shards)
    n = len(names)

    def body(place_ref, *refs):
        for w_ref, o_ref in zip(refs[:n], refs[n:]):
            o_ref[...] = w_ref[...].astype(BF16)

    def window(name):
        rows, cols = shards[name].shape
        if LARGE[name] == "col":
            return pl.BlockSpec((rows // 2, cols), lambda i, place_ref: (i, place_ref[0]))
        return pl.BlockSpec((rows // 2, cols), lambda i, place_ref: (2 * place_ref[0] + i, 0))

    def half(name):
        rows, cols = shards[name].shape
        return pl.BlockSpec((rows // 2, cols), lambda i, place_ref: (i, 0))

    return _pallas(body, [shards[name] for name in names], name="cast_weights", grid=(2,),
                   in_specs=[half(name) for name in names], out_specs=[window(name) for name in names],
                   out_shape=[jax.ShapeDtypeStruct(LARGE_SHAPE[name], BF16) for name in names],
                   semantics=("arbitrary",), jobs=jobs, prefetch=place)


SINGLE_BLOCK_BYTES = 1024 * 1024


def _adamw_math(w, g, m, v):
    m = ADAM_B1 * m + (1.0 - ADAM_B1) * g
    v = ADAM_B2 * v + (1.0 - ADAM_B2) * (g * g)
    m_hat = m / (1.0 - ADAM_B1 ** ADAM_STEP)
    v_hat = v / (1.0 - ADAM_B2 ** ADAM_STEP)
    delta = -ADAM_LR * (m_hat / (jnp.sqrt(v_hat) + ADAM_EPS) + ADAM_WD * w)
    return delta, m, v


def _adamw_large(w, g, m, v, name):
    rows, cols = w.shape
    steps = 1 if w.size * 4 <= SINGLE_BLOCK_BYTES else 2
    tr = rows // steps

    def body(w_ref, g_ref, m_ref, v_ref, d_ref, mo_ref, vo_ref):
        d_ref[...], mo_ref[...], vo_ref[...] = _adamw_math(w_ref[...], g_ref[...], m_ref[...], v_ref[...])

    blk = pl.BlockSpec((tr, cols), lambda i: (i, 0))
    out = jax.ShapeDtypeStruct(w.shape, F32)
    return pl.pallas_call(body, name=name + "_adamw", grid=(steps,), in_specs=[blk] * 4, out_specs=[blk] * 3, out_shape=[out] * 3,
                          compiler_params=_params(dimension_semantics=("parallel",)))(w, g, m, v)


def _adamw_small(ws, gs, ms, vs):
    n = len(ws)

    def body(*refs):
        for k in range(n):
            w_ref, g_ref, m_ref, v_ref = (refs[q * n + k] for q in range(4))
            d_ref, mo_ref, vo_ref = (refs[(4 + q) * n + k] for q in range(3))
            d_ref[...], mo_ref[...], vo_ref[...] = _adamw_math(w_ref[...], g_ref[...], m_ref[...], v_ref[...])

    out = [jax.ShapeDtypeStruct(w.shape, F32) for w in ws]
    res = pl.pallas_call(body, name="small_adamw", in_specs=[VMEM] * (4 * n), out_specs=[VMEM] * (3 * n), out_shape=out * 3,
                         compiler_params=_params())(*ws, *gs, *ms, *vs)
    return res[:n], res[n:2 * n], res[2 * n:]


WEIGHTS = ["norm_mix", "w_in", "w_pool_grp", "pool_scale", "w_pool_out", "conv_w", "conv_b", "w_rg_a", "b_rg_a", "w_rg_x",
           "b_rg_x", "lru_lambda", "w_rnn_out", "w_o", "norm_ffn", "w_ffn_in", "w_ffn_out", "norm_final"]
VEC_ITEMS = ["norm_mix", "norm_ffn", "norm_final", "pool_scale", "conv_b", "lru_lambda", "b_rg_a", "b_rg_x"]
MAT_ITEMS = ["w_pool_grp", "w_rg_a", "w_rg_x"]


def _as2d(name, a):
    if name in MAT_ITEMS:
        return a.reshape(-1, HEAD, HEAD)
    if name == "conv_w":
        return a.reshape(CONV_WIDTH, -1)
    return a.reshape(1, -1)


def kernel(x, norm_mix, w_in, w_pool_grp, pool_scale, w_pool_out, conv_w, conv_b, w_rg_a, b_rg_a, w_rg_x, b_rg_x, lru_lambda, w_rnn_out, w_o, norm_ffn, w_ffn_in, w_ffn_out, norm_final, loss_target, m_norm_mix, m_w_in, m_w_pool_grp, m_pool_scale, m_w_pool_out, m_conv_w, m_conv_b, m_w_rg_a, m_b_rg_a, m_w_rg_x, m_b_rg_x, m_lru_lambda, m_w_rnn_out, m_w_o, m_norm_ffn, m_w_ffn_in, m_w_ffn_out, m_norm_final, v_norm_mix, v_w_in, v_w_pool_grp, v_pool_scale, v_w_pool_out, v_conv_w, v_conv_b, v_w_rg_a, v_b_rg_a, v_w_rg_x, v_b_rg_x, v_lru_lambda, v_w_rnn_out, v_w_o, v_norm_ffn, v_w_ffn_in, v_w_ffn_out, v_norm_final):
    given = dict(locals())
    w = {name: given[name] for name in WEIGHTS}
    m = {name: given["m_" + name] for name in WEIGHTS}
    v = {name: given["v_" + name] for name in WEIGHTS}
    chip, c = _place()

    place = jnp.stack([chip, c]).astype(jnp.int32)
    conv_cols = w["conv_w"].shape[-1]
    conv_w_mine = lax.dynamic_update_slice_in_dim(jnp.zeros((CONV_WIDTH, D_RNN), F32), w["conv_w"][0], chip * conv_cols, axis=1)
    w_in_mine = _cast_into_whole(w["w_in"][0], "w_in", place)
    later = [name for name in LARGE if name != "w_in"]
    casts, ((w_in_full, conv_w_full),) = _cast_many_into_whole(
        {name: w[name][0] for name in later}, place, jobs=[_gather_job({"w_in": w_in_mine}, ["w_in"], conv_w_mine)])
    full = dict(zip(later, casts), w_in=w_in_full)
    small = {name: _as2d(name, w[name]) for name in WEIGHTS if name not in LARGE and name != "conv_w"}
    sq_cols, grad_x, grads = _step(x[0], loss_target[0], small, full, conv_w_full, place)
    loss = 0.5 / D_MODEL * jnp.sum(sq_cols)
    grads["conv_w"] = lax.dynamic_slice_in_dim(grads["conv_w"], chip * conv_cols, conv_cols, axis=1)

    delta, new_m, new_v = {}, {}, {}
    for name in LARGE:
        delta[name], new_m[name], new_v[name] = _adamw_large(w[name][0], grads[name], m[name][0], v[name][0], name)
    small_names = [name for name in WEIGHTS if name not in LARGE]
    flat = lambda d: [d[name].reshape(grads[name].shape) for name in small_names]
    ds, mo, vo = _adamw_small(flat(w), [grads[name] for name in small_names], flat(m), flat(v))
    for k, name in enumerate(small_names):
        delta[name], new_m[name], new_v[name] = ds[k], mo[k], vo[k]

    shaped = lambda d: [d[name].reshape(w[name].shape) for name in WEIGHTS]
    return (loss, grad_x[None], *shaped(grads), *shaped(delta), *shaped(new_m), *shaped(new_v))
```

```python
import functools
import math

import jax
import jax.numpy as jnp
from jax import lax
from jax.experimental import pallas as pl
from jax.experimental.pallas import tpu as pltpu

F32 = jnp.float32
BF16 = jnp.bfloat16

D_MODEL = 1024
D_POOL = 512
N_POOL_GROUPS = 4
D_RNN = 1024
N_RNN_HEADS = 8
HEAD = 128
CONV_WIDTH = 4
LRU_C = 8.0
D_FF = 2816
D_IN = D_POOL + 2 * D_RNN + 2 * D_MODEL
NORM_EPS = 1e-6
COL_RNN = D_POOL // HEAD
COL_GATE = (D_POOL + D_RNN) // HEAD

ADAM_LR = 0.001
ADAM_B1 = 0.9
ADAM_B2 = 0.999
ADAM_EPS = 1e-08
ADAM_WD = 0.01
ADAM_STEP = 10

N_CHIPS = 4
N_DEV = 8
MESH = pl.DeviceIdType.MESH
ANY = pl.BlockSpec(memory_space=pl.ANY)
VMEM = pl.BlockSpec(memory_space=pltpu.VMEM)
VMEM_LIMIT_BYTES = 60 * 1024 * 1024
SUBLANES = 8
POOL_HALO = 16
CHUNK = 1024

GELU_C = math.sqrt(2.0 / math.pi)
GELU_A = 0.044715


def _params(**kw):
    return pltpu.CompilerParams(vmem_limit_bytes=VMEM_LIMIT_BYTES, **kw)


def _sigmoid(x):
    return 0.5 * jnp.tanh(0.5 * x) + 0.5


def _log1p(y):
    u = 1.0 + y
    d = u - 1.0
    return jnp.where(d == 0.0, y, jnp.log(u) * (y / jnp.where(d == 0.0, 1.0, d)))


def _gelu_parts(x):
    x2 = x * x
    th = jnp.tanh(GELU_C * (x + GELU_A * x * x2))
    g = 0.5 * x * (1.0 + th)
    dg = 0.5 * (1.0 + th) + 0.5 * x * (1.0 - th * th) * GELU_C * (1.0 + 3.0 * GELU_A * x2)
    return g, dg


def _dot(a, b):
    return jnp.dot(a, b, preferred_element_type=F32)


def _dot_nt(a, b):
    return lax.dot_general(a, b, (((1,), (1,)), ((), ())), preferred_element_type=F32)


def _dot_tn(a, b):
    return lax.dot_general(a, b, (((0,), (0,)), ((), ())), preferred_element_type=F32)


def _rms_scale(xv):
    return lax.rsqrt(jnp.mean(xv * xv, axis=-1, keepdims=True) + NORM_EPS)


def _rms_bwd(dy, xv, g):
    r = _rms_scale(xv)
    xh = xv * r
    dyg = dy * g
    dx = r * (dyg - xh * jnp.mean(dyg * xh, axis=-1, keepdims=True))
    return dx, dy * xh


class _Job:
    def __init__(self, inputs, out_shapes, aliases, sem_shape, start, finish):
        self.inputs, self.out_shapes, self.aliases, self.sem_shape = list(inputs), list(out_shapes), dict(aliases), sem_shape
        self.start, self.finish = start, finish


def _pallas(body, operands, *, name, grid, in_specs, out_specs, out_shape, scratch_shapes=(), semantics=None, jobs=(),
            prefetch=None):
    n_in, n_out, n_scr = len(in_specs), len(out_specs), len(scratch_shapes)
    n_pre = 0 if prefetch is None else 1
    job_in = [a for job in jobs for a in job.inputs]
    job_out = [s for job in jobs for s in job.out_shapes]
    aliases, i0, o0 = {}, n_pre + n_in, n_out
    for job in jobs:
        aliases.update({i0 + i: o0 + o for i, o in job.aliases.items()})
        i0, o0 = i0 + len(job.inputs), o0 + len(job.out_shapes)

    def whole(*refs):
        pre, refs = refs[:n_pre], refs[n_pre:]
        ins, j_ins = refs[:n_in], refs[n_in:n_in + len(job_in)]
        outs = refs[n_in + len(job_in):][:n_out]
        j_outs = refs[n_in + len(job_in) + n_out:][:len(job_out)]
        rest = refs[n_in + len(job_in) + n_out + len(job_out):]
        scr, sems = rest[:n_scr], rest[n_scr:]

        def run(phase):
            i, o = 0, 0
            for k, job in enumerate(jobs):
                getattr(job, phase)(j_ins[i:i + len(job.inputs)], j_outs[o:o + len(job.out_shapes)], sems[2 * k], sems[2 * k + 1])
                i, o = i + len(job.inputs), o + len(job.out_shapes)

        def at(step_of, phase):
            if not jobs:
                return
            if not grid:
                run(phase)
                return
            cond = functools.reduce(jnp.logical_and, [pl.program_id(d) == step_of(d) for d in range(len(grid))])
            pl.when(cond)(functools.partial(run, phase))

        at(lambda d: 0, "start")
        body(*pre, *ins, *outs, *scr)
        at(lambda d: grid[d] - 1, "finish")

    layout = dict(grid=grid, in_specs=list(in_specs) + [ANY] * len(job_in), out_specs=list(out_specs) + [ANY] * len(job_out),
                  scratch_shapes=list(scratch_shapes) + [pltpu.SemaphoreType.DMA(job.sem_shape) for job in jobs for _ in range(2)])
    if prefetch is not None:
        layout = dict(grid_spec=pltpu.PrefetchScalarGridSpec(num_scalar_prefetch=1, **layout))
    res = pl.pallas_call(
        whole, name=name, out_shape=list(out_shape) + job_out, input_output_aliases=aliases,
        compiler_params=_params(dimension_semantics=semantics, has_side_effects=bool(jobs)), **layout,
    )(*([] if prefetch is None else [prefetch]), *operands, *job_in)
    per_job, o = [], n_out
    for job in jobs:
        per_job.append(res[o:o + len(job.out_shapes)])
        o += len(job.out_shapes)
    return res[:n_out], per_job


def _run_jobs(jobs, name):
    return _pallas(lambda: None, [], name=name, grid=(), in_specs=[], out_specs=[], out_shape=[], jobs=jobs)[1]


NORM_ROWS = 256
EPILOGUE_ROWS = 512


def _norm_rows(x_ref, g_ref, h_ref):
    g = g_ref[...]

    def rows(i, carry):
        r = pl.ds(pl.multiple_of(i * NORM_ROWS, NORM_ROWS), NORM_ROWS)
        xv = x_ref[r, :]
        h_ref[r, :] = (xv * _rms_scale(xv) * g).astype(BF16)
        return carry

    lax.fori_loop(0, x_ref.shape[0] // NORM_ROWS, rows, 0)


def _norm_matmul(x, g, w, *, tm, tn, name, jobs=()):
    T, K = x.shape
    N = w.shape[1]

    def body(x_ref, g_ref, w_ref, o_ref, h_ref):
        @pl.when(pl.program_id(1) == 0)
        def _():
            _norm_rows(x_ref, g_ref, h_ref)

        o_ref[...] = _dot(h_ref[...], w_ref[...])

    return _pallas(
        body, (x, g, w), name=name, grid=(T // tm, N // tn),
        in_specs=[pl.BlockSpec((tm, K), lambda i, j: (i, 0)), pl.BlockSpec((1, K), lambda i, j: (0, 0)),
                  pl.BlockSpec((K, tn), lambda i, j: (0, j))],
        out_specs=[pl.BlockSpec((tm, tn), lambda i, j: (i, j)), pl.BlockSpec((tm, K), lambda i, j: (i, 0))],
        out_shape=[jax.ShapeDtypeStruct((T, N), F32), jax.ShapeDtypeStruct((T, K), BF16)],
        semantics=("parallel", "arbitrary"), jobs=jobs)


def _ffn_in(x2, g, w, *, tm, tn, jobs=()):
    T, K = x2.shape
    nb = D_FF // tn

    def body(x_ref, g_ref, wg_ref, wu_ref, dup_ref, dgate_ref, act_ref, h_ref):
        @pl.when(pl.program_id(1) == 0)
        def _():
            _norm_rows(x_ref, g_ref, h_ref)

        wg, wu = wg_ref[...], wu_ref[...]
        for r in range(0, tm, EPILOGUE_ROWS):
            rows = pl.ds(r, min(EPILOGUE_ROWS, tm))
            h = h_ref[rows, :]
            gate, up = _dot(h, wg), _dot(h, wu)
            s = _sigmoid(gate)
            silu = gate * s
            dup_ref[rows, :] = silu.astype(BF16)
            dgate_ref[rows, :] = (up * (s + silu * (1.0 - s))).astype(BF16)
            act_ref[rows, :] = (silu * up).astype(BF16)

    blk = pl.BlockSpec((tm, tn), lambda i, j: (i, j))
    return _pallas(
        body, (x2, g, w, w), name="ffn_in", grid=(T // tm, nb),
        in_specs=[pl.BlockSpec((tm, K), lambda i, j: (i, 0)), pl.BlockSpec((1, K), lambda i, j: (0, 0)),
                  pl.BlockSpec((K, tn), lambda i, j: (0, j)), pl.BlockSpec((K, tn), lambda i, j: (0, j + nb))],
        out_specs=[blk, blk, blk, pl.BlockSpec((tm, K), lambda i, j: (i, 0))],
        out_shape=[jax.ShapeDtypeStruct((T, D_FF), BF16), jax.ShapeDtypeStruct((T, D_FF), BF16),
                   jax.ShapeDtypeStruct((T, D_FF), BF16), jax.ShapeDtypeStruct((T, K), BF16)],
        semantics=("parallel", "arbitrary"), jobs=jobs)


def _branch_mix(pm, z, w_pool_out, w_rnn_out, proj, *, tm, tn):
    T = pm.shape[0]
    col_gp = (D_POOL + 2 * D_RNN) // tn
    col_gr = col_gp + D_MODEL // tn

    def body(pm_ref, z_ref, wp_ref, wr_ref, gp_ref, gr_ref, by_gp_ref, by_gr_ref, sp_ref, sr_ref, mix_ref):
        yp = _dot(pm_ref[...], wp_ref[...])
        yr = _dot(z_ref[...], wr_ref[...])
        sp, sr = _sigmoid(gp_ref[...]), _sigmoid(gr_ref[...])
        by_gp_ref[...] = (yp * sp * (1.0 - sp)).astype(BF16)
        by_gr_ref[...] = (yr * sr * (1.0 - sr)).astype(BF16)
        sp_ref[...] = sp.astype(BF16)
        sr_ref[...] = sr.astype(BF16)
        mix_ref[...] = (sp * yp + sr * yr).astype(BF16)

    blk = pl.BlockSpec((tm, tn), lambda i, j: (i, j))
    out = jax.ShapeDtypeStruct((T, D_MODEL), BF16)
    return pl.pallas_call(
        body, name="branch_mix", grid=(T // tm, D_MODEL // tn),
        in_specs=[pl.BlockSpec((tm, D_POOL), lambda i, j: (i, 0)), pl.BlockSpec((tm, D_RNN), lambda i, j: (i, 0)),
                  pl.BlockSpec((D_POOL, tn), lambda i, j: (0, j)), pl.BlockSpec((D_RNN, tn), lambda i, j: (0, j)),
                  pl.BlockSpec((tm, tn), lambda i, j: (i, col_gp + j)), pl.BlockSpec((tm, tn), lambda i, j: (i, col_gr + j))],
        out_specs=[blk] * 5, out_shape=[out] * 5,
        compiler_params=_params(dimension_semantics=("parallel", "parallel")),
    )(pm, z, w_pool_out, w_rnn_out, proj, proj)


def _out_proj_residual(mix, w_o, x, *, tm):
    T = x.shape[0]

    def body(mix_ref, w_ref, x_ref, o_ref):
        o_ref[...] = x_ref[...] + _dot(mix_ref[...], w_ref[...])

    row = pl.BlockSpec((tm, D_MODEL), lambda i: (i, 0))
    return pl.pallas_call(
        body, name="out_proj_residual", grid=(T // tm,),
        in_specs=[row, pl.BlockSpec((D_MODEL, D_MODEL), lambda i: (0, 0)), row],
        out_specs=row, out_shape=jax.ShapeDtypeStruct((T, D_MODEL), F32),
        compiler_params=_params(dimension_semantics=("parallel",)),
    )(mix, w_o, x)


def _ffn_out_loss(act, w, x2, g3, target, *, tm):
    T = x2.shape[0]

    def body(act_ref, w_ref, x2_ref, g_ref, t_ref, dx_ref, dxb_ref, sq_ref, dg_ref):
        @pl.when(pl.program_id(0) == 0)
        def _():
            sq_ref[...] = jnp.zeros_like(sq_ref)
            dg_ref[...] = jnp.zeros_like(dg_ref)

        x3 = x2_ref[...] + _dot(act_ref[...], w_ref[...])
        g = g_ref[...]
        err = x3 * _rms_scale(x3) * g - t_ref[...]
        sq_ref[...] += jnp.sum(err * err, axis=0, keepdims=True)
        dx, dgp = _rms_bwd(err * (1.0 / D_MODEL), x3, g)
        dg_ref[...] += jnp.sum(dgp, axis=0, keepdims=True)
        dx_ref[...] = dx
        dxb_ref[...] = dx.astype(BF16)

    row = pl.BlockSpec((tm, D_MODEL), lambda i: (i, 0))
    vec = pl.BlockSpec((1, D_MODEL), lambda i: (0, 0))
    return pl.pallas_call(
        body, name="ffn_out_loss", grid=(T // tm,),
        in_specs=[pl.BlockSpec((tm, D_FF), lambda i: (i, 0)), pl.BlockSpec((D_FF, D_MODEL), lambda i: (0, 0)), row, vec, row],
        out_specs=[row, row, vec, vec],
        out_shape=[jax.ShapeDtypeStruct((T, D_MODEL), F32), jax.ShapeDtypeStruct((T, D_MODEL), BF16),
                   jax.ShapeDtypeStruct((1, D_MODEL), F32), jax.ShapeDtypeStruct((1, D_MODEL), F32)],
        compiler_params=_params(dimension_semantics=("arbitrary",)),
    )(act, w, x2, g3, target)


def _ffn_out_bwd(dx3b, w, act_by_gate, act_by_up, *, tm, tn):
    T = dx3b.shape[0]

    def body(dx_ref, w_ref, by_gate_ref, by_up_ref, dgate_ref, dup_ref):
        w = w_ref[...]
        for r in range(0, tm, EPILOGUE_ROWS):
            rows = pl.ds(r, min(EPILOGUE_ROWS, tm))
            dact = _dot_nt(dx_ref[rows, :], w)
            dgate_ref[rows, :] = (dact * by_gate_ref[rows, :].astype(F32)).astype(BF16)
            dup_ref[rows, :] = (dact * by_up_ref[rows, :].astype(F32)).astype(BF16)

    blk = pl.BlockSpec((tm, tn), lambda i, j: (i, j))
    return pl.pallas_call(
        body, name="ffn_out_bwd", grid=(T // tm, D_FF // tn),
        in_specs=[pl.BlockSpec((tm, D_MODEL), lambda i, j: (i, 0)), pl.BlockSpec((tn, D_MODEL), lambda i, j: (j, 0)), blk, blk],
        out_specs=[blk, blk],
        out_shape=[jax.ShapeDtypeStruct((T, D_FF), BF16), jax.ShapeDtypeStruct((T, D_FF), BF16)],
        compiler_params=_params(dimension_semantics=("parallel", "parallel")),
    )(dx3b, w, act_by_gate, act_by_up)


def _ffn_in_bwd(dgate, dup, w, dx3, x2, g2, *, tm, jobs=()):
    T = x2.shape[0]

    def body(dgate_ref, dup_ref, w_ref, dx3_ref, x2_ref, g_ref, dx_ref, dxb_ref, dg_ref):
        @pl.when(pl.program_id(0) == 0)
        def _():
            dg_ref[...] = jnp.zeros_like(dg_ref)

        dh = _dot_nt(dgate_ref[...], w_ref[:, :D_FF]) + _dot_nt(dup_ref[...], w_ref[:, D_FF:])
        dxn, dgp = _rms_bwd(dh, x2_ref[...], g_ref[...])
        dx = dx3_ref[...] + dxn
        dg_ref[...] += jnp.sum(dgp, axis=0, keepdims=True)
        dx_ref[...] = dx
        dxb_ref[...] = dx.astype(BF16)

    row = pl.BlockSpec((tm, D_MODEL), lambda i: (i, 0))
    wide = pl.BlockSpec((tm, D_FF), lambda i: (i, 0))
    vec = pl.BlockSpec((1, D_MODEL), lambda i: (0, 0))
    return _pallas(
        body, (dgate, dup, w, dx3, x2, g2), name="ffn_in_bwd", grid=(T // tm,),
        in_specs=[wide, wide, pl.BlockSpec((D_MODEL, 2 * D_FF), lambda i: (0, 0)), row, row, vec],
        out_specs=[row, row, vec],
        out_shape=[jax.ShapeDtypeStruct((T, D_MODEL), F32), jax.ShapeDtypeStruct((T, D_MODEL), BF16),
                   jax.ShapeDtypeStruct((1, D_MODEL), F32)],
        semantics=("arbitrary",), jobs=jobs)


def _out_proj_bwd(dx2b, w_o, mix_by, *, tm, tn, jobs=()):
    T = dx2b.shape[0]

    def body(dx_ref, w_ref, *refs):
        dmix = _dot_nt(dx_ref[...], w_ref[...])
        for by_ref, d_ref in zip(refs[:4], refs[4:]):
            d_ref[...] = (dmix * by_ref[...].astype(F32)).astype(BF16)

    blk = pl.BlockSpec((tm, tn), lambda i, j: (i, j))
    out = jax.ShapeDtypeStruct((T, D_MODEL), BF16)
    return _pallas(
        body, (dx2b, w_o, *mix_by), name="out_proj_bwd", grid=(T // tm, D_MODEL // tn),
        in_specs=[pl.BlockSpec((tm, D_MODEL), lambda i, j: (i, 0)), pl.BlockSpec((tn, D_MODEL), lambda i, j: (j, 0))] + [blk] * 4,
        out_specs=[blk] * 4, out_shape=[out] * 4, semantics=("parallel", "parallel"), jobs=jobs)


def _branch_bwd(dyp, dyr, w_pool_out, w_rnn_out, *, tm):
    T = dyp.shape[0]

    def body(dyp_ref, dyr_ref, wp_ref, wr_ref, dpm_ref, dz_ref):
        dpm_ref[...] = _dot_nt(dyp_ref[...], wp_ref[...])
        dz_ref[...] = _dot_nt(dyr_ref[...], wr_ref[...])

    row = pl.BlockSpec((tm, D_MODEL), lambda i: (i, 0))
    return pl.pallas_call(
        body, name="branch_bwd", grid=(T // tm,),
        in_specs=[row, row, pl.BlockSpec((D_POOL, D_MODEL), lambda i: (0, 0)), pl.BlockSpec((D_RNN, D_MODEL), lambda i: (0, 0))],
        out_specs=[pl.BlockSpec((tm, D_POOL), lambda i: (i, 0)), pl.BlockSpec((tm, D_RNN), lambda i: (i, 0))],
        out_shape=[jax.ShapeDtypeStruct((T, D_POOL), F32), jax.ShapeDtypeStruct((T, D_RNN), F32)],
        compiler_params=_params(dimension_semantics=("parallel",)),
    )(dyp, dyr, w_pool_out, w_rnn_out)


def _in_proj_bwd(segs, w, dx2, x, g1, *, tm, jobs=()):
    T = x.shape[0]
    widths = [s.shape[1] for s in segs]
    offs = [sum(widths[:k]) for k in range(len(widths))]
    n = len(segs)

    def body(*refs):
        seg_refs, (w_ref, dx2_ref, x_ref, g_ref, dx_ref, dg_ref) = refs[:n], refs[n:]

        @pl.when(pl.program_id(0) == 0)
        def _():
            dg_ref[...] = jnp.zeros_like(dg_ref)

        dh = _dot_nt(seg_refs[0][...], w_ref[:, offs[0]:offs[0] + widths[0]])
        for k in range(1, n):
            dh += _dot_nt(seg_refs[k][...], w_ref[:, offs[k]:offs[k] + widths[k]])
        dxn, dgp = _rms_bwd(dh, x_ref[...], g_ref[...])
        dg_ref[...] += jnp.sum(dgp, axis=0, keepdims=True)
        dx_ref[...] = dx2_ref[...] + dxn

    row = pl.BlockSpec((tm, D_MODEL), lambda i: (i, 0))
    vec = pl.BlockSpec((1, D_MODEL), lambda i: (0, 0))
    return _pallas(
        body, (*segs, w, dx2, x, g1), name="in_proj_bwd", grid=(T // tm,),
        in_specs=[pl.BlockSpec((tm, wd), lambda i: (i, 0)) for wd in widths]
        + [pl.BlockSpec((D_MODEL, D_IN), lambda i: (0, 0)), row, row, vec],
        out_specs=[row, vec],
        out_shape=[jax.ShapeDtypeStruct((T, D_MODEL), F32), jax.ShapeDtypeStruct((1, D_MODEL), F32)],
        semantics=("arbitrary",), jobs=jobs)


def _weight_grad(a, segs, *, tm, tn, name, jobs=None):
    T, M = a.shape
    nblk = [s.shape[1] // tn for s in segs]
    first = [sum(nblk[:k]) for k in range(len(segs))]
    n = len(segs)

    def body(a_ref, *refs):
        seg_refs, o_ref = refs[:n], refs[n]
        j = pl.program_id(1)
        for k in range(n):
            @pl.when((j >= first[k]) & (j < first[k] + nblk[k]))
            def _(k=k):
                o_ref[...] = _dot_tn(a_ref[...], seg_refs[k][...])

    def seg_spec(k):
        return pl.BlockSpec((T, tn), lambda i, j: (0, jnp.clip(j - first[k], 0, nblk[k] - 1)))

    (grad,), results = _pallas(
        body, (a, *segs), name=name, grid=(M // tm, sum(nblk)),
        in_specs=[pl.BlockSpec((T, tm), lambda i, j: (0, i))] + [seg_spec(k) for k in range(n)],
        out_specs=[pl.BlockSpec((tm, tn), lambda i, j: (i, j))],
        out_shape=[jax.ShapeDtypeStruct((M, sum(nblk) * tn), F32)],
        semantics=("parallel", "arbitrary"), jobs=jobs or ())
    return grad if jobs is None else (grad, results)


def _pad_front(dst, src, halo):
    dst[pl.ds(0, halo), :] = jnp.zeros((halo, src.shape[1]), F32)

    def fill(i, carry):
        r0 = pl.multiple_of(i * CHUNK, CHUNK)
        dst[pl.ds(r0 + halo, CHUNK), :] = src[pl.ds(r0, CHUNK), :]
        return carry

    lax.fori_loop(0, src.shape[0] // CHUNK, fill, 0)


def _shift_rows(v, k):
    return pltpu.roll(v, k % v.shape[0], axis=0)


def _window_sums(xs, direction):
    s2 = xs + _shift_rows(xs, direction)
    s4 = s2 + _shift_rows(s2, 2 * direction)
    s8 = s4 + _shift_rows(s4, 4 * direction)
    s16 = s8 + _shift_rows(s8, 8 * direction)
    return s2, s4, s8, s16


def _select_window(g, sums):
    s2, s4, s8, s16 = sums
    return jnp.where(g == 0, s2, jnp.where(g == 1, s4, jnp.where(g == 2, s8, s16)))


def _pool_count(g, start, rows):
    t = start + lax.broadcasted_iota(jnp.int32, (rows, 1), 0)
    return jnp.minimum(t + 1, jnp.left_shift(2, g)).astype(F32)


def _pool_fwd(proj, w_grp, scale):
    T = proj.shape[0]
    nchunk = T // CHUNK

    def body(u_ref, w_ref, s_ref, o_ref, upad):
        g = pl.program_id(0)
        _pad_front(upad, u_ref, POOL_HALO)
        w = w_ref[...].astype(BF16)
        scale_row = s_ref[...]

        def chunk(i, carry):
            r0 = pl.multiple_of(i * CHUNK, CHUNK)
            xs = upad[pl.ds(r0, CHUNK + POOL_HALO), :]
            win = _select_window(g, _window_sums(xs, 1))[POOL_HALO:]
            pooled = win / _pool_count(g, r0, CHUNK) - xs[POOL_HALO:]
            o_ref[pl.ds(r0, CHUNK), :] = (_dot(pooled.astype(BF16), w) * scale_row).astype(BF16)
            return carry

        lax.fori_loop(0, nchunk, chunk, 0)

    return pl.pallas_call(
        body, name="pool_fwd", grid=(N_POOL_GROUPS,),
        in_specs=[pl.BlockSpec((T, HEAD), lambda g: (0, g)), pl.BlockSpec((None, HEAD, HEAD), lambda g: (g, 0, 0)),
                  pl.BlockSpec((1, HEAD), lambda g: (0, g))],
        out_specs=pl.BlockSpec((T, HEAD), lambda g: (0, g)),
        out_shape=jax.ShapeDtypeStruct((T, D_POOL), BF16),
        scratch_shapes=[pltpu.VMEM((T + POOL_HALO, HEAD), F32)],
        compiler_params=_params(dimension_semantics=("parallel",)),
    )(proj, w_grp, scale)


def _pool_bwd(proj, dpm, w_grp, scale, jobs=()):
    T = proj.shape[0]
    nchunk = T // CHUNK

    def body(u_ref, dpm_ref, w_ref, s_ref, du_ref, dw_ref, ds_ref, upad, zpad, dpool):
        g = pl.program_id(0)
        _pad_front(upad, u_ref, POOL_HALO)
        zpad[pl.ds(T, POOL_HALO), :] = jnp.zeros((POOL_HALO, HEAD), F32)
        dw_ref[...] = jnp.zeros_like(dw_ref)
        ds_ref[...] = jnp.zeros_like(ds_ref)
        w = w_ref[...].astype(BF16)
        scale_row = s_ref[...]

        def chunk(i, carry):
            r0 = pl.multiple_of(i * CHUNK, CHUNK)
            xs = upad[pl.ds(r0, CHUNK + POOL_HALO), :]
            cnt = _pool_count(g, r0, CHUNK)
            pooled = (_select_window(g, _window_sums(xs, 1))[POOL_HALO:] / cnt - xs[POOL_HALO:]).astype(BF16)
            mixed = _dot(pooled, w)
            d = dpm_ref[pl.ds(r0, CHUNK), :]
            ds_ref[...] += jnp.sum(d * mixed, axis=0, keepdims=True)
            dmixed = (d * scale_row).astype(BF16)
            dw_ref[...] += _dot_tn(pooled, dmixed)
            dp = _dot_nt(dmixed, w)
            dpool[pl.ds(r0, CHUNK), :] = dp
            zpad[pl.ds(r0, CHUNK), :] = dp / cnt
            return carry

        lax.fori_loop(0, nchunk, chunk, 0)

        def chunk2(i, carry):
            r0 = pl.multiple_of(i * CHUNK, CHUNK)
            zs = zpad[pl.ds(r0, CHUNK + POOL_HALO), :]
            win = _select_window(g, _window_sums(zs, -1))[:CHUNK]
            du_ref[pl.ds(r0, CHUNK), :] = (win - dpool[pl.ds(r0, CHUNK), :]).astype(BF16)
            return carry

        lax.fori_loop(0, nchunk, chunk2, 0)

    col = pl.BlockSpec((T, HEAD), lambda g: (0, g))
    return _pallas(
        body, (proj, dpm, w_grp, scale), name="pool_bwd", grid=(N_POOL_GROUPS,),
        in_specs=[col, col, pl.BlockSpec((None, HEAD, HEAD), lambda g: (g, 0, 0)), pl.BlockSpec((1, HEAD), lambda g: (0, g))],
        out_specs=[col, pl.BlockSpec((None, HEAD, HEAD), lambda g: (g, 0, 0)), pl.BlockSpec((1, HEAD), lambda g: (0, g))],
        out_shape=[jax.ShapeDtypeStruct((T, D_POOL), BF16), jax.ShapeDtypeStruct((N_POOL_GROUPS, HEAD, HEAD), F32),
                   jax.ShapeDtypeStruct((1, D_POOL), F32)],
        scratch_shapes=[pltpu.VMEM((T + POOL_HALO, HEAD), F32), pltpu.VMEM((T + POOL_HALO, HEAD), F32), pltpu.VMEM((T, HEAD), F32)],
        semantics=("parallel",), jobs=jobs)


def _conv_taps(xs, cw):
    v = cw[CONV_WIDTH - 1] * xs[SUBLANES:]
    for k in range(CONV_WIDTH - 1):
        v += cw[k] * _shift_rows(xs, CONV_WIDTH - 1 - k)[SUBLANES:]
    return v


def _tap_rows(cw_ref):
    return [cw_ref[k:k + 1, :] for k in range(CONV_WIDTH)]


def _softplus_neg(lam):
    return jnp.maximum(-lam, 0.0) + _log1p(jnp.exp(-jnp.abs(lam)))


def _lru_gates(v, wa, ba, wx, bx, sp):
    vb = v.astype(BF16)
    ra = _sigmoid(_dot(vb, wa) + ba)
    ix = _sigmoid(_dot(vb, wx) + bx)
    log_a = -LRU_C * ra * sp
    a = jnp.exp(log_a)
    sq = jnp.sqrt(-jnp.tanh(log_a) * (a * a + 1.0))
    return ra, ix, a, sq


def _row_bcast(v, r):
    return jnp.broadcast_to(v[r:r + 1, :], v.shape)


TILE_BLOCK = 128


def _scan_in_tiles(coef, coef_shift, A_out, B, T, direction):
    order = list(range(SUBLANES)) if direction == 1 else list(range(SUBLANES - 1, -1, -1))
    tiles = min(TILE_BLOCK, T // SUBLANES)
    for base in range(0, T, tiles * SUBLANES):
        def rows(r, base=base):
            return pl.ds(base + r, tiles, stride=SUBLANES)

        A, Bv = coef[rows(order[0] + coef_shift), :], B[rows(order[0]), :]
        A_out[rows(order[0]), :] = A
        for r in order[1:]:
            a = coef[rows(r + coef_shift), :]
            Bv = a * Bv + B[rows(r), :]
            A = a * A
            A_out[rows(r), :] = A
            B[rows(r), :] = Bv


TILES_PER_STEP = 8


def _carry_tiles(A_s, B_s, out, ntile, direction):
    out_row = SUBLANES - 1 if direction == 1 else 0

    def step(k, carry):
        for j in range(TILES_PER_STEP):
            t = k * TILES_PER_STEP + j
            r0 = pl.multiple_of((t if direction == 1 else ntile - 1 - t) * SUBLANES, SUBLANES)
            A, B = A_s[pl.ds(r0, SUBLANES), :], B_s[pl.ds(r0, SUBLANES), :]
            out[pl.ds(r0, SUBLANES), :] = A * carry + B
            carry = _row_bcast(A, out_row) * carry + _row_bcast(B, out_row)
        return carry

    lax.fori_loop(0, ntile // TILES_PER_STEP, step, jnp.zeros((SUBLANES, HEAD), F32))


def _rnn_fwd(proj, conv_w, conv_b, w_a, b_a, w_x, b_x, lam, jobs=()):
    T = proj.shape[0]
    nchunk = T // CHUNK
    ntile = T // SUBLANES

    def body(u_ref, ug_ref, cw_ref, cb_ref, wa_ref, ba_ref, wx_ref, bx_ref, lam_ref,
             h_ref, z_ref, v_ref, ra_ref, ix_ref, a_ref, sq_ref, upad, a_s, b_s):
        _pad_front(upad, u_ref, SUBLANES)
        cw, cb = _tap_rows(cw_ref), cb_ref[...]
        wa, wx = wa_ref[...].astype(BF16), wx_ref[...].astype(BF16)
        ba, bx = ba_ref[...], bx_ref[...]
        sp = _softplus_neg(lam_ref[...])

        def chunk(i, carry):
            rows = pl.ds(pl.multiple_of(i * CHUNK, CHUNK), CHUNK)
            v = _conv_taps(upad[pl.ds(pl.multiple_of(i * CHUNK, CHUNK), CHUNK + SUBLANES), :], cw) + cb
            ra, ix, a, sq = _lru_gates(v, wa, ba, wx, bx, sp)
            v_ref[rows, :], ra_ref[rows, :], ix_ref[rows, :], a_ref[rows, :], sq_ref[rows, :] = v, ra, ix, a, sq
            a_s[rows, :], b_s[rows, :] = a, sq * ix * v
            return carry

        lax.fori_loop(0, nchunk, chunk, 0)
        _scan_in_tiles(a_s, 0, a_s, b_s, T, 1)
        _carry_tiles(a_s, b_s, h_ref, ntile, 1)

        def chunk3(i, carry):
            r0 = pl.multiple_of(i * CHUNK, CHUNK)
            gl, _ = _gelu_parts(ug_ref[pl.ds(r0, CHUNK), :])
            z_ref[pl.ds(r0, CHUNK), :] = (h_ref[pl.ds(r0, CHUNK), :] * gl).astype(BF16)
            return carry

        lax.fori_loop(0, nchunk, chunk3, 0)

    col = pl.BlockSpec((T, HEAD), lambda h: (0, h))
    vec = pl.BlockSpec((1, HEAD), lambda h: (0, h))
    mat = pl.BlockSpec((None, HEAD, HEAD), lambda h: (h, 0, 0))
    return _pallas(
        body, (proj, proj, conv_w, conv_b, w_a, b_a, w_x, b_x, lam), name="rnn_fwd", grid=(N_RNN_HEADS,),
        in_specs=[pl.BlockSpec((T, HEAD), lambda h: (0, COL_RNN + h)), pl.BlockSpec((T, HEAD), lambda h: (0, COL_GATE + h)),
                  pl.BlockSpec((CONV_WIDTH, HEAD), lambda h: (0, h)), vec, mat, vec, mat, vec, vec],
        out_specs=[col] * 7,
        out_shape=[jax.ShapeDtypeStruct((T, D_RNN), F32), jax.ShapeDtypeStruct((T, D_RNN), BF16)]
        + [jax.ShapeDtypeStruct((T, D_RNN), F32)] * 5,
        scratch_shapes=[pltpu.VMEM((T + SUBLANES, HEAD), F32), pltpu.VMEM((T, HEAD), F32), pltpu.VMEM((T, HEAD), F32)],
        semantics=("parallel",), jobs=jobs)


def _rnn_bwd(proj, hr, dz, gates, conv_w, w_a, w_x, lam, jobs=()):
    T = proj.shape[0]
    nchunk = T // CHUNK
    ntile = T // SUBLANES

    def body(u_ref, ug_ref, h_ref, dz_ref, v_ref, ra_ref, ix_ref, a_ref, sq_ref, cw_ref, wa_ref, wx_ref, lam_ref,
             du_ref, dug_ref, dwa_ref, dwx_ref, dba_ref, dbx_ref, dlam_ref, dcb_ref, dcw_ref,
             upad, hpad, apad, g_s, dvpad, ga_s):
        zero_tile = jnp.zeros((SUBLANES, HEAD), F32)
        _pad_front(upad, u_ref, SUBLANES)
        _pad_front(hpad, h_ref, SUBLANES)
        apad[pl.ds(T, SUBLANES), :] = zero_tile
        dvpad[pl.ds(T, SUBLANES), :] = zero_tile
        for ref in (dwa_ref, dwx_ref, dba_ref, dbx_ref, dlam_ref, dcb_ref, dcw_ref):
            ref[...] = jnp.zeros_like(ref)
        cw = _tap_rows(cw_ref)
        wa, wx = wa_ref[...].astype(BF16), wx_ref[...].astype(BF16)
        lam_row = lam_ref[...]
        sp = _softplus_neg(lam_row)

        def chunk(i, carry):
            rows = pl.ds(pl.multiple_of(i * CHUNK, CHUNK), CHUNK)
            apad[rows, :] = a_ref[rows, :]
            gl, dgl = _gelu_parts(ug_ref[rows, :])
            d = dz_ref[rows, :]
            g_s[rows, :] = d * gl
            dug_ref[rows, :] = (d * h_ref[rows, :] * dgl).astype(BF16)
            return carry

        lax.fori_loop(0, nchunk, chunk, 0)

        _scan_in_tiles(apad, 1, ga_s, g_s, T, -1)
        _carry_tiles(ga_s, g_s, g_s, ntile, -1)

        def chunk3(i, carry):
            r0 = pl.multiple_of(i * CHUNK, CHUNK)
            rows = pl.ds(r0, CHUNK)
            g = g_s[rows, :]
            h_prev = _shift_rows(hpad[pl.ds(r0, CHUNK + SUBLANES), :], 1)[SUBLANES:]
            v, ra, ix, sq, a = v_ref[rows, :], ra_ref[rows, :], ix_ref[rows, :], sq_ref[rows, :], a_ref[rows, :]
            d_sq = g * ix * v
            d_ix = g * sq * v
            d_la = a * g * h_prev - d_sq * a * a / sq
            dlam_ref[...] += jnp.sum(d_la * ra, axis=0, keepdims=True)
            d_pa = d_la * (-LRU_C) * sp * ra * (1.0 - ra)
            d_px = d_ix * ix * (1.0 - ix)
            vb, d_pab, d_pxb = v.astype(BF16), d_pa.astype(BF16), d_px.astype(BF16)
            dwa_ref[...] += _dot_tn(vb, d_pab)
            dwx_ref[...] += _dot_tn(vb, d_pxb)
            dba_ref[...] += jnp.sum(d_pa, axis=0, keepdims=True)
            dbx_ref[...] += jnp.sum(d_px, axis=0, keepdims=True)
            dv = g * sq * ix + _dot_nt(d_pab, wa) + _dot_nt(d_pxb, wx)
            dvpad[rows, :] = dv
            dcb_ref[...] += jnp.sum(dv, axis=0, keepdims=True)
            xs = upad[pl.ds(r0, CHUNK + SUBLANES), :]
            for k in range(CONV_WIDTH):
                u_k = _shift_rows(xs, CONV_WIDTH - 1 - k)[SUBLANES:] if k < CONV_WIDTH - 1 else xs[SUBLANES:]
                dcw_ref[k:k + 1, :] += jnp.sum(dv * u_k, axis=0, keepdims=True)
            return carry

        lax.fori_loop(0, nchunk, chunk3, 0)
        dlam_ref[...] = dlam_ref[...] * (LRU_C * _sigmoid(-lam_row))

        def chunk4(i, carry):
            r0 = pl.multiple_of(i * CHUNK, CHUNK)
            dvs = dvpad[pl.ds(r0, CHUNK + SUBLANES), :]
            du = cw[CONV_WIDTH - 1] * dvs[:CHUNK]
            for k in range(CONV_WIDTH - 1):
                du += cw[k] * _shift_rows(dvs, -(CONV_WIDTH - 1 - k))[:CHUNK]
            du_ref[pl.ds(r0, CHUNK), :] = du.astype(BF16)
            return carry

        lax.fori_loop(0, nchunk, chunk4, 0)

    col = pl.BlockSpec((T, HEAD), lambda h: (0, h))
    vec = pl.BlockSpec((1, HEAD), lambda h: (0, h))
    mat = pl.BlockSpec((None, HEAD, HEAD), lambda h: (h, 0, 0))
    taps = pl.BlockSpec((CONV_WIDTH, HEAD), lambda h: (0, h))
    vec_out = jax.ShapeDtypeStruct((1, D_RNN), F32)
    mat_out = jax.ShapeDtypeStruct((N_RNN_HEADS, HEAD, HEAD), F32)
    seq = pltpu.VMEM((T, HEAD), F32)
    seq_pad = pltpu.VMEM((T + SUBLANES, HEAD), F32)
    return _pallas(
        body, (proj, proj, hr, dz, *gates, conv_w, w_a, w_x, lam), name="rnn_bwd", grid=(N_RNN_HEADS,),
        in_specs=[pl.BlockSpec((T, HEAD), lambda h: (0, COL_RNN + h)), pl.BlockSpec((T, HEAD), lambda h: (0, COL_GATE + h))]
        + [col] * 7 + [taps, mat, mat, vec],
        out_specs=[col, col, mat, mat, vec, vec, vec, vec, taps],
        out_shape=[jax.ShapeDtypeStruct((T, D_RNN), BF16), jax.ShapeDtypeStruct((T, D_RNN), BF16), mat_out, mat_out,
                   vec_out, vec_out, vec_out, vec_out, jax.ShapeDtypeStruct((CONV_WIDTH, D_RNN), F32)],
        scratch_shapes=[seq_pad, seq_pad, seq_pad, seq, seq_pad, seq],
        semantics=("parallel",), jobs=jobs)


GROUP_FFN_OUT = ["w_ffn_out"]
GROUP_FFN_IN = ["w_ffn_in"]
GROUP_MIX = ["w_o", "w_pool_out", "w_rnn_out"]
GROUP_IN = ["w_in"]


def _step(x, target, s, full, conv_w, place):
    T = x.shape[0]
    tall, mid, low = min(T, 2048), min(T, 1024), min(T, 512)
    full = dict(full)

    def gathered(names, results):
        full.update(zip(names, results))

    early = ["w_pool_out", "w_rnn_out", "w_o", "w_ffn_out"]
    (proj, h1), (res,) = _norm_matmul(x, s["norm_mix"], full["w_in"], tm=tall, tn=512, name="in_proj", jobs=[_gather_job(full, early)])
    gathered(early, res)
    pm = _pool_fwd(proj, s["w_pool_grp"], s["pool_scale"])
    (hr, z, *gates), (res,) = _rnn_fwd(proj, conv_w, s["conv_b"], s["w_rg_a"], s["b_rg_a"], s["w_rg_x"], s["b_rg_x"],
                                       s["lru_lambda"], jobs=[_gather_job(full, ["w_ffn_in"])])
    gathered(["w_ffn_in"], res)
    *mix_by, mix = _branch_mix(pm, z, full["w_pool_out"], full["w_rnn_out"], proj, tm=tall, tn=256)
    x2 = _out_proj_residual(mix, full["w_o"], x, tm=mid)
    (act_by_up, act_by_gate, act, h2), _ = _ffn_in(x2, s["norm_ffn"], full["w_ffn_in"], tm=tall, tn=256)
    dx3, dx3b, sq_cols, g_norm_final = _ffn_out_loss(act, full["w_ffn_out"], x2, s["norm_final"], target, tm=low)

    g = {"norm_final": g_norm_final}

    def chip_sums(names, from_sibling):
        sums = {name: _chip_sum(name, g[name], got, place) for name, got in zip(names, from_sibling)}
        return {name: v[0] for name, v in sums.items()}, {name: v[1] for name, v in sums.items()}

    def final_sums(names, sums, from_chips):
        return {name: _final_sum(name, sums[name], got, place) for name, got in zip(names, from_chips)}

    dgate, dup = _ffn_out_bwd(dx3b, full["w_ffn_out"], act_by_gate, act_by_up, tm=tall, tn=256)
    g["w_ffn_out"] = _weight_grad(act, [dx3b], tm=256, tn=D_MODEL, name="w_ffn_out_grad")
    (dx2, dx2b, g["norm_ffn"]), (res,) = _ffn_in_bwd(dgate, dup, full["w_ffn_in"], dx3, x2, s["norm_ffn"], tm=low,
                                                     jobs=[_sibling_job(g, GROUP_FFN_OUT)])
    sums_ffn, sums_ffn_bf16 = chip_sums(GROUP_FFN_OUT, res)
    g["w_ffn_in"], (res,) = _weight_grad(h2, [dgate, dup], tm=D_MODEL, tn=256, name="w_ffn_in_grad",
                                         jobs=[_chips_job(sums_ffn_bf16, GROUP_FFN_OUT)])
    shards_ffn = final_sums(GROUP_FFN_OUT, sums_ffn, res)
    (dgp, dgr, dyp, dyr), (res,) = _out_proj_bwd(dx2b, full["w_o"], mix_by, tm=tall, tn=256,
                                                 jobs=[_sibling_job(g, GROUP_FFN_IN)])
    sums_ffn, sums_ffn_bf16 = chip_sums(GROUP_FFN_IN, res)
    g["w_o"] = _weight_grad(mix, [dx2b], tm=D_MODEL, tn=256, name="w_o_grad")
    dpm, dz = _branch_bwd(dyp, dyr, full["w_pool_out"], full["w_rnn_out"], tm=mid)
    g["w_pool_out"] = _weight_grad(pm, [dyp], tm=D_POOL, tn=256, name="w_pool_out_grad")
    g["w_rnn_out"] = _weight_grad(z, [dyr], tm=D_RNN, tn=256, name="w_rnn_out_grad")
    (dupool, g["w_pool_grp"], g["pool_scale"]), (res,) = _pool_bwd(proj, dpm, s["w_pool_grp"], s["pool_scale"],
                                                                   jobs=[_sibling_job(g, GROUP_MIX)])
    sums_mix, sums_mix_bf16 = chip_sums(GROUP_MIX, res)
    ((durnn, dugate, g["w_rg_a"], g["w_rg_x"], g["b_rg_a"], g["b_rg_x"], g["lru_lambda"], g["conv_b"], g["conv_w"]),
     (res,)) = _rnn_bwd(proj, hr, dz, gates, conv_w, s["w_rg_a"], s["w_rg_x"], s["lru_lambda"],
                        jobs=[_chips_job(sums_ffn_bf16, GROUP_FFN_IN)])
    shards_ffn.update(final_sums(GROUP_FFN_IN, sums_ffn, res))
    segs = [dupool, durnn, dugate, dgp, dgr]
    ffn = GROUP_FFN_OUT + GROUP_FFN_IN
    g["w_in"], (res, joined) = _weight_grad(h1, segs, tm=D_MODEL, tn=256, name="w_in_grad",
                                           jobs=[_chips_job(sums_mix_bf16, GROUP_MIX), _join_job(shards_ffn, ffn)])
    grads = dict(zip(ffn, joined))
    shards = final_sums(GROUP_MIX, sums_mix, res)
    (res,) = _run_jobs([_sibling_job(g, GROUP_IN)], "w_in_exchange_sibling")
    sums_in, sums_in_bf16 = chip_sums(GROUP_IN, res)
    (grad_x, g["norm_mix"]), (res,) = _in_proj_bwd(segs, full["w_in"], dx2, x, s["norm_mix"], tm=low,
                                                  jobs=[_chips_job(sums_in_bf16, GROUP_IN)])
    shards.update(final_sums(GROUP_IN, sums_in, res))

    vec_rows = [g[name] if name != "pool_scale" else jnp.pad(g[name], ((0, 0), (0, D_MODEL - D_POOL))) for name in VEC_ITEMS]
    vec_rows += [g["conv_w"], sq_cols, jnp.zeros((VEC_ROWS - len(VEC_ITEMS) - CONV_WIDTH - 1, D_MODEL), F32)]
    vec = jnp.concatenate(vec_rows, axis=0).reshape(VEC_ROWS, N_DEV, HEAD).transpose(1, 0, 2)
    mat = jnp.concatenate([g[name].reshape(-1, HEAD) for name in MAT_ITEMS], axis=0).reshape(N_DEV, -1, HEAD)
    (vec, mat), (joined,) = _all_reduce_small([vec, mat], jobs=[_join_job(shards, GROUP_MIX + GROUP_IN)])
    grads.update(zip(GROUP_MIX + GROUP_IN, joined))

    vec = vec.transpose(1, 0, 2).reshape(VEC_ROWS, D_MODEL)
    mat = mat.reshape(-1, HEAD)
    for k, name in enumerate(VEC_ITEMS):
        grads[name] = vec[k:k + 1, :s[name].shape[1]]
    grads["conv_w"] = vec[len(VEC_ITEMS):len(VEC_ITEMS) + CONV_WIDTH]
    row = 0
    for name in MAT_ITEMS:
        rows = s[name].shape[0] * HEAD
        grads[name] = mat[row:row + rows]
        row += rows
    return vec[len(VEC_ITEMS) + CONV_WIDTH], grad_x, grads


LARGE = {"w_in": "col", "w_pool_out": "col", "w_rnn_out": "row", "w_o": "row", "w_ffn_in": "col", "w_ffn_out": "row"}
LARGE_SHAPE = {"w_in": (D_MODEL, D_IN), "w_pool_out": (D_POOL, D_MODEL), "w_rnn_out": (D_RNN, D_MODEL),
               "w_o": (D_MODEL, D_MODEL), "w_ffn_in": (D_MODEL, 2 * D_FF), "w_ffn_out": (D_FF, D_MODEL)}


def _place():
    x, y, c = lax.axis_index("x"), lax.axis_index("y"), lax.axis_index("c")
    return 2 * x + y, c


def _chip_device(chip, c):
    return (chip // 2, chip % 2, c)


def _chip_window(ref, kind, shape, chip, half=None):
    K, N = shape
    if kind == "col":
        rows = slice(None) if half is None else pl.ds(half * (K // 2), K // 2)
        return ref.at[rows, pl.ds(chip * (N // N_CHIPS), N // N_CHIPS)]
    ks = K // N_CHIPS
    if half is None:
        return ref.at[pl.ds(chip * ks, ks), :]
    return ref.at[pl.ds(chip * ks + half * (ks // 2), ks // 2), :]


def _row_half(ref, half):
    rows = ref.shape[0] // 2
    return ref.at[pl.ds(half * rows, rows), :]


def _remote(win_src, win_dst, send_sems, recv_sems, idx, to):
    return pltpu.make_async_remote_copy(src_ref=win_src, dst_ref=win_dst, send_sem=send_sems.at[idx], recv_sem=recv_sems.at[idx],
                                        device_id=to, device_id_type=MESH)


def _gather_job(full, names, conv_w_full=None):
    n = len(names)
    cw_cols = D_RNN // N_CHIPS

    def windows(refs, chip, half):
        return [_chip_window(refs[k], LARGE[name], LARGE_SHAPE[name], chip, half) for k, name in enumerate(names)]

    def ici_copies(refs, send_sems, recv_sems, src_chip, dst_chip, c, r):
        wins = windows(refs, src_chip, c)
        if conv_w_full is not None:
            wins.append(refs[n].at[:, pl.ds(src_chip * cw_cols, cw_cols)])
        return [_remote(win, win, send_sems, recv_sems, (k, r), _chip_device(dst_chip, c)) for k, win in enumerate(wins)]

    def forwards(refs, send_sems, recv_sems, src_chip, half, to_core, chip, r):
        return [_remote(win, win, send_sems, recv_sems, (k, 3 + r), _chip_device(chip, to_core))
                for k, win in enumerate(windows(refs, src_chip, half))]

    def start(ins, outs, send_sems, recv_sems):
        chip, c = _place()
        for r in range(3):
            for cp in ici_copies(outs, send_sems, recv_sems, chip, chip ^ (r + 1), c, r):
                cp.start()

    def finish(ins, outs, send_sems, recv_sems):
        chip, c = _place()
        for r in range(3):
            for cp in ici_copies(outs, send_sems, recv_sems, chip ^ (r + 1), chip, c, r):
                cp.wait_recv()
            for cp in forwards(outs, send_sems, recv_sems, chip ^ (r + 1), c, 1 - c, chip, r):
                cp.start()
        for r in range(3):
            for cp in forwards(outs, send_sems, recv_sems, chip ^ (r + 1), 1 - c, c, chip, r):
                cp.wait_recv()
            for cp in ici_copies(outs, send_sems, recv_sems, chip, chip ^ (r + 1), c, r):
                cp.wait_send()
            for cp in forwards(outs, send_sems, recv_sems, chip ^ (r + 1), c, 1 - c, chip, r):
                cp.wait_send()

    arrays = [full[name] for name in names] + ([conv_w_full] if conv_w_full is not None else [])
    return _Job(arrays, [jax.ShapeDtypeStruct(a.shape, a.dtype) for a in arrays], {k: k for k in range(len(arrays))},
                (len(arrays), 6), start, finish)


def _core_halves(ref, kind, shape, c):
    return [_chip_window(ref, kind, shape, chip, c) for chip in range(N_CHIPS)]


def _sibling_job(grads, names):
    def start(ins, outs, send_sems, recv_sems):
        chip, c = _place()
        for k, name in enumerate(names):
            kind, shape = LARGE[name], LARGE_SHAPE[name]
            if kind == "col":
                pairs = [(_row_half(ins[k], 1 - c), outs[k])]
            else:
                rows = shape[0] // N_DEV
                pairs = [(win, outs[k].at[pl.ds(j * rows, rows), :]) for j, win in enumerate(_core_halves(ins[k], kind, shape, 1 - c))]
            for src, dst in pairs:
                _remote(src, dst, send_sems, recv_sems, k, _chip_device(chip, 1 - c)).start()

    def finish(ins, outs, send_sems, recv_sems):
        chip, c = _place()
        for k in range(len(names)):
            _remote(outs[k], outs[k], send_sems, recv_sems, k, _chip_device(chip, 1 - c)).wait()

    return _Job([grads[name] for name in names],
                [jax.ShapeDtypeStruct((LARGE_SHAPE[name][0] // 2, LARGE_SHAPE[name][1]), F32) for name in names], {},
                (len(names),), start, finish)


def _chip_sum(name, g, got, place):
    kind, (K, N) = LARGE[name], LARGE_SHAPE[name]
    rows = K // N_DEV
    piece_cols = N // N_CHIPS

    def body(place_ref, g_ref, got_ref, o_ref, ob_ref):
        total = g_ref[...] + got_ref[...]
        ob_ref[...] = total.astype(BF16)
        if kind == "col":
            for chip in range(N_CHIPS):
                @pl.when(place_ref[0] == chip)
                def _(chip=chip):
                    o_ref[...] = total[:, chip * piece_cols:(chip + 1) * piece_cols]
        else:
            @pl.when(pl.program_id(0) == place_ref[0])
            def _():
                o_ref[...] = total

    if kind == "col":
        mine = pl.BlockSpec((rows, N), lambda j, place_ref: (j + N_CHIPS * place_ref[1], 0))
        own = pl.BlockSpec((rows, piece_cols), lambda j, place_ref: (j, 0))
    else:
        mine = pl.BlockSpec((rows, N), lambda j, place_ref: (2 * j + place_ref[1], 0))
        own = pl.BlockSpec((rows, N), lambda j, place_ref: (0, 0))
    blk = pl.BlockSpec((rows, N), lambda j, place_ref: (j, 0))
    return pl.pallas_call(
        body, name=name + "_chip_sum",
        grid_spec=pltpu.PrefetchScalarGridSpec(num_scalar_prefetch=1, grid=(N_CHIPS,), in_specs=[mine, blk], out_specs=[own, blk]),
        out_shape=[jax.ShapeDtypeStruct(_piece_shape(name), F32), jax.ShapeDtypeStruct((K // 2, N), BF16)],
        compiler_params=_params(dimension_semantics=("arbitrary",)),
    )(place, g, got)


def _piece(ref, kind, shape, chip):
    K, N = shape
    if kind == "col":
        return ref.at[:, pl.ds(chip * (N // N_CHIPS), N // N_CHIPS)]
    return ref.at[pl.ds(chip * (K // N_DEV), K // N_DEV), :]


def _piece_shape(name):
    kind, (K, N) = LARGE[name], LARGE_SHAPE[name]
    return (K // 2, N // N_CHIPS) if kind == "col" else (K // N_DEV, N)


def _chips_job(sums, names):
    def copies(ins, outs, send_sems, recv_sems):
        chip, c = _place()
        return [_remote(_piece(ins[k], LARGE[name], LARGE_SHAPE[name], chip ^ (r + 1)), outs[k].at[r], send_sems, recv_sems, (k, r),
                        _chip_device(chip ^ (r + 1), c)) for k, name in enumerate(names) for r in range(3)]

    def start(*refs):
        for cp in copies(*refs):
            cp.start()

    def finish(*refs):
        for cp in copies(*refs):
            cp.wait()

    return _Job([sums[name] for name in names], [jax.ShapeDtypeStruct((3,) + _piece_shape(name), BF16) for name in names], {},
                (len(names), 3), start, finish)


def _final_sum(name, chip_sum, got, place):
    rows, cols = _piece_shape(name)

    def body(place_ref, s_ref, got_ref, o_ref):
        o_ref[...] = ((s_ref[...] + got_ref[0].astype(F32)) + got_ref[1].astype(F32)) + got_ref[2].astype(F32)

    mine = pl.BlockSpec((rows, cols), lambda i, place_ref: (0, 0))
    return pl.pallas_call(
        body, name=name + "_final_sum",
        grid_spec=pltpu.PrefetchScalarGridSpec(
            num_scalar_prefetch=1, grid=(1,), in_specs=[mine, pl.BlockSpec((3, rows, cols), lambda i, place_ref: (0, 0, 0))],
            out_specs=pl.BlockSpec((rows, cols), lambda i, place_ref: (place_ref[1], 0))),
        out_shape=jax.ShapeDtypeStruct((2 * rows, cols), F32),
        compiler_params=_params(dimension_semantics=("arbitrary",)),
    )(place, chip_sum, got)


def _join_job(shards, names):
    def half_copy(outs, send_sems, recv_sems, k, mine):
        chip, c = _place()
        win = _row_half(outs[k], c if mine else 1 - c)
        return _remote(win, win, send_sems, recv_sems, k, _chip_device(chip, 1 - c))

    def start(ins, outs, send_sems, recv_sems):
        for k in range(len(names)):
            half_copy(outs, send_sems, recv_sems, k, True).start()

    def finish(ins, outs, send_sems, recv_sems):
        for k in range(len(names)):
            half_copy(outs, send_sems, recv_sems, k, True).wait_send()
            half_copy(outs, send_sems, recv_sems, k, False).wait_recv()

    arrays = [shards[name] for name in names]
    return _Job(arrays, [jax.ShapeDtypeStruct(a.shape, F32) for a in arrays], {k: k for k in range(len(arrays))},
                (len(arrays),), start, finish)


VEC_ROWS = 16


def _all_reduce_small(slabs, jobs=()):
    n = len(slabs)

    def body(*refs):
        in_refs, out_refs, got_refs = refs[:n], refs[n:2 * n], refs[2 * n:3 * n]
        send_sems, recv_sems = refs[3 * n:]
        x, y, c = lax.axis_index("x"), lax.axis_index("y"), lax.axis_index("c")
        me = 4 * x + 2 * y + c

        def remote(src, dst, k, phase, r):
            other = me ^ r
            return pltpu.make_async_remote_copy(src_ref=src, dst_ref=dst, send_sem=send_sems.at[k, phase, r],
                                                recv_sem=recv_sems.at[k, phase, r],
                                                device_id=(other // 4, (other // 2) % 2, other % 2), device_id_type=MESH)

        scatter = [remote(in_refs[k].at[me ^ r], got_refs[k].at[r], k, 0, r) for r in range(1, N_DEV) for k in range(n)]
        for cp in scatter:
            cp.start()
        for cp in scatter:
            cp.wait()
        for k in range(n):
            total = in_refs[k][me]
            for r in range(1, N_DEV):
                total = total + got_refs[k][r]
            out_refs[k][me] = total
        gather = [remote(out_refs[k].at[me], out_refs[k].at[me], k, 1, r) for r in range(1, N_DEV) for k in range(n)]
        for cp in gather:
            cp.start()
        for r in range(1, N_DEV):
            for k in range(n):
                remote(out_refs[k].at[me ^ r], out_refs[k].at[me ^ r], k, 1, r).wait_recv()
        for cp in gather:
            cp.wait_send()

    return _pallas(
        body, slabs, name="all_reduce_small", grid=(), in_specs=[VMEM] * n, out_specs=[VMEM] * n,
        out_shape=[jax.ShapeDtypeStruct(s.shape, F32) for s in slabs],
        scratch_shapes=[pltpu.VMEM(s.shape, F32) for s in slabs]
        + [pltpu.SemaphoreType.DMA((n, 2, N_DEV)), pltpu.SemaphoreType.DMA((n, 2, N_DEV))], jobs=jobs)


def _cast_into_whole(w, name, place):
    rows, cols = w.shape
    tr = rows // 2

    def body(place_ref, w_ref, o_ref):
        o_ref[...] = w_ref[...].astype(BF16)

    if LARGE[name] == "col":
        window = pl.BlockSpec((tr, cols), lambda i, place_ref: (i, place_ref[0]))
    else:
        window = pl.BlockSpec((tr, cols), lambda i, place_ref: (2 * place_ref[0] + i, 0))
    return pl.pallas_call(
        body, name=name + "_cast",
        grid_spec=pltpu.PrefetchScalarGridSpec(num_scalar_prefetch=1, grid=(2,),
                                               in_specs=[pl.BlockSpec((tr, cols), lambda i, place_ref: (i, 0))], out_specs=window),
        out_shape=jax.ShapeDtypeStruct(LARGE_SHAPE[name], BF16),
        compiler_params=_params(dimension_semantics=("parallel",)))(place, w)


def _cast_many_into_whole(shards, place, jobs):
    names = list(shards)
    n = len(names)

    def body(place_ref, *refs):
        for w_ref, o_ref in zip(refs[:n], refs[n:]):
            o_ref[...] = w_ref[...].astype(BF16)

    def window(name):
        rows, cols = shards[name].shape
        if LARGE[name] == "col":
            return pl.BlockSpec((rows // 2, cols), lambda i, place_ref: (i, place_ref[0]))
        return pl.BlockSpec((rows // 2, cols), lambda i, place_ref: (2 * place_ref[0] + i, 0))

    def half(name):
        rows, cols = shards[name].shape
        return pl.BlockSpec((rows // 2, cols), lambda i, place_ref: (i, 0))

    return _pallas(body, [shards[name] for name in names], name="cast_weights", grid=(2,),
                   in_specs=[half(name) for name in names], out_specs=[window(name) for name in names],
                   out_shape=[jax.ShapeDtypeStruct(LARGE_SHAPE[name], BF16) for name in names],
                   semantics=("arbitrary",), jobs=jobs, prefetch=place)


def _adamw_math(w, g, m, v):
    m = ADAM_B1 * m + (1.0 - ADAM_B1) * g
    v = ADAM_B2 * v + (1.0 - ADAM_B2) * (g * g)
    m_hat = m / (1.0 - ADAM_B1 ** ADAM_STEP)
    v_hat = v / (1.0 - ADAM_B2 ** ADAM_STEP)
    delta = -ADAM_LR * (m_hat / (jnp.sqrt(v_hat) + ADAM_EPS) + ADAM_WD * w)
    return delta, m, v


def _adamw_large(w, g, m, v, name):
    rows, cols = w.shape
    steps = 2
    tr = rows // steps

    def body(w_ref, g_ref, m_ref, v_ref, d_ref, mo_ref, vo_ref):
        d_ref[...], mo_ref[...], vo_ref[...] = _adamw_math(w_ref[...], g_ref[...], m_ref[...], v_ref[...])

    blk = pl.BlockSpec((tr, cols), lambda i: (i, 0))
    out = jax.ShapeDtypeStruct(w.shape, F32)
    return pl.pallas_call(body, name=name + "_adamw", grid=(steps,), in_specs=[blk] * 4, out_specs=[blk] * 3, out_shape=[out] * 3,
                          compiler_params=_params(dimension_semantics=("parallel",)))(w, g, m, v)


def _adamw_small(ws, gs, ms, vs):
    n = len(ws)

    def body(*refs):
        for k in range(n):
            w_ref, g_ref, m_ref, v_ref = (refs[q * n + k] for q in range(4))
            d_ref, mo_ref, vo_ref = (refs[(4 + q) * n + k] for q in range(3))
            d_ref[...], mo_ref[...], vo_ref[...] = _adamw_math(w_ref[...], g_ref[...], m_ref[...], v_ref[...])

    out = [jax.ShapeDtypeStruct(w.shape, F32) for w in ws]
    res = pl.pallas_call(body, name="small_adamw", in_specs=[VMEM] * (4 * n), out_specs=[VMEM] * (3 * n), out_shape=out * 3,
                         compiler_params=_params())(*ws, *gs, *ms, *vs)
    return res[:n], res[n:2 * n], res[2 * n:]


WEIGHTS = ["norm_mix", "w_in", "w_pool_grp", "pool_scale", "w_pool_out", "conv_w", "conv_b", "w_rg_a", "b_rg_a", "w_rg_x",
           "b_rg_x", "lru_lambda", "w_rnn_out", "w_o", "norm_ffn", "w_ffn_in", "w_ffn_out", "norm_final"]
VEC_ITEMS = ["norm_mix", "norm_ffn", "norm_final", "pool_scale", "conv_b", "lru_lambda", "b_rg_a", "b_rg_x"]
MAT_ITEMS = ["w_pool_grp", "w_rg_a", "w_rg_x"]


def _as2d(name, a):
    if name in MAT_ITEMS:
        return a.reshape(-1, HEAD, HEAD)
    if name == "conv_w":
        return a.reshape(CONV_WIDTH, -1)
    return a.reshape(1, -1)


def kernel(x, norm_mix, w_in, w_pool_grp, pool_scale, w_pool_out, conv_w, conv_b, w_rg_a, b_rg_a, w_rg_x, b_rg_x, lru_lambda, w_rnn_out, w_o, norm_ffn, w_ffn_in, w_ffn_out, norm_final, loss_target, m_norm_mix, m_w_in, m_w_pool_grp, m_pool_scale, m_w_pool_out, m_conv_w, m_conv_b, m_w_rg_a, m_b_rg_a, m_w_rg_x, m_b_rg_x, m_lru_lambda, m_w_rnn_out, m_w_o, m_norm_ffn, m_w_ffn_in, m_w_ffn_out, m_norm_final, v_norm_mix, v_w_in, v_w_pool_grp, v_pool_scale, v_w_pool_out, v_conv_w, v_conv_b, v_w_rg_a, v_b_rg_a, v_w_rg_x, v_b_rg_x, v_lru_lambda, v_w_rnn_out, v_w_o, v_norm_ffn, v_w_ffn_in, v_w_ffn_out, v_norm_final):
    given = dict(locals())
    w = {name: given[name] for name in WEIGHTS}
    m = {name: given["m_" + name] for name in WEIGHTS}
    v = {name: given["v_" + name] for name in WEIGHTS}
    chip, c = _place()

    place = jnp.stack([chip, c]).astype(jnp.int32)
    conv_cols = w["conv_w"].shape[-1]
    conv_w_mine = lax.dynamic_update_slice_in_dim(jnp.zeros((CONV_WIDTH, D_RNN), F32), w["conv_w"][0], chip * conv_cols, axis=1)
    w_in_mine = _cast_into_whole(w["w_in"][0], "w_in", place)
    later = [name for name in LARGE if name != "w_in"]
    casts, ((w_in_full, conv_w_full),) = _cast_many_into_whole(
        {name: w[name][0] for name in later}, place, jobs=[_gather_job({"w_in": w_in_mine}, ["w_in"], conv_w_mine)])
    full = dict(zip(later, casts), w_in=w_in_full)
    small = {name: _as2d(name, w[name]) for name in WEIGHTS if name not in LARGE and name != "conv_w"}
    sq_cols, grad_x, grads = _step(x[0], loss_target[0], small, full, conv_w_full, place)
    loss = 0.5 / D_MODEL * jnp.sum(sq_cols)
    grads["conv_w"] = lax.dynamic_slice_in_dim(grads["conv_w"], chip * conv_cols, conv_cols, axis=1)

    delta, new_m, new_v = {}, {}, {}
    for name in LARGE:
        delta[name], new_m[name], new_v[name] = _adamw_large(w[name][0], grads[name], m[name][0], v[name][0], name)
    small_names = [name for name in WEIGHTS if name not in LARGE]
    flat = lambda d: [d[name].reshape(grads[name].shape) for name in small_names]
    ds, mo, vo = _adamw_small(flat(w), [grads[name] for name in small_names], flat(m), flat(v))
    for k, name in enumerate(small_names):
        delta[name], new_m[name], new_v[name] = ds[k], mo[k], vo[k]

    shaped = lambda d: [d[name].reshape(w[name].shape) for name in WEIGHTS]
    return (loss, grad_x[None], *shaped(grads), *shaped(delta), *shaped(new_m), *shaped(new_v))
```

```python
import functools
import math

import jax
import jax.numpy as jnp
from jax import lax
from jax.experimental import pallas as pl
from jax.experimental.pallas import tpu as pltpu

F32 = jnp.float32
BF16 = jnp.bfloat16

D_MODEL = 1024
D_POOL = 512
N_POOL_GROUPS = 4
D_RNN = 1024
N_RNN_HEADS = 8
HEAD = 128
CONV_WIDTH = 4
LRU_C = 8.0
D_FF = 2816
D_IN = D_POOL + 2 * D_RNN + 2 * D_MODEL
NORM_EPS = 1e-6
COL_RNN = D_POOL // HEAD
COL_GATE = (D_POOL + D_RNN) // HEAD

ADAM_LR = 0.001
ADAM_B1 = 0.9
ADAM_B2 = 0.999
ADAM_EPS = 1e-08
ADAM_WD = 0.01
ADAM_STEP = 10

N_CHIPS = 4
N_DEV = 8
MESH = pl.DeviceIdType.MESH
ANY = pl.BlockSpec(memory_space=pl.ANY)
VMEM = pl.BlockSpec(memory_space=pltpu.VMEM)
VMEM_LIMIT_BYTES = 60 * 1024 * 1024
SUBLANES = 8
POOL_HALO = 16
CHUNK = 1024

GELU_C = math.sqrt(2.0 / math.pi)
GELU_A = 0.044715


def _params(**kw):
    return pltpu.CompilerParams(vmem_limit_bytes=VMEM_LIMIT_BYTES, **kw)


def _sigmoid(x):
    return 0.5 * jnp.tanh(0.5 * x) + 0.5


def _log1p(y):
    u = 1.0 + y
    d = u - 1.0
    return jnp.where(d == 0.0, y, jnp.log(u) * (y / jnp.where(d == 0.0, 1.0, d)))


def _gelu_parts(x):
    x2 = x * x
    th = jnp.tanh(GELU_C * (x + GELU_A * x * x2))
    g = 0.5 * x * (1.0 + th)
    dg = 0.5 * (1.0 + th) + 0.5 * x * (1.0 - th * th) * GELU_C * (1.0 + 3.0 * GELU_A * x2)
    return g, dg


def _dot(a, b):
    return jnp.dot(a, b, preferred_element_type=F32)


def _dot_nt(a, b):
    return lax.dot_general(a, b, (((1,), (1,)), ((), ())), preferred_element_type=F32)


def _dot_tn(a, b):
    return lax.dot_general(a, b, (((0,), (0,)), ((), ())), preferred_element_type=F32)


def _rms_scale(xv):
    return lax.rsqrt(jnp.mean(xv * xv, axis=-1, keepdims=True) + NORM_EPS)


def _rms_bwd(dy, xv, g):
    r = _rms_scale(xv)
    xh = xv * r
    dyg = dy * g
    dx = r * (dyg - xh * jnp.mean(dyg * xh, axis=-1, keepdims=True))
    return dx, dy * xh


class _Job:
    def __init__(self, inputs, out_shapes, aliases, sem_shape, start, finish):
        self.inputs, self.out_shapes, self.aliases, self.sem_shape = list(inputs), list(out_shapes), dict(aliases), sem_shape
        self.start, self.finish = start, finish


def _pallas(body, operands, *, name, grid, in_specs, out_specs, out_shape, scratch_shapes=(), semantics=None, jobs=(),
            prefetch=None):
    n_in, n_out, n_scr = len(in_specs), len(out_specs), len(scratch_shapes)
    n_pre = 0 if prefetch is None else 1
    job_in = [a for job in jobs for a in job.inputs]
    job_out = [s for job in jobs for s in job.out_shapes]
    aliases, i0, o0 = {}, n_pre + n_in, n_out
    for job in jobs:
        aliases.update({i0 + i: o0 + o for i, o in job.aliases.items()})
        i0, o0 = i0 + len(job.inputs), o0 + len(job.out_shapes)

    def whole(*refs):
        pre, refs = refs[:n_pre], refs[n_pre:]
        ins, j_ins = refs[:n_in], refs[n_in:n_in + len(job_in)]
        outs = refs[n_in + len(job_in):][:n_out]
        j_outs = refs[n_in + len(job_in) + n_out:][:len(job_out)]
        rest = refs[n_in + len(job_in) + n_out + len(job_out):]
        scr, sems = rest[:n_scr], rest[n_scr:]

        def run(phase):
            i, o = 0, 0
            for k, job in enumerate(jobs):
                getattr(job, phase)(j_ins[i:i + len(job.inputs)], j_outs[o:o + len(job.out_shapes)], sems[2 * k], sems[2 * k + 1])
                i, o = i + len(job.inputs), o + len(job.out_shapes)

        def at(step_of, phase):
            if not jobs:
                return
            if not grid:
                run(phase)
                return
            cond = functools.reduce(jnp.logical_and, [pl.program_id(d) == step_of(d) for d in range(len(grid))])
            pl.when(cond)(functools.partial(run, phase))

        at(lambda d: 0, "start")
        body(*pre, *ins, *outs, *scr)
        at(lambda d: grid[d] - 1, "finish")

    layout = dict(grid=grid, in_specs=list(in_specs) + [ANY] * len(job_in), out_specs=list(out_specs) + [ANY] * len(job_out),
                  scratch_shapes=list(scratch_shapes) + [pltpu.SemaphoreType.DMA(job.sem_shape) for job in jobs for _ in range(2)])
    if prefetch is not None:
        layout = dict(grid_spec=pltpu.PrefetchScalarGridSpec(num_scalar_prefetch=1, **layout))
    res = pl.pallas_call(
        whole, name=name, out_shape=list(out_shape) + job_out, input_output_aliases=aliases,
        compiler_params=_params(dimension_semantics=semantics, has_side_effects=bool(jobs)), **layout,
    )(*([] if prefetch is None else [prefetch]), *operands, *job_in)
    per_job, o = [], n_out
    for job in jobs:
        per_job.append(res[o:o + len(job.out_shapes)])
        o += len(job.out_shapes)
    return res[:n_out], per_job


def _run_jobs(jobs, name):
    return _pallas(lambda: None, [], name=name, grid=(), in_specs=[], out_specs=[], out_shape=[], jobs=jobs)[1]


NORM_ROWS = 256
EPILOGUE_ROWS = 512


def _norm_rows(x_ref, g_ref, h_ref):
    g = g_ref[...]

    def rows(i, carry):
        r = pl.ds(pl.multiple_of(i * NORM_ROWS, NORM_ROWS), NORM_ROWS)
        xv = x_ref[r, :]
        h_ref[r, :] = (xv * _rms_scale(xv) * g).astype(BF16)
        return carry

    lax.fori_loop(0, x_ref.shape[0] // NORM_ROWS, rows, 0)


def _norm_matmul(x, g, w, *, tm, tn, name, jobs=()):
    T, K = x.shape
    N = w.shape[1]

    def body(x_ref, g_ref, w_ref, o_ref, h_ref):
        @pl.when(pl.program_id(1) == 0)
        def _():
            _norm_rows(x_ref, g_ref, h_ref)

        o_ref[...] = _dot(h_ref[...], w_ref[...])

    return _pallas(
        body, (x, g, w), name=name, grid=(T // tm, N // tn),
        in_specs=[pl.BlockSpec((tm, K), lambda i, j: (i, 0)), pl.BlockSpec((1, K), lambda i, j: (0, 0)),
                  pl.BlockSpec((K, tn), lambda i, j: (0, j))],
        out_specs=[pl.BlockSpec((tm, tn), lambda i, j: (i, j)), pl.BlockSpec((tm, K), lambda i, j: (i, 0))],
        out_shape=[jax.ShapeDtypeStruct((T, N), F32), jax.ShapeDtypeStruct((T, K), BF16)],
        semantics=("parallel", "arbitrary"), jobs=jobs)


def _ffn_in(x2, g, w, *, tm, tn, jobs=()):
    T, K = x2.shape
    nb = D_FF // tn

    def body(x_ref, g_ref, wg_ref, wu_ref, dup_ref, dgate_ref, act_ref, h_ref):
        @pl.when(pl.program_id(1) == 0)
        def _():
            _norm_rows(x_ref, g_ref, h_ref)

        wg, wu = wg_ref[...], wu_ref[...]
        for r in range(0, tm, EPILOGUE_ROWS):
            rows = pl.ds(r, min(EPILOGUE_ROWS, tm))
            h = h_ref[rows, :]
            gate, up = _dot(h, wg), _dot(h, wu)
            s = _sigmoid(gate)
            silu = gate * s
            dup_ref[rows, :] = silu.astype(BF16)
            dgate_ref[rows, :] = (up * (s + silu * (1.0 - s))).astype(BF16)
            act_ref[rows, :] = (silu * up).astype(BF16)

    blk = pl.BlockSpec((tm, tn), lambda i, j: (i, j))
    return _pallas(
        body, (x2, g, w, w), name="ffn_in", grid=(T // tm, nb),
        in_specs=[pl.BlockSpec((tm, K), lambda i, j: (i, 0)), pl.BlockSpec((1, K), lambda i, j: (0, 0)),
                  pl.BlockSpec((K, tn), lambda i, j: (0, j)), pl.BlockSpec((K, tn), lambda i, j: (0, j + nb))],
        out_specs=[blk, blk, blk, pl.BlockSpec((tm, K), lambda i, j: (i, 0))],
        out_shape=[jax.ShapeDtypeStruct((T, D_FF), BF16), jax.ShapeDtypeStruct((T, D_FF), BF16),
                   jax.ShapeDtypeStruct((T, D_FF), BF16), jax.ShapeDtypeStruct((T, K), BF16)],
        semantics=("parallel", "arbitrary"), jobs=jobs)


def _branch_mix(pm, z, w_pool_out, w_rnn_out, proj, *, tm, tn):
    T = pm.shape[0]
    col_gp = (D_POOL + 2 * D_RNN) // tn
    col_gr = col_gp + D_MODEL // tn

    def body(pm_ref, z_ref, wp_ref, wr_ref, gp_ref, gr_ref, by_gp_ref, by_gr_ref, sp_ref, sr_ref, mix_ref):
        wp, wr = wp_ref[...], wr_ref[...]
        for r in range(0, tm, EPILOGUE_ROWS):
            rows = pl.ds(r, min(EPILOGUE_ROWS, tm))
            yp, yr = _dot(pm_ref[rows, :], wp), _dot(z_ref[rows, :], wr)
            sp, sr = _sigmoid(gp_ref[rows, :]), _sigmoid(gr_ref[rows, :])
            by_gp_ref[rows, :] = (yp * sp * (1.0 - sp)).astype(BF16)
            by_gr_ref[rows, :] = (yr * sr * (1.0 - sr)).astype(BF16)
            sp_ref[rows, :] = sp.astype(BF16)
            sr_ref[rows, :] = sr.astype(BF16)
            mix_ref[rows, :] = (sp * yp + sr * yr).astype(BF16)

    blk = pl.BlockSpec((tm, tn), lambda i, j: (i, j))
    out = jax.ShapeDtypeStruct((T, D_MODEL), BF16)
    return pl.pallas_call(
        body, name="branch_mix", grid=(T // tm, D_MODEL // tn),
        in_specs=[pl.BlockSpec((tm, D_POOL), lambda i, j: (i, 0)), pl.BlockSpec((tm, D_RNN), lambda i, j: (i, 0)),
                  pl.BlockSpec((D_POOL, tn), lambda i, j: (0, j)), pl.BlockSpec((D_RNN, tn), lambda i, j: (0, j)),
                  pl.BlockSpec((tm, tn), lambda i, j: (i, col_gp + j)), pl.BlockSpec((tm, tn), lambda i, j: (i, col_gr + j))],
        out_specs=[blk] * 5, out_shape=[out] * 5,
        compiler_params=_params(dimension_semantics=("parallel", "parallel")),
    )(pm, z, w_pool_out, w_rnn_out, proj, proj)


def _out_proj_residual(mix, w_o, x, *, tm):
    T = x.shape[0]

    def body(mix_ref, w_ref, x_ref, o_ref):
        o_ref[...] = x_ref[...] + _dot(mix_ref[...], w_ref[...])

    row = pl.BlockSpec((tm, D_MODEL), lambda i: (i, 0))
    return pl.pallas_call(
        body, name="out_proj_residual", grid=(T // tm,),
        in_specs=[row, pl.BlockSpec((D_MODEL, D_MODEL), lambda i: (0, 0)), row],
        out_specs=row, out_shape=jax.ShapeDtypeStruct((T, D_MODEL), F32),
        compiler_params=_params(dimension_semantics=("parallel",)),
    )(mix, w_o, x)


def _ffn_out_loss(act, w, x2, g3, target, *, tm):
    T = x2.shape[0]

    def body(act_ref, w_ref, x2_ref, g_ref, t_ref, dx_ref, dxb_ref, sq_ref, dg_ref):
        @pl.when(pl.program_id(0) == 0)
        def _():
            sq_ref[...] = jnp.zeros_like(sq_ref)
            dg_ref[...] = jnp.zeros_like(dg_ref)

        x3 = x2_ref[...] + _dot(act_ref[...], w_ref[...])
        g = g_ref[...]
        err = x3 * _rms_scale(x3) * g - t_ref[...]
        sq_ref[...] += jnp.sum(err * err, axis=0, keepdims=True)
        dx, dgp = _rms_bwd(err * (1.0 / D_MODEL), x3, g)
        dg_ref[...] += jnp.sum(dgp, axis=0, keepdims=True)
        dx_ref[...] = dx
        dxb_ref[...] = dx.astype(BF16)

    row = pl.BlockSpec((tm, D_MODEL), lambda i: (i, 0))
    vec = pl.BlockSpec((1, D_MODEL), lambda i: (0, 0))
    return pl.pallas_call(
        body, name="ffn_out_loss", grid=(T // tm,),
        in_specs=[pl.BlockSpec((tm, D_FF), lambda i: (i, 0)), pl.BlockSpec((D_FF, D_MODEL), lambda i: (0, 0)), row, vec, row],
        out_specs=[row, row, vec, vec],
        out_shape=[jax.ShapeDtypeStruct((T, D_MODEL), F32), jax.ShapeDtypeStruct((T, D_MODEL), BF16),
                   jax.ShapeDtypeStruct((1, D_MODEL), F32), jax.ShapeDtypeStruct((1, D_MODEL), F32)],
        compiler_params=_params(dimension_semantics=("arbitrary",)),
    )(act, w, x2, g3, target)


def _ffn_out_bwd(dx3b, w, act_by_gate, act_by_up, *, tm, tn):
    T = dx3b.shape[0]

    def body(dx_ref, w_ref, by_gate_ref, by_up_ref, dgate_ref, dup_ref):
        w = w_ref[...]
        for r in range(0, tm, EPILOGUE_ROWS):
            rows = pl.ds(r, min(EPILOGUE_ROWS, tm))
            dact = _dot_nt(dx_ref[rows, :], w)
            dgate_ref[rows, :] = (dact * by_gate_ref[rows, :].astype(F32)).astype(BF16)
            dup_ref[rows, :] = (dact * by_up_ref[rows, :].astype(F32)).astype(BF16)

    blk = pl.BlockSpec((tm, tn), lambda i, j: (i, j))
    return pl.pallas_call(
        body, name="ffn_out_bwd", grid=(T // tm, D_FF // tn),
        in_specs=[pl.BlockSpec((tm, D_MODEL), lambda i, j: (i, 0)), pl.BlockSpec((tn, D_MODEL), lambda i, j: (j, 0)), blk, blk],
        out_specs=[blk, blk],
        out_shape=[jax.ShapeDtypeStruct((T, D_FF), BF16), jax.ShapeDtypeStruct((T, D_FF), BF16)],
        compiler_params=_params(dimension_semantics=("parallel", "parallel")),
    )(dx3b, w, act_by_gate, act_by_up)


def _ffn_in_bwd(dgate, dup, w, dx3, x2, g2, *, tm, jobs=()):
    T = x2.shape[0]

    def body(dgate_ref, dup_ref, w_ref, dx3_ref, x2_ref, g_ref, dx_ref, dxb_ref, dg_ref):
        @pl.when(pl.program_id(0) == 0)
        def _():
            dg_ref[...] = jnp.zeros_like(dg_ref)

        dh = _dot_nt(dgate_ref[...], w_ref[:, :D_FF]) + _dot_nt(dup_ref[...], w_ref[:, D_FF:])
        dxn, dgp = _rms_bwd(dh, x2_ref[...], g_ref[...])
        dx = dx3_ref[...] + dxn
        dg_ref[...] += jnp.sum(dgp, axis=0, keepdims=True)
        dx_ref[...] = dx
        dxb_ref[...] = dx.astype(BF16)

    row = pl.BlockSpec((tm, D_MODEL), lambda i: (i, 0))
    wide = pl.BlockSpec((tm, D_FF), lambda i: (i, 0))
    vec = pl.BlockSpec((1, D_MODEL), lambda i: (0, 0))
    return _pallas(
        body, (dgate, dup, w, dx3, x2, g2), name="ffn_in_bwd", grid=(T // tm,),
        in_specs=[wide, wide, pl.BlockSpec((D_MODEL, 2 * D_FF), lambda i: (0, 0)), row, row, vec],
        out_specs=[row, row, vec],
        out_shape=[jax.ShapeDtypeStruct((T, D_MODEL), F32), jax.ShapeDtypeStruct((T, D_MODEL), BF16),
                   jax.ShapeDtypeStruct((1, D_MODEL), F32)],
        semantics=("arbitrary",), jobs=jobs)


def _out_proj_bwd(dx2b, w_o, mix_by, *, tm, tn, jobs=()):
    T = dx2b.shape[0]

    def body(dx_ref, w_ref, *refs):
        w = w_ref[...]
        for r in range(0, tm, EPILOGUE_ROWS):
            rows = pl.ds(r, min(EPILOGUE_ROWS, tm))
            dmix = _dot_nt(dx_ref[rows, :], w)
            for by_ref, d_ref in zip(refs[:4], refs[4:]):
                d_ref[rows, :] = (dmix * by_ref[rows, :].astype(F32)).astype(BF16)

    blk = pl.BlockSpec((tm, tn), lambda i, j: (i, j))
    out = jax.ShapeDtypeStruct((T, D_MODEL), BF16)
    return _pallas(
        body, (dx2b, w_o, *mix_by), name="out_proj_bwd", grid=(T // tm, D_MODEL // tn),
        in_specs=[pl.BlockSpec((tm, D_MODEL), lambda i, j: (i, 0)), pl.BlockSpec((tn, D_MODEL), lambda i, j: (j, 0))] + [blk] * 4,
        out_specs=[blk] * 4, out_shape=[out] * 4, semantics=("parallel", "parallel"), jobs=jobs)


def _branch_bwd(dyp, dyr, w_pool_out, w_rnn_out, *, tm):
    T = dyp.shape[0]

    def body(dyp_ref, dyr_ref, wp_ref, wr_ref, dpm_ref, dz_ref):
        dpm_ref[...] = _dot_nt(dyp_ref[...], wp_ref[...])
        dz_ref[...] = _dot_nt(dyr_ref[...], wr_ref[...])

    row = pl.BlockSpec((tm, D_MODEL), lambda i: (i, 0))
    return pl.pallas_call(
        body, name="branch_bwd", grid=(T // tm,),
        in_specs=[row, row, pl.BlockSpec((D_POOL, D_MODEL), lambda i: (0, 0)), pl.BlockSpec((D_RNN, D_MODEL), lambda i: (0, 0))],
        out_specs=[pl.BlockSpec((tm, D_POOL), lambda i: (i, 0)), pl.BlockSpec((tm, D_RNN), lambda i: (i, 0))],
        out_shape=[jax.ShapeDtypeStruct((T, D_POOL), F32), jax.ShapeDtypeStruct((T, D_RNN), F32)],
        compiler_params=_params(dimension_semantics=("parallel",)),
    )(dyp, dyr, w_pool_out, w_rnn_out)


def _in_proj_bwd(segs, w, dx2, x, g1, *, tm, jobs=()):
    T = x.shape[0]
    widths = [s.shape[1] for s in segs]
    offs = [sum(widths[:k]) for k in range(len(widths))]
    n = len(segs)

    def body(*refs):
        seg_refs, (w_ref, dx2_ref, x_ref, g_ref, dx_ref, dg_ref) = refs[:n], refs[n:]

        @pl.when(pl.program_id(0) == 0)
        def _():
            dg_ref[...] = jnp.zeros_like(dg_ref)

        dh = _dot_nt(seg_refs[0][...], w_ref[:, offs[0]:offs[0] + widths[0]])
        for k in range(1, n):
            dh += _dot_nt(seg_refs[k][...], w_ref[:, offs[k]:offs[k] + widths[k]])
        dxn, dgp = _rms_bwd(dh, x_ref[...], g_ref[...])
        dg_ref[...] += jnp.sum(dgp, axis=0, keepdims=True)
        dx_ref[...] = dx2_ref[...] + dxn

    row = pl.BlockSpec((tm, D_MODEL), lambda i: (i, 0))
    vec = pl.BlockSpec((1, D_MODEL), lambda i: (0, 0))
    return _pallas(
        body, (*segs, w, dx2, x, g1), name="in_proj_bwd", grid=(T // tm,),
        in_specs=[pl.BlockSpec((tm, wd), lambda i: (i, 0)) for wd in widths]
        + [pl.BlockSpec((D_MODEL, D_IN), lambda i: (0, 0)), row, row, vec],
        out_specs=[row, vec],
        out_shape=[jax.ShapeDtypeStruct((T, D_MODEL), F32), jax.ShapeDtypeStruct((1, D_MODEL), F32)],
        semantics=("arbitrary",), jobs=jobs)


def _weight_grad(a, segs, *, tm, tn, name, jobs=None):
    T, M = a.shape
    nblk = [s.shape[1] // tn for s in segs]
    first = [sum(nblk[:k]) for k in range(len(segs))]
    n = len(segs)

    def body(a_ref, *refs):
        seg_refs, o_ref = refs[:n], refs[n]
        j = pl.program_id(1)
        for k in range(n):
            @pl.when((j >= first[k]) & (j < first[k] + nblk[k]))
            def _(k=k):
                o_ref[...] = _dot_tn(a_ref[...], seg_refs[k][...])

    def seg_spec(k):
        return pl.BlockSpec((T, tn), lambda i, j: (0, jnp.clip(j - first[k], 0, nblk[k] - 1)))

    (grad,), results = _pallas(
        body, (a, *segs), name=name, grid=(M // tm, sum(nblk)),
        in_specs=[pl.BlockSpec((T, tm), lambda i, j: (0, i))] + [seg_spec(k) for k in range(n)],
        out_specs=[pl.BlockSpec((tm, tn), lambda i, j: (i, j))],
        out_shape=[jax.ShapeDtypeStruct((M, sum(nblk) * tn), F32)],
        semantics=("parallel", "arbitrary"), jobs=jobs or ())
    return grad if jobs is None else (grad, results)


def _pad_front(dst, src, halo):
    dst[pl.ds(0, halo), :] = jnp.zeros((halo, src.shape[1]), F32)

    def fill(i, carry):
        r0 = pl.multiple_of(i * CHUNK, CHUNK)
        dst[pl.ds(r0 + halo, CHUNK), :] = src[pl.ds(r0, CHUNK), :]
        return carry

    lax.fori_loop(0, src.shape[0] // CHUNK, fill, 0)


def _shift_rows(v, k):
    return pltpu.roll(v, k % v.shape[0], axis=0)


def _window_sums(xs, direction):
    s2 = xs + _shift_rows(xs, direction)
    s4 = s2 + _shift_rows(s2, 2 * direction)
    s8 = s4 + _shift_rows(s4, 4 * direction)
    s16 = s8 + _shift_rows(s8, 8 * direction)
    return s2, s4, s8, s16


def _select_window(g, sums):
    s2, s4, s8, s16 = sums
    return jnp.where(g == 0, s2, jnp.where(g == 1, s4, jnp.where(g == 2, s8, s16)))


def _pool_count(g, start, rows):
    t = start + lax.broadcasted_iota(jnp.int32, (rows, 1), 0)
    return jnp.minimum(t + 1, jnp.left_shift(2, g)).astype(F32)


def _pool_fwd(proj, w_grp, scale):
    T = proj.shape[0]
    nchunk = T // CHUNK

    def body(u_ref, w_ref, s_ref, o_ref, upad):
        g = pl.program_id(0)
        _pad_front(upad, u_ref, POOL_HALO)
        w = w_ref[...].astype(BF16)
        scale_row = s_ref[...]

        def chunk(i, carry):
            r0 = pl.multiple_of(i * CHUNK, CHUNK)
            xs = upad[pl.ds(r0, CHUNK + POOL_HALO), :]
            win = _select_window(g, _window_sums(xs, 1))[POOL_HALO:]
            pooled = win / _pool_count(g, r0, CHUNK) - xs[POOL_HALO:]
            o_ref[pl.ds(r0, CHUNK), :] = (_dot(pooled.astype(BF16), w) * scale_row).astype(BF16)
            return carry

        lax.fori_loop(0, nchunk, chunk, 0)

    return pl.pallas_call(
        body, name="pool_fwd", grid=(N_POOL_GROUPS,),
        in_specs=[pl.BlockSpec((T, HEAD), lambda g: (0, g)), pl.BlockSpec((None, HEAD, HEAD), lambda g: (g, 0, 0)),
                  pl.BlockSpec((1, HEAD), lambda g: (0, g))],
        out_specs=pl.BlockSpec((T, HEAD), lambda g: (0, g)),
        out_shape=jax.ShapeDtypeStruct((T, D_POOL), BF16),
        scratch_shapes=[pltpu.VMEM((T + POOL_HALO, HEAD), F32)],
        compiler_params=_params(dimension_semantics=("parallel",)),
    )(proj, w_grp, scale)


def _pool_bwd(proj, dpm, w_grp, scale, jobs=()):
    T = proj.shape[0]
    nchunk = T // CHUNK

    def body(u_ref, dpm_ref, w_ref, s_ref, du_ref, dw_ref, ds_ref, upad, zpad, dpool):
        g = pl.program_id(0)
        _pad_front(upad, u_ref, POOL_HALO)
        zpad[pl.ds(T, POOL_HALO), :] = jnp.zeros((POOL_HALO, HEAD), F32)
        dw_ref[...] = jnp.zeros_like(dw_ref)
        ds_ref[...] = jnp.zeros_like(ds_ref)
        w = w_ref[...].astype(BF16)
        scale_row = s_ref[...]

        def chunk(i, carry):
            r0 = pl.multiple_of(i * CHUNK, CHUNK)
            xs = upad[pl.ds(r0, CHUNK + POOL_HALO), :]
            cnt = _pool_count(g, r0, CHUNK)
            pooled = (_select_window(g, _window_sums(xs, 1))[POOL_HALO:] / cnt - xs[POOL_HALO:]).astype(BF16)
            mixed = _dot(pooled, w)
            d = dpm_ref[pl.ds(r0, CHUNK), :]
            ds_ref[...] += jnp.sum(d * mixed, axis=0, keepdims=True)
            dmixed = (d * scale_row).astype(BF16)
            dw_ref[...] += _dot_tn(pooled, dmixed)
            dp = _dot_nt(dmixed, w)
            dpool[pl.ds(r0, CHUNK), :] = dp
            zpad[pl.ds(r0, CHUNK), :] = dp / cnt
            return carry

        lax.fori_loop(0, nchunk, chunk, 0)

        def chunk2(i, carry):
            r0 = pl.multiple_of(i * CHUNK, CHUNK)
            zs = zpad[pl.ds(r0, CHUNK + POOL_HALO), :]
            win = _select_window(g, _window_sums(zs, -1))[:CHUNK]
            du_ref[pl.ds(r0, CHUNK), :] = (win - dpool[pl.ds(r0, CHUNK), :]).astype(BF16)
            return carry

        lax.fori_loop(0, nchunk, chunk2, 0)

    col = pl.BlockSpec((T, HEAD), lambda g: (0, g))
    return _pallas(
        body, (proj, dpm, w_grp, scale), name="pool_bwd", grid=(N_POOL_GROUPS,),
        in_specs=[col, col, pl.BlockSpec((None, HEAD, HEAD), lambda g: (g, 0, 0)), pl.BlockSpec((1, HEAD), lambda g: (0, g))],
        out_specs=[col, pl.BlockSpec((None, HEAD, HEAD), lambda g: (g, 0, 0)), pl.BlockSpec((1, HEAD), lambda g: (0, g))],
        out_shape=[jax.ShapeDtypeStruct((T, D_POOL), BF16), jax.ShapeDtypeStruct((N_POOL_GROUPS, HEAD, HEAD), F32),
                   jax.ShapeDtypeStruct((1, D_POOL), F32)],
        scratch_shapes=[pltpu.VMEM((T + POOL_HALO, HEAD), F32), pltpu.VMEM((T + POOL_HALO, HEAD), F32), pltpu.VMEM((T, HEAD), F32)],
        semantics=("parallel",), jobs=jobs)


def _conv_taps(xs, cw):
    v = cw[CONV_WIDTH - 1] * xs[SUBLANES:]
    for k in range(CONV_WIDTH - 1):
        v += cw[k] * _shift_rows(xs, CONV_WIDTH - 1 - k)[SUBLANES:]
    return v


def _tap_rows(cw_ref):
    return [cw_ref[k:k + 1, :] for k in range(CONV_WIDTH)]


def _softplus_neg(lam):
    return jnp.maximum(-lam, 0.0) + _log1p(jnp.exp(-jnp.abs(lam)))


def _lru_gates(v, wa, ba, wx, bx, sp):
    vb = v.astype(BF16)
    ra = _sigmoid(_dot(vb, wa) + ba)
    ix = _sigmoid(_dot(vb, wx) + bx)
    log_a = -LRU_C * ra * sp
    a = jnp.exp(log_a)
    sq = jnp.sqrt(-jnp.tanh(log_a) * (a * a + 1.0))
    return ra, ix, a, sq


def _row_bcast(v, r):
    return jnp.broadcast_to(v[r:r + 1, :], v.shape)


TILE_BLOCK = 128


def _scan_in_tiles(coef, coef_shift, A_out, B, T, direction):
    order = list(range(SUBLANES)) if direction == 1 else list(range(SUBLANES - 1, -1, -1))
    tiles = min(TILE_BLOCK, T // SUBLANES)
    for base in range(0, T, tiles * SUBLANES):
        def rows(r, base=base):
            return pl.ds(base + r, tiles, stride=SUBLANES)

        A, Bv = coef[rows(order[0] + coef_shift), :], B[rows(order[0]), :]
        A_out[rows(order[0]), :] = A
        for r in order[1:]:
            a = coef[rows(r + coef_shift), :]
            Bv = a * Bv + B[rows(r), :]
            A = a * A
            A_out[rows(r), :] = A
            B[rows(r), :] = Bv


TILES_PER_STEP = 8


def _carry_tiles(A_s, B_s, out, ntile, direction):
    out_row = SUBLANES - 1 if direction == 1 else 0

    def step(k, carry):
        for j in range(TILES_PER_STEP):
            t = k * TILES_PER_STEP + j
            r0 = pl.multiple_of((t if direction == 1 else ntile - 1 - t) * SUBLANES, SUBLANES)
            A, B = A_s[pl.ds(r0, SUBLANES), :], B_s[pl.ds(r0, SUBLANES), :]
            out[pl.ds(r0, SUBLANES), :] = A * carry + B
            carry = _row_bcast(A, out_row) * carry + _row_bcast(B, out_row)
        return carry

    lax.fori_loop(0, ntile // TILES_PER_STEP, step, jnp.zeros((SUBLANES, HEAD), F32))


def _rnn_fwd(proj, conv_w, conv_b, w_a, b_a, w_x, b_x, lam, jobs=()):
    T = proj.shape[0]
    nchunk = T // CHUNK
    ntile = T // SUBLANES

    def body(u_ref, ug_ref, cw_ref, cb_ref, wa_ref, ba_ref, wx_ref, bx_ref, lam_ref,
             h_ref, z_ref, v_ref, ra_ref, ix_ref, a_ref, sq_ref, upad, a_s, b_s):
        _pad_front(upad, u_ref, SUBLANES)
        cw, cb = _tap_rows(cw_ref), cb_ref[...]
        wa, wx = wa_ref[...].astype(BF16), wx_ref[...].astype(BF16)
        ba, bx = ba_ref[...], bx_ref[...]
        sp = _softplus_neg(lam_ref[...])

        def chunk(i, carry):
            rows = pl.ds(pl.multiple_of(i * CHUNK, CHUNK), CHUNK)
            v = _conv_taps(upad[pl.ds(pl.multiple_of(i * CHUNK, CHUNK), CHUNK + SUBLANES), :], cw) + cb
            ra, ix, a, sq = _lru_gates(v, wa, ba, wx, bx, sp)
            v_ref[rows, :], ra_ref[rows, :], ix_ref[rows, :], a_ref[rows, :], sq_ref[rows, :] = v, ra, ix, a, sq
            a_s[rows, :], b_s[rows, :] = a, sq * ix * v
            return carry

        lax.fori_loop(0, nchunk, chunk, 0)
        _scan_in_tiles(a_s, 0, a_s, b_s, T, 1)
        _carry_tiles(a_s, b_s, h_ref, ntile, 1)

        def chunk3(i, carry):
            r0 = pl.multiple_of(i * CHUNK, CHUNK)
            gl, _ = _gelu_parts(ug_ref[pl.ds(r0, CHUNK), :])
            z_ref[pl.ds(r0, CHUNK), :] = (h_ref[pl.ds(r0, CHUNK), :] * gl).astype(BF16)
            return carry

        lax.fori_loop(0, nchunk, chunk3, 0)

    col = pl.BlockSpec((T, HEAD), lambda h: (0, h))
    vec = pl.BlockSpec((1, HEAD), lambda h: (0, h))
    mat = pl.BlockSpec((None, HEAD, HEAD), lambda h: (h, 0, 0))
    return _pallas(
        body, (proj, proj, conv_w, conv_b, w_a, b_a, w_x, b_x, lam), name="rnn_fwd", grid=(N_RNN_HEADS,),
        in_specs=[pl.BlockSpec((T, HEAD), lambda h: (0, COL_RNN + h)), pl.BlockSpec((T, HEAD), lambda h: (0, COL_GATE + h)),
                  pl.BlockSpec((CONV_WIDTH, HEAD), lambda h: (0, h)), vec, mat, vec, mat, vec, vec],
        out_specs=[col] * 7,
        out_shape=[jax.ShapeDtypeStruct((T, D_RNN), F32), jax.ShapeDtypeStruct((T, D_RNN), BF16)]
        + [jax.ShapeDtypeStruct((T, D_RNN), F32)] * 5,
        scratch_shapes=[pltpu.VMEM((T + SUBLANES, HEAD), F32), pltpu.VMEM((T, HEAD), F32), pltpu.VMEM((T, HEAD), F32)],
        semantics=("parallel",), jobs=jobs)


def _rnn_bwd(proj, hr, dz, gates, conv_w, w_a, w_x, lam, jobs=()):
    T = proj.shape[0]
    nchunk = T // CHUNK
    ntile = T // SUBLANES

    def body(u_ref, ug_ref, h_ref, dz_ref, v_ref, ra_ref, ix_ref, a_ref, sq_ref, cw_ref, wa_ref, wx_ref, lam_ref,
             du_ref, dug_ref, dwa_ref, dwx_ref, dba_ref, dbx_ref, dlam_ref, dcb_ref, dcw_ref,
             upad, hpad, apad, g_s, dvpad, ga_s):
        zero_tile = jnp.zeros((SUBLANES, HEAD), F32)
        _pad_front(upad, u_ref, SUBLANES)
        _pad_front(hpad, h_ref, SUBLANES)
        apad[pl.ds(T, SUBLANES), :] = zero_tile
        dvpad[pl.ds(T, SUBLANES), :] = zero_tile
        for ref in (dwa_ref, dwx_ref, dba_ref, dbx_ref, dlam_ref, dcb_ref, dcw_ref):
            ref[...] = jnp.zeros_like(ref)
        cw = _tap_rows(cw_ref)
        wa, wx = wa_ref[...].astype(BF16), wx_ref[...].astype(BF16)
        lam_row = lam_ref[...]
        sp = _softplus_neg(lam_row)

        def chunk(i, carry):
            rows = pl.ds(pl.multiple_of(i * CHUNK, CHUNK), CHUNK)
            apad[rows, :] = a_ref[rows, :]
            gl, dgl = _gelu_parts(ug_ref[rows, :])
            d = dz_ref[rows, :]
            g_s[rows, :] = d * gl
            dug_ref[rows, :] = (d * h_ref[rows, :] * dgl).astype(BF16)
            return carry

        lax.fori_loop(0, nchunk, chunk, 0)

        _scan_in_tiles(apad, 1, ga_s, g_s, T, -1)
        _carry_tiles(ga_s, g_s, g_s, ntile, -1)

        def chunk3(i, carry):
            r0 = pl.multiple_of(i * CHUNK, CHUNK)
            rows = pl.ds(r0, CHUNK)
            g = g_s[rows, :]
            h_prev = _shift_rows(hpad[pl.ds(r0, CHUNK + SUBLANES), :], 1)[SUBLANES:]
            v, ra, ix, sq, a = v_ref[rows, :], ra_ref[rows, :], ix_ref[rows, :], sq_ref[rows, :], a_ref[rows, :]
            d_sq = g * ix * v
            d_ix = g * sq * v
            d_la = a * g * h_prev - d_sq * a * a / sq
            dlam_ref[...] += jnp.sum(d_la * ra, axis=0, keepdims=True)
            d_pa = d_la * (-LRU_C) * sp * ra * (1.0 - ra)
            d_px = d_ix * ix * (1.0 - ix)
            vb, d_pab, d_pxb = v.astype(BF16), d_pa.astype(BF16), d_px.astype(BF16)
            dwa_ref[...] += _dot_tn(vb, d_pab)
            dwx_ref[...] += _dot_tn(vb, d_pxb)
            dba_ref[...] += jnp.sum(d_pa, axis=0, keepdims=True)
            dbx_ref[...] += jnp.sum(d_px, axis=0, keepdims=True)
            dv = g * sq * ix + _dot_nt(d_pab, wa) + _dot_nt(d_pxb, wx)
            dvpad[rows, :] = dv
            dcb_ref[...] += jnp.sum(dv, axis=0, keepdims=True)
            xs = upad[pl.ds(r0, CHUNK + SUBLANES), :]
            for k in range(CONV_WIDTH):
                u_k = _shift_rows(xs, CONV_WIDTH - 1 - k)[SUBLANES:] if k < CONV_WIDTH - 1 else xs[SUBLANES:]
                dcw_ref[k:k + 1, :] += jnp.sum(dv * u_k, axis=0, keepdims=True)
            return carry

        lax.fori_loop(0, nchunk, chunk3, 0)
        dlam_ref[...] = dlam_ref[...] * (LRU_C * _sigmoid(-lam_row))

        def chunk4(i, carry):
            r0 = pl.multiple_of(i * CHUNK, CHUNK)
            dvs = dvpad[pl.ds(r0, CHUNK + SUBLANES), :]
            du = cw[CONV_WIDTH - 1] * dvs[:CHUNK]
            for k in range(CONV_WIDTH - 1):
                du += cw[k] * _shift_rows(dvs, -(CONV_WIDTH - 1 - k))[:CHUNK]
            du_ref[pl.ds(r0, CHUNK), :] = du.astype(BF16)
            return carry

        lax.fori_loop(0, nchunk, chunk4, 0)

    col = pl.BlockSpec((T, HEAD), lambda h: (0, h))
    vec = pl.BlockSpec((1, HEAD), lambda h: (0, h))
    mat = pl.BlockSpec((None, HEAD, HEAD), lambda h: (h, 0, 0))
    taps = pl.BlockSpec((CONV_WIDTH, HEAD), lambda h: (0, h))
    vec_out = jax.ShapeDtypeStruct((1, D_RNN), F32)
    mat_out = jax.ShapeDtypeStruct((N_RNN_HEADS, HEAD, HEAD), F32)
    seq = pltpu.VMEM((T, HEAD), F32)
    seq_pad = pltpu.VMEM((T + SUBLANES, HEAD), F32)
    return _pallas(
        body, (proj, proj, hr, dz, *gates, conv_w, w_a, w_x, lam), name="rnn_bwd", grid=(N_RNN_HEADS,),
        in_specs=[pl.BlockSpec((T, HEAD), lambda h: (0, COL_RNN + h)), pl.BlockSpec((T, HEAD), lambda h: (0, COL_GATE + h))]
        + [col] * 7 + [taps, mat, mat, vec],
        out_specs=[col, col, mat, mat, vec, vec, vec, vec, taps],
        out_shape=[jax.ShapeDtypeStruct((T, D_RNN), BF16), jax.ShapeDtypeStruct((T, D_RNN), BF16), mat_out, mat_out,
                   vec_out, vec_out, vec_out, vec_out, jax.ShapeDtypeStruct((CONV_WIDTH, D_RNN), F32)],
        scratch_shapes=[seq_pad, seq_pad, seq_pad, seq, seq_pad, seq],
        semantics=("parallel",), jobs=jobs)


GROUP_FFN_OUT = ["w_ffn_out"]
GROUP_FFN_IN = ["w_ffn_in"]
GROUP_MIX = ["w_o", "w_pool_out", "w_rnn_out"]
GROUP_IN = ["w_in"]


def _step(x, target, s, full, conv_w, place):
    T = x.shape[0]
    tall, mid, low = min(T, 2048), min(T, 1024), min(T, 512)
    full = dict(full)

    def gathered(names, results):
        full.update(zip(names, results))

    early = ["w_pool_out", "w_rnn_out", "w_o", "w_ffn_out"]
    (proj, h1), (res,) = _norm_matmul(x, s["norm_mix"], full["w_in"], tm=tall, tn=512, name="in_proj", jobs=[_gather_job(full, early)])
    gathered(early, res)
    pm = _pool_fwd(proj, s["w_pool_grp"], s["pool_scale"])
    (hr, z, *gates), (res,) = _rnn_fwd(proj, conv_w, s["conv_b"], s["w_rg_a"], s["b_rg_a"], s["w_rg_x"], s["b_rg_x"],
                                       s["lru_lambda"], jobs=[_gather_job(full, ["w_ffn_in"])])
    gathered(["w_ffn_in"], res)
    *mix_by, mix = _branch_mix(pm, z, full["w_pool_out"], full["w_rnn_out"], proj, tm=tall, tn=256)
    x2 = _out_proj_residual(mix, full["w_o"], x, tm=mid)
    (act_by_up, act_by_gate, act, h2), _ = _ffn_in(x2, s["norm_ffn"], full["w_ffn_in"], tm=tall, tn=256)
    dx3, dx3b, sq_cols, g_norm_final = _ffn_out_loss(act, full["w_ffn_out"], x2, s["norm_final"], target, tm=low)

    g = {"norm_final": g_norm_final}

    def chip_sums(names, from_sibling):
        sums = {name: _chip_sum(name, g[name], got, place) for name, got in zip(names, from_sibling)}
        return {name: v[0] for name, v in sums.items()}, {name: v[1] for name, v in sums.items()}

    def final_sums(names, sums, from_chips):
        return {name: _final_sum(name, sums[name], got, place) for name, got in zip(names, from_chips)}

    dgate, dup = _ffn_out_bwd(dx3b, full["w_ffn_out"], act_by_gate, act_by_up, tm=tall, tn=256)
    g["w_ffn_out"] = _weight_grad(act, [dx3b], tm=256, tn=D_MODEL, name="w_ffn_out_grad")
    (dx2, dx2b, g["norm_ffn"]), (res,) = _ffn_in_bwd(dgate, dup, full["w_ffn_in"], dx3, x2, s["norm_ffn"], tm=low,
                                                     jobs=[_sibling_job(g, GROUP_FFN_OUT)])
    sums_ffn, sums_ffn_bf16 = chip_sums(GROUP_FFN_OUT, res)
    g["w_ffn_in"], (res,) = _weight_grad(h2, [dgate, dup], tm=D_MODEL, tn=256, name="w_ffn_in_grad",
                                         jobs=[_chips_job(sums_ffn_bf16, GROUP_FFN_OUT)])
    shards_ffn = final_sums(GROUP_FFN_OUT, sums_ffn, res)
    (dgp, dgr, dyp, dyr), (res,) = _out_proj_bwd(dx2b, full["w_o"], mix_by, tm=tall, tn=256,
                                                 jobs=[_sibling_job(g, GROUP_FFN_IN)])
    sums_ffn, sums_ffn_bf16 = chip_sums(GROUP_FFN_IN, res)
    g["w_o"] = _weight_grad(mix, [dx2b], tm=D_MODEL, tn=256, name="w_o_grad")
    dpm, dz = _branch_bwd(dyp, dyr, full["w_pool_out"], full["w_rnn_out"], tm=mid)
    g["w_pool_out"] = _weight_grad(pm, [dyp], tm=D_POOL, tn=256, name="w_pool_out_grad")
    g["w_rnn_out"] = _weight_grad(z, [dyr], tm=D_RNN, tn=256, name="w_rnn_out_grad")
    (dupool, g["w_pool_grp"], g["pool_scale"]), (res,) = _pool_bwd(proj, dpm, s["w_pool_grp"], s["pool_scale"],
                                                                   jobs=[_sibling_job(g, GROUP_MIX)])
    sums_mix, sums_mix_bf16 = chip_sums(GROUP_MIX, res)
    ((durnn, dugate, g["w_rg_a"], g["w_rg_x"], g["b_rg_a"], g["b_rg_x"], g["lru_lambda"], g["conv_b"], g["conv_w"]),
     (res,)) = _rnn_bwd(proj, hr, dz, gates, conv_w, s["w_rg_a"], s["w_rg_x"], s["lru_lambda"],
                        jobs=[_chips_job(sums_ffn_bf16, GROUP_FFN_IN)])
    shards_ffn.update(final_sums(GROUP_FFN_IN, sums_ffn, res))
    segs = [dupool, durnn, dugate, dgp, dgr]
    ffn = GROUP_FFN_OUT + GROUP_FFN_IN
    g["w_in"], (res, joined) = _weight_grad(h1, segs, tm=D_MODEL, tn=256, name="w_in_grad",
                                           jobs=[_chips_job(sums_mix_bf16, GROUP_MIX), _join_job(shards_ffn, ffn)])
    grads = dict(zip(ffn, joined))
    shards = final_sums(GROUP_MIX, sums_mix, res)
    (res,) = _run_jobs([_sibling_job(g, GROUP_IN)], "w_in_exchange_sibling")
    sums_in, sums_in_bf16 = chip_sums(GROUP_IN, res)
    (grad_x, g["norm_mix"]), (res,) = _in_proj_bwd(segs, full["w_in"], dx2, x, s["norm_mix"], tm=low,
                                                  jobs=[_chips_job(sums_in_bf16, GROUP_IN)])
    shards.update(final_sums(GROUP_IN, sums_in, res))

    vec_rows = [g[name] if name != "pool_scale" else jnp.pad(g[name], ((0, 0), (0, D_MODEL - D_POOL))) for name in VEC_ITEMS]
    vec_rows += [g["conv_w"], sq_cols, jnp.zeros((VEC_ROWS - len(VEC_ITEMS) - CONV_WIDTH - 1, D_MODEL), F32)]
    vec = jnp.concatenate(vec_rows, axis=0).reshape(VEC_ROWS, N_DEV, HEAD).transpose(1, 0, 2)
    mat = jnp.concatenate([g[name].reshape(-1, HEAD) for name in MAT_ITEMS], axis=0).reshape(N_DEV, -1, HEAD)
    (vec, mat), (joined,) = _all_reduce_small([vec, mat], jobs=[_join_job(shards, GROUP_MIX + GROUP_IN)])
    grads.update(zip(GROUP_MIX + GROUP_IN, joined))

    vec = vec.transpose(1, 0, 2).reshape(VEC_ROWS, D_MODEL)
    mat = mat.reshape(-1, HEAD)
    for k, name in enumerate(VEC_ITEMS):
        grads[name] = vec[k:k + 1, :s[name].shape[1]]
    grads["conv_w"] = vec[len(VEC_ITEMS):len(VEC_ITEMS) + CONV_WIDTH]
    row = 0
    for name in MAT_ITEMS:
        rows = s[name].shape[0] * HEAD
        grads[name] = mat[row:row + rows]
        row += rows
    return vec[len(VEC_ITEMS) + CONV_WIDTH], grad_x, grads


LARGE = {"w_in": "col", "w_pool_out": "col", "w_rnn_out": "row", "w_o": "row", "w_ffn_in": "col", "w_ffn_out": "row"}
LARGE_SHAPE = {"w_in": (D_MODEL, D_IN), "w_pool_out": (D_POOL, D_MODEL), "w_rnn_out": (D_RNN, D_MODEL),
               "w_o": (D_MODEL, D_MODEL), "w_ffn_in": (D_MODEL, 2 * D_FF), "w_ffn_out": (D_FF, D_MODEL)}


def _place():
    x, y, c = lax.axis_index("x"), lax.axis_index("y"), lax.axis_index("c")
    return 2 * x + y, c


def _chip_device(chip, c):
    return (chip // 2, chip % 2, c)


def _chip_window(ref, kind, shape, chip, half=None):
    K, N = shape
    if kind == "col":
        rows = slice(None) if half is None else pl.ds(half * (K // 2), K // 2)
        return ref.at[rows, pl.ds(chip * (N // N_CHIPS), N // N_CHIPS)]
    ks = K // N_CHIPS
    if half is None:
        return ref.at[pl.ds(chip * ks, ks), :]
    return ref.at[pl.ds(chip * ks + half * (ks // 2), ks // 2), :]


def _row_half(ref, half):
    rows = ref.shape[0] // 2
    return ref.at[pl.ds(half * rows, rows), :]


def _remote(win_src, win_dst, send_sems, recv_sems, idx, to):
    return pltpu.make_async_remote_copy(src_ref=win_src, dst_ref=win_dst, send_sem=send_sems.at[idx], recv_sem=recv_sems.at[idx],
                                        device_id=to, device_id_type=MESH)


def _gather_job(full, names, conv_w_full=None):
    n = len(names)
    cw_cols = D_RNN // N_CHIPS

    def windows(refs, chip, half):
        return [_chip_window(refs[k], LARGE[name], LARGE_SHAPE[name], chip, half) for k, name in enumerate(names)]

    def ici_copies(refs, send_sems, recv_sems, src_chip, dst_chip, c, r):
        wins = windows(refs, src_chip, c)
        if conv_w_full is not None:
            wins.append(refs[n].at[:, pl.ds(src_chip * cw_cols, cw_cols)])
        return [_remote(win, win, send_sems, recv_sems, (k, r), _chip_device(dst_chip, c)) for k, win in enumerate(wins)]

    def forwards(refs, send_sems, recv_sems, src_chip, half, to_core, chip, r):
        return [_remote(win, win, send_sems, recv_sems, (k, 3 + r), _chip_device(chip, to_core))
                for k, win in enumerate(windows(refs, src_chip, half))]

    def start(ins, outs, send_sems, recv_sems):
        chip, c = _place()
        for r in range(3):
            for cp in ici_copies(outs, send_sems, recv_sems, chip, chip ^ (r + 1), c, r):
                cp.start()

    def finish(ins, outs, send_sems, recv_sems):
        chip, c = _place()
        for r in range(3):
            for cp in ici_copies(outs, send_sems, recv_sems, chip ^ (r + 1), chip, c, r):
                cp.wait_recv()
            for cp in forwards(outs, send_sems, recv_sems, chip ^ (r + 1), c, 1 - c, chip, r):
                cp.start()
        for r in range(3):
            for cp in forwards(outs, send_sems, recv_sems, chip ^ (r + 1), 1 - c, c, chip, r):
                cp.wait_recv()
            for cp in ici_copies(outs, send_sems, recv_sems, chip, chip ^ (r + 1), c, r):
                cp.wait_send()
            for cp in forwards(outs, send_sems, recv_sems, chip ^ (r + 1), c, 1 - c, chip, r):
                cp.wait_send()

    arrays = [full[name] for name in names] + ([conv_w_full] if conv_w_full is not None else [])
    return _Job(arrays, [jax.ShapeDtypeStruct(a.shape, a.dtype) for a in arrays], {k: k for k in range(len(arrays))},
                (len(arrays), 6), start, finish)


def _core_halves(ref, kind, shape, c):
    return [_chip_window(ref, kind, shape, chip, c) for chip in range(N_CHIPS)]


def _sibling_job(grads, names):
    def start(ins, outs, send_sems, recv_sems):
        chip, c = _place()
        for k, name in enumerate(names):
            kind, shape = LARGE[name], LARGE_SHAPE[name]
            if kind == "col":
                pairs = [(_row_half(ins[k], 1 - c), outs[k])]
            else:
                rows = shape[0] // N_DEV
                pairs = [(win, outs[k].at[pl.ds(j * rows, rows), :]) for j, win in enumerate(_core_halves(ins[k], kind, shape, 1 - c))]
            for src, dst in pairs:
                _remote(src, dst, send_sems, recv_sems, k, _chip_device(chip, 1 - c)).start()

    def finish(ins, outs, send_sems, recv_sems):
        chip, c = _place()
        for k in range(len(names)):
            _remote(outs[k], outs[k], send_sems, recv_sems, k, _chip_device(chip, 1 - c)).wait()

    return _Job([grads[name] for name in names],
                [jax.ShapeDtypeStruct((LARGE_SHAPE[name][0] // 2, LARGE_SHAPE[name][1]), F32) for name in names], {},
                (len(names),), start, finish)


def _chip_sum(name, g, got, place):
    kind, (K, N) = LARGE[name], LARGE_SHAPE[name]
    rows = K // N_DEV
    piece_cols = N // N_CHIPS

    def body(place_ref, g_ref, got_ref, o_ref, ob_ref):
        total = g_ref[...] + got_ref[...]
        ob_ref[...] = total.astype(BF16)
        if kind == "col":
            for chip in range(N_CHIPS):
                @pl.when(place_ref[0] == chip)
                def _(chip=chip):
                    o_ref[...] = total[:, chip * piece_cols:(chip + 1) * piece_cols]
        else:
            @pl.when(pl.program_id(0) == place_ref[0])
            def _():
                o_ref[...] = total

    if kind == "col":
        mine = pl.BlockSpec((rows, N), lambda j, place_ref: (j + N_CHIPS * place_ref[1], 0))
        own = pl.BlockSpec((rows, piece_cols), lambda j, place_ref: (j, 0))
    else:
        mine = pl.BlockSpec((rows, N), lambda j, place_ref: (2 * j + place_ref[1], 0))
        own = pl.BlockSpec((rows, N), lambda j, place_ref: (0, 0))
    blk = pl.BlockSpec((rows, N), lambda j, place_ref: (j, 0))
    return pl.pallas_call(
        body, name=name + "_chip_sum",
        grid_spec=pltpu.PrefetchScalarGridSpec(num_scalar_prefetch=1, grid=(N_CHIPS,), in_specs=[mine, blk], out_specs=[own, blk]),
        out_shape=[jax.ShapeDtypeStruct(_piece_shape(name), F32), jax.ShapeDtypeStruct((K // 2, N), BF16)],
        compiler_params=_params(dimension_semantics=("arbitrary",)),
    )(place, g, got)


def _piece(ref, kind, shape, chip):
    K, N = shape
    if kind == "col":
        return ref.at[:, pl.ds(chip * (N // N_CHIPS), N // N_CHIPS)]
    return ref.at[pl.ds(chip * (K // N_DEV), K // N_DEV), :]


def _piece_shape(name):
    kind, (K, N) = LARGE[name], LARGE_SHAPE[name]
    return (K // 2, N // N_CHIPS) if kind == "col" else (K // N_DEV, N)


def _chips_job(sums, names):
    def copies(ins, outs, send_sems, recv_sems):
        chip, c = _place()
        return [_remote(_piece(ins[k], LARGE[name], LARGE_SHAPE[name], chip ^ (r + 1)), outs[k].at[r], send_sems, recv_sems, (k, r),
                        _chip_device(chip ^ (r + 1), c)) for k, name in enumerate(names) for r in range(3)]

    def start(*refs):
        for cp in copies(*refs):
            cp.start()

    def finish(*refs):
        for cp in copies(*refs):
            cp.wait()

    return _Job([sums[name] for name in names], [jax.ShapeDtypeStruct((3,) + _piece_shape(name), BF16) for name in names], {},
                (len(names), 3), start, finish)


def _final_sum(name, chip_sum, got, place):
    rows, cols = _piece_shape(name)

    def body(place_ref, s_ref, got_ref, o_ref):
        o_ref[...] = ((s_ref[...] + got_ref[0].astype(F32)) + got_ref[1].astype(F32)) + got_ref[2].astype(F32)

    mine = pl.BlockSpec((rows, cols), lambda i, place_ref: (0, 0))
    return pl.pallas_call(
        body, name=name + "_final_sum",
        grid_spec=pltpu.PrefetchScalarGridSpec(
            num_scalar_prefetch=1, grid=(1,), in_specs=[mine, pl.BlockSpec((3, rows, cols), lambda i, place_ref: (0, 0, 0))],
            out_specs=pl.BlockSpec((rows, cols), lambda i, place_ref: (place_ref[1], 0))),
        out_shape=jax.ShapeDtypeStruct((2 * rows, cols), F32),
        compiler_params=_params(dimension_semantics=("arbitrary",)),
    )(place, chip_sum, got)


def _join_job(shards, names):
    def half_copy(outs, send_sems, recv_sems, k, mine):
        chip, c = _place()
        win = _row_half(outs[k], c if mine else 1 - c)
        return _remote(win, win, send_sems, recv_sems, k, _chip_device(chip, 1 - c))

    def start(ins, outs, send_sems, recv_sems):
        for k in range(len(names)):
            half_copy(outs, send_sems, recv_sems, k, True).start()

    def finish(ins, outs, send_sems, recv_sems):
        for k in range(len(names)):
            half_copy(outs, send_sems, recv_sems, k, True).wait_send()
            half_copy(outs, send_sems, recv_sems, k, False).wait_recv()

    arrays = [shards[name] for name in names]
    return _Job(arrays, [jax.ShapeDtypeStruct(a.shape, F32) for a in arrays], {k: k for k in range(len(arrays))},
                (len(arrays),), start, finish)


VEC_ROWS = 16


def _all_reduce_small(slabs, jobs=()):
    n = len(slabs)

    def body(*refs):
        in_refs, out_refs, got_refs = refs[:n], refs[n:2 * n], refs[2 * n:3 * n]
        send_sems, recv_sems = refs[3 * n:]
        x, y, c = lax.axis_index("x"), lax.axis_index("y"), lax.axis_index("c")
        me = 4 * x + 2 * y + c

        def remote(src, dst, k, phase, r):
            other = me ^ r
            return pltpu.make_async_remote_copy(src_ref=src, dst_ref=dst, send_sem=send_sems.at[k, phase, r],
                                                recv_sem=recv_sems.at[k, phase, r],
                                                device_id=(other // 4, (other // 2) % 2, other % 2), device_id_type=MESH)

        scatter = [remote(in_refs[k].at[me ^ r], got_refs[k].at[r], k, 0, r) for r in range(1, N_DEV) for k in range(n)]
        for cp in scatter:
            cp.start()
        for cp in scatter:
            cp.wait()
        for k in range(n):
            total = in_refs[k][me]
            for r in range(1, N_DEV):
                total = total + got_refs[k][r]
            out_refs[k][me] = total
        gather = [remote(out_refs[k].at[me], out_refs[k].at[me], k, 1, r) for r in range(1, N_DEV) for k in range(n)]
        for cp in gather:
            cp.start()
        for r in range(1, N_DEV):
            for k in range(n):
                remote(out_refs[k].at[me ^ r], out_refs[k].at[me ^ r], k, 1, r).wait_recv()
        for cp in gather:
            cp.wait_send()

    return _pallas(
        body, slabs, name="all_reduce_small", grid=(), in_specs=[VMEM] * n, out_specs=[VMEM] * n,
        out_shape=[jax.ShapeDtypeStruct(s.shape, F32) for s in slabs],
        scratch_shapes=[pltpu.VMEM(s.shape, F32) for s in slabs]
        + [pltpu.SemaphoreType.DMA((n, 2, N_DEV)), pltpu.SemaphoreType.DMA((n, 2, N_DEV))], jobs=jobs)


def _cast_into_whole(w, name, place):
    rows, cols = w.shape
    tr = rows // 2

    def body(place_ref, w_ref, o_ref):
        o_ref[...] = w_ref[...].astype(BF16)

    if LARGE[name] == "col":
        window = pl.BlockSpec((tr, cols), lambda i, place_ref: (i, place_ref[0]))
    else:
        window = pl.BlockSpec((tr, cols), lambda i, place_ref: (2 * place_ref[0] + i, 0))
    return pl.pallas_call(
        body, name=name + "_cast",
        grid_spec=pltpu.PrefetchScalarGridSpec(num_scalar_prefetch=1, grid=(2,),
                                               in_specs=[pl.BlockSpec((tr, cols), lambda i, place_ref: (i, 0))], out_specs=window),
        out_shape=jax.ShapeDtypeStruct(LARGE_SHAPE[name], BF16),
        compiler_params=_params(dimension_semantics=("parallel",)))(place, w)


def _cast_many_into_whole(shards, place, jobs):
    names = list(shards)
    n = len(names)

    def body(place_ref, *refs):
        for w_ref, o_ref in zip(refs[:n], refs[n:]):
            o_ref[...] = w_ref[...].astype(BF16)

    def window(name):
        rows, cols = shards[name].shape
        if LARGE[name] == "col":
            return pl.BlockSpec((rows // 2, cols), lambda i, place_ref: (i, place_ref[0]))
        return pl.BlockSpec((rows // 2, cols), lambda i, place_ref: (2 * place_ref[0] + i, 0))

    def half(name):
        rows, cols = shards[name].shape
        return pl.BlockSpec((rows // 2, cols), lambda i, place_ref: (i, 0))

    return _pallas(body, [shards[name] for name in names], name="cast_weights", grid=(2,),
                   in_specs=[half(name) for name in names], out_specs=[window(name) for name in names],
                   out_shape=[jax.ShapeDtypeStruct(LARGE_SHAPE[name], BF16) for name in names],
                   semantics=("arbitrary",), jobs=jobs, prefetch=place)


def _adamw_math(w, g, m, v):
    m = ADAM_B1 * m + (1.0 - ADAM_B1) * g
    v = ADAM_B2 * v + (1.0 - ADAM_B2) * (g * g)
    m_hat = m / (1.0 - ADAM_B1 ** ADAM_STEP)
    v_hat = v / (1.0 - ADAM_B2 ** ADAM_STEP)
    delta = -ADAM_LR * (m_hat / (jnp.sqrt(v_hat) + ADAM_EPS) + ADAM_WD * w)
    return delta, m, v


def _adamw_large(w, g, m, v, name):
    rows, cols = w.shape
    steps = 2
    tr = rows // steps

    def body(w_ref, g_ref, m_ref, v_ref, d_ref, mo_ref, vo_ref):
        d_ref[...], mo_ref[...], vo_ref[...] = _adamw_math(w_ref[...], g_ref[...], m_ref[...], v_ref[...])

    blk = pl.BlockSpec((tr, cols), lambda i: (i, 0))
    out = jax.ShapeDtypeStruct(w.shape, F32)
    return pl.pallas_call(body, name=name + "_adamw", grid=(steps,), in_specs=[blk] * 4, out_specs=[blk] * 3, out_shape=[out] * 3,
                          compiler_params=_params(dimension_semantics=("parallel",)))(w, g, m, v)


def _adamw_small(ws, gs, ms, vs):
    n = len(ws)

    def body(*refs):
        for k in range(n):
            w_ref, g_ref, m_ref, v_ref = (refs[q * n + k] for q in range(4))
            d_ref, mo_ref, vo_ref = (refs[(4 + q) * n + k] for q in range(3))
            d_ref[...], mo_ref[...], vo_ref[...] = _adamw_math(w_ref[...], g_ref[...], m_ref[...], v_ref[...])

    out = [jax.ShapeDtypeStruct(w.shape, F32) for w in ws]
    res = pl.pallas_call(body, name="small_adamw", in_specs=[VMEM] * (4 * n), out_specs=[VMEM] * (3 * n), out_shape=out * 3,
                         compiler_params=_params())(*ws, *gs, *ms, *vs)
    return res[:n], res[n:2 * n], res[2 * n:]


WEIGHTS = ["norm_mix", "w_in", "w_pool_grp", "pool_scale", "w_pool_out", "conv_w", "conv_b", "w_rg_a", "b_rg_a", "w_rg_x",
           "b_rg_x", "lru_lambda", "w_rnn_out", "w_o", "norm_ffn", "w_ffn_in", "w_ffn_out", "norm_final"]
VEC_ITEMS = ["norm_mix", "norm_ffn", "norm_final", "pool_scale", "conv_b", "lru_lambda", "b_rg_a", "b_rg_x"]
MAT_ITEMS = ["w_pool_grp", "w_rg_a", "w_rg_x"]


def _as2d(name, a):
    if name in MAT_ITEMS:
        return a.reshape(-1, HEAD, HEAD)
    if name == "conv_w":
        return a.reshape(CONV_WIDTH, -1)
    return a.reshape(1, -1)


def kernel(x, norm_mix, w_in, w_pool_grp, pool_scale, w_pool_out, conv_w, conv_b, w_rg_a, b_rg_a, w_rg_x, b_rg_x, lru_lambda, w_rnn_out, w_o, norm_ffn, w_ffn_in, w_ffn_out, norm_final, loss_target, m_norm_mix, m_w_in, m_w_pool_grp, m_pool_scale, m_w_pool_out, m_conv_w, m_conv_b, m_w_rg_a, m_b_rg_a, m_w_rg_x, m_b_rg_x, m_lru_lambda, m_w_rnn_out, m_w_o, m_norm_ffn, m_w_ffn_in, m_w_ffn_out, m_norm_final, v_norm_mix, v_w_in, v_w_pool_grp, v_pool_scale, v_w_pool_out, v_conv_w, v_conv_b, v_w_rg_a, v_b_rg_a, v_w_rg_x, v_b_rg_x, v_lru_lambda, v_w_rnn_out, v_w_o, v_norm_ffn, v_w_ffn_in, v_w_ffn_out, v_norm_final):
    given = dict(locals())
    w = {name: given[name] for name in WEIGHTS}
    m = {name: given["m_" + name] for name in WEIGHTS}
    v = {name: given["v_" + name] for name in WEIGHTS}
    chip, c = _place()

    place = jnp.stack([chip, c]).astype(jnp.int32)
    conv_cols = w["conv_w"].shape[-1]
    conv_w_mine = lax.dynamic_update_slice_in_dim(jnp.zeros((CONV_WIDTH, D_RNN), F32), w["conv_w"][0], chip * conv_cols, axis=1)
    w_in_mine = _cast_into_whole(w["w_in"][0], "w_in", place)
    later = [name for name in LARGE if name != "w_in"]
    casts, ((w_in_full, conv_w_full),) = _cast_many_into_whole(
        {name: w[name][0] for name in later}, place, jobs=[_gather_job({"w_in": w_in_mine}, ["w_in"], conv_w_mine)])
    full = dict(zip(later, casts), w_in=w_in_full)
    small = {name: _as2d(name, w[name]) for name in WEIGHTS if name not in LARGE and name != "conv_w"}
    sq_cols, grad_x, grads = _step(x[0], loss_target[0], small, full, conv_w_full, place)
    loss = 0.5 / D_MODEL * jnp.sum(sq_cols)
    grads["conv_w"] = lax.dynamic_slice_in_dim(grads["conv_w"], chip * conv_cols, conv_cols, axis=1)

    delta, new_m, new_v = {}, {}, {}
    for name in LARGE:
        delta[name], new_m[name], new_v[name] = _adamw_large(w[name][0], grads[name], m[name][0], v[name][0], name)
    small_names = [name for name in WEIGHTS if name not in LARGE]
    flat = lambda d: [d[name].reshape(grads[name].shape) for name in small_names]
    ds, mo, vo = _adamw_small(flat(w), [grads[name] for name in small_names], flat(m), flat(v))
    for k, name in enumerate(small_names):
        delta[name], new_m[name], new_v[name] = ds[k], mo[k], vo[k]

    shaped = lambda d: [d[name].reshape(w[name].shape) for name in WEIGHTS]
    return (loss, grad_x[None], *shaped(grads), *shaped(delta), *shaped(new_m), *shaped(new_v))
```

```python
import functools
import math

import jax
import jax.numpy as jnp
from jax import lax
from jax.experimental import pallas as pl
from jax.experimental.pallas import tpu as pltpu

F32 = jnp.float32
BF16 = jnp.bfloat16

D_MODEL = 1024
D_POOL = 512
N_POOL_GROUPS = 4
D_RNN = 1024
N_RNN_HEADS = 8
HEAD = 128
CONV_WIDTH = 4
LRU_C = 8.0
D_FF = 2816
D_IN = D_POOL + 2 * D_RNN + 2 * D_MODEL
NORM_EPS = 1e-6
COL_RNN = D_POOL // HEAD
COL_GATE = (D_POOL + D_RNN) // HEAD

ADAM_LR = 0.001
ADAM_B1 = 0.9
ADAM_B2 = 0.999
ADAM_EPS = 1e-08
ADAM_WD = 0.01
ADAM_STEP = 10

N_CHIPS = 4
N_DEV = 8
MESH = pl.DeviceIdType.MESH
ANY = pl.BlockSpec(memory_space=pl.ANY)
VMEM = pl.BlockSpec(memory_space=pltpu.VMEM)
VMEM_LIMIT_BYTES = 60 * 1024 * 1024
SUBLANES = 8
POOL_HALO = 16
CHUNK = 1024

GELU_C = math.sqrt(2.0 / math.pi)
GELU_A = 0.044715


def _params(**kw):
    return pltpu.CompilerParams(vmem_limit_bytes=VMEM_LIMIT_BYTES, **kw)


def _sigmoid(x):
    return 0.5 * jnp.tanh(0.5 * x) + 0.5


def _log1p(y):
    u = 1.0 + y
    d = u - 1.0
    return jnp.where(d == 0.0, y, jnp.log(u) * (y / jnp.where(d == 0.0, 1.0, d)))


def _gelu_parts(x):
    x2 = x * x
    th = jnp.tanh(GELU_C * (x + GELU_A * x * x2))
    g = 0.5 * x * (1.0 + th)
    dg = 0.5 * (1.0 + th) + 0.5 * x * (1.0 - th * th) * GELU_C * (1.0 + 3.0 * GELU_A * x2)
    return g, dg


def _dot(a, b):
    return jnp.dot(a, b, preferred_element_type=F32)


def _dot_nt(a, b):
    return lax.dot_general(a, b, (((1,), (1,)), ((), ())), preferred_element_type=F32)


def _dot_tn(a, b):
    return lax.dot_general(a, b, (((0,), (0,)), ((), ())), preferred_element_type=F32)


def _rms_scale(xv):
    return lax.rsqrt(jnp.mean(xv * xv, axis=-1, keepdims=True) + NORM_EPS)


def _rms_bwd(dy, xv, g):
    r = _rms_scale(xv)
    xh = xv * r
    dyg = dy * g
    dx = r * (dyg - xh * jnp.mean(dyg * xh, axis=-1, keepdims=True))
    return dx, dy * xh


class _Job:
    def __init__(self, inputs, out_shapes, aliases, sem_shape, start, finish):
        self.inputs, self.out_shapes, self.aliases, self.sem_shape = list(inputs), list(out_shapes), dict(aliases), sem_shape
        self.start, self.finish = start, finish


def _pallas(body, operands, *, name, grid, in_specs, out_specs, out_shape, scratch_shapes=(), semantics=None, jobs=(),
            prefetch=None):
    n_in, n_out, n_scr = len(in_specs), len(out_specs), len(scratch_shapes)
    n_pre = 0 if prefetch is None else 1
    job_in = [a for job in jobs for a in job.inputs]
    job_out = [s for job in jobs for s in job.out_shapes]
    aliases, i0, o0 = {}, n_pre + n_in, n_out
    for job in jobs:
        aliases.update({i0 + i: o0 + o for i, o in job.aliases.items()})
        i0, o0 = i0 + len(job.inputs), o0 + len(job.out_shapes)

    def whole(*refs):
        pre, refs = refs[:n_pre], refs[n_pre:]
        ins, j_ins = refs[:n_in], refs[n_in:n_in + len(job_in)]
        outs = refs[n_in + len(job_in):][:n_out]
        j_outs = refs[n_in + len(job_in) + n_out:][:len(job_out)]
        rest = refs[n_in + len(job_in) + n_out + len(job_out):]
        scr, sems = rest[:n_scr], rest[n_scr:]

        def run(phase):
            i, o = 0, 0
            for k, job in enumerate(jobs):
                getattr(job, phase)(j_ins[i:i + len(job.inputs)], j_outs[o:o + len(job.out_shapes)], sems[2 * k], sems[2 * k + 1])
                i, o = i + len(job.inputs), o + len(job.out_shapes)

        def at(step_of, phase):
            if not jobs:
                return
            if not grid:
                run(phase)
                return
            cond = functools.reduce(jnp.logical_and, [pl.program_id(d) == step_of(d) for d in range(len(grid))])
            pl.when(cond)(functools.partial(run, phase))

        at(lambda d: 0, "start")
        body(*pre, *ins, *outs, *scr)
        at(lambda d: grid[d] - 1, "finish")

    layout = dict(grid=grid, in_specs=list(in_specs) + [ANY] * len(job_in), out_specs=list(out_specs) + [ANY] * len(job_out),
                  scratch_shapes=list(scratch_shapes) + [pltpu.SemaphoreType.DMA(job.sem_shape) for job in jobs for _ in range(2)])
    if prefetch is not None:
        layout = dict(grid_spec=pltpu.PrefetchScalarGridSpec(num_scalar_prefetch=1, **layout))
    res = pl.pallas_call(
        whole, name=name, out_shape=list(out_shape) + job_out, input_output_aliases=aliases,
        compiler_params=_params(dimension_semantics=semantics, has_side_effects=bool(jobs)), **layout,
    )(*([] if prefetch is None else [prefetch]), *operands, *job_in)
    per_job, o = [], n_out
    for job in jobs:
        per_job.append(res[o:o + len(job.out_shapes)])
        o += len(job.out_shapes)
    return res[:n_out], per_job


def _run_jobs(jobs, name):
    return _pallas(lambda: None, [], name=name, grid=(), in_specs=[], out_specs=[], out_shape=[], jobs=jobs)[1]


NORM_ROWS = 256
EPILOGUE_ROWS = 512


def _norm_rows(x_ref, g_ref, h_ref):
    g = g_ref[...]

    def rows(i, carry):
        r = pl.ds(pl.multiple_of(i * NORM_ROWS, NORM_ROWS), NORM_ROWS)
        xv = x_ref[r, :]
        h_ref[r, :] = (xv * _rms_scale(xv) * g).astype(BF16)
        return carry

    lax.fori_loop(0, x_ref.shape[0] // NORM_ROWS, rows, 0)


def _norm_matmul(x, g, w, *, tm, tn, name, jobs=()):
    T, K = x.shape
    N = w.shape[1]

    def body(x_ref, g_ref, w_ref, o_ref, h_ref):
        @pl.when(pl.program_id(1) == 0)
        def _():
            _norm_rows(x_ref, g_ref, h_ref)

        o_ref[...] = _dot(h_ref[...], w_ref[...])

    return _pallas(
        body, (x, g, w), name=name, grid=(T // tm, N // tn),
        in_specs=[pl.BlockSpec((tm, K), lambda i, j: (i, 0)), pl.BlockSpec((1, K), lambda i, j: (0, 0)),
                  pl.BlockSpec((K, tn), lambda i, j: (0, j))],
        out_specs=[pl.BlockSpec((tm, tn), lambda i, j: (i, j)), pl.BlockSpec((tm, K), lambda i, j: (i, 0))],
        out_shape=[jax.ShapeDtypeStruct((T, N), F32), jax.ShapeDtypeStruct((T, K), BF16)],
        semantics=("parallel", "arbitrary"), jobs=jobs)


def _ffn_in(x2, g, w, *, tm, tn, jobs=()):
    T, K = x2.shape
    nb = D_FF // tn

    def body(x_ref, g_ref, wg_ref, wu_ref, dup_ref, dgate_ref, act_ref, h_ref):
        @pl.when(pl.program_id(1) == 0)
        def _():
            _norm_rows(x_ref, g_ref, h_ref)

        wg, wu = wg_ref[...], wu_ref[...]
        for r in range(0, tm, EPILOGUE_ROWS):
            rows = pl.ds(r, min(EPILOGUE_ROWS, tm))
            h = h_ref[rows, :]
            gate, up = _dot(h, wg), _dot(h, wu)
            s = _sigmoid(gate)
            silu = gate * s
            dup_ref[rows, :] = silu.astype(BF16)
            dgate_ref[rows, :] = (up * (s + silu * (1.0 - s))).astype(BF16)
            act_ref[rows, :] = (silu * up).astype(BF16)

    blk = pl.BlockSpec((tm, tn), lambda i, j: (i, j))
    return _pallas(
        body, (x2, g, w, w), name="ffn_in", grid=(T // tm, nb),
        in_specs=[pl.BlockSpec((tm, K), lambda i, j: (i, 0)), pl.BlockSpec((1, K), lambda i, j: (0, 0)),
                  pl.BlockSpec((K, tn), lambda i, j: (0, j)), pl.BlockSpec((K, tn), lambda i, j: (0, j + nb))],
        out_specs=[blk, blk, blk, pl.BlockSpec((tm, K), lambda i, j: (i, 0))],
        out_shape=[jax.ShapeDtypeStruct((T, D_FF), BF16), jax.ShapeDtypeStruct((T, D_FF), BF16),
                   jax.ShapeDtypeStruct((T, D_FF), BF16), jax.ShapeDtypeStruct((T, K), BF16)],
        semantics=("parallel", "arbitrary"), jobs=jobs)


def _branch_mix(pm, z, w_pool_out, w_rnn_out, proj, *, tm, tn):
    T = pm.shape[0]
    col_gp = (D_POOL + 2 * D_RNN) // tn
    col_gr = col_gp + D_MODEL // tn

    def body(pm_ref, z_ref, wp_ref, wr_ref, gp_ref, gr_ref, by_gp_ref, by_gr_ref, sp_ref, sr_ref, mix_ref):
        wp, wr = wp_ref[...], wr_ref[...]
        for r in range(0, tm, EPILOGUE_ROWS):
            rows = pl.ds(r, min(EPILOGUE_ROWS, tm))
            yp, yr = _dot(pm_ref[rows, :], wp), _dot(z_ref[rows, :], wr)
            sp, sr = _sigmoid(gp_ref[rows, :]), _sigmoid(gr_ref[rows, :])
            by_gp_ref[rows, :] = (yp * sp * (1.0 - sp)).astype(BF16)
            by_gr_ref[rows, :] = (yr * sr * (1.0 - sr)).astype(BF16)
            sp_ref[rows, :] = sp.astype(BF16)
            sr_ref[rows, :] = sr.astype(BF16)
            mix_ref[rows, :] = (sp * yp + sr * yr).astype(BF16)

    blk = pl.BlockSpec((tm, tn), lambda i, j: (i, j))
    out = jax.ShapeDtypeStruct((T, D_MODEL), BF16)
    return pl.pallas_call(
        body, name="branch_mix", grid=(T // tm, D_MODEL // tn),
        in_specs=[pl.BlockSpec((tm, D_POOL), lambda i, j: (i, 0)), pl.BlockSpec((tm, D_RNN), lambda i, j: (i, 0)),
                  pl.BlockSpec((D_POOL, tn), lambda i, j: (0, j)), pl.BlockSpec((D_RNN, tn), lambda i, j: (0, j)),
                  pl.BlockSpec((tm, tn), lambda i, j: (i, col_gp + j)), pl.BlockSpec((tm, tn), lambda i, j: (i, col_gr + j))],
        out_specs=[blk] * 5, out_shape=[out] * 5,
        compiler_params=_params(dimension_semantics=("parallel", "parallel")),
    )(pm, z, w_pool_out, w_rnn_out, proj, proj)


def _out_proj_residual(mix, w_o, x, *, tm):
    T = x.shape[0]

    def body(mix_ref, w_ref, x_ref, o_ref):
        o_ref[...] = x_ref[...] + _dot(mix_ref[...], w_ref[...])

    row = pl.BlockSpec((tm, D_MODEL), lambda i: (i, 0))
    return pl.pallas_call(
        body, name="out_proj_residual", grid=(T // tm,),
        in_specs=[row, pl.BlockSpec((D_MODEL, D_MODEL), lambda i: (0, 0)), row],
        out_specs=row, out_shape=jax.ShapeDtypeStruct((T, D_MODEL), F32),
        compiler_params=_params(dimension_semantics=("parallel",)),
    )(mix, w_o, x)


def _ffn_out_loss(act, w, x2, g3, target, *, tm):
    T = x2.shape[0]

    def body(act_ref, w_ref, x2_ref, g_ref, t_ref, dx_ref, dxb_ref, sq_ref, dg_ref):
        @pl.when(pl.program_id(0) == 0)
        def _():
            sq_ref[...] = jnp.zeros_like(sq_ref)
            dg_ref[...] = jnp.zeros_like(dg_ref)

        g, w = g_ref[...], w_ref[...]
        for r in range(0, tm, NORM_ROWS):
            rows = pl.ds(r, min(NORM_ROWS, tm))
            x3 = x2_ref[rows, :] + _dot(act_ref[rows, :], w)
            err = x3 * _rms_scale(x3) * g - t_ref[rows, :]
            sq_ref[...] += jnp.sum(err * err, axis=0, keepdims=True)
            dx, dgp = _rms_bwd(err * (1.0 / D_MODEL), x3, g)
            dg_ref[...] += jnp.sum(dgp, axis=0, keepdims=True)
            dx_ref[rows, :] = dx
            dxb_ref[rows, :] = dx.astype(BF16)

    row = pl.BlockSpec((tm, D_MODEL), lambda i: (i, 0))
    vec = pl.BlockSpec((1, D_MODEL), lambda i: (0, 0))
    return pl.pallas_call(
        body, name="ffn_out_loss", grid=(T // tm,),
        in_specs=[pl.BlockSpec((tm, D_FF), lambda i: (i, 0)), pl.BlockSpec((D_FF, D_MODEL), lambda i: (0, 0)), row, vec, row],
        out_specs=[row, row, vec, vec],
        out_shape=[jax.ShapeDtypeStruct((T, D_MODEL), F32), jax.ShapeDtypeStruct((T, D_MODEL), BF16),
                   jax.ShapeDtypeStruct((1, D_MODEL), F32), jax.ShapeDtypeStruct((1, D_MODEL), F32)],
        compiler_params=_params(dimension_semantics=("arbitrary",)),
    )(act, w, x2, g3, target)


def _ffn_out_bwd(dx3b, w, act_by_gate, act_by_up, *, tm, tn):
    T = dx3b.shape[0]

    def body(dx_ref, w_ref, by_gate_ref, by_up_ref, dgate_ref, dup_ref):
        w = w_ref[...]
        for r in range(0, tm, EPILOGUE_ROWS):
            rows = pl.ds(r, min(EPILOGUE_ROWS, tm))
            dact = _dot_nt(dx_ref[rows, :], w)
            dgate_ref[rows, :] = (dact * by_gate_ref[rows, :].astype(F32)).astype(BF16)
            dup_ref[rows, :] = (dact * by_up_ref[rows, :].astype(F32)).astype(BF16)

    blk = pl.BlockSpec((tm, tn), lambda i, j: (i, j))
    return pl.pallas_call(
        body, name="ffn_out_bwd", grid=(T // tm, D_FF // tn),
        in_specs=[pl.BlockSpec((tm, D_MODEL), lambda i, j: (i, 0)), pl.BlockSpec((tn, D_MODEL), lambda i, j: (j, 0)), blk, blk],
        out_specs=[blk, blk],
        out_shape=[jax.ShapeDtypeStruct((T, D_FF), BF16), jax.ShapeDtypeStruct((T, D_FF), BF16)],
        compiler_params=_params(dimension_semantics=("parallel", "parallel")),
    )(dx3b, w, act_by_gate, act_by_up)


def _ffn_in_bwd(dgate, dup, w, dx3, x2, g2, *, tm, jobs=()):
    T = x2.shape[0]

    def body(dgate_ref, dup_ref, w_ref, dx3_ref, x2_ref, g_ref, dx_ref, dxb_ref, dg_ref):
        @pl.when(pl.program_id(0) == 0)
        def _():
            dg_ref[...] = jnp.zeros_like(dg_ref)

        g = g_ref[...]
        for r in range(0, tm, NORM_ROWS):
            rows = pl.ds(r, min(NORM_ROWS, tm))
            dh = _dot_nt(dgate_ref[rows, :], w_ref[:, :D_FF]) + _dot_nt(dup_ref[rows, :], w_ref[:, D_FF:])
            dxn, dgp = _rms_bwd(dh, x2_ref[rows, :], g)
            dx = dx3_ref[rows, :] + dxn
            dg_ref[...] += jnp.sum(dgp, axis=0, keepdims=True)
            dx_ref[rows, :] = dx
            dxb_ref[rows, :] = dx.astype(BF16)

    row = pl.BlockSpec((tm, D_MODEL), lambda i: (i, 0))
    wide = pl.BlockSpec((tm, D_FF), lambda i: (i, 0))
    vec = pl.BlockSpec((1, D_MODEL), lambda i: (0, 0))
    return _pallas(
        body, (dgate, dup, w, dx3, x2, g2), name="ffn_in_bwd", grid=(T // tm,),
        in_specs=[wide, wide, pl.BlockSpec((D_MODEL, 2 * D_FF), lambda i: (0, 0)), row, row, vec],
        out_specs=[row, row, vec],
        out_shape=[jax.ShapeDtypeStruct((T, D_MODEL), F32), jax.ShapeDtypeStruct((T, D_MODEL), BF16),
                   jax.ShapeDtypeStruct((1, D_MODEL), F32)],
        semantics=("arbitrary",), jobs=jobs)


def _out_proj_bwd(dx2b, w_o, mix_by, *, tm, tn, jobs=()):
    T = dx2b.shape[0]

    def body(dx_ref, w_ref, *refs):
        w = w_ref[...]
        for r in range(0, tm, EPILOGUE_ROWS):
            rows = pl.ds(r, min(EPILOGUE_ROWS, tm))
            dmix = _dot_nt(dx_ref[rows, :], w)
            for by_ref, d_ref in zip(refs[:4], refs[4:]):
                d_ref[rows, :] = (dmix * by_ref[rows, :].astype(F32)).astype(BF16)

    blk = pl.BlockSpec((tm, tn), lambda i, j: (i, j))
    out = jax.ShapeDtypeStruct((T, D_MODEL), BF16)
    return _pallas(
        body, (dx2b, w_o, *mix_by), name="out_proj_bwd", grid=(T // tm, D_MODEL // tn),
        in_specs=[pl.BlockSpec((tm, D_MODEL), lambda i, j: (i, 0)), pl.BlockSpec((tn, D_MODEL), lambda i, j: (j, 0))] + [blk] * 4,
        out_specs=[blk] * 4, out_shape=[out] * 4, semantics=("parallel", "parallel"), jobs=jobs)


def _branch_bwd(dyp, dyr, w_pool_out, w_rnn_out, *, tm):
    T = dyp.shape[0]

    def body(dyp_ref, dyr_ref, wp_ref, wr_ref, dpm_ref, dz_ref):
        dpm_ref[...] = _dot_nt(dyp_ref[...], wp_ref[...])
        dz_ref[...] = _dot_nt(dyr_ref[...], wr_ref[...])

    row = pl.BlockSpec((tm, D_MODEL), lambda i: (i, 0))
    return pl.pallas_call(
        body, name="branch_bwd", grid=(T // tm,),
        in_specs=[row, row, pl.BlockSpec((D_POOL, D_MODEL), lambda i: (0, 0)), pl.BlockSpec((D_RNN, D_MODEL), lambda i: (0, 0))],
        out_specs=[pl.BlockSpec((tm, D_POOL), lambda i: (i, 0)), pl.BlockSpec((tm, D_RNN), lambda i: (i, 0))],
        out_shape=[jax.ShapeDtypeStruct((T, D_POOL), F32), jax.ShapeDtypeStruct((T, D_RNN), F32)],
        compiler_params=_params(dimension_semantics=("parallel",)),
    )(dyp, dyr, w_pool_out, w_rnn_out)


def _in_proj_bwd(segs, w, dx2, x, g1, *, tm, jobs=()):
    T = x.shape[0]
    widths = [s.shape[1] for s in segs]
    offs = [sum(widths[:k]) for k in range(len(widths))]
    n = len(segs)

    def body(*refs):
        seg_refs, (w_ref, dx2_ref, x_ref, g_ref, dx_ref, dg_ref) = refs[:n], refs[n:]

        @pl.when(pl.program_id(0) == 0)
        def _():
            dg_ref[...] = jnp.zeros_like(dg_ref)

        g = g_ref[...]
        for r in range(0, tm, NORM_ROWS):
            rows = pl.ds(r, min(NORM_ROWS, tm))
            dh = _dot_nt(seg_refs[0][rows, :], w_ref[:, offs[0]:offs[0] + widths[0]])
            for k in range(1, n):
                dh += _dot_nt(seg_refs[k][rows, :], w_ref[:, offs[k]:offs[k] + widths[k]])
            dxn, dgp = _rms_bwd(dh, x_ref[rows, :], g)
            dg_ref[...] += jnp.sum(dgp, axis=0, keepdims=True)
            dx_ref[rows, :] = dx2_ref[rows, :] + dxn

    row = pl.BlockSpec((tm, D_MODEL), lambda i: (i, 0))
    vec = pl.BlockSpec((1, D_MODEL), lambda i: (0, 0))
    return _pallas(
        body, (*segs, w, dx2, x, g1), name="in_proj_bwd", grid=(T // tm,),
        in_specs=[pl.BlockSpec((tm, wd), lambda i: (i, 0)) for wd in widths]
        + [pl.BlockSpec((D_MODEL, D_IN), lambda i: (0, 0)), row, row, vec],
        out_specs=[row, vec],
        out_shape=[jax.ShapeDtypeStruct((T, D_MODEL), F32), jax.ShapeDtypeStruct((1, D_MODEL), F32)],
        semantics=("arbitrary",), jobs=jobs)


def _weight_grad(a, segs, *, tm, tn, name, jobs=None):
    T, M = a.shape
    nblk = [s.shape[1] // tn for s in segs]
    first = [sum(nblk[:k]) for k in range(len(segs))]
    n = len(segs)

    def body(a_ref, *refs):
        seg_refs, o_ref = refs[:n], refs[n]
        j = pl.program_id(1)
        for k in range(n):
            @pl.when((j >= first[k]) & (j < first[k] + nblk[k]))
            def _(k=k):
                o_ref[...] = _dot_tn(a_ref[...], seg_refs[k][...])

    def seg_spec(k):
        return pl.BlockSpec((T, tn), lambda i, j: (0, jnp.clip(j - first[k], 0, nblk[k] - 1)))

    (grad,), results = _pallas(
        body, (a, *segs), name=name, grid=(M // tm, sum(nblk)),
        in_specs=[pl.BlockSpec((T, tm), lambda i, j: (0, i))] + [seg_spec(k) for k in range(n)],
        out_specs=[pl.BlockSpec((tm, tn), lambda i, j: (i, j))],
        out_shape=[jax.ShapeDtypeStruct((M, sum(nblk) * tn), F32)],
        semantics=("parallel", "arbitrary"), jobs=jobs or ())
    return grad if jobs is None else (grad, results)


def _pad_front(dst, src, halo):
    dst[pl.ds(0, halo), :] = jnp.zeros((halo, src.shape[1]), F32)

    def fill(i, carry):
        r0 = pl.multiple_of(i * CHUNK, CHUNK)
        dst[pl.ds(r0 + halo, CHUNK), :] = src[pl.ds(r0, CHUNK), :]
        return carry

    lax.fori_loop(0, src.shape[0] // CHUNK, fill, 0)


def _shift_rows(v, k):
    return pltpu.roll(v, k % v.shape[0], axis=0)


def _window_sums(xs, direction):
    s2 = xs + _shift_rows(xs, direction)
    s4 = s2 + _shift_rows(s2, 2 * direction)
    s8 = s4 + _shift_rows(s4, 4 * direction)
    s16 = s8 + _shift_rows(s8, 8 * direction)
    return s2, s4, s8, s16


def _select_window(g, sums):
    s2, s4, s8, s16 = sums
    return jnp.where(g == 0, s2, jnp.where(g == 1, s4, jnp.where(g == 2, s8, s16)))


def _pool_count(g, start, rows):
    t = start + lax.broadcasted_iota(jnp.int32, (rows, 1), 0)
    return jnp.minimum(t + 1, jnp.left_shift(2, g)).astype(F32)


def _pool_fwd(proj, w_grp, scale):
    T = proj.shape[0]
    nchunk = T // CHUNK

    def body(u_ref, w_ref, s_ref, o_ref, upad):
        g = pl.program_id(0)
        _pad_front(upad, u_ref, POOL_HALO)
        w = w_ref[...].astype(BF16)
        scale_row = s_ref[...]

        def chunk(i, carry):
            r0 = pl.multiple_of(i * CHUNK, CHUNK)
            xs = upad[pl.ds(r0, CHUNK + POOL_HALO), :]
            win = _select_window(g, _window_sums(xs, 1))[POOL_HALO:]
            pooled = win / _pool_count(g, r0, CHUNK) - xs[POOL_HALO:]
            o_ref[pl.ds(r0, CHUNK), :] = (_dot(pooled.astype(BF16), w) * scale_row).astype(BF16)
            return carry

        lax.fori_loop(0, nchunk, chunk, 0)

    return pl.pallas_call(
        body, name="pool_fwd", grid=(N_POOL_GROUPS,),
        in_specs=[pl.BlockSpec((T, HEAD), lambda g: (0, g)), pl.BlockSpec((None, HEAD, HEAD), lambda g: (g, 0, 0)),
                  pl.BlockSpec((1, HEAD), lambda g: (0, g))],
        out_specs=pl.BlockSpec((T, HEAD), lambda g: (0, g)),
        out_shape=jax.ShapeDtypeStruct((T, D_POOL), BF16),
        scratch_shapes=[pltpu.VMEM((T + POOL_HALO, HEAD), F32)],
        compiler_params=_params(dimension_semantics=("parallel",)),
    )(proj, w_grp, scale)


def _pool_bwd(proj, dpm, w_grp, scale, jobs=()):
    T = proj.shape[0]
    nchunk = T // CHUNK

    def body(u_ref, dpm_ref, w_ref, s_ref, du_ref, dw_ref, ds_ref, upad, zpad, dpool):
        g = pl.program_id(0)
        _pad_front(upad, u_ref, POOL_HALO)
        zpad[pl.ds(T, POOL_HALO), :] = jnp.zeros((POOL_HALO, HEAD), F32)
        dw_ref[...] = jnp.zeros_like(dw_ref)
        ds_ref[...] = jnp.zeros_like(ds_ref)
        w = w_ref[...].astype(BF16)
        scale_row = s_ref[...]

        def chunk(i, carry):
            r0 = pl.multiple_of(i * CHUNK, CHUNK)
            xs = upad[pl.ds(r0, CHUNK + POOL_HALO), :]
            cnt = _pool_count(g, r0, CHUNK)
            pooled = (_select_window(g, _window_sums(xs, 1))[POOL_HALO:] / cnt - xs[POOL_HALO:]).astype(BF16)
            mixed = _dot(pooled, w)
            d = dpm_ref[pl.ds(r0, CHUNK), :]
            ds_ref[...] += jnp.sum(d * mixed, axis=0, keepdims=True)
            dmixed = (d * scale_row).astype(BF16)
            dw_ref[...] += _dot_tn(pooled, dmixed)
            dp = _dot_nt(dmixed, w)
            dpool[pl.ds(r0, CHUNK), :] = dp
            zpad[pl.ds(r0, CHUNK), :] = dp / cnt
            return carry

        lax.fori_loop(0, nchunk, chunk, 0)

        def chunk2(i, carry):
            r0 = pl.multiple_of(i * CHUNK, CHUNK)
            zs = zpad[pl.ds(r0, CHUNK + POOL_HALO), :]
            win = _select_window(g, _window_sums(zs, -1))[:CHUNK]
            du_ref[pl.ds(r0, CHUNK), :] = (win - dpool[pl.ds(r0, CHUNK), :]).astype(BF16)
            return carry

        lax.fori_loop(0, nchunk, chunk2, 0)

    col = pl.BlockSpec((T, HEAD), lambda g: (0, g))
    return _pallas(
        body, (proj, dpm, w_grp, scale), name="pool_bwd", grid=(N_POOL_GROUPS,),
        in_specs=[col, col, pl.BlockSpec((None, HEAD, HEAD), lambda g: (g, 0, 0)), pl.BlockSpec((1, HEAD), lambda g: (0, g))],
        out_specs=[col, pl.BlockSpec((None, HEAD, HEAD), lambda g: (g, 0, 0)), pl.BlockSpec((1, HEAD), lambda g: (0, g))],
        out_shape=[jax.ShapeDtypeStruct((T, D_POOL), BF16), jax.ShapeDtypeStruct((N_POOL_GROUPS, HEAD, HEAD), F32),
                   jax.ShapeDtypeStruct((1, D_POOL), F32)],
        scratch_shapes=[pltpu.VMEM((T + POOL_HALO, HEAD), F32), pltpu.VMEM((T + POOL_HALO, HEAD), F32), pltpu.VMEM((T, HEAD), F32)],
        semantics=("parallel",), jobs=jobs)


def _conv_taps(xs, cw):
    v = cw[CONV_WIDTH - 1] * xs[SUBLANES:]
    for k in range(CONV_WIDTH - 1):
        v += cw[k] * _shift_rows(xs, CONV_WIDTH - 1 - k)[SUBLANES:]
    return v


def _tap_rows(cw_ref):
    return [cw_ref[k:k + 1, :] for k in range(CONV_WIDTH)]


def _softplus_neg(lam):
    return jnp.maximum(-lam, 0.0) + _log1p(jnp.exp(-jnp.abs(lam)))


def _lru_gates(v, wa, ba, wx, bx, sp):
    vb = v.astype(BF16)
    ra = _sigmoid(_dot(vb, wa) + ba)
    ix = _sigmoid(_dot(vb, wx) + bx)
    log_a = -LRU_C * ra * sp
    a = jnp.exp(log_a)
    sq = jnp.sqrt(-jnp.tanh(log_a) * (a * a + 1.0))
    return ra, ix, a, sq


def _row_bcast(v, r):
    return jnp.broadcast_to(v[r:r + 1, :], v.shape)


TILE_BLOCK = 128


def _scan_in_tiles(coef, coef_shift, A_out, B, T, direction):
    order = list(range(SUBLANES)) if direction == 1 else list(range(SUBLANES - 1, -1, -1))
    tiles = min(TILE_BLOCK, T // SUBLANES)
    for base in range(0, T, tiles * SUBLANES):
        def rows(r, base=base):
            return pl.ds(base + r, tiles, stride=SUBLANES)

        A, Bv = coef[rows(order[0] + coef_shift), :], B[rows(order[0]), :]
        A_out[rows(order[0]), :] = A
        for r in order[1:]:
            a = coef[rows(r + coef_shift), :]
            Bv = a * Bv + B[rows(r), :]
            A = a * A
            A_out[rows(r), :] = A
            B[rows(r), :] = Bv


TILES_PER_STEP = 8


def _carry_tiles(A_s, B_s, out, ntile, direction):
    out_row = SUBLANES - 1 if direction == 1 else 0

    def step(k, carry):
        for j in range(TILES_PER_STEP):
            t = k * TILES_PER_STEP + j
            r0 = pl.multiple_of((t if direction == 1 else ntile - 1 - t) * SUBLANES, SUBLANES)
            A, B = A_s[pl.ds(r0, SUBLANES), :], B_s[pl.ds(r0, SUBLANES), :]
            out[pl.ds(r0, SUBLANES), :] = A * carry + B
            carry = _row_bcast(A, out_row) * carry + _row_bcast(B, out_row)
        return carry

    lax.fori_loop(0, ntile // TILES_PER_STEP, step, jnp.zeros((SUBLANES, HEAD), F32))


def _rnn_fwd(proj, conv_w, conv_b, w_a, b_a, w_x, b_x, lam, jobs=()):
    T = proj.shape[0]
    nchunk = T // CHUNK
    ntile = T // SUBLANES

    def body(u_ref, ug_ref, cw_ref, cb_ref, wa_ref, ba_ref, wx_ref, bx_ref, lam_ref,
             h_ref, z_ref, v_ref, ra_ref, ix_ref, a_ref, sq_ref, upad, a_s, b_s):
        _pad_front(upad, u_ref, SUBLANES)
        cw, cb = _tap_rows(cw_ref), cb_ref[...]
        wa, wx = wa_ref[...].astype(BF16), wx_ref[...].astype(BF16)
        ba, bx = ba_ref[...], bx_ref[...]
        sp = _softplus_neg(lam_ref[...])

        def chunk(i, carry):
            rows = pl.ds(pl.multiple_of(i * CHUNK, CHUNK), CHUNK)
            v = _conv_taps(upad[pl.ds(pl.multiple_of(i * CHUNK, CHUNK), CHUNK + SUBLANES), :], cw) + cb
            ra, ix, a, sq = _lru_gates(v, wa, ba, wx, bx, sp)
            v_ref[rows, :], ra_ref[rows, :], ix_ref[rows, :], a_ref[rows, :], sq_ref[rows, :] = v, ra, ix, a, sq
            a_s[rows, :], b_s[rows, :] = a, sq * ix * v
            return carry

        lax.fori_loop(0, nchunk, chunk, 0)
        _scan_in_tiles(a_s, 0, a_s, b_s, T, 1)
        _carry_tiles(a_s, b_s, h_ref, ntile, 1)

        def chunk3(i, carry):
            r0 = pl.multiple_of(i * CHUNK, CHUNK)
            gl, _ = _gelu_parts(ug_ref[pl.ds(r0, CHUNK), :])
            z_ref[pl.ds(r0, CHUNK), :] = (h_ref[pl.ds(r0, CHUNK), :] * gl).astype(BF16)
            return carry

        lax.fori_loop(0, nchunk, chunk3, 0)

    col = pl.BlockSpec((T, HEAD), lambda h: (0, h))
    vec = pl.BlockSpec((1, HEAD), lambda h: (0, h))
    mat = pl.BlockSpec((None, HEAD, HEAD), lambda h: (h, 0, 0))
    return _pallas(
        body, (proj, proj, conv_w, conv_b, w_a, b_a, w_x, b_x, lam), name="rnn_fwd", grid=(N_RNN_HEADS,),
        in_specs=[pl.BlockSpec((T, HEAD), lambda h: (0, COL_RNN + h)), pl.BlockSpec((T, HEAD), lambda h: (0, COL_GATE + h)),
                  pl.BlockSpec((CONV_WIDTH, HEAD), lambda h: (0, h)), vec, mat, vec, mat, vec, vec],
        out_specs=[col] * 7,
        out_shape=[jax.ShapeDtypeStruct((T, D_RNN), F32), jax.ShapeDtypeStruct((T, D_RNN), BF16)]
        + [jax.ShapeDtypeStruct((T, D_RNN), F32)] * 5,
        scratch_shapes=[pltpu.VMEM((T + SUBLANES, HEAD), F32), pltpu.VMEM((T, HEAD), F32), pltpu.VMEM((T, HEAD), F32)],
        semantics=("parallel",), jobs=jobs)


def _rnn_bwd(proj, hr, dz, gates, conv_w, w_a, w_x, lam, jobs=()):
    T = proj.shape[0]
    nchunk = T // CHUNK
    ntile = T // SUBLANES

    def body(u_ref, ug_ref, h_ref, dz_ref, v_ref, ra_ref, ix_ref, a_ref, sq_ref, cw_ref, wa_ref, wx_ref, lam_ref,
             du_ref, dug_ref, dwa_ref, dwx_ref, dba_ref, dbx_ref, dlam_ref, dcb_ref, dcw_ref,
             upad, hpad, apad, g_s, dvpad, ga_s):
        zero_tile = jnp.zeros((SUBLANES, HEAD), F32)
        _pad_front(upad, u_ref, SUBLANES)
        _pad_front(hpad, h_ref, SUBLANES)
        apad[pl.ds(T, SUBLANES), :] = zero_tile
        dvpad[pl.ds(T, SUBLANES), :] = zero_tile
        for ref in (dwa_ref, dwx_ref, dba_ref, dbx_ref, dlam_ref, dcb_ref, dcw_ref):
            ref[...] = jnp.zeros_like(ref)
        cw = _tap_rows(cw_ref)
        wa, wx = wa_ref[...].astype(BF16), wx_ref[...].astype(BF16)
        lam_row = lam_ref[...]
        sp = _softplus_neg(lam_row)

        def chunk(i, carry):
            rows = pl.ds(pl.multiple_of(i * CHUNK, CHUNK), CHUNK)
            apad[rows, :] = a_ref[rows, :]
            gl, dgl = _gelu_parts(ug_ref[rows, :])
            d = dz_ref[rows, :]
            g_s[rows, :] = d * gl
            dug_ref[rows, :] = (d * h_ref[rows, :] * dgl).astype(BF16)
            return carry

        lax.fori_loop(0, nchunk, chunk, 0)

        _scan_in_tiles(apad, 1, ga_s, g_s, T, -1)
        _carry_tiles(ga_s, g_s, g_s, ntile, -1)

        def chunk3(i, carry):
            r0 = pl.multiple_of(i * CHUNK, CHUNK)
            rows = pl.ds(r0, CHUNK)
            g = g_s[rows, :]
            h_prev = _shift_rows(hpad[pl.ds(r0, CHUNK + SUBLANES), :], 1)[SUBLANES:]
            v, ra, ix, sq, a = v_ref[rows, :], ra_ref[rows, :], ix_ref[rows, :], sq_ref[rows, :], a_ref[rows, :]
            d_sq = g * ix * v
            d_ix = g * sq * v
            d_la = a * g * h_prev - d_sq * a * a / sq
            dlam_ref[...] += jnp.sum(d_la * ra, axis=0, keepdims=True)
            d_pa = d_la * (-LRU_C) * sp * ra * (1.0 - ra)
            d_px = d_ix * ix * (1.0 - ix)
            vb, d_pab, d_pxb = v.astype(BF16), d_pa.astype(BF16), d_px.astype(BF16)
            dwa_ref[...] += _dot_tn(vb, d_pab)
            dwx_ref[...] += _dot_tn(vb, d_pxb)
            dba_ref[...] += jnp.sum(d_pa, axis=0, keepdims=True)
            dbx_ref[...] += jnp.sum(d_px, axis=0, keepdims=True)
            dv = g * sq * ix + _dot_nt(d_pab, wa) + _dot_nt(d_pxb, wx)
            dvpad[rows, :] = dv
            dcb_ref[...] += jnp.sum(dv, axis=0, keepdims=True)
            xs = upad[pl.ds(r0, CHUNK + SUBLANES), :]
            for k in range(CONV_WIDTH):
                u_k = _shift_rows(xs, CONV_WIDTH - 1 - k)[SUBLANES:] if k < CONV_WIDTH - 1 else xs[SUBLANES:]
                dcw_ref[k:k + 1, :] += jnp.sum(dv * u_k, axis=0, keepdims=True)
            return carry

        lax.fori_loop(0, nchunk, chunk3, 0)
        dlam_ref[...] = dlam_ref[...] * (LRU_C * _sigmoid(-lam_row))

        def chunk4(i, carry):
            r0 = pl.multiple_of(i * CHUNK, CHUNK)
            dvs = dvpad[pl.ds(r0, CHUNK + SUBLANES), :]
            du = cw[CONV_WIDTH - 1] * dvs[:CHUNK]
            for k in range(CONV_WIDTH - 1):
                du += cw[k] * _shift_rows(dvs, -(CONV_WIDTH - 1 - k))[:CHUNK]
            du_ref[pl.ds(r0, CHUNK), :] = du.astype(BF16)
            return carry

        lax.fori_loop(0, nchunk, chunk4, 0)

    col = pl.BlockSpec((T, HEAD), lambda h: (0, h))
    vec = pl.BlockSpec((1, HEAD), lambda h: (0, h))
    mat = pl.BlockSpec((None, HEAD, HEAD), lambda h: (h, 0, 0))
    taps = pl.BlockSpec((CONV_WIDTH, HEAD), lambda h: (0, h))
    vec_out = jax.ShapeDtypeStruct((1, D_RNN), F32)
    mat_out = jax.ShapeDtypeStruct((N_RNN_HEADS, HEAD, HEAD), F32)
    seq = pltpu.VMEM((T, HEAD), F32)
    seq_pad = pltpu.VMEM((T + SUBLANES, HEAD), F32)
    return _pallas(
        body, (proj, proj, hr, dz, *gates, conv_w, w_a, w_x, lam), name="rnn_bwd", grid=(N_RNN_HEADS,),
        in_specs=[pl.BlockSpec((T, HEAD), lambda h: (0, COL_RNN + h)), pl.BlockSpec((T, HEAD), lambda h: (0, COL_GATE + h))]
        + [col] * 7 + [taps, mat, mat, vec],
        out_specs=[col, col, mat, mat, vec, vec, vec, vec, taps],
        out_shape=[jax.ShapeDtypeStruct((T, D_RNN), BF16), jax.ShapeDtypeStruct((T, D_RNN), BF16), mat_out, mat_out,
                   vec_out, vec_out, vec_out, vec_out, jax.ShapeDtypeStruct((CONV_WIDTH, D_RNN), F32)],
        scratch_shapes=[seq_pad, seq_pad, seq_pad, seq, seq_pad, seq],
        semantics=("parallel",), jobs=jobs)


GROUP_FFN_OUT = ["w_ffn_out"]
GROUP_FFN_IN = ["w_ffn_in"]
GROUP_MIX = ["w_o", "w_pool_out", "w_rnn_out"]
GROUP_IN = ["w_in"]


def _step(x, target, s, full, conv_w, place):
    T = x.shape[0]
    tall, mid, low = min(T, 2048), min(T, 1024), min(T, 512)
    full = dict(full)

    def gathered(names, results):
        full.update(zip(names, results))

    early = ["w_pool_out", "w_rnn_out", "w_o", "w_ffn_out"]
    (proj, h1), (res,) = _norm_matmul(x, s["norm_mix"], full["w_in"], tm=tall, tn=512, name="in_proj", jobs=[_gather_job(full, early)])
    gathered(early, res)
    pm = _pool_fwd(proj, s["w_pool_grp"], s["pool_scale"])
    (hr, z, *gates), (res,) = _rnn_fwd(proj, conv_w, s["conv_b"], s["w_rg_a"], s["b_rg_a"], s["w_rg_x"], s["b_rg_x"],
                                       s["lru_lambda"], jobs=[_gather_job(full, ["w_ffn_in"])])
    gathered(["w_ffn_in"], res)
    *mix_by, mix = _branch_mix(pm, z, full["w_pool_out"], full["w_rnn_out"], proj, tm=tall, tn=256)
    x2 = _out_proj_residual(mix, full["w_o"], x, tm=mid)
    (act_by_up, act_by_gate, act, h2), _ = _ffn_in(x2, s["norm_ffn"], full["w_ffn_in"], tm=tall, tn=256)
    dx3, dx3b, sq_cols, g_norm_final = _ffn_out_loss(act, full["w_ffn_out"], x2, s["norm_final"], target, tm=low)

    g = {"norm_final": g_norm_final}

    def chip_sums(names, from_sibling):
        sums = {name: _chip_sum(name, g[name], got, place) for name, got in zip(names, from_sibling)}
        return {name: v[0] for name, v in sums.items()}, {name: v[1] for name, v in sums.items()}

    def final_sums(names, sums, from_chips):
        return {name: _final_sum(name, sums[name], got, place) for name, got in zip(names, from_chips)}

    dgate, dup = _ffn_out_bwd(dx3b, full["w_ffn_out"], act_by_gate, act_by_up, tm=tall, tn=256)
    g["w_ffn_out"] = _weight_grad(act, [dx3b], tm=256, tn=D_MODEL, name="w_ffn_out_grad")
    (dx2, dx2b, g["norm_ffn"]), (res,) = _ffn_in_bwd(dgate, dup, full["w_ffn_in"], dx3, x2, s["norm_ffn"], tm=low,
                                                     jobs=[_sibling_job(g, GROUP_FFN_OUT)])
    sums_ffn, sums_ffn_bf16 = chip_sums(GROUP_FFN_OUT, res)
    g["w_ffn_in"], (res,) = _weight_grad(h2, [dgate, dup], tm=D_MODEL, tn=256, name="w_ffn_in_grad",
                                         jobs=[_chips_job(sums_ffn_bf16, GROUP_FFN_OUT)])
    shards_ffn = final_sums(GROUP_FFN_OUT, sums_ffn, res)
    (dgp, dgr, dyp, dyr), (res,) = _out_proj_bwd(dx2b, full["w_o"], mix_by, tm=tall, tn=256,
                                                 jobs=[_sibling_job(g, GROUP_FFN_IN)])
    sums_ffn, sums_ffn_bf16 = chip_sums(GROUP_FFN_IN, res)
    g["w_o"] = _weight_grad(mix, [dx2b], tm=D_MODEL, tn=256, name="w_o_grad")
    dpm, dz = _branch_bwd(dyp, dyr, full["w_pool_out"], full["w_rnn_out"], tm=mid)
    g["w_pool_out"] = _weight_grad(pm, [dyp], tm=D_POOL, tn=256, name="w_pool_out_grad")
    g["w_rnn_out"] = _weight_grad(z, [dyr], tm=D_RNN, tn=256, name="w_rnn_out_grad")
    (dupool, g["w_pool_grp"], g["pool_scale"]), (res,) = _pool_bwd(proj, dpm, s["w_pool_grp"], s["pool_scale"],
                                                                   jobs=[_sibling_job(g, GROUP_MIX)])
    sums_mix, sums_mix_bf16 = chip_sums(GROUP_MIX, res)
    ((durnn, dugate, g["w_rg_a"], g["w_rg_x"], g["b_rg_a"], g["b_rg_x"], g["lru_lambda"], g["conv_b"], g["conv_w"]),
     (res,)) = _rnn_bwd(proj, hr, dz, gates, conv_w, s["w_rg_a"], s["w_rg_x"], s["lru_lambda"],
                        jobs=[_chips_job(sums_ffn_bf16, GROUP_FFN_IN)])
    shards_ffn.update(final_sums(GROUP_FFN_IN, sums_ffn, res))
    segs = [dupool, durnn, dugate, dgp, dgr]
    ffn = GROUP_FFN_OUT + GROUP_FFN_IN
    g["w_in"], (res, joined) = _weight_grad(h1, segs, tm=D_MODEL, tn=256, name="w_in_grad",
                                           jobs=[_chips_job(sums_mix_bf16, GROUP_MIX), _join_job(shards_ffn, ffn)])
    grads = dict(zip(ffn, joined))
    shards = final_sums(GROUP_MIX, sums_mix, res)
    (res,) = _run_jobs([_sibling_job(g, GROUP_IN)], "w_in_exchange_sibling")
    sums_in, sums_in_bf16 = chip_sums(GROUP_IN, res)
    (grad_x, g["norm_mix"]), (res,) = _in_proj_bwd(segs, full["w_in"], dx2, x, s["norm_mix"], tm=low,
                                                  jobs=[_chips_job(sums_in_bf16, GROUP_IN)])
    shards.update(final_sums(GROUP_IN, sums_in, res))

    vec_rows = [g[name] if name != "pool_scale" else jnp.pad(g[name], ((0, 0), (0, D_MODEL - D_POOL))) for name in VEC_ITEMS]
    vec_rows += [g["conv_w"], sq_cols, jnp.zeros((VEC_ROWS - len(VEC_ITEMS) - CONV_WIDTH - 1, D_MODEL), F32)]
    vec = jnp.concatenate(vec_rows, axis=0).reshape(VEC_ROWS, N_DEV, HEAD).transpose(1, 0, 2)
    mat = jnp.concatenate([g[name].reshape(-1, HEAD) for name in MAT_ITEMS], axis=0).reshape(N_DEV, -1, HEAD)
    (vec, mat), (joined,) = _all_reduce_small([vec, mat], jobs=[_join_job(shards, GROUP_MIX + GROUP_IN)])
    grads.update(zip(GROUP_MIX + GROUP_IN, joined))

    vec = vec.transpose(1, 0, 2).reshape(VEC_ROWS, D_MODEL)
    mat = mat.reshape(-1, HEAD)
    for k, name in enumerate(VEC_ITEMS):
        grads[name] = vec[k:k + 1, :s[name].shape[1]]
    grads["conv_w"] = vec[len(VEC_ITEMS):len(VEC_ITEMS) + CONV_WIDTH]
    row = 0
    for name in MAT_ITEMS:
        rows = s[name].shape[0] * HEAD
        grads[name] = mat[row:row + rows]
        row += rows
    return vec[len(VEC_ITEMS) + CONV_WIDTH], grad_x, grads


LARGE = {"w_in": "col", "w_pool_out": "col", "w_rnn_out": "row", "w_o": "row", "w_ffn_in": "col", "w_ffn_out": "row"}
LARGE_SHAPE = {"w_in": (D_MODEL, D_IN), "w_pool_out": (D_POOL, D_MODEL), "w_rnn_out": (D_RNN, D_MODEL),
               "w_o": (D_MODEL, D_MODEL), "w_ffn_in": (D_MODEL, 2 * D_FF), "w_ffn_out": (D_FF, D_MODEL)}


def _place():
    x, y, c = lax.axis_index("x"), lax.axis_index("y"), lax.axis_index("c")
    return 2 * x + y, c


def _chip_device(chip, c):
    return (chip // 2, chip % 2, c)


def _chip_window(ref, kind, shape, chip, half=None):
    K, N = shape
    if kind == "col":
        rows = slice(None) if half is None else pl.ds(half * (K // 2), K // 2)
        return ref.at[rows, pl.ds(chip * (N // N_CHIPS), N // N_CHIPS)]
    ks = K // N_CHIPS
    if half is None:
        return ref.at[pl.ds(chip * ks, ks), :]
    return ref.at[pl.ds(chip * ks + half * (ks // 2), ks // 2), :]


def _row_half(ref, half):
    rows = ref.shape[0] // 2
    return ref.at[pl.ds(half * rows, rows), :]


def _remote(win_src, win_dst, send_sems, recv_sems, idx, to):
    return pltpu.make_async_remote_copy(src_ref=win_src, dst_ref=win_dst, send_sem=send_sems.at[idx], recv_sem=recv_sems.at[idx],
                                        device_id=to, device_id_type=MESH)


def _gather_job(full, names, conv_w_full=None):
    n = len(names)
    cw_cols = D_RNN // N_CHIPS

    def windows(refs, chip, half):
        return [_chip_window(refs[k], LARGE[name], LARGE_SHAPE[name], chip, half) for k, name in enumerate(names)]

    def ici_copies(refs, send_sems, recv_sems, src_chip, dst_chip, c, r):
        wins = windows(refs, src_chip, c)
        if conv_w_full is not None:
            wins.append(refs[n].at[:, pl.ds(src_chip * cw_cols, cw_cols)])
        return [_remote(win, win, send_sems, recv_sems, (k, r), _chip_device(dst_chip, c)) for k, win in enumerate(wins)]

    def forwards(refs, send_sems, recv_sems, src_chip, half, to_core, chip, r):
        return [_remote(win, win, send_sems, recv_sems, (k, 3 + r), _chip_device(chip, to_core))
                for k, win in enumerate(windows(refs, src_chip, half))]

    def start(ins, outs, send_sems, recv_sems):
        chip, c = _place()
        for r in range(3):
            for cp in ici_copies(outs, send_sems, recv_sems, chip, chip ^ (r + 1), c, r):
                cp.start()

    def finish(ins, outs, send_sems, recv_sems):
        chip, c = _place()
        for r in range(3):
            for cp in ici_copies(outs, send_sems, recv_sems, chip ^ (r + 1), chip, c, r):
                cp.wait_recv()
            for cp in forwards(outs, send_sems, recv_sems, chip ^ (r + 1), c, 1 - c, chip, r):
                cp.start()
        for r in range(3):
            for cp in forwards(outs, send_sems, recv_sems, chip ^ (r + 1), 1 - c, c, chip, r):
                cp.wait_recv()
            for cp in ici_copies(outs, send_sems, recv_sems, chip, chip ^ (r + 1), c, r):
                cp.wait_send()
            for cp in forwards(outs, send_sems, recv_sems, chip ^ (r + 1), c, 1 - c, chip, r):
                cp.wait_send()

    arrays = [full[name] for name in names] + ([conv_w_full] if conv_w_full is not None else [])
    return _Job(arrays, [jax.ShapeDtypeStruct(a.shape, a.dtype) for a in arrays], {k: k for k in range(len(arrays))},
                (len(arrays), 6), start, finish)


def _core_halves(ref, kind, shape, c):
    return [_chip_window(ref, kind, shape, chip, c) for chip in range(N_CHIPS)]


def _sibling_job(grads, names):
    def start(ins, outs, send_sems, recv_sems):
        chip, c = _place()
        for k, name in enumerate(names):
            kind, shape = LARGE[name], LARGE_SHAPE[name]
            if kind == "col":
                pairs = [(_row_half(ins[k], 1 - c), outs[k])]
            else:
                rows = shape[0] // N_DEV
                pairs = [(win, outs[k].at[pl.ds(j * rows, rows), :]) for j, win in enumerate(_core_halves(ins[k], kind, shape, 1 - c))]
            for src, dst in pairs:
                _remote(src, dst, send_sems, recv_sems, k, _chip_device(chip, 1 - c)).start()

    def finish(ins, outs, send_sems, recv_sems):
        chip, c = _place()
        for k in range(len(names)):
            _remote(outs[k], outs[k], send_sems, recv_sems, k, _chip_device(chip, 1 - c)).wait()

    return _Job([grads[name] for name in names],
                [jax.ShapeDtypeStruct((LARGE_SHAPE[name][0] // 2, LARGE_SHAPE[name][1]), F32) for name in names], {},
                (len(names),), start, finish)


def _chip_sum(name, g, got, place):
    kind, (K, N) = LARGE[name], LARGE_SHAPE[name]
    rows = K // N_DEV
    piece_cols = N // N_CHIPS

    def body(place_ref, g_ref, got_ref, o_ref, ob_ref):
        total = g_ref[...] + got_ref[...]
        ob_ref[...] = total.astype(BF16)
        if kind == "col":
            for chip in range(N_CHIPS):
                @pl.when(place_ref[0] == chip)
                def _(chip=chip):
                    o_ref[...] = total[:, chip * piece_cols:(chip + 1) * piece_cols]
        else:
            @pl.when(pl.program_id(0) == place_ref[0])
            def _():
                o_ref[...] = total

    if kind == "col":
        mine = pl.BlockSpec((rows, N), lambda j, place_ref: (j + N_CHIPS * place_ref[1], 0))
        own = pl.BlockSpec((rows, piece_cols), lambda j, place_ref: (j, 0))
    else:
        mine = pl.BlockSpec((rows, N), lambda j, place_ref: (2 * j + place_ref[1], 0))
        own = pl.BlockSpec((rows, N), lambda j, place_ref: (0, 0))
    blk = pl.BlockSpec((rows, N), lambda j, place_ref: (j, 0))
    return pl.pallas_call(
        body, name=name + "_chip_sum",
        grid_spec=pltpu.PrefetchScalarGridSpec(num_scalar_prefetch=1, grid=(N_CHIPS,), in_specs=[mine, blk], out_specs=[own, blk]),
        out_shape=[jax.ShapeDtypeStruct(_piece_shape(name), F32), jax.ShapeDtypeStruct((K // 2, N), BF16)],
        compiler_params=_params(dimension_semantics=("arbitrary",)),
    )(place, g, got)


def _piece(ref, kind, shape, chip):
    K, N = shape
    if kind == "col":
        return ref.at[:, pl.ds(chip * (N // N_CHIPS), N // N_CHIPS)]
    return ref.at[pl.ds(chip * (K // N_DEV), K // N_DEV), :]


def _piece_shape(name):
    kind, (K, N) = LARGE[name], LARGE_SHAPE[name]
    return (K // 2, N // N_CHIPS) if kind == "col" else (K // N_DEV, N)


def _chips_job(sums, names):
    def copies(ins, outs, send_sems, recv_sems):
        chip, c = _place()
        return [_remote(_piece(ins[k], LARGE[name], LARGE_SHAPE[name], chip ^ (r + 1)), outs[k].at[r], send_sems, recv_sems, (k, r),
                        _chip_device(chip ^ (r + 1), c)) for k, name in enumerate(names) for r in range(3)]

    def start(*refs):
        for cp in copies(*refs):
            cp.start()

    def finish(*refs):
        for cp in copies(*refs):
            cp.wait()

    return _Job([sums[name] for name in names], [jax.ShapeDtypeStruct((3,) + _piece_shape(name), BF16) for name in names], {},
                (len(names), 3), start, finish)


def _final_sum(name, chip_sum, got, place):
    rows, cols = _piece_shape(name)

    def body(place_ref, s_ref, got_ref, o_ref):
        o_ref[...] = ((s_ref[...] + got_ref[0].astype(F32)) + got_ref[1].astype(F32)) + got_ref[2].astype(F32)

    mine = pl.BlockSpec((rows, cols), lambda i, place_ref: (0, 0))
    return pl.pallas_call(
        body, name=name + "_final_sum",
        grid_spec=pltpu.PrefetchScalarGridSpec(
            num_scalar_prefetch=1, grid=(1,), in_specs=[mine, pl.BlockSpec((3, rows, cols), lambda i, place_ref: (0, 0, 0))],
            out_specs=pl.BlockSpec((rows, cols), lambda i, place_ref: (place_ref[1], 0))),
        out_shape=jax.ShapeDtypeStruct((2 * rows, cols), F32),
        compiler_params=_params(dimension_semantics=("arbitrary",)),
    )(place, chip_sum, got)


def _join_job(shards, names):
    def half_copy(outs, send_sems, recv_sems, k, mine):
        chip, c = _place()
        win = _row_half(outs[k], c if mine else 1 - c)
        return _remote(win, win, send_sems, recv_sems, k, _chip_device(chip, 1 - c))

    def start(ins, outs, send_sems, recv_sems):
        for k in range(len(names)):
            half_copy(outs, send_sems, recv_sems, k, True).start()

    def finish(ins, outs, send_sems, recv_sems):
        for k in range(len(names)):
            half_copy(outs, send_sems, recv_sems, k, True).wait_send()
            half_copy(outs, send_sems, recv_sems, k, False).wait_recv()

    arrays = [shards[name] for name in names]
    return _Job(arrays, [jax.ShapeDtypeStruct(a.shape, F32) for a in arrays], {k: k for k in range(len(arrays))},
                (len(arrays),), start, finish)


VEC_ROWS = 16


def _all_reduce_small(slabs, jobs=()):
    n = len(slabs)

    def body(*refs):
        in_refs, out_refs, got_refs = refs[:n], refs[n:2 * n], refs[2 * n:3 * n]
        send_sems, recv_sems = refs[3 * n:]
        x, y, c = lax.axis_index("x"), lax.axis_index("y"), lax.axis_index("c")
        me = 4 * x + 2 * y + c

        def remote(src, dst, k, phase, r):
            other = me ^ r
            return pltpu.make_async_remote_copy(src_ref=src, dst_ref=dst, send_sem=send_sems.at[k, phase, r],
                                                recv_sem=recv_sems.at[k, phase, r],
                                                device_id=(other // 4, (other // 2) % 2, other % 2), device_id_type=MESH)

        scatter = [remote(in_refs[k].at[me ^ r], got_refs[k].at[r], k, 0, r) for r in range(1, N_DEV) for k in range(n)]
        for cp in scatter:
            cp.start()
        for cp in scatter:
            cp.wait()
        for k in range(n):
            total = in_refs[k][me]
            for r in range(1, N_DEV):
                total = total + got_refs[k][r]
            out_refs[k][me] = total
        gather = [remote(out_refs[k].at[me], out_refs[k].at[me], k, 1, r) for r in range(1, N_DEV) for k in range(n)]
        for cp in gather:
            cp.start()
        for r in range(1, N_DEV):
            for k in range(n):
                remote(out_refs[k].at[me ^ r], out_refs[k].at[me ^ r], k, 1, r).wait_recv()
        for cp in gather:
            cp.wait_send()

    return _pallas(
        body, slabs, name="all_reduce_small", grid=(), in_specs=[VMEM] * n, out_specs=[VMEM] * n,
        out_shape=[jax.ShapeDtypeStruct(s.shape, F32) for s in slabs],
        scratch_shapes=[pltpu.VMEM(s.shape, F32) for s in slabs]
        + [pltpu.SemaphoreType.DMA((n, 2, N_DEV)), pltpu.SemaphoreType.DMA((n, 2, N_DEV))], jobs=jobs)


def _cast_into_whole(w, name, place):
    rows, cols = w.shape
    tr = rows // 2

    def body(place_ref, w_ref, o_ref):
        o_ref[...] = w_ref[...].astype(BF16)

    if LARGE[name] == "col":
        window = pl.BlockSpec((tr, cols), lambda i, place_ref: (i, place_ref[0]))
    else:
        window = pl.BlockSpec((tr, cols), lambda i, place_ref: (2 * place_ref[0] + i, 0))
    return pl.pallas_call(
        body, name=name + "_cast",
        grid_spec=pltpu.PrefetchScalarGridSpec(num_scalar_prefetch=1, grid=(2,),
                                               in_specs=[pl.BlockSpec((tr, cols), lambda i, place_ref: (i, 0))], out_specs=window),
        out_shape=jax.ShapeDtypeStruct(LARGE_SHAPE[name], BF16),
        compiler_params=_params(dimension_semantics=("parallel",)))(place, w)


def _cast_many_into_whole(shards, place, jobs):
    names = list(shards)
    n = len(names)

    def body(place_ref, *refs):
        for w_ref, o_ref in zip(refs[:n], refs[n:]):
            o_ref[...] = w_ref[...].astype(BF16)

    def window(name):
        rows, cols = shards[name].shape
        if LARGE[name] == "col":
            return pl.BlockSpec((rows // 2, cols), lambda i, place_ref: (i, place_ref[0]))
        return pl.BlockSpec((rows // 2, cols), lambda i, place_ref: (2 * place_ref[0] + i, 0))

    def half(name):
        rows, cols = shards[name].shape
        return pl.BlockSpec((rows // 2, cols), lambda i, place_ref: (i, 0))

    return _pallas(body, [shards[name] for name in names], name="cast_weights", grid=(2,),
                   in_specs=[half(name) for name in names], out_specs=[window(name) for name in names],
                   out_shape=[jax.ShapeDtypeStruct(LARGE_SHAPE[name], BF16) for name in names],
                   semantics=("arbitrary",), jobs=jobs, prefetch=place)


def _adamw_math(w, g, m, v):
    m = ADAM_B1 * m + (1.0 - ADAM_B1) * g
    v = ADAM_B2 * v + (1.0 - ADAM_B2) * (g * g)
    m_hat = m / (1.0 - ADAM_B1 ** ADAM_STEP)
    v_hat = v / (1.0 - ADAM_B2 ** ADAM_STEP)
    delta = -ADAM_LR * (m_hat / (jnp.sqrt(v_hat) + ADAM_EPS) + ADAM_WD * w)
    return delta, m, v


def _adamw_large(w, g, m, v, name):
    rows, cols = w.shape
    steps = 2
    tr = rows // steps

    def body(w_ref, g_ref, m_ref, v_ref, d_ref, mo_ref, vo_ref):
        d_ref[...], mo_ref[...], vo_ref[...] = _adamw_math(w_ref[...], g_ref[...], m_ref[...], v_ref[...])

    blk = pl.BlockSpec((tr, cols), lambda i: (i, 0))
    out = jax.ShapeDtypeStruct(w.shape, F32)
    return pl.pallas_call(body, name=name + "_adamw", grid=(steps,), in_specs=[blk] * 4, out_specs=[blk] * 3, out_shape=[out] * 3,
                          compiler_params=_params(dimension_semantics=("parallel",)))(w, g, m, v)


def _adamw_small(ws, gs, ms, vs):
    n = len(ws)

    def body(*refs):
        for k in range(n):
            w_ref, g_ref, m_ref, v_ref = (refs[q * n + k] for q in range(4))
            d_ref, mo_ref, vo_ref = (refs[(4 + q) * n + k] for q in range(3))
            d_ref[...], mo_ref[...], vo_ref[...] = _adamw_math(w_ref[...], g_ref[...], m_ref[...], v_ref[...])

    out = [jax.ShapeDtypeStruct(w.shape, F32) for w in ws]
    res = pl.pallas_call(body, name="small_adamw", in_specs=[VMEM] * (4 * n), out_specs=[VMEM] * (3 * n), out_shape=out * 3,
                         compiler_params=_params())(*ws, *gs, *ms, *vs)
    return res[:n], res[n:2 * n], res[2 * n:]


WEIGHTS = ["norm_mix", "w_in", "w_pool_grp", "pool_scale", "w_pool_out", "conv_w", "conv_b", "w_rg_a", "b_rg_a", "w_rg_x",
           "b_rg_x", "lru_lambda", "w_rnn_out", "w_o", "norm_ffn", "w_ffn_in", "w_ffn_out", "norm_final"]
VEC_ITEMS = ["norm_mix", "norm_ffn", "norm_final", "pool_scale", "conv_b", "lru_lambda", "b_rg_a", "b_rg_x"]
MAT_ITEMS = ["w_pool_grp", "w_rg_a", "w_rg_x"]


def _as2d(name, a):
    if name in MAT_ITEMS:
        return a.reshape(-1, HEAD, HEAD)
    if name == "conv_w":
        return a.reshape(CONV_WIDTH, -1)
    return a.reshape(1, -1)


def kernel(x, norm_mix, w_in, w_pool_grp, pool_scale, w_pool_out, conv_w, conv_b, w_rg_a, b_rg_a, w_rg_x, b_rg_x, lru_lambda, w_rnn_out, w_o, norm_ffn, w_ffn_in, w_ffn_out, norm_final, loss_target, m_norm_mix, m_w_in, m_w_pool_grp, m_pool_scale, m_w_pool_out, m_conv_w, m_conv_b, m_w_rg_a, m_b_rg_a, m_w_rg_x, m_b_rg_x, m_lru_lambda, m_w_rnn_out, m_w_o, m_norm_ffn, m_w_ffn_in, m_w_ffn_out, m_norm_final, v_norm_mix, v_w_in, v_w_pool_grp, v_pool_scale, v_w_pool_out, v_conv_w, v_conv_b, v_w_rg_a, v_b_rg_a, v_w_rg_x, v_b_rg_x, v_lru_lambda, v_w_rnn_out, v_w_o, v_norm_ffn, v_w_ffn_in, v_w_ffn_out, v_norm_final):
    given = dict(locals())
    w = {name: given[name] for name in WEIGHTS}
    m = {name: given["m_" + name] for name in WEIGHTS}
    v = {name: given["v_" + name] for name in WEIGHTS}
    chip, c = _place()

    place = jnp.stack([chip, c]).astype(jnp.int32)
    conv_cols = w["conv_w"].shape[-1]
    conv_w_mine = lax.dynamic_update_slice_in_dim(jnp.zeros((CONV_WIDTH, D_RNN), F32), w["conv_w"][0], chip * conv_cols, axis=1)
    w_in_mine = _cast_into_whole(w["w_in"][0], "w_in", place)
    later = [name for name in LARGE if name != "w_in"]
    casts, ((w_in_full, conv_w_full),) = _cast_many_into_whole(
        {name: w[name][0] for name in later}, place, jobs=[_gather_job({"w_in": w_in_mine}, ["w_in"], conv_w_mine)])
    full = dict(zip(later, casts), w_in=w_in_full)
    small = {name: _as2d(name, w[name]) for name in WEIGHTS if name not in LARGE and name != "conv_w"}
    sq_cols, grad_x, grads = _step(x[0], loss_target[0], small, full, conv_w_full, place)
    loss = 0.5 / D_MODEL * jnp.sum(sq_cols)
    grads["conv_w"] = lax.dynamic_slice_in_dim(grads["conv_w"], chip * conv_cols, conv_cols, axis=1)

    delta, new_m, new_v = {}, {}, {}
    for name in LARGE:
        delta[name], new_m[name], new_v[name] = _adamw_large(w[name][0], grads[name], m[name][0], v[name][0], name)
    small_names = [name for name in WEIGHTS if name not in LARGE]
    flat = lambda d: [d[name].reshape(grads[name].shape) for name in small_names]
    ds, mo, vo = _adamw_small(flat(w), [grads[name] for name in small_names], flat(m), flat(v))
    for k, name in enumerate(small_names):
        delta[name], new_m[name], new_v[name] = ds[k], mo[k], vo[k]

    shaped = lambda d: [d[name].reshape(w[name].shape) for name in WEIGHTS]
    return (loss, grad_x[None], *shaped(grads), *shaped(delta), *shaped(new_m), *shaped(new_v))
```

```python
import functools
import math

import jax
import jax.numpy as jnp
from jax import lax
from jax.experimental import pallas as pl
from jax.experimental.pallas import tpu as pltpu

F32 = jnp.float32
BF16 = jnp.bfloat16

D_MODEL = 1024
D_POOL = 512
N_POOL_GROUPS = 4
D_RNN = 1024
N_RNN_HEADS = 8
HEAD = 128
CONV_WIDTH = 4
LRU_C = 8.0
D_FF = 2816
D_IN = D_POOL + 2 * D_RNN + 2 * D_MODEL
NORM_EPS = 1e-6
COL_RNN = D_POOL // HEAD
COL_GATE = (D_POOL + D_RNN) // HEAD

ADAM_LR = 0.001
ADAM_B1 = 0.9
ADAM_B2 = 0.999
ADAM_EPS = 1e-08
ADAM_WD = 0.01
ADAM_STEP = 10

N_CHIPS = 4
N_DEV = 8
MESH = pl.DeviceIdType.MESH
ANY = pl.BlockSpec(memory_space=pl.ANY)
VMEM = pl.BlockSpec(memory_space=pltpu.VMEM)
VMEM_LIMIT_BYTES = 60 * 1024 * 1024
SUBLANES = 8
POOL_HALO = 16
CHUNK = 1024

GELU_C = math.sqrt(2.0 / math.pi)
GELU_A = 0.044715


def _params(**kw):
    return pltpu.CompilerParams(vmem_limit_bytes=VMEM_LIMIT_BYTES, **kw)


def _sigmoid(x):
    return 0.5 * jnp.tanh(0.5 * x) + 0.5


def _log1p(y):
    u = 1.0 + y
    d = u - 1.0
    return jnp.where(d == 0.0, y, jnp.log(u) * (y / jnp.where(d == 0.0, 1.0, d)))


def _gelu_parts(x):
    x2 = x * x
    th = jnp.tanh(GELU_C * (x + GELU_A * x * x2))
    g = 0.5 * x * (1.0 + th)
    dg = 0.5 * (1.0 + th) + 0.5 * x * (1.0 - th * th) * GELU_C * (1.0 + 3.0 * GELU_A * x2)
    return g, dg


def _dot(a, b):
    return jnp.dot(a, b, preferred_element_type=F32)


def _dot_nt(a, b):
    return lax.dot_general(a, b, (((1,), (1,)), ((), ())), preferred_element_type=F32)


def _dot_tn(a, b):
    return lax.dot_general(a, b, (((0,), (0,)), ((), ())), preferred_element_type=F32)


def _rms_scale(xv):
    return lax.rsqrt(jnp.mean(xv * xv, axis=-1, keepdims=True) + NORM_EPS)


def _rms_bwd(dy, xv, g):
    r = _rms_scale(xv)
    xh = xv * r
    dyg = dy * g
    dx = r * (dyg - xh * jnp.mean(dyg * xh, axis=-1, keepdims=True))
    return dx, dy * xh


class _Job:
    def __init__(self, inputs, out_shapes, aliases, sem_shape, start, finish):
        self.inputs, self.out_shapes, self.aliases, self.sem_shape = list(inputs), list(out_shapes), dict(aliases), sem_shape
        self.start, self.finish = start, finish


def _pallas(body, operands, *, name, grid, in_specs, out_specs, out_shape, scratch_shapes=(), semantics=None, jobs=(),
            prefetch=None):
    n_in, n_out, n_scr = len(in_specs), len(out_specs), len(scratch_shapes)
    n_pre = 0 if prefetch is None else 1
    job_in = [a for job in jobs for a in job.inputs]
    job_out = [s for job in jobs for s in job.out_shapes]
    aliases, i0, o0 = {}, n_pre + n_in, n_out
    for job in jobs:
        aliases.update({i0 + i: o0 + o for i, o in job.aliases.items()})
        i0, o0 = i0 + len(job.inputs), o0 + len(job.out_shapes)

    def whole(*refs):
        pre, refs = refs[:n_pre], refs[n_pre:]
        ins, j_ins = refs[:n_in], refs[n_in:n_in + len(job_in)]
        outs = refs[n_in + len(job_in):][:n_out]
        j_outs = refs[n_in + len(job_in) + n_out:][:len(job_out)]
        rest = refs[n_in + len(job_in) + n_out + len(job_out):]
        scr, sems = rest[:n_scr], rest[n_scr:]

        def run(phase):
            i, o = 0, 0
            for k, job in enumerate(jobs):
                getattr(job, phase)(j_ins[i:i + len(job.inputs)], j_outs[o:o + len(job.out_shapes)], sems[2 * k], sems[2 * k + 1])
                i, o = i + len(job.inputs), o + len(job.out_shapes)

        def at(step_of, phase):
            if not jobs:
                return
            if not grid:
                run(phase)
                return
            cond = functools.reduce(jnp.logical_and, [pl.program_id(d) == step_of(d) for d in range(len(grid))])
            pl.when(cond)(functools.partial(run, phase))

        at(lambda d: 0, "start")
        body(*pre, *ins, *outs, *scr)
        at(lambda d: grid[d] - 1, "finish")

    layout = dict(grid=grid, in_specs=list(in_specs) + [ANY] * len(job_in), out_specs=list(out_specs) + [ANY] * len(job_out),
                  scratch_shapes=list(scratch_shapes) + [pltpu.SemaphoreType.DMA(job.sem_shape) for job in jobs for _ in range(2)])
    if prefetch is not None:
        layout = dict(grid_spec=pltpu.PrefetchScalarGridSpec(num_scalar_prefetch=1, **layout))
    res = pl.pallas_call(
        whole, name=name, out_shape=list(out_shape) + job_out, input_output_aliases=aliases,
        compiler_params=_params(dimension_semantics=semantics, has_side_effects=bool(jobs)), **layout,
    )(*([] if prefetch is None else [prefetch]), *operands, *job_in)
    per_job, o = [], n_out
    for job in jobs:
        per_job.append(res[o:o + len(job.out_shapes)])
        o += len(job.out_shapes)
    return res[:n_out], per_job


def _run_jobs(jobs, name):
    return _pallas(lambda: None, [], name=name, grid=(), in_specs=[], out_specs=[], out_shape=[], jobs=jobs)[1]


NORM_ROWS = 256
EPILOGUE_ROWS = 512


def _norm_rows(x_ref, g_ref, h_ref):
    g = g_ref[...]

    def rows(i, carry):
        r = pl.ds(pl.multiple_of(i * NORM_ROWS, NORM_ROWS), NORM_ROWS)
        xv = x_ref[r, :]
        h_ref[r, :] = (xv * _rms_scale(xv) * g).astype(BF16)
        return carry

    lax.fori_loop(0, x_ref.shape[0] // NORM_ROWS, rows, 0)


def _norm_matmul(x, g, w, *, tm, tn, name, jobs=()):
    T, K = x.shape
    N = w.shape[1]

    def body(x_ref, g_ref, w_ref, o_ref, h_ref):
        @pl.when(pl.program_id(1) == 0)
        def _():
            _norm_rows(x_ref, g_ref, h_ref)

        o_ref[...] = _dot(h_ref[...], w_ref[...])

    return _pallas(
        body, (x, g, w), name=name, grid=(T // tm, N // tn),
        in_specs=[pl.BlockSpec((tm, K), lambda i, j: (i, 0)), pl.BlockSpec((1, K), lambda i, j: (0, 0)),
                  pl.BlockSpec((K, tn), lambda i, j: (0, j))],
        out_specs=[pl.BlockSpec((tm, tn), lambda i, j: (i, j)), pl.BlockSpec((tm, K), lambda i, j: (i, 0))],
        out_shape=[jax.ShapeDtypeStruct((T, N), F32), jax.ShapeDtypeStruct((T, K), BF16)],
        semantics=("parallel", "arbitrary"), jobs=jobs)


def _ffn_in(x2, g, w, *, tm, tn, jobs=()):
    T, K = x2.shape
    nb = D_FF // tn

    def body(x_ref, g_ref, wg_ref, wu_ref, dup_ref, dgate_ref, act_ref, h_ref):
        @pl.when(pl.program_id(1) == 0)
        def _():
            _norm_rows(x_ref, g_ref, h_ref)

        wg, wu = wg_ref[...], wu_ref[...]
        for r in range(0, tm, EPILOGUE_ROWS):
            rows = pl.ds(r, min(EPILOGUE_ROWS, tm))
            h = h_ref[rows, :]
            gate, up = _dot(h, wg), _dot(h, wu)
            s = _sigmoid(gate)
            silu = gate * s
            dup_ref[rows, :] = silu.astype(BF16)
            dgate_ref[rows, :] = (up * (s + silu * (1.0 - s))).astype(BF16)
            act_ref[rows, :] = (silu * up).astype(BF16)

    blk = pl.BlockSpec((tm, tn), lambda i, j: (i, j))
    return _pallas(
        body, (x2, g, w, w), name="ffn_in", grid=(T // tm, nb),
        in_specs=[pl.BlockSpec((tm, K), lambda i, j: (i, 0)), pl.BlockSpec((1, K), lambda i, j: (0, 0)),
                  pl.BlockSpec((K, tn), lambda i, j: (0, j)), pl.BlockSpec((K, tn), lambda i, j: (0, j + nb))],
        out_specs=[blk, blk, blk, pl.BlockSpec((tm, K), lambda i, j: (i, 0))],
        out_shape=[jax.ShapeDtypeStruct((T, D_FF), BF16), jax.ShapeDtypeStruct((T, D_FF), BF16),
                   jax.ShapeDtypeStruct((T, D_FF), BF16), jax.ShapeDtypeStruct((T, K), BF16)],
        semantics=("parallel", "arbitrary"), jobs=jobs)


def _branch_mix(pm, z, w_pool_out, w_rnn_out, proj, *, tm, tn):
    T = pm.shape[0]
    col_gp = (D_POOL + 2 * D_RNN) // tn
    col_gr = col_gp + D_MODEL // tn

    def body(pm_ref, z_ref, wp_ref, wr_ref, gp_ref, gr_ref, by_gp_ref, by_gr_ref, sp_ref, sr_ref, mix_ref):
        wp, wr = wp_ref[...], wr_ref[...]
        for r in range(0, tm, EPILOGUE_ROWS):
            rows = pl.ds(r, min(EPILOGUE_ROWS, tm))
            yp, yr = _dot(pm_ref[rows, :], wp), _dot(z_ref[rows, :], wr)
            sp, sr = _sigmoid(gp_ref[rows, :]), _sigmoid(gr_ref[rows, :])
            by_gp_ref[rows, :] = (yp * sp * (1.0 - sp)).astype(BF16)
            by_gr_ref[rows, :] = (yr * sr * (1.0 - sr)).astype(BF16)
            sp_ref[rows, :] = sp.astype(BF16)
            sr_ref[rows, :] = sr.astype(BF16)
            mix_ref[rows, :] = (sp * yp + sr * yr).astype(BF16)

    blk = pl.BlockSpec((tm, tn), lambda i, j: (i, j))
    out = jax.ShapeDtypeStruct((T, D_MODEL), BF16)
    return pl.pallas_call(
        body, name="branch_mix", grid=(T // tm, D_MODEL // tn),
        in_specs=[pl.BlockSpec((tm, D_POOL), lambda i, j: (i, 0)), pl.BlockSpec((tm, D_RNN), lambda i, j: (i, 0)),
                  pl.BlockSpec((D_POOL, tn), lambda i, j: (0, j)), pl.BlockSpec((D_RNN, tn), lambda i, j: (0, j)),
                  pl.BlockSpec((tm, tn), lambda i, j: (i, col_gp + j)), pl.BlockSpec((tm, tn), lambda i, j: (i, col_gr + j))],
        out_specs=[blk] * 5, out_shape=[out] * 5,
        compiler_params=_params(dimension_semantics=("parallel", "parallel")),
    )(pm, z, w_pool_out, w_rnn_out, proj, proj)


def _out_proj_residual(mix, w_o, x, *, tm):
    T = x.shape[0]

    def body(mix_ref, w_ref, x_ref, o_ref):
        o_ref[...] = x_ref[...] + _dot(mix_ref[...], w_ref[...])

    row = pl.BlockSpec((tm, D_MODEL), lambda i: (i, 0))
    return pl.pallas_call(
        body, name="out_proj_residual", grid=(T // tm,),
        in_specs=[row, pl.BlockSpec((D_MODEL, D_MODEL), lambda i: (0, 0)), row],
        out_specs=row, out_shape=jax.ShapeDtypeStruct((T, D_MODEL), F32),
        compiler_params=_params(dimension_semantics=("parallel",)),
    )(mix, w_o, x)


def _ffn_out_loss(act, w, x2, g3, target, *, tm):
    T = x2.shape[0]

    def body(act_ref, w_ref, x2_ref, g_ref, t_ref, dx_ref, dxb_ref, sq_ref, dg_ref):
        @pl.when(pl.program_id(0) == 0)
        def _():
            sq_ref[...] = jnp.zeros_like(sq_ref)
            dg_ref[...] = jnp.zeros_like(dg_ref)

        g, w = g_ref[...], w_ref[...]
        for r in range(0, tm, NORM_ROWS):
            rows = pl.ds(r, min(NORM_ROWS, tm))
            x3 = x2_ref[rows, :] + _dot(act_ref[rows, :], w)
            err = x3 * _rms_scale(x3) * g - t_ref[rows, :]
            sq_ref[...] += jnp.sum(err * err, axis=0, keepdims=True)
            dx, dgp = _rms_bwd(err * (1.0 / D_MODEL), x3, g)
            dg_ref[...] += jnp.sum(dgp, axis=0, keepdims=True)
            dx_ref[rows, :] = dx
            dxb_ref[rows, :] = dx.astype(BF16)

    row = pl.BlockSpec((tm, D_MODEL), lambda i: (i, 0))
    vec = pl.BlockSpec((1, D_MODEL), lambda i: (0, 0))
    return pl.pallas_call(
        body, name="ffn_out_loss", grid=(T // tm,),
        in_specs=[pl.BlockSpec((tm, D_FF), lambda i: (i, 0)), pl.BlockSpec((D_FF, D_MODEL), lambda i: (0, 0)), row, vec, row],
        out_specs=[row, row, vec, vec],
        out_shape=[jax.ShapeDtypeStruct((T, D_MODEL), F32), jax.ShapeDtypeStruct((T, D_MODEL), BF16),
                   jax.ShapeDtypeStruct((1, D_MODEL), F32), jax.ShapeDtypeStruct((1, D_MODEL), F32)],
        compiler_params=_params(dimension_semantics=("arbitrary",)),
    )(act, w, x2, g3, target)


def _ffn_out_bwd(dx3b, w, act_by_gate, act_by_up, *, tm, tn):
    T = dx3b.shape[0]

    def body(dx_ref, w_ref, by_gate_ref, by_up_ref, dgate_ref, dup_ref):
        w = w_ref[...]
        for r in range(0, tm, EPILOGUE_ROWS):
            rows = pl.ds(r, min(EPILOGUE_ROWS, tm))
            dact = _dot_nt(dx_ref[rows, :], w)
            dgate_ref[rows, :] = (dact * by_gate_ref[rows, :].astype(F32)).astype(BF16)
            dup_ref[rows, :] = (dact * by_up_ref[rows, :].astype(F32)).astype(BF16)

    blk = pl.BlockSpec((tm, tn), lambda i, j: (i, j))
    return pl.pallas_call(
        body, name="ffn_out_bwd", grid=(T // tm, D_FF // tn),
        in_specs=[pl.BlockSpec((tm, D_MODEL), lambda i, j: (i, 0)), pl.BlockSpec((tn, D_MODEL), lambda i, j: (j, 0)), blk, blk],
        out_specs=[blk, blk],
        out_shape=[jax.ShapeDtypeStruct((T, D_FF), BF16), jax.ShapeDtypeStruct((T, D_FF), BF16)],
        compiler_params=_params(dimension_semantics=("parallel", "parallel")),
    )(dx3b, w, act_by_gate, act_by_up)


def _ffn_in_bwd(dgate, dup, w, dx3, x2, g2, *, tm, jobs=()):
    T = x2.shape[0]

    def body(dgate_ref, dup_ref, w_ref, dx3_ref, x2_ref, g_ref, dx_ref, dxb_ref, dg_ref):
        @pl.when(pl.program_id(0) == 0)
        def _():
            dg_ref[...] = jnp.zeros_like(dg_ref)

        g = g_ref[...]
        for r in range(0, tm, NORM_ROWS):
            rows = pl.ds(r, min(NORM_ROWS, tm))
            dh = _dot_nt(dgate_ref[rows, :], w_ref[:, :D_FF]) + _dot_nt(dup_ref[rows, :], w_ref[:, D_FF:])
            dxn, dgp = _rms_bwd(dh, x2_ref[rows, :], g)
            dx = dx3_ref[rows, :] + dxn
            dg_ref[...] += jnp.sum(dgp, axis=0, keepdims=True)
            dx_ref[rows, :] = dx
            dxb_ref[rows, :] = dx.astype(BF16)

    row = pl.BlockSpec((tm, D_MODEL), lambda i: (i, 0))
    wide = pl.BlockSpec((tm, D_FF), lambda i: (i, 0))
    vec = pl.BlockSpec((1, D_MODEL), lambda i: (0, 0))
    return _pallas(
        body, (dgate, dup, w, dx3, x2, g2), name="ffn_in_bwd", grid=(T // tm,),
        in_specs=[wide, wide, pl.BlockSpec((D_MODEL, 2 * D_FF), lambda i: (0, 0)), row, row, vec],
        out_specs=[row, row, vec],
        out_shape=[jax.ShapeDtypeStruct((T, D_MODEL), F32), jax.ShapeDtypeStruct((T, D_MODEL), BF16),
                   jax.ShapeDtypeStruct((1, D_MODEL), F32)],
        semantics=("arbitrary",), jobs=jobs)


def _out_proj_bwd(dx2b, w_o, mix_by, *, tm, tn, jobs=()):
    T = dx2b.shape[0]

    def body(dx_ref, w_ref, *refs):
        w = w_ref[...]
        for r in range(0, tm, EPILOGUE_ROWS):
            rows = pl.ds(r, min(EPILOGUE_ROWS, tm))
            dmix = _dot_nt(dx_ref[rows, :], w)
            for by_ref, d_ref in zip(refs[:4], refs[4:]):
                d_ref[rows, :] = (dmix * by_ref[rows, :].astype(F32)).astype(BF16)

    blk = pl.BlockSpec((tm, tn), lambda i, j: (i, j))
    out = jax.ShapeDtypeStruct((T, D_MODEL), BF16)
    return _pallas(
        body, (dx2b, w_o, *mix_by), name="out_proj_bwd", grid=(T // tm, D_MODEL // tn),
        in_specs=[pl.BlockSpec((tm, D_MODEL), lambda i, j: (i, 0)), pl.BlockSpec((tn, D_MODEL), lambda i, j: (j, 0))] + [blk] * 4,
        out_specs=[blk] * 4, out_shape=[out] * 4, semantics=("parallel", "parallel"), jobs=jobs)


def _branch_bwd(dyp, dyr, w_pool_out, w_rnn_out, *, tm):
    T = dyp.shape[0]

    def body(dyp_ref, dyr_ref, wp_ref, wr_ref, dpm_ref, dz_ref):
        dpm_ref[...] = _dot_nt(dyp_ref[...], wp_ref[...])
        dz_ref[...] = _dot_nt(dyr_ref[...], wr_ref[...])

    row = pl.BlockSpec((tm, D_MODEL), lambda i: (i, 0))
    return pl.pallas_call(
        body, name="branch_bwd", grid=(T // tm,),
        in_specs=[row, row, pl.BlockSpec((D_POOL, D_MODEL), lambda i: (0, 0)), pl.BlockSpec((D_RNN, D_MODEL), lambda i: (0, 0))],
        out_specs=[pl.BlockSpec((tm, D_POOL), lambda i: (i, 0)), pl.BlockSpec((tm, D_RNN), lambda i: (i, 0))],
        out_shape=[jax.ShapeDtypeStruct((T, D_POOL), F32), jax.ShapeDtypeStruct((T, D_RNN), F32)],
        compiler_params=_params(dimension_semantics=("parallel",)),
    )(dyp, dyr, w_pool_out, w_rnn_out)


def _in_proj_bwd(segs, w, dx2, x, g1, *, tm, jobs=()):
    T = x.shape[0]
    widths = [s.shape[1] for s in segs]
    offs = [sum(widths[:k]) for k in range(len(widths))]
    n = len(segs)

    def body(*refs):
        seg_refs, (w_ref, dx2_ref, x_ref, g_ref, dx_ref, dg_ref) = refs[:n], refs[n:]

        @pl.when(pl.program_id(0) == 0)
        def _():
            dg_ref[...] = jnp.zeros_like(dg_ref)

        g = g_ref[...]
        for r in range(0, tm, NORM_ROWS):
            rows = pl.ds(r, min(NORM_ROWS, tm))
            dh = _dot_nt(seg_refs[0][rows, :], w_ref[:, offs[0]:offs[0] + widths[0]])
            for k in range(1, n):
                dh += _dot_nt(seg_refs[k][rows, :], w_ref[:, offs[k]:offs[k] + widths[k]])
            dxn, dgp = _rms_bwd(dh, x_ref[rows, :], g)
            dg_ref[...] += jnp.sum(dgp, axis=0, keepdims=True)
            dx_ref[rows, :] = dx2_ref[rows, :] + dxn

    row = pl.BlockSpec((tm, D_MODEL), lambda i: (i, 0))
    vec = pl.BlockSpec((1, D_MODEL), lambda i: (0, 0))
    return _pallas(
        body, (*segs, w, dx2, x, g1), name="in_proj_bwd", grid=(T // tm,),
        in_specs=[pl.BlockSpec((tm, wd), lambda i: (i, 0)) for wd in widths]
        + [pl.BlockSpec((D_MODEL, D_IN), lambda i: (0, 0)), row, row, vec],
        out_specs=[row, vec],
        out_shape=[jax.ShapeDtypeStruct((T, D_MODEL), F32), jax.ShapeDtypeStruct((1, D_MODEL), F32)],
        semantics=("arbitrary",), jobs=jobs)


def _weight_grad(a, segs, *, tm, tn, name, jobs=None):
    T, M = a.shape
    nblk = [s.shape[1] // tn for s in segs]
    first = [sum(nblk[:k]) for k in range(len(segs))]
    n = len(segs)

    def body(a_ref, *refs):
        seg_refs, o_ref = refs[:n], refs[n]
        j = pl.program_id(1)
        for k in range(n):
            @pl.when((j >= first[k]) & (j < first[k] + nblk[k]))
            def _(k=k):
                o_ref[...] = _dot_tn(a_ref[...], seg_refs[k][...])

    def seg_spec(k):
        return pl.BlockSpec((T, tn), lambda i, j: (0, jnp.clip(j - first[k], 0, nblk[k] - 1)))

    (grad,), results = _pallas(
        body, (a, *segs), name=name, grid=(M // tm, sum(nblk)),
        in_specs=[pl.BlockSpec((T, tm), lambda i, j: (0, i))] + [seg_spec(k) for k in range(n)],
        out_specs=[pl.BlockSpec((tm, tn), lambda i, j: (i, j))],
        out_shape=[jax.ShapeDtypeStruct((M, sum(nblk) * tn), F32)],
        semantics=("parallel", "arbitrary"), jobs=jobs or ())
    return grad if jobs is None else (grad, results)


def _pad_front(dst, src, halo):
    dst[pl.ds(0, halo), :] = jnp.zeros((halo, src.shape[1]), F32)

    def fill(i, carry):
        r0 = pl.multiple_of(i * CHUNK, CHUNK)
        dst[pl.ds(r0 + halo, CHUNK), :] = src[pl.ds(r0, CHUNK), :]
        return carry

    lax.fori_loop(0, src.shape[0] // CHUNK, fill, 0)


def _shift_rows(v, k):
    return pltpu.roll(v, k % v.shape[0], axis=0)


def _window_sums(xs, direction):
    s2 = xs + _shift_rows(xs, direction)
    s4 = s2 + _shift_rows(s2, 2 * direction)
    s8 = s4 + _shift_rows(s4, 4 * direction)
    s16 = s8 + _shift_rows(s8, 8 * direction)
    return s2, s4, s8, s16


def _select_window(g, sums):
    s2, s4, s8, s16 = sums
    return jnp.where(g == 0, s2, jnp.where(g == 1, s4, jnp.where(g == 2, s8, s16)))


def _pool_count(g, start, rows):
    t = start + lax.broadcasted_iota(jnp.int32, (rows, 1), 0)
    return jnp.minimum(t + 1, jnp.left_shift(2, g)).astype(F32)


def _pool_fwd(proj, w_grp, scale):
    T = proj.shape[0]
    nchunk = T // CHUNK

    def body(u_ref, w_ref, s_ref, o_ref, upad):
        g = pl.program_id(0)
        _pad_front(upad, u_ref, POOL_HALO)
        w = w_ref[...].astype(BF16)
        scale_row = s_ref[...]

        def chunk(i, carry):
            r0 = pl.multiple_of(i * CHUNK, CHUNK)
            xs = upad[pl.ds(r0, CHUNK + POOL_HALO), :]
            win = _select_window(g, _window_sums(xs, 1))[POOL_HALO:]
            pooled = win / _pool_count(g, r0, CHUNK) - xs[POOL_HALO:]
            o_ref[pl.ds(r0, CHUNK), :] = (_dot(pooled.astype(BF16), w) * scale_row).astype(BF16)
            return carry

        lax.fori_loop(0, nchunk, chunk, 0)

    return pl.pallas_call(
        body, name="pool_fwd", grid=(N_POOL_GROUPS,),
        in_specs=[pl.BlockSpec((T, HEAD), lambda g: (0, g)), pl.BlockSpec((None, HEAD, HEAD), lambda g: (g, 0, 0)),
                  pl.BlockSpec((1, HEAD), lambda g: (0, g))],
        out_specs=pl.BlockSpec((T, HEAD), lambda g: (0, g)),
        out_shape=jax.ShapeDtypeStruct((T, D_POOL), BF16),
        scratch_shapes=[pltpu.VMEM((T + POOL_HALO, HEAD), F32)],
        compiler_params=_params(dimension_semantics=("parallel",)),
    )(proj, w_grp, scale)


def _pool_bwd(proj, dpm, w_grp, scale, jobs=()):
    T = proj.shape[0]
    nchunk = T // CHUNK

    def body(u_ref, dpm_ref, w_ref, s_ref, du_ref, dw_ref, ds_ref, upad, zpad, dpool):
        g = pl.program_id(0)
        _pad_front(upad, u_ref, POOL_HALO)
        zpad[pl.ds(T, POOL_HALO), :] = jnp.zeros((POOL_HALO, HEAD), F32)
        dw_ref[...] = jnp.zeros_like(dw_ref)
        ds_ref[...] = jnp.zeros_like(ds_ref)
        w = w_ref[...].astype(BF16)
        scale_row = s_ref[...]

        def chunk(i, carry):
            r0 = pl.multiple_of(i * CHUNK, CHUNK)
            xs = upad[pl.ds(r0, CHUNK + POOL_HALO), :]
            cnt = _pool_count(g, r0, CHUNK)
            pooled = (_select_window(g, _window_sums(xs, 1))[POOL_HALO:] / cnt - xs[POOL_HALO:]).astype(BF16)
            mixed = _dot(pooled, w)
            d = dpm_ref[pl.ds(r0, CHUNK), :]
            ds_ref[...] += jnp.sum(d * mixed, axis=0, keepdims=True)
            dmixed = (d * scale_row).astype(BF16)
            dw_ref[...] += _dot_tn(pooled, dmixed)
            dp = _dot_nt(dmixed, w)
            dpool[pl.ds(r0, CHUNK), :] = dp
            zpad[pl.ds(r0, CHUNK), :] = dp / cnt
            return carry

        lax.fori_loop(0, nchunk, chunk, 0)

        def chunk2(i, carry):
            r0 = pl.multiple_of(i * CHUNK, CHUNK)
            zs = zpad[pl.ds(r0, CHUNK + POOL_HALO), :]
            win = _select_window(g, _window_sums(zs, -1))[:CHUNK]
            du_ref[pl.ds(r0, CHUNK), :] = (win - dpool[pl.ds(r0, CHUNK), :]).astype(BF16)
            return carry

        lax.fori_loop(0, nchunk, chunk2, 0)

    col = pl.BlockSpec((T, HEAD), lambda g: (0, g))
    return _pallas(
        body, (proj, dpm, w_grp, scale), name="pool_bwd", grid=(N_POOL_GROUPS,),
        in_specs=[col, col, pl.BlockSpec((None, HEAD, HEAD), lambda g: (g, 0, 0)), pl.BlockSpec((1, HEAD), lambda g: (0, g))],
        out_specs=[col, pl.BlockSpec((None, HEAD, HEAD), lambda g: (g, 0, 0)), pl.BlockSpec((1, HEAD), lambda g: (0, g))],
        out_shape=[jax.ShapeDtypeStruct((T, D_POOL), BF16), jax.ShapeDtypeStruct((N_POOL_GROUPS, HEAD, HEAD), F32),
                   jax.ShapeDtypeStruct((1, D_POOL), F32)],
        scratch_shapes=[pltpu.VMEM((T + POOL_HALO, HEAD), F32), pltpu.VMEM((T + POOL_HALO, HEAD), F32), pltpu.VMEM((T, HEAD), F32)],
        semantics=("parallel",), jobs=jobs)


def _conv_taps(xs, cw):
    v = cw[CONV_WIDTH - 1] * xs[SUBLANES:]
    for k in range(CONV_WIDTH - 1):
        v += cw[k] * _shift_rows(xs, CONV_WIDTH - 1 - k)[SUBLANES:]
    return v


def _tap_rows(cw_ref):
    return [cw_ref[k:k + 1, :] for k in range(CONV_WIDTH)]


def _softplus_neg(lam):
    return jnp.maximum(-lam, 0.0) + _log1p(jnp.exp(-jnp.abs(lam)))


def _lru_gates(v, wa, ba, wx, bx, sp):
    vb = v.astype(BF16)
    ra = _sigmoid(_dot(vb, wa) + ba)
    ix = _sigmoid(_dot(vb, wx) + bx)
    log_a = -LRU_C * ra * sp
    a = jnp.exp(log_a)
    sq = jnp.sqrt(-jnp.tanh(log_a) * (a * a + 1.0))
    return ra, ix, a, sq


def _row_bcast(v, r):
    return jnp.broadcast_to(v[r:r + 1, :], v.shape)


TILE_BLOCK = 128


def _scan_in_tiles(coef, coef_shift, A_out, B, T, direction):
    order = list(range(SUBLANES)) if direction == 1 else list(range(SUBLANES - 1, -1, -1))
    tiles = min(TILE_BLOCK, T // SUBLANES)
    for base in range(0, T, tiles * SUBLANES):
        def rows(r, base=base):
            return pl.ds(base + r, tiles, stride=SUBLANES)

        A, Bv = coef[rows(order[0] + coef_shift), :], B[rows(order[0]), :]
        A_out[rows(order[0]), :] = A
        for r in order[1:]:
            a = coef[rows(r + coef_shift), :]
            Bv = a * Bv + B[rows(r), :]
            A = a * A
            A_out[rows(r), :] = A
            B[rows(r), :] = Bv


TILES_PER_STEP = 8


def _carry_tiles(A_s, B_s, out, ntile, direction):
    out_row = SUBLANES - 1 if direction == 1 else 0

    def step(k, carry):
        for j in range(TILES_PER_STEP):
            t = k * TILES_PER_STEP + j
            r0 = pl.multiple_of((t if direction == 1 else ntile - 1 - t) * SUBLANES, SUBLANES)
            A, B = A_s[pl.ds(r0, SUBLANES), :], B_s[pl.ds(r0, SUBLANES), :]
            out[pl.ds(r0, SUBLANES), :] = A * carry + B
            carry = _row_bcast(A, out_row) * carry + _row_bcast(B, out_row)
        return carry

    lax.fori_loop(0, ntile // TILES_PER_STEP, step, jnp.zeros((SUBLANES, HEAD), F32))


def _rnn_fwd(proj, conv_w, conv_b, w_a, b_a, w_x, b_x, lam, jobs=()):
    T = proj.shape[0]
    nchunk = T // CHUNK
    ntile = T // SUBLANES

    def body(u_ref, ug_ref, cw_ref, cb_ref, wa_ref, ba_ref, wx_ref, bx_ref, lam_ref,
             h_ref, z_ref, v_ref, ra_ref, ix_ref, a_ref, sq_ref, upad, a_s, b_s):
        _pad_front(upad, u_ref, SUBLANES)
        cw, cb = _tap_rows(cw_ref), cb_ref[...]
        wa, wx = wa_ref[...].astype(BF16), wx_ref[...].astype(BF16)
        ba, bx = ba_ref[...], bx_ref[...]
        sp = _softplus_neg(lam_ref[...])

        def chunk(i, carry):
            rows = pl.ds(pl.multiple_of(i * CHUNK, CHUNK), CHUNK)
            v = _conv_taps(upad[pl.ds(pl.multiple_of(i * CHUNK, CHUNK), CHUNK + SUBLANES), :], cw) + cb
            ra, ix, a, sq = _lru_gates(v, wa, ba, wx, bx, sp)
            v_ref[rows, :], ra_ref[rows, :], ix_ref[rows, :], a_ref[rows, :], sq_ref[rows, :] = v, ra, ix, a, sq
            a_s[rows, :], b_s[rows, :] = a, sq * ix * v
            return carry

        lax.fori_loop(0, nchunk, chunk, 0)
        _scan_in_tiles(a_s, 0, a_s, b_s, T, 1)
        _carry_tiles(a_s, b_s, h_ref, ntile, 1)

        def chunk3(i, carry):
            r0 = pl.multiple_of(i * CHUNK, CHUNK)
            gl, _ = _gelu_parts(ug_ref[pl.ds(r0, CHUNK), :])
            z_ref[pl.ds(r0, CHUNK), :] = (h_ref[pl.ds(r0, CHUNK), :] * gl).astype(BF16)
            return carry

        lax.fori_loop(0, nchunk, chunk3, 0)

    col = pl.BlockSpec((T, HEAD), lambda h: (0, h))
    vec = pl.BlockSpec((1, HEAD), lambda h: (0, h))
    mat = pl.BlockSpec((None, HEAD, HEAD), lambda h: (h, 0, 0))
    return _pallas(
        body, (proj, proj, conv_w, conv_b, w_a, b_a, w_x, b_x, lam), name="rnn_fwd", grid=(N_RNN_HEADS,),
        in_specs=[pl.BlockSpec((T, HEAD), lambda h: (0, COL_RNN + h)), pl.BlockSpec((T, HEAD), lambda h: (0, COL_GATE + h)),
                  pl.BlockSpec((CONV_WIDTH, HEAD), lambda h: (0, h)), vec, mat, vec, mat, vec, vec],
        out_specs=[col] * 7,
        out_shape=[jax.ShapeDtypeStruct((T, D_RNN), F32), jax.ShapeDtypeStruct((T, D_RNN), BF16)]
        + [jax.ShapeDtypeStruct((T, D_RNN), F32)] * 5,
        scratch_shapes=[pltpu.VMEM((T + SUBLANES, HEAD), F32), pltpu.VMEM((T, HEAD), F32), pltpu.VMEM((T, HEAD), F32)],
        semantics=("parallel",), jobs=jobs)


def _rnn_bwd(proj, hr, dz, gates, conv_w, w_a, w_x, lam, jobs=()):
    T = proj.shape[0]
    nchunk = T // CHUNK
    ntile = T // SUBLANES

    def body(u_ref, ug_ref, h_ref, dz_ref, v_ref, ra_ref, ix_ref, a_ref, sq_ref, cw_ref, wa_ref, wx_ref, lam_ref,
             du_ref, dug_ref, dwa_ref, dwx_ref, dba_ref, dbx_ref, dlam_ref, dcb_ref, dcw_ref,
             upad, hpad, apad, g_s, dvpad, ga_s):
        zero_tile = jnp.zeros((SUBLANES, HEAD), F32)
        _pad_front(upad, u_ref, SUBLANES)
        _pad_front(hpad, h_ref, SUBLANES)
        apad[pl.ds(T, SUBLANES), :] = zero_tile
        dvpad[pl.ds(T, SUBLANES), :] = zero_tile
        for ref in (dwa_ref, dwx_ref, dba_ref, dbx_ref, dlam_ref, dcb_ref, dcw_ref):
            ref[...] = jnp.zeros_like(ref)
        cw = _tap_rows(cw_ref)
        wa, wx = wa_ref[...].astype(BF16), wx_ref[...].astype(BF16)
        lam_row = lam_ref[...]
        sp = _softplus_neg(lam_row)

        def chunk(i, carry):
            rows = pl.ds(pl.multiple_of(i * CHUNK, CHUNK), CHUNK)
            apad[rows, :] = a_ref[rows, :]
            gl, dgl = _gelu_parts(ug_ref[rows, :])
            d = dz_ref[rows, :]
            g_s[rows, :] = d * gl
            dug_ref[rows, :] = (d * h_ref[rows, :] * dgl).astype(BF16)
            return carry

        lax.fori_loop(0, nchunk, chunk, 0)

        _scan_in_tiles(apad, 1, ga_s, g_s, T, -1)
        _carry_tiles(ga_s, g_s, g_s, ntile, -1)

        def chunk3(i, carry):
            r0 = pl.multiple_of(i * CHUNK, CHUNK)
            rows = pl.ds(r0, CHUNK)
            g = g_s[rows, :]
            h_prev = _shift_rows(hpad[pl.ds(r0, CHUNK + SUBLANES), :], 1)[SUBLANES:]
            v, ra, ix, sq, a = v_ref[rows, :], ra_ref[rows, :], ix_ref[rows, :], sq_ref[rows, :], a_ref[rows, :]
            d_sq = g * ix * v
            d_ix = g * sq * v
            d_la = a * g * h_prev - d_sq * a * a / sq
            dlam_ref[...] += jnp.sum(d_la * ra, axis=0, keepdims=True)
            d_pa = d_la * (-LRU_C) * sp * ra * (1.0 - ra)
            d_px = d_ix * ix * (1.0 - ix)
            vb, d_pab, d_pxb = v.astype(BF16), d_pa.astype(BF16), d_px.astype(BF16)
            dwa_ref[...] += _dot_tn(vb, d_pab)
            dwx_ref[...] += _dot_tn(vb, d_pxb)
            dba_ref[...] += jnp.sum(d_pa, axis=0, keepdims=True)
            dbx_ref[...] += jnp.sum(d_px, axis=0, keepdims=True)
            dv = g * sq * ix + _dot_nt(d_pab, wa) + _dot_nt(d_pxb, wx)
            dvpad[rows, :] = dv
            dcb_ref[...] += jnp.sum(dv, axis=0, keepdims=True)
            xs = upad[pl.ds(r0, CHUNK + SUBLANES), :]
            for k in range(CONV_WIDTH):
                u_k = _shift_rows(xs, CONV_WIDTH - 1 - k)[SUBLANES:] if k < CONV_WIDTH - 1 else xs[SUBLANES:]
                dcw_ref[k:k + 1, :] += jnp.sum(dv * u_k, axis=0, keepdims=True)
            return carry

        lax.fori_loop(0, nchunk, chunk3, 0)
        dlam_ref[...] = dlam_ref[...] * (LRU_C * _sigmoid(-lam_row))

        def chunk4(i, carry):
            r0 = pl.multiple_of(i * CHUNK, CHUNK)
            dvs = dvpad[pl.ds(r0, CHUNK + SUBLANES), :]
            du = cw[CONV_WIDTH - 1] * dvs[:CHUNK]
            for k in range(CONV_WIDTH - 1):
                du += cw[k] * _shift_rows(dvs, -(CONV_WIDTH - 1 - k))[:CHUNK]
            du_ref[pl.ds(r0, CHUNK), :] = du.astype(BF16)
            return carry

        lax.fori_loop(0, nchunk, chunk4, 0)

    col = pl.BlockSpec((T, HEAD), lambda h: (0, h))
    vec = pl.BlockSpec((1, HEAD), lambda h: (0, h))
    mat = pl.BlockSpec((None, HEAD, HEAD), lambda h: (h, 0, 0))
    taps = pl.BlockSpec((CONV_WIDTH, HEAD), lambda h: (0, h))
    vec_out = jax.ShapeDtypeStruct((1, D_RNN), F32)
    mat_out = jax.ShapeDtypeStruct((N_RNN_HEADS, HEAD, HEAD), F32)
    seq = pltpu.VMEM((T, HEAD), F32)
    seq_pad = pltpu.VMEM((T + SUBLANES, HEAD), F32)
    return _pallas(
        body, (proj, proj, hr, dz, *gates, conv_w, w_a, w_x, lam), name="rnn_bwd", grid=(N_RNN_HEADS,),
        in_specs=[pl.BlockSpec((T, HEAD), lambda h: (0, COL_RNN + h)), pl.BlockSpec((T, HEAD), lambda h: (0, COL_GATE + h))]
        + [col] * 7 + [taps, mat, mat, vec],
        out_specs=[col, col, mat, mat, vec, vec, vec, vec, taps],
        out_shape=[jax.ShapeDtypeStruct((T, D_RNN), BF16), jax.ShapeDtypeStruct((T, D_RNN), BF16), mat_out, mat_out,
                   vec_out, vec_out, vec_out, vec_out, jax.ShapeDtypeStruct((CONV_WIDTH, D_RNN), F32)],
        scratch_shapes=[seq_pad, seq_pad, seq_pad, seq, seq_pad, seq],
        semantics=("parallel",), jobs=jobs)


GROUP_FFN_OUT = ["w_ffn_out"]
GROUP_FFN_IN = ["w_ffn_in"]
GROUP_MIX = ["w_o", "w_pool_out", "w_rnn_out"]
GROUP_IN = ["w_in"]


def _step(x, target, s, full, conv_w, place):
    T = x.shape[0]
    tall, mid, low = min(T, 2048), min(T, 1024), min(T, 512)
    full = dict(full)

    def gathered(names, results):
        full.update(zip(names, results))

    early = ["w_pool_out", "w_rnn_out", "w_o", "w_ffn_out"]
    (proj, h1), (res,) = _norm_matmul(x, s["norm_mix"], full["w_in"], tm=tall, tn=512, name="in_proj", jobs=[_gather_job(full, early)])
    gathered(early, res)
    pm = _pool_fwd(proj, s["w_pool_grp"], s["pool_scale"])
    (hr, z, *gates), (res,) = _rnn_fwd(proj, conv_w, s["conv_b"], s["w_rg_a"], s["b_rg_a"], s["w_rg_x"], s["b_rg_x"],
                                       s["lru_lambda"], jobs=[_gather_job(full, ["w_ffn_in"])])
    gathered(["w_ffn_in"], res)
    *mix_by, mix = _branch_mix(pm, z, full["w_pool_out"], full["w_rnn_out"], proj, tm=tall, tn=256)
    x2 = _out_proj_residual(mix, full["w_o"], x, tm=mid)
    (act_by_up, act_by_gate, act, h2), _ = _ffn_in(x2, s["norm_ffn"], full["w_ffn_in"], tm=tall, tn=256)
    dx3, dx3b, sq_cols, g_norm_final = _ffn_out_loss(act, full["w_ffn_out"], x2, s["norm_final"], target, tm=low)

    g = {"norm_final": g_norm_final}

    def chip_sums(names, from_sibling):
        sums = {name: _chip_sum(name, g[name], got, place) for name, got in zip(names, from_sibling)}
        return {name: v[0] for name, v in sums.items()}, {name: v[1] for name, v in sums.items()}

    def final_sums(names, sums, from_chips):
        return {name: _final_sum(name, sums[name], got, place) for name, got in zip(names, from_chips)}

    dgate, dup = _ffn_out_bwd(dx3b, full["w_ffn_out"], act_by_gate, act_by_up, tm=tall, tn=256)
    g["w_ffn_out"] = _weight_grad(act, [dx3b], tm=256, tn=D_MODEL, name="w_ffn_out_grad")
    (dx2, dx2b, g["norm_ffn"]), (res,) = _ffn_in_bwd(dgate, dup, full["w_ffn_in"], dx3, x2, s["norm_ffn"], tm=low,
                                                     jobs=[_sibling_job(g, GROUP_FFN_OUT)])
    sums_ffn, sums_ffn_bf16 = chip_sums(GROUP_FFN_OUT, res)
    g["w_ffn_in"], (res,) = _weight_grad(h2, [dgate, dup], tm=D_MODEL, tn=256, name="w_ffn_in_grad",
                                         jobs=[_chips_job(sums_ffn_bf16, GROUP_FFN_OUT)])
    shards_ffn = final_sums(GROUP_FFN_OUT, sums_ffn, res)
    (dgp, dgr, dyp, dyr), (res,) = _out_proj_bwd(dx2b, full["w_o"], mix_by, tm=tall, tn=256,
                                                 jobs=[_sibling_job(g, GROUP_FFN_IN)])
    sums_ffn, sums_ffn_bf16 = chip_sums(GROUP_FFN_IN, res)
    g["w_o"] = _weight_grad(mix, [dx2b], tm=D_MODEL, tn=256, name="w_o_grad")
    dpm, dz = _branch_bwd(dyp, dyr, full["w_pool_out"], full["w_rnn_out"], tm=mid)
    g["w_pool_out"] = _weight_grad(pm, [dyp], tm=D_POOL, tn=256, name="w_pool_out_grad")
    g["w_rnn_out"] = _weight_grad(z, [dyr], tm=D_RNN, tn=256, name="w_rnn_out_grad")
    (dupool, g["w_pool_grp"], g["pool_scale"]), (res,) = _pool_bwd(proj, dpm, s["w_pool_grp"], s["pool_scale"],
                                                                   jobs=[_sibling_job(g, GROUP_MIX)])
    sums_mix, sums_mix_bf16 = chip_sums(GROUP_MIX, res)
    ((durnn, dugate, g["w_rg_a"], g["w_rg_x"], g["b_rg_a"], g["b_rg_x"], g["lru_lambda"], g["conv_b"], g["conv_w"]),
     (res,)) = _rnn_bwd(proj, hr, dz, gates, conv_w, s["w_rg_a"], s["w_rg_x"], s["lru_lambda"],
                        jobs=[_chips_job(sums_ffn_bf16, GROUP_FFN_IN)])
    shards_ffn.update(final_sums(GROUP_FFN_IN, sums_ffn, res))
    segs = [dupool, durnn, dugate, dgp, dgr]
    ffn = GROUP_FFN_OUT + GROUP_FFN_IN
    g["w_in"], (res, joined) = _weight_grad(h1, segs, tm=D_MODEL, tn=256, name="w_in_grad",
                                           jobs=[_chips_job(sums_mix_bf16, GROUP_MIX), _join_job(shards_ffn, ffn)])
    grads = dict(zip(ffn, joined))
    shards = final_sums(GROUP_MIX, sums_mix, res)
    (res,) = _run_jobs([_sibling_job(g, GROUP_IN)], "w_in_exchange_sibling")
    sums_in, sums_in_bf16 = chip_sums(GROUP_IN, res)
    (grad_x, g["norm_mix"]), (res,) = _in_proj_bwd(segs, full["w_in"], dx2, x, s["norm_mix"], tm=low,
                                                  jobs=[_chips_job(sums_in_bf16, GROUP_IN)])
    shards.update(final_sums(GROUP_IN, sums_in, res))

    vec_rows = [g[name] if name != "pool_scale" else jnp.pad(g[name], ((0, 0), (0, D_MODEL - D_POOL))) for name in VEC_ITEMS]
    vec_rows += [g["conv_w"], sq_cols, jnp.zeros((VEC_ROWS - len(VEC_ITEMS) - CONV_WIDTH - 1, D_MODEL), F32)]
    vec = jnp.concatenate(vec_rows, axis=0).reshape(VEC_ROWS, N_DEV, HEAD).transpose(1, 0, 2)
    mat = jnp.concatenate([g[name].reshape(-1, HEAD) for name in MAT_ITEMS], axis=0).reshape(N_DEV, -1, HEAD)
    (vec, mat), (joined,) = _all_reduce_small([vec, mat], [False, True], jobs=[_join_job(shards, GROUP_MIX + GROUP_IN)])
    grads.update(zip(GROUP_MIX + GROUP_IN, joined))

    vec = vec.transpose(1, 0, 2).reshape(VEC_ROWS, D_MODEL)
    mat = mat.reshape(-1, HEAD)
    for k, name in enumerate(VEC_ITEMS):
        grads[name] = vec[k:k + 1, :s[name].shape[1]]
    grads["conv_w"] = vec[len(VEC_ITEMS):len(VEC_ITEMS) + CONV_WIDTH]
    row = 0
    for name in MAT_ITEMS:
        rows = s[name].shape[0] * HEAD
        grads[name] = mat[row:row + rows]
        row += rows
    return vec[len(VEC_ITEMS) + CONV_WIDTH], grad_x, grads


LARGE = {"w_in": "col", "w_pool_out": "col", "w_rnn_out": "row", "w_o": "row", "w_ffn_in": "col", "w_ffn_out": "row"}
LARGE_SHAPE = {"w_in": (D_MODEL, D_IN), "w_pool_out": (D_POOL, D_MODEL), "w_rnn_out": (D_RNN, D_MODEL),
               "w_o": (D_MODEL, D_MODEL), "w_ffn_in": (D_MODEL, 2 * D_FF), "w_ffn_out": (D_FF, D_MODEL)}


def _place():
    x, y, c = lax.axis_index("x"), lax.axis_index("y"), lax.axis_index("c")
    return 2 * x + y, c


def _chip_device(chip, c):
    return (chip // 2, chip % 2, c)


def _chip_window(ref, kind, shape, chip, half=None):
    K, N = shape
    if kind == "col":
        rows = slice(None) if half is None else pl.ds(half * (K // 2), K // 2)
        return ref.at[rows, pl.ds(chip * (N // N_CHIPS), N // N_CHIPS)]
    ks = K // N_CHIPS
    if half is None:
        return ref.at[pl.ds(chip * ks, ks), :]
    return ref.at[pl.ds(chip * ks + half * (ks // 2), ks // 2), :]


def _row_half(ref, half):
    rows = ref.shape[0] // 2
    return ref.at[pl.ds(half * rows, rows), :]


def _remote(win_src, win_dst, send_sems, recv_sems, idx, to):
    return pltpu.make_async_remote_copy(src_ref=win_src, dst_ref=win_dst, send_sem=send_sems.at[idx], recv_sem=recv_sems.at[idx],
                                        device_id=to, device_id_type=MESH)


def _gather_job(full, names, conv_w_full=None):
    n = len(names)
    cw_cols = D_RNN // N_CHIPS

    def windows(refs, chip, half):
        return [_chip_window(refs[k], LARGE[name], LARGE_SHAPE[name], chip, half) for k, name in enumerate(names)]

    def ici_copies(refs, send_sems, recv_sems, src_chip, dst_chip, c, r):
        wins = windows(refs, src_chip, c)
        if conv_w_full is not None:
            wins.append(refs[n].at[:, pl.ds(src_chip * cw_cols, cw_cols)])
        return [_remote(win, win, send_sems, recv_sems, (k, r), _chip_device(dst_chip, c)) for k, win in enumerate(wins)]

    def forwards(refs, send_sems, recv_sems, src_chip, half, to_core, chip, r):
        return [_remote(win, win, send_sems, recv_sems, (k, 3 + r), _chip_device(chip, to_core))
                for k, win in enumerate(windows(refs, src_chip, half))]

    def start(ins, outs, send_sems, recv_sems):
        chip, c = _place()
        for r in range(3):
            for cp in ici_copies(outs, send_sems, recv_sems, chip, chip ^ (r + 1), c, r):
                cp.start()

    def finish(ins, outs, send_sems, recv_sems):
        chip, c = _place()
        for r in range(3):
            for cp in ici_copies(outs, send_sems, recv_sems, chip ^ (r + 1), chip, c, r):
                cp.wait_recv()
            for cp in forwards(outs, send_sems, recv_sems, chip ^ (r + 1), c, 1 - c, chip, r):
                cp.start()
        for r in range(3):
            for cp in forwards(outs, send_sems, recv_sems, chip ^ (r + 1), 1 - c, c, chip, r):
                cp.wait_recv()
            for cp in ici_copies(outs, send_sems, recv_sems, chip, chip ^ (r + 1), c, r):
                cp.wait_send()
            for cp in forwards(outs, send_sems, recv_sems, chip ^ (r + 1), c, 1 - c, chip, r):
                cp.wait_send()

    arrays = [full[name] for name in names] + ([conv_w_full] if conv_w_full is not None else [])
    return _Job(arrays, [jax.ShapeDtypeStruct(a.shape, a.dtype) for a in arrays], {k: k for k in range(len(arrays))},
                (len(arrays), 6), start, finish)


def _core_halves(ref, kind, shape, c):
    return [_chip_window(ref, kind, shape, chip, c) for chip in range(N_CHIPS)]


def _sibling_job(grads, names):
    def start(ins, outs, send_sems, recv_sems):
        chip, c = _place()
        for k, name in enumerate(names):
            kind, shape = LARGE[name], LARGE_SHAPE[name]
            if kind == "col":
                pairs = [(_row_half(ins[k], 1 - c), outs[k])]
            else:
                rows = shape[0] // N_DEV
                pairs = [(win, outs[k].at[pl.ds(j * rows, rows), :]) for j, win in enumerate(_core_halves(ins[k], kind, shape, 1 - c))]
            for src, dst in pairs:
                _remote(src, dst, send_sems, recv_sems, k, _chip_device(chip, 1 - c)).start()

    def finish(ins, outs, send_sems, recv_sems):
        chip, c = _place()
        for k in range(len(names)):
            _remote(outs[k], outs[k], send_sems, recv_sems, k, _chip_device(chip, 1 - c)).wait()

    return _Job([grads[name] for name in names],
                [jax.ShapeDtypeStruct((LARGE_SHAPE[name][0] // 2, LARGE_SHAPE[name][1]), F32) for name in names], {},
                (len(names),), start, finish)


def _chip_sum(name, g, got, place):
    kind, (K, N) = LARGE[name], LARGE_SHAPE[name]
    rows = K // N_DEV
    piece_cols = N // N_CHIPS

    def body(place_ref, g_ref, got_ref, o_ref, ob_ref):
        total = g_ref[...] + got_ref[...]
        ob_ref[...] = total.astype(BF16)
        if kind == "col":
            for chip in range(N_CHIPS):
                @pl.when(place_ref[0] == chip)
                def _(chip=chip):
                    o_ref[...] = total[:, chip * piece_cols:(chip + 1) * piece_cols]
        else:
            @pl.when(pl.program_id(0) == place_ref[0])
            def _():
                o_ref[...] = total

    if kind == "col":
        mine = pl.BlockSpec((rows, N), lambda j, place_ref: (j + N_CHIPS * place_ref[1], 0))
        own = pl.BlockSpec((rows, piece_cols), lambda j, place_ref: (j, 0))
    else:
        mine = pl.BlockSpec((rows, N), lambda j, place_ref: (2 * j + place_ref[1], 0))
        own = pl.BlockSpec((rows, N), lambda j, place_ref: (0, 0))
    blk = pl.BlockSpec((rows, N), lambda j, place_ref: (j, 0))
    return pl.pallas_call(
        body, name=name + "_chip_sum",
        grid_spec=pltpu.PrefetchScalarGridSpec(num_scalar_prefetch=1, grid=(N_CHIPS,), in_specs=[mine, blk], out_specs=[own, blk]),
        out_shape=[jax.ShapeDtypeStruct(_piece_shape(name), F32), jax.ShapeDtypeStruct((K // 2, N), BF16)],
        compiler_params=_params(dimension_semantics=("arbitrary",)),
    )(place, g, got)


def _piece(ref, kind, shape, chip):
    K, N = shape
    if kind == "col":
        return ref.at[:, pl.ds(chip * (N // N_CHIPS), N // N_CHIPS)]
    return ref.at[pl.ds(chip * (K // N_DEV), K // N_DEV), :]


def _piece_shape(name):
    kind, (K, N) = LARGE[name], LARGE_SHAPE[name]
    return (K // 2, N // N_CHIPS) if kind == "col" else (K // N_DEV, N)


def _chips_job(sums, names):
    def copies(ins, outs, send_sems, recv_sems):
        chip, c = _place()
        return [_remote(_piece(ins[k], LARGE[name], LARGE_SHAPE[name], chip ^ (r + 1)), outs[k].at[r], send_sems, recv_sems, (k, r),
                        _chip_device(chip ^ (r + 1), c)) for k, name in enumerate(names) for r in range(3)]

    def start(*refs):
        for cp in copies(*refs):
            cp.start()

    def finish(*refs):
        for cp in copies(*refs):
            cp.wait()

    return _Job([sums[name] for name in names], [jax.ShapeDtypeStruct((3,) + _piece_shape(name), BF16) for name in names], {},
                (len(names), 3), start, finish)


def _final_sum(name, chip_sum, got, place):
    rows, cols = _piece_shape(name)

    def body(place_ref, s_ref, got_ref, o_ref):
        o_ref[...] = ((s_ref[...] + got_ref[0].astype(F32)) + got_ref[1].astype(F32)) + got_ref[2].astype(F32)

    mine = pl.BlockSpec((rows, cols), lambda i, place_ref: (0, 0))
    return pl.pallas_call(
        body, name=name + "_final_sum",
        grid_spec=pltpu.PrefetchScalarGridSpec(
            num_scalar_prefetch=1, grid=(1,), in_specs=[mine, pl.BlockSpec((3, rows, cols), lambda i, place_ref: (0, 0, 0))],
            out_specs=pl.BlockSpec((rows, cols), lambda i, place_ref: (place_ref[1], 0))),
        out_shape=jax.ShapeDtypeStruct((2 * rows, cols), F32),
        compiler_params=_params(dimension_semantics=("arbitrary",)),
    )(place, chip_sum, got)


def _join_job(shards, names):
    def half_copy(outs, send_sems, recv_sems, k, mine):
        chip, c = _place()
        win = _row_half(outs[k], c if mine else 1 - c)
        return _remote(win, win, send_sems, recv_sems, k, _chip_device(chip, 1 - c))

    def start(ins, outs, send_sems, recv_sems):
        for k in range(len(names)):
            half_copy(outs, send_sems, recv_sems, k, True).start()

    def finish(ins, outs, send_sems, recv_sems):
        for k in range(len(names)):
            half_copy(outs, send_sems, recv_sems, k, True).wait_send()
            half_copy(outs, send_sems, recv_sems, k, False).wait_recv()

    arrays = [shards[name] for name in names]
    return _Job(arrays, [jax.ShapeDtypeStruct(a.shape, F32) for a in arrays], {k: k for k in range(len(arrays))},
                (len(arrays),), start, finish)


VEC_ROWS = 16


def _all_reduce_small(slabs, narrow, jobs=()):
    n = len(slabs)
    narrowed = [k for k in range(n) if narrow[k]]

    def body(*refs):
        in_refs, out_refs, got_refs = refs[:n], refs[n:2 * n], refs[2 * n:3 * n]
        bf16_refs = dict(zip(narrowed, refs[3 * n:3 * n + len(narrowed)]))
        send_sems, recv_sems = refs[3 * n + len(narrowed):]
        x, y, c = lax.axis_index("x"), lax.axis_index("y"), lax.axis_index("c")
        me = 4 * x + 2 * y + c
        for k, ref in bf16_refs.items():
            ref[...] = in_refs[k][...].astype(BF16)
        partial_refs = [bf16_refs.get(k, in_refs[k]) for k in range(n)]

        def remote(src, dst, k, phase, r):
            other = me ^ r
            return pltpu.make_async_remote_copy(src_ref=src, dst_ref=dst, send_sem=send_sems.at[k, phase, r],
                                                recv_sem=recv_sems.at[k, phase, r],
                                                device_id=(other // 4, (other // 2) % 2, other % 2), device_id_type=MESH)

        scatter = [remote(partial_refs[k].at[me ^ r], got_refs[k].at[r], k, 0, r) for r in range(1, N_DEV) for k in range(n)]
        for cp in scatter:
            cp.start()
        for cp in scatter:
            cp.wait()
        for k in range(n):
            total = in_refs[k][me]
            for r in range(1, N_DEV):
                total = total + got_refs[k][r].astype(F32)
            out_refs[k][me] = total
        gather = [remote(out_refs[k].at[me], out_refs[k].at[me], k, 1, r) for r in range(1, N_DEV) for k in range(n)]
        for cp in gather:
            cp.start()
        for r in range(1, N_DEV):
            for k in range(n):
                remote(out_refs[k].at[me ^ r], out_refs[k].at[me ^ r], k, 1, r).wait_recv()
        for cp in gather:
            cp.wait_send()

    return _pallas(
        body, slabs, name="all_reduce_small", grid=(), in_specs=[VMEM] * n, out_specs=[VMEM] * n,
        out_shape=[jax.ShapeDtypeStruct(s.shape, F32) for s in slabs],
        scratch_shapes=[pltpu.VMEM(s.shape, BF16 if narrow[k] else F32) for k, s in enumerate(slabs)]
        + [pltpu.VMEM(slabs[k].shape, BF16) for k in narrowed]
        + [pltpu.SemaphoreType.DMA((n, 2, N_DEV)), pltpu.SemaphoreType.DMA((n, 2, N_DEV))], jobs=jobs)


def _cast_into_whole(w, name, place):
    rows, cols = w.shape
    tr = rows // 2

    def body(place_ref, w_ref, o_ref):
        o_ref[...] = w_ref[...].astype(BF16)

    if LARGE[name] == "col":
        window = pl.BlockSpec((tr, cols), lambda i, place_ref: (i, place_ref[0]))
    else:
        window = pl.BlockSpec((tr, cols), lambda i, place_ref: (2 * place_ref[0] + i, 0))
    return pl.pallas_call(
        body, name=name + "_cast",
        grid_spec=pltpu.PrefetchScalarGridSpec(num_scalar_prefetch=1, grid=(2,),
                                               in_specs=[pl.BlockSpec((tr, cols), lambda i, place_ref: (i, 0))], out_specs=window),
        out_shape=jax.ShapeDtypeStruct(LARGE_SHAPE[name], BF16),
        compiler_params=_params(dimension_semantics=("parallel",)))(place, w)


def _cast_many_into_whole(shards, place, jobs):
    names = list(shards)
    n = len(names)

    def body(place_ref, *refs):
        for w_ref, o_ref in zip(refs[:n], refs[n:]):
            o_ref[...] = w_ref[...].astype(BF16)

    def window(name):
        rows, cols = shards[name].shape
        if LARGE[name] == "col":
            return pl.BlockSpec((rows // 2, cols), lambda i, place_ref: (i, place_ref[0]))
        return pl.BlockSpec((rows // 2, cols), lambda i, place_ref: (2 * place_ref[0] + i, 0))

    def half(name):
        rows, cols = shards[name].shape
        return pl.BlockSpec((rows // 2, cols), lambda i, place_ref: (i, 0))

    return _pallas(body, [shards[name] for name in names], name="cast_weights", grid=(2,),
                   in_specs=[half(name) for name in names], out_specs=[window(name) for name in names],
                   out_shape=[jax.ShapeDtypeStruct(LARGE_SHAPE[name], BF16) for name in names],
                   semantics=("arbitrary",), jobs=jobs, prefetch=place)


def _adamw_math(w, g, m, v):
    m = ADAM_B1 * m + (1.0 - ADAM_B1) * g
    v = ADAM_B2 * v + (1.0 - ADAM_B2) * (g * g)
    m_hat = m / (1.0 - ADAM_B1 ** ADAM_STEP)
    v_hat = v / (1.0 - ADAM_B2 ** ADAM_STEP)
    delta = -ADAM_LR * (m_hat / (jnp.sqrt(v_hat) + ADAM_EPS) + ADAM_WD * w)
    return delta, m, v


def _adamw_large(w, g, m, v, name):
    rows, cols = w.shape
    steps = 2
    tr = rows // steps

    def body(w_ref, g_ref, m_ref, v_ref, d_ref, mo_ref, vo_ref):
        d_ref[...], mo_ref[...], vo_ref[...] = _adamw_math(w_ref[...], g_ref[...], m_ref[...], v_ref[...])

    blk = pl.BlockSpec((tr, cols), lambda i: (i, 0))
    out = jax.ShapeDtypeStruct(w.shape, F32)
    return pl.pallas_call(body, name=name + "_adamw", grid=(steps,), in_specs=[blk] * 4, out_specs=[blk] * 3, out_shape=[out] * 3,
                          compiler_params=_params(dimension_semantics=("parallel",)))(w, g, m, v)


def _adamw_small(ws, gs, ms, vs):
    n = len(ws)

    def body(*refs):
        for k in range(n):
            w_ref, g_ref, m_ref, v_ref = (refs[q * n + k] for q in range(4))
            d_ref, mo_ref, vo_ref = (refs[(4 + q) * n + k] for q in range(3))
            d_ref[...], mo_ref[...], vo_ref[...] = _adamw_math(w_ref[...], g_ref[...], m_ref[...], v_ref[...])

    out = [jax.ShapeDtypeStruct(w.shape, F32) for w in ws]
    res = pl.pallas_call(body, name="small_adamw", in_specs=[VMEM] * (4 * n), out_specs=[VMEM] * (3 * n), out_shape=out * 3,
                         compiler_params=_params())(*ws, *gs, *ms, *vs)
    return res[:n], res[n:2 * n], res[2 * n:]


WEIGHTS = ["norm_mix", "w_in", "w_pool_grp", "pool_scale", "w_pool_out", "conv_w", "conv_b", "w_rg_a", "b_rg_a", "w_rg_x",
           "b_rg_x", "lru_lambda", "w_rnn_out", "w_o", "norm_ffn", "w_ffn_in", "w_ffn_out", "norm_final"]
VEC_ITEMS = ["norm_mix", "norm_ffn", "norm_final", "pool_scale", "conv_b", "lru_lambda", "b_rg_a", "b_rg_x"]
MAT_ITEMS = ["w_pool_grp", "w_rg_a", "w_rg_x"]


def _as2d(name, a):
    if name in MAT_ITEMS:
        return a.reshape(-1, HEAD, HEAD)
    if name == "conv_w":
        return a.reshape(CONV_WIDTH, -1)
    return a.reshape(1, -1)


def kernel(x, norm_mix, w_in, w_pool_grp, pool_scale, w_pool_out, conv_w, conv_b, w_rg_a, b_rg_a, w_rg_x, b_rg_x, lru_lambda, w_rnn_out, w_o, norm_ffn, w_ffn_in, w_ffn_out, norm_final, loss_target, m_norm_mix, m_w_in, m_w_pool_grp, m_pool_scale, m_w_pool_out, m_conv_w, m_conv_b, m_w_rg_a, m_b_rg_a, m_w_rg_x, m_b_rg_x, m_lru_lambda, m_w_rnn_out, m_w_o, m_norm_ffn, m_w_ffn_in, m_w_ffn_out, m_norm_final, v_norm_mix, v_w_in, v_w_pool_grp, v_pool_scale, v_w_pool_out, v_conv_w, v_conv_b, v_w_rg_a, v_b_rg_a, v_w_rg_x, v_b_rg_x, v_lru_lambda, v_w_rnn_out, v_w_o, v_norm_ffn, v_w_ffn_in, v_w_ffn_out, v_norm_final):
    given = dict(locals())
    w = {name: given[name] for name in WEIGHTS}
    m = {name: given["m_" + name] for name in WEIGHTS}
    v = {name: given["v_" + name] for name in WEIGHTS}
    chip, c = _place()

    place = jnp.stack([chip, c]).astype(jnp.int32)
    conv_cols = w["conv_w"].shape[-1]
    conv_w_mine = lax.dynamic_update_slice_in_dim(jnp.zeros((CONV_WIDTH, D_RNN), F32), w["conv_w"][0], chip * conv_cols, axis=1)
    w_in_mine = _cast_into_whole(w["w_in"][0], "w_in", place)
    later = [name for name in LARGE if name != "w_in"]
    casts, ((w_in_full, conv_w_full),) = _cast_many_into_whole(
        {name: w[name][0] for name in later}, place, jobs=[_gather_job({"w_in": w_in_mine}, ["w_in"], conv_w_mine)])
    full = dict(zip(later, casts), w_in=w_in_full)
    small = {name: _as2d(name, w[name]) for name in WEIGHTS if name not in LARGE and name != "conv_w"}
    sq_cols, grad_x, grads = _step(x[0], loss_target[0], small, full, conv_w_full, place)
    loss = 0.5 / D_MODEL * jnp.sum(sq_cols)
    grads["conv_w"] = lax.dynamic_slice_in_dim(grads["conv_w"], chip * conv_cols, conv_cols, axis=1)

    delta, new_m, new_v = {}, {}, {}
    for name in LARGE:
        delta[name], new_m[name], new_v[name] = _adamw_large(w[name][0], grads[name], m[name][0], v[name][0], name)
    small_names = [name for name in WEIGHTS if name not in LARGE]
    flat = lambda d: [d[name].reshape(grads[name].shape) for name in small_names]
    ds, mo, vo = _adamw_small(flat(w), [grads[name] for name in small_names], flat(m), flat(v))
    for k, name in enumerate(small_names):
        delta[name], new_m[name], new_v[name] = ds[k], mo[k], vo[k]

    shaped = lambda d: [d[name].reshape(w[name].shape) for name in WEIGHTS]
    return (loss, grad_x[None], *shaped(grads), *shaped(delta), *shaped(new_m), *shaped(new_v))
```

```python
import functools
import math

import jax
import jax.numpy as jnp
from jax import lax
from jax.experimental import pallas as pl
from jax.experimental.pallas import tpu as pltpu

F32 = jnp.float32
BF16 = jnp.bfloat16

D_MODEL = 1024
D_POOL = 512
N_POOL_GROUPS = 4
D_RNN = 1024
N_RNN_HEADS = 8
HEAD = 128
CONV_WIDTH = 4
LRU_C = 8.0
D_FF = 2816
D_IN = D_POOL + 2 * D_RNN + 2 * D_MODEL
NORM_EPS = 1e-6
COL_RNN = D_POOL // HEAD
COL_GATE = (D_POOL + D_RNN) // HEAD

ADAM_LR = 0.001
ADAM_B1 = 0.9
ADAM_B2 = 0.999
ADAM_EPS = 1e-08
ADAM_WD = 0.01
ADAM_STEP = 10

N_CHIPS = 4
N_DEV = 8
MESH = pl.DeviceIdType.MESH
ANY = pl.BlockSpec(memory_space=pl.ANY)
VMEM = pl.BlockSpec(memory_space=pltpu.VMEM)
VMEM_LIMIT_BYTES = 60 * 1024 * 1024
SUBLANES = 8
POOL_HALO = 16
CHUNK = 1024

GELU_C = math.sqrt(2.0 / math.pi)
GELU_A = 0.044715


def _params(**kw):
    return pltpu.CompilerParams(vmem_limit_bytes=VMEM_LIMIT_BYTES, **kw)


def _sigmoid(x):
    return 0.5 * jnp.tanh(0.5 * x) + 0.5


def _log1p(y):
    u = 1.0 + y
    d = u - 1.0
    return jnp.where(d == 0.0, y, jnp.log(u) * (y / jnp.where(d == 0.0, 1.0, d)))


def _gelu_parts(x):
    x2 = x * x
    th = jnp.tanh(GELU_C * (x + GELU_A * x * x2))
    g = 0.5 * x * (1.0 + th)
    dg = 0.5 * (1.0 + th) + 0.5 * x * (1.0 - th * th) * GELU_C * (1.0 + 3.0 * GELU_A * x2)
    return g, dg


def _dot(a, b):
    return jnp.dot(a, b, preferred_element_type=F32)


def _dot_nt(a, b):
    return lax.dot_general(a, b, (((1,), (1,)), ((), ())), preferred_element_type=F32)


def _dot_tn(a, b):
    return lax.dot_general(a, b, (((0,), (0,)), ((), ())), preferred_element_type=F32)


def _rms_scale(xv):
    return lax.rsqrt(jnp.mean(xv * xv, axis=-1, keepdims=True) + NORM_EPS)


def _rms_bwd(dy, xv, g):
    r = _rms_scale(xv)
    xh = xv * r
    dyg = dy * g
    dx = r * (dyg - xh * jnp.mean(dyg * xh, axis=-1, keepdims=True))
    return dx, dy * xh


class _Job:
    def __init__(self, inputs, out_shapes, aliases, sem_shape, start, finish):
        self.inputs, self.out_shapes, self.aliases, self.sem_shape = list(inputs), list(out_shapes), dict(aliases), sem_shape
        self.start, self.finish = start, finish


def _pallas(body, operands, *, name, grid, in_specs, out_specs, out_shape, scratch_shapes=(), semantics=None, jobs=(),
            prefetch=None):
    n_in, n_out, n_scr = len(in_specs), len(out_specs), len(scratch_shapes)
    n_pre = 0 if prefetch is None else 1
    job_in = [a for job in jobs for a in job.inputs]
    job_out = [s for job in jobs for s in job.out_shapes]
    aliases, i0, o0 = {}, n_pre + n_in, n_out
    for job in jobs:
        aliases.update({i0 + i: o0 + o for i, o in job.aliases.items()})
        i0, o0 = i0 + len(job.inputs), o0 + len(job.out_shapes)

    def whole(*refs):
        pre, refs = refs[:n_pre], refs[n_pre:]
        ins, j_ins = refs[:n_in], refs[n_in:n_in + len(job_in)]
        outs = refs[n_in + len(job_in):][:n_out]
        j_outs = refs[n_in + len(job_in) + n_out:][:len(job_out)]
        rest = refs[n_in + len(job_in) + n_out + len(job_out):]
        scr, sems = rest[:n_scr], rest[n_scr:]

        def run(phase):
            i, o = 0, 0
            for k, job in enumerate(jobs):
                getattr(job, phase)(j_ins[i:i + len(job.inputs)], j_outs[o:o + len(job.out_shapes)], sems[2 * k], sems[2 * k + 1])
                i, o = i + len(job.inputs), o + len(job.out_shapes)

        def at(step_of, phase):
            if not jobs:
                return
            if not grid:
                run(phase)
                return
            cond = functools.reduce(jnp.logical_and, [pl.program_id(d) == step_of(d) for d in range(len(grid))])
            pl.when(cond)(functools.partial(run, phase))

        at(lambda d: 0, "start")
        body(*pre, *ins, *outs, *scr)
        at(lambda d: grid[d] - 1, "finish")

    layout = dict(grid=grid, in_specs=list(in_specs) + [ANY] * len(job_in), out_specs=list(out_specs) + [ANY] * len(job_out),
                  scratch_shapes=list(scratch_shapes) + [pltpu.SemaphoreType.DMA(job.sem_shape) for job in jobs for _ in range(2)])
    if prefetch is not None:
        layout = dict(grid_spec=pltpu.PrefetchScalarGridSpec(num_scalar_prefetch=1, **layout))
    res = pl.pallas_call(
        whole, name=name, out_shape=list(out_shape) + job_out, input_output_aliases=aliases,
        compiler_params=_params(dimension_semantics=semantics, has_side_effects=bool(jobs)), **layout,
    )(*([] if prefetch is None else [prefetch]), *operands, *job_in)
    per_job, o = [], n_out
    for job in jobs:
        per_job.append(res[o:o + len(job.out_shapes)])
        o += len(job.out_shapes)
    return res[:n_out], per_job


def _run_jobs(jobs, name):
    return _pallas(lambda: None, [], name=name, grid=(), in_specs=[], out_specs=[], out_shape=[], jobs=jobs)[1]


NORM_ROWS = 256
EPILOGUE_ROWS = 512


def _norm_rows(x_ref, g_ref, h_ref):
    g = g_ref[...]

    def rows(i, carry):
        r = pl.ds(pl.multiple_of(i * NORM_ROWS, NORM_ROWS), NORM_ROWS)
        xv = x_ref[r, :]
        h_ref[r, :] = (xv * _rms_scale(xv) * g).astype(BF16)
        return carry

    lax.fori_loop(0, x_ref.shape[0] // NORM_ROWS, rows, 0)


def _norm_matmul(x, g, w, *, tm, tn, name, jobs=()):
    T, K = x.shape
    N = w.shape[1]

    def body(x_ref, g_ref, w_ref, o_ref, h_ref):
        @pl.when(pl.program_id(1) == 0)
        def _():
            _norm_rows(x_ref, g_ref, h_ref)

        o_ref[...] = _dot(h_ref[...], w_ref[...])

    return _pallas(
        body, (x, g, w), name=name, grid=(T // tm, N // tn),
        in_specs=[pl.BlockSpec((tm, K), lambda i, j: (i, 0)), pl.BlockSpec((1, K), lambda i, j: (0, 0)),
                  pl.BlockSpec((K, tn), lambda i, j: (0, j))],
        out_specs=[pl.BlockSpec((tm, tn), lambda i, j: (i, j)), pl.BlockSpec((tm, K), lambda i, j: (i, 0))],
        out_shape=[jax.ShapeDtypeStruct((T, N), F32), jax.ShapeDtypeStruct((T, K), BF16)],
        semantics=("parallel", "arbitrary"), jobs=jobs)


def _ffn_in(x2, g, w, *, tm, tn, jobs=()):
    T, K = x2.shape
    nb = D_FF // tn

    def body(x_ref, g_ref, wg_ref, wu_ref, dup_ref, dgate_ref, act_ref, h_ref):
        @pl.when(pl.program_id(1) == 0)
        def _():
            _norm_rows(x_ref, g_ref, h_ref)

        wg, wu = wg_ref[...], wu_ref[...]
        for r in range(0, tm, EPILOGUE_ROWS):
            rows = pl.ds(r, min(EPILOGUE_ROWS, tm))
            h = h_ref[rows, :]
            gate, up = _dot(h, wg), _dot(h, wu)
            s = _sigmoid(gate)
            silu = gate * s
            dup_ref[rows, :] = silu.astype(BF16)
            dgate_ref[rows, :] = (up * (s + silu * (1.0 - s))).astype(BF16)
            act_ref[rows, :] = (silu * up).astype(BF16)

    blk = pl.BlockSpec((tm, tn), lambda i, j: (i, j))
    return _pallas(
        body, (x2, g, w, w), name="ffn_in", grid=(T // tm, nb),
        in_specs=[pl.BlockSpec((tm, K), lambda i, j: (i, 0)), pl.BlockSpec((1, K), lambda i, j: (0, 0)),
                  pl.BlockSpec((K, tn), lambda i, j: (0, j)), pl.BlockSpec((K, tn), lambda i, j: (0, j + nb))],
        out_specs=[blk, blk, blk, pl.BlockSpec((tm, K), lambda i, j: (i, 0))],
        out_shape=[jax.ShapeDtypeStruct((T, D_FF), BF16), jax.ShapeDtypeStruct((T, D_FF), BF16),
                   jax.ShapeDtypeStruct((T, D_FF), BF16), jax.ShapeDtypeStruct((T, K), BF16)],
        semantics=("parallel", "arbitrary"), jobs=jobs)


def _branch_mix(pm, z, w_pool_out, w_rnn_out, proj, *, tm, tn):
    T = pm.shape[0]
    col_gp = (D_POOL + 2 * D_RNN) // tn
    col_gr = col_gp + D_MODEL // tn

    def body(pm_ref, z_ref, wp_ref, wr_ref, gp_ref, gr_ref, by_gp_ref, by_gr_ref, sp_ref, sr_ref, mix_ref):
        wp, wr = wp_ref[...], wr_ref[...]
        for r in range(0, tm, EPILOGUE_ROWS):
            rows = pl.ds(r, min(EPILOGUE_ROWS, tm))
            yp, yr = _dot(pm_ref[rows, :], wp), _dot(z_ref[rows, :], wr)
            sp, sr = _sigmoid(gp_ref[rows, :]), _sigmoid(gr_ref[rows, :])
            by_gp_ref[rows, :] = (yp * sp * (1.0 - sp)).astype(BF16)
            by_gr_ref[rows, :] = (yr * sr * (1.0 - sr)).astype(BF16)
            sp_ref[rows, :] = sp.astype(BF16)
            sr_ref[rows, :] = sr.astype(BF16)
            mix_ref[rows, :] = (sp * yp + sr * yr).astype(BF16)

    blk = pl.BlockSpec((tm, tn), lambda i, j: (i, j))
    out = jax.ShapeDtypeStruct((T, D_MODEL), BF16)
    return pl.pallas_call(
        body, name="branch_mix", grid=(T // tm, D_MODEL // tn),
        in_specs=[pl.BlockSpec((tm, D_POOL), lambda i, j: (i, 0)), pl.BlockSpec((tm, D_RNN), lambda i, j: (i, 0)),
                  pl.BlockSpec((D_POOL, tn), lambda i, j: (0, j)), pl.BlockSpec((D_RNN, tn), lambda i, j: (0, j)),
                  pl.BlockSpec((tm, tn), lambda i, j: (i, col_gp + j)), pl.BlockSpec((tm, tn), lambda i, j: (i, col_gr + j))],
        out_specs=[blk] * 5, out_shape=[out] * 5,
        compiler_params=_params(dimension_semantics=("parallel", "parallel")),
    )(pm, z, w_pool_out, w_rnn_out, proj, proj)


def _out_proj_residual(mix, w_o, x, *, tm):
    T = x.shape[0]

    def body(mix_ref, w_ref, x_ref, o_ref):
        o_ref[...] = x_ref[...] + _dot(mix_ref[...], w_ref[...])

    row = pl.BlockSpec((tm, D_MODEL), lambda i: (i, 0))
    return pl.pallas_call(
        body, name="out_proj_residual", grid=(T // tm,),
        in_specs=[row, pl.BlockSpec((D_MODEL, D_MODEL), lambda i: (0, 0)), row],
        out_specs=row, out_shape=jax.ShapeDtypeStruct((T, D_MODEL), F32),
        compiler_params=_params(dimension_semantics=("parallel",)),
    )(mix, w_o, x)


def _ffn_out_loss(act, w, x2, g3, target, *, tm):
    T = x2.shape[0]

    def body(act_ref, w_ref, x2_ref, g_ref, t_ref, dx_ref, dxb_ref, sq_ref, dg_ref):
        @pl.when(pl.program_id(0) == 0)
        def _():
            sq_ref[...] = jnp.zeros_like(sq_ref)
            dg_ref[...] = jnp.zeros_like(dg_ref)

        g, w = g_ref[...], w_ref[...]
        for r in range(0, tm, NORM_ROWS):
            rows = pl.ds(r, min(NORM_ROWS, tm))
            x3 = x2_ref[rows, :] + _dot(act_ref[rows, :], w)
            err = x3 * _rms_scale(x3) * g - t_ref[rows, :]
            sq_ref[...] += jnp.sum(err * err, axis=0, keepdims=True)
            dx, dgp = _rms_bwd(err * (1.0 / D_MODEL), x3, g)
            dg_ref[...] += jnp.sum(dgp, axis=0, keepdims=True)
            dx_ref[rows, :] = dx
            dxb_ref[rows, :] = dx.astype(BF16)

    row = pl.BlockSpec((tm, D_MODEL), lambda i: (i, 0))
    vec = pl.BlockSpec((1, D_MODEL), lambda i: (0, 0))
    return pl.pallas_call(
        body, name="ffn_out_loss", grid=(T // tm,),
        in_specs=[pl.BlockSpec((tm, D_FF), lambda i: (i, 0)), pl.BlockSpec((D_FF, D_MODEL), lambda i: (0, 0)), row, vec, row],
        out_specs=[row, row, vec, vec],
        out_shape=[jax.ShapeDtypeStruct((T, D_MODEL), F32), jax.ShapeDtypeStruct((T, D_MODEL), BF16),
                   jax.ShapeDtypeStruct((1, D_MODEL), F32), jax.ShapeDtypeStruct((1, D_MODEL), F32)],
        compiler_params=_params(dimension_semantics=("arbitrary",)),
    )(act, w, x2, g3, target)


def _ffn_out_bwd(dx3b, w, act_by_gate, act_by_up, *, tm, tn):
    T = dx3b.shape[0]

    def body(dx_ref, w_ref, by_gate_ref, by_up_ref, dgate_ref, dup_ref):
        w = w_ref[...]
        for r in range(0, tm, EPILOGUE_ROWS):
            rows = pl.ds(r, min(EPILOGUE_ROWS, tm))
            dact = _dot_nt(dx_ref[rows, :], w)
            dgate_ref[rows, :] = (dact * by_gate_ref[rows, :].astype(F32)).astype(BF16)
            dup_ref[rows, :] = (dact * by_up_ref[rows, :].astype(F32)).astype(BF16)

    blk = pl.BlockSpec((tm, tn), lambda i, j: (i, j))
    return pl.pallas_call(
        body, name="ffn_out_bwd", grid=(T // tm, D_FF // tn),
        in_specs=[pl.BlockSpec((tm, D_MODEL), lambda i, j: (i, 0)), pl.BlockSpec((tn, D_MODEL), lambda i, j: (j, 0)), blk, blk],
        out_specs=[blk, blk],
        out_shape=[jax.ShapeDtypeStruct((T, D_FF), BF16), jax.ShapeDtypeStruct((T, D_FF), BF16)],
        compiler_params=_params(dimension_semantics=("parallel", "parallel")),
    )(dx3b, w, act_by_gate, act_by_up)


def _ffn_in_bwd(dgate, dup, w, dx3, x2, g2, *, tm, jobs=()):
    T = x2.shape[0]

    def body(dgate_ref, dup_ref, w_ref, dx3_ref, x2_ref, g_ref, dx_ref, dxb_ref, dg_ref):
        @pl.when(pl.program_id(0) == 0)
        def _():
            dg_ref[...] = jnp.zeros_like(dg_ref)

        g = g_ref[...]
        for r in range(0, tm, NORM_ROWS):
            rows = pl.ds(r, min(NORM_ROWS, tm))
            dh = _dot_nt(dgate_ref[rows, :], w_ref[:, :D_FF]) + _dot_nt(dup_ref[rows, :], w_ref[:, D_FF:])
            dxn, dgp = _rms_bwd(dh, x2_ref[rows, :], g)
            dx = dx3_ref[rows, :] + dxn
            dg_ref[...] += jnp.sum(dgp, axis=0, keepdims=True)
            dx_ref[rows, :] = dx
            dxb_ref[rows, :] = dx.astype(BF16)

    row = pl.BlockSpec((tm, D_MODEL), lambda i: (i, 0))
    wide = pl.BlockSpec((tm, D_FF), lambda i: (i, 0))
    vec = pl.BlockSpec((1, D_MODEL), lambda i: (0, 0))
    return _pallas(
        body, (dgate, dup, w, dx3, x2, g2), name="ffn_in_bwd", grid=(T // tm,),
        in_specs=[wide, wide, pl.BlockSpec((D_MODEL, 2 * D_FF), lambda i: (0, 0)), row, row, vec],
        out_specs=[row, row, vec],
        out_shape=[jax.ShapeDtypeStruct((T, D_MODEL), F32), jax.ShapeDtypeStruct((T, D_MODEL), BF16),
                   jax.ShapeDtypeStruct((1, D_MODEL), F32)],
        semantics=("arbitrary",), jobs=jobs)


def _out_proj_bwd(dx2b, w_o, mix_by, *, tm, tn, jobs=()):
    T = dx2b.shape[0]

    def body(dx_ref, w_ref, *refs):
        w = w_ref[...]
        for r in range(0, tm, EPILOGUE_ROWS):
            rows = pl.ds(r, min(EPILOGUE_ROWS, tm))
            dmix = _dot_nt(dx_ref[rows, :], w)
            for by_ref, d_ref in zip(refs[:4], refs[4:]):
                d_ref[rows, :] = (dmix * by_ref[rows, :].astype(F32)).astype(BF16)

    blk = pl.BlockSpec((tm, tn), lambda i, j: (i, j))
    out = jax.ShapeDtypeStruct((T, D_MODEL), BF16)
    return _pallas(
        body, (dx2b, w_o, *mix_by), name="out_proj_bwd", grid=(T // tm, D_MODEL // tn),
        in_specs=[pl.BlockSpec((tm, D_MODEL), lambda i, j: (i, 0)), pl.BlockSpec((tn, D_MODEL), lambda i, j: (j, 0))] + [blk] * 4,
        out_specs=[blk] * 4, out_shape=[out] * 4, semantics=("parallel", "parallel"), jobs=jobs)


def _branch_bwd(dyp, dyr, w_pool_out, w_rnn_out, *, tm):
    T = dyp.shape[0]

    def body(dyp_ref, dyr_ref, wp_ref, wr_ref, dpm_ref, dz_ref):
        dpm_ref[...] = _dot_nt(dyp_ref[...], wp_ref[...])
        dz_ref[...] = _dot_nt(dyr_ref[...], wr_ref[...])

    row = pl.BlockSpec((tm, D_MODEL), lambda i: (i, 0))
    return pl.pallas_call(
        body, name="branch_bwd", grid=(T // tm,),
        in_specs=[row, row, pl.BlockSpec((D_POOL, D_MODEL), lambda i: (0, 0)), pl.BlockSpec((D_RNN, D_MODEL), lambda i: (0, 0))],
        out_specs=[pl.BlockSpec((tm, D_POOL), lambda i: (i, 0)), pl.BlockSpec((tm, D_RNN), lambda i: (i, 0))],
        out_shape=[jax.ShapeDtypeStruct((T, D_POOL), F32), jax.ShapeDtypeStruct((T, D_RNN), F32)],
        compiler_params=_params(dimension_semantics=("parallel",)),
    )(dyp, dyr, w_pool_out, w_rnn_out)


def _in_proj_bwd(segs, w, dx2, x, g1, *, tm, jobs=()):
    T = x.shape[0]
    widths = [s.shape[1] for s in segs]
    offs = [sum(widths[:k]) for k in range(len(widths))]
    n = len(segs)

    def body(*refs):
        seg_refs, (w_ref, dx2_ref, x_ref, g_ref, dx_ref, dg_ref) = refs[:n], refs[n:]

        @pl.when(pl.program_id(0) == 0)
        def _():
            dg_ref[...] = jnp.zeros_like(dg_ref)

        g = g_ref[...]
        for r in range(0, tm, NORM_ROWS):
            rows = pl.ds(r, min(NORM_ROWS, tm))
            dh = _dot_nt(seg_refs[0][rows, :], w_ref[:, offs[0]:offs[0] + widths[0]])
            for k in range(1, n):
                dh += _dot_nt(seg_refs[k][rows, :], w_ref[:, offs[k]:offs[k] + widths[k]])
            dxn, dgp = _rms_bwd(dh, x_ref[rows, :], g)
            dg_ref[...] += jnp.sum(dgp, axis=0, keepdims=True)
            dx_ref[rows, :] = dx2_ref[rows, :] + dxn

    row = pl.BlockSpec((tm, D_MODEL), lambda i: (i, 0))
    vec = pl.BlockSpec((1, D_MODEL), lambda i: (0, 0))
    return _pallas(
        body, (*segs, w, dx2, x, g1), name="in_proj_bwd", grid=(T // tm,),
        in_specs=[pl.BlockSpec((tm, wd), lambda i: (i, 0)) for wd in widths]
        + [pl.BlockSpec((D_MODEL, D_IN), lambda i: (0, 0)), row, row, vec],
        out_specs=[row, vec],
        out_shape=[jax.ShapeDtypeStruct((T, D_MODEL), F32), jax.ShapeDtypeStruct((1, D_MODEL), F32)],
        semantics=("arbitrary",), jobs=jobs)


def _weight_grad(a, segs, *, tm, tn, name, jobs=None, also_bf16=False):
    T, M = a.shape
    nblk = [s.shape[1] // tn for s in segs]
    first = [sum(nblk[:k]) for k in range(len(segs))]
    n = len(segs)

    def body(a_ref, *refs):
        seg_refs, o_refs = refs[:n], refs[n:]
        j = pl.program_id(1)
        for k in range(n):
            @pl.when((j >= first[k]) & (j < first[k] + nblk[k]))
            def _(k=k):
                grad = _dot_tn(a_ref[...], seg_refs[k][...])
                for o_ref in o_refs:
                    o_ref[...] = grad.astype(o_ref.dtype)

    def seg_spec(k):
        return pl.BlockSpec((T, tn), lambda i, j: (0, jnp.clip(j - first[k], 0, nblk[k] - 1)))

    dtypes = [F32, BF16] if also_bf16 else [F32]
    grad, results = _pallas(
        body, (a, *segs), name=name, grid=(M // tm, sum(nblk)),
        in_specs=[pl.BlockSpec((T, tm), lambda i, j: (0, i))] + [seg_spec(k) for k in range(n)],
        out_specs=[pl.BlockSpec((tm, tn), lambda i, j: (i, j))] * len(dtypes),
        out_shape=[jax.ShapeDtypeStruct((M, sum(nblk) * tn), dtype) for dtype in dtypes],
        semantics=("parallel", "arbitrary"), jobs=jobs or ())
    grad = tuple(grad) if also_bf16 else grad[0]
    return grad if jobs is None else (grad, results)


def _pad_front(dst, src, halo):
    dst[pl.ds(0, halo), :] = jnp.zeros((halo, src.shape[1]), F32)

    def fill(i, carry):
        r0 = pl.multiple_of(i * CHUNK, CHUNK)
        dst[pl.ds(r0 + halo, CHUNK), :] = src[pl.ds(r0, CHUNK), :]
        return carry

    lax.fori_loop(0, src.shape[0] // CHUNK, fill, 0)


def _shift_rows(v, k):
    return pltpu.roll(v, k % v.shape[0], axis=0)


def _window_sums(xs, direction):
    s2 = xs + _shift_rows(xs, direction)
    s4 = s2 + _shift_rows(s2, 2 * direction)
    s8 = s4 + _shift_rows(s4, 4 * direction)
    s16 = s8 + _shift_rows(s8, 8 * direction)
    return s2, s4, s8, s16


def _select_window(g, sums):
    s2, s4, s8, s16 = sums
    return jnp.where(g == 0, s2, jnp.where(g == 1, s4, jnp.where(g == 2, s8, s16)))


def _pool_count(g, start, rows):
    t = start + lax.broadcasted_iota(jnp.int32, (rows, 1), 0)
    return jnp.minimum(t + 1, jnp.left_shift(2, g)).astype(F32)


def _pool_fwd(proj, w_grp, scale):
    T = proj.shape[0]
    nchunk = T // CHUNK

    def body(u_ref, w_ref, s_ref, o_ref, upad):
        g = pl.program_id(0)
        _pad_front(upad, u_ref, POOL_HALO)
        w = w_ref[...].astype(BF16)
        scale_row = s_ref[...]

        def chunk(i, carry):
            r0 = pl.multiple_of(i * CHUNK, CHUNK)
            xs = upad[pl.ds(r0, CHUNK + POOL_HALO), :]
            win = _select_window(g, _window_sums(xs, 1))[POOL_HALO:]
            pooled = win / _pool_count(g, r0, CHUNK) - xs[POOL_HALO:]
            o_ref[pl.ds(r0, CHUNK), :] = (_dot(pooled.astype(BF16), w) * scale_row).astype(BF16)
            return carry

        lax.fori_loop(0, nchunk, chunk, 0)

    return pl.pallas_call(
        body, name="pool_fwd", grid=(N_POOL_GROUPS,),
        in_specs=[pl.BlockSpec((T, HEAD), lambda g: (0, g)), pl.BlockSpec((None, HEAD, HEAD), lambda g: (g, 0, 0)),
                  pl.BlockSpec((1, HEAD), lambda g: (0, g))],
        out_specs=pl.BlockSpec((T, HEAD), lambda g: (0, g)),
        out_shape=jax.ShapeDtypeStruct((T, D_POOL), BF16),
        scratch_shapes=[pltpu.VMEM((T + POOL_HALO, HEAD), F32)],
        compiler_params=_params(dimension_semantics=("parallel",)),
    )(proj, w_grp, scale)


def _pool_bwd(proj, dpm, w_grp, scale, jobs=()):
    T = proj.shape[0]
    nchunk = T // CHUNK

    def body(u_ref, dpm_ref, w_ref, s_ref, du_ref, dw_ref, ds_ref, upad, zpad, dpool):
        g = pl.program_id(0)
        _pad_front(upad, u_ref, POOL_HALO)
        zpad[pl.ds(T, POOL_HALO), :] = jnp.zeros((POOL_HALO, HEAD), F32)
        dw_ref[...] = jnp.zeros_like(dw_ref)
        ds_ref[...] = jnp.zeros_like(ds_ref)
        w = w_ref[...].astype(BF16)
        scale_row = s_ref[...]

        def chunk(i, carry):
            r0 = pl.multiple_of(i * CHUNK, CHUNK)
            xs = upad[pl.ds(r0, CHUNK + POOL_HALO), :]
            cnt = _pool_count(g, r0, CHUNK)
            pooled = (_select_window(g, _window_sums(xs, 1))[POOL_HALO:] / cnt - xs[POOL_HALO:]).astype(BF16)
            mixed = _dot(pooled, w)
            d = dpm_ref[pl.ds(r0, CHUNK), :]
            ds_ref[...] += jnp.sum(d * mixed, axis=0, keepdims=True)
            dmixed = (d * scale_row).astype(BF16)
            dw_ref[...] += _dot_tn(pooled, dmixed)
            dp = _dot_nt(dmixed, w)
            dpool[pl.ds(r0, CHUNK), :] = dp
            zpad[pl.ds(r0, CHUNK), :] = dp / cnt
            return carry

        lax.fori_loop(0, nchunk, chunk, 0)

        def chunk2(i, carry):
            r0 = pl.multiple_of(i * CHUNK, CHUNK)
            zs = zpad[pl.ds(r0, CHUNK + POOL_HALO), :]
            win = _select_window(g, _window_sums(zs, -1))[:CHUNK]
            du_ref[pl.ds(r0, CHUNK), :] = (win - dpool[pl.ds(r0, CHUNK), :]).astype(BF16)
            return carry

        lax.fori_loop(0, nchunk, chunk2, 0)

    col = pl.BlockSpec((T, HEAD), lambda g: (0, g))
    return _pallas(
        body, (proj, dpm, w_grp, scale), name="pool_bwd", grid=(N_POOL_GROUPS,),
        in_specs=[col, col, pl.BlockSpec((None, HEAD, HEAD), lambda g: (g, 0, 0)), pl.BlockSpec((1, HEAD), lambda g: (0, g))],
        out_specs=[col, pl.BlockSpec((None, HEAD, HEAD), lambda g: (g, 0, 0)), pl.BlockSpec((1, HEAD), lambda g: (0, g))],
        out_shape=[jax.ShapeDtypeStruct((T, D_POOL), BF16), jax.ShapeDtypeStruct((N_POOL_GROUPS, HEAD, HEAD), F32),
                   jax.ShapeDtypeStruct((1, D_POOL), F32)],
        scratch_shapes=[pltpu.VMEM((T + POOL_HALO, HEAD), F32), pltpu.VMEM((T + POOL_HALO, HEAD), F32), pltpu.VMEM((T, HEAD), F32)],
        semantics=("parallel",), jobs=jobs)


def _conv_taps(xs, cw):
    v = cw[CONV_WIDTH - 1] * xs[SUBLANES:]
    for k in range(CONV_WIDTH - 1):
        v += cw[k] * _shift_rows(xs, CONV_WIDTH - 1 - k)[SUBLANES:]
    return v


def _tap_rows(cw_ref):
    return [cw_ref[k:k + 1, :] for k in range(CONV_WIDTH)]


def _softplus_neg(lam):
    return jnp.maximum(-lam, 0.0) + _log1p(jnp.exp(-jnp.abs(lam)))


def _lru_gates(v, wa, ba, wx, bx, sp):
    vb = v.astype(BF16)
    ra = _sigmoid(_dot(vb, wa) + ba)
    ix = _sigmoid(_dot(vb, wx) + bx)
    log_a = -LRU_C * ra * sp
    a = jnp.exp(log_a)
    sq = jnp.sqrt(-jnp.tanh(log_a) * (a * a + 1.0))
    return ra, ix, a, sq


def _row_bcast(v, r):
    return jnp.broadcast_to(v[r:r + 1, :], v.shape)


TILE_BLOCK = 128


def _scan_in_tiles(coef, coef_shift, A_out, B, T, direction):
    order = list(range(SUBLANES)) if direction == 1 else list(range(SUBLANES - 1, -1, -1))
    tiles = min(TILE_BLOCK, T // SUBLANES)
    for base in range(0, T, tiles * SUBLANES):
        def rows(r, base=base):
            return pl.ds(base + r, tiles, stride=SUBLANES)

        A, Bv = coef[rows(order[0] + coef_shift), :], B[rows(order[0]), :]
        A_out[rows(order[0]), :] = A
        for r in order[1:]:
            a = coef[rows(r + coef_shift), :]
            Bv = a * Bv + B[rows(r), :]
            A = a * A
            A_out[rows(r), :] = A
            B[rows(r), :] = Bv


TILES_PER_STEP = 8


def _carry_tiles(A_s, B_s, out, ntile, direction):
    out_row = SUBLANES - 1 if direction == 1 else 0

    def step(k, carry):
        for j in range(TILES_PER_STEP):
            t = k * TILES_PER_STEP + j
            r0 = pl.multiple_of((t if direction == 1 else ntile - 1 - t) * SUBLANES, SUBLANES)
            A, B = A_s[pl.ds(r0, SUBLANES), :], B_s[pl.ds(r0, SUBLANES), :]
            out[pl.ds(r0, SUBLANES), :] = A * carry + B
            carry = _row_bcast(A, out_row) * carry + _row_bcast(B, out_row)
        return carry

    lax.fori_loop(0, ntile // TILES_PER_STEP, step, jnp.zeros((SUBLANES, HEAD), F32))


def _rnn_fwd(proj, conv_w, conv_b, w_a, b_a, w_x, b_x, lam, jobs=()):
    T = proj.shape[0]
    nchunk = T // CHUNK
    ntile = T // SUBLANES

    def body(u_ref, ug_ref, cw_ref, cb_ref, wa_ref, ba_ref, wx_ref, bx_ref, lam_ref,
             h_ref, z_ref, v_ref, ra_ref, ix_ref, a_ref, sq_ref, upad, a_s, b_s):
        _pad_front(upad, u_ref, SUBLANES)
        cw, cb = _tap_rows(cw_ref), cb_ref[...]
        wa, wx = wa_ref[...].astype(BF16), wx_ref[...].astype(BF16)
        ba, bx = ba_ref[...], bx_ref[...]
        sp = _softplus_neg(lam_ref[...])

        def chunk(i, carry):
            rows = pl.ds(pl.multiple_of(i * CHUNK, CHUNK), CHUNK)
            v = _conv_taps(upad[pl.ds(pl.multiple_of(i * CHUNK, CHUNK), CHUNK + SUBLANES), :], cw) + cb
            ra, ix, a, sq = _lru_gates(v, wa, ba, wx, bx, sp)
            v_ref[rows, :], ra_ref[rows, :], ix_ref[rows, :], a_ref[rows, :], sq_ref[rows, :] = v, ra, ix, a, sq
            a_s[rows, :], b_s[rows, :] = a, sq * ix * v
            return carry

        lax.fori_loop(0, nchunk, chunk, 0)
        _scan_in_tiles(a_s, 0, a_s, b_s, T, 1)
        _carry_tiles(a_s, b_s, h_ref, ntile, 1)

        def chunk3(i, carry):
            r0 = pl.multiple_of(i * CHUNK, CHUNK)
            gl, _ = _gelu_parts(ug_ref[pl.ds(r0, CHUNK), :])
            z_ref[pl.ds(r0, CHUNK), :] = (h_ref[pl.ds(r0, CHUNK), :] * gl).astype(BF16)
            return carry

        lax.fori_loop(0, nchunk, chunk3, 0)

    col = pl.BlockSpec((T, HEAD), lambda h: (0, h))
    vec = pl.BlockSpec((1, HEAD), lambda h: (0, h))
    mat = pl.BlockSpec((None, HEAD, HEAD), lambda h: (h, 0, 0))
    return _pallas(
        body, (proj, proj, conv_w, conv_b, w_a, b_a, w_x, b_x, lam), name="rnn_fwd", grid=(N_RNN_HEADS,),
        in_specs=[pl.BlockSpec((T, HEAD), lambda h: (0, COL_RNN + h)), pl.BlockSpec((T, HEAD), lambda h: (0, COL_GATE + h)),
                  pl.BlockSpec((CONV_WIDTH, HEAD), lambda h: (0, h)), vec, mat, vec, mat, vec, vec],
        out_specs=[col] * 7,
        out_shape=[jax.ShapeDtypeStruct((T, D_RNN), F32), jax.ShapeDtypeStruct((T, D_RNN), BF16)]
        + [jax.ShapeDtypeStruct((T, D_RNN), F32)] * 5,
        scratch_shapes=[pltpu.VMEM((T + SUBLANES, HEAD), F32), pltpu.VMEM((T, HEAD), F32), pltpu.VMEM((T, HEAD), F32)],
        semantics=("parallel",), jobs=jobs)


def _rnn_bwd(proj, hr, dz, gates, conv_w, w_a, w_x, lam, jobs=()):
    T = proj.shape[0]
    nchunk = T // CHUNK
    ntile = T // SUBLANES

    def body(u_ref, ug_ref, h_ref, dz_ref, v_ref, ra_ref, ix_ref, a_ref, sq_ref, cw_ref, wa_ref, wx_ref, lam_ref,
             du_ref, dug_ref, dwa_ref, dwx_ref, dba_ref, dbx_ref, dlam_ref, dcb_ref, dcw_ref,
             upad, hpad, apad, g_s, dvpad, ga_s):
        zero_tile = jnp.zeros((SUBLANES, HEAD), F32)
        _pad_front(upad, u_ref, SUBLANES)
        _pad_front(hpad, h_ref, SUBLANES)
        apad[pl.ds(T, SUBLANES), :] = zero_tile
        dvpad[pl.ds(T, SUBLANES), :] = zero_tile
        for ref in (dwa_ref, dwx_ref, dba_ref, dbx_ref, dlam_ref, dcb_ref, dcw_ref):
            ref[...] = jnp.zeros_like(ref)
        cw = _tap_rows(cw_ref)
        wa, wx = wa_ref[...].astype(BF16), wx_ref[...].astype(BF16)
        lam_row = lam_ref[...]
        sp = _softplus_neg(lam_row)

        def chunk(i, carry):
            rows = pl.ds(pl.multiple_of(i * CHUNK, CHUNK), CHUNK)
            apad[rows, :] = a_ref[rows, :]
            gl, dgl = _gelu_parts(ug_ref[rows, :])
            d = dz_ref[rows, :]
            g_s[rows, :] = d * gl
            dug_ref[rows, :] = (d * h_ref[rows, :] * dgl).astype(BF16)
            return carry

        lax.fori_loop(0, nchunk, chunk, 0)

        _scan_in_tiles(apad, 1, ga_s, g_s, T, -1)
        _carry_tiles(ga_s, g_s, g_s, ntile, -1)

        def chunk3(i, carry):
            r0 = pl.multiple_of(i * CHUNK, CHUNK)
            rows = pl.ds(r0, CHUNK)
            g = g_s[rows, :]
            h_prev = _shift_rows(hpad[pl.ds(r0, CHUNK + SUBLANES), :], 1)[SUBLANES:]
            v, ra, ix, sq, a = v_ref[rows, :], ra_ref[rows, :], ix_ref[rows, :], sq_ref[rows, :], a_ref[rows, :]
            d_sq = g * ix * v
            d_ix = g * sq * v
            d_la = a * g * h_prev - d_sq * a * a / sq
            dlam_ref[...] += jnp.sum(d_la * ra, axis=0, keepdims=True)
            d_pa = d_la * (-LRU_C) * sp * ra * (1.0 - ra)
            d_px = d_ix * ix * (1.0 - ix)
            vb, d_pab, d_pxb = v.astype(BF16), d_pa.astype(BF16), d_px.astype(BF16)
            dwa_ref[...] += _dot_tn(vb, d_pab)
            dwx_ref[...] += _dot_tn(vb, d_pxb)
            dba_ref[...] += jnp.sum(d_pa, axis=0, keepdims=True)
            dbx_ref[...] += jnp.sum(d_px, axis=0, keepdims=True)
            dv = g * sq * ix + _dot_nt(d_pab, wa) + _dot_nt(d_pxb, wx)
            dvpad[rows, :] = dv
            dcb_ref[...] += jnp.sum(dv, axis=0, keepdims=True)
            xs = upad[pl.ds(r0, CHUNK + SUBLANES), :]
            for k in range(CONV_WIDTH):
                u_k = _shift_rows(xs, CONV_WIDTH - 1 - k)[SUBLANES:] if k < CONV_WIDTH - 1 else xs[SUBLANES:]
                dcw_ref[k:k + 1, :] += jnp.sum(dv * u_k, axis=0, keepdims=True)
            return carry

        lax.fori_loop(0, nchunk, chunk3, 0)
        dlam_ref[...] = dlam_ref[...] * (LRU_C * _sigmoid(-lam_row))

        def chunk4(i, carry):
            r0 = pl.multiple_of(i * CHUNK, CHUNK)
            dvs = dvpad[pl.ds(r0, CHUNK + SUBLANES), :]
            du = cw[CONV_WIDTH - 1] * dvs[:CHUNK]
            for k in range(CONV_WIDTH - 1):
                du += cw[k] * _shift_rows(dvs, -(CONV_WIDTH - 1 - k))[:CHUNK]
            du_ref[pl.ds(r0, CHUNK), :] = du.astype(BF16)
            return carry

        lax.fori_loop(0, nchunk, chunk4, 0)

    col = pl.BlockSpec((T, HEAD), lambda h: (0, h))
    vec = pl.BlockSpec((1, HEAD), lambda h: (0, h))
    mat = pl.BlockSpec((None, HEAD, HEAD), lambda h: (h, 0, 0))
    taps = pl.BlockSpec((CONV_WIDTH, HEAD), lambda h: (0, h))
    vec_out = jax.ShapeDtypeStruct((1, D_RNN), F32)
    mat_out = jax.ShapeDtypeStruct((N_RNN_HEADS, HEAD, HEAD), F32)
    seq = pltpu.VMEM((T, HEAD), F32)
    seq_pad = pltpu.VMEM((T + SUBLANES, HEAD), F32)
    return _pallas(
        body, (proj, proj, hr, dz, *gates, conv_w, w_a, w_x, lam), name="rnn_bwd", grid=(N_RNN_HEADS,),
        in_specs=[pl.BlockSpec((T, HEAD), lambda h: (0, COL_RNN + h)), pl.BlockSpec((T, HEAD), lambda h: (0, COL_GATE + h))]
        + [col] * 7 + [taps, mat, mat, vec],
        out_specs=[col, col, mat, mat, vec, vec, vec, vec, taps],
        out_shape=[jax.ShapeDtypeStruct((T, D_RNN), BF16), jax.ShapeDtypeStruct((T, D_RNN), BF16), mat_out, mat_out,
                   vec_out, vec_out, vec_out, vec_out, jax.ShapeDtypeStruct((CONV_WIDTH, D_RNN), F32)],
        scratch_shapes=[seq_pad, seq_pad, seq_pad, seq, seq_pad, seq],
        semantics=("parallel",), jobs=jobs)


GROUP_FFN_OUT = ["w_ffn_out"]
GROUP_FFN_IN = ["w_ffn_in"]
GROUP_MIX = ["w_o", "w_pool_out", "w_rnn_out"]
GROUP_IN = ["w_in"]


def _step(x, target, s, full, conv_w, place):
    T = x.shape[0]
    tall, mid, low = min(T, 2048), min(T, 1024), min(T, 512)
    full = dict(full)

    def gathered(names, results):
        full.update(zip(names, results))

    early = ["w_pool_out", "w_rnn_out", "w_o", "w_ffn_out"]
    (proj, h1), (res,) = _norm_matmul(x, s["norm_mix"], full["w_in"], tm=tall, tn=512, name="in_proj", jobs=[_gather_job(full, early)])
    gathered(early, res)
    pm = _pool_fwd(proj, s["w_pool_grp"], s["pool_scale"])
    (hr, z, *gates), (res,) = _rnn_fwd(proj, conv_w, s["conv_b"], s["w_rg_a"], s["b_rg_a"], s["w_rg_x"], s["b_rg_x"],
                                       s["lru_lambda"], jobs=[_gather_job(full, ["w_ffn_in"])])
    gathered(["w_ffn_in"], res)
    *mix_by, mix = _branch_mix(pm, z, full["w_pool_out"], full["w_rnn_out"], proj, tm=tall, tn=256)
    x2 = _out_proj_residual(mix, full["w_o"], x, tm=mid)
    (act_by_up, act_by_gate, act, h2), _ = _ffn_in(x2, s["norm_ffn"], full["w_ffn_in"], tm=tall, tn=256)
    dx3, dx3b, sq_cols, g_norm_final = _ffn_out_loss(act, full["w_ffn_out"], x2, s["norm_final"], target, tm=low)

    g = {"norm_final": g_norm_final}

    def chip_sums(names, from_sibling):
        sums = {name: _chip_sum(name, g[name], got, place) for name, got in zip(names, from_sibling)}
        return {name: v[0] for name, v in sums.items()}, {name: v[1] for name, v in sums.items()}

    def final_sums(names, sums, from_chips):
        return {name: _final_sum(name, sums[name], got, place) for name, got in zip(names, from_chips)}

    dgate, dup = _ffn_out_bwd(dx3b, full["w_ffn_out"], act_by_gate, act_by_up, tm=tall, tn=256)
    g["w_ffn_out"] = _weight_grad(act, [dx3b], tm=256, tn=D_MODEL, name="w_ffn_out_grad")
    (dx2, dx2b, g["norm_ffn"]), (res,) = _ffn_in_bwd(dgate, dup, full["w_ffn_in"], dx3, x2, s["norm_ffn"], tm=low,
                                                     jobs=[_sibling_job(g, GROUP_FFN_OUT)])
    sums_ffn, sums_ffn_bf16 = chip_sums(GROUP_FFN_OUT, res)
    g["w_ffn_in"], (res,) = _weight_grad(h2, [dgate, dup], tm=D_MODEL, tn=256, name="w_ffn_in_grad",
                                         jobs=[_chips_job(sums_ffn_bf16, GROUP_FFN_OUT)])
    shards_ffn = final_sums(GROUP_FFN_OUT, sums_ffn, res)
    (dgp, dgr, dyp, dyr), (res,) = _out_proj_bwd(dx2b, full["w_o"], mix_by, tm=tall, tn=256,
                                                 jobs=[_sibling_job(g, GROUP_FFN_IN)])
    sums_ffn, sums_ffn_bf16 = chip_sums(GROUP_FFN_IN, res)
    g["w_o"] = _weight_grad(mix, [dx2b], tm=D_MODEL, tn=256, name="w_o_grad")
    dpm, dz = _branch_bwd(dyp, dyr, full["w_pool_out"], full["w_rnn_out"], tm=mid)
    g["w_pool_out"] = _weight_grad(pm, [dyp], tm=D_POOL, tn=256, name="w_pool_out_grad")
    g["w_rnn_out"] = _weight_grad(z, [dyr], tm=D_RNN, tn=256, name="w_rnn_out_grad")
    (dupool, g["w_pool_grp"], g["pool_scale"]), (res,) = _pool_bwd(proj, dpm, s["w_pool_grp"], s["pool_scale"],
                                                                   jobs=[_sibling_job(g, GROUP_MIX)])
    sums_mix, sums_mix_bf16 = chip_sums(GROUP_MIX, res)
    ((durnn, dugate, g["w_rg_a"], g["w_rg_x"], g["b_rg_a"], g["b_rg_x"], g["lru_lambda"], g["conv_b"], g["conv_w"]),
     (res,)) = _rnn_bwd(proj, hr, dz, gates, conv_w, s["w_rg_a"], s["w_rg_x"], s["lru_lambda"],
                        jobs=[_chips_job(sums_ffn_bf16, GROUP_FFN_IN)])
    shards_ffn.update(final_sums(GROUP_FFN_IN, sums_ffn, res))
    segs = [dupool, durnn, dugate, dgp, dgr]
    ffn = GROUP_FFN_OUT + GROUP_FFN_IN
    (g["w_in"], w_in_grad_bf16), (res, joined) = _weight_grad(
        h1, segs, tm=D_MODEL, tn=256, name="w_in_grad", also_bf16=True,
        jobs=[_chips_job(sums_mix_bf16, GROUP_MIX), _join_job(shards_ffn, ffn)])
    grads = dict(zip(ffn, joined))
    shards = final_sums(GROUP_MIX, sums_mix, res)
    (res,) = _run_jobs([_sibling_job({"w_in": w_in_grad_bf16}, GROUP_IN, BF16)], "w_in_exchange_sibling")
    sums_in, sums_in_bf16 = chip_sums(GROUP_IN, res)
    (grad_x, g["norm_mix"]), (res,) = _in_proj_bwd(segs, full["w_in"], dx2, x, s["norm_mix"], tm=low,
                                                  jobs=[_chips_job(sums_in_bf16, GROUP_IN)])
    shards.update(final_sums(GROUP_IN, sums_in, res))

    vec_rows = [g[name] if name != "pool_scale" else jnp.pad(g[name], ((0, 0), (0, D_MODEL - D_POOL))) for name in VEC_ITEMS]
    vec_rows += [g["conv_w"], sq_cols, jnp.zeros((VEC_ROWS - len(VEC_ITEMS) - CONV_WIDTH - 1, D_MODEL), F32)]
    vec = jnp.concatenate(vec_rows, axis=0).reshape(VEC_ROWS, N_DEV, HEAD).transpose(1, 0, 2)
    mat = jnp.concatenate([g[name].reshape(-1, HEAD) for name in MAT_ITEMS], axis=0).reshape(N_DEV, -1, HEAD)
    (vec, mat), (joined,) = _all_reduce_small([vec, mat], [False, True], jobs=[_join_job(shards, GROUP_MIX + GROUP_IN)])
    grads.update(zip(GROUP_MIX + GROUP_IN, joined))

    vec = vec.transpose(1, 0, 2).reshape(VEC_ROWS, D_MODEL)
    mat = mat.reshape(-1, HEAD)
    for k, name in enumerate(VEC_ITEMS):
        grads[name] = vec[k:k + 1, :s[name].shape[1]]
    grads["conv_w"] = vec[len(VEC_ITEMS):len(VEC_ITEMS) + CONV_WIDTH]
    row = 0
    for name in MAT_ITEMS:
        rows = s[name].shape[0] * HEAD
        grads[name] = mat[row:row + rows]
        row += rows
    return vec[len(VEC_ITEMS) + CONV_WIDTH], grad_x, grads


LARGE = {"w_in": "col", "w_pool_out": "col", "w_rnn_out": "row", "w_o": "row", "w_ffn_in": "col", "w_ffn_out": "row"}
LARGE_SHAPE = {"w_in": (D_MODEL, D_IN), "w_pool_out": (D_POOL, D_MODEL), "w_rnn_out": (D_RNN, D_MODEL),
               "w_o": (D_MODEL, D_MODEL), "w_ffn_in": (D_MODEL, 2 * D_FF), "w_ffn_out": (D_FF, D_MODEL)}


def _place():
    x, y, c = lax.axis_index("x"), lax.axis_index("y"), lax.axis_index("c")
    return 2 * x + y, c


def _chip_device(chip, c):
    return (chip // 2, chip % 2, c)


def _chip_window(ref, kind, shape, chip, half=None):
    K, N = shape
    if kind == "col":
        rows = slice(None) if half is None else pl.ds(half * (K // 2), K // 2)
        return ref.at[rows, pl.ds(chip * (N // N_CHIPS), N // N_CHIPS)]
    ks = K // N_CHIPS
    if half is None:
        return ref.at[pl.ds(chip * ks, ks), :]
    return ref.at[pl.ds(chip * ks + half * (ks // 2), ks // 2), :]


def _row_half(ref, half):
    rows = ref.shape[0] // 2
    return ref.at[pl.ds(half * rows, rows), :]


def _remote(win_src, win_dst, send_sems, recv_sems, idx, to):
    return pltpu.make_async_remote_copy(src_ref=win_src, dst_ref=win_dst, send_sem=send_sems.at[idx], recv_sem=recv_sems.at[idx],
                                        device_id=to, device_id_type=MESH)


def _gather_job(full, names, conv_w_full=None):
    n = len(names)
    cw_cols = D_RNN // N_CHIPS

    def windows(refs, chip, half):
        return [_chip_window(refs[k], LARGE[name], LARGE_SHAPE[name], chip, half) for k, name in enumerate(names)]

    def ici_copies(refs, send_sems, recv_sems, src_chip, dst_chip, c, r):
        wins = windows(refs, src_chip, c)
        if conv_w_full is not None:
            wins.append(refs[n].at[:, pl.ds(src_chip * cw_cols, cw_cols)])
        return [_remote(win, win, send_sems, recv_sems, (k, r), _chip_device(dst_chip, c)) for k, win in enumerate(wins)]

    def forwards(refs, send_sems, recv_sems, src_chip, half, to_core, chip, r):
        return [_remote(win, win, send_sems, recv_sems, (k, 3 + r), _chip_device(chip, to_core))
                for k, win in enumerate(windows(refs, src_chip, half))]

    def start(ins, outs, send_sems, recv_sems):
        chip, c = _place()
        for r in range(3):
            for cp in ici_copies(outs, send_sems, recv_sems, chip, chip ^ (r + 1), c, r):
                cp.start()

    def finish(ins, outs, send_sems, recv_sems):
        chip, c = _place()
        for r in range(3):
            for cp in ici_copies(outs, send_sems, recv_sems, chip ^ (r + 1), chip, c, r):
                cp.wait_recv()
            for cp in forwards(outs, send_sems, recv_sems, chip ^ (r + 1), c, 1 - c, chip, r):
                cp.start()
        for r in range(3):
            for cp in forwards(outs, send_sems, recv_sems, chip ^ (r + 1), 1 - c, c, chip, r):
                cp.wait_recv()
            for cp in ici_copies(outs, send_sems, recv_sems, chip, chip ^ (r + 1), c, r):
                cp.wait_send()
            for cp in forwards(outs, send_sems, recv_sems, chip ^ (r + 1), c, 1 - c, chip, r):
                cp.wait_send()

    arrays = [full[name] for name in names] + ([conv_w_full] if conv_w_full is not None else [])
    return _Job(arrays, [jax.ShapeDtypeStruct(a.shape, a.dtype) for a in arrays], {k: k for k in range(len(arrays))},
                (len(arrays), 6), start, finish)


def _core_halves(ref, kind, shape, c):
    return [_chip_window(ref, kind, shape, chip, c) for chip in range(N_CHIPS)]


def _sibling_job(grads, names, dtype=F32):
    def start(ins, outs, send_sems, recv_sems):
        chip, c = _place()
        for k, name in enumerate(names):
            kind, shape = LARGE[name], LARGE_SHAPE[name]
            if kind == "col":
                pairs = [(_row_half(ins[k], 1 - c), outs[k])]
            else:
                rows = shape[0] // N_DEV
                pairs = [(win, outs[k].at[pl.ds(j * rows, rows), :]) for j, win in enumerate(_core_halves(ins[k], kind, shape, 1 - c))]
            for src, dst in pairs:
                _remote(src, dst, send_sems, recv_sems, k, _chip_device(chip, 1 - c)).start()

    def finish(ins, outs, send_sems, recv_sems):
        chip, c = _place()
        for k in range(len(names)):
            _remote(outs[k], outs[k], send_sems, recv_sems, k, _chip_device(chip, 1 - c)).wait()

    return _Job([grads[name] for name in names],
                [jax.ShapeDtypeStruct((LARGE_SHAPE[name][0] // 2, LARGE_SHAPE[name][1]), dtype) for name in names], {},
                (len(names),), start, finish)


def _chip_sum(name, g, got, place):
    kind, (K, N) = LARGE[name], LARGE_SHAPE[name]
    rows = K // N_DEV
    piece_cols = N // N_CHIPS

    def body(place_ref, g_ref, got_ref, o_ref, ob_ref):
        total = g_ref[...] + got_ref[...].astype(F32)
        ob_ref[...] = total.astype(BF16)
        if kind == "col":
            for chip in range(N_CHIPS):
                @pl.when(place_ref[0] == chip)
                def _(chip=chip):
                    o_ref[...] = total[:, chip * piece_cols:(chip + 1) * piece_cols]
        else:
            @pl.when(pl.program_id(0) == place_ref[0])
            def _():
                o_ref[...] = total

    if kind == "col":
        mine = pl.BlockSpec((rows, N), lambda j, place_ref: (j + N_CHIPS * place_ref[1], 0))
        own = pl.BlockSpec((rows, piece_cols), lambda j, place_ref: (j, 0))
    else:
        mine = pl.BlockSpec((rows, N), lambda j, place_ref: (2 * j + place_ref[1], 0))
        own = pl.BlockSpec((rows, N), lambda j, place_ref: (0, 0))
    blk = pl.BlockSpec((rows, N), lambda j, place_ref: (j, 0))
    return pl.pallas_call(
        body, name=name + "_chip_sum",
        grid_spec=pltpu.PrefetchScalarGridSpec(num_scalar_prefetch=1, grid=(N_CHIPS,), in_specs=[mine, blk], out_specs=[own, blk]),
        out_shape=[jax.ShapeDtypeStruct(_piece_shape(name), F32), jax.ShapeDtypeStruct((K // 2, N), BF16)],
        compiler_params=_params(dimension_semantics=("arbitrary",)),
    )(place, g, got)


def _piece(ref, kind, shape, chip):
    K, N = shape
    if kind == "col":
        return ref.at[:, pl.ds(chip * (N // N_CHIPS), N // N_CHIPS)]
    return ref.at[pl.ds(chip * (K // N_DEV), K // N_DEV), :]


def _piece_shape(name):
    kind, (K, N) = LARGE[name], LARGE_SHAPE[name]
    return (K // 2, N // N_CHIPS) if kind == "col" else (K // N_DEV, N)


def _chips_job(sums, names):
    def copies(ins, outs, send_sems, recv_sems):
        chip, c = _place()
        return [_remote(_piece(ins[k], LARGE[name], LARGE_SHAPE[name], chip ^ (r + 1)), outs[k].at[r], send_sems, recv_sems, (k, r),
                        _chip_device(chip ^ (r + 1), c)) for k, name in enumerate(names) for r in range(3)]

    def start(*refs):
        for cp in copies(*refs):
            cp.start()

    def finish(*refs):
        for cp in copies(*refs):
            cp.wait()

    return _Job([sums[name] for name in names], [jax.ShapeDtypeStruct((3,) + _piece_shape(name), BF16) for name in names], {},
                (len(names), 3), start, finish)


def _final_sum(name, chip_sum, got, place):
    rows, cols = _piece_shape(name)

    def body(place_ref, s_ref, got_ref, o_ref):
        o_ref[...] = ((s_ref[...] + got_ref[0].astype(F32)) + got_ref[1].astype(F32)) + got_ref[2].astype(F32)

    mine = pl.BlockSpec((rows, cols), lambda i, place_ref: (0, 0))
    return pl.pallas_call(
        body, name=name + "_final_sum",
        grid_spec=pltpu.PrefetchScalarGridSpec(
            num_scalar_prefetch=1, grid=(1,), in_specs=[mine, pl.BlockSpec((3, rows, cols), lambda i, place_ref: (0, 0, 0))],
            out_specs=pl.BlockSpec((rows, cols), lambda i, place_ref: (place_ref[1], 0))),
        out_shape=jax.ShapeDtypeStruct((2 * rows, cols), F32),
        compiler_params=_params(dimension_semantics=("arbitrary",)),
    )(place, chip_sum, got)


def _join_job(shards, names):
    def half_copy(outs, send_sems, recv_sems, k, mine):
        chip, c = _place()
        win = _row_half(outs[k], c if mine else 1 - c)
        return _remote(win, win, send_sems, recv_sems, k, _chip_device(chip, 1 - c))

    def start(ins, outs, send_sems, recv_sems):
        for k in range(len(names)):
            half_copy(outs, send_sems, recv_sems, k, True).start()

    def finish(ins, outs, send_sems, recv_sems):
        for k in range(len(names)):
            half_copy(outs, send_sems, recv_sems, k, True).wait_send()
            half_copy(outs, send_sems, recv_sems, k, False).wait_recv()

    arrays = [shards[name] for name in names]
    return _Job(arrays, [jax.ShapeDtypeStruct(a.shape, F32) for a in arrays], {k: k for k in range(len(arrays))},
                (len(arrays),), start, finish)


VEC_ROWS = 16


def _all_reduce_small(slabs, narrow, jobs=()):
    n = len(slabs)
    narrowed = [k for k in range(n) if narrow[k]]

    def body(*refs):
        in_refs, out_refs, got_refs = refs[:n], refs[n:2 * n], refs[2 * n:3 * n]
        bf16_refs = dict(zip(narrowed, refs[3 * n:3 * n + len(narrowed)]))
        send_sems, recv_sems = refs[3 * n + len(narrowed):]
        x, y, c = lax.axis_index("x"), lax.axis_index("y"), lax.axis_index("c")
        me = 4 * x + 2 * y + c
        for k, ref in bf16_refs.items():
            ref[...] = in_refs[k][...].astype(BF16)
        partial_refs = [bf16_refs.get(k, in_refs[k]) for k in range(n)]

        def remote(src, dst, k, phase, r):
            other = me ^ r
            return pltpu.make_async_remote_copy(src_ref=src, dst_ref=dst, send_sem=send_sems.at[k, phase, r],
                                                recv_sem=recv_sems.at[k, phase, r],
                                                device_id=(other // 4, (other // 2) % 2, other % 2), device_id_type=MESH)

        scatter = [remote(partial_refs[k].at[me ^ r], got_refs[k].at[r], k, 0, r) for r in range(1, N_DEV) for k in range(n)]
        for cp in scatter:
            cp.start()
        for cp in scatter:
            cp.wait()
        for k in range(n):
            total = in_refs[k][me]
            for r in range(1, N_DEV):
                total = total + got_refs[k][r].astype(F32)
            out_refs[k][me] = total
        gather = [remote(out_refs[k].at[me], out_refs[k].at[me], k, 1, r) for r in range(1, N_DEV) for k in range(n)]
        for cp in gather:
            cp.start()
        for r in range(1, N_DEV):
            for k in range(n):
                remote(out_refs[k].at[me ^ r], out_refs[k].at[me ^ r], k, 1, r).wait_recv()
        for cp in gather:
            cp.wait_send()

    return _pallas(
        body, slabs, name="all_reduce_small", grid=(), in_specs=[VMEM] * n, out_specs=[VMEM] * n,
        out_shape=[jax.ShapeDtypeStruct(s.shape, F32) for s in slabs],
        scratch_shapes=[pltpu.VMEM(s.shape, BF16 if narrow[k] else F32) for k, s in enumerate(slabs)]
        + [pltpu.VMEM(slabs[k].shape, BF16) for k in narrowed]
        + [pltpu.SemaphoreType.DMA((n, 2, N_DEV)), pltpu.SemaphoreType.DMA((n, 2, N_DEV))], jobs=jobs)


def _cast_into_whole(w, name, place):
    rows, cols = w.shape
    tr = rows // 2

    def body(place_ref, w_ref, o_ref):
        o_ref[...] = w_ref[...].astype(BF16)

    if LARGE[name] == "col":
        window = pl.BlockSpec((tr, cols), lambda i, place_ref: (i, place_ref[0]))
    else:
        window = pl.BlockSpec((tr, cols), lambda i, place_ref: (2 * place_ref[0] + i, 0))
    return pl.pallas_call(
        body, name=name + "_cast",
        grid_spec=pltpu.PrefetchScalarGridSpec(num_scalar_prefetch=1, grid=(2,),
                                               in_specs=[pl.BlockSpec((tr, cols), lambda i, place_ref: (i, 0))], out_specs=window),
        out_shape=jax.ShapeDtypeStruct(LARGE_SHAPE[name], BF16),
        compiler_params=_params(dimension_semantics=("parallel",)))(place, w)


def _cast_many_into_whole(shards, place, jobs):
    names = list(shards)
    n = len(names)

    def body(place_ref, *refs):
        for w_ref, o_ref in zip(refs[:n], refs[n:]):
            o_ref[...] = w_ref[...].astype(BF16)

    def window(name):
        rows, cols = shards[name].shape
        if LARGE[name] == "col":
            return pl.BlockSpec((rows // 2, cols), lambda i, place_ref: (i, place_ref[0]))
        return pl.BlockSpec((rows // 2, cols), lambda i, place_ref: (2 * place_ref[0] + i, 0))

    def half(name):
        rows, cols = shards[name].shape
        return pl.BlockSpec((rows // 2, cols), lambda i, place_ref: (i, 0))

    return _pallas(body, [shards[name] for name in names], name="cast_weights", grid=(2,),
                   in_specs=[half(name) for name in names], out_specs=[window(name) for name in names],
                   out_shape=[jax.ShapeDtypeStruct(LARGE_SHAPE[name], BF16) for name in names],
                   semantics=("arbitrary",), jobs=jobs, prefetch=place)


def _adamw_math(w, g, m, v):
    m = ADAM_B1 * m + (1.0 - ADAM_B1) * g
    v = ADAM_B2 * v + (1.0 - ADAM_B2) * (g * g)
    m_hat = m / (1.0 - ADAM_B1 ** ADAM_STEP)
    v_hat = v / (1.0 - ADAM_B2 ** ADAM_STEP)
    delta = -ADAM_LR * (m_hat / (jnp.sqrt(v_hat) + ADAM_EPS) + ADAM_WD * w)
    return delta, m, v


def _adamw_large(w, g, m, v, name):
    rows, cols = w.shape
    steps = 2
    tr = rows // steps

    def body(w_ref, g_ref, m_ref, v_ref, d_ref, mo_ref, vo_ref):
        d_ref[...], mo_ref[...], vo_ref[...] = _adamw_math(w_ref[...], g_ref[...], m_ref[...], v_ref[...])

    blk = pl.BlockSpec((tr, cols), lambda i: (i, 0))
    out = jax.ShapeDtypeStruct(w.shape, F32)
    return pl.pallas_call(body, name=name + "_adamw", grid=(steps,), in_specs=[blk] * 4, out_specs=[blk] * 3, out_shape=[out] * 3,
                          compiler_params=_params(dimension_semantics=("parallel",)))(w, g, m, v)


def _adamw_small(ws, gs, ms, vs):
    n = len(ws)

    def body(*refs):
        for k in range(n):
            w_ref, g_ref, m_ref, v_ref = (refs[q * n + k] for q in range(4))
            d_ref, mo_ref, vo_ref = (refs[(4 + q) * n + k] for q in range(3))
            d_ref[...], mo_ref[...], vo_ref[...] = _adamw_math(w_ref[...], g_ref[...], m_ref[...], v_ref[...])

    out = [jax.ShapeDtypeStruct(w.shape, F32) for w in ws]
    res = pl.pallas_call(body, name="small_adamw", in_specs=[VMEM] * (4 * n), out_specs=[VMEM] * (3 * n), out_shape=out * 3,
                         compiler_params=_params())(*ws, *gs, *ms, *vs)
    return res[:n], res[n:2 * n], res[2 * n:]


WEIGHTS = ["norm_mix", "w_in", "w_pool_grp", "pool_scale", "w_pool_out", "conv_w", "conv_b", "w_rg_a", "b_rg_a", "w_rg_x",
           "b_rg_x", "lru_lambda", "w_rnn_out", "w_o", "norm_ffn", "w_ffn_in", "w_ffn_out", "norm_final"]
VEC_ITEMS = ["norm_mix", "norm_ffn", "norm_final", "pool_scale", "conv_b", "lru_lambda", "b_rg_a", "b_rg_x"]
MAT_ITEMS = ["w_pool_grp", "w_rg_a", "w_rg_x"]


def _as2d(name, a):
    if name in MAT_ITEMS:
        return a.reshape(-1, HEAD, HEAD)
    if name == "conv_w":
        return a.reshape(CONV_WIDTH, -1)
    return a.reshape(1, -1)


def kernel(x, norm_mix, w_in, w_pool_grp, pool_scale, w_pool_out, conv_w, conv_b, w_rg_a, b_rg_a, w_rg_x, b_rg_x, lru_lambda, w_rnn_out, w_o, norm_ffn, w_ffn_in, w_ffn_out, norm_final, loss_target, m_norm_mix, m_w_in, m_w_pool_grp, m_pool_scale, m_w_pool_out, m_conv_w, m_conv_b, m_w_rg_a, m_b_rg_a, m_w_rg_x, m_b_rg_x, m_lru_lambda, m_w_rnn_out, m_w_o, m_norm_ffn, m_w_ffn_in, m_w_ffn_out, m_norm_final, v_norm_mix, v_w_in, v_w_pool_grp, v_pool_scale, v_w_pool_out, v_conv_w, v_conv_b, v_w_rg_a, v_b_rg_a, v_w_rg_x, v_b_rg_x, v_lru_lambda, v_w_rnn_out, v_w_o, v_norm_ffn, v_w_ffn_in, v_w_ffn_out, v_norm_final):
    given = dict(locals())
    w = {name: given[name] for name in WEIGHTS}
    m = {name: given["m_" + name] for name in WEIGHTS}
    v = {name: given["v_" + name] for name in WEIGHTS}
    chip, c = _place()

    place = jnp.stack([chip, c]).astype(jnp.int32)
    conv_cols = w["conv_w"].shape[-1]
    conv_w_mine = lax.dynamic_update_slice_in_dim(jnp.zeros((CONV_WIDTH, D_RNN), F32), w["conv_w"][0], chip * conv_cols, axis=1)
    w_in_mine = _cast_into_whole(w["w_in"][0], "w_in", place)
    later = [name for name in LARGE if name != "w_in"]
    casts, ((w_in_full, conv_w_full),) = _cast_many_into_whole(
        {name: w[name][0] for name in later}, place, jobs=[_gather_job({"w_in": w_in_mine}, ["w_in"], conv_w_mine)])
    full = dict(zip(later, casts), w_in=w_in_full)
    small = {name: _as2d(name, w[name]) for name in WEIGHTS if name not in LARGE and name != "conv_w"}
    sq_cols, grad_x, grads = _step(x[0], loss_target[0], small, full, conv_w_full, place)
    loss = 0.5 / D_MODEL * jnp.sum(sq_cols)
    grads["conv_w"] = lax.dynamic_slice_in_dim(grads["conv_w"], chip * conv_cols, conv_cols, axis=1)

    delta, new_m, new_v = {}, {}, {}
    for name in LARGE:
        delta[name], new_m[name], new_v[name] = _adamw_large(w[name][0], grads[name], m[name][0], v[name][0], name)
    small_names = [name for name in WEIGHTS if name not in LARGE]
    flat = lambda d: [d[name].reshape(grads[name].shape) for name in small_names]
    ds, mo, vo = _adamw_small(flat(w), [grads[name] for name in small_names], flat(m), flat(v))
    for k, name in enumerate(small_names):
        delta[name], new_m[name], new_v[name] = ds[k], mo[k], vo[k]

    shaped = lambda d: [d[name].reshape(w[name].shape) for name in WEIGHTS]
    return (loss, grad_x[None], *shaped(grads), *shaped(delta), *shaped(new_m), *shaped(new_v))
```

```python
import functools
import math

import jax
import jax.numpy as jnp
from jax import lax
from jax.experimental import pallas as pl
from jax.experimental.pallas import tpu as pltpu

F32 = jnp.float32
BF16 = jnp.bfloat16

D_MODEL = 1024
D_POOL = 512
N_POOL_GROUPS = 4
D_RNN = 1024
N_RNN_HEADS = 8
HEAD = 128
CONV_WIDTH = 4
LRU_C = 8.0
D_FF = 2816
D_IN = D_POOL + 2 * D_RNN + 2 * D_MODEL
NORM_EPS = 1e-6
COL_RNN = D_POOL // HEAD
COL_GATE = (D_POOL + D_RNN) // HEAD

ADAM_LR = 0.001
ADAM_B1 = 0.9
ADAM_B2 = 0.999
ADAM_EPS = 1e-08
ADAM_WD = 0.01
ADAM_STEP = 10

N_CHIPS = 4
N_DEV = 8
MESH = pl.DeviceIdType.MESH
ANY = pl.BlockSpec(memory_space=pl.ANY)
VMEM = pl.BlockSpec(memory_space=pltpu.VMEM)
VMEM_LIMIT_BYTES = 60 * 1024 * 1024
SUBLANES = 8
POOL_HALO = 16
CHUNK = 1024

GELU_C = math.sqrt(2.0 / math.pi)
GELU_A = 0.044715


def _params(**kw):
    return pltpu.CompilerParams(vmem_limit_bytes=VMEM_LIMIT_BYTES, **kw)


def _sigmoid(x):
    return 0.5 * jnp.tanh(0.5 * x) + 0.5


def _log1p(y):
    u = 1.0 + y
    d = u - 1.0
    return jnp.where(d == 0.0, y, jnp.log(u) * (y / jnp.where(d == 0.0, 1.0, d)))


def _gelu_parts(x):
    x2 = x * x
    th = jnp.tanh(GELU_C * (x + GELU_A * x * x2))
    g = 0.5 * x * (1.0 + th)
    dg = 0.5 * (1.0 + th) + 0.5 * x * (1.0 - th * th) * GELU_C * (1.0 + 3.0 * GELU_A * x2)
    return g, dg


def _dot(a, b):
    return jnp.dot(a, b, preferred_element_type=F32)


def _dot_nt(a, b):
    return lax.dot_general(a, b, (((1,), (1,)), ((), ())), preferred_element_type=F32)


def _dot_tn(a, b):
    return lax.dot_general(a, b, (((0,), (0,)), ((), ())), preferred_element_type=F32)


def _rms_scale(xv):
    return lax.rsqrt(jnp.mean(xv * xv, axis=-1, keepdims=True) + NORM_EPS)


def _rms_bwd(dy, xv, g):
    r = _rms_scale(xv)
    xh = xv * r
    dyg = dy * g
    dx = r * (dyg - xh * jnp.mean(dyg * xh, axis=-1, keepdims=True))
    return dx, dy * xh


class _Job:
    def __init__(self, inputs, out_shapes, aliases, sem_shape, start, finish):
        self.inputs, self.out_shapes, self.aliases, self.sem_shape = list(inputs), list(out_shapes), dict(aliases), sem_shape
        self.start, self.finish = start, finish


def _pallas(body, operands, *, name, grid, in_specs, out_specs, out_shape, scratch_shapes=(), semantics=None, jobs=(),
            prefetch=None):
    n_in, n_out, n_scr = len(in_specs), len(out_specs), len(scratch_shapes)
    n_pre = 0 if prefetch is None else 1
    job_in = [a for job in jobs for a in job.inputs]
    job_out = [s for job in jobs for s in job.out_shapes]
    aliases, i0, o0 = {}, n_pre + n_in, n_out
    for job in jobs:
        aliases.update({i0 + i: o0 + o for i, o in job.aliases.items()})
        i0, o0 = i0 + len(job.inputs), o0 + len(job.out_shapes)

    def whole(*refs):
        pre, refs = refs[:n_pre], refs[n_pre:]
        ins, j_ins = refs[:n_in], refs[n_in:n_in + len(job_in)]
        outs = refs[n_in + len(job_in):][:n_out]
        j_outs = refs[n_in + len(job_in) + n_out:][:len(job_out)]
        rest = refs[n_in + len(job_in) + n_out + len(job_out):]
        scr, sems = rest[:n_scr], rest[n_scr:]

        def run(phase):
            i, o = 0, 0
            for k, job in enumerate(jobs):
                getattr(job, phase)(j_ins[i:i + len(job.inputs)], j_outs[o:o + len(job.out_shapes)], sems[2 * k], sems[2 * k + 1])
                i, o = i + len(job.inputs), o + len(job.out_shapes)

        def at(step_of, phase):
            if not jobs:
                return
            if not grid:
                run(phase)
                return
            cond = functools.reduce(jnp.logical_and, [pl.program_id(d) == step_of(d) for d in range(len(grid))])
            pl.when(cond)(functools.partial(run, phase))

        at(lambda d: 0, "start")
        body(*pre, *ins, *outs, *scr)
        at(lambda d: grid[d] - 1, "finish")

    layout = dict(grid=grid, in_specs=list(in_specs) + [ANY] * len(job_in), out_specs=list(out_specs) + [ANY] * len(job_out),
                  scratch_shapes=list(scratch_shapes) + [pltpu.SemaphoreType.DMA(job.sem_shape) for job in jobs for _ in range(2)])
    if prefetch is not None:
        layout = dict(grid_spec=pltpu.PrefetchScalarGridSpec(num_scalar_prefetch=1, **layout))
    res = pl.pallas_call(
        whole, name=name, out_shape=list(out_shape) + job_out, input_output_aliases=aliases,
        compiler_params=_params(dimension_semantics=semantics, has_side_effects=bool(jobs)), **layout,
    )(*([] if prefetch is None else [prefetch]), *operands, *job_in)
    per_job, o = [], n_out
    for job in jobs:
        per_job.append(res[o:o + len(job.out_shapes)])
        o += len(job.out_shapes)
    return res[:n_out], per_job


def _run_jobs(jobs, name):
    return _pallas(lambda: None, [], name=name, grid=(), in_specs=[], out_specs=[], out_shape=[], jobs=jobs)[1]


NORM_ROWS = 256
EPILOGUE_ROWS = 512


def _norm_rows(x_ref, g_ref, h_ref):
    g = g_ref[...]

    def rows(i, carry):
        r = pl.ds(pl.multiple_of(i * NORM_ROWS, NORM_ROWS), NORM_ROWS)
        xv = x_ref[r, :]
        h_ref[r, :] = (xv * _rms_scale(xv) * g).astype(BF16)
        return carry

    lax.fori_loop(0, x_ref.shape[0] // NORM_ROWS, rows, 0)


def _norm_matmul(x, g, w, *, tm, tn, name, jobs=()):
    T, K = x.shape
    N = w.shape[1]

    def body(x_ref, g_ref, w_ref, o_ref, h_ref):
        @pl.when(pl.program_id(1) == 0)
        def _():
            _norm_rows(x_ref, g_ref, h_ref)

        o_ref[...] = _dot(h_ref[...], w_ref[...])

    return _pallas(
        body, (x, g, w), name=name, grid=(T // tm, N // tn),
        in_specs=[pl.BlockSpec((tm, K), lambda i, j: (i, 0)), pl.BlockSpec((1, K), lambda i, j: (0, 0)),
                  pl.BlockSpec((K, tn), lambda i, j: (0, j))],
        out_specs=[pl.BlockSpec((tm, tn), lambda i, j: (i, j)), pl.BlockSpec((tm, K), lambda i, j: (i, 0))],
        out_shape=[jax.ShapeDtypeStruct((T, N), F32), jax.ShapeDtypeStruct((T, K), BF16)],
        semantics=("parallel", "arbitrary"), jobs=jobs)


def _ffn_in(x2, g, w, *, tm, tn, jobs=()):
    T, K = x2.shape
    nb = D_FF // tn

    def body(x_ref, g_ref, wg_ref, wu_ref, dup_ref, dgate_ref, act_ref, h_ref):
        @pl.when(pl.program_id(1) == 0)
        def _():
            _norm_rows(x_ref, g_ref, h_ref)

        wg, wu = wg_ref[...], wu_ref[...]
        for r in range(0, tm, EPILOGUE_ROWS):
            rows = pl.ds(r, min(EPILOGUE_ROWS, tm))
            h = h_ref[rows, :]
            gate, up = _dot(h, wg), _dot(h, wu)
            s = _sigmoid(gate)
            silu = gate * s
            dup_ref[rows, :] = silu.astype(BF16)
            dgate_ref[rows, :] = (up * (s + silu * (1.0 - s))).astype(BF16)
            act_ref[rows, :] = (silu * up).astype(BF16)

    blk = pl.BlockSpec((tm, tn), lambda i, j: (i, j))
    return _pallas(
        body, (x2, g, w, w), name="ffn_in", grid=(T // tm, nb),
        in_specs=[pl.BlockSpec((tm, K), lambda i, j: (i, 0)), pl.BlockSpec((1, K), lambda i, j: (0, 0)),
                  pl.BlockSpec((K, tn), lambda i, j: (0, j)), pl.BlockSpec((K, tn), lambda i, j: (0, j + nb))],
        out_specs=[blk, blk, blk, pl.BlockSpec((tm, K), lambda i, j: (i, 0))],
        out_shape=[jax.ShapeDtypeStruct((T, D_FF), BF16), jax.ShapeDtypeStruct((T, D_FF), BF16),
                   jax.ShapeDtypeStruct((T, D_FF), BF16), jax.ShapeDtypeStruct((T, K), BF16)],
        semantics=("parallel", "arbitrary"), jobs=jobs)


def _branch_mix(pm, z, w_pool_out, w_rnn_out, proj, *, tm, tn):
    T = pm.shape[0]
    col_gp = (D_POOL + 2 * D_RNN) // tn
    col_gr = col_gp + D_MODEL // tn

    def body(pm_ref, z_ref, wp_ref, wr_ref, gp_ref, gr_ref, by_gp_ref, by_gr_ref, sp_ref, sr_ref, mix_ref):
        wp, wr = wp_ref[...], wr_ref[...]
        for r in range(0, tm, EPILOGUE_ROWS):
            rows = pl.ds(r, min(EPILOGUE_ROWS, tm))
            yp, yr = _dot(pm_ref[rows, :], wp), _dot(z_ref[rows, :], wr)
            sp, sr = _sigmoid(gp_ref[rows, :]), _sigmoid(gr_ref[rows, :])
            by_gp_ref[rows, :] = (yp * sp * (1.0 - sp)).astype(BF16)
            by_gr_ref[rows, :] = (yr * sr * (1.0 - sr)).astype(BF16)
            sp_ref[rows, :] = sp.astype(BF16)
            sr_ref[rows, :] = sr.astype(BF16)
            mix_ref[rows, :] = (sp * yp + sr * yr).astype(BF16)

    blk = pl.BlockSpec((tm, tn), lambda i, j: (i, j))
    out = jax.ShapeDtypeStruct((T, D_MODEL), BF16)
    return pl.pallas_call(
        body, name="branch_mix", grid=(T // tm, D_MODEL // tn),
        in_specs=[pl.BlockSpec((tm, D_POOL), lambda i, j: (i, 0)), pl.BlockSpec((tm, D_RNN), lambda i, j: (i, 0)),
                  pl.BlockSpec((D_POOL, tn), lambda i, j: (0, j)), pl.BlockSpec((D_RNN, tn), lambda i, j: (0, j)),
                  pl.BlockSpec((tm, tn), lambda i, j: (i, col_gp + j)), pl.BlockSpec((tm, tn), lambda i, j: (i, col_gr + j))],
        out_specs=[blk] * 5, out_shape=[out] * 5,
        compiler_params=_params(dimension_semantics=("parallel", "parallel")),
    )(pm, z, w_pool_out, w_rnn_out, proj, proj)


def _out_proj_residual(mix, w_o, x, *, tm):
    T = x.shape[0]

    def body(mix_ref, w_ref, x_ref, o_ref):
        o_ref[...] = x_ref[...] + _dot(mix_ref[...], w_ref[...])

    row = pl.BlockSpec((tm, D_MODEL), lambda i: (i, 0))
    return pl.pallas_call(
        body, name="out_proj_residual", grid=(T // tm,),
        in_specs=[row, pl.BlockSpec((D_MODEL, D_MODEL), lambda i: (0, 0)), row],
        out_specs=row, out_shape=jax.ShapeDtypeStruct((T, D_MODEL), F32),
        compiler_params=_params(dimension_semantics=("parallel",)),
    )(mix, w_o, x)


def _ffn_out_loss(act, w, x2, g3, target, *, tm):
    T = x2.shape[0]

    def body(act_ref, w_ref, x2_ref, g_ref, t_ref, dx_ref, dxb_ref, sq_ref, dg_ref):
        @pl.when(pl.program_id(0) == 0)
        def _():
            sq_ref[...] = jnp.zeros_like(sq_ref)
            dg_ref[...] = jnp.zeros_like(dg_ref)

        g, w = g_ref[...], w_ref[...]
        for r in range(0, tm, NORM_ROWS):
            rows = pl.ds(r, min(NORM_ROWS, tm))
            x3 = x2_ref[rows, :] + _dot(act_ref[rows, :], w)
            err = x3 * _rms_scale(x3) * g - t_ref[rows, :]
            sq_ref[...] += jnp.sum(err * err, axis=0, keepdims=True)
            dx, dgp = _rms_bwd(err * (1.0 / D_MODEL), x3, g)
            dg_ref[...] += jnp.sum(dgp, axis=0, keepdims=True)
            dx_ref[rows, :] = dx
            dxb_ref[rows, :] = dx.astype(BF16)

    row = pl.BlockSpec((tm, D_MODEL), lambda i: (i, 0))
    vec = pl.BlockSpec((1, D_MODEL), lambda i: (0, 0))
    return pl.pallas_call(
        body, name="ffn_out_loss", grid=(T // tm,),
        in_specs=[pl.BlockSpec((tm, D_FF), lambda i: (i, 0)), pl.BlockSpec((D_FF, D_MODEL), lambda i: (0, 0)), row, vec, row],
        out_specs=[row, row, vec, vec],
        out_shape=[jax.ShapeDtypeStruct((T, D_MODEL), F32), jax.ShapeDtypeStruct((T, D_MODEL), BF16),
                   jax.ShapeDtypeStruct((1, D_MODEL), F32), jax.ShapeDtypeStruct((1, D_MODEL), F32)],
        compiler_params=_params(dimension_semantics=("arbitrary",)),
    )(act, w, x2, g3, target)


def _ffn_out_bwd(dx3b, w, act_by_gate, act_by_up, *, tm, tn):
    T = dx3b.shape[0]

    def body(dx_ref, w_ref, by_gate_ref, by_up_ref, dgate_ref, dup_ref):
        w = w_ref[...]
        for r in range(0, tm, EPILOGUE_ROWS):
            rows = pl.ds(r, min(EPILOGUE_ROWS, tm))
            dact = _dot_nt(dx_ref[rows, :], w)
            dgate_ref[rows, :] = (dact * by_gate_ref[rows, :].astype(F32)).astype(BF16)
            dup_ref[rows, :] = (dact * by_up_ref[rows, :].astype(F32)).astype(BF16)

    blk = pl.BlockSpec((tm, tn), lambda i, j: (i, j))
    return pl.pallas_call(
        body, name="ffn_out_bwd", grid=(T // tm, D_FF // tn),
        in_specs=[pl.BlockSpec((tm, D_MODEL), lambda i, j: (i, 0)), pl.BlockSpec((tn, D_MODEL), lambda i, j: (j, 0)), blk, blk],
        out_specs=[blk, blk],
        out_shape=[jax.ShapeDtypeStruct((T, D_FF), BF16), jax.ShapeDtypeStruct((T, D_FF), BF16)],
        compiler_params=_params(dimension_semantics=("parallel", "parallel")),
    )(dx3b, w, act_by_gate, act_by_up)


def _ffn_in_bwd(dgate, dup, w, dx3, x2, g2, *, tm, jobs=()):
    T = x2.shape[0]

    def body(dgate_ref, dup_ref, w_ref, dx3_ref, x2_ref, g_ref, dx_ref, dxb_ref, dg_ref):
        @pl.when(pl.program_id(0) == 0)
        def _():
            dg_ref[...] = jnp.zeros_like(dg_ref)

        g = g_ref[...]
        for r in range(0, tm, NORM_ROWS):
            rows = pl.ds(r, min(NORM_ROWS, tm))
            dh = _dot_nt(dgate_ref[rows, :], w_ref[:, :D_FF]) + _dot_nt(dup_ref[rows, :], w_ref[:, D_FF:])
            dxn, dgp = _rms_bwd(dh, x2_ref[rows, :], g)
            dx = dx3_ref[rows, :] + dxn
            dg_ref[...] += jnp.sum(dgp, axis=0, keepdims=True)
            dx_ref[rows, :] = dx
            dxb_ref[rows, :] = dx.astype(BF16)

    row = pl.BlockSpec((tm, D_MODEL), lambda i: (i, 0))
    wide = pl.BlockSpec((tm, D_FF), lambda i: (i, 0))
    vec = pl.BlockSpec((1, D_MODEL), lambda i: (0, 0))
    return _pallas(
        body, (dgate, dup, w, dx3, x2, g2), name="ffn_in_bwd", grid=(T // tm,),
        in_specs=[wide, wide, pl.BlockSpec((D_MODEL, 2 * D_FF), lambda i: (0, 0)), row, row, vec],
        out_specs=[row, row, vec],
        out_shape=[jax.ShapeDtypeStruct((T, D_MODEL), F32), jax.ShapeDtypeStruct((T, D_MODEL), BF16),
                   jax.ShapeDtypeStruct((1, D_MODEL), F32)],
        semantics=("arbitrary",), jobs=jobs)


def _out_proj_bwd(dx2b, w_o, mix_by, *, tm, tn, jobs=()):
    T = dx2b.shape[0]

    def body(dx_ref, w_ref, *refs):
        w = w_ref[...]
        for r in range(0, tm, EPILOGUE_ROWS):
            rows = pl.ds(r, min(EPILOGUE_ROWS, tm))
            dmix = _dot_nt(dx_ref[rows, :], w)
            for by_ref, d_ref in zip(refs[:4], refs[4:]):
                d_ref[rows, :] = (dmix * by_ref[rows, :].astype(F32)).astype(BF16)

    blk = pl.BlockSpec((tm, tn), lambda i, j: (i, j))
    out = jax.ShapeDtypeStruct((T, D_MODEL), BF16)
    return _pallas(
        body, (dx2b, w_o, *mix_by), name="out_proj_bwd", grid=(T // tm, D_MODEL // tn),
        in_specs=[pl.BlockSpec((tm, D_MODEL), lambda i, j: (i, 0)), pl.BlockSpec((tn, D_MODEL), lambda i, j: (j, 0))] + [blk] * 4,
        out_specs=[blk] * 4, out_shape=[out] * 4, semantics=("parallel", "parallel"), jobs=jobs)


def _branch_bwd(dyp, dyr, w_pool_out, w_rnn_out, *, tm):
    T = dyp.shape[0]

    def body(dyp_ref, dyr_ref, wp_ref, wr_ref, dpm_ref, dz_ref):
        dpm_ref[...] = _dot_nt(dyp_ref[...], wp_ref[...])
        dz_ref[...] = _dot_nt(dyr_ref[...], wr_ref[...])

    row = pl.BlockSpec((tm, D_MODEL), lambda i: (i, 0))
    return pl.pallas_call(
        body, name="branch_bwd", grid=(T // tm,),
        in_specs=[row, row, pl.BlockSpec((D_POOL, D_MODEL), lambda i: (0, 0)), pl.BlockSpec((D_RNN, D_MODEL), lambda i: (0, 0))],
        out_specs=[pl.BlockSpec((tm, D_POOL), lambda i: (i, 0)), pl.BlockSpec((tm, D_RNN), lambda i: (i, 0))],
        out_shape=[jax.ShapeDtypeStruct((T, D_POOL), F32), jax.ShapeDtypeStruct((T, D_RNN), F32)],
        compiler_params=_params(dimension_semantics=("parallel",)),
    )(dyp, dyr, w_pool_out, w_rnn_out)


def _in_proj_bwd(segs, w, dx2, x, g1, *, tm, jobs=()):
    T = x.shape[0]
    widths = [s.shape[1] for s in segs]
    offs = [sum(widths[:k]) for k in range(len(widths))]
    n = len(segs)

    def body(*refs):
        seg_refs, (w_ref, dx2_ref, x_ref, g_ref, dx_ref, dg_ref) = refs[:n], refs[n:]

        @pl.when(pl.program_id(0) == 0)
        def _():
            dg_ref[...] = jnp.zeros_like(dg_ref)

        g = g_ref[...]
        for r in range(0, tm, NORM_ROWS):
            rows = pl.ds(r, min(NORM_ROWS, tm))
            dh = _dot_nt(seg_refs[0][rows, :], w_ref[:, offs[0]:offs[0] + widths[0]])
            for k in range(1, n):
                dh += _dot_nt(seg_refs[k][rows, :], w_ref[:, offs[k]:offs[k] + widths[k]])
            dxn, dgp = _rms_bwd(dh, x_ref[rows, :], g)
            dg_ref[...] += jnp.sum(dgp, axis=0, keepdims=True)
            dx_ref[rows, :] = dx2_ref[rows, :] + dxn

    row = pl.BlockSpec((tm, D_MODEL), lambda i: (i, 0))
    vec = pl.BlockSpec((1, D_MODEL), lambda i: (0, 0))
    return _pallas(
        body, (*segs, w, dx2, x, g1), name="in_proj_bwd", grid=(T // tm,),
        in_specs=[pl.BlockSpec((tm, wd), lambda i: (i, 0)) for wd in widths]
        + [pl.BlockSpec((D_MODEL, D_IN), lambda i: (0, 0)), row, row, vec],
        out_specs=[row, vec],
        out_shape=[jax.ShapeDtypeStruct((T, D_MODEL), F32), jax.ShapeDtypeStruct((1, D_MODEL), F32)],
        semantics=("arbitrary",), jobs=jobs)


def _weight_grad(a, segs, *, tm, tn, name, jobs=None, also_bf16=False):
    T, M = a.shape
    nblk = [s.shape[1] // tn for s in segs]
    first = [sum(nblk[:k]) for k in range(len(segs))]
    n = len(segs)

    def body(a_ref, *refs):
        seg_refs, o_refs = refs[:n], refs[n:]
        j = pl.program_id(1)
        for k in range(n):
            @pl.when((j >= first[k]) & (j < first[k] + nblk[k]))
            def _(k=k):
                grad = _dot_tn(a_ref[...], seg_refs[k][...])
                for o_ref in o_refs:
                    o_ref[...] = grad.astype(o_ref.dtype)

    def seg_spec(k):
        return pl.BlockSpec((T, tn), lambda i, j: (0, jnp.clip(j - first[k], 0, nblk[k] - 1)))

    dtypes = [F32, BF16] if also_bf16 else [F32]
    grad, results = _pallas(
        body, (a, *segs), name=name, grid=(M // tm, sum(nblk)),
        in_specs=[pl.BlockSpec((T, tm), lambda i, j: (0, i))] + [seg_spec(k) for k in range(n)],
        out_specs=[pl.BlockSpec((tm, tn), lambda i, j: (i, j))] * len(dtypes),
        out_shape=[jax.ShapeDtypeStruct((M, sum(nblk) * tn), dtype) for dtype in dtypes],
        semantics=("parallel", "arbitrary"), jobs=jobs or ())
    grad = tuple(grad) if also_bf16 else grad[0]
    return grad if jobs is None else (grad, results)


def _pad_front(dst, src, halo):
    dst[pl.ds(0, halo), :] = jnp.zeros((halo, src.shape[1]), F32)

    def fill(i, carry):
        r0 = pl.multiple_of(i * CHUNK, CHUNK)
        dst[pl.ds(r0 + halo, CHUNK), :] = src[pl.ds(r0, CHUNK), :]
        return carry

    lax.fori_loop(0, src.shape[0] // CHUNK, fill, 0)


def _shift_rows(v, k):
    return pltpu.roll(v, k % v.shape[0], axis=0)


def _window_sums(xs, direction):
    s2 = xs + _shift_rows(xs, direction)
    s4 = s2 + _shift_rows(s2, 2 * direction)
    s8 = s4 + _shift_rows(s4, 4 * direction)
    s16 = s8 + _shift_rows(s8, 8 * direction)
    return s2, s4, s8, s16


def _select_window(g, sums):
    s2, s4, s8, s16 = sums
    return jnp.where(g == 0, s2, jnp.where(g == 1, s4, jnp.where(g == 2, s8, s16)))


def _pool_count(g, start, rows):
    t = start + lax.broadcasted_iota(jnp.int32, (rows, 1), 0)
    return jnp.minimum(t + 1, jnp.left_shift(2, g)).astype(F32)


def _pool_fwd(proj, w_grp, scale):
    T = proj.shape[0]
    nchunk = T // CHUNK

    def body(u_ref, w_ref, s_ref, o_ref, upad):
        g = pl.program_id(0)
        _pad_front(upad, u_ref, POOL_HALO)
        w = w_ref[...].astype(BF16)
        scale_row = s_ref[...]

        def chunk(i, carry):
            r0 = pl.multiple_of(i * CHUNK, CHUNK)
            xs = upad[pl.ds(r0, CHUNK + POOL_HALO), :]
            win = _select_window(g, _window_sums(xs, 1))[POOL_HALO:]
            pooled = win / _pool_count(g, r0, CHUNK) - xs[POOL_HALO:]
            o_ref[pl.ds(r0, CHUNK), :] = (_dot(pooled.astype(BF16), w) * scale_row).astype(BF16)
            return carry

        lax.fori_loop(0, nchunk, chunk, 0)

    return pl.pallas_call(
        body, name="pool_fwd", grid=(N_POOL_GROUPS,),
        in_specs=[pl.BlockSpec((T, HEAD), lambda g: (0, g)), pl.BlockSpec((None, HEAD, HEAD), lambda g: (g, 0, 0)),
                  pl.BlockSpec((1, HEAD), lambda g: (0, g))],
        out_specs=pl.BlockSpec((T, HEAD), lambda g: (0, g)),
        out_shape=jax.ShapeDtypeStruct((T, D_POOL), BF16),
        scratch_shapes=[pltpu.VMEM((T + POOL_HALO, HEAD), F32)],
        compiler_params=_params(dimension_semantics=("parallel",)),
    )(proj, w_grp, scale)


def _pool_bwd(proj, dpm, w_grp, scale, jobs=()):
    T = proj.shape[0]
    nchunk = T // CHUNK

    def body(u_ref, dpm_ref, w_ref, s_ref, du_ref, dw_ref, ds_ref, upad, zpad, dpool):
        g = pl.program_id(0)
        _pad_front(upad, u_ref, POOL_HALO)
        zpad[pl.ds(T, POOL_HALO), :] = jnp.zeros((POOL_HALO, HEAD), F32)
        dw_ref[...] = jnp.zeros_like(dw_ref)
        ds_ref[...] = jnp.zeros_like(ds_ref)
        w = w_ref[...].astype(BF16)
        scale_row = s_ref[...]

        def chunk(i, carry):
            r0 = pl.multiple_of(i * CHUNK, CHUNK)
            xs = upad[pl.ds(r0, CHUNK + POOL_HALO), :]
            cnt = _pool_count(g, r0, CHUNK)
            pooled = (_select_window(g, _window_sums(xs, 1))[POOL_HALO:] / cnt - xs[POOL_HALO:]).astype(BF16)
            mixed = _dot(pooled, w)
            d = dpm_ref[pl.ds(r0, CHUNK), :]
            ds_ref[...] += jnp.sum(d * mixed, axis=0, keepdims=True)
            dmixed = (d * scale_row).astype(BF16)
            dw_ref[...] += _dot_tn(pooled, dmixed)
            dp = _dot_nt(dmixed, w)
            dpool[pl.ds(r0, CHUNK), :] = dp
            zpad[pl.ds(r0, CHUNK), :] = dp / cnt
            return carry

        lax.fori_loop(0, nchunk, chunk, 0)

        def chunk2(i, carry):
            r0 = pl.multiple_of(i * CHUNK, CHUNK)
            zs = zpad[pl.ds(r0, CHUNK + POOL_HALO), :]
            win = _select_window(g, _window_sums(zs, -1))[:CHUNK]
            du_ref[pl.ds(r0, CHUNK), :] = (win - dpool[pl.ds(r0, CHUNK), :]).astype(BF16)
            return carry

        lax.fori_loop(0, nchunk, chunk2, 0)

    col = pl.BlockSpec((T, HEAD), lambda g: (0, g))
    return _pallas(
        body, (proj, dpm, w_grp, scale), name="pool_bwd", grid=(N_POOL_GROUPS,),
        in_specs=[col, col, pl.BlockSpec((None, HEAD, HEAD), lambda g: (g, 0, 0)), pl.BlockSpec((1, HEAD), lambda g: (0, g))],
        out_specs=[col, pl.BlockSpec((None, HEAD, HEAD), lambda g: (g, 0, 0)), pl.BlockSpec((1, HEAD), lambda g: (0, g))],
        out_shape=[jax.ShapeDtypeStruct((T, D_POOL), BF16), jax.ShapeDtypeStruct((N_POOL_GROUPS, HEAD, HEAD), F32),
                   jax.ShapeDtypeStruct((1, D_POOL), F32)],
        scratch_shapes=[pltpu.VMEM((T + POOL_HALO, HEAD), F32), pltpu.VMEM((T + POOL_HALO, HEAD), F32), pltpu.VMEM((T, HEAD), F32)],
        semantics=("parallel",), jobs=jobs)


def _conv_taps(xs, cw):
    v = cw[CONV_WIDTH - 1] * xs[SUBLANES:]
    for k in range(CONV_WIDTH - 1):
        v += cw[k] * _shift_rows(xs, CONV_WIDTH - 1 - k)[SUBLANES:]
    return v


def _tap_rows(cw_ref):
    return [cw_ref[k:k + 1, :] for k in range(CONV_WIDTH)]


def _softplus_neg(lam):
    return jnp.maximum(-lam, 0.0) + _log1p(jnp.exp(-jnp.abs(lam)))


def _lru_gates(v, wa, ba, wx, bx, sp):
    vb = v.astype(BF16)
    ra = _sigmoid(_dot(vb, wa) + ba)
    ix = _sigmoid(_dot(vb, wx) + bx)
    log_a = -LRU_C * ra * sp
    a = jnp.exp(log_a)
    sq = jnp.sqrt(-jnp.tanh(log_a) * (a * a + 1.0))
    return ra, ix, a, sq


def _row_bcast(v, r):
    return jnp.broadcast_to(v[r:r + 1, :], v.shape)


TILE_BLOCK = 128


def _scan_in_tiles(coef, coef_shift, A_out, B, T, direction):
    order = list(range(SUBLANES)) if direction == 1 else list(range(SUBLANES - 1, -1, -1))
    tiles = min(TILE_BLOCK, T // SUBLANES)
    for base in range(0, T, tiles * SUBLANES):
        def rows(r, base=base):
            return pl.ds(base + r, tiles, stride=SUBLANES)

        A, Bv = coef[rows(order[0] + coef_shift), :], B[rows(order[0]), :]
        A_out[rows(order[0]), :] = A
        for r in order[1:]:
            a = coef[rows(r + coef_shift), :]
            Bv = a * Bv + B[rows(r), :]
            A = a * A
            A_out[rows(r), :] = A
            B[rows(r), :] = Bv


TILES_PER_STEP = 8


def _carry_tiles(A_s, B_s, out, ntile, direction):
    out_row = SUBLANES - 1 if direction == 1 else 0

    def step(k, carry):
        for j in range(TILES_PER_STEP):
            t = k * TILES_PER_STEP + j
            r0 = pl.multiple_of((t if direction == 1 else ntile - 1 - t) * SUBLANES, SUBLANES)
            A, B = A_s[pl.ds(r0, SUBLANES), :], B_s[pl.ds(r0, SUBLANES), :]
            out[pl.ds(r0, SUBLANES), :] = A * carry + B
            carry = _row_bcast(A, out_row) * carry + _row_bcast(B, out_row)
        return carry

    lax.fori_loop(0, ntile // TILES_PER_STEP, step, jnp.zeros((SUBLANES, HEAD), F32))


def _rnn_fwd(proj, conv_w, conv_b, w_a, b_a, w_x, b_x, lam, jobs=()):
    T = proj.shape[0]
    nchunk = T // CHUNK
    ntile = T // SUBLANES

    def body(u_ref, ug_ref, cw_ref, cb_ref, wa_ref, ba_ref, wx_ref, bx_ref, lam_ref,
             h_ref, z_ref, v_ref, ra_ref, ix_ref, a_ref, sq_ref, upad, a_s, b_s):
        _pad_front(upad, u_ref, SUBLANES)
        cw, cb = _tap_rows(cw_ref), cb_ref[...]
        wa, wx = wa_ref[...].astype(BF16), wx_ref[...].astype(BF16)
        ba, bx = ba_ref[...], bx_ref[...]
        sp = _softplus_neg(lam_ref[...])

        def chunk(i, carry):
            rows = pl.ds(pl.multiple_of(i * CHUNK, CHUNK), CHUNK)
            v = _conv_taps(upad[pl.ds(pl.multiple_of(i * CHUNK, CHUNK), CHUNK + SUBLANES), :], cw) + cb
            ra, ix, a, sq = _lru_gates(v, wa, ba, wx, bx, sp)
            v_ref[rows, :], ra_ref[rows, :], ix_ref[rows, :], a_ref[rows, :], sq_ref[rows, :] = v, ra, ix, a, sq
            a_s[rows, :], b_s[rows, :] = a, sq * ix * v
            return carry

        lax.fori_loop(0, nchunk, chunk, 0)
        _scan_in_tiles(a_s, 0, a_s, b_s, T, 1)
        _carry_tiles(a_s, b_s, h_ref, ntile, 1)

        def chunk3(i, carry):
            r0 = pl.multiple_of(i * CHUNK, CHUNK)
            gl, _ = _gelu_parts(ug_ref[pl.ds(r0, CHUNK), :])
            z_ref[pl.ds(r0, CHUNK), :] = (h_ref[pl.ds(r0, CHUNK), :] * gl).astype(BF16)
            return carry

        lax.fori_loop(0, nchunk, chunk3, 0)

    col = pl.BlockSpec((T, HEAD), lambda h: (0, h))
    vec = pl.BlockSpec((1, HEAD), lambda h: (0, h))
    mat = pl.BlockSpec((None, HEAD, HEAD), lambda h: (h, 0, 0))
    return _pallas(
        body, (proj, proj, conv_w, conv_b, w_a, b_a, w_x, b_x, lam), name="rnn_fwd", grid=(N_RNN_HEADS,),
        in_specs=[pl.BlockSpec((T, HEAD), lambda h: (0, COL_RNN + h)), pl.BlockSpec((T, HEAD), lambda h: (0, COL_GATE + h)),
                  pl.BlockSpec((CONV_WIDTH, HEAD), lambda h: (0, h)), vec, mat, vec, mat, vec, vec],
        out_specs=[col] * 7,
        out_shape=[jax.ShapeDtypeStruct((T, D_RNN), F32), jax.ShapeDtypeStruct((T, D_RNN), BF16)]
        + [jax.ShapeDtypeStruct((T, D_RNN), F32)] * 5,
        scratch_shapes=[pltpu.VMEM((T + SUBLANES, HEAD), F32), pltpu.VMEM((T, HEAD), F32), pltpu.VMEM((T, HEAD), F32)],
        semantics=("parallel",), jobs=jobs)


def _rnn_bwd(proj, hr, dz, gates, conv_w, w_a, w_x, lam, jobs=()):
    T = proj.shape[0]
    nchunk = T // CHUNK
    ntile = T // SUBLANES

    def body(u_ref, ug_ref, h_ref, dz_ref, v_ref, ra_ref, ix_ref, a_ref, sq_ref, cw_ref, wa_ref, wx_ref, lam_ref,
             du_ref, dug_ref, dwa_ref, dwx_ref, dba_ref, dbx_ref, dlam_ref, dcb_ref, dcw_ref,
             upad, hpad, apad, g_s, dvpad, ga_s):
        zero_tile = jnp.zeros((SUBLANES, HEAD), F32)
        _pad_front(upad, u_ref, SUBLANES)
        _pad_front(hpad, h_ref, SUBLANES)
        apad[pl.ds(T, SUBLANES), :] = zero_tile
        dvpad[pl.ds(T, SUBLANES), :] = zero_tile
        for ref in (dwa_ref, dwx_ref, dba_ref, dbx_ref, dlam_ref, dcb_ref, dcw_ref):
            ref[...] = jnp.zeros_like(ref)
        cw = _tap_rows(cw_ref)
        wa, wx = wa_ref[...].astype(BF16), wx_ref[...].astype(BF16)
        lam_row = lam_ref[...]
        sp = _softplus_neg(lam_row)

        def chunk(i, carry):
            rows = pl.ds(pl.multiple_of(i * CHUNK, CHUNK), CHUNK)
            apad[rows, :] = a_ref[rows, :]
            gl, dgl = _gelu_parts(ug_ref[rows, :])
            d = dz_ref[rows, :]
            g_s[rows, :] = d * gl
            dug_ref[rows, :] = (d * h_ref[rows, :] * dgl).astype(BF16)
            return carry

        lax.fori_loop(0, nchunk, chunk, 0)

        _scan_in_tiles(apad, 1, ga_s, g_s, T, -1)
        _carry_tiles(ga_s, g_s, g_s, ntile, -1)

        def chunk3(i, carry):
            r0 = pl.multiple_of(i * CHUNK, CHUNK)
            rows = pl.ds(r0, CHUNK)
            g = g_s[rows, :]
            h_prev = _shift_rows(hpad[pl.ds(r0, CHUNK + SUBLANES), :], 1)[SUBLANES:]
            v, ra, ix, sq, a = v_ref[rows, :], ra_ref[rows, :], ix_ref[rows, :], sq_ref[rows, :], a_ref[rows, :]
            d_sq = g * ix * v
            d_ix = g * sq * v
            d_la = a * g * h_prev - d_sq * a * a / sq
            dlam_ref[...] += jnp.sum(d_la * ra, axis=0, keepdims=True)
            d_pa = d_la * (-LRU_C) * sp * ra * (1.0 - ra)
            d_px = d_ix * ix * (1.0 - ix)
            vb, d_pab, d_pxb = v.astype(BF16), d_pa.astype(BF16), d_px.astype(BF16)
            dwa_ref[...] += _dot_tn(vb, d_pab)
            dwx_ref[...] += _dot_tn(vb, d_pxb)
            dba_ref[...] += jnp.sum(d_pa, axis=0, keepdims=True)
            dbx_ref[...] += jnp.sum(d_px, axis=0, keepdims=True)
            dv = g * sq * ix + _dot_nt(d_pab, wa) + _dot_nt(d_pxb, wx)
            dvpad[rows, :] = dv
            dcb_ref[...] += jnp.sum(dv, axis=0, keepdims=True)
            xs = upad[pl.ds(r0, CHUNK + SUBLANES), :]
            for k in range(CONV_WIDTH):
                u_k = _shift_rows(xs, CONV_WIDTH - 1 - k)[SUBLANES:] if k < CONV_WIDTH - 1 else xs[SUBLANES:]
                dcw_ref[k:k + 1, :] += jnp.sum(dv * u_k, axis=0, keepdims=True)
            return carry

        lax.fori_loop(0, nchunk, chunk3, 0)
        dlam_ref[...] = dlam_ref[...] * (LRU_C * _sigmoid(-lam_row))

        def chunk4(i, carry):
            r0 = pl.multiple_of(i * CHUNK, CHUNK)
            dvs = dvpad[pl.ds(r0, CHUNK + SUBLANES), :]
            du = cw[CONV_WIDTH - 1] * dvs[:CHUNK]
            for k in range(CONV_WIDTH - 1):
                du += cw[k] * _shift_rows(dvs, -(CONV_WIDTH - 1 - k))[:CHUNK]
            du_ref[pl.ds(r0, CHUNK), :] = du.astype(BF16)
            return carry

        lax.fori_loop(0, nchunk, chunk4, 0)

    col = pl.BlockSpec((T, HEAD), lambda h: (0, h))
    vec = pl.BlockSpec((1, HEAD), lambda h: (0, h))
    mat = pl.BlockSpec((None, HEAD, HEAD), lambda h: (h, 0, 0))
    taps = pl.BlockSpec((CONV_WIDTH, HEAD), lambda h: (0, h))
    vec_out = jax.ShapeDtypeStruct((1, D_RNN), F32)
    mat_out = jax.ShapeDtypeStruct((N_RNN_HEADS, HEAD, HEAD), F32)
    seq = pltpu.VMEM((T, HEAD), F32)
    seq_pad = pltpu.VMEM((T + SUBLANES, HEAD), F32)
    return _pallas(
        body, (proj, proj, hr, dz, *gates, conv_w, w_a, w_x, lam), name="rnn_bwd", grid=(N_RNN_HEADS,),
        in_specs=[pl.BlockSpec((T, HEAD), lambda h: (0, COL_RNN + h)), pl.BlockSpec((T, HEAD), lambda h: (0, COL_GATE + h))]
        + [col] * 7 + [taps, mat, mat, vec],
        out_specs=[col, col, mat, mat, vec, vec, vec, vec, taps],
        out_shape=[jax.ShapeDtypeStruct((T, D_RNN), BF16), jax.ShapeDtypeStruct((T, D_RNN), BF16), mat_out, mat_out,
                   vec_out, vec_out, vec_out, vec_out, jax.ShapeDtypeStruct((CONV_WIDTH, D_RNN), F32)],
        scratch_shapes=[seq_pad, seq_pad, seq_pad, seq, seq_pad, seq],
        semantics=("parallel",), jobs=jobs)


GROUP_FFN_OUT = ["w_ffn_out"]
GROUP_FFN_IN = ["w_ffn_in"]
GROUP_MIX = ["w_o", "w_pool_out", "w_rnn_out"]
GROUP_IN = ["w_in"]


def _step(x, target, s, full, conv_w, place):
    T = x.shape[0]
    tall, mid, low = min(T, 2048), min(T, 1024), min(T, 512)
    full = dict(full)

    def gathered(names, results):
        full.update(zip(names, results))

    early = ["w_pool_out", "w_rnn_out", "w_o", "w_ffn_out"]
    (proj, h1), (res,) = _norm_matmul(x, s["norm_mix"], full["w_in"], tm=tall, tn=512, name="in_proj", jobs=[_gather_job(full, early)])
    gathered(early, res)
    pm = _pool_fwd(proj, s["w_pool_grp"], s["pool_scale"])
    (hr, z, *gates), (res,) = _rnn_fwd(proj, conv_w, s["conv_b"], s["w_rg_a"], s["b_rg_a"], s["w_rg_x"], s["b_rg_x"],
                                       s["lru_lambda"], jobs=[_gather_job(full, ["w_ffn_in"])])
    gathered(["w_ffn_in"], res)
    *mix_by, mix = _branch_mix(pm, z, full["w_pool_out"], full["w_rnn_out"], proj, tm=tall, tn=256)
    x2 = _out_proj_residual(mix, full["w_o"], x, tm=mid)
    (act_by_up, act_by_gate, act, h2), _ = _ffn_in(x2, s["norm_ffn"], full["w_ffn_in"], tm=tall, tn=256)
    dx3, dx3b, sq_cols, g_norm_final = _ffn_out_loss(act, full["w_ffn_out"], x2, s["norm_final"], target, tm=low)

    g = {"norm_final": g_norm_final}

    def chip_sums(names, from_sibling):
        sums = {name: _chip_sum(name, g[name], got, place) for name, got in zip(names, from_sibling)}
        return {name: v[0] for name, v in sums.items()}, {name: v[1] for name, v in sums.items()}

    def final_sums(names, sums, from_chips):
        return {name: _final_sum(name, sums[name], got, place) for name, got in zip(names, from_chips)}

    dgate, dup = _ffn_out_bwd(dx3b, full["w_ffn_out"], act_by_gate, act_by_up, tm=tall, tn=256)
    gb = {}
    g["w_ffn_out"], gb["w_ffn_out"] = _weight_grad(act, [dx3b], tm=256, tn=D_MODEL, name="w_ffn_out_grad", also_bf16=True)
    (dx2, dx2b, g["norm_ffn"]), (res,) = _ffn_in_bwd(dgate, dup, full["w_ffn_in"], dx3, x2, s["norm_ffn"], tm=low,
                                                     jobs=[_sibling_job(gb, GROUP_FFN_OUT, BF16)])
    sums_ffn, sums_ffn_bf16 = chip_sums(GROUP_FFN_OUT, res)
    (g["w_ffn_in"], gb["w_ffn_in"]), (res,) = _weight_grad(h2, [dgate, dup], tm=D_MODEL, tn=256, name="w_ffn_in_grad", also_bf16=True,
                                                           jobs=[_chips_job(sums_ffn_bf16, GROUP_FFN_OUT)])
    shards_ffn = final_sums(GROUP_FFN_OUT, sums_ffn, res)
    (dgp, dgr, dyp, dyr), (res,) = _out_proj_bwd(dx2b, full["w_o"], mix_by, tm=tall, tn=256,
                                                 jobs=[_sibling_job(gb, GROUP_FFN_IN, BF16)])
    sums_ffn, sums_ffn_bf16 = chip_sums(GROUP_FFN_IN, res)
    g["w_o"], gb["w_o"] = _weight_grad(mix, [dx2b], tm=D_MODEL, tn=256, name="w_o_grad", also_bf16=True)
    dpm, dz = _branch_bwd(dyp, dyr, full["w_pool_out"], full["w_rnn_out"], tm=mid)
    g["w_pool_out"], gb["w_pool_out"] = _weight_grad(pm, [dyp], tm=D_POOL, tn=256, name="w_pool_out_grad", also_bf16=True)
    g["w_rnn_out"], gb["w_rnn_out"] = _weight_grad(z, [dyr], tm=D_RNN, tn=256, name="w_rnn_out_grad", also_bf16=True)
    (dupool, g["w_pool_grp"], g["pool_scale"]), (res,) = _pool_bwd(proj, dpm, s["w_pool_grp"], s["pool_scale"],
                                                                   jobs=[_sibling_job(gb, GROUP_MIX, BF16)])
    sums_mix, sums_mix_bf16 = chip_sums(GROUP_MIX, res)
    ((durnn, dugate, g["w_rg_a"], g["w_rg_x"], g["b_rg_a"], g["b_rg_x"], g["lru_lambda"], g["conv_b"], g["conv_w"]),
     (res,)) = _rnn_bwd(proj, hr, dz, gates, conv_w, s["w_rg_a"], s["w_rg_x"], s["lru_lambda"],
                        jobs=[_chips_job(sums_ffn_bf16, GROUP_FFN_IN)])
    shards_ffn.update(final_sums(GROUP_FFN_IN, sums_ffn, res))
    segs = [dupool, durnn, dugate, dgp, dgr]
    ffn = GROUP_FFN_OUT + GROUP_FFN_IN
    (g["w_in"], gb["w_in"]), (res, joined) = _weight_grad(
        h1, segs, tm=D_MODEL, tn=256, name="w_in_grad", also_bf16=True,
        jobs=[_chips_job(sums_mix_bf16, GROUP_MIX), _join_job(shards_ffn, ffn)])
    grads = dict(zip(ffn, joined))
    shards = final_sums(GROUP_MIX, sums_mix, res)
    (res,) = _run_jobs([_sibling_job(gb, GROUP_IN, BF16)], "w_in_exchange_sibling")
    sums_in, sums_in_bf16 = chip_sums(GROUP_IN, res)
    (grad_x, g["norm_mix"]), (res,) = _in_proj_bwd(segs, full["w_in"], dx2, x, s["norm_mix"], tm=low,
                                                  jobs=[_chips_job(sums_in_bf16, GROUP_IN)])
    shards.update(final_sums(GROUP_IN, sums_in, res))

    vec_rows = [g[name] if name != "pool_scale" else jnp.pad(g[name], ((0, 0), (0, D_MODEL - D_POOL))) for name in VEC_ITEMS]
    vec_rows += [g["conv_w"], sq_cols, jnp.zeros((VEC_ROWS - len(VEC_ITEMS) - CONV_WIDTH - 1, D_MODEL), F32)]
    vec = jnp.concatenate(vec_rows, axis=0).reshape(VEC_ROWS, N_DEV, HEAD).transpose(1, 0, 2)
    mat = jnp.concatenate([g[name].reshape(-1, HEAD) for name in MAT_ITEMS], axis=0).reshape(N_DEV, -1, HEAD)
    (vec, mat), (joined,) = _all_reduce_small([vec, mat], [False, True], jobs=[_join_job(shards, GROUP_MIX + GROUP_IN)])
    grads.update(zip(GROUP_MIX + GROUP_IN, joined))

    vec = vec.transpose(1, 0, 2).reshape(VEC_ROWS, D_MODEL)
    mat = mat.reshape(-1, HEAD)
    for k, name in enumerate(VEC_ITEMS):
        grads[name] = vec[k:k + 1, :s[name].shape[1]]
    grads["conv_w"] = vec[len(VEC_ITEMS):len(VEC_ITEMS) + CONV_WIDTH]
    row = 0
    for name in MAT_ITEMS:
        rows = s[name].shape[0] * HEAD
        grads[name] = mat[row:row + rows]
        row += rows
    return vec[len(VEC_ITEMS) + CONV_WIDTH], grad_x, grads


LARGE = {"w_in": "col", "w_pool_out": "col", "w_rnn_out": "row", "w_o": "row", "w_ffn_in": "col", "w_ffn_out": "row"}
LARGE_SHAPE = {"w_in": (D_MODEL, D_IN), "w_pool_out": (D_POOL, D_MODEL), "w_rnn_out": (D_RNN, D_MODEL),
               "w_o": (D_MODEL, D_MODEL), "w_ffn_in": (D_MODEL, 2 * D_FF), "w_ffn_out": (D_FF, D_MODEL)}


def _place():
    x, y, c = lax.axis_index("x"), lax.axis_index("y"), lax.axis_index("c")
    return 2 * x + y, c


def _chip_device(chip, c):
    return (chip // 2, chip % 2, c)


def _chip_window(ref, kind, shape, chip, half=None):
    K, N = shape
    if kind == "col":
        rows = slice(None) if half is None else pl.ds(half * (K // 2), K // 2)
        return ref.at[rows, pl.ds(chip * (N // N_CHIPS), N // N_CHIPS)]
    ks = K // N_CHIPS
    if half is None:
        return ref.at[pl.ds(chip * ks, ks), :]
    return ref.at[pl.ds(chip * ks + half * (ks // 2), ks // 2), :]


def _row_half(ref, half):
    rows = ref.shape[0] // 2
    return ref.at[pl.ds(half * rows, rows), :]


def _remote(win_src, win_dst, send_sems, recv_sems, idx, to):
    return pltpu.make_async_remote_copy(src_ref=win_src, dst_ref=win_dst, send_sem=send_sems.at[idx], recv_sem=recv_sems.at[idx],
                                        device_id=to, device_id_type=MESH)


def _gather_job(full, names, conv_w_full=None):
    n = len(names)
    cw_cols = D_RNN // N_CHIPS

    def windows(refs, chip, half):
        return [_chip_window(refs[k], LARGE[name], LARGE_SHAPE[name], chip, half) for k, name in enumerate(names)]

    def ici_copies(refs, send_sems, recv_sems, src_chip, dst_chip, c, r):
        wins = windows(refs, src_chip, c)
        if conv_w_full is not None:
            wins.append(refs[n].at[:, pl.ds(src_chip * cw_cols, cw_cols)])
        return [_remote(win, win, send_sems, recv_sems, (k, r), _chip_device(dst_chip, c)) for k, win in enumerate(wins)]

    def forwards(refs, send_sems, recv_sems, src_chip, half, to_core, chip, r):
        return [_remote(win, win, send_sems, recv_sems, (k, 3 + r), _chip_device(chip, to_core))
                for k, win in enumerate(windows(refs, src_chip, half))]

    def start(ins, outs, send_sems, recv_sems):
        chip, c = _place()
        for r in range(3):
            for cp in ici_copies(outs, send_sems, recv_sems, chip, chip ^ (r + 1), c, r):
                cp.start()

    def finish(ins, outs, send_sems, recv_sems):
        chip, c = _place()
        for r in range(3):
            for cp in ici_copies(outs, send_sems, recv_sems, chip ^ (r + 1), chip, c, r):
                cp.wait_recv()
            for cp in forwards(outs, send_sems, recv_sems, chip ^ (r + 1), c, 1 - c, chip, r):
                cp.start()
        for r in range(3):
            for cp in forwards(outs, send_sems, recv_sems, chip ^ (r + 1), 1 - c, c, chip, r):
                cp.wait_recv()
            for cp in ici_copies(outs, send_sems, recv_sems, chip, chip ^ (r + 1), c, r):
                cp.wait_send()
            for cp in forwards(outs, send_sems, recv_sems, chip ^ (r + 1), c, 1 - c, chip, r):
                cp.wait_send()

    arrays = [full[name] for name in names] + ([conv_w_full] if conv_w_full is not None else [])
    return _Job(arrays, [jax.ShapeDtypeStruct(a.shape, a.dtype) for a in arrays], {k: k for k in range(len(arrays))},
                (len(arrays), 6), start, finish)


def _core_halves(ref, kind, shape, c):
    return [_chip_window(ref, kind, shape, chip, c) for chip in range(N_CHIPS)]


def _sibling_job(grads, names, dtype=F32):
    def start(ins, outs, send_sems, recv_sems):
        chip, c = _place()
        for k, name in enumerate(names):
            kind, shape = LARGE[name], LARGE_SHAPE[name]
            if kind == "col":
                pairs = [(_row_half(ins[k], 1 - c), outs[k])]
            else:
                rows = shape[0] // N_DEV
                pairs = [(win, outs[k].at[pl.ds(j * rows, rows), :]) for j, win in enumerate(_core_halves(ins[k], kind, shape, 1 - c))]
            for src, dst in pairs:
                _remote(src, dst, send_sems, recv_sems, k, _chip_device(chip, 1 - c)).start()

    def finish(ins, outs, send_sems, recv_sems):
        chip, c = _place()
        for k in range(len(names)):
            _remote(outs[k], outs[k], send_sems, recv_sems, k, _chip_device(chip, 1 - c)).wait()

    return _Job([grads[name] for name in names],
                [jax.ShapeDtypeStruct((LARGE_SHAPE[name][0] // 2, LARGE_SHAPE[name][1]), dtype) for name in names], {},
                (len(names),), start, finish)


def _chip_sum(name, g, got, place):
    kind, (K, N) = LARGE[name], LARGE_SHAPE[name]
    rows = K // N_DEV
    piece_cols = N // N_CHIPS

    def body(place_ref, g_ref, got_ref, o_ref, ob_ref):
        total = g_ref[...] + got_ref[...].astype(F32)
        ob_ref[...] = total.astype(BF16)
        if kind == "col":
            for chip in range(N_CHIPS):
                @pl.when(place_ref[0] == chip)
                def _(chip=chip):
                    o_ref[...] = total[:, chip * piece_cols:(chip + 1) * piece_cols]
        else:
            @pl.when(pl.program_id(0) == place_ref[0])
            def _():
                o_ref[...] = total

    if kind == "col":
        mine = pl.BlockSpec((rows, N), lambda j, place_ref: (j + N_CHIPS * place_ref[1], 0))
        own = pl.BlockSpec((rows, piece_cols), lambda j, place_ref: (j, 0))
    else:
        mine = pl.BlockSpec((rows, N), lambda j, place_ref: (2 * j + place_ref[1], 0))
        own = pl.BlockSpec((rows, N), lambda j, place_ref: (0, 0))
    blk = pl.BlockSpec((rows, N), lambda j, place_ref: (j, 0))
    return pl.pallas_call(
        body, name=name + "_chip_sum",
        grid_spec=pltpu.PrefetchScalarGridSpec(num_scalar_prefetch=1, grid=(N_CHIPS,), in_specs=[mine, blk], out_specs=[own, blk]),
        out_shape=[jax.ShapeDtypeStruct(_piece_shape(name), F32), jax.ShapeDtypeStruct((K // 2, N), BF16)],
        compiler_params=_params(dimension_semantics=("arbitrary",)),
    )(place, g, got)


def _piece(ref, kind, shape, chip):
    K, N = shape
    if kind == "col":
        return ref.at[:, pl.ds(chip * (N // N_CHIPS), N // N_CHIPS)]
    return ref.at[pl.ds(chip * (K // N_DEV), K // N_DEV), :]


def _piece_shape(name):
    kind, (K, N) = LARGE[name], LARGE_SHAPE[name]
    return (K // 2, N // N_CHIPS) if kind == "col" else (K // N_DEV, N)


def _chips_job(sums, names):
    def copies(ins, outs, send_sems, recv_sems):
        chip, c = _place()
        return [_remote(_piece(ins[k], LARGE[name], LARGE_SHAPE[name], chip ^ (r + 1)), outs[k].at[r], send_sems, recv_sems, (k, r),
                        _chip_device(chip ^ (r + 1), c)) for k, name in enumerate(names) for r in range(3)]

    def start(*refs):
        for cp in copies(*refs):
            cp.start()

    def finish(*refs):
        for cp in copies(*refs):
            cp.wait()

    return _Job([sums[name] for name in names], [jax.ShapeDtypeStruct((3,) + _piece_shape(name), BF16) for name in names], {},
                (len(names), 3), start, finish)


def _final_sum(name, chip_sum, got, place):
    rows, cols = _piece_shape(name)

    def body(place_ref, s_ref, got_ref, o_ref):
        o_ref[...] = ((s_ref[...] + got_ref[0].astype(F32)) + got_ref[1].astype(F32)) + got_ref[2].astype(F32)

    mine = pl.BlockSpec((rows, cols), lambda i, place_ref: (0, 0))
    return pl.pallas_call(
        body, name=name + "_final_sum",
        grid_spec=pltpu.PrefetchScalarGridSpec(
            num_scalar_prefetch=1, grid=(1,), in_specs=[mine, pl.BlockSpec((3, rows, cols), lambda i, place_ref: (0, 0, 0))],
            out_specs=pl.BlockSpec((rows, cols), lambda i, place_ref: (place_ref[1], 0))),
        out_shape=jax.ShapeDtypeStruct((2 * rows, cols), F32),
        compiler_params=_params(dimension_semantics=("arbitrary",)),
    )(place, chip_sum, got)


def _join_job(shards, names):
    def half_copy(outs, send_sems, recv_sems, k, mine):
        chip, c = _place()
        win = _row_half(outs[k], c if mine else 1 - c)
        return _remote(win, win, send_sems, recv_sems, k, _chip_device(chip, 1 - c))

    def start(ins, outs, send_sems, recv_sems):
        for k in range(len(names)):
            half_copy(outs, send_sems, recv_sems, k, True).start()

    def finish(ins, outs, send_sems, recv_sems):
        for k in range(len(names)):
            half_copy(outs, send_sems, recv_sems, k, True).wait_send()
            half_copy(outs, send_sems, recv_sems, k, False).wait_recv()

    arrays = [shards[name] for name in names]
    return _Job(arrays, [jax.ShapeDtypeStruct(a.shape, F32) for a in arrays], {k: k for k in range(len(arrays))},
                (len(arrays),), start, finish)


VEC_ROWS = 16


def _all_reduce_small(slabs, narrow, jobs=()):
    n = len(slabs)
    narrowed = [k for k in range(n) if narrow[k]]

    def body(*refs):
        in_refs, out_refs, got_refs = refs[:n], refs[n:2 * n], refs[2 * n:3 * n]
        bf16_refs = dict(zip(narrowed, refs[3 * n:3 * n + len(narrowed)]))
        send_sems, recv_sems = refs[3 * n + len(narrowed):]
        x, y, c = lax.axis_index("x"), lax.axis_index("y"), lax.axis_index("c")
        me = 4 * x + 2 * y + c
        for k, ref in bf16_refs.items():
            ref[...] = in_refs[k][...].astype(BF16)
        partial_refs = [bf16_refs.get(k, in_refs[k]) for k in range(n)]

        def remote(src, dst, k, phase, r):
            other = me ^ r
            return pltpu.make_async_remote_copy(src_ref=src, dst_ref=dst, send_sem=send_sems.at[k, phase, r],
                                                recv_sem=recv_sems.at[k, phase, r],
                                                device_id=(other // 4, (other // 2) % 2, other % 2), device_id_type=MESH)

        scatter = [remote(partial_refs[k].at[me ^ r], got_refs[k].at[r], k, 0, r) for r in range(1, N_DEV) for k in range(n)]
        for cp in scatter:
            cp.start()
        for cp in scatter:
            cp.wait()
        for k in range(n):
            total = in_refs[k][me]
            for r in range(1, N_DEV):
                total = total + got_refs[k][r].astype(F32)
            out_refs[k][me] = total
        gather = [remote(out_refs[k].at[me], out_refs[k].at[me], k, 1, r) for r in range(1, N_DEV) for k in range(n)]
        for cp in gather:
            cp.start()
        for r in range(1, N_DEV):
            for k in range(n):
                remote(out_refs[k].at[me ^ r], out_refs[k].at[me ^ r], k, 1, r).wait_recv()
        for cp in gather:
            cp.wait_send()

    return _pallas(
        body, slabs, name="all_reduce_small", grid=(), in_specs=[VMEM] * n, out_specs=[VMEM] * n,
        out_shape=[jax.ShapeDtypeStruct(s.shape, F32) for s in slabs],
        scratch_shapes=[pltpu.VMEM(s.shape, BF16 if narrow[k] else F32) for k, s in enumerate(slabs)]
        + [pltpu.VMEM(slabs[k].shape, BF16) for k in narrowed]
        + [pltpu.SemaphoreType.DMA((n, 2, N_DEV)), pltpu.SemaphoreType.DMA((n, 2, N_DEV))], jobs=jobs)


def _cast_into_whole(w, name, place):
    rows, cols = w.shape
    tr = rows // 2

    def body(place_ref, w_ref, o_ref):
        o_ref[...] = w_ref[...].astype(BF16)

    if LARGE[name] == "col":
        window = pl.BlockSpec((tr, cols), lambda i, place_ref: (i, place_ref[0]))
    else:
        window = pl.BlockSpec((tr, cols), lambda i, place_ref: (2 * place_ref[0] + i, 0))
    return pl.pallas_call(
        body, name=name + "_cast",
        grid_spec=pltpu.PrefetchScalarGridSpec(num_scalar_prefetch=1, grid=(2,),
                                               in_specs=[pl.BlockSpec((tr, cols), lambda i, place_ref: (i, 0))], out_specs=window),
        out_shape=jax.ShapeDtypeStruct(LARGE_SHAPE[name], BF16),
        compiler_params=_params(dimension_semantics=("parallel",)))(place, w)


def _cast_many_into_whole(shards, place, jobs):
    names = list(shards)
    n = len(names)

    def body(place_ref, *refs):
        for w_ref, o_ref in zip(refs[:n], refs[n:]):
            o_ref[...] = w_ref[...].astype(BF16)

    def window(name):
        rows, cols = shards[name].shape
        if LARGE[name] == "col":
            return pl.BlockSpec((rows // 2, cols), lambda i, place_ref: (i, place_ref[0]))
        return pl.BlockSpec((rows // 2, cols), lambda i, place_ref: (2 * place_ref[0] + i, 0))

    def half(name):
        rows, cols = shards[name].shape
        return pl.BlockSpec((rows // 2, cols), lambda i, place_ref: (i, 0))

    return _pallas(body, [shards[name] for name in names], name="cast_weights", grid=(2,),
                   in_specs=[half(name) for name in names], out_specs=[window(name) for name in names],
                   out_shape=[jax.ShapeDtypeStruct(LARGE_SHAPE[name], BF16) for name in names],
                   semantics=("arbitrary",), jobs=jobs, prefetch=place)


def _adamw_math(w, g, m, v):
    m = ADAM_B1 * m + (1.0 - ADAM_B1) * g
    v = ADAM_B2 * v + (1.0 - ADAM_B2) * (g * g)
    m_hat = m / (1.0 - ADAM_B1 ** ADAM_STEP)
    v_hat = v / (1.0 - ADAM_B2 ** ADAM_STEP)
    delta = -ADAM_LR * (m_hat / (jnp.sqrt(v_hat) + ADAM_EPS) + ADAM_WD * w)
    return delta, m, v


def _adamw_large(w, g, m, v, name):
    rows, cols = w.shape
    steps = 2
    tr = rows // steps

    def body(w_ref, g_ref, m_ref, v_ref, d_ref, mo_ref, vo_ref):
        d_ref[...], mo_ref[...], vo_ref[...] = _adamw_math(w_ref[...], g_ref[...], m_ref[...], v_ref[...])

    blk = pl.BlockSpec((tr, cols), lambda i: (i, 0))
    out = jax.ShapeDtypeStruct(w.shape, F32)
    return pl.pallas_call(body, name=name + "_adamw", grid=(steps,), in_specs=[blk] * 4, out_specs=[blk] * 3, out_shape=[out] * 3,
                          compiler_params=_params(dimension_semantics=("parallel",)))(w, g, m, v)


def _adamw_small(ws, gs, ms, vs):
    n = len(ws)

    def body(*refs):
        for k in range(n):
            w_ref, g_ref, m_ref, v_ref = (refs[q * n + k] for q in range(4))
            d_ref, mo_ref, vo_ref = (refs[(4 + q) * n + k] for q in range(3))
            d_ref[...], mo_ref[...], vo_ref[...] = _adamw_math(w_ref[...], g_ref[...], m_ref[...], v_ref[...])

    out = [jax.ShapeDtypeStruct(w.shape, F32) for w in ws]
    res = pl.pallas_call(body, name="small_adamw", in_specs=[VMEM] * (4 * n), out_specs=[VMEM] * (3 * n), out_shape=out * 3,
                         compiler_params=_params())(*ws, *gs, *ms, *vs)
    return res[:n], res[n:2 * n], res[2 * n:]


WEIGHTS = ["norm_mix", "w_in", "w_pool_grp", "pool_scale", "w_pool_out", "conv_w", "conv_b", "w_rg_a", "b_rg_a", "w_rg_x",
           "b_rg_x", "lru_lambda", "w_rnn_out", "w_o", "norm_ffn", "w_ffn_in", "w_ffn_out", "norm_final"]
VEC_ITEMS = ["norm_mix", "norm_ffn", "norm_final", "pool_scale", "conv_b", "lru_lambda", "b_rg_a", "b_rg_x"]
MAT_ITEMS = ["w_pool_grp", "w_rg_a", "w_rg_x"]


def _as2d(name, a):
    if name in MAT_ITEMS:
        return a.reshape(-1, HEAD, HEAD)
    if name == "conv_w":
        return a.reshape(CONV_WIDTH, -1)
    return a.reshape(1, -1)


def kernel(x, norm_mix, w_in, w_pool_grp, pool_scale, w_pool_out, conv_w, conv_b, w_rg_a, b_rg_a, w_rg_x, b_rg_x, lru_lambda, w_rnn_out, w_o, norm_ffn, w_ffn_in, w_ffn_out, norm_final, loss_target, m_norm_mix, m_w_in, m_w_pool_grp, m_pool_scale, m_w_pool_out, m_conv_w, m_conv_b, m_w_rg_a, m_b_rg_a, m_w_rg_x, m_b_rg_x, m_lru_lambda, m_w_rnn_out, m_w_o, m_norm_ffn, m_w_ffn_in, m_w_ffn_out, m_norm_final, v_norm_mix, v_w_in, v_w_pool_grp, v_pool_scale, v_w_pool_out, v_conv_w, v_conv_b, v_w_rg_a, v_b_rg_a, v_w_rg_x, v_b_rg_x, v_lru_lambda, v_w_rnn_out, v_w_o, v_norm_ffn, v_w_ffn_in, v_w_ffn_out, v_norm_final):
    given = dict(locals())
    w = {name: given[name] for name in WEIGHTS}
    m = {name: given["m_" + name] for name in WEIGHTS}
    v = {name: given["v_" + name] for name in WEIGHTS}
    chip, c = _place()

    place = jnp.stack([chip, c]).astype(jnp.int32)
    conv_cols = w["conv_w"].shape[-1]
    conv_w_mine = lax.dynamic_update_slice_in_dim(jnp.zeros((CONV_WIDTH, D_RNN), F32), w["conv_w"][0], chip * conv_cols, axis=1)
    w_in_mine = _cast_into_whole(w["w_in"][0], "w_in", place)
    later = [name for name in LARGE if name != "w_in"]
    casts, ((w_in_full, conv_w_full),) = _cast_many_into_whole(
        {name: w[name][0] for name in later}, place, jobs=[_gather_job({"w_in": w_in_mine}, ["w_in"], conv_w_mine)])
    full = dict(zip(later, casts), w_in=w_in_full)
    small = {name: _as2d(name, w[name]) for name in WEIGHTS if name not in LARGE and name != "conv_w"}
    sq_cols, grad_x, grads = _step(x[0], loss_target[0], small, full, conv_w_full, place)
    loss = 0.5 / D_MODEL * jnp.sum(sq_cols)
    grads["conv_w"] = lax.dynamic_slice_in_dim(grads["conv_w"], chip * conv_cols, conv_cols, axis=1)

    delta, new_m, new_v = {}, {}, {}
    for name in LARGE:
        delta[name], new_m[name], new_v[name] = _adamw_large(w[name][0], grads[name], m[name][0], v[name][0], name)
    small_names = [name for name in WEIGHTS if name not in LARGE]
    flat = lambda d: [d[name].reshape(grads[name].shape) for name in small_names]
    ds, mo, vo = _adamw_small(flat(w), [grads[name] for name in small_names], flat(m), flat(v))
    for k, name in enumerate(small_names):
        delta[name], new_m[name], new_v[name] = ds[k], mo[k], vo[k]

    shaped = lambda d: [d[name].reshape(w[name].shape) for name in WEIGHTS]
    return (loss, grad_x[None], *shaped(grads), *shaped(delta), *shaped(new_m), *shaped(new_v))
```

```python
import functools
import math

import jax
import jax.numpy as jnp
from jax import lax
from jax.experimental import pallas as pl
from jax.experimental.pallas import tpu as pltpu

F32 = jnp.float32
BF16 = jnp.bfloat16

D_MODEL = 1024
D_POOL = 512
N_POOL_GROUPS = 4
D_RNN = 1024
N_RNN_HEADS = 8
HEAD = 128
CONV_WIDTH = 4
LRU_C = 8.0
D_FF = 2816
D_IN = D_POOL + 2 * D_RNN + 2 * D_MODEL
NORM_EPS = 1e-6
COL_RNN = D_POOL // HEAD
COL_GATE = (D_POOL + D_RNN) // HEAD

ADAM_LR = 0.001
ADAM_B1 = 0.9
ADAM_B2 = 0.999
ADAM_EPS = 1e-08
ADAM_WD = 0.01
ADAM_STEP = 10

N_CHIPS = 4
N_DEV = 8
MESH = pl.DeviceIdType.MESH
ANY = pl.BlockSpec(memory_space=pl.ANY)
VMEM = pl.BlockSpec(memory_space=pltpu.VMEM)
VMEM_LIMIT_BYTES = 60 * 1024 * 1024
SUBLANES = 8
POOL_HALO = 16
CHUNK = 1024

GELU_C = math.sqrt(2.0 / math.pi)
GELU_A = 0.044715


def _params(**kw):
    return pltpu.CompilerParams(vmem_limit_bytes=VMEM_LIMIT_BYTES, **kw)


def _sigmoid(x):
    return 0.5 * jnp.tanh(0.5 * x) + 0.5


def _log1p(y):
    u = 1.0 + y
    d = u - 1.0
    return jnp.where(d == 0.0, y, jnp.log(u) * (y / jnp.where(d == 0.0, 1.0, d)))


def _gelu_parts(x):
    x2 = x * x
    th = jnp.tanh(GELU_C * (x + GELU_A * x * x2))
    g = 0.5 * x * (1.0 + th)
    dg = 0.5 * (1.0 + th) + 0.5 * x * (1.0 - th * th) * GELU_C * (1.0 + 3.0 * GELU_A * x2)
    return g, dg


def _dot(a, b):
    return jnp.dot(a, b, preferred_element_type=F32)


def _dot_nt(a, b):
    return lax.dot_general(a, b, (((1,), (1,)), ((), ())), preferred_element_type=F32)


def _dot_tn(a, b):
    return lax.dot_general(a, b, (((0,), (0,)), ((), ())), preferred_element_type=F32)


def _rms_scale(xv):
    return lax.rsqrt(jnp.mean(xv * xv, axis=-1, keepdims=True) + NORM_EPS)


def _rms_bwd(dy, xv, g):
    r = _rms_scale(xv)
    xh = xv * r
    dyg = dy * g
    dx = r * (dyg - xh * jnp.mean(dyg * xh, axis=-1, keepdims=True))
    return dx, dy * xh


class _Job:
    def __init__(self, inputs, out_shapes, aliases, sem_shape, start, finish):
        self.inputs, self.out_shapes, self.aliases, self.sem_shape = list(inputs), list(out_shapes), dict(aliases), sem_shape
        self.start, self.finish = start, finish


def _pallas(body, operands, *, name, grid, in_specs, out_specs, out_shape, scratch_shapes=(), semantics=None, jobs=(),
            prefetch=None):
    n_in, n_out, n_scr = len(in_specs), len(out_specs), len(scratch_shapes)
    n_pre = 0 if prefetch is None else 1
    job_in = [a for job in jobs for a in job.inputs]
    job_out = [s for job in jobs for s in job.out_shapes]
    aliases, i0, o0 = {}, n_pre + n_in, n_out
    for job in jobs:
        aliases.update({i0 + i: o0 + o for i, o in job.aliases.items()})
        i0, o0 = i0 + len(job.inputs), o0 + len(job.out_shapes)

    def whole(*refs):
        pre, refs = refs[:n_pre], refs[n_pre:]
        ins, j_ins = refs[:n_in], refs[n_in:n_in + len(job_in)]
        outs = refs[n_in + len(job_in):][:n_out]
        j_outs = refs[n_in + len(job_in) + n_out:][:len(job_out)]
        rest = refs[n_in + len(job_in) + n_out + len(job_out):]
        scr, sems = rest[:n_scr], rest[n_scr:]

        def run(phase):
            i, o = 0, 0
            for k, job in enumerate(jobs):
                getattr(job, phase)(j_ins[i:i + len(job.inputs)], j_outs[o:o + len(job.out_shapes)], sems[2 * k], sems[2 * k + 1])
                i, o = i + len(job.inputs), o + len(job.out_shapes)

        def at(step_of, phase):
            if not jobs:
                return
            if not grid:
                run(phase)
                return
            cond = functools.reduce(jnp.logical_and, [pl.program_id(d) == step_of(d) for d in range(len(grid))])
            pl.when(cond)(functools.partial(run, phase))

        at(lambda d: 0, "start")
        body(*pre, *ins, *outs, *scr)
        at(lambda d: grid[d] - 1, "finish")

    layout = dict(grid=grid, in_specs=list(in_specs) + [ANY] * len(job_in), out_specs=list(out_specs) + [ANY] * len(job_out),
                  scratch_shapes=list(scratch_shapes) + [pltpu.SemaphoreType.DMA(job.sem_shape) for job in jobs for _ in range(2)])
    if prefetch is not None:
        layout = dict(grid_spec=pltpu.PrefetchScalarGridSpec(num_scalar_prefetch=1, **layout))
    res = pl.pallas_call(
        whole, name=name, out_shape=list(out_shape) + job_out, input_output_aliases=aliases,
        compiler_params=_params(dimension_semantics=semantics, has_side_effects=bool(jobs)), **layout,
    )(*([] if prefetch is None else [prefetch]), *operands, *job_in)
    per_job, o = [], n_out
    for job in jobs:
        per_job.append(res[o:o + len(job.out_shapes)])
        o += len(job.out_shapes)
    return res[:n_out], per_job


def _run_jobs(jobs, name):
    return _pallas(lambda: None, [], name=name, grid=(), in_specs=[], out_specs=[], out_shape=[], jobs=jobs)[1]


NORM_ROWS = 256
EPILOGUE_ROWS = 512


def _norm_rows(x_ref, g_ref, h_ref):
    g = g_ref[...]

    def rows(i, carry):
        r = pl.ds(pl.multiple_of(i * NORM_ROWS, NORM_ROWS), NORM_ROWS)
        xv = x_ref[r, :]
        h_ref[r, :] = (xv * _rms_scale(xv) * g).astype(BF16)
        return carry

    lax.fori_loop(0, x_ref.shape[0] // NORM_ROWS, rows, 0)


def _norm_matmul(x, g, w, *, tm, tn, name, jobs=()):
    T, K = x.shape
    N = w.shape[1]

    def body(x_ref, g_ref, w_ref, o_ref, h_ref):
        @pl.when(pl.program_id(1) == 0)
        def _():
            _norm_rows(x_ref, g_ref, h_ref)

        o_ref[...] = _dot(h_ref[...], w_ref[...])

    return _pallas(
        body, (x, g, w), name=name, grid=(T // tm, N // tn),
        in_specs=[pl.BlockSpec((tm, K), lambda i, j: (i, 0)), pl.BlockSpec((1, K), lambda i, j: (0, 0)),
                  pl.BlockSpec((K, tn), lambda i, j: (0, j))],
        out_specs=[pl.BlockSpec((tm, tn), lambda i, j: (i, j)), pl.BlockSpec((tm, K), lambda i, j: (i, 0))],
        out_shape=[jax.ShapeDtypeStruct((T, N), F32), jax.ShapeDtypeStruct((T, K), BF16)],
        semantics=("parallel", "arbitrary"), jobs=jobs)


def _ffn_in(x2, g, w, *, tm, tn, jobs=()):
    T, K = x2.shape
    nb = D_FF // tn

    def body(x_ref, g_ref, wg_ref, wu_ref, dup_ref, dgate_ref, act_ref, h_ref):
        @pl.when(pl.program_id(1) == 0)
        def _():
            _norm_rows(x_ref, g_ref, h_ref)

        wg, wu = wg_ref[...], wu_ref[...]
        for r in range(0, tm, EPILOGUE_ROWS):
            rows = pl.ds(r, min(EPILOGUE_ROWS, tm))
            h = h_ref[rows, :]
            gate, up = _dot(h, wg), _dot(h, wu)
            s = _sigmoid(gate)
            silu = gate * s
            dup_ref[rows, :] = silu.astype(BF16)
            dgate_ref[rows, :] = (up * (s + silu * (1.0 - s))).astype(BF16)
            act_ref[rows, :] = (silu * up).astype(BF16)

    blk = pl.BlockSpec((tm, tn), lambda i, j: (i, j))
    return _pallas(
        body, (x2, g, w, w), name="ffn_in", grid=(T // tm, nb),
        in_specs=[pl.BlockSpec((tm, K), lambda i, j: (i, 0)), pl.BlockSpec((1, K), lambda i, j: (0, 0)),
                  pl.BlockSpec((K, tn), lambda i, j: (0, j)), pl.BlockSpec((K, tn), lambda i, j: (0, j + nb))],
        out_specs=[blk, blk, blk, pl.BlockSpec((tm, K), lambda i, j: (i, 0))],
        out_shape=[jax.ShapeDtypeStruct((T, D_FF), BF16), jax.ShapeDtypeStruct((T, D_FF), BF16),
                   jax.ShapeDtypeStruct((T, D_FF), BF16), jax.ShapeDtypeStruct((T, K), BF16)],
        semantics=("parallel", "arbitrary"), jobs=jobs)


def _branch_mix(pm, z, w_pool_out, w_rnn_out, proj, *, tm, tn):
    T = pm.shape[0]
    col_gp = (D_POOL + 2 * D_RNN) // tn
    col_gr = col_gp + D_MODEL // tn

    def body(pm_ref, z_ref, wp_ref, wr_ref, gp_ref, gr_ref, by_gp_ref, by_gr_ref, sp_ref, sr_ref, mix_ref):
        wp, wr = wp_ref[...], wr_ref[...]
        for r in range(0, tm, EPILOGUE_ROWS):
            rows = pl.ds(r, min(EPILOGUE_ROWS, tm))
            yp, yr = _dot(pm_ref[rows, :], wp), _dot(z_ref[rows, :], wr)
            sp, sr = _sigmoid(gp_ref[rows, :]), _sigmoid(gr_ref[rows, :])
            by_gp_ref[rows, :] = (yp * sp * (1.0 - sp)).astype(BF16)
            by_gr_ref[rows, :] = (yr * sr * (1.0 - sr)).astype(BF16)
            sp_ref[rows, :] = sp.astype(BF16)
            sr_ref[rows, :] = sr.astype(BF16)
            mix_ref[rows, :] = (sp * yp + sr * yr).astype(BF16)

    blk = pl.BlockSpec((tm, tn), lambda i, j: (i, j))
    out = jax.ShapeDtypeStruct((T, D_MODEL), BF16)
    return pl.pallas_call(
        body, name="branch_mix", grid=(T // tm, D_MODEL // tn),
        in_specs=[pl.BlockSpec((tm, D_POOL), lambda i, j: (i, 0)), pl.BlockSpec((tm, D_RNN), lambda i, j: (i, 0)),
                  pl.BlockSpec((D_POOL, tn), lambda i, j: (0, j)), pl.BlockSpec((D_RNN, tn), lambda i, j: (0, j)),
                  pl.BlockSpec((tm, tn), lambda i, j: (i, col_gp + j)), pl.BlockSpec((tm, tn), lambda i, j: (i, col_gr + j))],
        out_specs=[blk] * 5, out_shape=[out] * 5,
        compiler_params=_params(dimension_semantics=("parallel", "parallel")),
    )(pm, z, w_pool_out, w_rnn_out, proj, proj)


def _out_proj_residual(mix, w_o, x, *, tm):
    T = x.shape[0]

    def body(mix_ref, w_ref, x_ref, o_ref):
        o_ref[...] = x_ref[...] + _dot(mix_ref[...], w_ref[...])

    row = pl.BlockSpec((tm, D_MODEL), lambda i: (i, 0))
    return pl.pallas_call(
        body, name="out_proj_residual", grid=(T // tm,),
        in_specs=[row, pl.BlockSpec((D_MODEL, D_MODEL), lambda i: (0, 0)), row],
        out_specs=row, out_shape=jax.ShapeDtypeStruct((T, D_MODEL), F32),
        compiler_params=_params(dimension_semantics=("parallel",)),
    )(mix, w_o, x)


def _ffn_out_loss(act, w, x2, g3, target, *, tm):
    T = x2.shape[0]

    def body(act_ref, w_ref, x2_ref, g_ref, t_ref, dx_ref, dxb_ref, sq_ref, dg_ref):
        @pl.when(pl.program_id(0) == 0)
        def _():
            sq_ref[...] = jnp.zeros_like(sq_ref)
            dg_ref[...] = jnp.zeros_like(dg_ref)

        g, w = g_ref[...], w_ref[...]
        for r in range(0, tm, NORM_ROWS):
            rows = pl.ds(r, min(NORM_ROWS, tm))
            x3 = x2_ref[rows, :] + _dot(act_ref[rows, :], w)
            err = x3 * _rms_scale(x3) * g - t_ref[rows, :]
            sq_ref[...] += jnp.sum(err * err, axis=0, keepdims=True)
            dx, dgp = _rms_bwd(err * (1.0 / D_MODEL), x3, g)
            dg_ref[...] += jnp.sum(dgp, axis=0, keepdims=True)
            dx_ref[rows, :] = dx
            dxb_ref[rows, :] = dx.astype(BF16)

    row = pl.BlockSpec((tm, D_MODEL), lambda i: (i, 0))
    vec = pl.BlockSpec((1, D_MODEL), lambda i: (0, 0))
    return pl.pallas_call(
        body, name="ffn_out_loss", grid=(T // tm,),
        in_specs=[pl.BlockSpec((tm, D_FF), lambda i: (i, 0)), pl.BlockSpec((D_FF, D_MODEL), lambda i: (0, 0)), row, vec, row],
        out_specs=[row, row, vec, vec],
        out_shape=[jax.ShapeDtypeStruct((T, D_MODEL), F32), jax.ShapeDtypeStruct((T, D_MODEL), BF16),
                   jax.ShapeDtypeStruct((1, D_MODEL), F32), jax.ShapeDtypeStruct((1, D_MODEL), F32)],
        compiler_params=_params(dimension_semantics=("arbitrary",)),
    )(act, w, x2, g3, target)


def _ffn_out_bwd(dx3b, w, act_by_gate, act_by_up, *, tm, tn):
    T = dx3b.shape[0]

    def body(dx_ref, w_ref, by_gate_ref, by_up_ref, dgate_ref, dup_ref):
        w = w_ref[...]
        for r in range(0, tm, EPILOGUE_ROWS):
            rows = pl.ds(r, min(EPILOGUE_ROWS, tm))
            dact = _dot_nt(dx_ref[rows, :], w)
            dgate_ref[rows, :] = (dact * by_gate_ref[rows, :].astype(F32)).astype(BF16)
            dup_ref[rows, :] = (dact * by_up_ref[rows, :].astype(F32)).astype(BF16)

    blk = pl.BlockSpec((tm, tn), lambda i, j: (i, j))
    return pl.pallas_call(
        body, name="ffn_out_bwd", grid=(T // tm, D_FF // tn),
        in_specs=[pl.BlockSpec((tm, D_MODEL), lambda i, j: (i, 0)), pl.BlockSpec((tn, D_MODEL), lambda i, j: (j, 0)), blk, blk],
        out_specs=[blk, blk],
        out_shape=[jax.ShapeDtypeStruct((T, D_FF), BF16), jax.ShapeDtypeStruct((T, D_FF), BF16)],
        compiler_params=_params(dimension_semantics=("parallel", "parallel")),
    )(dx3b, w, act_by_gate, act_by_up)


def _ffn_in_bwd(dgate, dup, w, dx3, x2, g2, *, tm, jobs=()):
    T = x2.shape[0]

    def body(dgate_ref, dup_ref, w_ref, dx3_ref, x2_ref, g_ref, dx_ref, dxb_ref, dg_ref):
        @pl.when(pl.program_id(0) == 0)
        def _():
            dg_ref[...] = jnp.zeros_like(dg_ref)

        g = g_ref[...]
        for r in range(0, tm, NORM_ROWS):
            rows = pl.ds(r, min(NORM_ROWS, tm))
            dh = _dot_nt(dgate_ref[rows, :], w_ref[:, :D_FF]) + _dot_nt(dup_ref[rows, :], w_ref[:, D_FF:])
            dxn, dgp = _rms_bwd(dh, x2_ref[rows, :], g)
            dx = dx3_ref[rows, :] + dxn
            dg_ref[...] += jnp.sum(dgp, axis=0, keepdims=True)
            dx_ref[rows, :] = dx
            dxb_ref[rows, :] = dx.astype(BF16)

    row = pl.BlockSpec((tm, D_MODEL), lambda i: (i, 0))
    wide = pl.BlockSpec((tm, D_FF), lambda i: (i, 0))
    vec = pl.BlockSpec((1, D_MODEL), lambda i: (0, 0))
    return _pallas(
        body, (dgate, dup, w, dx3, x2, g2), name="ffn_in_bwd", grid=(T // tm,),
        in_specs=[wide, wide, pl.BlockSpec((D_MODEL, 2 * D_FF), lambda i: (0, 0)), row, row, vec],
        out_specs=[row, row, vec],
        out_shape=[jax.ShapeDtypeStruct((T, D_MODEL), F32), jax.ShapeDtypeStruct((T, D_MODEL), BF16),
                   jax.ShapeDtypeStruct((1, D_MODEL), F32)],
        semantics=("arbitrary",), jobs=jobs)


def _out_proj_bwd(dx2b, w_o, mix_by, *, tm, tn, jobs=()):
    T = dx2b.shape[0]

    def body(dx_ref, w_ref, *refs):
        w = w_ref[...]
        for r in range(0, tm, EPILOGUE_ROWS):
            rows = pl.ds(r, min(EPILOGUE_ROWS, tm))
            dmix = _dot_nt(dx_ref[rows, :], w)
            for by_ref, d_ref in zip(refs[:4], refs[4:]):
                d_ref[rows, :] = (dmix * by_ref[rows, :].astype(F32)).astype(BF16)

    blk = pl.BlockSpec((tm, tn), lambda i, j: (i, j))
    out = jax.ShapeDtypeStruct((T, D_MODEL), BF16)
    return _pallas(
        body, (dx2b, w_o, *mix_by), name="out_proj_bwd", grid=(T // tm, D_MODEL // tn),
        in_specs=[pl.BlockSpec((tm, D_MODEL), lambda i, j: (i, 0)), pl.BlockSpec((tn, D_MODEL), lambda i, j: (j, 0))] + [blk] * 4,
        out_specs=[blk] * 4, out_shape=[out] * 4, semantics=("parallel", "parallel"), jobs=jobs)


def _branch_bwd(dyp, dyr, w_pool_out, w_rnn_out, *, tm):
    T = dyp.shape[0]

    def body(dyp_ref, dyr_ref, wp_ref, wr_ref, dpm_ref, dz_ref):
        dpm_ref[...] = _dot_nt(dyp_ref[...], wp_ref[...])
        dz_ref[...] = _dot_nt(dyr_ref[...], wr_ref[...])

    row = pl.BlockSpec((tm, D_MODEL), lambda i: (i, 0))
    return pl.pallas_call(
        body, name="branch_bwd", grid=(T // tm,),
        in_specs=[row, row, pl.BlockSpec((D_POOL, D_MODEL), lambda i: (0, 0)), pl.BlockSpec((D_RNN, D_MODEL), lambda i: (0, 0))],
        out_specs=[pl.BlockSpec((tm, D_POOL), lambda i: (i, 0)), pl.BlockSpec((tm, D_RNN), lambda i: (i, 0))],
        out_shape=[jax.ShapeDtypeStruct((T, D_POOL), F32), jax.ShapeDtypeStruct((T, D_RNN), F32)],
        compiler_params=_params(dimension_semantics=("parallel",)),
    )(dyp, dyr, w_pool_out, w_rnn_out)


def _in_proj_bwd(segs, w, dx2, x, g1, *, tm, jobs=()):
    T = x.shape[0]
    widths = [s.shape[1] for s in segs]
    offs = [sum(widths[:k]) for k in range(len(widths))]
    n = len(segs)

    def body(*refs):
        seg_refs, (w_ref, dx2_ref, x_ref, g_ref, dx_ref, dg_ref) = refs[:n], refs[n:]

        @pl.when(pl.program_id(0) == 0)
        def _():
            dg_ref[...] = jnp.zeros_like(dg_ref)

        g = g_ref[...]
        for r in range(0, tm, NORM_ROWS):
            rows = pl.ds(r, min(NORM_ROWS, tm))
            dh = _dot_nt(seg_refs[0][rows, :], w_ref[:, offs[0]:offs[0] + widths[0]])
            for k in range(1, n):
                dh += _dot_nt(seg_refs[k][rows, :], w_ref[:, offs[k]:offs[k] + widths[k]])
            dxn, dgp = _rms_bwd(dh, x_ref[rows, :], g)
            dg_ref[...] += jnp.sum(dgp, axis=0, keepdims=True)
            dx_ref[rows, :] = dx2_ref[rows, :] + dxn

    row = pl.BlockSpec((tm, D_MODEL), lambda i: (i, 0))
    vec = pl.BlockSpec((1, D_MODEL), lambda i: (0, 0))
    return _pallas(
        body, (*segs, w, dx2, x, g1), name="in_proj_bwd", grid=(T // tm,),
        in_specs=[pl.BlockSpec((tm, wd), lambda i: (i, 0)) for wd in widths]
        + [pl.BlockSpec((D_MODEL, D_IN), lambda i: (0, 0)), row, row, vec],
        out_specs=[row, vec],
        out_shape=[jax.ShapeDtypeStruct((T, D_MODEL), F32), jax.ShapeDtypeStruct((1, D_MODEL), F32)],
        semantics=("arbitrary",), jobs=jobs)


def _weight_grad(a, segs, *, tm, tn, name, jobs=None, also_bf16=False):
    T, M = a.shape
    nblk = [s.shape[1] // tn for s in segs]
    first = [sum(nblk[:k]) for k in range(len(segs))]
    n = len(segs)

    def body(a_ref, *refs):
        seg_refs, o_refs = refs[:n], refs[n:]
        j = pl.program_id(1)
        for k in range(n):
            @pl.when((j >= first[k]) & (j < first[k] + nblk[k]))
            def _(k=k):
                grad = _dot_tn(a_ref[...], seg_refs[k][...])
                for o_ref in o_refs:
                    o_ref[...] = grad.astype(o_ref.dtype)

    def seg_spec(k):
        return pl.BlockSpec((T, tn), lambda i, j: (0, jnp.clip(j - first[k], 0, nblk[k] - 1)))

    dtypes = [F32, BF16] if also_bf16 else [F32]
    grad, results = _pallas(
        body, (a, *segs), name=name, grid=(M // tm, sum(nblk)),
        in_specs=[pl.BlockSpec((T, tm), lambda i, j: (0, i))] + [seg_spec(k) for k in range(n)],
        out_specs=[pl.BlockSpec((tm, tn), lambda i, j: (i, j))] * len(dtypes),
        out_shape=[jax.ShapeDtypeStruct((M, sum(nblk) * tn), dtype) for dtype in dtypes],
        semantics=("parallel", "arbitrary"), jobs=jobs or ())
    grad = tuple(grad) if also_bf16 else grad[0]
    return grad if jobs is None else (grad, results)


def _pad_front(dst, src, halo):
    dst[pl.ds(0, halo), :] = jnp.zeros((halo, src.shape[1]), F32)

    def fill(i, carry):
        r0 = pl.multiple_of(i * CHUNK, CHUNK)
        dst[pl.ds(r0 + halo, CHUNK), :] = src[pl.ds(r0, CHUNK), :]
        return carry

    lax.fori_loop(0, src.shape[0] // CHUNK, fill, 0)


def _shift_rows(v, k):
    return pltpu.roll(v, k % v.shape[0], axis=0)


def _window_sums(xs, direction):
    s2 = xs + _shift_rows(xs, direction)
    s4 = s2 + _shift_rows(s2, 2 * direction)
    s8 = s4 + _shift_rows(s4, 4 * direction)
    s16 = s8 + _shift_rows(s8, 8 * direction)
    return s2, s4, s8, s16


def _select_window(g, sums):
    s2, s4, s8, s16 = sums
    return jnp.where(g == 0, s2, jnp.where(g == 1, s4, jnp.where(g == 2, s8, s16)))


def _pool_count(g, start, rows):
    t = start + lax.broadcasted_iota(jnp.int32, (rows, 1), 0)
    return jnp.minimum(t + 1, jnp.left_shift(2, g)).astype(F32)


def _pool_fwd(proj, w_grp, scale):
    T = proj.shape[0]
    nchunk = T // CHUNK

    def body(u_ref, w_ref, s_ref, o_ref, upad):
        g = pl.program_id(0)
        _pad_front(upad, u_ref, POOL_HALO)
        w = w_ref[...].astype(BF16)
        scale_row = s_ref[...]

        def chunk(i, carry):
            r0 = pl.multiple_of(i * CHUNK, CHUNK)
            xs = upad[pl.ds(r0, CHUNK + POOL_HALO), :]
            win = _select_window(g, _window_sums(xs, 1))[POOL_HALO:]
            pooled = win * (1.0 / _pool_count(g, r0, CHUNK)) - xs[POOL_HALO:]
            o_ref[pl.ds(r0, CHUNK), :] = (_dot(pooled.astype(BF16), w) * scale_row).astype(BF16)
            return carry

        lax.fori_loop(0, nchunk, chunk, 0)

    return pl.pallas_call(
        body, name="pool_fwd", grid=(N_POOL_GROUPS,),
        in_specs=[pl.BlockSpec((T, HEAD), lambda g: (0, g)), pl.BlockSpec((None, HEAD, HEAD), lambda g: (g, 0, 0)),
                  pl.BlockSpec((1, HEAD), lambda g: (0, g))],
        out_specs=pl.BlockSpec((T, HEAD), lambda g: (0, g)),
        out_shape=jax.ShapeDtypeStruct((T, D_POOL), BF16),
        scratch_shapes=[pltpu.VMEM((T + POOL_HALO, HEAD), F32)],
        compiler_params=_params(dimension_semantics=("parallel",)),
    )(proj, w_grp, scale)


def _pool_bwd(proj, dpm, w_grp, scale, jobs=()):
    T = proj.shape[0]
    nchunk = T // CHUNK

    def body(u_ref, dpm_ref, w_ref, s_ref, du_ref, dw_ref, ds_ref, upad, zpad, dpool):
        g = pl.program_id(0)
        _pad_front(upad, u_ref, POOL_HALO)
        zpad[pl.ds(T, POOL_HALO), :] = jnp.zeros((POOL_HALO, HEAD), F32)
        dw_ref[...] = jnp.zeros_like(dw_ref)
        ds_ref[...] = jnp.zeros_like(ds_ref)
        w = w_ref[...].astype(BF16)
        scale_row = s_ref[...]

        def chunk(i, carry):
            r0 = pl.multiple_of(i * CHUNK, CHUNK)
            xs = upad[pl.ds(r0, CHUNK + POOL_HALO), :]
            inv_cnt = 1.0 / _pool_count(g, r0, CHUNK)
            pooled = (_select_window(g, _window_sums(xs, 1))[POOL_HALO:] * inv_cnt - xs[POOL_HALO:]).astype(BF16)
            mixed = _dot(pooled, w)
            d = dpm_ref[pl.ds(r0, CHUNK), :]
            ds_ref[...] += jnp.sum(d * mixed, axis=0, keepdims=True)
            dmixed = (d * scale_row).astype(BF16)
            dw_ref[...] += _dot_tn(pooled, dmixed)
            dp = _dot_nt(dmixed, w)
            dpool[pl.ds(r0, CHUNK), :] = dp
            zpad[pl.ds(r0, CHUNK), :] = dp * inv_cnt
            return carry

        lax.fori_loop(0, nchunk, chunk, 0)

        def chunk2(i, carry):
            r0 = pl.multiple_of(i * CHUNK, CHUNK)
            zs = zpad[pl.ds(r0, CHUNK + POOL_HALO), :]
            win = _select_window(g, _window_sums(zs, -1))[:CHUNK]
            du_ref[pl.ds(r0, CHUNK), :] = (win - dpool[pl.ds(r0, CHUNK), :]).astype(BF16)
            return carry

        lax.fori_loop(0, nchunk, chunk2, 0)

    col = pl.BlockSpec((T, HEAD), lambda g: (0, g))
    return _pallas(
        body, (proj, dpm, w_grp, scale), name="pool_bwd", grid=(N_POOL_GROUPS,),
        in_specs=[col, col, pl.BlockSpec((None, HEAD, HEAD), lambda g: (g, 0, 0)), pl.BlockSpec((1, HEAD), lambda g: (0, g))],
        out_specs=[col, pl.BlockSpec((None, HEAD, HEAD), lambda g: (g, 0, 0)), pl.BlockSpec((1, HEAD), lambda g: (0, g))],
        out_shape=[jax.ShapeDtypeStruct((T, D_POOL), BF16), jax.ShapeDtypeStruct((N_POOL_GROUPS, HEAD, HEAD), F32),
                   jax.ShapeDtypeStruct((1, D_POOL), F32)],
        scratch_shapes=[pltpu.VMEM((T + POOL_HALO, HEAD), F32), pltpu.VMEM((T + POOL_HALO, HEAD), F32), pltpu.VMEM((T, HEAD), F32)],
        semantics=("parallel",), jobs=jobs)


def _conv_taps(xs, cw):
    v = cw[CONV_WIDTH - 1] * xs[SUBLANES:]
    for k in range(CONV_WIDTH - 1):
        v += cw[k] * _shift_rows(xs, CONV_WIDTH - 1 - k)[SUBLANES:]
    return v


def _tap_rows(cw_ref):
    return [cw_ref[k:k + 1, :] for k in range(CONV_WIDTH)]


def _softplus_neg(lam):
    return jnp.maximum(-lam, 0.0) + _log1p(jnp.exp(-jnp.abs(lam)))


def _lru_gates(v, wa, ba, wx, bx, sp):
    vb = v.astype(BF16)
    ra = _sigmoid(_dot(vb, wa) + ba)
    ix = _sigmoid(_dot(vb, wx) + bx)
    log_a = -LRU_C * ra * sp
    a = jnp.exp(log_a)
    sq = jnp.sqrt(-jnp.tanh(log_a) * (a * a + 1.0))
    return ra, ix, a, sq


def _row_bcast(v, r):
    return jnp.broadcast_to(v[r:r + 1, :], v.shape)


TILE_BLOCK = 128


def _scan_in_tiles(coef, coef_shift, A_out, B, T, direction):
    order = list(range(SUBLANES)) if direction == 1 else list(range(SUBLANES - 1, -1, -1))
    tiles = min(TILE_BLOCK, T // SUBLANES)
    for base in range(0, T, tiles * SUBLANES):
        def rows(r, base=base):
            return pl.ds(base + r, tiles, stride=SUBLANES)

        A, Bv = coef[rows(order[0] + coef_shift), :], B[rows(order[0]), :]
        A_out[rows(order[0]), :] = A
        for r in order[1:]:
            a = coef[rows(r + coef_shift), :]
            Bv = a * Bv + B[rows(r), :]
            A = a * A
            A_out[rows(r), :] = A
            B[rows(r), :] = Bv


TILES_PER_STEP = 16


def _carry_tiles(A_s, B_s, out, ntile, direction):
    out_row = SUBLANES - 1 if direction == 1 else 0

    def step(k, carry):
        for j in range(TILES_PER_STEP):
            t = k * TILES_PER_STEP + j
            r0 = pl.multiple_of((t if direction == 1 else ntile - 1 - t) * SUBLANES, SUBLANES)
            A, B = A_s[pl.ds(r0, SUBLANES), :], B_s[pl.ds(r0, SUBLANES), :]
            out[pl.ds(r0, SUBLANES), :] = A * carry + B
            carry = _row_bcast(A, out_row) * carry + _row_bcast(B, out_row)
        return carry

    lax.fori_loop(0, ntile // TILES_PER_STEP, step, jnp.zeros((SUBLANES, HEAD), F32))


def _rnn_fwd(proj, conv_w, conv_b, w_a, b_a, w_x, b_x, lam, jobs=()):
    T = proj.shape[0]
    nchunk = T // CHUNK
    ntile = T // SUBLANES

    def body(u_ref, ug_ref, cw_ref, cb_ref, wa_ref, ba_ref, wx_ref, bx_ref, lam_ref,
             h_ref, z_ref, v_ref, ra_ref, ix_ref, a_ref, sq_ref, upad, a_s, b_s):
        _pad_front(upad, u_ref, SUBLANES)
        cw, cb = _tap_rows(cw_ref), cb_ref[...]
        wa, wx = wa_ref[...].astype(BF16), wx_ref[...].astype(BF16)
        ba, bx = ba_ref[...], bx_ref[...]
        sp = _softplus_neg(lam_ref[...])

        def chunk(i, carry):
            rows = pl.ds(pl.multiple_of(i * CHUNK, CHUNK), CHUNK)
            v = _conv_taps(upad[pl.ds(pl.multiple_of(i * CHUNK, CHUNK), CHUNK + SUBLANES), :], cw) + cb
            ra, ix, a, sq = _lru_gates(v, wa, ba, wx, bx, sp)
            v_ref[rows, :], ra_ref[rows, :], ix_ref[rows, :], a_ref[rows, :], sq_ref[rows, :] = v, ra, ix, a, sq
            a_s[rows, :], b_s[rows, :] = a, sq * ix * v
            return carry

        lax.fori_loop(0, nchunk, chunk, 0)
        _scan_in_tiles(a_s, 0, a_s, b_s, T, 1)
        _carry_tiles(a_s, b_s, h_ref, ntile, 1)

        def chunk3(i, carry):
            r0 = pl.multiple_of(i * CHUNK, CHUNK)
            gl, _ = _gelu_parts(ug_ref[pl.ds(r0, CHUNK), :])
            z_ref[pl.ds(r0, CHUNK), :] = (h_ref[pl.ds(r0, CHUNK), :] * gl).astype(BF16)
            return carry

        lax.fori_loop(0, nchunk, chunk3, 0)

    col = pl.BlockSpec((T, HEAD), lambda h: (0, h))
    vec = pl.BlockSpec((1, HEAD), lambda h: (0, h))
    mat = pl.BlockSpec((None, HEAD, HEAD), lambda h: (h, 0, 0))
    return _pallas(
        body, (proj, proj, conv_w, conv_b, w_a, b_a, w_x, b_x, lam), name="rnn_fwd", grid=(N_RNN_HEADS,),
        in_specs=[pl.BlockSpec((T, HEAD), lambda h: (0, COL_RNN + h)), pl.BlockSpec((T, HEAD), lambda h: (0, COL_GATE + h)),
                  pl.BlockSpec((CONV_WIDTH, HEAD), lambda h: (0, h)), vec, mat, vec, mat, vec, vec],
        out_specs=[col] * 7,
        out_shape=[jax.ShapeDtypeStruct((T, D_RNN), F32), jax.ShapeDtypeStruct((T, D_RNN), BF16)]
        + [jax.ShapeDtypeStruct((T, D_RNN), F32)] * 5,
        scratch_shapes=[pltpu.VMEM((T + SUBLANES, HEAD), F32), pltpu.VMEM((T, HEAD), F32), pltpu.VMEM((T, HEAD), F32)],
        semantics=("parallel",), jobs=jobs)


def _rnn_bwd(proj, hr, dz, gates, conv_w, w_a, w_x, lam, jobs=()):
    T = proj.shape[0]
    nchunk = T // CHUNK
    ntile = T // SUBLANES

    def body(u_ref, ug_ref, h_ref, dz_ref, v_ref, ra_ref, ix_ref, a_ref, sq_ref, cw_ref, wa_ref, wx_ref, lam_ref,
             du_ref, dug_ref, dwa_ref, dwx_ref, dba_ref, dbx_ref, dlam_ref, dcb_ref, dcw_ref,
             upad, hpad, apad, g_s, dvpad, ga_s):
        zero_tile = jnp.zeros((SUBLANES, HEAD), F32)
        _pad_front(upad, u_ref, SUBLANES)
        _pad_front(hpad, h_ref, SUBLANES)
        apad[pl.ds(T, SUBLANES), :] = zero_tile
        dvpad[pl.ds(T, SUBLANES), :] = zero_tile
        for ref in (dwa_ref, dwx_ref, dba_ref, dbx_ref, dlam_ref, dcb_ref, dcw_ref):
            ref[...] = jnp.zeros_like(ref)
        cw = _tap_rows(cw_ref)
        wa, wx = wa_ref[...].astype(BF16), wx_ref[...].astype(BF16)
        lam_row = lam_ref[...]
        sp = _softplus_neg(lam_row)

        def chunk(i, carry):
            rows = pl.ds(pl.multiple_of(i * CHUNK, CHUNK), CHUNK)
            apad[rows, :] = a_ref[rows, :]
            gl, dgl = _gelu_parts(ug_ref[rows, :])
            d = dz_ref[rows, :]
            g_s[rows, :] = d * gl
            dug_ref[rows, :] = (d * h_ref[rows, :] * dgl).astype(BF16)
            return carry

        lax.fori_loop(0, nchunk, chunk, 0)

        _scan_in_tiles(apad, 1, ga_s, g_s, T, -1)
        _carry_tiles(ga_s, g_s, g_s, ntile, -1)

        def chunk3(i, carry):
            r0 = pl.multiple_of(i * CHUNK, CHUNK)
            rows = pl.ds(r0, CHUNK)
            g = g_s[rows, :]
            h_prev = _shift_rows(hpad[pl.ds(r0, CHUNK + SUBLANES), :], 1)[SUBLANES:]
            v, ra, ix, sq, a = v_ref[rows, :], ra_ref[rows, :], ix_ref[rows, :], sq_ref[rows, :], a_ref[rows, :]
            d_sq = g * ix * v
            d_ix = g * sq * v
            d_la = a * g * h_prev - d_sq * a * a / sq
            dlam_ref[...] += jnp.sum(d_la * ra, axis=0, keepdims=True)
            d_pa = d_la * (-LRU_C) * sp * ra * (1.0 - ra)
            d_px = d_ix * ix * (1.0 - ix)
            vb, d_pab, d_pxb = v.astype(BF16), d_pa.astype(BF16), d_px.astype(BF16)
            dwa_ref[...] += _dot_tn(vb, d_pab)
            dwx_ref[...] += _dot_tn(vb, d_pxb)
            dba_ref[...] += jnp.sum(d_pa, axis=0, keepdims=True)
            dbx_ref[...] += jnp.sum(d_px, axis=0, keepdims=True)
            dv = g * sq * ix + _dot_nt(d_pab, wa) + _dot_nt(d_pxb, wx)
            dvpad[rows, :] = dv
            dcb_ref[...] += jnp.sum(dv, axis=0, keepdims=True)
            xs = upad[pl.ds(r0, CHUNK + SUBLANES), :]
            for k in range(CONV_WIDTH):
                u_k = _shift_rows(xs, CONV_WIDTH - 1 - k)[SUBLANES:] if k < CONV_WIDTH - 1 else xs[SUBLANES:]
                dcw_ref[k:k + 1, :] += jnp.sum(dv * u_k, axis=0, keepdims=True)
            return carry

        lax.fori_loop(0, nchunk, chunk3, 0)
        dlam_ref[...] = dlam_ref[...] * (LRU_C * _sigmoid(-lam_row))

        def chunk4(i, carry):
            r0 = pl.multiple_of(i * CHUNK, CHUNK)
            dvs = dvpad[pl.ds(r0, CHUNK + SUBLANES), :]
            du = cw[CONV_WIDTH - 1] * dvs[:CHUNK]
            for k in range(CONV_WIDTH - 1):
                du += cw[k] * _shift_rows(dvs, -(CONV_WIDTH - 1 - k))[:CHUNK]
            du_ref[pl.ds(r0, CHUNK), :] = du.astype(BF16)
            return carry

        lax.fori_loop(0, nchunk, chunk4, 0)

    col = pl.BlockSpec((T, HEAD), lambda h: (0, h))
    vec = pl.BlockSpec((1, HEAD), lambda h: (0, h))
    mat = pl.BlockSpec((None, HEAD, HEAD), lambda h: (h, 0, 0))
    taps = pl.BlockSpec((CONV_WIDTH, HEAD), lambda h: (0, h))
    vec_out = jax.ShapeDtypeStruct((1, D_RNN), F32)
    mat_out = jax.ShapeDtypeStruct((N_RNN_HEADS, HEAD, HEAD), F32)
    seq = pltpu.VMEM((T, HEAD), F32)
    seq_pad = pltpu.VMEM((T + SUBLANES, HEAD), F32)
    return _pallas(
        body, (proj, proj, hr, dz, *gates, conv_w, w_a, w_x, lam), name="rnn_bwd", grid=(N_RNN_HEADS,),
        in_specs=[pl.BlockSpec((T, HEAD), lambda h: (0, COL_RNN + h)), pl.BlockSpec((T, HEAD), lambda h: (0, COL_GATE + h))]
        + [col] * 7 + [taps, mat, mat, vec],
        out_specs=[col, col, mat, mat, vec, vec, vec, vec, taps],
        out_shape=[jax.ShapeDtypeStruct((T, D_RNN), BF16), jax.ShapeDtypeStruct((T, D_RNN), BF16), mat_out, mat_out,
                   vec_out, vec_out, vec_out, vec_out, jax.ShapeDtypeStruct((CONV_WIDTH, D_RNN), F32)],
        scratch_shapes=[seq_pad, seq_pad, seq_pad, seq, seq_pad, seq],
        semantics=("parallel",), jobs=jobs)


GROUP_FFN_OUT = ["w_ffn_out"]
GROUP_FFN_IN = ["w_ffn_in"]
GROUP_MIX = ["w_o", "w_pool_out", "w_rnn_out"]
GROUP_IN = ["w_in"]


def _step(x, target, s, full, conv_w, place):
    T = x.shape[0]
    tall, mid, low = min(T, 2048), min(T, 1024), min(T, 512)
    full = dict(full)

    def gathered(names, results):
        full.update(zip(names, results))

    early = ["w_pool_out", "w_rnn_out", "w_o", "w_ffn_out"]
    (proj, h1), (res,) = _norm_matmul(x, s["norm_mix"], full["w_in"], tm=tall, tn=512, name="in_proj", jobs=[_gather_job(full, early)])
    gathered(early, res)
    pm = _pool_fwd(proj, s["w_pool_grp"], s["pool_scale"])
    (hr, z, *gates), (res,) = _rnn_fwd(proj, conv_w, s["conv_b"], s["w_rg_a"], s["b_rg_a"], s["w_rg_x"], s["b_rg_x"],
                                       s["lru_lambda"], jobs=[_gather_job(full, ["w_ffn_in"])])
    gathered(["w_ffn_in"], res)
    *mix_by, mix = _branch_mix(pm, z, full["w_pool_out"], full["w_rnn_out"], proj, tm=tall, tn=256)
    x2 = _out_proj_residual(mix, full["w_o"], x, tm=mid)
    (act_by_up, act_by_gate, act, h2), _ = _ffn_in(x2, s["norm_ffn"], full["w_ffn_in"], tm=tall, tn=256)
    dx3, dx3b, sq_cols, g_norm_final = _ffn_out_loss(act, full["w_ffn_out"], x2, s["norm_final"], target, tm=low)

    g = {"norm_final": g_norm_final}

    def chip_sums(names, from_sibling):
        sums = {name: _chip_sum(name, g[name], got, place) for name, got in zip(names, from_sibling)}
        return {name: v[0] for name, v in sums.items()}, {name: v[1] for name, v in sums.items()}

    def final_sums(names, sums, from_chips):
        return {name: _final_sum(name, sums[name], got, place) for name, got in zip(names, from_chips)}

    dgate, dup = _ffn_out_bwd(dx3b, full["w_ffn_out"], act_by_gate, act_by_up, tm=tall, tn=256)
    gb = {}
    g["w_ffn_out"], gb["w_ffn_out"] = _weight_grad(act, [dx3b], tm=256, tn=D_MODEL, name="w_ffn_out_grad", also_bf16=True)
    (dx2, dx2b, g["norm_ffn"]), (res,) = _ffn_in_bwd(dgate, dup, full["w_ffn_in"], dx3, x2, s["norm_ffn"], tm=low,
                                                     jobs=[_sibling_job(gb, GROUP_FFN_OUT, BF16)])
    sums_ffn, sums_ffn_bf16 = chip_sums(GROUP_FFN_OUT, res)
    (g["w_ffn_in"], gb["w_ffn_in"]), (res,) = _weight_grad(h2, [dgate, dup], tm=D_MODEL, tn=256, name="w_ffn_in_grad", also_bf16=True,
                                                           jobs=[_chips_job(sums_ffn_bf16, GROUP_FFN_OUT)])
    shards_ffn = final_sums(GROUP_FFN_OUT, sums_ffn, res)
    (dgp, dgr, dyp, dyr), (res,) = _out_proj_bwd(dx2b, full["w_o"], mix_by, tm=tall, tn=256,
                                                 jobs=[_sibling_job(gb, GROUP_FFN_IN, BF16)])
    sums_ffn, sums_ffn_bf16 = chip_sums(GROUP_FFN_IN, res)
    g["w_o"], gb["w_o"] = _weight_grad(mix, [dx2b], tm=D_MODEL, tn=256, name="w_o_grad", also_bf16=True)
    dpm, dz = _branch_bwd(dyp, dyr, full["w_pool_out"], full["w_rnn_out"], tm=mid)
    g["w_pool_out"], gb["w_pool_out"] = _weight_grad(pm, [dyp], tm=D_POOL, tn=256, name="w_pool_out_grad", also_bf16=True)
    g["w_rnn_out"], gb["w_rnn_out"] = _weight_grad(z, [dyr], tm=D_RNN, tn=256, name="w_rnn_out_grad", also_bf16=True)
    (dupool, g["w_pool_grp"], g["pool_scale"]), (res,) = _pool_bwd(proj, dpm, s["w_pool_grp"], s["pool_scale"],
                                                                   jobs=[_sibling_job(gb, GROUP_MIX, BF16)])
    sums_mix, sums_mix_bf16 = chip_sums(GROUP_MIX, res)
    ((durnn, dugate, g["w_rg_a"], g["w_rg_x"], g["b_rg_a"], g["b_rg_x"], g["lru_lambda"], g["conv_b"], g["conv_w"]),
     (res,)) = _rnn_bwd(proj, hr, dz, gates, conv_w, s["w_rg_a"], s["w_rg_x"], s["lru_lambda"],
                        jobs=[_chips_job(sums_ffn_bf16, GROUP_FFN_IN)])
    shards_ffn.update(final_sums(GROUP_FFN_IN, sums_ffn, res))
    segs = [dupool, durnn, dugate, dgp, dgr]
    ffn = GROUP_FFN_OUT + GROUP_FFN_IN
    (g["w_in"], gb["w_in"]), (res, joined) = _weight_grad(
        h1, segs, tm=D_MODEL, tn=256, name="w_in_grad", also_bf16=True,
        jobs=[_chips_job(sums_mix_bf16, GROUP_MIX), _join_job(shards_ffn, ffn)])
    grads = dict(zip(ffn, joined))
    shards = final_sums(GROUP_MIX, sums_mix, res)
    (res,) = _run_jobs([_sibling_job(gb, GROUP_IN, BF16)], "w_in_exchange_sibling")
    sums_in, sums_in_bf16 = chip_sums(GROUP_IN, res)
    (grad_x, g["norm_mix"]), (res,) = _in_proj_bwd(segs, full["w_in"], dx2, x, s["norm_mix"], tm=low,
                                                  jobs=[_chips_job(sums_in_bf16, GROUP_IN)])
    shards.update(final_sums(GROUP_IN, sums_in, res))

    vec_rows = [g[name] if name != "pool_scale" else jnp.pad(g[name], ((0, 0), (0, D_MODEL - D_POOL))) for name in VEC_ITEMS]
    vec_rows += [g["conv_w"], sq_cols, jnp.zeros((VEC_ROWS - len(VEC_ITEMS) - CONV_WIDTH - 1, D_MODEL), F32)]
    vec = jnp.concatenate(vec_rows, axis=0).reshape(VEC_ROWS, N_DEV, HEAD).transpose(1, 0, 2)
    mat = jnp.concatenate([g[name].reshape(-1, HEAD) for name in MAT_ITEMS], axis=0).reshape(N_DEV, -1, HEAD)
    (vec, mat), (joined,) = _all_reduce_small([vec, mat], [False, True], jobs=[_join_job(shards, GROUP_MIX + GROUP_IN)])
    grads.update(zip(GROUP_MIX + GROUP_IN, joined))

    vec = vec.transpose(1, 0, 2).reshape(VEC_ROWS, D_MODEL)
    mat = mat.reshape(-1, HEAD)
    for k, name in enumerate(VEC_ITEMS):
        grads[name] = vec[k:k + 1, :s[name].shape[1]]
    grads["conv_w"] = vec[len(VEC_ITEMS):len(VEC_ITEMS) + CONV_WIDTH]
    row = 0
    for name in MAT_ITEMS:
        rows = s[name].shape[0] * HEAD
        grads[name] = mat[row:row + rows]
        row += rows
    return vec[len(VEC_ITEMS) + CONV_WIDTH], grad_x, grads


LARGE = {"w_in": "col", "w_pool_out": "col", "w_rnn_out": "row", "w_o": "row", "w_ffn_in": "col", "w_ffn_out": "row"}
LARGE_SHAPE = {"w_in": (D_MODEL, D_IN), "w_pool_out": (D_POOL, D_MODEL), "w_rnn_out": (D_RNN, D_MODEL),
               "w_o": (D_MODEL, D_MODEL), "w_ffn_in": (D_MODEL, 2 * D_FF), "w_ffn_out": (D_FF, D_MODEL)}


def _place():
    x, y, c = lax.axis_index("x"), lax.axis_index("y"), lax.axis_index("c")
    return 2 * x + y, c


def _chip_device(chip, c):
    return (chip // 2, chip % 2, c)


def _chip_window(ref, kind, shape, chip, half=None):
    K, N = shape
    if kind == "col":
        rows = slice(None) if half is None else pl.ds(half * (K // 2), K // 2)
        return ref.at[rows, pl.ds(chip * (N // N_CHIPS), N // N_CHIPS)]
    ks = K // N_CHIPS
    if half is None:
        return ref.at[pl.ds(chip * ks, ks), :]
    return ref.at[pl.ds(chip * ks + half * (ks // 2), ks // 2), :]


def _row_half(ref, half):
    rows = ref.shape[0] // 2
    return ref.at[pl.ds(half * rows, rows), :]


def _remote(win_src, win_dst, send_sems, recv_sems, idx, to):
    return pltpu.make_async_remote_copy(src_ref=win_src, dst_ref=win_dst, send_sem=send_sems.at[idx], recv_sem=recv_sems.at[idx],
                                        device_id=to, device_id_type=MESH)


def _gather_job(full, names, conv_w_full=None):
    n = len(names)
    cw_cols = D_RNN // N_CHIPS

    def windows(refs, chip, half):
        return [_chip_window(refs[k], LARGE[name], LARGE_SHAPE[name], chip, half) for k, name in enumerate(names)]

    def ici_copies(refs, send_sems, recv_sems, src_chip, dst_chip, c, r):
        wins = windows(refs, src_chip, c)
        if conv_w_full is not None:
            wins.append(refs[n].at[:, pl.ds(src_chip * cw_cols, cw_cols)])
        return [_remote(win, win, send_sems, recv_sems, (k, r), _chip_device(dst_chip, c)) for k, win in enumerate(wins)]

    def forwards(refs, send_sems, recv_sems, src_chip, half, to_core, chip, r):
        return [_remote(win, win, send_sems, recv_sems, (k, 3 + r), _chip_device(chip, to_core))
                for k, win in enumerate(windows(refs, src_chip, half))]

    def start(ins, outs, send_sems, recv_sems):
        chip, c = _place()
        for r in range(3):
            for cp in ici_copies(outs, send_sems, recv_sems, chip, chip ^ (r + 1), c, r):
                cp.start()

    def finish(ins, outs, send_sems, recv_sems):
        chip, c = _place()
        for r in range(3):
            for cp in ici_copies(outs, send_sems, recv_sems, chip ^ (r + 1), chip, c, r):
                cp.wait_recv()
            for cp in forwards(outs, send_sems, recv_sems, chip ^ (r + 1), c, 1 - c, chip, r):
                cp.start()
        for r in range(3):
            for cp in forwards(outs, send_sems, recv_sems, chip ^ (r + 1), 1 - c, c, chip, r):
                cp.wait_recv()
            for cp in ici_copies(outs, send_sems, recv_sems, chip, chip ^ (r + 1), c, r):
                cp.wait_send()
            for cp in forwards(outs, send_sems, recv_sems, chip ^ (r + 1), c, 1 - c, chip, r):
                cp.wait_send()

    arrays = [full[name] for name in names] + ([conv_w_full] if conv_w_full is not None else [])
    return _Job(arrays, [jax.ShapeDtypeStruct(a.shape, a.dtype) for a in arrays], {k: k for k in range(len(arrays))},
                (len(arrays), 6), start, finish)


def _core_halves(ref, kind, shape, c):
    return [_chip_window(ref, kind, shape, chip, c) for chip in range(N_CHIPS)]


def _sibling_job(grads, names, dtype=F32):
    def start(ins, outs, send_sems, recv_sems):
        chip, c = _place()
        for k, name in enumerate(names):
            kind, shape = LARGE[name], LARGE_SHAPE[name]
            if kind == "col":
                pairs = [(_row_half(ins[k], 1 - c), outs[k])]
            else:
                rows = shape[0] // N_DEV
                pairs = [(win, outs[k].at[pl.ds(j * rows, rows), :]) for j, win in enumerate(_core_halves(ins[k], kind, shape, 1 - c))]
            for src, dst in pairs:
                _remote(src, dst, send_sems, recv_sems, k, _chip_device(chip, 1 - c)).start()

    def finish(ins, outs, send_sems, recv_sems):
        chip, c = _place()
        for k in range(len(names)):
            _remote(outs[k], outs[k], send_sems, recv_sems, k, _chip_device(chip, 1 - c)).wait()

    return _Job([grads[name] for name in names],
                [jax.ShapeDtypeStruct((LARGE_SHAPE[name][0] // 2, LARGE_SHAPE[name][1]), dtype) for name in names], {},
                (len(names),), start, finish)


def _chip_sum(name, g, got, place):
    kind, (K, N) = LARGE[name], LARGE_SHAPE[name]
    rows = K // N_DEV
    piece_cols = N // N_CHIPS

    def body(place_ref, g_ref, got_ref, o_ref, ob_ref):
        total = g_ref[...] + got_ref[...].astype(F32)
        ob_ref[...] = total.astype(BF16)
        if kind == "col":
            for chip in range(N_CHIPS):
                @pl.when(place_ref[0] == chip)
                def _(chip=chip):
                    o_ref[...] = total[:, chip * piece_cols:(chip + 1) * piece_cols]
        else:
            @pl.when(pl.program_id(0) == place_ref[0])
            def _():
                o_ref[...] = total

    if kind == "col":
        mine = pl.BlockSpec((rows, N), lambda j, place_ref: (j + N_CHIPS * place_ref[1], 0))
        own = pl.BlockSpec((rows, piece_cols), lambda j, place_ref: (j, 0))
    else:
        mine = pl.BlockSpec((rows, N), lambda j, place_ref: (2 * j + place_ref[1], 0))
        own = pl.BlockSpec((rows, N), lambda j, place_ref: (0, 0))
    blk = pl.BlockSpec((rows, N), lambda j, place_ref: (j, 0))
    return pl.pallas_call(
        body, name=name + "_chip_sum",
        grid_spec=pltpu.PrefetchScalarGridSpec(num_scalar_prefetch=1, grid=(N_CHIPS,), in_specs=[mine, blk], out_specs=[own, blk]),
        out_shape=[jax.ShapeDtypeStruct(_piece_shape(name), F32), jax.ShapeDtypeStruct((K // 2, N), BF16)],
        compiler_params=_params(dimension_semantics=("arbitrary",)),
    )(place, g, got)


def _piece(ref, kind, shape, chip):
    K, N = shape
    if kind == "col":
        return ref.at[:, pl.ds(chip * (N // N_CHIPS), N // N_CHIPS)]
    return ref.at[pl.ds(chip * (K // N_DEV), K // N_DEV), :]


def _piece_shape(name):
    kind, (K, N) = LARGE[name], LARGE_SHAPE[name]
    return (K // 2, N // N_CHIPS) if kind == "col" else (K // N_DEV, N)


def _chips_job(sums, names):
    def copies(ins, outs, send_sems, recv_sems):
        chip, c = _place()
        return [_remote(_piece(ins[k], LARGE[name], LARGE_SHAPE[name], chip ^ (r + 1)), outs[k].at[r], send_sems, recv_sems, (k, r),
                        _chip_device(chip ^ (r + 1), c)) for k, name in enumerate(names) for r in range(3)]

    def start(*refs):
        for cp in copies(*refs):
            cp.start()

    def finish(*refs):
        for cp in copies(*refs):
            cp.wait()

    return _Job([sums[name] for name in names], [jax.ShapeDtypeStruct((3,) + _piece_shape(name), BF16) for name in names], {},
                (len(names), 3), start, finish)


def _final_sum(name, chip_sum, got, place):
    rows, cols = _piece_shape(name)

    def body(place_ref, s_ref, got_ref, o_ref):
        o_ref[...] = ((s_ref[...] + got_ref[0].astype(F32)) + got_ref[1].astype(F32)) + got_ref[2].astype(F32)

    mine = pl.BlockSpec((rows, cols), lambda i, place_ref: (0, 0))
    return pl.pallas_call(
        body, name=name + "_final_sum",
        grid_spec=pltpu.PrefetchScalarGridSpec(
            num_scalar_prefetch=1, grid=(1,), in_specs=[mine, pl.BlockSpec((3, rows, cols), lambda i, place_ref: (0, 0, 0))],
            out_specs=pl.BlockSpec((rows, cols), lambda i, place_ref: (place_ref[1], 0))),
        out_shape=jax.ShapeDtypeStruct((2 * rows, cols), F32),
        compiler_params=_params(dimension_semantics=("arbitrary",)),
    )(place, chip_sum, got)


def _join_job(shards, names):
    def half_copy(outs, send_sems, recv_sems, k, mine):
        chip, c = _place()
        win = _row_half(outs[k], c if mine else 1 - c)
        return _remote(win, win, send_sems, recv_sems, k, _chip_device(chip, 1 - c))

    def start(ins, outs, send_sems, recv_sems):
        for k in range(len(names)):
            half_copy(outs, send_sems, recv_sems, k, True).start()

    def finish(ins, outs, send_sems, recv_sems):
        for k in range(len(names)):
            half_copy(outs, send_sems, recv_sems, k, True).wait_send()
            half_copy(outs, send_sems, recv_sems, k, False).wait_recv()

    arrays = [shards[name] for name in names]
    return _Job(arrays, [jax.ShapeDtypeStruct(a.shape, F32) for a in arrays], {k: k for k in range(len(arrays))},
                (len(arrays),), start, finish)


VEC_ROWS = 16


def _all_reduce_small(slabs, narrow, jobs=()):
    n = len(slabs)
    narrowed = [k for k in range(n) if narrow[k]]

    def body(*refs):
        in_refs, out_refs, got_refs = refs[:n], refs[n:2 * n], refs[2 * n:3 * n]
        bf16_refs = dict(zip(narrowed, refs[3 * n:3 * n + len(narrowed)]))
        send_sems, recv_sems = refs[3 * n + len(narrowed):]
        x, y, c = lax.axis_index("x"), lax.axis_index("y"), lax.axis_index("c")
        me = 4 * x + 2 * y + c
        for k, ref in bf16_refs.items():
            ref[...] = in_refs[k][...].astype(BF16)
        partial_refs = [bf16_refs.get(k, in_refs[k]) for k in range(n)]

        def remote(src, dst, k, phase, r):
            other = me ^ r
            return pltpu.make_async_remote_copy(src_ref=src, dst_ref=dst, send_sem=send_sems.at[k, phase, r],
                                                recv_sem=recv_sems.at[k, phase, r],
                                                device_id=(other // 4, (other // 2) % 2, other % 2), device_id_type=MESH)

        scatter = [remote(partial_refs[k].at[me ^ r], got_refs[k].at[r], k, 0, r) for r in range(1, N_DEV) for k in range(n)]
        for cp in scatter:
            cp.start()
        for cp in scatter:
            cp.wait()
        for k in range(n):
            total = in_refs[k][me]
            for r in range(1, N_DEV):
                total = total + got_refs[k][r].astype(F32)
            out_refs[k][me] = total
        gather = [remote(out_refs[k].at[me], out_refs[k].at[me], k, 1, r) for r in range(1, N_DEV) for k in range(n)]
        for cp in gather:
            cp.start()
        for r in range(1, N_DEV):
            for k in range(n):
                remote(out_refs[k].at[me ^ r], out_refs[k].at[me ^ r], k, 1, r).wait_recv()
        for cp in gather:
            cp.wait_send()

    return _pallas(
        body, slabs, name="all_reduce_small", grid=(), in_specs=[VMEM] * n, out_specs=[VMEM] * n,
        out_shape=[jax.ShapeDtypeStruct(s.shape, F32) for s in slabs],
        scratch_shapes=[pltpu.VMEM(s.shape, BF16 if narrow[k] else F32) for k, s in enumerate(slabs)]
        + [pltpu.VMEM(slabs[k].shape, BF16) for k in narrowed]
        + [pltpu.SemaphoreType.DMA((n, 2, N_DEV)), pltpu.SemaphoreType.DMA((n, 2, N_DEV))], jobs=jobs)


def _cast_into_whole(w, name, place):
    rows, cols = w.shape
    tr = rows // 2

    def body(place_ref, w_ref, o_ref):
        o_ref[...] = w_ref[...].astype(BF16)

    if LARGE[name] == "col":
        window = pl.BlockSpec((tr, cols), lambda i, place_ref: (i, place_ref[0]))
    else:
        window = pl.BlockSpec((tr, cols), lambda i, place_ref: (2 * place_ref[0] + i, 0))
    return pl.pallas_call(
        body, name=name + "_cast",
        grid_spec=pltpu.PrefetchScalarGridSpec(num_scalar_prefetch=1, grid=(2,),
                                               in_specs=[pl.BlockSpec((tr, cols), lambda i, place_ref: (i, 0))], out_specs=window),
        out_shape=jax.ShapeDtypeStruct(LARGE_SHAPE[name], BF16),
        compiler_params=_params(dimension_semantics=("parallel",)))(place, w)


def _cast_many_into_whole(shards, place, jobs):
    names = list(shards)
    n = len(names)

    def body(place_ref, *refs):
        for w_ref, o_ref in zip(refs[:n], refs[n:]):
            o_ref[...] = w_ref[...].astype(BF16)

    def window(name):
        rows, cols = shards[name].shape
        if LARGE[name] == "col":
            return pl.BlockSpec((rows // 2, cols), lambda i, place_ref: (i, place_ref[0]))
        return pl.BlockSpec((rows // 2, cols), lambda i, place_ref: (2 * place_ref[0] + i, 0))

    def half(name):
        rows, cols = shards[name].shape
        return pl.BlockSpec((rows // 2, cols), lambda i, place_ref: (i, 0))

    return _pallas(body, [shards[name] for name in names], name="cast_weights", grid=(2,),
                   in_specs=[half(name) for name in names], out_specs=[window(name) for name in names],
                   out_shape=[jax.ShapeDtypeStruct(LARGE_SHAPE[name], BF16) for name in names],
                   semantics=("arbitrary",), jobs=jobs, prefetch=place)


def _adamw_math(w, g, m, v):
    m = ADAM_B1 * m + (1.0 - ADAM_B1) * g
    v = ADAM_B2 * v + (1.0 - ADAM_B2) * (g * g)
    m_hat = m / (1.0 - ADAM_B1 ** ADAM_STEP)
    v_hat = v / (1.0 - ADAM_B2 ** ADAM_STEP)
    delta = -ADAM_LR * (m_hat / (jnp.sqrt(v_hat) + ADAM_EPS) + ADAM_WD * w)
    return delta, m, v


def _adamw_large(w, g, m, v, name):
    rows, cols = w.shape
    steps = 2
    tr = rows // steps

    def body(w_ref, g_ref, m_ref, v_ref, d_ref, mo_ref, vo_ref):
        d_ref[...], mo_ref[...], vo_ref[...] = _adamw_math(w_ref[...], g_ref[...], m_ref[...], v_ref[...])

    blk = pl.BlockSpec((tr, cols), lambda i: (i, 0))
    out = jax.ShapeDtypeStruct(w.shape, F32)
    return pl.pallas_call(body, name=name + "_adamw", grid=(steps,), in_specs=[blk] * 4, out_specs=[blk] * 3, out_shape=[out] * 3,
                          compiler_params=_params(dimension_semantics=("parallel",)))(w, g, m, v)


def _adamw_small(ws, gs, ms, vs):
    n = len(ws)

    def body(*refs):
        for k in range(n):
            w_ref, g_ref, m_ref, v_ref = (refs[q * n + k] for q in range(4))
            d_ref, mo_ref, vo_ref = (refs[(4 + q) * n + k] for q in range(3))
            d_ref[...], mo_ref[...], vo_ref[...] = _adamw_math(w_ref[...], g_ref[...], m_ref[...], v_ref[...])

    out = [jax.ShapeDtypeStruct(w.shape, F32) for w in ws]
    res = pl.pallas_call(body, name="small_adamw", in_specs=[VMEM] * (4 * n), out_specs=[VMEM] * (3 * n), out_shape=out * 3,
                         compiler_params=_params())(*ws, *gs, *ms, *vs)
    return res[:n], res[n:2 * n], res[2 * n:]


WEIGHTS = ["norm_mix", "w_in", "w_pool_grp", "pool_scale", "w_pool_out", "conv_w", "conv_b", "w_rg_a", "b_rg_a", "w_rg_x",
           "b_rg_x", "lru_lambda", "w_rnn_out", "w_o", "norm_ffn", "w_ffn_in", "w_ffn_out", "norm_final"]
VEC_ITEMS = ["norm_mix", "norm_ffn", "norm_final", "pool_scale", "conv_b", "lru_lambda", "b_rg_a", "b_rg_x"]
MAT_ITEMS = ["w_pool_grp", "w_rg_a", "w_rg_x"]


def _as2d(name, a):
    if name in MAT_ITEMS:
        return a.reshape(-1, HEAD, HEAD)
    if name == "conv_w":
        return a.reshape(CONV_WIDTH, -1)
    return a.reshape(1, -1)


def kernel(x, norm_mix, w_in, w_pool_grp, pool_scale, w_pool_out, conv_w, conv_b, w_rg_a, b_rg_a, w_rg_x, b_rg_x, lru_lambda, w_rnn_out, w_o, norm_ffn, w_ffn_in, w_ffn_out, norm_final, loss_target, m_norm_mix, m_w_in, m_w_pool_grp, m_pool_scale, m_w_pool_out, m_conv_w, m_conv_b, m_w_rg_a, m_b_rg_a, m_w_rg_x, m_b_rg_x, m_lru_lambda, m_w_rnn_out, m_w_o, m_norm_ffn, m_w_ffn_in, m_w_ffn_out, m_norm_final, v_norm_mix, v_w_in, v_w_pool_grp, v_pool_scale, v_w_pool_out, v_conv_w, v_conv_b, v_w_rg_a, v_b_rg_a, v_w_rg_x, v_b_rg_x, v_lru_lambda, v_w_rnn_out, v_w_o, v_norm_ffn, v_w_ffn_in, v_w_ffn_out, v_norm_final):
    given = dict(locals())
    w = {name: given[name] for name in WEIGHTS}
    m = {name: given["m_" + name] for name in WEIGHTS}
    v = {name: given["v_" + name] for name in WEIGHTS}
    chip, c = _place()

    place = jnp.stack([chip, c]).astype(jnp.int32)
    conv_cols = w["conv_w"].shape[-1]
    conv_w_mine = lax.dynamic_update_slice_in_dim(jnp.zeros((CONV_WIDTH, D_RNN), F32), w["conv_w"][0], chip * conv_cols, axis=1)
    w_in_mine = _cast_into_whole(w["w_in"][0], "w_in", place)
    later = [name for name in LARGE if name != "w_in"]
    casts, ((w_in_full, conv_w_full),) = _cast_many_into_whole(
        {name: w[name][0] for name in later}, place, jobs=[_gather_job({"w_in": w_in_mine}, ["w_in"], conv_w_mine)])
    full = dict(zip(later, casts), w_in=w_in_full)
    small = {name: _as2d(name, w[name]) for name in WEIGHTS if name not in LARGE and name != "conv_w"}
    sq_cols, grad_x, grads = _step(x[0], loss_target[0], small, full, conv_w_full, place)
    loss = 0.5 / D_MODEL * jnp.sum(sq_cols)
    grads["conv_w"] = lax.dynamic_slice_in_dim(grads["conv_w"], chip * conv_cols, conv_cols, axis=1)

    delta, new_m, new_v = {}, {}, {}
    for name in LARGE:
        delta[name], new_m[name], new_v[name] = _adamw_large(w[name][0], grads[name], m[name][0], v[name][0], name)
    small_names = [name for name in WEIGHTS if name not in LARGE]
    flat = lambda d: [d[name].reshape(grads[name].shape) for name in small_names]
    ds, mo, vo = _adamw_small(flat(w), [grads[name] for name in small_names], flat(m), flat(v))
    for k, name in enumerate(small_names):
        delta[name], new_m[name], new_v[name] = ds[k], mo[k], vo[k]

    shaped = lambda d: [d[name].reshape(w[name].shape) for name in WEIGHTS]
    return (loss, grad_x[None], *shaped(grads), *shaped(delta), *shaped(new_m), *shaped(new_v))
```

```python
import functools
import math

import jax
import jax.numpy as jnp
from jax import lax
from jax.experimental import pallas as pl
from jax.experimental.pallas import tpu as pltpu

F32 = jnp.float32
BF16 = jnp.bfloat16

D_MODEL = 1024
D_POOL = 512
N_POOL_GROUPS = 4
D_RNN = 1024
N_RNN_HEADS = 8
HEAD = 128
CONV_WIDTH = 4
LRU_C = 8.0
D_FF = 2816
D_IN = D_POOL + 2 * D_RNN + 2 * D_MODEL
NORM_EPS = 1e-6
COL_RNN = D_POOL // HEAD
COL_GATE = (D_POOL + D_RNN) // HEAD

ADAM_LR = 0.001
ADAM_B1 = 0.9
ADAM_B2 = 0.999
ADAM_EPS = 1e-08
ADAM_WD = 0.01
ADAM_STEP = 10

N_CHIPS = 4
N_DEV = 8
MESH = pl.DeviceIdType.MESH
ANY = pl.BlockSpec(memory_space=pl.ANY)
VMEM = pl.BlockSpec(memory_space=pltpu.VMEM)
VMEM_LIMIT_BYTES = 60 * 1024 * 1024
SUBLANES = 8
POOL_HALO = 16
CHUNK = 1024

GELU_C = math.sqrt(2.0 / math.pi)
GELU_A = 0.044715


def _params(**kw):
    return pltpu.CompilerParams(vmem_limit_bytes=VMEM_LIMIT_BYTES, **kw)


def _sigmoid(x):
    return 0.5 * jnp.tanh(0.5 * x) + 0.5


def _log1p(y):
    u = 1.0 + y
    d = u - 1.0
    return jnp.where(d == 0.0, y, jnp.log(u) * (y / jnp.where(d == 0.0, 1.0, d)))


def _gelu_parts(x):
    x2 = x * x
    th = jnp.tanh(GELU_C * (x + GELU_A * x * x2))
    g = 0.5 * x * (1.0 + th)
    dg = 0.5 * (1.0 + th) + 0.5 * x * (1.0 - th * th) * GELU_C * (1.0 + 3.0 * GELU_A * x2)
    return g, dg


def _dot(a, b):
    return jnp.dot(a, b, preferred_element_type=F32)


def _dot_nt(a, b):
    return lax.dot_general(a, b, (((1,), (1,)), ((), ())), preferred_element_type=F32)


def _dot_tn(a, b):
    return lax.dot_general(a, b, (((0,), (0,)), ((), ())), preferred_element_type=F32)


def _rms_scale(xv):
    return lax.rsqrt(jnp.mean(xv * xv, axis=-1, keepdims=True) + NORM_EPS)


def _rms_bwd(dy, xv, g):
    r = _rms_scale(xv)
    xh = xv * r
    dyg = dy * g
    dx = r * (dyg - xh * jnp.mean(dyg * xh, axis=-1, keepdims=True))
    return dx, dy * xh


class _Job:
    def __init__(self, inputs, out_shapes, aliases, sem_shape, start, finish):
        self.inputs, self.out_shapes, self.aliases, self.sem_shape = list(inputs), list(out_shapes), dict(aliases), sem_shape
        self.start, self.finish = start, finish


def _pallas(body, operands, *, name, grid, in_specs, out_specs, out_shape, scratch_shapes=(), semantics=None, jobs=(),
            prefetch=None):
    n_in, n_out, n_scr = len(in_specs), len(out_specs), len(scratch_shapes)
    n_pre = 0 if prefetch is None else 1
    job_in = [a for job in jobs for a in job.inputs]
    job_out = [s for job in jobs for s in job.out_shapes]
    aliases, i0, o0 = {}, n_pre + n_in, n_out
    for job in jobs:
        aliases.update({i0 + i: o0 + o for i, o in job.aliases.items()})
        i0, o0 = i0 + len(job.inputs), o0 + len(job.out_shapes)

    def whole(*refs):
        pre, refs = refs[:n_pre], refs[n_pre:]
        ins, j_ins = refs[:n_in], refs[n_in:n_in + len(job_in)]
        outs = refs[n_in + len(job_in):][:n_out]
        j_outs = refs[n_in + len(job_in) + n_out:][:len(job_out)]
        rest = refs[n_in + len(job_in) + n_out + len(job_out):]
        scr, sems = rest[:n_scr], rest[n_scr:]

        def run(phase):
            i, o = 0, 0
            for k, job in enumerate(jobs):
                getattr(job, phase)(j_ins[i:i + len(job.inputs)], j_outs[o:o + len(job.out_shapes)], sems[2 * k], sems[2 * k + 1])
                i, o = i + len(job.inputs), o + len(job.out_shapes)

        def at(step_of, phase):
            if not jobs:
                return
            if not grid:
                run(phase)
                return
            cond = functools.reduce(jnp.logical_and, [pl.program_id(d) == step_of(d) for d in range(len(grid))])
            pl.when(cond)(functools.partial(run, phase))

        at(lambda d: 0, "start")
        body(*pre, *ins, *outs, *scr)
        at(lambda d: grid[d] - 1, "finish")

    layout = dict(grid=grid, in_specs=list(in_specs) + [ANY] * len(job_in), out_specs=list(out_specs) + [ANY] * len(job_out),
                  scratch_shapes=list(scratch_shapes) + [pltpu.SemaphoreType.DMA(job.sem_shape) for job in jobs for _ in range(2)])
    if prefetch is not None:
        layout = dict(grid_spec=pltpu.PrefetchScalarGridSpec(num_scalar_prefetch=1, **layout))
    res = pl.pallas_call(
        whole, name=name, out_shape=list(out_shape) + job_out, input_output_aliases=aliases,
        compiler_params=_params(dimension_semantics=semantics, has_side_effects=bool(jobs)), **layout,
    )(*([] if prefetch is None else [prefetch]), *operands, *job_in)
    per_job, o = [], n_out
    for job in jobs:
        per_job.append(res[o:o + len(job.out_shapes)])
        o += len(job.out_shapes)
    return res[:n_out], per_job


def _run_jobs(jobs, name):
    return _pallas(lambda: None, [], name=name, grid=(), in_specs=[], out_specs=[], out_shape=[], jobs=jobs)[1]


NORM_ROWS = 256
EPILOGUE_ROWS = 512


def _norm_rows(x_ref, g_ref, h_ref):
    g = g_ref[...]

    def rows(i, carry):
        r = pl.ds(pl.multiple_of(i * NORM_ROWS, NORM_ROWS), NORM_ROWS)
        xv = x_ref[r, :]
        h_ref[r, :] = (xv * _rms_scale(xv) * g).astype(BF16)
        return carry

    lax.fori_loop(0, x_ref.shape[0] // NORM_ROWS, rows, 0)


def _norm_matmul(x, g, w, *, tm, tn, name, jobs=()):
    T, K = x.shape
    N = w.shape[1]

    def body(x_ref, g_ref, w_ref, o_ref, h_ref):
        @pl.when(pl.program_id(1) == 0)
        def _():
            _norm_rows(x_ref, g_ref, h_ref)

        o_ref[...] = _dot(h_ref[...], w_ref[...])

    return _pallas(
        body, (x, g, w), name=name, grid=(T // tm, N // tn),
        in_specs=[pl.BlockSpec((tm, K), lambda i, j: (i, 0)), pl.BlockSpec((1, K), lambda i, j: (0, 0)),
                  pl.BlockSpec((K, tn), lambda i, j: (0, j))],
        out_specs=[pl.BlockSpec((tm, tn), lambda i, j: (i, j)), pl.BlockSpec((tm, K), lambda i, j: (i, 0))],
        out_shape=[jax.ShapeDtypeStruct((T, N), F32), jax.ShapeDtypeStruct((T, K), BF16)],
        semantics=("parallel", "arbitrary"), jobs=jobs)


def _ffn_in(x2, g, w, *, tm, tn, jobs=()):
    T, K = x2.shape
    nb = D_FF // tn

    def body(x_ref, g_ref, wg_ref, wu_ref, dup_ref, dgate_ref, act_ref, h_ref):
        @pl.when(pl.program_id(1) == 0)
        def _():
            _norm_rows(x_ref, g_ref, h_ref)

        wg, wu = wg_ref[...], wu_ref[...]
        for r in range(0, tm, EPILOGUE_ROWS):
            rows = pl.ds(r, min(EPILOGUE_ROWS, tm))
            h = h_ref[rows, :]
            gate, up = _dot(h, wg), _dot(h, wu)
            s = _sigmoid(gate)
            silu = gate * s
            dup_ref[rows, :] = silu.astype(BF16)
            dgate_ref[rows, :] = (up * (s + silu * (1.0 - s))).astype(BF16)
            act_ref[rows, :] = (silu * up).astype(BF16)

    blk = pl.BlockSpec((tm, tn), lambda i, j: (i, j))
    return _pallas(
        body, (x2, g, w, w), name="ffn_in", grid=(T // tm, nb),
        in_specs=[pl.BlockSpec((tm, K), lambda i, j: (i, 0)), pl.BlockSpec((1, K), lambda i, j: (0, 0)),
                  pl.BlockSpec((K, tn), lambda i, j: (0, j)), pl.BlockSpec((K, tn), lambda i, j: (0, j + nb))],
        out_specs=[blk, blk, blk, pl.BlockSpec((tm, K), lambda i, j: (i, 0))],
        out_shape=[jax.ShapeDtypeStruct((T, D_FF), BF16), jax.ShapeDtypeStruct((T, D_FF), BF16),
                   jax.ShapeDtypeStruct((T, D_FF), BF16), jax.ShapeDtypeStruct((T, K), BF16)],
        semantics=("parallel", "arbitrary"), jobs=jobs)


def _branch_mix(pm, z, w_pool_out, w_rnn_out, proj, *, tm, tn):
    T = pm.shape[0]
    col_gp = (D_POOL + 2 * D_RNN) // tn
    col_gr = col_gp + D_MODEL // tn

    def body(pm_ref, z_ref, wp_ref, wr_ref, gp_ref, gr_ref, by_gp_ref, by_gr_ref, sp_ref, sr_ref, mix_ref):
        wp, wr = wp_ref[...], wr_ref[...]
        for r in range(0, tm, EPILOGUE_ROWS):
            rows = pl.ds(r, min(EPILOGUE_ROWS, tm))
            yp, yr = _dot(pm_ref[rows, :], wp), _dot(z_ref[rows, :], wr)
            sp, sr = _sigmoid(gp_ref[rows, :]), _sigmoid(gr_ref[rows, :])
            by_gp_ref[rows, :] = (yp * sp * (1.0 - sp)).astype(BF16)
            by_gr_ref[rows, :] = (yr * sr * (1.0 - sr)).astype(BF16)
            sp_ref[rows, :] = sp.astype(BF16)
            sr_ref[rows, :] = sr.astype(BF16)
            mix_ref[rows, :] = (sp * yp + sr * yr).astype(BF16)

    blk = pl.BlockSpec((tm, tn), lambda i, j: (i, j))
    out = jax.ShapeDtypeStruct((T, D_MODEL), BF16)
    return pl.pallas_call(
        body, name="branch_mix", grid=(T // tm, D_MODEL // tn),
        in_specs=[pl.BlockSpec((tm, D_POOL), lambda i, j: (i, 0)), pl.BlockSpec((tm, D_RNN), lambda i, j: (i, 0)),
                  pl.BlockSpec((D_POOL, tn), lambda i, j: (0, j)), pl.BlockSpec((D_RNN, tn), lambda i, j: (0, j)),
                  pl.BlockSpec((tm, tn), lambda i, j: (i, col_gp + j)), pl.BlockSpec((tm, tn), lambda i, j: (i, col_gr + j))],
        out_specs=[blk] * 5, out_shape=[out] * 5,
        compiler_params=_params(dimension_semantics=("parallel", "parallel")),
    )(pm, z, w_pool_out, w_rnn_out, proj, proj)


def _out_proj_residual(mix, w_o, x, *, tm):
    T = x.shape[0]

    def body(mix_ref, w_ref, x_ref, o_ref):
        o_ref[...] = x_ref[...] + _dot(mix_ref[...], w_ref[...])

    row = pl.BlockSpec((tm, D_MODEL), lambda i: (i, 0))
    return pl.pallas_call(
        body, name="out_proj_residual", grid=(T // tm,),
        in_specs=[row, pl.BlockSpec((D_MODEL, D_MODEL), lambda i: (0, 0)), row],
        out_specs=row, out_shape=jax.ShapeDtypeStruct((T, D_MODEL), F32),
        compiler_params=_params(dimension_semantics=("parallel",)),
    )(mix, w_o, x)


def _ffn_out_loss(act, w, x2, g3, target, *, tm):
    T = x2.shape[0]

    def body(act_ref, w_ref, x2_ref, g_ref, t_ref, dx_ref, dxb_ref, sq_ref, dg_ref):
        @pl.when(pl.program_id(0) == 0)
        def _():
            sq_ref[...] = jnp.zeros_like(sq_ref)
            dg_ref[...] = jnp.zeros_like(dg_ref)

        g, w = g_ref[...], w_ref[...]
        for r in range(0, tm, NORM_ROWS):
            rows = pl.ds(r, min(NORM_ROWS, tm))
            x3 = x2_ref[rows, :] + _dot(act_ref[rows, :], w)
            err = x3 * _rms_scale(x3) * g - t_ref[rows, :]
            sq_ref[...] += jnp.sum(err * err, axis=0, keepdims=True)
            dx, dgp = _rms_bwd(err * (1.0 / D_MODEL), x3, g)
            dg_ref[...] += jnp.sum(dgp, axis=0, keepdims=True)
            dx_ref[rows, :] = dx
            dxb_ref[rows, :] = dx.astype(BF16)

    row = pl.BlockSpec((tm, D_MODEL), lambda i: (i, 0))
    vec = pl.BlockSpec((1, D_MODEL), lambda i: (0, 0))
    return pl.pallas_call(
        body, name="ffn_out_loss", grid=(T // tm,),
        in_specs=[pl.BlockSpec((tm, D_FF), lambda i: (i, 0)), pl.BlockSpec((D_FF, D_MODEL), lambda i: (0, 0)), row, vec, row],
        out_specs=[row, row, vec, vec],
        out_shape=[jax.ShapeDtypeStruct((T, D_MODEL), F32), jax.ShapeDtypeStruct((T, D_MODEL), BF16),
                   jax.ShapeDtypeStruct((1, D_MODEL), F32), jax.ShapeDtypeStruct((1, D_MODEL), F32)],
        compiler_params=_params(dimension_semantics=("arbitrary",)),
    )(act, w, x2, g3, target)


def _ffn_out_bwd(dx3b, w, act_by_gate, act_by_up, *, tm, tn):
    T = dx3b.shape[0]

    def body(dx_ref, w_ref, by_gate_ref, by_up_ref, dgate_ref, dup_ref):
        w = w_ref[...]
        for r in range(0, tm, EPILOGUE_ROWS):
            rows = pl.ds(r, min(EPILOGUE_ROWS, tm))
            dact = _dot_nt(dx_ref[rows, :], w)
            dgate_ref[rows, :] = (dact * by_gate_ref[rows, :].astype(F32)).astype(BF16)
            dup_ref[rows, :] = (dact * by_up_ref[rows, :].astype(F32)).astype(BF16)

    blk = pl.BlockSpec((tm, tn), lambda i, j: (i, j))
    return pl.pallas_call(
        body, name="ffn_out_bwd", grid=(T // tm, D_FF // tn),
        in_specs=[pl.BlockSpec((tm, D_MODEL), lambda i, j: (i, 0)), pl.BlockSpec((tn, D_MODEL), lambda i, j: (j, 0)), blk, blk],
        out_specs=[blk, blk],
        out_shape=[jax.ShapeDtypeStruct((T, D_FF), BF16), jax.ShapeDtypeStruct((T, D_FF), BF16)],
        compiler_params=_params(dimension_semantics=("parallel", "parallel")),
    )(dx3b, w, act_by_gate, act_by_up)


def _ffn_in_bwd(dgate, dup, w, dx3, x2, g2, *, tm, jobs=()):
    T = x2.shape[0]

    def body(dgate_ref, dup_ref, w_ref, dx3_ref, x2_ref, g_ref, dx_ref, dxb_ref, dg_ref):
        @pl.when(pl.program_id(0) == 0)
        def _():
            dg_ref[...] = jnp.zeros_like(dg_ref)

        g = g_ref[...]
        for r in range(0, tm, NORM_ROWS):
            rows = pl.ds(r, min(NORM_ROWS, tm))
            dh = _dot_nt(dgate_ref[rows, :], w_ref[:, :D_FF]) + _dot_nt(dup_ref[rows, :], w_ref[:, D_FF:])
            dxn, dgp = _rms_bwd(dh, x2_ref[rows, :], g)
            dx = dx3_ref[rows, :] + dxn
            dg_ref[...] += jnp.sum(dgp, axis=0, keepdims=True)
            dx_ref[rows, :] = dx
            dxb_ref[rows, :] = dx.astype(BF16)

    row = pl.BlockSpec((tm, D_MODEL), lambda i: (i, 0))
    wide = pl.BlockSpec((tm, D_FF), lambda i: (i, 0))
    vec = pl.BlockSpec((1, D_MODEL), lambda i: (0, 0))
    return _pallas(
        body, (dgate, dup, w, dx3, x2, g2), name="ffn_in_bwd", grid=(T // tm,),
        in_specs=[wide, wide, pl.BlockSpec((D_MODEL, 2 * D_FF), lambda i: (0, 0)), row, row, vec],
        out_specs=[row, row, vec],
        out_shape=[jax.ShapeDtypeStruct((T, D_MODEL), F32), jax.ShapeDtypeStruct((T, D_MODEL), BF16),
                   jax.ShapeDtypeStruct((1, D_MODEL), F32)],
        semantics=("arbitrary",), jobs=jobs)


def _out_proj_bwd(dx2b, w_o, mix_by, *, tm, tn, jobs=()):
    T = dx2b.shape[0]

    def body(dx_ref, w_ref, *refs):
        w = w_ref[...]
        for r in range(0, tm, EPILOGUE_ROWS):
            rows = pl.ds(r, min(EPILOGUE_ROWS, tm))
            dmix = _dot_nt(dx_ref[rows, :], w)
            for by_ref, d_ref in zip(refs[:4], refs[4:]):
                d_ref[rows, :] = (dmix * by_ref[rows, :].astype(F32)).astype(BF16)

    blk = pl.BlockSpec((tm, tn), lambda i, j: (i, j))
    out = jax.ShapeDtypeStruct((T, D_MODEL), BF16)
    return _pallas(
        body, (dx2b, w_o, *mix_by), name="out_proj_bwd", grid=(T // tm, D_MODEL // tn),
        in_specs=[pl.BlockSpec((tm, D_MODEL), lambda i, j: (i, 0)), pl.BlockSpec((tn, D_MODEL), lambda i, j: (j, 0))] + [blk] * 4,
        out_specs=[blk] * 4, out_shape=[out] * 4, semantics=("parallel", "parallel"), jobs=jobs)


def _branch_bwd(dyp, dyr, w_pool_out, w_rnn_out, *, tm):
    T = dyp.shape[0]

    def body(dyp_ref, dyr_ref, wp_ref, wr_ref, dpm_ref, dz_ref):
        dpm_ref[...] = _dot_nt(dyp_ref[...], wp_ref[...])
        dz_ref[...] = _dot_nt(dyr_ref[...], wr_ref[...])

    row = pl.BlockSpec((tm, D_MODEL), lambda i: (i, 0))
    return pl.pallas_call(
        body, name="branch_bwd", grid=(T // tm,),
        in_specs=[row, row, pl.BlockSpec((D_POOL, D_MODEL), lambda i: (0, 0)), pl.BlockSpec((D_RNN, D_MODEL), lambda i: (0, 0))],
        out_specs=[pl.BlockSpec((tm, D_POOL), lambda i: (i, 0)), pl.BlockSpec((tm, D_RNN), lambda i: (i, 0))],
        out_shape=[jax.ShapeDtypeStruct((T, D_POOL), F32), jax.ShapeDtypeStruct((T, D_RNN), F32)],
        compiler_params=_params(dimension_semantics=("parallel",)),
    )(dyp, dyr, w_pool_out, w_rnn_out)


def _in_proj_bwd(segs, w, dx2, x, g1, *, tm, jobs=()):
    T = x.shape[0]
    widths = [s.shape[1] for s in segs]
    offs = [sum(widths[:k]) for k in range(len(widths))]
    n = len(segs)

    def body(*refs):
        seg_refs, (w_ref, dx2_ref, x_ref, g_ref, dx_ref, dg_ref) = refs[:n], refs[n:]

        @pl.when(pl.program_id(0) == 0)
        def _():
            dg_ref[...] = jnp.zeros_like(dg_ref)

        g = g_ref[...]
        for r in range(0, tm, NORM_ROWS):
            rows = pl.ds(r, min(NORM_ROWS, tm))
            dh = _dot_nt(seg_refs[0][rows, :], w_ref[:, offs[0]:offs[0] + widths[0]])
            for k in range(1, n):
                dh += _dot_nt(seg_refs[k][rows, :], w_ref[:, offs[k]:offs[k] + widths[k]])
            dxn, dgp = _rms_bwd(dh, x_ref[rows, :], g)
            dg_ref[...] += jnp.sum(dgp, axis=0, keepdims=True)
            dx_ref[rows, :] = dx2_ref[rows, :] + dxn

    row = pl.BlockSpec((tm, D_MODEL), lambda i: (i, 0))
    vec = pl.BlockSpec((1, D_MODEL), lambda i: (0, 0))
    return _pallas(
        body, (*segs, w, dx2, x, g1), name="in_proj_bwd", grid=(T // tm,),
        in_specs=[pl.BlockSpec((tm, wd), lambda i: (i, 0)) for wd in widths]
        + [pl.BlockSpec((D_MODEL, D_IN), lambda i: (0, 0)), row, row, vec],
        out_specs=[row, vec],
        out_shape=[jax.ShapeDtypeStruct((T, D_MODEL), F32), jax.ShapeDtypeStruct((1, D_MODEL), F32)],
        semantics=("arbitrary",), jobs=jobs)


def _weight_grad(a, segs, *, tm, tn, name, jobs=None, also_bf16=False):
    T, M = a.shape
    nblk = [s.shape[1] // tn for s in segs]
    first = [sum(nblk[:k]) for k in range(len(segs))]
    n = len(segs)

    def body(a_ref, *refs):
        seg_refs, o_refs = refs[:n], refs[n:]
        j = pl.program_id(1)
        for k in range(n):
            @pl.when((j >= first[k]) & (j < first[k] + nblk[k]))
            def _(k=k):
                grad = _dot_tn(a_ref[...], seg_refs[k][...])
                for o_ref in o_refs:
                    o_ref[...] = grad.astype(o_ref.dtype)

    def seg_spec(k):
        return pl.BlockSpec((T, tn), lambda i, j: (0, jnp.clip(j - first[k], 0, nblk[k] - 1)))

    dtypes = [F32, BF16] if also_bf16 else [F32]
    grad, results = _pallas(
        body, (a, *segs), name=name, grid=(M // tm, sum(nblk)),
        in_specs=[pl.BlockSpec((T, tm), lambda i, j: (0, i))] + [seg_spec(k) for k in range(n)],
        out_specs=[pl.BlockSpec((tm, tn), lambda i, j: (i, j))] * len(dtypes),
        out_shape=[jax.ShapeDtypeStruct((M, sum(nblk) * tn), dtype) for dtype in dtypes],
        semantics=("parallel", "arbitrary"), jobs=jobs or ())
    grad = tuple(grad) if also_bf16 else grad[0]
    return grad if jobs is None else (grad, results)


def _pad_front(dst, src, halo):
    dst[pl.ds(0, halo), :] = jnp.zeros((halo, src.shape[1]), F32)

    def fill(i, carry):
        r0 = pl.multiple_of(i * CHUNK, CHUNK)
        dst[pl.ds(r0 + halo, CHUNK), :] = src[pl.ds(r0, CHUNK), :]
        return carry

    lax.fori_loop(0, src.shape[0] // CHUNK, fill, 0)


def _shift_rows(v, k):
    return pltpu.roll(v, k % v.shape[0], axis=0)


def _window_sums(xs, direction):
    s2 = xs + _shift_rows(xs, direction)
    s4 = s2 + _shift_rows(s2, 2 * direction)
    s8 = s4 + _shift_rows(s4, 4 * direction)
    s16 = s8 + _shift_rows(s8, 8 * direction)
    return s2, s4, s8, s16


def _select_window(g, sums):
    s2, s4, s8, s16 = sums
    return jnp.where(g == 0, s2, jnp.where(g == 1, s4, jnp.where(g == 2, s8, s16)))


def _pool_count(g, start, rows):
    t = start + lax.broadcasted_iota(jnp.int32, (rows, 1), 0)
    return jnp.minimum(t + 1, jnp.left_shift(2, g)).astype(F32)


def _pool_fwd(proj, w_grp, scale):
    T = proj.shape[0]
    nchunk = T // CHUNK

    def body(u_ref, w_ref, s_ref, o_ref, upad):
        g = pl.program_id(0)
        _pad_front(upad, u_ref, POOL_HALO)
        w = w_ref[...].astype(BF16)
        scale_row = s_ref[...]

        def chunk(i, carry):
            r0 = pl.multiple_of(i * CHUNK, CHUNK)
            xs = upad[pl.ds(r0, CHUNK + POOL_HALO), :]
            win = _select_window(g, _window_sums(xs, 1))[POOL_HALO:]
            pooled = win / _pool_count(g, r0, CHUNK) - xs[POOL_HALO:]
            o_ref[pl.ds(r0, CHUNK), :] = (_dot(pooled.astype(BF16), w) * scale_row).astype(BF16)
            return carry

        lax.fori_loop(0, nchunk, chunk, 0)

    return pl.pallas_call(
        body, name="pool_fwd", grid=(N_POOL_GROUPS,),
        in_specs=[pl.BlockSpec((T, HEAD), lambda g: (0, g)), pl.BlockSpec((None, HEAD, HEAD), lambda g: (g, 0, 0)),
                  pl.BlockSpec((1, HEAD), lambda g: (0, g))],
        out_specs=pl.BlockSpec((T, HEAD), lambda g: (0, g)),
        out_shape=jax.ShapeDtypeStruct((T, D_POOL), BF16),
        scratch_shapes=[pltpu.VMEM((T + POOL_HALO, HEAD), F32)],
        compiler_params=_params(dimension_semantics=("parallel",)),
    )(proj, w_grp, scale)


def _pool_bwd(proj, dpm, w_grp, scale, jobs=()):
    T = proj.shape[0]
    nchunk = T // CHUNK

    def body(u_ref, dpm_ref, w_ref, s_ref, du_ref, dw_ref, ds_ref, upad, zpad, dpool):
        g = pl.program_id(0)
        _pad_front(upad, u_ref, POOL_HALO)
        zpad[pl.ds(T, POOL_HALO), :] = jnp.zeros((POOL_HALO, HEAD), F32)
        dw_ref[...] = jnp.zeros_like(dw_ref)
        ds_ref[...] = jnp.zeros_like(ds_ref)
        w = w_ref[...].astype(BF16)
        scale_row = s_ref[...]

        def chunk(i, carry):
            r0 = pl.multiple_of(i * CHUNK, CHUNK)
            xs = upad[pl.ds(r0, CHUNK + POOL_HALO), :]
            cnt = _pool_count(g, r0, CHUNK)
            pooled = (_select_window(g, _window_sums(xs, 1))[POOL_HALO:] / cnt - xs[POOL_HALO:]).astype(BF16)
            mixed = _dot(pooled, w)
            d = dpm_ref[pl.ds(r0, CHUNK), :]
            ds_ref[...] += jnp.sum(d * mixed, axis=0, keepdims=True)
            dmixed = (d * scale_row).astype(BF16)
            dw_ref[...] += _dot_tn(pooled, dmixed)
            dp = _dot_nt(dmixed, w)
            dpool[pl.ds(r0, CHUNK), :] = dp
            zpad[pl.ds(r0, CHUNK), :] = dp / cnt
            return carry

        lax.fori_loop(0, nchunk, chunk, 0)

        def chunk2(i, carry):
            r0 = pl.multiple_of(i * CHUNK, CHUNK)
            zs = zpad[pl.ds(r0, CHUNK + POOL_HALO), :]
            win = _select_window(g, _window_sums(zs, -1))[:CHUNK]
            du_ref[pl.ds(r0, CHUNK), :] = (win - dpool[pl.ds(r0, CHUNK), :]).astype(BF16)
            return carry

        lax.fori_loop(0, nchunk, chunk2, 0)

    col = pl.BlockSpec((T, HEAD), lambda g: (0, g))
    return _pallas(
        body, (proj, dpm, w_grp, scale), name="pool_bwd", grid=(N_POOL_GROUPS,),
        in_specs=[col, col, pl.BlockSpec((None, HEAD, HEAD), lambda g: (g, 0, 0)), pl.BlockSpec((1, HEAD), lambda g: (0, g))],
        out_specs=[col, pl.BlockSpec((None, HEAD, HEAD), lambda g: (g, 0, 0)), pl.BlockSpec((1, HEAD), lambda g: (0, g))],
        out_shape=[jax.ShapeDtypeStruct((T, D_POOL), BF16), jax.ShapeDtypeStruct((N_POOL_GROUPS, HEAD, HEAD), F32),
                   jax.ShapeDtypeStruct((1, D_POOL), F32)],
        scratch_shapes=[pltpu.VMEM((T + POOL_HALO, HEAD), F32), pltpu.VMEM((T + POOL_HALO, HEAD), F32), pltpu.VMEM((T, HEAD), F32)],
        semantics=("parallel",), jobs=jobs)


def _conv_taps(xs, cw):
    v = cw[CONV_WIDTH - 1] * xs[SUBLANES:]
    for k in range(CONV_WIDTH - 1):
        v += cw[k] * _shift_rows(xs, CONV_WIDTH - 1 - k)[SUBLANES:]
    return v


def _tap_rows(cw_ref):
    return [cw_ref[k:k + 1, :] for k in range(CONV_WIDTH)]


def _softplus_neg(lam):
    return jnp.maximum(-lam, 0.0) + _log1p(jnp.exp(-jnp.abs(lam)))


def _lru_gates(v, wa, ba, wx, bx, sp):
    vb = v.astype(BF16)
    ra = _sigmoid(_dot(vb, wa) + ba)
    ix = _sigmoid(_dot(vb, wx) + bx)
    log_a = -LRU_C * ra * sp
    a = jnp.exp(log_a)
    sq = jnp.sqrt(-jnp.tanh(log_a) * (a * a + 1.0))
    return ra, ix, a, sq


def _row_bcast(v, r):
    return jnp.broadcast_to(v[r:r + 1, :], v.shape)


TILE_BLOCK = 128


def _scan_in_tiles(coef, coef_shift, A_out, B, T, direction):
    order = list(range(SUBLANES)) if direction == 1 else list(range(SUBLANES - 1, -1, -1))
    tiles = min(TILE_BLOCK, T // SUBLANES)
    for base in range(0, T, tiles * SUBLANES):
        def rows(r, base=base):
            return pl.ds(base + r, tiles, stride=SUBLANES)

        A, Bv = coef[rows(order[0] + coef_shift), :], B[rows(order[0]), :]
        A_out[rows(order[0]), :] = A
        for r in order[1:]:
            a = coef[rows(r + coef_shift), :]
            Bv = a * Bv + B[rows(r), :]
            A = a * A
            A_out[rows(r), :] = A
            B[rows(r), :] = Bv


TILES_PER_STEP = 8


def _carry_tiles(A_s, B_s, out, ntile, direction):
    out_row = SUBLANES - 1 if direction == 1 else 0

    def step(k, carry):
        for j in range(TILES_PER_STEP):
            t = k * TILES_PER_STEP + j
            r0 = pl.multiple_of((t if direction == 1 else ntile - 1 - t) * SUBLANES, SUBLANES)
            A, B = A_s[pl.ds(r0, SUBLANES), :], B_s[pl.ds(r0, SUBLANES), :]
            out[pl.ds(r0, SUBLANES), :] = A * carry + B
            carry = _row_bcast(A, out_row) * carry + _row_bcast(B, out_row)
        return carry

    lax.fori_loop(0, ntile // TILES_PER_STEP, step, jnp.zeros((SUBLANES, HEAD), F32))


def _rnn_fwd(proj, conv_w, conv_b, w_a, b_a, w_x, b_x, lam, jobs=()):
    T = proj.shape[0]
    nchunk = T // CHUNK
    ntile = T // SUBLANES

    def body(u_ref, ug_ref, cw_ref, cb_ref, wa_ref, ba_ref, wx_ref, bx_ref, lam_ref,
             h_ref, z_ref, v_ref, ra_ref, ix_ref, a_ref, sq_ref, upad, a_s, b_s):
        _pad_front(upad, u_ref, SUBLANES)
        cw, cb = _tap_rows(cw_ref), cb_ref[...]
        wa, wx = wa_ref[...].astype(BF16), wx_ref[...].astype(BF16)
        ba, bx = ba_ref[...], bx_ref[...]
        sp = _softplus_neg(lam_ref[...])

        def chunk(i, carry):
            rows = pl.ds(pl.multiple_of(i * CHUNK, CHUNK), CHUNK)
            v = _conv_taps(upad[pl.ds(pl.multiple_of(i * CHUNK, CHUNK), CHUNK + SUBLANES), :], cw) + cb
            ra, ix, a, sq = _lru_gates(v, wa, ba, wx, bx, sp)
            v_ref[rows, :], ra_ref[rows, :], ix_ref[rows, :], a_ref[rows, :], sq_ref[rows, :] = v, ra, ix, a, sq
            a_s[rows, :], b_s[rows, :] = a, sq * ix * v
            return carry

        lax.fori_loop(0, nchunk, chunk, 0)
        _scan_in_tiles(a_s, 0, a_s, b_s, T, 1)
        _carry_tiles(a_s, b_s, h_ref, ntile, 1)

        def chunk3(i, carry):
            r0 = pl.multiple_of(i * CHUNK, CHUNK)
            gl, _ = _gelu_parts(ug_ref[pl.ds(r0, CHUNK), :])
            z_ref[pl.ds(r0, CHUNK), :] = (h_ref[pl.ds(r0, CHUNK), :] * gl).astype(BF16)
            return carry

        lax.fori_loop(0, nchunk, chunk3, 0)

    col = pl.BlockSpec((T, HEAD), lambda h: (0, h))
    vec = pl.BlockSpec((1, HEAD), lambda h: (0, h))
    mat = pl.BlockSpec((None, HEAD, HEAD), lambda h: (h, 0, 0))
    return _pallas(
        body, (proj, proj, conv_w, conv_b, w_a, b_a, w_x, b_x, lam), name="rnn_fwd", grid=(N_RNN_HEADS,),
        in_specs=[pl.BlockSpec((T, HEAD), lambda h: (0, COL_RNN + h)), pl.BlockSpec((T, HEAD), lambda h: (0, COL_GATE + h)),
                  pl.BlockSpec((CONV_WIDTH, HEAD), lambda h: (0, h)), vec, mat, vec, mat, vec, vec],
        out_specs=[col] * 7,
        out_shape=[jax.ShapeDtypeStruct((T, D_RNN), F32), jax.ShapeDtypeStruct((T, D_RNN), BF16)]
        + [jax.ShapeDtypeStruct((T, D_RNN), F32)] * 5,
        scratch_shapes=[pltpu.VMEM((T + SUBLANES, HEAD), F32), pltpu.VMEM((T, HEAD), F32), pltpu.VMEM((T, HEAD), F32)],
        semantics=("parallel",), jobs=jobs)


def _rnn_bwd(proj, hr, dz, gates, conv_w, w_a, w_x, lam, jobs=()):
    T = proj.shape[0]
    nchunk = T // CHUNK
    ntile = T // SUBLANES

    def body(u_ref, ug_ref, h_ref, dz_ref, v_ref, ra_ref, ix_ref, a_ref, sq_ref, cw_ref, wa_ref, wx_ref, lam_ref,
             du_ref, dug_ref, dwa_ref, dwx_ref, dba_ref, dbx_ref, dlam_ref, dcb_ref, dcw_ref,
             upad, hpad, apad, g_s, dvpad, ga_s):
        zero_tile = jnp.zeros((SUBLANES, HEAD), F32)
        _pad_front(upad, u_ref, SUBLANES)
        _pad_front(hpad, h_ref, SUBLANES)
        apad[pl.ds(T, SUBLANES), :] = zero_tile
        dvpad[pl.ds(T, SUBLANES), :] = zero_tile
        for ref in (dwa_ref, dwx_ref, dba_ref, dbx_ref, dlam_ref, dcb_ref, dcw_ref):
            ref[...] = jnp.zeros_like(ref)
        cw = _tap_rows(cw_ref)
        wa, wx = wa_ref[...].astype(BF16), wx_ref[...].astype(BF16)
        lam_row = lam_ref[...]
        sp = _softplus_neg(lam_row)

        def chunk(i, carry):
            rows = pl.ds(pl.multiple_of(i * CHUNK, CHUNK), CHUNK)
            apad[rows, :] = a_ref[rows, :]
            gl, dgl = _gelu_parts(ug_ref[rows, :])
            d = dz_ref[rows, :]
            g_s[rows, :] = d * gl
            dug_ref[rows, :] = (d * h_ref[rows, :] * dgl).astype(BF16)
            return carry

        lax.fori_loop(0, nchunk, chunk, 0)

        _scan_in_tiles(apad, 1, ga_s, g_s, T, -1)
        _carry_tiles(ga_s, g_s, g_s, ntile, -1)

        def chunk3(i, carry):
            r0 = pl.multiple_of(i * CHUNK, CHUNK)
            rows = pl.ds(r0, CHUNK)
            g = g_s[rows, :]
            h_prev = _shift_rows(hpad[pl.ds(r0, CHUNK + SUBLANES), :], 1)[SUBLANES:]
            v, ra, ix, sq, a = v_ref[rows, :], ra_ref[rows, :], ix_ref[rows, :], sq_ref[rows, :], a_ref[rows, :]
            d_sq = g * ix * v
            d_ix = g * sq * v
            d_la = a * g * h_prev - d_sq * a * a / sq
            dlam_ref[...] += jnp.sum(d_la * ra, axis=0, keepdims=True)
            d_pa = d_la * (-LRU_C) * sp * ra * (1.0 - ra)
            d_px = d_ix * ix * (1.0 - ix)
            vb, d_pab, d_pxb = v.astype(BF16), d_pa.astype(BF16), d_px.astype(BF16)
            dwa_ref[...] += _dot_tn(vb, d_pab)
            dwx_ref[...] += _dot_tn(vb, d_pxb)
            dba_ref[...] += jnp.sum(d_pa, axis=0, keepdims=True)
            dbx_ref[...] += jnp.sum(d_px, axis=0, keepdims=True)
            dv = g * sq * ix + _dot_nt(d_pab, wa) + _dot_nt(d_pxb, wx)
            dvpad[rows, :] = dv
            dcb_ref[...] += jnp.sum(dv, axis=0, keepdims=True)
            xs = upad[pl.ds(r0, CHUNK + SUBLANES), :]
            for k in range(CONV_WIDTH):
                u_k = _shift_rows(xs, CONV_WIDTH - 1 - k)[SUBLANES:] if k < CONV_WIDTH - 1 else xs[SUBLANES:]
                dcw_ref[k:k + 1, :] += jnp.sum(dv * u_k, axis=0, keepdims=True)
            return carry

        lax.fori_loop(0, nchunk, chunk3, 0)
        dlam_ref[...] = dlam_ref[...] * (LRU_C * _sigmoid(-lam_row))

        def chunk4(i, carry):
            r0 = pl.multiple_of(i * CHUNK, CHUNK)
            dvs = dvpad[pl.ds(r0, CHUNK + SUBLANES), :]
            du = cw[CONV_WIDTH - 1] * dvs[:CHUNK]
            for k in range(CONV_WIDTH - 1):
                du += cw[k] * _shift_rows(dvs, -(CONV_WIDTH - 1 - k))[:CHUNK]
            du_ref[pl.ds(r0, CHUNK), :] = du.astype(BF16)
            return carry

        lax.fori_loop(0, nchunk, chunk4, 0)

    col = pl.BlockSpec((T, HEAD), lambda h: (0, h))
    vec = pl.BlockSpec((1, HEAD), lambda h: (0, h))
    mat = pl.BlockSpec((None, HEAD, HEAD), lambda h: (h, 0, 0))
    taps = pl.BlockSpec((CONV_WIDTH, HEAD), lambda h: (0, h))
    vec_out = jax.ShapeDtypeStruct((1, D_RNN), F32)
    mat_out = jax.ShapeDtypeStruct((N_RNN_HEADS, HEAD, HEAD), F32)
    seq = pltpu.VMEM((T, HEAD), F32)
    seq_pad = pltpu.VMEM((T + SUBLANES, HEAD), F32)
    return _pallas(
        body, (proj, proj, hr, dz, *gates, conv_w, w_a, w_x, lam), name="rnn_bwd", grid=(N_RNN_HEADS,),
        in_specs=[pl.BlockSpec((T, HEAD), lambda h: (0, COL_RNN + h)), pl.BlockSpec((T, HEAD), lambda h: (0, COL_GATE + h))]
        + [col] * 7 + [taps, mat, mat, vec],
        out_specs=[col, col, mat, mat, vec, vec, vec, vec, taps],
        out_shape=[jax.ShapeDtypeStruct((T, D_RNN), BF16), jax.ShapeDtypeStruct((T, D_RNN), BF16), mat_out, mat_out,
                   vec_out, vec_out, vec_out, vec_out, jax.ShapeDtypeStruct((CONV_WIDTH, D_RNN), F32)],
        scratch_shapes=[seq_pad, seq_pad, seq_pad, seq, seq_pad, seq],
        semantics=("parallel",), jobs=jobs)


GROUP_FFN_OUT = ["w_ffn_out"]
GROUP_FFN_IN = ["w_ffn_in"]
GROUP_MIX = ["w_o", "w_pool_out", "w_rnn_out"]
GROUP_IN = ["w_in"]


def _step(x, target, s, full, conv_w, place):
    T = x.shape[0]
    tall, mid, low = min(T, 2048), min(T, 1024), min(T, 512)
    full = dict(full)

    def gathered(names, results):
        full.update(zip(names, results))

    early = ["w_pool_out", "w_rnn_out", "w_o", "w_ffn_out"]
    (proj, h1), (res,) = _norm_matmul(x, s["norm_mix"], full["w_in"], tm=tall, tn=512, name="in_proj", jobs=[_gather_job(full, early)])
    gathered(early, res)
    pm = _pool_fwd(proj, s["w_pool_grp"], s["pool_scale"])
    (hr, z, *gates), (res,) = _rnn_fwd(proj, conv_w, s["conv_b"], s["w_rg_a"], s["b_rg_a"], s["w_rg_x"], s["b_rg_x"],
                                       s["lru_lambda"], jobs=[_gather_job(full, ["w_ffn_in"])])
    gathered(["w_ffn_in"], res)
    *mix_by, mix = _branch_mix(pm, z, full["w_pool_out"], full["w_rnn_out"], proj, tm=tall, tn=256)
    x2 = _out_proj_residual(mix, full["w_o"], x, tm=mid)
    (act_by_up, act_by_gate, act, h2), _ = _ffn_in(x2, s["norm_ffn"], full["w_ffn_in"], tm=tall, tn=256)
    dx3, dx3b, sq_cols, g_norm_final = _ffn_out_loss(act, full["w_ffn_out"], x2, s["norm_final"], target, tm=low)

    g = {"norm_final": g_norm_final}

    def chip_sums(names, from_sibling):
        sums = {name: _chip_sum(name, g[name], got, place) for name, got in zip(names, from_sibling)}
        return {name: v[0] for name, v in sums.items()}, {name: v[1] for name, v in sums.items()}

    def final_sums(names, sums, from_chips):
        return {name: _final_sum(name, sums[name], got, place) for name, got in zip(names, from_chips)}

    dgate, dup = _ffn_out_bwd(dx3b, full["w_ffn_out"], act_by_gate, act_by_up, tm=tall, tn=256)
    gb = {}
    g["w_ffn_out"], gb["w_ffn_out"] = _weight_grad(act, [dx3b], tm=256, tn=D_MODEL, name="w_ffn_out_grad", also_bf16=True)
    (dx2, dx2b, g["norm_ffn"]), (res,) = _ffn_in_bwd(dgate, dup, full["w_ffn_in"], dx3, x2, s["norm_ffn"], tm=low,
                                                     jobs=[_sibling_job(gb, GROUP_FFN_OUT, BF16)])
    sums_ffn, sums_ffn_bf16 = chip_sums(GROUP_FFN_OUT, res)
    (g["w_ffn_in"], gb["w_ffn_in"]), (res,) = _weight_grad(h2, [dgate, dup], tm=D_MODEL, tn=256, name="w_ffn_in_grad", also_bf16=True,
                                                           jobs=[_chips_job(sums_ffn_bf16, GROUP_FFN_OUT)])
    shards_ffn = final_sums(GROUP_FFN_OUT, sums_ffn, res)
    (dgp, dgr, dyp, dyr), (res,) = _out_proj_bwd(dx2b, full["w_o"], mix_by, tm=tall, tn=256,
                                                 jobs=[_sibling_job(gb, GROUP_FFN_IN, BF16)])
    sums_ffn, sums_ffn_bf16 = chip_sums(GROUP_FFN_IN, res)
    g["w_o"], gb["w_o"] = _weight_grad(mix, [dx2b], tm=D_MODEL, tn=256, name="w_o_grad", also_bf16=True)
    dpm, dz = _branch_bwd(dyp, dyr, full["w_pool_out"], full["w_rnn_out"], tm=mid)
    g["w_pool_out"], gb["w_pool_out"] = _weight_grad(pm, [dyp], tm=D_POOL, tn=256, name="w_pool_out_grad", also_bf16=True)
    g["w_rnn_out"], gb["w_rnn_out"] = _weight_grad(z, [dyr], tm=D_RNN, tn=256, name="w_rnn_out_grad", also_bf16=True)
    (dupool, g["w_pool_grp"], g["pool_scale"]), _ = _pool_bwd(proj, dpm, s["w_pool_grp"], s["pool_scale"])
    ((durnn, dugate, g["w_rg_a"], g["w_rg_x"], g["b_rg_a"], g["b_rg_x"], g["lru_lambda"], g["conv_b"], g["conv_w"]),
     (res, from_sibling)) = _rnn_bwd(proj, hr, dz, gates, conv_w, s["w_rg_a"], s["w_rg_x"], s["lru_lambda"],
                                     jobs=[_chips_job(sums_ffn_bf16, GROUP_FFN_IN), _sibling_job(gb, GROUP_MIX, BF16)])
    shards_ffn.update(final_sums(GROUP_FFN_IN, sums_ffn, res))
    sums_mix, sums_mix_bf16 = chip_sums(GROUP_MIX, from_sibling)
    segs = [dupool, durnn, dugate, dgp, dgr]
    ffn = GROUP_FFN_OUT + GROUP_FFN_IN
    (g["w_in"], gb["w_in"]), (res, joined) = _weight_grad(
        h1, segs, tm=D_MODEL, tn=256, name="w_in_grad", also_bf16=True,
        jobs=[_chips_job(sums_mix_bf16, GROUP_MIX), _join_job(shards_ffn, ffn)])
    grads = dict(zip(ffn, joined))
    shards = final_sums(GROUP_MIX, sums_mix, res)
    (res,) = _run_jobs([_sibling_job(gb, GROUP_IN, BF16)], "w_in_exchange_sibling")
    sums_in, sums_in_bf16 = chip_sums(GROUP_IN, res)
    (grad_x, g["norm_mix"]), (res,) = _in_proj_bwd(segs, full["w_in"], dx2, x, s["norm_mix"], tm=low,
                                                  jobs=[_chips_job(sums_in_bf16, GROUP_IN)])
    shards.update(final_sums(GROUP_IN, sums_in, res))

    vec_rows = [g[name] if name != "pool_scale" else jnp.pad(g[name], ((0, 0), (0, D_MODEL - D_POOL))) for name in VEC_ITEMS]
    vec_rows += [g["conv_w"], sq_cols, jnp.zeros((VEC_ROWS - len(VEC_ITEMS) - CONV_WIDTH - 1, D_MODEL), F32)]
    vec = jnp.concatenate(vec_rows, axis=0).reshape(VEC_ROWS, N_DEV, HEAD).transpose(1, 0, 2)
    mat = jnp.concatenate([g[name].reshape(-1, HEAD) for name in MAT_ITEMS], axis=0).reshape(N_DEV, -1, HEAD)
    (vec, mat), (joined,) = _all_reduce_small([vec, mat], [False, True], jobs=[_join_job(shards, GROUP_MIX + GROUP_IN)])
    grads.update(zip(GROUP_MIX + GROUP_IN, joined))

    vec = vec.transpose(1, 0, 2).reshape(VEC_ROWS, D_MODEL)
    mat = mat.reshape(-1, HEAD)
    for k, name in enumerate(VEC_ITEMS):
        grads[name] = vec[k:k + 1, :s[name].shape[1]]
    grads["conv_w"] = vec[len(VEC_ITEMS):len(VEC_ITEMS) + CONV_WIDTH]
    row = 0
    for name in MAT_ITEMS:
        rows = s[name].shape[0] * HEAD
        grads[name] = mat[row:row + rows]
        row += rows
    return vec[len(VEC_ITEMS) + CONV_WIDTH], grad_x, grads


LARGE = {"w_in": "col", "w_pool_out": "col", "w_rnn_out": "row", "w_o": "row", "w_ffn_in": "col", "w_ffn_out": "row"}
LARGE_SHAPE = {"w_in": (D_MODEL, D_IN), "w_pool_out": (D_POOL, D_MODEL), "w_rnn_out": (D_RNN, D_MODEL),
               "w_o": (D_MODEL, D_MODEL), "w_ffn_in": (D_MODEL, 2 * D_FF), "w_ffn_out": (D_FF, D_MODEL)}


def _place():
    x, y, c = lax.axis_index("x"), lax.axis_index("y"), lax.axis_index("c")
    return 2 * x + y, c


def _chip_device(chip, c):
    return (chip // 2, chip % 2, c)


def _chip_window(ref, kind, shape, chip, half=None):
    K, N = shape
    if kind == "col":
        rows = slice(None) if half is None else pl.ds(half * (K // 2), K // 2)
        return ref.at[rows, pl.ds(chip * (N // N_CHIPS), N // N_CHIPS)]
    ks = K // N_CHIPS
    if half is None:
        return ref.at[pl.ds(chip * ks, ks), :]
    return ref.at[pl.ds(chip * ks + half * (ks // 2), ks // 2), :]


def _row_half(ref, half):
    rows = ref.shape[0] // 2
    return ref.at[pl.ds(half * rows, rows), :]


def _remote(win_src, win_dst, send_sems, recv_sems, idx, to):
    return pltpu.make_async_remote_copy(src_ref=win_src, dst_ref=win_dst, send_sem=send_sems.at[idx], recv_sem=recv_sems.at[idx],
                                        device_id=to, device_id_type=MESH)


def _gather_job(full, names, conv_w_full=None):
    n = len(names)
    cw_cols = D_RNN // N_CHIPS

    def windows(refs, chip, half):
        return [_chip_window(refs[k], LARGE[name], LARGE_SHAPE[name], chip, half) for k, name in enumerate(names)]

    def ici_copies(refs, send_sems, recv_sems, src_chip, dst_chip, c, r):
        wins = windows(refs, src_chip, c)
        if conv_w_full is not None:
            wins.append(refs[n].at[:, pl.ds(src_chip * cw_cols, cw_cols)])
        return [_remote(win, win, send_sems, recv_sems, (k, r), _chip_device(dst_chip, c)) for k, win in enumerate(wins)]

    def forwards(refs, send_sems, recv_sems, src_chip, half, to_core, chip, r):
        return [_remote(win, win, send_sems, recv_sems, (k, 3 + r), _chip_device(chip, to_core))
                for k, win in enumerate(windows(refs, src_chip, half))]

    def start(ins, outs, send_sems, recv_sems):
        chip, c = _place()
        for r in range(3):
            for cp in ici_copies(outs, send_sems, recv_sems, chip, chip ^ (r + 1), c, r):
                cp.start()

    def finish(ins, outs, send_sems, recv_sems):
        chip, c = _place()
        for r in range(3):
            for cp in ici_copies(outs, send_sems, recv_sems, chip ^ (r + 1), chip, c, r):
                cp.wait_recv()
            for cp in forwards(outs, send_sems, recv_sems, chip ^ (r + 1), c, 1 - c, chip, r):
                cp.start()
        for r in range(3):
            for cp in forwards(outs, send_sems, recv_sems, chip ^ (r + 1), 1 - c, c, chip, r):
                cp.wait_recv()
            for cp in ici_copies(outs, send_sems, recv_sems, chip, chip ^ (r + 1), c, r):
                cp.wait_send()
            for cp in forwards(outs, send_sems, recv_sems, chip ^ (r + 1), c, 1 - c, chip, r):
                cp.wait_send()

    arrays = [full[name] for name in names] + ([conv_w_full] if conv_w_full is not None else [])
    return _Job(arrays, [jax.ShapeDtypeStruct(a.shape, a.dtype) for a in arrays], {k: k for k in range(len(arrays))},
                (len(arrays), 6), start, finish)


def _core_halves(ref, kind, shape, c):
    return [_chip_window(ref, kind, shape, chip, c) for chip in range(N_CHIPS)]


def _sibling_job(grads, names, dtype=F32):
    def start(ins, outs, send_sems, recv_sems):
        chip, c = _place()
        for k, name in enumerate(names):
            kind, shape = LARGE[name], LARGE_SHAPE[name]
            if kind == "col":
                pairs = [(_row_half(ins[k], 1 - c), outs[k])]
            else:
                rows = shape[0] // N_DEV
                pairs = [(win, outs[k].at[pl.ds(j * rows, rows), :]) for j, win in enumerate(_core_halves(ins[k], kind, shape, 1 - c))]
            for src, dst in pairs:
                _remote(src, dst, send_sems, recv_sems, k, _chip_device(chip, 1 - c)).start()

    def finish(ins, outs, send_sems, recv_sems):
        chip, c = _place()
        for k in range(len(names)):
            _remote(outs[k], outs[k], send_sems, recv_sems, k, _chip_device(chip, 1 - c)).wait()

    return _Job([grads[name] for name in names],
                [jax.ShapeDtypeStruct((LARGE_SHAPE[name][0] // 2, LARGE_SHAPE[name][1]), dtype) for name in names], {},
                (len(names),), start, finish)


def _chip_sum(name, g, got, place):
    kind, (K, N) = LARGE[name], LARGE_SHAPE[name]
    rows = K // N_DEV
    piece_cols = N // N_CHIPS

    def body(place_ref, g_ref, got_ref, o_ref, ob_ref):
        total = g_ref[...] + got_ref[...].astype(F32)
        ob_ref[...] = total.astype(BF16)
        if kind == "col":
            for chip in range(N_CHIPS):
                @pl.when(place_ref[0] == chip)
                def _(chip=chip):
                    o_ref[...] = total[:, chip * piece_cols:(chip + 1) * piece_cols]
        else:
            @pl.when(pl.program_id(0) == place_ref[0])
            def _():
                o_ref[...] = total

    if kind == "col":
        mine = pl.BlockSpec((rows, N), lambda j, place_ref: (j + N_CHIPS * place_ref[1], 0))
        own = pl.BlockSpec((rows, piece_cols), lambda j, place_ref: (j, 0))
    else:
        mine = pl.BlockSpec((rows, N), lambda j, place_ref: (2 * j + place_ref[1], 0))
        own = pl.BlockSpec((rows, N), lambda j, place_ref: (0, 0))
    blk = pl.BlockSpec((rows, N), lambda j, place_ref: (j, 0))
    return pl.pallas_call(
        body, name=name + "_chip_sum",
        grid_spec=pltpu.PrefetchScalarGridSpec(num_scalar_prefetch=1, grid=(N_CHIPS,), in_specs=[mine, blk], out_specs=[own, blk]),
        out_shape=[jax.ShapeDtypeStruct(_piece_shape(name), F32), jax.ShapeDtypeStruct((K // 2, N), BF16)],
        compiler_params=_params(dimension_semantics=("arbitrary",)),
    )(place, g, got)


def _piece(ref, kind, shape, chip):
    K, N = shape
    if kind == "col":
        return ref.at[:, pl.ds(chip * (N // N_CHIPS), N // N_CHIPS)]
    return ref.at[pl.ds(chip * (K // N_DEV), K // N_DEV), :]


def _piece_shape(name):
    kind, (K, N) = LARGE[name], LARGE_SHAPE[name]
    return (K // 2, N // N_CHIPS) if kind == "col" else (K // N_DEV, N)


def _chips_job(sums, names):
    def copies(ins, outs, send_sems, recv_sems):
        chip, c = _place()
        return [_remote(_piece(ins[k], LARGE[name], LARGE_SHAPE[name], chip ^ (r + 1)), outs[k].at[r], send_sems, recv_sems, (k, r),
                        _chip_device(chip ^ (r + 1), c)) for k, name in enumerate(names) for r in range(3)]

    def start(*refs):
        for cp in copies(*refs):
            cp.start()

    def finish(*refs):
        for cp in copies(*refs):
            cp.wait()

    return _Job([sums[name] for name in names], [jax.ShapeDtypeStruct((3,) + _piece_shape(name), BF16) for name in names], {},
                (len(names), 3), start, finish)


def _final_sum(name, chip_sum, got, place):
    rows, cols = _piece_shape(name)

    def body(place_ref, s_ref, got_ref, o_ref):
        o_ref[...] = ((s_ref[...] + got_ref[0].astype(F32)) + got_ref[1].astype(F32)) + got_ref[2].astype(F32)

    mine = pl.BlockSpec((rows, cols), lambda i, place_ref: (0, 0))
    return pl.pallas_call(
        body, name=name + "_final_sum",
        grid_spec=pltpu.PrefetchScalarGridSpec(
            num_scalar_prefetch=1, grid=(1,), in_specs=[mine, pl.BlockSpec((3, rows, cols), lambda i, place_ref: (0, 0, 0))],
            out_specs=pl.BlockSpec((rows, cols), lambda i, place_ref: (place_ref[1], 0))),
        out_shape=jax.ShapeDtypeStruct((2 * rows, cols), F32),
        compiler_params=_params(dimension_semantics=("arbitrary",)),
    )(place, chip_sum, got)


def _join_job(shards, names):
    def half_copy(outs, send_sems, recv_sems, k, mine):
        chip, c = _place()
        win = _row_half(outs[k], c if mine else 1 - c)
        return _remote(win, win, send_sems, recv_sems, k, _chip_device(chip, 1 - c))

    def start(ins, outs, send_sems, recv_sems):
        for k in range(len(names)):
            half_copy(outs, send_sems, recv_sems, k, True).start()

    def finish(ins, outs, send_sems, recv_sems):
        for k in range(len(names)):
            half_copy(outs, send_sems, recv_sems, k, True).wait_send()
            half_copy(outs, send_sems, recv_sems, k, False).wait_recv()

    arrays = [shards[name] for name in names]
    return _Job(arrays, [jax.ShapeDtypeStruct(a.shape, F32) for a in arrays], {k: k for k in range(len(arrays))},
                (len(arrays),), start, finish)


VEC_ROWS = 16


def _all_reduce_small(slabs, narrow, jobs=()):
    n = len(slabs)
    narrowed = [k for k in range(n) if narrow[k]]

    def body(*refs):
        in_refs, out_refs, got_refs = refs[:n], refs[n:2 * n], refs[2 * n:3 * n]
        bf16_refs = dict(zip(narrowed, refs[3 * n:3 * n + len(narrowed)]))
        send_sems, recv_sems = refs[3 * n + len(narrowed):]
        x, y, c = lax.axis_index("x"), lax.axis_index("y"), lax.axis_index("c")
        me = 4 * x + 2 * y + c
        for k, ref in bf16_refs.items():
            ref[...] = in_refs[k][...].astype(BF16)
        partial_refs = [bf16_refs.get(k, in_refs[k]) for k in range(n)]

        def remote(src, dst, k, phase, r):
            other = me ^ r
            return pltpu.make_async_remote_copy(src_ref=src, dst_ref=dst, send_sem=send_sems.at[k, phase, r],
                                                recv_sem=recv_sems.at[k, phase, r],
                                                device_id=(other // 4, (other // 2) % 2, other % 2), device_id_type=MESH)

        scatter = [remote(partial_refs[k].at[me ^ r], got_refs[k].at[r], k, 0, r) for r in range(1, N_DEV) for k in range(n)]
        for cp in scatter:
            cp.start()
        for cp in scatter:
            cp.wait()
        for k in range(n):
            total = in_refs[k][me]
            for r in range(1, N_DEV):
                total = total + got_refs[k][r].astype(F32)
            out_refs[k][me] = total
        gather = [remote(out_refs[k].at[me], out_refs[k].at[me], k, 1, r) for r in range(1, N_DEV) for k in range(n)]
        for cp in gather:
            cp.start()
        for r in range(1, N_DEV):
            for k in range(n):
                remote(out_refs[k].at[me ^ r], out_refs[k].at[me ^ r], k, 1, r).wait_recv()
        for cp in gather:
            cp.wait_send()

    return _pallas(
        body, slabs, name="all_reduce_small", grid=(), in_specs=[VMEM] * n, out_specs=[VMEM] * n,
        out_shape=[jax.ShapeDtypeStruct(s.shape, F32) for s in slabs],
        scratch_shapes=[pltpu.VMEM(s.shape, BF16 if narrow[k] else F32) for k, s in enumerate(slabs)]
        + [pltpu.VMEM(slabs[k].shape, BF16) for k in narrowed]
        + [pltpu.SemaphoreType.DMA((n, 2, N_DEV)), pltpu.SemaphoreType.DMA((n, 2, N_DEV))], jobs=jobs)


def _cast_into_whole(w, name, place):
    rows, cols = w.shape
    tr = rows // 2

    def body(place_ref, w_ref, o_ref):
        o_ref[...] = w_ref[...].astype(BF16)

    if LARGE[name] == "col":
        window = pl.BlockSpec((tr, cols), lambda i, place_ref: (i, place_ref[0]))
    else:
        window = pl.BlockSpec((tr, cols), lambda i, place_ref: (2 * place_ref[0] + i, 0))
    return pl.pallas_call(
        body, name=name + "_cast",
        grid_spec=pltpu.PrefetchScalarGridSpec(num_scalar_prefetch=1, grid=(2,),
                                               in_specs=[pl.BlockSpec((tr, cols), lambda i, place_ref: (i, 0))], out_specs=window),
        out_shape=jax.ShapeDtypeStruct(LARGE_SHAPE[name], BF16),
        compiler_params=_params(dimension_semantics=("parallel",)))(place, w)


def _cast_many_into_whole(shards, place, jobs):
    names = list(shards)
    n = len(names)

    def body(place_ref, *refs):
        for w_ref, o_ref in zip(refs[:n], refs[n:]):
            o_ref[...] = w_ref[...].astype(BF16)

    def window(name):
        rows, cols = shards[name].shape
        if LARGE[name] == "col":
            return pl.BlockSpec((rows // 2, cols), lambda i, place_ref: (i, place_ref[0]))
        return pl.BlockSpec((rows // 2, cols), lambda i, place_ref: (2 * place_ref[0] + i, 0))

    def half(name):
        rows, cols = shards[name].shape
        return pl.BlockSpec((rows // 2, cols), lambda i, place_ref: (i, 0))

    return _pallas(body, [shards[name] for name in names], name="cast_weights", grid=(2,),
                   in_specs=[half(name) for name in names], out_specs=[window(name) for name in names],
                   out_shape=[jax.ShapeDtypeStruct(LARGE_SHAPE[name], BF16) for name in names],
                   semantics=("arbitrary",), jobs=jobs, prefetch=place)


def _adamw_math(w, g, m, v):
    m = ADAM_B1 * m + (1.0 - ADAM_B1) * g
    v = ADAM_B2 * v + (1.0 - ADAM_B2) * (g * g)
    m_hat = m / (1.0 - ADAM_B1 ** ADAM_STEP)
    v_hat = v / (1.0 - ADAM_B2 ** ADAM_STEP)
    delta = -ADAM_LR * (m_hat / (jnp.sqrt(v_hat) + ADAM_EPS) + ADAM_WD * w)
    return delta, m, v


def _adamw_large(w, g, m, v, name):
    rows, cols = w.shape
    steps = 2
    tr = rows // steps

    def body(w_ref, g_ref, m_ref, v_ref, d_ref, mo_ref, vo_ref):
        d_ref[...], mo_ref[...], vo_ref[...] = _adamw_math(w_ref[...], g_ref[...], m_ref[...], v_ref[...])

    blk = pl.BlockSpec((tr, cols), lambda i: (i, 0))
    out = jax.ShapeDtypeStruct(w.shape, F32)
    return pl.pallas_call(body, name=name + "_adamw", grid=(steps,), in_specs=[blk] * 4, out_specs=[blk] * 3, out_shape=[out] * 3,
                          compiler_params=_params(dimension_semantics=("parallel",)))(w, g, m, v)


def _adamw_small(ws, gs, ms, vs):
    n = len(ws)

    def body(*refs):
        for k in range(n):
            w_ref, g_ref, m_ref, v_ref = (refs[q * n + k] for q in range(4))
            d_ref, mo_ref, vo_ref = (refs[(4 + q) * n + k] for q in range(3))
            d_ref[...], mo_ref[...], vo_ref[...] = _adamw_math(w_ref[...], g_ref[...], m_ref[...], v_ref[...])

    out = [jax.ShapeDtypeStruct(w.shape, F32) for w in ws]
    res = pl.pallas_call(body, name="small_adamw", in_specs=[VMEM] * (4 * n), out_specs=[VMEM] * (3 * n), out_shape=out * 3,
                         compiler_params=_params())(*ws, *gs, *ms, *vs)
    return res[:n], res[n:2 * n], res[2 * n:]


WEIGHTS = ["norm_mix", "w_in", "w_pool_grp", "pool_scale", "w_pool_out", "conv_w", "conv_b", "w_rg_a", "b_rg_a", "w_rg_x",
           "b_rg_x", "lru_lambda", "w_rnn_out", "w_o", "norm_ffn", "w_ffn_in", "w_ffn_out", "norm_final"]
VEC_ITEMS = ["norm_mix", "norm_ffn", "norm_final", "pool_scale", "conv_b", "lru_lambda", "b_rg_a", "b_rg_x"]
MAT_ITEMS = ["w_pool_grp", "w_rg_a", "w_rg_x"]


def _as2d(name, a):
    if name in MAT_ITEMS:
        return a.reshape(-1, HEAD, HEAD)
    if name == "conv_w":
        return a.reshape(CONV_WIDTH, -1)
    return a.reshape(1, -1)


def kernel(x, norm_mix, w_in, w_pool_grp, pool_scale, w_pool_out, conv_w, conv_b, w_rg_a, b_rg_a, w_rg_x, b_rg_x, lru_lambda, w_rnn_out, w_o, norm_ffn, w_ffn_in, w_ffn_out, norm_final, loss_target, m_norm_mix, m_w_in, m_w_pool_grp, m_pool_scale, m_w_pool_out, m_conv_w, m_conv_b, m_w_rg_a, m_b_rg_a, m_w_rg_x, m_b_rg_x, m_lru_lambda, m_w_rnn_out, m_w_o, m_norm_ffn, m_w_ffn_in, m_w_ffn_out, m_norm_final, v_norm_mix, v_w_in, v_w_pool_grp, v_pool_scale, v_w_pool_out, v_conv_w, v_conv_b, v_w_rg_a, v_b_rg_a, v_w_rg_x, v_b_rg_x, v_lru_lambda, v_w_rnn_out, v_w_o, v_norm_ffn, v_w_ffn_in, v_w_ffn_out, v_norm_final):
    given = dict(locals())
    w = {name: given[name] for name in WEIGHTS}
    m = {name: given["m_" + name] for name in WEIGHTS}
    v = {name: given["v_" + name] for name in WEIGHTS}
    chip, c = _place()

    place = jnp.stack([chip, c]).astype(jnp.int32)
    conv_cols = w["conv_w"].shape[-1]
    conv_w_mine = lax.dynamic_update_slice_in_dim(jnp.zeros((CONV_WIDTH, D_RNN), F32), w["conv_w"][0], chip * conv_cols, axis=1)
    w_in_mine = _cast_into_whole(w["w_in"][0], "w_in", place)
    later = [name for name in LARGE if name != "w_in"]
    casts, ((w_in_full, conv_w_full),) = _cast_many_into_whole(
        {name: w[name][0] for name in later}, place, jobs=[_gather_job({"w_in": w_in_mine}, ["w_in"], conv_w_mine)])
    full = dict(zip(later, casts), w_in=w_in_full)
    small = {name: _as2d(name, w[name]) for name in WEIGHTS if name not in LARGE and name != "conv_w"}
    sq_cols, grad_x, grads = _step(x[0], loss_target[0], small, full, conv_w_full, place)
    loss = 0.5 / D_MODEL * jnp.sum(sq_cols)
    grads["conv_w"] = lax.dynamic_slice_in_dim(grads["conv_w"], chip * conv_cols, conv_cols, axis=1)

    delta, new_m, new_v = {}, {}, {}
    for name in LARGE:
        delta[name], new_m[name], new_v[name] = _adamw_large(w[name][0], grads[name], m[name][0], v[name][0], name)
    small_names = [name for name in WEIGHTS if name not in LARGE]
    flat = lambda d: [d[name].reshape(grads[name].shape) for name in small_names]
    ds, mo, vo = _adamw_small(flat(w), [grads[name] for name in small_names], flat(m), flat(v))
    for k, name in enumerate(small_names):
        delta[name], new_m[name], new_v[name] = ds[k], mo[k], vo[k]

    shaped = lambda d: [d[name].reshape(w[name].shape) for name in WEIGHTS]
    return (loss, grad_x[None], *shaped(grads), *shaped(delta), *shaped(new_m), *shaped(new_v))
```

```python
import functools
import math

import jax
import jax.numpy as jnp
from jax import lax
from jax.experimental import pallas as pl
from jax.experimental.pallas import tpu as pltpu

F32 = jnp.float32
BF16 = jnp.bfloat16

D_MODEL = 1024
D_POOL = 512
N_POOL_GROUPS = 4
D_RNN = 1024
N_RNN_HEADS = 8
HEAD = 128
CONV_WIDTH = 4
LRU_C = 8.0
D_FF = 2816
D_IN = D_POOL + 2 * D_RNN + 2 * D_MODEL
NORM_EPS = 1e-6
COL_RNN = D_POOL // HEAD
COL_GATE = (D_POOL + D_RNN) // HEAD

ADAM_LR = 0.001
ADAM_B1 = 0.9
ADAM_B2 = 0.999
ADAM_EPS = 1e-08
ADAM_WD = 0.01
ADAM_STEP = 10

N_CHIPS = 4
N_DEV = 8
MESH = pl.DeviceIdType.MESH
ANY = pl.BlockSpec(memory_space=pl.ANY)
VMEM = pl.BlockSpec(memory_space=pltpu.VMEM)
VMEM_LIMIT_BYTES = 60 * 1024 * 1024
SUBLANES = 8
POOL_HALO = 16
CHUNK = 1024

GELU_C = math.sqrt(2.0 / math.pi)
GELU_A = 0.044715


def _params(**kw):
    return pltpu.CompilerParams(vmem_limit_bytes=VMEM_LIMIT_BYTES, **kw)


def _sigmoid(x):
    return 0.5 * jnp.tanh(0.5 * x) + 0.5


def _log1p(y):
    u = 1.0 + y
    d = u - 1.0
    return jnp.where(d == 0.0, y, jnp.log(u) * (y / jnp.where(d == 0.0, 1.0, d)))


def _gelu_parts(x):
    x2 = x * x
    th = jnp.tanh(GELU_C * (x + GELU_A * x * x2))
    g = 0.5 * x * (1.0 + th)
    dg = 0.5 * (1.0 + th) + 0.5 * x * (1.0 - th * th) * GELU_C * (1.0 + 3.0 * GELU_A * x2)
    return g, dg


def _dot(a, b):
    return jnp.dot(a, b, preferred_element_type=F32)


def _dot_nt(a, b):
    return lax.dot_general(a, b, (((1,), (1,)), ((), ())), preferred_element_type=F32)


def _dot_tn(a, b):
    return lax.dot_general(a, b, (((0,), (0,)), ((), ())), preferred_element_type=F32)


def _rms_scale(xv):
    return lax.rsqrt(jnp.mean(xv * xv, axis=-1, keepdims=True) + NORM_EPS)


def _rms_bwd(dy, xv, g):
    r = _rms_scale(xv)
    xh = xv * r
    dyg = dy * g
    dx = r * (dyg - xh * jnp.mean(dyg * xh, axis=-1, keepdims=True))
    return dx, dy * xh


class _Job:
    def __init__(self, inputs, out_shapes, aliases, sem_shape, start, finish):
        self.inputs, self.out_shapes, self.aliases, self.sem_shape = list(inputs), list(out_shapes), dict(aliases), sem_shape
        self.start, self.finish = start, finish


def _pallas(body, operands, *, name, grid, in_specs, out_specs, out_shape, scratch_shapes=(), semantics=None, jobs=(),
            prefetch=None):
    n_in, n_out, n_scr = len(in_specs), len(out_specs), len(scratch_shapes)
    n_pre = 0 if prefetch is None else 1
    job_in = [a for job in jobs for a in job.inputs]
    job_out = [s for job in jobs for s in job.out_shapes]
    aliases, i0, o0 = {}, n_pre + n_in, n_out
    for job in jobs:
        aliases.update({i0 + i: o0 + o for i, o in job.aliases.items()})
        i0, o0 = i0 + len(job.inputs), o0 + len(job.out_shapes)

    def whole(*refs):
        pre, refs = refs[:n_pre], refs[n_pre:]
        ins, j_ins = refs[:n_in], refs[n_in:n_in + len(job_in)]
        outs = refs[n_in + len(job_in):][:n_out]
        j_outs = refs[n_in + len(job_in) + n_out:][:len(job_out)]
        rest = refs[n_in + len(job_in) + n_out + len(job_out):]
        scr, sems = rest[:n_scr], rest[n_scr:]

        def run(phase):
            i, o = 0, 0
            for k, job in enumerate(jobs):
                getattr(job, phase)(j_ins[i:i + len(job.inputs)], j_outs[o:o + len(job.out_shapes)], sems[2 * k], sems[2 * k + 1])
                i, o = i + len(job.inputs), o + len(job.out_shapes)

        def at(step_of, phase):
            if not jobs:
                return
            if not grid:
                run(phase)
                return
            cond = functools.reduce(jnp.logical_and, [pl.program_id(d) == step_of(d) for d in range(len(grid))])
            pl.when(cond)(functools.partial(run, phase))

        at(lambda d: 0, "start")
        body(*pre, *ins, *outs, *scr)
        at(lambda d: grid[d] - 1, "finish")

    layout = dict(grid=grid, in_specs=list(in_specs) + [ANY] * len(job_in), out_specs=list(out_specs) + [ANY] * len(job_out),
                  scratch_shapes=list(scratch_shapes) + [pltpu.SemaphoreType.DMA(job.sem_shape) for job in jobs for _ in range(2)])
    if prefetch is not None:
        layout = dict(grid_spec=pltpu.PrefetchScalarGridSpec(num_scalar_prefetch=1, **layout))
    res = pl.pallas_call(
        whole, name=name, out_shape=list(out_shape) + job_out, input_output_aliases=aliases,
        compiler_params=_params(dimension_semantics=semantics, has_side_effects=bool(jobs)), **layout,
    )(*([] if prefetch is None else [prefetch]), *operands, *job_in)
    per_job, o = [], n_out
    for job in jobs:
        per_job.append(res[o:o + len(job.out_shapes)])
        o += len(job.out_shapes)
    return res[:n_out], per_job


def _run_jobs(jobs, name):
    return _pallas(lambda: None, [], name=name, grid=(), in_specs=[], out_specs=[], out_shape=[], jobs=jobs)[1]


NORM_ROWS = 256
EPILOGUE_ROWS = 512


def _norm_rows(x_ref, g_ref, h_ref):
    g = g_ref[...]

    def rows(i, carry):
        r = pl.ds(pl.multiple_of(i * NORM_ROWS, NORM_ROWS), NORM_ROWS)
        xv = x_ref[r, :]
        h_ref[r, :] = (xv * _rms_scale(xv) * g).astype(BF16)
        return carry

    lax.fori_loop(0, x_ref.shape[0] // NORM_ROWS, rows, 0)


def _norm_matmul(x, g, w, *, tm, tn, name, jobs=()):
    T, K = x.shape
    N = w.shape[1]

    def body(x_ref, g_ref, w_ref, o_ref, h_ref):
        @pl.when(pl.program_id(1) == 0)
        def _():
            _norm_rows(x_ref, g_ref, h_ref)

        o_ref[...] = _dot(h_ref[...], w_ref[...])

    return _pallas(
        body, (x, g, w), name=name, grid=(T // tm, N // tn),
        in_specs=[pl.BlockSpec((tm, K), lambda i, j: (i, 0)), pl.BlockSpec((1, K), lambda i, j: (0, 0)),
                  pl.BlockSpec((K, tn), lambda i, j: (0, j))],
        out_specs=[pl.BlockSpec((tm, tn), lambda i, j: (i, j)), pl.BlockSpec((tm, K), lambda i, j: (i, 0))],
        out_shape=[jax.ShapeDtypeStruct((T, N), F32), jax.ShapeDtypeStruct((T, K), BF16)],
        semantics=("parallel", "arbitrary"), jobs=jobs)


def _ffn_in(x2, g, w, *, tm, tn, jobs=()):
    T, K = x2.shape
    nb = D_FF // tn

    def body(x_ref, g_ref, wg_ref, wu_ref, dup_ref, dgate_ref, act_ref, h_ref):
        @pl.when(pl.program_id(1) == 0)
        def _():
            _norm_rows(x_ref, g_ref, h_ref)

        wg, wu = wg_ref[...], wu_ref[...]
        for r in range(0, tm, EPILOGUE_ROWS):
            rows = pl.ds(r, min(EPILOGUE_ROWS, tm))
            h = h_ref[rows, :]
            gate, up = _dot(h, wg), _dot(h, wu)
            s = _sigmoid(gate)
            silu = gate * s
            dup_ref[rows, :] = silu.astype(BF16)
            dgate_ref[rows, :] = (up * (s + silu * (1.0 - s))).astype(BF16)
            act_ref[rows, :] = (silu * up).astype(BF16)

    blk = pl.BlockSpec((tm, tn), lambda i, j: (i, j))
    return _pallas(
        body, (x2, g, w, w), name="ffn_in", grid=(T // tm, nb),
        in_specs=[pl.BlockSpec((tm, K), lambda i, j: (i, 0)), pl.BlockSpec((1, K), lambda i, j: (0, 0)),
                  pl.BlockSpec((K, tn), lambda i, j: (0, j)), pl.BlockSpec((K, tn), lambda i, j: (0, j + nb))],
        out_specs=[blk, blk, blk, pl.BlockSpec((tm, K), lambda i, j: (i, 0))],
        out_shape=[jax.ShapeDtypeStruct((T, D_FF), BF16), jax.ShapeDtypeStruct((T, D_FF), BF16),
                   jax.ShapeDtypeStruct((T, D_FF), BF16), jax.ShapeDtypeStruct((T, K), BF16)],
        semantics=("parallel", "arbitrary"), jobs=jobs)


def _branch_mix(pm, z, w_pool_out, w_rnn_out, proj, *, tm, tn):
    T = pm.shape[0]
    col_gp = (D_POOL + 2 * D_RNN) // tn
    col_gr = col_gp + D_MODEL // tn

    def body(pm_ref, z_ref, wp_ref, wr_ref, gp_ref, gr_ref, by_gp_ref, by_gr_ref, sp_ref, sr_ref, mix_ref):
        wp, wr = wp_ref[...], wr_ref[...]
        for r in range(0, tm, EPILOGUE_ROWS):
            rows = pl.ds(r, min(EPILOGUE_ROWS, tm))
            yp, yr = _dot(pm_ref[rows, :], wp), _dot(z_ref[rows, :], wr)
            sp, sr = _sigmoid(gp_ref[rows, :]), _sigmoid(gr_ref[rows, :])
            by_gp_ref[rows, :] = (yp * sp * (1.0 - sp)).astype(BF16)
            by_gr_ref[rows, :] = (yr * sr * (1.0 - sr)).astype(BF16)
            sp_ref[rows, :] = sp.astype(BF16)
            sr_ref[rows, :] = sr.astype(BF16)
            mix_ref[rows, :] = (sp * yp + sr * yr).astype(BF16)

    blk = pl.BlockSpec((tm, tn), lambda i, j: (i, j))
    out = jax.ShapeDtypeStruct((T, D_MODEL), BF16)
    return pl.pallas_call(
        body, name="branch_mix", grid=(T // tm, D_MODEL // tn),
        in_specs=[pl.BlockSpec((tm, D_POOL), lambda i, j: (i, 0)), pl.BlockSpec((tm, D_RNN), lambda i, j: (i, 0)),
                  pl.BlockSpec((D_POOL, tn), lambda i, j: (0, j)), pl.BlockSpec((D_RNN, tn), lambda i, j: (0, j)),
                  pl.BlockSpec((tm, tn), lambda i, j: (i, col_gp + j)), pl.BlockSpec((tm, tn), lambda i, j: (i, col_gr + j))],
        out_specs=[blk] * 5, out_shape=[out] * 5,
        compiler_params=_params(dimension_semantics=("parallel", "parallel")),
    )(pm, z, w_pool_out, w_rnn_out, proj, proj)


def _out_proj_residual(mix, w_o, x, *, tm):
    T = x.shape[0]

    def body(mix_ref, w_ref, x_ref, o_ref):
        o_ref[...] = x_ref[...] + _dot(mix_ref[...], w_ref[...])

    row = pl.BlockSpec((tm, D_MODEL), lambda i: (i, 0))
    return pl.pallas_call(
        body, name="out_proj_residual", grid=(T // tm,),
        in_specs=[row, pl.BlockSpec((D_MODEL, D_MODEL), lambda i: (0, 0)), row],
        out_specs=row, out_shape=jax.ShapeDtypeStruct((T, D_MODEL), F32),
        compiler_params=_params(dimension_semantics=("parallel",)),
    )(mix, w_o, x)


def _ffn_out_loss(act, w, x2, g3, target, *, tm):
    T = x2.shape[0]

    def body(act_ref, w_ref, x2_ref, g_ref, t_ref, dx_ref, dxb_ref, sq_ref, dg_ref):
        @pl.when(pl.program_id(0) == 0)
        def _():
            sq_ref[...] = jnp.zeros_like(sq_ref)
            dg_ref[...] = jnp.zeros_like(dg_ref)

        g, w = g_ref[...], w_ref[...]
        for r in range(0, tm, NORM_ROWS):
            rows = pl.ds(r, min(NORM_ROWS, tm))
            x3 = x2_ref[rows, :] + _dot(act_ref[rows, :], w)
            err = x3 * _rms_scale(x3) * g - t_ref[rows, :]
            sq_ref[...] += jnp.sum(err * err, axis=0, keepdims=True)
            dx, dgp = _rms_bwd(err * (1.0 / D_MODEL), x3, g)
            dg_ref[...] += jnp.sum(dgp, axis=0, keepdims=True)
            dx_ref[rows, :] = dx
            dxb_ref[rows, :] = dx.astype(BF16)

    row = pl.BlockSpec((tm, D_MODEL), lambda i: (i, 0))
    vec = pl.BlockSpec((1, D_MODEL), lambda i: (0, 0))
    return pl.pallas_call(
        body, name="ffn_out_loss", grid=(T // tm,),
        in_specs=[pl.BlockSpec((tm, D_FF), lambda i: (i, 0)), pl.BlockSpec((D_FF, D_MODEL), lambda i: (0, 0)), row, vec, row],
        out_specs=[row, row, vec, vec],
        out_shape=[jax.ShapeDtypeStruct((T, D_MODEL), F32), jax.ShapeDtypeStruct((T, D_MODEL), BF16),
                   jax.ShapeDtypeStruct((1, D_MODEL), F32), jax.ShapeDtypeStruct((1, D_MODEL), F32)],
        compiler_params=_params(dimension_semantics=("arbitrary",)),
    )(act, w, x2, g3, target)


def _ffn_out_bwd(dx3b, w, act_by_gate, act_by_up, *, tm, tn):
    T = dx3b.shape[0]

    def body(dx_ref, w_ref, by_gate_ref, by_up_ref, dgate_ref, dup_ref):
        w = w_ref[...]
        for r in range(0, tm, EPILOGUE_ROWS):
            rows = pl.ds(r, min(EPILOGUE_ROWS, tm))
            dact = _dot_nt(dx_ref[rows, :], w)
            dgate_ref[rows, :] = (dact * by_gate_ref[rows, :].astype(F32)).astype(BF16)
            dup_ref[rows, :] = (dact * by_up_ref[rows, :].astype(F32)).astype(BF16)

    blk = pl.BlockSpec((tm, tn), lambda i, j: (i, j))
    return pl.pallas_call(
        body, name="ffn_out_bwd", grid=(T // tm, D_FF // tn),
        in_specs=[pl.BlockSpec((tm, D_MODEL), lambda i, j: (i, 0)), pl.BlockSpec((tn, D_MODEL), lambda i, j: (j, 0)), blk, blk],
        out_specs=[blk, blk],
        out_shape=[jax.ShapeDtypeStruct((T, D_FF), BF16), jax.ShapeDtypeStruct((T, D_FF), BF16)],
        compiler_params=_params(dimension_semantics=("parallel", "parallel")),
    )(dx3b, w, act_by_gate, act_by_up)


def _ffn_in_bwd(dgate, dup, w, dx3, x2, g2, *, tm, jobs=()):
    T = x2.shape[0]

    def body(dgate_ref, dup_ref, w_ref, dx3_ref, x2_ref, g_ref, dx_ref, dxb_ref, dg_ref):
        @pl.when(pl.program_id(0) == 0)
        def _():
            dg_ref[...] = jnp.zeros_like(dg_ref)

        g = g_ref[...]
        for r in range(0, tm, NORM_ROWS):
            rows = pl.ds(r, min(NORM_ROWS, tm))
            dh = _dot_nt(dgate_ref[rows, :], w_ref[:, :D_FF]) + _dot_nt(dup_ref[rows, :], w_ref[:, D_FF:])
            dxn, dgp = _rms_bwd(dh, x2_ref[rows, :], g)
            dx = dx3_ref[rows, :] + dxn
            dg_ref[...] += jnp.sum(dgp, axis=0, keepdims=True)
            dx_ref[rows, :] = dx
            dxb_ref[rows, :] = dx.astype(BF16)

    row = pl.BlockSpec((tm, D_MODEL), lambda i: (i, 0))
    wide = pl.BlockSpec((tm, D_FF), lambda i: (i, 0))
    vec = pl.BlockSpec((1, D_MODEL), lambda i: (0, 0))
    return _pallas(
        body, (dgate, dup, w, dx3, x2, g2), name="ffn_in_bwd", grid=(T // tm,),
        in_specs=[wide, wide, pl.BlockSpec((D_MODEL, 2 * D_FF), lambda i: (0, 0)), row, row, vec],
        out_specs=[row, row, vec],
        out_shape=[jax.ShapeDtypeStruct((T, D_MODEL), F32), jax.ShapeDtypeStruct((T, D_MODEL), BF16),
                   jax.ShapeDtypeStruct((1, D_MODEL), F32)],
        semantics=("arbitrary",), jobs=jobs)


def _out_proj_bwd(dx2b, w_o, mix_by, *, tm, tn, jobs=()):
    T = dx2b.shape[0]

    def body(dx_ref, w_ref, *refs):
        w = w_ref[...]
        for r in range(0, tm, EPILOGUE_ROWS):
            rows = pl.ds(r, min(EPILOGUE_ROWS, tm))
            dmix = _dot_nt(dx_ref[rows, :], w)
            for by_ref, d_ref in zip(refs[:4], refs[4:]):
                d_ref[rows, :] = (dmix * by_ref[rows, :].astype(F32)).astype(BF16)

    blk = pl.BlockSpec((tm, tn), lambda i, j: (i, j))
    out = jax.ShapeDtypeStruct((T, D_MODEL), BF16)
    return _pallas(
        body, (dx2b, w_o, *mix_by), name="out_proj_bwd", grid=(T // tm, D_MODEL // tn),
        in_specs=[pl.BlockSpec((tm, D_MODEL), lambda i, j: (i, 0)), pl.BlockSpec((tn, D_MODEL), lambda i, j: (j, 0))] + [blk] * 4,
        out_specs=[blk] * 4, out_shape=[out] * 4, semantics=("parallel", "parallel"), jobs=jobs)


def _branch_bwd(dyp, dyr, w_pool_out, w_rnn_out, *, tm):
    T = dyp.shape[0]

    def body(dyp_ref, dyr_ref, wp_ref, wr_ref, dpm_ref, dz_ref):
        dpm_ref[...] = _dot_nt(dyp_ref[...], wp_ref[...])
        dz_ref[...] = _dot_nt(dyr_ref[...], wr_ref[...])

    row = pl.BlockSpec((tm, D_MODEL), lambda i: (i, 0))
    return pl.pallas_call(
        body, name="branch_bwd", grid=(T // tm,),
        in_specs=[row, row, pl.BlockSpec((D_POOL, D_MODEL), lambda i: (0, 0)), pl.BlockSpec((D_RNN, D_MODEL), lambda i: (0, 0))],
        out_specs=[pl.BlockSpec((tm, D_POOL), lambda i: (i, 0)), pl.BlockSpec((tm, D_RNN), lambda i: (i, 0))],
        out_shape=[jax.ShapeDtypeStruct((T, D_POOL), F32), jax.ShapeDtypeStruct((T, D_RNN), F32)],
        compiler_params=_params(dimension_semantics=("parallel",)),
    )(dyp, dyr, w_pool_out, w_rnn_out)


def _in_proj_bwd(segs, w, dx2, x, g1, *, tm, jobs=()):
    T = x.shape[0]
    widths = [s.shape[1] for s in segs]
    offs = [sum(widths[:k]) for k in range(len(widths))]
    n = len(segs)

    def body(*refs):
        seg_refs, (w_ref, dx2_ref, x_ref, g_ref, dx_ref, dg_ref) = refs[:n], refs[n:]

        @pl.when(pl.program_id(0) == 0)
        def _():
            dg_ref[...] = jnp.zeros_like(dg_ref)

        g = g_ref[...]
        for r in range(0, tm, NORM_ROWS):
            rows = pl.ds(r, min(NORM_ROWS, tm))
            dh = _dot_nt(seg_refs[0][rows, :], w_ref[:, offs[0]:offs[0] + widths[0]])
            for k in range(1, n):
                dh += _dot_nt(seg_refs[k][rows, :], w_ref[:, offs[k]:offs[k] + widths[k]])
            dxn, dgp = _rms_bwd(dh, x_ref[rows, :], g)
            dg_ref[...] += jnp.sum(dgp, axis=0, keepdims=True)
            dx_ref[rows, :] = dx2_ref[rows, :] + dxn

    row = pl.BlockSpec((tm, D_MODEL), lambda i: (i, 0))
    vec = pl.BlockSpec((1, D_MODEL), lambda i: (0, 0))
    return _pallas(
        body, (*segs, w, dx2, x, g1), name="in_proj_bwd", grid=(T // tm,),
        in_specs=[pl.BlockSpec((tm, wd), lambda i: (i, 0)) for wd in widths]
        + [pl.BlockSpec((D_MODEL, D_IN), lambda i: (0, 0)), row, row, vec],
        out_specs=[row, vec],
        out_shape=[jax.ShapeDtypeStruct((T, D_MODEL), F32), jax.ShapeDtypeStruct((1, D_MODEL), F32)],
        semantics=("arbitrary",), jobs=jobs)


def _weight_grad(a, segs, *, tm, tn, name, jobs=None, also_bf16=False):
    T, M = a.shape
    nblk = [s.shape[1] // tn for s in segs]
    first = [sum(nblk[:k]) for k in range(len(segs))]
    n = len(segs)

    def body(a_ref, *refs):
        seg_refs, o_refs = refs[:n], refs[n:]
        j = pl.program_id(1)
        for k in range(n):
            @pl.when((j >= first[k]) & (j < first[k] + nblk[k]))
            def _(k=k):
                grad = _dot_tn(a_ref[...], seg_refs[k][...])
                for o_ref in o_refs:
                    o_ref[...] = grad.astype(o_ref.dtype)

    def seg_spec(k):
        return pl.BlockSpec((T, tn), lambda i, j: (0, jnp.clip(j - first[k], 0, nblk[k] - 1)))

    dtypes = [F32, BF16] if also_bf16 else [F32]
    grad, results = _pallas(
        body, (a, *segs), name=name, grid=(M // tm, sum(nblk)),
        in_specs=[pl.BlockSpec((T, tm), lambda i, j: (0, i))] + [seg_spec(k) for k in range(n)],
        out_specs=[pl.BlockSpec((tm, tn), lambda i, j: (i, j))] * len(dtypes),
        out_shape=[jax.ShapeDtypeStruct((M, sum(nblk) * tn), dtype) for dtype in dtypes],
        semantics=("parallel", "arbitrary"), jobs=jobs or ())
    grad = tuple(grad) if also_bf16 else grad[0]
    return grad if jobs is None else (grad, results)


def _pad_front(dst, src, halo):
    dst[pl.ds(0, halo), :] = jnp.zeros((halo, src.shape[1]), F32)

    def fill(i, carry):
        r0 = pl.multiple_of(i * CHUNK, CHUNK)
        dst[pl.ds(r0 + halo, CHUNK), :] = src[pl.ds(r0, CHUNK), :]
        return carry

    lax.fori_loop(0, src.shape[0] // CHUNK, fill, 0)


def _shift_rows(v, k):
    return pltpu.roll(v, k % v.shape[0], axis=0)


def _window_sums(xs, direction):
    s2 = xs + _shift_rows(xs, direction)
    s4 = s2 + _shift_rows(s2, 2 * direction)
    s8 = s4 + _shift_rows(s4, 4 * direction)
    s16 = s8 + _shift_rows(s8, 8 * direction)
    return s2, s4, s8, s16


def _select_window(g, sums):
    s2, s4, s8, s16 = sums
    return jnp.where(g == 0, s2, jnp.where(g == 1, s4, jnp.where(g == 2, s8, s16)))


def _pool_count(g, start, rows):
    t = start + lax.broadcasted_iota(jnp.int32, (rows, 1), 0)
    return jnp.minimum(t + 1, jnp.left_shift(2, g)).astype(F32)


def _pool_fwd(proj, w_grp, scale):
    T = proj.shape[0]
    nchunk = T // CHUNK

    def body(u_ref, w_ref, s_ref, o_ref, upad):
        g = pl.program_id(0)
        _pad_front(upad, u_ref, POOL_HALO)
        w = w_ref[...].astype(BF16)
        scale_row = s_ref[...]

        def chunk(i, carry):
            r0 = pl.multiple_of(i * CHUNK, CHUNK)
            xs = upad[pl.ds(r0, CHUNK + POOL_HALO), :]
            win = _select_window(g, _window_sums(xs, 1))[POOL_HALO:]
            pooled = win / _pool_count(g, r0, CHUNK) - xs[POOL_HALO:]
            o_ref[pl.ds(r0, CHUNK), :] = (_dot(pooled.astype(BF16), w) * scale_row).astype(BF16)
            return carry

        lax.fori_loop(0, nchunk, chunk, 0)

    return pl.pallas_call(
        body, name="pool_fwd", grid=(N_POOL_GROUPS,),
        in_specs=[pl.BlockSpec((T, HEAD), lambda g: (0, g)), pl.BlockSpec((None, HEAD, HEAD), lambda g: (g, 0, 0)),
                  pl.BlockSpec((1, HEAD), lambda g: (0, g))],
        out_specs=pl.BlockSpec((T, HEAD), lambda g: (0, g)),
        out_shape=jax.ShapeDtypeStruct((T, D_POOL), BF16),
        scratch_shapes=[pltpu.VMEM((T + POOL_HALO, HEAD), F32)],
        compiler_params=_params(dimension_semantics=("parallel",)),
    )(proj, w_grp, scale)


def _pool_bwd(proj, dpm, w_grp, scale, jobs=()):
    T = proj.shape[0]
    nchunk = T // CHUNK

    def body(u_ref, dpm_ref, w_ref, s_ref, du_ref, dw_ref, ds_ref, upad, zpad, dpool):
        g = pl.program_id(0)
        _pad_front(upad, u_ref, POOL_HALO)
        zpad[pl.ds(T, POOL_HALO), :] = jnp.zeros((POOL_HALO, HEAD), F32)
        dw_ref[...] = jnp.zeros_like(dw_ref)
        ds_ref[...] = jnp.zeros_like(ds_ref)
        w = w_ref[...].astype(BF16)
        scale_row = s_ref[...]

        def chunk(i, carry):
            r0 = pl.multiple_of(i * CHUNK, CHUNK)
            xs = upad[pl.ds(r0, CHUNK + POOL_HALO), :]
            cnt = _pool_count(g, r0, CHUNK)
            pooled = (_select_window(g, _window_sums(xs, 1))[POOL_HALO:] / cnt - xs[POOL_HALO:]).astype(BF16)
            mixed = _dot(pooled, w)
            d = dpm_ref[pl.ds(r0, CHUNK), :]
            ds_ref[...] += jnp.sum(d * mixed, axis=0, keepdims=True)
            dmixed = (d * scale_row).astype(BF16)
            dw_ref[...] += _dot_tn(pooled, dmixed)
            dp = _dot_nt(dmixed, w)
            dpool[pl.ds(r0, CHUNK), :] = dp
            zpad[pl.ds(r0, CHUNK), :] = dp / cnt
            return carry

        lax.fori_loop(0, nchunk, chunk, 0)

        def chunk2(i, carry):
            r0 = pl.multiple_of(i * CHUNK, CHUNK)
            zs = zpad[pl.ds(r0, CHUNK + POOL_HALO), :]
            win = _select_window(g, _window_sums(zs, -1))[:CHUNK]
            du_ref[pl.ds(r0, CHUNK), :] = (win - dpool[pl.ds(r0, CHUNK), :]).astype(BF16)
            return carry

        lax.fori_loop(0, nchunk, chunk2, 0)

    col = pl.BlockSpec((T, HEAD), lambda g: (0, g))
    return _pallas(
        body, (proj, dpm, w_grp, scale), name="pool_bwd", grid=(N_POOL_GROUPS,),
        in_specs=[col, col, pl.BlockSpec((None, HEAD, HEAD), lambda g: (g, 0, 0)), pl.BlockSpec((1, HEAD), lambda g: (0, g))],
        out_specs=[col, pl.BlockSpec((None, HEAD, HEAD), lambda g: (g, 0, 0)), pl.BlockSpec((1, HEAD), lambda g: (0, g))],
        out_shape=[jax.ShapeDtypeStruct((T, D_POOL), BF16), jax.ShapeDtypeStruct((N_POOL_GROUPS, HEAD, HEAD), F32),
                   jax.ShapeDtypeStruct((1, D_POOL), F32)],
        scratch_shapes=[pltpu.VMEM((T + POOL_HALO, HEAD), F32), pltpu.VMEM((T + POOL_HALO, HEAD), F32), pltpu.VMEM((T, HEAD), F32)],
        semantics=("parallel",), jobs=jobs)


def _conv_taps(xs, cw):
    v = cw[CONV_WIDTH - 1] * xs[SUBLANES:]
    for k in range(CONV_WIDTH - 1):
        v += cw[k] * _shift_rows(xs, CONV_WIDTH - 1 - k)[SUBLANES:]
    return v


def _tap_rows(cw_ref):
    return [cw_ref[k:k + 1, :] for k in range(CONV_WIDTH)]


def _softplus_neg(lam):
    return jnp.maximum(-lam, 0.0) + _log1p(jnp.exp(-jnp.abs(lam)))


def _lru_gates(v, wa, ba, wx, bx, sp):
    vb = v.astype(BF16)
    ra = _sigmoid(_dot(vb, wa) + ba)
    ix = _sigmoid(_dot(vb, wx) + bx)
    log_a = -LRU_C * ra * sp
    a = jnp.exp(log_a)
    sq = jnp.sqrt(-jnp.tanh(log_a) * (a * a + 1.0))
    return ra, ix, a, sq


def _row_bcast(v, r):
    return jnp.broadcast_to(v[r:r + 1, :], v.shape)


TILE_BLOCK = 128


def _scan_in_tiles(coef, coef_shift, A_out, B, T, direction):
    order = list(range(SUBLANES)) if direction == 1 else list(range(SUBLANES - 1, -1, -1))
    tiles = min(TILE_BLOCK, T // SUBLANES)
    for base in range(0, T, tiles * SUBLANES):
        def rows(r, base=base):
            return pl.ds(base + r, tiles, stride=SUBLANES)

        A, Bv = coef[rows(order[0] + coef_shift), :], B[rows(order[0]), :]
        A_out[rows(order[0]), :] = A
        for r in order[1:]:
            a = coef[rows(r + coef_shift), :]
            Bv = a * Bv + B[rows(r), :]
            A = a * A
            A_out[rows(r), :] = A
            B[rows(r), :] = Bv


TILES_PER_STEP = 8


def _carry_tiles(A_s, B_s, out, ntile, direction):
    out_row = SUBLANES - 1 if direction == 1 else 0

    def step(k, carry):
        for j in range(TILES_PER_STEP):
            t = k * TILES_PER_STEP + j
            r0 = pl.multiple_of((t if direction == 1 else ntile - 1 - t) * SUBLANES, SUBLANES)
            A, B = A_s[pl.ds(r0, SUBLANES), :], B_s[pl.ds(r0, SUBLANES), :]
            out[pl.ds(r0, SUBLANES), :] = A * carry + B
            carry = _row_bcast(A, out_row) * carry + _row_bcast(B, out_row)
        return carry

    lax.fori_loop(0, ntile // TILES_PER_STEP, step, jnp.zeros((SUBLANES, HEAD), F32))


def _rnn_fwd(proj, conv_w, conv_b, w_a, b_a, w_x, b_x, lam, jobs=()):
    T = proj.shape[0]
    nchunk = T // CHUNK
    ntile = T // SUBLANES

    def body(u_ref, ug_ref, cw_ref, cb_ref, wa_ref, ba_ref, wx_ref, bx_ref, lam_ref,
             h_ref, z_ref, v_ref, ra_ref, ix_ref, a_ref, sq_ref, upad, a_s, b_s):
        _pad_front(upad, u_ref, SUBLANES)
        cw, cb = _tap_rows(cw_ref), cb_ref[...]
        wa, wx = wa_ref[...].astype(BF16), wx_ref[...].astype(BF16)
        ba, bx = ba_ref[...], bx_ref[...]
        sp = _softplus_neg(lam_ref[...])

        def chunk(i, carry):
            rows = pl.ds(pl.multiple_of(i * CHUNK, CHUNK), CHUNK)
            v = _conv_taps(upad[pl.ds(pl.multiple_of(i * CHUNK, CHUNK), CHUNK + SUBLANES), :], cw) + cb
            ra, ix, a, sq = _lru_gates(v, wa, ba, wx, bx, sp)
            v_ref[rows, :], ra_ref[rows, :], ix_ref[rows, :], a_ref[rows, :], sq_ref[rows, :] = v, ra, ix, a, sq
            a_s[rows, :], b_s[rows, :] = a, sq * ix * v
            return carry

        lax.fori_loop(0, nchunk, chunk, 0)
        _scan_in_tiles(a_s, 0, a_s, b_s, T, 1)
        _carry_tiles(a_s, b_s, h_ref, ntile, 1)

        def chunk3(i, carry):
            r0 = pl.multiple_of(i * CHUNK, CHUNK)
            gl, _ = _gelu_parts(ug_ref[pl.ds(r0, CHUNK), :])
            z_ref[pl.ds(r0, CHUNK), :] = (h_ref[pl.ds(r0, CHUNK), :] * gl).astype(BF16)
            return carry

        lax.fori_loop(0, nchunk, chunk3, 0)

    col = pl.BlockSpec((T, HEAD), lambda h: (0, h))
    vec = pl.BlockSpec((1, HEAD), lambda h: (0, h))
    mat = pl.BlockSpec((None, HEAD, HEAD), lambda h: (h, 0, 0))
    return _pallas(
        body, (proj, proj, conv_w, conv_b, w_a, b_a, w_x, b_x, lam), name="rnn_fwd", grid=(N_RNN_HEADS,),
        in_specs=[pl.BlockSpec((T, HEAD), lambda h: (0, COL_RNN + h)), pl.BlockSpec((T, HEAD), lambda h: (0, COL_GATE + h)),
                  pl.BlockSpec((CONV_WIDTH, HEAD), lambda h: (0, h)), vec, mat, vec, mat, vec, vec],
        out_specs=[col] * 7,
        out_shape=[jax.ShapeDtypeStruct((T, D_RNN), F32), jax.ShapeDtypeStruct((T, D_RNN), BF16)]
        + [jax.ShapeDtypeStruct((T, D_RNN), F32)] * 5,
        scratch_shapes=[pltpu.VMEM((T + SUBLANES, HEAD), F32), pltpu.VMEM((T, HEAD), F32), pltpu.VMEM((T, HEAD), F32)],
        semantics=("parallel",), jobs=jobs)


def _rnn_bwd(proj, hr, dz, gates, conv_w, w_a, w_x, lam, jobs=()):
    T = proj.shape[0]
    nchunk = T // CHUNK
    ntile = T // SUBLANES

    def body(u_ref, ug_ref, h_ref, dz_ref, v_ref, ra_ref, ix_ref, a_ref, sq_ref, cw_ref, wa_ref, wx_ref, lam_ref,
             du_ref, dug_ref, dwa_ref, dwx_ref, dba_ref, dbx_ref, dlam_ref, dcb_ref, dcw_ref,
             upad, hpad, apad, g_s, dvpad, ga_s):
        zero_tile = jnp.zeros((SUBLANES, HEAD), F32)
        _pad_front(upad, u_ref, SUBLANES)
        _pad_front(hpad, h_ref, SUBLANES)
        apad[pl.ds(T, SUBLANES), :] = zero_tile
        dvpad[pl.ds(T, SUBLANES), :] = zero_tile
        for ref in (dwa_ref, dwx_ref, dba_ref, dbx_ref, dlam_ref, dcb_ref, dcw_ref):
            ref[...] = jnp.zeros_like(ref)
        cw = _tap_rows(cw_ref)
        wa, wx = wa_ref[...].astype(BF16), wx_ref[...].astype(BF16)
        lam_row = lam_ref[...]
        sp = _softplus_neg(lam_row)

        def chunk(i, carry):
            rows = pl.ds(pl.multiple_of(i * CHUNK, CHUNK), CHUNK)
            apad[rows, :] = a_ref[rows, :]
            gl, dgl = _gelu_parts(ug_ref[rows, :])
            d = dz_ref[rows, :]
            g_s[rows, :] = d * gl
            dug_ref[rows, :] = (d * h_ref[rows, :] * dgl).astype(BF16)
            return carry

        lax.fori_loop(0, nchunk, chunk, 0)

        _scan_in_tiles(apad, 1, ga_s, g_s, T, -1)
        _carry_tiles(ga_s, g_s, g_s, ntile, -1)

        def chunk3(i, carry):
            r0 = pl.multiple_of(i * CHUNK, CHUNK)
            rows = pl.ds(r0, CHUNK)
            g = g_s[rows, :]
            h_prev = _shift_rows(hpad[pl.ds(r0, CHUNK + SUBLANES), :], 1)[SUBLANES:]
            v, ra, ix, sq, a = v_ref[rows, :], ra_ref[rows, :], ix_ref[rows, :], sq_ref[rows, :], a_ref[rows, :]
            d_sq = g * ix * v
            d_ix = g * sq * v
            d_la = a * g * h_prev - d_sq * a * a / sq
            dlam_ref[...] += jnp.sum(d_la * ra, axis=0, keepdims=True)
            d_pa = d_la * (-LRU_C) * sp * ra * (1.0 - ra)
            d_px = d_ix * ix * (1.0 - ix)
            vb, d_pab, d_pxb = v.astype(BF16), d_pa.astype(BF16), d_px.astype(BF16)
            dwa_ref[...] += _dot_tn(vb, d_pab)
            dwx_ref[...] += _dot_tn(vb, d_pxb)
            dba_ref[...] += jnp.sum(d_pa, axis=0, keepdims=True)
            dbx_ref[...] += jnp.sum(d_px, axis=0, keepdims=True)
            dv = g * sq * ix + _dot_nt(d_pab, wa) + _dot_nt(d_pxb, wx)
            dvpad[rows, :] = dv
            dcb_ref[...] += jnp.sum(dv, axis=0, keepdims=True)
            xs = upad[pl.ds(r0, CHUNK + SUBLANES), :]
            for k in range(CONV_WIDTH):
                u_k = _shift_rows(xs, CONV_WIDTH - 1 - k)[SUBLANES:] if k < CONV_WIDTH - 1 else xs[SUBLANES:]
                dcw_ref[k:k + 1, :] += jnp.sum(dv * u_k, axis=0, keepdims=True)
            return carry

        lax.fori_loop(0, nchunk, chunk3, 0)
        dlam_ref[...] = dlam_ref[...] * (LRU_C * _sigmoid(-lam_row))

        def chunk4(i, carry):
            r0 = pl.multiple_of(i * CHUNK, CHUNK)
            dvs = dvpad[pl.ds(r0, CHUNK + SUBLANES), :]
            du = cw[CONV_WIDTH - 1] * dvs[:CHUNK]
            for k in range(CONV_WIDTH - 1):
                du += cw[k] * _shift_rows(dvs, -(CONV_WIDTH - 1 - k))[:CHUNK]
            du_ref[pl.ds(r0, CHUNK), :] = du.astype(BF16)
            return carry

        lax.fori_loop(0, nchunk, chunk4, 0)

    col = pl.BlockSpec((T, HEAD), lambda h: (0, h))
    vec = pl.BlockSpec((1, HEAD), lambda h: (0, h))
    mat = pl.BlockSpec((None, HEAD, HEAD), lambda h: (h, 0, 0))
    taps = pl.BlockSpec((CONV_WIDTH, HEAD), lambda h: (0, h))
    vec_out = jax.ShapeDtypeStruct((1, D_RNN), F32)
    mat_out = jax.ShapeDtypeStruct((N_RNN_HEADS, HEAD, HEAD), F32)
    seq = pltpu.VMEM((T, HEAD), F32)
    seq_pad = pltpu.VMEM((T + SUBLANES, HEAD), F32)
    return _pallas(
        body, (proj, proj, hr, dz, *gates, conv_w, w_a, w_x, lam), name="rnn_bwd", grid=(N_RNN_HEADS,),
        in_specs=[pl.BlockSpec((T, HEAD), lambda h: (0, COL_RNN + h)), pl.BlockSpec((T, HEAD), lambda h: (0, COL_GATE + h))]
        + [col] * 7 + [taps, mat, mat, vec],
        out_specs=[col, col, mat, mat, vec, vec, vec, vec, taps],
        out_shape=[jax.ShapeDtypeStruct((T, D_RNN), BF16), jax.ShapeDtypeStruct((T, D_RNN), BF16), mat_out, mat_out,
                   vec_out, vec_out, vec_out, vec_out, jax.ShapeDtypeStruct((CONV_WIDTH, D_RNN), F32)],
        scratch_shapes=[seq_pad, seq_pad, seq_pad, seq, seq_pad, seq],
        semantics=("parallel",), jobs=jobs)


GROUP_FFN_OUT = ["w_ffn_out"]
GROUP_FFN_IN = ["w_ffn_in"]
GROUP_MIX = ["w_o", "w_pool_out", "w_rnn_out"]
GROUP_IN = ["w_in"]


def _step(x, target, s, full, conv_w, place):
    T = x.shape[0]
    tall, mid, low = min(T, 2048), min(T, 1024), min(T, 512)
    full = dict(full)

    def gathered(names, results):
        full.update(zip(names, results))

    early = ["w_pool_out", "w_rnn_out", "w_o", "w_ffn_out"]
    (proj, h1), (res,) = _norm_matmul(x, s["norm_mix"], full["w_in"], tm=tall, tn=512, name="in_proj", jobs=[_gather_job(full, early)])
    gathered(early, res)
    pm = _pool_fwd(proj, s["w_pool_grp"], s["pool_scale"])
    (hr, z, *gates), (res,) = _rnn_fwd(proj, conv_w, s["conv_b"], s["w_rg_a"], s["b_rg_a"], s["w_rg_x"], s["b_rg_x"],
                                       s["lru_lambda"], jobs=[_gather_job(full, ["w_ffn_in"])])
    gathered(["w_ffn_in"], res)
    *mix_by, mix = _branch_mix(pm, z, full["w_pool_out"], full["w_rnn_out"], proj, tm=tall, tn=256)
    x2 = _out_proj_residual(mix, full["w_o"], x, tm=mid)
    (act_by_up, act_by_gate, act, h2), _ = _ffn_in(x2, s["norm_ffn"], full["w_ffn_in"], tm=tall, tn=256)
    dx3, dx3b, sq_cols, g_norm_final = _ffn_out_loss(act, full["w_ffn_out"], x2, s["norm_final"], target, tm=low)

    g = {"norm_final": g_norm_final}

    def chip_sums(names, from_sibling):
        sums = {name: _chip_sum(name, g[name], got, place) for name, got in zip(names, from_sibling)}
        return {name: v[0] for name, v in sums.items()}, {name: v[1] for name, v in sums.items()}

    def final_sums(names, sums, from_chips):
        return {name: _final_sum(name, sums[name], got, place) for name, got in zip(names, from_chips)}

    dgate, dup = _ffn_out_bwd(dx3b, full["w_ffn_out"], act_by_gate, act_by_up, tm=tall, tn=256)
    gb = {}
    g["w_ffn_out"], gb["w_ffn_out"] = _weight_grad(act, [dx3b], tm=256, tn=D_MODEL, name="w_ffn_out_grad", also_bf16=True)
    (dx2, dx2b, g["norm_ffn"]), _ = _ffn_in_bwd(dgate, dup, full["w_ffn_in"], dx3, x2, s["norm_ffn"], tm=low)
    (g["w_ffn_in"], gb["w_ffn_in"]), (res,) = _weight_grad(h2, [dgate, dup], tm=D_MODEL, tn=256, name="w_ffn_in_grad", also_bf16=True,
                                                           jobs=[_sibling_job(gb, GROUP_FFN_OUT, BF16)])
    sums_out, sums_out_bf16 = chip_sums(GROUP_FFN_OUT, res)
    (dgp, dgr, dyp, dyr), (res, from_sibling) = _out_proj_bwd(
        dx2b, full["w_o"], mix_by, tm=tall, tn=256,
        jobs=[_chips_job(sums_out_bf16, GROUP_FFN_OUT), _sibling_job(gb, GROUP_FFN_IN, BF16)])
    shards_ffn = final_sums(GROUP_FFN_OUT, sums_out, res)
    sums_ffn, sums_ffn_bf16 = chip_sums(GROUP_FFN_IN, from_sibling)
    g["w_o"], gb["w_o"] = _weight_grad(mix, [dx2b], tm=D_MODEL, tn=256, name="w_o_grad", also_bf16=True)
    dpm, dz = _branch_bwd(dyp, dyr, full["w_pool_out"], full["w_rnn_out"], tm=mid)
    g["w_pool_out"], gb["w_pool_out"] = _weight_grad(pm, [dyp], tm=D_POOL, tn=256, name="w_pool_out_grad", also_bf16=True)
    g["w_rnn_out"], gb["w_rnn_out"] = _weight_grad(z, [dyr], tm=D_RNN, tn=256, name="w_rnn_out_grad", also_bf16=True)
    (dupool, g["w_pool_grp"], g["pool_scale"]), _ = _pool_bwd(proj, dpm, s["w_pool_grp"], s["pool_scale"])
    ((durnn, dugate, g["w_rg_a"], g["w_rg_x"], g["b_rg_a"], g["b_rg_x"], g["lru_lambda"], g["conv_b"], g["conv_w"]),
     (res, from_sibling)) = _rnn_bwd(proj, hr, dz, gates, conv_w, s["w_rg_a"], s["w_rg_x"], s["lru_lambda"],
                                     jobs=[_chips_job(sums_ffn_bf16, GROUP_FFN_IN), _sibling_job(gb, GROUP_MIX, BF16)])
    shards_ffn.update(final_sums(GROUP_FFN_IN, sums_ffn, res))
    sums_mix, sums_mix_bf16 = chip_sums(GROUP_MIX, from_sibling)
    segs = [dupool, durnn, dugate, dgp, dgr]
    ffn = GROUP_FFN_OUT + GROUP_FFN_IN
    (g["w_in"], gb["w_in"]), (res, joined) = _weight_grad(
        h1, segs, tm=D_MODEL, tn=256, name="w_in_grad", also_bf16=True,
        jobs=[_chips_job(sums_mix_bf16, GROUP_MIX), _join_job(shards_ffn, ffn)])
    grads = dict(zip(ffn, joined))
    shards = final_sums(GROUP_MIX, sums_mix, res)
    (res,) = _run_jobs([_sibling_job(gb, GROUP_IN, BF16)], "w_in_exchange_sibling")
    sums_in, sums_in_bf16 = chip_sums(GROUP_IN, res)
    (grad_x, g["norm_mix"]), (res,) = _in_proj_bwd(segs, full["w_in"], dx2, x, s["norm_mix"], tm=low,
                                                  jobs=[_chips_job(sums_in_bf16, GROUP_IN)])
    shards.update(final_sums(GROUP_IN, sums_in, res))

    vec_rows = [g[name] if name != "pool_scale" else jnp.pad(g[name], ((0, 0), (0, D_MODEL - D_POOL))) for name in VEC_ITEMS]
    vec_rows += [g["conv_w"], sq_cols, jnp.zeros((VEC_ROWS - len(VEC_ITEMS) - CONV_WIDTH - 1, D_MODEL), F32)]
    vec = jnp.concatenate(vec_rows, axis=0).reshape(VEC_ROWS, N_DEV, HEAD).transpose(1, 0, 2)
    mat = jnp.concatenate([g[name].reshape(-1, HEAD) for name in MAT_ITEMS], axis=0).reshape(N_DEV, -1, HEAD)
    (vec, mat), (joined,) = _all_reduce_small([vec, mat], [False, True], jobs=[_join_job(shards, GROUP_MIX + GROUP_IN)])
    grads.update(zip(GROUP_MIX + GROUP_IN, joined))

    vec = vec.transpose(1, 0, 2).reshape(VEC_ROWS, D_MODEL)
    mat = mat.reshape(-1, HEAD)
    for k, name in enumerate(VEC_ITEMS):
        grads[name] = vec[k:k + 1, :s[name].shape[1]]
    grads["conv_w"] = vec[len(VEC_ITEMS):len(VEC_ITEMS) + CONV_WIDTH]
    row = 0
    for name in MAT_ITEMS:
        rows = s[name].shape[0] * HEAD
        grads[name] = mat[row:row + rows]
        row += rows
    return vec[len(VEC_ITEMS) + CONV_WIDTH], grad_x, grads


LARGE = {"w_in": "col", "w_pool_out": "col", "w_rnn_out": "row", "w_o": "row", "w_ffn_in": "col", "w_ffn_out": "row"}
LARGE_SHAPE = {"w_in": (D_MODEL, D_IN), "w_pool_out": (D_POOL, D_MODEL), "w_rnn_out": (D_RNN, D_MODEL),
               "w_o": (D_MODEL, D_MODEL), "w_ffn_in": (D_MODEL, 2 * D_FF), "w_ffn_out": (D_FF, D_MODEL)}


def _place():
    x, y, c = lax.axis_index("x"), lax.axis_index("y"), lax.axis_index("c")
    return 2 * x + y, c


def _chip_device(chip, c):
    return (chip // 2, chip % 2, c)


def _chip_window(ref, kind, shape, chip, half=None):
    K, N = shape
    if kind == "col":
        rows = slice(None) if half is None else pl.ds(half * (K // 2), K // 2)
        return ref.at[rows, pl.ds(chip * (N // N_CHIPS), N // N_CHIPS)]
    ks = K // N_CHIPS
    if half is None:
        return ref.at[pl.ds(chip * ks, ks), :]
    return ref.at[pl.ds(chip * ks + half * (ks // 2), ks // 2), :]


def _row_half(ref, half):
    rows = ref.shape[0] // 2
    return ref.at[pl.ds(half * rows, rows), :]


def _remote(win_src, win_dst, send_sems, recv_sems, idx, to):
    return pltpu.make_async_remote_copy(src_ref=win_src, dst_ref=win_dst, send_sem=send_sems.at[idx], recv_sem=recv_sems.at[idx],
                                        device_id=to, device_id_type=MESH)


def _gather_job(full, names, conv_w_full=None):
    n = len(names)
    cw_cols = D_RNN // N_CHIPS

    def windows(refs, chip, half):
        return [_chip_window(refs[k], LARGE[name], LARGE_SHAPE[name], chip, half) for k, name in enumerate(names)]

    def ici_copies(refs, send_sems, recv_sems, src_chip, dst_chip, c, r):
        wins = windows(refs, src_chip, c)
        if conv_w_full is not None:
            wins.append(refs[n].at[:, pl.ds(src_chip * cw_cols, cw_cols)])
        return [_remote(win, win, send_sems, recv_sems, (k, r), _chip_device(dst_chip, c)) for k, win in enumerate(wins)]

    def forwards(refs, send_sems, recv_sems, src_chip, half, to_core, chip, r):
        return [_remote(win, win, send_sems, recv_sems, (k, 3 + r), _chip_device(chip, to_core))
                for k, win in enumerate(windows(refs, src_chip, half))]

    def start(ins, outs, send_sems, recv_sems):
        chip, c = _place()
        for r in range(3):
            for cp in ici_copies(outs, send_sems, recv_sems, chip, chip ^ (r + 1), c, r):
                cp.start()

    def finish(ins, outs, send_sems, recv_sems):
        chip, c = _place()
        for r in range(3):
            for cp in ici_copies(outs, send_sems, recv_sems, chip ^ (r + 1), chip, c, r):
                cp.wait_recv()
            for cp in forwards(outs, send_sems, recv_sems, chip ^ (r + 1), c, 1 - c, chip, r):
                cp.start()
        for r in range(3):
            for cp in forwards(outs, send_sems, recv_sems, chip ^ (r + 1), 1 - c, c, chip, r):
                cp.wait_recv()
            for cp in ici_copies(outs, send_sems, recv_sems, chip, chip ^ (r + 1), c, r):
                cp.wait_send()
            for cp in forwards(outs, send_sems, recv_sems, chip ^ (r + 1), c, 1 - c, chip, r):
                cp.wait_send()

    arrays = [full[name] for name in names] + ([conv_w_full] if conv_w_full is not None else [])
    return _Job(arrays, [jax.ShapeDtypeStruct(a.shape, a.dtype) for a in arrays], {k: k for k in range(len(arrays))},
                (len(arrays), 6), start, finish)


def _core_halves(ref, kind, shape, c):
    return [_chip_window(ref, kind, shape, chip, c) for chip in range(N_CHIPS)]


def _sibling_job(grads, names, dtype=F32):
    def start(ins, outs, send_sems, recv_sems):
        chip, c = _place()
        for k, name in enumerate(names):
            kind, shape = LARGE[name], LARGE_SHAPE[name]
            if kind == "col":
                pairs = [(_row_half(ins[k], 1 - c), outs[k])]
            else:
                rows = shape[0] // N_DEV
                pairs = [(win, outs[k].at[pl.ds(j * rows, rows), :]) for j, win in enumerate(_core_halves(ins[k], kind, shape, 1 - c))]
            for src, dst in pairs:
                _remote(src, dst, send_sems, recv_sems, k, _chip_device(chip, 1 - c)).start()

    def finish(ins, outs, send_sems, recv_sems):
        chip, c = _place()
        for k in range(len(names)):
            _remote(outs[k], outs[k], send_sems, recv_sems, k, _chip_device(chip, 1 - c)).wait()

    return _Job([grads[name] for name in names],
                [jax.ShapeDtypeStruct((LARGE_SHAPE[name][0] // 2, LARGE_SHAPE[name][1]), dtype) for name in names], {},
                (len(names),), start, finish)


def _chip_sum(name, g, got, place):
    kind, (K, N) = LARGE[name], LARGE_SHAPE[name]
    rows = K // N_DEV
    piece_cols = N // N_CHIPS

    def body(place_ref, g_ref, got_ref, o_ref, ob_ref):
        total = g_ref[...] + got_ref[...].astype(F32)
        ob_ref[...] = total.astype(BF16)
        if kind == "col":
            for chip in range(N_CHIPS):
                @pl.when(place_ref[0] == chip)
                def _(chip=chip):
                    o_ref[...] = total[:, chip * piece_cols:(chip + 1) * piece_cols]
        else:
            @pl.when(pl.program_id(0) == place_ref[0])
            def _():
                o_ref[...] = total

    if kind == "col":
        mine = pl.BlockSpec((rows, N), lambda j, place_ref: (j + N_CHIPS * place_ref[1], 0))
        own = pl.BlockSpec((rows, piece_cols), lambda j, place_ref: (j, 0))
    else:
        mine = pl.BlockSpec((rows, N), lambda j, place_ref: (2 * j + place_ref[1], 0))
        own = pl.BlockSpec((rows, N), lambda j, place_ref: (0, 0))
    blk = pl.BlockSpec((rows, N), lambda j, place_ref: (j, 0))
    return pl.pallas_call(
        body, name=name + "_chip_sum",
        grid_spec=pltpu.PrefetchScalarGridSpec(num_scalar_prefetch=1, grid=(N_CHIPS,), in_specs=[mine, blk], out_specs=[own, blk]),
        out_shape=[jax.ShapeDtypeStruct(_piece_shape(name), F32), jax.ShapeDtypeStruct((K // 2, N), BF16)],
        compiler_params=_params(dimension_semantics=("arbitrary",)),
    )(place, g, got)


def _piece(ref, kind, shape, chip):
    K, N = shape
    if kind == "col":
        return ref.at[:, pl.ds(chip * (N // N_CHIPS), N // N_CHIPS)]
    return ref.at[pl.ds(chip * (K // N_DEV), K // N_DEV), :]


def _piece_shape(name):
    kind, (K, N) = LARGE[name], LARGE_SHAPE[name]
    return (K // 2, N // N_CHIPS) if kind == "col" else (K // N_DEV, N)


def _chips_job(sums, names):
    def copies(ins, outs, send_sems, recv_sems):
        chip, c = _place()
        return [_remote(_piece(ins[k], LARGE[name], LARGE_SHAPE[name], chip ^ (r + 1)), outs[k].at[r], send_sems, recv_sems, (k, r),
                        _chip_device(chip ^ (r + 1), c)) for k, name in enumerate(names) for r in range(3)]

    def start(*refs):
        for cp in copies(*refs):
            cp.start()

    def finish(*refs):
        for cp in copies(*refs):
            cp.wait()

    return _Job([sums[name] for name in names], [jax.ShapeDtypeStruct((3,) + _piece_shape(name), BF16) for name in names], {},
                (len(names), 3), start, finish)


def _final_sum(name, chip_sum, got, place):
    rows, cols = _piece_shape(name)

    def body(place_ref, s_ref, got_ref, o_ref):
        o_ref[...] = ((s_ref[...] + got_ref[0].astype(F32)) + got_ref[1].astype(F32)) + got_ref[2].astype(F32)

    mine = pl.BlockSpec((rows, cols), lambda i, place_ref: (0, 0))
    return pl.pallas_call(
        body, name=name + "_final_sum",
        grid_spec=pltpu.PrefetchScalarGridSpec(
            num_scalar_prefetch=1, grid=(1,), in_specs=[mine, pl.BlockSpec((3, rows, cols), lambda i, place_ref: (0, 0, 0))],
            out_specs=pl.BlockSpec((rows, cols), lambda i, place_ref: (place_ref[1], 0))),
        out_shape=jax.ShapeDtypeStruct((2 * rows, cols), F32),
        compiler_params=_params(dimension_semantics=("arbitrary",)),
    )(place, chip_sum, got)


def _join_job(shards, names):
    def half_copy(outs, send_sems, recv_sems, k, mine):
        chip, c = _place()
        win = _row_half(outs[k], c if mine else 1 - c)
        return _remote(win, win, send_sems, recv_sems, k, _chip_device(chip, 1 - c))

    def start(ins, outs, send_sems, recv_sems):
        for k in range(len(names)):
            half_copy(outs, send_sems, recv_sems, k, True).start()

    def finish(ins, outs, send_sems, recv_sems):
        for k in range(len(names)):
            half_copy(outs, send_sems, recv_sems, k, True).wait_send()
            half_copy(outs, send_sems, recv_sems, k, False).wait_recv()

    arrays = [shards[name] for name in names]
    return _Job(arrays, [jax.ShapeDtypeStruct(a.shape, F32) for a in arrays], {k: k for k in range(len(arrays))},
                (len(arrays),), start, finish)


VEC_ROWS = 16


def _all_reduce_small(slabs, narrow, jobs=()):
    n = len(slabs)
    narrowed = [k for k in range(n) if narrow[k]]

    def body(*refs):
        in_refs, out_refs, got_refs = refs[:n], refs[n:2 * n], refs[2 * n:3 * n]
        bf16_refs = dict(zip(narrowed, refs[3 * n:3 * n + len(narrowed)]))
        send_sems, recv_sems = refs[3 * n + len(narrowed):]
        x, y, c = lax.axis_index("x"), lax.axis_index("y"), lax.axis_index("c")
        me = 4 * x + 2 * y + c
        for k, ref in bf16_refs.items():
            ref[...] = in_refs[k][...].astype(BF16)
        partial_refs = [bf16_refs.get(k, in_refs[k]) for k in range(n)]

        def remote(src, dst, k, phase, r):
            other = me ^ r
            return pltpu.make_async_remote_copy(src_ref=src, dst_ref=dst, send_sem=send_sems.at[k, phase, r],
                                                recv_sem=recv_sems.at[k, phase, r],
                                                device_id=(other // 4, (other // 2) % 2, other % 2), device_id_type=MESH)

        scatter = [remote(partial_refs[k].at[me ^ r], got_refs[k].at[r], k, 0, r) for r in range(1, N_DEV) for k in range(n)]
        for cp in scatter:
            cp.start()
        for cp in scatter:
            cp.wait()
        for k in range(n):
            total = in_refs[k][me]
            for r in range(1, N_DEV):
                total = total + got_refs[k][r].astype(F32)
            out_refs[k][me] = total
        gather = [remote(out_refs[k].at[me], out_refs[k].at[me], k, 1, r) for r in range(1, N_DEV) for k in range(n)]
        for cp in gather:
            cp.start()
        for r in range(1, N_DEV):
            for k in range(n):
                remote(out_refs[k].at[me ^ r], out_refs[k].at[me ^ r], k, 1, r).wait_recv()
        for cp in gather:
            cp.wait_send()

    return _pallas(
        body, slabs, name="all_reduce_small", grid=(), in_specs=[VMEM] * n, out_specs=[VMEM] * n,
        out_shape=[jax.ShapeDtypeStruct(s.shape, F32) for s in slabs],
        scratch_shapes=[pltpu.VMEM(s.shape, BF16 if narrow[k] else F32) for k, s in enumerate(slabs)]
        + [pltpu.VMEM(slabs[k].shape, BF16) for k in narrowed]
        + [pltpu.SemaphoreType.DMA((n, 2, N_DEV)), pltpu.SemaphoreType.DMA((n, 2, N_DEV))], jobs=jobs)


def _cast_into_whole(w, name, place):
    rows, cols = w.shape
    tr = rows // 2

    def body(place_ref, w_ref, o_ref):
        o_ref[...] = w_ref[...].astype(BF16)

    if LARGE[name] == "col":
        window = pl.BlockSpec((tr, cols), lambda i, place_ref: (i, place_ref[0]))
    else:
        window = pl.BlockSpec((tr, cols), lambda i, place_ref: (2 * place_ref[0] + i, 0))
    return pl.pallas_call(
        body, name=name + "_cast",
        grid_spec=pltpu.PrefetchScalarGridSpec(num_scalar_prefetch=1, grid=(2,),
                                               in_specs=[pl.BlockSpec((tr, cols), lambda i, place_ref: (i, 0))], out_specs=window),
        out_shape=jax.ShapeDtypeStruct(LARGE_SHAPE[name], BF16),
        compiler_params=_params(dimension_semantics=("parallel",)))(place, w)


def _cast_many_into_whole(shards, place, jobs):
    names = list(shards)
    n = len(names)

    def body(place_ref, *refs):
        for w_ref, o_ref in zip(refs[:n], refs[n:]):
            o_ref[...] = w_ref[...].astype(BF16)

    def window(name):
        rows, cols = shards[name].shape
        if LARGE[name] == "col":
            return pl.BlockSpec((rows // 2, cols), lambda i, place_ref: (i, place_ref[0]))
        return pl.BlockSpec((rows // 2, cols), lambda i, place_ref: (2 * place_ref[0] + i, 0))

    def half(name):
        rows, cols = shards[name].shape
        return pl.BlockSpec((rows // 2, cols), lambda i, place_ref: (i, 0))

    return _pallas(body, [shards[name] for name in names], name="cast_weights", grid=(2,),
                   in_specs=[half(name) for name in names], out_specs=[window(name) for name in names],
                   out_shape=[jax.ShapeDtypeStruct(LARGE_SHAPE[name], BF16) for name in names],
                   semantics=("arbitrary",), jobs=jobs, prefetch=place)


def _adamw_math(w, g, m, v):
    m = ADAM_B1 * m + (1.0 - ADAM_B1) * g
    v = ADAM_B2 * v + (1.0 - ADAM_B2) * (g * g)
    m_hat = m / (1.0 - ADAM_B1 ** ADAM_STEP)
    v_hat = v / (1.0 - ADAM_B2 ** ADAM_STEP)
    delta = -ADAM_LR * (m_hat / (jnp.sqrt(v_hat) + ADAM_EPS) + ADAM_WD * w)
    return delta, m, v


def _adamw_large(w, g, m, v, name):
    rows, cols = w.shape
    steps = 2
    tr = rows // steps

    def body(w_ref, g_ref, m_ref, v_ref, d_ref, mo_ref, vo_ref):
        d_ref[...], mo_ref[...], vo_ref[...] = _adamw_math(w_ref[...], g_ref[...], m_ref[...], v_ref[...])

    blk = pl.BlockSpec((tr, cols), lambda i: (i, 0))
    out = jax.ShapeDtypeStruct(w.shape, F32)
    return pl.pallas_call(body, name=name + "_adamw", grid=(steps,), in_specs=[blk] * 4, out_specs=[blk] * 3, out_shape=[out] * 3,
                          compiler_params=_params(dimension_semantics=("parallel",)))(w, g, m, v)


def _adamw_small(ws, gs, ms, vs):
    n = len(ws)

    def body(*refs):
        for k in range(n):
            w_ref, g_ref, m_ref, v_ref = (refs[q * n + k] for q in range(4))
            d_ref, mo_ref, vo_ref = (refs[(4 + q) * n + k] for q in range(3))
            d_ref[...], mo_ref[...], vo_ref[...] = _adamw_math(w_ref[...], g_ref[...], m_ref[...], v_ref[...])

    out = [jax.ShapeDtypeStruct(w.shape, F32) for w in ws]
    res = pl.pallas_call(body, name="small_adamw", in_specs=[VMEM] * (4 * n), out_specs=[VMEM] * (3 * n), out_shape=out * 3,
                         compiler_params=_params())(*ws, *gs, *ms, *vs)
    return res[:n], res[n:2 * n], res[2 * n:]


WEIGHTS = ["norm_mix", "w_in", "w_pool_grp", "pool_scale", "w_pool_out", "conv_w", "conv_b", "w_rg_a", "b_rg_a", "w_rg_x",
           "b_rg_x", "lru_lambda", "w_rnn_out", "w_o", "norm_ffn", "w_ffn_in", "w_ffn_out", "norm_final"]
VEC_ITEMS = ["norm_mix", "norm_ffn", "norm_final", "pool_scale", "conv_b", "lru_lambda", "b_rg_a", "b_rg_x"]
MAT_ITEMS = ["w_pool_grp", "w_rg_a", "w_rg_x"]


def _as2d(name, a):
    if name in MAT_ITEMS:
        return a.reshape(-1, HEAD, HEAD)
    if name == "conv_w":
        return a.reshape(CONV_WIDTH, -1)
    return a.reshape(1, -1)


def kernel(x, norm_mix, w_in, w_pool_grp, pool_scale, w_pool_out, conv_w, conv_b, w_rg_a, b_rg_a, w_rg_x, b_rg_x, lru_lambda, w_rnn_out, w_o, norm_ffn, w_ffn_in, w_ffn_out, norm_final, loss_target, m_norm_mix, m_w_in, m_w_pool_grp, m_pool_scale, m_w_pool_out, m_conv_w, m_conv_b, m_w_rg_a, m_b_rg_a, m_w_rg_x, m_b_rg_x, m_lru_lambda, m_w_rnn_out, m_w_o, m_norm_ffn, m_w_ffn_in, m_w_ffn_out, m_norm_final, v_norm_mix, v_w_in, v_w_pool_grp, v_pool_scale, v_w_pool_out, v_conv_w, v_conv_b, v_w_rg_a, v_b_rg_a, v_w_rg_x, v_b_rg_x, v_lru_lambda, v_w_rnn_out, v_w_o, v_norm_ffn, v_w_ffn_in, v_w_ffn_out, v_norm_final):
    given = dict(locals())
    w = {name: given[name] for name in WEIGHTS}
    m = {name: given["m_" + name] for name in WEIGHTS}
    v = {name: given["v_" + name] for name in WEIGHTS}
    chip, c = _place()

    place = jnp.stack([chip, c]).astype(jnp.int32)
    conv_cols = w["conv_w"].shape[-1]
    conv_w_mine = lax.dynamic_update_slice_in_dim(jnp.zeros((CONV_WIDTH, D_RNN), F32), w["conv_w"][0], chip * conv_cols, axis=1)
    w_in_mine = _cast_into_whole(w["w_in"][0], "w_in", place)
    later = [name for name in LARGE if name != "w_in"]
    casts, ((w_in_full, conv_w_full),) = _cast_many_into_whole(
        {name: w[name][0] for name in later}, place, jobs=[_gather_job({"w_in": w_in_mine}, ["w_in"], conv_w_mine)])
    full = dict(zip(later, casts), w_in=w_in_full)
    small = {name: _as2d(name, w[name]) for name in WEIGHTS if name not in LARGE and name != "conv_w"}
    sq_cols, grad_x, grads = _step(x[0], loss_target[0], small, full, conv_w_full, place)
    loss = 0.5 / D_MODEL * jnp.sum(sq_cols)
    grads["conv_w"] = lax.dynamic_slice_in_dim(grads["conv_w"], chip * conv_cols, conv_cols, axis=1)

    delta, new_m, new_v = {}, {}, {}
    for name in LARGE:
        delta[name], new_m[name], new_v[name] = _adamw_large(w[name][0], grads[name], m[name][0], v[name][0], name)
    small_names = [name for name in WEIGHTS if name not in LARGE]
    flat = lambda d: [d[name].reshape(grads[name].shape) for name in small_names]
    ds, mo, vo = _adamw_small(flat(w), [grads[name] for name in small_names], flat(m), flat(v))
    for k, name in enumerate(small_names):
        delta[name], new_m[name], new_v[name] = ds[k], mo[k], vo[k]

    shaped = lambda d: [d[name].reshape(w[name].shape) for name in WEIGHTS]
    return (loss, grad_x[None], *shaped(grads), *shaped(delta), *shaped(new_m), *shaped(new_v))
```

```python
import functools
import math

import jax
import jax.numpy as jnp
from jax import lax
from jax.experimental import pallas as pl
from jax.experimental.pallas import tpu as pltpu

F32 = jnp.float32
BF16 = jnp.bfloat16

D_MODEL = 1024
D_POOL = 512
N_POOL_GROUPS = 4
D_RNN = 1024
N_RNN_HEADS = 8
HEAD = 128
CONV_WIDTH = 4
LRU_C = 8.0
D_FF = 2816
D_IN = D_POOL + 2 * D_RNN + 2 * D_MODEL
NORM_EPS = 1e-6
COL_RNN = D_POOL // HEAD
COL_GATE = (D_POOL + D_RNN) // HEAD

ADAM_LR = 0.001
ADAM_B1 = 0.9
ADAM_B2 = 0.999
ADAM_EPS = 1e-08
ADAM_WD = 0.01
ADAM_STEP = 10

N_CHIPS = 4
N_DEV = 8
MESH = pl.DeviceIdType.MESH
ANY = pl.BlockSpec(memory_space=pl.ANY)
VMEM = pl.BlockSpec(memory_space=pltpu.VMEM)
VMEM_LIMIT_BYTES = 60 * 1024 * 1024
SUBLANES = 8
POOL_HALO = 16
CHUNK = 1024

GELU_C = math.sqrt(2.0 / math.pi)
GELU_A = 0.044715


def _params(**kw):
    return pltpu.CompilerParams(vmem_limit_bytes=VMEM_LIMIT_BYTES, **kw)


def _sigmoid(x):
    return 0.5 * jnp.tanh(0.5 * x) + 0.5


def _log1p(y):
    u = 1.0 + y
    d = u - 1.0
    return jnp.where(d == 0.0, y, jnp.log(u) * (y / jnp.where(d == 0.0, 1.0, d)))


def _gelu_parts(x):
    x2 = x * x
    th = jnp.tanh(GELU_C * (x + GELU_A * x * x2))
    g = 0.5 * x * (1.0 + th)
    dg = 0.5 * (1.0 + th) + 0.5 * x * (1.0 - th * th) * GELU_C * (1.0 + 3.0 * GELU_A * x2)
    return g, dg


def _dot(a, b):
    return jnp.dot(a, b, preferred_element_type=F32)


def _dot_nt(a, b):
    return lax.dot_general(a, b, (((1,), (1,)), ((), ())), preferred_element_type=F32)


def _dot_tn(a, b):
    return lax.dot_general(a, b, (((0,), (0,)), ((), ())), preferred_element_type=F32)


def _rms_scale(xv):
    return lax.rsqrt(jnp.mean(xv * xv, axis=-1, keepdims=True) + NORM_EPS)


def _rms_bwd(dy, xv, g):
    r = _rms_scale(xv)
    xh = xv * r
    dyg = dy * g
    dx = r * (dyg - xh * jnp.mean(dyg * xh, axis=-1, keepdims=True))
    return dx, dy * xh


class _Job:
    def __init__(self, inputs, out_shapes, aliases, sem_shape, start, finish):
        self.inputs, self.out_shapes, self.aliases, self.sem_shape = list(inputs), list(out_shapes), dict(aliases), sem_shape
        self.start, self.finish = start, finish


def _pallas(body, operands, *, name, grid, in_specs, out_specs, out_shape, scratch_shapes=(), semantics=None, jobs=(),
            prefetch=None):
    n_in, n_out, n_scr = len(in_specs), len(out_specs), len(scratch_shapes)
    n_pre = 0 if prefetch is None else 1
    job_in = [a for job in jobs for a in job.inputs]
    job_out = [s for job in jobs for s in job.out_shapes]
    aliases, i0, o0 = {}, n_pre + n_in, n_out
    for job in jobs:
        aliases.update({i0 + i: o0 + o for i, o in job.aliases.items()})
        i0, o0 = i0 + len(job.inputs), o0 + len(job.out_shapes)

    def whole(*refs):
        pre, refs = refs[:n_pre], refs[n_pre:]
        ins, j_ins = refs[:n_in], refs[n_in:n_in + len(job_in)]
        outs = refs[n_in + len(job_in):][:n_out]
        j_outs = refs[n_in + len(job_in) + n_out:][:len(job_out)]
        rest = refs[n_in + len(job_in) + n_out + len(job_out):]
        scr, sems = rest[:n_scr], rest[n_scr:]

        def run(phase):
            i, o = 0, 0
            for k, job in enumerate(jobs):
                getattr(job, phase)(j_ins[i:i + len(job.inputs)], j_outs[o:o + len(job.out_shapes)], sems[2 * k], sems[2 * k + 1])
                i, o = i + len(job.inputs), o + len(job.out_shapes)

        def at(step_of, phase):
            if not jobs:
                return
            if not grid:
                run(phase)
                return
            cond = functools.reduce(jnp.logical_and, [pl.program_id(d) == step_of(d) for d in range(len(grid))])
            pl.when(cond)(functools.partial(run, phase))

        at(lambda d: 0, "start")
        body(*pre, *ins, *outs, *scr)
        at(lambda d: grid[d] - 1, "finish")

    layout = dict(grid=grid, in_specs=list(in_specs) + [ANY] * len(job_in), out_specs=list(out_specs) + [ANY] * len(job_out),
                  scratch_shapes=list(scratch_shapes) + [pltpu.SemaphoreType.DMA(job.sem_shape) for job in jobs for _ in range(2)])
    if prefetch is not None:
        layout = dict(grid_spec=pltpu.PrefetchScalarGridSpec(num_scalar_prefetch=1, **layout))
    res = pl.pallas_call(
        whole, name=name, out_shape=list(out_shape) + job_out, input_output_aliases=aliases,
        compiler_params=_params(dimension_semantics=semantics, has_side_effects=bool(jobs)), **layout,
    )(*([] if prefetch is None else [prefetch]), *operands, *job_in)
    per_job, o = [], n_out
    for job in jobs:
        per_job.append(res[o:o + len(job.out_shapes)])
        o += len(job.out_shapes)
    return res[:n_out], per_job


def _run_jobs(jobs, name):
    return _pallas(lambda: None, [], name=name, grid=(), in_specs=[], out_specs=[], out_shape=[], jobs=jobs)[1]


NORM_ROWS = 256
EPILOGUE_ROWS = 512


def _norm_rows(x_ref, g_ref, h_ref):
    g = g_ref[...]

    def rows(i, carry):
        r = pl.ds(pl.multiple_of(i * NORM_ROWS, NORM_ROWS), NORM_ROWS)
        xv = x_ref[r, :]
        h_ref[r, :] = (xv * _rms_scale(xv) * g).astype(BF16)
        return carry

    lax.fori_loop(0, x_ref.shape[0] // NORM_ROWS, rows, 0)


def _norm_matmul(x, g, w, *, tm, tn, name, jobs=()):
    T, K = x.shape
    N = w.shape[1]

    def body(x_ref, g_ref, w_ref, o_ref, h_ref):
        @pl.when(pl.program_id(1) == 0)
        def _():
            _norm_rows(x_ref, g_ref, h_ref)

        o_ref[...] = _dot(h_ref[...], w_ref[...])

    return _pallas(
        body, (x, g, w), name=name, grid=(T // tm, N // tn),
        in_specs=[pl.BlockSpec((tm, K), lambda i, j: (i, 0)), pl.BlockSpec((1, K), lambda i, j: (0, 0)),
                  pl.BlockSpec((K, tn), lambda i, j: (0, j))],
        out_specs=[pl.BlockSpec((tm, tn), lambda i, j: (i, j)), pl.BlockSpec((tm, K), lambda i, j: (i, 0))],
        out_shape=[jax.ShapeDtypeStruct((T, N), F32), jax.ShapeDtypeStruct((T, K), BF16)],
        semantics=("parallel", "arbitrary"), jobs=jobs)


def _ffn_in(x2, g, w, *, tm, tn, jobs=()):
    T, K = x2.shape
    nb = D_FF // tn

    def body(x_ref, g_ref, wg_ref, wu_ref, dup_ref, dgate_ref, act_ref, h_ref):
        @pl.when(pl.program_id(1) == 0)
        def _():
            _norm_rows(x_ref, g_ref, h_ref)

        wg, wu = wg_ref[...], wu_ref[...]
        for r in range(0, tm, EPILOGUE_ROWS):
            rows = pl.ds(r, min(EPILOGUE_ROWS, tm))
            h = h_ref[rows, :]
            gate, up = _dot(h, wg), _dot(h, wu)
            s = _sigmoid(gate)
            silu = gate * s
            dup_ref[rows, :] = silu.astype(BF16)
            dgate_ref[rows, :] = (up * (s + silu * (1.0 - s))).astype(BF16)
            act_ref[rows, :] = (silu * up).astype(BF16)

    blk = pl.BlockSpec((tm, tn), lambda i, j: (i, j))
    return _pallas(
        body, (x2, g, w, w), name="ffn_in", grid=(T // tm, nb),
        in_specs=[pl.BlockSpec((tm, K), lambda i, j: (i, 0)), pl.BlockSpec((1, K), lambda i, j: (0, 0)),
                  pl.BlockSpec((K, tn), lambda i, j: (0, j)), pl.BlockSpec((K, tn), lambda i, j: (0, j + nb))],
        out_specs=[blk, blk, blk, pl.BlockSpec((tm, K), lambda i, j: (i, 0))],
        out_shape=[jax.ShapeDtypeStruct((T, D_FF), BF16), jax.ShapeDtypeStruct((T, D_FF), BF16),
                   jax.ShapeDtypeStruct((T, D_FF), BF16), jax.ShapeDtypeStruct((T, K), BF16)],
        semantics=("parallel", "arbitrary"), jobs=jobs)


def _branch_mix(pm, z, w_pool_out, w_rnn_out, proj, w_o, x, *, tm, tn):
    T = pm.shape[0]
    col_gp = (D_POOL + 2 * D_RNN) // tn
    col_gr = col_gp + D_MODEL // tn

    def body(pm_ref, z_ref, wp_ref, wr_ref, gp_ref, gr_ref, wo_ref, x_ref, by_gp_ref, by_gr_ref, sp_ref, sr_ref, mix_ref, x2_ref):
        @pl.when(pl.program_id(1) == 0)
        def _():
            x2_ref[...] = x_ref[...]

        wp, wr, wo = wp_ref[...], wr_ref[...], wo_ref[...]
        for r in range(0, tm, EPILOGUE_ROWS):
            rows = pl.ds(r, min(EPILOGUE_ROWS, tm))
            yp, yr = _dot(pm_ref[rows, :], wp), _dot(z_ref[rows, :], wr)
            sp, sr = _sigmoid(gp_ref[rows, :]), _sigmoid(gr_ref[rows, :])
            by_gp_ref[rows, :] = (yp * sp * (1.0 - sp)).astype(BF16)
            by_gr_ref[rows, :] = (yr * sr * (1.0 - sr)).astype(BF16)
            sp_ref[rows, :] = sp.astype(BF16)
            sr_ref[rows, :] = sr.astype(BF16)
            mix = (sp * yp + sr * yr).astype(BF16)
            mix_ref[rows, :] = mix
            x2_ref[rows, :] += _dot(mix, wo)

    blk = pl.BlockSpec((tm, tn), lambda i, j: (i, j))
    row = pl.BlockSpec((tm, D_MODEL), lambda i, j: (i, 0))
    out = jax.ShapeDtypeStruct((T, D_MODEL), BF16)
    return pl.pallas_call(
        body, name="branch_mix", grid=(T // tm, D_MODEL // tn),
        in_specs=[pl.BlockSpec((tm, D_POOL), lambda i, j: (i, 0)), pl.BlockSpec((tm, D_RNN), lambda i, j: (i, 0)),
                  pl.BlockSpec((D_POOL, tn), lambda i, j: (0, j)), pl.BlockSpec((D_RNN, tn), lambda i, j: (0, j)),
                  pl.BlockSpec((tm, tn), lambda i, j: (i, col_gp + j)), pl.BlockSpec((tm, tn), lambda i, j: (i, col_gr + j)),
                  pl.BlockSpec((tn, D_MODEL), lambda i, j: (j, 0)), row],
        out_specs=[blk] * 5 + [row], out_shape=[out] * 5 + [jax.ShapeDtypeStruct((T, D_MODEL), F32)],
        compiler_params=_params(dimension_semantics=("parallel", "arbitrary")),
    )(pm, z, w_pool_out, w_rnn_out, proj, proj, w_o, x)


def _ffn_out_loss(act, w, x2, g3, target, *, tm):
    T = x2.shape[0]

    def body(act_ref, w_ref, x2_ref, g_ref, t_ref, dx_ref, dxb_ref, sq_ref, dg_ref):
        @pl.when(pl.program_id(0) == 0)
        def _():
            sq_ref[...] = jnp.zeros_like(sq_ref)
            dg_ref[...] = jnp.zeros_like(dg_ref)

        g, w = g_ref[...], w_ref[...]
        for r in range(0, tm, NORM_ROWS):
            rows = pl.ds(r, min(NORM_ROWS, tm))
            x3 = x2_ref[rows, :] + _dot(act_ref[rows, :], w)
            err = x3 * _rms_scale(x3) * g - t_ref[rows, :]
            sq_ref[...] += jnp.sum(err * err, axis=0, keepdims=True)
            dx, dgp = _rms_bwd(err * (1.0 / D_MODEL), x3, g)
            dg_ref[...] += jnp.sum(dgp, axis=0, keepdims=True)
            dx_ref[rows, :] = dx
            dxb_ref[rows, :] = dx.astype(BF16)

    row = pl.BlockSpec((tm, D_MODEL), lambda i: (i, 0))
    vec = pl.BlockSpec((1, D_MODEL), lambda i: (0, 0))
    return pl.pallas_call(
        body, name="ffn_out_loss", grid=(T // tm,),
        in_specs=[pl.BlockSpec((tm, D_FF), lambda i: (i, 0)), pl.BlockSpec((D_FF, D_MODEL), lambda i: (0, 0)), row, vec, row],
        out_specs=[row, row, vec, vec],
        out_shape=[jax.ShapeDtypeStruct((T, D_MODEL), F32), jax.ShapeDtypeStruct((T, D_MODEL), BF16),
                   jax.ShapeDtypeStruct((1, D_MODEL), F32), jax.ShapeDtypeStruct((1, D_MODEL), F32)],
        compiler_params=_params(dimension_semantics=("arbitrary",)),
    )(act, w, x2, g3, target)


def _ffn_out_bwd(dx3b, w, act_by_gate, act_by_up, *, tm, tn):
    T = dx3b.shape[0]

    def body(dx_ref, w_ref, by_gate_ref, by_up_ref, dgate_ref, dup_ref):
        w = w_ref[...]
        for r in range(0, tm, EPILOGUE_ROWS):
            rows = pl.ds(r, min(EPILOGUE_ROWS, tm))
            dact = _dot_nt(dx_ref[rows, :], w)
            dgate_ref[rows, :] = (dact * by_gate_ref[rows, :].astype(F32)).astype(BF16)
            dup_ref[rows, :] = (dact * by_up_ref[rows, :].astype(F32)).astype(BF16)

    blk = pl.BlockSpec((tm, tn), lambda i, j: (i, j))
    return pl.pallas_call(
        body, name="ffn_out_bwd", grid=(T // tm, D_FF // tn),
        in_specs=[pl.BlockSpec((tm, D_MODEL), lambda i, j: (i, 0)), pl.BlockSpec((tn, D_MODEL), lambda i, j: (j, 0)), blk, blk],
        out_specs=[blk, blk],
        out_shape=[jax.ShapeDtypeStruct((T, D_FF), BF16), jax.ShapeDtypeStruct((T, D_FF), BF16)],
        compiler_params=_params(dimension_semantics=("parallel", "parallel")),
    )(dx3b, w, act_by_gate, act_by_up)


def _ffn_in_bwd(dgate, dup, w, dx3, x2, g2, *, tm, jobs=()):
    T = x2.shape[0]

    def body(dgate_ref, dup_ref, w_ref, dx3_ref, x2_ref, g_ref, dx_ref, dxb_ref, dg_ref):
        @pl.when(pl.program_id(0) == 0)
        def _():
            dg_ref[...] = jnp.zeros_like(dg_ref)

        g = g_ref[...]
        for r in range(0, tm, NORM_ROWS):
            rows = pl.ds(r, min(NORM_ROWS, tm))
            dh = _dot_nt(dgate_ref[rows, :], w_ref[:, :D_FF]) + _dot_nt(dup_ref[rows, :], w_ref[:, D_FF:])
            dxn, dgp = _rms_bwd(dh, x2_ref[rows, :], g)
            dx = dx3_ref[rows, :] + dxn
            dg_ref[...] += jnp.sum(dgp, axis=0, keepdims=True)
            dx_ref[rows, :] = dx
            dxb_ref[rows, :] = dx.astype(BF16)

    row = pl.BlockSpec((tm, D_MODEL), lambda i: (i, 0))
    wide = pl.BlockSpec((tm, D_FF), lambda i: (i, 0))
    vec = pl.BlockSpec((1, D_MODEL), lambda i: (0, 0))
    return _pallas(
        body, (dgate, dup, w, dx3, x2, g2), name="ffn_in_bwd", grid=(T // tm,),
        in_specs=[wide, wide, pl.BlockSpec((D_MODEL, 2 * D_FF), lambda i: (0, 0)), row, row, vec],
        out_specs=[row, row, vec],
        out_shape=[jax.ShapeDtypeStruct((T, D_MODEL), F32), jax.ShapeDtypeStruct((T, D_MODEL), BF16),
                   jax.ShapeDtypeStruct((1, D_MODEL), F32)],
        semantics=("arbitrary",), jobs=jobs)


def _out_proj_bwd(dx2b, w_o, mix_by, *, tm, tn, jobs=()):
    T = dx2b.shape[0]

    def body(dx_ref, w_ref, *refs):
        w = w_ref[...]
        for r in range(0, tm, EPILOGUE_ROWS):
            rows = pl.ds(r, min(EPILOGUE_ROWS, tm))
            dmix = _dot_nt(dx_ref[rows, :], w)
            for by_ref, d_ref in zip(refs[:4], refs[4:]):
                d_ref[rows, :] = (dmix * by_ref[rows, :].astype(F32)).astype(BF16)

    blk = pl.BlockSpec((tm, tn), lambda i, j: (i, j))
    out = jax.ShapeDtypeStruct((T, D_MODEL), BF16)
    return _pallas(
        body, (dx2b, w_o, *mix_by), name="out_proj_bwd", grid=(T // tm, D_MODEL // tn),
        in_specs=[pl.BlockSpec((tm, D_MODEL), lambda i, j: (i, 0)), pl.BlockSpec((tn, D_MODEL), lambda i, j: (j, 0))] + [blk] * 4,
        out_specs=[blk] * 4, out_shape=[out] * 4, semantics=("parallel", "parallel"), jobs=jobs)


def _branch_bwd(dyp, dyr, w_pool_out, w_rnn_out, *, tm):
    T = dyp.shape[0]

    def body(dyp_ref, dyr_ref, wp_ref, wr_ref, dpm_ref, dz_ref):
        dpm_ref[...] = _dot_nt(dyp_ref[...], wp_ref[...])
        dz_ref[...] = _dot_nt(dyr_ref[...], wr_ref[...])

    row = pl.BlockSpec((tm, D_MODEL), lambda i: (i, 0))
    return pl.pallas_call(
        body, name="branch_bwd", grid=(T // tm,),
        in_specs=[row, row, pl.BlockSpec((D_POOL, D_MODEL), lambda i: (0, 0)), pl.BlockSpec((D_RNN, D_MODEL), lambda i: (0, 0))],
        out_specs=[pl.BlockSpec((tm, D_POOL), lambda i: (i, 0)), pl.BlockSpec((tm, D_RNN), lambda i: (i, 0))],
        out_shape=[jax.ShapeDtypeStruct((T, D_POOL), F32), jax.ShapeDtypeStruct((T, D_RNN), F32)],
        compiler_params=_params(dimension_semantics=("parallel",)),
    )(dyp, dyr, w_pool_out, w_rnn_out)


def _in_proj_bwd(segs, w, dx2, x, g1, *, tm, jobs=()):
    T = x.shape[0]
    widths = [s.shape[1] for s in segs]
    offs = [sum(widths[:k]) for k in range(len(widths))]
    n = len(segs)

    def body(*refs):
        seg_refs, (w_ref, dx2_ref, x_ref, g_ref, dx_ref, dg_ref) = refs[:n], refs[n:]

        @pl.when(pl.program_id(0) == 0)
        def _():
            dg_ref[...] = jnp.zeros_like(dg_ref)

        g = g_ref[...]
        for r in range(0, tm, NORM_ROWS):
            rows = pl.ds(r, min(NORM_ROWS, tm))
            dh = _dot_nt(seg_refs[0][rows, :], w_ref[:, offs[0]:offs[0] + widths[0]])
            for k in range(1, n):
                dh += _dot_nt(seg_refs[k][rows, :], w_ref[:, offs[k]:offs[k] + widths[k]])
            dxn, dgp = _rms_bwd(dh, x_ref[rows, :], g)
            dg_ref[...] += jnp.sum(dgp, axis=0, keepdims=True)
            dx_ref[rows, :] = dx2_ref[rows, :] + dxn

    row = pl.BlockSpec((tm, D_MODEL), lambda i: (i, 0))
    vec = pl.BlockSpec((1, D_MODEL), lambda i: (0, 0))
    return _pallas(
        body, (*segs, w, dx2, x, g1), name="in_proj_bwd", grid=(T // tm,),
        in_specs=[pl.BlockSpec((tm, wd), lambda i: (i, 0)) for wd in widths]
        + [pl.BlockSpec((D_MODEL, D_IN), lambda i: (0, 0)), row, row, vec],
        out_specs=[row, vec],
        out_shape=[jax.ShapeDtypeStruct((T, D_MODEL), F32), jax.ShapeDtypeStruct((1, D_MODEL), F32)],
        semantics=("arbitrary",), jobs=jobs)


def _weight_grad(a, segs, *, tm, tn, name, jobs=None, also_bf16=False):
    T, M = a.shape
    nblk = [s.shape[1] // tn for s in segs]
    first = [sum(nblk[:k]) for k in range(len(segs))]
    n = len(segs)

    def body(a_ref, *refs):
        seg_refs, o_refs = refs[:n], refs[n:]
        j = pl.program_id(1)
        for k in range(n):
            @pl.when((j >= first[k]) & (j < first[k] + nblk[k]))
            def _(k=k):
                grad = _dot_tn(a_ref[...], seg_refs[k][...])
                for o_ref in o_refs:
                    o_ref[...] = grad.astype(o_ref.dtype)

    def seg_spec(k):
        return pl.BlockSpec((T, tn), lambda i, j: (0, jnp.clip(j - first[k], 0, nblk[k] - 1)))

    dtypes = [F32, BF16] if also_bf16 else [F32]
    grad, results = _pallas(
        body, (a, *segs), name=name, grid=(M // tm, sum(nblk)),
        in_specs=[pl.BlockSpec((T, tm), lambda i, j: (0, i))] + [seg_spec(k) for k in range(n)],
        out_specs=[pl.BlockSpec((tm, tn), lambda i, j: (i, j))] * len(dtypes),
        out_shape=[jax.ShapeDtypeStruct((M, sum(nblk) * tn), dtype) for dtype in dtypes],
        semantics=("parallel", "arbitrary"), jobs=jobs or ())
    grad = tuple(grad) if also_bf16 else grad[0]
    return grad if jobs is None else (grad, results)


def _pad_front(dst, src, halo):
    dst[pl.ds(0, halo), :] = jnp.zeros((halo, src.shape[1]), F32)

    def fill(i, carry):
        r0 = pl.multiple_of(i * CHUNK, CHUNK)
        dst[pl.ds(r0 + halo, CHUNK), :] = src[pl.ds(r0, CHUNK), :]
        return carry

    lax.fori_loop(0, src.shape[0] // CHUNK, fill, 0)


def _shift_rows(v, k):
    return pltpu.roll(v, k % v.shape[0], axis=0)


def _window_sums(xs, direction):
    s2 = xs + _shift_rows(xs, direction)
    s4 = s2 + _shift_rows(s2, 2 * direction)
    s8 = s4 + _shift_rows(s4, 4 * direction)
    s16 = s8 + _shift_rows(s8, 8 * direction)
    return s2, s4, s8, s16


def _select_window(g, sums):
    s2, s4, s8, s16 = sums
    return jnp.where(g == 0, s2, jnp.where(g == 1, s4, jnp.where(g == 2, s8, s16)))


def _pool_count(g, start, rows):
    t = start + lax.broadcasted_iota(jnp.int32, (rows, 1), 0)
    return jnp.minimum(t + 1, jnp.left_shift(2, g)).astype(F32)


def _pool_fwd(proj, w_grp, scale):
    T = proj.shape[0]
    nchunk = T // CHUNK

    def body(u_ref, w_ref, s_ref, o_ref, upad):
        g = pl.program_id(0)
        _pad_front(upad, u_ref, POOL_HALO)
        w = w_ref[...].astype(BF16)
        scale_row = s_ref[...]

        def chunk(i, carry):
            r0 = pl.multiple_of(i * CHUNK, CHUNK)
            xs = upad[pl.ds(r0, CHUNK + POOL_HALO), :]
            win = _select_window(g, _window_sums(xs, 1))[POOL_HALO:]
            pooled = win / _pool_count(g, r0, CHUNK) - xs[POOL_HALO:]
            o_ref[pl.ds(r0, CHUNK), :] = (_dot(pooled.astype(BF16), w) * scale_row).astype(BF16)
            return carry

        lax.fori_loop(0, nchunk, chunk, 0)

    return pl.pallas_call(
        body, name="pool_fwd", grid=(N_POOL_GROUPS,),
        in_specs=[pl.BlockSpec((T, HEAD), lambda g: (0, g)), pl.BlockSpec((None, HEAD, HEAD), lambda g: (g, 0, 0)),
                  pl.BlockSpec((1, HEAD), lambda g: (0, g))],
        out_specs=pl.BlockSpec((T, HEAD), lambda g: (0, g)),
        out_shape=jax.ShapeDtypeStruct((T, D_POOL), BF16),
        scratch_shapes=[pltpu.VMEM((T + POOL_HALO, HEAD), F32)],
        compiler_params=_params(dimension_semantics=("parallel",)),
    )(proj, w_grp, scale)


def _pool_bwd(proj, dpm, w_grp, scale, jobs=()):
    T = proj.shape[0]
    nchunk = T // CHUNK

    def body(u_ref, dpm_ref, w_ref, s_ref, du_ref, dw_ref, ds_ref, upad, zpad, dpool):
        g = pl.program_id(0)
        _pad_front(upad, u_ref, POOL_HALO)
        zpad[pl.ds(T, POOL_HALO), :] = jnp.zeros((POOL_HALO, HEAD), F32)
        dw_ref[...] = jnp.zeros_like(dw_ref)
        ds_ref[...] = jnp.zeros_like(ds_ref)
        w = w_ref[...].astype(BF16)
        scale_row = s_ref[...]

        def chunk(i, carry):
            r0 = pl.multiple_of(i * CHUNK, CHUNK)
            xs = upad[pl.ds(r0, CHUNK + POOL_HALO), :]
            cnt = _pool_count(g, r0, CHUNK)
            pooled = (_select_window(g, _window_sums(xs, 1))[POOL_HALO:] / cnt - xs[POOL_HALO:]).astype(BF16)
            mixed = _dot(pooled, w)
            d = dpm_ref[pl.ds(r0, CHUNK), :]
            ds_ref[...] += jnp.sum(d * mixed, axis=0, keepdims=True)
            dmixed = (d * scale_row).astype(BF16)
            dw_ref[...] += _dot_tn(pooled, dmixed)
            dp = _dot_nt(dmixed, w)
            dpool[pl.ds(r0, CHUNK), :] = dp
            zpad[pl.ds(r0, CHUNK), :] = dp / cnt
            return carry

        lax.fori_loop(0, nchunk, chunk, 0)

        def chunk2(i, carry):
            r0 = pl.multiple_of(i * CHUNK, CHUNK)
            zs = zpad[pl.ds(r0, CHUNK + POOL_HALO), :]
            win = _select_window(g, _window_sums(zs, -1))[:CHUNK]
            du_ref[pl.ds(r0, CHUNK), :] = (win - dpool[pl.ds(r0, CHUNK), :]).astype(BF16)
            return carry

        lax.fori_loop(0, nchunk, chunk2, 0)

    col = pl.BlockSpec((T, HEAD), lambda g: (0, g))
    return _pallas(
        body, (proj, dpm, w_grp, scale), name="pool_bwd", grid=(N_POOL_GROUPS,),
        in_specs=[col, col, pl.BlockSpec((None, HEAD, HEAD), lambda g: (g, 0, 0)), pl.BlockSpec((1, HEAD), lambda g: (0, g))],
        out_specs=[col, pl.BlockSpec((None, HEAD, HEAD), lambda g: (g, 0, 0)), pl.BlockSpec((1, HEAD), lambda g: (0, g))],
        out_shape=[jax.ShapeDtypeStruct((T, D_POOL), BF16), jax.ShapeDtypeStruct((N_POOL_GROUPS, HEAD, HEAD), F32),
                   jax.ShapeDtypeStruct((1, D_POOL), F32)],
        scratch_shapes=[pltpu.VMEM((T + POOL_HALO, HEAD), F32), pltpu.VMEM((T + POOL_HALO, HEAD), F32), pltpu.VMEM((T, HEAD), F32)],
        semantics=("parallel",), jobs=jobs)


def _conv_taps(xs, cw):
    v = cw[CONV_WIDTH - 1] * xs[SUBLANES:]
    for k in range(CONV_WIDTH - 1):
        v += cw[k] * _shift_rows(xs, CONV_WIDTH - 1 - k)[SUBLANES:]
    return v


def _tap_rows(cw_ref):
    return [cw_ref[k:k + 1, :] for k in range(CONV_WIDTH)]


def _softplus_neg(lam):
    return jnp.maximum(-lam, 0.0) + _log1p(jnp.exp(-jnp.abs(lam)))


def _lru_gates(v, wa, ba, wx, bx, sp):
    vb = v.astype(BF16)
    ra = _sigmoid(_dot(vb, wa) + ba)
    ix = _sigmoid(_dot(vb, wx) + bx)
    log_a = -LRU_C * ra * sp
    a = jnp.exp(log_a)
    sq = jnp.sqrt(-jnp.tanh(log_a) * (a * a + 1.0))
    return ra, ix, a, sq


def _row_bcast(v, r):
    return jnp.broadcast_to(v[r:r + 1, :], v.shape)


TILE_BLOCK = 128


def _scan_in_tiles(coef, coef_shift, A_out, B, T, direction):
    order = list(range(SUBLANES)) if direction == 1 else list(range(SUBLANES - 1, -1, -1))
    tiles = min(TILE_BLOCK, T // SUBLANES)
    for base in range(0, T, tiles * SUBLANES):
        def rows(r, base=base):
            return pl.ds(base + r, tiles, stride=SUBLANES)

        A, Bv = coef[rows(order[0] + coef_shift), :], B[rows(order[0]), :]
        A_out[rows(order[0]), :] = A
        for r in order[1:]:
            a = coef[rows(r + coef_shift), :]
            Bv = a * Bv + B[rows(r), :]
            A = a * A
            A_out[rows(r), :] = A
            B[rows(r), :] = Bv


TILES_PER_STEP = 8


def _carry_tiles(A_s, B_s, out, ntile, direction):
    out_row = SUBLANES - 1 if direction == 1 else 0

    def step(k, carry):
        for j in range(TILES_PER_STEP):
            t = k * TILES_PER_STEP + j
            r0 = pl.multiple_of((t if direction == 1 else ntile - 1 - t) * SUBLANES, SUBLANES)
            A, B = A_s[pl.ds(r0, SUBLANES), :], B_s[pl.ds(r0, SUBLANES), :]
            out[pl.ds(r0, SUBLANES), :] = A * carry + B
            carry = _row_bcast(A, out_row) * carry + _row_bcast(B, out_row)
        return carry

    lax.fori_loop(0, ntile // TILES_PER_STEP, step, jnp.zeros((SUBLANES, HEAD), F32))


def _rnn_fwd(proj, conv_w, conv_b, w_a, b_a, w_x, b_x, lam, jobs=()):
    T = proj.shape[0]
    nchunk = T // CHUNK
    ntile = T // SUBLANES

    def body(u_ref, ug_ref, cw_ref, cb_ref, wa_ref, ba_ref, wx_ref, bx_ref, lam_ref,
             h_ref, z_ref, v_ref, ra_ref, ix_ref, a_ref, sq_ref, upad, a_s, b_s):
        _pad_front(upad, u_ref, SUBLANES)
        cw, cb = _tap_rows(cw_ref), cb_ref[...]
        wa, wx = wa_ref[...].astype(BF16), wx_ref[...].astype(BF16)
        ba, bx = ba_ref[...], bx_ref[...]
        sp = _softplus_neg(lam_ref[...])

        def chunk(i, carry):
            rows = pl.ds(pl.multiple_of(i * CHUNK, CHUNK), CHUNK)
            v = _conv_taps(upad[pl.ds(pl.multiple_of(i * CHUNK, CHUNK), CHUNK + SUBLANES), :], cw) + cb
            ra, ix, a, sq = _lru_gates(v, wa, ba, wx, bx, sp)
            v_ref[rows, :], ra_ref[rows, :], ix_ref[rows, :], a_ref[rows, :], sq_ref[rows, :] = v, ra, ix, a, sq
            a_s[rows, :], b_s[rows, :] = a, sq * ix * v
            return carry

        lax.fori_loop(0, nchunk, chunk, 0)
        _scan_in_tiles(a_s, 0, a_s, b_s, T, 1)
        _carry_tiles(a_s, b_s, h_ref, ntile, 1)

        def chunk3(i, carry):
            r0 = pl.multiple_of(i * CHUNK, CHUNK)
            gl, _ = _gelu_parts(ug_ref[pl.ds(r0, CHUNK), :])
            z_ref[pl.ds(r0, CHUNK), :] = (h_ref[pl.ds(r0, CHUNK), :] * gl).astype(BF16)
            return carry

        lax.fori_loop(0, nchunk, chunk3, 0)

    col = pl.BlockSpec((T, HEAD), lambda h: (0, h))
    vec = pl.BlockSpec((1, HEAD), lambda h: (0, h))
    mat = pl.BlockSpec((None, HEAD, HEAD), lambda h: (h, 0, 0))
    return _pallas(
        body, (proj, proj, conv_w, conv_b, w_a, b_a, w_x, b_x, lam), name="rnn_fwd", grid=(N_RNN_HEADS,),
        in_specs=[pl.BlockSpec((T, HEAD), lambda h: (0, COL_RNN + h)), pl.BlockSpec((T, HEAD), lambda h: (0, COL_GATE + h)),
                  pl.BlockSpec((CONV_WIDTH, HEAD), lambda h: (0, h)), vec, mat, vec, mat, vec, vec],
        out_specs=[col] * 7,
        out_shape=[jax.ShapeDtypeStruct((T, D_RNN), F32), jax.ShapeDtypeStruct((T, D_RNN), BF16)]
        + [jax.ShapeDtypeStruct((T, D_RNN), F32)] * 5,
        scratch_shapes=[pltpu.VMEM((T + SUBLANES, HEAD), F32), pltpu.VMEM((T, HEAD), F32), pltpu.VMEM((T, HEAD), F32)],
        semantics=("parallel",), jobs=jobs)


def _rnn_bwd(proj, hr, dz, gates, conv_w, w_a, w_x, lam, jobs=()):
    T = proj.shape[0]
    nchunk = T // CHUNK
    ntile = T // SUBLANES

    def body(u_ref, ug_ref, h_ref, dz_ref, v_ref, ra_ref, ix_ref, a_ref, sq_ref, cw_ref, wa_ref, wx_ref, lam_ref,
             du_ref, dug_ref, dwa_ref, dwx_ref, dba_ref, dbx_ref, dlam_ref, dcb_ref, dcw_ref,
             upad, hpad, apad, g_s, dvpad, ga_s):
        zero_tile = jnp.zeros((SUBLANES, HEAD), F32)
        _pad_front(upad, u_ref, SUBLANES)
        _pad_front(hpad, h_ref, SUBLANES)
        apad[pl.ds(T, SUBLANES), :] = zero_tile
        dvpad[pl.ds(T, SUBLANES), :] = zero_tile
        for ref in (dwa_ref, dwx_ref, dba_ref, dbx_ref, dlam_ref, dcb_ref, dcw_ref):
            ref[...] = jnp.zeros_like(ref)
        cw = _tap_rows(cw_ref)
        wa, wx = wa_ref[...].astype(BF16), wx_ref[...].astype(BF16)
        lam_row = lam_ref[...]
        sp = _softplus_neg(lam_row)

        def chunk(i, carry):
            rows = pl.ds(pl.multiple_of(i * CHUNK, CHUNK), CHUNK)
            apad[rows, :] = a_ref[rows, :]
            gl, dgl = _gelu_parts(ug_ref[rows, :])
            d = dz_ref[rows, :]
            g_s[rows, :] = d * gl
            dug_ref[rows, :] = (d * h_ref[rows, :] * dgl).astype(BF16)
            return carry

        lax.fori_loop(0, nchunk, chunk, 0)

        _scan_in_tiles(apad, 1, ga_s, g_s, T, -1)
        _carry_tiles(ga_s, g_s, g_s, ntile, -1)

        def chunk3(i, carry):
            r0 = pl.multiple_of(i * CHUNK, CHUNK)
            rows = pl.ds(r0, CHUNK)
            g = g_s[rows, :]
            h_prev = _shift_rows(hpad[pl.ds(r0, CHUNK + SUBLANES), :], 1)[SUBLANES:]
            v, ra, ix, sq, a = v_ref[rows, :], ra_ref[rows, :], ix_ref[rows, :], sq_ref[rows, :], a_ref[rows, :]
            d_sq = g * ix * v
            d_ix = g * sq * v
            d_la = a * g * h_prev - d_sq * a * a / sq
            dlam_ref[...] += jnp.sum(d_la * ra, axis=0, keepdims=True)
            d_pa = d_la * (-LRU_C) * sp * ra * (1.0 - ra)
            d_px = d_ix * ix * (1.0 - ix)
            vb, d_pab, d_pxb = v.astype(BF16), d_pa.astype(BF16), d_px.astype(BF16)
            dwa_ref[...] += _dot_tn(vb, d_pab)
            dwx_ref[...] += _dot_tn(vb, d_pxb)
            dba_ref[...] += jnp.sum(d_pa, axis=0, keepdims=True)
            dbx_ref[...] += jnp.sum(d_px, axis=0, keepdims=True)
            dv = g * sq * ix + _dot_nt(d_pab, wa) + _dot_nt(d_pxb, wx)
            dvpad[rows, :] = dv
            dcb_ref[...] += jnp.sum(dv, axis=0, keepdims=True)
            xs = upad[pl.ds(r0, CHUNK + SUBLANES), :]
            for k in range(CONV_WIDTH):
                u_k = _shift_rows(xs, CONV_WIDTH - 1 - k)[SUBLANES:] if k < CONV_WIDTH - 1 else xs[SUBLANES:]
                dcw_ref[k:k + 1, :] += jnp.sum(dv * u_k, axis=0, keepdims=True)
            return carry

        lax.fori_loop(0, nchunk, chunk3, 0)
        dlam_ref[...] = dlam_ref[...] * (LRU_C * _sigmoid(-lam_row))

        def chunk4(i, carry):
            r0 = pl.multiple_of(i * CHUNK, CHUNK)
            dvs = dvpad[pl.ds(r0, CHUNK + SUBLANES), :]
            du = cw[CONV_WIDTH - 1] * dvs[:CHUNK]
            for k in range(CONV_WIDTH - 1):
                du += cw[k] * _shift_rows(dvs, -(CONV_WIDTH - 1 - k))[:CHUNK]
            du_ref[pl.ds(r0, CHUNK), :] = du.astype(BF16)
            return carry

        lax.fori_loop(0, nchunk, chunk4, 0)

    col = pl.BlockSpec((T, HEAD), lambda h: (0, h))
    vec = pl.BlockSpec((1, HEAD), lambda h: (0, h))
    mat = pl.BlockSpec((None, HEAD, HEAD), lambda h: (h, 0, 0))
    taps = pl.BlockSpec((CONV_WIDTH, HEAD), lambda h: (0, h))
    vec_out = jax.ShapeDtypeStruct((1, D_RNN), F32)
    mat_out = jax.ShapeDtypeStruct((N_RNN_HEADS, HEAD, HEAD), F32)
    seq = pltpu.VMEM((T, HEAD), F32)
    seq_pad = pltpu.VMEM((T + SUBLANES, HEAD), F32)
    return _pallas(
        body, (proj, proj, hr, dz, *gates, conv_w, w_a, w_x, lam), name="rnn_bwd", grid=(N_RNN_HEADS,),
        in_specs=[pl.BlockSpec((T, HEAD), lambda h: (0, COL_RNN + h)), pl.BlockSpec((T, HEAD), lambda h: (0, COL_GATE + h))]
        + [col] * 7 + [taps, mat, mat, vec],
        out_specs=[col, col, mat, mat, vec, vec, vec, vec, taps],
        out_shape=[jax.ShapeDtypeStruct((T, D_RNN), BF16), jax.ShapeDtypeStruct((T, D_RNN), BF16), mat_out, mat_out,
                   vec_out, vec_out, vec_out, vec_out, jax.ShapeDtypeStruct((CONV_WIDTH, D_RNN), F32)],
        scratch_shapes=[seq_pad, seq_pad, seq_pad, seq, seq_pad, seq],
        semantics=("parallel",), jobs=jobs)


GROUP_FFN_OUT = ["w_ffn_out"]
GROUP_FFN_IN = ["w_ffn_in"]
GROUP_MIX = ["w_o", "w_pool_out", "w_rnn_out"]
GROUP_IN = ["w_in"]


def _step(x, target, s, full, conv_w, place):
    T = x.shape[0]
    tall, mid, low = min(T, 2048), min(T, 1024), min(T, 512)
    full = dict(full)

    def gathered(names, results):
        full.update(zip(names, results))

    early = ["w_pool_out", "w_rnn_out", "w_o", "w_ffn_out"]
    (proj, h1), (res,) = _norm_matmul(x, s["norm_mix"], full["w_in"], tm=tall, tn=512, name="in_proj", jobs=[_gather_job(full, early)])
    gathered(early, res)
    pm = _pool_fwd(proj, s["w_pool_grp"], s["pool_scale"])
    (hr, z, *gates), (res,) = _rnn_fwd(proj, conv_w, s["conv_b"], s["w_rg_a"], s["b_rg_a"], s["w_rg_x"], s["b_rg_x"],
                                       s["lru_lambda"], jobs=[_gather_job(full, ["w_ffn_in"])])
    gathered(["w_ffn_in"], res)
    *mix_by, mix, x2 = _branch_mix(pm, z, full["w_pool_out"], full["w_rnn_out"], proj, full["w_o"], x, tm=mid, tn=256)
    (act_by_up, act_by_gate, act, h2), _ = _ffn_in(x2, s["norm_ffn"], full["w_ffn_in"], tm=tall, tn=256)
    dx3, dx3b, sq_cols, g_norm_final = _ffn_out_loss(act, full["w_ffn_out"], x2, s["norm_final"], target, tm=low)

    g = {"norm_final": g_norm_final}

    def chip_sums(names, from_sibling):
        sums = {name: _chip_sum(name, g[name], got, place) for name, got in zip(names, from_sibling)}
        return {name: v[0] for name, v in sums.items()}, {name: v[1] for name, v in sums.items()}

    def final_sums(names, sums, from_chips):
        return {name: _final_sum(name, sums[name], got, place) for name, got in zip(names, from_chips)}

    dgate, dup = _ffn_out_bwd(dx3b, full["w_ffn_out"], act_by_gate, act_by_up, tm=tall, tn=256)
    gb = {}
    g["w_ffn_out"], gb["w_ffn_out"] = _weight_grad(act, [dx3b], tm=256, tn=D_MODEL, name="w_ffn_out_grad", also_bf16=True)
    (dx2, dx2b, g["norm_ffn"]), (res,) = _ffn_in_bwd(dgate, dup, full["w_ffn_in"], dx3, x2, s["norm_ffn"], tm=low,
                                                     jobs=[_sibling_job(gb, GROUP_FFN_OUT, BF16)])
    sums_ffn, sums_ffn_bf16 = chip_sums(GROUP_FFN_OUT, res)
    (g["w_ffn_in"], gb["w_ffn_in"]), (res,) = _weight_grad(h2, [dgate, dup], tm=D_MODEL, tn=256, name="w_ffn_in_grad", also_bf16=True,
                                                           jobs=[_chips_job(sums_ffn_bf16, GROUP_FFN_OUT)])
    shards_ffn = final_sums(GROUP_FFN_OUT, sums_ffn, res)
    (dgp, dgr, dyp, dyr), (res,) = _out_proj_bwd(dx2b, full["w_o"], mix_by, tm=tall, tn=256,
                                                 jobs=[_sibling_job(gb, GROUP_FFN_IN, BF16)])
    sums_ffn, sums_ffn_bf16 = chip_sums(GROUP_FFN_IN, res)
    g["w_o"], gb["w_o"] = _weight_grad(mix, [dx2b], tm=D_MODEL, tn=256, name="w_o_grad", also_bf16=True)
    dpm, dz = _branch_bwd(dyp, dyr, full["w_pool_out"], full["w_rnn_out"], tm=mid)
    g["w_pool_out"], gb["w_pool_out"] = _weight_grad(pm, [dyp], tm=D_POOL, tn=256, name="w_pool_out_grad", also_bf16=True)
    g["w_rnn_out"], gb["w_rnn_out"] = _weight_grad(z, [dyr], tm=D_RNN, tn=256, name="w_rnn_out_grad", also_bf16=True)
    (dupool, g["w_pool_grp"], g["pool_scale"]), _ = _pool_bwd(proj, dpm, s["w_pool_grp"], s["pool_scale"])
    ((durnn, dugate, g["w_rg_a"], g["w_rg_x"], g["b_rg_a"], g["b_rg_x"], g["lru_lambda"], g["conv_b"], g["conv_w"]),
     (res, from_sibling)) = _rnn_bwd(proj, hr, dz, gates, conv_w, s["w_rg_a"], s["w_rg_x"], s["lru_lambda"],
                                     jobs=[_chips_job(sums_ffn_bf16, GROUP_FFN_IN), _sibling_job(gb, GROUP_MIX, BF16)])
    shards_ffn.update(final_sums(GROUP_FFN_IN, sums_ffn, res))
    sums_mix, sums_mix_bf16 = chip_sums(GROUP_MIX, from_sibling)
    segs = [dupool, durnn, dugate, dgp, dgr]
    ffn = GROUP_FFN_OUT + GROUP_FFN_IN
    (g["w_in"], gb["w_in"]), (res, joined) = _weight_grad(
        h1, segs, tm=D_MODEL, tn=256, name="w_in_grad", also_bf16=True,
        jobs=[_chips_job(sums_mix_bf16, GROUP_MIX), _join_job(shards_ffn, ffn)])
    grads = dict(zip(ffn, joined))
    shards = final_sums(GROUP_MIX, sums_mix, res)
    (res,) = _run_jobs([_sibling_job(gb, GROUP_IN, BF16)], "w_in_exchange_sibling")
    sums_in, sums_in_bf16 = chip_sums(GROUP_IN, res)
    (grad_x, g["norm_mix"]), (res,) = _in_proj_bwd(segs, full["w_in"], dx2, x, s["norm_mix"], tm=low,
                                                  jobs=[_chips_job(sums_in_bf16, GROUP_IN)])
    shards.update(final_sums(GROUP_IN, sums_in, res))

    vec_rows = [g[name] if name != "pool_scale" else jnp.pad(g[name], ((0, 0), (0, D_MODEL - D_POOL))) for name in VEC_ITEMS]
    vec_rows += [g["conv_w"], sq_cols, jnp.zeros((VEC_ROWS - len(VEC_ITEMS) - CONV_WIDTH - 1, D_MODEL), F32)]
    vec = jnp.concatenate(vec_rows, axis=0).reshape(VEC_ROWS, N_DEV, HEAD).transpose(1, 0, 2)
    mat = jnp.concatenate([g[name].reshape(-1, HEAD) for name in MAT_ITEMS], axis=0).reshape(N_DEV, -1, HEAD)
    (vec, mat), (joined,) = _all_reduce_small([vec, mat], [False, True], jobs=[_join_job(shards, GROUP_MIX + GROUP_IN)])
    grads.update(zip(GROUP_MIX + GROUP_IN, joined))

    vec = vec.transpose(1, 0, 2).reshape(VEC_ROWS, D_MODEL)
    mat = mat.reshape(-1, HEAD)
    for k, name in enumerate(VEC_ITEMS):
        grads[name] = vec[k:k + 1, :s[name].shape[1]]
    grads["conv_w"] = vec[len(VEC_ITEMS):len(VEC_ITEMS) + CONV_WIDTH]
    row = 0
    for name in MAT_ITEMS:
        rows = s[name].shape[0] * HEAD
        grads[name] = mat[row:row + rows]
        row += rows
    return vec[len(VEC_ITEMS) + CONV_WIDTH], grad_x, grads


LARGE = {"w_in": "col", "w_pool_out": "col", "w_rnn_out": "row", "w_o": "row", "w_ffn_in": "col", "w_ffn_out": "row"}
LARGE_SHAPE = {"w_in": (D_MODEL, D_IN), "w_pool_out": (D_POOL, D_MODEL), "w_rnn_out": (D_RNN, D_MODEL),
               "w_o": (D_MODEL, D_MODEL), "w_ffn_in": (D_MODEL, 2 * D_FF), "w_ffn_out": (D_FF, D_MODEL)}


def _place():
    x, y, c = lax.axis_index("x"), lax.axis_index("y"), lax.axis_index("c")
    return 2 * x + y, c


def _chip_device(chip, c):
    return (chip // 2, chip % 2, c)


def _chip_window(ref, kind, shape, chip, half=None):
    K, N = shape
    if kind == "col":
        rows = slice(None) if half is None else pl.ds(half * (K // 2), K // 2)
        return ref.at[rows, pl.ds(chip * (N // N_CHIPS), N // N_CHIPS)]
    ks = K // N_CHIPS
    if half is None:
        return ref.at[pl.ds(chip * ks, ks), :]
    return ref.at[pl.ds(chip * ks + half * (ks // 2), ks // 2), :]


def _row_half(ref, half):
    rows = ref.shape[0] // 2
    return ref.at[pl.ds(half * rows, rows), :]


def _remote(win_src, win_dst, send_sems, recv_sems, idx, to):
    return pltpu.make_async_remote_copy(src_ref=win_src, dst_ref=win_dst, send_sem=send_sems.at[idx], recv_sem=recv_sems.at[idx],
                                        device_id=to, device_id_type=MESH)


def _gather_job(full, names, conv_w_full=None):
    n = len(names)
    cw_cols = D_RNN // N_CHIPS

    def windows(refs, chip, half):
        return [_chip_window(refs[k], LARGE[name], LARGE_SHAPE[name], chip, half) for k, name in enumerate(names)]

    def ici_copies(refs, send_sems, recv_sems, src_chip, dst_chip, c, r):
        wins = windows(refs, src_chip, c)
        if conv_w_full is not None:
            wins.append(refs[n].at[:, pl.ds(src_chip * cw_cols, cw_cols)])
        return [_remote(win, win, send_sems, recv_sems, (k, r), _chip_device(dst_chip, c)) for k, win in enumerate(wins)]

    def forwards(refs, send_sems, recv_sems, src_chip, half, to_core, chip, r):
        return [_remote(win, win, send_sems, recv_sems, (k, 3 + r), _chip_device(chip, to_core))
                for k, win in enumerate(windows(refs, src_chip, half))]

    def start(ins, outs, send_sems, recv_sems):
        chip, c = _place()
        for r in range(3):
            for cp in ici_copies(outs, send_sems, recv_sems, chip, chip ^ (r + 1), c, r):
                cp.start()

    def finish(ins, outs, send_sems, recv_sems):
        chip, c = _place()
        for r in range(3):
            for cp in ici_copies(outs, send_sems, recv_sems, chip ^ (r + 1), chip, c, r):
                cp.wait_recv()
            for cp in forwards(outs, send_sems, recv_sems, chip ^ (r + 1), c, 1 - c, chip, r):
                cp.start()
        for r in range(3):
            for cp in forwards(outs, send_sems, recv_sems, chip ^ (r + 1), 1 - c, c, chip, r):
                cp.wait_recv()
            for cp in ici_copies(outs, send_sems, recv_sems, chip, chip ^ (r + 1), c, r):
                cp.wait_send()
            for cp in forwards(outs, send_sems, recv_sems, chip ^ (r + 1), c, 1 - c, chip, r):
                cp.wait_send()

    arrays = [full[name] for name in names] + ([conv_w_full] if conv_w_full is not None else [])
    return _Job(arrays, [jax.ShapeDtypeStruct(a.shape, a.dtype) for a in arrays], {k: k for k in range(len(arrays))},
                (len(arrays), 6), start, finish)


def _core_halves(ref, kind, shape, c):
    return [_chip_window(ref, kind, shape, chip, c) for chip in range(N_CHIPS)]


def _sibling_job(grads, names, dtype=F32):
    def start(ins, outs, send_sems, recv_sems):
        chip, c = _place()
        for k, name in enumerate(names):
            kind, shape = LARGE[name], LARGE_SHAPE[name]
            if kind == "col":
                pairs = [(_row_half(ins[k], 1 - c), outs[k])]
            else:
                rows = shape[0] // N_DEV
                pairs = [(win, outs[k].at[pl.ds(j * rows, rows), :]) for j, win in enumerate(_core_halves(ins[k], kind, shape, 1 - c))]
            for src, dst in pairs:
                _remote(src, dst, send_sems, recv_sems, k, _chip_device(chip, 1 - c)).start()

    def finish(ins, outs, send_sems, recv_sems):
        chip, c = _place()
        for k in range(len(names)):
            _remote(outs[k], outs[k], send_sems, recv_sems, k, _chip_device(chip, 1 - c)).wait()

    return _Job([grads[name] for name in names],
                [jax.ShapeDtypeStruct((LARGE_SHAPE[name][0] // 2, LARGE_SHAPE[name][1]), dtype) for name in names], {},
                (len(names),), start, finish)


def _chip_sum(name, g, got, place):
    kind, (K, N) = LARGE[name], LARGE_SHAPE[name]
    rows = K // N_DEV
    piece_cols = N // N_CHIPS

    def body(place_ref, g_ref, got_ref, o_ref, ob_ref):
        total = g_ref[...] + got_ref[...].astype(F32)
        ob_ref[...] = total.astype(BF16)
        if kind == "col":
            for chip in range(N_CHIPS):
                @pl.when(place_ref[0] == chip)
                def _(chip=chip):
                    o_ref[...] = total[:, chip * piece_cols:(chip + 1) * piece_cols]
        else:
            @pl.when(pl.program_id(0) == place_ref[0])
            def _():
                o_ref[...] = total

    if kind == "col":
        mine = pl.BlockSpec((rows, N), lambda j, place_ref: (j + N_CHIPS * place_ref[1], 0))
        own = pl.BlockSpec((rows, piece_cols), lambda j, place_ref: (j, 0))
    else:
        mine = pl.BlockSpec((rows, N), lambda j, place_ref: (2 * j + place_ref[1], 0))
        own = pl.BlockSpec((rows, N), lambda j, place_ref: (0, 0))
    blk = pl.BlockSpec((rows, N), lambda j, place_ref: (j, 0))
    return pl.pallas_call(
        body, name=name + "_chip_sum",
        grid_spec=pltpu.PrefetchScalarGridSpec(num_scalar_prefetch=1, grid=(N_CHIPS,), in_specs=[mine, blk], out_specs=[own, blk]),
        out_shape=[jax.ShapeDtypeStruct(_piece_shape(name), F32), jax.ShapeDtypeStruct((K // 2, N), BF16)],
        compiler_params=_params(dimension_semantics=("arbitrary",)),
    )(place, g, got)


def _piece(ref, kind, shape, chip):
    K, N = shape
    if kind == "col":
        return ref.at[:, pl.ds(chip * (N // N_CHIPS), N // N_CHIPS)]
    return ref.at[pl.ds(chip * (K // N_DEV), K // N_DEV), :]


def _piece_shape(name):
    kind, (K, N) = LARGE[name], LARGE_SHAPE[name]
    return (K // 2, N // N_CHIPS) if kind == "col" else (K // N_DEV, N)


def _chips_job(sums, names):
    def copies(ins, outs, send_sems, recv_sems):
        chip, c = _place()
        return [_remote(_piece(ins[k], LARGE[name], LARGE_SHAPE[name], chip ^ (r + 1)), outs[k].at[r], send_sems, recv_sems, (k, r),
                        _chip_device(chip ^ (r + 1), c)) for k, name in enumerate(names) for r in range(3)]

    def start(*refs):
        for cp in copies(*refs):
            cp.start()

    def finish(*refs):
        for cp in copies(*refs):
            cp.wait()

    return _Job([sums[name] for name in names], [jax.ShapeDtypeStruct((3,) + _piece_shape(name), BF16) for name in names], {},
                (len(names), 3), start, finish)


def _final_sum(name, chip_sum, got, place):
    rows, cols = _piece_shape(name)

    def body(place_ref, s_ref, got_ref, o_ref):
        o_ref[...] = ((s_ref[...] + got_ref[0].astype(F32)) + got_ref[1].astype(F32)) + got_ref[2].astype(F32)

    mine = pl.BlockSpec((rows, cols), lambda i, place_ref: (0, 0))
    return pl.pallas_call(
        body, name=name + "_final_sum",
        grid_spec=pltpu.PrefetchScalarGridSpec(
            num_scalar_prefetch=1, grid=(1,), in_specs=[mine, pl.BlockSpec((3, rows, cols), lambda i, place_ref: (0, 0, 0))],
            out_specs=pl.BlockSpec((rows, cols), lambda i, place_ref: (place_ref[1], 0))),
        out_shape=jax.ShapeDtypeStruct((2 * rows, cols), F32),
        compiler_params=_params(dimension_semantics=("arbitrary",)),
    )(place, chip_sum, got)


def _join_job(shards, names):
    def half_copy(outs, send_sems, recv_sems, k, mine):
        chip, c = _place()
        win = _row_half(outs[k], c if mine else 1 - c)
        return _remote(win, win, send_sems, recv_sems, k, _chip_device(chip, 1 - c))

    def start(ins, outs, send_sems, recv_sems):
        for k in range(len(names)):
            half_copy(outs, send_sems, recv_sems, k, True).start()

    def finish(ins, outs, send_sems, recv_sems):
        for k in range(len(names)):
            half_copy(outs, send_sems, recv_sems, k, True).wait_send()
            half_copy(outs, send_sems, recv_sems, k, False).wait_recv()

    arrays = [shards[name] for name in names]
    return _Job(arrays, [jax.ShapeDtypeStruct(a.shape, F32) for a in arrays], {k: k for k in range(len(arrays))},
                (len(arrays),), start, finish)


VEC_ROWS = 16


def _all_reduce_small(slabs, narrow, jobs=()):
    n = len(slabs)
    narrowed = [k for k in range(n) if narrow[k]]

    def body(*refs):
        in_refs, out_refs, got_refs = refs[:n], refs[n:2 * n], refs[2 * n:3 * n]
        bf16_refs = dict(zip(narrowed, refs[3 * n:3 * n + len(narrowed)]))
        send_sems, recv_sems = refs[3 * n + len(narrowed):]
        x, y, c = lax.axis_index("x"), lax.axis_index("y"), lax.axis_index("c")
        me = 4 * x + 2 * y + c
        for k, ref in bf16_refs.items():
            ref[...] = in_refs[k][...].astype(BF16)
        partial_refs = [bf16_refs.get(k, in_refs[k]) for k in range(n)]

        def remote(src, dst, k, phase, r):
            other = me ^ r
            return pltpu.make_async_remote_copy(src_ref=src, dst_ref=dst, send_sem=send_sems.at[k, phase, r],
                                                recv_sem=recv_sems.at[k, phase, r],
                                                device_id=(other // 4, (other // 2) % 2, other % 2), device_id_type=MESH)

        scatter = [remote(partial_refs[k].at[me ^ r], got_refs[k].at[r], k, 0, r) for r in range(1, N_DEV) for k in range(n)]
        for cp in scatter:
            cp.start()
        for cp in scatter:
            cp.wait()
        for k in range(n):
            total = in_refs[k][me]
            for r in range(1, N_DEV):
                total = total + got_refs[k][r].astype(F32)
            out_refs[k][me] = total
        gather = [remote(out_refs[k].at[me], out_refs[k].at[me], k, 1, r) for r in range(1, N_DEV) for k in range(n)]
        for cp in gather:
            cp.start()
        for r in range(1, N_DEV):
            for k in range(n):
                remote(out_refs[k].at[me ^ r], out_refs[k].at[me ^ r], k, 1, r).wait_recv()
        for cp in gather:
            cp.wait_send()

    return _pallas(
        body, slabs, name="all_reduce_small", grid=(), in_specs=[VMEM] * n, out_specs=[VMEM] * n,
        out_shape=[jax.ShapeDtypeStruct(s.shape, F32) for s in slabs],
        scratch_shapes=[pltpu.VMEM(s.shape, BF16 if narrow[k] else F32) for k, s in enumerate(slabs)]
        + [pltpu.VMEM(slabs[k].shape, BF16) for k in narrowed]
        + [pltpu.SemaphoreType.DMA((n, 2, N_DEV)), pltpu.SemaphoreType.DMA((n, 2, N_DEV))], jobs=jobs)


def _cast_into_whole(w, name, place):
    rows, cols = w.shape
    tr = rows // 2

    def body(place_ref, w_ref, o_ref):
        o_ref[...] = w_ref[...].astype(BF16)

    if LARGE[name] == "col":
        window = pl.BlockSpec((tr, cols), lambda i, place_ref: (i, place_ref[0]))
    else:
        window = pl.BlockSpec((tr, cols), lambda i, place_ref: (2 * place_ref[0] + i, 0))
    return pl.pallas_call(
        body, name=name + "_cast",
        grid_spec=pltpu.PrefetchScalarGridSpec(num_scalar_prefetch=1, grid=(2,),
                                               in_specs=[pl.BlockSpec((tr, cols), lambda i, place_ref: (i, 0))], out_specs=window),
        out_shape=jax.ShapeDtypeStruct(LARGE_SHAPE[name], BF16),
        compiler_params=_params(dimension_semantics=("parallel",)))(place, w)


def _cast_many_into_whole(shards, place, jobs):
    names = list(shards)
    n = len(names)

    def body(place_ref, *refs):
        for w_ref, o_ref in zip(refs[:n], refs[n:]):
            o_ref[...] = w_ref[...].astype(BF16)

    def window(name):
        rows, cols = shards[name].shape
        if LARGE[name] == "col":
            return pl.BlockSpec((rows // 2, cols), lambda i, place_ref: (i, place_ref[0]))
        return pl.BlockSpec((rows // 2, cols), lambda i, place_ref: (2 * place_ref[0] + i, 0))

    def half(name):
        rows, cols = shards[name].shape
        return pl.BlockSpec((rows // 2, cols), lambda i, place_ref: (i, 0))

    return _pallas(body, [shards[name] for name in names], name="cast_weights", grid=(2,),
                   in_specs=[half(name) for name in names], out_specs=[window(name) for name in names],
                   out_shape=[jax.ShapeDtypeStruct(LARGE_SHAPE[name], BF16) for name in names],
                   semantics=("arbitrary",), jobs=jobs, prefetch=place)


def _adamw_math(w, g, m, v):
    m = ADAM_B1 * m + (1.0 - ADAM_B1) * g
    v = ADAM_B2 * v + (1.0 - ADAM_B2) * (g * g)
    m_hat = m / (1.0 - ADAM_B1 ** ADAM_STEP)
    v_hat = v / (1.0 - ADAM_B2 ** ADAM_STEP)
    delta = -ADAM_LR * (m_hat / (jnp.sqrt(v_hat) + ADAM_EPS) + ADAM_WD * w)
    return delta, m, v


def _adamw_large(w, g, m, v, name):
    rows, cols = w.shape
    steps = 2
    tr = rows // steps

    def body(w_ref, g_ref, m_ref, v_ref, d_ref, mo_ref, vo_ref):
        d_ref[...], mo_ref[...], vo_ref[...] = _adamw_math(w_ref[...], g_ref[...], m_ref[...], v_ref[...])

    blk = pl.BlockSpec((tr, cols), lambda i: (i, 0))
    out = jax.ShapeDtypeStruct(w.shape, F32)
    return pl.pallas_call(body, name=name + "_adamw", grid=(steps,), in_specs=[blk] * 4, out_specs=[blk] * 3, out_shape=[out] * 3,
                          compiler_params=_params(dimension_semantics=("parallel",)))(w, g, m, v)


def _adamw_small(ws, gs, ms, vs):
    n = len(ws)

    def body(*refs):
        for k in range(n):
            w_ref, g_ref, m_ref, v_ref = (refs[q * n + k] for q in range(4))
            d_ref, mo_ref, vo_ref = (refs[(4 + q) * n + k] for q in range(3))
            d_ref[...], mo_ref[...], vo_ref[...] = _adamw_math(w_ref[...], g_ref[...], m_ref[...], v_ref[...])

    out = [jax.ShapeDtypeStruct(w.shape, F32) for w in ws]
    res = pl.pallas_call(body, name="small_adamw", in_specs=[VMEM] * (4 * n), out_specs=[VMEM] * (3 * n), out_shape=out * 3,
                         compiler_params=_params())(*ws, *gs, *ms, *vs)
    return res[:n], res[n:2 * n], res[2 * n:]


WEIGHTS = ["norm_mix", "w_in", "w_pool_grp", "pool_scale", "w_pool_out", "conv_w", "conv_b", "w_rg_a", "b_rg_a", "w_rg_x",
           "b_rg_x", "lru_lambda", "w_rnn_out", "w_o", "norm_ffn", "w_ffn_in", "w_ffn_out", "norm_final"]
VEC_ITEMS = ["norm_mix", "norm_ffn", "norm_final", "pool_scale", "conv_b", "lru_lambda", "b_rg_a", "b_rg_x"]
MAT_ITEMS = ["w_pool_grp", "w_rg_a", "w_rg_x"]


def _as2d(name, a):
    if name in MAT_ITEMS:
        return a.reshape(-1, HEAD, HEAD)
    if name == "conv_w":
        return a.reshape(CONV_WIDTH, -1)
    return a.reshape(1, -1)


def kernel(x, norm_mix, w_in, w_pool_grp, pool_scale, w_pool_out, conv_w, conv_b, w_rg_a, b_rg_a, w_rg_x, b_rg_x, lru_lambda, w_rnn_out, w_o, norm_ffn, w_ffn_in, w_ffn_out, norm_final, loss_target, m_norm_mix, m_w_in, m_w_pool_grp, m_pool_scale, m_w_pool_out, m_conv_w, m_conv_b, m_w_rg_a, m_b_rg_a, m_w_rg_x, m_b_rg_x, m_lru_lambda, m_w_rnn_out, m_w_o, m_norm_ffn, m_w_ffn_in, m_w_ffn_out, m_norm_final, v_norm_mix, v_w_in, v_w_pool_grp, v_pool_scale, v_w_pool_out, v_conv_w, v_conv_b, v_w_rg_a, v_b_rg_a, v_w_rg_x, v_b_rg_x, v_lru_lambda, v_w_rnn_out, v_w_o, v_norm_ffn, v_w_ffn_in, v_w_ffn_out, v_norm_final):
    given = dict(locals())
    w = {name: given[name] for name in WEIGHTS}
    m = {name: given["m_" + name] for name in WEIGHTS}
    v = {name: given["v_" + name] for name in WEIGHTS}
    chip, c = _place()

    place = jnp.stack([chip, c]).astype(jnp.int32)
    conv_cols = w["conv_w"].shape[-1]
    conv_w_mine = lax.dynamic_update_slice_in_dim(jnp.zeros((CONV_WIDTH, D_RNN), F32), w["conv_w"][0], chip * conv_cols, axis=1)
    w_in_mine = _cast_into_whole(w["w_in"][0], "w_in", place)
    later = [name for name in LARGE if name != "w_in"]
    casts, ((w_in_full, conv_w_full),) = _cast_many_into_whole(
        {name: w[name][0] for name in later}, place, jobs=[_gather_job({"w_in": w_in_mine}, ["w_in"], conv_w_mine)])
    full = dict(zip(later, casts), w_in=w_in_full)
    small = {name: _as2d(name, w[name]) for name in WEIGHTS if name not in LARGE and name != "conv_w"}
    sq_cols, grad_x, grads = _step(x[0], loss_target[0], small, full, conv_w_full, place)
    loss = 0.5 / D_MODEL * jnp.sum(sq_cols)
    grads["conv_w"] = lax.dynamic_slice_in_dim(grads["conv_w"], chip * conv_cols, conv_cols, axis=1)

    delta, new_m, new_v = {}, {}, {}
    for name in LARGE:
        delta[name], new_m[name], new_v[name] = _adamw_large(w[name][0], grads[name], m[name][0], v[name][0], name)
    small_names = [name for name in WEIGHTS if name not in LARGE]
    flat = lambda d: [d[name].reshape(grads[name].shape) for name in small_names]
    ds, mo, vo = _adamw_small(flat(w), [grads[name] for name in small_names], flat(m), flat(v))
    for k, name in enumerate(small_names):
        delta[name], new_m[name], new_v[name] = ds[k], mo[k], vo[k]

    shaped = lambda d: [d[name].reshape(w[name].shape) for name in WEIGHTS]
    return (loss, grad_x[None], *shaped(grads), *shaped(delta), *shaped(new_m), *shaped(new_v))
```
